```python
import jax, jax.numpy as jnp
from jax import lax
import numpy as np

D_MODEL = 2048
BATCH = 16
SEQ = 2048
DEPTH = 1

HEAD_DIM = 128
ATTN_PATTERNS = ((128, 1), (512, 4), (2048, 16))
N_GROUPS = len(ATTN_PATTERNS)
HEADS_PER_GROUP = 4
ATTN_QKV = N_GROUPS * HEADS_PER_GROUP * HEAD_DIM
ATTN_OUT = HEADS_PER_GROUP * HEAD_DIM
BLK = 128
CONV_WIDTH = 1024
CONV_K = 3
MEM_LEN = 256
MEM_HEADS = 4
MEM_HEAD_DIM = 256
MEM_W = MEM_HEADS * MEM_HEAD_DIM
N_BRANCH = 3
EPS = 1e-6

SPLIT_SIZES = (ATTN_QKV, ATTN_QKV, ATTN_QKV, ATTN_OUT,
               CONV_WIDTH, CONV_WIDTH, CONV_WIDTH, CONV_WIDTH,
               MEM_W, MEM_W, N_BRANCH * D_MODEL)
IN_COLS = int(sum(SPLIT_SIZES))
SPLIT_IDX = [int(i) for i in np.cumsum(SPLIT_SIZES)[:-1]]

kernel_name = "hybrid_dilated_attn_shortconv_memory_gated_merge"


def rms_norm(t, g):
    tf = t.astype(jnp.float32)
    y = tf * lax.rsqrt(jnp.mean(tf * tf, axis=-1, keepdims=True) + EPS) * g.astype(jnp.float32)
    return y.astype(t.dtype)


def banded_causal_attn(q, k, v, span):
    N, L, H, E = q.shape
    nb = -(-L // BLK)
    pad = nb * BLK - L
    q = jnp.pad(q, ((0, 0), (0, pad), (0, 0), (0, 0)))
    k = jnp.pad(k, ((0, 0), (BLK, pad), (0, 0), (0, 0)))
    v = jnp.pad(v, ((0, 0), (BLK, pad), (0, 0), (0, 0)))
    qb = q.reshape(N, nb, BLK, H, E)

    def two_blocks(t):
        t = t.reshape(N, nb + 1, BLK, H, E)
        return jnp.concatenate([t[:, :-1], t[:, 1:]], axis=2)

    kb, vb = two_blocks(k), two_blocks(v)
    s = jnp.einsum('nbqhe,nbkhe->nbhqk', qb.astype(jnp.float32), kb.astype(jnp.float32)) * (E ** -0.5)
    qpos = jnp.arange(BLK)[:, None] + BLK
    kpos = jnp.arange(2 * BLK)[None, :]
    rel = qpos - kpos
    band = (rel >= 0) & (rel <= span)
    valid = (jnp.arange(nb)[:, None] * BLK + kpos - BLK) >= 0
    mask = band[None] & valid[:, None, :]
    s = jnp.where(mask[None, :, None], s, -jnp.inf)
    m = jnp.max(s, axis=-1, keepdims=True)
    p = jnp.exp(s - m)
    den = jnp.sum(p, axis=-1, keepdims=True)
    o = jnp.einsum('nbhqk,nbkhe->nbqhe', p / den, vb.astype(jnp.float32))
    lse = (m + jnp.log(den))[..., 0]
    o = o.reshape(N, nb * BLK, H, E)[:, :L]
    lse = lse.transpose(0, 1, 3, 2).reshape(N, nb * BLK, H)[:, :L]
    return o, lse


def dilated_causal_attn(q, k, v, window, dilation):
    B, S, H, E = q.shape
    L = S // dilation

    def to_classes(t):
        return t.reshape(B, L, dilation, H, E).transpose(0, 2, 1, 3, 4).reshape(B * dilation, L, H, E)

    o, lse = banded_causal_attn(to_classes(q), to_classes(k), to_classes(v), window // dilation)
    o = o.reshape(B, dilation, L, H, E).transpose(0, 2, 1, 3, 4).reshape(B, S, H, E)
    lse = lse.reshape(B, dilation, L, H).transpose(0, 2, 1, 3).reshape(B, S, H)
    return o, lse


def _fwd_setup_inputs(seed: int = 0) -> dict:
    key = jax.random.key(seed)
    ks = jax.random.split(key, 16)
    f32 = jnp.float32

    def w(k, shape, fan_in):
        return jax.random.normal(k, shape, f32) * (fan_in ** -0.5)

    def gain(k, shape):
        return 1.0 + 0.02 * jax.random.normal(k, shape, f32)

    return {
        "x": jax.random.normal(ks[0], (BATCH, SEQ, D_MODEL), f32),
        "mem": jax.random.normal(ks[1], (BATCH, MEM_LEN, D_MODEL), f32),
        "norm_g": gain(ks[2], (D_MODEL,)),
        "mem_norm_g": gain(ks[3], (D_MODEL,)),
        "w_in": w(ks[4], (D_MODEL, IN_COLS), D_MODEL),
        "attn_q_norm": gain(ks[5], (N_GROUPS, HEAD_DIM)),
        "attn_k_norm": gain(ks[6], (N_GROUPS, HEAD_DIM)),
        "conv_w": w(ks[7], (CONV_K, CONV_WIDTH), CONV_K),
        "mem_w_kv": w(ks[8], (D_MODEL, 2 * MEM_W), D_MODEL),
        "mem_q_norm": gain(ks[9], (MEM_HEAD_DIM,)),
        "mem_k_norm": gain(ks[10], (MEM_HEAD_DIM,)),
        "w_br_attn": w(ks[11], (ATTN_OUT, D_MODEL), ATTN_OUT),
        "w_br_conv": w(ks[12], (CONV_WIDTH, D_MODEL), CONV_WIDTH),
        "w_br_mem": w(ks[13], (MEM_W, D_MODEL), MEM_W),
        "w_out": w(ks[14], (D_MODEL, D_MODEL), D_MODEL),
    }


def _fwd_reference(x, mem, norm_g, mem_norm_g, w_in, attn_q_norm, attn_k_norm, conv_w,
              mem_w_kv, mem_q_norm, mem_k_norm, w_br_attn, w_br_conv, w_br_mem, w_out):
    B, S, D = x.shape
    for _layer in range(DEPTH):
        h = rms_norm(x, norm_g)
        proj = jnp.einsum('bsd,dc->bsc', h, w_in)
        (q, k, v, z_attn, conv_b, conv_c, conv_v, z_conv,
         mem_q, z_mem, gates) = jnp.split(proj, SPLIT_IDX, axis=-1)

        q = q.reshape(B, S, N_GROUPS, HEADS_PER_GROUP, HEAD_DIM)
        k = k.reshape(B, S, N_GROUPS, HEADS_PER_GROUP, HEAD_DIM)
        v = v.reshape(B, S, N_GROUPS, HEADS_PER_GROUP, HEAD_DIM)
        outs, lses = [], []
        for g, (window, dilation) in enumerate(ATTN_PATTERNS):
            qg = rms_norm(q[:, :, g], attn_q_norm[g])
            kg = rms_norm(k[:, :, g], attn_k_norm[g])
            o, lse = dilated_causal_attn(qg, kg, v[:, :, g], window, dilation)
            outs.append(o)
            lses.append(lse)
        alpha = jax.nn.softmax(jnp.stack(lses, axis=0), axis=0)
        a = jnp.sum(alpha[..., None] * jnp.stack(outs, axis=0), axis=0)
        a = a.reshape(B, S, ATTN_OUT).astype(x.dtype) * jax.nn.silu(z_attn)

        u = conv_c * conv_v
        y = sum(conv_w[j] * jnp.pad(u, ((0, 0), (j, 0), (0, 0)))[:, :S] for j in range(CONV_K))
        c = conv_b * y * jax.nn.silu(z_conv)

        mh = rms_norm(mem, mem_norm_g)
        mkv = jnp.einsum('bmd,dc->bmc', mh, mem_w_kv)
        mk, mv = jnp.split(mkv, 2, axis=-1)
        M = mem.shape[1]
        mq = rms_norm(mem_q.reshape(B, S, MEM_HEADS, MEM_HEAD_DIM), mem_q_norm)
        mk = rms_norm(mk.reshape(B, M, MEM_HEADS, MEM_HEAD_DIM), mem_k_norm)
        mv = mv.reshape(B, M, MEM_HEADS, MEM_HEAD_DIM)
        ms = jnp.einsum('bshe,bmhe->bhsm', mq.astype(jnp.float32), mk.astype(jnp.float32)) * (MEM_HEAD_DIM ** -0.5)
        mp = jax.nn.softmax(ms, axis=-1)
        mo = jnp.einsum('bhsm,bmhe->bshe', mp, mv.astype(jnp.float32))
        mo = mo.reshape(B, S, MEM_W).astype(x.dtype) * jax.nn.silu(z_mem)

        gt = jax.nn.sigmoid(gates.astype(jnp.float32).reshape(B, S, N_BRANCH, D)).astype(x.dtype)
        merged = (gt[:, :, 0] * jnp.einsum('bsc,cd->bsd', a, w_br_attn)
                  + gt[:, :, 1] * jnp.einsum('bsc,cd->bsd', c, w_br_conv)
                  + gt[:, :, 2] * jnp.einsum('bsc,cd->bsd', mo, w_br_mem))
        x = x + jnp.einsum('bsd,de->bse', merged, w_out)
    return x


import jax as _jax
import jax.numpy as _jnp

TWIN_FORMAT = 'train_step'
FWD_PARAMS = ['x', 'mem', 'norm_g', 'mem_norm_g', 'w_in', 'attn_q_norm', 'attn_k_norm', 'conv_w', 'mem_w_kv', 'mem_q_norm', 'mem_k_norm', 'w_br_attn', 'w_br_conv', 'w_br_mem', 'w_out']
TWIN_WEIGHTS = ['norm_g', 'mem_norm_g', 'w_in', 'attn_q_norm', 'attn_k_norm', 'conv_w', 'mem_w_kv', 'mem_q_norm', 'mem_k_norm', 'w_br_attn', 'w_br_conv', 'w_br_mem', 'w_out']
TWIN_DIFF_INPUT = 'x'
TWIN_INPUTS = ['x', 'mem', 'norm_g', 'mem_norm_g', 'w_in', 'attn_q_norm', 'attn_k_norm', 'conv_w', 'mem_w_kv', 'mem_q_norm', 'mem_k_norm', 'w_br_attn', 'w_br_conv', 'w_br_mem', 'w_out', 'loss_target', 'm_norm_g', 'm_mem_norm_g', 'm_w_in', 'm_attn_q_norm', 'm_attn_k_norm', 'm_conv_w', 'm_mem_w_kv', 'm_mem_q_norm', 'm_mem_k_norm', 'm_w_br_attn', 'm_w_br_conv', 'm_w_br_mem', 'm_w_out', 'v_norm_g', 'v_mem_norm_g', 'v_w_in', 'v_attn_q_norm', 'v_attn_k_norm', 'v_conv_w', 'v_mem_w_kv', 'v_mem_q_norm', 'v_mem_k_norm', 'v_w_br_attn', 'v_w_br_conv', 'v_w_br_mem', 'v_w_out']
TWIN_OUTPUTS = ['loss', 'grad_x', 'grad_norm_g', 'grad_mem_norm_g', 'grad_w_in', 'grad_attn_q_norm', 'grad_attn_k_norm', 'grad_conv_w', 'grad_mem_w_kv', 'grad_mem_q_norm', 'grad_mem_k_norm', 'grad_w_br_attn', 'grad_w_br_conv', 'grad_w_br_mem', 'grad_w_out', 'delta_norm_g', 'delta_mem_norm_g', 'delta_w_in', 'delta_attn_q_norm', 'delta_attn_k_norm', 'delta_conv_w', 'delta_mem_w_kv', 'delta_mem_q_norm', 'delta_mem_k_norm', 'delta_w_br_attn', 'delta_w_br_conv', 'delta_w_br_mem', 'delta_w_out', 'new_m_norm_g', 'new_m_mem_norm_g', 'new_m_w_in', 'new_m_attn_q_norm', 'new_m_attn_k_norm', 'new_m_conv_w', 'new_m_mem_w_kv', 'new_m_mem_q_norm', 'new_m_mem_k_norm', 'new_m_w_br_attn', 'new_m_w_br_conv', 'new_m_w_br_mem', 'new_m_w_out', 'new_v_norm_g', 'new_v_mem_norm_g', 'new_v_w_in', 'new_v_attn_q_norm', 'new_v_attn_k_norm', 'new_v_conv_w', 'new_v_mem_w_kv', 'new_v_mem_q_norm', 'new_v_mem_k_norm', 'new_v_w_br_attn', 'new_v_w_br_conv', 'new_v_w_br_mem', 'new_v_w_out']
TWIN_LEAF_KINDS = {'loss': 'loss', 'grad_x': 'grad_x', 'grad_norm_g': 'grad_w', 'grad_mem_norm_g': 'grad_w', 'grad_w_in': 'grad_w', 'grad_attn_q_norm': 'grad_w', 'grad_attn_k_norm': 'grad_w', 'grad_conv_w': 'grad_w', 'grad_mem_w_kv': 'grad_w', 'grad_mem_q_norm': 'grad_w', 'grad_mem_k_norm': 'grad_w', 'grad_w_br_attn': 'grad_w', 'grad_w_br_conv': 'grad_w', 'grad_w_br_mem': 'grad_w', 'grad_w_out': 'grad_w', 'delta_norm_g': 'delta_w', 'delta_mem_norm_g': 'delta_w', 'delta_w_in': 'delta_w', 'delta_attn_q_norm': 'delta_w', 'delta_attn_k_norm': 'delta_w', 'delta_conv_w': 'delta_w', 'delta_mem_w_kv': 'delta_w', 'delta_mem_q_norm': 'delta_w', 'delta_mem_k_norm': 'delta_w', 'delta_w_br_attn': 'delta_w', 'delta_w_br_conv': 'delta_w', 'delta_w_br_mem': 'delta_w', 'delta_w_out': 'delta_w', 'new_m_norm_g': 'new_m', 'new_m_mem_norm_g': 'new_m', 'new_m_w_in': 'new_m', 'new_m_attn_q_norm': 'new_m', 'new_m_attn_k_norm': 'new_m', 'new_m_conv_w': 'new_m', 'new_m_mem_w_kv': 'new_m', 'new_m_mem_q_norm': 'new_m', 'new_m_mem_k_norm': 'new_m', 'new_m_w_br_attn': 'new_m', 'new_m_w_br_conv': 'new_m', 'new_m_w_br_mem': 'new_m', 'new_m_w_out': 'new_m', 'new_v_norm_g': 'new_v', 'new_v_mem_norm_g': 'new_v', 'new_v_w_in': 'new_v', 'new_v_attn_q_norm': 'new_v', 'new_v_attn_k_norm': 'new_v', 'new_v_conv_w': 'new_v', 'new_v_mem_w_kv': 'new_v', 'new_v_mem_q_norm': 'new_v', 'new_v_mem_k_norm': 'new_v', 'new_v_w_br_attn': 'new_v', 'new_v_w_br_conv': 'new_v', 'new_v_w_br_mem': 'new_v', 'new_v_w_out': 'new_v'}


def _forward(args):
    return _fwd_reference(*[args[k] for k in FWD_PARAMS])


def _output_shape():
    out = _jax.eval_shape(lambda: _forward(_fwd_setup_inputs(0)))
    return out.shape, out.dtype

N_MICROBATCH = 1
ADAM_LR = 0.001
ADAM_B1 = 0.9
ADAM_B2 = 0.999
ADAM_EPS = 1e-08
ADAM_WD = 0.01
ADAM_STEP = 10
PER_EXAMPLE_BATCH_AXIS = {'x': 0, 'mem': 0, 'loss_target': 0}
SHARED_INPUTS = []
_WEIGHT_DTYPES = {'norm_g': _jnp.float32, 'mem_norm_g': _jnp.float32, 'w_in': _jnp.float32, 'attn_q_norm': _jnp.float32, 'attn_k_norm': _jnp.float32, 'conv_w': _jnp.float32, 'mem_w_kv': _jnp.float32, 'mem_q_norm': _jnp.float32, 'mem_k_norm': _jnp.float32, 'w_br_attn': _jnp.float32, 'w_br_conv': _jnp.float32, 'w_br_mem': _jnp.float32, 'w_out': _jnp.float32}
MOMENT_SCALE = {'norm_g': 6.915830e+00, 'mem_norm_g': 1.403278e-02, 'w_in': 9.377028e-02, 'attn_q_norm': 4.906805e-02, 'attn_k_norm': 4.899751e-02, 'conv_w': 1.746656e+00, 'mem_w_kv': 6.917050e-03, 'mem_q_norm': 1.213095e-01, 'mem_k_norm': 1.209440e-01, 'w_br_attn': 7.873478e-03, 'w_br_conv': 6.645585e-02, 'w_br_mem': 4.811807e-03, 'w_out': 5.576065e-02}


def _to_microbatches(a, axis):
    t = _jnp.moveaxis(a, axis, 0)
    t = t.reshape((N_MICROBATCH, t.shape[0] // N_MICROBATCH) + t.shape[1:])
    return _jnp.moveaxis(t, 1, axis + 1)


def setup_inputs(seed: int = 0) -> dict:
    inp = _fwd_setup_inputs(seed)
    key = _jax.random.fold_in(_jax.random.key(seed), 7919)
    shape, _ = _output_shape()
    out = dict(inp)
    out["loss_target"] = _jax.random.normal(_jax.random.fold_in(key, 0), shape, _jnp.float32)
    for i, name in enumerate(TWIN_WEIGHTS):
        w = inp[name].astype(_jnp.float32)
        if MOMENT_SCALE is None:
            s = _jnp.sqrt(_jnp.mean(_jnp.square(w)) + 1e-30)
        else:
            s = MOMENT_SCALE[name]
        km, kv = _jax.random.split(_jax.random.fold_in(key, i + 1))
        out[name] = w
        out["m_" + name] = s * _jax.random.normal(km, w.shape, _jnp.float32)
        out["v_" + name] = (s * s) * _jax.random.uniform(kv, w.shape, _jnp.float32, 0.5, 1.5)
    if N_MICROBATCH > 1:
        for name, axis in PER_EXAMPLE_BATCH_AXIS.items():
            out[name] = _to_microbatches(out[name], axis)
    return {'x': out['x'], 'mem': out['mem'], 'norm_g': out['norm_g'], 'mem_norm_g': out['mem_norm_g'], 'w_in': out['w_in'], 'attn_q_norm': out['attn_q_norm'], 'attn_k_norm': out['attn_k_norm'], 'conv_w': out['conv_w'], 'mem_w_kv': out['mem_w_kv'], 'mem_q_norm': out['mem_q_norm'], 'mem_k_norm': out['mem_k_norm'], 'w_br_attn': out['w_br_attn'], 'w_br_conv': out['w_br_conv'], 'w_br_mem': out['w_br_mem'], 'w_out': out['w_out'], 'loss_target': out['loss_target'], 'm_norm_g': out['m_norm_g'], 'm_mem_norm_g': out['m_mem_norm_g'], 'm_w_in': out['m_w_in'], 'm_attn_q_norm': out['m_attn_q_norm'], 'm_attn_k_norm': out['m_attn_k_norm'], 'm_conv_w': out['m_conv_w'], 'm_mem_w_kv': out['m_mem_w_kv'], 'm_mem_q_norm': out['m_mem_q_norm'], 'm_mem_k_norm': out['m_mem_k_norm'], 'm_w_br_attn': out['m_w_br_attn'], 'm_w_br_conv': out['m_w_br_conv'], 'm_w_br_mem': out['m_w_br_mem'], 'm_w_out': out['m_w_out'], 'v_norm_g': out['v_norm_g'], 'v_mem_norm_g': out['v_mem_norm_g'], 'v_w_in': out['v_w_in'], 'v_attn_q_norm': out['v_attn_q_norm'], 'v_attn_k_norm': out['v_attn_k_norm'], 'v_conv_w': out['v_conv_w'], 'v_mem_w_kv': out['v_mem_w_kv'], 'v_mem_q_norm': out['v_mem_q_norm'], 'v_mem_k_norm': out['v_mem_k_norm'], 'v_w_br_attn': out['v_w_br_attn'], 'v_w_br_conv': out['v_w_br_conv'], 'v_w_br_mem': out['v_w_br_mem'], 'v_w_out': out['v_w_out']}


def _loss(weights, diff, rest, loss_target):
    with _jax.named_scope("forward"):
        args = {**rest, TWIN_DIFF_INPUT: diff, **{k: w.astype(_WEIGHT_DTYPES[k]) for k, w in weights.items()}}
        y = _forward(args)
    with _jax.named_scope("loss_head"):
        err = _jnp.square(y.astype(_jnp.float32) - loss_target)
        return 0.5 * _jnp.sum(_jnp.mean(err, axis=-1)) if err.ndim else 0.5 * err


def _adamw(w, g, m, v):
    m = ADAM_B1 * m + (1.0 - ADAM_B1) * g
    v = ADAM_B2 * v + (1.0 - ADAM_B2) * _jnp.square(g)
    m_hat = m / (1.0 - ADAM_B1 ** ADAM_STEP)
    v_hat = v / (1.0 - ADAM_B2 ** ADAM_STEP)
    delta = -ADAM_LR * (m_hat / (_jnp.sqrt(v_hat) + ADAM_EPS) + ADAM_WD * w)
    return delta, m, v


def reference(x, mem, norm_g, mem_norm_g, w_in, attn_q_norm, attn_k_norm, conv_w, mem_w_kv, mem_q_norm, mem_k_norm, w_br_attn, w_br_conv, w_br_mem, w_out, loss_target, m_norm_g, m_mem_norm_g, m_w_in, m_attn_q_norm, m_attn_k_norm, m_conv_w, m_mem_w_kv, m_mem_q_norm, m_mem_k_norm, m_w_br_attn, m_w_br_conv, m_w_br_mem, m_w_out, v_norm_g, v_mem_norm_g, v_w_in, v_attn_q_norm, v_attn_k_norm, v_conv_w, v_mem_w_kv, v_mem_q_norm, v_mem_k_norm, v_w_br_attn, v_w_br_conv, v_w_br_mem, v_w_out):
    given = dict(x=x, mem=mem, norm_g=norm_g, mem_norm_g=mem_norm_g, w_in=w_in, attn_q_norm=attn_q_norm, attn_k_norm=attn_k_norm, conv_w=conv_w, mem_w_kv=mem_w_kv, mem_q_norm=mem_q_norm, mem_k_norm=mem_k_norm, w_br_attn=w_br_attn, w_br_conv=w_br_conv, w_br_mem=w_br_mem, w_out=w_out, loss_target=loss_target, m_norm_g=m_norm_g, m_mem_norm_g=m_mem_norm_g, m_w_in=m_w_in, m_attn_q_norm=m_attn_q_norm, m_attn_k_norm=m_attn_k_norm, m_conv_w=m_conv_w, m_mem_w_kv=m_mem_w_kv, m_mem_q_norm=m_mem_q_norm, m_mem_k_norm=m_mem_k_norm, m_w_br_attn=m_w_br_attn, m_w_br_conv=m_w_br_conv, m_w_br_mem=m_w_br_mem, m_w_out=m_w_out, v_norm_g=v_norm_g, v_mem_norm_g=v_mem_norm_g, v_w_in=v_w_in, v_attn_q_norm=v_attn_q_norm, v_attn_k_norm=v_attn_k_norm, v_conv_w=v_conv_w, v_mem_w_kv=v_mem_w_kv, v_mem_q_norm=v_mem_q_norm, v_mem_k_norm=v_mem_k_norm, v_w_br_attn=v_w_br_attn, v_w_br_conv=v_w_br_conv, v_w_br_mem=v_w_br_mem, v_w_out=v_w_out)
    weights = {n: given[n] for n in TWIN_WEIGHTS}
    shared = {n: given[n] for n in SHARED_INPUTS}
    per_example = {n: given[n] for n in ['x', 'mem']}
    grad_fn = _jax.value_and_grad(_loss, argnums=(0, 1))

    def one_microbatch(ex, loss_target):
        ex = dict(ex)
        diff = ex.pop(TWIN_DIFF_INPUT)
        return grad_fn(weights, diff, {**shared, **ex}, loss_target)

    if N_MICROBATCH == 1:
        loss, (grad_w, grad_x) = one_microbatch(per_example, given["loss_target"])
    else:
        def body(carry, xs):
            loss_sum, grad_sum = carry
            l_k, (gw_k, gx_k) = one_microbatch(xs[0], xs[1])
            with _jax.named_scope("update"):
                return (loss_sum + l_k, _jax.tree.map(_jnp.add, grad_sum, gw_k)), gx_k

        init = (_jnp.zeros((), _jnp.float32), _jax.tree.map(_jnp.zeros_like, weights))
        (loss, grad_w), grad_x = _jax.lax.scan(body, init, (per_example, given["loss_target"]))
    with _jax.named_scope("update"):
        delta_w, new_m, new_v = {}, {}, {}
        for n in TWIN_WEIGHTS:
            delta_w[n], new_m[n], new_v[n] = _adamw(weights[n], grad_w[n], given["m_" + n], given["v_" + n])
    return (loss, grad_x, *[grad_w[n] for n in TWIN_WEIGHTS], *[delta_w[n] for n in TWIN_WEIGHTS],
            *[new_m[n] for n in TWIN_WEIGHTS], *[new_v[n] for n in TWIN_WEIGHTS])
```

```python
import functools

import jax
import jax.numpy as jnp
from jax import lax
from jax.experimental import pallas as pl
from jax.experimental.pallas import tpu as pltpu

F32 = jnp.float32
MXU_DTYPE = jnp.bfloat16
WIRE_DTYPE = jnp.bfloat16
EPS = 1e-6
NEG = -1e30

HEAD = 128
HPG = 4
GW = HPG * HEAD
DILATIONS = (1, 4, 16)
NGROUP = len(DILATIONS)
BLK = 128
QKV = NGROUP * GW
CONVW = 1024
MEM_HEADS = 4
MEM_HD = 256
MEMW = MEM_HEADS * MEM_HD
Q0, K0, V0 = 0, QKV, 2 * QKV
ZA = 3 * QKV
CB, CC, CV, ZC = ZA + GW, ZA + GW + CONVW, ZA + GW + 2 * CONVW, ZA + GW + 3 * CONVW
MQ = ZC + CONVW
ZM = MQ + MEMW
G0 = ZM + MEMW

ADAM_LR, ADAM_B1, ADAM_B2, ADAM_EPS, ADAM_WD, ADAM_STEP = 0.001, 0.9, 0.999, 1e-08, 0.01, 10

VMEM_LIMIT = 56 * 1024 * 1024
MESH = pl.DeviceIdType.MESH
ANY = pl.BlockSpec(memory_space=pl.ANY)


def _tile(n, pref, mult=128):
    t = min(pref, n)
    while t > mult and (n % t or t % mult):
        t -= mult
    assert n % t == 0, (n, pref)
    return t


def _call(body, *, name, out_shape, grid=(), in_specs=None, out_specs=None, scratch_shapes=(),
          aliases=None, grid_spec=None):
    kw = {}
    if grid_spec is not None:
        kw["grid_spec"] = grid_spec
        ngrid = len(grid_spec.grid)
    else:
        kw.update(grid=grid, in_specs=in_specs, out_specs=out_specs, scratch_shapes=list(scratch_shapes))
        ngrid = len(grid)
    params = pltpu.CompilerParams(dimension_semantics=("arbitrary",) * ngrid, vmem_limit_bytes=VMEM_LIMIT)
    return pl.pallas_call(body, name=name, out_shape=out_shape, compiler_params=params,
                          input_output_aliases=aliases or {}, **kw)


_DIMS = {"nn": (((1,), (0,)), ((), ())), "nt": (((1,), (1,)), ((), ())), "tn": (((0,), (0,)), ((), ()))}


def _mxu(a, b, mode):
    return lax.dot_general(a.astype(MXU_DTYPE), b.astype(MXU_DTYPE), _DIMS[mode], preferred_element_type=F32)


@functools.partial(jax.custom_vjp, nondiff_argnums=(2,))
def _dot(a, b, mode):
    return _mxu(a, b, mode)


def _dot_fwd(a, b, mode):
    return _mxu(a, b, mode), (a, b)


def _dot_bwd(mode, res, g):
    a, b = res
    if mode == "nn":
        return _mxu(g, b, "nt"), _mxu(a, g, "tn")
    if mode == "nt":
        return _mxu(g, b, "nn"), _mxu(g, a, "tn")
    return _mxu(b, g, "nt"), _mxu(a, g, "nn")


_dot.defvjp(_dot_fwd, _dot_bwd)


def _sig(z):
    return 1.0 / (1.0 + jnp.exp(-z))


def _silu(z):
    return z * _sig(z)


def _rms_rows(t, g):
    return t * lax.rsqrt(jnp.mean(t * t, axis=-1, keepdims=True) + EPS) * g


def _attn_block(q, k2, v2, gq, gk, first):
    qn = _rms_rows(q, gq)
    kn = _rms_rows(k2, gk)
    s = _dot(qn, kn, "nt") * (HEAD ** -0.5)
    a = lax.broadcasted_iota(jnp.int32, (BLK, 2 * BLK), 0)
    b = lax.broadcasted_iota(jnp.int32, (BLK, 2 * BLK), 1)
    lo = jnp.where(first, BLK, 0)
    mask = (b >= a) & (b <= a + BLK) & (b >= lo)
    s = jnp.where(mask, s, NEG)
    m = lax.stop_gradient(jnp.max(s, axis=-1, keepdims=True))
    p = jnp.exp(s - m)
    den = jnp.sum(p, axis=-1, keepdims=True)
    o = _dot(p, v2, "nn") / den
    return o, m + jnp.log(den)


def _combine(o1, o2, o3, l1, l2, l3, z):
    m = lax.stop_gradient(jnp.maximum(jnp.maximum(l1, l2), l3))
    e1, e2, e3 = jnp.exp(l1 - m), jnp.exp(l2 - m), jnp.exp(l3 - m)
    return (e1 * o1 + e2 * o2 + e3 * o3) / (e1 + e2 + e3) * _silu(z)


def _mem_block(q, z, kv, gq, gk):
    outs = []
    for h in range(MEM_HEADS):
        sl = slice(h * MEM_HD, (h + 1) * MEM_HD)
        qn = _rms_rows(q[:, sl], gq)
        kn = _rms_rows(kv[:, sl], gk)
        s = _dot(qn, kn, "nt") * (MEM_HD ** -0.5)
        m = lax.stop_gradient(jnp.max(s, axis=-1, keepdims=True))
        p = jnp.exp(s - m)
        den = jnp.sum(p, axis=-1, keepdims=True)
        outs.append(_dot(p, kv[:, MEMW + h * MEM_HD:MEMW + (h + 1) * MEM_HD], "nn") / den)
    return jnp.concatenate(outs, axis=-1) * _silu(z)


def _cast(w, name):
    R, C = w.shape
    tr, tc = _tile(R, 512, 8), _tile(C, 2176)

    def body(w_ref, o_ref):
        o_ref[...] = w_ref[...].astype(o_ref.dtype)

    spec = pl.BlockSpec((tr, tc), lambda i, j: (i, j))
    return _call(body, name=name, grid=(R // tr, C // tc), in_specs=[spec], out_specs=spec,
                 out_shape=jax.ShapeDtypeStruct((R, C), WIRE_DTYPE))(w)


def _matmul(a, b, mode, out_dtype, *, name, tm=512, tn=512, tk=512):
    if mode == "nn":
        (M, K), (_, N) = a.shape, b.shape
    elif mode == "nt":
        (M, K), (N, _) = a.shape, b.shape
    else:
        (K, M), (_, N) = a.shape, b.shape
    tm, tn, tk = _tile(M, tm), _tile(N, tn), _tile(K, tk)
    nk = K // tk

    def body(a_ref, b_ref, o_ref, *acc):
        part = lax.dot_general(a_ref[...], b_ref[...], _DIMS[mode], preferred_element_type=F32)
        if nk == 1:
            o_ref[...] = part.astype(o_ref.dtype)
            return
        acc_ref, = acc
        k = pl.program_id(2)

        @pl.when(k == 0)
        def _():
            acc_ref[...] = part

        @pl.when(k > 0)
        def _():
            acc_ref[...] += part

        @pl.when(k == nk - 1)
        def _():
            o_ref[...] = acc_ref[...].astype(o_ref.dtype)

    a_spec = pl.BlockSpec((tk, tm), lambda i, j, k: (k, i)) if mode == "tn" else pl.BlockSpec((tm, tk), lambda i, j, k: (i, k))
    b_spec = pl.BlockSpec((tn, tk), lambda i, j, k: (j, k)) if mode == "nt" else pl.BlockSpec((tk, tn), lambda i, j, k: (k, j))
    return _call(body, name=name, grid=(M // tm, N // tn, nk), in_specs=[a_spec, b_spec],
                 out_specs=pl.BlockSpec((tm, tn), lambda i, j, k: (i, j)),
                 out_shape=jax.ShapeDtypeStruct((M, N), out_dtype),
                 scratch_shapes=[] if nk == 1 else [pltpu.VMEM((tm, tn), F32)])(a, b)


def _rms_fwd(x, g, name):
    R, D = x.shape
    tr = _tile(R, 512)

    def body(x_ref, g_ref, o_ref):
        o_ref[...] = _rms_rows(x_ref[...], g_ref[...]).astype(o_ref.dtype)

    row = pl.BlockSpec((tr, D), lambda i: (i, 0))
    return _call(body, name=name, grid=(R // tr,), in_specs=[row, pl.BlockSpec((1, D), lambda i: (0, 0))],
                 out_specs=row, out_shape=jax.ShapeDtypeStruct((R, D), MXU_DTYPE))(x, g)


def _rms_bwd(x, dh, g, dy, name):
    R, D = x.shape
    tr = _tile(R, 256)
    with_dx = dy is not None

    def body(*refs):
        if with_dx:
            x_ref, dh_ref, g_ref, dy_ref, dx_ref, dg_ref = refs
        else:
            x_ref, dh_ref, g_ref, dg_ref = refs
        xv, dhv = x_ref[...], dh_ref[...]
        r = lax.rsqrt(jnp.mean(xv * xv, axis=-1, keepdims=True) + EPS)
        xh = xv * r

        @pl.when(pl.program_id(0) == 0)
        def _():
            dg_ref[...] = jnp.zeros_like(dg_ref)

        dg_ref[...] += jnp.sum(dhv * xh, axis=0, keepdims=True)
        if with_dx:
            dxh = dhv * g_ref[...]
            dx_ref[...] = dy_ref[...] + r * (dxh - xh * jnp.mean(dxh * xh, axis=-1, keepdims=True))

    row = pl.BlockSpec((tr, D), lambda i: (i, 0))
    vec = pl.BlockSpec((1, D), lambda i: (0, 0))
    dg_shape = jax.ShapeDtypeStruct((1, D), F32)
    if with_dx:
        return _call(body, name=name, grid=(R // tr,), in_specs=[row, row, vec, row], out_specs=[row, vec],
                     out_shape=[jax.ShapeDtypeStruct((R, D), F32), dg_shape])(x, dh, g, dy)
    return None, _call(body, name=name, grid=(R // tr,), in_specs=[row, row, vec], out_specs=vec,
                       out_shape=dg_shape)(x, dh, g)


def _attn_specs(d, IN, g):
    per = IN // GW
    cq, ck, cv = (Q0 + g * GW) // GW, (K0 + g * GW) // GW, (V0 + g * GW) // GW
    blk = (1, BLK, GW)
    return blk, per, cq, ck, cv


def _attn_fwd(proj3, gq, gk, g, d):
    Bl, L, W = proj3.shape
    blk, per, cq, ck, cv = _attn_specs(d, W // d, g)
    nb = L // BLK

    def body(q_ref, kp_ref, kc_ref, vp_ref, vc_ref, gq_ref, gk_ref, o_ref, lse_ref):
        first = pl.program_id(2) == 0
        for h in range(HPG):
            sl = slice(h * HEAD, (h + 1) * HEAD)
            k2 = jnp.concatenate([kp_ref[0, :, sl], kc_ref[0, :, sl]], axis=0)
            v2 = jnp.concatenate([vp_ref[0, :, sl], vc_ref[0, :, sl]], axis=0)
            o, lse = _attn_block(q_ref[0, :, sl], k2, v2, gq_ref[...], gk_ref[...], first)
            o_ref[0, :, sl] = o
            lse_ref[0, :, sl] = jnp.broadcast_to(lse, (BLK, HEAD))

    def cur(c0):
        return pl.BlockSpec(blk, lambda b, r, i: (b, i, r * per + c0))

    def prev(c0):
        return pl.BlockSpec(blk, lambda b, r, i: (b, jnp.maximum(i - 1, 0), r * per + c0))

    vec = pl.BlockSpec((1, HEAD), lambda b, r, i: (0, 0))
    out = pl.BlockSpec(blk, lambda b, r, i: (b, i, r))
    shp = jax.ShapeDtypeStruct((Bl, L, d * GW), F32)
    return _call(body, name=f"attn_fwd_g{g}", grid=(Bl, d, nb),
                 in_specs=[cur(cq), prev(ck), cur(ck), prev(cv), cur(cv), vec, vec],
                 out_specs=[out, out], out_shape=[shp, shp])(proj3, proj3, proj3, proj3, proj3, gq, gk)


def _attn_bwd(proj3, gq, gk, do3, dl3, g, d):
    Bl, L, W = proj3.shape
    blk, per, cq, ck, cv = _attn_specs(d, W // d, g)
    nb = L // BLK

    def body(q_ref, kp_ref, kc_ref, vp_ref, vc_ref, gq_ref, gk_ref, do_ref, dl_ref,
             dq_ref, dk_ref, dv_ref, dgq_ref, dgk_ref, ck_ref, cv_ref):
        i = pl.program_id(2)
        first = i == 0

        @pl.when((pl.program_id(0) == 0) & (pl.program_id(1) == 0) & first)
        def _():
            dgq_ref[...] = jnp.zeros_like(dgq_ref)
            dgk_ref[...] = jnp.zeros_like(dgk_ref)

        @pl.when(i < nb)
        def _():
            for h in range(HPG):
                sl = slice(h * HEAD, (h + 1) * HEAD)
                k2 = jnp.concatenate([kp_ref[0, :, sl], kc_ref[0, :, sl]], axis=0)
                v2 = jnp.concatenate([vp_ref[0, :, sl], vc_ref[0, :, sl]], axis=0)
                _, vjp = jax.vjp(lambda q, k, v, a, b: _attn_block(q, k, v, a, b, first),
                                 q_ref[0, :, sl], k2, v2, gq_ref[...], gk_ref[...])
                dq, dk2, dv2, dgq, dgk = vjp((do_ref[0, :, sl], dl_ref[0, :, sl][:, :1]))
                dq_ref[0, :, sl] = dq.astype(dq_ref.dtype)
                dgq_ref[...] += dgq
                dgk_ref[...] += dgk

                @pl.when(i > 0)
                def _():
                    dk_ref[0, :, sl] = (ck_ref[:, sl] + dk2[:BLK]).astype(dk_ref.dtype)
                    dv_ref[0, :, sl] = (cv_ref[:, sl] + dv2[:BLK]).astype(dv_ref.dtype)

                ck_ref[:, sl] = dk2[BLK:]
                cv_ref[:, sl] = dv2[BLK:]

        @pl.when(i == nb)
        def _():
            dk_ref[0] = ck_ref[...].astype(dk_ref.dtype)
            dv_ref[0] = cv_ref[...].astype(dv_ref.dtype)

    def cur(c0):
        return pl.BlockSpec(blk, lambda b, r, i: (b, jnp.minimum(i, nb - 1), r * per + c0))

    def prev(c0):
        return pl.BlockSpec(blk, lambda b, r, i: (b, jnp.clip(i - 1, 0, nb - 1), r * per + c0))

    vec = pl.BlockSpec((1, HEAD), lambda b, r, i: (0, 0))
    at_q = pl.BlockSpec(blk, lambda b, r, i: (b, jnp.minimum(i, nb - 1), r))
    at_k = pl.BlockSpec(blk, lambda b, r, i: (b, jnp.maximum(i - 1, 0), r))
    shp = jax.ShapeDtypeStruct((Bl, L, d * GW), MXU_DTYPE)
    gshp = jax.ShapeDtypeStruct((1, HEAD), F32)
    return _call(body, name=f"attn_bwd_g{g}", grid=(Bl, d, nb + 1),
                 in_specs=[cur(cq), prev(ck), cur(ck), prev(cv), cur(cv), vec, vec, at_q, at_q],
                 out_specs=[at_q, at_k, at_k, vec, vec], out_shape=[shp, shp, shp, gshp, gshp],
                 scratch_shapes=[pltpu.VMEM((BLK, GW), F32), pltpu.VMEM((BLK, GW), F32)],
                 )(proj3, proj3, proj3, proj3, proj3, gq, gk, do3, dl3)


def _combine_fwd(os, ls, proj2):
    T = proj2.shape[0]
    tr = _tile(T, 512)

    def body(o1, o2, o3, l1, l2, l3, z, a_ref):
        a_ref[...] = _combine(o1[...], o2[...], o3[...], l1[...], l2[...], l3[...], z[...]).astype(a_ref.dtype)

    row = pl.BlockSpec((tr, GW), lambda i: (i, 0))
    return _call(body, name="combine_fwd", grid=(T // tr,),
                 in_specs=[row] * 6 + [pl.BlockSpec((tr, GW), lambda i: (i, ZA // GW))], out_specs=row,
                 out_shape=jax.ShapeDtypeStruct((T, GW), MXU_DTYPE))(*os, *ls, proj2)


def _combine_bwd(os, ls, proj2, da):
    T = proj2.shape[0]
    tr = _tile(T, 256)

    def body(o1, o2, o3, l1, l2, l3, z, da_ref, d1, d2, d3, e1, e2, e3, dz_ref):
        _, vjp = jax.vjp(_combine, o1[...], o2[...], o3[...], l1[...], l2[...], l3[...], z[...])
        go1, go2, go3, gl1, gl2, gl3, gz = vjp(da_ref[...])
        d1[...], d2[...], d3[...] = go1, go2, go3
        dz_ref[...] = gz.astype(dz_ref.dtype)
        for ref, gl in ((e1, gl1), (e2, gl2), (e3, gl3)):
            for h in range(HPG):
                sl = slice(h * HEAD, (h + 1) * HEAD)
                ref[:, sl] = jnp.broadcast_to(jnp.sum(gl[:, sl], axis=-1, keepdims=True), (tr, HEAD))

    row = pl.BlockSpec((tr, GW), lambda i: (i, 0))
    f = jax.ShapeDtypeStruct((T, GW), F32)
    outs = _call(body, name="combine_bwd", grid=(T // tr,),
                 in_specs=[row] * 6 + [pl.BlockSpec((tr, GW), lambda i: (i, ZA // GW)), row],
                 out_specs=[row] * 7, out_shape=[f] * 6 + [jax.ShapeDtypeStruct((T, GW), MXU_DTYPE)],
                 )(*os, *ls, proj2, da)
    return outs[:3], outs[3:6], outs[6]


def _shift_down(u, j, t):
    return jnp.where(t >= j, pltpu.roll(u, j, 0), 0.0)


def _shift_up(u, j, t):
    n = u.shape[0]
    return jnp.where(t < n - j, pltpu.roll(u, n - j, 0), 0.0)


def _conv_specs(Bl, S, cw):
    def sec(c0):
        return pl.BlockSpec((1, S, cw), lambda j, b: (b, 0, c0 // cw + j))
    return [sec(CB), sec(CC), sec(CV), sec(ZC)], pl.BlockSpec((3, cw), lambda j, b: (0, j))


def _conv_fwd(proj3, conv_w):
    Bl, S, _ = proj3.shape
    cw = 256
    secs, wspec = _conv_specs(Bl, S, cw)

    def body(b_ref, c_ref, v_ref, z_ref, w_ref, o_ref):
        t = lax.broadcasted_iota(jnp.int32, (S, cw), 0)
        u = c_ref[0] * v_ref[0]
        y = w_ref[0:1, :] * u + w_ref[1:2, :] * _shift_down(u, 1, t) + w_ref[2:3, :] * _shift_down(u, 2, t)
        o_ref[0] = (b_ref[0] * y * _silu(z_ref[0])).astype(o_ref.dtype)

    return _call(body, name="conv_fwd", grid=(CONVW // cw, Bl), in_specs=secs + [wspec],
                 out_specs=pl.BlockSpec((1, S, cw), lambda j, b: (b, 0, j)),
                 out_shape=jax.ShapeDtypeStruct((Bl, S, CONVW), MXU_DTYPE))(proj3, proj3, proj3, proj3, conv_w)


def _conv_bwd(proj3, conv_w, dcc3):
    Bl, S, _ = proj3.shape
    cw = 256
    secs, wspec = _conv_specs(Bl, S, cw)

    def body(b_ref, c_ref, v_ref, z_ref, w_ref, d_ref, db_ref, dc_ref, dv_ref, dz_ref, dw_ref):
        t = lax.broadcasted_iota(jnp.int32, (S, cw), 0)
        bv, cv, vv, zv, dv = b_ref[0], c_ref[0], v_ref[0], z_ref[0], d_ref[0]
        u = cv * vv
        u1, u2 = _shift_down(u, 1, t), _shift_down(u, 2, t)
        y = w_ref[0:1, :] * u + w_ref[1:2, :] * u1 + w_ref[2:3, :] * u2
        sg = _sig(zv)
        sz = zv * sg
        gy = dv * bv * sz
        db_ref[0] = (dv * y * sz).astype(db_ref.dtype)
        dz_ref[0] = (dv * bv * y * sg * (1.0 + zv * (1.0 - sg))).astype(dz_ref.dtype)
        du = w_ref[0:1, :] * gy + w_ref[1:2, :] * _shift_up(gy, 1, t) + w_ref[2:3, :] * _shift_up(gy, 2, t)
        dc_ref[0] = (du * vv).astype(dc_ref.dtype)
        dv_ref[0] = (du * cv).astype(dv_ref.dtype)

        @pl.when(pl.program_id(1) == 0)
        def _():
            dw_ref[...] = jnp.zeros_like(dw_ref)

        dw_ref[0:1, :] += jnp.sum(gy * u, axis=0, keepdims=True)
        dw_ref[1:2, :] += jnp.sum(gy * u1, axis=0, keepdims=True)
        dw_ref[2:3, :] += jnp.sum(gy * u2, axis=0, keepdims=True)

    blk = pl.BlockSpec((1, S, cw), lambda j, b: (b, 0, j))
    shp = jax.ShapeDtypeStruct((Bl, S, CONVW), MXU_DTYPE)
    return _call(body, name="conv_bwd", grid=(CONVW // cw, Bl), in_specs=secs + [wspec, blk],
                 out_specs=[blk] * 4 + [wspec], out_shape=[shp] * 4 + [jax.ShapeDtypeStruct((3, CONVW), F32)],
                 )(proj3, proj3, proj3, proj3, conv_w, dcc3)


def _mem_specs(S, tq):
    q = pl.BlockSpec((1, tq, MEMW), lambda b, j: (b, j, MQ // MEMW))
    z = pl.BlockSpec((1, tq, MEMW), lambda b, j: (b, j, ZM // MEMW))
    kv = pl.BlockSpec((1, MEM_HD, 2 * MEMW), lambda b, j: (b, 0, 0))
    vec = pl.BlockSpec((1, MEM_HD), lambda b, j: (0, 0))
    blk = pl.BlockSpec((1, tq, MEMW), lambda b, j: (b, j, 0))
    return q, z, kv, vec, blk


def _mem_fwd(proj3, mkv3, gq, gk):
    Bl, S, _ = proj3.shape
    tq = _tile(S, 512)
    q, z, kv, vec, blk = _mem_specs(S, tq)

    def body(q_ref, z_ref, kv_ref, gq_ref, gk_ref, o_ref):
        o_ref[0] = _mem_block(q_ref[0], z_ref[0], kv_ref[0], gq_ref[...], gk_ref[...]).astype(o_ref.dtype)

    return _call(body, name="mem_fwd", grid=(Bl, S // tq), in_specs=[q, z, kv, vec, vec], out_specs=blk,
                 out_shape=jax.ShapeDtypeStruct((Bl, S, MEMW), MXU_DTYPE))(proj3, proj3, mkv3, gq, gk)


def _mem_bwd(proj3, mkv3, gq, gk, dmo3):
    Bl, S, _ = proj3.shape
    tq = _tile(S, 256)
    q, z, kv, vec, blk = _mem_specs(S, tq)

    def body(q_ref, z_ref, kv_ref, gq_ref, gk_ref, d_ref, dq_ref, dz_ref, dkv_ref, dgq_ref, dgk_ref):
        _, vjp = jax.vjp(_mem_block, q_ref[0], z_ref[0], kv_ref[0], gq_ref[...], gk_ref[...])
        dq, dz, dkv, dgq, dgk = vjp(d_ref[0])
        dq_ref[0] = dq.astype(dq_ref.dtype)
        dz_ref[0] = dz.astype(dz_ref.dtype)
        j = pl.program_id(1)

        @pl.when(j == 0)
        def _():
            dkv_ref[0] = jnp.zeros_like(dkv)

        @pl.when((j == 0) & (pl.program_id(0) == 0))
        def _():
            dgq_ref[...] = jnp.zeros_like(dgq_ref)
            dgk_ref[...] = jnp.zeros_like(dgk_ref)

        dkv_ref[0] += dkv
        dgq_ref[...] += dgq
        dgk_ref[...] += dgk

    shp = jax.ShapeDtypeStruct((Bl, S, MEMW), MXU_DTYPE)
    gshp = jax.ShapeDtypeStruct((1, MEM_HD), F32)
    return _call(body, name="mem_bwd", grid=(Bl, S // tq), in_specs=[q, z, kv, vec, vec, blk],
                 out_specs=[blk, blk, kv, vec, vec],
                 out_shape=[shp, shp, jax.ShapeDtypeStruct(mkv3.shape, F32), gshp, gshp],
                 )(proj3, proj3, mkv3, gq, gk, dmo3)


def _merge_specs(T, D, tm, tn):
    def act(w):
        return pl.BlockSpec((tm, w), lambda i, n: (i, 0))

    def wsp(w):
        return pl.BlockSpec((w, tn), lambda i, n: (0, n))

    gates = [pl.BlockSpec((tm, tn), lambda i, n, k=k: (i, (G0 + k * D) // tn + n)) for k in range(3)]
    tile = pl.BlockSpec((tm, tn), lambda i, n: (i, n))
    return act, wsp, gates, tile


def _merge_fwd(a, cc, mo, wa, wc, wm, proj2):
    T, D = a.shape[0], wa.shape[1]
    tm, tn = _tile(T, 512), _tile(D, 512)
    act, wsp, gates, tile = _merge_specs(T, D, tm, tn)

    def body(a_ref, c_ref, m_ref, wa_ref, wc_ref, wm_ref, g0, g1, g2, mg_ref, pa_ref, pc_ref, pm_ref):
        pa = jnp.dot(a_ref[...], wa_ref[...], preferred_element_type=F32)
        pc = jnp.dot(c_ref[...], wc_ref[...], preferred_element_type=F32)
        pm = jnp.dot(m_ref[...], wm_ref[...], preferred_element_type=F32)
        mg_ref[...] = (_sig(g0[...]) * pa + _sig(g1[...]) * pc + _sig(g2[...]) * pm).astype(mg_ref.dtype)
        pa_ref[...] = pa.astype(pa_ref.dtype)
        pc_ref[...] = pc.astype(pc_ref.dtype)
        pm_ref[...] = pm.astype(pm_ref.dtype)

    shp = jax.ShapeDtypeStruct((T, D), MXU_DTYPE)
    return _call(body, name="merge_fwd", grid=(T // tm, D // tn),
                 in_specs=[act(GW), act(CONVW), act(MEMW), wsp(GW), wsp(CONVW), wsp(MEMW)] + gates,
                 out_specs=[tile] * 4, out_shape=[shp] * 4)(a, cc, mo, wa, wc, wm, proj2, proj2, proj2)


def _merge_bwd(dyb, w_out, proj2, pa, pc, pm):
    T, D = dyb.shape
    tm, tn = _tile(T, 512), _tile(D, 512)
    _, _, gates, tile = _merge_specs(T, D, tm, tn)

    def body(dy_ref, w_ref, g0, g1, g2, p0, p1, p2, dp0, dp1, dp2, dg0, dg1, dg2):
        dm = lax.dot_general(dy_ref[...], w_ref[...], _DIMS["nt"], preferred_element_type=F32)
        for g_ref, p_ref, dp_ref, dg_ref in ((g0, p0, dp0, dg0), (g1, p1, dp1, dg1), (g2, p2, dp2, dg2)):
            gt = _sig(g_ref[...])
            dp_ref[...] = (gt * dm).astype(dp_ref.dtype)
            dg_ref[...] = (dm * p_ref[...].astype(F32) * gt * (1.0 - gt)).astype(dg_ref.dtype)

    shp = jax.ShapeDtypeStruct((T, D), MXU_DTYPE)
    return _call(body, name="merge_bwd", grid=(T // tm, D // tn),
                 in_specs=[pl.BlockSpec((tm, D), lambda i, n: (i, 0)), pl.BlockSpec((tn, D), lambda i, n: (n, 0))]
                 + gates + [tile] * 3,
                 out_specs=[tile] * 6, out_shape=[shp] * 6)(dyb, w_out, proj2, proj2, proj2, pa, pc, pm)


def _out_loss(merged, w_out, x, tgt):
    T, D = x.shape
    tm = _tile(T, 256)

    def body(m_ref, w_ref, x_ref, t_ref, dy_ref, dyb_ref, loss_ref):
        err = x_ref[...] + jnp.dot(m_ref[...], w_ref[...], preferred_element_type=F32) - t_ref[...]
        dy = err * (1.0 / D)
        dy_ref[...] = dy
        dyb_ref[...] = dy.astype(dyb_ref.dtype)

        @pl.when(pl.program_id(0) == 0)
        def _():
            loss_ref[...] = jnp.zeros_like(loss_ref)

        loss_ref[...] += jnp.sum(err * err) * (0.5 / D)

    row = pl.BlockSpec((tm, D), lambda i: (i, 0))
    return _call(body, name="out_loss", grid=(T // tm,),
                 in_specs=[row, pl.BlockSpec((D, D), lambda i: (0, 0)), row, row],
                 out_specs=[row, row, pl.BlockSpec((1, 128), lambda i: (0, 0))],
                 out_shape=[jax.ShapeDtypeStruct((T, D), F32), jax.ShapeDtypeStruct((T, D), MXU_DTYPE),
                            jax.ShapeDtypeStruct((1, 128), F32)])(merged, w_out, x, tgt)


def _local_step(x, mem, tgt, norm_g, mem_norm_g, gq_all, gk_all, conv_w, mem_gq, mem_gk, W):
    Bl, S, D = x.shape
    T = Bl * S
    IN = W["w_in"].shape[1]
    x2, tgt2 = x.reshape(T, D), tgt.reshape(T, D)
    mem2 = mem.reshape(-1, D)
    ng, mng = norm_g.reshape(1, D), mem_norm_g.reshape(1, D)
    mgq, mgk = mem_gq.reshape(1, MEM_HD), mem_gk.reshape(1, MEM_HD)
    gqs = [gq_all[g:g + 1] for g in range(NGROUP)]
    gks = [gk_all[g:g + 1] for g in range(NGROUP)]

    hb = _rms_fwd(x2, ng, "rms_x")
    proj2 = _matmul(hb, W["w_in"], "nn", F32, name="proj", tm=2048, tn=512, tk=D)
    proj3 = proj2.reshape(Bl, S, IN)
    views = [proj2.reshape(Bl, S // d, d * IN) for d in DILATIONS]
    os, ls = [], []
    for g, d in enumerate(DILATIONS):
        o, l = _attn_fwd(views[g], gqs[g], gks[g], g, d)
        os.append(o.reshape(T, GW))
        ls.append(l.reshape(T, GW))
    a = _combine_fwd(os, ls, proj2)
    cc = _conv_fwd(proj3, conv_w).reshape(T, CONVW)
    mhb = _rms_fwd(mem2, mng, "rms_mem")
    mkv = _matmul(mhb, W["mem_w_kv"], "nn", F32, name="mem_kv", tm=512, tn=1024, tk=D)
    mkv3 = mkv.reshape(Bl, -1, 2 * MEMW)
    mo = _mem_fwd(proj3, mkv3, mgq, mgk).reshape(T, MEMW)
    merged, pa, pc, pm = _merge_fwd(a, cc, mo, W["w_br_attn"], W["w_br_conv"], W["w_br_mem"], proj2)
    dy, dyb, loss = _out_loss(merged, W["w_out"], x2, tgt2)

    G = {}
    G["w_out"] = _matmul(merged, dyb, "tn", WIRE_DTYPE, name="dw_out", tm=1024, tn=1024, tk=512)
    dpa, dpc, dpm, dg0, dg1, dg2 = _merge_bwd(dyb, W["w_out"], proj2, pa, pc, pm)
    G["w_br_attn"] = _matmul(a, dpa, "tn", WIRE_DTYPE, name="dw_br_attn", tm=512, tn=1024, tk=512)
    G["w_br_conv"] = _matmul(cc, dpc, "tn", WIRE_DTYPE, name="dw_br_conv", tm=1024, tn=1024, tk=512)
    G["w_br_mem"] = _matmul(mo, dpm, "tn", WIRE_DTYPE, name="dw_br_mem", tm=1024, tn=1024, tk=512)
    da = _matmul(dpa, W["w_br_attn"], "nt", F32, name="d_attn", tm=1024, tn=512, tk=D)
    dcc = _matmul(dpc, W["w_br_conv"], "nt", F32, name="d_conv", tm=1024, tn=1024, tk=D)
    dmo = _matmul(dpm, W["w_br_mem"], "nt", F32, name="d_mem", tm=1024, tn=1024, tk=D)

    dos, dls, dza = _combine_bwd(os, ls, proj2, da)
    dqs, dks, dvs, dgq, dgk = [], [], [], [], []
    for g, d in enumerate(DILATIONS):
        shape3 = (Bl, S // d, d * GW)
        dq, dk, dv, gq_g, gk_g = _attn_bwd(views[g], gqs[g], gks[g], dos[g].reshape(shape3),
                                           dls[g].reshape(shape3), g, d)
        dqs.append(dq.reshape(T, GW))
        dks.append(dk.reshape(T, GW))
        dvs.append(dv.reshape(T, GW))
        dgq.append(gq_g)
        dgk.append(gk_g)
    dcb, dcc_, dcv, dzc, dconv_w = _conv_bwd(proj3, conv_w, dcc.reshape(Bl, S, CONVW))
    dmq, dzm, dmkv3, dmgq, dmgk = _mem_bwd(proj3, mkv3, mgq, mgk, dmo.reshape(Bl, S, MEMW))

    dmkv = _cast(dmkv3.reshape(-1, 2 * MEMW), "cast_dmkv")
    G["mem_w_kv"] = _matmul(mhb, dmkv, "tn", WIRE_DTYPE, name="dw_mem_kv", tm=1024, tn=1024, tk=512)
    dmh = _matmul(dmkv, W["mem_w_kv"], "nt", F32, name="d_memh", tm=512, tn=1024, tk=2 * MEMW)
    _, dmng = _rms_bwd(mem2, dmh, mng, None, "rms_mem_bwd")

    dproj = jnp.concatenate(dqs + dks + dvs + [dza] + [t.reshape(T, CONVW) for t in (dcb, dcc_, dcv, dzc)]
                            + [dmq.reshape(T, MEMW), dzm.reshape(T, MEMW), dg0, dg1, dg2], axis=1)
    G["w_in"] = _matmul(hb, dproj, "tn", WIRE_DTYPE, name="dw_in", tm=1024, tn=1024, tk=512)
    dh = _matmul(dproj, W["w_in"], "nt", F32, name="d_h", tm=1024, tn=1024, tk=1024)
    grad_x, dng = _rms_bwd(x2, dh, ng, dy, "rms_x_bwd")

    small = [loss, dng, dmng] + dgq + dgk + [dconv_w.reshape(1, 3 * CONVW), dmgq, dmgk]
    return grad_x.reshape(Bl, S, D), G, jnp.concatenate(small, axis=1)


BIG = (("w_in", "col"), ("mem_w_kv", "row"), ("w_br_attn", "col"), ("w_br_conv", "col"),
       ("w_br_mem", "col"), ("w_out", "row"))


def _coords():
    return lax.axis_index("x"), lax.axis_index("y"), lax.axis_index("c")


def _other_chips(x, y):
    return [(1 - x, y), (x, 1 - y), (1 - x, 1 - y)]


def _half(ref, kind, c):
    R, C = ref.shape
    if kind == "col":
        return ref.at[pl.ds(c * (R // 2), R // 2), :]
    return ref.at[:, pl.ds(c * (C // 2), C // 2)]


def _shard(ref, kind, s):
    R, C = ref.shape
    if kind == "col":
        return ref.at[:, pl.ds(s * (C // 4), C // 4)]
    return ref.at[pl.ds(s * (R // 4), R // 4), :]


def _piece(ref, kind, s, c):
    R, C = ref.shape
    if kind == "col":
        return ref.at[pl.ds(c * (R // 2), R // 2), pl.ds(s * (C // 4), C // 4)]
    return ref.at[pl.ds(s * (R // 4), R // 4), pl.ds(c * (C // 2), C // 2)]


def _remote(src, dst, sems_s, sems_r, k, dev):
    return pltpu.make_async_remote_copy(src_ref=src, dst_ref=dst, send_sem=sems_s.at[k], recv_sem=sems_r.at[k],
                                        device_id=dev, device_id_type=MESH)


def _gather_weights(shards, conv_shard):
    n = len(BIG)
    full_shapes = []
    for (name, kind), sh in zip(BIG, shards):
        R, C = sh.shape
        full_shapes.append(jax.ShapeDtypeStruct((R, 4 * C) if kind == "col" else (4 * R, C), sh.dtype))
    cR, cC = conv_shard.shape
    full_shapes.append(jax.ShapeDtypeStruct((cR, 4 * cC), conv_shard.dtype))

    def body(*refs):
        ins, outs = refs[:n + 1], refs[n + 1:2 * n + 2]
        send, recv, loc = refs[2 * n + 2:]
        x, y, c = _coords()
        me = 2 * x + y
        chips = _other_chips(x, y)
        sib = (x, y, 1 - c)
        kinds = [k for _, k in BIG]
        local = [pltpu.make_async_copy(ins[p], _shard(outs[p], kinds[p], me), loc.at[p]) for p in range(n)]
        local.append(pltpu.make_async_copy(ins[n], _shard(outs[n], "col", me), loc.at[n]))
        for cp in local:
            cp.start()
        first = []
        for p in range(n):
            for k, chip in enumerate(chips):
                first.append(_remote(_half(ins[p], kinds[p], c), _piece(outs[p], kinds[p], me, c),
                                     send, recv, 6 * p + k, (*chip, c)))
        for k, chip in enumerate(chips):
            first.append(_remote(ins[n], _shard(outs[n], "col", me), send, recv, 6 * n + k, (*chip, c)))
        for cp in first:
            cp.start()
        passed = []
        for k, chip in enumerate(chips):
            s = 2 * chip[0] + chip[1]
            for p in range(n):
                got = _piece(outs[p], kinds[p], s, c)
                _remote(got, got, send, recv, 6 * p + k, sib).wait_recv()
                fwd = _remote(got, got, send, recv, 6 * p + 3 + k, sib)
                fwd.start()
                passed.append(fwd)
            got = _shard(outs[n], "col", s)
            _remote(got, got, send, recv, 6 * n + k, sib).wait_recv()
        for k, chip in enumerate(chips):
            s = 2 * chip[0] + chip[1]
            for p in range(n):
                got = _piece(outs[p], kinds[p], s, 1 - c)
                _remote(got, got, send, recv, 6 * p + 3 + k, sib).wait_recv()
        for cp in first + passed:
            cp.wait_send()
        for cp in local:
            cp.wait()

    nsem = 6 * n + 3
    return pl.pallas_call(
        body, name="gather_weights", out_shape=full_shapes,
        in_specs=[ANY] * (n + 1), out_specs=[ANY] * (n + 1),
        scratch_shapes=[pltpu.SemaphoreType.DMA((nsem,)), pltpu.SemaphoreType.DMA((nsem,)),
                        pltpu.SemaphoreType.DMA((n + 1,))],
    )(*shards, conv_shard)


def _sibling_exchange(grads):
    n = len(BIG)
    shapes = []
    for (name, kind), g in zip(BIG, grads):
        R, C = g.shape
        shapes.append(jax.ShapeDtypeStruct((R // 2, C) if kind == "col" else (R, C // 2), g.dtype))

    def body(*refs):
        ins, outs = refs[:n], refs[n:2 * n]
        send, recv = refs[2 * n:]
        x, y, c = _coords()
        sib = (x, y, 1 - c)
        cps = [_remote(_half(ins[p], BIG[p][1], 1 - c), outs[p], send, recv, p, sib) for p in range(n)]
        for cp in cps:
            cp.start()
        for cp in cps:
            cp.wait()

    return pl.pallas_call(
        body, name="sibling_exchange", out_shape=shapes, in_specs=[ANY] * n, out_specs=[ANY] * n,
        scratch_shapes=[pltpu.SemaphoreType.DMA((n,)), pltpu.SemaphoreType.DMA((n,))],
    )(*grads)


def _presum(g, got, kind, c_arr, name):
    R, C = got.shape
    tr, tc = _tile(R, 512, 16), _tile(C, 2048)
    nr, nc = R // tr, C // tc

    def body(c_ref, a_ref, b_ref, o_ref):
        o_ref[...] = (a_ref[...].astype(F32) + b_ref[...].astype(F32)).astype(o_ref.dtype)

    if kind == "col":
        mine = pl.BlockSpec((tr, tc), lambda i, j, c_ref: (c_ref[0] * nr + i, j))
    else:
        mine = pl.BlockSpec((tr, tc), lambda i, j, c_ref: (i, c_ref[0] * nc + j))
    blk = pl.BlockSpec((tr, tc), lambda i, j, c_ref: (i, j))
    spec = pltpu.PrefetchScalarGridSpec(num_scalar_prefetch=1, grid=(nr, nc), in_specs=[mine, blk], out_specs=blk)
    return _call(body, name=name, grid_spec=spec, out_shape=jax.ShapeDtypeStruct((R, C), WIRE_DTYPE))(c_arr, g, got)


def _chip_exchange(pres):
    n = len(BIG)
    shapes = []
    for (name, kind), g in zip(BIG, pres):
        R, C = g.shape
        shapes.append(jax.ShapeDtypeStruct((4, R, C // 4) if kind == "col" else (4, R // 4, C), g.dtype))

    def body(*refs):
        ins, outs = refs[:n], refs[n:2 * n]
        send, recv, loc = refs[2 * n:]
        x, y, c = _coords()
        me = 2 * x + y
        chips = _other_chips(x, y)
        local = [pltpu.make_async_copy(_shard(ins[p], BIG[p][1], me), outs[p].at[3], loc.at[p]) for p in range(n)]
        for cp in local:
            cp.start()
        cps = []
        for k, chip in enumerate(chips):
            s = 2 * chip[0] + chip[1]
            for p in range(n):
                cps.append(_remote(_shard(ins[p], BIG[p][1], s), outs[p].at[k], send, recv, 3 * p + k, (*chip, c)))
        for cp in cps:
            cp.start()
        for cp in cps:
            cp.wait()
        for cp in local:
            cp.wait()

    return pl.pallas_call(
        body, name="chip_exchange", out_shape=shapes, in_specs=[ANY] * n, out_specs=[ANY] * n,
        scratch_shapes=[pltpu.SemaphoreType.DMA((3 * n,)), pltpu.SemaphoreType.DMA((3 * n,)),
                        pltpu.SemaphoreType.DMA((n,))],
    )(*pres)


def _sum_slots(slots, name):
    K, R, C = slots.shape
    tr, tc = _tile(R, 512, 16), _tile(C, 2176)

    def body(s_ref, o_ref):
        acc = s_ref[0].astype(F32)
        for k in range(1, K):
            acc = acc + s_ref[k].astype(F32)
        o_ref[...] = acc

    return _call(body, name=name, grid=(R // tr, C // tc),
                 in_specs=[pl.BlockSpec((K, tr, tc), lambda i, j: (0, i, j))],
                 out_specs=pl.BlockSpec((tr, tc), lambda i, j: (i, j)),
                 out_shape=jax.ShapeDtypeStruct((R, C), F32))(slots)


def _share_reduced(reds):
    n = len(BIG)
    shapes = []
    for (name, kind), r in zip(BIG, reds):
        R, C = r.shape
        shapes.append(jax.ShapeDtypeStruct((2 * R, C) if kind == "col" else (R, 2 * C), r.dtype))

    def body(*refs):
        ins, outs = refs[:n], refs[n:2 * n]
        send, recv, loc = refs[2 * n:]
        x, y, c = _coords()
        sib = (x, y, 1 - c)
        local = [pltpu.make_async_copy(ins[p], _half(outs[p], BIG[p][1], c), loc.at[p]) for p in range(n)]
        cps = [_remote(ins[p], _half(outs[p], BIG[p][1], c), send, recv, p, sib) for p in range(n)]
        for cp in local + cps:
            cp.start()
        for cp in cps:
            cp.wait_send()
        for p in range(n):
            got = _half(outs[p], BIG[p][1], 1 - c)
            _remote(got, got, send, recv, p, sib).wait_recv()
        for cp in local:
            cp.wait()

    return pl.pallas_call(
        body, name="share_reduced", out_shape=shapes, in_specs=[ANY] * n, out_specs=[ANY] * n,
        scratch_shapes=[pltpu.SemaphoreType.DMA((n,)), pltpu.SemaphoreType.DMA((n,)), pltpu.SemaphoreType.DMA((n,))],
    )(*reds)


def _gather_small(pack):
    _, N = pack.shape

    def body(in_ref, out_ref, send, recv, loc):
        x, y, c = _coords()
        me = 4 * x + 2 * y + c
        own = pltpu.make_async_copy(in_ref, out_ref.at[me], loc)
        own.start()
        cps = []
        for k in range(1, 8):
            dev = (x ^ (k >> 2), y ^ ((k >> 1) & 1), c ^ (k & 1))
            cps.append(_remote(in_ref, out_ref.at[me], send, recv, k - 1, dev))
        for cp in cps:
            cp.start()
        for k in range(1, 8):
            src = 4 * (x ^ (k >> 2)) + 2 * (y ^ ((k >> 1) & 1)) + (c ^ (k & 1))
            _remote(in_ref, out_ref.at[src], send, recv, k - 1, (x, y, c)).wait_recv()
        for cp in cps:
            cp.wait_send()
        own.wait()

    return pl.pallas_call(
        body, name="gather_small", out_shape=jax.ShapeDtypeStruct((8, 1, N), pack.dtype),
        in_specs=[ANY], out_specs=ANY,
        scratch_shapes=[pltpu.SemaphoreType.DMA((7,)), pltpu.SemaphoreType.DMA((7,)), pltpu.SemaphoreType.DMA(())],
    )(pack)


def _sum_small(slots):
    K, _, N = slots.shape

    def body(s_ref, o_ref):
        acc = s_ref[0]
        for k in range(1, K):
            acc = acc + s_ref[k]
        o_ref[...] = acc

    return _call(body, name="sum_small", in_specs=[pl.BlockSpec(memory_space=pltpu.VMEM)],
                 out_specs=pl.BlockSpec(memory_space=pltpu.VMEM), out_shape=jax.ShapeDtypeStruct((1, N), F32))(slots)


def _adamw(w, g, m, v, name):
    R, C = w.shape
    tr, tc = _tile(R, 256, 8), _tile(C, 2176)

    def body(w_ref, g_ref, m_ref, v_ref, d_ref, nm_ref, nv_ref):
        gv = g_ref[...]
        nm = ADAM_B1 * m_ref[...] + (1.0 - ADAM_B1) * gv
        nv = ADAM_B2 * v_ref[...] + (1.0 - ADAM_B2) * gv * gv
        m_hat = nm / (1.0 - ADAM_B1 ** ADAM_STEP)
        v_hat = nv / (1.0 - ADAM_B2 ** ADAM_STEP)
        d_ref[...] = -ADAM_LR * (m_hat / (jnp.sqrt(v_hat) + ADAM_EPS) + ADAM_WD * w_ref[...])
        nm_ref[...] = nm
        nv_ref[...] = nv

    spec = pl.BlockSpec((tr, tc), lambda i, j: (i, j))
    shp = jax.ShapeDtypeStruct((R, C), F32)
    return _call(body, name=name, grid=(R // tr, C // tc), in_specs=[spec] * 4, out_specs=[spec] * 3,
                 out_shape=[shp] * 3)(w, g, m, v)


SMALL = ("norm_g", "mem_norm_g", "attn_q_norm", "attn_k_norm", "conv_w", "mem_q_norm", "mem_k_norm")
WEIGHTS = ("norm_g", "mem_norm_g", "w_in", "attn_q_norm", "attn_k_norm", "conv_w", "mem_w_kv", "mem_q_norm",
           "mem_k_norm", "w_br_attn", "w_br_conv", "w_br_mem", "w_out")


def kernel(x, mem, norm_g, mem_norm_g, w_in, attn_q_norm, attn_k_norm, conv_w, mem_w_kv, mem_q_norm, mem_k_norm, w_br_attn, w_br_conv, w_br_mem, w_out, loss_target, m_norm_g, m_mem_norm_g, m_w_in, m_attn_q_norm, m_attn_k_norm, m_conv_w, m_mem_w_kv, m_mem_q_norm, m_mem_k_norm, m_w_br_attn, m_w_br_conv, m_w_br_mem, m_w_out, v_norm_g, v_mem_norm_g, v_w_in, v_attn_q_norm, v_attn_k_norm, v_conv_w, v_mem_w_kv, v_mem_q_norm, v_mem_k_norm, v_w_br_attn, v_w_br_conv, v_w_br_mem, v_w_out):
    w = dict(norm_g=norm_g, mem_norm_g=mem_norm_g, w_in=w_in, attn_q_norm=attn_q_norm, attn_k_norm=attn_k_norm,
             conv_w=conv_w, mem_w_kv=mem_w_kv, mem_q_norm=mem_q_norm, mem_k_norm=mem_k_norm, w_br_attn=w_br_attn,
             w_br_conv=w_br_conv, w_br_mem=w_br_mem, w_out=w_out)
    m = dict(norm_g=m_norm_g, mem_norm_g=m_mem_norm_g, w_in=m_w_in, attn_q_norm=m_attn_q_norm,
             attn_k_norm=m_attn_k_norm, conv_w=m_conv_w, mem_w_kv=m_mem_w_kv, mem_q_norm=m_mem_q_norm,
             mem_k_norm=m_mem_k_norm, w_br_attn=m_w_br_attn, w_br_conv=m_w_br_conv, w_br_mem=m_w_br_mem, w_out=m_w_out)
    v = dict(norm_g=v_norm_g, mem_norm_g=v_mem_norm_g, w_in=v_w_in, attn_q_norm=v_attn_q_norm,
             attn_k_norm=v_attn_k_norm, conv_w=v_conv_w, mem_w_kv=v_mem_w_kv, mem_q_norm=v_mem_q_norm,
             mem_k_norm=v_mem_k_norm, w_br_attn=v_w_br_attn, w_br_conv=v_w_br_conv, w_br_mem=v_w_br_mem, w_out=v_w_out)
    D = x.shape[-1]
    chip = 2 * lax.axis_index("x") + lax.axis_index("y")
    c_arr = lax.axis_index("c").astype(jnp.int32).reshape(1)

    shards = [_cast(w[name], "cast_" + name) for name, _ in BIG]
    gathered = _gather_weights(shards, conv_w)
    W = {name: gathered[p] for p, (name, _) in enumerate(BIG)}
    conv_full = gathered[len(BIG)]

    grad_x, G, small = _local_step(x, mem, loss_target, norm_g, mem_norm_g, attn_q_norm, attn_k_norm, conv_full,
                                   mem_q_norm, mem_k_norm, W)

    parts = [G[name] for name, _ in BIG]
    got = _sibling_exchange(parts)
    pres = [_presum(parts[p], got[p], kind, c_arr, "presum_" + name) for p, (name, kind) in enumerate(BIG)]
    slots = _chip_exchange(pres)
    reds = [_sum_slots(slots[p], "reduce_" + name) for p, (name, _) in enumerate(BIG)]
    grads = dict(zip([name for name, _ in BIG], _share_reduced(reds)))

    tot = _sum_small(_gather_small(small))[0]
    loss = tot[0]
    off = 128
    for name, size in (("norm_g", D), ("mem_norm_g", D), ("attn_q_norm", NGROUP * HEAD), ("attn_k_norm", NGROUP * HEAD),
                       ("conv_w", 3 * CONVW), ("mem_q_norm", MEM_HD), ("mem_k_norm", MEM_HD)):
        grads[name] = tot[off:off + size]
        off += size
    cw = conv_w.shape[1]
    grads["conv_w"] = lax.dynamic_slice(grads["conv_w"].reshape(3, CONVW), (0, chip * cw), (3, cw))
    for name in SMALL:
        grads[name] = grads[name].reshape(w[name].shape)

    delta, new_m, new_v = {}, {}, {}
    for name, _ in BIG:
        delta[name], new_m[name], new_v[name] = _adamw(w[name], grads[name], m[name], v[name], "adamw_" + name)

    def packed(t):
        return jnp.concatenate([t[name].reshape(1, -1) for name in SMALL], axis=1)

    ds, ms, vs = _adamw(packed(w), packed(grads), packed(m), packed(v), "adamw_small")
    off = 0
    for name in SMALL:
        size = w[name].size
        delta[name] = ds[0, off:off + size].reshape(w[name].shape)
        new_m[name] = ms[0, off:off + size].reshape(w[name].shape)
        new_v[name] = vs[0, off:off + size].reshape(w[name].shape)
        off += size

    return (loss, grad_x, *[grads[n] for n in WEIGHTS], *[delta[n] for n in WEIGHTS],
            *[new_m[n] for n in WEIGHTS], *[new_v[n] for n in WEIGHTS])
```

```python
import functools

import jax
import jax.numpy as jnp
from jax import lax
from jax.experimental import pallas as pl
from jax.experimental.pallas import tpu as pltpu

F32 = jnp.float32
MXU_DTYPE = jnp.bfloat16
WIRE_DTYPE = jnp.bfloat16
EPS = 1e-6
NEG = -1e30

HEAD = 128
HPG = 4
GW = HPG * HEAD
DILATIONS = (1, 4, 16)
NGROUP = len(DILATIONS)
BLK = 128
QKV = NGROUP * GW
CONVW = 1024
MEM_HEADS = 4
MEM_HD = 256
MEMW = MEM_HEADS * MEM_HD
Q0, K0, V0 = 0, QKV, 2 * QKV
ZA = 3 * QKV
CB, CC, CV, ZC = ZA + GW, ZA + GW + CONVW, ZA + GW + 2 * CONVW, ZA + GW + 3 * CONVW
MQ = ZC + CONVW
ZM = MQ + MEMW
G0 = ZM + MEMW

ADAM_LR, ADAM_B1, ADAM_B2, ADAM_EPS, ADAM_WD, ADAM_STEP = 0.001, 0.9, 0.999, 1e-08, 0.01, 10

VMEM_LIMIT = 56 * 1024 * 1024
MESH = pl.DeviceIdType.MESH
ANY = pl.BlockSpec(memory_space=pl.ANY)


def _tile(n, pref, mult=128):
    t = min(pref, n)
    while t > mult and (n % t or t % mult):
        t -= mult
    assert n % t == 0, (n, pref)
    return t


def _call(body, *, name, out_shape, grid=(), in_specs=None, out_specs=None, scratch_shapes=(),
          aliases=None, grid_spec=None):
    kw = {}
    if grid_spec is not None:
        kw["grid_spec"] = grid_spec
        ngrid = len(grid_spec.grid)
    else:
        kw.update(grid=grid, in_specs=in_specs, out_specs=out_specs, scratch_shapes=list(scratch_shapes))
        ngrid = len(grid)
    params = pltpu.CompilerParams(dimension_semantics=("arbitrary",) * ngrid, vmem_limit_bytes=VMEM_LIMIT)
    return pl.pallas_call(body, name=name, out_shape=out_shape, compiler_params=params,
                          input_output_aliases=aliases or {}, **kw)


_DIMS = {"nn": (((1,), (0,)), ((), ())), "nt": (((1,), (1,)), ((), ())), "tn": (((0,), (0,)), ((), ()))}


def _mxu(a, b, mode):
    return lax.dot_general(a.astype(MXU_DTYPE), b.astype(MXU_DTYPE), _DIMS[mode], preferred_element_type=F32)


@functools.partial(jax.custom_vjp, nondiff_argnums=(2,))
def _dot(a, b, mode):
    return _mxu(a, b, mode)


def _dot_fwd(a, b, mode):
    return _mxu(a, b, mode), (a, b)


def _dot_bwd(mode, res, g):
    a, b = res
    if mode == "nn":
        return _mxu(g, b, "nt"), _mxu(a, g, "tn")
    if mode == "nt":
        return _mxu(g, b, "nn"), _mxu(g, a, "tn")
    return _mxu(b, g, "nt"), _mxu(a, g, "nn")


_dot.defvjp(_dot_fwd, _dot_bwd)


def _sig(z):
    return 1.0 / (1.0 + jnp.exp(-z))


def _silu(z):
    return z * _sig(z)


def _rms_rows(t, g):
    return t * lax.rsqrt(jnp.mean(t * t, axis=-1, keepdims=True) + EPS) * g


def _attn_block(q, k2, v2, gq, gk, first):
    qn = _rms_rows(q, gq)
    kn = _rms_rows(k2, gk)
    s = _dot(qn, kn, "nt") * (HEAD ** -0.5)
    a = lax.broadcasted_iota(jnp.int32, (BLK, 2 * BLK), 0)
    b = lax.broadcasted_iota(jnp.int32, (BLK, 2 * BLK), 1)
    lo = jnp.where(first, BLK, 0)
    mask = (b >= a) & (b <= a + BLK) & (b >= lo)
    s = jnp.where(mask, s, NEG)
    m = lax.stop_gradient(jnp.max(s, axis=-1, keepdims=True))
    p = jnp.exp(s - m)
    den = jnp.sum(p, axis=-1, keepdims=True)
    o = _dot(p, v2, "nn") / den
    return o, m + jnp.log(den)


def _combine(o1, o2, o3, l1, l2, l3, z):
    m = lax.stop_gradient(jnp.maximum(jnp.maximum(l1, l2), l3))
    e1, e2, e3 = jnp.exp(l1 - m), jnp.exp(l2 - m), jnp.exp(l3 - m)
    return (e1 * o1 + e2 * o2 + e3 * o3) / (e1 + e2 + e3) * _silu(z)


def _mem_block(q, z, kv, gq, gk):
    outs = []
    for h in range(MEM_HEADS):
        sl = slice(h * MEM_HD, (h + 1) * MEM_HD)
        qn = _rms_rows(q[:, sl], gq)
        kn = _rms_rows(kv[:, sl], gk)
        s = _dot(qn, kn, "nt") * (MEM_HD ** -0.5)
        m = lax.stop_gradient(jnp.max(s, axis=-1, keepdims=True))
        p = jnp.exp(s - m)
        den = jnp.sum(p, axis=-1, keepdims=True)
        outs.append(_dot(p, kv[:, MEMW + h * MEM_HD:MEMW + (h + 1) * MEM_HD], "nn") / den)
    return jnp.concatenate(outs, axis=-1) * _silu(z)


def _cast(w, name):
    R, C = w.shape
    tr, tc = _tile(R, 512, 8), _tile(C, 2176)

    def body(w_ref, o_ref):
        o_ref[...] = w_ref[...].astype(o_ref.dtype)

    spec = pl.BlockSpec((tr, tc), lambda i, j: (i, j))
    return _call(body, name=name, grid=(R // tr, C // tc), in_specs=[spec], out_specs=spec,
                 out_shape=jax.ShapeDtypeStruct((R, C), WIRE_DTYPE))(w)


def _place_shard(w, kind, pos, dtype, name):
    R, C = w.shape
    tr, tc = _tile(R, 512, 8), _tile(C, 2176)
    nr, nc = R // tr, C // tc

    def body(pos_ref, w_ref, o_ref):
        o_ref[...] = w_ref[...].astype(o_ref.dtype)

    if kind == "col":
        full, out = (R, 4 * C), pl.BlockSpec((tr, tc), lambda i, j, pos_ref: (i, pos_ref[0] * nc + j))
    else:
        full, out = (4 * R, C), pl.BlockSpec((tr, tc), lambda i, j, pos_ref: (pos_ref[0] * nr + i, j))
    spec = pltpu.PrefetchScalarGridSpec(num_scalar_prefetch=1, grid=(nr, nc),
                                        in_specs=[pl.BlockSpec((tr, tc), lambda i, j, pos_ref: (i, j))], out_specs=out)
    return _call(body, name=name, grid_spec=spec, out_shape=jax.ShapeDtypeStruct(full, dtype))(pos, w)


def _matmul(a, b, mode, out_dtype, *, name, tm=512, tn=512, tk=512):
    if mode == "nn":
        (M, K), (_, N) = a.shape, b.shape
    elif mode == "nt":
        (M, K), (N, _) = a.shape, b.shape
    else:
        (K, M), (_, N) = a.shape, b.shape
    tm, tn, tk = _tile(M, tm), _tile(N, tn), _tile(K, tk)
    nk = K // tk

    def body(a_ref, b_ref, o_ref, *acc):
        part = lax.dot_general(a_ref[...], b_ref[...], _DIMS[mode], preferred_element_type=F32)
        if nk == 1:
            o_ref[...] = part.astype(o_ref.dtype)
            return
        acc_ref, = acc
        k = pl.program_id(2)

        @pl.when(k == 0)
        def _():
            acc_ref[...] = part

        @pl.when(k > 0)
        def _():
            acc_ref[...] += part

        @pl.when(k == nk - 1)
        def _():
            o_ref[...] = acc_ref[...].astype(o_ref.dtype)

    a_spec = pl.BlockSpec((tk, tm), lambda i, j, k: (k, i)) if mode == "tn" else pl.BlockSpec((tm, tk), lambda i, j, k: (i, k))
    b_spec = pl.BlockSpec((tn, tk), lambda i, j, k: (j, k)) if mode == "nt" else pl.BlockSpec((tk, tn), lambda i, j, k: (k, j))
    return _call(body, name=name, grid=(M // tm, N // tn, nk), in_specs=[a_spec, b_spec],
                 out_specs=pl.BlockSpec((tm, tn), lambda i, j, k: (i, j)),
                 out_shape=jax.ShapeDtypeStruct((M, N), out_dtype),
                 scratch_shapes=[] if nk == 1 else [pltpu.VMEM((tm, tn), F32)])(a, b)


def _rms_fwd(x, g, name):
    R, D = x.shape
    tr = _tile(R, 512)

    def body(x_ref, g_ref, o_ref, t_ref):
        y = _rms_rows(x_ref[...], g_ref[...])
        o_ref[...] = y.astype(o_ref.dtype)
        t_ref[...] = y.T.astype(t_ref.dtype)

    row = pl.BlockSpec((tr, D), lambda i: (i, 0))
    return _call(body, name=name, grid=(R // tr,), in_specs=[row, pl.BlockSpec((1, D), lambda i: (0, 0))],
                 out_specs=[row, pl.BlockSpec((D, tr), lambda i: (0, i))],
                 out_shape=[jax.ShapeDtypeStruct((R, D), MXU_DTYPE), jax.ShapeDtypeStruct((D, R), MXU_DTYPE)])(x, g)


def _rms_bwd(x, dh, g, dy, name):
    R, D = x.shape
    tr = _tile(R, 256)
    with_dx = dy is not None

    def body(*refs):
        if with_dx:
            x_ref, dh_ref, g_ref, dy_ref, dx_ref, dg_ref = refs
        else:
            x_ref, dh_ref, g_ref, dg_ref = refs
        xv, dhv = x_ref[...], dh_ref[...]
        r = lax.rsqrt(jnp.mean(xv * xv, axis=-1, keepdims=True) + EPS)
        xh = xv * r

        @pl.when(pl.program_id(0) == 0)
        def _():
            dg_ref[...] = jnp.zeros_like(dg_ref)

        dg_ref[...] += jnp.sum(dhv * xh, axis=0, keepdims=True)
        if with_dx:
            dxh = dhv * g_ref[...]
            dx_ref[...] = dy_ref[...] + r * (dxh - xh * jnp.mean(dxh * xh, axis=-1, keepdims=True))

    row = pl.BlockSpec((tr, D), lambda i: (i, 0))
    vec = pl.BlockSpec((1, D), lambda i: (0, 0))
    dg_shape = jax.ShapeDtypeStruct((1, D), F32)
    if with_dx:
        return _call(body, name=name, grid=(R // tr,), in_specs=[row, row, vec, row], out_specs=[row, vec],
                     out_shape=[jax.ShapeDtypeStruct((R, D), F32), dg_shape])(x, dh, g, dy)
    return None, _call(body, name=name, grid=(R // tr,), in_specs=[row, row, vec], out_specs=vec,
                       out_shape=dg_shape)(x, dh, g)


def _attn_geom(g, d):
    hc = HPG if d == 1 else 1
    cw = hc * HEAD
    cq, ck, cv = (Q0 + g * GW) // cw, (K0 + g * GW) // cw, (V0 + g * GW) // cw
    return (1, BLK * d, cw), hc, HPG // hc, cq, ck, cv


def _rows(ref, r, d, sl):
    if d == 1:
        return ref[0, :, sl]
    return ref.at[0][pl.ds(r, BLK, stride=d), sl]


def _set_rows(ref, r, d, sl, val):
    if d == 1:
        ref[0, :, sl] = val
    else:
        ref.at[0][pl.ds(r, BLK, stride=d), sl] = val


def _attn_fwd(proj3, gq, gk, g, d):
    Bl, S, _ = proj3.shape
    blk, hc, ncb, cq, ck, cv = _attn_geom(g, d)
    nb = S // blk[1]

    def body(q_ref, kp_ref, kc_ref, vp_ref, vc_ref, gq_ref, gk_ref, o_ref, lse_ref):
        first = pl.program_id(2) == 0
        for r in range(d):
            for h in range(hc):
                sl = slice(h * HEAD, (h + 1) * HEAD)
                k2 = jnp.concatenate([_rows(kp_ref, r, d, sl), _rows(kc_ref, r, d, sl)], axis=0)
                v2 = jnp.concatenate([_rows(vp_ref, r, d, sl), _rows(vc_ref, r, d, sl)], axis=0)
                o, lse = _attn_block(_rows(q_ref, r, d, sl), k2, v2, gq_ref[...], gk_ref[...], first)
                _set_rows(o_ref, r, d, sl, o)
                _set_rows(lse_ref, r, d, sl, jnp.broadcast_to(lse, (BLK, HEAD)))

    def cur(c0):
        return pl.BlockSpec(blk, lambda b, j, i: (b, i, c0 + j))

    def prev(c0):
        return pl.BlockSpec(blk, lambda b, j, i: (b, jnp.maximum(i - 1, 0), c0 + j))

    vec = pl.BlockSpec((1, HEAD), lambda b, j, i: (0, 0))
    out = pl.BlockSpec(blk, lambda b, j, i: (b, i, j))
    shp = jax.ShapeDtypeStruct((Bl, S, GW), F32)
    return _call(body, name=f"attn_fwd_g{g}", grid=(Bl, ncb, nb),
                 in_specs=[cur(cq), prev(ck), cur(ck), prev(cv), cur(cv), vec, vec],
                 out_specs=[out, out], out_shape=[shp, shp])(proj3, proj3, proj3, proj3, proj3, gq, gk)


def _attn_bwd(proj3, gq, gk, do3, dl3, g, d):
    Bl, S, _ = proj3.shape
    blk, hc, ncb, cq, ck, cv = _attn_geom(g, d)
    nb = S // blk[1]

    def body(q_ref, kp_ref, kc_ref, vp_ref, vc_ref, gq_ref, gk_ref, do_ref, dl_ref,
             dq_ref, dk_ref, dv_ref, dgq_ref, dgk_ref, ck_ref, cv_ref):
        i = pl.program_id(2)
        first = i == 0

        @pl.when((pl.program_id(0) == 0) & (pl.program_id(1) == 0) & first)
        def _():
            dgq_ref[...] = jnp.zeros_like(dgq_ref)
            dgk_ref[...] = jnp.zeros_like(dgk_ref)

        @pl.when(i < nb)
        def _():
            for r in range(d):
                rs = slice(r * BLK, (r + 1) * BLK)
                for h in range(hc):
                    sl = slice(h * HEAD, (h + 1) * HEAD)
                    k2 = jnp.concatenate([_rows(kp_ref, r, d, sl), _rows(kc_ref, r, d, sl)], axis=0)
                    v2 = jnp.concatenate([_rows(vp_ref, r, d, sl), _rows(vc_ref, r, d, sl)], axis=0)
                    _, vjp = jax.vjp(lambda q, k, v, a, b: _attn_block(q, k, v, a, b, first),
                                     _rows(q_ref, r, d, sl), k2, v2, gq_ref[...], gk_ref[...])
                    dq, dk2, dv2, dgq, dgk = vjp((_rows(do_ref, r, d, sl), _rows(dl_ref, r, d, sl)[:, :1]))
                    _set_rows(dq_ref, r, d, sl, dq)
                    dgq_ref[...] += dgq
                    dgk_ref[...] += dgk

                    @pl.when(i > 0)
                    def _():
                        _set_rows(dk_ref, r, d, sl, ck_ref[rs, sl] + dk2[:BLK])
                        _set_rows(dv_ref, r, d, sl, cv_ref[rs, sl] + dv2[:BLK])

                    ck_ref[rs, sl] = dk2[BLK:]
                    cv_ref[rs, sl] = dv2[BLK:]

        @pl.when(i == nb)
        def _():
            for r in range(d):
                rs = slice(r * BLK, (r + 1) * BLK)
                _set_rows(dk_ref, r, d, slice(None), ck_ref[rs, :])
                _set_rows(dv_ref, r, d, slice(None), cv_ref[rs, :])

    def cur(c0):
        return pl.BlockSpec(blk, lambda b, j, i: (b, jnp.minimum(i, nb - 1), c0 + j))

    def prev(c0):
        return pl.BlockSpec(blk, lambda b, j, i: (b, jnp.clip(i - 1, 0, nb - 1), c0 + j))

    vec = pl.BlockSpec((1, HEAD), lambda b, j, i: (0, 0))
    at_q = pl.BlockSpec(blk, lambda b, j, i: (b, jnp.minimum(i, nb - 1), j))
    at_k = pl.BlockSpec(blk, lambda b, j, i: (b, jnp.maximum(i - 1, 0), j))
    shp = jax.ShapeDtypeStruct((Bl, S, GW), F32)
    gshp = jax.ShapeDtypeStruct((1, HEAD), F32)
    return _call(body, name=f"attn_bwd_g{g}", grid=(Bl, ncb, nb + 1),
                 in_specs=[cur(cq), prev(ck), cur(ck), prev(cv), cur(cv), vec, vec, at_q, at_q],
                 out_specs=[at_q, at_k, at_k, vec, vec], out_shape=[shp, shp, shp, gshp, gshp],
                 scratch_shapes=[pltpu.VMEM(blk[1:], F32), pltpu.VMEM(blk[1:], F32)],
                 )(proj3, proj3, proj3, proj3, proj3, gq, gk, do3, dl3)


def _combine_fwd(os, ls, proj2):
    T = proj2.shape[0]
    tr = _tile(T, 512)

    def body(o1, o2, o3, l1, l2, l3, z, a_ref):
        a_ref[...] = _combine(o1[...], o2[...], o3[...], l1[...], l2[...], l3[...], z[...]).astype(a_ref.dtype)

    row = pl.BlockSpec((tr, GW), lambda i: (i, 0))
    return _call(body, name="combine_fwd", grid=(T // tr,),
                 in_specs=[row] * 6 + [pl.BlockSpec((tr, GW), lambda i: (i, ZA // GW))], out_specs=row,
                 out_shape=jax.ShapeDtypeStruct((T, GW), MXU_DTYPE))(*os, *ls, proj2)


def _combine_bwd(os, ls, proj2, da):
    T = proj2.shape[0]
    tr = _tile(T, 256)

    def body(o1, o2, o3, l1, l2, l3, z, da_ref, d1, d2, d3, e1, e2, e3, dz_ref):
        _, vjp = jax.vjp(_combine, o1[...], o2[...], o3[...], l1[...], l2[...], l3[...], z[...])
        go1, go2, go3, gl1, gl2, gl3, gz = vjp(da_ref[...])
        d1[...], d2[...], d3[...] = go1, go2, go3
        dz_ref[...] = gz.astype(dz_ref.dtype)
        for ref, gl in ((e1, gl1), (e2, gl2), (e3, gl3)):
            for h in range(HPG):
                sl = slice(h * HEAD, (h + 1) * HEAD)
                ref[:, sl] = jnp.broadcast_to(jnp.sum(gl[:, sl], axis=-1, keepdims=True), (tr, HEAD))

    row = pl.BlockSpec((tr, GW), lambda i: (i, 0))
    f = jax.ShapeDtypeStruct((T, GW), F32)
    outs = _call(body, name="combine_bwd", grid=(T // tr,),
                 in_specs=[row] * 6 + [pl.BlockSpec((tr, GW), lambda i: (i, ZA // GW)), row],
                 out_specs=[row] * 7, out_shape=[f] * 6 + [jax.ShapeDtypeStruct((T, GW), MXU_DTYPE)],
                 )(*os, *ls, proj2, da)
    return outs[:3], outs[3:6], outs[6]


def _shift_down(u, j, t):
    return jnp.where(t >= j, pltpu.roll(u, j, 0), 0.0)


def _shift_up(u, j, t):
    n = u.shape[0]
    return jnp.where(t < n - j, pltpu.roll(u, n - j, 0), 0.0)


def _conv_specs(Bl, S, cw):
    def sec(c0):
        return pl.BlockSpec((1, S, cw), lambda j, b: (b, 0, c0 // cw + j))
    return [sec(CB), sec(CC), sec(CV), sec(ZC)], pl.BlockSpec((3, cw), lambda j, b: (0, j))


def _conv_fwd(proj3, conv_w):
    Bl, S, _ = proj3.shape
    cw = 256
    secs, wspec = _conv_specs(Bl, S, cw)

    def body(b_ref, c_ref, v_ref, z_ref, w_ref, o_ref):
        t = lax.broadcasted_iota(jnp.int32, (S, cw), 0)
        u = c_ref[0] * v_ref[0]
        y = w_ref[0:1, :] * u + w_ref[1:2, :] * _shift_down(u, 1, t) + w_ref[2:3, :] * _shift_down(u, 2, t)
        o_ref[0] = (b_ref[0] * y * _silu(z_ref[0])).astype(o_ref.dtype)

    return _call(body, name="conv_fwd", grid=(CONVW // cw, Bl), in_specs=secs + [wspec],
                 out_specs=pl.BlockSpec((1, S, cw), lambda j, b: (b, 0, j)),
                 out_shape=jax.ShapeDtypeStruct((Bl, S, CONVW), MXU_DTYPE))(proj3, proj3, proj3, proj3, conv_w)


def _conv_bwd(proj3, conv_w, dcc3):
    Bl, S, _ = proj3.shape
    cw = 256
    secs, wspec = _conv_specs(Bl, S, cw)

    def body(b_ref, c_ref, v_ref, z_ref, w_ref, d_ref, db_ref, dc_ref, dv_ref, dz_ref, dw_ref):
        t = lax.broadcasted_iota(jnp.int32, (S, cw), 0)
        bv, cv, vv, zv, dv = b_ref[0], c_ref[0], v_ref[0], z_ref[0], d_ref[0]
        u = cv * vv
        u1, u2 = _shift_down(u, 1, t), _shift_down(u, 2, t)
        y = w_ref[0:1, :] * u + w_ref[1:2, :] * u1 + w_ref[2:3, :] * u2
        sg = _sig(zv)
        sz = zv * sg
        gy = dv * bv * sz
        db_ref[0] = (dv * y * sz).astype(db_ref.dtype)
        dz_ref[0] = (dv * bv * y * sg * (1.0 + zv * (1.0 - sg))).astype(dz_ref.dtype)
        du = w_ref[0:1, :] * gy + w_ref[1:2, :] * _shift_up(gy, 1, t) + w_ref[2:3, :] * _shift_up(gy, 2, t)
        dc_ref[0] = (du * vv).astype(dc_ref.dtype)
        dv_ref[0] = (du * cv).astype(dv_ref.dtype)

        @pl.when(pl.program_id(1) == 0)
        def _():
            dw_ref[...] = jnp.zeros_like(dw_ref)

        dw_ref[0:1, :] += jnp.sum(gy * u, axis=0, keepdims=True)
        dw_ref[1:2, :] += jnp.sum(gy * u1, axis=0, keepdims=True)
        dw_ref[2:3, :] += jnp.sum(gy * u2, axis=0, keepdims=True)

    blk = pl.BlockSpec((1, S, cw), lambda j, b: (b, 0, j))
    shp = jax.ShapeDtypeStruct((Bl, S, CONVW), MXU_DTYPE)
    return _call(body, name="conv_bwd", grid=(CONVW // cw, Bl), in_specs=secs + [wspec, blk],
                 out_specs=[blk] * 4 + [wspec], out_shape=[shp] * 4 + [jax.ShapeDtypeStruct((3, CONVW), F32)],
                 )(proj3, proj3, proj3, proj3, conv_w, dcc3)


def _mem_specs(S, tq):
    q = pl.BlockSpec((1, tq, MEMW), lambda b, j: (b, j, MQ // MEMW))
    z = pl.BlockSpec((1, tq, MEMW), lambda b, j: (b, j, ZM // MEMW))
    kv = pl.BlockSpec((1, MEM_HD, 2 * MEMW), lambda b, j: (b, 0, 0))
    vec = pl.BlockSpec((1, MEM_HD), lambda b, j: (0, 0))
    blk = pl.BlockSpec((1, tq, MEMW), lambda b, j: (b, j, 0))
    return q, z, kv, vec, blk


def _mem_fwd(proj3, mkv3, gq, gk):
    Bl, S, _ = proj3.shape
    tq = _tile(S, 512)
    q, z, kv, vec, blk = _mem_specs(S, tq)

    def body(q_ref, z_ref, kv_ref, gq_ref, gk_ref, o_ref):
        o_ref[0] = _mem_block(q_ref[0], z_ref[0], kv_ref[0], gq_ref[...], gk_ref[...]).astype(o_ref.dtype)

    return _call(body, name="mem_fwd", grid=(Bl, S // tq), in_specs=[q, z, kv, vec, vec], out_specs=blk,
                 out_shape=jax.ShapeDtypeStruct((Bl, S, MEMW), MXU_DTYPE))(proj3, proj3, mkv3, gq, gk)


def _mem_bwd(proj3, mkv3, gq, gk, dmo3):
    Bl, S, _ = proj3.shape
    tq = _tile(S, 256)
    q, z, kv, vec, blk = _mem_specs(S, tq)

    def body(q_ref, z_ref, kv_ref, gq_ref, gk_ref, d_ref, dq_ref, dz_ref, dkv_ref, dgq_ref, dgk_ref):
        _, vjp = jax.vjp(_mem_block, q_ref[0], z_ref[0], kv_ref[0], gq_ref[...], gk_ref[...])
        dq, dz, dkv, dgq, dgk = vjp(d_ref[0])
        dq_ref[0] = dq.astype(dq_ref.dtype)
        dz_ref[0] = dz.astype(dz_ref.dtype)
        j = pl.program_id(1)

        @pl.when(j == 0)
        def _():
            dkv_ref[0] = jnp.zeros_like(dkv)

        @pl.when((j == 0) & (pl.program_id(0) == 0))
        def _():
            dgq_ref[...] = jnp.zeros_like(dgq_ref)
            dgk_ref[...] = jnp.zeros_like(dgk_ref)

        dkv_ref[0] += dkv
        dgq_ref[...] += dgq
        dgk_ref[...] += dgk

    shp = jax.ShapeDtypeStruct((Bl, S, MEMW), MXU_DTYPE)
    gshp = jax.ShapeDtypeStruct((1, MEM_HD), F32)
    return _call(body, name="mem_bwd", grid=(Bl, S // tq), in_specs=[q, z, kv, vec, vec, blk],
                 out_specs=[blk, blk, kv, vec, vec],
                 out_shape=[shp, shp, jax.ShapeDtypeStruct(mkv3.shape, F32), gshp, gshp],
                 )(proj3, proj3, mkv3, gq, gk, dmo3)


def _merge_specs(T, D, tm, tn):
    def act(w):
        return pl.BlockSpec((tm, w), lambda i, n: (i, 0))

    def wsp(w):
        return pl.BlockSpec((w, tn), lambda i, n: (0, n))

    gates = [pl.BlockSpec((tm, tn), lambda i, n, k=k: (i, (G0 + k * D) // tn + n)) for k in range(3)]
    tile = pl.BlockSpec((tm, tn), lambda i, n: (i, n))
    return act, wsp, gates, tile


def _merge_fwd(a, cc, mo, wa, wc, wm, proj2):
    T, D = a.shape[0], wa.shape[1]
    tm, tn = _tile(T, 512), _tile(D, 512)
    act, wsp, gates, tile = _merge_specs(T, D, tm, tn)

    def body(a_ref, c_ref, m_ref, wa_ref, wc_ref, wm_ref, g0, g1, g2, mg_ref, mt_ref, pa_ref, pc_ref, pm_ref):
        pa = jnp.dot(a_ref[...], wa_ref[...], preferred_element_type=F32)
        pc = jnp.dot(c_ref[...], wc_ref[...], preferred_element_type=F32)
        pm = jnp.dot(m_ref[...], wm_ref[...], preferred_element_type=F32)
        mg = _sig(g0[...]) * pa + _sig(g1[...]) * pc + _sig(g2[...]) * pm
        mg_ref[...] = mg.astype(mg_ref.dtype)
        mt_ref[...] = mg.T.astype(mt_ref.dtype)
        pa_ref[...] = pa.astype(pa_ref.dtype)
        pc_ref[...] = pc.astype(pc_ref.dtype)
        pm_ref[...] = pm.astype(pm_ref.dtype)

    shp = jax.ShapeDtypeStruct((T, D), MXU_DTYPE)
    return _call(body, name="merge_fwd", grid=(T // tm, D // tn),
                 in_specs=[act(GW), act(CONVW), act(MEMW), wsp(GW), wsp(CONVW), wsp(MEMW)] + gates,
                 out_specs=[tile, pl.BlockSpec((tn, tm), lambda i, n: (n, i)), tile, tile, tile],
                 out_shape=[shp, jax.ShapeDtypeStruct((D, T), MXU_DTYPE), shp, shp, shp],
                 )(a, cc, mo, wa, wc, wm, proj2, proj2, proj2)


def _merge_bwd(dyb, w_out, proj2, pa, pc, pm):
    T, D = dyb.shape
    tm, tn = _tile(T, 512), _tile(D, 512)
    _, _, gates, tile = _merge_specs(T, D, tm, tn)

    def body(dy_ref, w_ref, g0, g1, g2, p0, p1, p2, dp0, dp1, dp2, dg0, dg1, dg2):
        dm = lax.dot_general(dy_ref[...], w_ref[...], _DIMS["nt"], preferred_element_type=F32)
        for g_ref, p_ref, dp_ref, dg_ref in ((g0, p0, dp0, dg0), (g1, p1, dp1, dg1), (g2, p2, dp2, dg2)):
            gt = _sig(g_ref[...])
            dp_ref[...] = (gt * dm).astype(dp_ref.dtype)
            dg_ref[...] = (dm * p_ref[...].astype(F32) * gt * (1.0 - gt)).astype(dg_ref.dtype)

    shp = jax.ShapeDtypeStruct((T, D), MXU_DTYPE)
    return _call(body, name="merge_bwd", grid=(T // tm, D // tn),
                 in_specs=[pl.BlockSpec((tm, D), lambda i, n: (i, 0)), pl.BlockSpec((tn, D), lambda i, n: (n, 0))]
                 + gates + [tile] * 3,
                 out_specs=[tile] * 6, out_shape=[shp] * 6)(dyb, w_out, proj2, proj2, proj2, pa, pc, pm)


def _out_loss(merged, w_out, x, tgt):
    T, D = x.shape
    tm = _tile(T, 256)

    def body(m_ref, w_ref, x_ref, t_ref, dy_ref, dyb_ref, loss_ref):
        err = x_ref[...] + jnp.dot(m_ref[...], w_ref[...], preferred_element_type=F32) - t_ref[...]
        dy = err * (1.0 / D)
        dy_ref[...] = dy
        dyb_ref[...] = dy.astype(dyb_ref.dtype)

        @pl.when(pl.program_id(0) == 0)
        def _():
            loss_ref[...] = jnp.zeros_like(loss_ref)

        loss_ref[...] += jnp.sum(err * err) * (0.5 / D)

    row = pl.BlockSpec((tm, D), lambda i: (i, 0))
    return _call(body, name="out_loss", grid=(T // tm,),
                 in_specs=[row, pl.BlockSpec((D, D), lambda i: (0, 0)), row, row],
                 out_specs=[row, row, pl.BlockSpec((1, 128), lambda i: (0, 0))],
                 out_shape=[jax.ShapeDtypeStruct((T, D), F32), jax.ShapeDtypeStruct((T, D), MXU_DTYPE),
                            jax.ShapeDtypeStruct((1, 128), F32)])(merged, w_out, x, tgt)


def _local_step(x, mem, tgt, norm_g, mem_norm_g, gq_all, gk_all, conv_w, mem_gq, mem_gk, W):
    Bl, S, D = x.shape
    T = Bl * S
    IN = W["w_in"].shape[1]
    x2, tgt2 = x.reshape(T, D), tgt.reshape(T, D)
    mem2 = mem.reshape(-1, D)
    ng, mng = norm_g.reshape(1, D), mem_norm_g.reshape(1, D)
    mgq, mgk = mem_gq.reshape(1, MEM_HD), mem_gk.reshape(1, MEM_HD)
    gqs = [gq_all[g:g + 1] for g in range(NGROUP)]
    gks = [gk_all[g:g + 1] for g in range(NGROUP)]

    hb, hbt = _rms_fwd(x2, ng, "rms_x")
    proj2 = _matmul(hb, W["w_in"], "nn", F32, name="proj", tm=2048, tn=512, tk=D)
    proj3 = proj2.reshape(Bl, S, IN)
    os, ls = [], []
    for g, d in enumerate(DILATIONS):
        o, l = _attn_fwd(proj3, gqs[g], gks[g], g, d)
        os.append(o.reshape(T, GW))
        ls.append(l.reshape(T, GW))
    a = _combine_fwd(os, ls, proj2)
    cc = _conv_fwd(proj3, conv_w).reshape(T, CONVW)
    mhb, _ = _rms_fwd(mem2, mng, "rms_mem")
    mkv = _matmul(mhb, W["mem_w_kv"], "nn", F32, name="mem_kv", tm=512, tn=1024, tk=D)
    mkv3 = mkv.reshape(Bl, -1, 2 * MEMW)
    mo = _mem_fwd(proj3, mkv3, mgq, mgk).reshape(T, MEMW)
    merged, mergedt, pa, pc, pm = _merge_fwd(a, cc, mo, W["w_br_attn"], W["w_br_conv"], W["w_br_mem"], proj2)
    dy, dyb, loss = _out_loss(merged, W["w_out"], x2, tgt2)

    G = {}
    G["w_out"] = _matmul(mergedt, dyb, "nn", WIRE_DTYPE, name="dw_out", tm=1024, tn=1024, tk=1024)
    dpa, dpc, dpm, dg0, dg1, dg2 = _merge_bwd(dyb, W["w_out"], proj2, pa, pc, pm)
    G["w_br_attn"] = _matmul(a, dpa, "tn", WIRE_DTYPE, name="dw_br_attn", tm=512, tn=1024, tk=512)
    G["w_br_conv"] = _matmul(cc, dpc, "tn", WIRE_DTYPE, name="dw_br_conv", tm=1024, tn=1024, tk=512)
    G["w_br_mem"] = _matmul(mo, dpm, "tn", WIRE_DTYPE, name="dw_br_mem", tm=1024, tn=1024, tk=512)
    da = _matmul(dpa, W["w_br_attn"], "nt", F32, name="d_attn", tm=1024, tn=512, tk=D)
    dcc = _matmul(dpc, W["w_br_conv"], "nt", F32, name="d_conv", tm=1024, tn=1024, tk=D)
    dmo = _matmul(dpm, W["w_br_mem"], "nt", F32, name="d_mem", tm=1024, tn=1024, tk=D)

    dos, dls, dza = _combine_bwd(os, ls, proj2, da)
    dqs, dks, dvs, dgq, dgk = [], [], [], [], []
    for g, d in enumerate(DILATIONS):
        dq, dk, dv, gq_g, gk_g = _attn_bwd(proj3, gqs[g], gks[g], dos[g].reshape(Bl, S, GW),
                                           dls[g].reshape(Bl, S, GW), g, d)
        dqs.append(dq.reshape(T, GW).astype(MXU_DTYPE))
        dks.append(dk.reshape(T, GW).astype(MXU_DTYPE))
        dvs.append(dv.reshape(T, GW).astype(MXU_DTYPE))
        dgq.append(gq_g)
        dgk.append(gk_g)
    dcb, dcc_, dcv, dzc, dconv_w = _conv_bwd(proj3, conv_w, dcc.reshape(Bl, S, CONVW))
    dmq, dzm, dmkv3, dmgq, dmgk = _mem_bwd(proj3, mkv3, mgq, mgk, dmo.reshape(Bl, S, MEMW))

    dmkv = _cast(dmkv3.reshape(-1, 2 * MEMW), "cast_dmkv")
    G["mem_w_kv"] = _matmul(mhb, dmkv, "tn", WIRE_DTYPE, name="dw_mem_kv", tm=1024, tn=1024, tk=512)
    dmh = _matmul(dmkv, W["mem_w_kv"], "nt", F32, name="d_memh", tm=512, tn=1024, tk=2 * MEMW)
    _, dmng = _rms_bwd(mem2, dmh, mng, None, "rms_mem_bwd")

    dproj = jnp.concatenate(dqs + dks + dvs + [dza] + [t.reshape(T, CONVW) for t in (dcb, dcc_, dcv, dzc)]
                            + [dmq.reshape(T, MEMW), dzm.reshape(T, MEMW), dg0, dg1, dg2], axis=1)
    G["w_in"] = _matmul(hbt, dproj, "nn", WIRE_DTYPE, name="dw_in", tm=1024, tn=1024, tk=1024)
    dh = _matmul(dproj, W["w_in"], "nt", F32, name="d_h", tm=1024, tn=1024, tk=1024)
    grad_x, dng = _rms_bwd(x2, dh, ng, dy, "rms_x_bwd")

    small = [loss, dng, dmng] + dgq + dgk + [dconv_w.reshape(1, 3 * CONVW), dmgq, dmgk]
    return grad_x.reshape(Bl, S, D), G, jnp.concatenate(small, axis=1)


BIG = (("w_in", "col"), ("mem_w_kv", "row"), ("w_br_attn", "col"), ("w_br_conv", "col"),
       ("w_br_mem", "col"), ("w_out", "row"))


def _coords():
    return lax.axis_index("x"), lax.axis_index("y"), lax.axis_index("c")


def _other_chips(x, y):
    return [(1 - x, y), (x, 1 - y), (1 - x, 1 - y)]


def _half(ref, kind, c):
    R, C = ref.shape
    if kind == "col":
        return ref.at[pl.ds(c * (R // 2), R // 2), :]
    return ref.at[:, pl.ds(c * (C // 2), C // 2)]


def _shard(ref, kind, s):
    R, C = ref.shape
    if kind == "col":
        return ref.at[:, pl.ds(s * (C // 4), C // 4)]
    return ref.at[pl.ds(s * (R // 4), R // 4), :]


def _piece(ref, kind, s, c):
    R, C = ref.shape
    if kind == "col":
        return ref.at[pl.ds(c * (R // 2), R // 2), pl.ds(s * (C // 4), C // 4)]
    return ref.at[pl.ds(s * (R // 4), R // 4), pl.ds(c * (C // 2), C // 2)]


def _remote(src, dst, sems_s, sems_r, k, dev):
    return pltpu.make_async_remote_copy(src_ref=src, dst_ref=dst, send_sem=sems_s.at[k], recv_sem=sems_r.at[k],
                                        device_id=dev, device_id_type=MESH)


def _gather_weights(fulls):
    n = len(BIG)

    def body(*refs):
        outs = refs[n + 1:2 * n + 2]
        send, recv = refs[2 * n + 2:]
        x, y, c = _coords()
        me = 2 * x + y
        chips = _other_chips(x, y)
        sib = (x, y, 1 - c)
        kinds = [k for _, k in BIG]
        first = []
        for p in range(n):
            for k, chip in enumerate(chips):
                mine = _piece(outs[p], kinds[p], me, c)
                first.append(_remote(mine, mine, send, recv, 6 * p + k, (*chip, c)))
        for k, chip in enumerate(chips):
            mine = _shard(outs[n], "col", me)
            first.append(_remote(mine, mine, send, recv, 6 * n + k, (*chip, c)))
        for cp in first:
            cp.start()
        passed = []
        for k, chip in enumerate(chips):
            s = 2 * chip[0] + chip[1]
            for p in range(n):
                got = _piece(outs[p], kinds[p], s, c)
                _remote(got, got, send, recv, 6 * p + k, sib).wait_recv()
                fwd = _remote(got, got, send, recv, 6 * p + 3 + k, sib)
                fwd.start()
                passed.append(fwd)
            got = _shard(outs[n], "col", s)
            _remote(got, got, send, recv, 6 * n + k, sib).wait_recv()
        for k, chip in enumerate(chips):
            s = 2 * chip[0] + chip[1]
            for p in range(n):
                got = _piece(outs[p], kinds[p], s, 1 - c)
                _remote(got, got, send, recv, 6 * p + 3 + k, sib).wait_recv()
        for cp in first + passed:
            cp.wait_send()

    nsem = 6 * n + 3
    return pl.pallas_call(
        body, name="gather_weights", out_shape=[jax.ShapeDtypeStruct(f.shape, f.dtype) for f in fulls],
        in_specs=[ANY] * (n + 1), out_specs=[ANY] * (n + 1), input_output_aliases={p: p for p in range(n + 1)},
        scratch_shapes=[pltpu.SemaphoreType.DMA((nsem,)), pltpu.SemaphoreType.DMA((nsem,))],
    )(*fulls)


def _sibling_exchange(grads):
    n = len(BIG)
    shapes = []
    for (name, kind), g in zip(BIG, grads):
        R, C = g.shape
        shapes.append(jax.ShapeDtypeStruct((R // 2, C) if kind == "col" else (R, C // 2), g.dtype))

    def body(*refs):
        ins, outs = refs[:n], refs[n:2 * n]
        send, recv = refs[2 * n:]
        x, y, c = _coords()
        sib = (x, y, 1 - c)
        cps = [_remote(_half(ins[p], BIG[p][1], 1 - c), outs[p], send, recv, p, sib) for p in range(n)]
        for cp in cps:
            cp.start()
        for cp in cps:
            cp.wait()

    return pl.pallas_call(
        body, name="sibling_exchange", out_shape=shapes, in_specs=[ANY] * n, out_specs=[ANY] * n,
        scratch_shapes=[pltpu.SemaphoreType.DMA((n,)), pltpu.SemaphoreType.DMA((n,))],
    )(*grads)


def _presum(g, got, kind, pos, name):
    R, C = got.shape
    tr, tc = _tile(R, 512, 16), _tile(C, 2048)
    nr, nc = R // tr, C // tc

    def body(pos_ref, a_ref, b_ref, o_ref):
        o_ref[...] = (a_ref[...].astype(F32) + b_ref[...].astype(F32)).astype(o_ref.dtype)

    if kind == "col":
        mine = pl.BlockSpec((tr, tc), lambda i, j, pos_ref: (pos_ref[1] * nr + i, j))
    else:
        mine = pl.BlockSpec((tr, tc), lambda i, j, pos_ref: (i, pos_ref[1] * nc + j))
    blk = pl.BlockSpec((tr, tc), lambda i, j, pos_ref: (i, j))
    spec = pltpu.PrefetchScalarGridSpec(num_scalar_prefetch=1, grid=(nr, nc), in_specs=[mine, blk], out_specs=blk)
    return _call(body, name=name, grid_spec=spec, out_shape=jax.ShapeDtypeStruct((R, C), WIRE_DTYPE))(pos, g, got)


def _chip_exchange(pres):
    n = len(BIG)
    shapes = []
    for (name, kind), g in zip(BIG, pres):
        R, C = g.shape
        shapes.append(jax.ShapeDtypeStruct((3, R, C // 4) if kind == "col" else (3, R // 4, C), g.dtype))

    def body(*refs):
        ins, outs = refs[:n], refs[n:2 * n]
        send, recv = refs[2 * n:]
        x, y, c = _coords()
        cps = []
        for k, chip in enumerate(_other_chips(x, y)):
            s = 2 * chip[0] + chip[1]
            for p in range(n):
                cps.append(_remote(_shard(ins[p], BIG[p][1], s), outs[p].at[k], send, recv, 3 * p + k, (*chip, c)))
        for cp in cps:
            cp.start()
        for cp in cps:
            cp.wait()

    return pl.pallas_call(
        body, name="chip_exchange", out_shape=shapes, in_specs=[ANY] * n, out_specs=[ANY] * n,
        scratch_shapes=[pltpu.SemaphoreType.DMA((3 * n,)), pltpu.SemaphoreType.DMA((3 * n,))],
    )(*pres)


def _reduce_into_shard(slots, pre, kind, pos, name):
    K, R, C = slots.shape
    tr, tc = _tile(R, 512, 16), _tile(C, 2176)
    nr, nc = R // tr, C // tc

    def body(pos_ref, s_ref, p_ref, o_ref):
        acc = p_ref[...].astype(F32)
        for k in range(K):
            acc = acc + s_ref[k].astype(F32)
        o_ref[...] = acc

    if kind == "col":
        own = pl.BlockSpec((tr, tc), lambda i, j, pos_ref: (i, pos_ref[0] * nc + j))
        full, out = (2 * R, C), pl.BlockSpec((tr, tc), lambda i, j, pos_ref: (pos_ref[1] * nr + i, j))
    else:
        own = pl.BlockSpec((tr, tc), lambda i, j, pos_ref: (pos_ref[0] * nr + i, j))
        full, out = (R, 2 * C), pl.BlockSpec((tr, tc), lambda i, j, pos_ref: (i, pos_ref[1] * nc + j))
    spec = pltpu.PrefetchScalarGridSpec(
        num_scalar_prefetch=1, grid=(nr, nc),
        in_specs=[pl.BlockSpec((K, tr, tc), lambda i, j, pos_ref: (0, i, j)), own], out_specs=out)
    return _call(body, name=name, grid_spec=spec, out_shape=jax.ShapeDtypeStruct(full, F32))(pos, slots, pre)


def _share_reduced(reds):
    n = len(BIG)

    def body(*refs):
        outs = refs[n:2 * n]
        send, recv = refs[2 * n:]
        x, y, c = _coords()
        sib = (x, y, 1 - c)
        cps = []
        for p in range(n):
            mine = _half(outs[p], BIG[p][1], c)
            cps.append(_remote(mine, mine, send, recv, p, sib))
        for cp in cps:
            cp.start()
        for cp in cps:
            cp.wait_send()
        for p in range(n):
            got = _half(outs[p], BIG[p][1], 1 - c)
            _remote(got, got, send, recv, p, sib).wait_recv()

    return pl.pallas_call(
        body, name="share_reduced", out_shape=[jax.ShapeDtypeStruct(r.shape, r.dtype) for r in reds],
        in_specs=[ANY] * n, out_specs=[ANY] * n, input_output_aliases={p: p for p in range(n)},
        scratch_shapes=[pltpu.SemaphoreType.DMA((n,)), pltpu.SemaphoreType.DMA((n,))],
    )(*reds)


def _gather_small(pack):
    _, N = pack.shape

    def body(in_ref, out_ref, send, recv, loc):
        x, y, c = _coords()
        me = 4 * x + 2 * y + c
        own = pltpu.make_async_copy(in_ref, out_ref.at[me], loc)
        own.start()
        cps = []
        for k in range(1, 8):
            dev = (x ^ (k >> 2), y ^ ((k >> 1) & 1), c ^ (k & 1))
            cps.append(_remote(in_ref, out_ref.at[me], send, recv, k - 1, dev))
        for cp in cps:
            cp.start()
        for k in range(1, 8):
            src = 4 * (x ^ (k >> 2)) + 2 * (y ^ ((k >> 1) & 1)) + (c ^ (k & 1))
            _remote(in_ref, out_ref.at[src], send, recv, k - 1, (x, y, c)).wait_recv()
        for cp in cps:
            cp.wait_send()
        own.wait()

    return pl.pallas_call(
        body, name="gather_small", out_shape=jax.ShapeDtypeStruct((8, 1, N), pack.dtype),
        in_specs=[ANY], out_specs=ANY,
        scratch_shapes=[pltpu.SemaphoreType.DMA((7,)), pltpu.SemaphoreType.DMA((7,)), pltpu.SemaphoreType.DMA(())],
    )(pack)


def _sum_small(slots):
    K, _, N = slots.shape

    def body(s_ref, o_ref):
        acc = s_ref[0]
        for k in range(1, K):
            acc = acc + s_ref[k]
        o_ref[...] = acc

    return _call(body, name="sum_small", in_specs=[pl.BlockSpec(memory_space=pltpu.VMEM)],
                 out_specs=pl.BlockSpec(memory_space=pltpu.VMEM), out_shape=jax.ShapeDtypeStruct((1, N), F32))(slots)


def _adamw(w, g, m, v, name):
    R, C = w.shape
    tr, tc = _tile(R, 256, 8), _tile(C, 2176)

    def body(w_ref, g_ref, m_ref, v_ref, d_ref, nm_ref, nv_ref):
        gv = g_ref[...]
        nm = ADAM_B1 * m_ref[...] + (1.0 - ADAM_B1) * gv
        nv = ADAM_B2 * v_ref[...] + (1.0 - ADAM_B2) * gv * gv
        m_hat = nm / (1.0 - ADAM_B1 ** ADAM_STEP)
        v_hat = nv / (1.0 - ADAM_B2 ** ADAM_STEP)
        d_ref[...] = -ADAM_LR * (m_hat / (jnp.sqrt(v_hat) + ADAM_EPS) + ADAM_WD * w_ref[...])
        nm_ref[...] = nm
        nv_ref[...] = nv

    spec = pl.BlockSpec((tr, tc), lambda i, j: (i, j))
    shp = jax.ShapeDtypeStruct((R, C), F32)
    return _call(body, name=name, grid=(R // tr, C // tc), in_specs=[spec] * 4, out_specs=[spec] * 3,
                 out_shape=[shp] * 3)(w, g, m, v)


SMALL = ("norm_g", "mem_norm_g", "attn_q_norm", "attn_k_norm", "conv_w", "mem_q_norm", "mem_k_norm")
WEIGHTS = ("norm_g", "mem_norm_g", "w_in", "attn_q_norm", "attn_k_norm", "conv_w", "mem_w_kv", "mem_q_norm",
           "mem_k_norm", "w_br_attn", "w_br_conv", "w_br_mem", "w_out")


def kernel(x, mem, norm_g, mem_norm_g, w_in, attn_q_norm, attn_k_norm, conv_w, mem_w_kv, mem_q_norm, mem_k_norm, w_br_attn, w_br_conv, w_br_mem, w_out, loss_target, m_norm_g, m_mem_norm_g, m_w_in, m_attn_q_norm, m_attn_k_norm, m_conv_w, m_mem_w_kv, m_mem_q_norm, m_mem_k_norm, m_w_br_attn, m_w_br_conv, m_w_br_mem, m_w_out, v_norm_g, v_mem_norm_g, v_w_in, v_attn_q_norm, v_attn_k_norm, v_conv_w, v_mem_w_kv, v_mem_q_norm, v_mem_k_norm, v_w_br_attn, v_w_br_conv, v_w_br_mem, v_w_out):
    w = dict(norm_g=norm_g, mem_norm_g=mem_norm_g, w_in=w_in, attn_q_norm=attn_q_norm, attn_k_norm=attn_k_norm,
             conv_w=conv_w, mem_w_kv=mem_w_kv, mem_q_norm=mem_q_norm, mem_k_norm=mem_k_norm, w_br_attn=w_br_attn,
             w_br_conv=w_br_conv, w_br_mem=w_br_mem, w_out=w_out)
    m = dict(norm_g=m_norm_g, mem_norm_g=m_mem_norm_g, w_in=m_w_in, attn_q_norm=m_attn_q_norm,
             attn_k_norm=m_attn_k_norm, conv_w=m_conv_w, mem_w_kv=m_mem_w_kv, mem_q_norm=m_mem_q_norm,
             mem_k_norm=m_mem_k_norm, w_br_attn=m_w_br_attn, w_br_conv=m_w_br_conv, w_br_mem=m_w_br_mem, w_out=m_w_out)
    v = dict(norm_g=v_norm_g, mem_norm_g=v_mem_norm_g, w_in=v_w_in, attn_q_norm=v_attn_q_norm,
             attn_k_norm=v_attn_k_norm, conv_w=v_conv_w, mem_w_kv=v_mem_w_kv, mem_q_norm=v_mem_q_norm,
             mem_k_norm=v_mem_k_norm, w_br_attn=v_w_br_attn, w_br_conv=v_w_br_conv, w_br_mem=v_w_br_mem, w_out=v_w_out)
    D = x.shape[-1]
    chip = 2 * lax.axis_index("x") + lax.axis_index("y")
    pos = jnp.stack([chip, lax.axis_index("c")]).astype(jnp.int32)

    fulls = [_place_shard(w[name], kind, pos, WIRE_DTYPE, "place_" + name) for name, kind in BIG]
    fulls.append(_place_shard(conv_w, "col", pos, F32, "place_conv_w"))
    gathered = _gather_weights(fulls)
    W = {name: gathered[p] for p, (name, _) in enumerate(BIG)}
    conv_full = gathered[len(BIG)]

    grad_x, G, small = _local_step(x, mem, loss_target, norm_g, mem_norm_g, attn_q_norm, attn_k_norm, conv_full,
                                   mem_q_norm, mem_k_norm, W)

    parts = [G[name] for name, _ in BIG]
    got = _sibling_exchange(parts)
    pres = [_presum(parts[p], got[p], kind, pos, "presum_" + name) for p, (name, kind) in enumerate(BIG)]
    slots = _chip_exchange(pres)
    reds = [_reduce_into_shard(slots[p], pres[p], kind, pos, "reduce_" + name) for p, (name, kind) in enumerate(BIG)]
    grads = dict(zip([name for name, _ in BIG], _share_reduced(reds)))

    tot = _sum_small(_gather_small(small))[0]
    loss = tot[0]
    off = 128
    for name, size in (("norm_g", D), ("mem_norm_g", D), ("attn_q_norm", NGROUP * HEAD), ("attn_k_norm", NGROUP * HEAD),
                       ("conv_w", 3 * CONVW), ("mem_q_norm", MEM_HD), ("mem_k_norm", MEM_HD)):
        grads[name] = tot[off:off + size]
        off += size
    cw = conv_w.shape[1]
    grads["conv_w"] = lax.dynamic_slice(grads["conv_w"].reshape(3, CONVW), (0, chip * cw), (3, cw))
    for name in SMALL:
        grads[name] = grads[name].reshape(w[name].shape)

    delta, new_m, new_v = {}, {}, {}
    for name, _ in BIG:
        delta[name], new_m[name], new_v[name] = _adamw(w[name], grads[name], m[name], v[name], "adamw_" + name)

    def packed(t):
        return jnp.concatenate([t[name].reshape(1, -1) for name in SMALL], axis=1)

    ds, ms, vs = _adamw(packed(w), packed(grads), packed(m), packed(v), "adamw_small")
    off = 0
    for name in SMALL:
        size = w[name].size
        delta[name] = ds[0, off:off + size].reshape(w[name].shape)
        new_m[name] = ms[0, off:off + size].reshape(w[name].shape)
        new_v[name] = vs[0, off:off + size].reshape(w[name].shape)
        off += size

    return (loss, grad_x, *[grads[n] for n in WEIGHTS], *[delta[n] for n in WEIGHTS],
            *[new_m[n] for n in WEIGHTS], *[new_v[n] for n in WEIGHTS])
```

```python
import functools

import jax
import jax.numpy as jnp
from jax import lax
from jax.experimental import pallas as pl
from jax.experimental.pallas import tpu as pltpu

F32 = jnp.float32
MXU_DTYPE = jnp.bfloat16
WIRE_DTYPE = jnp.bfloat16
EPS = 1e-6
NEG = -1e30

HEAD = 128
HPG = 4
GW = HPG * HEAD
DILATIONS = (1, 4, 16)
NGROUP = len(DILATIONS)
BLK = 128
QKV = NGROUP * GW
CONVW = 1024
MEM_HEADS = 4
MEM_HD = 256
MEMW = MEM_HEADS * MEM_HD
Q0, K0, V0 = 0, QKV, 2 * QKV
ZA = 3 * QKV
CB, CC, CV, ZC = ZA + GW, ZA + GW + CONVW, ZA + GW + 2 * CONVW, ZA + GW + 3 * CONVW
MQ = ZC + CONVW
ZM = MQ + MEMW
G0 = ZM + MEMW

ADAM_LR, ADAM_B1, ADAM_B2, ADAM_EPS, ADAM_WD, ADAM_STEP = 0.001, 0.9, 0.999, 1e-08, 0.01, 10

VMEM_LIMIT = 56 * 1024 * 1024
MESH = pl.DeviceIdType.MESH
ANY = pl.BlockSpec(memory_space=pl.ANY)


def _tile(n, pref, mult=128):
    t = min(pref, n)
    while t > mult and (n % t or t % mult):
        t -= mult
    assert n % t == 0, (n, pref)
    return t


def _call(body, *, name, out_shape, grid=(), in_specs=None, out_specs=None, scratch_shapes=(),
          aliases=None, grid_spec=None):
    kw = {}
    if grid_spec is not None:
        kw["grid_spec"] = grid_spec
        ngrid = len(grid_spec.grid)
    else:
        kw.update(grid=grid, in_specs=in_specs, out_specs=out_specs, scratch_shapes=list(scratch_shapes))
        ngrid = len(grid)
    params = pltpu.CompilerParams(dimension_semantics=("arbitrary",) * ngrid, vmem_limit_bytes=VMEM_LIMIT)
    return pl.pallas_call(body, name=name, out_shape=out_shape, compiler_params=params,
                          input_output_aliases=aliases or {}, **kw)


_DIMS = {"nn": (((1,), (0,)), ((), ())), "nt": (((1,), (1,)), ((), ())), "tn": (((0,), (0,)), ((), ()))}


def _mxu(a, b, mode):
    return lax.dot_general(a.astype(MXU_DTYPE), b.astype(MXU_DTYPE), _DIMS[mode], preferred_element_type=F32)


@functools.partial(jax.custom_vjp, nondiff_argnums=(2,))
def _dot(a, b, mode):
    return _mxu(a, b, mode)


def _dot_fwd(a, b, mode):
    return _mxu(a, b, mode), (a, b)


def _dot_bwd(mode, res, g):
    a, b = res
    if mode == "nn":
        return _mxu(g, b, "nt"), _mxu(a, g, "tn")
    if mode == "nt":
        return _mxu(g, b, "nn"), _mxu(g, a, "tn")
    return _mxu(b, g, "nt"), _mxu(a, g, "nn")


_dot.defvjp(_dot_fwd, _dot_bwd)


def _sig(z):
    return 1.0 / (1.0 + jnp.exp(-z))


def _silu(z):
    return z * _sig(z)


def _rms_rows(t, g):
    return t * lax.rsqrt(jnp.mean(t * t, axis=-1, keepdims=True) + EPS) * g


def _attn_block(q, k2, v2, gq, gk, first):
    qn = _rms_rows(q, gq)
    kn = _rms_rows(k2, gk)
    s = _dot(qn, kn, "nt") * (HEAD ** -0.5)
    a = lax.broadcasted_iota(jnp.int32, (BLK, 2 * BLK), 0)
    b = lax.broadcasted_iota(jnp.int32, (BLK, 2 * BLK), 1)
    lo = jnp.where(first, BLK, 0)
    mask = (b >= a) & (b <= a + BLK) & (b >= lo)
    s = jnp.where(mask, s, NEG)
    m = lax.stop_gradient(jnp.max(s, axis=-1, keepdims=True))
    p = jnp.exp(s - m)
    den = jnp.sum(p, axis=-1, keepdims=True)
    o = _dot(p, v2, "nn") / den
    return o, m + jnp.log(den)


def _combine(o1, o2, o3, l1, l2, l3, z):
    m = lax.stop_gradient(jnp.maximum(jnp.maximum(l1, l2), l3))
    e1, e2, e3 = jnp.exp(l1 - m), jnp.exp(l2 - m), jnp.exp(l3 - m)
    return (e1 * o1 + e2 * o2 + e3 * o3) / (e1 + e2 + e3) * _silu(z)


def _mem_block(q, z, kv, gq, gk):
    outs = []
    for h in range(MEM_HEADS):
        sl = slice(h * MEM_HD, (h + 1) * MEM_HD)
        qn = _rms_rows(q[:, sl], gq)
        kn = _rms_rows(kv[:, sl], gk)
        s = _dot(qn, kn, "nt") * (MEM_HD ** -0.5)
        m = lax.stop_gradient(jnp.max(s, axis=-1, keepdims=True))
        p = jnp.exp(s - m)
        den = jnp.sum(p, axis=-1, keepdims=True)
        outs.append(_dot(p, kv[:, MEMW + h * MEM_HD:MEMW + (h + 1) * MEM_HD], "nn") / den)
    return jnp.concatenate(outs, axis=-1) * _silu(z)


def _cast(w, name):
    R, C = w.shape
    tr, tc = _tile(R, 512, 8), _tile(C, 2176)

    def body(w_ref, o_ref):
        o_ref[...] = w_ref[...].astype(o_ref.dtype)

    spec = pl.BlockSpec((tr, tc), lambda i, j: (i, j))
    return _call(body, name=name, grid=(R // tr, C // tc), in_specs=[spec], out_specs=spec,
                 out_shape=jax.ShapeDtypeStruct((R, C), WIRE_DTYPE))(w)


def _place_shard(w, kind, pos, dtype, name):
    R, C = w.shape
    tr, tc = _tile(R, 512, 8), _tile(C, 2176)
    nr, nc = R // tr, C // tc

    def body(pos_ref, w_ref, o_ref):
        o_ref[...] = w_ref[...].astype(o_ref.dtype)

    if kind == "col":
        full, out = (R, 4 * C), pl.BlockSpec((tr, tc), lambda i, j, pos_ref: (i, pos_ref[0] * nc + j))
    else:
        full, out = (4 * R, C), pl.BlockSpec((tr, tc), lambda i, j, pos_ref: (pos_ref[0] * nr + i, j))
    spec = pltpu.PrefetchScalarGridSpec(num_scalar_prefetch=1, grid=(nr, nc),
                                        in_specs=[pl.BlockSpec((tr, tc), lambda i, j, pos_ref: (i, j))], out_specs=out)
    return _call(body, name=name, grid_spec=spec, out_shape=jax.ShapeDtypeStruct(full, dtype))(pos, w)


def _matmul(a, b, mode, out_dtype, *, name, tm=512, tn=512, tk=512):
    if mode == "nn":
        (M, K), (_, N) = a.shape, b.shape
    elif mode == "nt":
        (M, K), (N, _) = a.shape, b.shape
    else:
        (K, M), (_, N) = a.shape, b.shape
    tm, tn, tk = _tile(M, tm), _tile(N, tn), _tile(K, tk)
    nk = K // tk

    def body(a_ref, b_ref, o_ref, *acc):
        part = lax.dot_general(a_ref[...], b_ref[...], _DIMS[mode], preferred_element_type=F32)
        if nk == 1:
            o_ref[...] = part.astype(o_ref.dtype)
            return
        acc_ref, = acc
        k = pl.program_id(2)

        @pl.when(k == 0)
        def _():
            acc_ref[...] = part

        @pl.when(k > 0)
        def _():
            acc_ref[...] += part

        @pl.when(k == nk - 1)
        def _():
            o_ref[...] = acc_ref[...].astype(o_ref.dtype)

    a_spec = pl.BlockSpec((tk, tm), lambda i, j, k: (k, i)) if mode == "tn" else pl.BlockSpec((tm, tk), lambda i, j, k: (i, k))
    b_spec = pl.BlockSpec((tn, tk), lambda i, j, k: (j, k)) if mode == "nt" else pl.BlockSpec((tk, tn), lambda i, j, k: (k, j))
    return _call(body, name=name, grid=(M // tm, N // tn, nk), in_specs=[a_spec, b_spec],
                 out_specs=pl.BlockSpec((tm, tn), lambda i, j, k: (i, j)),
                 out_shape=jax.ShapeDtypeStruct((M, N), out_dtype),
                 scratch_shapes=[] if nk == 1 else [pltpu.VMEM((tm, tn), F32)])(a, b)


def _rms_fwd(x, g, name):
    R, D = x.shape
    tr = _tile(R, 512)

    def body(x_ref, g_ref, o_ref, t_ref):
        y = _rms_rows(x_ref[...], g_ref[...])
        o_ref[...] = y.astype(o_ref.dtype)
        t_ref[...] = y.T.astype(t_ref.dtype)

    row = pl.BlockSpec((tr, D), lambda i: (i, 0))
    return _call(body, name=name, grid=(R // tr,), in_specs=[row, pl.BlockSpec((1, D), lambda i: (0, 0))],
                 out_specs=[row, pl.BlockSpec((D, tr), lambda i: (0, i))],
                 out_shape=[jax.ShapeDtypeStruct((R, D), MXU_DTYPE), jax.ShapeDtypeStruct((D, R), MXU_DTYPE)])(x, g)


def _rms_bwd(x, dh, g, dy, name):
    R, D = x.shape
    tr = _tile(R, 256)
    with_dx = dy is not None

    def body(*refs):
        if with_dx:
            x_ref, dh_ref, g_ref, dy_ref, dx_ref, dg_ref = refs
        else:
            x_ref, dh_ref, g_ref, dg_ref = refs
        xv, dhv = x_ref[...], dh_ref[...]
        r = lax.rsqrt(jnp.mean(xv * xv, axis=-1, keepdims=True) + EPS)
        xh = xv * r

        @pl.when(pl.program_id(0) == 0)
        def _():
            dg_ref[...] = jnp.zeros_like(dg_ref)

        dg_ref[...] += jnp.sum(dhv * xh, axis=0, keepdims=True)
        if with_dx:
            dxh = dhv * g_ref[...]
            dx_ref[...] = dy_ref[...] + r * (dxh - xh * jnp.mean(dxh * xh, axis=-1, keepdims=True))

    row = pl.BlockSpec((tr, D), lambda i: (i, 0))
    vec = pl.BlockSpec((1, D), lambda i: (0, 0))
    dg_shape = jax.ShapeDtypeStruct((1, D), F32)
    if with_dx:
        return _call(body, name=name, grid=(R // tr,), in_specs=[row, row, vec, row], out_specs=[row, vec],
                     out_shape=[jax.ShapeDtypeStruct((R, D), F32), dg_shape])(x, dh, g, dy)
    return None, _call(body, name=name, grid=(R // tr,), in_specs=[row, row, vec], out_specs=vec,
                       out_shape=dg_shape)(x, dh, g)


def _attn_geom(g, d):
    hc = HPG if d == 1 else 1
    cw = hc * HEAD
    cq, ck, cv = (Q0 + g * GW) // cw, (K0 + g * GW) // cw, (V0 + g * GW) // cw
    return (1, BLK * d, cw), hc, HPG // hc, cq, ck, cv


def _rows(ref, r, d, sl):
    if d == 1:
        return ref[0, :, sl]
    return ref.at[0][pl.ds(r, BLK, stride=d), sl]


def _set_rows(ref, r, d, sl, val):
    if d == 1:
        ref[0, :, sl] = val
    else:
        ref.at[0][pl.ds(r, BLK, stride=d), sl] = val


def _attn_fwd(proj3, gq, gk, g, d):
    Bl, S, _ = proj3.shape
    blk, hc, ncb, cq, ck, cv = _attn_geom(g, d)
    nb = S // blk[1]

    def body(q_ref, kp_ref, kc_ref, vp_ref, vc_ref, gq_ref, gk_ref, o_ref, lse_ref):
        first = pl.program_id(2) == 0
        for r in range(d):
            for h in range(hc):
                sl = slice(h * HEAD, (h + 1) * HEAD)
                k2 = jnp.concatenate([_rows(kp_ref, r, d, sl), _rows(kc_ref, r, d, sl)], axis=0)
                v2 = jnp.concatenate([_rows(vp_ref, r, d, sl), _rows(vc_ref, r, d, sl)], axis=0)
                o, lse = _attn_block(_rows(q_ref, r, d, sl), k2, v2, gq_ref[...], gk_ref[...], first)
                _set_rows(o_ref, r, d, sl, o)
                _set_rows(lse_ref, r, d, sl, jnp.broadcast_to(lse, (BLK, HEAD)))

    def cur(c0):
        return pl.BlockSpec(blk, lambda b, j, i: (b, i, c0 + j))

    def prev(c0):
        return pl.BlockSpec(blk, lambda b, j, i: (b, jnp.maximum(i - 1, 0), c0 + j))

    vec = pl.BlockSpec((1, HEAD), lambda b, j, i: (0, 0))
    out = pl.BlockSpec(blk, lambda b, j, i: (b, i, j))
    shp = jax.ShapeDtypeStruct((Bl, S, GW), F32)
    return _call(body, name=f"attn_fwd_g{g}", grid=(Bl, ncb, nb),
                 in_specs=[cur(cq), prev(ck), cur(ck), prev(cv), cur(cv), vec, vec],
                 out_specs=[out, out], out_shape=[shp, shp])(proj3, proj3, proj3, proj3, proj3, gq, gk)


def _attn_bwd(proj3, gq, gk, do3, dl3, g, d):
    Bl, S, _ = proj3.shape
    blk, hc, ncb, cq, ck, cv = _attn_geom(g, d)
    nb = S // blk[1]

    def body(q_ref, kp_ref, kc_ref, vp_ref, vc_ref, gq_ref, gk_ref, do_ref, dl_ref,
             dq_ref, dk_ref, dv_ref, dgq_ref, dgk_ref, ck_ref, cv_ref):
        i = pl.program_id(2)
        first = i == 0

        @pl.when((pl.program_id(0) == 0) & (pl.program_id(1) == 0) & first)
        def _():
            dgq_ref[...] = jnp.zeros_like(dgq_ref)
            dgk_ref[...] = jnp.zeros_like(dgk_ref)

        @pl.when(i < nb)
        def _():
            for r in range(d):
                rs = slice(r * BLK, (r + 1) * BLK)
                for h in range(hc):
                    sl = slice(h * HEAD, (h + 1) * HEAD)
                    k2 = jnp.concatenate([_rows(kp_ref, r, d, sl), _rows(kc_ref, r, d, sl)], axis=0)
                    v2 = jnp.concatenate([_rows(vp_ref, r, d, sl), _rows(vc_ref, r, d, sl)], axis=0)
                    _, vjp = jax.vjp(lambda q, k, v, a, b: _attn_block(q, k, v, a, b, first),
                                     _rows(q_ref, r, d, sl), k2, v2, gq_ref[...], gk_ref[...])
                    dq, dk2, dv2, dgq, dgk = vjp((_rows(do_ref, r, d, sl), _rows(dl_ref, r, d, sl)[:, :1]))
                    _set_rows(dq_ref, r, d, sl, dq)
                    dgq_ref[...] += dgq
                    dgk_ref[...] += dgk

                    @pl.when(i > 0)
                    def _():
                        _set_rows(dk_ref, r, d, sl, ck_ref[rs, sl] + dk2[:BLK])
                        _set_rows(dv_ref, r, d, sl, cv_ref[rs, sl] + dv2[:BLK])

                    ck_ref[rs, sl] = dk2[BLK:]
                    cv_ref[rs, sl] = dv2[BLK:]

        @pl.when(i == nb)
        def _():
            for r in range(d):
                rs = slice(r * BLK, (r + 1) * BLK)
                _set_rows(dk_ref, r, d, slice(None), ck_ref[rs, :])
                _set_rows(dv_ref, r, d, slice(None), cv_ref[rs, :])

    def cur(c0):
        return pl.BlockSpec(blk, lambda b, j, i: (b, jnp.minimum(i, nb - 1), c0 + j))

    def prev(c0):
        return pl.BlockSpec(blk, lambda b, j, i: (b, jnp.clip(i - 1, 0, nb - 1), c0 + j))

    vec = pl.BlockSpec((1, HEAD), lambda b, j, i: (0, 0))
    at_q = pl.BlockSpec(blk, lambda b, j, i: (b, jnp.minimum(i, nb - 1), j))
    at_k = pl.BlockSpec(blk, lambda b, j, i: (b, jnp.maximum(i - 1, 0), j))
    shp = jax.ShapeDtypeStruct((Bl, S, GW), F32)
    gshp = jax.ShapeDtypeStruct((1, HEAD), F32)
    return _call(body, name=f"attn_bwd_g{g}", grid=(Bl, ncb, nb + 1),
                 in_specs=[cur(cq), prev(ck), cur(ck), prev(cv), cur(cv), vec, vec, at_q, at_q],
                 out_specs=[at_q, at_k, at_k, vec, vec], out_shape=[shp, shp, shp, gshp, gshp],
                 scratch_shapes=[pltpu.VMEM(blk[1:], F32), pltpu.VMEM(blk[1:], F32)],
                 )(proj3, proj3, proj3, proj3, proj3, gq, gk, do3, dl3)


def _combine_fwd(os, ls, proj2):
    T = proj2.shape[0]
    tr = _tile(T, 512)

    def body(o1, o2, o3, l1, l2, l3, z, a_ref):
        a_ref[...] = _combine(o1[...], o2[...], o3[...], l1[...], l2[...], l3[...], z[...]).astype(a_ref.dtype)

    row = pl.BlockSpec((tr, GW), lambda i: (i, 0))
    return _call(body, name="combine_fwd", grid=(T // tr,),
                 in_specs=[row] * 6 + [pl.BlockSpec((tr, GW), lambda i: (i, ZA // GW))], out_specs=row,
                 out_shape=jax.ShapeDtypeStruct((T, GW), MXU_DTYPE))(*os, *ls, proj2)


def _combine_bwd(os, ls, proj2, da):
    T = proj2.shape[0]
    tr = _tile(T, 256)

    def body(o1, o2, o3, l1, l2, l3, z, da_ref, d1, d2, d3, e1, e2, e3, dz_ref):
        _, vjp = jax.vjp(_combine, o1[...], o2[...], o3[...], l1[...], l2[...], l3[...], z[...])
        go1, go2, go3, gl1, gl2, gl3, gz = vjp(da_ref[...])
        d1[...], d2[...], d3[...] = go1, go2, go3
        dz_ref[...] = gz.astype(dz_ref.dtype)
        for ref, gl in ((e1, gl1), (e2, gl2), (e3, gl3)):
            for h in range(HPG):
                sl = slice(h * HEAD, (h + 1) * HEAD)
                ref[:, sl] = jnp.broadcast_to(jnp.sum(gl[:, sl], axis=-1, keepdims=True), (tr, HEAD))

    row = pl.BlockSpec((tr, GW), lambda i: (i, 0))
    f = jax.ShapeDtypeStruct((T, GW), F32)
    outs = _call(body, name="combine_bwd", grid=(T // tr,),
                 in_specs=[row] * 6 + [pl.BlockSpec((tr, GW), lambda i: (i, ZA // GW)), row],
                 out_specs=[row] * 7, out_shape=[f] * 6 + [jax.ShapeDtypeStruct((T, GW), MXU_DTYPE)],
                 )(*os, *ls, proj2, da)
    return outs[:3], outs[3:6], outs[6]


def _shift_down(u, j, t):
    return jnp.where(t >= j, pltpu.roll(u, j, 0), 0.0)


def _shift_up(u, j, t):
    n = u.shape[0]
    return jnp.where(t < n - j, pltpu.roll(u, n - j, 0), 0.0)


def _conv_specs(Bl, S, cw):
    def sec(c0):
        return pl.BlockSpec((1, S, cw), lambda j, b: (b, 0, c0 // cw + j))
    return [sec(CB), sec(CC), sec(CV), sec(ZC)], pl.BlockSpec((3, cw), lambda j, b: (0, j))


def _conv_fwd(proj3, conv_w):
    Bl, S, _ = proj3.shape
    cw = 256
    secs, wspec = _conv_specs(Bl, S, cw)

    def body(b_ref, c_ref, v_ref, z_ref, w_ref, o_ref):
        t = lax.broadcasted_iota(jnp.int32, (S, cw), 0)
        u = c_ref[0] * v_ref[0]
        y = w_ref[0:1, :] * u + w_ref[1:2, :] * _shift_down(u, 1, t) + w_ref[2:3, :] * _shift_down(u, 2, t)
        o_ref[0] = (b_ref[0] * y * _silu(z_ref[0])).astype(o_ref.dtype)

    return _call(body, name="conv_fwd", grid=(CONVW // cw, Bl), in_specs=secs + [wspec],
                 out_specs=pl.BlockSpec((1, S, cw), lambda j, b: (b, 0, j)),
                 out_shape=jax.ShapeDtypeStruct((Bl, S, CONVW), MXU_DTYPE))(proj3, proj3, proj3, proj3, conv_w)


def _conv_bwd(proj3, conv_w, dcc3):
    Bl, S, _ = proj3.shape
    cw = 256
    secs, wspec = _conv_specs(Bl, S, cw)

    def body(b_ref, c_ref, v_ref, z_ref, w_ref, d_ref, db_ref, dc_ref, dv_ref, dz_ref, dw_ref):
        t = lax.broadcasted_iota(jnp.int32, (S, cw), 0)
        bv, cv, vv, zv, dv = b_ref[0], c_ref[0], v_ref[0], z_ref[0], d_ref[0]
        u = cv * vv
        u1, u2 = _shift_down(u, 1, t), _shift_down(u, 2, t)
        y = w_ref[0:1, :] * u + w_ref[1:2, :] * u1 + w_ref[2:3, :] * u2
        sg = _sig(zv)
        sz = zv * sg
        gy = dv * bv * sz
        db_ref[0] = (dv * y * sz).astype(db_ref.dtype)
        dz_ref[0] = (dv * bv * y * sg * (1.0 + zv * (1.0 - sg))).astype(dz_ref.dtype)
        du = w_ref[0:1, :] * gy + w_ref[1:2, :] * _shift_up(gy, 1, t) + w_ref[2:3, :] * _shift_up(gy, 2, t)
        dc_ref[0] = (du * vv).astype(dc_ref.dtype)
        dv_ref[0] = (du * cv).astype(dv_ref.dtype)

        @pl.when(pl.program_id(1) == 0)
        def _():
            dw_ref[...] = jnp.zeros_like(dw_ref)

        dw_ref[0:1, :] += jnp.sum(gy * u, axis=0, keepdims=True)
        dw_ref[1:2, :] += jnp.sum(gy * u1, axis=0, keepdims=True)
        dw_ref[2:3, :] += jnp.sum(gy * u2, axis=0, keepdims=True)

    blk = pl.BlockSpec((1, S, cw), lambda j, b: (b, 0, j))
    shp = jax.ShapeDtypeStruct((Bl, S, CONVW), MXU_DTYPE)
    return _call(body, name="conv_bwd", grid=(CONVW // cw, Bl), in_specs=secs + [wspec, blk],
                 out_specs=[blk] * 4 + [wspec], out_shape=[shp] * 4 + [jax.ShapeDtypeStruct((3, CONVW), F32)],
                 )(proj3, proj3, proj3, proj3, conv_w, dcc3)


def _mem_specs(S, tq):
    q = pl.BlockSpec((1, tq, MEMW), lambda b, j: (b, j, MQ // MEMW))
    z = pl.BlockSpec((1, tq, MEMW), lambda b, j: (b, j, ZM // MEMW))
    kv = pl.BlockSpec((1, MEM_HD, 2 * MEMW), lambda b, j: (b, 0, 0))
    vec = pl.BlockSpec((1, MEM_HD), lambda b, j: (0, 0))
    blk = pl.BlockSpec((1, tq, MEMW), lambda b, j: (b, j, 0))
    return q, z, kv, vec, blk


def _mem_fwd(proj3, mkv3, gq, gk):
    Bl, S, _ = proj3.shape
    tq = _tile(S, 512)
    q, z, kv, vec, blk = _mem_specs(S, tq)

    def body(q_ref, z_ref, kv_ref, gq_ref, gk_ref, o_ref):
        o_ref[0] = _mem_block(q_ref[0], z_ref[0], kv_ref[0], gq_ref[...], gk_ref[...]).astype(o_ref.dtype)

    return _call(body, name="mem_fwd", grid=(Bl, S // tq), in_specs=[q, z, kv, vec, vec], out_specs=blk,
                 out_shape=jax.ShapeDtypeStruct((Bl, S, MEMW), MXU_DTYPE))(proj3, proj3, mkv3, gq, gk)


def _mem_bwd(proj3, mkv3, gq, gk, dmo3):
    Bl, S, _ = proj3.shape
    tq = _tile(S, 256)
    q, z, kv, vec, blk = _mem_specs(S, tq)

    def body(q_ref, z_ref, kv_ref, gq_ref, gk_ref, d_ref, dq_ref, dz_ref, dkv_ref, dgq_ref, dgk_ref):
        _, vjp = jax.vjp(_mem_block, q_ref[0], z_ref[0], kv_ref[0], gq_ref[...], gk_ref[...])
        dq, dz, dkv, dgq, dgk = vjp(d_ref[0])
        dq_ref[0] = dq.astype(dq_ref.dtype)
        dz_ref[0] = dz.astype(dz_ref.dtype)
        j = pl.program_id(1)

        @pl.when(j == 0)
        def _():
            dkv_ref[0] = jnp.zeros_like(dkv)

        @pl.when((j == 0) & (pl.program_id(0) == 0))
        def _():
            dgq_ref[...] = jnp.zeros_like(dgq_ref)
            dgk_ref[...] = jnp.zeros_like(dgk_ref)

        dkv_ref[0] += dkv
        dgq_ref[...] += dgq
        dgk_ref[...] += dgk

    shp = jax.ShapeDtypeStruct((Bl, S, MEMW), MXU_DTYPE)
    gshp = jax.ShapeDtypeStruct((1, MEM_HD), F32)
    return _call(body, name="mem_bwd", grid=(Bl, S // tq), in_specs=[q, z, kv, vec, vec, blk],
                 out_specs=[blk, blk, kv, vec, vec],
                 out_shape=[shp, shp, jax.ShapeDtypeStruct(mkv3.shape, F32), gshp, gshp],
                 )(proj3, proj3, mkv3, gq, gk, dmo3)


def _merge_specs(T, D, tm, tn):
    def act(w):
        return pl.BlockSpec((tm, w), lambda i, n: (i, 0))

    def wsp(w):
        return pl.BlockSpec((w, tn), lambda i, n: (0, n))

    gates = [pl.BlockSpec((tm, tn), lambda i, n, k=k: (i, (G0 + k * D) // tn + n)) for k in range(3)]
    tile = pl.BlockSpec((tm, tn), lambda i, n: (i, n))
    return act, wsp, gates, tile


def _merge_fwd(a, cc, mo, wa, wc, wm, proj2):
    T, D = a.shape[0], wa.shape[1]
    tm, tn = _tile(T, 512), _tile(D, 512)
    act, wsp, gates, tile = _merge_specs(T, D, tm, tn)

    def body(a_ref, c_ref, m_ref, wa_ref, wc_ref, wm_ref, g0, g1, g2, mg_ref, mt_ref, pa_ref, pc_ref, pm_ref):
        pa = jnp.dot(a_ref[...], wa_ref[...], preferred_element_type=F32)
        pc = jnp.dot(c_ref[...], wc_ref[...], preferred_element_type=F32)
        pm = jnp.dot(m_ref[...], wm_ref[...], preferred_element_type=F32)
        mg = _sig(g0[...]) * pa + _sig(g1[...]) * pc + _sig(g2[...]) * pm
        mg_ref[...] = mg.astype(mg_ref.dtype)
        mt_ref[...] = mg.T.astype(mt_ref.dtype)
        pa_ref[...] = pa.astype(pa_ref.dtype)
        pc_ref[...] = pc.astype(pc_ref.dtype)
        pm_ref[...] = pm.astype(pm_ref.dtype)

    shp = jax.ShapeDtypeStruct((T, D), MXU_DTYPE)
    return _call(body, name="merge_fwd", grid=(T // tm, D // tn),
                 in_specs=[act(GW), act(CONVW), act(MEMW), wsp(GW), wsp(CONVW), wsp(MEMW)] + gates,
                 out_specs=[tile, pl.BlockSpec((tn, tm), lambda i, n: (n, i)), tile, tile, tile],
                 out_shape=[shp, jax.ShapeDtypeStruct((D, T), MXU_DTYPE), shp, shp, shp],
                 )(a, cc, mo, wa, wc, wm, proj2, proj2, proj2)


def _merge_bwd(dyb, w_out, proj2, pa, pc, pm):
    T, D = dyb.shape
    tm, tn = _tile(T, 512), _tile(D, 512)
    _, _, gates, tile = _merge_specs(T, D, tm, tn)

    def body(dy_ref, w_ref, g0, g1, g2, p0, p1, p2, dp0, dp1, dp2, dg0, dg1, dg2):
        dm = lax.dot_general(dy_ref[...], w_ref[...], _DIMS["nt"], preferred_element_type=F32)
        for g_ref, p_ref, dp_ref, dg_ref in ((g0, p0, dp0, dg0), (g1, p1, dp1, dg1), (g2, p2, dp2, dg2)):
            gt = _sig(g_ref[...])
            dp_ref[...] = (gt * dm).astype(dp_ref.dtype)
            dg_ref[...] = (dm * p_ref[...].astype(F32) * gt * (1.0 - gt)).astype(dg_ref.dtype)

    shp = jax.ShapeDtypeStruct((T, D), MXU_DTYPE)
    return _call(body, name="merge_bwd", grid=(T // tm, D // tn),
                 in_specs=[pl.BlockSpec((tm, D), lambda i, n: (i, 0)), pl.BlockSpec((tn, D), lambda i, n: (n, 0))]
                 + gates + [tile] * 3,
                 out_specs=[tile] * 6, out_shape=[shp] * 6)(dyb, w_out, proj2, proj2, proj2, pa, pc, pm)


def _out_loss(merged, w_out, x, tgt):
    T, D = x.shape
    tm = _tile(T, 256)

    def body(m_ref, w_ref, x_ref, t_ref, dy_ref, dyb_ref, loss_ref):
        err = x_ref[...] + jnp.dot(m_ref[...], w_ref[...], preferred_element_type=F32) - t_ref[...]
        dy = err * (1.0 / D)
        dy_ref[...] = dy
        dyb_ref[...] = dy.astype(dyb_ref.dtype)

        @pl.when(pl.program_id(0) == 0)
        def _():
            loss_ref[...] = jnp.zeros_like(loss_ref)

        loss_ref[...] += jnp.sum(err * err) * (0.5 / D)

    row = pl.BlockSpec((tm, D), lambda i: (i, 0))
    return _call(body, name="out_loss", grid=(T // tm,),
                 in_specs=[row, pl.BlockSpec((D, D), lambda i: (0, 0)), row, row],
                 out_specs=[row, row, pl.BlockSpec((1, 128), lambda i: (0, 0))],
                 out_shape=[jax.ShapeDtypeStruct((T, D), F32), jax.ShapeDtypeStruct((T, D), MXU_DTYPE),
                            jax.ShapeDtypeStruct((1, 128), F32)])(merged, w_out, x, tgt)


def _weight_grads(x, mem, tgt, norm_g, mem_norm_g, gq_all, gk_all, conv_w, mem_gq, mem_gk, W):
    Bl, S, D = x.shape
    T = Bl * S
    IN = W["w_in"].shape[1]
    x2, tgt2 = x.reshape(T, D), tgt.reshape(T, D)
    mem2 = mem.reshape(-1, D)
    ng, mng = norm_g.reshape(1, D), mem_norm_g.reshape(1, D)
    mgq, mgk = mem_gq.reshape(1, MEM_HD), mem_gk.reshape(1, MEM_HD)
    gqs = [gq_all[g:g + 1] for g in range(NGROUP)]
    gks = [gk_all[g:g + 1] for g in range(NGROUP)]

    hb, hbt = _rms_fwd(x2, ng, "rms_x")
    proj2 = _matmul(hb, W["w_in"], "nn", F32, name="proj", tm=2048, tn=512, tk=D)
    proj3 = proj2.reshape(Bl, S, IN)
    os, ls = [], []
    for g, d in enumerate(DILATIONS):
        o, l = _attn_fwd(proj3, gqs[g], gks[g], g, d)
        os.append(o.reshape(T, GW))
        ls.append(l.reshape(T, GW))
    a = _combine_fwd(os, ls, proj2)
    cc = _conv_fwd(proj3, conv_w).reshape(T, CONVW)
    mhb, _ = _rms_fwd(mem2, mng, "rms_mem")
    mkv = _matmul(mhb, W["mem_w_kv"], "nn", F32, name="mem_kv", tm=512, tn=1024, tk=D)
    mkv3 = mkv.reshape(Bl, -1, 2 * MEMW)
    mo = _mem_fwd(proj3, mkv3, mgq, mgk).reshape(T, MEMW)
    merged, mergedt, pa, pc, pm = _merge_fwd(a, cc, mo, W["w_br_attn"], W["w_br_conv"], W["w_br_mem"], proj2)
    dy, dyb, loss = _out_loss(merged, W["w_out"], x2, tgt2)

    G = {}
    G["w_out"] = _matmul(mergedt, dyb, "nn", WIRE_DTYPE, name="dw_out", tm=1024, tn=1024, tk=1024)
    dpa, dpc, dpm, dg0, dg1, dg2 = _merge_bwd(dyb, W["w_out"], proj2, pa, pc, pm)
    G["w_br_attn"] = _matmul(a, dpa, "tn", WIRE_DTYPE, name="dw_br_attn", tm=512, tn=1024, tk=512)
    G["w_br_conv"] = _matmul(cc, dpc, "tn", WIRE_DTYPE, name="dw_br_conv", tm=1024, tn=1024, tk=512)
    G["w_br_mem"] = _matmul(mo, dpm, "tn", WIRE_DTYPE, name="dw_br_mem", tm=1024, tn=1024, tk=512)
    da = _matmul(dpa, W["w_br_attn"], "nt", F32, name="d_attn", tm=1024, tn=512, tk=D)
    dcc = _matmul(dpc, W["w_br_conv"], "nt", F32, name="d_conv", tm=1024, tn=1024, tk=D)
    dmo = _matmul(dpm, W["w_br_mem"], "nt", F32, name="d_mem", tm=1024, tn=1024, tk=D)

    dos, dls, dza = _combine_bwd(os, ls, proj2, da)
    dqs, dks, dvs, dgq, dgk = [], [], [], [], []
    for g, d in enumerate(DILATIONS):
        dq, dk, dv, gq_g, gk_g = _attn_bwd(proj3, gqs[g], gks[g], dos[g].reshape(Bl, S, GW),
                                           dls[g].reshape(Bl, S, GW), g, d)
        dqs.append(dq.reshape(T, GW).astype(MXU_DTYPE))
        dks.append(dk.reshape(T, GW).astype(MXU_DTYPE))
        dvs.append(dv.reshape(T, GW).astype(MXU_DTYPE))
        dgq.append(gq_g)
        dgk.append(gk_g)
    dcb, dcc_, dcv, dzc, dconv_w = _conv_bwd(proj3, conv_w, dcc.reshape(Bl, S, CONVW))
    dmq, dzm, dmkv3, dmgq, dmgk = _mem_bwd(proj3, mkv3, mgq, mgk, dmo.reshape(Bl, S, MEMW))

    dmkv = _cast(dmkv3.reshape(-1, 2 * MEMW), "cast_dmkv")
    G["mem_w_kv"] = _matmul(mhb, dmkv, "tn", WIRE_DTYPE, name="dw_mem_kv", tm=1024, tn=1024, tk=512)
    dmh = _matmul(dmkv, W["mem_w_kv"], "nt", F32, name="d_memh", tm=512, tn=1024, tk=2 * MEMW)
    _, dmng = _rms_bwd(mem2, dmh, mng, None, "rms_mem_bwd")

    dproj = jnp.concatenate(dqs + dks + dvs + [dza] + [t.reshape(T, CONVW) for t in (dcb, dcc_, dcv, dzc)]
                            + [dmq.reshape(T, MEMW), dzm.reshape(T, MEMW), dg0, dg1, dg2], axis=1)
    G["w_in"] = _matmul(hbt, dproj, "nn", WIRE_DTYPE, name="dw_in", tm=1024, tn=1024, tk=1024)
    small = [loss, None, dmng] + dgq + dgk + [dconv_w.reshape(1, 3 * CONVW), dmgq, dmgk]
    return G, (dproj, x2, ng, dy, small)


def _input_grad(rest, w_in):
    dproj, x2, ng, dy, small = rest
    dh = _matmul(dproj, w_in, "nt", F32, name="d_h", tm=1024, tn=1024, tk=1024)
    grad_x, dng = _rms_bwd(x2, dh, ng, dy, "rms_x_bwd")
    small = [dng if t is None else t for t in small]
    return grad_x, jnp.concatenate(small, axis=1)


def _local_step(x, mem, tgt, norm_g, mem_norm_g, gq_all, gk_all, conv_w, mem_gq, mem_gk, W):
    G, rest = _weight_grads(x, mem, tgt, norm_g, mem_norm_g, gq_all, gk_all, conv_w, mem_gq, mem_gk, W)
    grad_x, small = _input_grad(rest, W["w_in"])
    return grad_x.reshape(x.shape), G, small


BIG = (("w_in", "col"), ("mem_w_kv", "row"), ("w_br_attn", "col"), ("w_br_conv", "col"),
       ("w_br_mem", "col"), ("w_out", "row"))


def _coords():
    return lax.axis_index("x"), lax.axis_index("y"), lax.axis_index("c")


def _other_chips(x, y):
    return [(1 - x, y), (x, 1 - y), (1 - x, 1 - y)]


def _half(ref, kind, c):
    R, C = ref.shape
    if kind == "col":
        return ref.at[pl.ds(c * (R // 2), R // 2), :]
    return ref.at[:, pl.ds(c * (C // 2), C // 2)]


def _shard(ref, kind, s):
    R, C = ref.shape
    if kind == "col":
        return ref.at[:, pl.ds(s * (C // 4), C // 4)]
    return ref.at[pl.ds(s * (R // 4), R // 4), :]


def _piece(ref, kind, s, c):
    R, C = ref.shape
    if kind == "col":
        return ref.at[pl.ds(c * (R // 2), R // 2), pl.ds(s * (C // 4), C // 4)]
    return ref.at[pl.ds(s * (R // 4), R // 4), pl.ds(c * (C // 2), C // 2)]


def _remote(src, dst, sems_s, sems_r, k, dev):
    return pltpu.make_async_remote_copy(src_ref=src, dst_ref=dst, send_sem=sems_s.at[k], recv_sem=sems_r.at[k],
                                        device_id=dev, device_id_type=MESH)


def _gather_weights(fulls):
    n = len(BIG)

    def body(*refs):
        outs = refs[n + 1:2 * n + 2]
        send, recv = refs[2 * n + 2:]
        x, y, c = _coords()
        me = 2 * x + y
        chips = _other_chips(x, y)
        sib = (x, y, 1 - c)
        kinds = [k for _, k in BIG]
        first = []
        for p in range(n):
            for k, chip in enumerate(chips):
                mine = _piece(outs[p], kinds[p], me, c)
                first.append(_remote(mine, mine, send, recv, 6 * p + k, (*chip, c)))
        for k, chip in enumerate(chips):
            mine = _shard(outs[n], "col", me)
            first.append(_remote(mine, mine, send, recv, 6 * n + k, (*chip, c)))
        for cp in first:
            cp.start()
        passed = []
        for k, chip in enumerate(chips):
            s = 2 * chip[0] + chip[1]
            for p in range(n):
                got = _piece(outs[p], kinds[p], s, c)
                _remote(got, got, send, recv, 6 * p + k, sib).wait_recv()
                fwd = _remote(got, got, send, recv, 6 * p + 3 + k, sib)
                fwd.start()
                passed.append(fwd)
            got = _shard(outs[n], "col", s)
            _remote(got, got, send, recv, 6 * n + k, sib).wait_recv()
        for k, chip in enumerate(chips):
            s = 2 * chip[0] + chip[1]
            for p in range(n):
                got = _piece(outs[p], kinds[p], s, 1 - c)
                _remote(got, got, send, recv, 6 * p + 3 + k, sib).wait_recv()
        for cp in first + passed:
            cp.wait_send()

    nsem = 6 * n + 3
    return pl.pallas_call(
        body, name="gather_weights", out_shape=[jax.ShapeDtypeStruct(f.shape, f.dtype) for f in fulls],
        in_specs=[ANY] * (n + 1), out_specs=[ANY] * (n + 1), input_output_aliases={p: p for p in range(n + 1)},
        scratch_shapes=[pltpu.SemaphoreType.DMA((nsem,)), pltpu.SemaphoreType.DMA((nsem,))],
    )(*fulls)


def _sibling_exchange(grads):
    n = len(BIG)
    shapes = []
    for (name, kind), g in zip(BIG, grads):
        R, C = g.shape
        shapes.append(jax.ShapeDtypeStruct((R // 2, C) if kind == "col" else (R, C // 2), g.dtype))

    def body(*refs):
        ins, outs = refs[:n], refs[n:2 * n]
        send, recv = refs[2 * n:]
        x, y, c = _coords()
        sib = (x, y, 1 - c)
        cps = [_remote(_half(ins[p], BIG[p][1], 1 - c), outs[p], send, recv, p, sib) for p in range(n)]
        for cp in cps:
            cp.start()
        for cp in cps:
            cp.wait()

    return pl.pallas_call(
        body, name="sibling_exchange", out_shape=shapes, in_specs=[ANY] * n, out_specs=[ANY] * n,
        scratch_shapes=[pltpu.SemaphoreType.DMA((n,)), pltpu.SemaphoreType.DMA((n,))],
    )(*grads)


def _presum(g, got, kind, pos, name):
    R, C = got.shape
    tr, tc = _tile(R, 512, 16), _tile(C, 2048)
    nr, nc = R // tr, C // tc

    def body(pos_ref, a_ref, b_ref, o_ref):
        o_ref[...] = (a_ref[...].astype(F32) + b_ref[...].astype(F32)).astype(o_ref.dtype)

    if kind == "col":
        mine = pl.BlockSpec((tr, tc), lambda i, j, pos_ref: (pos_ref[1] * nr + i, j))
    else:
        mine = pl.BlockSpec((tr, tc), lambda i, j, pos_ref: (i, pos_ref[1] * nc + j))
    blk = pl.BlockSpec((tr, tc), lambda i, j, pos_ref: (i, j))
    spec = pltpu.PrefetchScalarGridSpec(num_scalar_prefetch=1, grid=(nr, nc), in_specs=[mine, blk], out_specs=blk)
    return _call(body, name=name, grid_spec=spec, out_shape=jax.ShapeDtypeStruct((R, C), WIRE_DTYPE))(pos, g, got)


HBM = pl.BlockSpec(memory_space=pltpu.HBM)
SEM = pl.BlockSpec(memory_space=pltpu.SEMAPHORE)
EFFECT = pltpu.SideEffectType.DATAFLOW_SIDE_EFFECTING


def _hbm(a):
    return pltpu.with_memory_space_constraint(a, pltpu.HBM)


def _chip_copies(pre_refs, land_refs, send, recv):
    x, y, c = _coords()
    cps = []
    for k, chip in enumerate(_other_chips(x, y)):
        s = 2 * chip[0] + chip[1]
        for p in range(len(BIG)):
            cps.append(_remote(_shard(pre_refs[p], BIG[p][1], s), land_refs[p].at[k], send, recv, 3 * p + k, (*chip, c)))
    return cps


def _chip_exchange_start(pres, carry):
    n = len(BIG)
    lands = []
    for (name, kind), g in zip(BIG, pres):
        R, C = g.shape
        lands.append(_hbm(lax.empty((3, R, C // 4) if kind == "col" else (3, R // 4, C), g.dtype)))

    def body(*refs):
        pre_refs, land_refs = refs[:n], refs[n:2 * n]
        send, recv = refs[2 * n + 1], refs[2 * n + 2]
        for cp in _chip_copies(pre_refs, land_refs, send, recv):
            cp.start()

    shapes = [jax.ShapeDtypeStruct(t.shape, t.dtype) for t in (*pres, *lands, carry)]
    outs = pl.pallas_call(
        body, name="chip_exchange_start",
        out_shape=(pltpu.SemaphoreType.DMA((3 * n,)), pltpu.SemaphoreType.DMA((3 * n,)), *shapes),
        in_specs=[HBM] * (2 * n + 1), out_specs=(SEM, SEM, *([HBM] * (2 * n + 1))),
        input_output_aliases={i: i + 2 for i in range(2 * n + 1)},
        compiler_params=pltpu.CompilerParams(has_side_effects=EFFECT),
    )(*[_hbm(t) for t in pres], *lands, _hbm(carry))
    return outs[0], outs[1], outs[2:2 + n], outs[2 + n:2 + 2 * n], outs[2 + 2 * n]


def _chip_exchange_wait(send, recv, pres, lands, after):
    n = len(BIG)

    def body(*refs):
        pre_refs, land_refs = refs[:n], refs[n:2 * n]
        send_ref, recv_ref = refs[2 * n], refs[2 * n + 1]
        for cp in _chip_copies(pre_refs, land_refs, send_ref, recv_ref):
            cp.wait_send()
            cp.wait_recv()

    shapes = [jax.ShapeDtypeStruct(t.shape, t.dtype) for t in (*pres, *lands)]
    outs = pl.pallas_call(
        body, name="chip_exchange_wait", out_shape=shapes,
        in_specs=[HBM] * (2 * n) + [SEM, SEM, ANY], out_specs=[HBM] * (2 * n),
        input_output_aliases={i: i for i in range(2 * n)},
        compiler_params=pltpu.CompilerParams(has_side_effects=EFFECT),
    )(*pres, *lands, send, recv, after)
    return outs[:n], outs[n:]


def _reduce_into_shard(slots, pre, kind, pos, name):
    K, R, C = slots.shape
    tr, tc = _tile(R, 512, 16), _tile(C, 2176)
    nr, nc = R // tr, C // tc

    def body(pos_ref, s_ref, p_ref, o_ref):
        acc = p_ref[...].astype(F32)
        for k in range(K):
            acc = acc + s_ref[k].astype(F32)
        o_ref[...] = acc

    if kind == "col":
        own = pl.BlockSpec((tr, tc), lambda i, j, pos_ref: (i, pos_ref[0] * nc + j))
        full, out = (2 * R, C), pl.BlockSpec((tr, tc), lambda i, j, pos_ref: (pos_ref[1] * nr + i, j))
    else:
        own = pl.BlockSpec((tr, tc), lambda i, j, pos_ref: (pos_ref[0] * nr + i, j))
        full, out = (R, 2 * C), pl.BlockSpec((tr, tc), lambda i, j, pos_ref: (i, pos_ref[1] * nc + j))
    spec = pltpu.PrefetchScalarGridSpec(
        num_scalar_prefetch=1, grid=(nr, nc),
        in_specs=[pl.BlockSpec((K, tr, tc), lambda i, j, pos_ref: (0, i, j)), own], out_specs=out)
    return _call(body, name=name, grid_spec=spec, out_shape=jax.ShapeDtypeStruct(full, F32))(pos, slots, pre)


def _share_reduced(reds):
    n = len(BIG)

    def body(*refs):
        outs = refs[n:2 * n]
        send, recv = refs[2 * n:]
        x, y, c = _coords()
        sib = (x, y, 1 - c)
        cps = []
        for p in range(n):
            mine = _half(outs[p], BIG[p][1], c)
            cps.append(_remote(mine, mine, send, recv, p, sib))
        for cp in cps:
            cp.start()
        for cp in cps:
            cp.wait_send()
        for p in range(n):
            got = _half(outs[p], BIG[p][1], 1 - c)
            _remote(got, got, send, recv, p, sib).wait_recv()

    return pl.pallas_call(
        body, name="share_reduced", out_shape=[jax.ShapeDtypeStruct(r.shape, r.dtype) for r in reds],
        in_specs=[ANY] * n, out_specs=[ANY] * n, input_output_aliases={p: p for p in range(n)},
        scratch_shapes=[pltpu.SemaphoreType.DMA((n,)), pltpu.SemaphoreType.DMA((n,))],
    )(*reds)


def _gather_small(pack):
    _, N = pack.shape

    def body(in_ref, out_ref, send, recv, loc):
        x, y, c = _coords()
        me = 4 * x + 2 * y + c
        own = pltpu.make_async_copy(in_ref, out_ref.at[me], loc)
        own.start()
        cps = []
        for k in range(1, 8):
            dev = (x ^ (k >> 2), y ^ ((k >> 1) & 1), c ^ (k & 1))
            cps.append(_remote(in_ref, out_ref.at[me], send, recv, k - 1, dev))
        for cp in cps:
            cp.start()
        for k in range(1, 8):
            src = 4 * (x ^ (k >> 2)) + 2 * (y ^ ((k >> 1) & 1)) + (c ^ (k & 1))
            _remote(in_ref, out_ref.at[src], send, recv, k - 1, (x, y, c)).wait_recv()
        for cp in cps:
            cp.wait_send()
        own.wait()

    return pl.pallas_call(
        body, name="gather_small", out_shape=jax.ShapeDtypeStruct((8, 1, N), pack.dtype),
        in_specs=[ANY], out_specs=ANY,
        scratch_shapes=[pltpu.SemaphoreType.DMA((7,)), pltpu.SemaphoreType.DMA((7,)), pltpu.SemaphoreType.DMA(())],
    )(pack)


def _sum_small(slots):
    K, _, N = slots.shape

    def body(s_ref, o_ref):
        acc = s_ref[0]
        for k in range(1, K):
            acc = acc + s_ref[k]
        o_ref[...] = acc

    return _call(body, name="sum_small", in_specs=[pl.BlockSpec(memory_space=pltpu.VMEM)],
                 out_specs=pl.BlockSpec(memory_space=pltpu.VMEM), out_shape=jax.ShapeDtypeStruct((1, N), F32))(slots)


def _adamw(w, g, m, v, name):
    R, C = w.shape
    tr, tc = _tile(R, 256, 8), _tile(C, 2176)

    def body(w_ref, g_ref, m_ref, v_ref, d_ref, nm_ref, nv_ref):
        gv = g_ref[...]
        nm = ADAM_B1 * m_ref[...] + (1.0 - ADAM_B1) * gv
        nv = ADAM_B2 * v_ref[...] + (1.0 - ADAM_B2) * gv * gv
        m_hat = nm / (1.0 - ADAM_B1 ** ADAM_STEP)
        v_hat = nv / (1.0 - ADAM_B2 ** ADAM_STEP)
        d_ref[...] = -ADAM_LR * (m_hat / (jnp.sqrt(v_hat) + ADAM_EPS) + ADAM_WD * w_ref[...])
        nm_ref[...] = nm
        nv_ref[...] = nv

    spec = pl.BlockSpec((tr, tc), lambda i, j: (i, j))
    shp = jax.ShapeDtypeStruct((R, C), F32)
    return _call(body, name=name, grid=(R // tr, C // tc), in_specs=[spec] * 4, out_specs=[spec] * 3,
                 out_shape=[shp] * 3)(w, g, m, v)


SMALL = ("norm_g", "mem_norm_g", "attn_q_norm", "attn_k_norm", "conv_w", "mem_q_norm", "mem_k_norm")
WEIGHTS = ("norm_g", "mem_norm_g", "w_in", "attn_q_norm", "attn_k_norm", "conv_w", "mem_w_kv", "mem_q_norm",
           "mem_k_norm", "w_br_attn", "w_br_conv", "w_br_mem", "w_out")


def kernel(x, mem, norm_g, mem_norm_g, w_in, attn_q_norm, attn_k_norm, conv_w, mem_w_kv, mem_q_norm, mem_k_norm, w_br_attn, w_br_conv, w_br_mem, w_out, loss_target, m_norm_g, m_mem_norm_g, m_w_in, m_attn_q_norm, m_attn_k_norm, m_conv_w, m_mem_w_kv, m_mem_q_norm, m_mem_k_norm, m_w_br_attn, m_w_br_conv, m_w_br_mem, m_w_out, v_norm_g, v_mem_norm_g, v_w_in, v_attn_q_norm, v_attn_k_norm, v_conv_w, v_mem_w_kv, v_mem_q_norm, v_mem_k_norm, v_w_br_attn, v_w_br_conv, v_w_br_mem, v_w_out):
    w = dict(norm_g=norm_g, mem_norm_g=mem_norm_g, w_in=w_in, attn_q_norm=attn_q_norm, attn_k_norm=attn_k_norm,
             conv_w=conv_w, mem_w_kv=mem_w_kv, mem_q_norm=mem_q_norm, mem_k_norm=mem_k_norm, w_br_attn=w_br_attn,
             w_br_conv=w_br_conv, w_br_mem=w_br_mem, w_out=w_out)
    m = dict(norm_g=m_norm_g, mem_norm_g=m_mem_norm_g, w_in=m_w_in, attn_q_norm=m_attn_q_norm,
             attn_k_norm=m_attn_k_norm, conv_w=m_conv_w, mem_w_kv=m_mem_w_kv, mem_q_norm=m_mem_q_norm,
             mem_k_norm=m_mem_k_norm, w_br_attn=m_w_br_attn, w_br_conv=m_w_br_conv, w_br_mem=m_w_br_mem, w_out=m_w_out)
    v = dict(norm_g=v_norm_g, mem_norm_g=v_mem_norm_g, w_in=v_w_in, attn_q_norm=v_attn_q_norm,
             attn_k_norm=v_attn_k_norm, conv_w=v_conv_w, mem_w_kv=v_mem_w_kv, mem_q_norm=v_mem_q_norm,
             mem_k_norm=v_mem_k_norm, w_br_attn=v_w_br_attn, w_br_conv=v_w_br_conv, w_br_mem=v_w_br_mem, w_out=v_w_out)
    D = x.shape[-1]
    chip = 2 * lax.axis_index("x") + lax.axis_index("y")
    pos = jnp.stack([chip, lax.axis_index("c")]).astype(jnp.int32)

    fulls = [_place_shard(w[name], kind, pos, WIRE_DTYPE, "place_" + name) for name, kind in BIG]
    fulls.append(_place_shard(conv_w, "col", pos, F32, "place_conv_w"))
    gathered = _gather_weights(fulls)
    W = {name: gathered[p] for p, (name, _) in enumerate(BIG)}
    conv_full = gathered[len(BIG)]

    G, rest = _weight_grads(x, mem, loss_target, norm_g, mem_norm_g, attn_q_norm, attn_k_norm, conv_full,
                            mem_q_norm, mem_k_norm, W)

    parts = [G[name] for name, _ in BIG]
    got = _sibling_exchange(parts)
    pres = [_presum(parts[p], got[p], kind, pos, "presum_" + name) for p, (name, kind) in enumerate(BIG)]
    send, recv, pres, lands, dproj = _chip_exchange_start(pres, rest[0])
    grad_x, small = _input_grad((dproj, *rest[1:]), W["w_in"])
    pres, slots = _chip_exchange_wait(send, recv, pres, lands, grad_x)
    grad_x = grad_x.reshape(x.shape)
    reds = [_reduce_into_shard(slots[p], pres[p], kind, pos, "reduce_" + name) for p, (name, kind) in enumerate(BIG)]
    grads = dict(zip([name for name, _ in BIG], _share_reduced(reds)))

    tot = _sum_small(_gather_small(small))[0]
    loss = tot[0]
    off = 128
    for name, size in (("norm_g", D), ("mem_norm_g", D), ("attn_q_norm", NGROUP * HEAD), ("attn_k_norm", NGROUP * HEAD),
                       ("conv_w", 3 * CONVW), ("mem_q_norm", MEM_HD), ("mem_k_norm", MEM_HD)):
        grads[name] = tot[off:off + size]
        off += size
    cw = conv_w.shape[1]
    grads["conv_w"] = lax.dynamic_slice(grads["conv_w"].reshape(3, CONVW), (0, chip * cw), (3, cw))
    for name in SMALL:
        grads[name] = grads[name].reshape(w[name].shape)

    delta, new_m, new_v = {}, {}, {}
    for name, _ in BIG:
        delta[name], new_m[name], new_v[name] = _adamw(w[name], grads[name], m[name], v[name], "adamw_" + name)

    def packed(t):
        return jnp.concatenate([t[name].reshape(1, -1) for name in SMALL], axis=1)

    ds, ms, vs = _adamw(packed(w), packed(grads), packed(m), packed(v), "adamw_small")
    off = 0
    for name in SMALL:
        size = w[name].size
        delta[name] = ds[0, off:off + size].reshape(w[name].shape)
        new_m[name] = ms[0, off:off + size].reshape(w[name].shape)
        new_v[name] = vs[0, off:off + size].reshape(w[name].shape)
        off += size

    return (loss, grad_x, *[grads[n] for n in WEIGHTS], *[delta[n] for n in WEIGHTS],
            *[new_m[n] for n in WEIGHTS], *[new_v[n] for n in WEIGHTS])
```

```python
import functools

import jax
import jax.numpy as jnp
from jax import lax
from jax.experimental import pallas as pl
from jax.experimental.pallas import tpu as pltpu

F32 = jnp.float32
MXU_DTYPE = jnp.bfloat16
WIRE_DTYPE = jnp.bfloat16
EPS = 1e-6
NEG = -1e30

HEAD = 128
HPG = 4
GW = HPG * HEAD
DILATIONS = (1, 4, 16)
NGROUP = len(DILATIONS)
BLK = 128
QKV = NGROUP * GW
CONVW = 1024
MEM_HEADS = 4
MEM_HD = 256
MEMW = MEM_HEADS * MEM_HD
Q0, K0, V0 = 0, QKV, 2 * QKV
ZA = 3 * QKV
CB, CC, CV, ZC = ZA + GW, ZA + GW + CONVW, ZA + GW + 2 * CONVW, ZA + GW + 3 * CONVW
MQ = ZC + CONVW
ZM = MQ + MEMW
G0 = ZM + MEMW

ADAM_LR, ADAM_B1, ADAM_B2, ADAM_EPS, ADAM_WD, ADAM_STEP = 0.001, 0.9, 0.999, 1e-08, 0.01, 10

VMEM_LIMIT = 56 * 1024 * 1024
MESH = pl.DeviceIdType.MESH
ANY = pl.BlockSpec(memory_space=pl.ANY)


def _tile(n, pref, mult=128):
    t = min(pref, n)
    while t > mult and (n % t or t % mult):
        t -= mult
    assert n % t == 0, (n, pref)
    return t


def _call(body, *, name, out_shape, grid=(), in_specs=None, out_specs=None, scratch_shapes=(),
          aliases=None, grid_spec=None):
    kw = {}
    if grid_spec is not None:
        kw["grid_spec"] = grid_spec
        ngrid = len(grid_spec.grid)
    else:
        kw.update(grid=grid, in_specs=in_specs, out_specs=out_specs, scratch_shapes=list(scratch_shapes))
        ngrid = len(grid)
    params = pltpu.CompilerParams(dimension_semantics=("arbitrary",) * ngrid, vmem_limit_bytes=VMEM_LIMIT)
    return pl.pallas_call(body, name=name, out_shape=out_shape, compiler_params=params,
                          input_output_aliases=aliases or {}, **kw)


_DIMS = {"nn": (((1,), (0,)), ((), ())), "nt": (((1,), (1,)), ((), ())), "tn": (((0,), (0,)), ((), ()))}


def _mxu(a, b, mode):
    return lax.dot_general(a.astype(MXU_DTYPE), b.astype(MXU_DTYPE), _DIMS[mode], preferred_element_type=F32)


@functools.partial(jax.custom_vjp, nondiff_argnums=(2,))
def _dot(a, b, mode):
    return _mxu(a, b, mode)


def _dot_fwd(a, b, mode):
    return _mxu(a, b, mode), (a, b)


def _dot_bwd(mode, res, g):
    a, b = res
    if mode == "nn":
        return _mxu(g, b, "nt"), _mxu(a, g, "tn")
    if mode == "nt":
        return _mxu(g, b, "nn"), _mxu(g, a, "tn")
    return _mxu(b, g, "nt"), _mxu(a, g, "nn")


_dot.defvjp(_dot_fwd, _dot_bwd)


def _sig(z):
    return 1.0 / (1.0 + jnp.exp(-z))


def _silu(z):
    return z * _sig(z)


def _rms_rows(t, g):
    return t * lax.rsqrt(jnp.mean(t * t, axis=-1, keepdims=True) + EPS) * g


def _attn_block(q, k2, v2, gq, gk, first):
    qn = _rms_rows(q, gq)
    kn = _rms_rows(k2, gk)
    s = _dot(qn, kn, "nt") * (HEAD ** -0.5)
    a = lax.broadcasted_iota(jnp.int32, (BLK, 2 * BLK), 0)
    b = lax.broadcasted_iota(jnp.int32, (BLK, 2 * BLK), 1)
    lo = jnp.where(first, BLK, 0)
    mask = (b >= a) & (b <= a + BLK) & (b >= lo)
    s = jnp.where(mask, s, NEG)
    m = lax.stop_gradient(jnp.max(s, axis=-1, keepdims=True))
    p = jnp.exp(s - m)
    den = jnp.sum(p, axis=-1, keepdims=True)
    o = _dot(p, v2, "nn") / den
    return o, m + jnp.log(den)


def _combine(o1, o2, o3, l1, l2, l3, z):
    m = lax.stop_gradient(jnp.maximum(jnp.maximum(l1, l2), l3))
    e1, e2, e3 = jnp.exp(l1 - m), jnp.exp(l2 - m), jnp.exp(l3 - m)
    return (e1 * o1 + e2 * o2 + e3 * o3) / (e1 + e2 + e3) * _silu(z)


def _mem_block(q, z, kv, gq, gk):
    outs = []
    for h in range(MEM_HEADS):
        sl = slice(h * MEM_HD, (h + 1) * MEM_HD)
        qn = _rms_rows(q[:, sl], gq)
        kn = _rms_rows(kv[:, sl], gk)
        s = _dot(qn, kn, "nt") * (MEM_HD ** -0.5)
        m = lax.stop_gradient(jnp.max(s, axis=-1, keepdims=True))
        p = jnp.exp(s - m)
        den = jnp.sum(p, axis=-1, keepdims=True)
        outs.append(_dot(p, kv[:, MEMW + h * MEM_HD:MEMW + (h + 1) * MEM_HD], "nn") / den)
    return jnp.concatenate(outs, axis=-1) * _silu(z)


def _cast(w, name):
    R, C = w.shape
    tr, tc = _tile(R, 512, 8), _tile(C, 2176)

    def body(w_ref, o_ref):
        o_ref[...] = w_ref[...].astype(o_ref.dtype)

    spec = pl.BlockSpec((tr, tc), lambda i, j: (i, j))
    return _call(body, name=name, grid=(R // tr, C // tc), in_specs=[spec], out_specs=spec,
                 out_shape=jax.ShapeDtypeStruct((R, C), WIRE_DTYPE))(w)


def _place_shard(w, kind, pos, dtype, name, slot=0, into=None):
    R, C = w.shape
    tr, tc = _tile(R, 512, 8), _tile(C, 2176)
    nr, nc = R // tr, C // tc

    def body(pos_ref, w_ref, *rest):
        rest[-1][...] = w_ref[...].astype(rest[-1].dtype)

    if kind == "col":
        full, out = (R, 4 * C), pl.BlockSpec((tr, tc), lambda i, j, pos_ref: (i, pos_ref[slot] * nc + j))
    else:
        full, out = (4 * R, C), pl.BlockSpec((tr, tc), lambda i, j, pos_ref: (pos_ref[slot] * nr + i, j))
    in_specs, args = [pl.BlockSpec((tr, tc), lambda i, j, pos_ref: (i, j))], [pos, w]
    if into is not None:
        in_specs.append(ANY)
        args.append(into)
    spec = pltpu.PrefetchScalarGridSpec(num_scalar_prefetch=1, grid=(nr, nc), in_specs=in_specs, out_specs=out)
    return _call(body, name=name, grid_spec=spec, out_shape=jax.ShapeDtypeStruct(full, dtype),
                 aliases={} if into is None else {2: 0})(*args)


def _matmul(a, b, mode, out_dtype, *, name, tm=512, tn=512, tk=512):
    if mode == "nn":
        (M, K), (_, N) = a.shape, b.shape
    elif mode == "nt":
        (M, K), (N, _) = a.shape, b.shape
    else:
        (K, M), (_, N) = a.shape, b.shape
    tm, tn, tk = _tile(M, tm), _tile(N, tn), _tile(K, tk)
    nk = K // tk

    def body(a_ref, b_ref, o_ref, *acc):
        part = lax.dot_general(a_ref[...], b_ref[...], _DIMS[mode], preferred_element_type=F32)
        if nk == 1:
            o_ref[...] = part.astype(o_ref.dtype)
            return
        acc_ref, = acc
        k = pl.program_id(2)

        @pl.when(k == 0)
        def _():
            acc_ref[...] = part

        @pl.when(k > 0)
        def _():
            acc_ref[...] += part

        @pl.when(k == nk - 1)
        def _():
            o_ref[...] = acc_ref[...].astype(o_ref.dtype)

    a_spec = pl.BlockSpec((tk, tm), lambda i, j, k: (k, i)) if mode == "tn" else pl.BlockSpec((tm, tk), lambda i, j, k: (i, k))
    b_spec = pl.BlockSpec((tn, tk), lambda i, j, k: (j, k)) if mode == "nt" else pl.BlockSpec((tk, tn), lambda i, j, k: (k, j))
    return _call(body, name=name, grid=(M // tm, N // tn, nk), in_specs=[a_spec, b_spec],
                 out_specs=pl.BlockSpec((tm, tn), lambda i, j, k: (i, j)),
                 out_shape=jax.ShapeDtypeStruct((M, N), out_dtype),
                 scratch_shapes=[] if nk == 1 else [pltpu.VMEM((tm, tn), F32)])(a, b)


def _rms_fwd(x, g, name):
    R, D = x.shape
    tr = _tile(R, 512)

    def body(x_ref, g_ref, o_ref, t_ref):
        y = _rms_rows(x_ref[...], g_ref[...])
        o_ref[...] = y.astype(o_ref.dtype)
        t_ref[...] = y.T.astype(t_ref.dtype)

    row = pl.BlockSpec((tr, D), lambda i: (i, 0))
    return _call(body, name=name, grid=(R // tr,), in_specs=[row, pl.BlockSpec((1, D), lambda i: (0, 0))],
                 out_specs=[row, pl.BlockSpec((D, tr), lambda i: (0, i))],
                 out_shape=[jax.ShapeDtypeStruct((R, D), MXU_DTYPE), jax.ShapeDtypeStruct((D, R), MXU_DTYPE)])(x, g)


def _rms_bwd(x, dh, g, dy, name):
    R, D = x.shape
    tr = _tile(R, 256)
    with_dx = dy is not None

    def body(*refs):
        if with_dx:
            x_ref, dh_ref, g_ref, dy_ref, dx_ref, dg_ref = refs
        else:
            x_ref, dh_ref, g_ref, dg_ref = refs
        xv, dhv = x_ref[...], dh_ref[...]
        r = lax.rsqrt(jnp.mean(xv * xv, axis=-1, keepdims=True) + EPS)
        xh = xv * r

        @pl.when(pl.program_id(0) == 0)
        def _():
            dg_ref[...] = jnp.zeros_like(dg_ref)

        dg_ref[...] += jnp.sum(dhv * xh, axis=0, keepdims=True)
        if with_dx:
            dxh = dhv * g_ref[...]
            dx_ref[...] = dy_ref[...] + r * (dxh - xh * jnp.mean(dxh * xh, axis=-1, keepdims=True))

    row = pl.BlockSpec((tr, D), lambda i: (i, 0))
    vec = pl.BlockSpec((1, D), lambda i: (0, 0))
    dg_shape = jax.ShapeDtypeStruct((1, D), F32)
    if with_dx:
        return _call(body, name=name, grid=(R // tr,), in_specs=[row, row, vec, row], out_specs=[row, vec],
                     out_shape=[jax.ShapeDtypeStruct((R, D), F32), dg_shape])(x, dh, g, dy)
    return None, _call(body, name=name, grid=(R // tr,), in_specs=[row, row, vec], out_specs=vec,
                       out_shape=dg_shape)(x, dh, g)


def _attn_geom(g, d):
    hc = HPG if d == 1 else 1
    cw = hc * HEAD
    cq, ck, cv = (Q0 + g * GW) // cw, (K0 + g * GW) // cw, (V0 + g * GW) // cw
    return (1, BLK * d, cw), hc, HPG // hc, cq, ck, cv


def _rows(ref, r, d, sl):
    if d == 1:
        return ref[0, :, sl]
    return ref.at[0][pl.ds(r, BLK, stride=d), sl]


def _set_rows(ref, r, d, sl, val):
    if d == 1:
        ref[0, :, sl] = val
    else:
        ref.at[0][pl.ds(r, BLK, stride=d), sl] = val


def _attn_fwd(proj3, gq, gk, g, d):
    Bl, S, _ = proj3.shape
    blk, hc, ncb, cq, ck, cv = _attn_geom(g, d)
    nb = S // blk[1]

    def body(q_ref, kp_ref, kc_ref, vp_ref, vc_ref, gq_ref, gk_ref, o_ref, lse_ref):
        first = pl.program_id(2) == 0
        for r in range(d):
            for h in range(hc):
                sl = slice(h * HEAD, (h + 1) * HEAD)
                k2 = jnp.concatenate([_rows(kp_ref, r, d, sl), _rows(kc_ref, r, d, sl)], axis=0)
                v2 = jnp.concatenate([_rows(vp_ref, r, d, sl), _rows(vc_ref, r, d, sl)], axis=0)
                o, lse = _attn_block(_rows(q_ref, r, d, sl), k2, v2, gq_ref[...], gk_ref[...], first)
                _set_rows(o_ref, r, d, sl, o)
                _set_rows(lse_ref, r, d, sl, jnp.broadcast_to(lse, (BLK, HEAD)))

    def cur(c0):
        return pl.BlockSpec(blk, lambda b, j, i: (b, i, c0 + j))

    def prev(c0):
        return pl.BlockSpec(blk, lambda b, j, i: (b, jnp.maximum(i - 1, 0), c0 + j))

    vec = pl.BlockSpec((1, HEAD), lambda b, j, i: (0, 0))
    out = pl.BlockSpec(blk, lambda b, j, i: (b, i, j))
    shp = jax.ShapeDtypeStruct((Bl, S, GW), F32)
    return _call(body, name=f"attn_fwd_g{g}", grid=(Bl, ncb, nb),
                 in_specs=[cur(cq), prev(ck), cur(ck), prev(cv), cur(cv), vec, vec],
                 out_specs=[out, out], out_shape=[shp, shp])(proj3, proj3, proj3, proj3, proj3, gq, gk)


def _attn_bwd(proj3, gq, gk, do3, dl3, g, d):
    Bl, S, _ = proj3.shape
    blk, hc, ncb, cq, ck, cv = _attn_geom(g, d)
    nb = S // blk[1]

    def body(q_ref, kp_ref, kc_ref, vp_ref, vc_ref, gq_ref, gk_ref, do_ref, dl_ref,
             dq_ref, dk_ref, dv_ref, dgq_ref, dgk_ref, ck_ref, cv_ref):
        i = pl.program_id(2)
        first = i == 0

        @pl.when((pl.program_id(0) == 0) & (pl.program_id(1) == 0) & first)
        def _():
            dgq_ref[...] = jnp.zeros_like(dgq_ref)
            dgk_ref[...] = jnp.zeros_like(dgk_ref)

        @pl.when(i < nb)
        def _():
            for r in range(d):
                rs = slice(r * BLK, (r + 1) * BLK)
                for h in range(hc):
                    sl = slice(h * HEAD, (h + 1) * HEAD)
                    k2 = jnp.concatenate([_rows(kp_ref, r, d, sl), _rows(kc_ref, r, d, sl)], axis=0)
                    v2 = jnp.concatenate([_rows(vp_ref, r, d, sl), _rows(vc_ref, r, d, sl)], axis=0)
                    _, vjp = jax.vjp(lambda q, k, v, a, b: _attn_block(q, k, v, a, b, first),
                                     _rows(q_ref, r, d, sl), k2, v2, gq_ref[...], gk_ref[...])
                    dq, dk2, dv2, dgq, dgk = vjp((_rows(do_ref, r, d, sl), _rows(dl_ref, r, d, sl)[:, :1]))
                    _set_rows(dq_ref, r, d, sl, dq)
                    dgq_ref[...] += dgq
                    dgk_ref[...] += dgk

                    @pl.when(i > 0)
                    def _():
                        _set_rows(dk_ref, r, d, sl, ck_ref[rs, sl] + dk2[:BLK])
                        _set_rows(dv_ref, r, d, sl, cv_ref[rs, sl] + dv2[:BLK])

                    ck_ref[rs, sl] = dk2[BLK:]
                    cv_ref[rs, sl] = dv2[BLK:]

        @pl.when(i == nb)
        def _():
            for r in range(d):
                rs = slice(r * BLK, (r + 1) * BLK)
                _set_rows(dk_ref, r, d, slice(None), ck_ref[rs, :])
                _set_rows(dv_ref, r, d, slice(None), cv_ref[rs, :])

    def cur(c0):
        return pl.BlockSpec(blk, lambda b, j, i: (b, jnp.minimum(i, nb - 1), c0 + j))

    def prev(c0):
        return pl.BlockSpec(blk, lambda b, j, i: (b, jnp.clip(i - 1, 0, nb - 1), c0 + j))

    vec = pl.BlockSpec((1, HEAD), lambda b, j, i: (0, 0))
    at_q = pl.BlockSpec(blk, lambda b, j, i: (b, jnp.minimum(i, nb - 1), j))
    at_k = pl.BlockSpec(blk, lambda b, j, i: (b, jnp.maximum(i - 1, 0), j))
    shp = jax.ShapeDtypeStruct((Bl, S, GW), F32)
    gshp = jax.ShapeDtypeStruct((1, HEAD), F32)
    return _call(body, name=f"attn_bwd_g{g}", grid=(Bl, ncb, nb + 1),
                 in_specs=[cur(cq), prev(ck), cur(ck), prev(cv), cur(cv), vec, vec, at_q, at_q],
                 out_specs=[at_q, at_k, at_k, vec, vec], out_shape=[shp, shp, shp, gshp, gshp],
                 scratch_shapes=[pltpu.VMEM(blk[1:], F32), pltpu.VMEM(blk[1:], F32)],
                 )(proj3, proj3, proj3, proj3, proj3, gq, gk, do3, dl3)


def _combine_fwd(os, ls, proj2):
    T = proj2.shape[0]
    tr = _tile(T, 512)

    def body(o1, o2, o3, l1, l2, l3, z, a_ref):
        a_ref[...] = _combine(o1[...], o2[...], o3[...], l1[...], l2[...], l3[...], z[...]).astype(a_ref.dtype)

    row = pl.BlockSpec((tr, GW), lambda i: (i, 0))
    return _call(body, name="combine_fwd", grid=(T // tr,),
                 in_specs=[row] * 6 + [pl.BlockSpec((tr, GW), lambda i: (i, ZA // GW))], out_specs=row,
                 out_shape=jax.ShapeDtypeStruct((T, GW), MXU_DTYPE))(*os, *ls, proj2)


def _combine_bwd(os, ls, proj2, da):
    T = proj2.shape[0]
    tr = _tile(T, 256)

    def body(o1, o2, o3, l1, l2, l3, z, da_ref, d1, d2, d3, e1, e2, e3, dz_ref):
        _, vjp = jax.vjp(_combine, o1[...], o2[...], o3[...], l1[...], l2[...], l3[...], z[...])
        go1, go2, go3, gl1, gl2, gl3, gz = vjp(da_ref[...])
        d1[...], d2[...], d3[...] = go1, go2, go3
        dz_ref[...] = gz.astype(dz_ref.dtype)
        for ref, gl in ((e1, gl1), (e2, gl2), (e3, gl3)):
            for h in range(HPG):
                sl = slice(h * HEAD, (h + 1) * HEAD)
                ref[:, sl] = jnp.broadcast_to(jnp.sum(gl[:, sl], axis=-1, keepdims=True), (tr, HEAD))

    row = pl.BlockSpec((tr, GW), lambda i: (i, 0))
    f = jax.ShapeDtypeStruct((T, GW), F32)
    outs = _call(body, name="combine_bwd", grid=(T // tr,),
                 in_specs=[row] * 6 + [pl.BlockSpec((tr, GW), lambda i: (i, ZA // GW)), row],
                 out_specs=[row] * 7, out_shape=[f] * 6 + [jax.ShapeDtypeStruct((T, GW), MXU_DTYPE)],
                 )(*os, *ls, proj2, da)
    return outs[:3], outs[3:6], outs[6]


def _shift_down(u, j, t):
    return jnp.where(t >= j, pltpu.roll(u, j, 0), 0.0)


def _shift_up(u, j, t):
    n = u.shape[0]
    return jnp.where(t < n - j, pltpu.roll(u, n - j, 0), 0.0)


def _conv_specs(Bl, S, cw):
    def sec(c0):
        return pl.BlockSpec((1, S, cw), lambda j, b: (b, 0, c0 // cw + j))
    return [sec(CB), sec(CC), sec(CV), sec(ZC)], pl.BlockSpec((3, cw), lambda j, b: (0, j))


def _conv_fwd(proj3, conv_w):
    Bl, S, _ = proj3.shape
    cw = 256
    secs, wspec = _conv_specs(Bl, S, cw)

    def body(b_ref, c_ref, v_ref, z_ref, w_ref, o_ref):
        t = lax.broadcasted_iota(jnp.int32, (S, cw), 0)
        u = c_ref[0] * v_ref[0]
        y = w_ref[0:1, :] * u + w_ref[1:2, :] * _shift_down(u, 1, t) + w_ref[2:3, :] * _shift_down(u, 2, t)
        o_ref[0] = (b_ref[0] * y * _silu(z_ref[0])).astype(o_ref.dtype)

    return _call(body, name="conv_fwd", grid=(CONVW // cw, Bl), in_specs=secs + [wspec],
                 out_specs=pl.BlockSpec((1, S, cw), lambda j, b: (b, 0, j)),
                 out_shape=jax.ShapeDtypeStruct((Bl, S, CONVW), MXU_DTYPE))(proj3, proj3, proj3, proj3, conv_w)


def _conv_bwd(proj3, conv_w, dcc3):
    Bl, S, _ = proj3.shape
    cw = 256
    secs, wspec = _conv_specs(Bl, S, cw)

    def body(b_ref, c_ref, v_ref, z_ref, w_ref, d_ref, db_ref, dc_ref, dv_ref, dz_ref, dw_ref):
        t = lax.broadcasted_iota(jnp.int32, (S, cw), 0)
        bv, cv, vv, zv, dv = b_ref[0], c_ref[0], v_ref[0], z_ref[0], d_ref[0]
        u = cv * vv
        u1, u2 = _shift_down(u, 1, t), _shift_down(u, 2, t)
        y = w_ref[0:1, :] * u + w_ref[1:2, :] * u1 + w_ref[2:3, :] * u2
        sg = _sig(zv)
        sz = zv * sg
        gy = dv * bv * sz
        db_ref[0] = (dv * y * sz).astype(db_ref.dtype)
        dz_ref[0] = (dv * bv * y * sg * (1.0 + zv * (1.0 - sg))).astype(dz_ref.dtype)
        du = w_ref[0:1, :] * gy + w_ref[1:2, :] * _shift_up(gy, 1, t) + w_ref[2:3, :] * _shift_up(gy, 2, t)
        dc_ref[0] = (du * vv).astype(dc_ref.dtype)
        dv_ref[0] = (du * cv).astype(dv_ref.dtype)

        @pl.when(pl.program_id(1) == 0)
        def _():
            dw_ref[...] = jnp.zeros_like(dw_ref)

        dw_ref[0:1, :] += jnp.sum(gy * u, axis=0, keepdims=True)
        dw_ref[1:2, :] += jnp.sum(gy * u1, axis=0, keepdims=True)
        dw_ref[2:3, :] += jnp.sum(gy * u2, axis=0, keepdims=True)

    blk = pl.BlockSpec((1, S, cw), lambda j, b: (b, 0, j))
    shp = jax.ShapeDtypeStruct((Bl, S, CONVW), MXU_DTYPE)
    return _call(body, name="conv_bwd", grid=(CONVW // cw, Bl), in_specs=secs + [wspec, blk],
                 out_specs=[blk] * 4 + [wspec], out_shape=[shp] * 4 + [jax.ShapeDtypeStruct((3, CONVW), F32)],
                 )(proj3, proj3, proj3, proj3, conv_w, dcc3)


def _mem_specs(S, tq):
    q = pl.BlockSpec((1, tq, MEMW), lambda b, j: (b, j, MQ // MEMW))
    z = pl.BlockSpec((1, tq, MEMW), lambda b, j: (b, j, ZM // MEMW))
    kv = pl.BlockSpec((1, MEM_HD, 2 * MEMW), lambda b, j: (b, 0, 0))
    vec = pl.BlockSpec((1, MEM_HD), lambda b, j: (0, 0))
    blk = pl.BlockSpec((1, tq, MEMW), lambda b, j: (b, j, 0))
    return q, z, kv, vec, blk


def _mem_fwd(proj3, mkv3, gq, gk):
    Bl, S, _ = proj3.shape
    tq = _tile(S, 512)
    q, z, kv, vec, blk = _mem_specs(S, tq)

    def body(q_ref, z_ref, kv_ref, gq_ref, gk_ref, o_ref):
        o_ref[0] = _mem_block(q_ref[0], z_ref[0], kv_ref[0], gq_ref[...], gk_ref[...]).astype(o_ref.dtype)

    return _call(body, name="mem_fwd", grid=(Bl, S // tq), in_specs=[q, z, kv, vec, vec], out_specs=blk,
                 out_shape=jax.ShapeDtypeStruct((Bl, S, MEMW), MXU_DTYPE))(proj3, proj3, mkv3, gq, gk)


def _mem_bwd(proj3, mkv3, gq, gk, dmo3):
    Bl, S, _ = proj3.shape
    tq = _tile(S, 256)
    q, z, kv, vec, blk = _mem_specs(S, tq)

    def body(q_ref, z_ref, kv_ref, gq_ref, gk_ref, d_ref, dq_ref, dz_ref, dkv_ref, dgq_ref, dgk_ref):
        _, vjp = jax.vjp(_mem_block, q_ref[0], z_ref[0], kv_ref[0], gq_ref[...], gk_ref[...])
        dq, dz, dkv, dgq, dgk = vjp(d_ref[0])
        dq_ref[0] = dq.astype(dq_ref.dtype)
        dz_ref[0] = dz.astype(dz_ref.dtype)
        j = pl.program_id(1)

        @pl.when(j == 0)
        def _():
            dkv_ref[0] = jnp.zeros_like(dkv)

        @pl.when((j == 0) & (pl.program_id(0) == 0))
        def _():
            dgq_ref[...] = jnp.zeros_like(dgq_ref)
            dgk_ref[...] = jnp.zeros_like(dgk_ref)

        dkv_ref[0] += dkv
        dgq_ref[...] += dgq
        dgk_ref[...] += dgk

    shp = jax.ShapeDtypeStruct((Bl, S, MEMW), MXU_DTYPE)
    gshp = jax.ShapeDtypeStruct((1, MEM_HD), F32)
    return _call(body, name="mem_bwd", grid=(Bl, S // tq), in_specs=[q, z, kv, vec, vec, blk],
                 out_specs=[blk, blk, kv, vec, vec],
                 out_shape=[shp, shp, jax.ShapeDtypeStruct(mkv3.shape, F32), gshp, gshp],
                 )(proj3, proj3, mkv3, gq, gk, dmo3)


def _merge_specs(T, D, tm, tn):
    def act(w):
        return pl.BlockSpec((tm, w), lambda i, n: (i, 0))

    def wsp(w):
        return pl.BlockSpec((w, tn), lambda i, n: (0, n))

    gates = [pl.BlockSpec((tm, tn), lambda i, n, k=k: (i, (G0 + k * D) // tn + n)) for k in range(3)]
    tile = pl.BlockSpec((tm, tn), lambda i, n: (i, n))
    return act, wsp, gates, tile


def _merge_fwd(a, cc, mo, wa, wc, wm, proj2):
    T, D = a.shape[0], wa.shape[1]
    tm, tn = _tile(T, 512), _tile(D, 512)
    act, wsp, gates, tile = _merge_specs(T, D, tm, tn)

    def body(a_ref, c_ref, m_ref, wa_ref, wc_ref, wm_ref, g0, g1, g2, mg_ref, mt_ref, pa_ref, pc_ref, pm_ref):
        pa = jnp.dot(a_ref[...], wa_ref[...], preferred_element_type=F32)
        pc = jnp.dot(c_ref[...], wc_ref[...], preferred_element_type=F32)
        pm = jnp.dot(m_ref[...], wm_ref[...], preferred_element_type=F32)
        mg = _sig(g0[...]) * pa + _sig(g1[...]) * pc + _sig(g2[...]) * pm
        mg_ref[...] = mg.astype(mg_ref.dtype)
        mt_ref[...] = mg.T.astype(mt_ref.dtype)
        pa_ref[...] = pa.astype(pa_ref.dtype)
        pc_ref[...] = pc.astype(pc_ref.dtype)
        pm_ref[...] = pm.astype(pm_ref.dtype)

    shp = jax.ShapeDtypeStruct((T, D), MXU_DTYPE)
    return _call(body, name="merge_fwd", grid=(T // tm, D // tn),
                 in_specs=[act(GW), act(CONVW), act(MEMW), wsp(GW), wsp(CONVW), wsp(MEMW)] + gates,
                 out_specs=[tile, pl.BlockSpec((tn, tm), lambda i, n: (n, i)), tile, tile, tile],
                 out_shape=[shp, jax.ShapeDtypeStruct((D, T), MXU_DTYPE), shp, shp, shp],
                 )(a, cc, mo, wa, wc, wm, proj2, proj2, proj2)


def _merge_bwd(dyb, w_out, proj2, pa, pc, pm):
    T, D = dyb.shape
    tm, tn = _tile(T, 512), _tile(D, 512)
    _, _, gates, tile = _merge_specs(T, D, tm, tn)

    def body(dy_ref, w_ref, g0, g1, g2, p0, p1, p2, dp0, dp1, dp2, dg0, dg1, dg2):
        dm = lax.dot_general(dy_ref[...], w_ref[...], _DIMS["nt"], preferred_element_type=F32)
        for g_ref, p_ref, dp_ref, dg_ref in ((g0, p0, dp0, dg0), (g1, p1, dp1, dg1), (g2, p2, dp2, dg2)):
            gt = _sig(g_ref[...])
            dp_ref[...] = (gt * dm).astype(dp_ref.dtype)
            dg_ref[...] = (dm * p_ref[...].astype(F32) * gt * (1.0 - gt)).astype(dg_ref.dtype)

    shp = jax.ShapeDtypeStruct((T, D), MXU_DTYPE)
    return _call(body, name="merge_bwd", grid=(T // tm, D // tn),
                 in_specs=[pl.BlockSpec((tm, D), lambda i, n: (i, 0)), pl.BlockSpec((tn, D), lambda i, n: (n, 0))]
                 + gates + [tile] * 3,
                 out_specs=[tile] * 6, out_shape=[shp] * 6)(dyb, w_out, proj2, proj2, proj2, pa, pc, pm)


def _out_loss(merged, w_out, x, tgt):
    T, D = x.shape
    tm = _tile(T, 256)

    def body(m_ref, w_ref, x_ref, t_ref, dy_ref, dyb_ref, loss_ref):
        err = x_ref[...] + jnp.dot(m_ref[...], w_ref[...], preferred_element_type=F32) - t_ref[...]
        dy = err * (1.0 / D)
        dy_ref[...] = dy
        dyb_ref[...] = dy.astype(dyb_ref.dtype)

        @pl.when(pl.program_id(0) == 0)
        def _():
            loss_ref[...] = jnp.zeros_like(loss_ref)

        loss_ref[...] += jnp.sum(err * err) * (0.5 / D)

    row = pl.BlockSpec((tm, D), lambda i: (i, 0))
    return _call(body, name="out_loss", grid=(T // tm,),
                 in_specs=[row, pl.BlockSpec((D, D), lambda i: (0, 0)), row, row],
                 out_specs=[row, row, pl.BlockSpec((1, 128), lambda i: (0, 0))],
                 out_shape=[jax.ShapeDtypeStruct((T, D), F32), jax.ShapeDtypeStruct((T, D), MXU_DTYPE),
                            jax.ShapeDtypeStruct((1, 128), F32)])(merged, w_out, x, tgt)


def _proj_chunk(hb, w, order, j, buf, name):
    T, D = hb.shape
    Cs = w.shape[1]
    tm, tn = _tile(T, 1024), _tile(Cs, 2176)
    nj = Cs // tn

    def body(order_ref, a_ref, b_ref, *rest):
        rest[-1][...] = jnp.dot(a_ref[...], b_ref[...], preferred_element_type=F32)

    in_specs = [pl.BlockSpec((tm, D), lambda n, i, o: (i, 0)), pl.BlockSpec((D, tn), lambda n, i, o: (0, n))]
    args = [order, hb, w]
    if buf is not None:
        in_specs.append(ANY)
        args.append(buf)
    spec = pltpu.PrefetchScalarGridSpec(
        num_scalar_prefetch=1, grid=(nj, T // tm), in_specs=in_specs,
        out_specs=pl.BlockSpec((tm, tn), lambda n, i, o: (i, o[j] * nj + n)))
    return _call(body, name=name, grid_spec=spec, out_shape=jax.ShapeDtypeStruct((T, 4 * Cs), F32),
                 aliases={} if buf is None else {3: 0})(*args)


def _norms(x, mem, norm_g, mem_norm_g):
    D = x.shape[-1]
    hb, hbt = _rms_fwd(x.reshape(-1, D), norm_g.reshape(1, D), "rms_x")
    mhb, _ = _rms_fwd(mem.reshape(-1, D), mem_norm_g.reshape(1, D), "rms_mem")
    return hb, hbt, mhb


def _mix_fwd(proj2, Bl, gq_all, gk_all, conv_w):
    T, IN = proj2.shape
    proj3 = proj2.reshape(Bl, T // Bl, IN)
    os, ls = [], []
    for g, d in enumerate(DILATIONS):
        o, l = _attn_fwd(proj3, gq_all[g:g + 1], gk_all[g:g + 1], g, d)
        os.append(o.reshape(T, GW))
        ls.append(l.reshape(T, GW))
    a = _combine_fwd(os, ls, proj2)
    cc = _conv_fwd(proj3, conv_w).reshape(T, CONVW)
    return os, ls, a, cc


def _weight_grads(x, mem, tgt, norm_g, mem_norm_g, gq_all, gk_all, conv_w, mem_gq, mem_gk, W, pre):
    Bl, S, D = x.shape
    T = Bl * S
    hb, hbt, mhb, proj2, os, ls, a, cc = pre
    IN = proj2.shape[1]
    proj3 = proj2.reshape(Bl, S, IN)
    x2, tgt2 = x.reshape(T, D), tgt.reshape(T, D)
    mem2 = mem.reshape(-1, D)
    ng, mng = norm_g.reshape(1, D), mem_norm_g.reshape(1, D)
    mgq, mgk = mem_gq.reshape(1, MEM_HD), mem_gk.reshape(1, MEM_HD)
    gqs = [gq_all[g:g + 1] for g in range(NGROUP)]
    gks = [gk_all[g:g + 1] for g in range(NGROUP)]

    mkv = _matmul(mhb, W["mem_w_kv"], "nn", F32, name="mem_kv", tm=512, tn=1024, tk=D)
    mkv3 = mkv.reshape(Bl, -1, 2 * MEMW)
    mo = _mem_fwd(proj3, mkv3, mgq, mgk).reshape(T, MEMW)
    merged, mergedt, pa, pc, pm = _merge_fwd(a, cc, mo, W["w_br_attn"], W["w_br_conv"], W["w_br_mem"], proj2)
    dy, dyb, loss = _out_loss(merged, W["w_out"], x2, tgt2)

    G = {}
    G["w_out"] = _matmul(mergedt, dyb, "nn", WIRE_DTYPE, name="dw_out", tm=1024, tn=512, tk=T)
    dpa, dpc, dpm, dg0, dg1, dg2 = _merge_bwd(dyb, W["w_out"], proj2, pa, pc, pm)
    G["w_br_attn"] = _matmul(a, dpa, "tn", WIRE_DTYPE, name="dw_br_attn", tm=512, tn=1024, tk=512)
    G["w_br_conv"] = _matmul(cc, dpc, "tn", WIRE_DTYPE, name="dw_br_conv", tm=1024, tn=1024, tk=512)
    G["w_br_mem"] = _matmul(mo, dpm, "tn", WIRE_DTYPE, name="dw_br_mem", tm=1024, tn=1024, tk=512)
    da = _matmul(dpa, W["w_br_attn"], "nt", F32, name="d_attn", tm=1024, tn=512, tk=D)
    dcc = _matmul(dpc, W["w_br_conv"], "nt", F32, name="d_conv", tm=1024, tn=1024, tk=D)
    dmo = _matmul(dpm, W["w_br_mem"], "nt", F32, name="d_mem", tm=1024, tn=1024, tk=D)

    dos, dls, dza = _combine_bwd(os, ls, proj2, da)
    dqs, dks, dvs, dgq, dgk = [], [], [], [], []
    for g, d in enumerate(DILATIONS):
        dq, dk, dv, gq_g, gk_g = _attn_bwd(proj3, gqs[g], gks[g], dos[g].reshape(Bl, S, GW),
                                           dls[g].reshape(Bl, S, GW), g, d)
        dqs.append(dq.reshape(T, GW).astype(MXU_DTYPE))
        dks.append(dk.reshape(T, GW).astype(MXU_DTYPE))
        dvs.append(dv.reshape(T, GW).astype(MXU_DTYPE))
        dgq.append(gq_g)
        dgk.append(gk_g)
    dcb, dcc_, dcv, dzc, dconv_w = _conv_bwd(proj3, conv_w, dcc.reshape(Bl, S, CONVW))
    dmq, dzm, dmkv3, dmgq, dmgk = _mem_bwd(proj3, mkv3, mgq, mgk, dmo.reshape(Bl, S, MEMW))

    dmkv = _cast(dmkv3.reshape(-1, 2 * MEMW), "cast_dmkv")
    G["mem_w_kv"] = _matmul(mhb, dmkv, "tn", WIRE_DTYPE, name="dw_mem_kv", tm=1024, tn=1024, tk=512)
    dmh = _matmul(dmkv, W["mem_w_kv"], "nt", F32, name="d_memh", tm=512, tn=1024, tk=2 * MEMW)
    _, dmng = _rms_bwd(mem2, dmh, mng, None, "rms_mem_bwd")

    dproj = jnp.concatenate(dqs + dks + dvs + [dza] + [t.reshape(T, CONVW) for t in (dcb, dcc_, dcv, dzc)]
                            + [dmq.reshape(T, MEMW), dzm.reshape(T, MEMW), dg0, dg1, dg2], axis=1)
    G["w_in"] = _matmul(hbt, dproj, "nn", WIRE_DTYPE, name="dw_in", tm=1024, tn=512, tk=T)
    small = [loss, None, dmng] + dgq + dgk + [dconv_w.reshape(1, 3 * CONVW), dmgq, dmgk]
    return G, (dproj, x2, ng, dy, small)


def _input_grad(rest, w_in):
    dproj, x2, ng, dy, small = rest
    dh = _matmul(dproj, w_in, "nt", F32, name="d_h", tm=1024, tn=1024, tk=2176)
    grad_x, dng = _rms_bwd(x2, dh, ng, dy, "rms_x_bwd")
    small = [dng if t is None else t for t in small]
    return grad_x, jnp.concatenate(small, axis=1)


def _local_step(x, mem, tgt, norm_g, mem_norm_g, gq_all, gk_all, conv_w, mem_gq, mem_gk, W):
    hb, hbt, mhb = _norms(x, mem, norm_g, mem_norm_g)
    Cs = W["w_in"].shape[1] // 4
    order = jnp.arange(4, dtype=jnp.int32)
    proj2 = None
    for j in range(4):
        proj2 = _proj_chunk(hb, W["w_in"][:, j * Cs:(j + 1) * Cs], order, j, proj2, f"proj_{j}")
    pre = (hb, hbt, mhb, proj2, *_mix_fwd(proj2, x.shape[0], gq_all, gk_all, conv_w))
    G, rest = _weight_grads(x, mem, tgt, norm_g, mem_norm_g, gq_all, gk_all, conv_w, mem_gq, mem_gk, W, pre)
    grad_x, small = _input_grad(rest, W["w_in"])
    return grad_x.reshape(x.shape), G, small


BIG = (("w_in", "col"), ("mem_w_kv", "row"), ("w_br_attn", "col"), ("w_br_conv", "col"),
       ("w_br_mem", "col"), ("w_out", "row"))


def _coords():
    return lax.axis_index("x"), lax.axis_index("y"), lax.axis_index("c")


def _other_chips(x, y):
    return [(1 - x, y), (x, 1 - y), (1 - x, 1 - y)]


def _half(ref, kind, c):
    R, C = ref.shape
    if kind == "col":
        return ref.at[pl.ds(c * (R // 2), R // 2), :]
    return ref.at[:, pl.ds(c * (C // 2), C // 2)]


def _shard(ref, kind, s):
    R, C = ref.shape
    if kind == "col":
        return ref.at[:, pl.ds(s * (C // 4), C // 4)]
    return ref.at[pl.ds(s * (R // 4), R // 4), :]


def _piece(ref, kind, s, c):
    R, C = ref.shape
    if kind == "col":
        return ref.at[pl.ds(c * (R // 2), R // 2), pl.ds(s * (C // 4), C // 4)]
    return ref.at[pl.ds(s * (R // 4), R // 4), pl.ds(c * (C // 2), C // 2)]


def _remote(src, dst, sems_s, sems_r, k, dev):
    return pltpu.make_async_remote_copy(src_ref=src, dst_ref=dst, send_sem=sems_s.at[k], recv_sem=sems_r.at[k],
                                        device_id=dev, device_id_type=MESH)


HBM = pl.BlockSpec(memory_space=pltpu.HBM)
SEM = pl.BlockSpec(memory_space=pltpu.SEMAPHORE)
EFFECT = pltpu.SideEffectType.DATAFLOW_SIDE_EFFECTING


def _hbm(a):
    return pltpu.with_memory_space_constraint(a, pltpu.HBM)


def _start_copies(name, arrays, ncopies, make):
    n = len(arrays)

    def body(*refs):
        for cp in make(refs[:n], refs[n], refs[n + 1]):
            cp.start()

    outs = pl.pallas_call(
        body, name=name,
        out_shape=(pltpu.SemaphoreType.DMA((ncopies,)), pltpu.SemaphoreType.DMA((ncopies,)),
                   *[jax.ShapeDtypeStruct(t.shape, t.dtype) for t in arrays]),
        in_specs=[HBM] * n, out_specs=(SEM, SEM, *([HBM] * n)),
        input_output_aliases={i: i + 2 for i in range(n)},
        compiler_params=pltpu.CompilerParams(has_side_effects=EFFECT),
    )(*[_hbm(t) for t in arrays])
    return outs[0], outs[1], list(outs[2:])


def _wait_copies(name, send, recv, arrays, make, after):
    n = len(arrays)

    def body(*refs):
        for cp in make(refs[:n], refs[n], refs[n + 1]):
            cp.wait_send()
            cp.wait_recv()

    outs = pl.pallas_call(
        body, name=name, out_shape=[jax.ShapeDtypeStruct(t.shape, t.dtype) for t in arrays],
        in_specs=[HBM] * n + [SEM, SEM, ANY], out_specs=[HBM] * n,
        input_output_aliases={i: i for i in range(n)},
        compiler_params=pltpu.CompilerParams(has_side_effects=EFFECT),
    )(*arrays, send, recv, after)
    return list(outs)


def _w_in_copies(relations):
    def make(refs, send, recv):
        x, y, c = _coords()
        me = 2 * x + y
        chips = _other_chips(x, y)
        own, conv = refs[0], refs[1 + len(relations)]
        cps = []
        for i, k in enumerate(relations):
            cps.append(_remote(_half(own, "col", c), _half(refs[1 + i], "col", c), send, recv, 2 * i, (*chips[k], c)))
            mine = _shard(conv, "col", me)
            cps.append(_remote(mine, mine, send, recv, 2 * i + 1, (*chips[k], c)))
        return cps
    return make


def _other_weight_copies(refs, send, recv):
    x, y, c = _coords()
    me = 2 * x + y
    cps = []
    for k, chip in enumerate(_other_chips(x, y)):
        for p, (_, kind) in enumerate(BIG[1:]):
            mine = _piece(refs[p], kind, me, c)
            cps.append(_remote(mine, mine, send, recv, 3 * p + k, (*chip, c)))
    return cps


def _sibling_forward(name, arrays, ncp, halves):
    n = len(arrays)

    def body(*refs):
        outs = refs[n:2 * n]
        send, recv = refs[2 * n:]
        x, y, c = _coords()
        sib = (x, y, 1 - c)
        cps = [_remote(got, got, send, recv, i, sib) for i, got in enumerate(halves(outs, c))]
        for cp in cps:
            cp.start()
        for cp in cps:
            cp.wait_send()
        for i, got in enumerate(halves(outs, 1 - c)):
            _remote(got, got, send, recv, i, sib).wait_recv()

    return pl.pallas_call(
        body, name=name, out_shape=[jax.ShapeDtypeStruct(t.shape, t.dtype) for t in arrays],
        in_specs=[ANY] * n, out_specs=[ANY] * n, input_output_aliases={i: i for i in range(n)},
        scratch_shapes=[pltpu.SemaphoreType.DMA((ncp,)), pltpu.SemaphoreType.DMA((ncp,))],
    )(*arrays)


def _landed_halves(refs, c):
    return [_half(r, "col", c) for r in refs]


def _other_weight_halves(refs, c):
    x, y, _ = _coords()
    out = []
    for chip in _other_chips(x, y):
        s = 2 * chip[0] + chip[1]
        out += [_piece(refs[p], kind, s, c) for p, (_, kind) in enumerate(BIG[1:])]
    return out


def _sibling_exchange(grads):
    n = len(BIG)
    shapes = []
    for (name, kind), g in zip(BIG, grads):
        R, C = g.shape
        shapes.append(jax.ShapeDtypeStruct((R // 2, C) if kind == "col" else (R, C // 2), g.dtype))

    def body(*refs):
        ins, outs = refs[:n], refs[n:2 * n]
        send, recv = refs[2 * n:]
        x, y, c = _coords()
        sib = (x, y, 1 - c)
        cps = [_remote(_half(ins[p], BIG[p][1], 1 - c), outs[p], send, recv, p, sib) for p in range(n)]
        for cp in cps:
            cp.start()
        for cp in cps:
            cp.wait()

    return pl.pallas_call(
        body, name="sibling_exchange", out_shape=shapes, in_specs=[ANY] * n, out_specs=[ANY] * n,
        scratch_shapes=[pltpu.SemaphoreType.DMA((n,)), pltpu.SemaphoreType.DMA((n,))],
    )(*grads)


def _presum(g, got, kind, pos, name):
    R, C = got.shape
    tr, tc = _tile(R, 512, 16), _tile(C, 2048)
    nr, nc = R // tr, C // tc

    def body(pos_ref, a_ref, b_ref, o_ref):
        o_ref[...] = (a_ref[...].astype(F32) + b_ref[...].astype(F32)).astype(o_ref.dtype)

    if kind == "col":
        mine = pl.BlockSpec((tr, tc), lambda i, j, pos_ref: (pos_ref[1] * nr + i, j))
    else:
        mine = pl.BlockSpec((tr, tc), lambda i, j, pos_ref: (i, pos_ref[1] * nc + j))
    blk = pl.BlockSpec((tr, tc), lambda i, j, pos_ref: (i, j))
    spec = pltpu.PrefetchScalarGridSpec(num_scalar_prefetch=1, grid=(nr, nc), in_specs=[mine, blk], out_specs=blk)
    return _call(body, name=name, grid_spec=spec, out_shape=jax.ShapeDtypeStruct((R, C), WIRE_DTYPE))(pos, g, got)


def _chip_copies(refs, send, recv):
    n = len(BIG)
    x, y, c = _coords()
    cps = []
    for k, chip in enumerate(_other_chips(x, y)):
        s = 2 * chip[0] + chip[1]
        for p in range(n):
            cps.append(_remote(_shard(refs[p], BIG[p][1], s), refs[n + p].at[k], send, recv, 3 * p + k, (*chip, c)))
    return cps


def _landing_zones(pres):
    lands = []
    for (name, kind), g in zip(BIG, pres):
        R, C = g.shape
        lands.append(lax.empty((3, R, C // 4) if kind == "col" else (3, R // 4, C), g.dtype))
    return lands


def _reduce_into_shard(slots, pre, kind, pos, name):
    K, R, C = slots.shape
    tr, tc = _tile(R, 512, 16), _tile(C, 2176)
    nr, nc = R // tr, C // tc

    def body(pos_ref, s_ref, p_ref, o_ref):
        acc = p_ref[...].astype(F32)
        for k in range(K):
            acc = acc + s_ref[k].astype(F32)
        o_ref[...] = acc

    if kind == "col":
        own = pl.BlockSpec((tr, tc), lambda i, j, pos_ref: (i, pos_ref[0] * nc + j))
        full, out = (2 * R, C), pl.BlockSpec((tr, tc), lambda i, j, pos_ref: (pos_ref[1] * nr + i, j))
    else:
        own = pl.BlockSpec((tr, tc), lambda i, j, pos_ref: (pos_ref[0] * nr + i, j))
        full, out = (R, 2 * C), pl.BlockSpec((tr, tc), lambda i, j, pos_ref: (i, pos_ref[1] * nc + j))
    spec = pltpu.PrefetchScalarGridSpec(
        num_scalar_prefetch=1, grid=(nr, nc),
        in_specs=[pl.BlockSpec((K, tr, tc), lambda i, j, pos_ref: (0, i, j)), own], out_specs=out)
    return _call(body, name=name, grid_spec=spec, out_shape=jax.ShapeDtypeStruct(full, F32))(pos, slots, pre)


def _share_reduced(reds):
    n = len(BIG)

    def body(*refs):
        outs = refs[n:2 * n]
        send, recv = refs[2 * n:]
        x, y, c = _coords()
        sib = (x, y, 1 - c)
        cps = []
        for p in range(n):
            mine = _half(outs[p], BIG[p][1], c)
            cps.append(_remote(mine, mine, send, recv, p, sib))
        for cp in cps:
            cp.start()
        for cp in cps:
            cp.wait_send()
        for p in range(n):
            got = _half(outs[p], BIG[p][1], 1 - c)
            _remote(got, got, send, recv, p, sib).wait_recv()

    return pl.pallas_call(
        body, name="share_reduced", out_shape=[jax.ShapeDtypeStruct(r.shape, r.dtype) for r in reds],
        in_specs=[ANY] * n, out_specs=[ANY] * n, input_output_aliases={p: p for p in range(n)},
        scratch_shapes=[pltpu.SemaphoreType.DMA((n,)), pltpu.SemaphoreType.DMA((n,))],
    )(*reds)


def _gather_small(pack):
    _, N = pack.shape

    def body(in_ref, out_ref, send, recv, loc):
        x, y, c = _coords()
        me = 4 * x + 2 * y + c
        own = pltpu.make_async_copy(in_ref, out_ref.at[me], loc)
        own.start()
        cps = []
        for k in range(1, 8):
            dev = (x ^ (k >> 2), y ^ ((k >> 1) & 1), c ^ (k & 1))
            cps.append(_remote(in_ref, out_ref.at[me], send, recv, k - 1, dev))
        for cp in cps:
            cp.start()
        for k in range(1, 8):
            src = 4 * (x ^ (k >> 2)) + 2 * (y ^ ((k >> 1) & 1)) + (c ^ (k & 1))
            _remote(in_ref, out_ref.at[src], send, recv, k - 1, (x, y, c)).wait_recv()
        for cp in cps:
            cp.wait_send()
        own.wait()

    return pl.pallas_call(
        body, name="gather_small", out_shape=jax.ShapeDtypeStruct((8, 1, N), pack.dtype),
        in_specs=[ANY], out_specs=ANY,
        scratch_shapes=[pltpu.SemaphoreType.DMA((7,)), pltpu.SemaphoreType.DMA((7,)), pltpu.SemaphoreType.DMA(())],
    )(pack)


def _sum_small(slots):
    K, _, N = slots.shape

    def body(s_ref, o_ref):
        acc = s_ref[0]
        for k in range(1, K):
            acc = acc + s_ref[k]
        o_ref[...] = acc

    return _call(body, name="sum_small", in_specs=[pl.BlockSpec(memory_space=pltpu.VMEM)],
                 out_specs=pl.BlockSpec(memory_space=pltpu.VMEM), out_shape=jax.ShapeDtypeStruct((1, N), F32))(slots)


def _adamw(w, g, m, v, name):
    R, C = w.shape
    tr, tc = _tile(R, 256, 8), _tile(C, 2176)

    def body(w_ref, g_ref, m_ref, v_ref, d_ref, nm_ref, nv_ref):
        gv = g_ref[...]
        nm = ADAM_B1 * m_ref[...] + (1.0 - ADAM_B1) * gv
        nv = ADAM_B2 * v_ref[...] + (1.0 - ADAM_B2) * gv * gv
        m_hat = nm / (1.0 - ADAM_B1 ** ADAM_STEP)
        v_hat = nv / (1.0 - ADAM_B2 ** ADAM_STEP)
        d_ref[...] = -ADAM_LR * (m_hat / (jnp.sqrt(v_hat) + ADAM_EPS) + ADAM_WD * w_ref[...])
        nm_ref[...] = nm
        nv_ref[...] = nv

    spec = pl.BlockSpec((tr, tc), lambda i, j: (i, j))
    shp = jax.ShapeDtypeStruct((R, C), F32)
    return _call(body, name=name, grid=(R // tr, C // tc), in_specs=[spec] * 4, out_specs=[spec] * 3,
                 out_shape=[shp] * 3)(w, g, m, v)


SMALL = ("norm_g", "mem_norm_g", "attn_q_norm", "attn_k_norm", "conv_w", "mem_q_norm", "mem_k_norm")
WEIGHTS = ("norm_g", "mem_norm_g", "w_in", "attn_q_norm", "attn_k_norm", "conv_w", "mem_w_kv", "mem_q_norm",
           "mem_k_norm", "w_br_attn", "w_br_conv", "w_br_mem", "w_out")


def kernel(x, mem, norm_g, mem_norm_g, w_in, attn_q_norm, attn_k_norm, conv_w, mem_w_kv, mem_q_norm, mem_k_norm, w_br_attn, w_br_conv, w_br_mem, w_out, loss_target, m_norm_g, m_mem_norm_g, m_w_in, m_attn_q_norm, m_attn_k_norm, m_conv_w, m_mem_w_kv, m_mem_q_norm, m_mem_k_norm, m_w_br_attn, m_w_br_conv, m_w_br_mem, m_w_out, v_norm_g, v_mem_norm_g, v_w_in, v_attn_q_norm, v_attn_k_norm, v_conv_w, v_mem_w_kv, v_mem_q_norm, v_mem_k_norm, v_w_br_attn, v_w_br_conv, v_w_br_mem, v_w_out):
    w = dict(norm_g=norm_g, mem_norm_g=mem_norm_g, w_in=w_in, attn_q_norm=attn_q_norm, attn_k_norm=attn_k_norm,
             conv_w=conv_w, mem_w_kv=mem_w_kv, mem_q_norm=mem_q_norm, mem_k_norm=mem_k_norm, w_br_attn=w_br_attn,
             w_br_conv=w_br_conv, w_br_mem=w_br_mem, w_out=w_out)
    m = dict(norm_g=m_norm_g, mem_norm_g=m_mem_norm_g, w_in=m_w_in, attn_q_norm=m_attn_q_norm,
             attn_k_norm=m_attn_k_norm, conv_w=m_conv_w, mem_w_kv=m_mem_w_kv, mem_q_norm=m_mem_q_norm,
             mem_k_norm=m_mem_k_norm, w_br_attn=m_w_br_attn, w_br_conv=m_w_br_conv, w_br_mem=m_w_br_mem, w_out=m_w_out)
    v = dict(norm_g=v_norm_g, mem_norm_g=v_mem_norm_g, w_in=v_w_in, attn_q_norm=v_attn_q_norm,
             attn_k_norm=v_attn_k_norm, conv_w=v_conv_w, mem_w_kv=v_mem_w_kv, mem_q_norm=v_mem_q_norm,
             mem_k_norm=v_mem_k_norm, w_br_attn=v_w_br_attn, w_br_conv=v_w_br_conv, w_br_mem=v_w_br_mem, w_out=v_w_out)
    Bl, _, D = x.shape
    cx, cy = lax.axis_index("x"), lax.axis_index("y")
    chip = 2 * cx + cy
    pos = jnp.stack([chip, lax.axis_index("c")]).astype(jnp.int32)
    order = jnp.stack([chip] + [2 * a + b for a, b in _other_chips(cx, cy)]).astype(jnp.int32)
    n = len(BIG)

    own = _cast(w["w_in"], "cast_w_in")
    lands = [lax.empty(own.shape, own.dtype) for _ in range(3)]
    conv_full = _place_shard(conv_w, "col", pos, F32, "place_conv_w")
    others = [_place_shard(w[name], kind, pos, WIRE_DTYPE, "place_" + name) for name, kind in BIG[1:]]
    hb, hbt, mhb = _norms(x, mem, norm_g, mem_norm_g)

    near = _w_in_copies((0, 1))
    send, recv, (own, l0, l1, conv_full) = _start_copies("gather_near_start", [own, lands[0], lands[1], conv_full], 4, near)
    proj = _proj_chunk(hb, own, order, 0, None, "proj_own")
    own, l0, l1, conv_full = _wait_copies("gather_near_wait", send, recv, [own, l0, l1, conv_full], near, proj)
    l0, l1 = _sibling_forward("gather_near_forward", [l0, l1], 2, _landed_halves)

    far = _w_in_copies((2,))
    send, recv, (own, l2, conv_full, l0) = _start_copies("gather_far_start", [own, lands[2], conv_full, l0], 2, far)
    proj = _proj_chunk(hb, l0, order, 1, proj, "proj_near_x")
    proj = _proj_chunk(hb, l1, order, 2, proj, "proj_near_y")
    own, l2, conv_full, l0 = _wait_copies("gather_far_wait", send, recv, [own, l2, conv_full, l0], far, proj)
    l2, = _sibling_forward("gather_far_forward", [l2], 1, _landed_halves)

    send, recv, (*others, l2) = _start_copies("gather_rest_start", [*others, l2], 3 * (n - 1), _other_weight_copies)
    proj = _proj_chunk(hb, l2, order, 3, proj, "proj_far")
    mixed = _mix_fwd(proj, Bl, attn_q_norm, attn_k_norm, conv_full)
    *others, l2 = _wait_copies("gather_rest_wait", send, recv, [*others, l2], _other_weight_copies, mixed[2])
    others = _sibling_forward("gather_rest_forward", others, 3 * (n - 1), _other_weight_halves)
    W = {name: others[p] for p, (name, _) in enumerate(BIG[1:])}

    G, rest = _weight_grads(x, mem, loss_target, norm_g, mem_norm_g, attn_q_norm, attn_k_norm, conv_full,
                            mem_q_norm, mem_k_norm, W, (hb, hbt, mhb, proj, *mixed))

    w_in_full = None
    for j, shard in enumerate((own, l0, l1, l2)):
        w_in_full = _place_shard(shard, "col", order, WIRE_DTYPE, f"assemble_w_in_{j}", slot=j, into=w_in_full)

    parts = [G[name] for name, _ in BIG]
    got = _sibling_exchange(parts)
    pres = [_presum(parts[p], got[p], kind, pos, "presum_" + name) for p, (name, kind) in enumerate(BIG)]
    send, recv, thru = _start_copies("chip_exchange_start", [*pres, *_landing_zones(pres), rest[0]], 3 * n, _chip_copies)
    grad_x, small = _input_grad((thru[2 * n], *rest[1:]), w_in_full)
    thru = _wait_copies("chip_exchange_wait", send, recv, thru[:2 * n], _chip_copies, grad_x)
    pres, slots = thru[:n], thru[n:]
    grad_x = grad_x.reshape(x.shape)
    reds = [_reduce_into_shard(slots[p], pres[p], kind, pos, "reduce_" + name) for p, (name, kind) in enumerate(BIG)]
    grads = dict(zip([name for name, _ in BIG], _share_reduced(reds)))

    tot = _sum_small(_gather_small(small))[0]
    loss = tot[0]
    off = 128
    for name, size in (("norm_g", D), ("mem_norm_g", D), ("attn_q_norm", NGROUP * HEAD), ("attn_k_norm", NGROUP * HEAD),
                       ("conv_w", 3 * CONVW), ("mem_q_norm", MEM_HD), ("mem_k_norm", MEM_HD)):
        grads[name] = tot[off:off + size]
        off += size
    cw = conv_w.shape[1]
    grads["conv_w"] = lax.dynamic_slice(grads["conv_w"].reshape(3, CONVW), (0, chip * cw), (3, cw))
    for name in SMALL:
        grads[name] = grads[name].reshape(w[name].shape)

    delta, new_m, new_v = {}, {}, {}
    for name, _ in BIG:
        delta[name], new_m[name], new_v[name] = _adamw(w[name], grads[name], m[name], v[name], "adamw_" + name)

    def packed(t):
        return jnp.concatenate([t[name].reshape(1, -1) for name in SMALL], axis=1)

    ds, ms, vs = _adamw(packed(w), packed(grads), packed(m), packed(v), "adamw_small")
    off = 0
    for name in SMALL:
        size = w[name].size
        delta[name] = ds[0, off:off + size].reshape(w[name].shape)
        new_m[name] = ms[0, off:off + size].reshape(w[name].shape)
        new_v[name] = vs[0, off:off + size].reshape(w[name].shape)
        off += size

    return (loss, grad_x, *[grads[n] for n in WEIGHTS], *[delta[n] for n in WEIGHTS],
            *[new_m[n] for n in WEIGHTS], *[new_v[n] for n in WEIGHTS])
```

```python
import functools

import jax
import jax.numpy as jnp
from jax import lax
from jax.experimental import pallas as pl
from jax.experimental.pallas import tpu as pltpu

F32 = jnp.float32
MXU_DTYPE = jnp.bfloat16
WIRE_DTYPE = jnp.bfloat16
EPS = 1e-6
NEG = -1e30

HEAD = 128
HPG = 4
GW = HPG * HEAD
DILATIONS = (1, 4, 16)
NGROUP = len(DILATIONS)
BLK = 128
QKV = NGROUP * GW
CONVW = 1024
MEM_HEADS = 4
MEM_HD = 256
MEMW = MEM_HEADS * MEM_HD
Q0, K0, V0 = 0, QKV, 2 * QKV
ZA = 3 * QKV
CB, CC, CV, ZC = ZA + GW, ZA + GW + CONVW, ZA + GW + 2 * CONVW, ZA + GW + 3 * CONVW
MQ = ZC + CONVW
ZM = MQ + MEMW
G0 = ZM + MEMW

ADAM_LR, ADAM_B1, ADAM_B2, ADAM_EPS, ADAM_WD, ADAM_STEP = 0.001, 0.9, 0.999, 1e-08, 0.01, 10

VMEM_LIMIT = 56 * 1024 * 1024
MESH = pl.DeviceIdType.MESH
ANY = pl.BlockSpec(memory_space=pl.ANY)


def _tile(n, pref, mult=128):
    t = min(pref, n)
    while t > mult and (n % t or t % mult):
        t -= mult
    assert n % t == 0, (n, pref)
    return t


def _call(body, *, name, out_shape, grid=(), in_specs=None, out_specs=None, scratch_shapes=(),
          aliases=None, grid_spec=None):
    kw = {}
    if grid_spec is not None:
        kw["grid_spec"] = grid_spec
        ngrid = len(grid_spec.grid)
    else:
        kw.update(grid=grid, in_specs=in_specs, out_specs=out_specs, scratch_shapes=list(scratch_shapes))
        ngrid = len(grid)
    params = pltpu.CompilerParams(dimension_semantics=("arbitrary",) * ngrid, vmem_limit_bytes=VMEM_LIMIT)
    return pl.pallas_call(body, name=name, out_shape=out_shape, compiler_params=params,
                          input_output_aliases=aliases or {}, **kw)


_DIMS = {"nn": (((1,), (0,)), ((), ())), "nt": (((1,), (1,)), ((), ())), "tn": (((0,), (0,)), ((), ()))}


def _mxu(a, b, mode):
    return lax.dot_general(a.astype(MXU_DTYPE), b.astype(MXU_DTYPE), _DIMS[mode], preferred_element_type=F32)


@functools.partial(jax.custom_vjp, nondiff_argnums=(2,))
def _dot(a, b, mode):
    return _mxu(a, b, mode)


def _dot_fwd(a, b, mode):
    return _mxu(a, b, mode), (a, b)


def _dot_bwd(mode, res, g):
    a, b = res
    if mode == "nn":
        return _mxu(g, b, "nt"), _mxu(a, g, "tn")
    if mode == "nt":
        return _mxu(g, b, "nn"), _mxu(g, a, "tn")
    return _mxu(b, g, "nt"), _mxu(a, g, "nn")


_dot.defvjp(_dot_fwd, _dot_bwd)


def _sig(z):
    return 1.0 / (1.0 + jnp.exp(-z))


def _silu(z):
    return z * _sig(z)


def _rms_rows(t, g):
    return t * lax.rsqrt(jnp.mean(t * t, axis=-1, keepdims=True) + EPS) * g


def _attn_block(q, k2, v2, gq, gk, first):
    qn = _rms_rows(q, gq)
    kn = _rms_rows(k2, gk)
    s = jnp.where(_band_mask(first), _dot(qn, kn, "nt") * (HEAD ** -0.5), NEG)
    m = lax.stop_gradient(jnp.max(s, axis=-1, keepdims=True))
    p = jnp.exp(s - m)
    den = jnp.sum(p, axis=-1, keepdims=True)
    o = _dot(p, v2, "nn") / den
    return o, m + jnp.log(den)


def _band_mask(first):
    a = lax.broadcasted_iota(jnp.int32, (BLK, 2 * BLK), 0)
    b = lax.broadcasted_iota(jnp.int32, (BLK, 2 * BLK), 1)
    return (b >= a) & (b <= a + BLK) & (b >= jnp.where(first, BLK, 0))


def _norm_parts(t):
    r = lax.rsqrt(jnp.mean(t * t, axis=-1, keepdims=True) + EPS)
    return r, t * r


def _norm_bwd(dn, g, r, th):
    dth = dn * g
    return r * (dth - th * jnp.mean(dth * th, axis=-1, keepdims=True)), jnp.sum(dn * th, axis=0, keepdims=True)


def _attn_block_bwd(q, k2, v2, gq, gk, first, do, o, lse, dlse):
    scale = HEAD ** -0.5
    rq, qh = _norm_parts(q)
    rk, kh = _norm_parts(k2)
    qn, kn = qh * gq, kh * gk
    s = jnp.where(_band_mask(first), _mxu(qn, kn, "nt") * scale, NEG)
    p = jnp.exp(s - lse)
    ds = p * (_mxu(do, v2, "nt") + (dlse - jnp.sum(do * o, axis=-1, keepdims=True))) * scale
    dq, dgq = _norm_bwd(_mxu(ds, kn, "nn"), gq, rq, qh)
    dk2, dgk = _norm_bwd(_mxu(ds, qn, "tn"), gk, rk, kh)
    return dq, dk2, _mxu(p, do, "tn"), dgq, dgk


def _combine(o1, o2, o3, l1, l2, l3, z):
    m = lax.stop_gradient(jnp.maximum(jnp.maximum(l1, l2), l3))
    e1, e2, e3 = jnp.exp(l1 - m), jnp.exp(l2 - m), jnp.exp(l3 - m)
    return (e1 * o1 + e2 * o2 + e3 * o3) / (e1 + e2 + e3) * _silu(z)


def _mem_block(q, z, kv, gq, gk):
    outs = []
    for h in range(MEM_HEADS):
        sl = slice(h * MEM_HD, (h + 1) * MEM_HD)
        qn = _rms_rows(q[:, sl], gq)
        kn = _rms_rows(kv[:, sl], gk)
        s = _dot(qn, kn, "nt") * (MEM_HD ** -0.5)
        m = lax.stop_gradient(jnp.max(s, axis=-1, keepdims=True))
        p = jnp.exp(s - m)
        den = jnp.sum(p, axis=-1, keepdims=True)
        outs.append(_dot(p, kv[:, MEMW + h * MEM_HD:MEMW + (h + 1) * MEM_HD], "nn") / den)
    return jnp.concatenate(outs, axis=-1) * _silu(z)


def _cast(w, name):
    R, C = w.shape
    tr, tc = _tile(R, 512, 8), _tile(C, 2176)

    def body(w_ref, o_ref):
        o_ref[...] = w_ref[...].astype(o_ref.dtype)

    spec = pl.BlockSpec((tr, tc), lambda i, j: (i, j))
    return _call(body, name=name, grid=(R // tr, C // tc), in_specs=[spec], out_specs=spec,
                 out_shape=jax.ShapeDtypeStruct((R, C), WIRE_DTYPE))(w)


def _place_shard(w, kind, pos, dtype, name, slot=0, into=None):
    R, C = w.shape
    tr, tc = _tile(R, 512, 8), _tile(C, 2176)
    nr, nc = R // tr, C // tc

    def body(pos_ref, w_ref, *rest):
        rest[-1][...] = w_ref[...].astype(rest[-1].dtype)

    if kind == "col":
        full, out = (R, 4 * C), pl.BlockSpec((tr, tc), lambda i, j, pos_ref: (i, pos_ref[slot] * nc + j))
    else:
        full, out = (4 * R, C), pl.BlockSpec((tr, tc), lambda i, j, pos_ref: (pos_ref[slot] * nr + i, j))
    in_specs, args = [pl.BlockSpec((tr, tc), lambda i, j, pos_ref: (i, j))], [pos, w]
    if into is not None:
        in_specs.append(ANY)
        args.append(into)
    spec = pltpu.PrefetchScalarGridSpec(num_scalar_prefetch=1, grid=(nr, nc), in_specs=in_specs, out_specs=out)
    return _call(body, name=name, grid_spec=spec, out_shape=jax.ShapeDtypeStruct(full, dtype),
                 aliases={} if into is None else {2: 0})(*args)


def _matmul(a, b, mode, out_dtype, *, name, tm=512, tn=512, tk=512):
    if mode == "nn":
        (M, K), (_, N) = a.shape, b.shape
    elif mode == "nt":
        (M, K), (N, _) = a.shape, b.shape
    else:
        (K, M), (_, N) = a.shape, b.shape
    tm, tn, tk = _tile(M, tm), _tile(N, tn), _tile(K, tk)
    nk = K // tk

    def body(a_ref, b_ref, o_ref, *acc):
        part = lax.dot_general(a_ref[...], b_ref[...], _DIMS[mode], preferred_element_type=F32)
        if nk == 1:
            o_ref[...] = part.astype(o_ref.dtype)
            return
        acc_ref, = acc
        k = pl.program_id(2)

        @pl.when(k == 0)
        def _():
            acc_ref[...] = part

        @pl.when(k > 0)
        def _():
            acc_ref[...] += part

        @pl.when(k == nk - 1)
        def _():
            o_ref[...] = acc_ref[...].astype(o_ref.dtype)

    a_spec = pl.BlockSpec((tk, tm), lambda i, j, k: (k, i)) if mode == "tn" else pl.BlockSpec((tm, tk), lambda i, j, k: (i, k))
    b_spec = pl.BlockSpec((tn, tk), lambda i, j, k: (j, k)) if mode == "nt" else pl.BlockSpec((tk, tn), lambda i, j, k: (k, j))
    return _call(body, name=name, grid=(M // tm, N // tn, nk), in_specs=[a_spec, b_spec],
                 out_specs=pl.BlockSpec((tm, tn), lambda i, j, k: (i, j)),
                 out_shape=jax.ShapeDtypeStruct((M, N), out_dtype),
                 scratch_shapes=[] if nk == 1 else [pltpu.VMEM((tm, tn), F32)])(a, b)


def _rms_fwd(x, g, name):
    R, D = x.shape
    tr = _tile(R, 512)

    def body(x_ref, g_ref, o_ref, t_ref):
        y = _rms_rows(x_ref[...], g_ref[...])
        o_ref[...] = y.astype(o_ref.dtype)
        t_ref[...] = y.T.astype(t_ref.dtype)

    row = pl.BlockSpec((tr, D), lambda i: (i, 0))
    return _call(body, name=name, grid=(R // tr,), in_specs=[row, pl.BlockSpec((1, D), lambda i: (0, 0))],
                 out_specs=[row, pl.BlockSpec((D, tr), lambda i: (0, i))],
                 out_shape=[jax.ShapeDtypeStruct((R, D), MXU_DTYPE), jax.ShapeDtypeStruct((D, R), MXU_DTYPE)])(x, g)


def _rms_bwd(x, dh, g, dy, name):
    R, D = x.shape
    tr = _tile(R, 256)
    with_dx = dy is not None

    def body(*refs):
        if with_dx:
            x_ref, dh_ref, g_ref, dy_ref, dx_ref, dg_ref = refs
        else:
            x_ref, dh_ref, g_ref, dg_ref = refs
        xv, dhv = x_ref[...], dh_ref[...]
        r = lax.rsqrt(jnp.mean(xv * xv, axis=-1, keepdims=True) + EPS)
        xh = xv * r

        @pl.when(pl.program_id(0) == 0)
        def _():
            dg_ref[...] = jnp.zeros_like(dg_ref)

        dg_ref[...] += jnp.sum(dhv * xh, axis=0, keepdims=True)
        if with_dx:
            dxh = dhv * g_ref[...]
            dx_ref[...] = dy_ref[...] + r * (dxh - xh * jnp.mean(dxh * xh, axis=-1, keepdims=True))

    row = pl.BlockSpec((tr, D), lambda i: (i, 0))
    vec = pl.BlockSpec((1, D), lambda i: (0, 0))
    dg_shape = jax.ShapeDtypeStruct((1, D), F32)
    if with_dx:
        return _call(body, name=name, grid=(R // tr,), in_specs=[row, row, vec, row], out_specs=[row, vec],
                     out_shape=[jax.ShapeDtypeStruct((R, D), F32), dg_shape])(x, dh, g, dy)
    return None, _call(body, name=name, grid=(R // tr,), in_specs=[row, row, vec], out_specs=vec,
                       out_shape=dg_shape)(x, dh, g)


def _attn_geom(g, d):
    hc = HPG if d == 1 else 1
    cw = hc * HEAD
    cq, ck, cv = (Q0 + g * GW) // cw, (K0 + g * GW) // cw, (V0 + g * GW) // cw
    return (1, BLK * d, cw), hc, HPG // hc, cq, ck, cv


def _rows(ref, r, d, sl):
    if d == 1:
        return ref[0, :, sl]
    return ref.at[0][pl.ds(r, BLK, stride=d), sl]


def _set_rows(ref, r, d, sl, val):
    if d == 1:
        ref[0, :, sl] = val
    else:
        ref.at[0][pl.ds(r, BLK, stride=d), sl] = val


def _attn_fwd(proj3, gq, gk, g, d):
    Bl, S, _ = proj3.shape
    blk, hc, ncb, cq, ck, cv = _attn_geom(g, d)
    nb = S // blk[1]

    def body(q_ref, kp_ref, kc_ref, vp_ref, vc_ref, gq_ref, gk_ref, o_ref, lse_ref):
        first = pl.program_id(2) == 0
        for r in range(d):
            for h in range(hc):
                sl = slice(h * HEAD, (h + 1) * HEAD)
                k2 = jnp.concatenate([_rows(kp_ref, r, d, sl), _rows(kc_ref, r, d, sl)], axis=0)
                v2 = jnp.concatenate([_rows(vp_ref, r, d, sl), _rows(vc_ref, r, d, sl)], axis=0)
                o, lse = _attn_block(_rows(q_ref, r, d, sl), k2, v2, gq_ref[...], gk_ref[...], first)
                _set_rows(o_ref, r, d, sl, o)
                _set_rows(lse_ref, r, d, sl, jnp.broadcast_to(lse, (BLK, HEAD)))

    def cur(c0):
        return pl.BlockSpec(blk, lambda b, j, i: (b, i, c0 + j))

    def prev(c0):
        return pl.BlockSpec(blk, lambda b, j, i: (b, jnp.maximum(i - 1, 0), c0 + j))

    vec = pl.BlockSpec((1, HEAD), lambda b, j, i: (0, 0))
    out = pl.BlockSpec(blk, lambda b, j, i: (b, i, j))
    shp = jax.ShapeDtypeStruct((Bl, S, GW), F32)
    return _call(body, name=f"attn_fwd_g{g}", grid=(Bl, ncb, nb),
                 in_specs=[cur(cq), prev(ck), cur(ck), prev(cv), cur(cv), vec, vec],
                 out_specs=[out, out], out_shape=[shp, shp])(proj3, proj3, proj3, proj3, proj3, gq, gk)


def _attn_bwd(proj3, gq, gk, o3, l3, do3, dl3, g, d):
    Bl, S, _ = proj3.shape
    blk, hc, ncb, cq, ck, cv = _attn_geom(g, d)
    nb = S // blk[1]

    def body(q_ref, kp_ref, kc_ref, vp_ref, vc_ref, gq_ref, gk_ref, o_ref, l_ref, do_ref, dl_ref,
             dq_ref, dk_ref, dv_ref, dgq_ref, dgk_ref, ck_ref, cv_ref):
        i = pl.program_id(2)
        first = i == 0

        @pl.when((pl.program_id(0) == 0) & (pl.program_id(1) == 0) & first)
        def _():
            dgq_ref[...] = jnp.zeros_like(dgq_ref)
            dgk_ref[...] = jnp.zeros_like(dgk_ref)

        @pl.when(first)
        def _():
            ck_ref[...] = jnp.zeros_like(ck_ref)
            cv_ref[...] = jnp.zeros_like(cv_ref)

        @pl.when(i < nb)
        def _():
            dgq, dgk = jnp.zeros((1, HEAD), F32), jnp.zeros((1, HEAD), F32)
            for r in range(d):
                rs = slice(r * BLK, (r + 1) * BLK)
                for h in range(hc):
                    sl = slice(h * HEAD, (h + 1) * HEAD)
                    k2 = jnp.concatenate([_rows(kp_ref, r, d, sl), _rows(kc_ref, r, d, sl)], axis=0)
                    v2 = jnp.concatenate([_rows(vp_ref, r, d, sl), _rows(vc_ref, r, d, sl)], axis=0)
                    dq, dk2, dv2, a, b = _attn_block_bwd(
                        _rows(q_ref, r, d, sl), k2, v2, gq_ref[...], gk_ref[...], first, _rows(do_ref, r, d, sl),
                        _rows(o_ref, r, d, sl), _rows(l_ref, r, d, sl)[:, :1], _rows(dl_ref, r, d, sl)[:, :1])
                    _set_rows(dq_ref, r, d, sl, dq)
                    _set_rows(dk_ref, r, d, sl, ck_ref[rs, sl] + dk2[:BLK])
                    _set_rows(dv_ref, r, d, sl, cv_ref[rs, sl] + dv2[:BLK])
                    ck_ref[rs, sl] = dk2[BLK:]
                    cv_ref[rs, sl] = dv2[BLK:]
                    dgq, dgk = dgq + a, dgk + b
            dgq_ref[...] += dgq
            dgk_ref[...] += dgk

        @pl.when(i == nb)
        def _():
            for r in range(d):
                rs = slice(r * BLK, (r + 1) * BLK)
                _set_rows(dk_ref, r, d, slice(None), ck_ref[rs, :])
                _set_rows(dv_ref, r, d, slice(None), cv_ref[rs, :])

    def cur(c0):
        return pl.BlockSpec(blk, lambda b, j, i: (b, jnp.minimum(i, nb - 1), c0 + j))

    def prev(c0):
        return pl.BlockSpec(blk, lambda b, j, i: (b, jnp.clip(i - 1, 0, nb - 1), c0 + j))

    vec = pl.BlockSpec((1, HEAD), lambda b, j, i: (0, 0))
    at_q = pl.BlockSpec(blk, lambda b, j, i: (b, jnp.minimum(i, nb - 1), j))
    at_k = pl.BlockSpec(blk, lambda b, j, i: (b, jnp.maximum(i - 1, 0), j))
    shp = jax.ShapeDtypeStruct((Bl, S, GW), F32)
    gshp = jax.ShapeDtypeStruct((1, HEAD), F32)
    return _call(body, name=f"attn_bwd_g{g}", grid=(Bl, ncb, nb + 1),
                 in_specs=[cur(cq), prev(ck), cur(ck), prev(cv), cur(cv), vec, vec, at_q, at_q, at_q, at_q],
                 out_specs=[at_q, at_k, at_k, vec, vec], out_shape=[shp, shp, shp, gshp, gshp],
                 scratch_shapes=[pltpu.VMEM(blk[1:], F32), pltpu.VMEM(blk[1:], F32)],
                 )(proj3, proj3, proj3, proj3, proj3, gq, gk, o3, l3, do3, dl3)


def _combine_fwd(os, ls, proj2):
    T = proj2.shape[0]
    tr = _tile(T, 512)

    def body(o1, o2, o3, l1, l2, l3, z, a_ref):
        a_ref[...] = _combine(o1[...], o2[...], o3[...], l1[...], l2[...], l3[...], z[...]).astype(a_ref.dtype)

    row = pl.BlockSpec((tr, GW), lambda i: (i, 0))
    return _call(body, name="combine_fwd", grid=(T // tr,),
                 in_specs=[row] * 6 + [pl.BlockSpec((tr, GW), lambda i: (i, ZA // GW))], out_specs=row,
                 out_shape=jax.ShapeDtypeStruct((T, GW), MXU_DTYPE))(*os, *ls, proj2)


def _combine_bwd(os, ls, proj2, da):
    T = proj2.shape[0]
    tr = _tile(T, 256)

    def body(o1, o2, o3, l1, l2, l3, z, da_ref, d1, d2, d3, e1, e2, e3, dz_ref):
        _, vjp = jax.vjp(_combine, o1[...], o2[...], o3[...], l1[...], l2[...], l3[...], z[...])
        go1, go2, go3, gl1, gl2, gl3, gz = vjp(da_ref[...])
        d1[...], d2[...], d3[...] = go1, go2, go3
        dz_ref[...] = gz.astype(dz_ref.dtype)
        for ref, gl in ((e1, gl1), (e2, gl2), (e3, gl3)):
            for h in range(HPG):
                sl = slice(h * HEAD, (h + 1) * HEAD)
                ref[:, sl] = jnp.broadcast_to(jnp.sum(gl[:, sl], axis=-1, keepdims=True), (tr, HEAD))

    row = pl.BlockSpec((tr, GW), lambda i: (i, 0))
    f = jax.ShapeDtypeStruct((T, GW), F32)
    outs = _call(body, name="combine_bwd", grid=(T // tr,),
                 in_specs=[row] * 6 + [pl.BlockSpec((tr, GW), lambda i: (i, ZA // GW)), row],
                 out_specs=[row] * 7, out_shape=[f] * 6 + [jax.ShapeDtypeStruct((T, GW), MXU_DTYPE)],
                 )(*os, *ls, proj2, da)
    return outs[:3], outs[3:6], outs[6]


def _shift_down(u, j, t):
    return jnp.where(t >= j, pltpu.roll(u, j, 0), 0.0)


def _shift_up(u, j, t):
    n = u.shape[0]
    return jnp.where(t < n - j, pltpu.roll(u, n - j, 0), 0.0)


def _conv_specs(Bl, S, cw):
    def sec(c0):
        return pl.BlockSpec((1, S, cw), lambda j, b: (b, 0, c0 // cw + j))
    return [sec(CB), sec(CC), sec(CV), sec(ZC)], pl.BlockSpec((3, cw), lambda j, b: (0, j))


def _conv_fwd(proj3, conv_w):
    Bl, S, _ = proj3.shape
    cw = 256
    secs, wspec = _conv_specs(Bl, S, cw)

    def body(b_ref, c_ref, v_ref, z_ref, w_ref, o_ref):
        t = lax.broadcasted_iota(jnp.int32, (S, cw), 0)
        u = c_ref[0] * v_ref[0]
        y = w_ref[0:1, :] * u + w_ref[1:2, :] * _shift_down(u, 1, t) + w_ref[2:3, :] * _shift_down(u, 2, t)
        o_ref[0] = (b_ref[0] * y * _silu(z_ref[0])).astype(o_ref.dtype)

    return _call(body, name="conv_fwd", grid=(CONVW // cw, Bl), in_specs=secs + [wspec],
                 out_specs=pl.BlockSpec((1, S, cw), lambda j, b: (b, 0, j)),
                 out_shape=jax.ShapeDtypeStruct((Bl, S, CONVW), MXU_DTYPE))(proj3, proj3, proj3, proj3, conv_w)


def _conv_bwd(proj3, conv_w, dcc3):
    Bl, S, _ = proj3.shape
    cw = 256
    secs, wspec = _conv_specs(Bl, S, cw)

    def body(b_ref, c_ref, v_ref, z_ref, w_ref, d_ref, db_ref, dc_ref, dv_ref, dz_ref, dw_ref):
        t = lax.broadcasted_iota(jnp.int32, (S, cw), 0)
        bv, cv, vv, zv, dv = b_ref[0], c_ref[0], v_ref[0], z_ref[0], d_ref[0]
        u = cv * vv
        u1, u2 = _shift_down(u, 1, t), _shift_down(u, 2, t)
        y = w_ref[0:1, :] * u + w_ref[1:2, :] * u1 + w_ref[2:3, :] * u2
        sg = _sig(zv)
        sz = zv * sg
        gy = dv * bv * sz
        db_ref[0] = (dv * y * sz).astype(db_ref.dtype)
        dz_ref[0] = (dv * bv * y * sg * (1.0 + zv * (1.0 - sg))).astype(dz_ref.dtype)
        du = w_ref[0:1, :] * gy + w_ref[1:2, :] * _shift_up(gy, 1, t) + w_ref[2:3, :] * _shift_up(gy, 2, t)
        dc_ref[0] = (du * vv).astype(dc_ref.dtype)
        dv_ref[0] = (du * cv).astype(dv_ref.dtype)

        @pl.when(pl.program_id(1) == 0)
        def _():
            dw_ref[...] = jnp.zeros_like(dw_ref)

        dw_ref[0:1, :] += jnp.sum(gy * u, axis=0, keepdims=True)
        dw_ref[1:2, :] += jnp.sum(gy * u1, axis=0, keepdims=True)
        dw_ref[2:3, :] += jnp.sum(gy * u2, axis=0, keepdims=True)

    blk = pl.BlockSpec((1, S, cw), lambda j, b: (b, 0, j))
    shp = jax.ShapeDtypeStruct((Bl, S, CONVW), MXU_DTYPE)
    return _call(body, name="conv_bwd", grid=(CONVW // cw, Bl), in_specs=secs + [wspec, blk],
                 out_specs=[blk] * 4 + [wspec], out_shape=[shp] * 4 + [jax.ShapeDtypeStruct((3, CONVW), F32)],
                 )(proj3, proj3, proj3, proj3, conv_w, dcc3)


def _mem_specs(S, tq):
    q = pl.BlockSpec((1, tq, MEMW), lambda b, j: (b, j, MQ // MEMW))
    z = pl.BlockSpec((1, tq, MEMW), lambda b, j: (b, j, ZM // MEMW))
    kv = pl.BlockSpec((1, MEM_HD, 2 * MEMW), lambda b, j: (b, 0, 0))
    vec = pl.BlockSpec((1, MEM_HD), lambda b, j: (0, 0))
    blk = pl.BlockSpec((1, tq, MEMW), lambda b, j: (b, j, 0))
    return q, z, kv, vec, blk


def _mem_fwd(proj3, mkv3, gq, gk):
    Bl, S, _ = proj3.shape
    tq = _tile(S, 512)
    q, z, kv, vec, blk = _mem_specs(S, tq)

    def body(q_ref, z_ref, kv_ref, gq_ref, gk_ref, o_ref):
        o_ref[0] = _mem_block(q_ref[0], z_ref[0], kv_ref[0], gq_ref[...], gk_ref[...]).astype(o_ref.dtype)

    return _call(body, name="mem_fwd", grid=(Bl, S // tq), in_specs=[q, z, kv, vec, vec], out_specs=blk,
                 out_shape=jax.ShapeDtypeStruct((Bl, S, MEMW), MXU_DTYPE))(proj3, proj3, mkv3, gq, gk)


def _mem_bwd(proj3, mkv3, gq, gk, dmo3):
    Bl, S, _ = proj3.shape
    tq = _tile(S, 256)
    q, z, kv, vec, blk = _mem_specs(S, tq)

    def body(q_ref, z_ref, kv_ref, gq_ref, gk_ref, d_ref, dq_ref, dz_ref, dkv_ref, dgq_ref, dgk_ref):
        _, vjp = jax.vjp(_mem_block, q_ref[0], z_ref[0], kv_ref[0], gq_ref[...], gk_ref[...])
        dq, dz, dkv, dgq, dgk = vjp(d_ref[0])
        dq_ref[0] = dq.astype(dq_ref.dtype)
        dz_ref[0] = dz.astype(dz_ref.dtype)
        j = pl.program_id(1)

        @pl.when(j == 0)
        def _():
            dkv_ref[0] = jnp.zeros_like(dkv)

        @pl.when((j == 0) & (pl.program_id(0) == 0))
        def _():
            dgq_ref[...] = jnp.zeros_like(dgq_ref)
            dgk_ref[...] = jnp.zeros_like(dgk_ref)

        dkv_ref[0] += dkv
        dgq_ref[...] += dgq
        dgk_ref[...] += dgk

    shp = jax.ShapeDtypeStruct((Bl, S, MEMW), MXU_DTYPE)
    gshp = jax.ShapeDtypeStruct((1, MEM_HD), F32)
    return _call(body, name="mem_bwd", grid=(Bl, S // tq), in_specs=[q, z, kv, vec, vec, blk],
                 out_specs=[blk, blk, kv, vec, vec],
                 out_shape=[shp, shp, jax.ShapeDtypeStruct(mkv3.shape, F32), gshp, gshp],
                 )(proj3, proj3, mkv3, gq, gk, dmo3)


def _merge_specs(T, D, tm, tn):
    def act(w):
        return pl.BlockSpec((tm, w), lambda i, n: (i, 0))

    def wsp(w):
        return pl.BlockSpec((w, tn), lambda i, n: (0, n))

    gates = [pl.BlockSpec((tm, tn), lambda i, n, k=k: (i, (G0 + k * D) // tn + n)) for k in range(3)]
    tile = pl.BlockSpec((tm, tn), lambda i, n: (i, n))
    return act, wsp, gates, tile


def _merge_fwd(a, cc, mo, wa, wc, wm, proj2):
    T, D = a.shape[0], wa.shape[1]
    tm, tn = _tile(T, 512), _tile(D, 512)
    act, wsp, gates, tile = _merge_specs(T, D, tm, tn)

    def body(a_ref, c_ref, m_ref, wa_ref, wc_ref, wm_ref, g0, g1, g2, mg_ref, mt_ref, pa_ref, pc_ref, pm_ref):
        pa = jnp.dot(a_ref[...], wa_ref[...], preferred_element_type=F32)
        pc = jnp.dot(c_ref[...], wc_ref[...], preferred_element_type=F32)
        pm = jnp.dot(m_ref[...], wm_ref[...], preferred_element_type=F32)
        mg = _sig(g0[...]) * pa + _sig(g1[...]) * pc + _sig(g2[...]) * pm
        mg_ref[...] = mg.astype(mg_ref.dtype)
        mt_ref[...] = mg.T.astype(mt_ref.dtype)
        pa_ref[...] = pa.astype(pa_ref.dtype)
        pc_ref[...] = pc.astype(pc_ref.dtype)
        pm_ref[...] = pm.astype(pm_ref.dtype)

    shp = jax.ShapeDtypeStruct((T, D), MXU_DTYPE)
    return _call(body, name="merge_fwd", grid=(T // tm, D // tn),
                 in_specs=[act(GW), act(CONVW), act(MEMW), wsp(GW), wsp(CONVW), wsp(MEMW)] + gates,
                 out_specs=[tile, pl.BlockSpec((tn, tm), lambda i, n: (n, i)), tile, tile, tile],
                 out_shape=[shp, jax.ShapeDtypeStruct((D, T), MXU_DTYPE), shp, shp, shp],
                 )(a, cc, mo, wa, wc, wm, proj2, proj2, proj2)


def _merge_bwd(dyb, w_out, proj2, pa, pc, pm):
    T, D = dyb.shape
    tm, tn = _tile(T, 512), _tile(D, 512)
    _, _, gates, tile = _merge_specs(T, D, tm, tn)

    def body(dy_ref, w_ref, g0, g1, g2, p0, p1, p2, dp0, dp1, dp2, dg0, dg1, dg2):
        dm = lax.dot_general(dy_ref[...], w_ref[...], _DIMS["nt"], preferred_element_type=F32)
        for g_ref, p_ref, dp_ref, dg_ref in ((g0, p0, dp0, dg0), (g1, p1, dp1, dg1), (g2, p2, dp2, dg2)):
            gt = _sig(g_ref[...])
            dp_ref[...] = (gt * dm).astype(dp_ref.dtype)
            dg_ref[...] = (dm * p_ref[...].astype(F32) * gt * (1.0 - gt)).astype(dg_ref.dtype)

    shp = jax.ShapeDtypeStruct((T, D), MXU_DTYPE)
    return _call(body, name="merge_bwd", grid=(T // tm, D // tn),
                 in_specs=[pl.BlockSpec((tm, D), lambda i, n: (i, 0)), pl.BlockSpec((tn, D), lambda i, n: (n, 0))]
                 + gates + [tile] * 3,
                 out_specs=[tile] * 6, out_shape=[shp] * 6)(dyb, w_out, proj2, proj2, proj2, pa, pc, pm)


def _out_loss(merged, w_out, x, tgt):
    T, D = x.shape
    tm = _tile(T, 256)

    def body(m_ref, w_ref, x_ref, t_ref, dy_ref, dyb_ref, loss_ref):
        err = x_ref[...] + jnp.dot(m_ref[...], w_ref[...], preferred_element_type=F32) - t_ref[...]
        dy = err * (1.0 / D)
        dy_ref[...] = dy
        dyb_ref[...] = dy.astype(dyb_ref.dtype)

        @pl.when(pl.program_id(0) == 0)
        def _():
            loss_ref[...] = jnp.zeros_like(loss_ref)

        loss_ref[...] += jnp.sum(err * err) * (0.5 / D)

    row = pl.BlockSpec((tm, D), lambda i: (i, 0))
    return _call(body, name="out_loss", grid=(T // tm,),
                 in_specs=[row, pl.BlockSpec((D, D), lambda i: (0, 0)), row, row],
                 out_specs=[row, row, pl.BlockSpec((1, 128), lambda i: (0, 0))],
                 out_shape=[jax.ShapeDtypeStruct((T, D), F32), jax.ShapeDtypeStruct((T, D), MXU_DTYPE),
                            jax.ShapeDtypeStruct((1, 128), F32)])(merged, w_out, x, tgt)


def _proj_chunk(hb, w, order, j, buf, name):
    T, D = hb.shape
    Cs = w.shape[1]
    tm, tn = _tile(T, 1024), _tile(Cs, 2176)
    nj = Cs // tn

    def body(order_ref, a_ref, b_ref, *rest):
        rest[-1][...] = jnp.dot(a_ref[...], b_ref[...], preferred_element_type=F32)

    in_specs = [pl.BlockSpec((tm, D), lambda n, i, o: (i, 0)), pl.BlockSpec((D, tn), lambda n, i, o: (0, n))]
    args = [order, hb, w]
    if buf is not None:
        in_specs.append(ANY)
        args.append(buf)
    spec = pltpu.PrefetchScalarGridSpec(
        num_scalar_prefetch=1, grid=(nj, T // tm), in_specs=in_specs,
        out_specs=pl.BlockSpec((tm, tn), lambda n, i, o: (i, o[j] * nj + n)))
    return _call(body, name=name, grid_spec=spec, out_shape=jax.ShapeDtypeStruct((T, 4 * Cs), F32),
                 aliases={} if buf is None else {3: 0})(*args)


def _norms(x, mem, norm_g, mem_norm_g):
    D = x.shape[-1]
    hb, hbt = _rms_fwd(x.reshape(-1, D), norm_g.reshape(1, D), "rms_x")
    mhb, _ = _rms_fwd(mem.reshape(-1, D), mem_norm_g.reshape(1, D), "rms_mem")
    return hb, hbt, mhb


def _mix_fwd(proj2, Bl, gq_all, gk_all, conv_w):
    T, IN = proj2.shape
    proj3 = proj2.reshape(Bl, T // Bl, IN)
    os, ls = [], []
    for g, d in enumerate(DILATIONS):
        o, l = _attn_fwd(proj3, gq_all[g:g + 1], gk_all[g:g + 1], g, d)
        os.append(o.reshape(T, GW))
        ls.append(l.reshape(T, GW))
    a = _combine_fwd(os, ls, proj2)
    cc = _conv_fwd(proj3, conv_w).reshape(T, CONVW)
    return os, ls, a, cc


def _weight_grads(x, mem, tgt, norm_g, mem_norm_g, gq_all, gk_all, conv_w, mem_gq, mem_gk, W, pre):
    Bl, S, D = x.shape
    T = Bl * S
    hb, hbt, mhb, proj2, os, ls, a, cc = pre
    IN = proj2.shape[1]
    proj3 = proj2.reshape(Bl, S, IN)
    x2, tgt2 = x.reshape(T, D), tgt.reshape(T, D)
    mem2 = mem.reshape(-1, D)
    ng, mng = norm_g.reshape(1, D), mem_norm_g.reshape(1, D)
    mgq, mgk = mem_gq.reshape(1, MEM_HD), mem_gk.reshape(1, MEM_HD)
    gqs = [gq_all[g:g + 1] for g in range(NGROUP)]
    gks = [gk_all[g:g + 1] for g in range(NGROUP)]

    mkv = _matmul(mhb, W["mem_w_kv"], "nn", F32, name="mem_kv", tm=512, tn=1024, tk=D)
    mkv3 = mkv.reshape(Bl, -1, 2 * MEMW)
    mo = _mem_fwd(proj3, mkv3, mgq, mgk).reshape(T, MEMW)
    merged, mergedt, pa, pc, pm = _merge_fwd(a, cc, mo, W["w_br_attn"], W["w_br_conv"], W["w_br_mem"], proj2)
    dy, dyb, loss = _out_loss(merged, W["w_out"], x2, tgt2)

    G = {}
    G["w_out"] = _matmul(mergedt, dyb, "nn", WIRE_DTYPE, name="dw_out", tm=1024, tn=512, tk=T)
    dpa, dpc, dpm, dg0, dg1, dg2 = _merge_bwd(dyb, W["w_out"], proj2, pa, pc, pm)
    G["w_br_attn"] = _matmul(a, dpa, "tn", WIRE_DTYPE, name="dw_br_attn", tm=512, tn=1024, tk=512)
    G["w_br_conv"] = _matmul(cc, dpc, "tn", WIRE_DTYPE, name="dw_br_conv", tm=1024, tn=1024, tk=512)
    G["w_br_mem"] = _matmul(mo, dpm, "tn", WIRE_DTYPE, name="dw_br_mem", tm=1024, tn=1024, tk=512)
    da = _matmul(dpa, W["w_br_attn"], "nt", F32, name="d_attn", tm=1024, tn=512, tk=D)
    dcc = _matmul(dpc, W["w_br_conv"], "nt", F32, name="d_conv", tm=1024, tn=1024, tk=D)
    dmo = _matmul(dpm, W["w_br_mem"], "nt", F32, name="d_mem", tm=1024, tn=1024, tk=D)

    dos, dls, dza = _combine_bwd(os, ls, proj2, da)
    dqs, dks, dvs, dgq, dgk = [], [], [], [], []
    for g, d in enumerate(DILATIONS):
        dq, dk, dv, gq_g, gk_g = _attn_bwd(proj3, gqs[g], gks[g], os[g].reshape(Bl, S, GW), ls[g].reshape(Bl, S, GW),
                                           dos[g].reshape(Bl, S, GW), dls[g].reshape(Bl, S, GW), g, d)
        dqs.append(dq.reshape(T, GW).astype(MXU_DTYPE))
        dks.append(dk.reshape(T, GW).astype(MXU_DTYPE))
        dvs.append(dv.reshape(T, GW).astype(MXU_DTYPE))
        dgq.append(gq_g)
        dgk.append(gk_g)
    dcb, dcc_, dcv, dzc, dconv_w = _conv_bwd(proj3, conv_w, dcc.reshape(Bl, S, CONVW))
    dmq, dzm, dmkv3, dmgq, dmgk = _mem_bwd(proj3, mkv3, mgq, mgk, dmo.reshape(Bl, S, MEMW))

    dmkv = _cast(dmkv3.reshape(-1, 2 * MEMW), "cast_dmkv")
    G["mem_w_kv"] = _matmul(mhb, dmkv, "tn", WIRE_DTYPE, name="dw_mem_kv", tm=1024, tn=1024, tk=512)
    dmh = _matmul(dmkv, W["mem_w_kv"], "nt", F32, name="d_memh", tm=512, tn=1024, tk=2 * MEMW)
    _, dmng = _rms_bwd(mem2, dmh, mng, None, "rms_mem_bwd")

    dproj = jnp.concatenate(dqs + dks + dvs + [dza] + [t.reshape(T, CONVW) for t in (dcb, dcc_, dcv, dzc)]
                            + [dmq.reshape(T, MEMW), dzm.reshape(T, MEMW), dg0, dg1, dg2], axis=1)
    G["w_in"] = _matmul(hbt, dproj, "nn", WIRE_DTYPE, name="dw_in", tm=1024, tn=512, tk=T)
    small = [loss, None, dmng] + dgq + dgk + [dconv_w.reshape(1, 3 * CONVW), dmgq, dmgk]
    return G, (dproj, x2, ng, dy, small)


def _input_grad(rest, w_in):
    dproj, x2, ng, dy, small = rest
    dh = _matmul(dproj, w_in, "nt", F32, name="d_h", tm=1024, tn=1024, tk=2176)
    grad_x, dng = _rms_bwd(x2, dh, ng, dy, "rms_x_bwd")
    small = [dng if t is None else t for t in small]
    return grad_x, jnp.concatenate(small, axis=1)


def _local_step(x, mem, tgt, norm_g, mem_norm_g, gq_all, gk_all, conv_w, mem_gq, mem_gk, W):
    hb, hbt, mhb = _norms(x, mem, norm_g, mem_norm_g)
    Cs = W["w_in"].shape[1] // 4
    order = jnp.arange(4, dtype=jnp.int32)
    proj2 = None
    for j in range(4):
        proj2 = _proj_chunk(hb, W["w_in"][:, j * Cs:(j + 1) * Cs], order, j, proj2, f"proj_{j}")
    pre = (hb, hbt, mhb, proj2, *_mix_fwd(proj2, x.shape[0], gq_all, gk_all, conv_w))
    G, rest = _weight_grads(x, mem, tgt, norm_g, mem_norm_g, gq_all, gk_all, conv_w, mem_gq, mem_gk, W, pre)
    grad_x, small = _input_grad(rest, W["w_in"])
    return grad_x.reshape(x.shape), G, small


BIG = (("w_in", "col"), ("mem_w_kv", "row"), ("w_br_attn", "col"), ("w_br_conv", "col"),
       ("w_br_mem", "col"), ("w_out", "row"))


def _coords():
    return lax.axis_index("x"), lax.axis_index("y"), lax.axis_index("c")


def _other_chips(x, y):
    return [(1 - x, y), (x, 1 - y), (1 - x, 1 - y)]


def _half(ref, kind, c):
    R, C = ref.shape
    if kind == "col":
        return ref.at[pl.ds(c * (R // 2), R // 2), :]
    return ref.at[:, pl.ds(c * (C // 2), C // 2)]


def _shard(ref, kind, s):
    R, C = ref.shape
    if kind == "col":
        return ref.at[:, pl.ds(s * (C // 4), C // 4)]
    return ref.at[pl.ds(s * (R // 4), R // 4), :]


def _piece(ref, kind, s, c):
    R, C = ref.shape
    if kind == "col":
        return ref.at[pl.ds(c * (R // 2), R // 2), pl.ds(s * (C // 4), C // 4)]
    return ref.at[pl.ds(s * (R // 4), R // 4), pl.ds(c * (C // 2), C // 2)]


def _remote(src, dst, sems_s, sems_r, k, dev):
    return pltpu.make_async_remote_copy(src_ref=src, dst_ref=dst, send_sem=sems_s.at[k], recv_sem=sems_r.at[k],
                                        device_id=dev, device_id_type=MESH)


HBM = pl.BlockSpec(memory_space=pltpu.HBM)
SEM = pl.BlockSpec(memory_space=pltpu.SEMAPHORE)
EFFECT = pltpu.SideEffectType.DATAFLOW_SIDE_EFFECTING


def _hbm(a):
    return pltpu.with_memory_space_constraint(a, pltpu.HBM)


def _start_copies(name, arrays, ncopies, make):
    n = len(arrays)

    def body(*refs):
        for cp in make(refs[:n], refs[n], refs[n + 1]):
            cp.start()

    outs = pl.pallas_call(
        body, name=name,
        out_shape=(pltpu.SemaphoreType.DMA((ncopies,)), pltpu.SemaphoreType.DMA((ncopies,)),
                   *[jax.ShapeDtypeStruct(t.shape, t.dtype) for t in arrays]),
        in_specs=[HBM] * n, out_specs=(SEM, SEM, *([HBM] * n)),
        input_output_aliases={i: i + 2 for i in range(n)},
        compiler_params=pltpu.CompilerParams(has_side_effects=EFFECT),
    )(*[_hbm(t) for t in arrays])
    return outs[0], outs[1], list(outs[2:])


def _wait_copies(name, send, recv, arrays, make, after):
    n = len(arrays)

    def body(*refs):
        for cp in make(refs[:n], refs[n], refs[n + 1]):
            cp.wait_send()
            cp.wait_recv()

    outs = pl.pallas_call(
        body, name=name, out_shape=[jax.ShapeDtypeStruct(t.shape, t.dtype) for t in arrays],
        in_specs=[HBM] * n + [SEM, SEM, ANY], out_specs=[HBM] * n,
        input_output_aliases={i: i for i in range(n)},
        compiler_params=pltpu.CompilerParams(has_side_effects=EFFECT),
    )(*arrays, send, recv, after)
    return list(outs)


def _w_in_copies(relations):
    def make(refs, send, recv):
        x, y, c = _coords()
        me = 2 * x + y
        chips = _other_chips(x, y)
        own, conv = refs[0], refs[1 + len(relations)]
        cps = []
        for i, k in enumerate(relations):
            cps.append(_remote(_half(own, "col", c), _half(refs[1 + i], "col", c), send, recv, 2 * i, (*chips[k], c)))
            mine = _shard(conv, "col", me)
            cps.append(_remote(mine, mine, send, recv, 2 * i + 1, (*chips[k], c)))
        return cps
    return make


def _other_weight_copies(refs, send, recv):
    x, y, c = _coords()
    me = 2 * x + y
    cps = []
    for k, chip in enumerate(_other_chips(x, y)):
        for p, (_, kind) in enumerate(BIG[1:]):
            mine = _piece(refs[p], kind, me, c)
            cps.append(_remote(mine, mine, send, recv, 3 * p + k, (*chip, c)))
    return cps


def _sibling_forward(name, arrays, ncp, halves):
    n = len(arrays)

    def body(*refs):
        outs = refs[n:2 * n]
        send, recv = refs[2 * n:]
        x, y, c = _coords()
        sib = (x, y, 1 - c)
        cps = [_remote(got, got, send, recv, i, sib) for i, got in enumerate(halves(outs, c))]
        for cp in cps:
            cp.start()
        for cp in cps:
            cp.wait_send()
        for i, got in enumerate(halves(outs, 1 - c)):
            _remote(got, got, send, recv, i, sib).wait_recv()

    return pl.pallas_call(
        body, name=name, out_shape=[jax.ShapeDtypeStruct(t.shape, t.dtype) for t in arrays],
        in_specs=[ANY] * n, out_specs=[ANY] * n, input_output_aliases={i: i for i in range(n)},
        scratch_shapes=[pltpu.SemaphoreType.DMA((ncp,)), pltpu.SemaphoreType.DMA((ncp,))],
    )(*arrays)


def _landed_halves(refs, c):
    return [_half(r, "col", c) for r in refs]


def _other_weight_halves(refs, c):
    x, y, _ = _coords()
    out = []
    for chip in _other_chips(x, y):
        s = 2 * chip[0] + chip[1]
        out += [_piece(refs[p], kind, s, c) for p, (_, kind) in enumerate(BIG[1:])]
    return out


def _sibling_exchange(grads):
    n = len(BIG)
    shapes = []
    for (name, kind), g in zip(BIG, grads):
        R, C = g.shape
        shapes.append(jax.ShapeDtypeStruct((R // 2, C) if kind == "col" else (R, C // 2), g.dtype))

    def body(*refs):
        ins, outs = refs[:n], refs[n:2 * n]
        send, recv = refs[2 * n:]
        x, y, c = _coords()
        sib = (x, y, 1 - c)
        cps = [_remote(_half(ins[p], BIG[p][1], 1 - c), outs[p], send, recv, p, sib) for p in range(n)]
        for cp in cps:
            cp.start()
        for cp in cps:
            cp.wait()

    return pl.pallas_call(
        body, name="sibling_exchange", out_shape=shapes, in_specs=[ANY] * n, out_specs=[ANY] * n,
        scratch_shapes=[pltpu.SemaphoreType.DMA((n,)), pltpu.SemaphoreType.DMA((n,))],
    )(*grads)


def _presum(g, got, kind, pos, name):
    R, C = got.shape
    tr, tc = _tile(R, 512, 16), _tile(C, 2048)
    nr, nc = R // tr, C // tc

    def body(pos_ref, a_ref, b_ref, o_ref):
        o_ref[...] = (a_ref[...].astype(F32) + b_ref[...].astype(F32)).astype(o_ref.dtype)

    if kind == "col":
        mine = pl.BlockSpec((tr, tc), lambda i, j, pos_ref: (pos_ref[1] * nr + i, j))
    else:
        mine = pl.BlockSpec((tr, tc), lambda i, j, pos_ref: (i, pos_ref[1] * nc + j))
    blk = pl.BlockSpec((tr, tc), lambda i, j, pos_ref: (i, j))
    spec = pltpu.PrefetchScalarGridSpec(num_scalar_prefetch=1, grid=(nr, nc), in_specs=[mine, blk], out_specs=blk)
    return _call(body, name=name, grid_spec=spec, out_shape=jax.ShapeDtypeStruct((R, C), WIRE_DTYPE))(pos, g, got)


def _chip_copies(refs, send, recv):
    n = len(BIG)
    x, y, c = _coords()
    cps = []
    for k, chip in enumerate(_other_chips(x, y)):
        s = 2 * chip[0] + chip[1]
        for p in range(n):
            cps.append(_remote(_shard(refs[p], BIG[p][1], s), refs[n + p].at[k], send, recv, 3 * p + k, (*chip, c)))
    return cps


def _landing_zones(pres):
    lands = []
    for (name, kind), g in zip(BIG, pres):
        R, C = g.shape
        lands.append(lax.empty((3, R, C // 4) if kind == "col" else (3, R // 4, C), g.dtype))
    return lands


def _reduce_into_shard(slots, pre, kind, pos, name):
    K, R, C = slots.shape
    tr, tc = _tile(R, 512, 16), _tile(C, 2176)
    nr, nc = R // tr, C // tc

    def body(pos_ref, s_ref, p_ref, o_ref):
        acc = p_ref[...].astype(F32)
        for k in range(K):
            acc = acc + s_ref[k].astype(F32)
        o_ref[...] = acc

    if kind == "col":
        own = pl.BlockSpec((tr, tc), lambda i, j, pos_ref: (i, pos_ref[0] * nc + j))
        full, out = (2 * R, C), pl.BlockSpec((tr, tc), lambda i, j, pos_ref: (pos_ref[1] * nr + i, j))
    else:
        own = pl.BlockSpec((tr, tc), lambda i, j, pos_ref: (pos_ref[0] * nr + i, j))
        full, out = (R, 2 * C), pl.BlockSpec((tr, tc), lambda i, j, pos_ref: (i, pos_ref[1] * nc + j))
    spec = pltpu.PrefetchScalarGridSpec(
        num_scalar_prefetch=1, grid=(nr, nc),
        in_specs=[pl.BlockSpec((K, tr, tc), lambda i, j, pos_ref: (0, i, j)), own], out_specs=out)
    return _call(body, name=name, grid_spec=spec, out_shape=jax.ShapeDtypeStruct(full, F32))(pos, slots, pre)


def _share_reduced(reds):
    n = len(BIG)

    def body(*refs):
        outs = refs[n:2 * n]
        send, recv = refs[2 * n:]
        x, y, c = _coords()
        sib = (x, y, 1 - c)
        cps = []
        for p in range(n):
            mine = _half(outs[p], BIG[p][1], c)
            cps.append(_remote(mine, mine, send, recv, p, sib))
        for cp in cps:
            cp.start()
        for cp in cps:
            cp.wait_send()
        for p in range(n):
            got = _half(outs[p], BIG[p][1], 1 - c)
            _remote(got, got, send, recv, p, sib).wait_recv()

    return pl.pallas_call(
        body, name="share_reduced", out_shape=[jax.ShapeDtypeStruct(r.shape, r.dtype) for r in reds],
        in_specs=[ANY] * n, out_specs=[ANY] * n, input_output_aliases={p: p for p in range(n)},
        scratch_shapes=[pltpu.SemaphoreType.DMA((n,)), pltpu.SemaphoreType.DMA((n,))],
    )(*reds)


def _gather_small(pack):
    _, N = pack.shape

    def body(in_ref, out_ref, send, recv, loc):
        x, y, c = _coords()
        me = 4 * x + 2 * y + c
        own = pltpu.make_async_copy(in_ref, out_ref.at[me], loc)
        own.start()
        cps = []
        for k in range(1, 8):
            dev = (x ^ (k >> 2), y ^ ((k >> 1) & 1), c ^ (k & 1))
            cps.append(_remote(in_ref, out_ref.at[me], send, recv, k - 1, dev))
        for cp in cps:
            cp.start()
        for k in range(1, 8):
            src = 4 * (x ^ (k >> 2)) + 2 * (y ^ ((k >> 1) & 1)) + (c ^ (k & 1))
            _remote(in_ref, out_ref.at[src], send, recv, k - 1, (x, y, c)).wait_recv()
        for cp in cps:
            cp.wait_send()
        own.wait()

    return pl.pallas_call(
        body, name="gather_small", out_shape=jax.ShapeDtypeStruct((8, 1, N), pack.dtype),
        in_specs=[ANY], out_specs=ANY,
        scratch_shapes=[pltpu.SemaphoreType.DMA((7,)), pltpu.SemaphoreType.DMA((7,)), pltpu.SemaphoreType.DMA(())],
    )(pack)


def _sum_small(slots):
    K, _, N = slots.shape

    def body(s_ref, o_ref):
        acc = s_ref[0]
        for k in range(1, K):
            acc = acc + s_ref[k]
        o_ref[...] = acc

    return _call(body, name="sum_small", in_specs=[pl.BlockSpec(memory_space=pltpu.VMEM)],
                 out_specs=pl.BlockSpec(memory_space=pltpu.VMEM), out_shape=jax.ShapeDtypeStruct((1, N), F32))(slots)


def _adamw(w, g, m, v, name):
    R, C = w.shape
    tr, tc = _tile(R, 256, 8), _tile(C, 2176)

    def body(w_ref, g_ref, m_ref, v_ref, d_ref, nm_ref, nv_ref):
        gv = g_ref[...]
        nm = ADAM_B1 * m_ref[...] + (1.0 - ADAM_B1) * gv
        nv = ADAM_B2 * v_ref[...] + (1.0 - ADAM_B2) * gv * gv
        m_hat = nm / (1.0 - ADAM_B1 ** ADAM_STEP)
        v_hat = nv / (1.0 - ADAM_B2 ** ADAM_STEP)
        d_ref[...] = -ADAM_LR * (m_hat / (jnp.sqrt(v_hat) + ADAM_EPS) + ADAM_WD * w_ref[...])
        nm_ref[...] = nm
        nv_ref[...] = nv

    spec = pl.BlockSpec((tr, tc), lambda i, j: (i, j))
    shp = jax.ShapeDtypeStruct((R, C), F32)
    return _call(body, name=name, grid=(R // tr, C // tc), in_specs=[spec] * 4, out_specs=[spec] * 3,
                 out_shape=[shp] * 3)(w, g, m, v)


SMALL = ("norm_g", "mem_norm_g", "attn_q_norm", "attn_k_norm", "conv_w", "mem_q_norm", "mem_k_norm")
WEIGHTS = ("norm_g", "mem_norm_g", "w_in", "attn_q_norm", "attn_k_norm", "conv_w", "mem_w_kv", "mem_q_norm",
           "mem_k_norm", "w_br_attn", "w_br_conv", "w_br_mem", "w_out")


def kernel(x, mem, norm_g, mem_norm_g, w_in, attn_q_norm, attn_k_norm, conv_w, mem_w_kv, mem_q_norm, mem_k_norm, w_br_attn, w_br_conv, w_br_mem, w_out, loss_target, m_norm_g, m_mem_norm_g, m_w_in, m_attn_q_norm, m_attn_k_norm, m_conv_w, m_mem_w_kv, m_mem_q_norm, m_mem_k_norm, m_w_br_attn, m_w_br_conv, m_w_br_mem, m_w_out, v_norm_g, v_mem_norm_g, v_w_in, v_attn_q_norm, v_attn_k_norm, v_conv_w, v_mem_w_kv, v_mem_q_norm, v_mem_k_norm, v_w_br_attn, v_w_br_conv, v_w_br_mem, v_w_out):
    w = dict(norm_g=norm_g, mem_norm_g=mem_norm_g, w_in=w_in, attn_q_norm=attn_q_norm, attn_k_norm=attn_k_norm,
             conv_w=conv_w, mem_w_kv=mem_w_kv, mem_q_norm=mem_q_norm, mem_k_norm=mem_k_norm, w_br_attn=w_br_attn,
             w_br_conv=w_br_conv, w_br_mem=w_br_mem, w_out=w_out)
    m = dict(norm_g=m_norm_g, mem_norm_g=m_mem_norm_g, w_in=m_w_in, attn_q_norm=m_attn_q_norm,
             attn_k_norm=m_attn_k_norm, conv_w=m_conv_w, mem_w_kv=m_mem_w_kv, mem_q_norm=m_mem_q_norm,
             mem_k_norm=m_mem_k_norm, w_br_attn=m_w_br_attn, w_br_conv=m_w_br_conv, w_br_mem=m_w_br_mem, w_out=m_w_out)
    v = dict(norm_g=v_norm_g, mem_norm_g=v_mem_norm_g, w_in=v_w_in, attn_q_norm=v_attn_q_norm,
             attn_k_norm=v_attn_k_norm, conv_w=v_conv_w, mem_w_kv=v_mem_w_kv, mem_q_norm=v_mem_q_norm,
             mem_k_norm=v_mem_k_norm, w_br_attn=v_w_br_attn, w_br_conv=v_w_br_conv, w_br_mem=v_w_br_mem, w_out=v_w_out)
    Bl, _, D = x.shape
    cx, cy = lax.axis_index("x"), lax.axis_index("y")
    chip = 2 * cx + cy
    pos = jnp.stack([chip, lax.axis_index("c")]).astype(jnp.int32)
    order = jnp.stack([chip] + [2 * a + b for a, b in _other_chips(cx, cy)]).astype(jnp.int32)
    n = len(BIG)

    own = _cast(w["w_in"], "cast_w_in")
    lands = [lax.empty(own.shape, own.dtype) for _ in range(3)]
    conv_full = _place_shard(conv_w, "col", pos, F32, "place_conv_w")
    others = [_place_shard(w[name], kind, pos, WIRE_DTYPE, "place_" + name) for name, kind in BIG[1:]]
    hb, hbt, mhb = _norms(x, mem, norm_g, mem_norm_g)

    near = _w_in_copies((0, 1))
    send, recv, (own, l0, l1, conv_full) = _start_copies("gather_near_start", [own, lands[0], lands[1], conv_full], 4, near)
    proj = _proj_chunk(hb, own, order, 0, None, "proj_own")
    own, l0, l1, conv_full = _wait_copies("gather_near_wait", send, recv, [own, l0, l1, conv_full], near, proj)
    l0, l1 = _sibling_forward("gather_near_forward", [l0, l1], 2, _landed_halves)

    far = _w_in_copies((2,))
    send, recv, (own, l2, conv_full, l0) = _start_copies("gather_far_start", [own, lands[2], conv_full, l0], 2, far)
    proj = _proj_chunk(hb, l0, order, 1, proj, "proj_near_x")
    proj = _proj_chunk(hb, l1, order, 2, proj, "proj_near_y")
    own, l2, conv_full, l0 = _wait_copies("gather_far_wait", send, recv, [own, l2, conv_full, l0], far, proj)
    l2, = _sibling_forward("gather_far_forward", [l2], 1, _landed_halves)

    send, recv, (*others, l2) = _start_copies("gather_rest_start", [*others, l2], 3 * (n - 1), _other_weight_copies)
    proj = _proj_chunk(hb, l2, order, 3, proj, "proj_far")
    mixed = _mix_fwd(proj, Bl, attn_q_norm, attn_k_norm, conv_full)
    *others, l2 = _wait_copies("gather_rest_wait", send, recv, [*others, l2], _other_weight_copies, mixed[2])
    others = _sibling_forward("gather_rest_forward", others, 3 * (n - 1), _other_weight_halves)
    W = {name: others[p] for p, (name, _) in enumerate(BIG[1:])}

    G, rest = _weight_grads(x, mem, loss_target, norm_g, mem_norm_g, attn_q_norm, attn_k_norm, conv_full,
                            mem_q_norm, mem_k_norm, W, (hb, hbt, mhb, proj, *mixed))

    w_in_full = None
    for j, shard in enumerate((own, l0, l1, l2)):
        w_in_full = _place_shard(shard, "col", order, WIRE_DTYPE, f"assemble_w_in_{j}", slot=j, into=w_in_full)

    parts = [G[name] for name, _ in BIG]
    got = _sibling_exchange(parts)
    pres = [_presum(parts[p], got[p], kind, pos, "presum_" + name) for p, (name, kind) in enumerate(BIG)]
    send, recv, thru = _start_copies("chip_exchange_start", [*pres, *_landing_zones(pres), rest[0]], 3 * n, _chip_copies)
    grad_x, small = _input_grad((thru[2 * n], *rest[1:]), w_in_full)
    thru = _wait_copies("chip_exchange_wait", send, recv, thru[:2 * n], _chip_copies, grad_x)
    pres, slots = thru[:n], thru[n:]
    grad_x = grad_x.reshape(x.shape)
    reds = [_reduce_into_shard(slots[p], pres[p], kind, pos, "reduce_" + name) for p, (name, kind) in enumerate(BIG)]
    grads = dict(zip([name for name, _ in BIG], _share_reduced(reds)))

    tot = _sum_small(_gather_small(small))[0]
    loss = tot[0]
    off = 128
    for name, size in (("norm_g", D), ("mem_norm_g", D), ("attn_q_norm", NGROUP * HEAD), ("attn_k_norm", NGROUP * HEAD),
                       ("conv_w", 3 * CONVW), ("mem_q_norm", MEM_HD), ("mem_k_norm", MEM_HD)):
        grads[name] = tot[off:off + size]
        off += size
    cw = conv_w.shape[1]
    grads["conv_w"] = lax.dynamic_slice(grads["conv_w"].reshape(3, CONVW), (0, chip * cw), (3, cw))
    for name in SMALL:
        grads[name] = grads[name].reshape(w[name].shape)

    delta, new_m, new_v = {}, {}, {}
    for name, _ in BIG:
        delta[name], new_m[name], new_v[name] = _adamw(w[name], grads[name], m[name], v[name], "adamw_" + name)

    def packed(t):
        return jnp.concatenate([t[name].reshape(1, -1) for name in SMALL], axis=1)

    ds, ms, vs = _adamw(packed(w), packed(grads), packed(m), packed(v), "adamw_small")
    off = 0
    for name in SMALL:
        size = w[name].size
        delta[name] = ds[0, off:off + size].reshape(w[name].shape)
        new_m[name] = ms[0, off:off + size].reshape(w[name].shape)
        new_v[name] = vs[0, off:off + size].reshape(w[name].shape)
        off += size

    return (loss, grad_x, *[grads[n] for n in WEIGHTS], *[delta[n] for n in WEIGHTS],
            *[new_m[n] for n in WEIGHTS], *[new_v[n] for n in WEIGHTS])
```

```python
import functools

import jax
import jax.numpy as jnp
from jax import lax
from jax.experimental import pallas as pl
from jax.experimental.pallas import tpu as pltpu

F32 = jnp.float32
MXU_DTYPE = jnp.bfloat16
WIRE_DTYPE = jnp.bfloat16
EPS = 1e-6
NEG = -1e30

HEAD = 128
HPG = 4
GW = HPG * HEAD
DILATIONS = (1, 4, 16)
NGROUP = len(DILATIONS)
BLK = 128
QKV = NGROUP * GW
CONVW = 1024
MEM_HEADS = 4
MEM_HD = 256
MEMW = MEM_HEADS * MEM_HD
Q0, K0, V0 = 0, QKV, 2 * QKV
ZA = 3 * QKV
CB, CC, CV, ZC = ZA + GW, ZA + GW + CONVW, ZA + GW + 2 * CONVW, ZA + GW + 3 * CONVW
MQ = ZC + CONVW
ZM = MQ + MEMW
G0 = ZM + MEMW

ADAM_LR, ADAM_B1, ADAM_B2, ADAM_EPS, ADAM_WD, ADAM_STEP = 0.001, 0.9, 0.999, 1e-08, 0.01, 10

VMEM_LIMIT = 56 * 1024 * 1024
MESH = pl.DeviceIdType.MESH
ANY = pl.BlockSpec(memory_space=pl.ANY)


def _tile(n, pref, mult=128):
    t = min(pref, n)
    while t > mult and (n % t or t % mult):
        t -= mult
    assert n % t == 0, (n, pref)
    return t


def _call(body, *, name, out_shape, grid=(), in_specs=None, out_specs=None, scratch_shapes=(),
          aliases=None, grid_spec=None):
    kw = {}
    if grid_spec is not None:
        kw["grid_spec"] = grid_spec
        ngrid = len(grid_spec.grid)
    else:
        kw.update(grid=grid, in_specs=in_specs, out_specs=out_specs, scratch_shapes=list(scratch_shapes))
        ngrid = len(grid)
    params = pltpu.CompilerParams(dimension_semantics=("arbitrary",) * ngrid, vmem_limit_bytes=VMEM_LIMIT)
    return pl.pallas_call(body, name=name, out_shape=out_shape, compiler_params=params,
                          input_output_aliases=aliases or {}, **kw)


_DIMS = {"nn": (((1,), (0,)), ((), ())), "nt": (((1,), (1,)), ((), ())), "tn": (((0,), (0,)), ((), ()))}


def _mxu(a, b, mode):
    return lax.dot_general(a.astype(MXU_DTYPE), b.astype(MXU_DTYPE), _DIMS[mode], preferred_element_type=F32)


@functools.partial(jax.custom_vjp, nondiff_argnums=(2,))
def _dot(a, b, mode):
    return _mxu(a, b, mode)


def _dot_fwd(a, b, mode):
    return _mxu(a, b, mode), (a, b)


def _dot_bwd(mode, res, g):
    a, b = res
    if mode == "nn":
        return _mxu(g, b, "nt"), _mxu(a, g, "tn")
    if mode == "nt":
        return _mxu(g, b, "nn"), _mxu(g, a, "tn")
    return _mxu(b, g, "nt"), _mxu(a, g, "nn")


_dot.defvjp(_dot_fwd, _dot_bwd)


def _sig(z):
    return 1.0 / (1.0 + jnp.exp(-z))


def _silu(z):
    return z * _sig(z)


def _rms_rows(t, g):
    return t * lax.rsqrt(jnp.mean(t * t, axis=-1, keepdims=True) + EPS) * g


def _attn_block(q, k2, v2, gq, gk, first):
    qn = _rms_rows(q, gq)
    kn = _rms_rows(k2, gk)
    s = jnp.where(_band_mask(first), _dot(qn, kn, "nt") * (HEAD ** -0.5), NEG)
    m = lax.stop_gradient(jnp.max(s, axis=-1, keepdims=True))
    p = jnp.exp(s - m)
    den = jnp.sum(p, axis=-1, keepdims=True)
    o = _dot(p, v2, "nn") / den
    return o, m + jnp.log(den)


def _band_mask(first):
    a = lax.broadcasted_iota(jnp.int32, (BLK, 2 * BLK), 0)
    b = lax.broadcasted_iota(jnp.int32, (BLK, 2 * BLK), 1)
    return (b >= a) & (b <= a + BLK) & (b >= jnp.where(first, BLK, 0))


def _norm_parts(t):
    r = lax.rsqrt(jnp.mean(t * t, axis=-1, keepdims=True) + EPS)
    return r, t * r


def _norm_bwd(dn, g, r, th):
    dth = dn * g
    return r * (dth - th * jnp.mean(dth * th, axis=-1, keepdims=True)), jnp.sum(dn * th, axis=0, keepdims=True)


def _attn_block_bwd(q, k2, v2, gq, gk, first, do, o, lse, dlse):
    scale = HEAD ** -0.5
    rq, qh = _norm_parts(q)
    rk, kh = _norm_parts(k2)
    qn, kn = qh * gq, kh * gk
    s = jnp.where(_band_mask(first), _mxu(qn, kn, "nt") * scale, NEG)
    p = jnp.exp(s - lse)
    ds = p * (_mxu(do, v2, "nt") + (dlse - jnp.sum(do * o, axis=-1, keepdims=True))) * scale
    dq, dgq = _norm_bwd(_mxu(ds, kn, "nn"), gq, rq, qh)
    dk2, dgk = _norm_bwd(_mxu(ds, qn, "tn"), gk, rk, kh)
    return dq, dk2, _mxu(p, do, "tn"), dgq, dgk


def _combine(o1, o2, o3, l1, l2, l3, z):
    m = lax.stop_gradient(jnp.maximum(jnp.maximum(l1, l2), l3))
    e1, e2, e3 = jnp.exp(l1 - m), jnp.exp(l2 - m), jnp.exp(l3 - m)
    return (e1 * o1 + e2 * o2 + e3 * o3) / (e1 + e2 + e3) * _silu(z)


def _mem_block(q, z, kv, gq, gk):
    outs = []
    for h in range(MEM_HEADS):
        sl = slice(h * MEM_HD, (h + 1) * MEM_HD)
        qn = _rms_rows(q[:, sl], gq)
        kn = _rms_rows(kv[:, sl], gk)
        s = _dot(qn, kn, "nt") * (MEM_HD ** -0.5)
        m = lax.stop_gradient(jnp.max(s, axis=-1, keepdims=True))
        p = jnp.exp(s - m)
        den = jnp.sum(p, axis=-1, keepdims=True)
        outs.append(_dot(p, kv[:, MEMW + h * MEM_HD:MEMW + (h + 1) * MEM_HD], "nn") / den)
    return jnp.concatenate(outs, axis=-1) * _silu(z)


def _cast(w, name):
    R, C = w.shape
    tr, tc = _tile(R, 512, 8), _tile(C, 2176)

    def body(w_ref, o_ref):
        o_ref[...] = w_ref[...].astype(o_ref.dtype)

    spec = pl.BlockSpec((tr, tc), lambda i, j: (i, j))
    return _call(body, name=name, grid=(R // tr, C // tc), in_specs=[spec], out_specs=spec,
                 out_shape=jax.ShapeDtypeStruct((R, C), WIRE_DTYPE))(w)


def _place_shard(w, kind, pos, dtype, name, slot=0, into=None):
    R, C = w.shape
    tr, tc = _tile(R, 512, 8), _tile(C, 2176)
    nr, nc = R // tr, C // tc

    def body(pos_ref, w_ref, *rest):
        rest[-1][...] = w_ref[...].astype(rest[-1].dtype)

    if kind == "col":
        full, out = (R, 4 * C), pl.BlockSpec((tr, tc), lambda i, j, pos_ref: (i, pos_ref[slot] * nc + j))
    else:
        full, out = (4 * R, C), pl.BlockSpec((tr, tc), lambda i, j, pos_ref: (pos_ref[slot] * nr + i, j))
    in_specs, args = [pl.BlockSpec((tr, tc), lambda i, j, pos_ref: (i, j))], [pos, w]
    if into is not None:
        in_specs.append(ANY)
        args.append(into)
    spec = pltpu.PrefetchScalarGridSpec(num_scalar_prefetch=1, grid=(nr, nc), in_specs=in_specs, out_specs=out)
    return _call(body, name=name, grid_spec=spec, out_shape=jax.ShapeDtypeStruct(full, dtype),
                 aliases={} if into is None else {2: 0})(*args)


def _matmul(a, b, mode, out_dtype, *, name, tm=512, tn=512, tk=512):
    if mode == "nn":
        (M, K), (_, N) = a.shape, b.shape
    elif mode == "nt":
        (M, K), (N, _) = a.shape, b.shape
    else:
        (K, M), (_, N) = a.shape, b.shape
    tm, tn, tk = _tile(M, tm), _tile(N, tn), _tile(K, tk)
    nk = K // tk

    def body(a_ref, b_ref, o_ref, *acc):
        part = lax.dot_general(a_ref[...], b_ref[...], _DIMS[mode], preferred_element_type=F32)
        if nk == 1:
            o_ref[...] = part.astype(o_ref.dtype)
            return
        acc_ref, = acc
        k = pl.program_id(2)

        @pl.when(k == 0)
        def _():
            acc_ref[...] = part

        @pl.when(k > 0)
        def _():
            acc_ref[...] += part

        @pl.when(k == nk - 1)
        def _():
            o_ref[...] = acc_ref[...].astype(o_ref.dtype)

    a_spec = pl.BlockSpec((tk, tm), lambda i, j, k: (k, i)) if mode == "tn" else pl.BlockSpec((tm, tk), lambda i, j, k: (i, k))
    b_spec = pl.BlockSpec((tn, tk), lambda i, j, k: (j, k)) if mode == "nt" else pl.BlockSpec((tk, tn), lambda i, j, k: (k, j))
    return _call(body, name=name, grid=(M // tm, N // tn, nk), in_specs=[a_spec, b_spec],
                 out_specs=pl.BlockSpec((tm, tn), lambda i, j, k: (i, j)),
                 out_shape=jax.ShapeDtypeStruct((M, N), out_dtype),
                 scratch_shapes=[] if nk == 1 else [pltpu.VMEM((tm, tn), F32)])(a, b)


def _rms_fwd(x, g, name):
    R, D = x.shape
    tr = _tile(R, 512)

    def body(x_ref, g_ref, o_ref, t_ref):
        y = _rms_rows(x_ref[...], g_ref[...])
        o_ref[...] = y.astype(o_ref.dtype)
        t_ref[...] = y.T.astype(t_ref.dtype)

    row = pl.BlockSpec((tr, D), lambda i: (i, 0))
    return _call(body, name=name, grid=(R // tr,), in_specs=[row, pl.BlockSpec((1, D), lambda i: (0, 0))],
                 out_specs=[row, pl.BlockSpec((D, tr), lambda i: (0, i))],
                 out_shape=[jax.ShapeDtypeStruct((R, D), MXU_DTYPE), jax.ShapeDtypeStruct((D, R), MXU_DTYPE)])(x, g)


def _rms_bwd(x, dh, g, dy, name):
    R, D = x.shape
    tr = _tile(R, 256)
    with_dx = dy is not None

    def body(*refs):
        if with_dx:
            x_ref, dh_ref, g_ref, dy_ref, dx_ref, dg_ref = refs
        else:
            x_ref, dh_ref, g_ref, dg_ref = refs
        xv, dhv = x_ref[...], dh_ref[...]
        r = lax.rsqrt(jnp.mean(xv * xv, axis=-1, keepdims=True) + EPS)
        xh = xv * r

        @pl.when(pl.program_id(0) == 0)
        def _():
            dg_ref[...] = jnp.zeros_like(dg_ref)

        dg_ref[...] += jnp.sum(dhv * xh, axis=0, keepdims=True)
        if with_dx:
            dxh = dhv * g_ref[...]
            dx_ref[...] = dy_ref[...] + r * (dxh - xh * jnp.mean(dxh * xh, axis=-1, keepdims=True))

    row = pl.BlockSpec((tr, D), lambda i: (i, 0))
    vec = pl.BlockSpec((1, D), lambda i: (0, 0))
    dg_shape = jax.ShapeDtypeStruct((1, D), F32)
    if with_dx:
        return _call(body, name=name, grid=(R // tr,), in_specs=[row, row, vec, row], out_specs=[row, vec],
                     out_shape=[jax.ShapeDtypeStruct((R, D), F32), dg_shape])(x, dh, g, dy)
    return None, _call(body, name=name, grid=(R // tr,), in_specs=[row, row, vec], out_specs=vec,
                       out_shape=dg_shape)(x, dh, g)


def _attn_geom(g, d):
    hc = HPG if d == 1 else 1
    cw = hc * HEAD
    cq, ck, cv = (Q0 + g * GW) // cw, (K0 + g * GW) // cw, (V0 + g * GW) // cw
    return (1, BLK * d, cw), hc, HPG // hc, cq, ck, cv


def _rows(ref, r, d, sl):
    if d == 1:
        return ref[0, :, sl]
    return ref.at[0][pl.ds(r, BLK, stride=d), sl]


def _set_rows(ref, r, d, sl, val):
    if d == 1:
        ref[0, :, sl] = val
    else:
        ref.at[0][pl.ds(r, BLK, stride=d), sl] = val


def _attn_fwd(proj3, gq, gk, g, d):
    Bl, S, _ = proj3.shape
    blk, hc, ncb, cq, ck, cv = _attn_geom(g, d)
    nb = S // blk[1]

    def body(q_ref, kp_ref, kc_ref, vp_ref, vc_ref, gq_ref, gk_ref, o_ref, lse_ref):
        first = pl.program_id(2) == 0
        for r in range(d):
            for h in range(hc):
                sl = slice(h * HEAD, (h + 1) * HEAD)
                k2 = jnp.concatenate([_rows(kp_ref, r, d, sl), _rows(kc_ref, r, d, sl)], axis=0)
                v2 = jnp.concatenate([_rows(vp_ref, r, d, sl), _rows(vc_ref, r, d, sl)], axis=0)
                o, lse = _attn_block(_rows(q_ref, r, d, sl), k2, v2, gq_ref[...], gk_ref[...], first)
                _set_rows(o_ref, r, d, sl, o)
                _set_rows(lse_ref, r, d, sl, jnp.broadcast_to(lse, (BLK, HEAD)))

    def cur(c0):
        return pl.BlockSpec(blk, lambda b, j, i: (b, i, c0 + j))

    def prev(c0):
        return pl.BlockSpec(blk, lambda b, j, i: (b, jnp.maximum(i - 1, 0), c0 + j))

    vec = pl.BlockSpec((1, HEAD), lambda b, j, i: (0, 0))
    out = pl.BlockSpec(blk, lambda b, j, i: (b, i, j))
    shp = jax.ShapeDtypeStruct((Bl, S, GW), F32)
    return _call(body, name=f"attn_fwd_g{g}", grid=(Bl, ncb, nb),
                 in_specs=[cur(cq), prev(ck), cur(ck), prev(cv), cur(cv), vec, vec],
                 out_specs=[out, out], out_shape=[shp, shp])(proj3, proj3, proj3, proj3, proj3, gq, gk)


def _attn_bwd(proj3, gq, gk, o3, l3, do3, dl3, g, d):
    Bl, S, _ = proj3.shape
    blk, hc, ncb, cq, ck, cv = _attn_geom(g, d)
    nb = S // blk[1]

    def body(q_ref, kp_ref, kc_ref, vp_ref, vc_ref, gq_ref, gk_ref, o_ref, l_ref, do_ref, dl_ref,
             dq_ref, dk_ref, dv_ref, dgq_ref, dgk_ref, ck_ref, cv_ref):
        i = pl.program_id(2)
        first = i == 0

        @pl.when((pl.program_id(0) == 0) & (pl.program_id(1) == 0) & first)
        def _():
            dgq_ref[...] = jnp.zeros_like(dgq_ref)
            dgk_ref[...] = jnp.zeros_like(dgk_ref)

        @pl.when(first)
        def _():
            ck_ref[...] = jnp.zeros_like(ck_ref)
            cv_ref[...] = jnp.zeros_like(cv_ref)

        @pl.when(i < nb)
        def _():
            dgq, dgk = jnp.zeros((1, HEAD), F32), jnp.zeros((1, HEAD), F32)
            for r in range(d):
                rs = slice(r * BLK, (r + 1) * BLK)
                for h in range(hc):
                    sl = slice(h * HEAD, (h + 1) * HEAD)
                    k2 = jnp.concatenate([_rows(kp_ref, r, d, sl), _rows(kc_ref, r, d, sl)], axis=0)
                    v2 = jnp.concatenate([_rows(vp_ref, r, d, sl), _rows(vc_ref, r, d, sl)], axis=0)
                    dq, dk2, dv2, a, b = _attn_block_bwd(
                        _rows(q_ref, r, d, sl), k2, v2, gq_ref[...], gk_ref[...], first, _rows(do_ref, r, d, sl),
                        _rows(o_ref, r, d, sl), _rows(l_ref, r, d, sl)[:, :1], _rows(dl_ref, r, d, sl)[:, :1])
                    _set_rows(dq_ref, r, d, sl, dq)
                    _set_rows(dk_ref, r, d, sl, ck_ref[rs, sl] + dk2[:BLK])
                    _set_rows(dv_ref, r, d, sl, cv_ref[rs, sl] + dv2[:BLK])
                    ck_ref[rs, sl] = dk2[BLK:]
                    cv_ref[rs, sl] = dv2[BLK:]
                    dgq, dgk = dgq + a, dgk + b
            dgq_ref[...] += dgq
            dgk_ref[...] += dgk

        @pl.when(i == nb)
        def _():
            for r in range(d):
                rs = slice(r * BLK, (r + 1) * BLK)
                _set_rows(dk_ref, r, d, slice(None), ck_ref[rs, :])
                _set_rows(dv_ref, r, d, slice(None), cv_ref[rs, :])

    def cur(c0):
        return pl.BlockSpec(blk, lambda b, j, i: (b, jnp.minimum(i, nb - 1), c0 + j))

    def prev(c0):
        return pl.BlockSpec(blk, lambda b, j, i: (b, jnp.clip(i - 1, 0, nb - 1), c0 + j))

    vec = pl.BlockSpec((1, HEAD), lambda b, j, i: (0, 0))
    at_q = pl.BlockSpec(blk, lambda b, j, i: (b, jnp.minimum(i, nb - 1), j))
    at_k = pl.BlockSpec(blk, lambda b, j, i: (b, jnp.maximum(i - 1, 0), j))
    shp = jax.ShapeDtypeStruct((Bl, S, GW), F32)
    gshp = jax.ShapeDtypeStruct((1, HEAD), F32)
    return _call(body, name=f"attn_bwd_g{g}", grid=(Bl, ncb, nb + 1),
                 in_specs=[cur(cq), prev(ck), cur(ck), prev(cv), cur(cv), vec, vec, at_q, at_q, at_q, at_q],
                 out_specs=[at_q, at_k, at_k, vec, vec], out_shape=[shp, shp, shp, gshp, gshp],
                 scratch_shapes=[pltpu.VMEM(blk[1:], F32), pltpu.VMEM(blk[1:], F32)],
                 )(proj3, proj3, proj3, proj3, proj3, gq, gk, o3, l3, do3, dl3)


def _combine_fwd(os, ls, proj2):
    T = proj2.shape[0]
    tr = _tile(T, 512)

    def body(o1, o2, o3, l1, l2, l3, z, a_ref):
        a_ref[...] = _combine(o1[...], o2[...], o3[...], l1[...], l2[...], l3[...], z[...]).astype(a_ref.dtype)

    row = pl.BlockSpec((tr, GW), lambda i: (i, 0))
    return _call(body, name="combine_fwd", grid=(T // tr,),
                 in_specs=[row] * 6 + [pl.BlockSpec((tr, GW), lambda i: (i, ZA // GW))], out_specs=row,
                 out_shape=jax.ShapeDtypeStruct((T, GW), MXU_DTYPE))(*os, *ls, proj2)


def _combine_bwd(os, ls, proj2, da):
    T = proj2.shape[0]
    tr = _tile(T, 256)

    def body(o1, o2, o3, l1, l2, l3, z, da_ref, d1, d2, d3, e1, e2, e3, dz_ref):
        _, vjp = jax.vjp(_combine, o1[...], o2[...], o3[...], l1[...], l2[...], l3[...], z[...])
        go1, go2, go3, gl1, gl2, gl3, gz = vjp(da_ref[...])
        d1[...], d2[...], d3[...] = go1, go2, go3
        dz_ref[...] = gz.astype(dz_ref.dtype)
        for ref, gl in ((e1, gl1), (e2, gl2), (e3, gl3)):
            for h in range(HPG):
                sl = slice(h * HEAD, (h + 1) * HEAD)
                ref[:, sl] = jnp.broadcast_to(jnp.sum(gl[:, sl], axis=-1, keepdims=True), (tr, HEAD))

    row = pl.BlockSpec((tr, GW), lambda i: (i, 0))
    f = jax.ShapeDtypeStruct((T, GW), F32)
    outs = _call(body, name="combine_bwd", grid=(T // tr,),
                 in_specs=[row] * 6 + [pl.BlockSpec((tr, GW), lambda i: (i, ZA // GW)), row],
                 out_specs=[row] * 7, out_shape=[f] * 6 + [jax.ShapeDtypeStruct((T, GW), MXU_DTYPE)],
                 )(*os, *ls, proj2, da)
    return outs[:3], outs[3:6], outs[6]


def _shift_down(u, j, t):
    return jnp.where(t >= j, pltpu.roll(u, j, 0), 0.0)


def _shift_up(u, j, t):
    n = u.shape[0]
    return jnp.where(t < n - j, pltpu.roll(u, n - j, 0), 0.0)


def _conv_specs(Bl, S, cw):
    def sec(c0):
        return pl.BlockSpec((1, S, cw), lambda j, b: (b, 0, c0 // cw + j))
    return [sec(CB), sec(CC), sec(CV), sec(ZC)], pl.BlockSpec((3, cw), lambda j, b: (0, j))


def _conv_fwd(proj3, conv_w):
    Bl, S, _ = proj3.shape
    cw = 256
    secs, wspec = _conv_specs(Bl, S, cw)

    def body(b_ref, c_ref, v_ref, z_ref, w_ref, o_ref):
        t = lax.broadcasted_iota(jnp.int32, (S, cw), 0)
        u = c_ref[0] * v_ref[0]
        y = w_ref[0:1, :] * u + w_ref[1:2, :] * _shift_down(u, 1, t) + w_ref[2:3, :] * _shift_down(u, 2, t)
        o_ref[0] = (b_ref[0] * y * _silu(z_ref[0])).astype(o_ref.dtype)

    return _call(body, name="conv_fwd", grid=(CONVW // cw, Bl), in_specs=secs + [wspec],
                 out_specs=pl.BlockSpec((1, S, cw), lambda j, b: (b, 0, j)),
                 out_shape=jax.ShapeDtypeStruct((Bl, S, CONVW), MXU_DTYPE))(proj3, proj3, proj3, proj3, conv_w)


def _conv_bwd(proj3, conv_w, dcc3):
    Bl, S, _ = proj3.shape
    cw = 256
    secs, wspec = _conv_specs(Bl, S, cw)

    def body(b_ref, c_ref, v_ref, z_ref, w_ref, d_ref, db_ref, dc_ref, dv_ref, dz_ref, dw_ref):
        t = lax.broadcasted_iota(jnp.int32, (S, cw), 0)
        bv, cv, vv, zv, dv = b_ref[0], c_ref[0], v_ref[0], z_ref[0], d_ref[0]
        u = cv * vv
        u1, u2 = _shift_down(u, 1, t), _shift_down(u, 2, t)
        y = w_ref[0:1, :] * u + w_ref[1:2, :] * u1 + w_ref[2:3, :] * u2
        sg = _sig(zv)
        sz = zv * sg
        gy = dv * bv * sz
        db_ref[0] = (dv * y * sz).astype(db_ref.dtype)
        dz_ref[0] = (dv * bv * y * sg * (1.0 + zv * (1.0 - sg))).astype(dz_ref.dtype)
        du = w_ref[0:1, :] * gy + w_ref[1:2, :] * _shift_up(gy, 1, t) + w_ref[2:3, :] * _shift_up(gy, 2, t)
        dc_ref[0] = (du * vv).astype(dc_ref.dtype)
        dv_ref[0] = (du * cv).astype(dv_ref.dtype)

        @pl.when(pl.program_id(1) == 0)
        def _():
            dw_ref[...] = jnp.zeros_like(dw_ref)

        dw_ref[0:1, :] += jnp.sum(gy * u, axis=0, keepdims=True)
        dw_ref[1:2, :] += jnp.sum(gy * u1, axis=0, keepdims=True)
        dw_ref[2:3, :] += jnp.sum(gy * u2, axis=0, keepdims=True)

    blk = pl.BlockSpec((1, S, cw), lambda j, b: (b, 0, j))
    shp = jax.ShapeDtypeStruct((Bl, S, CONVW), MXU_DTYPE)
    return _call(body, name="conv_bwd", grid=(CONVW // cw, Bl), in_specs=secs + [wspec, blk],
                 out_specs=[blk] * 4 + [wspec], out_shape=[shp] * 4 + [jax.ShapeDtypeStruct((3, CONVW), F32)],
                 )(proj3, proj3, proj3, proj3, conv_w, dcc3)


def _mem_specs(S, tq):
    q = pl.BlockSpec((1, tq, MEMW), lambda b, j: (b, j, MQ // MEMW))
    z = pl.BlockSpec((1, tq, MEMW), lambda b, j: (b, j, ZM // MEMW))
    kv = pl.BlockSpec((1, MEM_HD, 2 * MEMW), lambda b, j: (b, 0, 0))
    vec = pl.BlockSpec((1, MEM_HD), lambda b, j: (0, 0))
    blk = pl.BlockSpec((1, tq, MEMW), lambda b, j: (b, j, 0))
    return q, z, kv, vec, blk


def _mem_fwd(proj3, mkv3, gq, gk):
    Bl, S, _ = proj3.shape
    tq = _tile(S, 512)
    q, z, kv, vec, blk = _mem_specs(S, tq)

    def body(q_ref, z_ref, kv_ref, gq_ref, gk_ref, o_ref):
        o_ref[0] = _mem_block(q_ref[0], z_ref[0], kv_ref[0], gq_ref[...], gk_ref[...]).astype(o_ref.dtype)

    return _call(body, name="mem_fwd", grid=(Bl, S // tq), in_specs=[q, z, kv, vec, vec], out_specs=blk,
                 out_shape=jax.ShapeDtypeStruct((Bl, S, MEMW), MXU_DTYPE))(proj3, proj3, mkv3, gq, gk)


def _mem_bwd(proj3, mkv3, gq, gk, dmo3):
    Bl, S, _ = proj3.shape
    tq = _tile(S, 256)
    q, z, kv, vec, blk = _mem_specs(S, tq)

    def body(q_ref, z_ref, kv_ref, gq_ref, gk_ref, d_ref, dq_ref, dz_ref, dkv_ref, dgq_ref, dgk_ref):
        _, vjp = jax.vjp(_mem_block, q_ref[0], z_ref[0], kv_ref[0], gq_ref[...], gk_ref[...])
        dq, dz, dkv, dgq, dgk = vjp(d_ref[0])
        dq_ref[0] = dq.astype(dq_ref.dtype)
        dz_ref[0] = dz.astype(dz_ref.dtype)
        j = pl.program_id(1)

        @pl.when(j == 0)
        def _():
            dkv_ref[0] = jnp.zeros_like(dkv)

        @pl.when((j == 0) & (pl.program_id(0) == 0))
        def _():
            dgq_ref[...] = jnp.zeros_like(dgq_ref)
            dgk_ref[...] = jnp.zeros_like(dgk_ref)

        dkv_ref[0] += dkv
        dgq_ref[...] += dgq
        dgk_ref[...] += dgk

    shp = jax.ShapeDtypeStruct((Bl, S, MEMW), MXU_DTYPE)
    gshp = jax.ShapeDtypeStruct((1, MEM_HD), F32)
    return _call(body, name="mem_bwd", grid=(Bl, S // tq), in_specs=[q, z, kv, vec, vec, blk],
                 out_specs=[blk, blk, kv, vec, vec],
                 out_shape=[shp, shp, jax.ShapeDtypeStruct(mkv3.shape, F32), gshp, gshp],
                 )(proj3, proj3, mkv3, gq, gk, dmo3)


def _merge_specs(T, D, tm, tn):
    def act(w):
        return pl.BlockSpec((tm, w), lambda i, n: (i, 0))

    def wsp(w):
        return pl.BlockSpec((w, tn), lambda i, n: (0, n))

    gates = [pl.BlockSpec((tm, tn), lambda i, n, k=k: (i, (G0 + k * D) // tn + n)) for k in range(3)]
    tile = pl.BlockSpec((tm, tn), lambda i, n: (i, n))
    return act, wsp, gates, tile


def _merge_fwd(a, cc, mo, wa, wc, wm, proj2):
    T, D = a.shape[0], wa.shape[1]
    tm, tn = _tile(T, 512), _tile(D, 512)
    act, wsp, gates, tile = _merge_specs(T, D, tm, tn)

    def body(a_ref, c_ref, m_ref, wa_ref, wc_ref, wm_ref, g0, g1, g2, mg_ref, mt_ref, pa_ref, pc_ref, pm_ref):
        pa = jnp.dot(a_ref[...], wa_ref[...], preferred_element_type=F32)
        pc = jnp.dot(c_ref[...], wc_ref[...], preferred_element_type=F32)
        pm = jnp.dot(m_ref[...], wm_ref[...], preferred_element_type=F32)
        mg = _sig(g0[...]) * pa + _sig(g1[...]) * pc + _sig(g2[...]) * pm
        mg_ref[...] = mg.astype(mg_ref.dtype)
        mt_ref[...] = mg.T.astype(mt_ref.dtype)
        pa_ref[...] = pa.astype(pa_ref.dtype)
        pc_ref[...] = pc.astype(pc_ref.dtype)
        pm_ref[...] = pm.astype(pm_ref.dtype)

    shp = jax.ShapeDtypeStruct((T, D), MXU_DTYPE)
    return _call(body, name="merge_fwd", grid=(T // tm, D // tn),
                 in_specs=[act(GW), act(CONVW), act(MEMW), wsp(GW), wsp(CONVW), wsp(MEMW)] + gates,
                 out_specs=[tile, pl.BlockSpec((tn, tm), lambda i, n: (n, i)), tile, tile, tile],
                 out_shape=[shp, jax.ShapeDtypeStruct((D, T), MXU_DTYPE), shp, shp, shp],
                 )(a, cc, mo, wa, wc, wm, proj2, proj2, proj2)


def _merge_bwd(dyb, w_out, proj2, pa, pc, pm):
    T, D = dyb.shape
    tm, tn = _tile(T, 512), _tile(D, 512)
    _, _, gates, tile = _merge_specs(T, D, tm, tn)

    def body(dy_ref, w_ref, g0, g1, g2, p0, p1, p2, dp0, dp1, dp2, dg0, dg1, dg2):
        dm = lax.dot_general(dy_ref[...], w_ref[...], _DIMS["nt"], preferred_element_type=F32)
        for g_ref, p_ref, dp_ref, dg_ref in ((g0, p0, dp0, dg0), (g1, p1, dp1, dg1), (g2, p2, dp2, dg2)):
            gt = _sig(g_ref[...])
            dp_ref[...] = (gt * dm).astype(dp_ref.dtype)
            dg_ref[...] = (dm * p_ref[...].astype(F32) * gt * (1.0 - gt)).astype(dg_ref.dtype)

    shp = jax.ShapeDtypeStruct((T, D), MXU_DTYPE)
    return _call(body, name="merge_bwd", grid=(T // tm, D // tn),
                 in_specs=[pl.BlockSpec((tm, D), lambda i, n: (i, 0)), pl.BlockSpec((tn, D), lambda i, n: (n, 0))]
                 + gates + [tile] * 3,
                 out_specs=[tile] * 6, out_shape=[shp] * 6)(dyb, w_out, proj2, proj2, proj2, pa, pc, pm)


def _out_loss(merged, w_out, x, tgt):
    T, D = x.shape
    tm = _tile(T, 256)

    def body(m_ref, w_ref, x_ref, t_ref, dy_ref, dyb_ref, loss_ref):
        err = x_ref[...] + jnp.dot(m_ref[...], w_ref[...], preferred_element_type=F32) - t_ref[...]
        dy = err * (1.0 / D)
        dy_ref[...] = dy
        dyb_ref[...] = dy.astype(dyb_ref.dtype)

        @pl.when(pl.program_id(0) == 0)
        def _():
            loss_ref[...] = jnp.zeros_like(loss_ref)

        loss_ref[...] += jnp.sum(err * err) * (0.5 / D)

    row = pl.BlockSpec((tm, D), lambda i: (i, 0))
    return _call(body, name="out_loss", grid=(T // tm,),
                 in_specs=[row, pl.BlockSpec((D, D), lambda i: (0, 0)), row, row],
                 out_specs=[row, row, pl.BlockSpec((1, 128), lambda i: (0, 0))],
                 out_shape=[jax.ShapeDtypeStruct((T, D), F32), jax.ShapeDtypeStruct((T, D), MXU_DTYPE),
                            jax.ShapeDtypeStruct((1, 128), F32)])(merged, w_out, x, tgt)


def _proj_chunk(hb, w, order, j, buf, name):
    T, D = hb.shape
    Cs = w.shape[1]
    tm, tn = _tile(T, 1024), _tile(Cs, 2176)
    nj = Cs // tn

    def body(order_ref, a_ref, b_ref, *rest):
        rest[-1][...] = jnp.dot(a_ref[...], b_ref[...], preferred_element_type=F32)

    in_specs = [pl.BlockSpec((tm, D), lambda n, i, o: (i, 0)), pl.BlockSpec((D, tn), lambda n, i, o: (0, n))]
    args = [order, hb, w]
    if buf is not None:
        in_specs.append(ANY)
        args.append(buf)
    spec = pltpu.PrefetchScalarGridSpec(
        num_scalar_prefetch=1, grid=(nj, T // tm), in_specs=in_specs,
        out_specs=pl.BlockSpec((tm, tn), lambda n, i, o: (i, o[j] * nj + n)))
    return _call(body, name=name, grid_spec=spec, out_shape=jax.ShapeDtypeStruct((T, 4 * Cs), F32),
                 aliases={} if buf is None else {3: 0})(*args)


def _norms(x, mem, norm_g, mem_norm_g):
    D = x.shape[-1]
    hb, hbt = _rms_fwd(x.reshape(-1, D), norm_g.reshape(1, D), "rms_x")
    mhb, _ = _rms_fwd(mem.reshape(-1, D), mem_norm_g.reshape(1, D), "rms_mem")
    return hb, hbt, mhb


def _mix_fwd(proj2, Bl, gq_all, gk_all, conv_w):
    T, IN = proj2.shape
    proj3 = proj2.reshape(Bl, T // Bl, IN)
    os, ls = [], []
    for g, d in enumerate(DILATIONS):
        o, l = _attn_fwd(proj3, gq_all[g:g + 1], gk_all[g:g + 1], g, d)
        os.append(o.reshape(T, GW))
        ls.append(l.reshape(T, GW))
    a = _combine_fwd(os, ls, proj2)
    cc = _conv_fwd(proj3, conv_w).reshape(T, CONVW)
    return os, ls, a, cc


def _weight_grads(x, mem, tgt, norm_g, mem_norm_g, gq_all, gk_all, conv_w, mem_gq, mem_gk, W, pre, early=None):
    Bl, S, D = x.shape
    T = Bl * S
    hb, hbt, mhb, proj2, os, ls, a, cc = pre
    IN = proj2.shape[1]
    proj3 = proj2.reshape(Bl, S, IN)
    x2, tgt2 = x.reshape(T, D), tgt.reshape(T, D)
    mem2 = mem.reshape(-1, D)
    ng, mng = norm_g.reshape(1, D), mem_norm_g.reshape(1, D)
    mgq, mgk = mem_gq.reshape(1, MEM_HD), mem_gk.reshape(1, MEM_HD)
    gqs = [gq_all[g:g + 1] for g in range(NGROUP)]
    gks = [gk_all[g:g + 1] for g in range(NGROUP)]

    mkv = _matmul(mhb, W["mem_w_kv"], "nn", F32, name="mem_kv", tm=512, tn=1024, tk=D)
    mkv3 = mkv.reshape(Bl, -1, 2 * MEMW)
    mo = _mem_fwd(proj3, mkv3, mgq, mgk).reshape(T, MEMW)
    merged, mergedt, pa, pc, pm = _merge_fwd(a, cc, mo, W["w_br_attn"], W["w_br_conv"], W["w_br_mem"], proj2)
    dy, dyb, loss = _out_loss(merged, W["w_out"], x2, tgt2)

    G = {}
    G["w_out"] = _matmul(mergedt, dyb, "nn", WIRE_DTYPE, name="dw_out", tm=1024, tn=512, tk=T)
    dpa, dpc, dpm, dg0, dg1, dg2 = _merge_bwd(dyb, W["w_out"], proj2, pa, pc, pm)
    G["w_br_attn"] = _matmul(a, dpa, "tn", WIRE_DTYPE, name="dw_br_attn", tm=512, tn=1024, tk=512)
    G["w_br_conv"] = _matmul(cc, dpc, "tn", WIRE_DTYPE, name="dw_br_conv", tm=1024, tn=1024, tk=512)
    G["w_br_mem"] = _matmul(mo, dpm, "tn", WIRE_DTYPE, name="dw_br_mem", tm=1024, tn=1024, tk=512)
    da = _matmul(dpa, W["w_br_attn"], "nt", F32, name="d_attn", tm=1024, tn=512, tk=D)
    dcc = _matmul(dpc, W["w_br_conv"], "nt", F32, name="d_conv", tm=1024, tn=1024, tk=D)
    dmo = _matmul(dpm, W["w_br_mem"], "nt", F32, name="d_mem", tm=1024, tn=1024, tk=D)
    dmq, dzm, dmkv3, dmgq, dmgk = _mem_bwd(proj3, mkv3, mgq, mgk, dmo.reshape(Bl, S, MEMW))
    dmkv = _cast(dmkv3.reshape(-1, 2 * MEMW), "cast_dmkv")
    G["mem_w_kv"] = _matmul(mhb, dmkv, "tn", WIRE_DTYPE, name="dw_mem_kv", tm=1024, tn=1024, tk=512)
    early_state, da = (None, da) if early is None else early(G, da)
    dmh = _matmul(dmkv, W["mem_w_kv"], "nt", F32, name="d_memh", tm=512, tn=1024, tk=2 * MEMW)
    _, dmng = _rms_bwd(mem2, dmh, mng, None, "rms_mem_bwd")

    dos, dls, dza = _combine_bwd(os, ls, proj2, da)
    dqs, dks, dvs, dgq, dgk = [], [], [], [], []
    for g, d in enumerate(DILATIONS):
        dq, dk, dv, gq_g, gk_g = _attn_bwd(proj3, gqs[g], gks[g], os[g].reshape(Bl, S, GW), ls[g].reshape(Bl, S, GW),
                                           dos[g].reshape(Bl, S, GW), dls[g].reshape(Bl, S, GW), g, d)
        dqs.append(dq.reshape(T, GW).astype(MXU_DTYPE))
        dks.append(dk.reshape(T, GW).astype(MXU_DTYPE))
        dvs.append(dv.reshape(T, GW).astype(MXU_DTYPE))
        dgq.append(gq_g)
        dgk.append(gk_g)
    dcb, dcc_, dcv, dzc, dconv_w = _conv_bwd(proj3, conv_w, dcc.reshape(Bl, S, CONVW))

    dproj = jnp.concatenate(dqs + dks + dvs + [dza] + [t.reshape(T, CONVW) for t in (dcb, dcc_, dcv, dzc)]
                            + [dmq.reshape(T, MEMW), dzm.reshape(T, MEMW), dg0, dg1, dg2], axis=1)
    G["w_in"] = _matmul(hbt, dproj, "nn", WIRE_DTYPE, name="dw_in", tm=1024, tn=512, tk=T)
    small = [loss, None, dmng] + dgq + dgk + [dconv_w.reshape(1, 3 * CONVW), dmgq, dmgk]
    return G, (dproj, x2, ng, dy, small), early_state


def _input_grad(rest, w_in):
    dproj, x2, ng, dy, small = rest
    dh = _matmul(dproj, w_in, "nt", F32, name="d_h", tm=1024, tn=1024, tk=4352)
    grad_x, dng = _rms_bwd(x2, dh, ng, dy, "rms_x_bwd")
    small = [dng if t is None else t for t in small]
    return grad_x, jnp.concatenate(small, axis=1)


def _local_step(x, mem, tgt, norm_g, mem_norm_g, gq_all, gk_all, conv_w, mem_gq, mem_gk, W):
    hb, hbt, mhb = _norms(x, mem, norm_g, mem_norm_g)
    Cs = W["w_in"].shape[1] // 4
    order = jnp.arange(4, dtype=jnp.int32)
    proj2 = None
    for j in range(4):
        proj2 = _proj_chunk(hb, W["w_in"][:, j * Cs:(j + 1) * Cs], order, j, proj2, f"proj_{j}")
    pre = (hb, hbt, mhb, proj2, *_mix_fwd(proj2, x.shape[0], gq_all, gk_all, conv_w))
    G, rest, _ = _weight_grads(x, mem, tgt, norm_g, mem_norm_g, gq_all, gk_all, conv_w, mem_gq, mem_gk, W, pre)
    grad_x, small = _input_grad(rest, W["w_in"])
    return grad_x.reshape(x.shape), G, small


BIG = (("w_in", "col"), ("mem_w_kv", "row"), ("w_br_attn", "col"), ("w_br_conv", "col"),
       ("w_br_mem", "col"), ("w_out", "row"))


def _coords():
    return lax.axis_index("x"), lax.axis_index("y"), lax.axis_index("c")


def _other_chips(x, y):
    return [(1 - x, y), (x, 1 - y), (1 - x, 1 - y)]


def _half(ref, kind, c):
    R, C = ref.shape
    if kind == "col":
        return ref.at[pl.ds(c * (R // 2), R // 2), :]
    return ref.at[:, pl.ds(c * (C // 2), C // 2)]


def _shard(ref, kind, s):
    R, C = ref.shape
    if kind == "col":
        return ref.at[:, pl.ds(s * (C // 4), C // 4)]
    return ref.at[pl.ds(s * (R // 4), R // 4), :]


def _piece(ref, kind, s, c):
    R, C = ref.shape
    if kind == "col":
        return ref.at[pl.ds(c * (R // 2), R // 2), pl.ds(s * (C // 4), C // 4)]
    return ref.at[pl.ds(s * (R // 4), R // 4), pl.ds(c * (C // 2), C // 2)]


def _remote(src, dst, sems_s, sems_r, k, dev):
    return pltpu.make_async_remote_copy(src_ref=src, dst_ref=dst, send_sem=sems_s.at[k], recv_sem=sems_r.at[k],
                                        device_id=dev, device_id_type=MESH)


HBM = pl.BlockSpec(memory_space=pltpu.HBM)
SEM = pl.BlockSpec(memory_space=pltpu.SEMAPHORE)
EFFECT = pltpu.SideEffectType.DATAFLOW_SIDE_EFFECTING


def _hbm(a):
    return pltpu.with_memory_space_constraint(a, pltpu.HBM)


def _start_copies(name, arrays, ncopies, make):
    n = len(arrays)

    def body(*refs):
        for cp in make(refs[:n], refs[n], refs[n + 1]):
            cp.start()

    outs = pl.pallas_call(
        body, name=name,
        out_shape=(pltpu.SemaphoreType.DMA((ncopies,)), pltpu.SemaphoreType.DMA((ncopies,)),
                   *[jax.ShapeDtypeStruct(t.shape, t.dtype) for t in arrays]),
        in_specs=[HBM] * n, out_specs=(SEM, SEM, *([HBM] * n)),
        input_output_aliases={i: i + 2 for i in range(n)},
        compiler_params=pltpu.CompilerParams(has_side_effects=EFFECT),
    )(*[_hbm(t) for t in arrays])
    return outs[0], outs[1], list(outs[2:])


def _wait_copies(name, send, recv, arrays, make, after):
    n = len(arrays)

    def body(*refs):
        for cp in make(refs[:n], refs[n], refs[n + 1]):
            cp.wait_send()
            cp.wait_recv()

    outs = pl.pallas_call(
        body, name=name, out_shape=[jax.ShapeDtypeStruct(t.shape, t.dtype) for t in arrays],
        in_specs=[HBM] * n + [SEM, SEM, ANY], out_specs=[HBM] * n,
        input_output_aliases={i: i for i in range(n)},
        compiler_params=pltpu.CompilerParams(has_side_effects=EFFECT),
    )(*arrays, send, recv, after)
    return list(outs)


def _w_in_copies(relations):
    def make(refs, send, recv):
        x, y, c = _coords()
        me = 2 * x + y
        chips = _other_chips(x, y)
        own, conv = refs[0], refs[1 + len(relations)]
        cps = []
        for i, k in enumerate(relations):
            cps.append(_remote(_half(own, "col", c), _half(refs[1 + i], "col", c), send, recv, 2 * i, (*chips[k], c)))
            mine = _shard(conv, "col", me)
            cps.append(_remote(mine, mine, send, recv, 2 * i + 1, (*chips[k], c)))
        return cps
    return make


def _other_weight_copies(refs, send, recv):
    x, y, c = _coords()
    me = 2 * x + y
    cps = []
    for k, chip in enumerate(_other_chips(x, y)):
        for p, (_, kind) in enumerate(BIG[1:]):
            mine = _piece(refs[p], kind, me, c)
            cps.append(_remote(mine, mine, send, recv, 3 * p + k, (*chip, c)))
    return cps


def _sibling_forward(name, arrays, ncp, halves):
    n = len(arrays)

    def body(*refs):
        outs = refs[n:2 * n]
        send, recv = refs[2 * n:]
        x, y, c = _coords()
        sib = (x, y, 1 - c)
        cps = [_remote(got, got, send, recv, i, sib) for i, got in enumerate(halves(outs, c))]
        for cp in cps:
            cp.start()
        for cp in cps:
            cp.wait_send()
        for i, got in enumerate(halves(outs, 1 - c)):
            _remote(got, got, send, recv, i, sib).wait_recv()

    return pl.pallas_call(
        body, name=name, out_shape=[jax.ShapeDtypeStruct(t.shape, t.dtype) for t in arrays],
        in_specs=[ANY] * n, out_specs=[ANY] * n, input_output_aliases={i: i for i in range(n)},
        scratch_shapes=[pltpu.SemaphoreType.DMA((ncp,)), pltpu.SemaphoreType.DMA((ncp,))],
    )(*arrays)


def _landed_halves(refs, c):
    return [_half(r, "col", c) for r in refs]


def _other_weight_halves(refs, c):
    x, y, _ = _coords()
    out = []
    for chip in _other_chips(x, y):
        s = 2 * chip[0] + chip[1]
        out += [_piece(refs[p], kind, s, c) for p, (_, kind) in enumerate(BIG[1:])]
    return out


def _sibling_exchange(grads, group, name):
    n = len(group)
    shapes = []
    for (_, kind), g in zip(group, grads):
        R, C = g.shape
        shapes.append(jax.ShapeDtypeStruct((R // 2, C) if kind == "col" else (R, C // 2), g.dtype))

    def body(*refs):
        ins, outs = refs[:n], refs[n:2 * n]
        send, recv = refs[2 * n:]
        x, y, c = _coords()
        sib = (x, y, 1 - c)
        cps = [_remote(_half(ins[p], group[p][1], 1 - c), outs[p], send, recv, p, sib) for p in range(n)]
        for cp in cps:
            cp.start()
        for cp in cps:
            cp.wait()

    return pl.pallas_call(
        body, name=name, out_shape=shapes, in_specs=[ANY] * n, out_specs=[ANY] * n,
        scratch_shapes=[pltpu.SemaphoreType.DMA((n,)), pltpu.SemaphoreType.DMA((n,))],
    )(*grads)


def _presum(g, got, kind, pos, name):
    R, C = got.shape
    tr, tc = _tile(R, 512, 16), _tile(C, 2048)
    nr, nc = R // tr, C // tc

    def body(pos_ref, a_ref, b_ref, o_ref):
        o_ref[...] = (a_ref[...].astype(F32) + b_ref[...].astype(F32)).astype(o_ref.dtype)

    if kind == "col":
        mine = pl.BlockSpec((tr, tc), lambda i, j, pos_ref: (pos_ref[1] * nr + i, j))
    else:
        mine = pl.BlockSpec((tr, tc), lambda i, j, pos_ref: (i, pos_ref[1] * nc + j))
    blk = pl.BlockSpec((tr, tc), lambda i, j, pos_ref: (i, j))
    spec = pltpu.PrefetchScalarGridSpec(num_scalar_prefetch=1, grid=(nr, nc), in_specs=[mine, blk], out_specs=blk)
    return _call(body, name=name, grid_spec=spec, out_shape=jax.ShapeDtypeStruct((R, C), WIRE_DTYPE))(pos, g, got)


def _chip_copies(group):
    n = len(group)

    def make(refs, send, recv):
        x, y, c = _coords()
        cps = []
        for k, chip in enumerate(_other_chips(x, y)):
            s = 2 * chip[0] + chip[1]
            for p in range(n):
                cps.append(_remote(_shard(refs[p], group[p][1], s), refs[n + p].at[k], send, recv, 3 * p + k, (*chip, c)))
        return cps
    return make


def _landing_zones(pres, group):
    lands = []
    for (_, kind), g in zip(group, pres):
        R, C = g.shape
        lands.append(lax.empty((3, R, C // 4) if kind == "col" else (3, R // 4, C), g.dtype))
    return lands


def _exchange_start(G, group, pos, carry, tag):
    n = len(group)
    parts = [G[name] for name, _ in group]
    got = _sibling_exchange(parts, group, "sibling_exchange_" + tag)
    pres = [_presum(parts[p], got[p], kind, pos, "presum_" + name) for p, (name, kind) in enumerate(group)]
    make = _chip_copies(group)
    send, recv, thru = _start_copies("chip_exchange_start_" + tag, [*pres, *_landing_zones(pres, group), carry], 3 * n, make)
    return (send, recv, thru[:2 * n], make, tag), thru[2 * n]


def _exchange_wait(state, after):
    send, recv, arrays, make, tag = state
    thru = _wait_copies("chip_exchange_wait_" + tag, send, recv, arrays, make, after)
    n = len(thru) // 2
    return thru[:n], thru[n:]


def _reduce_into_shard(slots, pre, kind, pos, name):
    K, R, C = slots.shape
    tr, tc = _tile(R, 512, 16), _tile(C, 2176)
    nr, nc = R // tr, C // tc

    def body(pos_ref, s_ref, p_ref, o_ref):
        acc = p_ref[...].astype(F32)
        for k in range(K):
            acc = acc + s_ref[k].astype(F32)
        o_ref[...] = acc

    if kind == "col":
        own = pl.BlockSpec((tr, tc), lambda i, j, pos_ref: (i, pos_ref[0] * nc + j))
        full, out = (2 * R, C), pl.BlockSpec((tr, tc), lambda i, j, pos_ref: (pos_ref[1] * nr + i, j))
    else:
        own = pl.BlockSpec((tr, tc), lambda i, j, pos_ref: (pos_ref[0] * nr + i, j))
        full, out = (R, 2 * C), pl.BlockSpec((tr, tc), lambda i, j, pos_ref: (i, pos_ref[1] * nc + j))
    spec = pltpu.PrefetchScalarGridSpec(
        num_scalar_prefetch=1, grid=(nr, nc),
        in_specs=[pl.BlockSpec((K, tr, tc), lambda i, j, pos_ref: (0, i, j)), own], out_specs=out)
    return _call(body, name=name, grid_spec=spec, out_shape=jax.ShapeDtypeStruct(full, F32))(pos, slots, pre)


def _share_reduced(reds):
    n = len(BIG)

    def body(*refs):
        outs = refs[n:2 * n]
        send, recv = refs[2 * n:]
        x, y, c = _coords()
        sib = (x, y, 1 - c)
        cps = []
        for p in range(n):
            mine = _half(outs[p], BIG[p][1], c)
            cps.append(_remote(mine, mine, send, recv, p, sib))
        for cp in cps:
            cp.start()
        for cp in cps:
            cp.wait_send()
        for p in range(n):
            got = _half(outs[p], BIG[p][1], 1 - c)
            _remote(got, got, send, recv, p, sib).wait_recv()

    return pl.pallas_call(
        body, name="share_reduced", out_shape=[jax.ShapeDtypeStruct(r.shape, r.dtype) for r in reds],
        in_specs=[ANY] * n, out_specs=[ANY] * n, input_output_aliases={p: p for p in range(n)},
        scratch_shapes=[pltpu.SemaphoreType.DMA((n,)), pltpu.SemaphoreType.DMA((n,))],
    )(*reds)


def _gather_small(pack):
    _, N = pack.shape

    def body(in_ref, out_ref, send, recv, loc):
        x, y, c = _coords()
        me = 4 * x + 2 * y + c
        own = pltpu.make_async_copy(in_ref, out_ref.at[me], loc)
        own.start()
        cps = []
        for k in range(1, 8):
            dev = (x ^ (k >> 2), y ^ ((k >> 1) & 1), c ^ (k & 1))
            cps.append(_remote(in_ref, out_ref.at[me], send, recv, k - 1, dev))
        for cp in cps:
            cp.start()
        for k in range(1, 8):
            src = 4 * (x ^ (k >> 2)) + 2 * (y ^ ((k >> 1) & 1)) + (c ^ (k & 1))
            _remote(in_ref, out_ref.at[src], send, recv, k - 1, (x, y, c)).wait_recv()
        for cp in cps:
            cp.wait_send()
        own.wait()

    return pl.pallas_call(
        body, name="gather_small", out_shape=jax.ShapeDtypeStruct((8, 1, N), pack.dtype),
        in_specs=[ANY], out_specs=ANY,
        scratch_shapes=[pltpu.SemaphoreType.DMA((7,)), pltpu.SemaphoreType.DMA((7,)), pltpu.SemaphoreType.DMA(())],
    )(pack)


def _sum_small(slots):
    K, _, N = slots.shape

    def body(s_ref, o_ref):
        acc = s_ref[0]
        for k in range(1, K):
            acc = acc + s_ref[k]
        o_ref[...] = acc

    return _call(body, name="sum_small", in_specs=[pl.BlockSpec(memory_space=pltpu.VMEM)],
                 out_specs=pl.BlockSpec(memory_space=pltpu.VMEM), out_shape=jax.ShapeDtypeStruct((1, N), F32))(slots)


def _adamw(w, g, m, v, name):
    R, C = w.shape
    tr, tc = _tile(R, 256, 8), _tile(C, 2176)

    def body(w_ref, g_ref, m_ref, v_ref, d_ref, nm_ref, nv_ref):
        gv = g_ref[...]
        nm = ADAM_B1 * m_ref[...] + (1.0 - ADAM_B1) * gv
        nv = ADAM_B2 * v_ref[...] + (1.0 - ADAM_B2) * gv * gv
        m_hat = nm / (1.0 - ADAM_B1 ** ADAM_STEP)
        v_hat = nv / (1.0 - ADAM_B2 ** ADAM_STEP)
        d_ref[...] = -ADAM_LR * (m_hat / (jnp.sqrt(v_hat) + ADAM_EPS) + ADAM_WD * w_ref[...])
        nm_ref[...] = nm
        nv_ref[...] = nv

    spec = pl.BlockSpec((tr, tc), lambda i, j: (i, j))
    shp = jax.ShapeDtypeStruct((R, C), F32)
    return _call(body, name=name, grid=(R // tr, C // tc), in_specs=[spec] * 4, out_specs=[spec] * 3,
                 out_shape=[shp] * 3)(w, g, m, v)


SMALL = ("norm_g", "mem_norm_g", "attn_q_norm", "attn_k_norm", "conv_w", "mem_q_norm", "mem_k_norm")
WEIGHTS = ("norm_g", "mem_norm_g", "w_in", "attn_q_norm", "attn_k_norm", "conv_w", "mem_w_kv", "mem_q_norm",
           "mem_k_norm", "w_br_attn", "w_br_conv", "w_br_mem", "w_out")


def kernel(x, mem, norm_g, mem_norm_g, w_in, attn_q_norm, attn_k_norm, conv_w, mem_w_kv, mem_q_norm, mem_k_norm, w_br_attn, w_br_conv, w_br_mem, w_out, loss_target, m_norm_g, m_mem_norm_g, m_w_in, m_attn_q_norm, m_attn_k_norm, m_conv_w, m_mem_w_kv, m_mem_q_norm, m_mem_k_norm, m_w_br_attn, m_w_br_conv, m_w_br_mem, m_w_out, v_norm_g, v_mem_norm_g, v_w_in, v_attn_q_norm, v_attn_k_norm, v_conv_w, v_mem_w_kv, v_mem_q_norm, v_mem_k_norm, v_w_br_attn, v_w_br_conv, v_w_br_mem, v_w_out):
    w = dict(norm_g=norm_g, mem_norm_g=mem_norm_g, w_in=w_in, attn_q_norm=attn_q_norm, attn_k_norm=attn_k_norm,
             conv_w=conv_w, mem_w_kv=mem_w_kv, mem_q_norm=mem_q_norm, mem_k_norm=mem_k_norm, w_br_attn=w_br_attn,
             w_br_conv=w_br_conv, w_br_mem=w_br_mem, w_out=w_out)
    m = dict(norm_g=m_norm_g, mem_norm_g=m_mem_norm_g, w_in=m_w_in, attn_q_norm=m_attn_q_norm,
             attn_k_norm=m_attn_k_norm, conv_w=m_conv_w, mem_w_kv=m_mem_w_kv, mem_q_norm=m_mem_q_norm,
             mem_k_norm=m_mem_k_norm, w_br_attn=m_w_br_attn, w_br_conv=m_w_br_conv, w_br_mem=m_w_br_mem, w_out=m_w_out)
    v = dict(norm_g=v_norm_g, mem_norm_g=v_mem_norm_g, w_in=v_w_in, attn_q_norm=v_attn_q_norm,
             attn_k_norm=v_attn_k_norm, conv_w=v_conv_w, mem_w_kv=v_mem_w_kv, mem_q_norm=v_mem_q_norm,
             mem_k_norm=v_mem_k_norm, w_br_attn=v_w_br_attn, w_br_conv=v_w_br_conv, w_br_mem=v_w_br_mem, w_out=v_w_out)
    Bl, _, D = x.shape
    cx, cy = lax.axis_index("x"), lax.axis_index("y")
    chip = 2 * cx + cy
    pos = jnp.stack([chip, lax.axis_index("c")]).astype(jnp.int32)
    order = jnp.stack([chip] + [2 * a + b for a, b in _other_chips(cx, cy)]).astype(jnp.int32)
    n = len(BIG)

    own = _cast(w["w_in"], "cast_w_in")
    lands = [lax.empty(own.shape, own.dtype) for _ in range(3)]
    conv_full = _place_shard(conv_w, "col", pos, F32, "place_conv_w")
    others = [_place_shard(w[name], kind, pos, WIRE_DTYPE, "place_" + name) for name, kind in BIG[1:]]
    hb, hbt, mhb = _norms(x, mem, norm_g, mem_norm_g)

    near = _w_in_copies((0, 1))
    send, recv, (own, l0, l1, conv_full) = _start_copies("gather_near_start", [own, lands[0], lands[1], conv_full], 4, near)
    proj = _proj_chunk(hb, own, order, 0, None, "proj_own")
    own, l0, l1, conv_full = _wait_copies("gather_near_wait", send, recv, [own, l0, l1, conv_full], near, proj)
    l0, l1 = _sibling_forward("gather_near_forward", [l0, l1], 2, _landed_halves)

    far = _w_in_copies((2,))
    send, recv, (own, l2, conv_full, l0) = _start_copies("gather_far_start", [own, lands[2], conv_full, l0], 2, far)
    proj = _proj_chunk(hb, l0, order, 1, proj, "proj_near_x")
    proj = _proj_chunk(hb, l1, order, 2, proj, "proj_near_y")
    own, l2, conv_full, l0 = _wait_copies("gather_far_wait", send, recv, [own, l2, conv_full, l0], far, proj)
    l2, = _sibling_forward("gather_far_forward", [l2], 1, _landed_halves)

    send, recv, (*others, l2) = _start_copies("gather_rest_start", [*others, l2], 3 * (n - 1), _other_weight_copies)
    proj = _proj_chunk(hb, l2, order, 3, proj, "proj_far")
    mixed = _mix_fwd(proj, Bl, attn_q_norm, attn_k_norm, conv_full)
    *others, l2 = _wait_copies("gather_rest_wait", send, recv, [*others, l2], _other_weight_copies, mixed[2])
    others = _sibling_forward("gather_rest_forward", others, 3 * (n - 1), _other_weight_halves)
    W = {name: others[p] for p, (name, _) in enumerate(BIG[1:])}

    G, rest, rest_state = _weight_grads(
        x, mem, loss_target, norm_g, mem_norm_g, attn_q_norm, attn_k_norm, conv_full, mem_q_norm, mem_k_norm, W,
        (hb, hbt, mhb, proj, *mixed), early=lambda G, carry: _exchange_start(G, BIG[1:], pos, carry, "rest"))

    w_in_full = None
    for j, shard in enumerate((own, l0, l1, l2)):
        w_in_full = _place_shard(shard, "col", order, WIRE_DTYPE, f"assemble_w_in_{j}", slot=j, into=w_in_full)

    w_in_state, dproj = _exchange_start(G, BIG[:1], pos, rest[0], "w_in")
    grad_x, small = _input_grad((dproj, *rest[1:]), w_in_full)
    pres, slots = _exchange_wait(w_in_state, grad_x)
    pres_rest, slots_rest = _exchange_wait(rest_state, grad_x)
    pres, slots = pres + pres_rest, slots + slots_rest
    grad_x = grad_x.reshape(x.shape)
    reds = [_reduce_into_shard(slots[p], pres[p], kind, pos, "reduce_" + name) for p, (name, kind) in enumerate(BIG)]
    grads = dict(zip([name for name, _ in BIG], _share_reduced(reds)))

    tot = _sum_small(_gather_small(small))[0]
    loss = tot[0]
    off = 128
    for name, size in (("norm_g", D), ("mem_norm_g", D), ("attn_q_norm", NGROUP * HEAD), ("attn_k_norm", NGROUP * HEAD),
                       ("conv_w", 3 * CONVW), ("mem_q_norm", MEM_HD), ("mem_k_norm", MEM_HD)):
        grads[name] = tot[off:off + size]
        off += size
    cw = conv_w.shape[1]
    grads["conv_w"] = lax.dynamic_slice(grads["conv_w"].reshape(3, CONVW), (0, chip * cw), (3, cw))
    for name in SMALL:
        grads[name] = grads[name].reshape(w[name].shape)

    delta, new_m, new_v = {}, {}, {}
    for name, _ in BIG:
        delta[name], new_m[name], new_v[name] = _adamw(w[name], grads[name], m[name], v[name], "adamw_" + name)

    def packed(t):
        return jnp.concatenate([t[name].reshape(1, -1) for name in SMALL], axis=1)

    ds, ms, vs = _adamw(packed(w), packed(grads), packed(m), packed(v), "adamw_small")
    off = 0
    for name in SMALL:
        size = w[name].size
        delta[name] = ds[0, off:off + size].reshape(w[name].shape)
        new_m[name] = ms[0, off:off + size].reshape(w[name].shape)
        new_v[name] = vs[0, off:off + size].reshape(w[name].shape)
        off += size

    return (loss, grad_x, *[grads[n] for n in WEIGHTS], *[delta[n] for n in WEIGHTS],
            *[new_m[n] for n in WEIGHTS], *[new_v[n] for n in WEIGHTS])
```

```python
import functools

import jax
import jax.numpy as jnp
from jax import lax
from jax.experimental import pallas as pl
from jax.experimental.pallas import tpu as pltpu

F32 = jnp.float32
MXU_DTYPE = jnp.bfloat16
WIRE_DTYPE = jnp.bfloat16
EPS = 1e-6
NEG = -1e30

HEAD = 128
HPG = 4
GW = HPG * HEAD
DILATIONS = (1, 4, 16)
NGROUP = len(DILATIONS)
BLK = 128
QKV = NGROUP * GW
CONVW = 1024
MEM_HEADS = 4
MEM_HD = 256
MEMW = MEM_HEADS * MEM_HD
Q0, K0, V0 = 0, QKV, 2 * QKV
ZA = 3 * QKV
CB, CC, CV, ZC = ZA + GW, ZA + GW + CONVW, ZA + GW + 2 * CONVW, ZA + GW + 3 * CONVW
MQ = ZC + CONVW
ZM = MQ + MEMW
G0 = ZM + MEMW

ADAM_LR, ADAM_B1, ADAM_B2, ADAM_EPS, ADAM_WD, ADAM_STEP = 0.001, 0.9, 0.999, 1e-08, 0.01, 10

VMEM_LIMIT = 56 * 1024 * 1024
MESH = pl.DeviceIdType.MESH
ANY = pl.BlockSpec(memory_space=pl.ANY)


def _tile(n, pref, mult=128):
    t = min(pref, n)
    while t > mult and (n % t or t % mult):
        t -= mult
    assert n % t == 0, (n, pref)
    return t


def _call(body, *, name, out_shape, grid=(), in_specs=None, out_specs=None, scratch_shapes=(),
          aliases=None, grid_spec=None):
    kw = {}
    if grid_spec is not None:
        kw["grid_spec"] = grid_spec
        ngrid = len(grid_spec.grid)
    else:
        kw.update(grid=grid, in_specs=in_specs, out_specs=out_specs, scratch_shapes=list(scratch_shapes))
        ngrid = len(grid)
    params = pltpu.CompilerParams(dimension_semantics=("arbitrary",) * ngrid, vmem_limit_bytes=VMEM_LIMIT)
    return pl.pallas_call(body, name=name, out_shape=out_shape, compiler_params=params,
                          input_output_aliases=aliases or {}, **kw)


_DIMS = {"nn": (((1,), (0,)), ((), ())), "nt": (((1,), (1,)), ((), ())), "tn": (((0,), (0,)), ((), ()))}


def _mxu(a, b, mode):
    return lax.dot_general(a.astype(MXU_DTYPE), b.astype(MXU_DTYPE), _DIMS[mode], preferred_element_type=F32)


@functools.partial(jax.custom_vjp, nondiff_argnums=(2,))
def _dot(a, b, mode):
    return _mxu(a, b, mode)


def _dot_fwd(a, b, mode):
    return _mxu(a, b, mode), (a, b)


def _dot_bwd(mode, res, g):
    a, b = res
    if mode == "nn":
        return _mxu(g, b, "nt"), _mxu(a, g, "tn")
    if mode == "nt":
        return _mxu(g, b, "nn"), _mxu(g, a, "tn")
    return _mxu(b, g, "nt"), _mxu(a, g, "nn")


_dot.defvjp(_dot_fwd, _dot_bwd)


def _sig(z):
    return 1.0 / (1.0 + jnp.exp(-z))


def _silu(z):
    return z * _sig(z)


def _rms_rows(t, g):
    return t * lax.rsqrt(jnp.mean(t * t, axis=-1, keepdims=True) + EPS) * g


def _attn_block(q, k2, v2, gq, gk, first):
    qn = _rms_rows(q, gq)
    kn = _rms_rows(k2, gk)
    s = jnp.where(_band_mask(first), _dot(qn, kn, "nt") * (HEAD ** -0.5), NEG)
    m = lax.stop_gradient(jnp.max(s, axis=-1, keepdims=True))
    p = jnp.exp(s - m)
    den = jnp.sum(p, axis=-1, keepdims=True)
    o = _dot(p, v2, "nn") / den
    return o, m + jnp.log(den)


def _band_mask(first):
    a = lax.broadcasted_iota(jnp.int32, (BLK, 2 * BLK), 0)
    b = lax.broadcasted_iota(jnp.int32, (BLK, 2 * BLK), 1)
    return (b >= a) & (b <= a + BLK) & (b >= jnp.where(first, BLK, 0))


def _norm_parts(t):
    r = lax.rsqrt(jnp.mean(t * t, axis=-1, keepdims=True) + EPS)
    return r, t * r


def _norm_bwd(dn, g, r, th):
    dth = dn * g
    return r * (dth - th * jnp.mean(dth * th, axis=-1, keepdims=True)), jnp.sum(dn * th, axis=0, keepdims=True)


def _attn_block_bwd(q, k2, v2, gq, gk, first, do, o, lse, dlse):
    scale = HEAD ** -0.5
    rq, qh = _norm_parts(q)
    rk, kh = _norm_parts(k2)
    qn, kn = qh * gq, kh * gk
    s = jnp.where(_band_mask(first), _mxu(qn, kn, "nt") * scale, NEG)
    p = jnp.exp(s - lse)
    ds = p * (_mxu(do, v2, "nt") + (dlse - jnp.sum(do * o, axis=-1, keepdims=True))) * scale
    dq, dgq = _norm_bwd(_mxu(ds, kn, "nn"), gq, rq, qh)
    dk2, dgk = _norm_bwd(_mxu(ds, qn, "tn"), gk, rk, kh)
    return dq, dk2, _mxu(p, do, "tn"), dgq, dgk


def _combine(o1, o2, o3, l1, l2, l3, z):
    m = lax.stop_gradient(jnp.maximum(jnp.maximum(l1, l2), l3))
    e1, e2, e3 = jnp.exp(l1 - m), jnp.exp(l2 - m), jnp.exp(l3 - m)
    return (e1 * o1 + e2 * o2 + e3 * o3) / (e1 + e2 + e3) * _silu(z)


def _mem_block(q, z, kv, gq, gk):
    outs = []
    for h in range(MEM_HEADS):
        sl = slice(h * MEM_HD, (h + 1) * MEM_HD)
        qn = _rms_rows(q[:, sl], gq)
        kn = _rms_rows(kv[:, sl], gk)
        s = _dot(qn, kn, "nt") * (MEM_HD ** -0.5)
        m = lax.stop_gradient(jnp.max(s, axis=-1, keepdims=True))
        p = jnp.exp(s - m)
        den = jnp.sum(p, axis=-1, keepdims=True)
        outs.append(_dot(p, kv[:, MEMW + h * MEM_HD:MEMW + (h + 1) * MEM_HD], "nn") / den)
    return jnp.concatenate(outs, axis=-1) * _silu(z)


def _cast(w, name):
    R, C = w.shape
    tr, tc = _tile(R, 512, 8), _tile(C, 2176)

    def body(w_ref, o_ref):
        o_ref[...] = w_ref[...].astype(o_ref.dtype)

    spec = pl.BlockSpec((tr, tc), lambda i, j: (i, j))
    return _call(body, name=name, grid=(R // tr, C // tc), in_specs=[spec], out_specs=spec,
                 out_shape=jax.ShapeDtypeStruct((R, C), WIRE_DTYPE))(w)


def _place_shard(w, kind, pos, dtype, name, slot=0, into=None):
    R, C = w.shape
    tr, tc = _tile(R, 512, 8), _tile(C, 2176)
    nr, nc = R // tr, C // tc

    def body(pos_ref, w_ref, *rest):
        rest[-1][...] = w_ref[...].astype(rest[-1].dtype)

    if kind == "col":
        full, out = (R, 4 * C), pl.BlockSpec((tr, tc), lambda i, j, pos_ref: (i, pos_ref[slot] * nc + j))
    else:
        full, out = (4 * R, C), pl.BlockSpec((tr, tc), lambda i, j, pos_ref: (pos_ref[slot] * nr + i, j))
    in_specs, args = [pl.BlockSpec((tr, tc), lambda i, j, pos_ref: (i, j))], [pos, w]
    if into is not None:
        in_specs.append(ANY)
        args.append(into)
    spec = pltpu.PrefetchScalarGridSpec(num_scalar_prefetch=1, grid=(nr, nc), in_specs=in_specs, out_specs=out)
    return _call(body, name=name, grid_spec=spec, out_shape=jax.ShapeDtypeStruct(full, dtype),
                 aliases={} if into is None else {2: 0})(*args)


def _matmul(a, b, mode, out_dtype, *, name, tm=512, tn=512, tk=512):
    if mode == "nn":
        (M, K), (_, N) = a.shape, b.shape
    elif mode == "nt":
        (M, K), (N, _) = a.shape, b.shape
    else:
        (K, M), (_, N) = a.shape, b.shape
    tm, tn, tk = _tile(M, tm), _tile(N, tn), _tile(K, tk)
    nk = K // tk

    def body(a_ref, b_ref, o_ref, *acc):
        part = lax.dot_general(a_ref[...], b_ref[...], _DIMS[mode], preferred_element_type=F32)
        if nk == 1:
            o_ref[...] = part.astype(o_ref.dtype)
            return
        acc_ref, = acc
        k = pl.program_id(2)

        @pl.when(k == 0)
        def _():
            acc_ref[...] = part

        @pl.when(k > 0)
        def _():
            acc_ref[...] += part

        @pl.when(k == nk - 1)
        def _():
            o_ref[...] = acc_ref[...].astype(o_ref.dtype)

    a_spec = pl.BlockSpec((tk, tm), lambda i, j, k: (k, i)) if mode == "tn" else pl.BlockSpec((tm, tk), lambda i, j, k: (i, k))
    b_spec = pl.BlockSpec((tn, tk), lambda i, j, k: (j, k)) if mode == "nt" else pl.BlockSpec((tk, tn), lambda i, j, k: (k, j))
    return _call(body, name=name, grid=(M // tm, N // tn, nk), in_specs=[a_spec, b_spec],
                 out_specs=pl.BlockSpec((tm, tn), lambda i, j, k: (i, j)),
                 out_shape=jax.ShapeDtypeStruct((M, N), out_dtype),
                 scratch_shapes=[] if nk == 1 else [pltpu.VMEM((tm, tn), F32)])(a, b)


def _rms_fwd(x, g, name):
    R, D = x.shape
    tr = _tile(R, 512)

    def body(x_ref, g_ref, o_ref, t_ref):
        y = _rms_rows(x_ref[...], g_ref[...])
        o_ref[...] = y.astype(o_ref.dtype)
        t_ref[...] = y.T.astype(t_ref.dtype)

    row = pl.BlockSpec((tr, D), lambda i: (i, 0))
    return _call(body, name=name, grid=(R // tr,), in_specs=[row, pl.BlockSpec((1, D), lambda i: (0, 0))],
                 out_specs=[row, pl.BlockSpec((D, tr), lambda i: (0, i))],
                 out_shape=[jax.ShapeDtypeStruct((R, D), MXU_DTYPE), jax.ShapeDtypeStruct((D, R), MXU_DTYPE)])(x, g)


def _rms_bwd(x, dh, g, dy, name):
    R, D = x.shape
    tr = _tile(R, 256)
    with_dx = dy is not None

    def body(*refs):
        if with_dx:
            x_ref, dh_ref, g_ref, dy_ref, dx_ref, dg_ref = refs
        else:
            x_ref, dh_ref, g_ref, dg_ref = refs
        xv, dhv = x_ref[...], dh_ref[...]
        r = lax.rsqrt(jnp.mean(xv * xv, axis=-1, keepdims=True) + EPS)
        xh = xv * r

        @pl.when(pl.program_id(0) == 0)
        def _():
            dg_ref[...] = jnp.zeros_like(dg_ref)

        dg_ref[...] += jnp.sum(dhv * xh, axis=0, keepdims=True)
        if with_dx:
            dxh = dhv * g_ref[...]
            dx_ref[...] = dy_ref[...] + r * (dxh - xh * jnp.mean(dxh * xh, axis=-1, keepdims=True))

    row = pl.BlockSpec((tr, D), lambda i: (i, 0))
    vec = pl.BlockSpec((1, D), lambda i: (0, 0))
    dg_shape = jax.ShapeDtypeStruct((1, D), F32)
    if with_dx:
        return _call(body, name=name, grid=(R // tr,), in_specs=[row, row, vec, row], out_specs=[row, vec],
                     out_shape=[jax.ShapeDtypeStruct((R, D), F32), dg_shape])(x, dh, g, dy)
    return None, _call(body, name=name, grid=(R // tr,), in_specs=[row, row, vec], out_specs=vec,
                       out_shape=dg_shape)(x, dh, g)


def _attn_geom(g, d):
    hc = HPG if d == 1 else 1
    cw = hc * HEAD
    cq, ck, cv = (Q0 + g * GW) // cw, (K0 + g * GW) // cw, (V0 + g * GW) // cw
    return (1, BLK * d, cw), hc, HPG // hc, cq, ck, cv


def _rows(ref, r, d, sl):
    if d == 1:
        return ref[0, :, sl]
    return ref.at[0][pl.ds(r, BLK, stride=d), sl]


def _set_rows(ref, r, d, sl, val):
    if d == 1:
        ref[0, :, sl] = val
    else:
        ref.at[0][pl.ds(r, BLK, stride=d), sl] = val


def _attn_fwd(proj3, gq, gk, g, d):
    Bl, S, _ = proj3.shape
    blk, hc, ncb, cq, ck, cv = _attn_geom(g, d)
    nb = S // blk[1]

    def body(q_ref, kp_ref, kc_ref, vp_ref, vc_ref, gq_ref, gk_ref, o_ref, lse_ref):
        first = pl.program_id(2) == 0
        for r in range(d):
            for h in range(hc):
                sl = slice(h * HEAD, (h + 1) * HEAD)
                k2 = jnp.concatenate([_rows(kp_ref, r, d, sl), _rows(kc_ref, r, d, sl)], axis=0)
                v2 = jnp.concatenate([_rows(vp_ref, r, d, sl), _rows(vc_ref, r, d, sl)], axis=0)
                o, lse = _attn_block(_rows(q_ref, r, d, sl), k2, v2, gq_ref[...], gk_ref[...], first)
                _set_rows(o_ref, r, d, sl, o)
                _set_rows(lse_ref, r, d, sl, jnp.broadcast_to(lse, (BLK, HEAD)))

    def cur(c0):
        return pl.BlockSpec(blk, lambda b, j, i: (b, i, c0 + j))

    def prev(c0):
        return pl.BlockSpec(blk, lambda b, j, i: (b, jnp.maximum(i - 1, 0), c0 + j))

    vec = pl.BlockSpec((1, HEAD), lambda b, j, i: (0, 0))
    out = pl.BlockSpec(blk, lambda b, j, i: (b, i, j))
    shp = jax.ShapeDtypeStruct((Bl, S, GW), F32)
    return _call(body, name=f"attn_fwd_g{g}", grid=(Bl, ncb, nb),
                 in_specs=[cur(cq), prev(ck), cur(ck), prev(cv), cur(cv), vec, vec],
                 out_specs=[out, out], out_shape=[shp, shp])(proj3, proj3, proj3, proj3, proj3, gq, gk)


def _attn_bwd(proj3, gq, gk, o3, l3, do3, dl3, g, d):
    Bl, S, _ = proj3.shape
    blk, hc, ncb, cq, ck, cv = _attn_geom(g, d)
    nb = S // blk[1]

    def body(q_ref, kp_ref, kc_ref, vp_ref, vc_ref, gq_ref, gk_ref, o_ref, l_ref, do_ref, dl_ref,
             dq_ref, dk_ref, dv_ref, dgq_ref, dgk_ref, ck_ref, cv_ref):
        i = pl.program_id(2)
        first = i == 0

        @pl.when((pl.program_id(0) == 0) & (pl.program_id(1) == 0) & first)
        def _():
            dgq_ref[...] = jnp.zeros_like(dgq_ref)
            dgk_ref[...] = jnp.zeros_like(dgk_ref)

        @pl.when(first)
        def _():
            ck_ref[...] = jnp.zeros_like(ck_ref)
            cv_ref[...] = jnp.zeros_like(cv_ref)

        @pl.when(i < nb)
        def _():
            dgq, dgk = jnp.zeros((1, HEAD), F32), jnp.zeros((1, HEAD), F32)
            for r in range(d):
                rs = slice(r * BLK, (r + 1) * BLK)
                for h in range(hc):
                    sl = slice(h * HEAD, (h + 1) * HEAD)
                    k2 = jnp.concatenate([_rows(kp_ref, r, d, sl), _rows(kc_ref, r, d, sl)], axis=0)
                    v2 = jnp.concatenate([_rows(vp_ref, r, d, sl), _rows(vc_ref, r, d, sl)], axis=0)
                    dq, dk2, dv2, a, b = _attn_block_bwd(
                        _rows(q_ref, r, d, sl), k2, v2, gq_ref[...], gk_ref[...], first, _rows(do_ref, r, d, sl),
                        _rows(o_ref, r, d, sl), _rows(l_ref, r, d, sl)[:, :1], _rows(dl_ref, r, d, sl)[:, :1])
                    _set_rows(dq_ref, r, d, sl, dq)
                    _set_rows(dk_ref, r, d, sl, ck_ref[rs, sl] + dk2[:BLK])
                    _set_rows(dv_ref, r, d, sl, cv_ref[rs, sl] + dv2[:BLK])
                    ck_ref[rs, sl] = dk2[BLK:]
                    cv_ref[rs, sl] = dv2[BLK:]
                    dgq, dgk = dgq + a, dgk + b
            dgq_ref[...] += dgq
            dgk_ref[...] += dgk

        @pl.when(i == nb)
        def _():
            for r in range(d):
                rs = slice(r * BLK, (r + 1) * BLK)
                _set_rows(dk_ref, r, d, slice(None), ck_ref[rs, :])
                _set_rows(dv_ref, r, d, slice(None), cv_ref[rs, :])

    def cur(c0):
        return pl.BlockSpec(blk, lambda b, j, i: (b, jnp.minimum(i, nb - 1), c0 + j))

    def prev(c0):
        return pl.BlockSpec(blk, lambda b, j, i: (b, jnp.clip(i - 1, 0, nb - 1), c0 + j))

    vec = pl.BlockSpec((1, HEAD), lambda b, j, i: (0, 0))
    at_q = pl.BlockSpec(blk, lambda b, j, i: (b, jnp.minimum(i, nb - 1), j))
    at_k = pl.BlockSpec(blk, lambda b, j, i: (b, jnp.maximum(i - 1, 0), j))
    shp = jax.ShapeDtypeStruct((Bl, S, GW), F32)
    gshp = jax.ShapeDtypeStruct((1, HEAD), F32)
    return _call(body, name=f"attn_bwd_g{g}", grid=(Bl, ncb, nb + 1),
                 in_specs=[cur(cq), prev(ck), cur(ck), prev(cv), cur(cv), vec, vec, at_q, at_q, at_q, at_q],
                 out_specs=[at_q, at_k, at_k, vec, vec], out_shape=[shp, shp, shp, gshp, gshp],
                 scratch_shapes=[pltpu.VMEM(blk[1:], F32), pltpu.VMEM(blk[1:], F32)],
                 )(proj3, proj3, proj3, proj3, proj3, gq, gk, o3, l3, do3, dl3)


def _combine_fwd(os, ls, proj2):
    T = proj2.shape[0]
    tr = _tile(T, 512)

    def body(o1, o2, o3, l1, l2, l3, z, a_ref):
        a_ref[...] = _combine(o1[...], o2[...], o3[...], l1[...], l2[...], l3[...], z[...]).astype(a_ref.dtype)

    row = pl.BlockSpec((tr, GW), lambda i: (i, 0))
    return _call(body, name="combine_fwd", grid=(T // tr,),
                 in_specs=[row] * 6 + [pl.BlockSpec((tr, GW), lambda i: (i, ZA // GW))], out_specs=row,
                 out_shape=jax.ShapeDtypeStruct((T, GW), MXU_DTYPE))(*os, *ls, proj2)


def _combine_bwd(os, ls, proj2, da):
    T = proj2.shape[0]
    tr = _tile(T, 256)

    def body(o1, o2, o3, l1, l2, l3, z, da_ref, d1, d2, d3, e1, e2, e3, dz_ref):
        _, vjp = jax.vjp(_combine, o1[...], o2[...], o3[...], l1[...], l2[...], l3[...], z[...])
        go1, go2, go3, gl1, gl2, gl3, gz = vjp(da_ref[...])
        d1[...], d2[...], d3[...] = go1, go2, go3
        dz_ref[...] = gz.astype(dz_ref.dtype)
        for ref, gl in ((e1, gl1), (e2, gl2), (e3, gl3)):
            for h in range(HPG):
                sl = slice(h * HEAD, (h + 1) * HEAD)
                ref[:, sl] = jnp.broadcast_to(jnp.sum(gl[:, sl], axis=-1, keepdims=True), (tr, HEAD))

    row = pl.BlockSpec((tr, GW), lambda i: (i, 0))
    f = jax.ShapeDtypeStruct((T, GW), F32)
    outs = _call(body, name="combine_bwd", grid=(T // tr,),
                 in_specs=[row] * 6 + [pl.BlockSpec((tr, GW), lambda i: (i, ZA // GW)), row],
                 out_specs=[row] * 7, out_shape=[f] * 6 + [jax.ShapeDtypeStruct((T, GW), MXU_DTYPE)],
                 )(*os, *ls, proj2, da)
    return outs[:3], outs[3:6], outs[6]


def _shift_down(u, j, t):
    return jnp.where(t >= j, pltpu.roll(u, j, 0), 0.0)


def _shift_up(u, j, t):
    n = u.shape[0]
    return jnp.where(t < n - j, pltpu.roll(u, n - j, 0), 0.0)


def _conv_specs(Bl, S, cw):
    def sec(c0):
        return pl.BlockSpec((1, S, cw), lambda j, b: (b, 0, c0 // cw + j))
    return [sec(CB), sec(CC), sec(CV), sec(ZC)], pl.BlockSpec((3, cw), lambda j, b: (0, j))


def _conv_fwd(proj3, conv_w):
    Bl, S, _ = proj3.shape
    cw = 256
    secs, wspec = _conv_specs(Bl, S, cw)

    def body(b_ref, c_ref, v_ref, z_ref, w_ref, o_ref):
        t = lax.broadcasted_iota(jnp.int32, (S, cw), 0)
        u = c_ref[0] * v_ref[0]
        y = w_ref[0:1, :] * u + w_ref[1:2, :] * _shift_down(u, 1, t) + w_ref[2:3, :] * _shift_down(u, 2, t)
        o_ref[0] = (b_ref[0] * y * _silu(z_ref[0])).astype(o_ref.dtype)

    return _call(body, name="conv_fwd", grid=(CONVW // cw, Bl), in_specs=secs + [wspec],
                 out_specs=pl.BlockSpec((1, S, cw), lambda j, b: (b, 0, j)),
                 out_shape=jax.ShapeDtypeStruct((Bl, S, CONVW), MXU_DTYPE))(proj3, proj3, proj3, proj3, conv_w)


def _conv_bwd(proj3, conv_w, dcc3):
    Bl, S, _ = proj3.shape
    cw = 256
    secs, wspec = _conv_specs(Bl, S, cw)

    def body(b_ref, c_ref, v_ref, z_ref, w_ref, d_ref, db_ref, dc_ref, dv_ref, dz_ref, dw_ref):
        t = lax.broadcasted_iota(jnp.int32, (S, cw), 0)
        bv, cv, vv, zv, dv = b_ref[0], c_ref[0], v_ref[0], z_ref[0], d_ref[0]
        u = cv * vv
        u1, u2 = _shift_down(u, 1, t), _shift_down(u, 2, t)
        y = w_ref[0:1, :] * u + w_ref[1:2, :] * u1 + w_ref[2:3, :] * u2
        sg = _sig(zv)
        sz = zv * sg
        gy = dv * bv * sz
        db_ref[0] = (dv * y * sz).astype(db_ref.dtype)
        dz_ref[0] = (dv * bv * y * sg * (1.0 + zv * (1.0 - sg))).astype(dz_ref.dtype)
        du = w_ref[0:1, :] * gy + w_ref[1:2, :] * _shift_up(gy, 1, t) + w_ref[2:3, :] * _shift_up(gy, 2, t)
        dc_ref[0] = (du * vv).astype(dc_ref.dtype)
        dv_ref[0] = (du * cv).astype(dv_ref.dtype)

        @pl.when(pl.program_id(1) == 0)
        def _():
            dw_ref[...] = jnp.zeros_like(dw_ref)

        dw_ref[0:1, :] += jnp.sum(gy * u, axis=0, keepdims=True)
        dw_ref[1:2, :] += jnp.sum(gy * u1, axis=0, keepdims=True)
        dw_ref[2:3, :] += jnp.sum(gy * u2, axis=0, keepdims=True)

    blk = pl.BlockSpec((1, S, cw), lambda j, b: (b, 0, j))
    shp = jax.ShapeDtypeStruct((Bl, S, CONVW), MXU_DTYPE)
    return _call(body, name="conv_bwd", grid=(CONVW // cw, Bl), in_specs=secs + [wspec, blk],
                 out_specs=[blk] * 4 + [wspec], out_shape=[shp] * 4 + [jax.ShapeDtypeStruct((3, CONVW), F32)],
                 )(proj3, proj3, proj3, proj3, conv_w, dcc3)


def _mem_specs(S, tq):
    q = pl.BlockSpec((1, tq, MEMW), lambda b, j: (b, j, MQ // MEMW))
    z = pl.BlockSpec((1, tq, MEMW), lambda b, j: (b, j, ZM // MEMW))
    kv = pl.BlockSpec((1, MEM_HD, 2 * MEMW), lambda b, j: (b, 0, 0))
    vec = pl.BlockSpec((1, MEM_HD), lambda b, j: (0, 0))
    blk = pl.BlockSpec((1, tq, MEMW), lambda b, j: (b, j, 0))
    return q, z, kv, vec, blk


def _mem_fwd(proj3, mkv3, gq, gk):
    Bl, S, _ = proj3.shape
    tq = _tile(S, 512)
    q, z, kv, vec, blk = _mem_specs(S, tq)

    def body(q_ref, z_ref, kv_ref, gq_ref, gk_ref, o_ref):
        o_ref[0] = _mem_block(q_ref[0], z_ref[0], kv_ref[0], gq_ref[...], gk_ref[...]).astype(o_ref.dtype)

    return _call(body, name="mem_fwd", grid=(Bl, S // tq), in_specs=[q, z, kv, vec, vec], out_specs=blk,
                 out_shape=jax.ShapeDtypeStruct((Bl, S, MEMW), MXU_DTYPE))(proj3, proj3, mkv3, gq, gk)


def _mem_bwd(proj3, mkv3, gq, gk, dmo3):
    Bl, S, _ = proj3.shape
    tq = _tile(S, 256)
    q, z, kv, vec, blk = _mem_specs(S, tq)

    def body(q_ref, z_ref, kv_ref, gq_ref, gk_ref, d_ref, dq_ref, dz_ref, dkv_ref, dgq_ref, dgk_ref):
        _, vjp = jax.vjp(_mem_block, q_ref[0], z_ref[0], kv_ref[0], gq_ref[...], gk_ref[...])
        dq, dz, dkv, dgq, dgk = vjp(d_ref[0])
        dq_ref[0] = dq.astype(dq_ref.dtype)
        dz_ref[0] = dz.astype(dz_ref.dtype)
        j = pl.program_id(1)

        @pl.when(j == 0)
        def _():
            dkv_ref[0] = jnp.zeros_like(dkv)

        @pl.when((j == 0) & (pl.program_id(0) == 0))
        def _():
            dgq_ref[...] = jnp.zeros_like(dgq_ref)
            dgk_ref[...] = jnp.zeros_like(dgk_ref)

        dkv_ref[0] += dkv
        dgq_ref[...] += dgq
        dgk_ref[...] += dgk

    shp = jax.ShapeDtypeStruct((Bl, S, MEMW), MXU_DTYPE)
    gshp = jax.ShapeDtypeStruct((1, MEM_HD), F32)
    return _call(body, name="mem_bwd", grid=(Bl, S // tq), in_specs=[q, z, kv, vec, vec, blk],
                 out_specs=[blk, blk, kv, vec, vec],
                 out_shape=[shp, shp, jax.ShapeDtypeStruct(mkv3.shape, F32), gshp, gshp],
                 )(proj3, proj3, mkv3, gq, gk, dmo3)


def _merge_specs(T, D, tm, tn):
    def act(w):
        return pl.BlockSpec((tm, w), lambda i, n: (i, 0))

    def wsp(w):
        return pl.BlockSpec((w, tn), lambda i, n: (0, n))

    gates = [pl.BlockSpec((tm, tn), lambda i, n, k=k: (i, (G0 + k * D) // tn + n)) for k in range(3)]
    tile = pl.BlockSpec((tm, tn), lambda i, n: (i, n))
    return act, wsp, gates, tile


def _merge_fwd(a, cc, mo, wa, wc, wm, proj2):
    T, D = a.shape[0], wa.shape[1]
    tm, tn = _tile(T, 512), _tile(D, 512)
    act, wsp, gates, tile = _merge_specs(T, D, tm, tn)

    def body(a_ref, c_ref, m_ref, wa_ref, wc_ref, wm_ref, g0, g1, g2, mg_ref, mt_ref, pa_ref, pc_ref, pm_ref):
        pa = jnp.dot(a_ref[...], wa_ref[...], preferred_element_type=F32)
        pc = jnp.dot(c_ref[...], wc_ref[...], preferred_element_type=F32)
        pm = jnp.dot(m_ref[...], wm_ref[...], preferred_element_type=F32)
        mg = _sig(g0[...]) * pa + _sig(g1[...]) * pc + _sig(g2[...]) * pm
        mg_ref[...] = mg.astype(mg_ref.dtype)
        mt_ref[...] = mg.T.astype(mt_ref.dtype)
        pa_ref[...] = pa.astype(pa_ref.dtype)
        pc_ref[...] = pc.astype(pc_ref.dtype)
        pm_ref[...] = pm.astype(pm_ref.dtype)

    shp = jax.ShapeDtypeStruct((T, D), MXU_DTYPE)
    return _call(body, name="merge_fwd", grid=(T // tm, D // tn),
                 in_specs=[act(GW), act(CONVW), act(MEMW), wsp(GW), wsp(CONVW), wsp(MEMW)] + gates,
                 out_specs=[tile, pl.BlockSpec((tn, tm), lambda i, n: (n, i)), tile, tile, tile],
                 out_shape=[shp, jax.ShapeDtypeStruct((D, T), MXU_DTYPE), shp, shp, shp],
                 )(a, cc, mo, wa, wc, wm, proj2, proj2, proj2)


def _merge_bwd(dyb, w_out, proj2, pa, pc, pm):
    T, D = dyb.shape
    tm, tn = _tile(T, 512), _tile(D, 512)
    _, _, gates, tile = _merge_specs(T, D, tm, tn)

    def body(dy_ref, w_ref, g0, g1, g2, p0, p1, p2, dp0, dp1, dp2, dg0, dg1, dg2):
        dm = lax.dot_general(dy_ref[...], w_ref[...], _DIMS["nt"], preferred_element_type=F32)
        for g_ref, p_ref, dp_ref, dg_ref in ((g0, p0, dp0, dg0), (g1, p1, dp1, dg1), (g2, p2, dp2, dg2)):
            gt = _sig(g_ref[...])
            dp_ref[...] = (gt * dm).astype(dp_ref.dtype)
            dg_ref[...] = (dm * p_ref[...].astype(F32) * gt * (1.0 - gt)).astype(dg_ref.dtype)

    shp = jax.ShapeDtypeStruct((T, D), MXU_DTYPE)
    return _call(body, name="merge_bwd", grid=(T // tm, D // tn),
                 in_specs=[pl.BlockSpec((tm, D), lambda i, n: (i, 0)), pl.BlockSpec((tn, D), lambda i, n: (n, 0))]
                 + gates + [tile] * 3,
                 out_specs=[tile] * 6, out_shape=[shp] * 6)(dyb, w_out, proj2, proj2, proj2, pa, pc, pm)


def _out_loss(merged, w_out, x, tgt):
    T, D = x.shape
    tm = _tile(T, 256)

    def body(m_ref, w_ref, x_ref, t_ref, dy_ref, dyb_ref, loss_ref):
        err = x_ref[...] + jnp.dot(m_ref[...], w_ref[...], preferred_element_type=F32) - t_ref[...]
        dy = err * (1.0 / D)
        dy_ref[...] = dy
        dyb_ref[...] = dy.astype(dyb_ref.dtype)

        @pl.when(pl.program_id(0) == 0)
        def _():
            loss_ref[...] = jnp.zeros_like(loss_ref)

        loss_ref[...] += jnp.sum(err * err) * (0.5 / D)

    row = pl.BlockSpec((tm, D), lambda i: (i, 0))
    return _call(body, name="out_loss", grid=(T // tm,),
                 in_specs=[row, pl.BlockSpec((D, D), lambda i: (0, 0)), row, row],
                 out_specs=[row, row, pl.BlockSpec((1, 128), lambda i: (0, 0))],
                 out_shape=[jax.ShapeDtypeStruct((T, D), F32), jax.ShapeDtypeStruct((T, D), MXU_DTYPE),
                            jax.ShapeDtypeStruct((1, 128), F32)])(merged, w_out, x, tgt)


def _proj_chunk(hb, w, order, j, buf, name):
    T, D = hb.shape
    Cs = w.shape[1] // 4
    tm, tn = _tile(T, 1024), _tile(Cs, 2176)
    nj = Cs // tn

    def body(order_ref, a_ref, b_ref, *rest):
        rest[-1][...] = jnp.dot(a_ref[...], b_ref[...], preferred_element_type=F32)

    in_specs = [pl.BlockSpec((tm, D), lambda n, i, o: (i, 0)), pl.BlockSpec((D, tn), lambda n, i, o: (0, j * nj + n))]
    args = [order, hb, w]
    if buf is not None:
        in_specs.append(ANY)
        args.append(buf)
    spec = pltpu.PrefetchScalarGridSpec(
        num_scalar_prefetch=1, grid=(nj, T // tm), in_specs=in_specs,
        out_specs=pl.BlockSpec((tm, tn), lambda n, i, o: (i, o[j] * nj + n)))
    return _call(body, name=name, grid_spec=spec, out_shape=jax.ShapeDtypeStruct((T, 4 * Cs), F32),
                 aliases={} if buf is None else {3: 0})(*args)


def _norms(x, mem, norm_g, mem_norm_g):
    D = x.shape[-1]
    hb, hbt = _rms_fwd(x.reshape(-1, D), norm_g.reshape(1, D), "rms_x")
    mhb, _ = _rms_fwd(mem.reshape(-1, D), mem_norm_g.reshape(1, D), "rms_mem")
    return hb, hbt, mhb


def _mix_fwd(proj2, Bl, gq_all, gk_all, conv_w):
    T, IN = proj2.shape
    proj3 = proj2.reshape(Bl, T // Bl, IN)
    os, ls = [], []
    for g, d in enumerate(DILATIONS):
        o, l = _attn_fwd(proj3, gq_all[g:g + 1], gk_all[g:g + 1], g, d)
        os.append(o.reshape(T, GW))
        ls.append(l.reshape(T, GW))
    a = _combine_fwd(os, ls, proj2)
    cc = _conv_fwd(proj3, conv_w).reshape(T, CONVW)
    return os, ls, a, cc


def _weight_grads(x, mem, tgt, norm_g, mem_norm_g, gq_all, gk_all, conv_w, mem_gq, mem_gk, W, pre, early=None):
    Bl, S, D = x.shape
    T = Bl * S
    hb, hbt, mhb, proj2, os, ls, a, cc = pre
    IN = proj2.shape[1]
    proj3 = proj2.reshape(Bl, S, IN)
    x2, tgt2 = x.reshape(T, D), tgt.reshape(T, D)
    mem2 = mem.reshape(-1, D)
    ng, mng = norm_g.reshape(1, D), mem_norm_g.reshape(1, D)
    mgq, mgk = mem_gq.reshape(1, MEM_HD), mem_gk.reshape(1, MEM_HD)
    gqs = [gq_all[g:g + 1] for g in range(NGROUP)]
    gks = [gk_all[g:g + 1] for g in range(NGROUP)]

    mkv = _matmul(mhb, W["mem_w_kv"], "nn", F32, name="mem_kv", tm=512, tn=1024, tk=D)
    mkv3 = mkv.reshape(Bl, -1, 2 * MEMW)
    mo = _mem_fwd(proj3, mkv3, mgq, mgk).reshape(T, MEMW)
    merged, mergedt, pa, pc, pm = _merge_fwd(a, cc, mo, W["w_br_attn"], W["w_br_conv"], W["w_br_mem"], proj2)
    dy, dyb, loss = _out_loss(merged, W["w_out"], x2, tgt2)

    G = {}
    G["w_out"] = _matmul(mergedt, dyb, "nn", WIRE_DTYPE, name="dw_out", tm=1024, tn=512, tk=T)
    dpa, dpc, dpm, dg0, dg1, dg2 = _merge_bwd(dyb, W["w_out"], proj2, pa, pc, pm)
    G["w_br_attn"] = _matmul(a, dpa, "tn", WIRE_DTYPE, name="dw_br_attn", tm=512, tn=1024, tk=512)
    G["w_br_conv"] = _matmul(cc, dpc, "tn", WIRE_DTYPE, name="dw_br_conv", tm=1024, tn=1024, tk=512)
    G["w_br_mem"] = _matmul(mo, dpm, "tn", WIRE_DTYPE, name="dw_br_mem", tm=1024, tn=1024, tk=512)
    da = _matmul(dpa, W["w_br_attn"], "nt", F32, name="d_attn", tm=1024, tn=512, tk=D)
    dcc = _matmul(dpc, W["w_br_conv"], "nt", F32, name="d_conv", tm=1024, tn=1024, tk=D)
    dmo = _matmul(dpm, W["w_br_mem"], "nt", F32, name="d_mem", tm=1024, tn=1024, tk=D)
    dmq, dzm, dmkv3, dmgq, dmgk = _mem_bwd(proj3, mkv3, mgq, mgk, dmo.reshape(Bl, S, MEMW))
    dmkv = _cast(dmkv3.reshape(-1, 2 * MEMW), "cast_dmkv")
    G["mem_w_kv"] = _matmul(mhb, dmkv, "tn", WIRE_DTYPE, name="dw_mem_kv", tm=1024, tn=1024, tk=512)
    early_state, da = (None, da) if early is None else early(G, da)
    dmh = _matmul(dmkv, W["mem_w_kv"], "nt", F32, name="d_memh", tm=512, tn=1024, tk=2 * MEMW)
    _, dmng = _rms_bwd(mem2, dmh, mng, None, "rms_mem_bwd")

    dos, dls, dza = _combine_bwd(os, ls, proj2, da)
    dqs, dks, dvs, dgq, dgk = [], [], [], [], []
    for g, d in enumerate(DILATIONS):
        dq, dk, dv, gq_g, gk_g = _attn_bwd(proj3, gqs[g], gks[g], os[g].reshape(Bl, S, GW), ls[g].reshape(Bl, S, GW),
                                           dos[g].reshape(Bl, S, GW), dls[g].reshape(Bl, S, GW), g, d)
        dqs.append(dq.reshape(T, GW).astype(MXU_DTYPE))
        dks.append(dk.reshape(T, GW).astype(MXU_DTYPE))
        dvs.append(dv.reshape(T, GW).astype(MXU_DTYPE))
        dgq.append(gq_g)
        dgk.append(gk_g)
    dcb, dcc_, dcv, dzc, dconv_w = _conv_bwd(proj3, conv_w, dcc.reshape(Bl, S, CONVW))

    dproj = jnp.concatenate(dqs + dks + dvs + [dza] + [t.reshape(T, CONVW) for t in (dcb, dcc_, dcv, dzc)]
                            + [dmq.reshape(T, MEMW), dzm.reshape(T, MEMW), dg0, dg1, dg2], axis=1)
    small = [loss, None, dmng] + dgq + dgk + [dconv_w.reshape(1, 3 * CONVW), dmgq, dmgk]
    return G, (dproj, x2, ng, dy, small), early_state


def _dw_in_half(hbt, dproj, pos, own, name):
    D, T = hbt.shape
    IN = dproj.shape[1]
    R, tn = D // 2, _tile(IN, 512)

    def body(pos_ref, a_ref, b_ref, o_ref):
        o_ref[...] = jnp.dot(a_ref[...], b_ref[...], preferred_element_type=F32).astype(o_ref.dtype)

    spec = pltpu.PrefetchScalarGridSpec(
        num_scalar_prefetch=1, grid=(IN // tn,),
        in_specs=[pl.BlockSpec((R, T), lambda j, p: (p[1] if own else 1 - p[1], 0)),
                  pl.BlockSpec((T, tn), lambda j, p: (0, j))],
        out_specs=pl.BlockSpec((R, tn), lambda j, p: (0, j)))
    return _call(body, name=name, grid_spec=spec, out_shape=jax.ShapeDtypeStruct((R, IN), WIRE_DTYPE))(pos, hbt, dproj)


def _d_h(dproj, w, order):
    T, IN = dproj.shape
    D, Cs = w.shape[0], IN // 4
    tm, tn = _tile(T, 1024), _tile(D, 1024)

    def body(order_ref, a_ref, b_ref, o_ref, acc_ref):
        part = lax.dot_general(a_ref[...], b_ref[...], _DIMS["nt"], preferred_element_type=F32)
        k = pl.program_id(2)

        @pl.when(k == 0)
        def _():
            acc_ref[...] = part

        @pl.when(k > 0)
        def _():
            acc_ref[...] += part

        @pl.when(k == 3)
        def _():
            o_ref[...] = acc_ref[...]

    spec = pltpu.PrefetchScalarGridSpec(
        num_scalar_prefetch=1, grid=(T // tm, D // tn, 4),
        in_specs=[pl.BlockSpec((tm, Cs), lambda i, n, k, o: (i, o[k])), pl.BlockSpec((tn, Cs), lambda i, n, k, o: (n, k))],
        out_specs=pl.BlockSpec((tm, tn), lambda i, n, k, o: (i, n)), scratch_shapes=[pltpu.VMEM((tm, tn), F32)])
    return _call(body, name="d_h", grid_spec=spec, out_shape=jax.ShapeDtypeStruct((T, D), F32))(order, dproj, w)


def _input_grad(rest, w_in, order):
    dproj, x2, ng, dy, small = rest
    dh = _d_h(dproj, w_in, order)
    grad_x, dng = _rms_bwd(x2, dh, ng, dy, "rms_x_bwd")
    small = [dng if t is None else t for t in small]
    return grad_x, jnp.concatenate(small, axis=1)


def _local_step(x, mem, tgt, norm_g, mem_norm_g, gq_all, gk_all, conv_w, mem_gq, mem_gk, W):
    hb, hbt, mhb = _norms(x, mem, norm_g, mem_norm_g)
    Cs = W["w_in"].shape[1] // 4
    shards = (0, 2, 1, 3)
    order = jnp.array(shards, dtype=jnp.int32)
    w_rel = jnp.concatenate([W["w_in"][:, s * Cs:(s + 1) * Cs] for s in shards], axis=1)
    proj2 = None
    for j in range(4):
        proj2 = _proj_chunk(hb, w_rel, order, j, proj2, f"proj_{j}")
    pre = (hb, hbt, mhb, proj2, *_mix_fwd(proj2, x.shape[0], gq_all, gk_all, conv_w))
    G, rest, _ = _weight_grads(x, mem, tgt, norm_g, mem_norm_g, gq_all, gk_all, conv_w, mem_gq, mem_gk, W, pre)
    pos = jnp.zeros((2,), jnp.int32)
    G["w_in"] = jnp.concatenate([_dw_in_half(hbt, rest[0], pos, True, "dw_in_own"),
                                 _dw_in_half(hbt, rest[0], pos, False, "dw_in_sibling")], axis=0)
    grad_x, small = _input_grad(rest, w_rel, order)
    return grad_x.reshape(x.shape), G, small


BIG = (("w_in", "col"), ("mem_w_kv", "row"), ("w_br_attn", "col"), ("w_br_conv", "col"),
       ("w_br_mem", "col"), ("w_out", "row"))


def _coords():
    return lax.axis_index("x"), lax.axis_index("y"), lax.axis_index("c")


def _other_chips(x, y):
    return [(1 - x, y), (x, 1 - y), (1 - x, 1 - y)]


def _half(ref, kind, c):
    R, C = ref.shape
    if kind == "col":
        return ref.at[pl.ds(c * (R // 2), R // 2), :]
    return ref.at[:, pl.ds(c * (C // 2), C // 2)]


def _shard(ref, kind, s):
    R, C = ref.shape
    if kind == "col":
        return ref.at[:, pl.ds(s * (C // 4), C // 4)]
    return ref.at[pl.ds(s * (R // 4), R // 4), :]


def _piece(ref, kind, s, c):
    R, C = ref.shape
    if kind == "col":
        return ref.at[pl.ds(c * (R // 2), R // 2), pl.ds(s * (C // 4), C // 4)]
    return ref.at[pl.ds(s * (R // 4), R // 4), pl.ds(c * (C // 2), C // 2)]


def _remote(src, dst, sems_s, sems_r, k, dev):
    return pltpu.make_async_remote_copy(src_ref=src, dst_ref=dst, send_sem=sems_s.at[k], recv_sem=sems_r.at[k],
                                        device_id=dev, device_id_type=MESH)


HBM = pl.BlockSpec(memory_space=pltpu.HBM)
SEM = pl.BlockSpec(memory_space=pltpu.SEMAPHORE)
EFFECT = pltpu.SideEffectType.DATAFLOW_SIDE_EFFECTING


def _hbm(a):
    return pltpu.with_memory_space_constraint(a, pltpu.HBM)


def _start_copies(name, arrays, ncopies, make):
    n = len(arrays)

    def body(*refs):
        for cp in make(refs[:n], refs[n], refs[n + 1]):
            cp.start()

    outs = pl.pallas_call(
        body, name=name,
        out_shape=(pltpu.SemaphoreType.DMA((ncopies,)), pltpu.SemaphoreType.DMA((ncopies,)),
                   *[jax.ShapeDtypeStruct(t.shape, t.dtype) for t in arrays]),
        in_specs=[HBM] * n, out_specs=(SEM, SEM, *([HBM] * n)),
        input_output_aliases={i: i + 2 for i in range(n)},
        compiler_params=pltpu.CompilerParams(has_side_effects=EFFECT),
    )(*[_hbm(t) for t in arrays])
    return outs[0], outs[1], list(outs[2:])


def _wait_copies(name, send, recv, arrays, make, after):
    n = len(arrays)

    def body(*refs):
        for cp in make(refs[:n], refs[n], refs[n + 1]):
            cp.wait_send()
            cp.wait_recv()

    outs = pl.pallas_call(
        body, name=name, out_shape=[jax.ShapeDtypeStruct(t.shape, t.dtype) for t in arrays],
        in_specs=[HBM] * n + [SEM, SEM, ANY], out_specs=[HBM] * n,
        input_output_aliases={i: i for i in range(n)},
        compiler_params=pltpu.CompilerParams(has_side_effects=EFFECT),
    )(*arrays, send, recv, after)
    return list(outs)


def _w_in_copies(relations):
    def make(refs, send, recv):
        x, y, c = _coords()
        me = 2 * x + y
        chips = _other_chips(x, y)
        w, conv = refs[0], refs[1]
        cps = []
        for i, k in enumerate(relations):
            cps.append(_remote(_piece(w, "col", 0, c), _piece(w, "col", 1 + k, c), send, recv, 2 * i, (*chips[k], c)))
            mine = _shard(conv, "col", me)
            cps.append(_remote(mine, mine, send, recv, 2 * i + 1, (*chips[k], c)))
        return cps
    return make


def _w_in_landed(relations):
    return lambda refs, c: [_piece(refs[0], "col", 1 + k, c) for k in relations]


def _sibling_copy(refs, send, recv):
    x, y, c = _coords()
    return [_remote(refs[0], refs[1], send, recv, 0, (x, y, 1 - c))]


def _other_weight_copies(refs, send, recv):
    x, y, c = _coords()
    me = 2 * x + y
    cps = []
    for k, chip in enumerate(_other_chips(x, y)):
        for p, (_, kind) in enumerate(BIG[1:]):
            mine = _piece(refs[p], kind, me, c)
            cps.append(_remote(mine, mine, send, recv, 3 * p + k, (*chip, c)))
    return cps


def _sibling_forward(name, arrays, ncp, halves):
    n = len(arrays)

    def body(*refs):
        outs = refs[n:2 * n]
        send, recv = refs[2 * n:]
        x, y, c = _coords()
        sib = (x, y, 1 - c)
        cps = [_remote(got, got, send, recv, i, sib) for i, got in enumerate(halves(outs, c))]
        for cp in cps:
            cp.start()
        for cp in cps:
            cp.wait_send()
        for i, got in enumerate(halves(outs, 1 - c)):
            _remote(got, got, send, recv, i, sib).wait_recv()

    return pl.pallas_call(
        body, name=name, out_shape=[jax.ShapeDtypeStruct(t.shape, t.dtype) for t in arrays],
        in_specs=[ANY] * n, out_specs=[ANY] * n, input_output_aliases={i: i for i in range(n)},
        scratch_shapes=[pltpu.SemaphoreType.DMA((ncp,)), pltpu.SemaphoreType.DMA((ncp,))],
    )(*arrays)


def _landed_halves(refs, c):
    return [_half(r, "col", c) for r in refs]


def _other_weight_halves(refs, c):
    x, y, _ = _coords()
    out = []
    for chip in _other_chips(x, y):
        s = 2 * chip[0] + chip[1]
        out += [_piece(refs[p], kind, s, c) for p, (_, kind) in enumerate(BIG[1:])]
    return out


def _sibling_exchange(grads, group, name):
    n = len(group)
    shapes = []
    for (_, kind), g in zip(group, grads):
        R, C = g.shape
        shapes.append(jax.ShapeDtypeStruct((R // 2, C) if kind == "col" else (R, C // 2), g.dtype))

    def body(*refs):
        ins, outs = refs[:n], refs[n:2 * n]
        send, recv = refs[2 * n:]
        x, y, c = _coords()
        sib = (x, y, 1 - c)
        cps = [_remote(_half(ins[p], group[p][1], 1 - c), outs[p], send, recv, p, sib) for p in range(n)]
        for cp in cps:
            cp.start()
        for cp in cps:
            cp.wait()

    return pl.pallas_call(
        body, name=name, out_shape=shapes, in_specs=[ANY] * n, out_specs=[ANY] * n,
        scratch_shapes=[pltpu.SemaphoreType.DMA((n,)), pltpu.SemaphoreType.DMA((n,))],
    )(*grads)


def _presum(g, got, kind, pos, name):
    R, C = got.shape
    tr, tc = _tile(R, 512, 16), _tile(C, 2048)
    nr, nc = R // tr, C // tc

    def body(pos_ref, a_ref, b_ref, o_ref):
        o_ref[...] = (a_ref[...].astype(F32) + b_ref[...].astype(F32)).astype(o_ref.dtype)

    blk = pl.BlockSpec((tr, tc), lambda i, j, pos_ref: (i, j))
    if g.shape == got.shape:
        mine = blk
    elif kind == "col":
        mine = pl.BlockSpec((tr, tc), lambda i, j, pos_ref: (pos_ref[1] * nr + i, j))
    else:
        mine = pl.BlockSpec((tr, tc), lambda i, j, pos_ref: (i, pos_ref[1] * nc + j))
    spec = pltpu.PrefetchScalarGridSpec(num_scalar_prefetch=1, grid=(nr, nc), in_specs=[mine, blk], out_specs=blk)
    return _call(body, name=name, grid_spec=spec, out_shape=jax.ShapeDtypeStruct((R, C), WIRE_DTYPE))(pos, g, got)


def _chip_copies(group):
    n = len(group)

    def make(refs, send, recv):
        x, y, c = _coords()
        cps = []
        for k, chip in enumerate(_other_chips(x, y)):
            s = 2 * chip[0] + chip[1]
            for p in range(n):
                cps.append(_remote(_shard(refs[p], group[p][1], s), refs[n + p].at[k], send, recv, 3 * p + k, (*chip, c)))
        return cps
    return make


def _landing_zones(pres, group):
    lands = []
    for (_, kind), g in zip(group, pres):
        R, C = g.shape
        lands.append(lax.empty((3, R, C // 4) if kind == "col" else (3, R // 4, C), g.dtype))
    return lands


def _exchange_start(G, group, pos, carry, tag, pres=None):
    n = len(group)
    if pres is None:
        parts = [G[name] for name, _ in group]
        got = _sibling_exchange(parts, group, "sibling_exchange_" + tag)
        pres = [_presum(parts[p], got[p], kind, pos, "presum_" + name) for p, (name, kind) in enumerate(group)]
    make = _chip_copies(group)
    send, recv, thru = _start_copies("chip_exchange_start_" + tag, [*pres, *_landing_zones(pres, group), carry], 3 * n, make)
    return (send, recv, thru[:2 * n], make, tag), thru[2 * n]


def _exchange_wait(state, after):
    send, recv, arrays, make, tag = state
    thru = _wait_copies("chip_exchange_wait_" + tag, send, recv, arrays, make, after)
    n = len(thru) // 2
    return thru[:n], thru[n:]


def _reduce_into_shard(slots, pre, kind, pos, name):
    K, R, C = slots.shape
    tr, tc = _tile(R, 512, 16), _tile(C, 2176)
    nr, nc = R // tr, C // tc

    def body(pos_ref, s_ref, p_ref, o_ref):
        acc = p_ref[...].astype(F32)
        for k in range(K):
            acc = acc + s_ref[k].astype(F32)
        o_ref[...] = acc

    if kind == "col":
        own = pl.BlockSpec((tr, tc), lambda i, j, pos_ref: (i, pos_ref[0] * nc + j))
        full, out = (2 * R, C), pl.BlockSpec((tr, tc), lambda i, j, pos_ref: (pos_ref[1] * nr + i, j))
    else:
        own = pl.BlockSpec((tr, tc), lambda i, j, pos_ref: (pos_ref[0] * nr + i, j))
        full, out = (R, 2 * C), pl.BlockSpec((tr, tc), lambda i, j, pos_ref: (i, pos_ref[1] * nc + j))
    spec = pltpu.PrefetchScalarGridSpec(
        num_scalar_prefetch=1, grid=(nr, nc),
        in_specs=[pl.BlockSpec((K, tr, tc), lambda i, j, pos_ref: (0, i, j)), own], out_specs=out)
    return _call(body, name=name, grid_spec=spec, out_shape=jax.ShapeDtypeStruct(full, F32))(pos, slots, pre)


def _share_reduced(reds):
    n = len(BIG)

    def body(*refs):
        outs = refs[n:2 * n]
        send, recv = refs[2 * n:]
        x, y, c = _coords()
        sib = (x, y, 1 - c)
        cps = []
        for p in range(n):
            mine = _half(outs[p], BIG[p][1], c)
            cps.append(_remote(mine, mine, send, recv, p, sib))
        for cp in cps:
            cp.start()
        for cp in cps:
            cp.wait_send()
        for p in range(n):
            got = _half(outs[p], BIG[p][1], 1 - c)
            _remote(got, got, send, recv, p, sib).wait_recv()

    return pl.pallas_call(
        body, name="share_reduced", out_shape=[jax.ShapeDtypeStruct(r.shape, r.dtype) for r in reds],
        in_specs=[ANY] * n, out_specs=[ANY] * n, input_output_aliases={p: p for p in range(n)},
        scratch_shapes=[pltpu.SemaphoreType.DMA((n,)), pltpu.SemaphoreType.DMA((n,))],
    )(*reds)


def _gather_small(pack):
    _, N = pack.shape

    def body(in_ref, out_ref, send, recv, loc):
        x, y, c = _coords()
        me = 4 * x + 2 * y + c
        own = pltpu.make_async_copy(in_ref, out_ref.at[me], loc)
        own.start()
        cps = []
        for k in range(1, 8):
            dev = (x ^ (k >> 2), y ^ ((k >> 1) & 1), c ^ (k & 1))
            cps.append(_remote(in_ref, out_ref.at[me], send, recv, k - 1, dev))
        for cp in cps:
            cp.start()
        for k in range(1, 8):
            src = 4 * (x ^ (k >> 2)) + 2 * (y ^ ((k >> 1) & 1)) + (c ^ (k & 1))
            _remote(in_ref, out_ref.at[src], send, recv, k - 1, (x, y, c)).wait_recv()
        for cp in cps:
            cp.wait_send()
        own.wait()

    return pl.pallas_call(
        body, name="gather_small", out_shape=jax.ShapeDtypeStruct((8, 1, N), pack.dtype),
        in_specs=[ANY], out_specs=ANY,
        scratch_shapes=[pltpu.SemaphoreType.DMA((7,)), pltpu.SemaphoreType.DMA((7,)), pltpu.SemaphoreType.DMA(())],
    )(pack)


def _sum_small(slots):
    K, _, N = slots.shape

    def body(s_ref, o_ref):
        acc = s_ref[0]
        for k in range(1, K):
            acc = acc + s_ref[k]
        o_ref[...] = acc

    return _call(body, name="sum_small", in_specs=[pl.BlockSpec(memory_space=pltpu.VMEM)],
                 out_specs=pl.BlockSpec(memory_space=pltpu.VMEM), out_shape=jax.ShapeDtypeStruct((1, N), F32))(slots)


def _adamw(w, g, m, v, name):
    R, C = w.shape
    tr, tc = _tile(R, 256, 8), _tile(C, 2176)

    def body(w_ref, g_ref, m_ref, v_ref, d_ref, nm_ref, nv_ref):
        gv = g_ref[...]
        nm = ADAM_B1 * m_ref[...] + (1.0 - ADAM_B1) * gv
        nv = ADAM_B2 * v_ref[...] + (1.0 - ADAM_B2) * gv * gv
        m_hat = nm / (1.0 - ADAM_B1 ** ADAM_STEP)
        v_hat = nv / (1.0 - ADAM_B2 ** ADAM_STEP)
        d_ref[...] = -ADAM_LR * (m_hat / (jnp.sqrt(v_hat) + ADAM_EPS) + ADAM_WD * w_ref[...])
        nm_ref[...] = nm
        nv_ref[...] = nv

    spec = pl.BlockSpec((tr, tc), lambda i, j: (i, j))
    shp = jax.ShapeDtypeStruct((R, C), F32)
    return _call(body, name=name, grid=(R // tr, C // tc), in_specs=[spec] * 4, out_specs=[spec] * 3,
                 out_shape=[shp] * 3)(w, g, m, v)


SMALL = ("norm_g", "mem_norm_g", "attn_q_norm", "attn_k_norm", "conv_w", "mem_q_norm", "mem_k_norm")
WEIGHTS = ("norm_g", "mem_norm_g", "w_in", "attn_q_norm", "attn_k_norm", "conv_w", "mem_w_kv", "mem_q_norm",
           "mem_k_norm", "w_br_attn", "w_br_conv", "w_br_mem", "w_out")


def kernel(x, mem, norm_g, mem_norm_g, w_in, attn_q_norm, attn_k_norm, conv_w, mem_w_kv, mem_q_norm, mem_k_norm, w_br_attn, w_br_conv, w_br_mem, w_out, loss_target, m_norm_g, m_mem_norm_g, m_w_in, m_attn_q_norm, m_attn_k_norm, m_conv_w, m_mem_w_kv, m_mem_q_norm, m_mem_k_norm, m_w_br_attn, m_w_br_conv, m_w_br_mem, m_w_out, v_norm_g, v_mem_norm_g, v_w_in, v_attn_q_norm, v_attn_k_norm, v_conv_w, v_mem_w_kv, v_mem_q_norm, v_mem_k_norm, v_w_br_attn, v_w_br_conv, v_w_br_mem, v_w_out):
    w = dict(norm_g=norm_g, mem_norm_g=mem_norm_g, w_in=w_in, attn_q_norm=attn_q_norm, attn_k_norm=attn_k_norm,
             conv_w=conv_w, mem_w_kv=mem_w_kv, mem_q_norm=mem_q_norm, mem_k_norm=mem_k_norm, w_br_attn=w_br_attn,
             w_br_conv=w_br_conv, w_br_mem=w_br_mem, w_out=w_out)
    m = dict(norm_g=m_norm_g, mem_norm_g=m_mem_norm_g, w_in=m_w_in, attn_q_norm=m_attn_q_norm,
             attn_k_norm=m_attn_k_norm, conv_w=m_conv_w, mem_w_kv=m_mem_w_kv, mem_q_norm=m_mem_q_norm,
             mem_k_norm=m_mem_k_norm, w_br_attn=m_w_br_attn, w_br_conv=m_w_br_conv, w_br_mem=m_w_br_mem, w_out=m_w_out)
    v = dict(norm_g=v_norm_g, mem_norm_g=v_mem_norm_g, w_in=v_w_in, attn_q_norm=v_attn_q_norm,
             attn_k_norm=v_attn_k_norm, conv_w=v_conv_w, mem_w_kv=v_mem_w_kv, mem_q_norm=v_mem_q_norm,
             mem_k_norm=v_mem_k_norm, w_br_attn=v_w_br_attn, w_br_conv=v_w_br_conv, w_br_mem=v_w_br_mem, w_out=v_w_out)
    Bl, _, D = x.shape
    cx, cy = lax.axis_index("x"), lax.axis_index("y")
    chip = 2 * cx + cy
    pos = jnp.stack([chip, lax.axis_index("c")]).astype(jnp.int32)
    order = jnp.stack([chip] + [2 * a + b for a, b in _other_chips(cx, cy)]).astype(jnp.int32)
    n = len(BIG)

    w_rel = _place_shard(w["w_in"], "col", jnp.zeros((1,), jnp.int32), WIRE_DTYPE, "place_w_in")
    conv_full = _place_shard(conv_w, "col", pos, F32, "place_conv_w")
    others = [_place_shard(w[name], kind, pos, WIRE_DTYPE, "place_" + name) for name, kind in BIG[1:]]
    hb, hbt, mhb = _norms(x, mem, norm_g, mem_norm_g)

    near = _w_in_copies((0, 1))
    send, recv, (w_rel, conv_full) = _start_copies("gather_near_start", [w_rel, conv_full], 4, near)
    proj = _proj_chunk(hb, w_rel, order, 0, None, "proj_own")
    w_rel, conv_full = _wait_copies("gather_near_wait", send, recv, [w_rel, conv_full], near, proj)
    w_rel, = _sibling_forward("gather_near_forward", [w_rel], 2, _w_in_landed((0, 1)))

    far = _w_in_copies((2,))
    send, recv, (w_rel, conv_full) = _start_copies("gather_far_start", [w_rel, conv_full], 2, far)
    proj = _proj_chunk(hb, w_rel, order, 1, proj, "proj_near_x")
    proj = _proj_chunk(hb, w_rel, order, 2, proj, "proj_near_y")
    w_rel, conv_full = _wait_copies("gather_far_wait", send, recv, [w_rel, conv_full], far, proj)
    w_rel, = _sibling_forward("gather_far_forward", [w_rel], 1, _w_in_landed((2,)))

    send, recv, (*others, w_rel) = _start_copies("gather_rest_start", [*others, w_rel], 3 * (n - 1), _other_weight_copies)
    proj = _proj_chunk(hb, w_rel, order, 3, proj, "proj_far")
    mixed = _mix_fwd(proj, Bl, attn_q_norm, attn_k_norm, conv_full)
    *others, w_rel = _wait_copies("gather_rest_wait", send, recv, [*others, w_rel], _other_weight_copies, mixed[2])
    others = _sibling_forward("gather_rest_forward", others, 3 * (n - 1), _other_weight_halves)
    W = {name: others[p] for p, (name, _) in enumerate(BIG[1:])}

    G, rest, rest_state = _weight_grads(
        x, mem, loss_target, norm_g, mem_norm_g, attn_q_norm, attn_k_norm, conv_full, mem_q_norm, mem_k_norm, W,
        (hb, hbt, mhb, proj, *mixed), early=lambda G, carry: _exchange_start(G, BIG[1:], pos, carry, "rest"))

    for_sibling = _dw_in_half(hbt, rest[0], pos, False, "dw_in_sibling")
    send, recv, (for_sibling, got, dproj) = _start_copies(
        "sibling_w_in_start", [for_sibling, lax.empty(for_sibling.shape, for_sibling.dtype), rest[0]], 1, _sibling_copy)
    mine = _dw_in_half(hbt, dproj, pos, True, "dw_in_own")
    for_sibling, got = _wait_copies("sibling_w_in_wait", send, recv, [for_sibling, got], _sibling_copy, mine)
    pre_w_in = _presum(mine, got, "col", pos, "presum_w_in")

    w_in_state, dproj = _exchange_start(G, BIG[:1], pos, dproj, "w_in", pres=[pre_w_in])
    grad_x, small = _input_grad((dproj, *rest[1:]), w_rel, order)
    pres, slots = _exchange_wait(w_in_state, grad_x)
    pres_rest, slots_rest = _exchange_wait(rest_state, grad_x)
    pres, slots = pres + pres_rest, slots + slots_rest
    grad_x = grad_x.reshape(x.shape)
    reds = [_reduce_into_shard(slots[p], pres[p], kind, pos, "reduce_" + name) for p, (name, kind) in enumerate(BIG)]
    grads = dict(zip([name for name, _ in BIG], _share_reduced(reds)))

    tot = _sum_small(_gather_small(small))[0]
    loss = tot[0]
    off = 128
    for name, size in (("norm_g", D), ("mem_norm_g", D), ("attn_q_norm", NGROUP * HEAD), ("attn_k_norm", NGROUP * HEAD),
                       ("conv_w", 3 * CONVW), ("mem_q_norm", MEM_HD), ("mem_k_norm", MEM_HD)):
        grads[name] = tot[off:off + size]
        off += size
    cw = conv_w.shape[1]
    grads["conv_w"] = lax.dynamic_slice(grads["conv_w"].reshape(3, CONVW), (0, chip * cw), (3, cw))
    for name in SMALL:
        grads[name] = grads[name].reshape(w[name].shape)

    delta, new_m, new_v = {}, {}, {}
    for name, _ in BIG:
        delta[name], new_m[name], new_v[name] = _adamw(w[name], grads[name], m[name], v[name], "adamw_" + name)

    def packed(t):
        return jnp.concatenate([t[name].reshape(1, -1) for name in SMALL], axis=1)

    ds, ms, vs = _adamw(packed(w), packed(grads), packed(m), packed(v), "adamw_small")
    off = 0
    for name in SMALL:
        size = w[name].size
        delta[name] = ds[0, off:off + size].reshape(w[name].shape)
        new_m[name] = ms[0, off:off + size].reshape(w[name].shape)
        new_v[name] = vs[0, off:off + size].reshape(w[name].shape)
        off += size

    return (loss, grad_x, *[grads[n] for n in WEIGHTS], *[delta[n] for n in WEIGHTS],
            *[new_m[n] for n in WEIGHTS], *[new_v[n] for n in WEIGHTS])
```

```python
import functools

import jax
import jax.numpy as jnp
from jax import lax
from jax.experimental import pallas as pl
from jax.experimental.pallas import tpu as pltpu

F32 = jnp.float32
MXU_DTYPE = jnp.bfloat16
WIRE_DTYPE = jnp.bfloat16
EPS = 1e-6
NEG = -1e30

HEAD = 128
HPG = 4
GW = HPG * HEAD
DILATIONS = (1, 4, 16)
NGROUP = len(DILATIONS)
BLK = 128
QKV = NGROUP * GW
CONVW = 1024
MEM_HEADS = 4
MEM_HD = 256
MEMW = MEM_HEADS * MEM_HD
Q0, K0, V0 = 0, QKV, 2 * QKV
ZA = 3 * QKV
CB, CC, CV, ZC = ZA + GW, ZA + GW + CONVW, ZA + GW + 2 * CONVW, ZA + GW + 3 * CONVW
MQ = ZC + CONVW
ZM = MQ + MEMW
G0 = ZM + MEMW

ADAM_LR, ADAM_B1, ADAM_B2, ADAM_EPS, ADAM_WD, ADAM_STEP = 0.001, 0.9, 0.999, 1e-08, 0.01, 10

VMEM_LIMIT = 56 * 1024 * 1024
MESH = pl.DeviceIdType.MESH
ANY = pl.BlockSpec(memory_space=pl.ANY)


def _tile(n, pref, mult=128):
    t = min(pref, n)
    while t > mult and (n % t or t % mult):
        t -= mult
    assert n % t == 0, (n, pref)
    return t


def _call(body, *, name, out_shape, grid=(), in_specs=None, out_specs=None, scratch_shapes=(),
          aliases=None, grid_spec=None):
    kw = {}
    if grid_spec is not None:
        kw["grid_spec"] = grid_spec
        ngrid = len(grid_spec.grid)
    else:
        kw.update(grid=grid, in_specs=in_specs, out_specs=out_specs, scratch_shapes=list(scratch_shapes))
        ngrid = len(grid)
    params = pltpu.CompilerParams(dimension_semantics=("arbitrary",) * ngrid, vmem_limit_bytes=VMEM_LIMIT)
    return pl.pallas_call(body, name=name, out_shape=out_shape, compiler_params=params,
                          input_output_aliases=aliases or {}, **kw)


_DIMS = {"nn": (((1,), (0,)), ((), ())), "nt": (((1,), (1,)), ((), ())), "tn": (((0,), (0,)), ((), ()))}


def _mxu(a, b, mode):
    return lax.dot_general(a.astype(MXU_DTYPE), b.astype(MXU_DTYPE), _DIMS[mode], preferred_element_type=F32)


@functools.partial(jax.custom_vjp, nondiff_argnums=(2,))
def _dot(a, b, mode):
    return _mxu(a, b, mode)


def _dot_fwd(a, b, mode):
    return _mxu(a, b, mode), (a, b)


def _dot_bwd(mode, res, g):
    a, b = res
    if mode == "nn":
        return _mxu(g, b, "nt"), _mxu(a, g, "tn")
    if mode == "nt":
        return _mxu(g, b, "nn"), _mxu(g, a, "tn")
    return _mxu(b, g, "nt"), _mxu(a, g, "nn")


_dot.defvjp(_dot_fwd, _dot_bwd)


def _sig(z):
    return 1.0 / (1.0 + jnp.exp(-z))


def _silu(z):
    return z * _sig(z)


def _rms_rows(t, g):
    return t * lax.rsqrt(jnp.mean(t * t, axis=-1, keepdims=True) + EPS) * g


def _attn_block(q, k2, v2, gq, gk, first):
    qn = _rms_rows(q, gq)
    kn = _rms_rows(k2, gk)
    s = jnp.where(_band_mask(first), _dot(qn, kn, "nt") * (HEAD ** -0.5), NEG)
    m = lax.stop_gradient(jnp.max(s, axis=-1, keepdims=True))
    p = jnp.exp(s - m)
    den = jnp.sum(p, axis=-1, keepdims=True)
    o = _dot(p, v2, "nn") / den
    return o, m + jnp.log(den)


def _band_mask(first):
    a = lax.broadcasted_iota(jnp.int32, (BLK, 2 * BLK), 0)
    b = lax.broadcasted_iota(jnp.int32, (BLK, 2 * BLK), 1)
    return (b >= a) & (b <= a + BLK) & (b >= jnp.where(first, BLK, 0))


def _norm_parts(t):
    r = lax.rsqrt(jnp.mean(t * t, axis=-1, keepdims=True) + EPS)
    return r, t * r


def _norm_bwd(dn, g, r, th):
    dth = dn * g
    return r * (dth - th * jnp.mean(dth * th, axis=-1, keepdims=True)), jnp.sum(dn * th, axis=0, keepdims=True)


def _attn_block_bwd(q, k2, v2, gq, gk, first, do, o, lse, dlse):
    scale = HEAD ** -0.5
    rq, qh = _norm_parts(q)
    rk, kh = _norm_parts(k2)
    qn, kn = qh * gq, kh * gk
    s = jnp.where(_band_mask(first), _mxu(qn, kn, "nt") * scale, NEG)
    p = jnp.exp(s - lse)
    ds = p * (_mxu(do, v2, "nt") + (dlse - jnp.sum(do * o, axis=-1, keepdims=True))) * scale
    dq, dgq = _norm_bwd(_mxu(ds, kn, "nn"), gq, rq, qh)
    dk2, dgk = _norm_bwd(_mxu(ds, qn, "tn"), gk, rk, kh)
    return dq, dk2, _mxu(p, do, "tn"), dgq, dgk


def _combine(o1, o2, o3, l1, l2, l3, z):
    m = lax.stop_gradient(jnp.maximum(jnp.maximum(l1, l2), l3))
    e1, e2, e3 = jnp.exp(l1 - m), jnp.exp(l2 - m), jnp.exp(l3 - m)
    return (e1 * o1 + e2 * o2 + e3 * o3) / (e1 + e2 + e3) * _silu(z)


def _mem_block(q, z, kv, gq, gk):
    outs = []
    for h in range(MEM_HEADS):
        sl = slice(h * MEM_HD, (h + 1) * MEM_HD)
        qn = _rms_rows(q[:, sl], gq)
        kn = _rms_rows(kv[:, sl], gk)
        s = _dot(qn, kn, "nt") * (MEM_HD ** -0.5)
        m = lax.stop_gradient(jnp.max(s, axis=-1, keepdims=True))
        p = jnp.exp(s - m)
        den = jnp.sum(p, axis=-1, keepdims=True)
        outs.append(_dot(p, kv[:, MEMW + h * MEM_HD:MEMW + (h + 1) * MEM_HD], "nn") / den)
    return jnp.concatenate(outs, axis=-1) * _silu(z)


def _cast(w, name):
    R, C = w.shape
    tr, tc = _tile(R, 512, 8), _tile(C, 2176)

    def body(w_ref, o_ref):
        o_ref[...] = w_ref[...].astype(o_ref.dtype)

    spec = pl.BlockSpec((tr, tc), lambda i, j: (i, j))
    return _call(body, name=name, grid=(R // tr, C // tc), in_specs=[spec], out_specs=spec,
                 out_shape=jax.ShapeDtypeStruct((R, C), WIRE_DTYPE))(w)


def _place_shard(w, kind, pos, dtype, name, slot=0, into=None):
    R, C = w.shape
    tr, tc = _tile(R, 512, 8), _tile(C, 2176)
    nr, nc = R // tr, C // tc

    def body(pos_ref, w_ref, *rest):
        rest[-1][...] = w_ref[...].astype(rest[-1].dtype)

    if kind == "col":
        full, out = (R, 4 * C), pl.BlockSpec((tr, tc), lambda i, j, pos_ref: (i, pos_ref[slot] * nc + j))
    else:
        full, out = (4 * R, C), pl.BlockSpec((tr, tc), lambda i, j, pos_ref: (pos_ref[slot] * nr + i, j))
    in_specs, args = [pl.BlockSpec((tr, tc), lambda i, j, pos_ref: (i, j))], [pos, w]
    if into is not None:
        in_specs.append(ANY)
        args.append(into)
    spec = pltpu.PrefetchScalarGridSpec(num_scalar_prefetch=1, grid=(nr, nc), in_specs=in_specs, out_specs=out)
    return _call(body, name=name, grid_spec=spec, out_shape=jax.ShapeDtypeStruct(full, dtype),
                 aliases={} if into is None else {2: 0})(*args)


def _matmul(a, b, mode, out_dtype, *, name, tm=512, tn=512, tk=512):
    if mode == "nn":
        (M, K), (_, N) = a.shape, b.shape
    elif mode == "nt":
        (M, K), (N, _) = a.shape, b.shape
    else:
        (K, M), (_, N) = a.shape, b.shape
    tm, tn, tk = _tile(M, tm), _tile(N, tn), _tile(K, tk)
    nk = K // tk

    def body(a_ref, b_ref, o_ref, *acc):
        part = lax.dot_general(a_ref[...], b_ref[...], _DIMS[mode], preferred_element_type=F32)
        if nk == 1:
            o_ref[...] = part.astype(o_ref.dtype)
            return
        acc_ref, = acc
        k = pl.program_id(2)

        @pl.when(k == 0)
        def _():
            acc_ref[...] = part

        @pl.when(k > 0)
        def _():
            acc_ref[...] += part

        @pl.when(k == nk - 1)
        def _():
            o_ref[...] = acc_ref[...].astype(o_ref.dtype)

    a_spec = pl.BlockSpec((tk, tm), lambda i, j, k: (k, i)) if mode == "tn" else pl.BlockSpec((tm, tk), lambda i, j, k: (i, k))
    b_spec = pl.BlockSpec((tn, tk), lambda i, j, k: (j, k)) if mode == "nt" else pl.BlockSpec((tk, tn), lambda i, j, k: (k, j))
    return _call(body, name=name, grid=(M // tm, N // tn, nk), in_specs=[a_spec, b_spec],
                 out_specs=pl.BlockSpec((tm, tn), lambda i, j, k: (i, j)),
                 out_shape=jax.ShapeDtypeStruct((M, N), out_dtype),
                 scratch_shapes=[] if nk == 1 else [pltpu.VMEM((tm, tn), F32)])(a, b)


def _rms_fwd(x, g, name):
    R, D = x.shape
    tr = _tile(R, 512)

    def body(x_ref, g_ref, o_ref, t_ref):
        y = _rms_rows(x_ref[...], g_ref[...])
        o_ref[...] = y.astype(o_ref.dtype)
        t_ref[...] = y.T.astype(t_ref.dtype)

    row = pl.BlockSpec((tr, D), lambda i: (i, 0))
    return _call(body, name=name, grid=(R // tr,), in_specs=[row, pl.BlockSpec((1, D), lambda i: (0, 0))],
                 out_specs=[row, pl.BlockSpec((D, tr), lambda i: (0, i))],
                 out_shape=[jax.ShapeDtypeStruct((R, D), MXU_DTYPE), jax.ShapeDtypeStruct((D, R), MXU_DTYPE)])(x, g)


def _rms_bwd(x, dh, g, dy, name):
    R, D = x.shape
    tr = _tile(R, 256)
    with_dx = dy is not None

    def body(*refs):
        if with_dx:
            x_ref, dh_ref, g_ref, dy_ref, dx_ref, dg_ref = refs
        else:
            x_ref, dh_ref, g_ref, dg_ref = refs
        xv, dhv = x_ref[...], dh_ref[...]
        r = lax.rsqrt(jnp.mean(xv * xv, axis=-1, keepdims=True) + EPS)
        xh = xv * r

        @pl.when(pl.program_id(0) == 0)
        def _():
            dg_ref[...] = jnp.zeros_like(dg_ref)

        dg_ref[...] += jnp.sum(dhv * xh, axis=0, keepdims=True)
        if with_dx:
            dxh = dhv * g_ref[...]
            dx_ref[...] = dy_ref[...] + r * (dxh - xh * jnp.mean(dxh * xh, axis=-1, keepdims=True))

    row = pl.BlockSpec((tr, D), lambda i: (i, 0))
    vec = pl.BlockSpec((1, D), lambda i: (0, 0))
    dg_shape = jax.ShapeDtypeStruct((1, D), F32)
    if with_dx:
        return _call(body, name=name, grid=(R // tr,), in_specs=[row, row, vec, row], out_specs=[row, vec],
                     out_shape=[jax.ShapeDtypeStruct((R, D), F32), dg_shape])(x, dh, g, dy)
    return None, _call(body, name=name, grid=(R // tr,), in_specs=[row, row, vec], out_specs=vec,
                       out_shape=dg_shape)(x, dh, g)


def _attn_geom(g, d):
    hc = HPG if d == 1 else 1
    cw = hc * HEAD
    cq, ck, cv = (Q0 + g * GW) // cw, (K0 + g * GW) // cw, (V0 + g * GW) // cw
    return (1, BLK * d, cw), hc, HPG // hc, cq, ck, cv


def _rows(ref, r, d, sl):
    if d == 1:
        return ref[0, :, sl]
    return ref.at[0][pl.ds(r, BLK, stride=d), sl]


def _set_rows(ref, r, d, sl, val):
    if d == 1:
        ref[0, :, sl] = val
    else:
        ref.at[0][pl.ds(r, BLK, stride=d), sl] = val


def _stage_rows(ref, r, d, sl, val):
    if d == 1:
        ref[:, sl] = val
    else:
        ref[pl.ds(r, BLK, stride=d), sl] = val


def _attn_fwd(proj3, gq, gk, g, d):
    Bl, S, _ = proj3.shape
    blk, hc, ncb, cq, ck, cv = _attn_geom(g, d)
    nb = S // blk[1]

    def body(q_ref, kp_ref, kc_ref, vp_ref, vc_ref, gq_ref, gk_ref, o_ref, lse_ref):
        first = pl.program_id(2) == 0
        for r in range(d):
            for h in range(hc):
                sl = slice(h * HEAD, (h + 1) * HEAD)
                k2 = jnp.concatenate([_rows(kp_ref, r, d, sl), _rows(kc_ref, r, d, sl)], axis=0)
                v2 = jnp.concatenate([_rows(vp_ref, r, d, sl), _rows(vc_ref, r, d, sl)], axis=0)
                o, lse = _attn_block(_rows(q_ref, r, d, sl), k2, v2, gq_ref[...], gk_ref[...], first)
                _set_rows(o_ref, r, d, sl, o)
                _set_rows(lse_ref, r, d, sl, jnp.broadcast_to(lse, (BLK, HEAD)))

    def cur(c0):
        return pl.BlockSpec(blk, lambda b, j, i: (b, i, c0 + j))

    def prev(c0):
        return pl.BlockSpec(blk, lambda b, j, i: (b, jnp.maximum(i - 1, 0), c0 + j))

    vec = pl.BlockSpec((1, HEAD), lambda b, j, i: (0, 0))
    out = pl.BlockSpec(blk, lambda b, j, i: (b, i, j))
    shp = jax.ShapeDtypeStruct((Bl, S, GW), F32)
    return _call(body, name=f"attn_fwd_g{g}", grid=(Bl, ncb, nb),
                 in_specs=[cur(cq), prev(ck), cur(ck), prev(cv), cur(cv), vec, vec],
                 out_specs=[out, out], out_shape=[shp, shp])(proj3, proj3, proj3, proj3, proj3, gq, gk)


def _attn_bwd(proj3, gq, gk, o3, l3, do3, dl3, g, d):
    Bl, S, _ = proj3.shape
    blk, hc, ncb, cq, ck, cv = _attn_geom(g, d)
    nb = S // blk[1]

    def body(q_ref, kp_ref, kc_ref, vp_ref, vc_ref, gq_ref, gk_ref, o_ref, l_ref, do_ref, dl_ref,
             dq_ref, dk_ref, dv_ref, dgq_ref, dgk_ref, ck_ref, cv_ref, sq_ref, sk_ref, sv_ref):
        i = pl.program_id(2)
        first = i == 0

        @pl.when((pl.program_id(0) == 0) & (pl.program_id(1) == 0) & first)
        def _():
            dgq_ref[...] = jnp.zeros_like(dgq_ref)
            dgk_ref[...] = jnp.zeros_like(dgk_ref)

        @pl.when(first)
        def _():
            ck_ref[...] = jnp.zeros_like(ck_ref)
            cv_ref[...] = jnp.zeros_like(cv_ref)

        @pl.when(i < nb)
        def _():
            dgq, dgk = jnp.zeros((1, HEAD), F32), jnp.zeros((1, HEAD), F32)
            for r in range(d):
                rs = slice(r * BLK, (r + 1) * BLK)
                for h in range(hc):
                    sl = slice(h * HEAD, (h + 1) * HEAD)
                    k2 = jnp.concatenate([_rows(kp_ref, r, d, sl), _rows(kc_ref, r, d, sl)], axis=0)
                    v2 = jnp.concatenate([_rows(vp_ref, r, d, sl), _rows(vc_ref, r, d, sl)], axis=0)
                    dq, dk2, dv2, a, b = _attn_block_bwd(
                        _rows(q_ref, r, d, sl), k2, v2, gq_ref[...], gk_ref[...], first, _rows(do_ref, r, d, sl),
                        _rows(o_ref, r, d, sl), _rows(l_ref, r, d, sl)[:, :1], _rows(dl_ref, r, d, sl)[:, :1])
                    _stage_rows(sq_ref, r, d, sl, dq)
                    _stage_rows(sk_ref, r, d, sl, ck_ref[rs, sl] + dk2[:BLK])
                    _stage_rows(sv_ref, r, d, sl, cv_ref[rs, sl] + dv2[:BLK])
                    ck_ref[rs, sl] = dk2[BLK:]
                    cv_ref[rs, sl] = dv2[BLK:]
                    dgq, dgk = dgq + a, dgk + b
            dgq_ref[...] += dgq
            dgk_ref[...] += dgk
            dq_ref[0] = sq_ref[...].astype(dq_ref.dtype)

        @pl.when(i == nb)
        def _():
            for r in range(d):
                rs = slice(r * BLK, (r + 1) * BLK)
                _stage_rows(sk_ref, r, d, slice(None), ck_ref[rs, :])
                _stage_rows(sv_ref, r, d, slice(None), cv_ref[rs, :])

        dk_ref[0] = sk_ref[...].astype(dk_ref.dtype)
        dv_ref[0] = sv_ref[...].astype(dv_ref.dtype)

    def cur(c0):
        return pl.BlockSpec(blk, lambda b, j, i: (b, jnp.minimum(i, nb - 1), c0 + j))

    def prev(c0):
        return pl.BlockSpec(blk, lambda b, j, i: (b, jnp.clip(i - 1, 0, nb - 1), c0 + j))

    vec = pl.BlockSpec((1, HEAD), lambda b, j, i: (0, 0))
    at_q = pl.BlockSpec(blk, lambda b, j, i: (b, jnp.minimum(i, nb - 1), j))
    at_k = pl.BlockSpec(blk, lambda b, j, i: (b, jnp.maximum(i - 1, 0), j))
    shp = jax.ShapeDtypeStruct((Bl, S, GW), MXU_DTYPE)
    gshp = jax.ShapeDtypeStruct((1, HEAD), F32)
    return _call(body, name=f"attn_bwd_g{g}", grid=(Bl, ncb, nb + 1),
                 in_specs=[cur(cq), prev(ck), cur(ck), prev(cv), cur(cv), vec, vec, at_q, at_q, at_q, at_q],
                 out_specs=[at_q, at_k, at_k, vec, vec], out_shape=[shp, shp, shp, gshp, gshp],
                 scratch_shapes=[pltpu.VMEM(blk[1:], F32)] * 5,
                 )(proj3, proj3, proj3, proj3, proj3, gq, gk, o3, l3, do3, dl3)


def _combine_fwd(os, ls, proj2):
    T = proj2.shape[0]
    tr = _tile(T, 512)

    def body(o1, o2, o3, l1, l2, l3, z, a_ref):
        a_ref[...] = _combine(o1[...], o2[...], o3[...], l1[...], l2[...], l3[...], z[...]).astype(a_ref.dtype)

    row = pl.BlockSpec((tr, GW), lambda i: (i, 0))
    return _call(body, name="combine_fwd", grid=(T // tr,),
                 in_specs=[row] * 6 + [pl.BlockSpec((tr, GW), lambda i: (i, ZA // GW))], out_specs=row,
                 out_shape=jax.ShapeDtypeStruct((T, GW), MXU_DTYPE))(*os, *ls, proj2)


def _combine_bwd(os, ls, proj2, da):
    T = proj2.shape[0]
    tr = _tile(T, 256)

    def body(o1, o2, o3, l1, l2, l3, z, da_ref, d1, d2, d3, e1, e2, e3, dz_ref):
        _, vjp = jax.vjp(_combine, o1[...], o2[...], o3[...], l1[...], l2[...], l3[...], z[...])
        go1, go2, go3, gl1, gl2, gl3, gz = vjp(da_ref[...])
        d1[...], d2[...], d3[...] = go1, go2, go3
        dz_ref[...] = gz.astype(dz_ref.dtype)
        for ref, gl in ((e1, gl1), (e2, gl2), (e3, gl3)):
            for h in range(HPG):
                sl = slice(h * HEAD, (h + 1) * HEAD)
                ref[:, sl] = jnp.broadcast_to(jnp.sum(gl[:, sl], axis=-1, keepdims=True), (tr, HEAD))

    row = pl.BlockSpec((tr, GW), lambda i: (i, 0))
    f = jax.ShapeDtypeStruct((T, GW), F32)
    outs = _call(body, name="combine_bwd", grid=(T // tr,),
                 in_specs=[row] * 6 + [pl.BlockSpec((tr, GW), lambda i: (i, ZA // GW)), row],
                 out_specs=[row] * 7, out_shape=[f] * 6 + [jax.ShapeDtypeStruct((T, GW), MXU_DTYPE)],
                 )(*os, *ls, proj2, da)
    return outs[:3], outs[3:6], outs[6]


def _shift_down(u, j, t):
    return jnp.where(t >= j, pltpu.roll(u, j, 0), 0.0)


def _shift_up(u, j, t):
    n = u.shape[0]
    return jnp.where(t < n - j, pltpu.roll(u, n - j, 0), 0.0)


def _conv_specs(Bl, S, cw):
    def sec(c0):
        return pl.BlockSpec((1, S, cw), lambda j, b: (b, 0, c0 // cw + j))
    return [sec(CB), sec(CC), sec(CV), sec(ZC)], pl.BlockSpec((3, cw), lambda j, b: (0, j))


def _conv_fwd(proj3, conv_w):
    Bl, S, _ = proj3.shape
    cw = 256
    secs, wspec = _conv_specs(Bl, S, cw)

    def body(b_ref, c_ref, v_ref, z_ref, w_ref, o_ref):
        t = lax.broadcasted_iota(jnp.int32, (S, cw), 0)
        u = c_ref[0] * v_ref[0]
        y = w_ref[0:1, :] * u + w_ref[1:2, :] * _shift_down(u, 1, t) + w_ref[2:3, :] * _shift_down(u, 2, t)
        o_ref[0] = (b_ref[0] * y * _silu(z_ref[0])).astype(o_ref.dtype)

    return _call(body, name="conv_fwd", grid=(CONVW // cw, Bl), in_specs=secs + [wspec],
                 out_specs=pl.BlockSpec((1, S, cw), lambda j, b: (b, 0, j)),
                 out_shape=jax.ShapeDtypeStruct((Bl, S, CONVW), MXU_DTYPE))(proj3, proj3, proj3, proj3, conv_w)


def _conv_bwd(proj3, conv_w, dcc3):
    Bl, S, _ = proj3.shape
    cw = 256
    secs, wspec = _conv_specs(Bl, S, cw)

    def body(b_ref, c_ref, v_ref, z_ref, w_ref, d_ref, db_ref, dc_ref, dv_ref, dz_ref, dw_ref):
        t = lax.broadcasted_iota(jnp.int32, (S, cw), 0)
        bv, cv, vv, zv, dv = b_ref[0], c_ref[0], v_ref[0], z_ref[0], d_ref[0]
        u = cv * vv
        u1, u2 = _shift_down(u, 1, t), _shift_down(u, 2, t)
        y = w_ref[0:1, :] * u + w_ref[1:2, :] * u1 + w_ref[2:3, :] * u2
        sg = _sig(zv)
        sz = zv * sg
        gy = dv * bv * sz
        db_ref[0] = (dv * y * sz).astype(db_ref.dtype)
        dz_ref[0] = (dv * bv * y * sg * (1.0 + zv * (1.0 - sg))).astype(dz_ref.dtype)
        du = w_ref[0:1, :] * gy + w_ref[1:2, :] * _shift_up(gy, 1, t) + w_ref[2:3, :] * _shift_up(gy, 2, t)
        dc_ref[0] = (du * vv).astype(dc_ref.dtype)
        dv_ref[0] = (du * cv).astype(dv_ref.dtype)

        @pl.when(pl.program_id(1) == 0)
        def _():
            dw_ref[...] = jnp.zeros_like(dw_ref)

        dw_ref[0:1, :] += jnp.sum(gy * u, axis=0, keepdims=True)
        dw_ref[1:2, :] += jnp.sum(gy * u1, axis=0, keepdims=True)
        dw_ref[2:3, :] += jnp.sum(gy * u2, axis=0, keepdims=True)

    blk = pl.BlockSpec((1, S, cw), lambda j, b: (b, 0, j))
    shp = jax.ShapeDtypeStruct((Bl, S, CONVW), MXU_DTYPE)
    return _call(body, name="conv_bwd", grid=(CONVW // cw, Bl), in_specs=secs + [wspec, blk],
                 out_specs=[blk] * 4 + [wspec], out_shape=[shp] * 4 + [jax.ShapeDtypeStruct((3, CONVW), F32)],
                 )(proj3, proj3, proj3, proj3, conv_w, dcc3)


def _mem_specs(S, tq):
    q = pl.BlockSpec((1, tq, MEMW), lambda b, j: (b, j, MQ // MEMW))
    z = pl.BlockSpec((1, tq, MEMW), lambda b, j: (b, j, ZM // MEMW))
    kv = pl.BlockSpec((1, MEM_HD, 2 * MEMW), lambda b, j: (b, 0, 0))
    vec = pl.BlockSpec((1, MEM_HD), lambda b, j: (0, 0))
    blk = pl.BlockSpec((1, tq, MEMW), lambda b, j: (b, j, 0))
    return q, z, kv, vec, blk


def _mem_fwd(proj3, mkv3, gq, gk):
    Bl, S, _ = proj3.shape
    tq = _tile(S, 512)
    q, z, kv, vec, blk = _mem_specs(S, tq)

    def body(q_ref, z_ref, kv_ref, gq_ref, gk_ref, o_ref):
        o_ref[0] = _mem_block(q_ref[0], z_ref[0], kv_ref[0], gq_ref[...], gk_ref[...]).astype(o_ref.dtype)

    return _call(body, name="mem_fwd", grid=(Bl, S // tq), in_specs=[q, z, kv, vec, vec], out_specs=blk,
                 out_shape=jax.ShapeDtypeStruct((Bl, S, MEMW), MXU_DTYPE))(proj3, proj3, mkv3, gq, gk)


def _mem_bwd(proj3, mkv3, gq, gk, dmo3):
    Bl, S, _ = proj3.shape
    tq = _tile(S, 256)
    q, z, kv, vec, blk = _mem_specs(S, tq)

    def body(q_ref, z_ref, kv_ref, gq_ref, gk_ref, d_ref, dq_ref, dz_ref, dkv_ref, dgq_ref, dgk_ref):
        _, vjp = jax.vjp(_mem_block, q_ref[0], z_ref[0], kv_ref[0], gq_ref[...], gk_ref[...])
        dq, dz, dkv, dgq, dgk = vjp(d_ref[0])
        dq_ref[0] = dq.astype(dq_ref.dtype)
        dz_ref[0] = dz.astype(dz_ref.dtype)
        j = pl.program_id(1)

        @pl.when(j == 0)
        def _():
            dkv_ref[0] = jnp.zeros_like(dkv)

        @pl.when((j == 0) & (pl.program_id(0) == 0))
        def _():
            dgq_ref[...] = jnp.zeros_like(dgq_ref)
            dgk_ref[...] = jnp.zeros_like(dgk_ref)

        dkv_ref[0] += dkv
        dgq_ref[...] += dgq
        dgk_ref[...] += dgk

    shp = jax.ShapeDtypeStruct((Bl, S, MEMW), MXU_DTYPE)
    gshp = jax.ShapeDtypeStruct((1, MEM_HD), F32)
    return _call(body, name="mem_bwd", grid=(Bl, S // tq), in_specs=[q, z, kv, vec, vec, blk],
                 out_specs=[blk, blk, kv, vec, vec],
                 out_shape=[shp, shp, jax.ShapeDtypeStruct(mkv3.shape, F32), gshp, gshp],
                 )(proj3, proj3, mkv3, gq, gk, dmo3)


def _merge_specs(T, D, tm, tn):
    def act(w):
        return pl.BlockSpec((tm, w), lambda i, n: (i, 0))

    def wsp(w):
        return pl.BlockSpec((w, tn), lambda i, n: (0, n))

    gates = [pl.BlockSpec((tm, tn), lambda i, n, k=k: (i, (G0 + k * D) // tn + n)) for k in range(3)]
    tile = pl.BlockSpec((tm, tn), lambda i, n: (i, n))
    return act, wsp, gates, tile


def _merge_fwd(a, cc, mo, wa, wc, wm, proj2):
    T, D = a.shape[0], wa.shape[1]
    tm, tn = _tile(T, 1024), _tile(D, 512)
    act, wsp, gates, tile = _merge_specs(T, D, tm, tn)

    def body(a_ref, c_ref, m_ref, wa_ref, wc_ref, wm_ref, g0, g1, g2, mg_ref, mt_ref, pa_ref, pc_ref, pm_ref):
        pa = jnp.dot(a_ref[...], wa_ref[...], preferred_element_type=F32)
        pc = jnp.dot(c_ref[...], wc_ref[...], preferred_element_type=F32)
        pm = jnp.dot(m_ref[...], wm_ref[...], preferred_element_type=F32)
        mg = _sig(g0[...]) * pa + _sig(g1[...]) * pc + _sig(g2[...]) * pm
        mg_ref[...] = mg.astype(mg_ref.dtype)
        mt_ref[...] = mg.T.astype(mt_ref.dtype)
        pa_ref[...] = pa.astype(pa_ref.dtype)
        pc_ref[...] = pc.astype(pc_ref.dtype)
        pm_ref[...] = pm.astype(pm_ref.dtype)

    shp = jax.ShapeDtypeStruct((T, D), MXU_DTYPE)
    return _call(body, name="merge_fwd", grid=(T // tm, D // tn),
                 in_specs=[act(GW), act(CONVW), act(MEMW), wsp(GW), wsp(CONVW), wsp(MEMW)] + gates,
                 out_specs=[tile, pl.BlockSpec((tn, tm), lambda i, n: (n, i)), tile, tile, tile],
                 out_shape=[shp, jax.ShapeDtypeStruct((D, T), MXU_DTYPE), shp, shp, shp],
                 )(a, cc, mo, wa, wc, wm, proj2, proj2, proj2)


def _merge_bwd(dyb, w_out, proj2, pa, pc, pm):
    T, D = dyb.shape
    tm, tn = _tile(T, 1024), _tile(D, 512)
    _, _, gates, tile = _merge_specs(T, D, tm, tn)

    def body(dy_ref, w_ref, g0, g1, g2, p0, p1, p2, dp0, dp1, dp2, dg0, dg1, dg2):
        dm = lax.dot_general(dy_ref[...], w_ref[...], _DIMS["nt"], preferred_element_type=F32)
        for g_ref, p_ref, dp_ref, dg_ref in ((g0, p0, dp0, dg0), (g1, p1, dp1, dg1), (g2, p2, dp2, dg2)):
            gt = _sig(g_ref[...])
            dp_ref[...] = (gt * dm).astype(dp_ref.dtype)
            dg_ref[...] = (dm * p_ref[...].astype(F32) * gt * (1.0 - gt)).astype(dg_ref.dtype)

    shp = jax.ShapeDtypeStruct((T, D), MXU_DTYPE)
    return _call(body, name="merge_bwd", grid=(T // tm, D // tn),
                 in_specs=[pl.BlockSpec((tm, D), lambda i, n: (i, 0)), pl.BlockSpec((tn, D), lambda i, n: (n, 0))]
                 + gates + [tile] * 3,
                 out_specs=[tile] * 6, out_shape=[shp] * 6)(dyb, w_out, proj2, proj2, proj2, pa, pc, pm)


def _out_loss(merged, w_out, x, tgt):
    T, D = x.shape
    tm = _tile(T, 512)

    def body(m_ref, w_ref, x_ref, t_ref, dy_ref, dyb_ref, loss_ref):
        err = x_ref[...] + jnp.dot(m_ref[...], w_ref[...], preferred_element_type=F32) - t_ref[...]
        dy = err * (1.0 / D)
        dy_ref[...] = dy
        dyb_ref[...] = dy.astype(dyb_ref.dtype)

        @pl.when(pl.program_id(0) == 0)
        def _():
            loss_ref[...] = jnp.zeros_like(loss_ref)

        loss_ref[...] += jnp.sum(err * err) * (0.5 / D)

    row = pl.BlockSpec((tm, D), lambda i: (i, 0))
    return _call(body, name="out_loss", grid=(T // tm,),
                 in_specs=[row, pl.BlockSpec((D, D), lambda i: (0, 0)), row, row],
                 out_specs=[row, row, pl.BlockSpec((1, 128), lambda i: (0, 0))],
                 out_shape=[jax.ShapeDtypeStruct((T, D), F32), jax.ShapeDtypeStruct((T, D), MXU_DTYPE),
                            jax.ShapeDtypeStruct((1, 128), F32)])(merged, w_out, x, tgt)


def _proj_chunk(hb, w, order, j, buf, name):
    T, D = hb.shape
    Cs = w.shape[1] // 4
    tm, tn = _tile(T, 1024), _tile(Cs, 2176)
    nj = Cs // tn

    def body(order_ref, a_ref, b_ref, *rest):
        rest[-1][...] = jnp.dot(a_ref[...], b_ref[...], preferred_element_type=F32)

    in_specs = [pl.BlockSpec((tm, D), lambda n, i, o: (i, 0)), pl.BlockSpec((D, tn), lambda n, i, o: (0, j * nj + n))]
    args = [order, hb, w]
    if buf is not None:
        in_specs.append(ANY)
        args.append(buf)
    spec = pltpu.PrefetchScalarGridSpec(
        num_scalar_prefetch=1, grid=(nj, T // tm), in_specs=in_specs,
        out_specs=pl.BlockSpec((tm, tn), lambda n, i, o: (i, o[j] * nj + n)))
    return _call(body, name=name, grid_spec=spec, out_shape=jax.ShapeDtypeStruct((T, 4 * Cs), F32),
                 aliases={} if buf is None else {3: 0})(*args)


def _norms(x, mem, norm_g, mem_norm_g):
    D = x.shape[-1]
    hb, hbt = _rms_fwd(x.reshape(-1, D), norm_g.reshape(1, D), "rms_x")
    mhb, _ = _rms_fwd(mem.reshape(-1, D), mem_norm_g.reshape(1, D), "rms_mem")
    return hb, hbt, mhb


def _mix_fwd(proj2, Bl, gq_all, gk_all, conv_w):
    T, IN = proj2.shape
    proj3 = proj2.reshape(Bl, T // Bl, IN)
    os, ls = [], []
    for g, d in enumerate(DILATIONS):
        o, l = _attn_fwd(proj3, gq_all[g:g + 1], gk_all[g:g + 1], g, d)
        os.append(o.reshape(T, GW))
        ls.append(l.reshape(T, GW))
    a = _combine_fwd(os, ls, proj2)
    cc = _conv_fwd(proj3, conv_w).reshape(T, CONVW)
    return os, ls, a, cc


def _weight_grads(x, mem, tgt, norm_g, mem_norm_g, gq_all, gk_all, conv_w, mem_gq, mem_gk, W, pre, early=None):
    Bl, S, D = x.shape
    T = Bl * S
    hb, hbt, mhb, proj2, os, ls, a, cc = pre
    IN = proj2.shape[1]
    proj3 = proj2.reshape(Bl, S, IN)
    x2, tgt2 = x.reshape(T, D), tgt.reshape(T, D)
    mem2 = mem.reshape(-1, D)
    ng, mng = norm_g.reshape(1, D), mem_norm_g.reshape(1, D)
    mgq, mgk = mem_gq.reshape(1, MEM_HD), mem_gk.reshape(1, MEM_HD)
    gqs = [gq_all[g:g + 1] for g in range(NGROUP)]
    gks = [gk_all[g:g + 1] for g in range(NGROUP)]

    mkv = _matmul(mhb, W["mem_w_kv"], "nn", F32, name="mem_kv", tm=512, tn=1024, tk=D)
    mkv3 = mkv.reshape(Bl, -1, 2 * MEMW)
    mo = _mem_fwd(proj3, mkv3, mgq, mgk).reshape(T, MEMW)
    merged, mergedt, pa, pc, pm = _merge_fwd(a, cc, mo, W["w_br_attn"], W["w_br_conv"], W["w_br_mem"], proj2)
    dy, dyb, loss = _out_loss(merged, W["w_out"], x2, tgt2)

    G = {}
    G["w_out"] = _matmul(mergedt, dyb, "nn", WIRE_DTYPE, name="dw_out", tm=1024, tn=512, tk=T)
    dpa, dpc, dpm, dg0, dg1, dg2 = _merge_bwd(dyb, W["w_out"], proj2, pa, pc, pm)
    G["w_br_attn"] = _matmul(a, dpa, "tn", WIRE_DTYPE, name="dw_br_attn", tm=512, tn=1024, tk=512)
    G["w_br_conv"] = _matmul(cc, dpc, "tn", WIRE_DTYPE, name="dw_br_conv", tm=1024, tn=1024, tk=512)
    G["w_br_mem"] = _matmul(mo, dpm, "tn", WIRE_DTYPE, name="dw_br_mem", tm=1024, tn=1024, tk=512)
    da = _matmul(dpa, W["w_br_attn"], "nt", F32, name="d_attn", tm=1024, tn=512, tk=D)
    dcc = _matmul(dpc, W["w_br_conv"], "nt", F32, name="d_conv", tm=1024, tn=1024, tk=D)
    dmo = _matmul(dpm, W["w_br_mem"], "nt", F32, name="d_mem", tm=1024, tn=1024, tk=D)
    dmq, dzm, dmkv3, dmgq, dmgk = _mem_bwd(proj3, mkv3, mgq, mgk, dmo.reshape(Bl, S, MEMW))
    dmkv = _cast(dmkv3.reshape(-1, 2 * MEMW), "cast_dmkv")
    G["mem_w_kv"] = _matmul(mhb, dmkv, "tn", WIRE_DTYPE, name="dw_mem_kv", tm=1024, tn=1024, tk=512)
    early_state, da = (None, da) if early is None else early(G, da)
    dmh = _matmul(dmkv, W["mem_w_kv"], "nt", F32, name="d_memh", tm=512, tn=1024, tk=2 * MEMW)
    _, dmng = _rms_bwd(mem2, dmh, mng, None, "rms_mem_bwd")

    dos, dls, dza = _combine_bwd(os, ls, proj2, da)
    dqs, dks, dvs, dgq, dgk = [], [], [], [], []
    for g, d in enumerate(DILATIONS):
        dq, dk, dv, gq_g, gk_g = _attn_bwd(proj3, gqs[g], gks[g], os[g].reshape(Bl, S, GW), ls[g].reshape(Bl, S, GW),
                                           dos[g].reshape(Bl, S, GW), dls[g].reshape(Bl, S, GW), g, d)
        dqs.append(dq.reshape(T, GW).astype(MXU_DTYPE))
        dks.append(dk.reshape(T, GW).astype(MXU_DTYPE))
        dvs.append(dv.reshape(T, GW).astype(MXU_DTYPE))
        dgq.append(gq_g)
        dgk.append(gk_g)
    dcb, dcc_, dcv, dzc, dconv_w = _conv_bwd(proj3, conv_w, dcc.reshape(Bl, S, CONVW))

    dproj = jnp.concatenate(dqs + dks + dvs + [dza] + [t.reshape(T, CONVW) for t in (dcb, dcc_, dcv, dzc)]
                            + [dmq.reshape(T, MEMW), dzm.reshape(T, MEMW), dg0, dg1, dg2], axis=1)
    small = [loss, None, dmng] + dgq + dgk + [dconv_w.reshape(1, 3 * CONVW), dmgq, dmgk]
    return G, (dproj, x2, ng, dy, small), early_state


def _dw_in_half(hbt, dproj, pos, own, name):
    D, T = hbt.shape
    IN = dproj.shape[1]
    R, tn = D // 2, _tile(IN, 512)

    def body(pos_ref, a_ref, b_ref, o_ref):
        o_ref[...] = jnp.dot(a_ref[...], b_ref[...], preferred_element_type=F32).astype(o_ref.dtype)

    spec = pltpu.PrefetchScalarGridSpec(
        num_scalar_prefetch=1, grid=(IN // tn,),
        in_specs=[pl.BlockSpec((R, T), lambda j, p: (p[1] if own else 1 - p[1], 0)),
                  pl.BlockSpec((T, tn), lambda j, p: (0, j))],
        out_specs=pl.BlockSpec((R, tn), lambda j, p: (0, j)))
    return _call(body, name=name, grid_spec=spec, out_shape=jax.ShapeDtypeStruct((R, IN), WIRE_DTYPE))(pos, hbt, dproj)


def _d_h(dproj, w, order):
    T, IN = dproj.shape
    D, Cs = w.shape[0], IN // 4
    tm, tn = _tile(T, 1024), _tile(D, 1024)

    def body(order_ref, a_ref, b_ref, o_ref, acc_ref):
        part = lax.dot_general(a_ref[...], b_ref[...], _DIMS["nt"], preferred_element_type=F32)
        k = pl.program_id(2)

        @pl.when(k == 0)
        def _():
            acc_ref[...] = part

        @pl.when(k > 0)
        def _():
            acc_ref[...] += part

        @pl.when(k == 3)
        def _():
            o_ref[...] = acc_ref[...]

    spec = pltpu.PrefetchScalarGridSpec(
        num_scalar_prefetch=1, grid=(T // tm, D // tn, 4),
        in_specs=[pl.BlockSpec((tm, Cs), lambda i, n, k, o: (i, o[k])), pl.BlockSpec((tn, Cs), lambda i, n, k, o: (n, k))],
        out_specs=pl.BlockSpec((tm, tn), lambda i, n, k, o: (i, n)), scratch_shapes=[pltpu.VMEM((tm, tn), F32)])
    return _call(body, name="d_h", grid_spec=spec, out_shape=jax.ShapeDtypeStruct((T, D), F32))(order, dproj, w)


def _input_grad(rest, w_in, order):
    dproj, x2, ng, dy, small = rest
    dh = _d_h(dproj, w_in, order)
    grad_x, dng = _rms_bwd(x2, dh, ng, dy, "rms_x_bwd")
    small = [dng if t is None else t for t in small]
    return grad_x, jnp.concatenate(small, axis=1)


def _local_step(x, mem, tgt, norm_g, mem_norm_g, gq_all, gk_all, conv_w, mem_gq, mem_gk, W):
    hb, hbt, mhb = _norms(x, mem, norm_g, mem_norm_g)
    Cs = W["w_in"].shape[1] // 4
    shards = (0, 2, 1, 3)
    order = jnp.array(shards, dtype=jnp.int32)
    w_rel = jnp.concatenate([W["w_in"][:, s * Cs:(s + 1) * Cs] for s in shards], axis=1)
    proj2 = None
    for j in range(4):
        proj2 = _proj_chunk(hb, w_rel, order, j, proj2, f"proj_{j}")
    pre = (hb, hbt, mhb, proj2, *_mix_fwd(proj2, x.shape[0], gq_all, gk_all, conv_w))
    G, rest, _ = _weight_grads(x, mem, tgt, norm_g, mem_norm_g, gq_all, gk_all, conv_w, mem_gq, mem_gk, W, pre)
    pos = jnp.zeros((2,), jnp.int32)
    G["w_in"] = jnp.concatenate([_dw_in_half(hbt, rest[0], pos, True, "dw_in_own"),
                                 _dw_in_half(hbt, rest[0], pos, False, "dw_in_sibling")], axis=0)
    grad_x, small = _input_grad(rest, w_rel, order)
    return grad_x.reshape(x.shape), G, small


BIG = (("w_in", "col"), ("mem_w_kv", "row"), ("w_br_attn", "col"), ("w_br_conv", "col"),
       ("w_br_mem", "col"), ("w_out", "row"))


def _coords():
    return lax.axis_index("x"), lax.axis_index("y"), lax.axis_index("c")


def _other_chips(x, y):
    return [(1 - x, y), (x, 1 - y), (1 - x, 1 - y)]


def _half(ref, kind, c):
    R, C = ref.shape
    if kind == "col":
        return ref.at[pl.ds(c * (R // 2), R // 2), :]
    return ref.at[:, pl.ds(c * (C // 2), C // 2)]


def _shard(ref, kind, s):
    R, C = ref.shape
    if kind == "col":
        return ref.at[:, pl.ds(s * (C // 4), C // 4)]
    return ref.at[pl.ds(s * (R // 4), R // 4), :]


def _piece(ref, kind, s, c):
    R, C = ref.shape
    if kind == "col":
        return ref.at[pl.ds(c * (R // 2), R // 2), pl.ds(s * (C // 4), C // 4)]
    return ref.at[pl.ds(s * (R // 4), R // 4), pl.ds(c * (C // 2), C // 2)]


def _remote(src, dst, sems_s, sems_r, k, dev):
    return pltpu.make_async_remote_copy(src_ref=src, dst_ref=dst, send_sem=sems_s.at[k], recv_sem=sems_r.at[k],
                                        device_id=dev, device_id_type=MESH)


HBM = pl.BlockSpec(memory_space=pltpu.HBM)
SEM = pl.BlockSpec(memory_space=pltpu.SEMAPHORE)
EFFECT = pltpu.SideEffectType.DATAFLOW_SIDE_EFFECTING


def _hbm(a):
    return pltpu.with_memory_space_constraint(a, pltpu.HBM)


def _start_copies(name, arrays, ncopies, make):
    n = len(arrays)

    def body(*refs):
        for cp in make(refs[:n], refs[n], refs[n + 1]):
            cp.start()

    outs = pl.pallas_call(
        body, name=name,
        out_shape=(pltpu.SemaphoreType.DMA((ncopies,)), pltpu.SemaphoreType.DMA((ncopies,)),
                   *[jax.ShapeDtypeStruct(t.shape, t.dtype) for t in arrays]),
        in_specs=[HBM] * n, out_specs=(SEM, SEM, *([HBM] * n)),
        input_output_aliases={i: i + 2 for i in range(n)},
        compiler_params=pltpu.CompilerParams(has_side_effects=EFFECT),
    )(*[_hbm(t) for t in arrays])
    return outs[0], outs[1], list(outs[2:])


def _wait_copies(name, send, recv, arrays, make, after):
    n = len(arrays)

    def body(*refs):
        for cp in make(refs[:n], refs[n], refs[n + 1]):
            cp.wait_send()
            cp.wait_recv()

    outs = pl.pallas_call(
        body, name=name, out_shape=[jax.ShapeDtypeStruct(t.shape, t.dtype) for t in arrays],
        in_specs=[HBM] * n + [SEM, SEM, ANY], out_specs=[HBM] * n,
        input_output_aliases={i: i for i in range(n)},
        compiler_params=pltpu.CompilerParams(has_side_effects=EFFECT),
    )(*arrays, send, recv, after)
    return list(outs)


def _w_in_copies(relations):
    def make(refs, send, recv):
        x, y, c = _coords()
        me = 2 * x + y
        chips = _other_chips(x, y)
        w, conv = refs[0], refs[1]
        cps = []
        for i, k in enumerate(relations):
            cps.append(_remote(_piece(w, "col", 0, c), _piece(w, "col", 1 + k, c), send, recv, 2 * i, (*chips[k], c)))
            mine = _shard(conv, "col", me)
            cps.append(_remote(mine, mine, send, recv, 2 * i + 1, (*chips[k], c)))
        return cps
    return make


def _w_in_landed(relations):
    return lambda refs, c: [_piece(refs[0], "col", 1 + k, c) for k in relations]


def _sibling_copy(refs, send, recv):
    x, y, c = _coords()
    return [_remote(refs[0], refs[1], send, recv, 0, (x, y, 1 - c))]


def _other_weight_copies(refs, send, recv):
    x, y, c = _coords()
    me = 2 * x + y
    cps = []
    for k, chip in enumerate(_other_chips(x, y)):
        for p, (_, kind) in enumerate(BIG[1:]):
            mine = _piece(refs[p], kind, me, c)
            cps.append(_remote(mine, mine, send, recv, 3 * p + k, (*chip, c)))
    return cps


def _sibling_forward(name, arrays, ncp, halves):
    n = len(arrays)

    def body(*refs):
        outs = refs[n:2 * n]
        send, recv = refs[2 * n:]
        x, y, c = _coords()
        sib = (x, y, 1 - c)
        cps = [_remote(got, got, send, recv, i, sib) for i, got in enumerate(halves(outs, c))]
        for cp in cps:
            cp.start()
        for cp in cps:
            cp.wait_send()
        for i, got in enumerate(halves(outs, 1 - c)):
            _remote(got, got, send, recv, i, sib).wait_recv()

    return pl.pallas_call(
        body, name=name, out_shape=[jax.ShapeDtypeStruct(t.shape, t.dtype) for t in arrays],
        in_specs=[ANY] * n, out_specs=[ANY] * n, input_output_aliases={i: i for i in range(n)},
        scratch_shapes=[pltpu.SemaphoreType.DMA((ncp,)), pltpu.SemaphoreType.DMA((ncp,))],
    )(*arrays)


def _landed_halves(refs, c):
    return [_half(r, "col", c) for r in refs]


def _other_weight_halves(refs, c):
    x, y, _ = _coords()
    out = []
    for chip in _other_chips(x, y):
        s = 2 * chip[0] + chip[1]
        out += [_piece(refs[p], kind, s, c) for p, (_, kind) in enumerate(BIG[1:])]
    return out


def _sibling_exchange(grads, group, name):
    n = len(group)
    shapes = []
    for (_, kind), g in zip(group, grads):
        R, C = g.shape
        shapes.append(jax.ShapeDtypeStruct((R // 2, C) if kind == "col" else (R, C // 2), g.dtype))

    def body(*refs):
        ins, outs = refs[:n], refs[n:2 * n]
        send, recv = refs[2 * n:]
        x, y, c = _coords()
        sib = (x, y, 1 - c)
        cps = [_remote(_half(ins[p], group[p][1], 1 - c), outs[p], send, recv, p, sib) for p in range(n)]
        for cp in cps:
            cp.start()
        for cp in cps:
            cp.wait()

    return pl.pallas_call(
        body, name=name, out_shape=shapes, in_specs=[ANY] * n, out_specs=[ANY] * n,
        scratch_shapes=[pltpu.SemaphoreType.DMA((n,)), pltpu.SemaphoreType.DMA((n,))],
    )(*grads)


def _presum(g, got, kind, pos, name):
    R, C = got.shape
    tr, tc = _tile(R, 512, 16), _tile(C, 2048)
    nr, nc = R // tr, C // tc

    def body(pos_ref, a_ref, b_ref, o_ref):
        o_ref[...] = (a_ref[...].astype(F32) + b_ref[...].astype(F32)).astype(o_ref.dtype)

    blk = pl.BlockSpec((tr, tc), lambda i, j, pos_ref: (i, j))
    if g.shape == got.shape:
        mine = blk
    elif kind == "col":
        mine = pl.BlockSpec((tr, tc), lambda i, j, pos_ref: (pos_ref[1] * nr + i, j))
    else:
        mine = pl.BlockSpec((tr, tc), lambda i, j, pos_ref: (i, pos_ref[1] * nc + j))
    spec = pltpu.PrefetchScalarGridSpec(num_scalar_prefetch=1, grid=(nr, nc), in_specs=[mine, blk], out_specs=blk)
    return _call(body, name=name, grid_spec=spec, out_shape=jax.ShapeDtypeStruct((R, C), WIRE_DTYPE))(pos, g, got)


def _chip_copies(group):
    n = len(group)

    def make(refs, send, recv):
        x, y, c = _coords()
        cps = []
        for k, chip in enumerate(_other_chips(x, y)):
            s = 2 * chip[0] + chip[1]
            for p in range(n):
                cps.append(_remote(_shard(refs[p], group[p][1], s), refs[n + p].at[k], send, recv, 3 * p + k, (*chip, c)))
        return cps
    return make


def _landing_zones(pres, group):
    lands = []
    for (_, kind), g in zip(group, pres):
        R, C = g.shape
        lands.append(lax.empty((3, R, C // 4) if kind == "col" else (3, R // 4, C), g.dtype))
    return lands


def _exchange_start(G, group, pos, carry, tag, pres=None):
    n = len(group)
    if pres is None:
        parts = [G[name] for name, _ in group]
        got = _sibling_exchange(parts, group, "sibling_exchange_" + tag)
        pres = [_presum(parts[p], got[p], kind, pos, "presum_" + name) for p, (name, kind) in enumerate(group)]
    make = _chip_copies(group)
    send, recv, thru = _start_copies("chip_exchange_start_" + tag, [*pres, *_landing_zones(pres, group), carry], 3 * n, make)
    return (send, recv, thru[:2 * n], make, tag), thru[2 * n]


def _exchange_wait(state, after):
    send, recv, arrays, make, tag = state
    thru = _wait_copies("chip_exchange_wait_" + tag, send, recv, arrays, make, after)
    n = len(thru) // 2
    return thru[:n], thru[n:]


def _reduce_into_shard(slots, pre, kind, pos, name):
    K, R, C = slots.shape
    tr, tc = _tile(R, 512, 16), _tile(C, 2176)
    nr, nc = R // tr, C // tc

    def body(pos_ref, s_ref, p_ref, o_ref):
        acc = p_ref[...].astype(F32)
        for k in range(K):
            acc = acc + s_ref[k].astype(F32)
        o_ref[...] = acc

    if kind == "col":
        own = pl.BlockSpec((tr, tc), lambda i, j, pos_ref: (i, pos_ref[0] * nc + j))
        full, out = (2 * R, C), pl.BlockSpec((tr, tc), lambda i, j, pos_ref: (pos_ref[1] * nr + i, j))
    else:
        own = pl.BlockSpec((tr, tc), lambda i, j, pos_ref: (pos_ref[0] * nr + i, j))
        full, out = (R, 2 * C), pl.BlockSpec((tr, tc), lambda i, j, pos_ref: (i, pos_ref[1] * nc + j))
    spec = pltpu.PrefetchScalarGridSpec(
        num_scalar_prefetch=1, grid=(nr, nc),
        in_specs=[pl.BlockSpec((K, tr, tc), lambda i, j, pos_ref: (0, i, j)), own], out_specs=out)
    return _call(body, name=name, grid_spec=spec, out_shape=jax.ShapeDtypeStruct(full, F32))(pos, slots, pre)


def _share_reduced(reds):
    n = len(BIG)

    def body(*refs):
        outs = refs[n:2 * n]
        send, recv = refs[2 * n:]
        x, y, c = _coords()
        sib = (x, y, 1 - c)
        cps = []
        for p in range(n):
            mine = _half(outs[p], BIG[p][1], c)
            cps.append(_remote(mine, mine, send, recv, p, sib))
        for cp in cps:
            cp.start()
        for cp in cps:
            cp.wait_send()
        for p in range(n):
            got = _half(outs[p], BIG[p][1], 1 - c)
            _remote(got, got, send, recv, p, sib).wait_recv()

    return pl.pallas_call(
        body, name="share_reduced", out_shape=[jax.ShapeDtypeStruct(r.shape, r.dtype) for r in reds],
        in_specs=[ANY] * n, out_specs=[ANY] * n, input_output_aliases={p: p for p in range(n)},
        scratch_shapes=[pltpu.SemaphoreType.DMA((n,)), pltpu.SemaphoreType.DMA((n,))],
    )(*reds)


def _gather_small(pack):
    _, N = pack.shape

    def body(in_ref, out_ref, send, recv, loc):
        x, y, c = _coords()
        me = 4 * x + 2 * y + c
        own = pltpu.make_async_copy(in_ref, out_ref.at[me], loc)
        own.start()
        cps = []
        for k in range(1, 8):
            dev = (x ^ (k >> 2), y ^ ((k >> 1) & 1), c ^ (k & 1))
            cps.append(_remote(in_ref, out_ref.at[me], send, recv, k - 1, dev))
        for cp in cps:
            cp.start()
        for k in range(1, 8):
            src = 4 * (x ^ (k >> 2)) + 2 * (y ^ ((k >> 1) & 1)) + (c ^ (k & 1))
            _remote(in_ref, out_ref.at[src], send, recv, k - 1, (x, y, c)).wait_recv()
        for cp in cps:
            cp.wait_send()
        own.wait()

    return pl.pallas_call(
        body, name="gather_small", out_shape=jax.ShapeDtypeStruct((8, 1, N), pack.dtype),
        in_specs=[ANY], out_specs=ANY,
        scratch_shapes=[pltpu.SemaphoreType.DMA((7,)), pltpu.SemaphoreType.DMA((7,)), pltpu.SemaphoreType.DMA(())],
    )(pack)


def _sum_small(slots):
    K, _, N = slots.shape

    def body(s_ref, o_ref):
        acc = s_ref[0]
        for k in range(1, K):
            acc = acc + s_ref[k]
        o_ref[...] = acc

    return _call(body, name="sum_small", in_specs=[pl.BlockSpec(memory_space=pltpu.VMEM)],
                 out_specs=pl.BlockSpec(memory_space=pltpu.VMEM), out_shape=jax.ShapeDtypeStruct((1, N), F32))(slots)


def _adamw(w, g, m, v, name):
    R, C = w.shape
    tr, tc = _tile(R, 256, 8), _tile(C, 2176)

    def body(w_ref, g_ref, m_ref, v_ref, d_ref, nm_ref, nv_ref):
        gv = g_ref[...]
        nm = ADAM_B1 * m_ref[...] + (1.0 - ADAM_B1) * gv
        nv = ADAM_B2 * v_ref[...] + (1.0 - ADAM_B2) * gv * gv
        m_hat = nm / (1.0 - ADAM_B1 ** ADAM_STEP)
        v_hat = nv / (1.0 - ADAM_B2 ** ADAM_STEP)
        d_ref[...] = -ADAM_LR * (m_hat / (jnp.sqrt(v_hat) + ADAM_EPS) + ADAM_WD * w_ref[...])
        nm_ref[...] = nm
        nv_ref[...] = nv

    spec = pl.BlockSpec((tr, tc), lambda i, j: (i, j))
    shp = jax.ShapeDtypeStruct((R, C), F32)
    return _call(body, name=name, grid=(R // tr, C // tc), in_specs=[spec] * 4, out_specs=[spec] * 3,
                 out_shape=[shp] * 3)(w, g, m, v)


SMALL = ("norm_g", "mem_norm_g", "attn_q_norm", "attn_k_norm", "conv_w", "mem_q_norm", "mem_k_norm")
WEIGHTS = ("norm_g", "mem_norm_g", "w_in", "attn_q_norm", "attn_k_norm", "conv_w", "mem_w_kv", "mem_q_norm",
           "mem_k_norm", "w_br_attn", "w_br_conv", "w_br_mem", "w_out")


def kernel(x, mem, norm_g, mem_norm_g, w_in, attn_q_norm, attn_k_norm, conv_w, mem_w_kv, mem_q_norm, mem_k_norm, w_br_attn, w_br_conv, w_br_mem, w_out, loss_target, m_norm_g, m_mem_norm_g, m_w_in, m_attn_q_norm, m_attn_k_norm, m_conv_w, m_mem_w_kv, m_mem_q_norm, m_mem_k_norm, m_w_br_attn, m_w_br_conv, m_w_br_mem, m_w_out, v_norm_g, v_mem_norm_g, v_w_in, v_attn_q_norm, v_attn_k_norm, v_conv_w, v_mem_w_kv, v_mem_q_norm, v_mem_k_norm, v_w_br_attn, v_w_br_conv, v_w_br_mem, v_w_out):
    w = dict(norm_g=norm_g, mem_norm_g=mem_norm_g, w_in=w_in, attn_q_norm=attn_q_norm, attn_k_norm=attn_k_norm,
             conv_w=conv_w, mem_w_kv=mem_w_kv, mem_q_norm=mem_q_norm, mem_k_norm=mem_k_norm, w_br_attn=w_br_attn,
             w_br_conv=w_br_conv, w_br_mem=w_br_mem, w_out=w_out)
    m = dict(norm_g=m_norm_g, mem_norm_g=m_mem_norm_g, w_in=m_w_in, attn_q_norm=m_attn_q_norm,
             attn_k_norm=m_attn_k_norm, conv_w=m_conv_w, mem_w_kv=m_mem_w_kv, mem_q_norm=m_mem_q_norm,
             mem_k_norm=m_mem_k_norm, w_br_attn=m_w_br_attn, w_br_conv=m_w_br_conv, w_br_mem=m_w_br_mem, w_out=m_w_out)
    v = dict(norm_g=v_norm_g, mem_norm_g=v_mem_norm_g, w_in=v_w_in, attn_q_norm=v_attn_q_norm,
             attn_k_norm=v_attn_k_norm, conv_w=v_conv_w, mem_w_kv=v_mem_w_kv, mem_q_norm=v_mem_q_norm,
             mem_k_norm=v_mem_k_norm, w_br_attn=v_w_br_attn, w_br_conv=v_w_br_conv, w_br_mem=v_w_br_mem, w_out=v_w_out)
    Bl, _, D = x.shape
    cx, cy = lax.axis_index("x"), lax.axis_index("y")
    chip = 2 * cx + cy
    pos = jnp.stack([chip, lax.axis_index("c")]).astype(jnp.int32)
    order = jnp.stack([chip] + [2 * a + b for a, b in _other_chips(cx, cy)]).astype(jnp.int32)
    n = len(BIG)

    w_rel = _place_shard(w["w_in"], "col", jnp.zeros((1,), jnp.int32), WIRE_DTYPE, "place_w_in")
    conv_full = _place_shard(conv_w, "col", pos, F32, "place_conv_w")
    others = [_place_shard(w[name], kind, pos, WIRE_DTYPE, "place_" + name) for name, kind in BIG[1:]]
    hb, hbt, mhb = _norms(x, mem, norm_g, mem_norm_g)

    near = _w_in_copies((0, 1))
    send, recv, (w_rel, conv_full) = _start_copies("gather_near_start", [w_rel, conv_full], 4, near)
    proj = _proj_chunk(hb, w_rel, order, 0, None, "proj_own")
    w_rel, conv_full = _wait_copies("gather_near_wait", send, recv, [w_rel, conv_full], near, proj)
    w_rel, = _sibling_forward("gather_near_forward", [w_rel], 2, _w_in_landed((0, 1)))

    far = _w_in_copies((2,))
    send, recv, (w_rel, conv_full) = _start_copies("gather_far_start", [w_rel, conv_full], 2, far)
    proj = _proj_chunk(hb, w_rel, order, 1, proj, "proj_near_x")
    proj = _proj_chunk(hb, w_rel, order, 2, proj, "proj_near_y")
    w_rel, conv_full = _wait_copies("gather_far_wait", send, recv, [w_rel, conv_full], far, proj)
    w_rel, = _sibling_forward("gather_far_forward", [w_rel], 1, _w_in_landed((2,)))

    send, recv, (*others, w_rel) = _start_copies("gather_rest_start", [*others, w_rel], 3 * (n - 1), _other_weight_copies)
    proj = _proj_chunk(hb, w_rel, order, 3, proj, "proj_far")
    mixed = _mix_fwd(proj, Bl, attn_q_norm, attn_k_norm, conv_full)
    *others, w_rel = _wait_copies("gather_rest_wait", send, recv, [*others, w_rel], _other_weight_copies, mixed[2])
    others = _sibling_forward("gather_rest_forward", others, 3 * (n - 1), _other_weight_halves)
    W = {name: others[p] for p, (name, _) in enumerate(BIG[1:])}

    G, rest, rest_state = _weight_grads(
        x, mem, loss_target, norm_g, mem_norm_g, attn_q_norm, attn_k_norm, conv_full, mem_q_norm, mem_k_norm, W,
        (hb, hbt, mhb, proj, *mixed), early=lambda G, carry: _exchange_start(G, BIG[1:], pos, carry, "rest"))

    for_sibling = _dw_in_half(hbt, rest[0], pos, False, "dw_in_sibling")
    send, recv, (for_sibling, got, dproj) = _start_copies(
        "sibling_w_in_start", [for_sibling, lax.empty(for_sibling.shape, for_sibling.dtype), rest[0]], 1, _sibling_copy)
    mine = _dw_in_half(hbt, dproj, pos, True, "dw_in_own")
    for_sibling, got = _wait_copies("sibling_w_in_wait", send, recv, [for_sibling, got], _sibling_copy, mine)
    pre_w_in = _presum(mine, got, "col", pos, "presum_w_in")

    w_in_state, dproj = _exchange_start(G, BIG[:1], pos, dproj, "w_in", pres=[pre_w_in])
    grad_x, small = _input_grad((dproj, *rest[1:]), w_rel, order)
    pres, slots = _exchange_wait(w_in_state, grad_x)
    pres_rest, slots_rest = _exchange_wait(rest_state, grad_x)
    pres, slots = pres + pres_rest, slots + slots_rest
    grad_x = grad_x.reshape(x.shape)
    reds = [_reduce_into_shard(slots[p], pres[p], kind, pos, "reduce_" + name) for p, (name, kind) in enumerate(BIG)]
    grads = dict(zip([name for name, _ in BIG], _share_reduced(reds)))

    tot = _sum_small(_gather_small(small))[0]
    loss = tot[0]
    off = 128
    for name, size in (("norm_g", D), ("mem_norm_g", D), ("attn_q_norm", NGROUP * HEAD), ("attn_k_norm", NGROUP * HEAD),
                       ("conv_w", 3 * CONVW), ("mem_q_norm", MEM_HD), ("mem_k_norm", MEM_HD)):
        grads[name] = tot[off:off + size]
        off += size
    cw = conv_w.shape[1]
    grads["conv_w"] = lax.dynamic_slice(grads["conv_w"].reshape(3, CONVW), (0, chip * cw), (3, cw))
    for name in SMALL:
        grads[name] = grads[name].reshape(w[name].shape)

    delta, new_m, new_v = {}, {}, {}
    for name, _ in BIG:
        delta[name], new_m[name], new_v[name] = _adamw(w[name], grads[name], m[name], v[name], "adamw_" + name)

    def packed(t):
        return jnp.concatenate([t[name].reshape(1, -1) for name in SMALL], axis=1)

    ds, ms, vs = _adamw(packed(w), packed(grads), packed(m), packed(v), "adamw_small")
    off = 0
    for name in SMALL:
        size = w[name].size
        delta[name] = ds[0, off:off + size].reshape(w[name].shape)
        new_m[name] = ms[0, off:off + size].reshape(w[name].shape)
        new_v[name] = vs[0, off:off + size].reshape(w[name].shape)
        off += size

    return (loss, grad_x, *[grads[n] for n in WEIGHTS], *[delta[n] for n in WEIGHTS],
            *[new_m[n] for n in WEIGHTS], *[new_v[n] for n in WEIGHTS])
```

```python
import functools

import jax
import jax.numpy as jnp
from jax import lax
from jax.experimental import pallas as pl
from jax.experimental.pallas import tpu as pltpu

F32 = jnp.float32
MXU_DTYPE = jnp.bfloat16
WIRE_DTYPE = jnp.bfloat16
EPS = 1e-6
NEG = -1e30

HEAD = 128
HPG = 4
GW = HPG * HEAD
DILATIONS = (1, 4, 16)
NGROUP = len(DILATIONS)
BLK = 128
QKV = NGROUP * GW
CONVW = 1024
MEM_HEADS = 4
MEM_HD = 256
MEMW = MEM_HEADS * MEM_HD
Q0, K0, V0 = 0, QKV, 2 * QKV
ZA = 3 * QKV
CB, CC, CV, ZC = ZA + GW, ZA + GW + CONVW, ZA + GW + 2 * CONVW, ZA + GW + 3 * CONVW
MQ = ZC + CONVW
ZM = MQ + MEMW
G0 = ZM + MEMW

ADAM_LR, ADAM_B1, ADAM_B2, ADAM_EPS, ADAM_WD, ADAM_STEP = 0.001, 0.9, 0.999, 1e-08, 0.01, 10

VMEM_LIMIT = 56 * 1024 * 1024
MESH = pl.DeviceIdType.MESH
ANY = pl.BlockSpec(memory_space=pl.ANY)


def _tile(n, pref, mult=128):
    t = min(pref, n)
    while t > mult and (n % t or t % mult):
        t -= mult
    assert n % t == 0, (n, pref)
    return t


def _call(body, *, name, out_shape, grid=(), in_specs=None, out_specs=None, scratch_shapes=(),
          aliases=None, grid_spec=None):
    kw = {}
    if grid_spec is not None:
        kw["grid_spec"] = grid_spec
        ngrid = len(grid_spec.grid)
    else:
        kw.update(grid=grid, in_specs=in_specs, out_specs=out_specs, scratch_shapes=list(scratch_shapes))
        ngrid = len(grid)
    params = pltpu.CompilerParams(dimension_semantics=("arbitrary",) * ngrid, vmem_limit_bytes=VMEM_LIMIT)
    return pl.pallas_call(body, name=name, out_shape=out_shape, compiler_params=params,
                          input_output_aliases=aliases or {}, **kw)


_DIMS = {"nn": (((1,), (0,)), ((), ())), "nt": (((1,), (1,)), ((), ())), "tn": (((0,), (0,)), ((), ()))}


def _mxu(a, b, mode):
    return lax.dot_general(a.astype(MXU_DTYPE), b.astype(MXU_DTYPE), _DIMS[mode], preferred_element_type=F32)


@functools.partial(jax.custom_vjp, nondiff_argnums=(2,))
def _dot(a, b, mode):
    return _mxu(a, b, mode)


def _dot_fwd(a, b, mode):
    return _mxu(a, b, mode), (a, b)


def _dot_bwd(mode, res, g):
    a, b = res
    if mode == "nn":
        return _mxu(g, b, "nt"), _mxu(a, g, "tn")
    if mode == "nt":
        return _mxu(g, b, "nn"), _mxu(g, a, "tn")
    return _mxu(b, g, "nt"), _mxu(a, g, "nn")


_dot.defvjp(_dot_fwd, _dot_bwd)


def _sig(z):
    return 1.0 / (1.0 + jnp.exp(-z))


def _silu(z):
    return z * _sig(z)


def _rms_rows(t, g):
    return t * lax.rsqrt(jnp.mean(t * t, axis=-1, keepdims=True) + EPS) * g


def _attn_block(q, k2, v2, gq, gk, first):
    qn = _rms_rows(q, gq)
    kn = _rms_rows(k2, gk)
    s = jnp.where(_band_mask(first), _dot(qn, kn, "nt") * (HEAD ** -0.5), NEG)
    m = lax.stop_gradient(jnp.max(s, axis=-1, keepdims=True))
    p = jnp.exp(s - m)
    den = jnp.sum(p, axis=-1, keepdims=True)
    o = _dot(p, v2, "nn") / den
    return o, m + jnp.log(den)


def _band_mask(first):
    a = lax.broadcasted_iota(jnp.int32, (BLK, 2 * BLK), 0)
    b = lax.broadcasted_iota(jnp.int32, (BLK, 2 * BLK), 1)
    return (b >= a) & (b <= a + BLK) & (b >= jnp.where(first, BLK, 0))


def _norm_parts(t):
    r = lax.rsqrt(jnp.mean(t * t, axis=-1, keepdims=True) + EPS)
    return r, t * r


def _norm_bwd(dn, g, r, th):
    dth = dn * g
    return r * (dth - th * jnp.mean(dth * th, axis=-1, keepdims=True)), jnp.sum(dn * th, axis=0, keepdims=True)


def _attn_block_bwd(q, k2, v2, gq, gk, first, do, o, lse, dlse):
    scale = HEAD ** -0.5
    rq, qh = _norm_parts(q)
    rk, kh = _norm_parts(k2)
    qn, kn = qh * gq, kh * gk
    s = jnp.where(_band_mask(first), _mxu(qn, kn, "nt") * scale, NEG)
    p = jnp.exp(s - lse)
    ds = p * (_mxu(do, v2, "nt") + (dlse - jnp.sum(do * o, axis=-1, keepdims=True))) * scale
    dq, dgq = _norm_bwd(_mxu(ds, kn, "nn"), gq, rq, qh)
    dk2, dgk = _norm_bwd(_mxu(ds, qn, "tn"), gk, rk, kh)
    return dq, dk2, _mxu(p, do, "tn"), dgq, dgk


def _combine(o1, o2, o3, l1, l2, l3, z):
    m = lax.stop_gradient(jnp.maximum(jnp.maximum(l1, l2), l3))
    e1, e2, e3 = jnp.exp(l1 - m), jnp.exp(l2 - m), jnp.exp(l3 - m)
    return (e1 * o1 + e2 * o2 + e3 * o3) / (e1 + e2 + e3) * _silu(z)


def _mem_block(q, z, kv, gq, gk):
    outs = []
    for h in range(MEM_HEADS):
        sl = slice(h * MEM_HD, (h + 1) * MEM_HD)
        qn = _rms_rows(q[:, sl], gq)
        kn = _rms_rows(kv[:, sl], gk)
        s = _dot(qn, kn, "nt") * (MEM_HD ** -0.5)
        m = lax.stop_gradient(jnp.max(s, axis=-1, keepdims=True))
        p = jnp.exp(s - m)
        den = jnp.sum(p, axis=-1, keepdims=True)
        outs.append(_dot(p, kv[:, MEMW + h * MEM_HD:MEMW + (h + 1) * MEM_HD], "nn") / den)
    return jnp.concatenate(outs, axis=-1) * _silu(z)


def _cast(w, name):
    R, C = w.shape
    tr, tc = _tile(R, 512, 8), _tile(C, 2176)

    def body(w_ref, o_ref):
        o_ref[...] = w_ref[...].astype(o_ref.dtype)

    spec = pl.BlockSpec((tr, tc), lambda i, j: (i, j))
    return _call(body, name=name, grid=(R // tr, C // tc), in_specs=[spec], out_specs=spec,
                 out_shape=jax.ShapeDtypeStruct((R, C), WIRE_DTYPE))(w)


def _place_shard(w, kind, pos, dtype, name, slot=0, into=None):
    R, C = w.shape
    tr, tc = _tile(R, 512, 8), _tile(C, 2176)
    nr, nc = R // tr, C // tc

    def body(pos_ref, w_ref, *rest):
        rest[-1][...] = w_ref[...].astype(rest[-1].dtype)

    if kind == "col":
        full, out = (R, 4 * C), pl.BlockSpec((tr, tc), lambda i, j, pos_ref: (i, pos_ref[slot] * nc + j))
    else:
        full, out = (4 * R, C), pl.BlockSpec((tr, tc), lambda i, j, pos_ref: (pos_ref[slot] * nr + i, j))
    in_specs, args = [pl.BlockSpec((tr, tc), lambda i, j, pos_ref: (i, j))], [pos, w]
    if into is not None:
        in_specs.append(ANY)
        args.append(into)
    spec = pltpu.PrefetchScalarGridSpec(num_scalar_prefetch=1, grid=(nr, nc), in_specs=in_specs, out_specs=out)
    return _call(body, name=name, grid_spec=spec, out_shape=jax.ShapeDtypeStruct(full, dtype),
                 aliases={} if into is None else {2: 0})(*args)


def _matmul(a, b, mode, out_dtype, *, name, tm=512, tn=512, tk=512):
    if mode == "nn":
        (M, K), (_, N) = a.shape, b.shape
    elif mode == "nt":
        (M, K), (N, _) = a.shape, b.shape
    else:
        (K, M), (_, N) = a.shape, b.shape
    tm, tn, tk = _tile(M, tm), _tile(N, tn), _tile(K, tk)
    nk = K // tk

    def body(a_ref, b_ref, o_ref, *acc):
        part = lax.dot_general(a_ref[...], b_ref[...], _DIMS[mode], preferred_element_type=F32)
        if nk == 1:
            o_ref[...] = part.astype(o_ref.dtype)
            return
        acc_ref, = acc
        k = pl.program_id(2)

        @pl.when(k == 0)
        def _():
            acc_ref[...] = part

        @pl.when(k > 0)
        def _():
            acc_ref[...] += part

        @pl.when(k == nk - 1)
        def _():
            o_ref[...] = acc_ref[...].astype(o_ref.dtype)

    a_spec = pl.BlockSpec((tk, tm), lambda i, j, k: (k, i)) if mode == "tn" else pl.BlockSpec((tm, tk), lambda i, j, k: (i, k))
    b_spec = pl.BlockSpec((tn, tk), lambda i, j, k: (j, k)) if mode == "nt" else pl.BlockSpec((tk, tn), lambda i, j, k: (k, j))
    return _call(body, name=name, grid=(M // tm, N // tn, nk), in_specs=[a_spec, b_spec],
                 out_specs=pl.BlockSpec((tm, tn), lambda i, j, k: (i, j)),
                 out_shape=jax.ShapeDtypeStruct((M, N), out_dtype),
                 scratch_shapes=[] if nk == 1 else [pltpu.VMEM((tm, tn), F32)])(a, b)


def _rms_fwd(x, g, name):
    R, D = x.shape
    tr = _tile(R, 512)

    def body(x_ref, g_ref, o_ref, t_ref):
        y = _rms_rows(x_ref[...], g_ref[...])
        o_ref[...] = y.astype(o_ref.dtype)
        t_ref[...] = y.T.astype(t_ref.dtype)

    row = pl.BlockSpec((tr, D), lambda i: (i, 0))
    return _call(body, name=name, grid=(R // tr,), in_specs=[row, pl.BlockSpec((1, D), lambda i: (0, 0))],
                 out_specs=[row, pl.BlockSpec((D, tr), lambda i: (0, i))],
                 out_shape=[jax.ShapeDtypeStruct((R, D), MXU_DTYPE), jax.ShapeDtypeStruct((D, R), MXU_DTYPE)])(x, g)


def _rms_bwd(x, dh, g, dy, name):
    R, D = x.shape
    tr = _tile(R, 256)
    with_dx = dy is not None

    def body(*refs):
        if with_dx:
            x_ref, dh_ref, g_ref, dy_ref, dx_ref, dg_ref = refs
        else:
            x_ref, dh_ref, g_ref, dg_ref = refs
        xv, dhv = x_ref[...], dh_ref[...]
        r = lax.rsqrt(jnp.mean(xv * xv, axis=-1, keepdims=True) + EPS)
        xh = xv * r

        @pl.when(pl.program_id(0) == 0)
        def _():
            dg_ref[...] = jnp.zeros_like(dg_ref)

        dg_ref[...] += jnp.sum(dhv * xh, axis=0, keepdims=True)
        if with_dx:
            dxh = dhv * g_ref[...]
            dx_ref[...] = dy_ref[...] + r * (dxh - xh * jnp.mean(dxh * xh, axis=-1, keepdims=True))

    row = pl.BlockSpec((tr, D), lambda i: (i, 0))
    vec = pl.BlockSpec((1, D), lambda i: (0, 0))
    dg_shape = jax.ShapeDtypeStruct((1, D), F32)
    if with_dx:
        return _call(body, name=name, grid=(R // tr,), in_specs=[row, row, vec, row], out_specs=[row, vec],
                     out_shape=[jax.ShapeDtypeStruct((R, D), F32), dg_shape])(x, dh, g, dy)
    return None, _call(body, name=name, grid=(R // tr,), in_specs=[row, row, vec], out_specs=vec,
                       out_shape=dg_shape)(x, dh, g)


def _attn_geom(g, d):
    hc = HPG if d == 1 else 1
    cw = hc * HEAD
    cq, ck, cv = (Q0 + g * GW) // cw, (K0 + g * GW) // cw, (V0 + g * GW) // cw
    return (1, BLK * d, cw), hc, HPG // hc, cq, ck, cv


def _rows(ref, r, d, sl):
    if d == 1:
        return ref[0, :, sl]
    return ref.at[0][pl.ds(r, BLK, stride=d), sl]


def _set_rows(ref, r, d, sl, val):
    if d == 1:
        ref[0, :, sl] = val
    else:
        ref.at[0][pl.ds(r, BLK, stride=d), sl] = val


def _stage_rows(ref, r, d, sl, val):
    if d == 1:
        ref[:, sl] = val
    else:
        ref[pl.ds(r, BLK, stride=d), sl] = val


def _attn_fwd(proj3, gq, gk, g, d):
    Bl, S, _ = proj3.shape
    blk, hc, ncb, cq, ck, cv = _attn_geom(g, d)
    nb = S // blk[1]

    def body(q_ref, kp_ref, kc_ref, vp_ref, vc_ref, gq_ref, gk_ref, o_ref, lse_ref):
        first = pl.program_id(2) == 0
        for r in range(d):
            for h in range(hc):
                sl = slice(h * HEAD, (h + 1) * HEAD)
                k2 = jnp.concatenate([_rows(kp_ref, r, d, sl), _rows(kc_ref, r, d, sl)], axis=0)
                v2 = jnp.concatenate([_rows(vp_ref, r, d, sl), _rows(vc_ref, r, d, sl)], axis=0)
                o, lse = _attn_block(_rows(q_ref, r, d, sl), k2, v2, gq_ref[...], gk_ref[...], first)
                _set_rows(o_ref, r, d, sl, o)
                _set_rows(lse_ref, r, d, sl, jnp.broadcast_to(lse, (BLK, HEAD)))

    def cur(c0):
        return pl.BlockSpec(blk, lambda b, j, i: (b, i, c0 + j))

    def prev(c0):
        return pl.BlockSpec(blk, lambda b, j, i: (b, jnp.maximum(i - 1, 0), c0 + j))

    vec = pl.BlockSpec((1, HEAD), lambda b, j, i: (0, 0))
    out = pl.BlockSpec(blk, lambda b, j, i: (b, i, j))
    shp = jax.ShapeDtypeStruct((Bl, S, GW), F32)
    return _call(body, name=f"attn_fwd_g{g}", grid=(Bl, ncb, nb),
                 in_specs=[cur(cq), prev(ck), cur(ck), prev(cv), cur(cv), vec, vec],
                 out_specs=[out, out], out_shape=[shp, shp])(proj3, proj3, proj3, proj3, proj3, gq, gk)


def _attn_bwd(proj3, gq, gk, o3, l3, do3, dl3, g, d):
    Bl, S, _ = proj3.shape
    blk, hc, ncb, cq, ck, cv = _attn_geom(g, d)
    nb = S // blk[1]

    def body(q_ref, kp_ref, kc_ref, vp_ref, vc_ref, gq_ref, gk_ref, o_ref, l_ref, do_ref, dl_ref,
             dq_ref, dk_ref, dv_ref, dgq_ref, dgk_ref, ck_ref, cv_ref, sq_ref, sk_ref, sv_ref):
        i = pl.program_id(2)
        first = i == 0

        @pl.when((pl.program_id(0) == 0) & (pl.program_id(1) == 0) & first)
        def _():
            dgq_ref[...] = jnp.zeros_like(dgq_ref)
            dgk_ref[...] = jnp.zeros_like(dgk_ref)

        @pl.when(first)
        def _():
            ck_ref[...] = jnp.zeros_like(ck_ref)
            cv_ref[...] = jnp.zeros_like(cv_ref)

        @pl.when(i < nb)
        def _():
            dgq, dgk = jnp.zeros((1, HEAD), F32), jnp.zeros((1, HEAD), F32)
            for r in range(d):
                rs = slice(r * BLK, (r + 1) * BLK)
                for h in range(hc):
                    sl = slice(h * HEAD, (h + 1) * HEAD)
                    k2 = jnp.concatenate([_rows(kp_ref, r, d, sl), _rows(kc_ref, r, d, sl)], axis=0)
                    v2 = jnp.concatenate([_rows(vp_ref, r, d, sl), _rows(vc_ref, r, d, sl)], axis=0)
                    dq, dk2, dv2, a, b = _attn_block_bwd(
                        _rows(q_ref, r, d, sl), k2, v2, gq_ref[...], gk_ref[...], first, _rows(do_ref, r, d, sl),
                        _rows(o_ref, r, d, sl), _rows(l_ref, r, d, sl)[:, :1], _rows(dl_ref, r, d, sl)[:, :1])
                    _stage_rows(sq_ref, r, d, sl, dq)
                    _stage_rows(sk_ref, r, d, sl, ck_ref[rs, sl] + dk2[:BLK])
                    _stage_rows(sv_ref, r, d, sl, cv_ref[rs, sl] + dv2[:BLK])
                    ck_ref[rs, sl] = dk2[BLK:]
                    cv_ref[rs, sl] = dv2[BLK:]
                    dgq, dgk = dgq + a, dgk + b
            dgq_ref[...] += dgq
            dgk_ref[...] += dgk
            dq_ref[0] = sq_ref[...].astype(dq_ref.dtype)

        @pl.when(i == nb)
        def _():
            for r in range(d):
                rs = slice(r * BLK, (r + 1) * BLK)
                _stage_rows(sk_ref, r, d, slice(None), ck_ref[rs, :])
                _stage_rows(sv_ref, r, d, slice(None), cv_ref[rs, :])

        dk_ref[0] = sk_ref[...].astype(dk_ref.dtype)
        dv_ref[0] = sv_ref[...].astype(dv_ref.dtype)

    def cur(c0):
        return pl.BlockSpec(blk, lambda b, j, i: (b, jnp.minimum(i, nb - 1), c0 + j))

    def prev(c0):
        return pl.BlockSpec(blk, lambda b, j, i: (b, jnp.clip(i - 1, 0, nb - 1), c0 + j))

    vec = pl.BlockSpec((1, HEAD), lambda b, j, i: (0, 0))
    at_q = pl.BlockSpec(blk, lambda b, j, i: (b, jnp.minimum(i, nb - 1), j))
    at_k = pl.BlockSpec(blk, lambda b, j, i: (b, jnp.maximum(i - 1, 0), j))
    shp = jax.ShapeDtypeStruct((Bl, S, GW), MXU_DTYPE)
    gshp = jax.ShapeDtypeStruct((1, HEAD), F32)
    return _call(body, name=f"attn_bwd_g{g}", grid=(Bl, ncb, nb + 1),
                 in_specs=[cur(cq), prev(ck), cur(ck), prev(cv), cur(cv), vec, vec, at_q, at_q, at_q, at_q],
                 out_specs=[at_q, at_k, at_k, vec, vec], out_shape=[shp, shp, shp, gshp, gshp],
                 scratch_shapes=[pltpu.VMEM(blk[1:], F32)] * 5,
                 )(proj3, proj3, proj3, proj3, proj3, gq, gk, o3, l3, do3, dl3)


def _combine_fwd(os, ls, proj2):
    T = proj2.shape[0]
    tr = _tile(T, 512)

    def body(o1, o2, o3, l1, l2, l3, z, a_ref):
        a_ref[...] = _combine(o1[...], o2[...], o3[...], l1[...], l2[...], l3[...], z[...]).astype(a_ref.dtype)

    row = pl.BlockSpec((tr, GW), lambda i: (i, 0))
    return _call(body, name="combine_fwd", grid=(T // tr,),
                 in_specs=[row] * 6 + [pl.BlockSpec((tr, GW), lambda i: (i, ZA // GW))], out_specs=row,
                 out_shape=jax.ShapeDtypeStruct((T, GW), MXU_DTYPE))(*os, *ls, proj2)


def _combine_bwd(os, ls, proj2, da):
    T = proj2.shape[0]
    tr = _tile(T, 256)

    def body(o1, o2, o3, l1, l2, l3, z, da_ref, d1, d2, d3, e1, e2, e3, dz_ref):
        _, vjp = jax.vjp(_combine, o1[...], o2[...], o3[...], l1[...], l2[...], l3[...], z[...])
        go1, go2, go3, gl1, gl2, gl3, gz = vjp(da_ref[...])
        d1[...], d2[...], d3[...] = go1, go2, go3
        dz_ref[...] = gz.astype(dz_ref.dtype)
        for ref, gl in ((e1, gl1), (e2, gl2), (e3, gl3)):
            for h in range(HPG):
                sl = slice(h * HEAD, (h + 1) * HEAD)
                ref[:, sl] = jnp.broadcast_to(jnp.sum(gl[:, sl], axis=-1, keepdims=True), (tr, HEAD))

    row = pl.BlockSpec((tr, GW), lambda i: (i, 0))
    f = jax.ShapeDtypeStruct((T, GW), F32)
    outs = _call(body, name="combine_bwd", grid=(T // tr,),
                 in_specs=[row] * 6 + [pl.BlockSpec((tr, GW), lambda i: (i, ZA // GW)), row],
                 out_specs=[row] * 7, out_shape=[f] * 6 + [jax.ShapeDtypeStruct((T, GW), MXU_DTYPE)],
                 )(*os, *ls, proj2, da)
    return outs[:3], outs[3:6], outs[6]


def _shift_down(u, j, t):
    return jnp.where(t >= j, pltpu.roll(u, j, 0), 0.0)


def _shift_up(u, j, t):
    n = u.shape[0]
    return jnp.where(t < n - j, pltpu.roll(u, n - j, 0), 0.0)


def _conv_specs(Bl, S, cw):
    def sec(c0):
        return pl.BlockSpec((1, S, cw), lambda j, b: (b, 0, c0 // cw + j))
    return [sec(CB), sec(CC), sec(CV), sec(ZC)], pl.BlockSpec((3, cw), lambda j, b: (0, j))


def _conv_fwd(proj3, conv_w):
    Bl, S, _ = proj3.shape
    cw = 256
    secs, wspec = _conv_specs(Bl, S, cw)

    def body(b_ref, c_ref, v_ref, z_ref, w_ref, o_ref):
        t = lax.broadcasted_iota(jnp.int32, (S, cw), 0)
        u = c_ref[0] * v_ref[0]
        y = w_ref[0:1, :] * u + w_ref[1:2, :] * _shift_down(u, 1, t) + w_ref[2:3, :] * _shift_down(u, 2, t)
        o_ref[0] = (b_ref[0] * y * _silu(z_ref[0])).astype(o_ref.dtype)

    return _call(body, name="conv_fwd", grid=(CONVW // cw, Bl), in_specs=secs + [wspec],
                 out_specs=pl.BlockSpec((1, S, cw), lambda j, b: (b, 0, j)),
                 out_shape=jax.ShapeDtypeStruct((Bl, S, CONVW), MXU_DTYPE))(proj3, proj3, proj3, proj3, conv_w)


def _conv_bwd(proj3, conv_w, dcc3):
    Bl, S, _ = proj3.shape
    cw = 256
    secs, wspec = _conv_specs(Bl, S, cw)

    def body(b_ref, c_ref, v_ref, z_ref, w_ref, d_ref, db_ref, dc_ref, dv_ref, dz_ref, dw_ref):
        t = lax.broadcasted_iota(jnp.int32, (S, cw), 0)
        bv, cv, vv, zv, dv = b_ref[0], c_ref[0], v_ref[0], z_ref[0], d_ref[0]
        u = cv * vv
        u1, u2 = _shift_down(u, 1, t), _shift_down(u, 2, t)
        y = w_ref[0:1, :] * u + w_ref[1:2, :] * u1 + w_ref[2:3, :] * u2
        sg = _sig(zv)
        sz = zv * sg
        gy = dv * bv * sz
        db_ref[0] = (dv * y * sz).astype(db_ref.dtype)
        dz_ref[0] = (dv * bv * y * sg * (1.0 + zv * (1.0 - sg))).astype(dz_ref.dtype)
        du = w_ref[0:1, :] * gy + w_ref[1:2, :] * _shift_up(gy, 1, t) + w_ref[2:3, :] * _shift_up(gy, 2, t)
        dc_ref[0] = (du * vv).astype(dc_ref.dtype)
        dv_ref[0] = (du * cv).astype(dv_ref.dtype)

        @pl.when(pl.program_id(1) == 0)
        def _():
            dw_ref[...] = jnp.zeros_like(dw_ref)

        dw_ref[0:1, :] += jnp.sum(gy * u, axis=0, keepdims=True)
        dw_ref[1:2, :] += jnp.sum(gy * u1, axis=0, keepdims=True)
        dw_ref[2:3, :] += jnp.sum(gy * u2, axis=0, keepdims=True)

    blk = pl.BlockSpec((1, S, cw), lambda j, b: (b, 0, j))
    shp = jax.ShapeDtypeStruct((Bl, S, CONVW), MXU_DTYPE)
    return _call(body, name="conv_bwd", grid=(CONVW // cw, Bl), in_specs=secs + [wspec, blk],
                 out_specs=[blk] * 4 + [wspec], out_shape=[shp] * 4 + [jax.ShapeDtypeStruct((3, CONVW), F32)],
                 )(proj3, proj3, proj3, proj3, conv_w, dcc3)


def _mem_specs(S, tq):
    q = pl.BlockSpec((1, tq, MEMW), lambda b, j: (b, j, MQ // MEMW))
    z = pl.BlockSpec((1, tq, MEMW), lambda b, j: (b, j, ZM // MEMW))
    kv = pl.BlockSpec((1, MEM_HD, 2 * MEMW), lambda b, j: (b, 0, 0))
    vec = pl.BlockSpec((1, MEM_HD), lambda b, j: (0, 0))
    blk = pl.BlockSpec((1, tq, MEMW), lambda b, j: (b, j, 0))
    return q, z, kv, vec, blk


def _mem_fwd(proj3, mkv3, gq, gk):
    Bl, S, _ = proj3.shape
    tq = _tile(S, 512)
    q, z, kv, vec, blk = _mem_specs(S, tq)

    def body(q_ref, z_ref, kv_ref, gq_ref, gk_ref, o_ref):
        o_ref[0] = _mem_block(q_ref[0], z_ref[0], kv_ref[0], gq_ref[...], gk_ref[...]).astype(o_ref.dtype)

    return _call(body, name="mem_fwd", grid=(Bl, S // tq), in_specs=[q, z, kv, vec, vec], out_specs=blk,
                 out_shape=jax.ShapeDtypeStruct((Bl, S, MEMW), MXU_DTYPE))(proj3, proj3, mkv3, gq, gk)


def _mem_bwd(proj3, mkv3, gq, gk, dmo3):
    Bl, S, _ = proj3.shape
    tq = _tile(S, 256)
    q, z, kv, vec, blk = _mem_specs(S, tq)

    def body(q_ref, z_ref, kv_ref, gq_ref, gk_ref, d_ref, dq_ref, dz_ref, dkv_ref, dgq_ref, dgk_ref):
        _, vjp = jax.vjp(_mem_block, q_ref[0], z_ref[0], kv_ref[0], gq_ref[...], gk_ref[...])
        dq, dz, dkv, dgq, dgk = vjp(d_ref[0])
        dq_ref[0] = dq.astype(dq_ref.dtype)
        dz_ref[0] = dz.astype(dz_ref.dtype)
        j = pl.program_id(1)

        @pl.when(j == 0)
        def _():
            dkv_ref[0] = jnp.zeros_like(dkv)

        @pl.when((j == 0) & (pl.program_id(0) == 0))
        def _():
            dgq_ref[...] = jnp.zeros_like(dgq_ref)
            dgk_ref[...] = jnp.zeros_like(dgk_ref)

        dkv_ref[0] += dkv
        dgq_ref[...] += dgq
        dgk_ref[...] += dgk

    shp = jax.ShapeDtypeStruct((Bl, S, MEMW), MXU_DTYPE)
    gshp = jax.ShapeDtypeStruct((1, MEM_HD), F32)
    return _call(body, name="mem_bwd", grid=(Bl, S // tq), in_specs=[q, z, kv, vec, vec, blk],
                 out_specs=[blk, blk, kv, vec, vec],
                 out_shape=[shp, shp, jax.ShapeDtypeStruct(mkv3.shape, F32), gshp, gshp],
                 )(proj3, proj3, mkv3, gq, gk, dmo3)


def _merge_specs(T, D, tm, tn):
    def act(w):
        return pl.BlockSpec((tm, w), lambda i, n: (i, 0))

    def wsp(w):
        return pl.BlockSpec((w, tn), lambda i, n: (0, n))

    gates = [pl.BlockSpec((tm, tn), lambda i, n, k=k: (i, (G0 + k * D) // tn + n)) for k in range(3)]
    tile = pl.BlockSpec((tm, tn), lambda i, n: (i, n))
    return act, wsp, gates, tile


def _merge_fwd(a, cc, mo, wa, wc, wm, proj2):
    T, D = a.shape[0], wa.shape[1]
    tm, tn = _tile(T, 1024), _tile(D, 512)
    act, wsp, gates, tile = _merge_specs(T, D, tm, tn)

    def body(a_ref, c_ref, m_ref, wa_ref, wc_ref, wm_ref, g0, g1, g2, mg_ref, mt_ref, pa_ref, pc_ref, pm_ref):
        pa = jnp.dot(a_ref[...], wa_ref[...], preferred_element_type=F32)
        pc = jnp.dot(c_ref[...], wc_ref[...], preferred_element_type=F32)
        pm = jnp.dot(m_ref[...], wm_ref[...], preferred_element_type=F32)
        mg = _sig(g0[...]) * pa + _sig(g1[...]) * pc + _sig(g2[...]) * pm
        mg_ref[...] = mg.astype(mg_ref.dtype)
        mt_ref[...] = mg.T.astype(mt_ref.dtype)
        pa_ref[...] = pa.astype(pa_ref.dtype)
        pc_ref[...] = pc.astype(pc_ref.dtype)
        pm_ref[...] = pm.astype(pm_ref.dtype)

    shp = jax.ShapeDtypeStruct((T, D), MXU_DTYPE)
    return _call(body, name="merge_fwd", grid=(T // tm, D // tn),
                 in_specs=[act(GW), act(CONVW), act(MEMW), wsp(GW), wsp(CONVW), wsp(MEMW)] + gates,
                 out_specs=[tile, pl.BlockSpec((tn, tm), lambda i, n: (n, i)), tile, tile, tile],
                 out_shape=[shp, jax.ShapeDtypeStruct((D, T), MXU_DTYPE), shp, shp, shp],
                 )(a, cc, mo, wa, wc, wm, proj2, proj2, proj2)


def _merge_bwd(dyb, w_out, proj2, pa, pc, pm):
    T, D = dyb.shape
    tm, tn = _tile(T, 1024), _tile(D, 512)
    _, _, gates, tile = _merge_specs(T, D, tm, tn)

    def body(dy_ref, w_ref, g0, g1, g2, p0, p1, p2, dp0, dp1, dp2, dg0, dg1, dg2):
        dm = lax.dot_general(dy_ref[...], w_ref[...], _DIMS["nt"], preferred_element_type=F32)
        for g_ref, p_ref, dp_ref, dg_ref in ((g0, p0, dp0, dg0), (g1, p1, dp1, dg1), (g2, p2, dp2, dg2)):
            gt = _sig(g_ref[...])
            dp_ref[...] = (gt * dm).astype(dp_ref.dtype)
            dg_ref[...] = (dm * p_ref[...].astype(F32) * gt * (1.0 - gt)).astype(dg_ref.dtype)

    shp = jax.ShapeDtypeStruct((T, D), MXU_DTYPE)
    return _call(body, name="merge_bwd", grid=(T // tm, D // tn),
                 in_specs=[pl.BlockSpec((tm, D), lambda i, n: (i, 0)), pl.BlockSpec((tn, D), lambda i, n: (n, 0))]
                 + gates + [tile] * 3,
                 out_specs=[tile] * 6, out_shape=[shp] * 6)(dyb, w_out, proj2, proj2, proj2, pa, pc, pm)


def _out_loss(merged, w_out, x, tgt):
    T, D = x.shape
    tm = _tile(T, 512)

    def body(m_ref, w_ref, x_ref, t_ref, dy_ref, dyb_ref, loss_ref):
        err = x_ref[...] + jnp.dot(m_ref[...], w_ref[...], preferred_element_type=F32) - t_ref[...]
        dy = err * (1.0 / D)
        dy_ref[...] = dy
        dyb_ref[...] = dy.astype(dyb_ref.dtype)

        @pl.when(pl.program_id(0) == 0)
        def _():
            loss_ref[...] = jnp.zeros_like(loss_ref)

        loss_ref[...] += jnp.sum(err * err) * (0.5 / D)

    row = pl.BlockSpec((tm, D), lambda i: (i, 0))
    return _call(body, name="out_loss", grid=(T // tm,),
                 in_specs=[row, pl.BlockSpec((D, D), lambda i: (0, 0)), row, row],
                 out_specs=[row, row, pl.BlockSpec((1, 128), lambda i: (0, 0))],
                 out_shape=[jax.ShapeDtypeStruct((T, D), F32), jax.ShapeDtypeStruct((T, D), MXU_DTYPE),
                            jax.ShapeDtypeStruct((1, 128), F32)])(merged, w_out, x, tgt)


def _proj_chunk(hb, w, meta, j, half, buf, name):
    T, D = hb.shape
    Cs = w.shape[1] // 4
    tm, tn = _tile(T, 1024), _tile(Cs // 2, 2176)
    nh = Cs // 2 // tn

    def body(meta_ref, a_ref, b_ref, *rest):
        rest[-1][...] = jnp.dot(a_ref[...], b_ref[...], preferred_element_type=F32)

    def tile(n, m):
        if half is None:
            return n
        return (m[4] if half == 0 else 1 - m[4]) * nh + n

    in_specs = [pl.BlockSpec((tm, D), lambda n, i, m: (i, 0)),
                pl.BlockSpec((D, tn), lambda n, i, m: (0, j * 2 * nh + tile(n, m)))]
    args = [meta, hb, w]
    if buf is not None:
        in_specs.append(ANY)
        args.append(buf)
    spec = pltpu.PrefetchScalarGridSpec(
        num_scalar_prefetch=1, grid=(nh if half is not None else 2 * nh, T // tm), in_specs=in_specs,
        out_specs=pl.BlockSpec((tm, tn), lambda n, i, m: (i, m[j] * 2 * nh + tile(n, m))))
    return _call(body, name=name, grid_spec=spec, out_shape=jax.ShapeDtypeStruct((T, 4 * Cs), F32),
                 aliases={} if buf is None else {3: 0})(*args)


def _norms(x, mem, norm_g, mem_norm_g):
    D = x.shape[-1]
    hb, hbt = _rms_fwd(x.reshape(-1, D), norm_g.reshape(1, D), "rms_x")
    mhb, _ = _rms_fwd(mem.reshape(-1, D), mem_norm_g.reshape(1, D), "rms_mem")
    return hb, hbt, mhb


def _mix_fwd(proj2, Bl, gq_all, gk_all, conv_w):
    T, IN = proj2.shape
    proj3 = proj2.reshape(Bl, T // Bl, IN)
    os, ls = [], []
    for g, d in enumerate(DILATIONS):
        o, l = _attn_fwd(proj3, gq_all[g:g + 1], gk_all[g:g + 1], g, d)
        os.append(o.reshape(T, GW))
        ls.append(l.reshape(T, GW))
    a = _combine_fwd(os, ls, proj2)
    cc = _conv_fwd(proj3, conv_w).reshape(T, CONVW)
    return os, ls, a, cc


def _weight_grads(x, mem, tgt, norm_g, mem_norm_g, gq_all, gk_all, conv_w, mem_gq, mem_gk, W, pre, early=None):
    Bl, S, D = x.shape
    T = Bl * S
    hb, hbt, mhb, proj2, os, ls, a, cc = pre
    IN = proj2.shape[1]
    proj3 = proj2.reshape(Bl, S, IN)
    x2, tgt2 = x.reshape(T, D), tgt.reshape(T, D)
    mem2 = mem.reshape(-1, D)
    ng, mng = norm_g.reshape(1, D), mem_norm_g.reshape(1, D)
    mgq, mgk = mem_gq.reshape(1, MEM_HD), mem_gk.reshape(1, MEM_HD)
    gqs = [gq_all[g:g + 1] for g in range(NGROUP)]
    gks = [gk_all[g:g + 1] for g in range(NGROUP)]

    mkv = _matmul(mhb, W["mem_w_kv"], "nn", F32, name="mem_kv", tm=512, tn=1024, tk=D)
    mkv3 = mkv.reshape(Bl, -1, 2 * MEMW)
    mo = _mem_fwd(proj3, mkv3, mgq, mgk).reshape(T, MEMW)
    merged, mergedt, pa, pc, pm = _merge_fwd(a, cc, mo, W["w_br_attn"], W["w_br_conv"], W["w_br_mem"], proj2)
    dy, dyb, loss = _out_loss(merged, W["w_out"], x2, tgt2)

    G = {}
    G["w_out"] = _matmul(mergedt, dyb, "nn", WIRE_DTYPE, name="dw_out", tm=1024, tn=512, tk=T)
    dpa, dpc, dpm, dg0, dg1, dg2 = _merge_bwd(dyb, W["w_out"], proj2, pa, pc, pm)
    G["w_br_attn"] = _matmul(a, dpa, "tn", WIRE_DTYPE, name="dw_br_attn", tm=512, tn=1024, tk=512)
    G["w_br_conv"] = _matmul(cc, dpc, "tn", WIRE_DTYPE, name="dw_br_conv", tm=1024, tn=1024, tk=512)
    G["w_br_mem"] = _matmul(mo, dpm, "tn", WIRE_DTYPE, name="dw_br_mem", tm=1024, tn=1024, tk=512)
    da = _matmul(dpa, W["w_br_attn"], "nt", F32, name="d_attn", tm=1024, tn=512, tk=D)
    dcc = _matmul(dpc, W["w_br_conv"], "nt", F32, name="d_conv", tm=1024, tn=1024, tk=D)
    dmo = _matmul(dpm, W["w_br_mem"], "nt", F32, name="d_mem", tm=1024, tn=1024, tk=D)
    dmq, dzm, dmkv3, dmgq, dmgk = _mem_bwd(proj3, mkv3, mgq, mgk, dmo.reshape(Bl, S, MEMW))
    dmkv = _cast(dmkv3.reshape(-1, 2 * MEMW), "cast_dmkv")
    G["mem_w_kv"] = _matmul(mhb, dmkv, "tn", WIRE_DTYPE, name="dw_mem_kv", tm=1024, tn=1024, tk=512)
    early_state, da = (None, da) if early is None else early(G, da)
    dmh = _matmul(dmkv, W["mem_w_kv"], "nt", F32, name="d_memh", tm=512, tn=1024, tk=2 * MEMW)
    _, dmng = _rms_bwd(mem2, dmh, mng, None, "rms_mem_bwd")

    dos, dls, dza = _combine_bwd(os, ls, proj2, da)
    dqs, dks, dvs, dgq, dgk = [], [], [], [], []
    for g, d in enumerate(DILATIONS):
        dq, dk, dv, gq_g, gk_g = _attn_bwd(proj3, gqs[g], gks[g], os[g].reshape(Bl, S, GW), ls[g].reshape(Bl, S, GW),
                                           dos[g].reshape(Bl, S, GW), dls[g].reshape(Bl, S, GW), g, d)
        dqs.append(dq.reshape(T, GW).astype(MXU_DTYPE))
        dks.append(dk.reshape(T, GW).astype(MXU_DTYPE))
        dvs.append(dv.reshape(T, GW).astype(MXU_DTYPE))
        dgq.append(gq_g)
        dgk.append(gk_g)
    dcb, dcc_, dcv, dzc, dconv_w = _conv_bwd(proj3, conv_w, dcc.reshape(Bl, S, CONVW))

    dproj = jnp.concatenate(dqs + dks + dvs + [dza] + [t.reshape(T, CONVW) for t in (dcb, dcc_, dcv, dzc)]
                            + [dmq.reshape(T, MEMW), dzm.reshape(T, MEMW), dg0, dg1, dg2], axis=1)
    small = [loss, None, dmng] + dgq + dgk + [dconv_w.reshape(1, 3 * CONVW), dmgq, dmgk]
    return G, (dproj, x2, ng, dy, small), early_state


def _dw_in_half(hbt, dproj, pos, own, name):
    D, T = hbt.shape
    IN = dproj.shape[1]
    R, tn = D // 2, _tile(IN, 512)

    def body(pos_ref, a_ref, b_ref, o_ref):
        o_ref[...] = jnp.dot(a_ref[...], b_ref[...], preferred_element_type=F32).astype(o_ref.dtype)

    spec = pltpu.PrefetchScalarGridSpec(
        num_scalar_prefetch=1, grid=(IN // tn,),
        in_specs=[pl.BlockSpec((R, T), lambda j, p: (p[1] if own else 1 - p[1], 0)),
                  pl.BlockSpec((T, tn), lambda j, p: (0, j))],
        out_specs=pl.BlockSpec((R, tn), lambda j, p: (0, j)))
    return _call(body, name=name, grid_spec=spec, out_shape=jax.ShapeDtypeStruct((R, IN), WIRE_DTYPE))(pos, hbt, dproj)


def _d_h(dproj, w, order):
    T, IN = dproj.shape
    D, Cs = w.shape[0], IN // 4
    tm, tn = _tile(T, 1024), _tile(D, 1024)

    def body(order_ref, a_ref, b_ref, o_ref, acc_ref):
        part = lax.dot_general(a_ref[...], b_ref[...], _DIMS["nt"], preferred_element_type=F32)
        k = pl.program_id(2)

        @pl.when(k == 0)
        def _():
            acc_ref[...] = part

        @pl.when(k > 0)
        def _():
            acc_ref[...] += part

        @pl.when(k == 3)
        def _():
            o_ref[...] = acc_ref[...]

    spec = pltpu.PrefetchScalarGridSpec(
        num_scalar_prefetch=1, grid=(T // tm, D // tn, 4),
        in_specs=[pl.BlockSpec((tm, Cs), lambda i, n, k, o: (i, o[k])), pl.BlockSpec((tn, Cs), lambda i, n, k, o: (n, k))],
        out_specs=pl.BlockSpec((tm, tn), lambda i, n, k, o: (i, n)), scratch_shapes=[pltpu.VMEM((tm, tn), F32)])
    return _call(body, name="d_h", grid_spec=spec, out_shape=jax.ShapeDtypeStruct((T, D), F32))(order, dproj, w)


def _input_grad(rest, w_in, order):
    dproj, x2, ng, dy, small = rest
    dh = _d_h(dproj, w_in, order)
    grad_x, dng = _rms_bwd(x2, dh, ng, dy, "rms_x_bwd")
    small = [dng if t is None else t for t in small]
    return grad_x, jnp.concatenate(small, axis=1)


def _local_step(x, mem, tgt, norm_g, mem_norm_g, gq_all, gk_all, conv_w, mem_gq, mem_gk, W):
    hb, hbt, mhb = _norms(x, mem, norm_g, mem_norm_g)
    Cs = W["w_in"].shape[1] // 4
    shards = (0, 2, 1, 3)
    order = jnp.array(shards, dtype=jnp.int32)
    w_rel = jnp.concatenate([W["w_in"][:, s * Cs:(s + 1) * Cs] for s in shards], axis=1)
    meta = jnp.array(shards + (0,), dtype=jnp.int32)
    proj2 = _proj_chunk(hb, w_rel, meta, 0, None, None, "proj_0")
    for j in range(1, 4):
        for half in (1, 0):
            proj2 = _proj_chunk(hb, w_rel, meta, j, half, proj2, f"proj_{j}_{half}")
    pre = (hb, hbt, mhb, proj2, *_mix_fwd(proj2, x.shape[0], gq_all, gk_all, conv_w))
    G, rest, _ = _weight_grads(x, mem, tgt, norm_g, mem_norm_g, gq_all, gk_all, conv_w, mem_gq, mem_gk, W, pre)
    pos = jnp.zeros((2,), jnp.int32)
    G["w_in"] = jnp.concatenate([_dw_in_half(hbt, rest[0], pos, True, "dw_in_own"),
                                 _dw_in_half(hbt, rest[0], pos, False, "dw_in_sibling")], axis=0)
    grad_x, small = _input_grad(rest, w_rel, order)
    return grad_x.reshape(x.shape), G, small


BIG = (("w_in", "col"), ("mem_w_kv", "row"), ("w_br_attn", "col"), ("w_br_conv", "col"),
       ("w_br_mem", "col"), ("w_out", "row"))


def _coords():
    return lax.axis_index("x"), lax.axis_index("y"), lax.axis_index("c")


def _other_chips(x, y):
    return [(1 - x, y), (x, 1 - y), (1 - x, 1 - y)]


def _half(ref, kind, c):
    R, C = ref.shape
    if kind == "col":
        return ref.at[pl.ds(c * (R // 2), R // 2), :]
    return ref.at[:, pl.ds(c * (C // 2), C // 2)]


def _shard(ref, kind, s):
    R, C = ref.shape
    if kind == "col":
        return ref.at[:, pl.ds(s * (C // 4), C // 4)]
    return ref.at[pl.ds(s * (R // 4), R // 4), :]


def _piece(ref, kind, s, c):
    R, C = ref.shape
    if kind == "col":
        return ref.at[pl.ds(c * (R // 2), R // 2), pl.ds(s * (C // 4), C // 4)]
    return ref.at[pl.ds(s * (R // 4), R // 4), pl.ds(c * (C // 2), C // 2)]


def _remote(src, dst, sems_s, sems_r, k, dev):
    return pltpu.make_async_remote_copy(src_ref=src, dst_ref=dst, send_sem=sems_s.at[k], recv_sem=sems_r.at[k],
                                        device_id=dev, device_id_type=MESH)


HBM = pl.BlockSpec(memory_space=pltpu.HBM)
SEM = pl.BlockSpec(memory_space=pltpu.SEMAPHORE)
EFFECT = pltpu.SideEffectType.DATAFLOW_SIDE_EFFECTING


def _hbm(a):
    return pltpu.with_memory_space_constraint(a, pltpu.HBM)


def _start_copies(name, arrays, ncopies, make):
    n = len(arrays)

    def body(*refs):
        for cp in make(refs[:n], refs[n], refs[n + 1]):
            cp.start()

    outs = pl.pallas_call(
        body, name=name,
        out_shape=(pltpu.SemaphoreType.DMA((ncopies,)), pltpu.SemaphoreType.DMA((ncopies,)),
                   *[jax.ShapeDtypeStruct(t.shape, t.dtype) for t in arrays]),
        in_specs=[HBM] * n, out_specs=(SEM, SEM, *([HBM] * n)),
        input_output_aliases={i: i + 2 for i in range(n)},
        compiler_params=pltpu.CompilerParams(has_side_effects=EFFECT),
    )(*[_hbm(t) for t in arrays])
    return outs[0], outs[1], list(outs[2:])


def _wait_copies(name, send, recv, arrays, make, after):
    n = len(arrays)

    def body(*refs):
        for cp in make(refs[:n], refs[n], refs[n + 1]):
            cp.wait_send()
            cp.wait_recv()

    outs = pl.pallas_call(
        body, name=name, out_shape=[jax.ShapeDtypeStruct(t.shape, t.dtype) for t in arrays],
        in_specs=[HBM] * n + [SEM, SEM, ANY], out_specs=[HBM] * n,
        input_output_aliases={i: i for i in range(n)},
        compiler_params=pltpu.CompilerParams(has_side_effects=EFFECT),
    )(*arrays, send, recv, after)
    return list(outs)


def _w_in_copies(relations):
    def make(refs, send, recv):
        x, y, c = _coords()
        me = 2 * x + y
        chips = _other_chips(x, y)
        w, conv = refs[0], refs[1]
        cps = []
        for i, k in enumerate(relations):
            cps.append(_remote(_column_half(w, 0, c), _column_half(w, 1 + k, c), send, recv, 2 * i, (*chips[k], c)))
            mine = _shard(conv, "col", me)
            cps.append(_remote(mine, mine, send, recv, 2 * i + 1, (*chips[k], c)))
        return cps
    return make


def _column_half(w, slot, c):
    half = w.shape[1] // 8
    return w.at[:, pl.ds((2 * slot + c) * half, half)]


def _w_in_forward(relations):
    def make(refs, send, recv):
        x, y, c = _coords()
        cps = []
        for i, k in enumerate(relations):
            got = _column_half(refs[0], 1 + k, c)
            cps.append(_remote(got, got, send, recv, i, (x, y, 1 - c)))
        return cps
    return make


def _sibling_copy(refs, send, recv):
    x, y, c = _coords()
    return [_remote(refs[0], refs[1], send, recv, 0, (x, y, 1 - c))]


def _other_weight_copies(refs, send, recv):
    x, y, c = _coords()
    me = 2 * x + y
    cps = []
    for k, chip in enumerate(_other_chips(x, y)):
        for p, (_, kind) in enumerate(BIG[1:]):
            mine = _piece(refs[p], kind, me, c)
            cps.append(_remote(mine, mine, send, recv, 3 * p + k, (*chip, c)))
    return cps


def _sibling_forward(name, arrays, ncp, halves):
    n = len(arrays)

    def body(*refs):
        outs = refs[n:2 * n]
        send, recv = refs[2 * n:]
        x, y, c = _coords()
        sib = (x, y, 1 - c)
        cps = [_remote(got, got, send, recv, i, sib) for i, got in enumerate(halves(outs, c))]
        for cp in cps:
            cp.start()
        for cp in cps:
            cp.wait_send()
        for i, got in enumerate(halves(outs, 1 - c)):
            _remote(got, got, send, recv, i, sib).wait_recv()

    return pl.pallas_call(
        body, name=name, out_shape=[jax.ShapeDtypeStruct(t.shape, t.dtype) for t in arrays],
        in_specs=[ANY] * n, out_specs=[ANY] * n, input_output_aliases={i: i for i in range(n)},
        scratch_shapes=[pltpu.SemaphoreType.DMA((ncp,)), pltpu.SemaphoreType.DMA((ncp,))],
    )(*arrays)


def _landed_halves(refs, c):
    return [_half(r, "col", c) for r in refs]


def _other_weight_halves(refs, c):
    x, y, _ = _coords()
    out = []
    for chip in _other_chips(x, y):
        s = 2 * chip[0] + chip[1]
        out += [_piece(refs[p], kind, s, c) for p, (_, kind) in enumerate(BIG[1:])]
    return out


def _sibling_exchange(grads, group, name):
    n = len(group)
    shapes = []
    for (_, kind), g in zip(group, grads):
        R, C = g.shape
        shapes.append(jax.ShapeDtypeStruct((R // 2, C) if kind == "col" else (R, C // 2), g.dtype))

    def body(*refs):
        ins, outs = refs[:n], refs[n:2 * n]
        send, recv = refs[2 * n:]
        x, y, c = _coords()
        sib = (x, y, 1 - c)
        cps = [_remote(_half(ins[p], group[p][1], 1 - c), outs[p], send, recv, p, sib) for p in range(n)]
        for cp in cps:
            cp.start()
        for cp in cps:
            cp.wait()

    return pl.pallas_call(
        body, name=name, out_shape=shapes, in_specs=[ANY] * n, out_specs=[ANY] * n,
        scratch_shapes=[pltpu.SemaphoreType.DMA((n,)), pltpu.SemaphoreType.DMA((n,))],
    )(*grads)


def _presum(g, got, kind, pos, name):
    R, C = got.shape
    tr, tc = _tile(R, 512, 16), _tile(C, 2048)
    nr, nc = R // tr, C // tc

    def body(pos_ref, a_ref, b_ref, o_ref):
        o_ref[...] = (a_ref[...].astype(F32) + b_ref[...].astype(F32)).astype(o_ref.dtype)

    blk = pl.BlockSpec((tr, tc), lambda i, j, pos_ref: (i, j))
    if g.shape == got.shape:
        mine = blk
    elif kind == "col":
        mine = pl.BlockSpec((tr, tc), lambda i, j, pos_ref: (pos_ref[1] * nr + i, j))
    else:
        mine = pl.BlockSpec((tr, tc), lambda i, j, pos_ref: (i, pos_ref[1] * nc + j))
    spec = pltpu.PrefetchScalarGridSpec(num_scalar_prefetch=1, grid=(nr, nc), in_specs=[mine, blk], out_specs=blk)
    return _call(body, name=name, grid_spec=spec, out_shape=jax.ShapeDtypeStruct((R, C), WIRE_DTYPE))(pos, g, got)


def _chip_copies(group):
    n = len(group)

    def make(refs, send, recv):
        x, y, c = _coords()
        cps = []
        for k, chip in enumerate(_other_chips(x, y)):
            s = 2 * chip[0] + chip[1]
            for p in range(n):
                cps.append(_remote(_shard(refs[p], group[p][1], s), refs[n + p].at[k], send, recv, 3 * p + k, (*chip, c)))
        return cps
    return make


def _landing_zones(pres, group):
    lands = []
    for (_, kind), g in zip(group, pres):
        R, C = g.shape
        lands.append(lax.empty((3, R, C // 4) if kind == "col" else (3, R // 4, C), g.dtype))
    return lands


def _exchange_start(G, group, pos, carry, tag, pres=None):
    n = len(group)
    if pres is None:
        parts = [G[name] for name, _ in group]
        got = _sibling_exchange(parts, group, "sibling_exchange_" + tag)
        pres = [_presum(parts[p], got[p], kind, pos, "presum_" + name) for p, (name, kind) in enumerate(group)]
    make = _chip_copies(group)
    send, recv, thru = _start_copies("chip_exchange_start_" + tag, [*pres, *_landing_zones(pres, group), carry], 3 * n, make)
    return (send, recv, thru[:2 * n], make, tag), thru[2 * n]


def _exchange_wait(state, after):
    send, recv, arrays, make, tag = state
    thru = _wait_copies("chip_exchange_wait_" + tag, send, recv, arrays, make, after)
    n = len(thru) // 2
    return thru[:n], thru[n:]


def _reduce_into_shard(slots, pre, kind, pos, name):
    K, R, C = slots.shape
    tr, tc = _tile(R, 512, 16), _tile(C, 2176)
    nr, nc = R // tr, C // tc

    def body(pos_ref, s_ref, p_ref, o_ref):
        acc = p_ref[...].astype(F32)
        for k in range(K):
            acc = acc + s_ref[k].astype(F32)
        o_ref[...] = acc

    if kind == "col":
        own = pl.BlockSpec((tr, tc), lambda i, j, pos_ref: (i, pos_ref[0] * nc + j))
        full, out = (2 * R, C), pl.BlockSpec((tr, tc), lambda i, j, pos_ref: (pos_ref[1] * nr + i, j))
    else:
        own = pl.BlockSpec((tr, tc), lambda i, j, pos_ref: (pos_ref[0] * nr + i, j))
        full, out = (R, 2 * C), pl.BlockSpec((tr, tc), lambda i, j, pos_ref: (i, pos_ref[1] * nc + j))
    spec = pltpu.PrefetchScalarGridSpec(
        num_scalar_prefetch=1, grid=(nr, nc),
        in_specs=[pl.BlockSpec((K, tr, tc), lambda i, j, pos_ref: (0, i, j)), own], out_specs=out)
    return _call(body, name=name, grid_spec=spec, out_shape=jax.ShapeDtypeStruct(full, F32))(pos, slots, pre)


def _share_reduced(reds):
    n = len(BIG)

    def body(*refs):
        outs = refs[n:2 * n]
        send, recv = refs[2 * n:]
        x, y, c = _coords()
        sib = (x, y, 1 - c)
        cps = []
        for p in range(n):
            mine = _half(outs[p], BIG[p][1], c)
            cps.append(_remote(mine, mine, send, recv, p, sib))
        for cp in cps:
            cp.start()
        for cp in cps:
            cp.wait_send()
        for p in range(n):
            got = _half(outs[p], BIG[p][1], 1 - c)
            _remote(got, got, send, recv, p, sib).wait_recv()

    return pl.pallas_call(
        body, name="share_reduced", out_shape=[jax.ShapeDtypeStruct(r.shape, r.dtype) for r in reds],
        in_specs=[ANY] * n, out_specs=[ANY] * n, input_output_aliases={p: p for p in range(n)},
        scratch_shapes=[pltpu.SemaphoreType.DMA((n,)), pltpu.SemaphoreType.DMA((n,))],
    )(*reds)


def _gather_small(pack):
    _, N = pack.shape

    def body(in_ref, out_ref, send, recv, loc):
        x, y, c = _coords()
        me = 4 * x + 2 * y + c
        own = pltpu.make_async_copy(in_ref, out_ref.at[me], loc)
        own.start()
        cps = []
        for k in range(1, 8):
            dev = (x ^ (k >> 2), y ^ ((k >> 1) & 1), c ^ (k & 1))
            cps.append(_remote(in_ref, out_ref.at[me], send, recv, k - 1, dev))
        for cp in cps:
            cp.start()
        for k in range(1, 8):
            src = 4 * (x ^ (k >> 2)) + 2 * (y ^ ((k >> 1) & 1)) + (c ^ (k & 1))
            _remote(in_ref, out_ref.at[src], send, recv, k - 1, (x, y, c)).wait_recv()
        for cp in cps:
            cp.wait_send()
        own.wait()

    return pl.pallas_call(
        body, name="gather_small", out_shape=jax.ShapeDtypeStruct((8, 1, N), pack.dtype),
        in_specs=[ANY], out_specs=ANY,
        scratch_shapes=[pltpu.SemaphoreType.DMA((7,)), pltpu.SemaphoreType.DMA((7,)), pltpu.SemaphoreType.DMA(())],
    )(pack)


def _sum_small(slots):
    K, _, N = slots.shape

    def body(s_ref, o_ref):
        acc = s_ref[0]
        for k in range(1, K):
            acc = acc + s_ref[k]
        o_ref[...] = acc

    return _call(body, name="sum_small", in_specs=[pl.BlockSpec(memory_space=pltpu.VMEM)],
                 out_specs=pl.BlockSpec(memory_space=pltpu.VMEM), out_shape=jax.ShapeDtypeStruct((1, N), F32))(slots)


def _adamw(w, g, m, v, name):
    R, C = w.shape
    tr, tc = _tile(R, 256, 8), _tile(C, 2176)

    def body(w_ref, g_ref, m_ref, v_ref, d_ref, nm_ref, nv_ref):
        gv = g_ref[...]
        nm = ADAM_B1 * m_ref[...] + (1.0 - ADAM_B1) * gv
        nv = ADAM_B2 * v_ref[...] + (1.0 - ADAM_B2) * gv * gv
        m_hat = nm / (1.0 - ADAM_B1 ** ADAM_STEP)
        v_hat = nv / (1.0 - ADAM_B2 ** ADAM_STEP)
        d_ref[...] = -ADAM_LR * (m_hat / (jnp.sqrt(v_hat) + ADAM_EPS) + ADAM_WD * w_ref[...])
        nm_ref[...] = nm
        nv_ref[...] = nv

    spec = pl.BlockSpec((tr, tc), lambda i, j: (i, j))
    shp = jax.ShapeDtypeStruct((R, C), F32)
    return _call(body, name=name, grid=(R // tr, C // tc), in_specs=[spec] * 4, out_specs=[spec] * 3,
                 out_shape=[shp] * 3)(w, g, m, v)


SMALL = ("norm_g", "mem_norm_g", "attn_q_norm", "attn_k_norm", "conv_w", "mem_q_norm", "mem_k_norm")
WEIGHTS = ("norm_g", "mem_norm_g", "w_in", "attn_q_norm", "attn_k_norm", "conv_w", "mem_w_kv", "mem_q_norm",
           "mem_k_norm", "w_br_attn", "w_br_conv", "w_br_mem", "w_out")


def kernel(x, mem, norm_g, mem_norm_g, w_in, attn_q_norm, attn_k_norm, conv_w, mem_w_kv, mem_q_norm, mem_k_norm, w_br_attn, w_br_conv, w_br_mem, w_out, loss_target, m_norm_g, m_mem_norm_g, m_w_in, m_attn_q_norm, m_attn_k_norm, m_conv_w, m_mem_w_kv, m_mem_q_norm, m_mem_k_norm, m_w_br_attn, m_w_br_conv, m_w_br_mem, m_w_out, v_norm_g, v_mem_norm_g, v_w_in, v_attn_q_norm, v_attn_k_norm, v_conv_w, v_mem_w_kv, v_mem_q_norm, v_mem_k_norm, v_w_br_attn, v_w_br_conv, v_w_br_mem, v_w_out):
    w = dict(norm_g=norm_g, mem_norm_g=mem_norm_g, w_in=w_in, attn_q_norm=attn_q_norm, attn_k_norm=attn_k_norm,
             conv_w=conv_w, mem_w_kv=mem_w_kv, mem_q_norm=mem_q_norm, mem_k_norm=mem_k_norm, w_br_attn=w_br_attn,
             w_br_conv=w_br_conv, w_br_mem=w_br_mem, w_out=w_out)
    m = dict(norm_g=m_norm_g, mem_norm_g=m_mem_norm_g, w_in=m_w_in, attn_q_norm=m_attn_q_norm,
             attn_k_norm=m_attn_k_norm, conv_w=m_conv_w, mem_w_kv=m_mem_w_kv, mem_q_norm=m_mem_q_norm,
             mem_k_norm=m_mem_k_norm, w_br_attn=m_w_br_attn, w_br_conv=m_w_br_conv, w_br_mem=m_w_br_mem, w_out=m_w_out)
    v = dict(norm_g=v_norm_g, mem_norm_g=v_mem_norm_g, w_in=v_w_in, attn_q_norm=v_attn_q_norm,
             attn_k_norm=v_attn_k_norm, conv_w=v_conv_w, mem_w_kv=v_mem_w_kv, mem_q_norm=v_mem_q_norm,
             mem_k_norm=v_mem_k_norm, w_br_attn=v_w_br_attn, w_br_conv=v_w_br_conv, w_br_mem=v_w_br_mem, w_out=v_w_out)
    Bl, _, D = x.shape
    cx, cy = lax.axis_index("x"), lax.axis_index("y")
    chip = 2 * cx + cy
    pos = jnp.stack([chip, lax.axis_index("c")]).astype(jnp.int32)
    order = jnp.stack([chip] + [2 * a + b for a, b in _other_chips(cx, cy)]).astype(jnp.int32)
    n = len(BIG)

    w_rel = _place_shard(w["w_in"], "col", jnp.zeros((1,), jnp.int32), WIRE_DTYPE, "place_w_in")
    conv_full = _place_shard(conv_w, "col", pos, F32, "place_conv_w")
    others = [_place_shard(w[name], kind, pos, WIRE_DTYPE, "place_" + name) for name, kind in BIG[1:]]
    hb, hbt, mhb = _norms(x, mem, norm_g, mem_norm_g)

    meta = jnp.concatenate([order, pos[1:]])
    near, near_fwd = _w_in_copies((0, 1)), _w_in_forward((0, 1))
    send, recv, (w_rel, conv_full) = _start_copies("gather_near_start", [w_rel, conv_full], 4, near)
    proj = _proj_chunk(hb, w_rel, meta, 0, None, None, "proj_own")
    w_rel, conv_full = _wait_copies("gather_near_wait", send, recv, [w_rel, conv_full], near, proj)

    fsend, frecv, (w_rel,) = _start_copies("gather_near_forward_start", [w_rel], 2, near_fwd)
    far, far_fwd = _w_in_copies((2,)), _w_in_forward((2,))
    send, recv, (w_rel, conv_full) = _start_copies("gather_far_start", [w_rel, conv_full], 2, far)
    proj = _proj_chunk(hb, w_rel, meta, 1, 0, proj, "proj_near_x_landed")
    proj = _proj_chunk(hb, w_rel, meta, 2, 0, proj, "proj_near_y_landed")
    w_rel, = _wait_copies("gather_near_forward_wait", fsend, frecv, [w_rel], near_fwd, proj)
    proj = _proj_chunk(hb, w_rel, meta, 1, 1, proj, "proj_near_x_forwarded")
    proj = _proj_chunk(hb, w_rel, meta, 2, 1, proj, "proj_near_y_forwarded")
    w_rel, conv_full = _wait_copies("gather_far_wait", send, recv, [w_rel, conv_full], far, proj)

    fsend, frecv, (w_rel,) = _start_copies("gather_far_forward_start", [w_rel], 1, far_fwd)
    send, recv, (*others, w_rel) = _start_copies("gather_rest_start", [*others, w_rel], 3 * (n - 1), _other_weight_copies)
    proj = _proj_chunk(hb, w_rel, meta, 3, 0, proj, "proj_far_landed")
    w_rel, = _wait_copies("gather_far_forward_wait", fsend, frecv, [w_rel], far_fwd, proj)
    proj = _proj_chunk(hb, w_rel, meta, 3, 1, proj, "proj_far_forwarded")
    mixed = _mix_fwd(proj, Bl, attn_q_norm, attn_k_norm, conv_full)
    *others, w_rel = _wait_copies("gather_rest_wait", send, recv, [*others, w_rel], _other_weight_copies, mixed[2])
    others = _sibling_forward("gather_rest_forward", others, 3 * (n - 1), _other_weight_halves)
    W = {name: others[p] for p, (name, _) in enumerate(BIG[1:])}

    G, rest, rest_state = _weight_grads(
        x, mem, loss_target, norm_g, mem_norm_g, attn_q_norm, attn_k_norm, conv_full, mem_q_norm, mem_k_norm, W,
        (hb, hbt, mhb, proj, *mixed), early=lambda G, carry: _exchange_start(G, BIG[1:], pos, carry, "rest"))

    for_sibling = _dw_in_half(hbt, rest[0], pos, False, "dw_in_sibling")
    send, recv, (for_sibling, got, dproj) = _start_copies(
        "sibling_w_in_start", [for_sibling, lax.empty(for_sibling.shape, for_sibling.dtype), rest[0]], 1, _sibling_copy)
    mine = _dw_in_half(hbt, dproj, pos, True, "dw_in_own")
    for_sibling, got = _wait_copies("sibling_w_in_wait", send, recv, [for_sibling, got], _sibling_copy, mine)
    pre_w_in = _presum(mine, got, "col", pos, "presum_w_in")

    w_in_state, dproj = _exchange_start(G, BIG[:1], pos, dproj, "w_in", pres=[pre_w_in])
    grad_x, small = _input_grad((dproj, *rest[1:]), w_rel, order)
    pres, slots = _exchange_wait(w_in_state, grad_x)
    pres_rest, slots_rest = _exchange_wait(rest_state, grad_x)
    pres, slots = pres + pres_rest, slots + slots_rest
    grad_x = grad_x.reshape(x.shape)
    reds = [_reduce_into_shard(slots[p], pres[p], kind, pos, "reduce_" + name) for p, (name, kind) in enumerate(BIG)]
    grads = dict(zip([name for name, _ in BIG], _share_reduced(reds)))

    tot = _sum_small(_gather_small(small))[0]
    loss = tot[0]
    off = 128
    for name, size in (("norm_g", D), ("mem_norm_g", D), ("attn_q_norm", NGROUP * HEAD), ("attn_k_norm", NGROUP * HEAD),
                       ("conv_w", 3 * CONVW), ("mem_q_norm", MEM_HD), ("mem_k_norm", MEM_HD)):
        grads[name] = tot[off:off + size]
        off += size
    cw = conv_w.shape[1]
    grads["conv_w"] = lax.dynamic_slice(grads["conv_w"].reshape(3, CONVW), (0, chip * cw), (3, cw))
    for name in SMALL:
        grads[name] = grads[name].reshape(w[name].shape)

    delta, new_m, new_v = {}, {}, {}
    for name, _ in BIG:
        delta[name], new_m[name], new_v[name] = _adamw(w[name], grads[name], m[name], v[name], "adamw_" + name)

    def packed(t):
        return jnp.concatenate([t[name].reshape(1, -1) for name in SMALL], axis=1)

    ds, ms, vs = _adamw(packed(w), packed(grads), packed(m), packed(v), "adamw_small")
    off = 0
    for name in SMALL:
        size = w[name].size
        delta[name] = ds[0, off:off + size].reshape(w[name].shape)
        new_m[name] = ms[0, off:off + size].reshape(w[name].shape)
        new_v[name] = vs[0, off:off + size].reshape(w[name].shape)
        off += size

    return (loss, grad_x, *[grads[n] for n in WEIGHTS], *[delta[n] for n in WEIGHTS],
            *[new_m[n] for n in WEIGHTS], *[new_v[n] for n in WEIGHTS])
```

```python
import functools

import jax
import jax.numpy as jnp
from jax import lax
from jax.experimental import pallas as pl
from jax.experimental.pallas import tpu as pltpu

F32 = jnp.float32
MXU_DTYPE = jnp.bfloat16
WIRE_DTYPE = jnp.bfloat16
EPS = 1e-6
NEG = -1e30

HEAD = 128
HPG = 4
GW = HPG * HEAD
DILATIONS = (1, 4, 16)
NGROUP = len(DILATIONS)
BLK = 128
QKV = NGROUP * GW
CONVW = 1024
MEM_HEADS = 4
MEM_HD = 256
MEMW = MEM_HEADS * MEM_HD
Q0, K0, V0 = 0, QKV, 2 * QKV
ZA = 3 * QKV
CB, CC, CV, ZC = ZA + GW, ZA + GW + CONVW, ZA + GW + 2 * CONVW, ZA + GW + 3 * CONVW
MQ = ZC + CONVW
ZM = MQ + MEMW
G0 = ZM + MEMW

ADAM_LR, ADAM_B1, ADAM_B2, ADAM_EPS, ADAM_WD, ADAM_STEP = 0.001, 0.9, 0.999, 1e-08, 0.01, 10

VMEM_LIMIT = 56 * 1024 * 1024
MESH = pl.DeviceIdType.MESH
ANY = pl.BlockSpec(memory_space=pl.ANY)


def _tile(n, pref, mult=128):
    t = min(pref, n)
    while t > mult and (n % t or t % mult):
        t -= mult
    assert n % t == 0, (n, pref)
    return t


def _call(body, *, name, out_shape, grid=(), in_specs=None, out_specs=None, scratch_shapes=(),
          aliases=None, grid_spec=None):
    kw = {}
    if grid_spec is not None:
        kw["grid_spec"] = grid_spec
        ngrid = len(grid_spec.grid)
    else:
        kw.update(grid=grid, in_specs=in_specs, out_specs=out_specs, scratch_shapes=list(scratch_shapes))
        ngrid = len(grid)
    params = pltpu.CompilerParams(dimension_semantics=("arbitrary",) * ngrid, vmem_limit_bytes=VMEM_LIMIT)
    return pl.pallas_call(body, name=name, out_shape=out_shape, compiler_params=params,
                          input_output_aliases=aliases or {}, **kw)


_DIMS = {"nn": (((1,), (0,)), ((), ())), "nt": (((1,), (1,)), ((), ())), "tn": (((0,), (0,)), ((), ()))}


def _mxu(a, b, mode):
    return lax.dot_general(a.astype(MXU_DTYPE), b.astype(MXU_DTYPE), _DIMS[mode], preferred_element_type=F32)


@functools.partial(jax.custom_vjp, nondiff_argnums=(2,))
def _dot(a, b, mode):
    return _mxu(a, b, mode)


def _dot_fwd(a, b, mode):
    return _mxu(a, b, mode), (a, b)


def _dot_bwd(mode, res, g):
    a, b = res
    if mode == "nn":
        return _mxu(g, b, "nt"), _mxu(a, g, "tn")
    if mode == "nt":
        return _mxu(g, b, "nn"), _mxu(g, a, "tn")
    return _mxu(b, g, "nt"), _mxu(a, g, "nn")


_dot.defvjp(_dot_fwd, _dot_bwd)


def _sig(z):
    return 1.0 / (1.0 + jnp.exp(-z))


def _silu(z):
    return z * _sig(z)


def _rms_rows(t, g):
    return t * lax.rsqrt(jnp.mean(t * t, axis=-1, keepdims=True) + EPS) * g


def _attn_block(q, k2, v2, gq, gk, first):
    qn = _rms_rows(q, gq)
    kn = _rms_rows(k2, gk)
    s = jnp.where(_band_mask(first), _dot(qn, kn, "nt") * (HEAD ** -0.5), NEG)
    m = lax.stop_gradient(jnp.max(s, axis=-1, keepdims=True))
    p = jnp.exp(s - m)
    den = jnp.sum(p, axis=-1, keepdims=True)
    o = _dot(p, v2, "nn") / den
    return o, m + jnp.log(den)


def _band_mask(first):
    a = lax.broadcasted_iota(jnp.int32, (BLK, 2 * BLK), 0)
    b = lax.broadcasted_iota(jnp.int32, (BLK, 2 * BLK), 1)
    return (b >= a) & (b <= a + BLK) & (b >= jnp.where(first, BLK, 0))


def _norm_parts(t):
    r = lax.rsqrt(jnp.mean(t * t, axis=-1, keepdims=True) + EPS)
    return r, t * r


def _norm_bwd(dn, g, r, th):
    dth = dn * g
    return r * (dth - th * jnp.mean(dth * th, axis=-1, keepdims=True)), jnp.sum(dn * th, axis=0, keepdims=True)


def _attn_block_bwd(q, k2, v2, gq, gk, first, do, o, lse, dlse):
    scale = HEAD ** -0.5
    rq, qh = _norm_parts(q)
    rk, kh = _norm_parts(k2)
    qn, kn = qh * gq, kh * gk
    s = jnp.where(_band_mask(first), _mxu(qn, kn, "nt") * scale, NEG)
    p = jnp.exp(s - lse)
    ds = p * (_mxu(do, v2, "nt") + (dlse - jnp.sum(do * o, axis=-1, keepdims=True))) * scale
    dq, dgq = _norm_bwd(_mxu(ds, kn, "nn"), gq, rq, qh)
    dk2, dgk = _norm_bwd(_mxu(ds, qn, "tn"), gk, rk, kh)
    return dq, dk2, _mxu(p, do, "tn"), dgq, dgk


def _combine(o1, o2, o3, l1, l2, l3, z):
    m = lax.stop_gradient(jnp.maximum(jnp.maximum(l1, l2), l3))
    e1, e2, e3 = jnp.exp(l1 - m), jnp.exp(l2 - m), jnp.exp(l3 - m)
    return (e1 * o1 + e2 * o2 + e3 * o3) / (e1 + e2 + e3) * _silu(z)


def _mem_block(q, z, kv, gq, gk):
    outs = []
    for h in range(MEM_HEADS):
        sl = slice(h * MEM_HD, (h + 1) * MEM_HD)
        qn = _rms_rows(q[:, sl], gq)
        kn = _rms_rows(kv[:, sl], gk)
        s = _dot(qn, kn, "nt") * (MEM_HD ** -0.5)
        m = lax.stop_gradient(jnp.max(s, axis=-1, keepdims=True))
        p = jnp.exp(s - m)
        den = jnp.sum(p, axis=-1, keepdims=True)
        outs.append(_dot(p, kv[:, MEMW + h * MEM_HD:MEMW + (h + 1) * MEM_HD], "nn") / den)
    return jnp.concatenate(outs, axis=-1) * _silu(z)


def _cast(w, name):
    R, C = w.shape
    tr, tc = _tile(R, 512, 8), _tile(C, 2176)

    def body(w_ref, o_ref):
        o_ref[...] = w_ref[...].astype(o_ref.dtype)

    spec = pl.BlockSpec((tr, tc), lambda i, j: (i, j))
    return _call(body, name=name, grid=(R // tr, C // tc), in_specs=[spec], out_specs=spec,
                 out_shape=jax.ShapeDtypeStruct((R, C), WIRE_DTYPE))(w)


def _place_shard(w, kind, pos, dtype, name, slot=0, into=None):
    R, C = w.shape
    tr, tc = _tile(R, 512, 8), _tile(C, 2176)
    nr, nc = R // tr, C // tc

    def body(pos_ref, w_ref, *rest):
        rest[-1][...] = w_ref[...].astype(rest[-1].dtype)

    if kind == "col":
        full, out = (R, 4 * C), pl.BlockSpec((tr, tc), lambda i, j, pos_ref: (i, pos_ref[slot] * nc + j))
    else:
        full, out = (4 * R, C), pl.BlockSpec((tr, tc), lambda i, j, pos_ref: (pos_ref[slot] * nr + i, j))
    in_specs, args = [pl.BlockSpec((tr, tc), lambda i, j, pos_ref: (i, j))], [pos, w]
    if into is not None:
        in_specs.append(ANY)
        args.append(into)
    spec = pltpu.PrefetchScalarGridSpec(num_scalar_prefetch=1, grid=(nr, nc), in_specs=in_specs, out_specs=out)
    return _call(body, name=name, grid_spec=spec, out_shape=jax.ShapeDtypeStruct(full, dtype),
                 aliases={} if into is None else {2: 0})(*args)


def _matmul(a, b, mode, out_dtype, *, name, tm=512, tn=512, tk=512):
    if mode == "nn":
        (M, K), (_, N) = a.shape, b.shape
    elif mode == "nt":
        (M, K), (N, _) = a.shape, b.shape
    else:
        (K, M), (_, N) = a.shape, b.shape
    tm, tn, tk = _tile(M, tm), _tile(N, tn), _tile(K, tk)
    nk = K // tk

    def body(a_ref, b_ref, o_ref, *acc):
        part = lax.dot_general(a_ref[...], b_ref[...], _DIMS[mode], preferred_element_type=F32)
        if nk == 1:
            o_ref[...] = part.astype(o_ref.dtype)
            return
        acc_ref, = acc
        k = pl.program_id(2)

        @pl.when(k == 0)
        def _():
            acc_ref[...] = part

        @pl.when(k > 0)
        def _():
            acc_ref[...] += part

        @pl.when(k == nk - 1)
        def _():
            o_ref[...] = acc_ref[...].astype(o_ref.dtype)

    a_spec = pl.BlockSpec((tk, tm), lambda i, j, k: (k, i)) if mode == "tn" else pl.BlockSpec((tm, tk), lambda i, j, k: (i, k))
    b_spec = pl.BlockSpec((tn, tk), lambda i, j, k: (j, k)) if mode == "nt" else pl.BlockSpec((tk, tn), lambda i, j, k: (k, j))
    return _call(body, name=name, grid=(M // tm, N // tn, nk), in_specs=[a_spec, b_spec],
                 out_specs=pl.BlockSpec((tm, tn), lambda i, j, k: (i, j)),
                 out_shape=jax.ShapeDtypeStruct((M, N), out_dtype),
                 scratch_shapes=[] if nk == 1 else [pltpu.VMEM((tm, tn), F32)])(a, b)


def _rms_fwd(x, g, name):
    R, D = x.shape
    tr = _tile(R, 512)

    def body(x_ref, g_ref, o_ref, t_ref):
        y = _rms_rows(x_ref[...], g_ref[...])
        o_ref[...] = y.astype(o_ref.dtype)
        t_ref[...] = y.T.astype(t_ref.dtype)

    row = pl.BlockSpec((tr, D), lambda i: (i, 0))
    return _call(body, name=name, grid=(R // tr,), in_specs=[row, pl.BlockSpec((1, D), lambda i: (0, 0))],
                 out_specs=[row, pl.BlockSpec((D, tr), lambda i: (0, i))],
                 out_shape=[jax.ShapeDtypeStruct((R, D), MXU_DTYPE), jax.ShapeDtypeStruct((D, R), MXU_DTYPE)])(x, g)


def _rms_bwd(x, dh, g, dy, name):
    R, D = x.shape
    tr = _tile(R, 256)
    with_dx = dy is not None

    def body(*refs):
        if with_dx:
            x_ref, dh_ref, g_ref, dy_ref, dx_ref, dg_ref = refs
        else:
            x_ref, dh_ref, g_ref, dg_ref = refs
        xv, dhv = x_ref[...], dh_ref[...]
        r = lax.rsqrt(jnp.mean(xv * xv, axis=-1, keepdims=True) + EPS)
        xh = xv * r

        @pl.when(pl.program_id(0) == 0)
        def _():
            dg_ref[...] = jnp.zeros_like(dg_ref)

        dg_ref[...] += jnp.sum(dhv * xh, axis=0, keepdims=True)
        if with_dx:
            dxh = dhv * g_ref[...]
            dx_ref[...] = dy_ref[...] + r * (dxh - xh * jnp.mean(dxh * xh, axis=-1, keepdims=True))

    row = pl.BlockSpec((tr, D), lambda i: (i, 0))
    vec = pl.BlockSpec((1, D), lambda i: (0, 0))
    dg_shape = jax.ShapeDtypeStruct((1, D), F32)
    if with_dx:
        return _call(body, name=name, grid=(R // tr,), in_specs=[row, row, vec, row], out_specs=[row, vec],
                     out_shape=[jax.ShapeDtypeStruct((R, D), F32), dg_shape])(x, dh, g, dy)
    return None, _call(body, name=name, grid=(R // tr,), in_specs=[row, row, vec], out_specs=vec,
                       out_shape=dg_shape)(x, dh, g)


def _attn_geom(g, d):
    hc = HPG if d == 1 else 1
    cw = hc * HEAD
    cq, ck, cv = (Q0 + g * GW) // cw, (K0 + g * GW) // cw, (V0 + g * GW) // cw
    return (1, BLK * d, cw), hc, HPG // hc, cq, ck, cv


def _rows(ref, r, d, sl):
    if d == 1:
        return ref[0, :, sl]
    return ref.at[0][pl.ds(r, BLK, stride=d), sl]


def _set_rows(ref, r, d, sl, val):
    if d == 1:
        ref[0, :, sl] = val
    else:
        ref.at[0][pl.ds(r, BLK, stride=d), sl] = val


def _stage_rows(ref, r, d, sl, val):
    if d == 1:
        ref[:, sl] = val
    else:
        ref[pl.ds(r, BLK, stride=d), sl] = val


def _attn_fwd(proj3, gq, gk, g, d):
    Bl, S, _ = proj3.shape
    blk, hc, ncb, cq, ck, cv = _attn_geom(g, d)
    nb = S // blk[1]

    def body(q_ref, kp_ref, kc_ref, vp_ref, vc_ref, gq_ref, gk_ref, o_ref, lse_ref):
        first = pl.program_id(2) == 0
        for r in range(d):
            for h in range(hc):
                sl = slice(h * HEAD, (h + 1) * HEAD)
                k2 = jnp.concatenate([_rows(kp_ref, r, d, sl), _rows(kc_ref, r, d, sl)], axis=0)
                v2 = jnp.concatenate([_rows(vp_ref, r, d, sl), _rows(vc_ref, r, d, sl)], axis=0)
                o, lse = _attn_block(_rows(q_ref, r, d, sl), k2, v2, gq_ref[...], gk_ref[...], first)
                _set_rows(o_ref, r, d, sl, o)
                _set_rows(lse_ref, r, d, sl, jnp.broadcast_to(lse, (BLK, HEAD)))

    def cur(c0):
        return pl.BlockSpec(blk, lambda b, j, i: (b, i, c0 + j))

    def prev(c0):
        return pl.BlockSpec(blk, lambda b, j, i: (b, jnp.maximum(i - 1, 0), c0 + j))

    vec = pl.BlockSpec((1, HEAD), lambda b, j, i: (0, 0))
    out = pl.BlockSpec(blk, lambda b, j, i: (b, i, j))
    shp = jax.ShapeDtypeStruct((Bl, S, GW), F32)
    return _call(body, name=f"attn_fwd_g{g}", grid=(Bl, ncb, nb),
                 in_specs=[cur(cq), prev(ck), cur(ck), prev(cv), cur(cv), vec, vec],
                 out_specs=[out, out], out_shape=[shp, shp])(proj3, proj3, proj3, proj3, proj3, gq, gk)


def _attn_bwd(proj3, gq, gk, o3, l3, do3, dl3, g, d):
    Bl, S, _ = proj3.shape
    blk, hc, ncb, cq, ck, cv = _attn_geom(g, d)
    nb = S // blk[1]

    def body(q_ref, kp_ref, kc_ref, vp_ref, vc_ref, gq_ref, gk_ref, o_ref, l_ref, do_ref, dl_ref,
             dq_ref, dk_ref, dv_ref, dgq_ref, dgk_ref, ck_ref, cv_ref, sq_ref, sk_ref, sv_ref):
        i = pl.program_id(2)
        first = i == 0

        @pl.when((pl.program_id(0) == 0) & (pl.program_id(1) == 0) & first)
        def _():
            dgq_ref[...] = jnp.zeros_like(dgq_ref)
            dgk_ref[...] = jnp.zeros_like(dgk_ref)

        @pl.when(first)
        def _():
            ck_ref[...] = jnp.zeros_like(ck_ref)
            cv_ref[...] = jnp.zeros_like(cv_ref)

        @pl.when(i < nb)
        def _():
            dgq, dgk = jnp.zeros((1, HEAD), F32), jnp.zeros((1, HEAD), F32)
            for r in range(d):
                rs = slice(r * BLK, (r + 1) * BLK)
                for h in range(hc):
                    sl = slice(h * HEAD, (h + 1) * HEAD)
                    k2 = jnp.concatenate([_rows(kp_ref, r, d, sl), _rows(kc_ref, r, d, sl)], axis=0)
                    v2 = jnp.concatenate([_rows(vp_ref, r, d, sl), _rows(vc_ref, r, d, sl)], axis=0)
                    dq, dk2, dv2, a, b = _attn_block_bwd(
                        _rows(q_ref, r, d, sl), k2, v2, gq_ref[...], gk_ref[...], first, _rows(do_ref, r, d, sl),
                        _rows(o_ref, r, d, sl), _rows(l_ref, r, d, sl)[:, :1], _rows(dl_ref, r, d, sl)[:, :1])
                    _stage_rows(sq_ref, r, d, sl, dq)
                    _stage_rows(sk_ref, r, d, sl, ck_ref[rs, sl] + dk2[:BLK])
                    _stage_rows(sv_ref, r, d, sl, cv_ref[rs, sl] + dv2[:BLK])
                    ck_ref[rs, sl] = dk2[BLK:]
                    cv_ref[rs, sl] = dv2[BLK:]
                    dgq, dgk = dgq + a, dgk + b
            dgq_ref[...] += dgq
            dgk_ref[...] += dgk
            dq_ref[0] = sq_ref[...].astype(dq_ref.dtype)

        @pl.when(i == nb)
        def _():
            for r in range(d):
                rs = slice(r * BLK, (r + 1) * BLK)
                _stage_rows(sk_ref, r, d, slice(None), ck_ref[rs, :])
                _stage_rows(sv_ref, r, d, slice(None), cv_ref[rs, :])

        dk_ref[0] = sk_ref[...].astype(dk_ref.dtype)
        dv_ref[0] = sv_ref[...].astype(dv_ref.dtype)

    def cur(c0):
        return pl.BlockSpec(blk, lambda b, j, i: (b, jnp.minimum(i, nb - 1), c0 + j))

    def prev(c0):
        return pl.BlockSpec(blk, lambda b, j, i: (b, jnp.clip(i - 1, 0, nb - 1), c0 + j))

    vec = pl.BlockSpec((1, HEAD), lambda b, j, i: (0, 0))
    at_q = pl.BlockSpec(blk, lambda b, j, i: (b, jnp.minimum(i, nb - 1), j))
    at_k = pl.BlockSpec(blk, lambda b, j, i: (b, jnp.maximum(i - 1, 0), j))
    shp = jax.ShapeDtypeStruct((Bl, S, GW), MXU_DTYPE)
    gshp = jax.ShapeDtypeStruct((1, HEAD), F32)
    return _call(body, name=f"attn_bwd_g{g}", grid=(Bl, ncb, nb + 1),
                 in_specs=[cur(cq), prev(ck), cur(ck), prev(cv), cur(cv), vec, vec, at_q, at_q, at_q, at_q],
                 out_specs=[at_q, at_k, at_k, vec, vec], out_shape=[shp, shp, shp, gshp, gshp],
                 scratch_shapes=[pltpu.VMEM(blk[1:], F32)] * 5,
                 )(proj3, proj3, proj3, proj3, proj3, gq, gk, o3, l3, do3, dl3)


def _combine_fwd(os, ls, proj2):
    T = proj2.shape[0]
    tr = _tile(T, 512)

    def body(o1, o2, o3, l1, l2, l3, z, a_ref):
        a_ref[...] = _combine(o1[...], o2[...], o3[...], l1[...], l2[...], l3[...], z[...]).astype(a_ref.dtype)

    row = pl.BlockSpec((tr, GW), lambda i: (i, 0))
    return _call(body, name="combine_fwd", grid=(T // tr,),
                 in_specs=[row] * 6 + [pl.BlockSpec((tr, GW), lambda i: (i, ZA // GW))], out_specs=row,
                 out_shape=jax.ShapeDtypeStruct((T, GW), MXU_DTYPE))(*os, *ls, proj2)


def _combine_bwd(os, ls, proj2, da):
    T = proj2.shape[0]
    tr = _tile(T, 256)

    def body(o1, o2, o3, l1, l2, l3, z, da_ref, d1, d2, d3, e1, e2, e3, dz_ref):
        _, vjp = jax.vjp(_combine, o1[...], o2[...], o3[...], l1[...], l2[...], l3[...], z[...])
        go1, go2, go3, gl1, gl2, gl3, gz = vjp(da_ref[...])
        d1[...], d2[...], d3[...] = go1, go2, go3
        dz_ref[...] = gz.astype(dz_ref.dtype)
        for ref, gl in ((e1, gl1), (e2, gl2), (e3, gl3)):
            for h in range(HPG):
                sl = slice(h * HEAD, (h + 1) * HEAD)
                ref[:, sl] = jnp.broadcast_to(jnp.sum(gl[:, sl], axis=-1, keepdims=True), (tr, HEAD))

    row = pl.BlockSpec((tr, GW), lambda i: (i, 0))
    f = jax.ShapeDtypeStruct((T, GW), F32)
    outs = _call(body, name="combine_bwd", grid=(T // tr,),
                 in_specs=[row] * 6 + [pl.BlockSpec((tr, GW), lambda i: (i, ZA // GW)), row],
                 out_specs=[row] * 7, out_shape=[f] * 6 + [jax.ShapeDtypeStruct((T, GW), MXU_DTYPE)],
                 )(*os, *ls, proj2, da)
    return outs[:3], outs[3:6], outs[6]


def _shift_down(u, j, t):
    return jnp.where(t >= j, pltpu.roll(u, j, 0), 0.0)


def _shift_up(u, j, t):
    n = u.shape[0]
    return jnp.where(t < n - j, pltpu.roll(u, n - j, 0), 0.0)


def _conv_specs(Bl, S, cw):
    def sec(c0):
        return pl.BlockSpec((1, S, cw), lambda j, b: (b, 0, c0 // cw + j))
    return [sec(CB), sec(CC), sec(CV), sec(ZC)], pl.BlockSpec((3, cw), lambda j, b: (0, j))


def _conv_fwd(proj3, conv_w):
    Bl, S, _ = proj3.shape
    cw = 256
    secs, wspec = _conv_specs(Bl, S, cw)

    def body(b_ref, c_ref, v_ref, z_ref, w_ref, o_ref):
        t = lax.broadcasted_iota(jnp.int32, (S, cw), 0)
        u = c_ref[0] * v_ref[0]
        y = w_ref[0:1, :] * u + w_ref[1:2, :] * _shift_down(u, 1, t) + w_ref[2:3, :] * _shift_down(u, 2, t)
        o_ref[0] = (b_ref[0] * y * _silu(z_ref[0])).astype(o_ref.dtype)

    return _call(body, name="conv_fwd", grid=(CONVW // cw, Bl), in_specs=secs + [wspec],
                 out_specs=pl.BlockSpec((1, S, cw), lambda j, b: (b, 0, j)),
                 out_shape=jax.ShapeDtypeStruct((Bl, S, CONVW), MXU_DTYPE))(proj3, proj3, proj3, proj3, conv_w)


def _conv_bwd(proj3, conv_w, dcc3):
    Bl, S, _ = proj3.shape
    cw = 256
    secs, wspec = _conv_specs(Bl, S, cw)

    def body(b_ref, c_ref, v_ref, z_ref, w_ref, d_ref, db_ref, dc_ref, dv_ref, dz_ref, dw_ref):
        t = lax.broadcasted_iota(jnp.int32, (S, cw), 0)
        bv, cv, vv, zv, dv = b_ref[0], c_ref[0], v_ref[0], z_ref[0], d_ref[0]
        u = cv * vv
        u1, u2 = _shift_down(u, 1, t), _shift_down(u, 2, t)
        y = w_ref[0:1, :] * u + w_ref[1:2, :] * u1 + w_ref[2:3, :] * u2
        sg = _sig(zv)
        sz = zv * sg
        gy = dv * bv * sz
        db_ref[0] = (dv * y * sz).astype(db_ref.dtype)
        dz_ref[0] = (dv * bv * y * sg * (1.0 + zv * (1.0 - sg))).astype(dz_ref.dtype)
        du = w_ref[0:1, :] * gy + w_ref[1:2, :] * _shift_up(gy, 1, t) + w_ref[2:3, :] * _shift_up(gy, 2, t)
        dc_ref[0] = (du * vv).astype(dc_ref.dtype)
        dv_ref[0] = (du * cv).astype(dv_ref.dtype)

        @pl.when(pl.program_id(1) == 0)
        def _():
            dw_ref[...] = jnp.zeros_like(dw_ref)

        dw_ref[0:1, :] += jnp.sum(gy * u, axis=0, keepdims=True)
        dw_ref[1:2, :] += jnp.sum(gy * u1, axis=0, keepdims=True)
        dw_ref[2:3, :] += jnp.sum(gy * u2, axis=0, keepdims=True)

    blk = pl.BlockSpec((1, S, cw), lambda j, b: (b, 0, j))
    shp = jax.ShapeDtypeStruct((Bl, S, CONVW), MXU_DTYPE)
    return _call(body, name="conv_bwd", grid=(CONVW // cw, Bl), in_specs=secs + [wspec, blk],
                 out_specs=[blk] * 4 + [wspec], out_shape=[shp] * 4 + [jax.ShapeDtypeStruct((3, CONVW), F32)],
                 )(proj3, proj3, proj3, proj3, conv_w, dcc3)


def _mem_specs(S, tq):
    q = pl.BlockSpec((1, tq, MEMW), lambda b, j: (b, j, MQ // MEMW))
    z = pl.BlockSpec((1, tq, MEMW), lambda b, j: (b, j, ZM // MEMW))
    kv = pl.BlockSpec((1, MEM_HD, 2 * MEMW), lambda b, j: (b, 0, 0))
    vec = pl.BlockSpec((1, MEM_HD), lambda b, j: (0, 0))
    blk = pl.BlockSpec((1, tq, MEMW), lambda b, j: (b, j, 0))
    return q, z, kv, vec, blk


def _mem_fwd(proj3, mkv3, gq, gk):
    Bl, S, _ = proj3.shape
    tq = _tile(S, 512)
    q, z, kv, vec, blk = _mem_specs(S, tq)

    def body(q_ref, z_ref, kv_ref, gq_ref, gk_ref, o_ref):
        o_ref[0] = _mem_block(q_ref[0], z_ref[0], kv_ref[0], gq_ref[...], gk_ref[...]).astype(o_ref.dtype)

    return _call(body, name="mem_fwd", grid=(Bl, S // tq), in_specs=[q, z, kv, vec, vec], out_specs=blk,
                 out_shape=jax.ShapeDtypeStruct((Bl, S, MEMW), MXU_DTYPE))(proj3, proj3, mkv3, gq, gk)


def _mem_bwd(proj3, mkv3, gq, gk, dmo3):
    Bl, S, _ = proj3.shape
    tq = _tile(S, 256)
    q, z, kv, vec, blk = _mem_specs(S, tq)

    def body(q_ref, z_ref, kv_ref, gq_ref, gk_ref, d_ref, dq_ref, dz_ref, dkv_ref, dgq_ref, dgk_ref):
        _, vjp = jax.vjp(_mem_block, q_ref[0], z_ref[0], kv_ref[0], gq_ref[...], gk_ref[...])
        dq, dz, dkv, dgq, dgk = vjp(d_ref[0])
        dq_ref[0] = dq.astype(dq_ref.dtype)
        dz_ref[0] = dz.astype(dz_ref.dtype)
        j = pl.program_id(1)

        @pl.when(j == 0)
        def _():
            dkv_ref[0] = jnp.zeros_like(dkv)

        @pl.when((j == 0) & (pl.program_id(0) == 0))
        def _():
            dgq_ref[...] = jnp.zeros_like(dgq_ref)
            dgk_ref[...] = jnp.zeros_like(dgk_ref)

        dkv_ref[0] += dkv
        dgq_ref[...] += dgq
        dgk_ref[...] += dgk

    shp = jax.ShapeDtypeStruct((Bl, S, MEMW), MXU_DTYPE)
    gshp = jax.ShapeDtypeStruct((1, MEM_HD), F32)
    return _call(body, name="mem_bwd", grid=(Bl, S // tq), in_specs=[q, z, kv, vec, vec, blk],
                 out_specs=[blk, blk, kv, vec, vec],
                 out_shape=[shp, shp, jax.ShapeDtypeStruct(mkv3.shape, F32), gshp, gshp],
                 )(proj3, proj3, mkv3, gq, gk, dmo3)


def _merge_specs(T, D, tm, tn):
    def act(w):
        return pl.BlockSpec((tm, w), lambda i, n: (i, 0))

    def wsp(w):
        return pl.BlockSpec((w, tn), lambda i, n: (0, n))

    gates = [pl.BlockSpec((tm, tn), lambda i, n, k=k: (i, (G0 + k * D) // tn + n)) for k in range(3)]
    tile = pl.BlockSpec((tm, tn), lambda i, n: (i, n))
    return act, wsp, gates, tile


def _merge_fwd(a, cc, mo, wa, wc, wm, proj2):
    T, D = a.shape[0], wa.shape[1]
    tm, tn = _tile(T, 1024), _tile(D, 512)
    act, wsp, gates, tile = _merge_specs(T, D, tm, tn)

    def body(a_ref, c_ref, m_ref, wa_ref, wc_ref, wm_ref, g0, g1, g2, mg_ref, mt_ref, pa_ref, pc_ref, pm_ref):
        pa = jnp.dot(a_ref[...], wa_ref[...], preferred_element_type=F32)
        pc = jnp.dot(c_ref[...], wc_ref[...], preferred_element_type=F32)
        pm = jnp.dot(m_ref[...], wm_ref[...], preferred_element_type=F32)
        mg = _sig(g0[...]) * pa + _sig(g1[...]) * pc + _sig(g2[...]) * pm
        mg_ref[...] = mg.astype(mg_ref.dtype)
        mt_ref[...] = mg.T.astype(mt_ref.dtype)
        pa_ref[...] = pa.astype(pa_ref.dtype)
        pc_ref[...] = pc.astype(pc_ref.dtype)
        pm_ref[...] = pm.astype(pm_ref.dtype)

    shp = jax.ShapeDtypeStruct((T, D), MXU_DTYPE)
    return _call(body, name="merge_fwd", grid=(T // tm, D // tn),
                 in_specs=[act(GW), act(CONVW), act(MEMW), wsp(GW), wsp(CONVW), wsp(MEMW)] + gates,
                 out_specs=[tile, pl.BlockSpec((tn, tm), lambda i, n: (n, i)), tile, tile, tile],
                 out_shape=[shp, jax.ShapeDtypeStruct((D, T), MXU_DTYPE), shp, shp, shp],
                 )(a, cc, mo, wa, wc, wm, proj2, proj2, proj2)


def _merge_bwd(dyb, w_out, proj2, pa, pc, pm):
    T, D = dyb.shape
    tm, tn = _tile(T, 1024), _tile(D, 512)
    _, _, gates, tile = _merge_specs(T, D, tm, tn)

    def body(dy_ref, w_ref, g0, g1, g2, p0, p1, p2, dp0, dp1, dp2, dg0, dg1, dg2):
        dm = lax.dot_general(dy_ref[...], w_ref[...], _DIMS["nt"], preferred_element_type=F32)
        for g_ref, p_ref, dp_ref, dg_ref in ((g0, p0, dp0, dg0), (g1, p1, dp1, dg1), (g2, p2, dp2, dg2)):
            gt = _sig(g_ref[...])
            dp_ref[...] = (gt * dm).astype(dp_ref.dtype)
            dg_ref[...] = (dm * p_ref[...].astype(F32) * gt * (1.0 - gt)).astype(dg_ref.dtype)

    shp = jax.ShapeDtypeStruct((T, D), MXU_DTYPE)
    return _call(body, name="merge_bwd", grid=(T // tm, D // tn),
                 in_specs=[pl.BlockSpec((tm, D), lambda i, n: (i, 0)), pl.BlockSpec((tn, D), lambda i, n: (n, 0))]
                 + gates + [tile] * 3,
                 out_specs=[tile] * 6, out_shape=[shp] * 6)(dyb, w_out, proj2, proj2, proj2, pa, pc, pm)


def _out_loss(merged, w_out, x, tgt):
    T, D = x.shape
    tm = _tile(T, 512)

    def body(m_ref, w_ref, x_ref, t_ref, dy_ref, dyb_ref, loss_ref):
        err = x_ref[...] + jnp.dot(m_ref[...], w_ref[...], preferred_element_type=F32) - t_ref[...]
        dy = err * (1.0 / D)
        dy_ref[...] = dy
        dyb_ref[...] = dy.astype(dyb_ref.dtype)

        @pl.when(pl.program_id(0) == 0)
        def _():
            loss_ref[...] = jnp.zeros_like(loss_ref)

        loss_ref[...] += jnp.sum(err * err) * (0.5 / D)

    row = pl.BlockSpec((tm, D), lambda i: (i, 0))
    return _call(body, name="out_loss", grid=(T // tm,),
                 in_specs=[row, pl.BlockSpec((D, D), lambda i: (0, 0)), row, row],
                 out_specs=[row, row, pl.BlockSpec((1, 128), lambda i: (0, 0))],
                 out_shape=[jax.ShapeDtypeStruct((T, D), F32), jax.ShapeDtypeStruct((T, D), MXU_DTYPE),
                            jax.ShapeDtypeStruct((1, 128), F32)])(merged, w_out, x, tgt)


def _proj_chunk(hb, w, meta, j, nslots, half, buf, name):
    T, D = hb.shape
    Cs = w.shape[1] // 4
    tm, tn = _tile(T, 1024), _tile(Cs // 2, 2176)
    nh = Cs // 2 // tn
    per = nh if half is not None else 2 * nh

    def body(meta_ref, a_ref, b_ref, *rest):
        rest[-1][...] = jnp.dot(a_ref[...], b_ref[...], preferred_element_type=F32)

    def tile(n, m):
        if half is None:
            return n % per
        return (m[4] if half == 0 else 1 - m[4]) * nh + n % per

    in_specs = [pl.BlockSpec((tm, D), lambda n, i, m: (i, 0)),
                pl.BlockSpec((D, tn), lambda n, i, m: (0, (j + n // per) * 2 * nh + tile(n, m)))]
    args = [meta, hb, w]
    if buf is not None:
        in_specs.append(ANY)
        args.append(buf)
    spec = pltpu.PrefetchScalarGridSpec(
        num_scalar_prefetch=1, grid=(nslots * per, T // tm), in_specs=in_specs,
        out_specs=pl.BlockSpec((tm, tn), lambda n, i, m: (i, m[j + n // per] * 2 * nh + tile(n, m))))
    return _call(body, name=name, grid_spec=spec, out_shape=jax.ShapeDtypeStruct((T, 4 * Cs), F32),
                 aliases={} if buf is None else {3: 0})(*args)


def _norms(x, mem, norm_g, mem_norm_g):
    D = x.shape[-1]
    hb, hbt = _rms_fwd(x.reshape(-1, D), norm_g.reshape(1, D), "rms_x")
    mhb, _ = _rms_fwd(mem.reshape(-1, D), mem_norm_g.reshape(1, D), "rms_mem")
    return hb, hbt, mhb


def _attention_fwd(proj2, Bl, gq_all, gk_all):
    T, IN = proj2.shape
    proj3 = proj2.reshape(Bl, T // Bl, IN)
    os, ls = [], []
    for g, d in enumerate(DILATIONS):
        o, l = _attn_fwd(proj3, gq_all[g:g + 1], gk_all[g:g + 1], g, d)
        os.append(o.reshape(T, GW))
        ls.append(l.reshape(T, GW))
    return os, ls, _combine_fwd(os, ls, proj2)


def _conv_branch_fwd(proj2, Bl, conv_w):
    T, IN = proj2.shape
    return _conv_fwd(proj2.reshape(Bl, T // Bl, IN), conv_w).reshape(T, CONVW)


def _weight_grads(x, mem, tgt, norm_g, mem_norm_g, gq_all, gk_all, conv_w, mem_gq, mem_gk, W, pre, early=None):
    Bl, S, D = x.shape
    T = Bl * S
    hb, hbt, mhb, proj2, os, ls, a, cc = pre
    IN = proj2.shape[1]
    proj3 = proj2.reshape(Bl, S, IN)
    x2, tgt2 = x.reshape(T, D), tgt.reshape(T, D)
    mem2 = mem.reshape(-1, D)
    ng, mng = norm_g.reshape(1, D), mem_norm_g.reshape(1, D)
    mgq, mgk = mem_gq.reshape(1, MEM_HD), mem_gk.reshape(1, MEM_HD)
    gqs = [gq_all[g:g + 1] for g in range(NGROUP)]
    gks = [gk_all[g:g + 1] for g in range(NGROUP)]

    mkv = _matmul(mhb, W["mem_w_kv"], "nn", F32, name="mem_kv", tm=512, tn=1024, tk=D)
    mkv3 = mkv.reshape(Bl, -1, 2 * MEMW)
    mo = _mem_fwd(proj3, mkv3, mgq, mgk).reshape(T, MEMW)
    merged, mergedt, pa, pc, pm = _merge_fwd(a, cc, mo, W["w_br_attn"], W["w_br_conv"], W["w_br_mem"], proj2)
    dy, dyb, loss = _out_loss(merged, W["w_out"], x2, tgt2)

    G = {}
    G["w_out"] = _matmul(mergedt, dyb, "nn", WIRE_DTYPE, name="dw_out", tm=1024, tn=512, tk=T)
    dpa, dpc, dpm, dg0, dg1, dg2 = _merge_bwd(dyb, W["w_out"], proj2, pa, pc, pm)
    G["w_br_attn"] = _matmul(a, dpa, "tn", WIRE_DTYPE, name="dw_br_attn", tm=512, tn=1024, tk=512)
    G["w_br_conv"] = _matmul(cc, dpc, "tn", WIRE_DTYPE, name="dw_br_conv", tm=1024, tn=1024, tk=512)
    G["w_br_mem"] = _matmul(mo, dpm, "tn", WIRE_DTYPE, name="dw_br_mem", tm=1024, tn=1024, tk=512)
    da = _matmul(dpa, W["w_br_attn"], "nt", F32, name="d_attn", tm=1024, tn=512, tk=D)
    dcc = _matmul(dpc, W["w_br_conv"], "nt", F32, name="d_conv", tm=1024, tn=1024, tk=D)
    dmo = _matmul(dpm, W["w_br_mem"], "nt", F32, name="d_mem", tm=1024, tn=1024, tk=D)
    dmq, dzm, dmkv3, dmgq, dmgk = _mem_bwd(proj3, mkv3, mgq, mgk, dmo.reshape(Bl, S, MEMW))
    dmkv = _cast(dmkv3.reshape(-1, 2 * MEMW), "cast_dmkv")
    G["mem_w_kv"] = _matmul(mhb, dmkv, "tn", WIRE_DTYPE, name="dw_mem_kv", tm=1024, tn=1024, tk=512)
    early_state, da = (None, da) if early is None else early(G, da)
    dmh = _matmul(dmkv, W["mem_w_kv"], "nt", F32, name="d_memh", tm=512, tn=1024, tk=2 * MEMW)
    _, dmng = _rms_bwd(mem2, dmh, mng, None, "rms_mem_bwd")

    dos, dls, dza = _combine_bwd(os, ls, proj2, da)
    dqs, dks, dvs, dgq, dgk = [], [], [], [], []
    for g, d in enumerate(DILATIONS):
        dq, dk, dv, gq_g, gk_g = _attn_bwd(proj3, gqs[g], gks[g], os[g].reshape(Bl, S, GW), ls[g].reshape(Bl, S, GW),
                                           dos[g].reshape(Bl, S, GW), dls[g].reshape(Bl, S, GW), g, d)
        dqs.append(dq.reshape(T, GW).astype(MXU_DTYPE))
        dks.append(dk.reshape(T, GW).astype(MXU_DTYPE))
        dvs.append(dv.reshape(T, GW).astype(MXU_DTYPE))
        dgq.append(gq_g)
        dgk.append(gk_g)
    dcb, dcc_, dcv, dzc, dconv_w = _conv_bwd(proj3, conv_w, dcc.reshape(Bl, S, CONVW))

    dproj = jnp.concatenate(dqs + dks + dvs + [dza] + [t.reshape(T, CONVW) for t in (dcb, dcc_, dcv, dzc)]
                            + [dmq.reshape(T, MEMW), dzm.reshape(T, MEMW), dg0, dg1, dg2], axis=1)
    small = [loss, None, dmng] + dgq + dgk + [dconv_w.reshape(1, 3 * CONVW), dmgq, dmgk]
    return G, (dproj, x2, ng, dy, small), early_state


def _dw_in_half(hbt, dproj, pos, own, name):
    D, T = hbt.shape
    IN = dproj.shape[1]
    R, tn = D // 2, _tile(IN, 512)

    def body(pos_ref, a_ref, b_ref, o_ref):
        o_ref[...] = jnp.dot(a_ref[...], b_ref[...], preferred_element_type=F32).astype(o_ref.dtype)

    spec = pltpu.PrefetchScalarGridSpec(
        num_scalar_prefetch=1, grid=(IN // tn,),
        in_specs=[pl.BlockSpec((R, T), lambda j, p: (p[1] if own else 1 - p[1], 0)),
                  pl.BlockSpec((T, tn), lambda j, p: (0, j))],
        out_specs=pl.BlockSpec((R, tn), lambda j, p: (0, j)))
    return _call(body, name=name, grid_spec=spec, out_shape=jax.ShapeDtypeStruct((R, IN), WIRE_DTYPE))(pos, hbt, dproj)


def _d_h(dproj, w, order):
    T, IN = dproj.shape
    D, Cs = w.shape[0], IN // 4
    tm, tn = _tile(T, 1024), _tile(D, 1024)

    def body(order_ref, a_ref, b_ref, o_ref, acc_ref):
        part = lax.dot_general(a_ref[...], b_ref[...], _DIMS["nt"], preferred_element_type=F32)
        k = pl.program_id(2)

        @pl.when(k == 0)
        def _():
            acc_ref[...] = part

        @pl.when(k > 0)
        def _():
            acc_ref[...] += part

        @pl.when(k == 3)
        def _():
            o_ref[...] = acc_ref[...]

    spec = pltpu.PrefetchScalarGridSpec(
        num_scalar_prefetch=1, grid=(T // tm, D // tn, 4),
        in_specs=[pl.BlockSpec((tm, Cs), lambda i, n, k, o: (i, o[k])), pl.BlockSpec((tn, Cs), lambda i, n, k, o: (n, k))],
        out_specs=pl.BlockSpec((tm, tn), lambda i, n, k, o: (i, n)), scratch_shapes=[pltpu.VMEM((tm, tn), F32)])
    return _call(body, name="d_h", grid_spec=spec, out_shape=jax.ShapeDtypeStruct((T, D), F32))(order, dproj, w)


def _input_grad(rest, w_in, order):
    dproj, x2, ng, dy, small = rest
    dh = _d_h(dproj, w_in, order)
    grad_x, dng = _rms_bwd(x2, dh, ng, dy, "rms_x_bwd")
    small = [dng if t is None else t for t in small]
    return grad_x, jnp.concatenate(small, axis=1)


def _local_step(x, mem, tgt, norm_g, mem_norm_g, gq_all, gk_all, conv_w, mem_gq, mem_gk, W):
    hb, hbt, mhb = _norms(x, mem, norm_g, mem_norm_g)
    Cs = W["w_in"].shape[1] // 4
    shards = (0, 2, 1, 3)
    order = jnp.array(shards, dtype=jnp.int32)
    w_rel = jnp.concatenate([W["w_in"][:, s * Cs:(s + 1) * Cs] for s in shards], axis=1)
    meta = jnp.array(shards + (0,), dtype=jnp.int32)
    proj2 = _proj_chunk(hb, w_rel, meta, 0, 1, None, None, "proj_0")
    for j, nslots in ((1, 2), (3, 1)):
        for half in (1, 0):
            proj2 = _proj_chunk(hb, w_rel, meta, j, nslots, half, proj2, f"proj_{j}_{half}")
    pre = (hb, hbt, mhb, proj2, *_attention_fwd(proj2, x.shape[0], gq_all, gk_all),
           _conv_branch_fwd(proj2, x.shape[0], conv_w))
    G, rest, _ = _weight_grads(x, mem, tgt, norm_g, mem_norm_g, gq_all, gk_all, conv_w, mem_gq, mem_gk, W, pre)
    pos = jnp.zeros((2,), jnp.int32)
    G["w_in"] = jnp.concatenate([_dw_in_half(hbt, rest[0], pos, True, "dw_in_own"),
                                 _dw_in_half(hbt, rest[0], pos, False, "dw_in_sibling")], axis=0)
    grad_x, small = _input_grad(rest, w_rel, order)
    return grad_x.reshape(x.shape), G, small


BIG = (("w_in", "col"), ("mem_w_kv", "row"), ("w_br_attn", "col"), ("w_br_conv", "col"),
       ("w_br_mem", "col"), ("w_out", "row"))


def _coords():
    return lax.axis_index("x"), lax.axis_index("y"), lax.axis_index("c")


def _other_chips(x, y):
    return [(1 - x, y), (x, 1 - y), (1 - x, 1 - y)]


def _half(ref, kind, c):
    R, C = ref.shape
    if kind == "col":
        return ref.at[pl.ds(c * (R // 2), R // 2), :]
    return ref.at[:, pl.ds(c * (C // 2), C // 2)]


def _shard(ref, kind, s):
    R, C = ref.shape
    if kind == "col":
        return ref.at[:, pl.ds(s * (C // 4), C // 4)]
    return ref.at[pl.ds(s * (R // 4), R // 4), :]


def _piece(ref, kind, s, c):
    R, C = ref.shape
    if kind == "col":
        return ref.at[pl.ds(c * (R // 2), R // 2), pl.ds(s * (C // 4), C // 4)]
    return ref.at[pl.ds(s * (R // 4), R // 4), pl.ds(c * (C // 2), C // 2)]


def _remote(src, dst, sems_s, sems_r, k, dev):
    return pltpu.make_async_remote_copy(src_ref=src, dst_ref=dst, send_sem=sems_s.at[k], recv_sem=sems_r.at[k],
                                        device_id=dev, device_id_type=MESH)


HBM = pl.BlockSpec(memory_space=pltpu.HBM)
SEM = pl.BlockSpec(memory_space=pltpu.SEMAPHORE)
EFFECT = pltpu.SideEffectType.DATAFLOW_SIDE_EFFECTING


def _hbm(a):
    return pltpu.with_memory_space_constraint(a, pltpu.HBM)


def _start_copies(name, arrays, ncopies, make):
    n = len(arrays)

    def body(*refs):
        for cp in make(refs[:n], refs[n], refs[n + 1]):
            cp.start()

    outs = pl.pallas_call(
        body, name=name,
        out_shape=(pltpu.SemaphoreType.DMA((ncopies,)), pltpu.SemaphoreType.DMA((ncopies,)),
                   *[jax.ShapeDtypeStruct(t.shape, t.dtype) for t in arrays]),
        in_specs=[HBM] * n, out_specs=(SEM, SEM, *([HBM] * n)),
        input_output_aliases={i: i + 2 for i in range(n)},
        compiler_params=pltpu.CompilerParams(has_side_effects=EFFECT),
    )(*[_hbm(t) for t in arrays])
    return outs[0], outs[1], list(outs[2:])


def _wait_copies(name, send, recv, arrays, make, after):
    n = len(arrays)

    def body(*refs):
        for cp in make(refs[:n], refs[n], refs[n + 1]):
            cp.wait_send()
            cp.wait_recv()

    outs = pl.pallas_call(
        body, name=name, out_shape=[jax.ShapeDtypeStruct(t.shape, t.dtype) for t in arrays],
        in_specs=[HBM] * n + [SEM, SEM, ANY], out_specs=[HBM] * n,
        input_output_aliases={i: i for i in range(n)},
        compiler_params=pltpu.CompilerParams(has_side_effects=EFFECT),
    )(*arrays, send, recv, after)
    return list(outs)


def _w_in_copies(relations):
    def make(refs, send, recv):
        x, y, c = _coords()
        me = 2 * x + y
        chips = _other_chips(x, y)
        w, conv = refs[0], refs[1]
        cps = []
        for i, k in enumerate(relations):
            cps.append(_remote(_column_half(w, 0, c), _column_half(w, 1 + k, c), send, recv, 2 * i, (*chips[k], c)))
            mine = _shard(conv, "col", me)
            cps.append(_remote(mine, mine, send, recv, 2 * i + 1, (*chips[k], c)))
        return cps
    return make


def _column_half(w, slot, c):
    half = w.shape[1] // 8
    return w.at[:, pl.ds((2 * slot + c) * half, half)]


def _w_in_forward(relations):
    def make(refs, send, recv):
        x, y, c = _coords()
        cps = []
        for i, k in enumerate(relations):
            got = _column_half(refs[0], 1 + k, c)
            cps.append(_remote(got, got, send, recv, i, (x, y, 1 - c)))
        return cps
    return make


def _sibling_copy(refs, send, recv):
    x, y, c = _coords()
    return [_remote(refs[0], refs[1], send, recv, 0, (x, y, 1 - c))]


def _other_weight_copies(refs, send, recv):
    x, y, c = _coords()
    me = 2 * x + y
    cps = []
    for k, chip in enumerate(_other_chips(x, y)):
        for p, (_, kind) in enumerate(BIG[1:]):
            mine = _piece(refs[p], kind, me, c)
            cps.append(_remote(mine, mine, send, recv, 3 * p + k, (*chip, c)))
    return cps


def _other_weight_forward(refs, send, recv):
    x, y, c = _coords()
    cps = []
    for k, chip in enumerate(_other_chips(x, y)):
        s = 2 * chip[0] + chip[1]
        for p, (_, kind) in enumerate(BIG[1:]):
            got = _piece(refs[p], kind, s, c)
            cps.append(_remote(got, got, send, recv, 3 * p + k, (x, y, 1 - c)))
    return cps


def _share_copies(group):
    def make(refs, send, recv):
        x, y, c = _coords()
        cps = []
        for p, (_, kind) in enumerate(group):
            mine = _half(refs[p], kind, c)
            cps.append(_remote(mine, mine, send, recv, p, (x, y, 1 - c)))
        return cps
    return make


def _sibling_forward(name, arrays, ncp, halves):
    n = len(arrays)

    def body(*refs):
        outs = refs[n:2 * n]
        send, recv = refs[2 * n:]
        x, y, c = _coords()
        sib = (x, y, 1 - c)
        cps = [_remote(got, got, send, recv, i, sib) for i, got in enumerate(halves(outs, c))]
        for cp in cps:
            cp.start()
        for cp in cps:
            cp.wait_send()
        for i, got in enumerate(halves(outs, 1 - c)):
            _remote(got, got, send, recv, i, sib).wait_recv()

    return pl.pallas_call(
        body, name=name, out_shape=[jax.ShapeDtypeStruct(t.shape, t.dtype) for t in arrays],
        in_specs=[ANY] * n, out_specs=[ANY] * n, input_output_aliases={i: i for i in range(n)},
        scratch_shapes=[pltpu.SemaphoreType.DMA((ncp,)), pltpu.SemaphoreType.DMA((ncp,))],
    )(*arrays)


def _landed_halves(refs, c):
    return [_half(r, "col", c) for r in refs]


def _other_weight_halves(refs, c):
    x, y, _ = _coords()
    out = []
    for chip in _other_chips(x, y):
        s = 2 * chip[0] + chip[1]
        out += [_piece(refs[p], kind, s, c) for p, (_, kind) in enumerate(BIG[1:])]
    return out


def _sibling_exchange(grads, group, name):
    n = len(group)
    shapes = []
    for (_, kind), g in zip(group, grads):
        R, C = g.shape
        shapes.append(jax.ShapeDtypeStruct((R // 2, C) if kind == "col" else (R, C // 2), g.dtype))

    def body(*refs):
        ins, outs = refs[:n], refs[n:2 * n]
        send, recv = refs[2 * n:]
        x, y, c = _coords()
        sib = (x, y, 1 - c)
        cps = [_remote(_half(ins[p], group[p][1], 1 - c), outs[p], send, recv, p, sib) for p in range(n)]
        for cp in cps:
            cp.start()
        for cp in cps:
            cp.wait()

    return pl.pallas_call(
        body, name=name, out_shape=shapes, in_specs=[ANY] * n, out_specs=[ANY] * n,
        scratch_shapes=[pltpu.SemaphoreType.DMA((n,)), pltpu.SemaphoreType.DMA((n,))],
    )(*grads)


def _presum(g, got, kind, pos, name):
    R, C = got.shape
    tr, tc = _tile(R, 512, 16), _tile(C, 2048)
    nr, nc = R // tr, C // tc

    def body(pos_ref, a_ref, b_ref, o_ref):
        o_ref[...] = (a_ref[...].astype(F32) + b_ref[...].astype(F32)).astype(o_ref.dtype)

    blk = pl.BlockSpec((tr, tc), lambda i, j, pos_ref: (i, j))
    if g.shape == got.shape:
        mine = blk
    elif kind == "col":
        mine = pl.BlockSpec((tr, tc), lambda i, j, pos_ref: (pos_ref[1] * nr + i, j))
    else:
        mine = pl.BlockSpec((tr, tc), lambda i, j, pos_ref: (i, pos_ref[1] * nc + j))
    spec = pltpu.PrefetchScalarGridSpec(num_scalar_prefetch=1, grid=(nr, nc), in_specs=[mine, blk], out_specs=blk)
    return _call(body, name=name, grid_spec=spec, out_shape=jax.ShapeDtypeStruct((R, C), WIRE_DTYPE))(pos, g, got)


def _chip_copies(group):
    n = len(group)

    def make(refs, send, recv):
        x, y, c = _coords()
        cps = []
        for k, chip in enumerate(_other_chips(x, y)):
            s = 2 * chip[0] + chip[1]
            for p in range(n):
                cps.append(_remote(_shard(refs[p], group[p][1], s), refs[n + p].at[k], send, recv, 3 * p + k, (*chip, c)))
        return cps
    return make


def _landing_zones(pres, group):
    lands = []
    for (_, kind), g in zip(group, pres):
        R, C = g.shape
        lands.append(lax.empty((3, R, C // 4) if kind == "col" else (3, R // 4, C), g.dtype))
    return lands


def _exchange_start(G, group, pos, carry, tag, pres=None):
    n = len(group)
    if pres is None:
        parts = [G[name] for name, _ in group]
        got = _sibling_exchange(parts, group, "sibling_exchange_" + tag)
        pres = [_presum(parts[p], got[p], kind, pos, "presum_" + name) for p, (name, kind) in enumerate(group)]
    make = _chip_copies(group)
    send, recv, thru = _start_copies("chip_exchange_start_" + tag, [*pres, *_landing_zones(pres, group), carry], 3 * n, make)
    return (send, recv, thru[:2 * n], make, tag), thru[2 * n]


def _exchange_wait(state, after):
    send, recv, arrays, make, tag = state
    thru = _wait_copies("chip_exchange_wait_" + tag, send, recv, arrays, make, after)
    n = len(thru) // 2
    return thru[:n], thru[n:]


def _reduce_into_shard(slots, pre, kind, pos, name):
    K, R, C = slots.shape
    tr, tc = _tile(R, 512, 16), _tile(C, 2176)
    nr, nc = R // tr, C // tc

    def body(pos_ref, s_ref, p_ref, o_ref):
        acc = p_ref[...].astype(F32)
        for k in range(K):
            acc = acc + s_ref[k].astype(F32)
        o_ref[...] = acc

    if kind == "col":
        own = pl.BlockSpec((tr, tc), lambda i, j, pos_ref: (i, pos_ref[0] * nc + j))
        full, out = (2 * R, C), pl.BlockSpec((tr, tc), lambda i, j, pos_ref: (pos_ref[1] * nr + i, j))
    else:
        own = pl.BlockSpec((tr, tc), lambda i, j, pos_ref: (pos_ref[0] * nr + i, j))
        full, out = (R, 2 * C), pl.BlockSpec((tr, tc), lambda i, j, pos_ref: (i, pos_ref[1] * nc + j))
    spec = pltpu.PrefetchScalarGridSpec(
        num_scalar_prefetch=1, grid=(nr, nc),
        in_specs=[pl.BlockSpec((K, tr, tc), lambda i, j, pos_ref: (0, i, j)), own], out_specs=out)
    return _call(body, name=name, grid_spec=spec, out_shape=jax.ShapeDtypeStruct(full, F32))(pos, slots, pre)


def _share_reduced(reds):
    n = len(BIG)

    def body(*refs):
        outs = refs[n:2 * n]
        send, recv = refs[2 * n:]
        x, y, c = _coords()
        sib = (x, y, 1 - c)
        cps = []
        for p in range(n):
            mine = _half(outs[p], BIG[p][1], c)
            cps.append(_remote(mine, mine, send, recv, p, sib))
        for cp in cps:
            cp.start()
        for cp in cps:
            cp.wait_send()
        for p in range(n):
            got = _half(outs[p], BIG[p][1], 1 - c)
            _remote(got, got, send, recv, p, sib).wait_recv()

    return pl.pallas_call(
        body, name="share_reduced", out_shape=[jax.ShapeDtypeStruct(r.shape, r.dtype) for r in reds],
        in_specs=[ANY] * n, out_specs=[ANY] * n, input_output_aliases={p: p for p in range(n)},
        scratch_shapes=[pltpu.SemaphoreType.DMA((n,)), pltpu.SemaphoreType.DMA((n,))],
    )(*reds)


def _gather_small(pack):
    _, N = pack.shape

    def body(in_ref, out_ref, send, recv, loc):
        x, y, c = _coords()
        me = 4 * x + 2 * y + c
        own = pltpu.make_async_copy(in_ref, out_ref.at[me], loc)
        own.start()
        cps = []
        for k in range(1, 8):
            dev = (x ^ (k >> 2), y ^ ((k >> 1) & 1), c ^ (k & 1))
            cps.append(_remote(in_ref, out_ref.at[me], send, recv, k - 1, dev))
        for cp in cps:
            cp.start()
        for k in range(1, 8):
            src = 4 * (x ^ (k >> 2)) + 2 * (y ^ ((k >> 1) & 1)) + (c ^ (k & 1))
            _remote(in_ref, out_ref.at[src], send, recv, k - 1, (x, y, c)).wait_recv()
        for cp in cps:
            cp.wait_send()
        own.wait()

    return pl.pallas_call(
        body, name="gather_small", out_shape=jax.ShapeDtypeStruct((8, 1, N), pack.dtype),
        in_specs=[ANY], out_specs=ANY,
        scratch_shapes=[pltpu.SemaphoreType.DMA((7,)), pltpu.SemaphoreType.DMA((7,)), pltpu.SemaphoreType.DMA(())],
    )(pack)


def _sum_small(slots):
    K, _, N = slots.shape

    def body(s_ref, o_ref):
        acc = s_ref[0]
        for k in range(1, K):
            acc = acc + s_ref[k]
        o_ref[...] = acc

    return _call(body, name="sum_small", in_specs=[pl.BlockSpec(memory_space=pltpu.VMEM)],
                 out_specs=pl.BlockSpec(memory_space=pltpu.VMEM), out_shape=jax.ShapeDtypeStruct((1, N), F32))(slots)


def _adamw(w, g, m, v, name):
    R, C = w.shape
    tr, tc = _tile(R, 256, 8), _tile(C, 2176)

    def body(w_ref, g_ref, m_ref, v_ref, d_ref, nm_ref, nv_ref):
        gv = g_ref[...]
        nm = ADAM_B1 * m_ref[...] + (1.0 - ADAM_B1) * gv
        nv = ADAM_B2 * v_ref[...] + (1.0 - ADAM_B2) * gv * gv
        m_hat = nm / (1.0 - ADAM_B1 ** ADAM_STEP)
        v_hat = nv / (1.0 - ADAM_B2 ** ADAM_STEP)
        d_ref[...] = -ADAM_LR * (m_hat / (jnp.sqrt(v_hat) + ADAM_EPS) + ADAM_WD * w_ref[...])
        nm_ref[...] = nm
        nv_ref[...] = nv

    spec = pl.BlockSpec((tr, tc), lambda i, j: (i, j))
    shp = jax.ShapeDtypeStruct((R, C), F32)
    return _call(body, name=name, grid=(R // tr, C // tc), in_specs=[spec] * 4, out_specs=[spec] * 3,
                 out_shape=[shp] * 3)(w, g, m, v)


SMALL = ("norm_g", "mem_norm_g", "attn_q_norm", "attn_k_norm", "conv_w", "mem_q_norm", "mem_k_norm")
WEIGHTS = ("norm_g", "mem_norm_g", "w_in", "attn_q_norm", "attn_k_norm", "conv_w", "mem_w_kv", "mem_q_norm",
           "mem_k_norm", "w_br_attn", "w_br_conv", "w_br_mem", "w_out")


def kernel(x, mem, norm_g, mem_norm_g, w_in, attn_q_norm, attn_k_norm, conv_w, mem_w_kv, mem_q_norm, mem_k_norm, w_br_attn, w_br_conv, w_br_mem, w_out, loss_target, m_norm_g, m_mem_norm_g, m_w_in, m_attn_q_norm, m_attn_k_norm, m_conv_w, m_mem_w_kv, m_mem_q_norm, m_mem_k_norm, m_w_br_attn, m_w_br_conv, m_w_br_mem, m_w_out, v_norm_g, v_mem_norm_g, v_w_in, v_attn_q_norm, v_attn_k_norm, v_conv_w, v_mem_w_kv, v_mem_q_norm, v_mem_k_norm, v_w_br_attn, v_w_br_conv, v_w_br_mem, v_w_out):
    w = dict(norm_g=norm_g, mem_norm_g=mem_norm_g, w_in=w_in, attn_q_norm=attn_q_norm, attn_k_norm=attn_k_norm,
             conv_w=conv_w, mem_w_kv=mem_w_kv, mem_q_norm=mem_q_norm, mem_k_norm=mem_k_norm, w_br_attn=w_br_attn,
             w_br_conv=w_br_conv, w_br_mem=w_br_mem, w_out=w_out)
    m = dict(norm_g=m_norm_g, mem_norm_g=m_mem_norm_g, w_in=m_w_in, attn_q_norm=m_attn_q_norm,
             attn_k_norm=m_attn_k_norm, conv_w=m_conv_w, mem_w_kv=m_mem_w_kv, mem_q_norm=m_mem_q_norm,
             mem_k_norm=m_mem_k_norm, w_br_attn=m_w_br_attn, w_br_conv=m_w_br_conv, w_br_mem=m_w_br_mem, w_out=m_w_out)
    v = dict(norm_g=v_norm_g, mem_norm_g=v_mem_norm_g, w_in=v_w_in, attn_q_norm=v_attn_q_norm,
             attn_k_norm=v_attn_k_norm, conv_w=v_conv_w, mem_w_kv=v_mem_w_kv, mem_q_norm=v_mem_q_norm,
             mem_k_norm=v_mem_k_norm, w_br_attn=v_w_br_attn, w_br_conv=v_w_br_conv, w_br_mem=v_w_br_mem, w_out=v_w_out)
    Bl, _, D = x.shape
    cx, cy = lax.axis_index("x"), lax.axis_index("y")
    chip = 2 * cx + cy
    pos = jnp.stack([chip, lax.axis_index("c")]).astype(jnp.int32)
    order = jnp.stack([chip] + [2 * a + b for a, b in _other_chips(cx, cy)]).astype(jnp.int32)
    n = len(BIG)

    w_rel = _place_shard(w["w_in"], "col", jnp.zeros((1,), jnp.int32), WIRE_DTYPE, "place_w_in")
    conv_full = _place_shard(conv_w, "col", pos, F32, "place_conv_w")
    others = [_place_shard(w[name], kind, pos, WIRE_DTYPE, "place_" + name) for name, kind in BIG[1:]]
    hb, hbt, mhb = _norms(x, mem, norm_g, mem_norm_g)

    meta = jnp.concatenate([order, pos[1:]])
    near, near_fwd = _w_in_copies((0, 1)), _w_in_forward((0, 1))
    send, recv, (w_rel, conv_full) = _start_copies("gather_near_start", [w_rel, conv_full], 4, near)
    proj = _proj_chunk(hb, w_rel, meta, 0, 1, None, None, "proj_own")
    w_rel, conv_full, *others = _wait_copies("gather_near_wait", send, recv, [w_rel, conv_full, *others], near, proj)

    fsend, frecv, (w_rel,) = _start_copies("gather_near_forward_start", [w_rel], 2, near_fwd)
    far, far_fwd = _w_in_copies((2,)), _w_in_forward((2,))
    send, recv, (w_rel, conv_full) = _start_copies("gather_far_start", [w_rel, conv_full], 2, far)
    proj = _proj_chunk(hb, w_rel, meta, 1, 2, 0, proj, "proj_near_landed")
    w_rel, = _wait_copies("gather_near_forward_wait", fsend, frecv, [w_rel], near_fwd, proj)
    proj = _proj_chunk(hb, w_rel, meta, 1, 2, 1, proj, "proj_near_forwarded")
    w_rel, conv_full = _wait_copies("gather_far_wait", send, recv, [w_rel, conv_full], far, proj)

    fsend, frecv, (w_rel,) = _start_copies("gather_far_forward_start", [w_rel], 1, far_fwd)
    send, recv, (*others, w_rel) = _start_copies("gather_rest_start", [*others, w_rel], 3 * (n - 1), _other_weight_copies)
    proj = _proj_chunk(hb, w_rel, meta, 3, 1, 0, proj, "proj_far_landed")
    w_rel, = _wait_copies("gather_far_forward_wait", fsend, frecv, [w_rel], far_fwd, proj)
    proj = _proj_chunk(hb, w_rel, meta, 3, 1, 1, proj, "proj_far_forwarded")
    os, ls, a = _attention_fwd(proj, Bl, attn_q_norm, attn_k_norm)
    *others, w_rel = _wait_copies("gather_rest_wait", send, recv, [*others, w_rel], _other_weight_copies, a)
    fsend, frecv, (*others, proj) = _start_copies("gather_rest_forward_start", [*others, proj], 3 * (n - 1),
                                                  _other_weight_forward)
    cc = _conv_branch_fwd(proj, Bl, conv_full)
    others = _wait_copies("gather_rest_forward_wait", fsend, frecv, others, _other_weight_forward, cc)
    W = {name: others[p] for p, (name, _) in enumerate(BIG[1:])}

    G, rest, rest_state = _weight_grads(
        x, mem, loss_target, norm_g, mem_norm_g, attn_q_norm, attn_k_norm, conv_full, mem_q_norm, mem_k_norm, W,
        (hb, hbt, mhb, proj, os, ls, a, cc), early=lambda G, carry: _exchange_start(G, BIG[1:], pos, carry, "rest"))

    for_sibling = _dw_in_half(hbt, rest[0], pos, False, "dw_in_sibling")
    send, recv, (for_sibling, got, dproj) = _start_copies(
        "sibling_w_in_start", [for_sibling, lax.empty(for_sibling.shape, for_sibling.dtype), rest[0]], 1, _sibling_copy)
    mine = _dw_in_half(hbt, dproj, pos, True, "dw_in_own")
    for_sibling, got = _wait_copies("sibling_w_in_wait", send, recv, [for_sibling, got], _sibling_copy, mine)
    pre_w_in = _presum(mine, got, "col", pos, "presum_w_in")

    w_in_state, dproj = _exchange_start(G, BIG[:1], pos, dproj, "w_in", pres=[pre_w_in])
    grad_x, small = _input_grad((dproj, *rest[1:]), w_rel, order)
    pres_rest, slots_rest = _exchange_wait(rest_state, grad_x)
    reds_rest = [_reduce_into_shard(slots_rest[p], pres_rest[p], kind, pos, "reduce_" + name)
                 for p, (name, kind) in enumerate(BIG[1:])]
    share_rest = _share_copies(BIG[1:])
    rsend, rrecv, reds_rest = _start_copies("share_rest_start", reds_rest, n - 1, share_rest)
    pres, slots = _exchange_wait(w_in_state, grad_x)
    red_w_in = _reduce_into_shard(slots[0], pres[0], "col", pos, "reduce_w_in")
    share_w_in = _share_copies(BIG[:1])
    wsend, wrecv, (red_w_in,) = _start_copies("share_w_in_start", [red_w_in], 1, share_w_in)
    grad_x = grad_x.reshape(x.shape)

    tot = _sum_small(_gather_small(small))
    reds_rest = _wait_copies("share_rest_wait", rsend, rrecv, reds_rest, share_rest, tot)
    grads = dict(zip([name for name, _ in BIG[1:]], reds_rest))
    tot = tot[0]
    loss = tot[0]
    off = 128
    for name, size in (("norm_g", D), ("mem_norm_g", D), ("attn_q_norm", NGROUP * HEAD), ("attn_k_norm", NGROUP * HEAD),
                       ("conv_w", 3 * CONVW), ("mem_q_norm", MEM_HD), ("mem_k_norm", MEM_HD)):
        grads[name] = tot[off:off + size]
        off += size
    cw = conv_w.shape[1]
    grads["conv_w"] = lax.dynamic_slice(grads["conv_w"].reshape(3, CONVW), (0, chip * cw), (3, cw))
    for name in SMALL:
        grads[name] = grads[name].reshape(w[name].shape)

    delta, new_m, new_v = {}, {}, {}
    for name, _ in BIG[1:]:
        delta[name], new_m[name], new_v[name] = _adamw(w[name], grads[name], m[name], v[name], "adamw_" + name)

    def packed(t):
        return jnp.concatenate([t[name].reshape(1, -1) for name in SMALL], axis=1)

    ds, ms, vs = _adamw(packed(w), packed(grads), packed(m), packed(v), "adamw_small")
    grads["w_in"], = _wait_copies("share_w_in_wait", wsend, wrecv, [red_w_in], share_w_in, ds)
    delta["w_in"], new_m["w_in"], new_v["w_in"] = _adamw(w["w_in"], grads["w_in"], m["w_in"], v["w_in"], "adamw_w_in")
    off = 0
    for name in SMALL:
        size = w[name].size
        delta[name] = ds[0, off:off + size].reshape(w[name].shape)
        new_m[name] = ms[0, off:off + size].reshape(w[name].shape)
        new_v[name] = vs[0, off:off + size].reshape(w[name].shape)
        off += size

    return (loss, grad_x, *[grads[n] for n in WEIGHTS], *[delta[n] for n in WEIGHTS],
            *[new_m[n] for n in WEIGHTS], *[new_v[n] for n in WEIGHTS])
```

```python
import functools

import jax
import jax.numpy as jnp
from jax import lax
from jax.experimental import pallas as pl
from jax.experimental.pallas import tpu as pltpu

F32 = jnp.float32
MXU_DTYPE = jnp.bfloat16
WIRE_DTYPE = jnp.bfloat16
PROJ_DTYPE = jnp.bfloat16
EPS = 1e-6
NEG = -1e30

HEAD = 128
HPG = 4
GW = HPG * HEAD
DILATIONS = (1, 4, 16)
NGROUP = len(DILATIONS)
BLK = 128
QKV = NGROUP * GW
CONVW = 1024
MEM_HEADS = 4
MEM_HD = 256
MEMW = MEM_HEADS * MEM_HD
Q0, K0, V0 = 0, QKV, 2 * QKV
ZA = 3 * QKV
CB, CC, CV, ZC = ZA + GW, ZA + GW + CONVW, ZA + GW + 2 * CONVW, ZA + GW + 3 * CONVW
MQ = ZC + CONVW
ZM = MQ + MEMW
G0 = ZM + MEMW

ADAM_LR, ADAM_B1, ADAM_B2, ADAM_EPS, ADAM_WD, ADAM_STEP = 0.001, 0.9, 0.999, 1e-08, 0.01, 10

VMEM_LIMIT = 56 * 1024 * 1024
MESH = pl.DeviceIdType.MESH
ANY = pl.BlockSpec(memory_space=pl.ANY)


def _tile(n, pref, mult=128):
    t = min(pref, n)
    while t > mult and (n % t or t % mult):
        t -= mult
    assert n % t == 0, (n, pref)
    return t


def _call(body, *, name, out_shape, grid=(), in_specs=None, out_specs=None, scratch_shapes=(),
          aliases=None, grid_spec=None):
    kw = {}
    if grid_spec is not None:
        kw["grid_spec"] = grid_spec
        ngrid = len(grid_spec.grid)
    else:
        kw.update(grid=grid, in_specs=in_specs, out_specs=out_specs, scratch_shapes=list(scratch_shapes))
        ngrid = len(grid)
    params = pltpu.CompilerParams(dimension_semantics=("arbitrary",) * ngrid, vmem_limit_bytes=VMEM_LIMIT)
    return pl.pallas_call(body, name=name, out_shape=out_shape, compiler_params=params,
                          input_output_aliases=aliases or {}, **kw)


_DIMS = {"nn": (((1,), (0,)), ((), ())), "nt": (((1,), (1,)), ((), ())), "tn": (((0,), (0,)), ((), ()))}


def _mxu(a, b, mode):
    return lax.dot_general(a.astype(MXU_DTYPE), b.astype(MXU_DTYPE), _DIMS[mode], preferred_element_type=F32)


@functools.partial(jax.custom_vjp, nondiff_argnums=(2,))
def _dot(a, b, mode):
    return _mxu(a, b, mode)


def _dot_fwd(a, b, mode):
    return _mxu(a, b, mode), (a, b)


def _dot_bwd(mode, res, g):
    a, b = res
    if mode == "nn":
        return _mxu(g, b, "nt"), _mxu(a, g, "tn")
    if mode == "nt":
        return _mxu(g, b, "nn"), _mxu(g, a, "tn")
    return _mxu(b, g, "nt"), _mxu(a, g, "nn")


_dot.defvjp(_dot_fwd, _dot_bwd)


def _sig(z):
    return 1.0 / (1.0 + jnp.exp(-z))


def _silu(z):
    return z * _sig(z)


def _rms_rows(t, g):
    return t * lax.rsqrt(jnp.mean(t * t, axis=-1, keepdims=True) + EPS) * g


def _attn_block(q, k2, v2, gq, gk, first):
    qn = _rms_rows(q, gq)
    kn = _rms_rows(k2, gk)
    s = jnp.where(_band_mask(first), _dot(qn, kn, "nt") * (HEAD ** -0.5), NEG)
    m = lax.stop_gradient(jnp.max(s, axis=-1, keepdims=True))
    p = jnp.exp(s - m)
    den = jnp.sum(p, axis=-1, keepdims=True)
    o = _dot(p, v2, "nn") / den
    return o, m + jnp.log(den)


def _band_mask(first):
    a = lax.broadcasted_iota(jnp.int32, (BLK, 2 * BLK), 0)
    b = lax.broadcasted_iota(jnp.int32, (BLK, 2 * BLK), 1)
    return (b >= a) & (b <= a + BLK) & (b >= jnp.where(first, BLK, 0))


def _norm_parts(t):
    r = lax.rsqrt(jnp.mean(t * t, axis=-1, keepdims=True) + EPS)
    return r, t * r


def _norm_bwd(dn, g, r, th):
    dth = dn * g
    return r * (dth - th * jnp.mean(dth * th, axis=-1, keepdims=True)), jnp.sum(dn * th, axis=0, keepdims=True)


def _attn_block_bwd(q, k2, v2, gq, gk, first, do, o, lse, dlse):
    scale = HEAD ** -0.5
    rq, qh = _norm_parts(q)
    rk, kh = _norm_parts(k2)
    qn, kn = qh * gq, kh * gk
    s = jnp.where(_band_mask(first), _mxu(qn, kn, "nt") * scale, NEG)
    p = jnp.exp(s - lse)
    ds = p * (_mxu(do, v2, "nt") + (dlse - jnp.sum(do * o, axis=-1, keepdims=True))) * scale
    dq, dgq = _norm_bwd(_mxu(ds, kn, "nn"), gq, rq, qh)
    dk2, dgk = _norm_bwd(_mxu(ds, qn, "tn"), gk, rk, kh)
    return dq, dk2, _mxu(p, do, "tn"), dgq, dgk


def _combine(o1, o2, o3, l1, l2, l3, z):
    m = lax.stop_gradient(jnp.maximum(jnp.maximum(l1, l2), l3))
    e1, e2, e3 = jnp.exp(l1 - m), jnp.exp(l2 - m), jnp.exp(l3 - m)
    return (e1 * o1 + e2 * o2 + e3 * o3) / (e1 + e2 + e3) * _silu(z)


def _mem_block(q, z, kv, gq, gk):
    outs = []
    for h in range(MEM_HEADS):
        sl = slice(h * MEM_HD, (h + 1) * MEM_HD)
        qn = _rms_rows(q[:, sl], gq)
        kn = _rms_rows(kv[:, sl], gk)
        s = _dot(qn, kn, "nt") * (MEM_HD ** -0.5)
        m = lax.stop_gradient(jnp.max(s, axis=-1, keepdims=True))
        p = jnp.exp(s - m)
        den = jnp.sum(p, axis=-1, keepdims=True)
        outs.append(_dot(p, kv[:, MEMW + h * MEM_HD:MEMW + (h + 1) * MEM_HD], "nn") / den)
    return jnp.concatenate(outs, axis=-1) * _silu(z)


def _cast(w, name):
    R, C = w.shape
    tr, tc = _tile(R, 512, 8), _tile(C, 2176)

    def body(w_ref, o_ref):
        o_ref[...] = w_ref[...].astype(o_ref.dtype)

    spec = pl.BlockSpec((tr, tc), lambda i, j: (i, j))
    return _call(body, name=name, grid=(R // tr, C // tc), in_specs=[spec], out_specs=spec,
                 out_shape=jax.ShapeDtypeStruct((R, C), WIRE_DTYPE))(w)


def _place_shard(w, kind, pos, dtype, name, slot=0, into=None):
    R, C = w.shape
    tr, tc = _tile(R, 512, 8), _tile(C, 2176)
    nr, nc = R // tr, C // tc

    def body(pos_ref, w_ref, *rest):
        rest[-1][...] = w_ref[...].astype(rest[-1].dtype)

    if kind == "col":
        full, out = (R, 4 * C), pl.BlockSpec((tr, tc), lambda i, j, pos_ref: (i, pos_ref[slot] * nc + j))
    else:
        full, out = (4 * R, C), pl.BlockSpec((tr, tc), lambda i, j, pos_ref: (pos_ref[slot] * nr + i, j))
    in_specs, args = [pl.BlockSpec((tr, tc), lambda i, j, pos_ref: (i, j))], [pos, w]
    if into is not None:
        in_specs.append(ANY)
        args.append(into)
    spec = pltpu.PrefetchScalarGridSpec(num_scalar_prefetch=1, grid=(nr, nc), in_specs=in_specs, out_specs=out)
    return _call(body, name=name, grid_spec=spec, out_shape=jax.ShapeDtypeStruct(full, dtype),
                 aliases={} if into is None else {2: 0})(*args)


def _matmul(a, b, mode, out_dtype, *, name, tm=512, tn=512, tk=512):
    if mode == "nn":
        (M, K), (_, N) = a.shape, b.shape
    elif mode == "nt":
        (M, K), (N, _) = a.shape, b.shape
    else:
        (K, M), (_, N) = a.shape, b.shape
    tm, tn, tk = _tile(M, tm), _tile(N, tn), _tile(K, tk)
    nk = K // tk

    def body(a_ref, b_ref, o_ref, *acc):
        part = lax.dot_general(a_ref[...], b_ref[...], _DIMS[mode], preferred_element_type=F32)
        if nk == 1:
            o_ref[...] = part.astype(o_ref.dtype)
            return
        acc_ref, = acc
        k = pl.program_id(2)

        @pl.when(k == 0)
        def _():
            acc_ref[...] = part

        @pl.when(k > 0)
        def _():
            acc_ref[...] += part

        @pl.when(k == nk - 1)
        def _():
            o_ref[...] = acc_ref[...].astype(o_ref.dtype)

    a_spec = pl.BlockSpec((tk, tm), lambda i, j, k: (k, i)) if mode == "tn" else pl.BlockSpec((tm, tk), lambda i, j, k: (i, k))
    b_spec = pl.BlockSpec((tn, tk), lambda i, j, k: (j, k)) if mode == "nt" else pl.BlockSpec((tk, tn), lambda i, j, k: (k, j))
    return _call(body, name=name, grid=(M // tm, N // tn, nk), in_specs=[a_spec, b_spec],
                 out_specs=pl.BlockSpec((tm, tn), lambda i, j, k: (i, j)),
                 out_shape=jax.ShapeDtypeStruct((M, N), out_dtype),
                 scratch_shapes=[] if nk == 1 else [pltpu.VMEM((tm, tn), F32)])(a, b)


def _rms_fwd(x, g, name):
    R, D = x.shape
    tr = _tile(R, 512)

    def body(x_ref, g_ref, o_ref, t_ref):
        y = _rms_rows(x_ref[...], g_ref[...])
        o_ref[...] = y.astype(o_ref.dtype)
        t_ref[...] = y.T.astype(t_ref.dtype)

    row = pl.BlockSpec((tr, D), lambda i: (i, 0))
    return _call(body, name=name, grid=(R // tr,), in_specs=[row, pl.BlockSpec((1, D), lambda i: (0, 0))],
                 out_specs=[row, pl.BlockSpec((D, tr), lambda i: (0, i))],
                 out_shape=[jax.ShapeDtypeStruct((R, D), MXU_DTYPE), jax.ShapeDtypeStruct((D, R), MXU_DTYPE)])(x, g)


def _rms_bwd(x, dh, g, dy, name):
    R, D = x.shape
    tr = _tile(R, 256)
    with_dx = dy is not None

    def body(*refs):
        if with_dx:
            x_ref, dh_ref, g_ref, dy_ref, dx_ref, dg_ref = refs
        else:
            x_ref, dh_ref, g_ref, dg_ref = refs
        xv, dhv = x_ref[...], dh_ref[...]
        r = lax.rsqrt(jnp.mean(xv * xv, axis=-1, keepdims=True) + EPS)
        xh = xv * r

        @pl.when(pl.program_id(0) == 0)
        def _():
            dg_ref[...] = jnp.zeros_like(dg_ref)

        dg_ref[...] += jnp.sum(dhv * xh, axis=0, keepdims=True)
        if with_dx:
            dxh = dhv * g_ref[...]
            dx_ref[...] = dy_ref[...] + r * (dxh - xh * jnp.mean(dxh * xh, axis=-1, keepdims=True))

    row = pl.BlockSpec((tr, D), lambda i: (i, 0))
    vec = pl.BlockSpec((1, D), lambda i: (0, 0))
    dg_shape = jax.ShapeDtypeStruct((1, D), F32)
    if with_dx:
        return _call(body, name=name, grid=(R // tr,), in_specs=[row, row, vec, row], out_specs=[row, vec],
                     out_shape=[jax.ShapeDtypeStruct((R, D), F32), dg_shape])(x, dh, g, dy)
    return None, _call(body, name=name, grid=(R // tr,), in_specs=[row, row, vec], out_specs=vec,
                       out_shape=dg_shape)(x, dh, g)


def _attn_geom(g, d):
    hc = HPG if d == 1 else 1
    cw = hc * HEAD
    cq, ck, cv = (Q0 + g * GW) // cw, (K0 + g * GW) // cw, (V0 + g * GW) // cw
    return (1, BLK * d, cw), hc, HPG // hc, cq, ck, cv


def _rows(ref, r, d, sl):
    if d == 1:
        return ref[0, :, sl]
    return ref.at[0][pl.ds(r, BLK, stride=d), sl]


def _set_rows(ref, r, d, sl, val):
    if d == 1:
        ref[0, :, sl] = val
    else:
        ref.at[0][pl.ds(r, BLK, stride=d), sl] = val


def _stage_rows(ref, r, d, sl, val):
    if d == 1:
        ref[:, sl] = val
    else:
        ref[pl.ds(r, BLK, stride=d), sl] = val


def _proj_stages(blk, d):
    return [] if d == 1 else [pltpu.VMEM(blk[1:], F32)] * 5


def _proj_rows(refs, stages, d):
    if d == 1:
        return [lambda r, sl, ref=ref: ref[0, :, sl].astype(F32) for ref in refs]
    for ref, stage in zip(refs, stages):
        stage[...] = ref[0].astype(F32)
    return [lambda r, sl, stage=stage: stage[pl.ds(r, BLK, stride=d), sl] for stage in stages]


def _attn_fwd(proj3, gq, gk, g, d):
    Bl, S, _ = proj3.shape
    blk, hc, ncb, cq, ck, cv = _attn_geom(g, d)
    nb = S // blk[1]

    def body(q_ref, kp_ref, kc_ref, vp_ref, vc_ref, gq_ref, gk_ref, o_ref, lse_ref, *stages):
        first = pl.program_id(2) == 0
        q, kp, kc, vp, vc = _proj_rows((q_ref, kp_ref, kc_ref, vp_ref, vc_ref), stages, d)
        for r in range(d):
            for h in range(hc):
                sl = slice(h * HEAD, (h + 1) * HEAD)
                k2 = jnp.concatenate([kp(r, sl), kc(r, sl)], axis=0)
                v2 = jnp.concatenate([vp(r, sl), vc(r, sl)], axis=0)
                o, lse = _attn_block(q(r, sl), k2, v2, gq_ref[...], gk_ref[...], first)
                _set_rows(o_ref, r, d, sl, o)
                _set_rows(lse_ref, r, d, sl, jnp.broadcast_to(lse, (BLK, HEAD)))

    def cur(c0):
        return pl.BlockSpec(blk, lambda b, j, i: (b, i, c0 + j))

    def prev(c0):
        return pl.BlockSpec(blk, lambda b, j, i: (b, jnp.maximum(i - 1, 0), c0 + j))

    vec = pl.BlockSpec((1, HEAD), lambda b, j, i: (0, 0))
    out = pl.BlockSpec(blk, lambda b, j, i: (b, i, j))
    shp = jax.ShapeDtypeStruct((Bl, S, GW), F32)
    return _call(body, name=f"attn_fwd_g{g}", grid=(Bl, ncb, nb),
                 in_specs=[cur(cq), prev(ck), cur(ck), prev(cv), cur(cv), vec, vec],
                 out_specs=[out, out], out_shape=[shp, shp], scratch_shapes=_proj_stages(blk, d),
                 )(proj3, proj3, proj3, proj3, proj3, gq, gk)


def _attn_bwd(proj3, gq, gk, o3, l3, do3, dl3, g, d):
    Bl, S, _ = proj3.shape
    blk, hc, ncb, cq, ck, cv = _attn_geom(g, d)
    nb = S // blk[1]

    def body(q_ref, kp_ref, kc_ref, vp_ref, vc_ref, gq_ref, gk_ref, o_ref, l_ref, do_ref, dl_ref,
             dq_ref, dk_ref, dv_ref, dgq_ref, dgk_ref, ck_ref, cv_ref, sq_ref, sk_ref, sv_ref, *stages):
        i = pl.program_id(2)
        first = i == 0

        @pl.when((pl.program_id(0) == 0) & (pl.program_id(1) == 0) & first)
        def _():
            dgq_ref[...] = jnp.zeros_like(dgq_ref)
            dgk_ref[...] = jnp.zeros_like(dgk_ref)

        @pl.when(first)
        def _():
            ck_ref[...] = jnp.zeros_like(ck_ref)
            cv_ref[...] = jnp.zeros_like(cv_ref)

        @pl.when(i < nb)
        def _():
            dgq, dgk = jnp.zeros((1, HEAD), F32), jnp.zeros((1, HEAD), F32)
            q, kp, kc, vp, vc = _proj_rows((q_ref, kp_ref, kc_ref, vp_ref, vc_ref), stages, d)
            for r in range(d):
                rs = slice(r * BLK, (r + 1) * BLK)
                for h in range(hc):
                    sl = slice(h * HEAD, (h + 1) * HEAD)
                    k2 = jnp.concatenate([kp(r, sl), kc(r, sl)], axis=0)
                    v2 = jnp.concatenate([vp(r, sl), vc(r, sl)], axis=0)
                    dq, dk2, dv2, a, b = _attn_block_bwd(
                        q(r, sl), k2, v2, gq_ref[...], gk_ref[...], first, _rows(do_ref, r, d, sl),
                        _rows(o_ref, r, d, sl), _rows(l_ref, r, d, sl)[:, :1], _rows(dl_ref, r, d, sl)[:, :1])
                    _stage_rows(sq_ref, r, d, sl, dq)
                    _stage_rows(sk_ref, r, d, sl, ck_ref[rs, sl] + dk2[:BLK])
                    _stage_rows(sv_ref, r, d, sl, cv_ref[rs, sl] + dv2[:BLK])
                    ck_ref[rs, sl] = dk2[BLK:]
                    cv_ref[rs, sl] = dv2[BLK:]
                    dgq, dgk = dgq + a, dgk + b
            dgq_ref[...] += dgq
            dgk_ref[...] += dgk
            dq_ref[0] = sq_ref[...].astype(dq_ref.dtype)

        @pl.when(i == nb)
        def _():
            for r in range(d):
                rs = slice(r * BLK, (r + 1) * BLK)
                _stage_rows(sk_ref, r, d, slice(None), ck_ref[rs, :])
                _stage_rows(sv_ref, r, d, slice(None), cv_ref[rs, :])

        dk_ref[0] = sk_ref[...].astype(dk_ref.dtype)
        dv_ref[0] = sv_ref[...].astype(dv_ref.dtype)

    def cur(c0):
        return pl.BlockSpec(blk, lambda b, j, i: (b, jnp.minimum(i, nb - 1), c0 + j))

    def prev(c0):
        return pl.BlockSpec(blk, lambda b, j, i: (b, jnp.clip(i - 1, 0, nb - 1), c0 + j))

    vec = pl.BlockSpec((1, HEAD), lambda b, j, i: (0, 0))
    at_q = pl.BlockSpec(blk, lambda b, j, i: (b, jnp.minimum(i, nb - 1), j))
    at_k = pl.BlockSpec(blk, lambda b, j, i: (b, jnp.maximum(i - 1, 0), j))
    shp = jax.ShapeDtypeStruct((Bl, S, GW), MXU_DTYPE)
    gshp = jax.ShapeDtypeStruct((1, HEAD), F32)
    return _call(body, name=f"attn_bwd_g{g}", grid=(Bl, ncb, nb + 1),
                 in_specs=[cur(cq), prev(ck), cur(ck), prev(cv), cur(cv), vec, vec, at_q, at_q, at_q, at_q],
                 out_specs=[at_q, at_k, at_k, vec, vec], out_shape=[shp, shp, shp, gshp, gshp],
                 scratch_shapes=[pltpu.VMEM(blk[1:], F32)] * 5 + _proj_stages(blk, d),
                 )(proj3, proj3, proj3, proj3, proj3, gq, gk, o3, l3, do3, dl3)


def _combine_fwd(os, ls, proj2):
    T = proj2.shape[0]
    tr = _tile(T, 512)

    def body(o1, o2, o3, l1, l2, l3, z, a_ref):
        a_ref[...] = _combine(o1[...], o2[...], o3[...], l1[...], l2[...], l3[...], z[...].astype(F32)).astype(a_ref.dtype)

    row = pl.BlockSpec((tr, GW), lambda i: (i, 0))
    return _call(body, name="combine_fwd", grid=(T // tr,),
                 in_specs=[row] * 6 + [pl.BlockSpec((tr, GW), lambda i: (i, ZA // GW))], out_specs=row,
                 out_shape=jax.ShapeDtypeStruct((T, GW), MXU_DTYPE))(*os, *ls, proj2)


def _combine_bwd(os, ls, proj2, da):
    T = proj2.shape[0]
    tr = _tile(T, 256)

    def body(o1, o2, o3, l1, l2, l3, z, da_ref, d1, d2, d3, e1, e2, e3, dz_ref):
        _, vjp = jax.vjp(_combine, o1[...], o2[...], o3[...], l1[...], l2[...], l3[...], z[...].astype(F32))
        go1, go2, go3, gl1, gl2, gl3, gz = vjp(da_ref[...])
        d1[...], d2[...], d3[...] = go1, go2, go3
        dz_ref[...] = gz.astype(dz_ref.dtype)
        for ref, gl in ((e1, gl1), (e2, gl2), (e3, gl3)):
            for h in range(HPG):
                sl = slice(h * HEAD, (h + 1) * HEAD)
                ref[:, sl] = jnp.broadcast_to(jnp.sum(gl[:, sl], axis=-1, keepdims=True), (tr, HEAD))

    row = pl.BlockSpec((tr, GW), lambda i: (i, 0))
    f = jax.ShapeDtypeStruct((T, GW), F32)
    outs = _call(body, name="combine_bwd", grid=(T // tr,),
                 in_specs=[row] * 6 + [pl.BlockSpec((tr, GW), lambda i: (i, ZA // GW)), row],
                 out_specs=[row] * 7, out_shape=[f] * 6 + [jax.ShapeDtypeStruct((T, GW), MXU_DTYPE)],
                 )(*os, *ls, proj2, da)
    return outs[:3], outs[3:6], outs[6]


def _shift_down(u, j, t):
    return jnp.where(t >= j, pltpu.roll(u, j, 0), 0.0)


def _shift_up(u, j, t):
    n = u.shape[0]
    return jnp.where(t < n - j, pltpu.roll(u, n - j, 0), 0.0)


def _conv_specs(Bl, S, cw):
    def sec(c0):
        return pl.BlockSpec((1, S, cw), lambda j, b: (b, 0, c0 // cw + j))
    return [sec(CB), sec(CC), sec(CV), sec(ZC)], pl.BlockSpec((3, cw), lambda j, b: (0, j))


def _conv_fwd(proj3, conv_w):
    Bl, S, _ = proj3.shape
    cw = 256
    secs, wspec = _conv_specs(Bl, S, cw)

    def body(b_ref, c_ref, v_ref, z_ref, w_ref, o_ref):
        t = lax.broadcasted_iota(jnp.int32, (S, cw), 0)
        u = c_ref[0].astype(F32) * v_ref[0].astype(F32)
        y = w_ref[0:1, :] * u + w_ref[1:2, :] * _shift_down(u, 1, t) + w_ref[2:3, :] * _shift_down(u, 2, t)
        o_ref[0] = (b_ref[0].astype(F32) * y * _silu(z_ref[0].astype(F32))).astype(o_ref.dtype)

    return _call(body, name="conv_fwd", grid=(CONVW // cw, Bl), in_specs=secs + [wspec],
                 out_specs=pl.BlockSpec((1, S, cw), lambda j, b: (b, 0, j)),
                 out_shape=jax.ShapeDtypeStruct((Bl, S, CONVW), MXU_DTYPE))(proj3, proj3, proj3, proj3, conv_w)


def _conv_bwd(proj3, conv_w, dcc3):
    Bl, S, _ = proj3.shape
    cw = 256
    secs, wspec = _conv_specs(Bl, S, cw)

    def body(b_ref, c_ref, v_ref, z_ref, w_ref, d_ref, db_ref, dc_ref, dv_ref, dz_ref, dw_ref):
        t = lax.broadcasted_iota(jnp.int32, (S, cw), 0)
        bv, cv, vv, zv = (r[0].astype(F32) for r in (b_ref, c_ref, v_ref, z_ref))
        dv = d_ref[0]
        u = cv * vv
        u1, u2 = _shift_down(u, 1, t), _shift_down(u, 2, t)
        y = w_ref[0:1, :] * u + w_ref[1:2, :] * u1 + w_ref[2:3, :] * u2
        sg = _sig(zv)
        sz = zv * sg
        gy = dv * bv * sz
        db_ref[0] = (dv * y * sz).astype(db_ref.dtype)
        dz_ref[0] = (dv * bv * y * sg * (1.0 + zv * (1.0 - sg))).astype(dz_ref.dtype)
        du = w_ref[0:1, :] * gy + w_ref[1:2, :] * _shift_up(gy, 1, t) + w_ref[2:3, :] * _shift_up(gy, 2, t)
        dc_ref[0] = (du * vv).astype(dc_ref.dtype)
        dv_ref[0] = (du * cv).astype(dv_ref.dtype)

        @pl.when(pl.program_id(1) == 0)
        def _():
            dw_ref[...] = jnp.zeros_like(dw_ref)

        dw_ref[0:1, :] += jnp.sum(gy * u, axis=0, keepdims=True)
        dw_ref[1:2, :] += jnp.sum(gy * u1, axis=0, keepdims=True)
        dw_ref[2:3, :] += jnp.sum(gy * u2, axis=0, keepdims=True)

    blk = pl.BlockSpec((1, S, cw), lambda j, b: (b, 0, j))
    shp = jax.ShapeDtypeStruct((Bl, S, CONVW), MXU_DTYPE)
    return _call(body, name="conv_bwd", grid=(CONVW // cw, Bl), in_specs=secs + [wspec, blk],
                 out_specs=[blk] * 4 + [wspec], out_shape=[shp] * 4 + [jax.ShapeDtypeStruct((3, CONVW), F32)],
                 )(proj3, proj3, proj3, proj3, conv_w, dcc3)


def _mem_specs(S, tq):
    q = pl.BlockSpec((1, tq, MEMW), lambda b, j: (b, j, MQ // MEMW))
    z = pl.BlockSpec((1, tq, MEMW), lambda b, j: (b, j, ZM // MEMW))
    kv = pl.BlockSpec((1, MEM_HD, 2 * MEMW), lambda b, j: (b, 0, 0))
    vec = pl.BlockSpec((1, MEM_HD), lambda b, j: (0, 0))
    blk = pl.BlockSpec((1, tq, MEMW), lambda b, j: (b, j, 0))
    return q, z, kv, vec, blk


def _mem_fwd(proj3, mkv3, gq, gk):
    Bl, S, _ = proj3.shape
    tq = _tile(S, 512)
    q, z, kv, vec, blk = _mem_specs(S, tq)

    def body(q_ref, z_ref, kv_ref, gq_ref, gk_ref, o_ref):
        o_ref[0] = _mem_block(q_ref[0].astype(F32), z_ref[0].astype(F32), kv_ref[0], gq_ref[...],
                              gk_ref[...]).astype(o_ref.dtype)

    return _call(body, name="mem_fwd", grid=(Bl, S // tq), in_specs=[q, z, kv, vec, vec], out_specs=blk,
                 out_shape=jax.ShapeDtypeStruct((Bl, S, MEMW), MXU_DTYPE))(proj3, proj3, mkv3, gq, gk)


def _mem_bwd(proj3, mkv3, gq, gk, dmo3):
    Bl, S, _ = proj3.shape
    tq = _tile(S, 256)
    q, z, kv, vec, blk = _mem_specs(S, tq)

    def body(q_ref, z_ref, kv_ref, gq_ref, gk_ref, d_ref, dq_ref, dz_ref, dkv_ref, dgq_ref, dgk_ref):
        _, vjp = jax.vjp(_mem_block, q_ref[0].astype(F32), z_ref[0].astype(F32), kv_ref[0], gq_ref[...], gk_ref[...])
        dq, dz, dkv, dgq, dgk = vjp(d_ref[0])
        dq_ref[0] = dq.astype(dq_ref.dtype)
        dz_ref[0] = dz.astype(dz_ref.dtype)
        j = pl.program_id(1)

        @pl.when(j == 0)
        def _():
            dkv_ref[0] = jnp.zeros_like(dkv)

        @pl.when((j == 0) & (pl.program_id(0) == 0))
        def _():
            dgq_ref[...] = jnp.zeros_like(dgq_ref)
            dgk_ref[...] = jnp.zeros_like(dgk_ref)

        dkv_ref[0] += dkv
        dgq_ref[...] += dgq
        dgk_ref[...] += dgk

    shp = jax.ShapeDtypeStruct((Bl, S, MEMW), MXU_DTYPE)
    gshp = jax.ShapeDtypeStruct((1, MEM_HD), F32)
    return _call(body, name="mem_bwd", grid=(Bl, S // tq), in_specs=[q, z, kv, vec, vec, blk],
                 out_specs=[blk, blk, kv, vec, vec],
                 out_shape=[shp, shp, jax.ShapeDtypeStruct(mkv3.shape, F32), gshp, gshp],
                 )(proj3, proj3, mkv3, gq, gk, dmo3)


def _merge_specs(T, D, tm, tn):
    def act(w):
        return pl.BlockSpec((tm, w), lambda i, n: (i, 0))

    def wsp(w):
        return pl.BlockSpec((w, tn), lambda i, n: (0, n))

    gates = [pl.BlockSpec((tm, tn), lambda i, n, k=k: (i, (G0 + k * D) // tn + n)) for k in range(3)]
    tile = pl.BlockSpec((tm, tn), lambda i, n: (i, n))
    return act, wsp, gates, tile


def _merge_fwd(a, cc, mo, wa, wc, wm, proj2):
    T, D = a.shape[0], wa.shape[1]
    tm, tn = _tile(T, 1024), _tile(D, 512)
    act, wsp, gates, tile = _merge_specs(T, D, tm, tn)

    def body(a_ref, c_ref, m_ref, wa_ref, wc_ref, wm_ref, g0, g1, g2, mg_ref, mt_ref, pa_ref, pc_ref, pm_ref):
        pa = jnp.dot(a_ref[...], wa_ref[...], preferred_element_type=F32)
        pc = jnp.dot(c_ref[...], wc_ref[...], preferred_element_type=F32)
        pm = jnp.dot(m_ref[...], wm_ref[...], preferred_element_type=F32)
        mg = _sig(g0[...].astype(F32)) * pa + _sig(g1[...].astype(F32)) * pc + _sig(g2[...].astype(F32)) * pm
        mg_ref[...] = mg.astype(mg_ref.dtype)
        mt_ref[...] = mg.T.astype(mt_ref.dtype)
        pa_ref[...] = pa.astype(pa_ref.dtype)
        pc_ref[...] = pc.astype(pc_ref.dtype)
        pm_ref[...] = pm.astype(pm_ref.dtype)

    shp = jax.ShapeDtypeStruct((T, D), MXU_DTYPE)
    return _call(body, name="merge_fwd", grid=(T // tm, D // tn),
                 in_specs=[act(GW), act(CONVW), act(MEMW), wsp(GW), wsp(CONVW), wsp(MEMW)] + gates,
                 out_specs=[tile, pl.BlockSpec((tn, tm), lambda i, n: (n, i)), tile, tile, tile],
                 out_shape=[shp, jax.ShapeDtypeStruct((D, T), MXU_DTYPE), shp, shp, shp],
                 )(a, cc, mo, wa, wc, wm, proj2, proj2, proj2)


def _merge_bwd(dyb, w_out, proj2, pa, pc, pm):
    T, D = dyb.shape
    tm, tn = _tile(T, 1024), _tile(D, 512)
    _, _, gates, tile = _merge_specs(T, D, tm, tn)

    def body(dy_ref, w_ref, g0, g1, g2, p0, p1, p2, dp0, dp1, dp2, dg0, dg1, dg2):
        dm = lax.dot_general(dy_ref[...], w_ref[...], _DIMS["nt"], preferred_element_type=F32)
        for g_ref, p_ref, dp_ref, dg_ref in ((g0, p0, dp0, dg0), (g1, p1, dp1, dg1), (g2, p2, dp2, dg2)):
            gt = _sig(g_ref[...].astype(F32))
            dp_ref[...] = (gt * dm).astype(dp_ref.dtype)
            dg_ref[...] = (dm * p_ref[...].astype(F32) * gt * (1.0 - gt)).astype(dg_ref.dtype)

    shp = jax.ShapeDtypeStruct((T, D), MXU_DTYPE)
    return _call(body, name="merge_bwd", grid=(T // tm, D // tn),
                 in_specs=[pl.BlockSpec((tm, D), lambda i, n: (i, 0)), pl.BlockSpec((tn, D), lambda i, n: (n, 0))]
                 + gates + [tile] * 3,
                 out_specs=[tile] * 6, out_shape=[shp] * 6)(dyb, w_out, proj2, proj2, proj2, pa, pc, pm)


def _out_loss(merged, w_out, x, tgt):
    T, D = x.shape
    tm = _tile(T, 512)

    def body(m_ref, w_ref, x_ref, t_ref, dy_ref, dyb_ref, loss_ref):
        err = x_ref[...] + jnp.dot(m_ref[...], w_ref[...], preferred_element_type=F32) - t_ref[...]
        dy = err * (1.0 / D)
        dy_ref[...] = dy
        dyb_ref[...] = dy.astype(dyb_ref.dtype)

        @pl.when(pl.program_id(0) == 0)
        def _():
            loss_ref[...] = jnp.zeros_like(loss_ref)

        loss_ref[...] += jnp.sum(err * err) * (0.5 / D)

    row = pl.BlockSpec((tm, D), lambda i: (i, 0))
    return _call(body, name="out_loss", grid=(T // tm,),
                 in_specs=[row, pl.BlockSpec((D, D), lambda i: (0, 0)), row, row],
                 out_specs=[row, row, pl.BlockSpec((1, 128), lambda i: (0, 0))],
                 out_shape=[jax.ShapeDtypeStruct((T, D), F32), jax.ShapeDtypeStruct((T, D), MXU_DTYPE),
                            jax.ShapeDtypeStruct((1, 128), F32)])(merged, w_out, x, tgt)


def _proj_chunk(hb, w, meta, j, nslots, half, buf, name):
    T, D = hb.shape
    Cs = w.shape[1] // 4
    tm, tn = _tile(T, 1024), _tile(Cs // 2, 2176)
    nh = Cs // 2 // tn
    per = nh if half is not None else 2 * nh

    def body(meta_ref, a_ref, b_ref, *rest):
        rest[-1][...] = jnp.dot(a_ref[...], b_ref[...], preferred_element_type=F32).astype(rest[-1].dtype)

    def tile(n, m):
        if half is None:
            return n % per
        return (m[4] if half == 0 else 1 - m[4]) * nh + n % per

    in_specs = [pl.BlockSpec((tm, D), lambda n, i, m: (i, 0)),
                pl.BlockSpec((D, tn), lambda n, i, m: (0, (j + n // per) * 2 * nh + tile(n, m)))]
    args = [meta, hb, w]
    if buf is not None:
        in_specs.append(ANY)
        args.append(buf)
    spec = pltpu.PrefetchScalarGridSpec(
        num_scalar_prefetch=1, grid=(nslots * per, T // tm), in_specs=in_specs,
        out_specs=pl.BlockSpec((tm, tn), lambda n, i, m: (i, m[j + n // per] * 2 * nh + tile(n, m))))
    return _call(body, name=name, grid_spec=spec, out_shape=jax.ShapeDtypeStruct((T, 4 * Cs), PROJ_DTYPE),
                 aliases={} if buf is None else {3: 0})(*args)


def _norms(x, mem, norm_g, mem_norm_g):
    D = x.shape[-1]
    hb, hbt = _rms_fwd(x.reshape(-1, D), norm_g.reshape(1, D), "rms_x")
    mhb, _ = _rms_fwd(mem.reshape(-1, D), mem_norm_g.reshape(1, D), "rms_mem")
    return hb, hbt, mhb


def _attention_fwd(proj2, Bl, gq_all, gk_all):
    T, IN = proj2.shape
    proj3 = proj2.reshape(Bl, T // Bl, IN)
    os, ls = [], []
    for g, d in enumerate(DILATIONS):
        o, l = _attn_fwd(proj3, gq_all[g:g + 1], gk_all[g:g + 1], g, d)
        os.append(o.reshape(T, GW))
        ls.append(l.reshape(T, GW))
    return os, ls, _combine_fwd(os, ls, proj2)


def _conv_branch_fwd(proj2, Bl, conv_w):
    T, IN = proj2.shape
    return _conv_fwd(proj2.reshape(Bl, T // Bl, IN), conv_w).reshape(T, CONVW)


def _weight_grads(x, mem, tgt, norm_g, mem_norm_g, gq_all, gk_all, conv_w, mem_gq, mem_gk, W, pre, early=None):
    Bl, S, D = x.shape
    T = Bl * S
    hb, hbt, mhb, proj2, os, ls, a, cc = pre
    IN = proj2.shape[1]
    proj3 = proj2.reshape(Bl, S, IN)
    x2, tgt2 = x.reshape(T, D), tgt.reshape(T, D)
    mem2 = mem.reshape(-1, D)
    ng, mng = norm_g.reshape(1, D), mem_norm_g.reshape(1, D)
    mgq, mgk = mem_gq.reshape(1, MEM_HD), mem_gk.reshape(1, MEM_HD)
    gqs = [gq_all[g:g + 1] for g in range(NGROUP)]
    gks = [gk_all[g:g + 1] for g in range(NGROUP)]

    mkv = _matmul(mhb, W["mem_w_kv"], "nn", F32, name="mem_kv", tm=512, tn=1024, tk=D)
    mkv3 = mkv.reshape(Bl, -1, 2 * MEMW)
    mo = _mem_fwd(proj3, mkv3, mgq, mgk).reshape(T, MEMW)
    merged, mergedt, pa, pc, pm = _merge_fwd(a, cc, mo, W["w_br_attn"], W["w_br_conv"], W["w_br_mem"], proj2)
    dy, dyb, loss = _out_loss(merged, W["w_out"], x2, tgt2)

    G = {}
    G["w_out"] = _matmul(mergedt, dyb, "nn", WIRE_DTYPE, name="dw_out", tm=1024, tn=512, tk=T)
    dpa, dpc, dpm, dg0, dg1, dg2 = _merge_bwd(dyb, W["w_out"], proj2, pa, pc, pm)
    G["w_br_attn"] = _matmul(a, dpa, "tn", WIRE_DTYPE, name="dw_br_attn", tm=512, tn=1024, tk=512)
    G["w_br_conv"] = _matmul(cc, dpc, "tn", WIRE_DTYPE, name="dw_br_conv", tm=1024, tn=1024, tk=512)
    G["w_br_mem"] = _matmul(mo, dpm, "tn", WIRE_DTYPE, name="dw_br_mem", tm=1024, tn=1024, tk=512)
    da = _matmul(dpa, W["w_br_attn"], "nt", F32, name="d_attn", tm=1024, tn=512, tk=D)
    dcc = _matmul(dpc, W["w_br_conv"], "nt", F32, name="d_conv", tm=1024, tn=1024, tk=D)
    dmo = _matmul(dpm, W["w_br_mem"], "nt", F32, name="d_mem", tm=1024, tn=1024, tk=D)
    dmq, dzm, dmkv3, dmgq, dmgk = _mem_bwd(proj3, mkv3, mgq, mgk, dmo.reshape(Bl, S, MEMW))
    dmkv = _cast(dmkv3.reshape(-1, 2 * MEMW), "cast_dmkv")
    G["mem_w_kv"] = _matmul(mhb, dmkv, "tn", WIRE_DTYPE, name="dw_mem_kv", tm=1024, tn=1024, tk=512)
    early_state, da = (None, da) if early is None else early(G, da)
    dmh = _matmul(dmkv, W["mem_w_kv"], "nt", F32, name="d_memh", tm=512, tn=1024, tk=2 * MEMW)
    _, dmng = _rms_bwd(mem2, dmh, mng, None, "rms_mem_bwd")

    dos, dls, dza = _combine_bwd(os, ls, proj2, da)
    dqs, dks, dvs, dgq, dgk = [], [], [], [], []
    for g, d in enumerate(DILATIONS):
        dq, dk, dv, gq_g, gk_g = _attn_bwd(proj3, gqs[g], gks[g], os[g].reshape(Bl, S, GW), ls[g].reshape(Bl, S, GW),
                                           dos[g].reshape(Bl, S, GW), dls[g].reshape(Bl, S, GW), g, d)
        dqs.append(dq.reshape(T, GW).astype(MXU_DTYPE))
        dks.append(dk.reshape(T, GW).astype(MXU_DTYPE))
        dvs.append(dv.reshape(T, GW).astype(MXU_DTYPE))
        dgq.append(gq_g)
        dgk.append(gk_g)
    dcb, dcc_, dcv, dzc, dconv_w = _conv_bwd(proj3, conv_w, dcc.reshape(Bl, S, CONVW))

    dproj = jnp.concatenate(dqs + dks + dvs + [dza] + [t.reshape(T, CONVW) for t in (dcb, dcc_, dcv, dzc)]
                            + [dmq.reshape(T, MEMW), dzm.reshape(T, MEMW), dg0, dg1, dg2], axis=1)
    small = [loss, None, dmng] + dgq + dgk + [dconv_w.reshape(1, 3 * CONVW), dmgq, dmgk]
    return G, (dproj, x2, ng, dy, small), early_state


def _dw_in_half(hbt, dproj, pos, own, name):
    D, T = hbt.shape
    IN = dproj.shape[1]
    R, tn = D // 2, _tile(IN, 512)

    def body(pos_ref, a_ref, b_ref, o_ref):
        o_ref[...] = jnp.dot(a_ref[...], b_ref[...], preferred_element_type=F32).astype(o_ref.dtype)

    spec = pltpu.PrefetchScalarGridSpec(
        num_scalar_prefetch=1, grid=(IN // tn,),
        in_specs=[pl.BlockSpec((R, T), lambda j, p: (p[1] if own else 1 - p[1], 0)),
                  pl.BlockSpec((T, tn), lambda j, p: (0, j))],
        out_specs=pl.BlockSpec((R, tn), lambda j, p: (0, j)))
    return _call(body, name=name, grid_spec=spec, out_shape=jax.ShapeDtypeStruct((R, IN), WIRE_DTYPE))(pos, hbt, dproj)


def _d_h(dproj, w, order):
    T, IN = dproj.shape
    D, Cs = w.shape[0], IN // 4
    tm, tn = _tile(T, 1024), _tile(D, 1024)

    def body(order_ref, a_ref, b_ref, o_ref, acc_ref):
        part = lax.dot_general(a_ref[...], b_ref[...], _DIMS["nt"], preferred_element_type=F32)
        k = pl.program_id(2)

        @pl.when(k == 0)
        def _():
            acc_ref[...] = part

        @pl.when(k > 0)
        def _():
            acc_ref[...] += part

        @pl.when(k == 3)
        def _():
            o_ref[...] = acc_ref[...]

    spec = pltpu.PrefetchScalarGridSpec(
        num_scalar_prefetch=1, grid=(T // tm, D // tn, 4),
        in_specs=[pl.BlockSpec((tm, Cs), lambda i, n, k, o: (i, o[k])), pl.BlockSpec((tn, Cs), lambda i, n, k, o: (n, k))],
        out_specs=pl.BlockSpec((tm, tn), lambda i, n, k, o: (i, n)), scratch_shapes=[pltpu.VMEM((tm, tn), F32)])
    return _call(body, name="d_h", grid_spec=spec, out_shape=jax.ShapeDtypeStruct((T, D), F32))(order, dproj, w)


def _input_grad(rest, w_in, order):
    dproj, x2, ng, dy, small = rest
    dh = _d_h(dproj, w_in, order)
    grad_x, dng = _rms_bwd(x2, dh, ng, dy, "rms_x_bwd")
    small = [dng if t is None else t for t in small]
    return grad_x, jnp.concatenate(small, axis=1)


def _local_step(x, mem, tgt, norm_g, mem_norm_g, gq_all, gk_all, conv_w, mem_gq, mem_gk, W):
    hb, hbt, mhb = _norms(x, mem, norm_g, mem_norm_g)
    Cs = W["w_in"].shape[1] // 4
    shards = (0, 2, 1, 3)
    order = jnp.array(shards, dtype=jnp.int32)
    w_rel = jnp.concatenate([W["w_in"][:, s * Cs:(s + 1) * Cs] for s in shards], axis=1)
    meta = jnp.array(shards + (0,), dtype=jnp.int32)
    proj2 = _proj_chunk(hb, w_rel, meta, 0, 1, None, None, "proj_0")
    for j, nslots in ((1, 2), (3, 1)):
        for half in (1, 0):
            proj2 = _proj_chunk(hb, w_rel, meta, j, nslots, half, proj2, f"proj_{j}_{half}")
    pre = (hb, hbt, mhb, proj2, *_attention_fwd(proj2, x.shape[0], gq_all, gk_all),
           _conv_branch_fwd(proj2, x.shape[0], conv_w))
    G, rest, _ = _weight_grads(x, mem, tgt, norm_g, mem_norm_g, gq_all, gk_all, conv_w, mem_gq, mem_gk, W, pre)
    pos = jnp.zeros((2,), jnp.int32)
    G["w_in"] = jnp.concatenate([_dw_in_half(hbt, rest[0], pos, True, "dw_in_own"),
                                 _dw_in_half(hbt, rest[0], pos, False, "dw_in_sibling")], axis=0)
    grad_x, small = _input_grad(rest, w_rel, order)
    return grad_x.reshape(x.shape), G, small


BIG = (("w_in", "col"), ("mem_w_kv", "row"), ("w_br_attn", "col"), ("w_br_conv", "col"),
       ("w_br_mem", "col"), ("w_out", "row"))


def _coords():
    return lax.axis_index("x"), lax.axis_index("y"), lax.axis_index("c")


def _other_chips(x, y):
    return [(1 - x, y), (x, 1 - y), (1 - x, 1 - y)]


def _half(ref, kind, c):
    R, C = ref.shape
    if kind == "col":
        return ref.at[pl.ds(c * (R // 2), R // 2), :]
    return ref.at[:, pl.ds(c * (C // 2), C // 2)]


def _shard(ref, kind, s):
    R, C = ref.shape
    if kind == "col":
        return ref.at[:, pl.ds(s * (C // 4), C // 4)]
    return ref.at[pl.ds(s * (R // 4), R // 4), :]


def _piece(ref, kind, s, c):
    R, C = ref.shape
    if kind == "col":
        return ref.at[pl.ds(c * (R // 2), R // 2), pl.ds(s * (C // 4), C // 4)]
    return ref.at[pl.ds(s * (R // 4), R // 4), pl.ds(c * (C // 2), C // 2)]


def _remote(src, dst, sems_s, sems_r, k, dev):
    return pltpu.make_async_remote_copy(src_ref=src, dst_ref=dst, send_sem=sems_s.at[k], recv_sem=sems_r.at[k],
                                        device_id=dev, device_id_type=MESH)


HBM = pl.BlockSpec(memory_space=pltpu.HBM)
SEM = pl.BlockSpec(memory_space=pltpu.SEMAPHORE)
EFFECT = pltpu.SideEffectType.DATAFLOW_SIDE_EFFECTING


def _hbm(a):
    return pltpu.with_memory_space_constraint(a, pltpu.HBM)


def _start_copies(name, arrays, ncopies, make):
    n = len(arrays)

    def body(*refs):
        for cp in make(refs[:n], refs[n], refs[n + 1]):
            cp.start()

    outs = pl.pallas_call(
        body, name=name,
        out_shape=(pltpu.SemaphoreType.DMA((ncopies,)), pltpu.SemaphoreType.DMA((ncopies,)),
                   *[jax.ShapeDtypeStruct(t.shape, t.dtype) for t in arrays]),
        in_specs=[HBM] * n, out_specs=(SEM, SEM, *([HBM] * n)),
        input_output_aliases={i: i + 2 for i in range(n)},
        compiler_params=pltpu.CompilerParams(has_side_effects=EFFECT),
    )(*[_hbm(t) for t in arrays])
    return outs[0], outs[1], list(outs[2:])


def _wait_copies(name, send, recv, arrays, make, after):
    n = len(arrays)

    def body(*refs):
        for cp in make(refs[:n], refs[n], refs[n + 1]):
            cp.wait_send()
            cp.wait_recv()

    outs = pl.pallas_call(
        body, name=name, out_shape=[jax.ShapeDtypeStruct(t.shape, t.dtype) for t in arrays],
        in_specs=[HBM] * n + [SEM, SEM, ANY], out_specs=[HBM] * n,
        input_output_aliases={i: i for i in range(n)},
        compiler_params=pltpu.CompilerParams(has_side_effects=EFFECT),
    )(*arrays, send, recv, after)
    return list(outs)


def _w_in_copies(relations):
    def make(refs, send, recv):
        x, y, c = _coords()
        me = 2 * x + y
        chips = _other_chips(x, y)
        w, conv = refs[0], refs[1]
        cps = []
        for i, k in enumerate(relations):
            cps.append(_remote(_column_half(w, 0, c), _column_half(w, 1 + k, c), send, recv, 2 * i, (*chips[k], c)))
            mine = _shard(conv, "col", me)
            cps.append(_remote(mine, mine, send, recv, 2 * i + 1, (*chips[k], c)))
        return cps
    return make


def _column_half(w, slot, c):
    half = w.shape[1] // 8
    return w.at[:, pl.ds((2 * slot + c) * half, half)]


def _w_in_forward(relations):
    def make(refs, send, recv):
        x, y, c = _coords()
        cps = []
        for i, k in enumerate(relations):
            got = _column_half(refs[0], 1 + k, c)
            cps.append(_remote(got, got, send, recv, i, (x, y, 1 - c)))
        return cps
    return make


def _sibling_copy(refs, send, recv):
    x, y, c = _coords()
    return [_remote(refs[0], refs[1], send, recv, 0, (x, y, 1 - c))]


def _other_weight_copies(refs, send, recv):
    x, y, c = _coords()
    me = 2 * x + y
    cps = []
    for k, chip in enumerate(_other_chips(x, y)):
        for p, (_, kind) in enumerate(BIG[1:]):
            mine = _piece(refs[p], kind, me, c)
            cps.append(_remote(mine, mine, send, recv, 3 * p + k, (*chip, c)))
    return cps


def _other_weight_forward(refs, send, recv):
    x, y, c = _coords()
    cps = []
    for k, chip in enumerate(_other_chips(x, y)):
        s = 2 * chip[0] + chip[1]
        for p, (_, kind) in enumerate(BIG[1:]):
            got = _piece(refs[p], kind, s, c)
            cps.append(_remote(got, got, send, recv, 3 * p + k, (x, y, 1 - c)))
    return cps


def _share_copies(group):
    def make(refs, send, recv):
        x, y, c = _coords()
        cps = []
        for p, (_, kind) in enumerate(group):
            mine = _half(refs[p], kind, c)
            cps.append(_remote(mine, mine, send, recv, p, (x, y, 1 - c)))
        return cps
    return make


def _sibling_forward(name, arrays, ncp, halves):
    n = len(arrays)

    def body(*refs):
        outs = refs[n:2 * n]
        send, recv = refs[2 * n:]
        x, y, c = _coords()
        sib = (x, y, 1 - c)
        cps = [_remote(got, got, send, recv, i, sib) for i, got in enumerate(halves(outs, c))]
        for cp in cps:
            cp.start()
        for cp in cps:
            cp.wait_send()
        for i, got in enumerate(halves(outs, 1 - c)):
            _remote(got, got, send, recv, i, sib).wait_recv()

    return pl.pallas_call(
        body, name=name, out_shape=[jax.ShapeDtypeStruct(t.shape, t.dtype) for t in arrays],
        in_specs=[ANY] * n, out_specs=[ANY] * n, input_output_aliases={i: i for i in range(n)},
        scratch_shapes=[pltpu.SemaphoreType.DMA((ncp,)), pltpu.SemaphoreType.DMA((ncp,))],
    )(*arrays)


def _landed_halves(refs, c):
    return [_half(r, "col", c) for r in refs]


def _other_weight_halves(refs, c):
    x, y, _ = _coords()
    out = []
    for chip in _other_chips(x, y):
        s = 2 * chip[0] + chip[1]
        out += [_piece(refs[p], kind, s, c) for p, (_, kind) in enumerate(BIG[1:])]
    return out


def _sibling_exchange(grads, group, name):
    n = len(group)
    shapes = []
    for (_, kind), g in zip(group, grads):
        R, C = g.shape
        shapes.append(jax.ShapeDtypeStruct((R // 2, C) if kind == "col" else (R, C // 2), g.dtype))

    def body(*refs):
        ins, outs = refs[:n], refs[n:2 * n]
        send, recv = refs[2 * n:]
        x, y, c = _coords()
        sib = (x, y, 1 - c)
        cps = [_remote(_half(ins[p], group[p][1], 1 - c), outs[p], send, recv, p, sib) for p in range(n)]
        for cp in cps:
            cp.start()
        for cp in cps:
            cp.wait()

    return pl.pallas_call(
        body, name=name, out_shape=shapes, in_specs=[ANY] * n, out_specs=[ANY] * n,
        scratch_shapes=[pltpu.SemaphoreType.DMA((n,)), pltpu.SemaphoreType.DMA((n,))],
    )(*grads)


def _presum(g, got, kind, pos, name):
    R, C = got.shape
    tr, tc = _tile(R, 512, 16), _tile(C, 2048)
    nr, nc = R // tr, C // tc

    def body(pos_ref, a_ref, b_ref, o_ref):
        o_ref[...] = (a_ref[...].astype(F32) + b_ref[...].astype(F32)).astype(o_ref.dtype)

    blk = pl.BlockSpec((tr, tc), lambda i, j, pos_ref: (i, j))
    if g.shape == got.shape:
        mine = blk
    elif kind == "col":
        mine = pl.BlockSpec((tr, tc), lambda i, j, pos_ref: (pos_ref[1] * nr + i, j))
    else:
        mine = pl.BlockSpec((tr, tc), lambda i, j, pos_ref: (i, pos_ref[1] * nc + j))
    spec = pltpu.PrefetchScalarGridSpec(num_scalar_prefetch=1, grid=(nr, nc), in_specs=[mine, blk], out_specs=blk)
    return _call(body, name=name, grid_spec=spec, out_shape=jax.ShapeDtypeStruct((R, C), WIRE_DTYPE))(pos, g, got)


def _chip_copies(group):
    n = len(group)

    def make(refs, send, recv):
        x, y, c = _coords()
        cps = []
        for k, chip in enumerate(_other_chips(x, y)):
            s = 2 * chip[0] + chip[1]
            for p in range(n):
                cps.append(_remote(_shard(refs[p], group[p][1], s), refs[n + p].at[k], send, recv, 3 * p + k, (*chip, c)))
        return cps
    return make


def _landing_zones(pres, group):
    lands = []
    for (_, kind), g in zip(group, pres):
        R, C = g.shape
        lands.append(lax.empty((3, R, C // 4) if kind == "col" else (3, R // 4, C), g.dtype))
    return lands


def _exchange_start(G, group, pos, carry, tag, pres=None):
    n = len(group)
    if pres is None:
        parts = [G[name] for name, _ in group]
        got = _sibling_exchange(parts, group, "sibling_exchange_" + tag)
        pres = [_presum(parts[p], got[p], kind, pos, "presum_" + name) for p, (name, kind) in enumerate(group)]
    make = _chip_copies(group)
    send, recv, thru = _start_copies("chip_exchange_start_" + tag, [*pres, *_landing_zones(pres, group), carry], 3 * n, make)
    return (send, recv, thru[:2 * n], make, tag), thru[2 * n]


def _exchange_wait(state, after):
    send, recv, arrays, make, tag = state
    thru = _wait_copies("chip_exchange_wait_" + tag, send, recv, arrays, make, after)
    n = len(thru) // 2
    return thru[:n], thru[n:]


def _reduce_into_shard(slots, pre, kind, pos, name):
    K, R, C = slots.shape
    tr, tc = _tile(R, 512, 16), _tile(C, 2176)
    nr, nc = R // tr, C // tc

    def body(pos_ref, s_ref, p_ref, o_ref):
        acc = p_ref[...].astype(F32)
        for k in range(K):
            acc = acc + s_ref[k].astype(F32)
        o_ref[...] = acc

    if kind == "col":
        own = pl.BlockSpec((tr, tc), lambda i, j, pos_ref: (i, pos_ref[0] * nc + j))
        full, out = (2 * R, C), pl.BlockSpec((tr, tc), lambda i, j, pos_ref: (pos_ref[1] * nr + i, j))
    else:
        own = pl.BlockSpec((tr, tc), lambda i, j, pos_ref: (pos_ref[0] * nr + i, j))
        full, out = (R, 2 * C), pl.BlockSpec((tr, tc), lambda i, j, pos_ref: (i, pos_ref[1] * nc + j))
    spec = pltpu.PrefetchScalarGridSpec(
        num_scalar_prefetch=1, grid=(nr, nc),
        in_specs=[pl.BlockSpec((K, tr, tc), lambda i, j, pos_ref: (0, i, j)), own], out_specs=out)
    return _call(body, name=name, grid_spec=spec, out_shape=jax.ShapeDtypeStruct(full, F32))(pos, slots, pre)


def _share_reduced(reds):
    n = len(BIG)

    def body(*refs):
        outs = refs[n:2 * n]
        send, recv = refs[2 * n:]
        x, y, c = _coords()
        sib = (x, y, 1 - c)
        cps = []
        for p in range(n):
            mine = _half(outs[p], BIG[p][1], c)
            cps.append(_remote(mine, mine, send, recv, p, sib))
        for cp in cps:
            cp.start()
        for cp in cps:
            cp.wait_send()
        for p in range(n):
            got = _half(outs[p], BIG[p][1], 1 - c)
            _remote(got, got, send, recv, p, sib).wait_recv()

    return pl.pallas_call(
        body, name="share_reduced", out_shape=[jax.ShapeDtypeStruct(r.shape, r.dtype) for r in reds],
        in_specs=[ANY] * n, out_specs=[ANY] * n, input_output_aliases={p: p for p in range(n)},
        scratch_shapes=[pltpu.SemaphoreType.DMA((n,)), pltpu.SemaphoreType.DMA((n,))],
    )(*reds)


def _gather_small(pack):
    _, N = pack.shape

    def body(in_ref, out_ref, send, recv, loc):
        x, y, c = _coords()
        me = 4 * x + 2 * y + c
        own = pltpu.make_async_copy(in_ref, out_ref.at[me], loc)
        own.start()
        cps = []
        for k in range(1, 8):
            dev = (x ^ (k >> 2), y ^ ((k >> 1) & 1), c ^ (k & 1))
            cps.append(_remote(in_ref, out_ref.at[me], send, recv, k - 1, dev))
        for cp in cps:
            cp.start()
        for k in range(1, 8):
            src = 4 * (x ^ (k >> 2)) + 2 * (y ^ ((k >> 1) & 1)) + (c ^ (k & 1))
            _remote(in_ref, out_ref.at[src], send, recv, k - 1, (x, y, c)).wait_recv()
        for cp in cps:
            cp.wait_send()
        own.wait()

    return pl.pallas_call(
        body, name="gather_small", out_shape=jax.ShapeDtypeStruct((8, 1, N), pack.dtype),
        in_specs=[ANY], out_specs=ANY,
        scratch_shapes=[pltpu.SemaphoreType.DMA((7,)), pltpu.SemaphoreType.DMA((7,)), pltpu.SemaphoreType.DMA(())],
    )(pack)


def _sum_small(slots):
    K, _, N = slots.shape

    def body(s_ref, o_ref):
        acc = s_ref[0]
        for k in range(1, K):
            acc = acc + s_ref[k]
        o_ref[...] = acc

    return _call(body, name="sum_small", in_specs=[pl.BlockSpec(memory_space=pltpu.VMEM)],
                 out_specs=pl.BlockSpec(memory_space=pltpu.VMEM), out_shape=jax.ShapeDtypeStruct((1, N), F32))(slots)


def _adamw(w, g, m, v, name):
    R, C = w.shape
    tr, tc = _tile(R, 256, 8), _tile(C, 2176)

    def body(w_ref, g_ref, m_ref, v_ref, d_ref, nm_ref, nv_ref):
        gv = g_ref[...]
        nm = ADAM_B1 * m_ref[...] + (1.0 - ADAM_B1) * gv
        nv = ADAM_B2 * v_ref[...] + (1.0 - ADAM_B2) * gv * gv
        m_hat = nm / (1.0 - ADAM_B1 ** ADAM_STEP)
        v_hat = nv / (1.0 - ADAM_B2 ** ADAM_STEP)
        d_ref[...] = -ADAM_LR * (m_hat / (jnp.sqrt(v_hat) + ADAM_EPS) + ADAM_WD * w_ref[...])
        nm_ref[...] = nm
        nv_ref[...] = nv

    spec = pl.BlockSpec((tr, tc), lambda i, j: (i, j))
    shp = jax.ShapeDtypeStruct((R, C), F32)
    return _call(body, name=name, grid=(R // tr, C // tc), in_specs=[spec] * 4, out_specs=[spec] * 3,
                 out_shape=[shp] * 3)(w, g, m, v)


SMALL = ("norm_g", "mem_norm_g", "attn_q_norm", "attn_k_norm", "conv_w", "mem_q_norm", "mem_k_norm")
WEIGHTS = ("norm_g", "mem_norm_g", "w_in", "attn_q_norm", "attn_k_norm", "conv_w", "mem_w_kv", "mem_q_norm",
           "mem_k_norm", "w_br_attn", "w_br_conv", "w_br_mem", "w_out")


def kernel(x, mem, norm_g, mem_norm_g, w_in, attn_q_norm, attn_k_norm, conv_w, mem_w_kv, mem_q_norm, mem_k_norm, w_br_attn, w_br_conv, w_br_mem, w_out, loss_target, m_norm_g, m_mem_norm_g, m_w_in, m_attn_q_norm, m_attn_k_norm, m_conv_w, m_mem_w_kv, m_mem_q_norm, m_mem_k_norm, m_w_br_attn, m_w_br_conv, m_w_br_mem, m_w_out, v_norm_g, v_mem_norm_g, v_w_in, v_attn_q_norm, v_attn_k_norm, v_conv_w, v_mem_w_kv, v_mem_q_norm, v_mem_k_norm, v_w_br_attn, v_w_br_conv, v_w_br_mem, v_w_out):
    w = dict(norm_g=norm_g, mem_norm_g=mem_norm_g, w_in=w_in, attn_q_norm=attn_q_norm, attn_k_norm=attn_k_norm,
             conv_w=conv_w, mem_w_kv=mem_w_kv, mem_q_norm=mem_q_norm, mem_k_norm=mem_k_norm, w_br_attn=w_br_attn,
             w_br_conv=w_br_conv, w_br_mem=w_br_mem, w_out=w_out)
    m = dict(norm_g=m_norm_g, mem_norm_g=m_mem_norm_g, w_in=m_w_in, attn_q_norm=m_attn_q_norm,
             attn_k_norm=m_attn_k_norm, conv_w=m_conv_w, mem_w_kv=m_mem_w_kv, mem_q_norm=m_mem_q_norm,
             mem_k_norm=m_mem_k_norm, w_br_attn=m_w_br_attn, w_br_conv=m_w_br_conv, w_br_mem=m_w_br_mem, w_out=m_w_out)
    v = dict(norm_g=v_norm_g, mem_norm_g=v_mem_norm_g, w_in=v_w_in, attn_q_norm=v_attn_q_norm,
             attn_k_norm=v_attn_k_norm, conv_w=v_conv_w, mem_w_kv=v_mem_w_kv, mem_q_norm=v_mem_q_norm,
             mem_k_norm=v_mem_k_norm, w_br_attn=v_w_br_attn, w_br_conv=v_w_br_conv, w_br_mem=v_w_br_mem, w_out=v_w_out)
    Bl, _, D = x.shape
    cx, cy = lax.axis_index("x"), lax.axis_index("y")
    chip = 2 * cx + cy
    pos = jnp.stack([chip, lax.axis_index("c")]).astype(jnp.int32)
    order = jnp.stack([chip] + [2 * a + b for a, b in _other_chips(cx, cy)]).astype(jnp.int32)
    n = len(BIG)

    w_rel = _place_shard(w["w_in"], "col", jnp.zeros((1,), jnp.int32), WIRE_DTYPE, "place_w_in")
    conv_full = _place_shard(conv_w, "col", pos, F32, "place_conv_w")
    others = [_place_shard(w[name], kind, pos, WIRE_DTYPE, "place_" + name) for name, kind in BIG[1:]]
    hb, hbt, mhb = _norms(x, mem, norm_g, mem_norm_g)

    meta = jnp.concatenate([order, pos[1:]])
    near, near_fwd = _w_in_copies((0, 1)), _w_in_forward((0, 1))
    send, recv, (w_rel, conv_full) = _start_copies("gather_near_start", [w_rel, conv_full], 4, near)
    proj = _proj_chunk(hb, w_rel, meta, 0, 1, None, None, "proj_own")
    w_rel, conv_full, *others = _wait_copies("gather_near_wait", send, recv, [w_rel, conv_full, *others], near, proj)

    fsend, frecv, (w_rel,) = _start_copies("gather_near_forward_start", [w_rel], 2, near_fwd)
    far, far_fwd = _w_in_copies((2,)), _w_in_forward((2,))
    send, recv, (w_rel, conv_full) = _start_copies("gather_far_start", [w_rel, conv_full], 2, far)
    proj = _proj_chunk(hb, w_rel, meta, 1, 2, 0, proj, "proj_near_landed")
    w_rel, = _wait_copies("gather_near_forward_wait", fsend, frecv, [w_rel], near_fwd, proj)
    proj = _proj_chunk(hb, w_rel, meta, 1, 2, 1, proj, "proj_near_forwarded")
    w_rel, conv_full = _wait_copies("gather_far_wait", send, recv, [w_rel, conv_full], far, proj)

    fsend, frecv, (w_rel,) = _start_copies("gather_far_forward_start", [w_rel], 1, far_fwd)
    send, recv, (*others, w_rel) = _start_copies("gather_rest_start", [*others, w_rel], 3 * (n - 1), _other_weight_copies)
    proj = _proj_chunk(hb, w_rel, meta, 3, 1, 0, proj, "proj_far_landed")
    w_rel, = _wait_copies("gather_far_forward_wait", fsend, frecv, [w_rel], far_fwd, proj)
    proj = _proj_chunk(hb, w_rel, meta, 3, 1, 1, proj, "proj_far_forwarded")
    os, ls, a = _attention_fwd(proj, Bl, attn_q_norm, attn_k_norm)
    *others, w_rel = _wait_copies("gather_rest_wait", send, recv, [*others, w_rel], _other_weight_copies, a)
    fsend, frecv, (*others, proj) = _start_copies("gather_rest_forward_start", [*others, proj], 3 * (n - 1),
                                                  _other_weight_forward)
    cc = _conv_branch_fwd(proj, Bl, conv_full)
    others = _wait_copies("gather_rest_forward_wait", fsend, frecv, others, _other_weight_forward, cc)
    W = {name: others[p] for p, (name, _) in enumerate(BIG[1:])}

    G, rest, rest_state = _weight_grads(
        x, mem, loss_target, norm_g, mem_norm_g, attn_q_norm, attn_k_norm, conv_full, mem_q_norm, mem_k_norm, W,
        (hb, hbt, mhb, proj, os, ls, a, cc), early=lambda G, carry: _exchange_start(G, BIG[1:], pos, carry, "rest"))

    for_sibling = _dw_in_half(hbt, rest[0], pos, False, "dw_in_sibling")
    send, recv, (for_sibling, got, dproj) = _start_copies(
        "sibling_w_in_start", [for_sibling, lax.empty(for_sibling.shape, for_sibling.dtype), rest[0]], 1, _sibling_copy)
    mine = _dw_in_half(hbt, dproj, pos, True, "dw_in_own")
    for_sibling, got = _wait_copies("sibling_w_in_wait", send, recv, [for_sibling, got], _sibling_copy, mine)
    pre_w_in = _presum(mine, got, "col", pos, "presum_w_in")

    w_in_state, dproj = _exchange_start(G, BIG[:1], pos, dproj, "w_in", pres=[pre_w_in])
    grad_x, small = _input_grad((dproj, *rest[1:]), w_rel, order)
    pres_rest, slots_rest = _exchange_wait(rest_state, grad_x)
    reds_rest = [_reduce_into_shard(slots_rest[p], pres_rest[p], kind, pos, "reduce_" + name)
                 for p, (name, kind) in enumerate(BIG[1:])]
    share_rest = _share_copies(BIG[1:])
    rsend, rrecv, reds_rest = _start_copies("share_rest_start", reds_rest, n - 1, share_rest)
    pres, slots = _exchange_wait(w_in_state, grad_x)
    red_w_in = _reduce_into_shard(slots[0], pres[0], "col", pos, "reduce_w_in")
    share_w_in = _share_copies(BIG[:1])
    wsend, wrecv, (red_w_in, small) = _start_copies("share_w_in_start", [red_w_in, small], 1, share_w_in)
    grad_x = grad_x.reshape(x.shape)

    tot = _sum_small(_gather_small(small))
    reds_rest = _wait_copies("share_rest_wait", rsend, rrecv, reds_rest, share_rest, tot)
    grads = dict(zip([name for name, _ in BIG[1:]], reds_rest))
    tot = tot[0]
    loss = tot[0]
    off = 128
    for name, size in (("norm_g", D), ("mem_norm_g", D), ("attn_q_norm", NGROUP * HEAD), ("attn_k_norm", NGROUP * HEAD),
                       ("conv_w", 3 * CONVW), ("mem_q_norm", MEM_HD), ("mem_k_norm", MEM_HD)):
        grads[name] = tot[off:off + size]
        off += size
    cw = conv_w.shape[1]
    grads["conv_w"] = lax.dynamic_slice(grads["conv_w"].reshape(3, CONVW), (0, chip * cw), (3, cw))
    for name in SMALL:
        grads[name] = grads[name].reshape(w[name].shape)

    delta, new_m, new_v = {}, {}, {}
    for name, _ in BIG[1:]:
        delta[name], new_m[name], new_v[name] = _adamw(w[name], grads[name], m[name], v[name], "adamw_" + name)

    def packed(t):
        return jnp.concatenate([t[name].reshape(1, -1) for name in SMALL], axis=1)

    ds, ms, vs = _adamw(packed(w), packed(grads), packed(m), packed(v), "adamw_small")
    grads["w_in"], = _wait_copies("share_w_in_wait", wsend, wrecv, [red_w_in], share_w_in, ds)
    delta["w_in"], new_m["w_in"], new_v["w_in"] = _adamw(w["w_in"], grads["w_in"], m["w_in"], v["w_in"], "adamw_w_in")
    off = 0
    for name in SMALL:
        size = w[name].size
        delta[name] = ds[0, off:off + size].reshape(w[name].shape)
        new_m[name] = ms[0, off:off + size].reshape(w[name].shape)
        new_v[name] = vs[0, off:off + size].reshape(w[name].shape)
        off += size

    return (loss, grad_x, *[grads[n] for n in WEIGHTS], *[delta[n] for n in WEIGHTS],
            *[new_m[n] for n in WEIGHTS], *[new_v[n] for n in WEIGHTS])
```

```python
import functools

import jax
import jax.numpy as jnp
from jax import lax
from jax.experimental import pallas as pl
from jax.experimental.pallas import tpu as pltpu

F32 = jnp.float32
MXU_DTYPE = jnp.bfloat16
WIRE_DTYPE = jnp.bfloat16
PROJ_DTYPE = jnp.bfloat16
EPS = 1e-6
NEG = -1e30

HEAD = 128
HPG = 4
GW = HPG * HEAD
DILATIONS = (1, 4, 16)
NGROUP = len(DILATIONS)
BLK = 128
QKV = NGROUP * GW
CONVW = 1024
MEM_HEADS = 4
MEM_HD = 256
MEMW = MEM_HEADS * MEM_HD
Q0, K0, V0 = 0, QKV, 2 * QKV
ZA = 3 * QKV
CB, CC, CV, ZC = ZA + GW, ZA + GW + CONVW, ZA + GW + 2 * CONVW, ZA + GW + 3 * CONVW
MQ = ZC + CONVW
ZM = MQ + MEMW
G0 = ZM + MEMW

ADAM_LR, ADAM_B1, ADAM_B2, ADAM_EPS, ADAM_WD, ADAM_STEP = 0.001, 0.9, 0.999, 1e-08, 0.01, 10

VMEM_LIMIT = 56 * 1024 * 1024
MESH = pl.DeviceIdType.MESH
ANY = pl.BlockSpec(memory_space=pl.ANY)


def _tile(n, pref, mult=128):
    t = min(pref, n)
    while t > mult and (n % t or t % mult):
        t -= mult
    assert n % t == 0, (n, pref)
    return t


def _call(body, *, name, out_shape, grid=(), in_specs=None, out_specs=None, scratch_shapes=(),
          aliases=None, grid_spec=None):
    kw = {}
    if grid_spec is not None:
        kw["grid_spec"] = grid_spec
        ngrid = len(grid_spec.grid)
    else:
        kw.update(grid=grid, in_specs=in_specs, out_specs=out_specs, scratch_shapes=list(scratch_shapes))
        ngrid = len(grid)
    params = pltpu.CompilerParams(dimension_semantics=("arbitrary",) * ngrid, vmem_limit_bytes=VMEM_LIMIT)
    return pl.pallas_call(body, name=name, out_shape=out_shape, compiler_params=params,
                          input_output_aliases=aliases or {}, **kw)


_DIMS = {"nn": (((1,), (0,)), ((), ())), "nt": (((1,), (1,)), ((), ())), "tn": (((0,), (0,)), ((), ()))}


def _mxu(a, b, mode):
    return lax.dot_general(a.astype(MXU_DTYPE), b.astype(MXU_DTYPE), _DIMS[mode], preferred_element_type=F32)


@functools.partial(jax.custom_vjp, nondiff_argnums=(2,))
def _dot(a, b, mode):
    return _mxu(a, b, mode)


def _dot_fwd(a, b, mode):
    return _mxu(a, b, mode), (a, b)


def _dot_bwd(mode, res, g):
    a, b = res
    if mode == "nn":
        return _mxu(g, b, "nt"), _mxu(a, g, "tn")
    if mode == "nt":
        return _mxu(g, b, "nn"), _mxu(g, a, "tn")
    return _mxu(b, g, "nt"), _mxu(a, g, "nn")


_dot.defvjp(_dot_fwd, _dot_bwd)


def _sig(z):
    return 1.0 / (1.0 + jnp.exp(-z))


def _silu(z):
    return z * _sig(z)


def _rms_rows(t, g):
    return t * lax.rsqrt(jnp.mean(t * t, axis=-1, keepdims=True) + EPS) * g


def _attn_block(q, k2, v2, gq, gk, first):
    qn = _rms_rows(q, gq)
    kn = _rms_rows(k2, gk)
    s = jnp.where(_band_mask(first, k2.shape[0]), _dot(qn, kn, "nt") * (HEAD ** -0.5), NEG)
    m = lax.stop_gradient(jnp.max(s, axis=-1, keepdims=True))
    p = jnp.exp(s - m)
    den = jnp.sum(p, axis=-1, keepdims=True)
    o = _dot(p, v2, "nn") / den
    return o, m + jnp.log(den)


def _band_mask(first, nkeys):
    a = lax.broadcasted_iota(jnp.int32, (BLK, nkeys), 0)
    b = lax.broadcasted_iota(jnp.int32, (BLK, nkeys), 1)
    if nkeys == BLK:
        return b <= a
    return (b >= a) & (b <= a + BLK) & (b >= jnp.where(first, BLK, 0))


def _norm_parts(t):
    r = lax.rsqrt(jnp.mean(t * t, axis=-1, keepdims=True) + EPS)
    return r, t * r


def _norm_bwd(dn, g, r, th):
    dth = dn * g
    return r * (dth - th * jnp.mean(dth * th, axis=-1, keepdims=True)), jnp.sum(dn * th, axis=0, keepdims=True)


def _attn_block_bwd(q, k2, v2, gq, gk, first, do, o, lse, dlse):
    scale = HEAD ** -0.5
    rq, qh = _norm_parts(q)
    rk, kh = _norm_parts(k2)
    qn, kn = qh * gq, kh * gk
    s = jnp.where(_band_mask(first, k2.shape[0]), _mxu(qn, kn, "nt") * scale, NEG)
    p = jnp.exp(s - lse)
    ds = p * (_mxu(do, v2, "nt") + (dlse - jnp.sum(do * o, axis=-1, keepdims=True))) * scale
    dq, dgq = _norm_bwd(_mxu(ds, kn, "nn"), gq, rq, qh)
    dk2, dgk = _norm_bwd(_mxu(ds, qn, "tn"), gk, rk, kh)
    return dq, dk2, _mxu(p, do, "tn"), dgq, dgk


def _combine(o1, o2, o3, l1, l2, l3, z):
    m = lax.stop_gradient(jnp.maximum(jnp.maximum(l1, l2), l3))
    e1, e2, e3 = jnp.exp(l1 - m), jnp.exp(l2 - m), jnp.exp(l3 - m)
    return (e1 * o1 + e2 * o2 + e3 * o3) / (e1 + e2 + e3) * _silu(z)


def _mem_block(q, z, kv, gq, gk):
    outs = []
    for h in range(MEM_HEADS):
        sl = slice(h * MEM_HD, (h + 1) * MEM_HD)
        qn = _rms_rows(q[:, sl], gq)
        kn = _rms_rows(kv[:, sl], gk)
        s = _dot(qn, kn, "nt") * (MEM_HD ** -0.5)
        m = lax.stop_gradient(jnp.max(s, axis=-1, keepdims=True))
        p = jnp.exp(s - m)
        den = jnp.sum(p, axis=-1, keepdims=True)
        outs.append(_dot(p, kv[:, MEMW + h * MEM_HD:MEMW + (h + 1) * MEM_HD], "nn") / den)
    return jnp.concatenate(outs, axis=-1) * _silu(z)


def _cast(w, name):
    R, C = w.shape
    tr, tc = _tile(R, 512, 8), _tile(C, 2176)

    def body(w_ref, o_ref):
        o_ref[...] = w_ref[...].astype(o_ref.dtype)

    spec = pl.BlockSpec((tr, tc), lambda i, j: (i, j))
    return _call(body, name=name, grid=(R // tr, C // tc), in_specs=[spec], out_specs=spec,
                 out_shape=jax.ShapeDtypeStruct((R, C), WIRE_DTYPE))(w)


def _place_shard(w, kind, pos, dtype, name, slot=0, into=None):
    R, C = w.shape
    tr, tc = _tile(R, 512, 8), _tile(C, 2176)
    nr, nc = R // tr, C // tc

    def body(pos_ref, w_ref, *rest):
        rest[-1][...] = w_ref[...].astype(rest[-1].dtype)

    if kind == "col":
        full, out = (R, 4 * C), pl.BlockSpec((tr, tc), lambda i, j, pos_ref: (i, pos_ref[slot] * nc + j))
    else:
        full, out = (4 * R, C), pl.BlockSpec((tr, tc), lambda i, j, pos_ref: (pos_ref[slot] * nr + i, j))
    in_specs, args = [pl.BlockSpec((tr, tc), lambda i, j, pos_ref: (i, j))], [pos, w]
    if into is not None:
        in_specs.append(ANY)
        args.append(into)
    spec = pltpu.PrefetchScalarGridSpec(num_scalar_prefetch=1, grid=(nr, nc), in_specs=in_specs, out_specs=out)
    return _call(body, name=name, grid_spec=spec, out_shape=jax.ShapeDtypeStruct(full, dtype),
                 aliases={} if into is None else {2: 0})(*args)


def _matmul(a, b, mode, out_dtype, *, name, tm=512, tn=512, tk=512):
    if mode == "nn":
        (M, K), (_, N) = a.shape, b.shape
    elif mode == "nt":
        (M, K), (N, _) = a.shape, b.shape
    else:
        (K, M), (_, N) = a.shape, b.shape
    tm, tn, tk = _tile(M, tm), _tile(N, tn), _tile(K, tk)
    nk = K // tk

    def body(a_ref, b_ref, o_ref, *acc):
        part = lax.dot_general(a_ref[...], b_ref[...], _DIMS[mode], preferred_element_type=F32)
        if nk == 1:
            o_ref[...] = part.astype(o_ref.dtype)
            return
        acc_ref, = acc
        k = pl.program_id(2)

        @pl.when(k == 0)
        def _():
            acc_ref[...] = part

        @pl.when(k > 0)
        def _():
            acc_ref[...] += part

        @pl.when(k == nk - 1)
        def _():
            o_ref[...] = acc_ref[...].astype(o_ref.dtype)

    a_spec = pl.BlockSpec((tk, tm), lambda i, j, k: (k, i)) if mode == "tn" else pl.BlockSpec((tm, tk), lambda i, j, k: (i, k))
    b_spec = pl.BlockSpec((tn, tk), lambda i, j, k: (j, k)) if mode == "nt" else pl.BlockSpec((tk, tn), lambda i, j, k: (k, j))
    return _call(body, name=name, grid=(M // tm, N // tn, nk), in_specs=[a_spec, b_spec],
                 out_specs=pl.BlockSpec((tm, tn), lambda i, j, k: (i, j)),
                 out_shape=jax.ShapeDtypeStruct((M, N), out_dtype),
                 scratch_shapes=[] if nk == 1 else [pltpu.VMEM((tm, tn), F32)])(a, b)


def _rms_fwd(x, g, name):
    R, D = x.shape
    tr = _tile(R, 512)

    def body(x_ref, g_ref, o_ref, t_ref):
        y = _rms_rows(x_ref[...], g_ref[...])
        o_ref[...] = y.astype(o_ref.dtype)
        t_ref[...] = y.T.astype(t_ref.dtype)

    row = pl.BlockSpec((tr, D), lambda i: (i, 0))
    return _call(body, name=name, grid=(R // tr,), in_specs=[row, pl.BlockSpec((1, D), lambda i: (0, 0))],
                 out_specs=[row, pl.BlockSpec((D, tr), lambda i: (0, i))],
                 out_shape=[jax.ShapeDtypeStruct((R, D), MXU_DTYPE), jax.ShapeDtypeStruct((D, R), MXU_DTYPE)])(x, g)


def _rms_bwd(x, dh, g, dy, name):
    R, D = x.shape
    tr = _tile(R, 256)
    with_dx = dy is not None

    def body(*refs):
        if with_dx:
            x_ref, dh_ref, g_ref, dy_ref, dx_ref, dg_ref = refs
        else:
            x_ref, dh_ref, g_ref, dg_ref = refs
        xv, dhv = x_ref[...], dh_ref[...]
        r = lax.rsqrt(jnp.mean(xv * xv, axis=-1, keepdims=True) + EPS)
        xh = xv * r

        @pl.when(pl.program_id(0) == 0)
        def _():
            dg_ref[...] = jnp.zeros_like(dg_ref)

        dg_ref[...] += jnp.sum(dhv * xh, axis=0, keepdims=True)
        if with_dx:
            dxh = dhv * g_ref[...]
            dx_ref[...] = dy_ref[...] + r * (dxh - xh * jnp.mean(dxh * xh, axis=-1, keepdims=True))

    row = pl.BlockSpec((tr, D), lambda i: (i, 0))
    vec = pl.BlockSpec((1, D), lambda i: (0, 0))
    dg_shape = jax.ShapeDtypeStruct((1, D), F32)
    if with_dx:
        return _call(body, name=name, grid=(R // tr,), in_specs=[row, row, vec, row], out_specs=[row, vec],
                     out_shape=[jax.ShapeDtypeStruct((R, D), F32), dg_shape])(x, dh, g, dy)
    return None, _call(body, name=name, grid=(R // tr,), in_specs=[row, row, vec], out_specs=vec,
                       out_shape=dg_shape)(x, dh, g)


def _attn_geom(g, d):
    hc = HPG if d == 1 else 1
    cw = hc * HEAD
    cq, ck, cv = (Q0 + g * GW) // cw, (K0 + g * GW) // cw, (V0 + g * GW) // cw
    return (1, BLK * d, cw), hc, HPG // hc, cq, ck, cv


def _rows(ref, r, d, sl):
    if d == 1:
        return ref[0, :, sl]
    return ref.at[0][pl.ds(r, BLK, stride=d), sl]


def _set_rows(ref, r, d, sl, val):
    if d == 1:
        ref[0, :, sl] = val
    else:
        ref.at[0][pl.ds(r, BLK, stride=d), sl] = val


def _stage_rows(ref, r, d, sl, val):
    if d == 1:
        ref[:, sl] = val
    else:
        ref[pl.ds(r, BLK, stride=d), sl] = val


def _proj_stages(blk, d):
    return [] if d == 1 else [pltpu.VMEM(blk[1:], F32)] * 5


def _proj_rows(refs, stages, d):
    if d == 1:
        return [lambda r, sl, ref=ref: ref[0, :, sl].astype(F32) for ref in refs]
    for ref, stage in zip(refs, stages):
        stage[...] = ref[0].astype(F32)
    return [lambda r, sl, stage=stage: stage[pl.ds(r, BLK, stride=d), sl] for stage in stages]


def _attn_fwd(proj3, gq, gk, g, d):
    Bl, S, _ = proj3.shape
    blk, hc, ncb, cq, ck, cv = _attn_geom(g, d)
    nb = S // blk[1]
    if nb == 1:
        return _attn_single_fwd(proj3, gq, gk, g, d)

    def body(q_ref, kp_ref, kc_ref, vp_ref, vc_ref, gq_ref, gk_ref, o_ref, lse_ref, *stages):
        first = pl.program_id(2) == 0
        q, kp, kc, vp, vc = _proj_rows((q_ref, kp_ref, kc_ref, vp_ref, vc_ref), stages, d)
        for r in range(d):
            for h in range(hc):
                sl = slice(h * HEAD, (h + 1) * HEAD)
                k2 = jnp.concatenate([kp(r, sl), kc(r, sl)], axis=0)
                v2 = jnp.concatenate([vp(r, sl), vc(r, sl)], axis=0)
                o, lse = _attn_block(q(r, sl), k2, v2, gq_ref[...], gk_ref[...], first)
                _set_rows(o_ref, r, d, sl, o)
                _set_rows(lse_ref, r, d, sl, jnp.broadcast_to(lse, (BLK, HEAD)))

    def cur(c0):
        return pl.BlockSpec(blk, lambda b, j, i: (b, i, c0 + j))

    def prev(c0):
        return pl.BlockSpec(blk, lambda b, j, i: (b, jnp.maximum(i - 1, 0), c0 + j))

    vec = pl.BlockSpec((1, HEAD), lambda b, j, i: (0, 0))
    out = pl.BlockSpec(blk, lambda b, j, i: (b, i, j))
    shp = jax.ShapeDtypeStruct((Bl, S, GW), F32)
    return _call(body, name=f"attn_fwd_g{g}", grid=(Bl, ncb, nb),
                 in_specs=[cur(cq), prev(ck), cur(ck), prev(cv), cur(cv), vec, vec],
                 out_specs=[out, out], out_shape=[shp, shp], scratch_shapes=_proj_stages(blk, d),
                 )(proj3, proj3, proj3, proj3, proj3, gq, gk)


def _attn_single_fwd(proj3, gq, gk, g, d):
    Bl, S, _ = proj3.shape
    blk, hc, ncb, cq, ck, cv = _attn_geom(g, d)

    def body(q_ref, k_ref, v_ref, gq_ref, gk_ref, o_ref, lse_ref, *stages):
        q, k, v = _proj_rows((q_ref, k_ref, v_ref), stages, d)
        for r in range(d):
            for h in range(hc):
                sl = slice(h * HEAD, (h + 1) * HEAD)
                o, lse = _attn_block(q(r, sl), k(r, sl), v(r, sl), gq_ref[...], gk_ref[...], True)
                _set_rows(o_ref, r, d, sl, o)
                _set_rows(lse_ref, r, d, sl, jnp.broadcast_to(lse, (BLK, HEAD)))

    def at(c0):
        return pl.BlockSpec(blk, lambda b, j: (b, 0, c0 + j))

    vec = pl.BlockSpec((1, HEAD), lambda b, j: (0, 0))
    shp = jax.ShapeDtypeStruct((Bl, S, GW), F32)
    return _call(body, name=f"attn_fwd_g{g}", grid=(Bl, ncb), in_specs=[at(cq), at(ck), at(cv), vec, vec],
                 out_specs=[at(0), at(0)], out_shape=[shp, shp], scratch_shapes=_proj_stages(blk, d)[:3],
                 )(proj3, proj3, proj3, gq, gk)


def _attn_single_bwd(proj3, gq, gk, o3, l3, do3, dl3, g, d):
    Bl, S, _ = proj3.shape
    blk, hc, ncb, cq, ck, cv = _attn_geom(g, d)

    def body(q_ref, k_ref, v_ref, gq_ref, gk_ref, o_ref, l_ref, do_ref, dl_ref,
             dq_ref, dk_ref, dv_ref, dgq_ref, dgk_ref, sq_ref, sk_ref, sv_ref, *stages):
        @pl.when((pl.program_id(0) == 0) & (pl.program_id(1) == 0))
        def _():
            dgq_ref[...] = jnp.zeros_like(dgq_ref)
            dgk_ref[...] = jnp.zeros_like(dgk_ref)

        dgq, dgk = jnp.zeros((1, HEAD), F32), jnp.zeros((1, HEAD), F32)
        q, k, v = _proj_rows((q_ref, k_ref, v_ref), stages, d)
        for r in range(d):
            for h in range(hc):
                sl = slice(h * HEAD, (h + 1) * HEAD)
                dq, dk, dv, a, b = _attn_block_bwd(
                    q(r, sl), k(r, sl), v(r, sl), gq_ref[...], gk_ref[...], True, _rows(do_ref, r, d, sl),
                    _rows(o_ref, r, d, sl), _rows(l_ref, r, d, sl)[:, :1], _rows(dl_ref, r, d, sl)[:, :1])
                _stage_rows(sq_ref, r, d, sl, dq)
                _stage_rows(sk_ref, r, d, sl, dk)
                _stage_rows(sv_ref, r, d, sl, dv)
                dgq, dgk = dgq + a, dgk + b
        dgq_ref[...] += dgq
        dgk_ref[...] += dgk
        dq_ref[0] = sq_ref[...].astype(dq_ref.dtype)
        dk_ref[0] = sk_ref[...].astype(dk_ref.dtype)
        dv_ref[0] = sv_ref[...].astype(dv_ref.dtype)

    def at(c0):
        return pl.BlockSpec(blk, lambda b, j: (b, 0, c0 + j))

    vec = pl.BlockSpec((1, HEAD), lambda b, j: (0, 0))
    shp = jax.ShapeDtypeStruct((Bl, S, GW), MXU_DTYPE)
    gshp = jax.ShapeDtypeStruct((1, HEAD), F32)
    return _call(body, name=f"attn_bwd_g{g}", grid=(Bl, ncb),
                 in_specs=[at(cq), at(ck), at(cv), vec, vec, at(0), at(0), at(0), at(0)],
                 out_specs=[at(0), at(0), at(0), vec, vec], out_shape=[shp, shp, shp, gshp, gshp],
                 scratch_shapes=[pltpu.VMEM(blk[1:], F32)] * 3 + _proj_stages(blk, d)[:3],
                 )(proj3, proj3, proj3, gq, gk, o3, l3, do3, dl3)


def _attn_bwd(proj3, gq, gk, o3, l3, do3, dl3, g, d):
    Bl, S, _ = proj3.shape
    blk, hc, ncb, cq, ck, cv = _attn_geom(g, d)
    nb = S // blk[1]
    if nb == 1:
        return _attn_single_bwd(proj3, gq, gk, o3, l3, do3, dl3, g, d)

    def body(q_ref, kp_ref, kc_ref, vp_ref, vc_ref, gq_ref, gk_ref, o_ref, l_ref, do_ref, dl_ref,
             dq_ref, dk_ref, dv_ref, dgq_ref, dgk_ref, ck_ref, cv_ref, sq_ref, sk_ref, sv_ref, *stages):
        i = pl.program_id(2)
        first = i == 0

        @pl.when((pl.program_id(0) == 0) & (pl.program_id(1) == 0) & first)
        def _():
            dgq_ref[...] = jnp.zeros_like(dgq_ref)
            dgk_ref[...] = jnp.zeros_like(dgk_ref)

        @pl.when(first)
        def _():
            ck_ref[...] = jnp.zeros_like(ck_ref)
            cv_ref[...] = jnp.zeros_like(cv_ref)

        @pl.when(i < nb)
        def _():
            dgq, dgk = jnp.zeros((1, HEAD), F32), jnp.zeros((1, HEAD), F32)
            q, kp, kc, vp, vc = _proj_rows((q_ref, kp_ref, kc_ref, vp_ref, vc_ref), stages, d)
            for r in range(d):
                rs = slice(r * BLK, (r + 1) * BLK)
                for h in range(hc):
                    sl = slice(h * HEAD, (h + 1) * HEAD)
                    k2 = jnp.concatenate([kp(r, sl), kc(r, sl)], axis=0)
                    v2 = jnp.concatenate([vp(r, sl), vc(r, sl)], axis=0)
                    dq, dk2, dv2, a, b = _attn_block_bwd(
                        q(r, sl), k2, v2, gq_ref[...], gk_ref[...], first, _rows(do_ref, r, d, sl),
                        _rows(o_ref, r, d, sl), _rows(l_ref, r, d, sl)[:, :1], _rows(dl_ref, r, d, sl)[:, :1])
                    _stage_rows(sq_ref, r, d, sl, dq)
                    _stage_rows(sk_ref, r, d, sl, ck_ref[rs, sl] + dk2[:BLK])
                    _stage_rows(sv_ref, r, d, sl, cv_ref[rs, sl] + dv2[:BLK])
                    ck_ref[rs, sl] = dk2[BLK:]
                    cv_ref[rs, sl] = dv2[BLK:]
                    dgq, dgk = dgq + a, dgk + b
            dgq_ref[...] += dgq
            dgk_ref[...] += dgk
            dq_ref[0] = sq_ref[...].astype(dq_ref.dtype)

        @pl.when(i == nb)
        def _():
            for r in range(d):
                rs = slice(r * BLK, (r + 1) * BLK)
                _stage_rows(sk_ref, r, d, slice(None), ck_ref[rs, :])
                _stage_rows(sv_ref, r, d, slice(None), cv_ref[rs, :])

        dk_ref[0] = sk_ref[...].astype(dk_ref.dtype)
        dv_ref[0] = sv_ref[...].astype(dv_ref.dtype)

    def cur(c0):
        return pl.BlockSpec(blk, lambda b, j, i: (b, jnp.minimum(i, nb - 1), c0 + j))

    def prev(c0):
        return pl.BlockSpec(blk, lambda b, j, i: (b, jnp.clip(i - 1, 0, nb - 1), c0 + j))

    vec = pl.BlockSpec((1, HEAD), lambda b, j, i: (0, 0))
    at_q = pl.BlockSpec(blk, lambda b, j, i: (b, jnp.minimum(i, nb - 1), j))
    at_k = pl.BlockSpec(blk, lambda b, j, i: (b, jnp.maximum(i - 1, 0), j))
    shp = jax.ShapeDtypeStruct((Bl, S, GW), MXU_DTYPE)
    gshp = jax.ShapeDtypeStruct((1, HEAD), F32)
    return _call(body, name=f"attn_bwd_g{g}", grid=(Bl, ncb, nb + 1),
                 in_specs=[cur(cq), prev(ck), cur(ck), prev(cv), cur(cv), vec, vec, at_q, at_q, at_q, at_q],
                 out_specs=[at_q, at_k, at_k, vec, vec], out_shape=[shp, shp, shp, gshp, gshp],
                 scratch_shapes=[pltpu.VMEM(blk[1:], F32)] * 5 + _proj_stages(blk, d),
                 )(proj3, proj3, proj3, proj3, proj3, gq, gk, o3, l3, do3, dl3)


def _combine_fwd(os, ls, proj2):
    T = proj2.shape[0]
    tr = _tile(T, 512)

    def body(o1, o2, o3, l1, l2, l3, z, a_ref):
        a_ref[...] = _combine(o1[...], o2[...], o3[...], l1[...], l2[...], l3[...], z[...].astype(F32)).astype(a_ref.dtype)

    row = pl.BlockSpec((tr, GW), lambda i: (i, 0))
    return _call(body, name="combine_fwd", grid=(T // tr,),
                 in_specs=[row] * 6 + [pl.BlockSpec((tr, GW), lambda i: (i, ZA // GW))], out_specs=row,
                 out_shape=jax.ShapeDtypeStruct((T, GW), MXU_DTYPE))(*os, *ls, proj2)


def _combine_bwd(os, ls, proj2, da):
    T = proj2.shape[0]
    tr = _tile(T, 256)

    def body(o1, o2, o3, l1, l2, l3, z, da_ref, d1, d2, d3, e1, e2, e3, dz_ref):
        _, vjp = jax.vjp(_combine, o1[...], o2[...], o3[...], l1[...], l2[...], l3[...], z[...].astype(F32))
        go1, go2, go3, gl1, gl2, gl3, gz = vjp(da_ref[...])
        d1[...], d2[...], d3[...] = go1, go2, go3
        dz_ref[...] = gz.astype(dz_ref.dtype)
        for ref, gl in ((e1, gl1), (e2, gl2), (e3, gl3)):
            for h in range(HPG):
                sl = slice(h * HEAD, (h + 1) * HEAD)
                ref[:, sl] = jnp.broadcast_to(jnp.sum(gl[:, sl], axis=-1, keepdims=True), (tr, HEAD))

    row = pl.BlockSpec((tr, GW), lambda i: (i, 0))
    f = jax.ShapeDtypeStruct((T, GW), F32)
    outs = _call(body, name="combine_bwd", grid=(T // tr,),
                 in_specs=[row] * 6 + [pl.BlockSpec((tr, GW), lambda i: (i, ZA // GW)), row],
                 out_specs=[row] * 7, out_shape=[f] * 6 + [jax.ShapeDtypeStruct((T, GW), MXU_DTYPE)],
                 )(*os, *ls, proj2, da)
    return outs[:3], outs[3:6], outs[6]


def _shift_down(u, j, t):
    return jnp.where(t >= j, pltpu.roll(u, j, 0), 0.0)


def _shift_up(u, j, t):
    n = u.shape[0]
    return jnp.where(t < n - j, pltpu.roll(u, n - j, 0), 0.0)


def _conv_specs(Bl, S, cw):
    def sec(c0):
        return pl.BlockSpec((1, S, cw), lambda j, b: (b, 0, c0 // cw + j))
    return [sec(CB), sec(CC), sec(CV), sec(ZC)], pl.BlockSpec((3, cw), lambda j, b: (0, j))


def _conv_fwd(proj3, conv_w):
    Bl, S, _ = proj3.shape
    cw = 256
    secs, wspec = _conv_specs(Bl, S, cw)

    def body(b_ref, c_ref, v_ref, z_ref, w_ref, o_ref):
        t = lax.broadcasted_iota(jnp.int32, (S, cw), 0)
        u = c_ref[0].astype(F32) * v_ref[0].astype(F32)
        y = w_ref[0:1, :] * u + w_ref[1:2, :] * _shift_down(u, 1, t) + w_ref[2:3, :] * _shift_down(u, 2, t)
        o_ref[0] = (b_ref[0].astype(F32) * y * _silu(z_ref[0].astype(F32))).astype(o_ref.dtype)

    return _call(body, name="conv_fwd", grid=(CONVW // cw, Bl), in_specs=secs + [wspec],
                 out_specs=pl.BlockSpec((1, S, cw), lambda j, b: (b, 0, j)),
                 out_shape=jax.ShapeDtypeStruct((Bl, S, CONVW), MXU_DTYPE))(proj3, proj3, proj3, proj3, conv_w)


def _conv_bwd(proj3, conv_w, dcc3):
    Bl, S, _ = proj3.shape
    cw = 256
    secs, wspec = _conv_specs(Bl, S, cw)

    def body(b_ref, c_ref, v_ref, z_ref, w_ref, d_ref, db_ref, dc_ref, dv_ref, dz_ref, dw_ref):
        t = lax.broadcasted_iota(jnp.int32, (S, cw), 0)
        bv, cv, vv, zv = (r[0].astype(F32) for r in (b_ref, c_ref, v_ref, z_ref))
        dv = d_ref[0]
        u = cv * vv
        u1, u2 = _shift_down(u, 1, t), _shift_down(u, 2, t)
        y = w_ref[0:1, :] * u + w_ref[1:2, :] * u1 + w_ref[2:3, :] * u2
        sg = _sig(zv)
        sz = zv * sg
        gy = dv * bv * sz
        db_ref[0] = (dv * y * sz).astype(db_ref.dtype)
        dz_ref[0] = (dv * bv * y * sg * (1.0 + zv * (1.0 - sg))).astype(dz_ref.dtype)
        du = w_ref[0:1, :] * gy + w_ref[1:2, :] * _shift_up(gy, 1, t) + w_ref[2:3, :] * _shift_up(gy, 2, t)
        dc_ref[0] = (du * vv).astype(dc_ref.dtype)
        dv_ref[0] = (du * cv).astype(dv_ref.dtype)

        @pl.when(pl.program_id(1) == 0)
        def _():
            dw_ref[...] = jnp.zeros_like(dw_ref)

        dw_ref[0:1, :] += jnp.sum(gy * u, axis=0, keepdims=True)
        dw_ref[1:2, :] += jnp.sum(gy * u1, axis=0, keepdims=True)
        dw_ref[2:3, :] += jnp.sum(gy * u2, axis=0, keepdims=True)

    blk = pl.BlockSpec((1, S, cw), lambda j, b: (b, 0, j))
    shp = jax.ShapeDtypeStruct((Bl, S, CONVW), MXU_DTYPE)
    return _call(body, name="conv_bwd", grid=(CONVW // cw, Bl), in_specs=secs + [wspec, blk],
                 out_specs=[blk] * 4 + [wspec], out_shape=[shp] * 4 + [jax.ShapeDtypeStruct((3, CONVW), F32)],
                 )(proj3, proj3, proj3, proj3, conv_w, dcc3)


def _mem_specs(S, tq):
    q = pl.BlockSpec((1, tq, MEMW), lambda b, j: (b, j, MQ // MEMW))
    z = pl.BlockSpec((1, tq, MEMW), lambda b, j: (b, j, ZM // MEMW))
    kv = pl.BlockSpec((1, MEM_HD, 2 * MEMW), lambda b, j: (b, 0, 0))
    vec = pl.BlockSpec((1, MEM_HD), lambda b, j: (0, 0))
    blk = pl.BlockSpec((1, tq, MEMW), lambda b, j: (b, j, 0))
    return q, z, kv, vec, blk


def _mem_fwd(proj3, mkv3, gq, gk):
    Bl, S, _ = proj3.shape
    tq = _tile(S, 512)
    q, z, kv, vec, blk = _mem_specs(S, tq)

    def body(q_ref, z_ref, kv_ref, gq_ref, gk_ref, o_ref):
        o_ref[0] = _mem_block(q_ref[0].astype(F32), z_ref[0].astype(F32), kv_ref[0], gq_ref[...],
                              gk_ref[...]).astype(o_ref.dtype)

    return _call(body, name="mem_fwd", grid=(Bl, S // tq), in_specs=[q, z, kv, vec, vec], out_specs=blk,
                 out_shape=jax.ShapeDtypeStruct((Bl, S, MEMW), MXU_DTYPE))(proj3, proj3, mkv3, gq, gk)


def _mem_bwd(proj3, mkv3, gq, gk, dmo3):
    Bl, S, _ = proj3.shape
    tq = _tile(S, 256)
    q, z, kv, vec, blk = _mem_specs(S, tq)

    def body(q_ref, z_ref, kv_ref, gq_ref, gk_ref, d_ref, dq_ref, dz_ref, dkv_ref, dgq_ref, dgk_ref):
        _, vjp = jax.vjp(_mem_block, q_ref[0].astype(F32), z_ref[0].astype(F32), kv_ref[0], gq_ref[...], gk_ref[...])
        dq, dz, dkv, dgq, dgk = vjp(d_ref[0])
        dq_ref[0] = dq.astype(dq_ref.dtype)
        dz_ref[0] = dz.astype(dz_ref.dtype)
        j = pl.program_id(1)

        @pl.when(j == 0)
        def _():
            dkv_ref[0] = jnp.zeros_like(dkv)

        @pl.when((j == 0) & (pl.program_id(0) == 0))
        def _():
            dgq_ref[...] = jnp.zeros_like(dgq_ref)
            dgk_ref[...] = jnp.zeros_like(dgk_ref)

        dkv_ref[0] += dkv
        dgq_ref[...] += dgq
        dgk_ref[...] += dgk

    shp = jax.ShapeDtypeStruct((Bl, S, MEMW), MXU_DTYPE)
    gshp = jax.ShapeDtypeStruct((1, MEM_HD), F32)
    return _call(body, name="mem_bwd", grid=(Bl, S // tq), in_specs=[q, z, kv, vec, vec, blk],
                 out_specs=[blk, blk, kv, vec, vec],
                 out_shape=[shp, shp, jax.ShapeDtypeStruct(mkv3.shape, F32), gshp, gshp],
                 )(proj3, proj3, mkv3, gq, gk, dmo3)


def _merge_specs(T, D, tm, tn):
    def act(w):
        return pl.BlockSpec((tm, w), lambda i, n: (i, 0))

    def wsp(w):
        return pl.BlockSpec((w, tn), lambda i, n: (0, n))

    gates = [pl.BlockSpec((tm, tn), lambda i, n, k=k: (i, (G0 + k * D) // tn + n)) for k in range(3)]
    tile = pl.BlockSpec((tm, tn), lambda i, n: (i, n))
    return act, wsp, gates, tile


def _merge_fwd(a, cc, mo, wa, wc, wm, proj2):
    T, D = a.shape[0], wa.shape[1]
    tm, tn = _tile(T, 1024), _tile(D, 512)
    act, wsp, gates, tile = _merge_specs(T, D, tm, tn)

    def body(a_ref, c_ref, m_ref, wa_ref, wc_ref, wm_ref, g0, g1, g2, mg_ref, mt_ref, pa_ref, pc_ref, pm_ref):
        pa = jnp.dot(a_ref[...], wa_ref[...], preferred_element_type=F32)
        pc = jnp.dot(c_ref[...], wc_ref[...], preferred_element_type=F32)
        pm = jnp.dot(m_ref[...], wm_ref[...], preferred_element_type=F32)
        mg = _sig(g0[...].astype(F32)) * pa + _sig(g1[...].astype(F32)) * pc + _sig(g2[...].astype(F32)) * pm
        mg_ref[...] = mg.astype(mg_ref.dtype)
        mt_ref[...] = mg.T.astype(mt_ref.dtype)
        pa_ref[...] = pa.astype(pa_ref.dtype)
        pc_ref[...] = pc.astype(pc_ref.dtype)
        pm_ref[...] = pm.astype(pm_ref.dtype)

    shp = jax.ShapeDtypeStruct((T, D), MXU_DTYPE)
    return _call(body, name="merge_fwd", grid=(T // tm, D // tn),
                 in_specs=[act(GW), act(CONVW), act(MEMW), wsp(GW), wsp(CONVW), wsp(MEMW)] + gates,
                 out_specs=[tile, pl.BlockSpec((tn, tm), lambda i, n: (n, i)), tile, tile, tile],
                 out_shape=[shp, jax.ShapeDtypeStruct((D, T), MXU_DTYPE), shp, shp, shp],
                 )(a, cc, mo, wa, wc, wm, proj2, proj2, proj2)


def _merge_bwd(dyb, w_out, proj2, pa, pc, pm):
    T, D = dyb.shape
    tm, tn = _tile(T, 1024), _tile(D, 512)
    _, _, gates, tile = _merge_specs(T, D, tm, tn)

    def body(dy_ref, w_ref, g0, g1, g2, p0, p1, p2, dp0, dp1, dp2, dg0, dg1, dg2):
        dm = lax.dot_general(dy_ref[...], w_ref[...], _DIMS["nt"], preferred_element_type=F32)
        for g_ref, p_ref, dp_ref, dg_ref in ((g0, p0, dp0, dg0), (g1, p1, dp1, dg1), (g2, p2, dp2, dg2)):
            gt = _sig(g_ref[...].astype(F32))
            dp_ref[...] = (gt * dm).astype(dp_ref.dtype)
            dg_ref[...] = (dm * p_ref[...].astype(F32) * gt * (1.0 - gt)).astype(dg_ref.dtype)

    shp = jax.ShapeDtypeStruct((T, D), MXU_DTYPE)
    return _call(body, name="merge_bwd", grid=(T // tm, D // tn),
                 in_specs=[pl.BlockSpec((tm, D), lambda i, n: (i, 0)), pl.BlockSpec((tn, D), lambda i, n: (n, 0))]
                 + gates + [tile] * 3,
                 out_specs=[tile] * 6, out_shape=[shp] * 6)(dyb, w_out, proj2, proj2, proj2, pa, pc, pm)


def _out_loss(merged, w_out, x, tgt):
    T, D = x.shape
    tm = _tile(T, 512)

    def body(m_ref, w_ref, x_ref, t_ref, dy_ref, dyb_ref, loss_ref):
        err = x_ref[...] + jnp.dot(m_ref[...], w_ref[...], preferred_element_type=F32) - t_ref[...]
        dy = err * (1.0 / D)
        dy_ref[...] = dy
        dyb_ref[...] = dy.astype(dyb_ref.dtype)

        @pl.when(pl.program_id(0) == 0)
        def _():
            loss_ref[...] = jnp.zeros_like(loss_ref)

        loss_ref[...] += jnp.sum(err * err) * (0.5 / D)

    row = pl.BlockSpec((tm, D), lambda i: (i, 0))
    return _call(body, name="out_loss", grid=(T // tm,),
                 in_specs=[row, pl.BlockSpec((D, D), lambda i: (0, 0)), row, row],
                 out_specs=[row, row, pl.BlockSpec((1, 128), lambda i: (0, 0))],
                 out_shape=[jax.ShapeDtypeStruct((T, D), F32), jax.ShapeDtypeStruct((T, D), MXU_DTYPE),
                            jax.ShapeDtypeStruct((1, 128), F32)])(merged, w_out, x, tgt)


def _proj_chunk(hb, w, meta, j, nslots, half, buf, name):
    T, D = hb.shape
    Cs = w.shape[1] // 4
    tm, tn = _tile(T, 1024), _tile(Cs // 2, 2176)
    nh = Cs // 2 // tn
    per = nh if half is not None else 2 * nh

    def body(meta_ref, a_ref, b_ref, *rest):
        rest[-1][...] = jnp.dot(a_ref[...], b_ref[...], preferred_element_type=F32).astype(rest[-1].dtype)

    def tile(n, m):
        if half is None:
            return n % per
        return (m[4] if half == 0 else 1 - m[4]) * nh + n % per

    in_specs = [pl.BlockSpec((tm, D), lambda n, i, m: (i, 0)),
                pl.BlockSpec((D, tn), lambda n, i, m: (0, (j + n // per) * 2 * nh + tile(n, m)))]
    args = [meta, hb, w]
    if buf is not None:
        in_specs.append(ANY)
        args.append(buf)
    spec = pltpu.PrefetchScalarGridSpec(
        num_scalar_prefetch=1, grid=(nslots * per, T // tm), in_specs=in_specs,
        out_specs=pl.BlockSpec((tm, tn), lambda n, i, m: (i, m[j + n // per] * 2 * nh + tile(n, m))))
    return _call(body, name=name, grid_spec=spec, out_shape=jax.ShapeDtypeStruct((T, 4 * Cs), PROJ_DTYPE),
                 aliases={} if buf is None else {3: 0})(*args)


def _norms(x, mem, norm_g, mem_norm_g):
    D = x.shape[-1]
    hb, hbt = _rms_fwd(x.reshape(-1, D), norm_g.reshape(1, D), "rms_x")
    mhb, _ = _rms_fwd(mem.reshape(-1, D), mem_norm_g.reshape(1, D), "rms_mem")
    return hb, hbt, mhb


def _attention_fwd(proj2, Bl, gq_all, gk_all):
    T, IN = proj2.shape
    proj3 = proj2.reshape(Bl, T // Bl, IN)
    os, ls = [], []
    for g, d in enumerate(DILATIONS):
        o, l = _attn_fwd(proj3, gq_all[g:g + 1], gk_all[g:g + 1], g, d)
        os.append(o.reshape(T, GW))
        ls.append(l.reshape(T, GW))
    return os, ls, _combine_fwd(os, ls, proj2)


def _conv_branch_fwd(proj2, Bl, conv_w):
    T, IN = proj2.shape
    return _conv_fwd(proj2.reshape(Bl, T // Bl, IN), conv_w).reshape(T, CONVW)


def _weight_grads(x, mem, tgt, norm_g, mem_norm_g, gq_all, gk_all, conv_w, mem_gq, mem_gk, W, pre, early=None):
    Bl, S, D = x.shape
    T = Bl * S
    hb, hbt, mhb, proj2, os, ls, a, cc = pre
    IN = proj2.shape[1]
    proj3 = proj2.reshape(Bl, S, IN)
    x2, tgt2 = x.reshape(T, D), tgt.reshape(T, D)
    mem2 = mem.reshape(-1, D)
    ng, mng = norm_g.reshape(1, D), mem_norm_g.reshape(1, D)
    mgq, mgk = mem_gq.reshape(1, MEM_HD), mem_gk.reshape(1, MEM_HD)
    gqs = [gq_all[g:g + 1] for g in range(NGROUP)]
    gks = [gk_all[g:g + 1] for g in range(NGROUP)]

    mkv = _matmul(mhb, W["mem_w_kv"], "nn", F32, name="mem_kv", tm=512, tn=1024, tk=D)
    mkv3 = mkv.reshape(Bl, -1, 2 * MEMW)
    mo = _mem_fwd(proj3, mkv3, mgq, mgk).reshape(T, MEMW)
    merged, mergedt, pa, pc, pm = _merge_fwd(a, cc, mo, W["w_br_attn"], W["w_br_conv"], W["w_br_mem"], proj2)
    dy, dyb, loss = _out_loss(merged, W["w_out"], x2, tgt2)

    G = {}
    G["w_out"] = _matmul(mergedt, dyb, "nn", WIRE_DTYPE, name="dw_out", tm=1024, tn=512, tk=T)
    dpa, dpc, dpm, dg0, dg1, dg2 = _merge_bwd(dyb, W["w_out"], proj2, pa, pc, pm)
    G["w_br_attn"] = _matmul(a, dpa, "tn", WIRE_DTYPE, name="dw_br_attn", tm=512, tn=1024, tk=512)
    G["w_br_conv"] = _matmul(cc, dpc, "tn", WIRE_DTYPE, name="dw_br_conv", tm=1024, tn=1024, tk=512)
    G["w_br_mem"] = _matmul(mo, dpm, "tn", WIRE_DTYPE, name="dw_br_mem", tm=1024, tn=1024, tk=512)
    da = _matmul(dpa, W["w_br_attn"], "nt", F32, name="d_attn", tm=1024, tn=512, tk=D)
    dcc = _matmul(dpc, W["w_br_conv"], "nt", F32, name="d_conv", tm=1024, tn=1024, tk=D)
    dmo = _matmul(dpm, W["w_br_mem"], "nt", F32, name="d_mem", tm=1024, tn=1024, tk=D)
    dmq, dzm, dmkv3, dmgq, dmgk = _mem_bwd(proj3, mkv3, mgq, mgk, dmo.reshape(Bl, S, MEMW))
    dmkv = _cast(dmkv3.reshape(-1, 2 * MEMW), "cast_dmkv")
    G["mem_w_kv"] = _matmul(mhb, dmkv, "tn", WIRE_DTYPE, name="dw_mem_kv", tm=1024, tn=1024, tk=512)
    early_state, da = (None, da) if early is None else early(G, da)
    dmh = _matmul(dmkv, W["mem_w_kv"], "nt", F32, name="d_memh", tm=512, tn=1024, tk=2 * MEMW)
    _, dmng = _rms_bwd(mem2, dmh, mng, None, "rms_mem_bwd")

    dos, dls, dza = _combine_bwd(os, ls, proj2, da)
    dqs, dks, dvs, dgq, dgk = [], [], [], [], []
    for g, d in enumerate(DILATIONS):
        dq, dk, dv, gq_g, gk_g = _attn_bwd(proj3, gqs[g], gks[g], os[g].reshape(Bl, S, GW), ls[g].reshape(Bl, S, GW),
                                           dos[g].reshape(Bl, S, GW), dls[g].reshape(Bl, S, GW), g, d)
        dqs.append(dq.reshape(T, GW).astype(MXU_DTYPE))
        dks.append(dk.reshape(T, GW).astype(MXU_DTYPE))
        dvs.append(dv.reshape(T, GW).astype(MXU_DTYPE))
        dgq.append(gq_g)
        dgk.append(gk_g)
    dcb, dcc_, dcv, dzc, dconv_w = _conv_bwd(proj3, conv_w, dcc.reshape(Bl, S, CONVW))

    dproj = jnp.concatenate(dqs + dks + dvs + [dza] + [t.reshape(T, CONVW) for t in (dcb, dcc_, dcv, dzc)]
                            + [dmq.reshape(T, MEMW), dzm.reshape(T, MEMW), dg0, dg1, dg2], axis=1)
    small = [loss, None, dmng] + dgq + dgk + [dconv_w.reshape(1, 3 * CONVW), dmgq, dmgk]
    return G, (dproj, x2, ng, dy, small), early_state


def _dw_in_half(hbt, dproj, pos, own, name):
    D, T = hbt.shape
    IN = dproj.shape[1]
    R, tn = D // 2, _tile(IN, 512)

    def body(pos_ref, a_ref, b_ref, o_ref):
        o_ref[...] = jnp.dot(a_ref[...], b_ref[...], preferred_element_type=F32).astype(o_ref.dtype)

    spec = pltpu.PrefetchScalarGridSpec(
        num_scalar_prefetch=1, grid=(IN // tn,),
        in_specs=[pl.BlockSpec((R, T), lambda j, p: (p[1] if own else 1 - p[1], 0)),
                  pl.BlockSpec((T, tn), lambda j, p: (0, j))],
        out_specs=pl.BlockSpec((R, tn), lambda j, p: (0, j)))
    return _call(body, name=name, grid_spec=spec, out_shape=jax.ShapeDtypeStruct((R, IN), WIRE_DTYPE))(pos, hbt, dproj)


def _d_h(dproj, w, order):
    T, IN = dproj.shape
    D, Cs = w.shape[0], IN // 4
    tm, tn = _tile(T, 1024), _tile(D, 1024)

    def body(order_ref, a_ref, b_ref, o_ref, acc_ref):
        part = lax.dot_general(a_ref[...], b_ref[...], _DIMS["nt"], preferred_element_type=F32)
        k = pl.program_id(2)

        @pl.when(k == 0)
        def _():
            acc_ref[...] = part

        @pl.when(k > 0)
        def _():
            acc_ref[...] += part

        @pl.when(k == 3)
        def _():
            o_ref[...] = acc_ref[...]

    spec = pltpu.PrefetchScalarGridSpec(
        num_scalar_prefetch=1, grid=(T // tm, D // tn, 4),
        in_specs=[pl.BlockSpec((tm, Cs), lambda i, n, k, o: (i, o[k])), pl.BlockSpec((tn, Cs), lambda i, n, k, o: (n, k))],
        out_specs=pl.BlockSpec((tm, tn), lambda i, n, k, o: (i, n)), scratch_shapes=[pltpu.VMEM((tm, tn), F32)])
    return _call(body, name="d_h", grid_spec=spec, out_shape=jax.ShapeDtypeStruct((T, D), F32))(order, dproj, w)


def _input_grad(rest, w_in, order):
    dproj, x2, ng, dy, small = rest
    dh = _d_h(dproj, w_in, order)
    grad_x, dng = _rms_bwd(x2, dh, ng, dy, "rms_x_bwd")
    small = [dng if t is None else t for t in small]
    return grad_x, jnp.concatenate(small, axis=1)


def _local_step(x, mem, tgt, norm_g, mem_norm_g, gq_all, gk_all, conv_w, mem_gq, mem_gk, W):
    hb, hbt, mhb = _norms(x, mem, norm_g, mem_norm_g)
    Cs = W["w_in"].shape[1] // 4
    shards = (0, 2, 1, 3)
    order = jnp.array(shards, dtype=jnp.int32)
    w_rel = jnp.concatenate([W["w_in"][:, s * Cs:(s + 1) * Cs] for s in shards], axis=1)
    meta = jnp.array(shards + (0,), dtype=jnp.int32)
    proj2 = _proj_chunk(hb, w_rel, meta, 0, 1, None, None, "proj_0")
    for j, nslots in ((1, 2), (3, 1)):
        for half in (1, 0):
            proj2 = _proj_chunk(hb, w_rel, meta, j, nslots, half, proj2, f"proj_{j}_{half}")
    pre = (hb, hbt, mhb, proj2, *_attention_fwd(proj2, x.shape[0], gq_all, gk_all),
           _conv_branch_fwd(proj2, x.shape[0], conv_w))
    G, rest, _ = _weight_grads(x, mem, tgt, norm_g, mem_norm_g, gq_all, gk_all, conv_w, mem_gq, mem_gk, W, pre)
    pos = jnp.zeros((2,), jnp.int32)
    G["w_in"] = jnp.concatenate([_dw_in_half(hbt, rest[0], pos, True, "dw_in_own"),
                                 _dw_in_half(hbt, rest[0], pos, False, "dw_in_sibling")], axis=0)
    grad_x, small = _input_grad(rest, w_rel, order)
    return grad_x.reshape(x.shape), G, small


BIG = (("w_in", "col"), ("mem_w_kv", "row"), ("w_br_attn", "col"), ("w_br_conv", "col"),
       ("w_br_mem", "col"), ("w_out", "row"))


def _coords():
    return lax.axis_index("x"), lax.axis_index("y"), lax.axis_index("c")


def _other_chips(x, y):
    return [(1 - x, y), (x, 1 - y), (1 - x, 1 - y)]


def _half(ref, kind, c):
    R, C = ref.shape
    if kind == "col":
        return ref.at[pl.ds(c * (R // 2), R // 2), :]
    return ref.at[:, pl.ds(c * (C // 2), C // 2)]


def _shard(ref, kind, s):
    R, C = ref.shape
    if kind == "col":
        return ref.at[:, pl.ds(s * (C // 4), C // 4)]
    return ref.at[pl.ds(s * (R // 4), R // 4), :]


def _piece(ref, kind, s, c):
    R, C = ref.shape
    if kind == "col":
        return ref.at[pl.ds(c * (R // 2), R // 2), pl.ds(s * (C // 4), C // 4)]
    return ref.at[pl.ds(s * (R // 4), R // 4), pl.ds(c * (C // 2), C // 2)]


def _remote(src, dst, sems_s, sems_r, k, dev):
    return pltpu.make_async_remote_copy(src_ref=src, dst_ref=dst, send_sem=sems_s.at[k], recv_sem=sems_r.at[k],
                                        device_id=dev, device_id_type=MESH)


HBM = pl.BlockSpec(memory_space=pltpu.HBM)
SEM = pl.BlockSpec(memory_space=pltpu.SEMAPHORE)
EFFECT = pltpu.SideEffectType.DATAFLOW_SIDE_EFFECTING


def _hbm(a):
    return pltpu.with_memory_space_constraint(a, pltpu.HBM)


def _start_copies(name, arrays, ncopies, make):
    n = len(arrays)

    def body(*refs):
        for cp in make(refs[:n], refs[n], refs[n + 1]):
            cp.start()

    outs = pl.pallas_call(
        body, name=name,
        out_shape=(pltpu.SemaphoreType.DMA((ncopies,)), pltpu.SemaphoreType.DMA((ncopies,)),
                   *[jax.ShapeDtypeStruct(t.shape, t.dtype) for t in arrays]),
        in_specs=[HBM] * n, out_specs=(SEM, SEM, *([HBM] * n)),
        input_output_aliases={i: i + 2 for i in range(n)},
        compiler_params=pltpu.CompilerParams(has_side_effects=EFFECT),
    )(*[_hbm(t) for t in arrays])
    return outs[0], outs[1], list(outs[2:])


def _wait_copies(name, send, recv, arrays, make, after):
    n = len(arrays)

    def body(*refs):
        for cp in make(refs[:n], refs[n], refs[n + 1]):
            cp.wait_send()
            cp.wait_recv()

    outs = pl.pallas_call(
        body, name=name, out_shape=[jax.ShapeDtypeStruct(t.shape, t.dtype) for t in arrays],
        in_specs=[HBM] * n + [SEM, SEM, ANY], out_specs=[HBM] * n,
        input_output_aliases={i: i for i in range(n)},
        compiler_params=pltpu.CompilerParams(has_side_effects=EFFECT),
    )(*arrays, send, recv, after)
    return list(outs)


def _w_in_copies(relations):
    def make(refs, send, recv):
        x, y, c = _coords()
        me = 2 * x + y
        chips = _other_chips(x, y)
        w, conv = refs[0], refs[1]
        cps = []
        for i, k in enumerate(relations):
            cps.append(_remote(_column_half(w, 0, c), _column_half(w, 1 + k, c), send, recv, 2 * i, (*chips[k], c)))
            mine = _shard(conv, "col", me)
            cps.append(_remote(mine, mine, send, recv, 2 * i + 1, (*chips[k], c)))
        return cps
    return make


def _column_half(w, slot, c):
    half = w.shape[1] // 8
    return w.at[:, pl.ds((2 * slot + c) * half, half)]


def _w_in_forward(relations):
    def make(refs, send, recv):
        x, y, c = _coords()
        cps = []
        for i, k in enumerate(relations):
            got = _column_half(refs[0], 1 + k, c)
            cps.append(_remote(got, got, send, recv, i, (x, y, 1 - c)))
        return cps
    return make


def _sibling_copy(refs, send, recv):
    x, y, c = _coords()
    return [_remote(refs[0], refs[1], send, recv, 0, (x, y, 1 - c))]


def _other_weight_copies(refs, send, recv):
    x, y, c = _coords()
    me = 2 * x + y
    cps = []
    for k, chip in enumerate(_other_chips(x, y)):
        for p, (_, kind) in enumerate(BIG[1:]):
            mine = _piece(refs[p], kind, me, c)
            cps.append(_remote(mine, mine, send, recv, 3 * p + k, (*chip, c)))
    return cps


def _other_weight_forward(refs, send, recv):
    x, y, c = _coords()
    cps = []
    for k, chip in enumerate(_other_chips(x, y)):
        s = 2 * chip[0] + chip[1]
        for p, (_, kind) in enumerate(BIG[1:]):
            got = _piece(refs[p], kind, s, c)
            cps.append(_remote(got, got, send, recv, 3 * p + k, (x, y, 1 - c)))
    return cps


def _share_copies(group):
    def make(refs, send, recv):
        x, y, c = _coords()
        cps = []
        for p, (_, kind) in enumerate(group):
            mine = _half(refs[p], kind, c)
            cps.append(_remote(mine, mine, send, recv, p, (x, y, 1 - c)))
        return cps
    return make


def _sibling_forward(name, arrays, ncp, halves):
    n = len(arrays)

    def body(*refs):
        outs = refs[n:2 * n]
        send, recv = refs[2 * n:]
        x, y, c = _coords()
        sib = (x, y, 1 - c)
        cps = [_remote(got, got, send, recv, i, sib) for i, got in enumerate(halves(outs, c))]
        for cp in cps:
            cp.start()
        for cp in cps:
            cp.wait_send()
        for i, got in enumerate(halves(outs, 1 - c)):
            _remote(got, got, send, recv, i, sib).wait_recv()

    return pl.pallas_call(
        body, name=name, out_shape=[jax.ShapeDtypeStruct(t.shape, t.dtype) for t in arrays],
        in_specs=[ANY] * n, out_specs=[ANY] * n, input_output_aliases={i: i for i in range(n)},
        scratch_shapes=[pltpu.SemaphoreType.DMA((ncp,)), pltpu.SemaphoreType.DMA((ncp,))],
    )(*arrays)


def _landed_halves(refs, c):
    return [_half(r, "col", c) for r in refs]


def _other_weight_halves(refs, c):
    x, y, _ = _coords()
    out = []
    for chip in _other_chips(x, y):
        s = 2 * chip[0] + chip[1]
        out += [_piece(refs[p], kind, s, c) for p, (_, kind) in enumerate(BIG[1:])]
    return out


def _sibling_exchange(grads, group, name):
    n = len(group)
    shapes = []
    for (_, kind), g in zip(group, grads):
        R, C = g.shape
        shapes.append(jax.ShapeDtypeStruct((R // 2, C) if kind == "col" else (R, C // 2), g.dtype))

    def body(*refs):
        ins, outs = refs[:n], refs[n:2 * n]
        send, recv = refs[2 * n:]
        x, y, c = _coords()
        sib = (x, y, 1 - c)
        cps = [_remote(_half(ins[p], group[p][1], 1 - c), outs[p], send, recv, p, sib) for p in range(n)]
        for cp in cps:
            cp.start()
        for cp in cps:
            cp.wait()

    return pl.pallas_call(
        body, name=name, out_shape=shapes, in_specs=[ANY] * n, out_specs=[ANY] * n,
        scratch_shapes=[pltpu.SemaphoreType.DMA((n,)), pltpu.SemaphoreType.DMA((n,))],
    )(*grads)


def _presum(g, got, kind, pos, name):
    R, C = got.shape
    tr, tc = _tile(R, 512, 16), _tile(C, 2048)
    nr, nc = R // tr, C // tc

    def body(pos_ref, a_ref, b_ref, o_ref):
        o_ref[...] = (a_ref[...].astype(F32) + b_ref[...].astype(F32)).astype(o_ref.dtype)

    blk = pl.BlockSpec((tr, tc), lambda i, j, pos_ref: (i, j))
    if g.shape == got.shape:
        mine = blk
    elif kind == "col":
        mine = pl.BlockSpec((tr, tc), lambda i, j, pos_ref: (pos_ref[1] * nr + i, j))
    else:
        mine = pl.BlockSpec((tr, tc), lambda i, j, pos_ref: (i, pos_ref[1] * nc + j))
    spec = pltpu.PrefetchScalarGridSpec(num_scalar_prefetch=1, grid=(nr, nc), in_specs=[mine, blk], out_specs=blk)
    return _call(body, name=name, grid_spec=spec, out_shape=jax.ShapeDtypeStruct((R, C), WIRE_DTYPE))(pos, g, got)


def _chip_copies(group):
    n = len(group)

    def make(refs, send, recv):
        x, y, c = _coords()
        cps = []
        for k, chip in enumerate(_other_chips(x, y)):
            s = 2 * chip[0] + chip[1]
            for p in range(n):
                cps.append(_remote(_shard(refs[p], group[p][1], s), refs[n + p].at[k], send, recv, 3 * p + k, (*chip, c)))
        return cps
    return make


def _landing_zones(pres, group):
    lands = []
    for (_, kind), g in zip(group, pres):
        R, C = g.shape
        lands.append(lax.empty((3, R, C // 4) if kind == "col" else (3, R // 4, C), g.dtype))
    return lands


def _exchange_start(G, group, pos, carry, tag, pres=None):
    n = len(group)
    if pres is None:
        parts = [G[name] for name, _ in group]
        got = _sibling_exchange(parts, group, "sibling_exchange_" + tag)
        pres = [_presum(parts[p], got[p], kind, pos, "presum_" + name) for p, (name, kind) in enumerate(group)]
    make = _chip_copies(group)
    send, recv, thru = _start_copies("chip_exchange_start_" + tag, [*pres, *_landing_zones(pres, group), carry], 3 * n, make)
    return (send, recv, thru[:2 * n], make, tag), thru[2 * n]


def _exchange_wait(state, after):
    send, recv, arrays, make, tag = state
    thru = _wait_copies("chip_exchange_wait_" + tag, send, recv, arrays, make, after)
    n = len(thru) // 2
    return thru[:n], thru[n:]


def _reduce_into_shard(slots, pre, kind, pos, name):
    K, R, C = slots.shape
    tr, tc = _tile(R, 512, 16), _tile(C, 2176)
    nr, nc = R // tr, C // tc

    def body(pos_ref, s_ref, p_ref, o_ref):
        acc = p_ref[...].astype(F32)
        for k in range(K):
            acc = acc + s_ref[k].astype(F32)
        o_ref[...] = acc

    if kind == "col":
        own = pl.BlockSpec((tr, tc), lambda i, j, pos_ref: (i, pos_ref[0] * nc + j))
        full, out = (2 * R, C), pl.BlockSpec((tr, tc), lambda i, j, pos_ref: (pos_ref[1] * nr + i, j))
    else:
        own = pl.BlockSpec((tr, tc), lambda i, j, pos_ref: (pos_ref[0] * nr + i, j))
        full, out = (R, 2 * C), pl.BlockSpec((tr, tc), lambda i, j, pos_ref: (i, pos_ref[1] * nc + j))
    spec = pltpu.PrefetchScalarGridSpec(
        num_scalar_prefetch=1, grid=(nr, nc),
        in_specs=[pl.BlockSpec((K, tr, tc), lambda i, j, pos_ref: (0, i, j)), own], out_specs=out)
    return _call(body, name=name, grid_spec=spec, out_shape=jax.ShapeDtypeStruct(full, F32))(pos, slots, pre)


def _share_reduced(reds):
    n = len(BIG)

    def body(*refs):
        outs = refs[n:2 * n]
        send, recv = refs[2 * n:]
        x, y, c = _coords()
        sib = (x, y, 1 - c)
        cps = []
        for p in range(n):
            mine = _half(outs[p], BIG[p][1], c)
            cps.append(_remote(mine, mine, send, recv, p, sib))
        for cp in cps:
            cp.start()
        for cp in cps:
            cp.wait_send()
        for p in range(n):
            got = _half(outs[p], BIG[p][1], 1 - c)
            _remote(got, got, send, recv, p, sib).wait_recv()

    return pl.pallas_call(
        body, name="share_reduced", out_shape=[jax.ShapeDtypeStruct(r.shape, r.dtype) for r in reds],
        in_specs=[ANY] * n, out_specs=[ANY] * n, input_output_aliases={p: p for p in range(n)},
        scratch_shapes=[pltpu.SemaphoreType.DMA((n,)), pltpu.SemaphoreType.DMA((n,))],
    )(*reds)


def _gather_small(pack):
    _, N = pack.shape

    def body(in_ref, out_ref, send, recv, loc):
        x, y, c = _coords()
        me = 4 * x + 2 * y + c
        own = pltpu.make_async_copy(in_ref, out_ref.at[me], loc)
        own.start()
        cps = []
        for k in range(1, 8):
            dev = (x ^ (k >> 2), y ^ ((k >> 1) & 1), c ^ (k & 1))
            cps.append(_remote(in_ref, out_ref.at[me], send, recv, k - 1, dev))
        for cp in cps:
            cp.start()
        for k in range(1, 8):
            src = 4 * (x ^ (k >> 2)) + 2 * (y ^ ((k >> 1) & 1)) + (c ^ (k & 1))
            _remote(in_ref, out_ref.at[src], send, recv, k - 1, (x, y, c)).wait_recv()
        for cp in cps:
            cp.wait_send()
        own.wait()

    return pl.pallas_call(
        body, name="gather_small", out_shape=jax.ShapeDtypeStruct((8, 1, N), pack.dtype),
        in_specs=[ANY], out_specs=ANY,
        scratch_shapes=[pltpu.SemaphoreType.DMA((7,)), pltpu.SemaphoreType.DMA((7,)), pltpu.SemaphoreType.DMA(())],
    )(pack)


def _sum_small(slots):
    K, _, N = slots.shape

    def body(s_ref, o_ref):
        acc = s_ref[0]
        for k in range(1, K):
            acc = acc + s_ref[k]
        o_ref[...] = acc

    return _call(body, name="sum_small", in_specs=[pl.BlockSpec(memory_space=pltpu.VMEM)],
                 out_specs=pl.BlockSpec(memory_space=pltpu.VMEM), out_shape=jax.ShapeDtypeStruct((1, N), F32))(slots)


def _adamw(w, g, m, v, name):
    R, C = w.shape
    tr, tc = _tile(R, 256, 8), _tile(C, 2176)

    def body(w_ref, g_ref, m_ref, v_ref, d_ref, nm_ref, nv_ref):
        gv = g_ref[...]
        nm = ADAM_B1 * m_ref[...] + (1.0 - ADAM_B1) * gv
        nv = ADAM_B2 * v_ref[...] + (1.0 - ADAM_B2) * gv * gv
        m_hat = nm / (1.0 - ADAM_B1 ** ADAM_STEP)
        v_hat = nv / (1.0 - ADAM_B2 ** ADAM_STEP)
        d_ref[...] = -ADAM_LR * (m_hat / (jnp.sqrt(v_hat) + ADAM_EPS) + ADAM_WD * w_ref[...])
        nm_ref[...] = nm
        nv_ref[...] = nv

    spec = pl.BlockSpec((tr, tc), lambda i, j: (i, j))
    shp = jax.ShapeDtypeStruct((R, C), F32)
    return _call(body, name=name, grid=(R // tr, C // tc), in_specs=[spec] * 4, out_specs=[spec] * 3,
                 out_shape=[shp] * 3)(w, g, m, v)


SMALL = ("norm_g", "mem_norm_g", "attn_q_norm", "attn_k_norm", "conv_w", "mem_q_norm", "mem_k_norm")
WEIGHTS = ("norm_g", "mem_norm_g", "w_in", "attn_q_norm", "attn_k_norm", "conv_w", "mem_w_kv", "mem_q_norm",
           "mem_k_norm", "w_br_attn", "w_br_conv", "w_br_mem", "w_out")


def kernel(x, mem, norm_g, mem_norm_g, w_in, attn_q_norm, attn_k_norm, conv_w, mem_w_kv, mem_q_norm, mem_k_norm, w_br_attn, w_br_conv, w_br_mem, w_out, loss_target, m_norm_g, m_mem_norm_g, m_w_in, m_attn_q_norm, m_attn_k_norm, m_conv_w, m_mem_w_kv, m_mem_q_norm, m_mem_k_norm, m_w_br_attn, m_w_br_conv, m_w_br_mem, m_w_out, v_norm_g, v_mem_norm_g, v_w_in, v_attn_q_norm, v_attn_k_norm, v_conv_w, v_mem_w_kv, v_mem_q_norm, v_mem_k_norm, v_w_br_attn, v_w_br_conv, v_w_br_mem, v_w_out):
    w = dict(norm_g=norm_g, mem_norm_g=mem_norm_g, w_in=w_in, attn_q_norm=attn_q_norm, attn_k_norm=attn_k_norm,
             conv_w=conv_w, mem_w_kv=mem_w_kv, mem_q_norm=mem_q_norm, mem_k_norm=mem_k_norm, w_br_attn=w_br_attn,
             w_br_conv=w_br_conv, w_br_mem=w_br_mem, w_out=w_out)
    m = dict(norm_g=m_norm_g, mem_norm_g=m_mem_norm_g, w_in=m_w_in, attn_q_norm=m_attn_q_norm,
             attn_k_norm=m_attn_k_norm, conv_w=m_conv_w, mem_w_kv=m_mem_w_kv, mem_q_norm=m_mem_q_norm,
             mem_k_norm=m_mem_k_norm, w_br_attn=m_w_br_attn, w_br_conv=m_w_br_conv, w_br_mem=m_w_br_mem, w_out=m_w_out)
    v = dict(norm_g=v_norm_g, mem_norm_g=v_mem_norm_g, w_in=v_w_in, attn_q_norm=v_attn_q_norm,
             attn_k_norm=v_attn_k_norm, conv_w=v_conv_w, mem_w_kv=v_mem_w_kv, mem_q_norm=v_mem_q_norm,
             mem_k_norm=v_mem_k_norm, w_br_attn=v_w_br_attn, w_br_conv=v_w_br_conv, w_br_mem=v_w_br_mem, w_out=v_w_out)
    Bl, _, D = x.shape
    cx, cy = lax.axis_index("x"), lax.axis_index("y")
    chip = 2 * cx + cy
    pos = jnp.stack([chip, lax.axis_index("c")]).astype(jnp.int32)
    order = jnp.stack([chip] + [2 * a + b for a, b in _other_chips(cx, cy)]).astype(jnp.int32)
    n = len(BIG)

    w_rel = _place_shard(w["w_in"], "col", jnp.zeros((1,), jnp.int32), WIRE_DTYPE, "place_w_in")
    conv_full = _place_shard(conv_w, "col", pos, F32, "place_conv_w")
    others = [_place_shard(w[name], kind, pos, WIRE_DTYPE, "place_" + name) for name, kind in BIG[1:]]
    hb, hbt, mhb = _norms(x, mem, norm_g, mem_norm_g)

    meta = jnp.concatenate([order, pos[1:]])
    near, near_fwd = _w_in_copies((0, 1)), _w_in_forward((0, 1))
    send, recv, (w_rel, conv_full) = _start_copies("gather_near_start", [w_rel, conv_full], 4, near)
    proj = _proj_chunk(hb, w_rel, meta, 0, 1, None, None, "proj_own")
    w_rel, conv_full, *others = _wait_copies("gather_near_wait", send, recv, [w_rel, conv_full, *others], near, proj)

    fsend, frecv, (w_rel,) = _start_copies("gather_near_forward_start", [w_rel], 2, near_fwd)
    far, far_fwd = _w_in_copies((2,)), _w_in_forward((2,))
    send, recv, (w_rel, conv_full) = _start_copies("gather_far_start", [w_rel, conv_full], 2, far)
    proj = _proj_chunk(hb, w_rel, meta, 1, 2, 0, proj, "proj_near_landed")
    w_rel, = _wait_copies("gather_near_forward_wait", fsend, frecv, [w_rel], near_fwd, proj)
    proj = _proj_chunk(hb, w_rel, meta, 1, 2, 1, proj, "proj_near_forwarded")
    w_rel, conv_full = _wait_copies("gather_far_wait", send, recv, [w_rel, conv_full], far, proj)

    fsend, frecv, (w_rel,) = _start_copies("gather_far_forward_start", [w_rel], 1, far_fwd)
    send, recv, (*others, w_rel) = _start_copies("gather_rest_start", [*others, w_rel], 3 * (n - 1), _other_weight_copies)
    proj = _proj_chunk(hb, w_rel, meta, 3, 1, 0, proj, "proj_far_landed")
    w_rel, = _wait_copies("gather_far_forward_wait", fsend, frecv, [w_rel], far_fwd, proj)
    proj = _proj_chunk(hb, w_rel, meta, 3, 1, 1, proj, "proj_far_forwarded")
    os, ls, a = _attention_fwd(proj, Bl, attn_q_norm, attn_k_norm)
    *others, w_rel = _wait_copies("gather_rest_wait", send, recv, [*others, w_rel], _other_weight_copies, a)
    fsend, frecv, (*others, proj) = _start_copies("gather_rest_forward_start", [*others, proj], 3 * (n - 1),
                                                  _other_weight_forward)
    cc = _conv_branch_fwd(proj, Bl, conv_full)
    others = _wait_copies("gather_rest_forward_wait", fsend, frecv, others, _other_weight_forward, cc)
    W = {name: others[p] for p, (name, _) in enumerate(BIG[1:])}

    G, rest, rest_state = _weight_grads(
        x, mem, loss_target, norm_g, mem_norm_g, attn_q_norm, attn_k_norm, conv_full, mem_q_norm, mem_k_norm, W,
        (hb, hbt, mhb, proj, os, ls, a, cc), early=lambda G, carry: _exchange_start(G, BIG[1:], pos, carry, "rest"))

    for_sibling = _dw_in_half(hbt, rest[0], pos, False, "dw_in_sibling")
    send, recv, (for_sibling, got, dproj) = _start_copies(
        "sibling_w_in_start", [for_sibling, lax.empty(for_sibling.shape, for_sibling.dtype), rest[0]], 1, _sibling_copy)
    mine = _dw_in_half(hbt, dproj, pos, True, "dw_in_own")
    for_sibling, got = _wait_copies("sibling_w_in_wait", send, recv, [for_sibling, got], _sibling_copy, mine)
    pre_w_in = _presum(mine, got, "col", pos, "presum_w_in")

    w_in_state, dproj = _exchange_start(G, BIG[:1], pos, dproj, "w_in", pres=[pre_w_in])
    grad_x, small = _input_grad((dproj, *rest[1:]), w_rel, order)
    pres_rest, slots_rest = _exchange_wait(rest_state, grad_x)
    reds_rest = [_reduce_into_shard(slots_rest[p], pres_rest[p], kind, pos, "reduce_" + name)
                 for p, (name, kind) in enumerate(BIG[1:])]
    share_rest = _share_copies(BIG[1:])
    rsend, rrecv, reds_rest = _start_copies("share_rest_start", reds_rest, n - 1, share_rest)
    pres, slots = _exchange_wait(w_in_state, grad_x)
    red_w_in = _reduce_into_shard(slots[0], pres[0], "col", pos, "reduce_w_in")
    share_w_in = _share_copies(BIG[:1])
    wsend, wrecv, (red_w_in, small) = _start_copies("share_w_in_start", [red_w_in, small], 1, share_w_in)
    grad_x = grad_x.reshape(x.shape)

    tot = _sum_small(_gather_small(small))
    reds_rest = _wait_copies("share_rest_wait", rsend, rrecv, reds_rest, share_rest, tot)
    grads = dict(zip([name for name, _ in BIG[1:]], reds_rest))
    tot = tot[0]
    loss = tot[0]
    off = 128
    for name, size in (("norm_g", D), ("mem_norm_g", D), ("attn_q_norm", NGROUP * HEAD), ("attn_k_norm", NGROUP * HEAD),
                       ("conv_w", 3 * CONVW), ("mem_q_norm", MEM_HD), ("mem_k_norm", MEM_HD)):
        grads[name] = tot[off:off + size]
        off += size
    cw = conv_w.shape[1]
    grads["conv_w"] = lax.dynamic_slice(grads["conv_w"].reshape(3, CONVW), (0, chip * cw), (3, cw))
    for name in SMALL:
        grads[name] = grads[name].reshape(w[name].shape)

    delta, new_m, new_v = {}, {}, {}
    for name, _ in BIG[1:]:
        delta[name], new_m[name], new_v[name] = _adamw(w[name], grads[name], m[name], v[name], "adamw_" + name)

    def packed(t):
        return jnp.concatenate([t[name].reshape(1, -1) for name in SMALL], axis=1)

    ds, ms, vs = _adamw(packed(w), packed(grads), packed(m), packed(v), "adamw_small")
    grads["w_in"], = _wait_copies("share_w_in_wait", wsend, wrecv, [red_w_in], share_w_in, ds)
    delta["w_in"], new_m["w_in"], new_v["w_in"] = _adamw(w["w_in"], grads["w_in"], m["w_in"], v["w_in"], "adamw_w_in")
    off = 0
    for name in SMALL:
        size = w[name].size
        delta[name] = ds[0, off:off + size].reshape(w[name].shape)
        new_m[name] = ms[0, off:off + size].reshape(w[name].shape)
        new_v[name] = vs[0, off:off + size].reshape(w[name].shape)
        off += size

    return (loss, grad_x, *[grads[n] for n in WEIGHTS], *[delta[n] for n in WEIGHTS],
            *[new_m[n] for n in WEIGHTS], *[new_v[n] for n in WEIGHTS])
```

```python
import functools

import jax
import jax.numpy as jnp
from jax import lax
from jax.experimental import pallas as pl
from jax.experimental.pallas import tpu as pltpu

F32 = jnp.float32
MXU_DTYPE = jnp.bfloat16
WIRE_DTYPE = jnp.bfloat16
PROJ_DTYPE = jnp.bfloat16
EPS = 1e-6
NEG = -1e30

HEAD = 128
HPG = 4
GW = HPG * HEAD
DILATIONS = (1, 4, 16)
NGROUP = len(DILATIONS)
BLK = 128
QKV = NGROUP * GW
CONVW = 1024
MEM_HEADS = 4
MEM_HD = 256
MEMW = MEM_HEADS * MEM_HD
Q0, K0, V0 = 0, QKV, 2 * QKV
ZA = 3 * QKV
CB, CC, CV, ZC = ZA + GW, ZA + GW + CONVW, ZA + GW + 2 * CONVW, ZA + GW + 3 * CONVW
MQ = ZC + CONVW
ZM = MQ + MEMW
G0 = ZM + MEMW

ADAM_LR, ADAM_B1, ADAM_B2, ADAM_EPS, ADAM_WD, ADAM_STEP = 0.001, 0.9, 0.999, 1e-08, 0.01, 10

VMEM_LIMIT = 56 * 1024 * 1024
MESH = pl.DeviceIdType.MESH
ANY = pl.BlockSpec(memory_space=pl.ANY)


def _tile(n, pref, mult=128):
    t = min(pref, n)
    while t > mult and (n % t or t % mult):
        t -= mult
    assert n % t == 0, (n, pref)
    return t


def _call(body, *, name, out_shape, grid=(), in_specs=None, out_specs=None, scratch_shapes=(),
          aliases=None, grid_spec=None):
    kw = {}
    if grid_spec is not None:
        kw["grid_spec"] = grid_spec
        ngrid = len(grid_spec.grid)
    else:
        kw.update(grid=grid, in_specs=in_specs, out_specs=out_specs, scratch_shapes=list(scratch_shapes))
        ngrid = len(grid)
    params = pltpu.CompilerParams(dimension_semantics=("arbitrary",) * ngrid, vmem_limit_bytes=VMEM_LIMIT)
    return pl.pallas_call(body, name=name, out_shape=out_shape, compiler_params=params,
                          input_output_aliases=aliases or {}, **kw)


_DIMS = {"nn": (((1,), (0,)), ((), ())), "nt": (((1,), (1,)), ((), ())), "tn": (((0,), (0,)), ((), ()))}


def _mxu(a, b, mode):
    return lax.dot_general(a.astype(MXU_DTYPE), b.astype(MXU_DTYPE), _DIMS[mode], preferred_element_type=F32)


@functools.partial(jax.custom_vjp, nondiff_argnums=(2,))
def _dot(a, b, mode):
    return _mxu(a, b, mode)


def _dot_fwd(a, b, mode):
    return _mxu(a, b, mode), (a, b)


def _dot_bwd(mode, res, g):
    a, b = res
    if mode == "nn":
        return _mxu(g, b, "nt"), _mxu(a, g, "tn")
    if mode == "nt":
        return _mxu(g, b, "nn"), _mxu(g, a, "tn")
    return _mxu(b, g, "nt"), _mxu(a, g, "nn")


_dot.defvjp(_dot_fwd, _dot_bwd)


def _sig(z):
    return 1.0 / (1.0 + jnp.exp(-z))


def _silu(z):
    return z * _sig(z)


def _rms_rows(t, g):
    return t * lax.rsqrt(jnp.mean(t * t, axis=-1, keepdims=True) + EPS) * g


def _attn_block(q, k2, v2, gq, gk, first):
    qn = _rms_rows(q, gq)
    kn = _rms_rows(k2, gk)
    s = jnp.where(_band_mask(first, k2.shape[0]), _dot(qn, kn, "nt") * (HEAD ** -0.5), NEG)
    m = lax.stop_gradient(jnp.max(s, axis=-1, keepdims=True))
    p = jnp.exp(s - m)
    den = jnp.sum(p, axis=-1, keepdims=True)
    o = _dot(p, v2, "nn") / den
    return o, m + jnp.log(den)


def _band_mask(first, nkeys):
    a = lax.broadcasted_iota(jnp.int32, (BLK, nkeys), 0)
    b = lax.broadcasted_iota(jnp.int32, (BLK, nkeys), 1)
    if nkeys == BLK:
        return b <= a
    return (b >= a) & (b <= a + BLK) & (b >= jnp.where(first, BLK, 0))


def _norm_parts(t):
    r = lax.rsqrt(jnp.mean(t * t, axis=-1, keepdims=True) + EPS)
    return r, t * r


def _norm_bwd(dn, g, r, th):
    dth = dn * g
    return r * (dth - th * jnp.mean(dth * th, axis=-1, keepdims=True)), jnp.sum(dn * th, axis=0, keepdims=True)


def _attn_block_bwd(q, k2, v2, gq, gk, first, do, o, lse, dlse):
    scale = HEAD ** -0.5
    rq, qh = _norm_parts(q)
    rk, kh = _norm_parts(k2)
    qn, kn = qh * gq, kh * gk
    s = jnp.where(_band_mask(first, k2.shape[0]), _mxu(qn, kn, "nt") * scale, NEG)
    p = jnp.exp(s - lse)
    ds = p * (_mxu(do, v2, "nt") + (dlse - jnp.sum(do * o, axis=-1, keepdims=True))) * scale
    dq, dgq = _norm_bwd(_mxu(ds, kn, "nn"), gq, rq, qh)
    dk2, dgk = _norm_bwd(_mxu(ds, qn, "tn"), gk, rk, kh)
    return dq, dk2, _mxu(p, do, "tn"), dgq, dgk


def _combine(o1, o2, o3, l1, l2, l3, z):
    m = lax.stop_gradient(jnp.maximum(jnp.maximum(l1, l2), l3))
    e1, e2, e3 = jnp.exp(l1 - m), jnp.exp(l2 - m), jnp.exp(l3 - m)
    return (e1 * o1 + e2 * o2 + e3 * o3) / (e1 + e2 + e3) * _silu(z)


def _mem_block(q, z, kv, gq, gk):
    outs = []
    for h in range(MEM_HEADS):
        sl = slice(h * MEM_HD, (h + 1) * MEM_HD)
        qn = _rms_rows(q[:, sl], gq)
        kn = _rms_rows(kv[:, sl], gk)
        s = _dot(qn, kn, "nt") * (MEM_HD ** -0.5)
        m = lax.stop_gradient(jnp.max(s, axis=-1, keepdims=True))
        p = jnp.exp(s - m)
        den = jnp.sum(p, axis=-1, keepdims=True)
        outs.append(_dot(p, kv[:, MEMW + h * MEM_HD:MEMW + (h + 1) * MEM_HD], "nn") / den)
    return jnp.concatenate(outs, axis=-1) * _silu(z)


def _cast(w, name):
    R, C = w.shape
    tr, tc = _tile(R, 512, 8), _tile(C, 2176)

    def body(w_ref, o_ref):
        o_ref[...] = w_ref[...].astype(o_ref.dtype)

    spec = pl.BlockSpec((tr, tc), lambda i, j: (i, j))
    return _call(body, name=name, grid=(R // tr, C // tc), in_specs=[spec], out_specs=spec,
                 out_shape=jax.ShapeDtypeStruct((R, C), WIRE_DTYPE))(w)


def _place_shard(w, kind, pos, dtype, name, slot=0, into=None):
    R, C = w.shape
    tr, tc = _tile(R, 512, 8), _tile(C, 2176)
    nr, nc = R // tr, C // tc

    def body(pos_ref, w_ref, *rest):
        rest[-1][...] = w_ref[...].astype(rest[-1].dtype)

    if kind == "col":
        full, out = (R, 4 * C), pl.BlockSpec((tr, tc), lambda i, j, pos_ref: (i, pos_ref[slot] * nc + j))
    else:
        full, out = (4 * R, C), pl.BlockSpec((tr, tc), lambda i, j, pos_ref: (pos_ref[slot] * nr + i, j))
    in_specs, args = [pl.BlockSpec((tr, tc), lambda i, j, pos_ref: (i, j))], [pos, w]
    if into is not None:
        in_specs.append(ANY)
        args.append(into)
    spec = pltpu.PrefetchScalarGridSpec(num_scalar_prefetch=1, grid=(nr, nc), in_specs=in_specs, out_specs=out)
    return _call(body, name=name, grid_spec=spec, out_shape=jax.ShapeDtypeStruct(full, dtype),
                 aliases={} if into is None else {2: 0})(*args)


def _matmul(a, b, mode, out_dtype, *, name, tm=512, tn=512, tk=512):
    if mode == "nn":
        (M, K), (_, N) = a.shape, b.shape
    elif mode == "nt":
        (M, K), (N, _) = a.shape, b.shape
    else:
        (K, M), (_, N) = a.shape, b.shape
    tm, tn, tk = _tile(M, tm), _tile(N, tn), _tile(K, tk)
    nk = K // tk

    def body(a_ref, b_ref, o_ref, *acc):
        part = lax.dot_general(a_ref[...], b_ref[...], _DIMS[mode], preferred_element_type=F32)
        if nk == 1:
            o_ref[...] = part.astype(o_ref.dtype)
            return
        acc_ref, = acc
        k = pl.program_id(2)

        @pl.when(k == 0)
        def _():
            acc_ref[...] = part

        @pl.when(k > 0)
        def _():
            acc_ref[...] += part

        @pl.when(k == nk - 1)
        def _():
            o_ref[...] = acc_ref[...].astype(o_ref.dtype)

    a_spec = pl.BlockSpec((tk, tm), lambda i, j, k: (k, i)) if mode == "tn" else pl.BlockSpec((tm, tk), lambda i, j, k: (i, k))
    b_spec = pl.BlockSpec((tn, tk), lambda i, j, k: (j, k)) if mode == "nt" else pl.BlockSpec((tk, tn), lambda i, j, k: (k, j))
    return _call(body, name=name, grid=(M // tm, N // tn, nk), in_specs=[a_spec, b_spec],
                 out_specs=pl.BlockSpec((tm, tn), lambda i, j, k: (i, j)),
                 out_shape=jax.ShapeDtypeStruct((M, N), out_dtype),
                 scratch_shapes=[] if nk == 1 else [pltpu.VMEM((tm, tn), F32)])(a, b)


def _rms_fwd(x, g, name):
    R, D = x.shape
    tr = _tile(R, 512)

    def body(x_ref, g_ref, o_ref, t_ref):
        y = _rms_rows(x_ref[...], g_ref[...])
        o_ref[...] = y.astype(o_ref.dtype)
        t_ref[...] = y.T.astype(t_ref.dtype)

    row = pl.BlockSpec((tr, D), lambda i: (i, 0))
    return _call(body, name=name, grid=(R // tr,), in_specs=[row, pl.BlockSpec((1, D), lambda i: (0, 0))],
                 out_specs=[row, pl.BlockSpec((D, tr), lambda i: (0, i))],
                 out_shape=[jax.ShapeDtypeStruct((R, D), MXU_DTYPE), jax.ShapeDtypeStruct((D, R), MXU_DTYPE)])(x, g)


def _rms_bwd(x, dh, g, dy, name):
    R, D = x.shape
    tr = _tile(R, 256)
    with_dx = dy is not None

    def body(*refs):
        if with_dx:
            x_ref, dh_ref, g_ref, dy_ref, dx_ref, dg_ref = refs
        else:
            x_ref, dh_ref, g_ref, dg_ref = refs
        xv, dhv = x_ref[...], dh_ref[...]
        r = lax.rsqrt(jnp.mean(xv * xv, axis=-1, keepdims=True) + EPS)
        xh = xv * r

        @pl.when(pl.program_id(0) == 0)
        def _():
            dg_ref[...] = jnp.zeros_like(dg_ref)

        dg_ref[...] += jnp.sum(dhv * xh, axis=0, keepdims=True)
        if with_dx:
            dxh = dhv * g_ref[...]
            dx_ref[...] = dy_ref[...] + r * (dxh - xh * jnp.mean(dxh * xh, axis=-1, keepdims=True))

    row = pl.BlockSpec((tr, D), lambda i: (i, 0))
    vec = pl.BlockSpec((1, D), lambda i: (0, 0))
    dg_shape = jax.ShapeDtypeStruct((1, D), F32)
    if with_dx:
        return _call(body, name=name, grid=(R // tr,), in_specs=[row, row, vec, row], out_specs=[row, vec],
                     out_shape=[jax.ShapeDtypeStruct((R, D), F32), dg_shape])(x, dh, g, dy)
    return None, _call(body, name=name, grid=(R // tr,), in_specs=[row, row, vec], out_specs=vec,
                       out_shape=dg_shape)(x, dh, g)


def _attn_geom(g, d):
    hc = HPG if d == 1 else 1
    cw = hc * HEAD
    cq, ck, cv = (Q0 + g * GW) // cw, (K0 + g * GW) // cw, (V0 + g * GW) // cw
    return (1, BLK * d, cw), hc, HPG // hc, cq, ck, cv


def _rows(ref, r, d, sl):
    if d == 1:
        return ref[0, :, sl]
    return ref.at[0][pl.ds(r, BLK, stride=d), sl]


def _set_rows(ref, r, d, sl, val):
    if d == 1:
        ref[0, :, sl] = val
    else:
        ref.at[0][pl.ds(r, BLK, stride=d), sl] = val


def _stage_rows(ref, r, d, sl, val):
    if d == 1:
        ref[:, sl] = val
    else:
        ref[pl.ds(r, BLK, stride=d), sl] = val


def _proj_stages(blk, d):
    return [] if d == 1 else [pltpu.VMEM(blk[1:], F32)] * 5


def _proj_rows(refs, stages, d):
    if d == 1:
        return [lambda r, sl, ref=ref: ref[0, :, sl].astype(F32) for ref in refs]
    for ref, stage in zip(refs, stages):
        stage[...] = ref[0].astype(F32)
    return [lambda r, sl, stage=stage: stage[pl.ds(r, BLK, stride=d), sl] for stage in stages]


def _attn_fwd(proj3, gq, gk, g, d):
    Bl, S, _ = proj3.shape
    blk, hc, ncb, cq, ck, cv = _attn_geom(g, d)
    nb = S // blk[1]
    if nb == 1:
        return _attn_single_fwd(proj3, gq, gk, g, d)

    def body(q_ref, kp_ref, kc_ref, vp_ref, vc_ref, gq_ref, gk_ref, o_ref, lse_ref, *stages):
        first = pl.program_id(2) == 0
        q, kp, kc, vp, vc = _proj_rows((q_ref, kp_ref, kc_ref, vp_ref, vc_ref), stages, d)
        for r in range(d):
            for h in range(hc):
                sl = slice(h * HEAD, (h + 1) * HEAD)
                k2 = jnp.concatenate([kp(r, sl), kc(r, sl)], axis=0)
                v2 = jnp.concatenate([vp(r, sl), vc(r, sl)], axis=0)
                o, lse = _attn_block(q(r, sl), k2, v2, gq_ref[...], gk_ref[...], first)
                _set_rows(o_ref, r, d, sl, o)
                _set_rows(lse_ref, r, d, sl, jnp.broadcast_to(lse, (BLK, HEAD)))

    def cur(c0):
        return pl.BlockSpec(blk, lambda b, j, i: (b, i, c0 + j))

    def prev(c0):
        return pl.BlockSpec(blk, lambda b, j, i: (b, jnp.maximum(i - 1, 0), c0 + j))

    vec = pl.BlockSpec((1, HEAD), lambda b, j, i: (0, 0))
    out = pl.BlockSpec(blk, lambda b, j, i: (b, i, j))
    shp = jax.ShapeDtypeStruct((Bl, S, GW), F32)
    return _call(body, name=f"attn_fwd_g{g}", grid=(Bl, ncb, nb),
                 in_specs=[cur(cq), prev(ck), cur(ck), prev(cv), cur(cv), vec, vec],
                 out_specs=[out, out], out_shape=[shp, shp], scratch_shapes=_proj_stages(blk, d),
                 )(proj3, proj3, proj3, proj3, proj3, gq, gk)


def _attn_single_fwd(proj3, gq, gk, g, d):
    Bl, S, _ = proj3.shape
    blk, hc, ncb, cq, ck, cv = _attn_geom(g, d)

    def body(q_ref, k_ref, v_ref, gq_ref, gk_ref, o_ref, lse_ref, *stages):
        q, k, v = _proj_rows((q_ref, k_ref, v_ref), stages, d)
        for r in range(d):
            for h in range(hc):
                sl = slice(h * HEAD, (h + 1) * HEAD)
                o, lse = _attn_block(q(r, sl), k(r, sl), v(r, sl), gq_ref[...], gk_ref[...], True)
                _set_rows(o_ref, r, d, sl, o)
                _set_rows(lse_ref, r, d, sl, jnp.broadcast_to(lse, (BLK, HEAD)))

    def at(c0):
        return pl.BlockSpec(blk, lambda b, j: (b, 0, c0 + j))

    vec = pl.BlockSpec((1, HEAD), lambda b, j: (0, 0))
    shp = jax.ShapeDtypeStruct((Bl, S, GW), F32)
    return _call(body, name=f"attn_fwd_g{g}", grid=(Bl, ncb), in_specs=[at(cq), at(ck), at(cv), vec, vec],
                 out_specs=[at(0), at(0)], out_shape=[shp, shp], scratch_shapes=_proj_stages(blk, d)[:3],
                 )(proj3, proj3, proj3, gq, gk)


def _attn_single_bwd(proj3, gq, gk, o3, l3, do3, dl3, g, d):
    Bl, S, _ = proj3.shape
    blk, hc, ncb, cq, ck, cv = _attn_geom(g, d)

    def body(q_ref, k_ref, v_ref, gq_ref, gk_ref, o_ref, l_ref, do_ref, dl_ref,
             dq_ref, dk_ref, dv_ref, dgq_ref, dgk_ref, sq_ref, sk_ref, sv_ref, *stages):
        @pl.when((pl.program_id(0) == 0) & (pl.program_id(1) == 0))
        def _():
            dgq_ref[...] = jnp.zeros_like(dgq_ref)
            dgk_ref[...] = jnp.zeros_like(dgk_ref)

        dgq, dgk = jnp.zeros((1, HEAD), F32), jnp.zeros((1, HEAD), F32)
        q, k, v = _proj_rows((q_ref, k_ref, v_ref), stages, d)
        for r in range(d):
            for h in range(hc):
                sl = slice(h * HEAD, (h + 1) * HEAD)
                dq, dk, dv, a, b = _attn_block_bwd(
                    q(r, sl), k(r, sl), v(r, sl), gq_ref[...], gk_ref[...], True, _rows(do_ref, r, d, sl),
                    _rows(o_ref, r, d, sl), _rows(l_ref, r, d, sl)[:, :1], _rows(dl_ref, r, d, sl)[:, :1])
                _stage_rows(sq_ref, r, d, sl, dq)
                _stage_rows(sk_ref, r, d, sl, dk)
                _stage_rows(sv_ref, r, d, sl, dv)
                dgq, dgk = dgq + a, dgk + b
        dgq_ref[...] += dgq
        dgk_ref[...] += dgk
        dq_ref[0] = sq_ref[...].astype(dq_ref.dtype)
        dk_ref[0] = sk_ref[...].astype(dk_ref.dtype)
        dv_ref[0] = sv_ref[...].astype(dv_ref.dtype)

    def at(c0):
        return pl.BlockSpec(blk, lambda b, j: (b, 0, c0 + j))

    vec = pl.BlockSpec((1, HEAD), lambda b, j: (0, 0))
    shp = jax.ShapeDtypeStruct((Bl, S, GW), MXU_DTYPE)
    gshp = jax.ShapeDtypeStruct((1, HEAD), F32)
    return _call(body, name=f"attn_bwd_g{g}", grid=(Bl, ncb),
                 in_specs=[at(cq), at(ck), at(cv), vec, vec, at(0), at(0), at(0), at(0)],
                 out_specs=[at(0), at(0), at(0), vec, vec], out_shape=[shp, shp, shp, gshp, gshp],
                 scratch_shapes=[pltpu.VMEM(blk[1:], F32)] * 3 + _proj_stages(blk, d)[:3],
                 )(proj3, proj3, proj3, gq, gk, o3, l3, do3, dl3)


def _attn_bwd(proj3, gq, gk, o3, l3, do3, dl3, g, d):
    Bl, S, _ = proj3.shape
    blk, hc, ncb, cq, ck, cv = _attn_geom(g, d)
    nb = S // blk[1]
    if nb == 1:
        return _attn_single_bwd(proj3, gq, gk, o3, l3, do3, dl3, g, d)

    def body(q_ref, kp_ref, kc_ref, vp_ref, vc_ref, gq_ref, gk_ref, o_ref, l_ref, do_ref, dl_ref,
             dq_ref, dk_ref, dv_ref, dgq_ref, dgk_ref, ck_ref, cv_ref, sq_ref, sk_ref, sv_ref, *stages):
        i = pl.program_id(2)
        first = i == 0

        @pl.when((pl.program_id(0) == 0) & (pl.program_id(1) == 0) & first)
        def _():
            dgq_ref[...] = jnp.zeros_like(dgq_ref)
            dgk_ref[...] = jnp.zeros_like(dgk_ref)

        @pl.when(first)
        def _():
            ck_ref[...] = jnp.zeros_like(ck_ref)
            cv_ref[...] = jnp.zeros_like(cv_ref)

        @pl.when(i < nb)
        def _():
            dgq, dgk = jnp.zeros((1, HEAD), F32), jnp.zeros((1, HEAD), F32)
            q, kp, kc, vp, vc = _proj_rows((q_ref, kp_ref, kc_ref, vp_ref, vc_ref), stages, d)
            for r in range(d):
                rs = slice(r * BLK, (r + 1) * BLK)
                for h in range(hc):
                    sl = slice(h * HEAD, (h + 1) * HEAD)
                    k2 = jnp.concatenate([kp(r, sl), kc(r, sl)], axis=0)
                    v2 = jnp.concatenate([vp(r, sl), vc(r, sl)], axis=0)
                    dq, dk2, dv2, a, b = _attn_block_bwd(
                        q(r, sl), k2, v2, gq_ref[...], gk_ref[...], first, _rows(do_ref, r, d, sl),
                        _rows(o_ref, r, d, sl), _rows(l_ref, r, d, sl)[:, :1], _rows(dl_ref, r, d, sl)[:, :1])
                    _stage_rows(sq_ref, r, d, sl, dq)
                    _stage_rows(sk_ref, r, d, sl, ck_ref[rs, sl] + dk2[:BLK])
                    _stage_rows(sv_ref, r, d, sl, cv_ref[rs, sl] + dv2[:BLK])
                    ck_ref[rs, sl] = dk2[BLK:]
                    cv_ref[rs, sl] = dv2[BLK:]
                    dgq, dgk = dgq + a, dgk + b
            dgq_ref[...] += dgq
            dgk_ref[...] += dgk
            dq_ref[0] = sq_ref[...].astype(dq_ref.dtype)

        @pl.when(i == nb)
        def _():
            for r in range(d):
                rs = slice(r * BLK, (r + 1) * BLK)
                _stage_rows(sk_ref, r, d, slice(None), ck_ref[rs, :])
                _stage_rows(sv_ref, r, d, slice(None), cv_ref[rs, :])

        dk_ref[0] = sk_ref[...].astype(dk_ref.dtype)
        dv_ref[0] = sv_ref[...].astype(dv_ref.dtype)

    def cur(c0):
        return pl.BlockSpec(blk, lambda b, j, i: (b, jnp.minimum(i, nb - 1), c0 + j))

    def prev(c0):
        return pl.BlockSpec(blk, lambda b, j, i: (b, jnp.clip(i - 1, 0, nb - 1), c0 + j))

    vec = pl.BlockSpec((1, HEAD), lambda b, j, i: (0, 0))
    at_q = pl.BlockSpec(blk, lambda b, j, i: (b, jnp.minimum(i, nb - 1), j))
    at_k = pl.BlockSpec(blk, lambda b, j, i: (b, jnp.maximum(i - 1, 0), j))
    shp = jax.ShapeDtypeStruct((Bl, S, GW), MXU_DTYPE)
    gshp = jax.ShapeDtypeStruct((1, HEAD), F32)
    return _call(body, name=f"attn_bwd_g{g}", grid=(Bl, ncb, nb + 1),
                 in_specs=[cur(cq), prev(ck), cur(ck), prev(cv), cur(cv), vec, vec, at_q, at_q, at_q, at_q],
                 out_specs=[at_q, at_k, at_k, vec, vec], out_shape=[shp, shp, shp, gshp, gshp],
                 scratch_shapes=[pltpu.VMEM(blk[1:], F32)] * 5 + _proj_stages(blk, d),
                 )(proj3, proj3, proj3, proj3, proj3, gq, gk, o3, l3, do3, dl3)


def _combine_fwd(os, ls, proj2):
    T = proj2.shape[0]
    tr = _tile(T, 512)

    def body(o1, o2, o3, l1, l2, l3, z, a_ref, at_ref):
        a = _combine(o1[...], o2[...], o3[...], l1[...], l2[...], l3[...], z[...].astype(F32))
        a_ref[...] = a.astype(a_ref.dtype)
        at_ref[...] = a.T.astype(at_ref.dtype)

    row = pl.BlockSpec((tr, GW), lambda i: (i, 0))
    return _call(body, name="combine_fwd", grid=(T // tr,),
                 in_specs=[row] * 6 + [pl.BlockSpec((tr, GW), lambda i: (i, ZA // GW))],
                 out_specs=[row, pl.BlockSpec((GW, tr), lambda i: (0, i))],
                 out_shape=[jax.ShapeDtypeStruct((T, GW), MXU_DTYPE), jax.ShapeDtypeStruct((GW, T), MXU_DTYPE)],
                 )(*os, *ls, proj2)


def _combine_bwd(os, ls, proj2, da):
    T = proj2.shape[0]
    tr = _tile(T, 256)

    def body(o1, o2, o3, l1, l2, l3, z, da_ref, d1, d2, d3, e1, e2, e3, dz_ref):
        _, vjp = jax.vjp(_combine, o1[...], o2[...], o3[...], l1[...], l2[...], l3[...], z[...].astype(F32))
        go1, go2, go3, gl1, gl2, gl3, gz = vjp(da_ref[...])
        d1[...], d2[...], d3[...] = go1, go2, go3
        dz_ref[...] = gz.astype(dz_ref.dtype)
        for ref, gl in ((e1, gl1), (e2, gl2), (e3, gl3)):
            for h in range(HPG):
                sl = slice(h * HEAD, (h + 1) * HEAD)
                ref[:, sl] = jnp.broadcast_to(jnp.sum(gl[:, sl], axis=-1, keepdims=True), (tr, HEAD))

    row = pl.BlockSpec((tr, GW), lambda i: (i, 0))
    f = jax.ShapeDtypeStruct((T, GW), F32)
    outs = _call(body, name="combine_bwd", grid=(T // tr,),
                 in_specs=[row] * 6 + [pl.BlockSpec((tr, GW), lambda i: (i, ZA // GW)), row],
                 out_specs=[row] * 7, out_shape=[f] * 6 + [jax.ShapeDtypeStruct((T, GW), MXU_DTYPE)],
                 )(*os, *ls, proj2, da)
    return outs[:3], outs[3:6], outs[6]


def _shift_down(u, j, t):
    return jnp.where(t >= j, pltpu.roll(u, j, 0), 0.0)


def _shift_up(u, j, t):
    n = u.shape[0]
    return jnp.where(t < n - j, pltpu.roll(u, n - j, 0), 0.0)


def _conv_specs(Bl, S, cw):
    def sec(c0):
        return pl.BlockSpec((1, S, cw), lambda j, b: (b, 0, c0 // cw + j))
    return [sec(CB), sec(CC), sec(CV), sec(ZC)], pl.BlockSpec((3, cw), lambda j, b: (0, j))


def _conv_fwd(proj3, conv_w):
    Bl, S, _ = proj3.shape
    cw = 256
    secs, wspec = _conv_specs(Bl, S, cw)

    def body(b_ref, c_ref, v_ref, z_ref, w_ref, o_ref, ot_ref):
        t = lax.broadcasted_iota(jnp.int32, (S, cw), 0)
        u = c_ref[0].astype(F32) * v_ref[0].astype(F32)
        y = w_ref[0:1, :] * u + w_ref[1:2, :] * _shift_down(u, 1, t) + w_ref[2:3, :] * _shift_down(u, 2, t)
        out = b_ref[0].astype(F32) * y * _silu(z_ref[0].astype(F32))
        o_ref[0] = out.astype(o_ref.dtype)
        ot_ref[...] = out.T.astype(ot_ref.dtype)

    return _call(body, name="conv_fwd", grid=(CONVW // cw, Bl), in_specs=secs + [wspec],
                 out_specs=[pl.BlockSpec((1, S, cw), lambda j, b: (b, 0, j)), pl.BlockSpec((cw, S), lambda j, b: (j, b))],
                 out_shape=[jax.ShapeDtypeStruct((Bl, S, CONVW), MXU_DTYPE),
                            jax.ShapeDtypeStruct((CONVW, Bl * S), MXU_DTYPE)])(proj3, proj3, proj3, proj3, conv_w)


def _conv_bwd(proj3, conv_w, dcc3):
    Bl, S, _ = proj3.shape
    cw = 256
    secs, wspec = _conv_specs(Bl, S, cw)

    def body(b_ref, c_ref, v_ref, z_ref, w_ref, d_ref, db_ref, dc_ref, dv_ref, dz_ref, dw_ref):
        t = lax.broadcasted_iota(jnp.int32, (S, cw), 0)
        bv, cv, vv, zv = (r[0].astype(F32) for r in (b_ref, c_ref, v_ref, z_ref))
        dv = d_ref[0]
        u = cv * vv
        u1, u2 = _shift_down(u, 1, t), _shift_down(u, 2, t)
        y = w_ref[0:1, :] * u + w_ref[1:2, :] * u1 + w_ref[2:3, :] * u2
        sg = _sig(zv)
        sz = zv * sg
        gy = dv * bv * sz
        db_ref[0] = (dv * y * sz).astype(db_ref.dtype)
        dz_ref[0] = (dv * bv * y * sg * (1.0 + zv * (1.0 - sg))).astype(dz_ref.dtype)
        du = w_ref[0:1, :] * gy + w_ref[1:2, :] * _shift_up(gy, 1, t) + w_ref[2:3, :] * _shift_up(gy, 2, t)
        dc_ref[0] = (du * vv).astype(dc_ref.dtype)
        dv_ref[0] = (du * cv).astype(dv_ref.dtype)

        @pl.when(pl.program_id(1) == 0)
        def _():
            dw_ref[...] = jnp.zeros_like(dw_ref)

        dw_ref[0:1, :] += jnp.sum(gy * u, axis=0, keepdims=True)
        dw_ref[1:2, :] += jnp.sum(gy * u1, axis=0, keepdims=True)
        dw_ref[2:3, :] += jnp.sum(gy * u2, axis=0, keepdims=True)

    blk = pl.BlockSpec((1, S, cw), lambda j, b: (b, 0, j))
    shp = jax.ShapeDtypeStruct((Bl, S, CONVW), MXU_DTYPE)
    return _call(body, name="conv_bwd", grid=(CONVW // cw, Bl), in_specs=secs + [wspec, blk],
                 out_specs=[blk] * 4 + [wspec], out_shape=[shp] * 4 + [jax.ShapeDtypeStruct((3, CONVW), F32)],
                 )(proj3, proj3, proj3, proj3, conv_w, dcc3)


def _mem_specs(S, tq):
    q = pl.BlockSpec((1, tq, MEMW), lambda b, j: (b, j, MQ // MEMW))
    z = pl.BlockSpec((1, tq, MEMW), lambda b, j: (b, j, ZM // MEMW))
    kv = pl.BlockSpec((1, MEM_HD, 2 * MEMW), lambda b, j: (b, 0, 0))
    vec = pl.BlockSpec((1, MEM_HD), lambda b, j: (0, 0))
    blk = pl.BlockSpec((1, tq, MEMW), lambda b, j: (b, j, 0))
    return q, z, kv, vec, blk


def _mem_fwd(proj3, mkv3, gq, gk):
    Bl, S, _ = proj3.shape
    tq = _tile(S, 512)
    q, z, kv, vec, blk = _mem_specs(S, tq)

    def body(q_ref, z_ref, kv_ref, gq_ref, gk_ref, o_ref, ot_ref):
        out = _mem_block(q_ref[0].astype(F32), z_ref[0].astype(F32), kv_ref[0], gq_ref[...], gk_ref[...])
        o_ref[0] = out.astype(o_ref.dtype)
        ot_ref[...] = out.T.astype(ot_ref.dtype)

    nq = S // tq
    return _call(body, name="mem_fwd", grid=(Bl, nq), in_specs=[q, z, kv, vec, vec],
                 out_specs=[blk, pl.BlockSpec((MEMW, tq), lambda b, j: (0, b * nq + j))],
                 out_shape=[jax.ShapeDtypeStruct((Bl, S, MEMW), MXU_DTYPE),
                            jax.ShapeDtypeStruct((MEMW, Bl * S), MXU_DTYPE)])(proj3, proj3, mkv3, gq, gk)


def _mem_bwd(proj3, mkv3, gq, gk, dmo3):
    Bl, S, _ = proj3.shape
    tq = _tile(S, 256)
    q, z, kv, vec, blk = _mem_specs(S, tq)

    def body(q_ref, z_ref, kv_ref, gq_ref, gk_ref, d_ref, dq_ref, dz_ref, dkv_ref, dgq_ref, dgk_ref):
        _, vjp = jax.vjp(_mem_block, q_ref[0].astype(F32), z_ref[0].astype(F32), kv_ref[0], gq_ref[...], gk_ref[...])
        dq, dz, dkv, dgq, dgk = vjp(d_ref[0])
        dq_ref[0] = dq.astype(dq_ref.dtype)
        dz_ref[0] = dz.astype(dz_ref.dtype)
        j = pl.program_id(1)

        @pl.when(j == 0)
        def _():
            dkv_ref[0] = jnp.zeros_like(dkv)

        @pl.when((j == 0) & (pl.program_id(0) == 0))
        def _():
            dgq_ref[...] = jnp.zeros_like(dgq_ref)
            dgk_ref[...] = jnp.zeros_like(dgk_ref)

        dkv_ref[0] += dkv
        dgq_ref[...] += dgq
        dgk_ref[...] += dgk

    shp = jax.ShapeDtypeStruct((Bl, S, MEMW), MXU_DTYPE)
    gshp = jax.ShapeDtypeStruct((1, MEM_HD), F32)
    return _call(body, name="mem_bwd", grid=(Bl, S // tq), in_specs=[q, z, kv, vec, vec, blk],
                 out_specs=[blk, blk, kv, vec, vec],
                 out_shape=[shp, shp, jax.ShapeDtypeStruct(mkv3.shape, F32), gshp, gshp],
                 )(proj3, proj3, mkv3, gq, gk, dmo3)


def _merge_specs(T, D, tm, tn):
    def act(w):
        return pl.BlockSpec((tm, w), lambda i, n: (i, 0))

    def wsp(w):
        return pl.BlockSpec((w, tn), lambda i, n: (0, n))

    gates = [pl.BlockSpec((tm, tn), lambda i, n, k=k: (i, (G0 + k * D) // tn + n)) for k in range(3)]
    tile = pl.BlockSpec((tm, tn), lambda i, n: (i, n))
    return act, wsp, gates, tile


def _merge_fwd(a, cc, mo, wa, wc, wm, proj2):
    T, D = a.shape[0], wa.shape[1]
    tm, tn = _tile(T, 1024), _tile(D, 512)
    act, wsp, gates, tile = _merge_specs(T, D, tm, tn)

    def body(a_ref, c_ref, m_ref, wa_ref, wc_ref, wm_ref, g0, g1, g2, mg_ref, mt_ref, pa_ref, pc_ref, pm_ref):
        pa = jnp.dot(a_ref[...], wa_ref[...], preferred_element_type=F32)
        pc = jnp.dot(c_ref[...], wc_ref[...], preferred_element_type=F32)
        pm = jnp.dot(m_ref[...], wm_ref[...], preferred_element_type=F32)
        mg = _sig(g0[...].astype(F32)) * pa + _sig(g1[...].astype(F32)) * pc + _sig(g2[...].astype(F32)) * pm
        mg_ref[...] = mg.astype(mg_ref.dtype)
        mt_ref[...] = mg.T.astype(mt_ref.dtype)
        pa_ref[...] = pa.astype(pa_ref.dtype)
        pc_ref[...] = pc.astype(pc_ref.dtype)
        pm_ref[...] = pm.astype(pm_ref.dtype)

    shp = jax.ShapeDtypeStruct((T, D), MXU_DTYPE)
    return _call(body, name="merge_fwd", grid=(T // tm, D // tn),
                 in_specs=[act(GW), act(CONVW), act(MEMW), wsp(GW), wsp(CONVW), wsp(MEMW)] + gates,
                 out_specs=[tile, pl.BlockSpec((tn, tm), lambda i, n: (n, i)), tile, tile, tile],
                 out_shape=[shp, jax.ShapeDtypeStruct((D, T), MXU_DTYPE), shp, shp, shp],
                 )(a, cc, mo, wa, wc, wm, proj2, proj2, proj2)


def _merge_bwd(dyb, w_out, proj2, pa, pc, pm):
    T, D = dyb.shape
    tm, tn = _tile(T, 1024), _tile(D, 512)
    _, _, gates, tile = _merge_specs(T, D, tm, tn)

    def body(dy_ref, w_ref, g0, g1, g2, p0, p1, p2, dp0, dp1, dp2, dg0, dg1, dg2):
        dm = lax.dot_general(dy_ref[...], w_ref[...], _DIMS["nt"], preferred_element_type=F32)
        for g_ref, p_ref, dp_ref, dg_ref in ((g0, p0, dp0, dg0), (g1, p1, dp1, dg1), (g2, p2, dp2, dg2)):
            gt = _sig(g_ref[...].astype(F32))
            dp_ref[...] = (gt * dm).astype(dp_ref.dtype)
            dg_ref[...] = (dm * p_ref[...].astype(F32) * gt * (1.0 - gt)).astype(dg_ref.dtype)

    shp = jax.ShapeDtypeStruct((T, D), MXU_DTYPE)
    return _call(body, name="merge_bwd", grid=(T // tm, D // tn),
                 in_specs=[pl.BlockSpec((tm, D), lambda i, n: (i, 0)), pl.BlockSpec((tn, D), lambda i, n: (n, 0))]
                 + gates + [tile] * 3,
                 out_specs=[tile] * 6, out_shape=[shp] * 6)(dyb, w_out, proj2, proj2, proj2, pa, pc, pm)


def _out_loss(merged, w_out, x, tgt):
    T, D = x.shape
    tm = _tile(T, 512)

    def body(m_ref, w_ref, x_ref, t_ref, dy_ref, dyb_ref, loss_ref):
        err = x_ref[...] + jnp.dot(m_ref[...], w_ref[...], preferred_element_type=F32) - t_ref[...]
        dy = err * (1.0 / D)
        dy_ref[...] = dy
        dyb_ref[...] = dy.astype(dyb_ref.dtype)

        @pl.when(pl.program_id(0) == 0)
        def _():
            loss_ref[...] = jnp.zeros_like(loss_ref)

        loss_ref[...] += jnp.sum(err * err) * (0.5 / D)

    row = pl.BlockSpec((tm, D), lambda i: (i, 0))
    return _call(body, name="out_loss", grid=(T // tm,),
                 in_specs=[row, pl.BlockSpec((D, D), lambda i: (0, 0)), row, row],
                 out_specs=[row, row, pl.BlockSpec((1, 128), lambda i: (0, 0))],
                 out_shape=[jax.ShapeDtypeStruct((T, D), F32), jax.ShapeDtypeStruct((T, D), MXU_DTYPE),
                            jax.ShapeDtypeStruct((1, 128), F32)])(merged, w_out, x, tgt)


def _proj_chunk(hb, w, meta, j, nslots, half, buf, name):
    T, D = hb.shape
    Cs = w.shape[1] // 4
    tm, tn = _tile(T, 1024), _tile(Cs // 2, 2176)
    nh = Cs // 2 // tn
    per = nh if half is not None else 2 * nh

    def body(meta_ref, a_ref, b_ref, *rest):
        rest[-1][...] = jnp.dot(a_ref[...], b_ref[...], preferred_element_type=F32).astype(rest[-1].dtype)

    def tile(n, m):
        if half is None:
            return n % per
        return (m[4] if half == 0 else 1 - m[4]) * nh + n % per

    in_specs = [pl.BlockSpec((tm, D), lambda n, i, m: (i, 0)),
                pl.BlockSpec((D, tn), lambda n, i, m: (0, (j + n // per) * 2 * nh + tile(n, m)))]
    args = [meta, hb, w]
    if buf is not None:
        in_specs.append(ANY)
        args.append(buf)
    spec = pltpu.PrefetchScalarGridSpec(
        num_scalar_prefetch=1, grid=(nslots * per, T // tm), in_specs=in_specs,
        out_specs=pl.BlockSpec((tm, tn), lambda n, i, m: (i, m[j + n // per] * 2 * nh + tile(n, m))))
    return _call(body, name=name, grid_spec=spec, out_shape=jax.ShapeDtypeStruct((T, 4 * Cs), PROJ_DTYPE),
                 aliases={} if buf is None else {3: 0})(*args)


def _norms(x, mem, norm_g, mem_norm_g):
    D = x.shape[-1]
    hb, hbt = _rms_fwd(x.reshape(-1, D), norm_g.reshape(1, D), "rms_x")
    mhb, _ = _rms_fwd(mem.reshape(-1, D), mem_norm_g.reshape(1, D), "rms_mem")
    return hb, hbt, mhb


def _attention_fwd(proj2, Bl, gq_all, gk_all):
    T, IN = proj2.shape
    proj3 = proj2.reshape(Bl, T // Bl, IN)
    os, ls = [], []
    for g, d in enumerate(DILATIONS):
        o, l = _attn_fwd(proj3, gq_all[g:g + 1], gk_all[g:g + 1], g, d)
        os.append(o.reshape(T, GW))
        ls.append(l.reshape(T, GW))
    return os, ls, _combine_fwd(os, ls, proj2)


def _conv_branch_fwd(proj2, Bl, conv_w):
    T, IN = proj2.shape
    cc, cct = _conv_fwd(proj2.reshape(Bl, T // Bl, IN), conv_w)
    return cc.reshape(T, CONVW), cct


def _weight_grads(x, mem, tgt, norm_g, mem_norm_g, gq_all, gk_all, conv_w, mem_gq, mem_gk, W, pre, early=None):
    Bl, S, D = x.shape
    T = Bl * S
    hb, hbt, mhb, proj2, os, ls, (a, at), (cc, cct) = pre
    IN = proj2.shape[1]
    proj3 = proj2.reshape(Bl, S, IN)
    x2, tgt2 = x.reshape(T, D), tgt.reshape(T, D)
    mem2 = mem.reshape(-1, D)
    ng, mng = norm_g.reshape(1, D), mem_norm_g.reshape(1, D)
    mgq, mgk = mem_gq.reshape(1, MEM_HD), mem_gk.reshape(1, MEM_HD)
    gqs = [gq_all[g:g + 1] for g in range(NGROUP)]
    gks = [gk_all[g:g + 1] for g in range(NGROUP)]

    mkv = _matmul(mhb, W["mem_w_kv"], "nn", F32, name="mem_kv", tm=512, tn=1024, tk=D)
    mkv3 = mkv.reshape(Bl, -1, 2 * MEMW)
    mo, mot = _mem_fwd(proj3, mkv3, mgq, mgk)
    mo = mo.reshape(T, MEMW)
    merged, mergedt, pa, pc, pm = _merge_fwd(a, cc, mo, W["w_br_attn"], W["w_br_conv"], W["w_br_mem"], proj2)
    dy, dyb, loss = _out_loss(merged, W["w_out"], x2, tgt2)

    G = {}
    G["w_out"] = _matmul(mergedt, dyb, "nn", WIRE_DTYPE, name="dw_out", tm=1024, tn=512, tk=T)
    dpa, dpc, dpm, dg0, dg1, dg2 = _merge_bwd(dyb, W["w_out"], proj2, pa, pc, pm)
    G["w_br_attn"] = _matmul(at, dpa, "nn", WIRE_DTYPE, name="dw_br_attn", tm=512, tn=512, tk=T)
    G["w_br_conv"] = _matmul(cct, dpc, "nn", WIRE_DTYPE, name="dw_br_conv", tm=1024, tn=512, tk=T)
    G["w_br_mem"] = _matmul(mot, dpm, "nn", WIRE_DTYPE, name="dw_br_mem", tm=1024, tn=512, tk=T)
    da = _matmul(dpa, W["w_br_attn"], "nt", F32, name="d_attn", tm=1024, tn=512, tk=D)
    dcc = _matmul(dpc, W["w_br_conv"], "nt", F32, name="d_conv", tm=1024, tn=1024, tk=D)
    dmo = _matmul(dpm, W["w_br_mem"], "nt", F32, name="d_mem", tm=1024, tn=1024, tk=D)
    dmq, dzm, dmkv3, dmgq, dmgk = _mem_bwd(proj3, mkv3, mgq, mgk, dmo.reshape(Bl, S, MEMW))
    dmkv = _cast(dmkv3.reshape(-1, 2 * MEMW), "cast_dmkv")
    G["mem_w_kv"] = _matmul(mhb, dmkv, "tn", WIRE_DTYPE, name="dw_mem_kv", tm=1024, tn=1024, tk=512)
    early_state, da = (None, da) if early is None else early(G, da)
    dmh = _matmul(dmkv, W["mem_w_kv"], "nt", F32, name="d_memh", tm=512, tn=1024, tk=2 * MEMW)
    _, dmng = _rms_bwd(mem2, dmh, mng, None, "rms_mem_bwd")

    dos, dls, dza = _combine_bwd(os, ls, proj2, da)
    dqs, dks, dvs, dgq, dgk = [], [], [], [], []
    for g, d in enumerate(DILATIONS):
        dq, dk, dv, gq_g, gk_g = _attn_bwd(proj3, gqs[g], gks[g], os[g].reshape(Bl, S, GW), ls[g].reshape(Bl, S, GW),
                                           dos[g].reshape(Bl, S, GW), dls[g].reshape(Bl, S, GW), g, d)
        dqs.append(dq.reshape(T, GW).astype(MXU_DTYPE))
        dks.append(dk.reshape(T, GW).astype(MXU_DTYPE))
        dvs.append(dv.reshape(T, GW).astype(MXU_DTYPE))
        dgq.append(gq_g)
        dgk.append(gk_g)
    dcb, dcc_, dcv, dzc, dconv_w = _conv_bwd(proj3, conv_w, dcc.reshape(Bl, S, CONVW))

    dproj = jnp.concatenate(dqs + dks + dvs + [dza] + [t.reshape(T, CONVW) for t in (dcb, dcc_, dcv, dzc)]
                            + [dmq.reshape(T, MEMW), dzm.reshape(T, MEMW), dg0, dg1, dg2], axis=1)
    small = [loss, None, dmng] + dgq + dgk + [dconv_w.reshape(1, 3 * CONVW), dmgq, dmgk]
    return G, (dproj, x2, ng, dy, small), early_state


def _dw_in_half(hbt, dproj, pos, own, name):
    D, T = hbt.shape
    IN = dproj.shape[1]
    R, tn = D // 2, _tile(IN, 512)

    def body(pos_ref, a_ref, b_ref, o_ref):
        o_ref[...] = jnp.dot(a_ref[...], b_ref[...], preferred_element_type=F32).astype(o_ref.dtype)

    spec = pltpu.PrefetchScalarGridSpec(
        num_scalar_prefetch=1, grid=(IN // tn,),
        in_specs=[pl.BlockSpec((R, T), lambda j, p: (p[1] if own else 1 - p[1], 0)),
                  pl.BlockSpec((T, tn), lambda j, p: (0, j))],
        out_specs=pl.BlockSpec((R, tn), lambda j, p: (0, j)))
    return _call(body, name=name, grid_spec=spec, out_shape=jax.ShapeDtypeStruct((R, IN), WIRE_DTYPE))(pos, hbt, dproj)


def _d_h(dproj, w, order):
    T, IN = dproj.shape
    D, Cs = w.shape[0], IN // 4
    tm, tn = _tile(T, 1024), _tile(D, 1024)

    def body(order_ref, a_ref, b_ref, o_ref, acc_ref):
        part = lax.dot_general(a_ref[...], b_ref[...], _DIMS["nt"], preferred_element_type=F32)
        k = pl.program_id(2)

        @pl.when(k == 0)
        def _():
            acc_ref[...] = part

        @pl.when(k > 0)
        def _():
            acc_ref[...] += part

        @pl.when(k == 3)
        def _():
            o_ref[...] = acc_ref[...]

    spec = pltpu.PrefetchScalarGridSpec(
        num_scalar_prefetch=1, grid=(T // tm, D // tn, 4),
        in_specs=[pl.BlockSpec((tm, Cs), lambda i, n, k, o: (i, o[k])), pl.BlockSpec((tn, Cs), lambda i, n, k, o: (n, k))],
        out_specs=pl.BlockSpec((tm, tn), lambda i, n, k, o: (i, n)), scratch_shapes=[pltpu.VMEM((tm, tn), F32)])
    return _call(body, name="d_h", grid_spec=spec, out_shape=jax.ShapeDtypeStruct((T, D), F32))(order, dproj, w)


def _input_grad(rest, w_in, order):
    dproj, x2, ng, dy, small = rest
    dh = _d_h(dproj, w_in, order)
    grad_x, dng = _rms_bwd(x2, dh, ng, dy, "rms_x_bwd")
    small = [dng if t is None else t for t in small]
    return grad_x, jnp.concatenate(small, axis=1)


def _local_step(x, mem, tgt, norm_g, mem_norm_g, gq_all, gk_all, conv_w, mem_gq, mem_gk, W):
    hb, hbt, mhb = _norms(x, mem, norm_g, mem_norm_g)
    Cs = W["w_in"].shape[1] // 4
    shards = (0, 2, 1, 3)
    order = jnp.array(shards, dtype=jnp.int32)
    w_rel = jnp.concatenate([W["w_in"][:, s * Cs:(s + 1) * Cs] for s in shards], axis=1)
    meta = jnp.array(shards + (0,), dtype=jnp.int32)
    proj2 = _proj_chunk(hb, w_rel, meta, 0, 1, None, None, "proj_0")
    for j, nslots in ((1, 2), (3, 1)):
        for half in (1, 0):
            proj2 = _proj_chunk(hb, w_rel, meta, j, nslots, half, proj2, f"proj_{j}_{half}")
    pre = (hb, hbt, mhb, proj2, *_attention_fwd(proj2, x.shape[0], gq_all, gk_all),
           _conv_branch_fwd(proj2, x.shape[0], conv_w))
    G, rest, _ = _weight_grads(x, mem, tgt, norm_g, mem_norm_g, gq_all, gk_all, conv_w, mem_gq, mem_gk, W, pre)
    pos = jnp.zeros((2,), jnp.int32)
    G["w_in"] = jnp.concatenate([_dw_in_half(hbt, rest[0], pos, True, "dw_in_own"),
                                 _dw_in_half(hbt, rest[0], pos, False, "dw_in_sibling")], axis=0)
    grad_x, small = _input_grad(rest, w_rel, order)
    return grad_x.reshape(x.shape), G, small


BIG = (("w_in", "col"), ("mem_w_kv", "row"), ("w_br_attn", "col"), ("w_br_conv", "col"),
       ("w_br_mem", "col"), ("w_out", "row"))


def _coords():
    return lax.axis_index("x"), lax.axis_index("y"), lax.axis_index("c")


def _other_chips(x, y):
    return [(1 - x, y), (x, 1 - y), (1 - x, 1 - y)]


def _half(ref, kind, c):
    R, C = ref.shape
    if kind == "col":
        return ref.at[pl.ds(c * (R // 2), R // 2), :]
    return ref.at[:, pl.ds(c * (C // 2), C // 2)]


def _shard(ref, kind, s):
    R, C = ref.shape
    if kind == "col":
        return ref.at[:, pl.ds(s * (C // 4), C // 4)]
    return ref.at[pl.ds(s * (R // 4), R // 4), :]


def _piece(ref, kind, s, c):
    R, C = ref.shape
    if kind == "col":
        return ref.at[pl.ds(c * (R // 2), R // 2), pl.ds(s * (C // 4), C // 4)]
    return ref.at[pl.ds(s * (R // 4), R // 4), pl.ds(c * (C // 2), C // 2)]


def _remote(src, dst, sems_s, sems_r, k, dev):
    return pltpu.make_async_remote_copy(src_ref=src, dst_ref=dst, send_sem=sems_s.at[k], recv_sem=sems_r.at[k],
                                        device_id=dev, device_id_type=MESH)


HBM = pl.BlockSpec(memory_space=pltpu.HBM)
SEM = pl.BlockSpec(memory_space=pltpu.SEMAPHORE)
EFFECT = pltpu.SideEffectType.DATAFLOW_SIDE_EFFECTING


def _hbm(a):
    return pltpu.with_memory_space_constraint(a, pltpu.HBM)


def _start_copies(name, arrays, ncopies, make):
    n = len(arrays)

    def body(*refs):
        for cp in make(refs[:n], refs[n], refs[n + 1]):
            cp.start()

    outs = pl.pallas_call(
        body, name=name,
        out_shape=(pltpu.SemaphoreType.DMA((ncopies,)), pltpu.SemaphoreType.DMA((ncopies,)),
                   *[jax.ShapeDtypeStruct(t.shape, t.dtype) for t in arrays]),
        in_specs=[HBM] * n, out_specs=(SEM, SEM, *([HBM] * n)),
        input_output_aliases={i: i + 2 for i in range(n)},
        compiler_params=pltpu.CompilerParams(has_side_effects=EFFECT),
    )(*[_hbm(t) for t in arrays])
    return outs[0], outs[1], list(outs[2:])


def _wait_copies(name, send, recv, arrays, make, after):
    n = len(arrays)

    def body(*refs):
        for cp in make(refs[:n], refs[n], refs[n + 1]):
            cp.wait_send()
            cp.wait_recv()

    outs = pl.pallas_call(
        body, name=name, out_shape=[jax.ShapeDtypeStruct(t.shape, t.dtype) for t in arrays],
        in_specs=[HBM] * n + [SEM, SEM, ANY], out_specs=[HBM] * n,
        input_output_aliases={i: i for i in range(n)},
        compiler_params=pltpu.CompilerParams(has_side_effects=EFFECT),
    )(*arrays, send, recv, after)
    return list(outs)


def _w_in_copies(relations):
    def make(refs, send, recv):
        x, y, c = _coords()
        me = 2 * x + y
        chips = _other_chips(x, y)
        w, conv = refs[0], refs[1]
        cps = []
        for i, k in enumerate(relations):
            cps.append(_remote(_column_half(w, 0, c), _column_half(w, 1 + k, c), send, recv, 2 * i, (*chips[k], c)))
            mine = _shard(conv, "col", me)
            cps.append(_remote(mine, mine, send, recv, 2 * i + 1, (*chips[k], c)))
        return cps
    return make


def _column_half(w, slot, c):
    half = w.shape[1] // 8
    return w.at[:, pl.ds((2 * slot + c) * half, half)]


def _w_in_forward(relations):
    def make(refs, send, recv):
        x, y, c = _coords()
        cps = []
        for i, k in enumerate(relations):
            got = _column_half(refs[0], 1 + k, c)
            cps.append(_remote(got, got, send, recv, i, (x, y, 1 - c)))
        return cps
    return make


def _sibling_copy(refs, send, recv):
    x, y, c = _coords()
    return [_remote(refs[0], refs[1], send, recv, 0, (x, y, 1 - c))]


def _other_weight_copies(refs, send, recv):
    x, y, c = _coords()
    me = 2 * x + y
    cps = []
    for k, chip in enumerate(_other_chips(x, y)):
        for p, (_, kind) in enumerate(BIG[1:]):
            mine = _piece(refs[p], kind, me, c)
            cps.append(_remote(mine, mine, send, recv, 3 * p + k, (*chip, c)))
    return cps


def _other_weight_forward(refs, send, recv):
    x, y, c = _coords()
    cps = []
    for k, chip in enumerate(_other_chips(x, y)):
        s = 2 * chip[0] + chip[1]
        for p, (_, kind) in enumerate(BIG[1:]):
            got = _piece(refs[p], kind, s, c)
            cps.append(_remote(got, got, send, recv, 3 * p + k, (x, y, 1 - c)))
    return cps


def _share_copies(group):
    def make(refs, send, recv):
        x, y, c = _coords()
        cps = []
        for p, (_, kind) in enumerate(group):
            mine = _half(refs[p], kind, c)
            cps.append(_remote(mine, mine, send, recv, p, (x, y, 1 - c)))
        return cps
    return make


def _sibling_forward(name, arrays, ncp, halves):
    n = len(arrays)

    def body(*refs):
        outs = refs[n:2 * n]
        send, recv = refs[2 * n:]
        x, y, c = _coords()
        sib = (x, y, 1 - c)
        cps = [_remote(got, got, send, recv, i, sib) for i, got in enumerate(halves(outs, c))]
        for cp in cps:
            cp.start()
        for cp in cps:
            cp.wait_send()
        for i, got in enumerate(halves(outs, 1 - c)):
            _remote(got, got, send, recv, i, sib).wait_recv()

    return pl.pallas_call(
        body, name=name, out_shape=[jax.ShapeDtypeStruct(t.shape, t.dtype) for t in arrays],
        in_specs=[ANY] * n, out_specs=[ANY] * n, input_output_aliases={i: i for i in range(n)},
        scratch_shapes=[pltpu.SemaphoreType.DMA((ncp,)), pltpu.SemaphoreType.DMA((ncp,))],
    )(*arrays)


def _landed_halves(refs, c):
    return [_half(r, "col", c) for r in refs]


def _other_weight_halves(refs, c):
    x, y, _ = _coords()
    out = []
    for chip in _other_chips(x, y):
        s = 2 * chip[0] + chip[1]
        out += [_piece(refs[p], kind, s, c) for p, (_, kind) in enumerate(BIG[1:])]
    return out


def _sibling_exchange(grads, group, name):
    n = len(group)
    shapes = []
    for (_, kind), g in zip(group, grads):
        R, C = g.shape
        shapes.append(jax.ShapeDtypeStruct((R // 2, C) if kind == "col" else (R, C // 2), g.dtype))

    def body(*refs):
        ins, outs = refs[:n], refs[n:2 * n]
        send, recv = refs[2 * n:]
        x, y, c = _coords()
        sib = (x, y, 1 - c)
        cps = [_remote(_half(ins[p], group[p][1], 1 - c), outs[p], send, recv, p, sib) for p in range(n)]
        for cp in cps:
            cp.start()
        for cp in cps:
            cp.wait()

    return pl.pallas_call(
        body, name=name, out_shape=shapes, in_specs=[ANY] * n, out_specs=[ANY] * n,
        scratch_shapes=[pltpu.SemaphoreType.DMA((n,)), pltpu.SemaphoreType.DMA((n,))],
    )(*grads)


def _presum(g, got, kind, pos, name):
    R, C = got.shape
    tr, tc = _tile(R, 512, 16), _tile(C, 2048)
    nr, nc = R // tr, C // tc

    def body(pos_ref, a_ref, b_ref, o_ref):
        o_ref[...] = (a_ref[...].astype(F32) + b_ref[...].astype(F32)).astype(o_ref.dtype)

    blk = pl.BlockSpec((tr, tc), lambda i, j, pos_ref: (i, j))
    if g.shape == got.shape:
        mine = blk
    elif kind == "col":
        mine = pl.BlockSpec((tr, tc), lambda i, j, pos_ref: (pos_ref[1] * nr + i, j))
    else:
        mine = pl.BlockSpec((tr, tc), lambda i, j, pos_ref: (i, pos_ref[1] * nc + j))
    spec = pltpu.PrefetchScalarGridSpec(num_scalar_prefetch=1, grid=(nr, nc), in_specs=[mine, blk], out_specs=blk)
    return _call(body, name=name, grid_spec=spec, out_shape=jax.ShapeDtypeStruct((R, C), WIRE_DTYPE))(pos, g, got)


def _chip_copies(group):
    n = len(group)

    def make(refs, send, recv):
        x, y, c = _coords()
        cps = []
        for k, chip in enumerate(_other_chips(x, y)):
            s = 2 * chip[0] + chip[1]
            for p in range(n):
                cps.append(_remote(_shard(refs[p], group[p][1], s), refs[n + p].at[k], send, recv, 3 * p + k, (*chip, c)))
        return cps
    return make


def _landing_zones(pres, group):
    lands = []
    for (_, kind), g in zip(group, pres):
        R, C = g.shape
        lands.append(lax.empty((3, R, C // 4) if kind == "col" else (3, R // 4, C), g.dtype))
    return lands


def _exchange_start(G, group, pos, carry, tag, pres=None):
    n = len(group)
    if pres is None:
        parts = [G[name] for name, _ in group]
        got = _sibling_exchange(parts, group, "sibling_exchange_" + tag)
        pres = [_presum(parts[p], got[p], kind, pos, "presum_" + name) for p, (name, kind) in enumerate(group)]
    make = _chip_copies(group)
    send, recv, thru = _start_copies("chip_exchange_start_" + tag, [*pres, *_landing_zones(pres, group), carry], 3 * n, make)
    return (send, recv, thru[:2 * n], make, tag), thru[2 * n]


def _exchange_wait(state, after):
    send, recv, arrays, make, tag = state
    thru = _wait_copies("chip_exchange_wait_" + tag, send, recv, arrays, make, after)
    n = len(thru) // 2
    return thru[:n], thru[n:]


def _reduce_into_shard(slots, pre, kind, pos, name):
    K, R, C = slots.shape
    tr, tc = _tile(R, 512, 16), _tile(C, 2176)
    nr, nc = R // tr, C // tc

    def body(pos_ref, s_ref, p_ref, o_ref):
        acc = p_ref[...].astype(F32)
        for k in range(K):
            acc = acc + s_ref[k].astype(F32)
        o_ref[...] = acc

    if kind == "col":
        own = pl.BlockSpec((tr, tc), lambda i, j, pos_ref: (i, pos_ref[0] * nc + j))
        full, out = (2 * R, C), pl.BlockSpec((tr, tc), lambda i, j, pos_ref: (pos_ref[1] * nr + i, j))
    else:
        own = pl.BlockSpec((tr, tc), lambda i, j, pos_ref: (pos_ref[0] * nr + i, j))
        full, out = (R, 2 * C), pl.BlockSpec((tr, tc), lambda i, j, pos_ref: (i, pos_ref[1] * nc + j))
    spec = pltpu.PrefetchScalarGridSpec(
        num_scalar_prefetch=1, grid=(nr, nc),
        in_specs=[pl.BlockSpec((K, tr, tc), lambda i, j, pos_ref: (0, i, j)), own], out_specs=out)
    return _call(body, name=name, grid_spec=spec, out_shape=jax.ShapeDtypeStruct(full, F32))(pos, slots, pre)


def _share_reduced(reds):
    n = len(BIG)

    def body(*refs):
        outs = refs[n:2 * n]
        send, recv = refs[2 * n:]
        x, y, c = _coords()
        sib = (x, y, 1 - c)
        cps = []
        for p in range(n):
            mine = _half(outs[p], BIG[p][1], c)
            cps.append(_remote(mine, mine, send, recv, p, sib))
        for cp in cps:
            cp.start()
        for cp in cps:
            cp.wait_send()
        for p in range(n):
            got = _half(outs[p], BIG[p][1], 1 - c)
            _remote(got, got, send, recv, p, sib).wait_recv()

    return pl.pallas_call(
        body, name="share_reduced", out_shape=[jax.ShapeDtypeStruct(r.shape, r.dtype) for r in reds],
        in_specs=[ANY] * n, out_specs=[ANY] * n, input_output_aliases={p: p for p in range(n)},
        scratch_shapes=[pltpu.SemaphoreType.DMA((n,)), pltpu.SemaphoreType.DMA((n,))],
    )(*reds)


def _gather_small(pack):
    _, N = pack.shape

    def body(in_ref, out_ref, send, recv, loc):
        x, y, c = _coords()
        me = 4 * x + 2 * y + c
        own = pltpu.make_async_copy(in_ref, out_ref.at[me], loc)
        own.start()
        cps = []
        for k in range(1, 8):
            dev = (x ^ (k >> 2), y ^ ((k >> 1) & 1), c ^ (k & 1))
            cps.append(_remote(in_ref, out_ref.at[me], send, recv, k - 1, dev))
        for cp in cps:
            cp.start()
        for k in range(1, 8):
            src = 4 * (x ^ (k >> 2)) + 2 * (y ^ ((k >> 1) & 1)) + (c ^ (k & 1))
            _remote(in_ref, out_ref.at[src], send, recv, k - 1, (x, y, c)).wait_recv()
        for cp in cps:
            cp.wait_send()
        own.wait()

    return pl.pallas_call(
        body, name="gather_small", out_shape=jax.ShapeDtypeStruct((8, 1, N), pack.dtype),
        in_specs=[ANY], out_specs=ANY,
        scratch_shapes=[pltpu.SemaphoreType.DMA((7,)), pltpu.SemaphoreType.DMA((7,)), pltpu.SemaphoreType.DMA(())],
    )(pack)


def _sum_small(slots):
    K, _, N = slots.shape

    def body(s_ref, o_ref):
        acc = s_ref[0]
        for k in range(1, K):
            acc = acc + s_ref[k]
        o_ref[...] = acc

    return _call(body, name="sum_small", in_specs=[pl.BlockSpec(memory_space=pltpu.VMEM)],
                 out_specs=pl.BlockSpec(memory_space=pltpu.VMEM), out_shape=jax.ShapeDtypeStruct((1, N), F32))(slots)


def _adamw(w, g, m, v, name, with_grad=False):
    R, C = w.shape
    tr, tc = _tile(R, 256, 8), _tile(C, 2176)

    def body(w_ref, g_ref, m_ref, v_ref, d_ref, nm_ref, nv_ref, *g_out):
        gv = g_ref[...]
        for ref in g_out:
            ref[...] = gv
        nm = ADAM_B1 * m_ref[...] + (1.0 - ADAM_B1) * gv
        nv = ADAM_B2 * v_ref[...] + (1.0 - ADAM_B2) * gv * gv
        m_hat = nm / (1.0 - ADAM_B1 ** ADAM_STEP)
        v_hat = nv / (1.0 - ADAM_B2 ** ADAM_STEP)
        d_ref[...] = -ADAM_LR * (m_hat / (jnp.sqrt(v_hat) + ADAM_EPS) + ADAM_WD * w_ref[...])
        nm_ref[...] = nm
        nv_ref[...] = nv

    spec = pl.BlockSpec((tr, tc), lambda i, j: (i, j))
    shp = jax.ShapeDtypeStruct((R, C), F32)
    nout = 4 if with_grad else 3
    return _call(body, name=name, grid=(R // tr, C // tc), in_specs=[spec] * 4, out_specs=[spec] * nout,
                 out_shape=[shp] * nout)(w, g, m, v)


SMALL = ("norm_g", "mem_norm_g", "attn_q_norm", "attn_k_norm", "conv_w", "mem_q_norm", "mem_k_norm")
WEIGHTS = ("norm_g", "mem_norm_g", "w_in", "attn_q_norm", "attn_k_norm", "conv_w", "mem_w_kv", "mem_q_norm",
           "mem_k_norm", "w_br_attn", "w_br_conv", "w_br_mem", "w_out")


def kernel(x, mem, norm_g, mem_norm_g, w_in, attn_q_norm, attn_k_norm, conv_w, mem_w_kv, mem_q_norm, mem_k_norm, w_br_attn, w_br_conv, w_br_mem, w_out, loss_target, m_norm_g, m_mem_norm_g, m_w_in, m_attn_q_norm, m_attn_k_norm, m_conv_w, m_mem_w_kv, m_mem_q_norm, m_mem_k_norm, m_w_br_attn, m_w_br_conv, m_w_br_mem, m_w_out, v_norm_g, v_mem_norm_g, v_w_in, v_attn_q_norm, v_attn_k_norm, v_conv_w, v_mem_w_kv, v_mem_q_norm, v_mem_k_norm, v_w_br_attn, v_w_br_conv, v_w_br_mem, v_w_out):
    w = dict(norm_g=norm_g, mem_norm_g=mem_norm_g, w_in=w_in, attn_q_norm=attn_q_norm, attn_k_norm=attn_k_norm,
             conv_w=conv_w, mem_w_kv=mem_w_kv, mem_q_norm=mem_q_norm, mem_k_norm=mem_k_norm, w_br_attn=w_br_attn,
             w_br_conv=w_br_conv, w_br_mem=w_br_mem, w_out=w_out)
    m = dict(norm_g=m_norm_g, mem_norm_g=m_mem_norm_g, w_in=m_w_in, attn_q_norm=m_attn_q_norm,
             attn_k_norm=m_attn_k_norm, conv_w=m_conv_w, mem_w_kv=m_mem_w_kv, mem_q_norm=m_mem_q_norm,
             mem_k_norm=m_mem_k_norm, w_br_attn=m_w_br_attn, w_br_conv=m_w_br_conv, w_br_mem=m_w_br_mem, w_out=m_w_out)
    v = dict(norm_g=v_norm_g, mem_norm_g=v_mem_norm_g, w_in=v_w_in, attn_q_norm=v_attn_q_norm,
             attn_k_norm=v_attn_k_norm, conv_w=v_conv_w, mem_w_kv=v_mem_w_kv, mem_q_norm=v_mem_q_norm,
             mem_k_norm=v_mem_k_norm, w_br_attn=v_w_br_attn, w_br_conv=v_w_br_conv, w_br_mem=v_w_br_mem, w_out=v_w_out)
    Bl, _, D = x.shape
    cx, cy = lax.axis_index("x"), lax.axis_index("y")
    chip = 2 * cx + cy
    pos = jnp.stack([chip, lax.axis_index("c")]).astype(jnp.int32)
    order = jnp.stack([chip] + [2 * a + b for a, b in _other_chips(cx, cy)]).astype(jnp.int32)
    n = len(BIG)

    w_rel = _place_shard(w["w_in"], "col", jnp.zeros((1,), jnp.int32), WIRE_DTYPE, "place_w_in")
    conv_full = _place_shard(conv_w, "col", pos, F32, "place_conv_w")
    others = [_place_shard(w[name], kind, pos, WIRE_DTYPE, "place_" + name) for name, kind in BIG[1:]]
    hb, hbt, mhb = _norms(x, mem, norm_g, mem_norm_g)

    meta = jnp.concatenate([order, pos[1:]])
    near, near_fwd = _w_in_copies((0, 1)), _w_in_forward((0, 1))
    send, recv, (w_rel, conv_full) = _start_copies("gather_near_start", [w_rel, conv_full], 4, near)
    proj = _proj_chunk(hb, w_rel, meta, 0, 1, None, None, "proj_own")
    w_rel, conv_full, *others = _wait_copies("gather_near_wait", send, recv, [w_rel, conv_full, *others], near, proj)

    fsend, frecv, (w_rel,) = _start_copies("gather_near_forward_start", [w_rel], 2, near_fwd)
    far, far_fwd = _w_in_copies((2,)), _w_in_forward((2,))
    send, recv, (w_rel, conv_full) = _start_copies("gather_far_start", [w_rel, conv_full], 2, far)
    proj = _proj_chunk(hb, w_rel, meta, 1, 2, 0, proj, "proj_near_landed")
    w_rel, = _wait_copies("gather_near_forward_wait", fsend, frecv, [w_rel], near_fwd, proj)
    proj = _proj_chunk(hb, w_rel, meta, 1, 2, 1, proj, "proj_near_forwarded")
    w_rel, conv_full = _wait_copies("gather_far_wait", send, recv, [w_rel, conv_full], far, proj)

    fsend, frecv, (w_rel,) = _start_copies("gather_far_forward_start", [w_rel], 1, far_fwd)
    send, recv, (*others, w_rel) = _start_copies("gather_rest_start", [*others, w_rel], 3 * (n - 1), _other_weight_copies)
    proj = _proj_chunk(hb, w_rel, meta, 3, 1, 0, proj, "proj_far_landed")
    w_rel, = _wait_copies("gather_far_forward_wait", fsend, frecv, [w_rel], far_fwd, proj)
    proj = _proj_chunk(hb, w_rel, meta, 3, 1, 1, proj, "proj_far_forwarded")
    os, ls, a = _attention_fwd(proj, Bl, attn_q_norm, attn_k_norm)
    *others, w_rel = _wait_copies("gather_rest_wait", send, recv, [*others, w_rel], _other_weight_copies, a[0])
    fsend, frecv, (*others, proj) = _start_copies("gather_rest_forward_start", [*others, proj], 3 * (n - 1),
                                                  _other_weight_forward)
    cc = _conv_branch_fwd(proj, Bl, conv_full)
    others = _wait_copies("gather_rest_forward_wait", fsend, frecv, others, _other_weight_forward, cc[0])
    W = {name: others[p] for p, (name, _) in enumerate(BIG[1:])}

    G, rest, rest_state = _weight_grads(
        x, mem, loss_target, norm_g, mem_norm_g, attn_q_norm, attn_k_norm, conv_full, mem_q_norm, mem_k_norm, W,
        (hb, hbt, mhb, proj, os, ls, a, cc), early=lambda G, carry: _exchange_start(G, BIG[1:], pos, carry, "rest"))

    for_sibling = _dw_in_half(hbt, rest[0], pos, False, "dw_in_sibling")
    send, recv, (for_sibling, got, dproj) = _start_copies(
        "sibling_w_in_start", [for_sibling, lax.empty(for_sibling.shape, for_sibling.dtype), rest[0]], 1, _sibling_copy)
    mine = _dw_in_half(hbt, dproj, pos, True, "dw_in_own")
    for_sibling, got = _wait_copies("sibling_w_in_wait", send, recv, [for_sibling, got], _sibling_copy, mine)
    pre_w_in = _presum(mine, got, "col", pos, "presum_w_in")

    w_in_state, dproj = _exchange_start(G, BIG[:1], pos, dproj, "w_in", pres=[pre_w_in])
    grad_x, small = _input_grad((dproj, *rest[1:]), w_rel, order)
    pres_rest, slots_rest = _exchange_wait(rest_state, grad_x)
    reds_rest = [_reduce_into_shard(slots_rest[p], pres_rest[p], kind, pos, "reduce_" + name)
                 for p, (name, kind) in enumerate(BIG[1:])]
    share_rest = _share_copies(BIG[1:])
    rsend, rrecv, reds_rest = _start_copies("share_rest_start", reds_rest, n - 1, share_rest)
    pres, slots = _exchange_wait(w_in_state, grad_x)
    red_w_in = _reduce_into_shard(slots[0], pres[0], "col", pos, "reduce_w_in")
    share_w_in = _share_copies(BIG[:1])
    wsend, wrecv, (red_w_in, small) = _start_copies("share_w_in_start", [red_w_in, small], 1, share_w_in)
    grad_x = grad_x.reshape(x.shape)

    tot = _sum_small(_gather_small(small))
    reds_rest = _wait_copies("share_rest_wait", rsend, rrecv, reds_rest, share_rest, tot)
    grads = dict(zip([name for name, _ in BIG[1:]], reds_rest))
    tot = tot[0]
    loss = tot[0]
    off = 128
    for name, size in (("norm_g", D), ("mem_norm_g", D), ("attn_q_norm", NGROUP * HEAD), ("attn_k_norm", NGROUP * HEAD),
                       ("conv_w", 3 * CONVW), ("mem_q_norm", MEM_HD), ("mem_k_norm", MEM_HD)):
        grads[name] = tot[off:off + size]
        off += size
    cw = conv_w.shape[1]
    grads["conv_w"] = lax.dynamic_slice(grads["conv_w"].reshape(3, CONVW), (0, chip * cw), (3, cw))
    for name in SMALL:
        grads[name] = grads[name].reshape(w[name].shape)

    delta, new_m, new_v = {}, {}, {}
    for name, _ in BIG[1:]:
        delta[name], new_m[name], new_v[name], grads[name] = _adamw(w[name], grads[name], m[name], v[name],
                                                                    "adamw_" + name, with_grad=True)

    def packed(t):
        return jnp.concatenate([t[name].reshape(1, -1) for name in SMALL], axis=1)

    ds, ms, vs = _adamw(packed(w), packed(grads), packed(m), packed(v), "adamw_small")
    shared, = _wait_copies("share_w_in_wait", wsend, wrecv, [red_w_in], share_w_in, ds)
    delta["w_in"], new_m["w_in"], new_v["w_in"], grads["w_in"] = _adamw(w["w_in"], shared, m["w_in"], v["w_in"],
                                                                        "adamw_w_in", with_grad=True)
    off = 0
    for name in SMALL:
        size = w[name].size
        delta[name] = ds[0, off:off + size].reshape(w[name].shape)
        new_m[name] = ms[0, off:off + size].reshape(w[name].shape)
        new_v[name] = vs[0, off:off + size].reshape(w[name].shape)
        off += size

    return (loss, grad_x, *[grads[n] for n in WEIGHTS], *[delta[n] for n in WEIGHTS],
            *[new_m[n] for n in WEIGHTS], *[new_v[n] for n in WEIGHTS])
```

```python
import functools

import jax
import jax.numpy as jnp
from jax import lax
from jax.experimental import pallas as pl
from jax.experimental.pallas import tpu as pltpu

F32 = jnp.float32
MXU_DTYPE = jnp.bfloat16
WIRE_DTYPE = jnp.bfloat16
PROJ_DTYPE = jnp.bfloat16
EPS = 1e-6
NEG = -1e30

HEAD = 128
HPG = 4
GW = HPG * HEAD
DILATIONS = (1, 4, 16)
NGROUP = len(DILATIONS)
BLK = 128
QKV = NGROUP * GW
CONVW = 1024
MEM_HEADS = 4
MEM_HD = 256
MEMW = MEM_HEADS * MEM_HD
Q0, K0, V0 = 0, QKV, 2 * QKV
ZA = 3 * QKV
CB, CC, CV, ZC = ZA + GW, ZA + GW + CONVW, ZA + GW + 2 * CONVW, ZA + GW + 3 * CONVW
MQ = ZC + CONVW
ZM = MQ + MEMW
G0 = ZM + MEMW

ADAM_LR, ADAM_B1, ADAM_B2, ADAM_EPS, ADAM_WD, ADAM_STEP = 0.001, 0.9, 0.999, 1e-08, 0.01, 10

VMEM_LIMIT = 56 * 1024 * 1024
MESH = pl.DeviceIdType.MESH
ANY = pl.BlockSpec(memory_space=pl.ANY)


def _tile(n, pref, mult=128):
    t = min(pref, n)
    while t > mult and (n % t or t % mult):
        t -= mult
    assert n % t == 0, (n, pref)
    return t


def _call(body, *, name, out_shape, grid=(), in_specs=None, out_specs=None, scratch_shapes=(),
          aliases=None, grid_spec=None):
    kw = {}
    if grid_spec is not None:
        kw["grid_spec"] = grid_spec
        ngrid = len(grid_spec.grid)
    else:
        kw.update(grid=grid, in_specs=in_specs, out_specs=out_specs, scratch_shapes=list(scratch_shapes))
        ngrid = len(grid)
    params = pltpu.CompilerParams(dimension_semantics=("arbitrary",) * ngrid, vmem_limit_bytes=VMEM_LIMIT)
    return pl.pallas_call(body, name=name, out_shape=out_shape, compiler_params=params,
                          input_output_aliases=aliases or {}, **kw)


_DIMS = {"nn": (((1,), (0,)), ((), ())), "nt": (((1,), (1,)), ((), ())), "tn": (((0,), (0,)), ((), ()))}


def _mxu(a, b, mode):
    return lax.dot_general(a.astype(MXU_DTYPE), b.astype(MXU_DTYPE), _DIMS[mode], preferred_element_type=F32)


@functools.partial(jax.custom_vjp, nondiff_argnums=(2,))
def _dot(a, b, mode):
    return _mxu(a, b, mode)


def _dot_fwd(a, b, mode):
    return _mxu(a, b, mode), (a, b)


def _dot_bwd(mode, res, g):
    a, b = res
    if mode == "nn":
        return _mxu(g, b, "nt"), _mxu(a, g, "tn")
    if mode == "nt":
        return _mxu(g, b, "nn"), _mxu(g, a, "tn")
    return _mxu(b, g, "nt"), _mxu(a, g, "nn")


_dot.defvjp(_dot_fwd, _dot_bwd)


def _sig(z):
    return 1.0 / (1.0 + jnp.exp(-z))


def _silu(z):
    return z * _sig(z)


def _rms_rows(t, g):
    return t * lax.rsqrt(jnp.mean(t * t, axis=-1, keepdims=True) + EPS) * g


def _attn_block(q, k2, v2, gq, gk, first):
    qn = _rms_rows(q, gq)
    kn = _rms_rows(k2, gk)
    s = jnp.where(_band_mask(first, k2.shape[0]), _dot(qn, kn, "nt") * (HEAD ** -0.5), NEG)
    m = lax.stop_gradient(jnp.max(s, axis=-1, keepdims=True))
    p = jnp.exp(s - m)
    den = jnp.sum(p, axis=-1, keepdims=True)
    o = _dot(p, v2, "nn") / den
    return o, m + jnp.log(den)


def _band_mask(first, nkeys):
    a = lax.broadcasted_iota(jnp.int32, (BLK, nkeys), 0)
    b = lax.broadcasted_iota(jnp.int32, (BLK, nkeys), 1)
    if nkeys == BLK:
        return b <= a
    return (b >= a) & (b <= a + BLK) & (b >= jnp.where(first, BLK, 0))


def _norm_parts(t):
    r = lax.rsqrt(jnp.mean(t * t, axis=-1, keepdims=True) + EPS)
    return r, t * r


def _norm_bwd(dn, g, r, th):
    dth = dn * g
    return r * (dth - th * jnp.mean(dth * th, axis=-1, keepdims=True)), jnp.sum(dn * th, axis=0, keepdims=True)


def _attn_block_bwd(q, k2, v2, gq, gk, first, do, o, lse, dlse):
    scale = HEAD ** -0.5
    rq, qh = _norm_parts(q)
    rk, kh = _norm_parts(k2)
    qn, kn = qh * gq, kh * gk
    s = jnp.where(_band_mask(first, k2.shape[0]), _mxu(qn, kn, "nt") * scale, NEG)
    p = jnp.exp(s - lse)
    ds = p * (_mxu(do, v2, "nt") + (dlse - jnp.sum(do * o, axis=-1, keepdims=True))) * scale
    dq, dgq = _norm_bwd(_mxu(ds, kn, "nn"), gq, rq, qh)
    dk2, dgk = _norm_bwd(_mxu(ds, qn, "tn"), gk, rk, kh)
    return dq, dk2, _mxu(p, do, "tn"), dgq, dgk


def _combine(o1, o2, o3, l1, l2, l3, z):
    m = lax.stop_gradient(jnp.maximum(jnp.maximum(l1, l2), l3))
    e1, e2, e3 = jnp.exp(l1 - m), jnp.exp(l2 - m), jnp.exp(l3 - m)
    return (e1 * o1 + e2 * o2 + e3 * o3) / (e1 + e2 + e3) * _silu(z)


def _mem_block(q, z, kv, gq, gk):
    outs = []
    for h in range(MEM_HEADS):
        sl = slice(h * MEM_HD, (h + 1) * MEM_HD)
        qn = _rms_rows(q[:, sl], gq)
        kn = _rms_rows(kv[:, sl], gk)
        s = _dot(qn, kn, "nt") * (MEM_HD ** -0.5)
        m = lax.stop_gradient(jnp.max(s, axis=-1, keepdims=True))
        p = jnp.exp(s - m)
        den = jnp.sum(p, axis=-1, keepdims=True)
        outs.append(_dot(p, kv[:, MEMW + h * MEM_HD:MEMW + (h + 1) * MEM_HD], "nn") / den)
    return jnp.concatenate(outs, axis=-1) * _silu(z)


def _cast(w, name):
    R, C = w.shape
    tr, tc = _tile(R, 512, 8), _tile(C, 2176)

    def body(w_ref, o_ref):
        o_ref[...] = w_ref[...].astype(o_ref.dtype)

    spec = pl.BlockSpec((tr, tc), lambda i, j: (i, j))
    return _call(body, name=name, grid=(R // tr, C // tc), in_specs=[spec], out_specs=spec,
                 out_shape=jax.ShapeDtypeStruct((R, C), WIRE_DTYPE))(w)


def _place_shard(w, kind, pos, dtype, name, slot=0, into=None):
    R, C = w.shape
    tr, tc = _tile(R, 512, 8), _tile(C, 2176)
    nr, nc = R // tr, C // tc

    def body(pos_ref, w_ref, *rest):
        rest[-1][...] = w_ref[...].astype(rest[-1].dtype)

    if kind == "col":
        full, out = (R, 4 * C), pl.BlockSpec((tr, tc), lambda i, j, pos_ref: (i, pos_ref[slot] * nc + j))
    else:
        full, out = (4 * R, C), pl.BlockSpec((tr, tc), lambda i, j, pos_ref: (pos_ref[slot] * nr + i, j))
    in_specs, args = [pl.BlockSpec((tr, tc), lambda i, j, pos_ref: (i, j))], [pos, w]
    if into is not None:
        in_specs.append(ANY)
        args.append(into)
    spec = pltpu.PrefetchScalarGridSpec(num_scalar_prefetch=1, grid=(nr, nc), in_specs=in_specs, out_specs=out)
    return _call(body, name=name, grid_spec=spec, out_shape=jax.ShapeDtypeStruct(full, dtype),
                 aliases={} if into is None else {2: 0})(*args)


def _matmul(a, b, mode, out_dtype, *, name, tm=512, tn=512, tk=512):
    if mode == "nn":
        (M, K), (_, N) = a.shape, b.shape
    elif mode == "nt":
        (M, K), (N, _) = a.shape, b.shape
    else:
        (K, M), (_, N) = a.shape, b.shape
    tm, tn, tk = _tile(M, tm), _tile(N, tn), _tile(K, tk)
    nk = K // tk

    def body(a_ref, b_ref, o_ref, *acc):
        part = lax.dot_general(a_ref[...], b_ref[...], _DIMS[mode], preferred_element_type=F32)
        if nk == 1:
            o_ref[...] = part.astype(o_ref.dtype)
            return
        acc_ref, = acc
        k = pl.program_id(2)

        @pl.when(k == 0)
        def _():
            acc_ref[...] = part

        @pl.when(k > 0)
        def _():
            acc_ref[...] += part

        @pl.when(k == nk - 1)
        def _():
            o_ref[...] = acc_ref[...].astype(o_ref.dtype)

    a_spec = pl.BlockSpec((tk, tm), lambda i, j, k: (k, i)) if mode == "tn" else pl.BlockSpec((tm, tk), lambda i, j, k: (i, k))
    b_spec = pl.BlockSpec((tn, tk), lambda i, j, k: (j, k)) if mode == "nt" else pl.BlockSpec((tk, tn), lambda i, j, k: (k, j))
    return _call(body, name=name, grid=(M // tm, N // tn, nk), in_specs=[a_spec, b_spec],
                 out_specs=pl.BlockSpec((tm, tn), lambda i, j, k: (i, j)),
                 out_shape=jax.ShapeDtypeStruct((M, N), out_dtype),
                 scratch_shapes=[] if nk == 1 else [pltpu.VMEM((tm, tn), F32)])(a, b)


def _rms_fwd(x, g, name):
    R, D = x.shape
    tr = _tile(R, 512)

    def body(x_ref, g_ref, o_ref, t_ref):
        y = _rms_rows(x_ref[...], g_ref[...])
        o_ref[...] = y.astype(o_ref.dtype)
        t_ref[...] = y.T.astype(t_ref.dtype)

    row = pl.BlockSpec((tr, D), lambda i: (i, 0))
    return _call(body, name=name, grid=(R // tr,), in_specs=[row, pl.BlockSpec((1, D), lambda i: (0, 0))],
                 out_specs=[row, pl.BlockSpec((D, tr), lambda i: (0, i))],
                 out_shape=[jax.ShapeDtypeStruct((R, D), MXU_DTYPE), jax.ShapeDtypeStruct((D, R), MXU_DTYPE)])(x, g)


def _rms_bwd(x, dh, g, dy, name):
    R, D = x.shape
    tr = _tile(R, 256)
    with_dx = dy is not None

    def body(*refs):
        if with_dx:
            x_ref, dh_ref, g_ref, dy_ref, dx_ref, dg_ref = refs
        else:
            x_ref, dh_ref, g_ref, dg_ref = refs
        xv, dhv = x_ref[...], dh_ref[...]
        r = lax.rsqrt(jnp.mean(xv * xv, axis=-1, keepdims=True) + EPS)
        xh = xv * r

        @pl.when(pl.program_id(0) == 0)
        def _():
            dg_ref[...] = jnp.zeros_like(dg_ref)

        dg_ref[...] += jnp.sum(dhv * xh, axis=0, keepdims=True)
        if with_dx:
            dxh = dhv * g_ref[...]
            dx_ref[...] = dy_ref[...].astype(F32) + r * (dxh - xh * jnp.mean(dxh * xh, axis=-1, keepdims=True))

    row = pl.BlockSpec((tr, D), lambda i: (i, 0))
    vec = pl.BlockSpec((1, D), lambda i: (0, 0))
    dg_shape = jax.ShapeDtypeStruct((1, D), F32)
    if with_dx:
        return _call(body, name=name, grid=(R // tr,), in_specs=[row, row, vec, row], out_specs=[row, vec],
                     out_shape=[jax.ShapeDtypeStruct((R, D), F32), dg_shape])(x, dh, g, dy)
    return None, _call(body, name=name, grid=(R // tr,), in_specs=[row, row, vec], out_specs=vec,
                       out_shape=dg_shape)(x, dh, g)


def _attn_geom(g, d):
    hc = HPG if d == 1 else 1
    cw = hc * HEAD
    cq, ck, cv = (Q0 + g * GW) // cw, (K0 + g * GW) // cw, (V0 + g * GW) // cw
    return (1, BLK * d, cw), hc, HPG // hc, cq, ck, cv


def _rows(ref, r, d, sl):
    if d == 1:
        return ref[0, :, sl]
    return ref.at[0][pl.ds(r, BLK, stride=d), sl]


def _set_rows(ref, r, d, sl, val):
    if d == 1:
        ref[0, :, sl] = val
    else:
        ref.at[0][pl.ds(r, BLK, stride=d), sl] = val


def _stage_rows(ref, r, d, sl, val):
    if d == 1:
        ref[:, sl] = val
    else:
        ref[pl.ds(r, BLK, stride=d), sl] = val


def _proj_stages(blk, d):
    return [] if d == 1 else [pltpu.VMEM(blk[1:], F32)] * 5


def _proj_rows(refs, stages, d):
    if d == 1:
        return [lambda r, sl, ref=ref: ref[0, :, sl].astype(F32) for ref in refs]
    for ref, stage in zip(refs, stages):
        stage[...] = ref[0].astype(F32)
    return [lambda r, sl, stage=stage: stage[pl.ds(r, BLK, stride=d), sl] for stage in stages]


def _attn_fwd(proj3, gq, gk, g, d):
    Bl, S, _ = proj3.shape
    blk, hc, ncb, cq, ck, cv = _attn_geom(g, d)
    nb = S // blk[1]
    if nb == 1:
        return _attn_single_fwd(proj3, gq, gk, g, d)

    def body(q_ref, kp_ref, kc_ref, vp_ref, vc_ref, gq_ref, gk_ref, o_ref, lse_ref, *stages):
        first = pl.program_id(2) == 0
        q, kp, kc, vp, vc = _proj_rows((q_ref, kp_ref, kc_ref, vp_ref, vc_ref), stages, d)
        for r in range(d):
            for h in range(hc):
                sl = slice(h * HEAD, (h + 1) * HEAD)
                k2 = jnp.concatenate([kp(r, sl), kc(r, sl)], axis=0)
                v2 = jnp.concatenate([vp(r, sl), vc(r, sl)], axis=0)
                o, lse = _attn_block(q(r, sl), k2, v2, gq_ref[...], gk_ref[...], first)
                _set_rows(o_ref, r, d, sl, o)
                _set_rows(lse_ref, r, d, sl, jnp.broadcast_to(lse, (BLK, HEAD)))

    def cur(c0):
        return pl.BlockSpec(blk, lambda b, j, i: (b, i, c0 + j))

    def prev(c0):
        return pl.BlockSpec(blk, lambda b, j, i: (b, jnp.maximum(i - 1, 0), c0 + j))

    vec = pl.BlockSpec((1, HEAD), lambda b, j, i: (0, 0))
    out = pl.BlockSpec(blk, lambda b, j, i: (b, i, j))
    shp = jax.ShapeDtypeStruct((Bl, S, GW), F32)
    return _call(body, name=f"attn_fwd_g{g}", grid=(Bl, ncb, nb),
                 in_specs=[cur(cq), prev(ck), cur(ck), prev(cv), cur(cv), vec, vec],
                 out_specs=[out, out], out_shape=[shp, shp], scratch_shapes=_proj_stages(blk, d),
                 )(proj3, proj3, proj3, proj3, proj3, gq, gk)


def _attn_single_fwd(proj3, gq, gk, g, d):
    Bl, S, _ = proj3.shape
    blk, hc, ncb, cq, ck, cv = _attn_geom(g, d)

    def body(q_ref, k_ref, v_ref, gq_ref, gk_ref, o_ref, lse_ref, *stages):
        q, k, v = _proj_rows((q_ref, k_ref, v_ref), stages, d)
        for r in range(d):
            for h in range(hc):
                sl = slice(h * HEAD, (h + 1) * HEAD)
                o, lse = _attn_block(q(r, sl), k(r, sl), v(r, sl), gq_ref[...], gk_ref[...], True)
                _set_rows(o_ref, r, d, sl, o)
                _set_rows(lse_ref, r, d, sl, jnp.broadcast_to(lse, (BLK, HEAD)))

    def at(c0):
        return pl.BlockSpec(blk, lambda b, j: (b, 0, c0 + j))

    vec = pl.BlockSpec((1, HEAD), lambda b, j: (0, 0))
    shp = jax.ShapeDtypeStruct((Bl, S, GW), F32)
    return _call(body, name=f"attn_fwd_g{g}", grid=(Bl, ncb), in_specs=[at(cq), at(ck), at(cv), vec, vec],
                 out_specs=[at(0), at(0)], out_shape=[shp, shp], scratch_shapes=_proj_stages(blk, d)[:3],
                 )(proj3, proj3, proj3, gq, gk)


def _attn_single_bwd(proj3, gq, gk, o3, l3, do3, dl3, g, d):
    Bl, S, _ = proj3.shape
    blk, hc, ncb, cq, ck, cv = _attn_geom(g, d)

    def body(q_ref, k_ref, v_ref, gq_ref, gk_ref, o_ref, l_ref, do_ref, dl_ref,
             dq_ref, dk_ref, dv_ref, dgq_ref, dgk_ref, sq_ref, sk_ref, sv_ref, *stages):
        @pl.when((pl.program_id(0) == 0) & (pl.program_id(1) == 0))
        def _():
            dgq_ref[...] = jnp.zeros_like(dgq_ref)
            dgk_ref[...] = jnp.zeros_like(dgk_ref)

        dgq, dgk = jnp.zeros((1, HEAD), F32), jnp.zeros((1, HEAD), F32)
        q, k, v = _proj_rows((q_ref, k_ref, v_ref), stages, d)
        for r in range(d):
            for h in range(hc):
                sl = slice(h * HEAD, (h + 1) * HEAD)
                dq, dk, dv, a, b = _attn_block_bwd(
                    q(r, sl), k(r, sl), v(r, sl), gq_ref[...], gk_ref[...], True, _rows(do_ref, r, d, sl),
                    _rows(o_ref, r, d, sl), _rows(l_ref, r, d, sl)[:, :1], _rows(dl_ref, r, d, sl)[:, :1])
                _stage_rows(sq_ref, r, d, sl, dq)
                _stage_rows(sk_ref, r, d, sl, dk)
                _stage_rows(sv_ref, r, d, sl, dv)
                dgq, dgk = dgq + a, dgk + b
        dgq_ref[...] += dgq
        dgk_ref[...] += dgk
        dq_ref[0] = sq_ref[...].astype(dq_ref.dtype)
        dk_ref[0] = sk_ref[...].astype(dk_ref.dtype)
        dv_ref[0] = sv_ref[...].astype(dv_ref.dtype)

    def at(c0):
        return pl.BlockSpec(blk, lambda b, j: (b, 0, c0 + j))

    vec = pl.BlockSpec((1, HEAD), lambda b, j: (0, 0))
    shp = jax.ShapeDtypeStruct((Bl, S, GW), MXU_DTYPE)
    gshp = jax.ShapeDtypeStruct((1, HEAD), F32)
    return _call(body, name=f"attn_bwd_g{g}", grid=(Bl, ncb),
                 in_specs=[at(cq), at(ck), at(cv), vec, vec, at(0), at(0), at(0), at(0)],
                 out_specs=[at(0), at(0), at(0), vec, vec], out_shape=[shp, shp, shp, gshp, gshp],
                 scratch_shapes=[pltpu.VMEM(blk[1:], F32)] * 3 + _proj_stages(blk, d)[:3],
                 )(proj3, proj3, proj3, gq, gk, o3, l3, do3, dl3)


def _attn_bwd(proj3, gq, gk, o3, l3, do3, dl3, g, d):
    Bl, S, _ = proj3.shape
    blk, hc, ncb, cq, ck, cv = _attn_geom(g, d)
    nb = S // blk[1]
    if nb == 1:
        return _attn_single_bwd(proj3, gq, gk, o3, l3, do3, dl3, g, d)

    def body(q_ref, kp_ref, kc_ref, vp_ref, vc_ref, gq_ref, gk_ref, o_ref, l_ref, do_ref, dl_ref,
             dq_ref, dk_ref, dv_ref, dgq_ref, dgk_ref, ck_ref, cv_ref, sq_ref, sk_ref, sv_ref, *stages):
        i = pl.program_id(2)
        first = i == 0

        @pl.when((pl.program_id(0) == 0) & (pl.program_id(1) == 0) & first)
        def _():
            dgq_ref[...] = jnp.zeros_like(dgq_ref)
            dgk_ref[...] = jnp.zeros_like(dgk_ref)

        @pl.when(first)
        def _():
            ck_ref[...] = jnp.zeros_like(ck_ref)
            cv_ref[...] = jnp.zeros_like(cv_ref)

        @pl.when(i < nb)
        def _():
            dgq, dgk = jnp.zeros((1, HEAD), F32), jnp.zeros((1, HEAD), F32)
            q, kp, kc, vp, vc = _proj_rows((q_ref, kp_ref, kc_ref, vp_ref, vc_ref), stages, d)
            for r in range(d):
                rs = slice(r * BLK, (r + 1) * BLK)
                for h in range(hc):
                    sl = slice(h * HEAD, (h + 1) * HEAD)
                    k2 = jnp.concatenate([kp(r, sl), kc(r, sl)], axis=0)
                    v2 = jnp.concatenate([vp(r, sl), vc(r, sl)], axis=0)
                    dq, dk2, dv2, a, b = _attn_block_bwd(
                        q(r, sl), k2, v2, gq_ref[...], gk_ref[...], first, _rows(do_ref, r, d, sl),
                        _rows(o_ref, r, d, sl), _rows(l_ref, r, d, sl)[:, :1], _rows(dl_ref, r, d, sl)[:, :1])
                    _stage_rows(sq_ref, r, d, sl, dq)
                    _stage_rows(sk_ref, r, d, sl, ck_ref[rs, sl] + dk2[:BLK])
                    _stage_rows(sv_ref, r, d, sl, cv_ref[rs, sl] + dv2[:BLK])
                    ck_ref[rs, sl] = dk2[BLK:]
                    cv_ref[rs, sl] = dv2[BLK:]
                    dgq, dgk = dgq + a, dgk + b
            dgq_ref[...] += dgq
            dgk_ref[...] += dgk
            dq_ref[0] = sq_ref[...].astype(dq_ref.dtype)

        @pl.when(i == nb)
        def _():
            for r in range(d):
                rs = slice(r * BLK, (r + 1) * BLK)
                _stage_rows(sk_ref, r, d, slice(None), ck_ref[rs, :])
                _stage_rows(sv_ref, r, d, slice(None), cv_ref[rs, :])

        dk_ref[0] = sk_ref[...].astype(dk_ref.dtype)
        dv_ref[0] = sv_ref[...].astype(dv_ref.dtype)

    def cur(c0):
        return pl.BlockSpec(blk, lambda b, j, i: (b, jnp.minimum(i, nb - 1), c0 + j))

    def prev(c0):
        return pl.BlockSpec(blk, lambda b, j, i: (b, jnp.clip(i - 1, 0, nb - 1), c0 + j))

    vec = pl.BlockSpec((1, HEAD), lambda b, j, i: (0, 0))
    at_q = pl.BlockSpec(blk, lambda b, j, i: (b, jnp.minimum(i, nb - 1), j))
    at_k = pl.BlockSpec(blk, lambda b, j, i: (b, jnp.maximum(i - 1, 0), j))
    shp = jax.ShapeDtypeStruct((Bl, S, GW), MXU_DTYPE)
    gshp = jax.ShapeDtypeStruct((1, HEAD), F32)
    return _call(body, name=f"attn_bwd_g{g}", grid=(Bl, ncb, nb + 1),
                 in_specs=[cur(cq), prev(ck), cur(ck), prev(cv), cur(cv), vec, vec, at_q, at_q, at_q, at_q],
                 out_specs=[at_q, at_k, at_k, vec, vec], out_shape=[shp, shp, shp, gshp, gshp],
                 scratch_shapes=[pltpu.VMEM(blk[1:], F32)] * 5 + _proj_stages(blk, d),
                 )(proj3, proj3, proj3, proj3, proj3, gq, gk, o3, l3, do3, dl3)


def _combine_fwd(os, ls, proj2):
    T = proj2.shape[0]
    tr = _tile(T, 512)

    def body(o1, o2, o3, l1, l2, l3, z, a_ref, at_ref):
        a = _combine(o1[...], o2[...], o3[...], l1[...], l2[...], l3[...], z[...].astype(F32))
        a_ref[...] = a.astype(a_ref.dtype)
        at_ref[...] = a.T.astype(at_ref.dtype)

    row = pl.BlockSpec((tr, GW), lambda i: (i, 0))
    return _call(body, name="combine_fwd", grid=(T // tr,),
                 in_specs=[row] * 6 + [pl.BlockSpec((tr, GW), lambda i: (i, ZA // GW))],
                 out_specs=[row, pl.BlockSpec((GW, tr), lambda i: (0, i))],
                 out_shape=[jax.ShapeDtypeStruct((T, GW), MXU_DTYPE), jax.ShapeDtypeStruct((GW, T), MXU_DTYPE)],
                 )(*os, *ls, proj2)


def _combine_bwd(os, ls, proj2, da):
    T = proj2.shape[0]
    tr = _tile(T, 256)

    def body(o1, o2, o3, l1, l2, l3, z, da_ref, d1, d2, d3, e1, e2, e3, dz_ref):
        _, vjp = jax.vjp(_combine, o1[...], o2[...], o3[...], l1[...], l2[...], l3[...], z[...].astype(F32))
        go1, go2, go3, gl1, gl2, gl3, gz = vjp(da_ref[...])
        d1[...], d2[...], d3[...] = go1, go2, go3
        dz_ref[...] = gz.astype(dz_ref.dtype)
        for ref, gl in ((e1, gl1), (e2, gl2), (e3, gl3)):
            for h in range(HPG):
                sl = slice(h * HEAD, (h + 1) * HEAD)
                ref[:, sl] = jnp.broadcast_to(jnp.sum(gl[:, sl], axis=-1, keepdims=True), (tr, HEAD))

    row = pl.BlockSpec((tr, GW), lambda i: (i, 0))
    f = jax.ShapeDtypeStruct((T, GW), F32)
    outs = _call(body, name="combine_bwd", grid=(T // tr,),
                 in_specs=[row] * 6 + [pl.BlockSpec((tr, GW), lambda i: (i, ZA // GW)), row],
                 out_specs=[row] * 7, out_shape=[f] * 6 + [jax.ShapeDtypeStruct((T, GW), MXU_DTYPE)],
                 )(*os, *ls, proj2, da)
    return outs[:3], outs[3:6], outs[6]


def _shift_down(u, j, t):
    return jnp.where(t >= j, pltpu.roll(u, j, 0), 0.0)


def _shift_up(u, j, t):
    n = u.shape[0]
    return jnp.where(t < n - j, pltpu.roll(u, n - j, 0), 0.0)


def _conv_specs(Bl, S, cw):
    def sec(c0):
        return pl.BlockSpec((1, S, cw), lambda j, b: (b, 0, c0 // cw + j))
    return [sec(CB), sec(CC), sec(CV), sec(ZC)], pl.BlockSpec((3, cw), lambda j, b: (0, j))


def _conv_fwd(proj3, conv_w):
    Bl, S, _ = proj3.shape
    cw = 256
    secs, wspec = _conv_specs(Bl, S, cw)

    def body(b_ref, c_ref, v_ref, z_ref, w_ref, o_ref, ot_ref):
        t = lax.broadcasted_iota(jnp.int32, (S, cw), 0)
        u = c_ref[0].astype(F32) * v_ref[0].astype(F32)
        y = w_ref[0:1, :] * u + w_ref[1:2, :] * _shift_down(u, 1, t) + w_ref[2:3, :] * _shift_down(u, 2, t)
        out = b_ref[0].astype(F32) * y * _silu(z_ref[0].astype(F32))
        o_ref[0] = out.astype(o_ref.dtype)
        ot_ref[...] = out.T.astype(ot_ref.dtype)

    return _call(body, name="conv_fwd", grid=(CONVW // cw, Bl), in_specs=secs + [wspec],
                 out_specs=[pl.BlockSpec((1, S, cw), lambda j, b: (b, 0, j)), pl.BlockSpec((cw, S), lambda j, b: (j, b))],
                 out_shape=[jax.ShapeDtypeStruct((Bl, S, CONVW), MXU_DTYPE),
                            jax.ShapeDtypeStruct((CONVW, Bl * S), MXU_DTYPE)])(proj3, proj3, proj3, proj3, conv_w)


def _conv_bwd(proj3, conv_w, dcc3):
    Bl, S, _ = proj3.shape
    cw = 256
    secs, wspec = _conv_specs(Bl, S, cw)

    def body(b_ref, c_ref, v_ref, z_ref, w_ref, d_ref, db_ref, dc_ref, dv_ref, dz_ref, dw_ref):
        t = lax.broadcasted_iota(jnp.int32, (S, cw), 0)
        bv, cv, vv, zv = (r[0].astype(F32) for r in (b_ref, c_ref, v_ref, z_ref))
        dv = d_ref[0]
        u = cv * vv
        u1, u2 = _shift_down(u, 1, t), _shift_down(u, 2, t)
        y = w_ref[0:1, :] * u + w_ref[1:2, :] * u1 + w_ref[2:3, :] * u2
        sg = _sig(zv)
        sz = zv * sg
        gy = dv * bv * sz
        db_ref[0] = (dv * y * sz).astype(db_ref.dtype)
        dz_ref[0] = (dv * bv * y * sg * (1.0 + zv * (1.0 - sg))).astype(dz_ref.dtype)
        du = w_ref[0:1, :] * gy + w_ref[1:2, :] * _shift_up(gy, 1, t) + w_ref[2:3, :] * _shift_up(gy, 2, t)
        dc_ref[0] = (du * vv).astype(dc_ref.dtype)
        dv_ref[0] = (du * cv).astype(dv_ref.dtype)

        @pl.when(pl.program_id(1) == 0)
        def _():
            dw_ref[...] = jnp.zeros_like(dw_ref)

        dw_ref[0:1, :] += jnp.sum(gy * u, axis=0, keepdims=True)
        dw_ref[1:2, :] += jnp.sum(gy * u1, axis=0, keepdims=True)
        dw_ref[2:3, :] += jnp.sum(gy * u2, axis=0, keepdims=True)

    blk = pl.BlockSpec((1, S, cw), lambda j, b: (b, 0, j))
    shp = jax.ShapeDtypeStruct((Bl, S, CONVW), MXU_DTYPE)
    return _call(body, name="conv_bwd", grid=(CONVW // cw, Bl), in_specs=secs + [wspec, blk],
                 out_specs=[blk] * 4 + [wspec], out_shape=[shp] * 4 + [jax.ShapeDtypeStruct((3, CONVW), F32)],
                 )(proj3, proj3, proj3, proj3, conv_w, dcc3)


def _mem_specs(S, tq):
    q = pl.BlockSpec((1, tq, MEMW), lambda b, j: (b, j, MQ // MEMW))
    z = pl.BlockSpec((1, tq, MEMW), lambda b, j: (b, j, ZM // MEMW))
    kv = pl.BlockSpec((1, MEM_HD, 2 * MEMW), lambda b, j: (b, 0, 0))
    vec = pl.BlockSpec((1, MEM_HD), lambda b, j: (0, 0))
    blk = pl.BlockSpec((1, tq, MEMW), lambda b, j: (b, j, 0))
    return q, z, kv, vec, blk


def _mem_fwd(proj3, mkv3, gq, gk):
    Bl, S, _ = proj3.shape
    tq = _tile(S, 512)
    q, z, kv, vec, blk = _mem_specs(S, tq)

    def body(q_ref, z_ref, kv_ref, gq_ref, gk_ref, o_ref, ot_ref):
        out = _mem_block(q_ref[0].astype(F32), z_ref[0].astype(F32), kv_ref[0], gq_ref[...], gk_ref[...])
        o_ref[0] = out.astype(o_ref.dtype)
        ot_ref[...] = out.T.astype(ot_ref.dtype)

    nq = S // tq
    return _call(body, name="mem_fwd", grid=(Bl, nq), in_specs=[q, z, kv, vec, vec],
                 out_specs=[blk, pl.BlockSpec((MEMW, tq), lambda b, j: (0, b * nq + j))],
                 out_shape=[jax.ShapeDtypeStruct((Bl, S, MEMW), MXU_DTYPE),
                            jax.ShapeDtypeStruct((MEMW, Bl * S), MXU_DTYPE)])(proj3, proj3, mkv3, gq, gk)


def _mem_bwd(proj3, mkv3, gq, gk, dmo3):
    Bl, S, _ = proj3.shape
    tq = _tile(S, 256)
    q, z, kv, vec, blk = _mem_specs(S, tq)

    def body(q_ref, z_ref, kv_ref, gq_ref, gk_ref, d_ref, dq_ref, dz_ref, dkv_ref, dgq_ref, dgk_ref):
        _, vjp = jax.vjp(_mem_block, q_ref[0].astype(F32), z_ref[0].astype(F32), kv_ref[0], gq_ref[...], gk_ref[...])
        dq, dz, dkv, dgq, dgk = vjp(d_ref[0])
        dq_ref[0] = dq.astype(dq_ref.dtype)
        dz_ref[0] = dz.astype(dz_ref.dtype)
        j = pl.program_id(1)

        @pl.when(j == 0)
        def _():
            dkv_ref[0] = jnp.zeros_like(dkv)

        @pl.when((j == 0) & (pl.program_id(0) == 0))
        def _():
            dgq_ref[...] = jnp.zeros_like(dgq_ref)
            dgk_ref[...] = jnp.zeros_like(dgk_ref)

        dkv_ref[0] += dkv
        dgq_ref[...] += dgq
        dgk_ref[...] += dgk

    shp = jax.ShapeDtypeStruct((Bl, S, MEMW), MXU_DTYPE)
    gshp = jax.ShapeDtypeStruct((1, MEM_HD), F32)
    return _call(body, name="mem_bwd", grid=(Bl, S // tq), in_specs=[q, z, kv, vec, vec, blk],
                 out_specs=[blk, blk, kv, vec, vec],
                 out_shape=[shp, shp, jax.ShapeDtypeStruct(mkv3.shape, F32), gshp, gshp],
                 )(proj3, proj3, mkv3, gq, gk, dmo3)


def _merge_specs(T, D, tm, tn):
    def act(w):
        return pl.BlockSpec((tm, w), lambda i, n: (i, 0))

    def wsp(w):
        return pl.BlockSpec((w, tn), lambda i, n: (0, n))

    gates = [pl.BlockSpec((tm, tn), lambda i, n, k=k: (i, (G0 + k * D) // tn + n)) for k in range(3)]
    tile = pl.BlockSpec((tm, tn), lambda i, n: (i, n))
    return act, wsp, gates, tile


def _merge_fwd(a, cc, mo, wa, wc, wm, proj2):
    T, D = a.shape[0], wa.shape[1]
    tm, tn = _tile(T, 1024), _tile(D, 512)
    act, wsp, gates, tile = _merge_specs(T, D, tm, tn)

    def body(a_ref, c_ref, m_ref, wa_ref, wc_ref, wm_ref, g0, g1, g2, mg_ref, mt_ref, pa_ref, pc_ref, pm_ref):
        pa = jnp.dot(a_ref[...], wa_ref[...], preferred_element_type=F32)
        pc = jnp.dot(c_ref[...], wc_ref[...], preferred_element_type=F32)
        pm = jnp.dot(m_ref[...], wm_ref[...], preferred_element_type=F32)
        mg = _sig(g0[...].astype(F32)) * pa + _sig(g1[...].astype(F32)) * pc + _sig(g2[...].astype(F32)) * pm
        mg_ref[...] = mg.astype(mg_ref.dtype)
        mt_ref[...] = mg.T.astype(mt_ref.dtype)
        pa_ref[...] = pa.astype(pa_ref.dtype)
        pc_ref[...] = pc.astype(pc_ref.dtype)
        pm_ref[...] = pm.astype(pm_ref.dtype)

    shp = jax.ShapeDtypeStruct((T, D), MXU_DTYPE)
    return _call(body, name="merge_fwd", grid=(T // tm, D // tn),
                 in_specs=[act(GW), act(CONVW), act(MEMW), wsp(GW), wsp(CONVW), wsp(MEMW)] + gates,
                 out_specs=[tile, pl.BlockSpec((tn, tm), lambda i, n: (n, i)), tile, tile, tile],
                 out_shape=[shp, jax.ShapeDtypeStruct((D, T), MXU_DTYPE), shp, shp, shp],
                 )(a, cc, mo, wa, wc, wm, proj2, proj2, proj2)


def _merge_bwd(dyb, w_out, proj2, pa, pc, pm):
    T, D = dyb.shape
    tm, tn = _tile(T, 1024), _tile(D, 512)
    _, _, gates, tile = _merge_specs(T, D, tm, tn)

    def body(dy_ref, w_ref, g0, g1, g2, p0, p1, p2, dp0, dp1, dp2, dg0, dg1, dg2):
        dm = lax.dot_general(dy_ref[...], w_ref[...], _DIMS["nt"], preferred_element_type=F32)
        for g_ref, p_ref, dp_ref, dg_ref in ((g0, p0, dp0, dg0), (g1, p1, dp1, dg1), (g2, p2, dp2, dg2)):
            gt = _sig(g_ref[...].astype(F32))
            dp_ref[...] = (gt * dm).astype(dp_ref.dtype)
            dg_ref[...] = (dm * p_ref[...].astype(F32) * gt * (1.0 - gt)).astype(dg_ref.dtype)

    shp = jax.ShapeDtypeStruct((T, D), MXU_DTYPE)
    return _call(body, name="merge_bwd", grid=(T // tm, D // tn),
                 in_specs=[pl.BlockSpec((tm, D), lambda i, n: (i, 0)), pl.BlockSpec((tn, D), lambda i, n: (n, 0))]
                 + gates + [tile] * 3,
                 out_specs=[tile] * 6, out_shape=[shp] * 6)(dyb, w_out, proj2, proj2, proj2, pa, pc, pm)


def _out_loss(merged, w_out, x, tgt):
    T, D = x.shape
    tm = _tile(T, 512)

    def body(m_ref, w_ref, x_ref, t_ref, dyb_ref, loss_ref):
        err = x_ref[...] + jnp.dot(m_ref[...], w_ref[...], preferred_element_type=F32) - t_ref[...]
        dyb_ref[...] = (err * (1.0 / D)).astype(dyb_ref.dtype)

        @pl.when(pl.program_id(0) == 0)
        def _():
            loss_ref[...] = jnp.zeros_like(loss_ref)

        loss_ref[...] += jnp.sum(err * err) * (0.5 / D)

    row = pl.BlockSpec((tm, D), lambda i: (i, 0))
    return _call(body, name="out_loss", grid=(T // tm,),
                 in_specs=[row, pl.BlockSpec((D, D), lambda i: (0, 0)), row, row],
                 out_specs=[row, pl.BlockSpec((1, 128), lambda i: (0, 0))],
                 out_shape=[jax.ShapeDtypeStruct((T, D), MXU_DTYPE), jax.ShapeDtypeStruct((1, 128), F32)],
                 )(merged, w_out, x, tgt)


def _proj_chunk(hb, w, meta, j, nslots, half, buf, name):
    T, D = hb.shape
    Cs = w.shape[1] // 4
    tm, tn = _tile(T, 1024), _tile(Cs // 2, 2176)
    nh = Cs // 2 // tn
    per = nh if half is not None else 2 * nh

    def body(meta_ref, a_ref, b_ref, *rest):
        rest[-1][...] = jnp.dot(a_ref[...], b_ref[...], preferred_element_type=F32).astype(rest[-1].dtype)

    def tile(n, m):
        if half is None:
            return n % per
        return (m[4] if half == 0 else 1 - m[4]) * nh + n % per

    in_specs = [pl.BlockSpec((tm, D), lambda n, i, m: (i, 0)),
                pl.BlockSpec((D, tn), lambda n, i, m: (0, (j + n // per) * 2 * nh + tile(n, m)))]
    args = [meta, hb, w]
    if buf is not None:
        in_specs.append(ANY)
        args.append(buf)
    spec = pltpu.PrefetchScalarGridSpec(
        num_scalar_prefetch=1, grid=(nslots * per, T // tm), in_specs=in_specs,
        out_specs=pl.BlockSpec((tm, tn), lambda n, i, m: (i, m[j + n // per] * 2 * nh + tile(n, m))))
    return _call(body, name=name, grid_spec=spec, out_shape=jax.ShapeDtypeStruct((T, 4 * Cs), PROJ_DTYPE),
                 aliases={} if buf is None else {3: 0})(*args)


def _norms(x, mem, norm_g, mem_norm_g):
    D = x.shape[-1]
    hb, hbt = _rms_fwd(x.reshape(-1, D), norm_g.reshape(1, D), "rms_x")
    mhb, _ = _rms_fwd(mem.reshape(-1, D), mem_norm_g.reshape(1, D), "rms_mem")
    return hb, hbt, mhb


def _attention_fwd(proj2, Bl, gq_all, gk_all):
    T, IN = proj2.shape
    proj3 = proj2.reshape(Bl, T // Bl, IN)
    os, ls = [], []
    for g, d in enumerate(DILATIONS):
        o, l = _attn_fwd(proj3, gq_all[g:g + 1], gk_all[g:g + 1], g, d)
        os.append(o.reshape(T, GW))
        ls.append(l.reshape(T, GW))
    return os, ls, _combine_fwd(os, ls, proj2)


def _conv_branch_fwd(proj2, Bl, conv_w):
    T, IN = proj2.shape
    cc, cct = _conv_fwd(proj2.reshape(Bl, T // Bl, IN), conv_w)
    return cc.reshape(T, CONVW), cct


def _weight_grads(x, mem, tgt, norm_g, mem_norm_g, gq_all, gk_all, conv_w, mem_gq, mem_gk, W, pre, early=None):
    Bl, S, D = x.shape
    T = Bl * S
    hb, hbt, mhb, proj2, os, ls, (a, at), (cc, cct) = pre
    IN = proj2.shape[1]
    proj3 = proj2.reshape(Bl, S, IN)
    x2, tgt2 = x.reshape(T, D), tgt.reshape(T, D)
    mem2 = mem.reshape(-1, D)
    ng, mng = norm_g.reshape(1, D), mem_norm_g.reshape(1, D)
    mgq, mgk = mem_gq.reshape(1, MEM_HD), mem_gk.reshape(1, MEM_HD)
    gqs = [gq_all[g:g + 1] for g in range(NGROUP)]
    gks = [gk_all[g:g + 1] for g in range(NGROUP)]

    mkv = _matmul(mhb, W["mem_w_kv"], "nn", F32, name="mem_kv", tm=512, tn=1024, tk=D)
    mkv3 = mkv.reshape(Bl, -1, 2 * MEMW)
    mo, mot = _mem_fwd(proj3, mkv3, mgq, mgk)
    mo = mo.reshape(T, MEMW)
    merged, mergedt, pa, pc, pm = _merge_fwd(a, cc, mo, W["w_br_attn"], W["w_br_conv"], W["w_br_mem"], proj2)
    dyb, loss = _out_loss(merged, W["w_out"], x2, tgt2)
    dy = dyb

    G = {}
    G["w_out"] = _matmul(mergedt, dyb, "nn", WIRE_DTYPE, name="dw_out", tm=1024, tn=512, tk=T)
    dpa, dpc, dpm, dg0, dg1, dg2 = _merge_bwd(dyb, W["w_out"], proj2, pa, pc, pm)
    G["w_br_attn"] = _matmul(at, dpa, "nn", WIRE_DTYPE, name="dw_br_attn", tm=512, tn=512, tk=T)
    G["w_br_conv"] = _matmul(cct, dpc, "nn", WIRE_DTYPE, name="dw_br_conv", tm=1024, tn=512, tk=T)
    G["w_br_mem"] = _matmul(mot, dpm, "nn", WIRE_DTYPE, name="dw_br_mem", tm=1024, tn=512, tk=T)
    da = _matmul(dpa, W["w_br_attn"], "nt", F32, name="d_attn", tm=1024, tn=512, tk=D)
    dcc = _matmul(dpc, W["w_br_conv"], "nt", F32, name="d_conv", tm=1024, tn=1024, tk=D)
    dmo = _matmul(dpm, W["w_br_mem"], "nt", F32, name="d_mem", tm=1024, tn=1024, tk=D)
    dmq, dzm, dmkv3, dmgq, dmgk = _mem_bwd(proj3, mkv3, mgq, mgk, dmo.reshape(Bl, S, MEMW))
    dmkv = _cast(dmkv3.reshape(-1, 2 * MEMW), "cast_dmkv")
    G["mem_w_kv"] = _matmul(mhb, dmkv, "tn", WIRE_DTYPE, name="dw_mem_kv", tm=1024, tn=1024, tk=512)
    early_state, da = (None, da) if early is None else early(G, da)
    dmh = _matmul(dmkv, W["mem_w_kv"], "nt", F32, name="d_memh", tm=512, tn=1024, tk=2 * MEMW)
    _, dmng = _rms_bwd(mem2, dmh, mng, None, "rms_mem_bwd")

    dos, dls, dza = _combine_bwd(os, ls, proj2, da)
    dqs, dks, dvs, dgq, dgk = [], [], [], [], []
    for g, d in enumerate(DILATIONS):
        dq, dk, dv, gq_g, gk_g = _attn_bwd(proj3, gqs[g], gks[g], os[g].reshape(Bl, S, GW), ls[g].reshape(Bl, S, GW),
                                           dos[g].reshape(Bl, S, GW), dls[g].reshape(Bl, S, GW), g, d)
        dqs.append(dq.reshape(T, GW).astype(MXU_DTYPE))
        dks.append(dk.reshape(T, GW).astype(MXU_DTYPE))
        dvs.append(dv.reshape(T, GW).astype(MXU_DTYPE))
        dgq.append(gq_g)
        dgk.append(gk_g)
    dcb, dcc_, dcv, dzc, dconv_w = _conv_bwd(proj3, conv_w, dcc.reshape(Bl, S, CONVW))

    dproj = jnp.concatenate(dqs + dks + dvs + [dza] + [t.reshape(T, CONVW) for t in (dcb, dcc_, dcv, dzc)]
                            + [dmq.reshape(T, MEMW), dzm.reshape(T, MEMW), dg0, dg1, dg2], axis=1)
    small = [loss, None, dmng] + dgq + dgk + [dconv_w.reshape(1, 3 * CONVW), dmgq, dmgk]
    return G, (dproj, x2, ng, dy, small), early_state


def _dw_in_half(hbt, dproj, pos, own, name):
    D, T = hbt.shape
    IN = dproj.shape[1]
    R, tn = D // 2, _tile(IN, 1024)

    def body(pos_ref, a_ref, b_ref, o_ref):
        o_ref[...] = jnp.dot(a_ref[...], b_ref[...], preferred_element_type=F32).astype(o_ref.dtype)

    spec = pltpu.PrefetchScalarGridSpec(
        num_scalar_prefetch=1, grid=(IN // tn,),
        in_specs=[pl.BlockSpec((R, T), lambda j, p: (p[1] if own else 1 - p[1], 0)),
                  pl.BlockSpec((T, tn), lambda j, p: (0, j))],
        out_specs=pl.BlockSpec((R, tn), lambda j, p: (0, j)))
    return _call(body, name=name, grid_spec=spec, out_shape=jax.ShapeDtypeStruct((R, IN), WIRE_DTYPE))(pos, hbt, dproj)


def _d_h(dproj, w, order):
    T, IN = dproj.shape
    D, Cs = w.shape[0], IN // 4
    tm, tn = _tile(T, 1024), _tile(D, 1024)

    def body(order_ref, a_ref, b_ref, o_ref, acc_ref):
        part = lax.dot_general(a_ref[...], b_ref[...], _DIMS["nt"], preferred_element_type=F32)
        k = pl.program_id(2)

        @pl.when(k == 0)
        def _():
            acc_ref[...] = part

        @pl.when(k > 0)
        def _():
            acc_ref[...] += part

        @pl.when(k == 3)
        def _():
            o_ref[...] = acc_ref[...]

    spec = pltpu.PrefetchScalarGridSpec(
        num_scalar_prefetch=1, grid=(T // tm, D // tn, 4),
        in_specs=[pl.BlockSpec((tm, Cs), lambda i, n, k, o: (i, o[k])), pl.BlockSpec((tn, Cs), lambda i, n, k, o: (n, k))],
        out_specs=pl.BlockSpec((tm, tn), lambda i, n, k, o: (i, n)), scratch_shapes=[pltpu.VMEM((tm, tn), F32)])
    return _call(body, name="d_h", grid_spec=spec, out_shape=jax.ShapeDtypeStruct((T, D), F32))(order, dproj, w)


def _input_grad(rest, w_in, order):
    dproj, x2, ng, dy, small = rest
    dh = _d_h(dproj, w_in, order)
    grad_x, dng = _rms_bwd(x2, dh, ng, dy, "rms_x_bwd")
    small = [dng if t is None else t for t in small]
    return grad_x, jnp.concatenate(small, axis=1)


def _local_step(x, mem, tgt, norm_g, mem_norm_g, gq_all, gk_all, conv_w, mem_gq, mem_gk, W):
    hb, hbt, mhb = _norms(x, mem, norm_g, mem_norm_g)
    Cs = W["w_in"].shape[1] // 4
    shards = (0, 2, 1, 3)
    order = jnp.array(shards, dtype=jnp.int32)
    w_rel = jnp.concatenate([W["w_in"][:, s * Cs:(s + 1) * Cs] for s in shards], axis=1)
    meta = jnp.array(shards + (0,), dtype=jnp.int32)
    proj2 = _proj_chunk(hb, w_rel, meta, 0, 1, None, None, "proj_0")
    for j, nslots in ((1, 2), (3, 1)):
        for half in (1, 0):
            proj2 = _proj_chunk(hb, w_rel, meta, j, nslots, half, proj2, f"proj_{j}_{half}")
    pre = (hb, hbt, mhb, proj2, *_attention_fwd(proj2, x.shape[0], gq_all, gk_all),
           _conv_branch_fwd(proj2, x.shape[0], conv_w))
    G, rest, _ = _weight_grads(x, mem, tgt, norm_g, mem_norm_g, gq_all, gk_all, conv_w, mem_gq, mem_gk, W, pre)
    pos = jnp.zeros((2,), jnp.int32)
    G["w_in"] = jnp.concatenate([_dw_in_half(hbt, rest[0], pos, True, "dw_in_own"),
                                 _dw_in_half(hbt, rest[0], pos, False, "dw_in_sibling")], axis=0)
    grad_x, small = _input_grad(rest, w_rel, order)
    return grad_x.reshape(x.shape), G, small


BIG = (("w_in", "col"), ("mem_w_kv", "row"), ("w_br_attn", "col"), ("w_br_conv", "col"),
       ("w_br_mem", "col"), ("w_out", "row"))


def _coords():
    return lax.axis_index("x"), lax.axis_index("y"), lax.axis_index("c")


def _other_chips(x, y):
    return [(1 - x, y), (x, 1 - y), (1 - x, 1 - y)]


def _half(ref, kind, c):
    R, C = ref.shape
    if kind == "col":
        return ref.at[pl.ds(c * (R // 2), R // 2), :]
    return ref.at[:, pl.ds(c * (C // 2), C // 2)]


def _shard(ref, kind, s):
    R, C = ref.shape
    if kind == "col":
        return ref.at[:, pl.ds(s * (C // 4), C // 4)]
    return ref.at[pl.ds(s * (R // 4), R // 4), :]


def _piece(ref, kind, s, c):
    R, C = ref.shape
    if kind == "col":
        return ref.at[pl.ds(c * (R // 2), R // 2), pl.ds(s * (C // 4), C // 4)]
    return ref.at[pl.ds(s * (R // 4), R // 4), pl.ds(c * (C // 2), C // 2)]


def _remote(src, dst, sems_s, sems_r, k, dev):
    return pltpu.make_async_remote_copy(src_ref=src, dst_ref=dst, send_sem=sems_s.at[k], recv_sem=sems_r.at[k],
                                        device_id=dev, device_id_type=MESH)


HBM = pl.BlockSpec(memory_space=pltpu.HBM)
SEM = pl.BlockSpec(memory_space=pltpu.SEMAPHORE)
EFFECT = pltpu.SideEffectType.DATAFLOW_SIDE_EFFECTING


def _hbm(a):
    return pltpu.with_memory_space_constraint(a, pltpu.HBM)


def _start_copies(name, arrays, ncopies, make):
    n = len(arrays)

    def body(*refs):
        for cp in make(refs[:n], refs[n], refs[n + 1]):
            cp.start()

    outs = pl.pallas_call(
        body, name=name,
        out_shape=(pltpu.SemaphoreType.DMA((ncopies,)), pltpu.SemaphoreType.DMA((ncopies,)),
                   *[jax.ShapeDtypeStruct(t.shape, t.dtype) for t in arrays]),
        in_specs=[HBM] * n, out_specs=(SEM, SEM, *([HBM] * n)),
        input_output_aliases={i: i + 2 for i in range(n)},
        compiler_params=pltpu.CompilerParams(has_side_effects=EFFECT),
    )(*[_hbm(t) for t in arrays])
    return outs[0], outs[1], list(outs[2:])


def _wait_copies(name, send, recv, arrays, make, after):
    n = len(arrays)

    def body(*refs):
        for cp in make(refs[:n], refs[n], refs[n + 1]):
            cp.wait_send()
            cp.wait_recv()

    outs = pl.pallas_call(
        body, name=name, out_shape=[jax.ShapeDtypeStruct(t.shape, t.dtype) for t in arrays],
        in_specs=[HBM] * n + [SEM, SEM, ANY], out_specs=[HBM] * n,
        input_output_aliases={i: i for i in range(n)},
        compiler_params=pltpu.CompilerParams(has_side_effects=EFFECT),
    )(*arrays, send, recv, after)
    return list(outs)


def _w_in_copies(relations):
    def make(refs, send, recv):
        x, y, c = _coords()
        me = 2 * x + y
        chips = _other_chips(x, y)
        w, conv = refs[0], refs[1]
        cps = []
        for i, k in enumerate(relations):
            cps.append(_remote(_column_half(w, 0, c), _column_half(w, 1 + k, c), send, recv, 2 * i, (*chips[k], c)))
            mine = _shard(conv, "col", me)
            cps.append(_remote(mine, mine, send, recv, 2 * i + 1, (*chips[k], c)))
        return cps
    return make


def _column_half(w, slot, c):
    half = w.shape[1] // 8
    return w.at[:, pl.ds((2 * slot + c) * half, half)]


def _w_in_forward(relations):
    def make(refs, send, recv):
        x, y, c = _coords()
        cps = []
        for i, k in enumerate(relations):
            got = _column_half(refs[0], 1 + k, c)
            cps.append(_remote(got, got, send, recv, i, (x, y, 1 - c)))
        return cps
    return make


def _sibling_copy(refs, send, recv):
    x, y, c = _coords()
    return [_remote(refs[0], refs[1], send, recv, 0, (x, y, 1 - c))]


def _other_weight_copies(refs, send, recv):
    x, y, c = _coords()
    me = 2 * x + y
    cps = []
    for k, chip in enumerate(_other_chips(x, y)):
        for p, (_, kind) in enumerate(BIG[1:]):
            mine = _piece(refs[p], kind, me, c)
            cps.append(_remote(mine, mine, send, recv, 3 * p + k, (*chip, c)))
    return cps


def _other_weight_forward(refs, send, recv):
    x, y, c = _coords()
    cps = []
    for k, chip in enumerate(_other_chips(x, y)):
        s = 2 * chip[0] + chip[1]
        for p, (_, kind) in enumerate(BIG[1:]):
            got = _piece(refs[p], kind, s, c)
            cps.append(_remote(got, got, send, recv, 3 * p + k, (x, y, 1 - c)))
    return cps


def _share_copies(group):
    def make(refs, send, recv):
        x, y, c = _coords()
        cps = []
        for p, (_, kind) in enumerate(group):
            mine = _half(refs[p], kind, c)
            cps.append(_remote(mine, mine, send, recv, p, (x, y, 1 - c)))
        return cps
    return make


def _sibling_forward(name, arrays, ncp, halves):
    n = len(arrays)

    def body(*refs):
        outs = refs[n:2 * n]
        send, recv = refs[2 * n:]
        x, y, c = _coords()
        sib = (x, y, 1 - c)
        cps = [_remote(got, got, send, recv, i, sib) for i, got in enumerate(halves(outs, c))]
        for cp in cps:
            cp.start()
        for cp in cps:
            cp.wait_send()
        for i, got in enumerate(halves(outs, 1 - c)):
            _remote(got, got, send, recv, i, sib).wait_recv()

    return pl.pallas_call(
        body, name=name, out_shape=[jax.ShapeDtypeStruct(t.shape, t.dtype) for t in arrays],
        in_specs=[ANY] * n, out_specs=[ANY] * n, input_output_aliases={i: i for i in range(n)},
        scratch_shapes=[pltpu.SemaphoreType.DMA((ncp,)), pltpu.SemaphoreType.DMA((ncp,))],
    )(*arrays)


def _landed_halves(refs, c):
    return [_half(r, "col", c) for r in refs]


def _other_weight_halves(refs, c):
    x, y, _ = _coords()
    out = []
    for chip in _other_chips(x, y):
        s = 2 * chip[0] + chip[1]
        out += [_piece(refs[p], kind, s, c) for p, (_, kind) in enumerate(BIG[1:])]
    return out


def _sibling_exchange(grads, group, name):
    n = len(group)
    shapes = []
    for (_, kind), g in zip(group, grads):
        R, C = g.shape
        shapes.append(jax.ShapeDtypeStruct((R // 2, C) if kind == "col" else (R, C // 2), g.dtype))

    def body(*refs):
        ins, outs = refs[:n], refs[n:2 * n]
        send, recv = refs[2 * n:]
        x, y, c = _coords()
        sib = (x, y, 1 - c)
        cps = [_remote(_half(ins[p], group[p][1], 1 - c), outs[p], send, recv, p, sib) for p in range(n)]
        for cp in cps:
            cp.start()
        for cp in cps:
            cp.wait()

    return pl.pallas_call(
        body, name=name, out_shape=shapes, in_specs=[ANY] * n, out_specs=[ANY] * n,
        scratch_shapes=[pltpu.SemaphoreType.DMA((n,)), pltpu.SemaphoreType.DMA((n,))],
    )(*grads)


def _presum(g, got, kind, pos, name):
    R, C = got.shape
    tr, tc = _tile(R, 512, 16), _tile(C, 2048)
    nr, nc = R // tr, C // tc

    def body(pos_ref, a_ref, b_ref, o_ref):
        o_ref[...] = (a_ref[...].astype(F32) + b_ref[...].astype(F32)).astype(o_ref.dtype)

    blk = pl.BlockSpec((tr, tc), lambda i, j, pos_ref: (i, j))
    if g.shape == got.shape:
        mine = blk
    elif kind == "col":
        mine = pl.BlockSpec((tr, tc), lambda i, j, pos_ref: (pos_ref[1] * nr + i, j))
    else:
        mine = pl.BlockSpec((tr, tc), lambda i, j, pos_ref: (i, pos_ref[1] * nc + j))
    spec = pltpu.PrefetchScalarGridSpec(num_scalar_prefetch=1, grid=(nr, nc), in_specs=[mine, blk], out_specs=blk)
    return _call(body, name=name, grid_spec=spec, out_shape=jax.ShapeDtypeStruct((R, C), WIRE_DTYPE))(pos, g, got)


def _chip_copies(group):
    n = len(group)

    def make(refs, send, recv):
        x, y, c = _coords()
        cps = []
        for k, chip in enumerate(_other_chips(x, y)):
            s = 2 * chip[0] + chip[1]
            for p in range(n):
                cps.append(_remote(_shard(refs[p], group[p][1], s), refs[n + p].at[k], send, recv, 3 * p + k, (*chip, c)))
        return cps
    return make


def _landing_zones(pres, group):
    lands = []
    for (_, kind), g in zip(group, pres):
        R, C = g.shape
        lands.append(lax.empty((3, R, C // 4) if kind == "col" else (3, R // 4, C), g.dtype))
    return lands


def _exchange_start(G, group, pos, carry, tag, pres=None):
    n = len(group)
    if pres is None:
        parts = [G[name] for name, _ in group]
        got = _sibling_exchange(parts, group, "sibling_exchange_" + tag)
        pres = [_presum(parts[p], got[p], kind, pos, "presum_" + name) for p, (name, kind) in enumerate(group)]
    make = _chip_copies(group)
    send, recv, thru = _start_copies("chip_exchange_start_" + tag, [*pres, *_landing_zones(pres, group), carry], 3 * n, make)
    return (send, recv, thru[:2 * n], make, tag), thru[2 * n]


def _exchange_wait(state, after):
    send, recv, arrays, make, tag = state
    thru = _wait_copies("chip_exchange_wait_" + tag, send, recv, arrays, make, after)
    n = len(thru) // 2
    return thru[:n], thru[n:]


def _reduce_into_shard(slots, pre, kind, pos, name):
    K, R, C = slots.shape
    tr, tc = _tile(R, 512, 16), _tile(C, 2176)
    nr, nc = R // tr, C // tc

    def body(pos_ref, s_ref, p_ref, o_ref):
        acc = p_ref[...].astype(F32)
        for k in range(K):
            acc = acc + s_ref[k].astype(F32)
        o_ref[...] = acc

    if kind == "col":
        own = pl.BlockSpec((tr, tc), lambda i, j, pos_ref: (i, pos_ref[0] * nc + j))
        full, out = (2 * R, C), pl.BlockSpec((tr, tc), lambda i, j, pos_ref: (pos_ref[1] * nr + i, j))
    else:
        own = pl.BlockSpec((tr, tc), lambda i, j, pos_ref: (pos_ref[0] * nr + i, j))
        full, out = (R, 2 * C), pl.BlockSpec((tr, tc), lambda i, j, pos_ref: (i, pos_ref[1] * nc + j))
    spec = pltpu.PrefetchScalarGridSpec(
        num_scalar_prefetch=1, grid=(nr, nc),
        in_specs=[pl.BlockSpec((K, tr, tc), lambda i, j, pos_ref: (0, i, j)), own], out_specs=out)
    return _call(body, name=name, grid_spec=spec, out_shape=jax.ShapeDtypeStruct(full, F32))(pos, slots, pre)


def _share_reduced(reds):
    n = len(BIG)

    def body(*refs):
        outs = refs[n:2 * n]
        send, recv = refs[2 * n:]
        x, y, c = _coords()
        sib = (x, y, 1 - c)
        cps = []
        for p in range(n):
            mine = _half(outs[p], BIG[p][1], c)
            cps.append(_remote(mine, mine, send, recv, p, sib))
        for cp in cps:
            cp.start()
        for cp in cps:
            cp.wait_send()
        for p in range(n):
            got = _half(outs[p], BIG[p][1], 1 - c)
            _remote(got, got, send, recv, p, sib).wait_recv()

    return pl.pallas_call(
        body, name="share_reduced", out_shape=[jax.ShapeDtypeStruct(r.shape, r.dtype) for r in reds],
        in_specs=[ANY] * n, out_specs=[ANY] * n, input_output_aliases={p: p for p in range(n)},
        scratch_shapes=[pltpu.SemaphoreType.DMA((n,)), pltpu.SemaphoreType.DMA((n,))],
    )(*reds)


def _gather_small(pack):
    _, N = pack.shape

    def body(in_ref, out_ref, send, recv, loc):
        x, y, c = _coords()
        me = 4 * x + 2 * y + c
        own = pltpu.make_async_copy(in_ref, out_ref.at[me], loc)
        own.start()
        cps = []
        for k in range(1, 8):
            dev = (x ^ (k >> 2), y ^ ((k >> 1) & 1), c ^ (k & 1))
            cps.append(_remote(in_ref, out_ref.at[me], send, recv, k - 1, dev))
        for cp in cps:
            cp.start()
        for k in range(1, 8):
            src = 4 * (x ^ (k >> 2)) + 2 * (y ^ ((k >> 1) & 1)) + (c ^ (k & 1))
            _remote(in_ref, out_ref.at[src], send, recv, k - 1, (x, y, c)).wait_recv()
        for cp in cps:
            cp.wait_send()
        own.wait()

    return pl.pallas_call(
        body, name="gather_small", out_shape=jax.ShapeDtypeStruct((8, 1, N), pack.dtype),
        in_specs=[ANY], out_specs=ANY,
        scratch_shapes=[pltpu.SemaphoreType.DMA((7,)), pltpu.SemaphoreType.DMA((7,)), pltpu.SemaphoreType.DMA(())],
    )(pack)


def _sum_small(slots):
    K, _, N = slots.shape

    def body(s_ref, o_ref):
        acc = s_ref[0]
        for k in range(1, K):
            acc = acc + s_ref[k]
        o_ref[...] = acc

    return _call(body, name="sum_small", in_specs=[pl.BlockSpec(memory_space=pltpu.VMEM)],
                 out_specs=pl.BlockSpec(memory_space=pltpu.VMEM), out_shape=jax.ShapeDtypeStruct((1, N), F32))(slots)


def _adamw(w, g, m, v, name, with_grad=False):
    R, C = w.shape
    tr, tc = _tile(R, 256, 8), _tile(C, 2176)

    def body(w_ref, g_ref, m_ref, v_ref, d_ref, nm_ref, nv_ref, *g_out):
        gv = g_ref[...]
        for ref in g_out:
            ref[...] = gv
        nm = ADAM_B1 * m_ref[...] + (1.0 - ADAM_B1) * gv
        nv = ADAM_B2 * v_ref[...] + (1.0 - ADAM_B2) * gv * gv
        m_hat = nm / (1.0 - ADAM_B1 ** ADAM_STEP)
        v_hat = nv / (1.0 - ADAM_B2 ** ADAM_STEP)
        d_ref[...] = -ADAM_LR * (m_hat / (jnp.sqrt(v_hat) + ADAM_EPS) + ADAM_WD * w_ref[...])
        nm_ref[...] = nm
        nv_ref[...] = nv

    spec = pl.BlockSpec((tr, tc), lambda i, j: (i, j))
    shp = jax.ShapeDtypeStruct((R, C), F32)
    nout = 4 if with_grad else 3
    return _call(body, name=name, grid=(R // tr, C // tc), in_specs=[spec] * 4, out_specs=[spec] * nout,
                 out_shape=[shp] * nout)(w, g, m, v)


SMALL = ("norm_g", "mem_norm_g", "attn_q_norm", "attn_k_norm", "conv_w", "mem_q_norm", "mem_k_norm")
WEIGHTS = ("norm_g", "mem_norm_g", "w_in", "attn_q_norm", "attn_k_norm", "conv_w", "mem_w_kv", "mem_q_norm",
           "mem_k_norm", "w_br_attn", "w_br_conv", "w_br_mem", "w_out")


def kernel(x, mem, norm_g, mem_norm_g, w_in, attn_q_norm, attn_k_norm, conv_w, mem_w_kv, mem_q_norm, mem_k_norm, w_br_attn, w_br_conv, w_br_mem, w_out, loss_target, m_norm_g, m_mem_norm_g, m_w_in, m_attn_q_norm, m_attn_k_norm, m_conv_w, m_mem_w_kv, m_mem_q_norm, m_mem_k_norm, m_w_br_attn, m_w_br_conv, m_w_br_mem, m_w_out, v_norm_g, v_mem_norm_g, v_w_in, v_attn_q_norm, v_attn_k_norm, v_conv_w, v_mem_w_kv, v_mem_q_norm, v_mem_k_norm, v_w_br_attn, v_w_br_conv, v_w_br_mem, v_w_out):
    w = dict(norm_g=norm_g, mem_norm_g=mem_norm_g, w_in=w_in, attn_q_norm=attn_q_norm, attn_k_norm=attn_k_norm,
             conv_w=conv_w, mem_w_kv=mem_w_kv, mem_q_norm=mem_q_norm, mem_k_norm=mem_k_norm, w_br_attn=w_br_attn,
             w_br_conv=w_br_conv, w_br_mem=w_br_mem, w_out=w_out)
    m = dict(norm_g=m_norm_g, mem_norm_g=m_mem_norm_g, w_in=m_w_in, attn_q_norm=m_attn_q_norm,
             attn_k_norm=m_attn_k_norm, conv_w=m_conv_w, mem_w_kv=m_mem_w_kv, mem_q_norm=m_mem_q_norm,
             mem_k_norm=m_mem_k_norm, w_br_attn=m_w_br_attn, w_br_conv=m_w_br_conv, w_br_mem=m_w_br_mem, w_out=m_w_out)
    v = dict(norm_g=v_norm_g, mem_norm_g=v_mem_norm_g, w_in=v_w_in, attn_q_norm=v_attn_q_norm,
             attn_k_norm=v_attn_k_norm, conv_w=v_conv_w, mem_w_kv=v_mem_w_kv, mem_q_norm=v_mem_q_norm,
             mem_k_norm=v_mem_k_norm, w_br_attn=v_w_br_attn, w_br_conv=v_w_br_conv, w_br_mem=v_w_br_mem, w_out=v_w_out)
    Bl, _, D = x.shape
    cx, cy = lax.axis_index("x"), lax.axis_index("y")
    chip = 2 * cx + cy
    pos = jnp.stack([chip, lax.axis_index("c")]).astype(jnp.int32)
    order = jnp.stack([chip] + [2 * a + b for a, b in _other_chips(cx, cy)]).astype(jnp.int32)
    n = len(BIG)

    w_rel = _place_shard(w["w_in"], "col", jnp.zeros((1,), jnp.int32), WIRE_DTYPE, "place_w_in")
    conv_full = _place_shard(conv_w, "col", pos, F32, "place_conv_w")
    others = [_place_shard(w[name], kind, pos, WIRE_DTYPE, "place_" + name) for name, kind in BIG[1:]]
    hb, hbt, mhb = _norms(x, mem, norm_g, mem_norm_g)

    meta = jnp.concatenate([order, pos[1:]])
    near, near_fwd = _w_in_copies((0, 1)), _w_in_forward((0, 1))
    send, recv, (w_rel, conv_full) = _start_copies("gather_near_start", [w_rel, conv_full], 4, near)
    proj = _proj_chunk(hb, w_rel, meta, 0, 1, None, None, "proj_own")
    w_rel, conv_full, *others = _wait_copies("gather_near_wait", send, recv, [w_rel, conv_full, *others], near, proj)

    fsend, frecv, (w_rel,) = _start_copies("gather_near_forward_start", [w_rel], 2, near_fwd)
    far, far_fwd = _w_in_copies((2,)), _w_in_forward((2,))
    send, recv, (w_rel, conv_full) = _start_copies("gather_far_start", [w_rel, conv_full], 2, far)
    proj = _proj_chunk(hb, w_rel, meta, 1, 2, 0, proj, "proj_near_landed")
    w_rel, = _wait_copies("gather_near_forward_wait", fsend, frecv, [w_rel], near_fwd, proj)
    proj = _proj_chunk(hb, w_rel, meta, 1, 2, 1, proj, "proj_near_forwarded")
    w_rel, conv_full = _wait_copies("gather_far_wait", send, recv, [w_rel, conv_full], far, proj)

    fsend, frecv, (w_rel,) = _start_copies("gather_far_forward_start", [w_rel], 1, far_fwd)
    send, recv, (*others, w_rel) = _start_copies("gather_rest_start", [*others, w_rel], 3 * (n - 1), _other_weight_copies)
    proj = _proj_chunk(hb, w_rel, meta, 3, 1, 0, proj, "proj_far_landed")
    w_rel, = _wait_copies("gather_far_forward_wait", fsend, frecv, [w_rel], far_fwd, proj)
    proj = _proj_chunk(hb, w_rel, meta, 3, 1, 1, proj, "proj_far_forwarded")
    os, ls, a = _attention_fwd(proj, Bl, attn_q_norm, attn_k_norm)
    *others, w_rel = _wait_copies("gather_rest_wait", send, recv, [*others, w_rel], _other_weight_copies, a[0])
    fsend, frecv, (*others, proj) = _start_copies("gather_rest_forward_start", [*others, proj], 3 * (n - 1),
                                                  _other_weight_forward)
    cc = _conv_branch_fwd(proj, Bl, conv_full)
    others = _wait_copies("gather_rest_forward_wait", fsend, frecv, others, _other_weight_forward, cc[0])
    W = {name: others[p] for p, (name, _) in enumerate(BIG[1:])}

    G, rest, rest_state = _weight_grads(
        x, mem, loss_target, norm_g, mem_norm_g, attn_q_norm, attn_k_norm, conv_full, mem_q_norm, mem_k_norm, W,
        (hb, hbt, mhb, proj, os, ls, a, cc), early=lambda G, carry: _exchange_start(G, BIG[1:], pos, carry, "rest"))

    for_sibling = _dw_in_half(hbt, rest[0], pos, False, "dw_in_sibling")
    send, recv, (for_sibling, got, dproj) = _start_copies(
        "sibling_w_in_start", [for_sibling, lax.empty(for_sibling.shape, for_sibling.dtype), rest[0]], 1, _sibling_copy)
    mine = _dw_in_half(hbt, dproj, pos, True, "dw_in_own")
    for_sibling, got = _wait_copies("sibling_w_in_wait", send, recv, [for_sibling, got], _sibling_copy, mine)
    pre_w_in = _presum(mine, got, "col", pos, "presum_w_in")

    w_in_state, dproj = _exchange_start(G, BIG[:1], pos, dproj, "w_in", pres=[pre_w_in])
    grad_x, small = _input_grad((dproj, *rest[1:]), w_rel, order)
    pres_rest, slots_rest = _exchange_wait(rest_state, grad_x)
    reds_rest = [_reduce_into_shard(slots_rest[p], pres_rest[p], kind, pos, "reduce_" + name)
                 for p, (name, kind) in enumerate(BIG[1:])]
    share_rest = _share_copies(BIG[1:])
    rsend, rrecv, reds_rest = _start_copies("share_rest_start", reds_rest, n - 1, share_rest)
    pres, slots = _exchange_wait(w_in_state, grad_x)
    red_w_in = _reduce_into_shard(slots[0], pres[0], "col", pos, "reduce_w_in")
    share_w_in = _share_copies(BIG[:1])
    wsend, wrecv, (red_w_in, small) = _start_copies("share_w_in_start", [red_w_in, small], 1, share_w_in)
    grad_x = grad_x.reshape(x.shape)

    tot = _sum_small(_gather_small(small))
    reds_rest = _wait_copies("share_rest_wait", rsend, rrecv, reds_rest, share_rest, tot)
    grads = dict(zip([name for name, _ in BIG[1:]], reds_rest))
    tot = tot[0]
    loss = tot[0]
    off = 128
    for name, size in (("norm_g", D), ("mem_norm_g", D), ("attn_q_norm", NGROUP * HEAD), ("attn_k_norm", NGROUP * HEAD),
                       ("conv_w", 3 * CONVW), ("mem_q_norm", MEM_HD), ("mem_k_norm", MEM_HD)):
        grads[name] = tot[off:off + size]
        off += size
    cw = conv_w.shape[1]
    grads["conv_w"] = lax.dynamic_slice(grads["conv_w"].reshape(3, CONVW), (0, chip * cw), (3, cw))
    for name in SMALL:
        grads[name] = grads[name].reshape(w[name].shape)

    delta, new_m, new_v = {}, {}, {}
    for name, _ in BIG[1:]:
        delta[name], new_m[name], new_v[name], grads[name] = _adamw(w[name], grads[name], m[name], v[name],
                                                                    "adamw_" + name, with_grad=True)

    def packed(t):
        return jnp.concatenate([t[name].reshape(1, -1) for name in SMALL], axis=1)

    ds, ms, vs = _adamw(packed(w), packed(grads), packed(m), packed(v), "adamw_small")
    shared, = _wait_copies("share_w_in_wait", wsend, wrecv, [red_w_in], share_w_in, ds)
    delta["w_in"], new_m["w_in"], new_v["w_in"], grads["w_in"] = _adamw(w["w_in"], shared, m["w_in"], v["w_in"],
                                                                        "adamw_w_in", with_grad=True)
    off = 0
    for name in SMALL:
        size = w[name].size
        delta[name] = ds[0, off:off + size].reshape(w[name].shape)
        new_m[name] = ms[0, off:off + size].reshape(w[name].shape)
        new_v[name] = vs[0, off:off + size].reshape(w[name].shape)
        off += size

    return (loss, grad_x, *[grads[n] for n in WEIGHTS], *[delta[n] for n in WEIGHTS],
            *[new_m[n] for n in WEIGHTS], *[new_v[n] for n in WEIGHTS])
```

```python
import functools

import jax
import jax.numpy as jnp
from jax import lax
from jax.experimental import pallas as pl
from jax.experimental.pallas import tpu as pltpu

F32 = jnp.float32
MXU_DTYPE = jnp.bfloat16
WIRE_DTYPE = jnp.bfloat16
PROJ_DTYPE = jnp.bfloat16
EPS = 1e-6
NEG = -1e30

HEAD = 128
HPG = 4
GW = HPG * HEAD
DILATIONS = (1, 4, 16)
NGROUP = len(DILATIONS)
BLK = 128
QKV = NGROUP * GW
CONVW = 1024
MEM_HEADS = 4
MEM_HD = 256
MEMW = MEM_HEADS * MEM_HD
Q0, K0, V0 = 0, QKV, 2 * QKV
ZA = 3 * QKV
CB, CC, CV, ZC = ZA + GW, ZA + GW + CONVW, ZA + GW + 2 * CONVW, ZA + GW + 3 * CONVW
MQ = ZC + CONVW
ZM = MQ + MEMW
G0 = ZM + MEMW

ADAM_LR, ADAM_B1, ADAM_B2, ADAM_EPS, ADAM_WD, ADAM_STEP = 0.001, 0.9, 0.999, 1e-08, 0.01, 10

VMEM_LIMIT = 56 * 1024 * 1024
MESH = pl.DeviceIdType.MESH
ANY = pl.BlockSpec(memory_space=pl.ANY)


def _tile(n, pref, mult=128):
    t = min(pref, n)
    while t > mult and (n % t or t % mult):
        t -= mult
    assert n % t == 0, (n, pref)
    return t


def _call(body, *, name, out_shape, grid=(), in_specs=None, out_specs=None, scratch_shapes=(),
          aliases=None, grid_spec=None):
    kw = {}
    if grid_spec is not None:
        kw["grid_spec"] = grid_spec
        ngrid = len(grid_spec.grid)
    else:
        kw.update(grid=grid, in_specs=in_specs, out_specs=out_specs, scratch_shapes=list(scratch_shapes))
        ngrid = len(grid)
    params = pltpu.CompilerParams(dimension_semantics=("arbitrary",) * ngrid, vmem_limit_bytes=VMEM_LIMIT)
    return pl.pallas_call(body, name=name, out_shape=out_shape, compiler_params=params,
                          input_output_aliases=aliases or {}, **kw)


_DIMS = {"nn": (((1,), (0,)), ((), ())), "nt": (((1,), (1,)), ((), ())), "tn": (((0,), (0,)), ((), ()))}


def _mxu(a, b, mode):
    return lax.dot_general(a.astype(MXU_DTYPE), b.astype(MXU_DTYPE), _DIMS[mode], preferred_element_type=F32)


@functools.partial(jax.custom_vjp, nondiff_argnums=(2,))
def _dot(a, b, mode):
    return _mxu(a, b, mode)


def _dot_fwd(a, b, mode):
    return _mxu(a, b, mode), (a, b)


def _dot_bwd(mode, res, g):
    a, b = res
    if mode == "nn":
        return _mxu(g, b, "nt"), _mxu(a, g, "tn")
    if mode == "nt":
        return _mxu(g, b, "nn"), _mxu(g, a, "tn")
    return _mxu(b, g, "nt"), _mxu(a, g, "nn")


_dot.defvjp(_dot_fwd, _dot_bwd)


def _sig(z):
    return 1.0 / (1.0 + jnp.exp(-z))


def _silu(z):
    return z * _sig(z)


def _rms_rows(t, g):
    return t * lax.rsqrt(jnp.mean(t * t, axis=-1, keepdims=True) + EPS) * g


def _attn_block(q, k2, v2, gq, gk, first):
    qn = _rms_rows(q, gq)
    kn = _rms_rows(k2, gk)
    s = jnp.where(_band_mask(first, k2.shape[0]), _dot(qn, kn, "nt") * (HEAD ** -0.5), NEG)
    m = lax.stop_gradient(jnp.max(s, axis=-1, keepdims=True))
    p = jnp.exp(s - m)
    den = jnp.sum(p, axis=-1, keepdims=True)
    o = _dot(p, v2, "nn") / den
    return o, m + jnp.log(den)


def _band_mask(first, nkeys):
    a = lax.broadcasted_iota(jnp.int32, (BLK, nkeys), 0)
    b = lax.broadcasted_iota(jnp.int32, (BLK, nkeys), 1)
    if nkeys == BLK:
        return b <= a
    return (b >= a) & (b <= a + BLK) & (b >= jnp.where(first, BLK, 0))


def _norm_parts(t):
    r = lax.rsqrt(jnp.mean(t * t, axis=-1, keepdims=True) + EPS)
    return r, t * r


def _norm_bwd(dn, g, r, th):
    dth = dn * g
    return r * (dth - th * jnp.mean(dth * th, axis=-1, keepdims=True)), jnp.sum(dn * th, axis=0, keepdims=True)


def _attn_block_bwd(q, k2, v2, gq, gk, first, do, o, lse, dlse):
    scale = HEAD ** -0.5
    rq, qh = _norm_parts(q)
    rk, kh = _norm_parts(k2)
    qn, kn = qh * gq, kh * gk
    s = jnp.where(_band_mask(first, k2.shape[0]), _mxu(qn, kn, "nt") * scale, NEG)
    p = jnp.exp(s - lse)
    ds = p * (_mxu(do, v2, "nt") + (dlse - jnp.sum(do * o, axis=-1, keepdims=True))) * scale
    dq, dgq = _norm_bwd(_mxu(ds, kn, "nn"), gq, rq, qh)
    dk2, dgk = _norm_bwd(_mxu(ds, qn, "tn"), gk, rk, kh)
    return dq, dk2, _mxu(p, do, "tn"), dgq, dgk


def _combine(o1, o2, o3, l1, l2, l3, z):
    m = lax.stop_gradient(jnp.maximum(jnp.maximum(l1, l2), l3))
    e1, e2, e3 = jnp.exp(l1 - m), jnp.exp(l2 - m), jnp.exp(l3 - m)
    return (e1 * o1 + e2 * o2 + e3 * o3) / (e1 + e2 + e3) * _silu(z)


def _mem_block(q, z, kv, gq, gk):
    outs = []
    for h in range(MEM_HEADS):
        sl = slice(h * MEM_HD, (h + 1) * MEM_HD)
        qn = _rms_rows(q[:, sl], gq)
        kn = _rms_rows(kv[:, sl], gk)
        s = _dot(qn, kn, "nt") * (MEM_HD ** -0.5)
        m = lax.stop_gradient(jnp.max(s, axis=-1, keepdims=True))
        p = jnp.exp(s - m)
        den = jnp.sum(p, axis=-1, keepdims=True)
        outs.append(_dot(p, kv[:, MEMW + h * MEM_HD:MEMW + (h + 1) * MEM_HD], "nn") / den)
    return jnp.concatenate(outs, axis=-1) * _silu(z)


def _cast(w, name):
    R, C = w.shape
    tr, tc = _tile(R, 512, 8), _tile(C, 2176)

    def body(w_ref, o_ref):
        o_ref[...] = w_ref[...].astype(o_ref.dtype)

    spec = pl.BlockSpec((tr, tc), lambda i, j: (i, j))
    return _call(body, name=name, grid=(R // tr, C // tc), in_specs=[spec], out_specs=spec,
                 out_shape=jax.ShapeDtypeStruct((R, C), WIRE_DTYPE))(w)


def _place_shard(w, kind, pos, dtype, name, slot=0, into=None):
    R, C = w.shape
    tr, tc = _tile(R, 512, 8), _tile(C, 2176)
    nr, nc = R // tr, C // tc

    def body(pos_ref, w_ref, *rest):
        rest[-1][...] = w_ref[...].astype(rest[-1].dtype)

    if kind == "col":
        full, out = (R, 4 * C), pl.BlockSpec((tr, tc), lambda i, j, pos_ref: (i, pos_ref[slot] * nc + j))
    else:
        full, out = (4 * R, C), pl.BlockSpec((tr, tc), lambda i, j, pos_ref: (pos_ref[slot] * nr + i, j))
    in_specs, args = [pl.BlockSpec((tr, tc), lambda i, j, pos_ref: (i, j))], [pos, w]
    if into is not None:
        in_specs.append(ANY)
        args.append(into)
    spec = pltpu.PrefetchScalarGridSpec(num_scalar_prefetch=1, grid=(nr, nc), in_specs=in_specs, out_specs=out)
    return _call(body, name=name, grid_spec=spec, out_shape=jax.ShapeDtypeStruct(full, dtype),
                 aliases={} if into is None else {2: 0})(*args)


def _matmul(a, b, mode, out_dtype, *, name, tm=512, tn=512, tk=512):
    if mode == "nn":
        (M, K), (_, N) = a.shape, b.shape
    elif mode == "nt":
        (M, K), (N, _) = a.shape, b.shape
    else:
        (K, M), (_, N) = a.shape, b.shape
    tm, tn, tk = _tile(M, tm), _tile(N, tn), _tile(K, tk)
    nk = K // tk

    def body(a_ref, b_ref, o_ref, *acc):
        part = lax.dot_general(a_ref[...], b_ref[...], _DIMS[mode], preferred_element_type=F32)
        if nk == 1:
            o_ref[...] = part.astype(o_ref.dtype)
            return
        acc_ref, = acc
        k = pl.program_id(2)

        @pl.when(k == 0)
        def _():
            acc_ref[...] = part

        @pl.when(k > 0)
        def _():
            acc_ref[...] += part

        @pl.when(k == nk - 1)
        def _():
            o_ref[...] = acc_ref[...].astype(o_ref.dtype)

    a_spec = pl.BlockSpec((tk, tm), lambda i, j, k: (k, i)) if mode == "tn" else pl.BlockSpec((tm, tk), lambda i, j, k: (i, k))
    b_spec = pl.BlockSpec((tn, tk), lambda i, j, k: (j, k)) if mode == "nt" else pl.BlockSpec((tk, tn), lambda i, j, k: (k, j))
    return _call(body, name=name, grid=(M // tm, N // tn, nk), in_specs=[a_spec, b_spec],
                 out_specs=pl.BlockSpec((tm, tn), lambda i, j, k: (i, j)),
                 out_shape=jax.ShapeDtypeStruct((M, N), out_dtype),
                 scratch_shapes=[] if nk == 1 else [pltpu.VMEM((tm, tn), F32)])(a, b)


def _rms_fwd(x, g, name):
    R, D = x.shape
    tr = _tile(R, 512)

    def body(x_ref, g_ref, o_ref, t_ref):
        y = _rms_rows(x_ref[...], g_ref[...])
        o_ref[...] = y.astype(o_ref.dtype)
        t_ref[...] = y.T.astype(t_ref.dtype)

    row = pl.BlockSpec((tr, D), lambda i: (i, 0))
    return _call(body, name=name, grid=(R // tr,), in_specs=[row, pl.BlockSpec((1, D), lambda i: (0, 0))],
                 out_specs=[row, pl.BlockSpec((D, tr), lambda i: (0, i))],
                 out_shape=[jax.ShapeDtypeStruct((R, D), MXU_DTYPE), jax.ShapeDtypeStruct((D, R), MXU_DTYPE)])(x, g)


def _rms_bwd(x, dh, g, dy, name):
    R, D = x.shape
    tr = _tile(R, 256)
    with_dx = dy is not None

    def body(*refs):
        if with_dx:
            x_ref, dh_ref, g_ref, dy_ref, dx_ref, dg_ref = refs
        else:
            x_ref, dh_ref, g_ref, dg_ref = refs
        xv, dhv = x_ref[...], dh_ref[...]
        r = lax.rsqrt(jnp.mean(xv * xv, axis=-1, keepdims=True) + EPS)
        xh = xv * r

        @pl.when(pl.program_id(0) == 0)
        def _():
            dg_ref[...] = jnp.zeros_like(dg_ref)

        dg_ref[...] += jnp.sum(dhv * xh, axis=0, keepdims=True)
        if with_dx:
            dxh = dhv * g_ref[...]
            dx_ref[...] = dy_ref[...] + r * (dxh - xh * jnp.mean(dxh * xh, axis=-1, keepdims=True))

    row = pl.BlockSpec((tr, D), lambda i: (i, 0))
    vec = pl.BlockSpec((1, D), lambda i: (0, 0))
    dg_shape = jax.ShapeDtypeStruct((1, D), F32)
    if with_dx:
        return _call(body, name=name, grid=(R // tr,), in_specs=[row, row, vec, row], out_specs=[row, vec],
                     out_shape=[jax.ShapeDtypeStruct((R, D), F32), dg_shape])(x, dh, g, dy)
    return None, _call(body, name=name, grid=(R // tr,), in_specs=[row, row, vec], out_specs=vec,
                       out_shape=dg_shape)(x, dh, g)


def _attn_geom(g, d):
    hc = HPG if d == 1 else 1
    cw = hc * HEAD
    cq, ck, cv = (Q0 + g * GW) // cw, (K0 + g * GW) // cw, (V0 + g * GW) // cw
    return (1, BLK * d, cw), hc, HPG // hc, cq, ck, cv


def _rows(ref, r, d, sl):
    if d == 1:
        return ref[0, :, sl]
    return ref.at[0][pl.ds(r, BLK, stride=d), sl]


def _set_rows(ref, r, d, sl, val):
    if d == 1:
        ref[0, :, sl] = val
    else:
        ref.at[0][pl.ds(r, BLK, stride=d), sl] = val


def _stage_rows(ref, r, d, sl, val):
    if d == 1:
        ref[:, sl] = val
    else:
        ref[pl.ds(r, BLK, stride=d), sl] = val


def _proj_stages(blk, d):
    return [] if d == 1 else [pltpu.VMEM(blk[1:], F32)] * 5


def _proj_rows(refs, stages, d):
    if d == 1:
        return [lambda r, sl, ref=ref: ref[0, :, sl].astype(F32) for ref in refs]
    for ref, stage in zip(refs, stages):
        stage[...] = ref[0].astype(F32)
    return [lambda r, sl, stage=stage: stage[pl.ds(r, BLK, stride=d), sl] for stage in stages]


def _attn_fwd(proj3, gq, gk, g, d):
    Bl, S, _ = proj3.shape
    blk, hc, ncb, cq, ck, cv = _attn_geom(g, d)
    nb = S // blk[1]
    if nb == 1:
        return _attn_single_fwd(proj3, gq, gk, g, d)

    def body(q_ref, kp_ref, kc_ref, vp_ref, vc_ref, gq_ref, gk_ref, o_ref, lse_ref, *stages):
        first = pl.program_id(2) == 0
        q, kp, kc, vp, vc = _proj_rows((q_ref, kp_ref, kc_ref, vp_ref, vc_ref), stages, d)
        for r in range(d):
            for h in range(hc):
                sl = slice(h * HEAD, (h + 1) * HEAD)
                k2 = jnp.concatenate([kp(r, sl), kc(r, sl)], axis=0)
                v2 = jnp.concatenate([vp(r, sl), vc(r, sl)], axis=0)
                o, lse = _attn_block(q(r, sl), k2, v2, gq_ref[...], gk_ref[...], first)
                _set_rows(o_ref, r, d, sl, o)
                _set_rows(lse_ref, r, d, sl, jnp.broadcast_to(lse, (BLK, HEAD)))

    def cur(c0):
        return pl.BlockSpec(blk, lambda b, j, i: (b, i, c0 + j))

    def prev(c0):
        return pl.BlockSpec(blk, lambda b, j, i: (b, jnp.maximum(i - 1, 0), c0 + j))

    vec = pl.BlockSpec((1, HEAD), lambda b, j, i: (0, 0))
    out = pl.BlockSpec(blk, lambda b, j, i: (b, i, j))
    shp = jax.ShapeDtypeStruct((Bl, S, GW), F32)
    return _call(body, name=f"attn_fwd_g{g}", grid=(Bl, ncb, nb),
                 in_specs=[cur(cq), prev(ck), cur(ck), prev(cv), cur(cv), vec, vec],
                 out_specs=[out, out], out_shape=[shp, shp], scratch_shapes=_proj_stages(blk, d),
                 )(proj3, proj3, proj3, proj3, proj3, gq, gk)


def _attn_single_fwd(proj3, gq, gk, g, d):
    Bl, S, _ = proj3.shape
    blk, hc, ncb, cq, ck, cv = _attn_geom(g, d)

    def body(q_ref, k_ref, v_ref, gq_ref, gk_ref, o_ref, lse_ref, *stages):
        q, k, v = _proj_rows((q_ref, k_ref, v_ref), stages, d)
        for r in range(d):
            for h in range(hc):
                sl = slice(h * HEAD, (h + 1) * HEAD)
                o, lse = _attn_block(q(r, sl), k(r, sl), v(r, sl), gq_ref[...], gk_ref[...], True)
                _set_rows(o_ref, r, d, sl, o)
                _set_rows(lse_ref, r, d, sl, jnp.broadcast_to(lse, (BLK, HEAD)))

    def at(c0):
        return pl.BlockSpec(blk, lambda b, j: (b, 0, c0 + j))

    vec = pl.BlockSpec((1, HEAD), lambda b, j: (0, 0))
    shp = jax.ShapeDtypeStruct((Bl, S, GW), F32)
    return _call(body, name=f"attn_fwd_g{g}", grid=(Bl, ncb), in_specs=[at(cq), at(ck), at(cv), vec, vec],
                 out_specs=[at(0), at(0)], out_shape=[shp, shp], scratch_shapes=_proj_stages(blk, d)[:3],
                 )(proj3, proj3, proj3, gq, gk)


def _attn_single_bwd(proj3, gq, gk, o3, l3, do3, dl3, g, d):
    Bl, S, _ = proj3.shape
    blk, hc, ncb, cq, ck, cv = _attn_geom(g, d)

    def body(q_ref, k_ref, v_ref, gq_ref, gk_ref, o_ref, l_ref, do_ref, dl_ref,
             dq_ref, dk_ref, dv_ref, dgq_ref, dgk_ref, sq_ref, sk_ref, sv_ref, *stages):
        @pl.when((pl.program_id(0) == 0) & (pl.program_id(1) == 0))
        def _():
            dgq_ref[...] = jnp.zeros_like(dgq_ref)
            dgk_ref[...] = jnp.zeros_like(dgk_ref)

        dgq, dgk = jnp.zeros((1, HEAD), F32), jnp.zeros((1, HEAD), F32)
        q, k, v = _proj_rows((q_ref, k_ref, v_ref), stages, d)
        for r in range(d):
            for h in range(hc):
                sl = slice(h * HEAD, (h + 1) * HEAD)
                dq, dk, dv, a, b = _attn_block_bwd(
                    q(r, sl), k(r, sl), v(r, sl), gq_ref[...], gk_ref[...], True, _rows(do_ref, r, d, sl),
                    _rows(o_ref, r, d, sl), _rows(l_ref, r, d, sl)[:, :1], _rows(dl_ref, r, d, sl)[:, :1])
                _stage_rows(sq_ref, r, d, sl, dq)
                _stage_rows(sk_ref, r, d, sl, dk)
                _stage_rows(sv_ref, r, d, sl, dv)
                dgq, dgk = dgq + a, dgk + b
        dgq_ref[...] += dgq
        dgk_ref[...] += dgk
        dq_ref[0] = sq_ref[...].astype(dq_ref.dtype)
        dk_ref[0] = sk_ref[...].astype(dk_ref.dtype)
        dv_ref[0] = sv_ref[...].astype(dv_ref.dtype)

    def at(c0):
        return pl.BlockSpec(blk, lambda b, j: (b, 0, c0 + j))

    vec = pl.BlockSpec((1, HEAD), lambda b, j: (0, 0))
    shp = jax.ShapeDtypeStruct((Bl, S, GW), MXU_DTYPE)
    gshp = jax.ShapeDtypeStruct((1, HEAD), F32)
    return _call(body, name=f"attn_bwd_g{g}", grid=(Bl, ncb),
                 in_specs=[at(cq), at(ck), at(cv), vec, vec, at(0), at(0), at(0), at(0)],
                 out_specs=[at(0), at(0), at(0), vec, vec], out_shape=[shp, shp, shp, gshp, gshp],
                 scratch_shapes=[pltpu.VMEM(blk[1:], F32)] * 3 + _proj_stages(blk, d)[:3],
                 )(proj3, proj3, proj3, gq, gk, o3, l3, do3, dl3)


def _attn_bwd(proj3, gq, gk, o3, l3, do3, dl3, g, d):
    Bl, S, _ = proj3.shape
    blk, hc, ncb, cq, ck, cv = _attn_geom(g, d)
    nb = S // blk[1]
    if nb == 1:
        return _attn_single_bwd(proj3, gq, gk, o3, l3, do3, dl3, g, d)

    def body(q_ref, kp_ref, kc_ref, vp_ref, vc_ref, gq_ref, gk_ref, o_ref, l_ref, do_ref, dl_ref,
             dq_ref, dk_ref, dv_ref, dgq_ref, dgk_ref, ck_ref, cv_ref, sq_ref, sk_ref, sv_ref, *stages):
        i = pl.program_id(2)
        first = i == 0

        @pl.when((pl.program_id(0) == 0) & (pl.program_id(1) == 0) & first)
        def _():
            dgq_ref[...] = jnp.zeros_like(dgq_ref)
            dgk_ref[...] = jnp.zeros_like(dgk_ref)

        @pl.when(first)
        def _():
            ck_ref[...] = jnp.zeros_like(ck_ref)
            cv_ref[...] = jnp.zeros_like(cv_ref)

        @pl.when(i < nb)
        def _():
            dgq, dgk = jnp.zeros((1, HEAD), F32), jnp.zeros((1, HEAD), F32)
            q, kp, kc, vp, vc = _proj_rows((q_ref, kp_ref, kc_ref, vp_ref, vc_ref), stages, d)
            for r in range(d):
                rs = slice(r * BLK, (r + 1) * BLK)
                for h in range(hc):
                    sl = slice(h * HEAD, (h + 1) * HEAD)
                    k2 = jnp.concatenate([kp(r, sl), kc(r, sl)], axis=0)
                    v2 = jnp.concatenate([vp(r, sl), vc(r, sl)], axis=0)
                    dq, dk2, dv2, a, b = _attn_block_bwd(
                        q(r, sl), k2, v2, gq_ref[...], gk_ref[...], first, _rows(do_ref, r, d, sl),
                        _rows(o_ref, r, d, sl), _rows(l_ref, r, d, sl)[:, :1], _rows(dl_ref, r, d, sl)[:, :1])
                    _stage_rows(sq_ref, r, d, sl, dq)
                    _stage_rows(sk_ref, r, d, sl, ck_ref[rs, sl] + dk2[:BLK])
                    _stage_rows(sv_ref, r, d, sl, cv_ref[rs, sl] + dv2[:BLK])
                    ck_ref[rs, sl] = dk2[BLK:]
                    cv_ref[rs, sl] = dv2[BLK:]
                    dgq, dgk = dgq + a, dgk + b
            dgq_ref[...] += dgq
            dgk_ref[...] += dgk
            dq_ref[0] = sq_ref[...].astype(dq_ref.dtype)

        @pl.when(i == nb)
        def _():
            for r in range(d):
                rs = slice(r * BLK, (r + 1) * BLK)
                _stage_rows(sk_ref, r, d, slice(None), ck_ref[rs, :])
                _stage_rows(sv_ref, r, d, slice(None), cv_ref[rs, :])

        dk_ref[0] = sk_ref[...].astype(dk_ref.dtype)
        dv_ref[0] = sv_ref[...].astype(dv_ref.dtype)

    def cur(c0):
        return pl.BlockSpec(blk, lambda b, j, i: (b, jnp.minimum(i, nb - 1), c0 + j))

    def prev(c0):
        return pl.BlockSpec(blk, lambda b, j, i: (b, jnp.clip(i - 1, 0, nb - 1), c0 + j))

    vec = pl.BlockSpec((1, HEAD), lambda b, j, i: (0, 0))
    at_q = pl.BlockSpec(blk, lambda b, j, i: (b, jnp.minimum(i, nb - 1), j))
    at_k = pl.BlockSpec(blk, lambda b, j, i: (b, jnp.maximum(i - 1, 0), j))
    shp = jax.ShapeDtypeStruct((Bl, S, GW), MXU_DTYPE)
    gshp = jax.ShapeDtypeStruct((1, HEAD), F32)
    return _call(body, name=f"attn_bwd_g{g}", grid=(Bl, ncb, nb + 1),
                 in_specs=[cur(cq), prev(ck), cur(ck), prev(cv), cur(cv), vec, vec, at_q, at_q, at_q, at_q],
                 out_specs=[at_q, at_k, at_k, vec, vec], out_shape=[shp, shp, shp, gshp, gshp],
                 scratch_shapes=[pltpu.VMEM(blk[1:], F32)] * 5 + _proj_stages(blk, d),
                 )(proj3, proj3, proj3, proj3, proj3, gq, gk, o3, l3, do3, dl3)


def _combine_fwd(os, ls, proj2):
    T = proj2.shape[0]
    tr = _tile(T, 512)

    def body(o1, o2, o3, l1, l2, l3, z, a_ref, at_ref):
        a = _combine(o1[...], o2[...], o3[...], l1[...], l2[...], l3[...], z[...].astype(F32))
        a_ref[...] = a.astype(a_ref.dtype)
        at_ref[...] = a.T.astype(at_ref.dtype)

    row = pl.BlockSpec((tr, GW), lambda i: (i, 0))
    return _call(body, name="combine_fwd", grid=(T // tr,),
                 in_specs=[row] * 6 + [pl.BlockSpec((tr, GW), lambda i: (i, ZA // GW))],
                 out_specs=[row, pl.BlockSpec((GW, tr), lambda i: (0, i))],
                 out_shape=[jax.ShapeDtypeStruct((T, GW), MXU_DTYPE), jax.ShapeDtypeStruct((GW, T), MXU_DTYPE)],
                 )(*os, *ls, proj2)


def _combine_bwd(os, ls, proj2, da, dproj):
    T = proj2.shape[0]
    tr = _tile(T, 256)

    def body(o1, o2, o3, l1, l2, l3, z, da_ref, _, d1, d2, d3, e1, e2, e3, dz_ref):
        _, vjp = jax.vjp(_combine, o1[...], o2[...], o3[...], l1[...], l2[...], l3[...], z[...].astype(F32))
        go1, go2, go3, gl1, gl2, gl3, gz = vjp(da_ref[...])
        d1[...], d2[...], d3[...] = go1, go2, go3
        dz_ref[...] = gz.astype(dz_ref.dtype)
        for ref, gl in ((e1, gl1), (e2, gl2), (e3, gl3)):
            for h in range(HPG):
                sl = slice(h * HEAD, (h + 1) * HEAD)
                ref[:, sl] = jnp.broadcast_to(jnp.sum(gl[:, sl], axis=-1, keepdims=True), (tr, HEAD))

    row = pl.BlockSpec((tr, GW), lambda i: (i, 0))
    f = jax.ShapeDtypeStruct((T, GW), F32)
    z_attn = pl.BlockSpec((tr, GW), lambda i: (i, ZA // GW))
    outs = _call(body, name="combine_bwd", grid=(T // tr,), in_specs=[row] * 6 + [z_attn, row, ANY],
                 out_specs=[row] * 6 + [z_attn], out_shape=[f] * 6 + [jax.ShapeDtypeStruct(dproj.shape, dproj.dtype)],
                 aliases={8: 6})(*os, *ls, proj2, da, dproj)
    return outs[:3], outs[3:6], outs[6]


def _shift_down(u, j, t):
    return jnp.where(t >= j, pltpu.roll(u, j, 0), 0.0)


def _shift_up(u, j, t):
    n = u.shape[0]
    return jnp.where(t < n - j, pltpu.roll(u, n - j, 0), 0.0)


def _conv_specs(Bl, S, cw):
    def sec(c0):
        return pl.BlockSpec((1, S, cw), lambda j, b: (b, 0, c0 // cw + j))
    return [sec(CB), sec(CC), sec(CV), sec(ZC)], pl.BlockSpec((3, cw), lambda j, b: (0, j))


def _conv_fwd(proj3, conv_w):
    Bl, S, _ = proj3.shape
    cw = 256
    secs, wspec = _conv_specs(Bl, S, cw)

    def body(b_ref, c_ref, v_ref, z_ref, w_ref, o_ref, ot_ref):
        t = lax.broadcasted_iota(jnp.int32, (S, cw), 0)
        u = c_ref[0].astype(F32) * v_ref[0].astype(F32)
        y = w_ref[0:1, :] * u + w_ref[1:2, :] * _shift_down(u, 1, t) + w_ref[2:3, :] * _shift_down(u, 2, t)
        out = b_ref[0].astype(F32) * y * _silu(z_ref[0].astype(F32))
        o_ref[0] = out.astype(o_ref.dtype)
        ot_ref[...] = out.T.astype(ot_ref.dtype)

    return _call(body, name="conv_fwd", grid=(CONVW // cw, Bl), in_specs=secs + [wspec],
                 out_specs=[pl.BlockSpec((1, S, cw), lambda j, b: (b, 0, j)), pl.BlockSpec((cw, S), lambda j, b: (j, b))],
                 out_shape=[jax.ShapeDtypeStruct((Bl, S, CONVW), MXU_DTYPE),
                            jax.ShapeDtypeStruct((CONVW, Bl * S), MXU_DTYPE)])(proj3, proj3, proj3, proj3, conv_w)


def _conv_bwd(proj3, conv_w, dcc3, dproj):
    Bl, S, _ = proj3.shape
    cw = 256
    secs, wspec = _conv_specs(Bl, S, cw)

    def body(b_ref, c_ref, v_ref, z_ref, w_ref, d_ref, _, dproj_ref, dw_ref, stage, sems):
        t = lax.broadcasted_iota(jnp.int32, (S, cw), 0)
        bv, cv, vv, zv = (r[0].astype(F32) for r in (b_ref, c_ref, v_ref, z_ref))
        dv = d_ref[0]
        u = cv * vv
        u1, u2 = _shift_down(u, 1, t), _shift_down(u, 2, t)
        y = w_ref[0:1, :] * u + w_ref[1:2, :] * u1 + w_ref[2:3, :] * u2
        sg = _sig(zv)
        sz = zv * sg
        gy = dv * bv * sz
        du = w_ref[0:1, :] * gy + w_ref[1:2, :] * _shift_up(gy, 1, t) + w_ref[2:3, :] * _shift_up(gy, 2, t)
        j, b = pl.program_id(0), pl.program_id(1)
        tiles = [dv * y * sz, du * vv, du * cv, dv * bv * y * sg * (1.0 + zv * (1.0 - sg))]
        dsts = [dproj_ref.at[pl.ds(b * S, S), pl.ds(c0 + j * cw, cw)] for c0 in (CB, CC, CV, ZC)]
        _emit_tiles(j * Bl + b, (CONVW // cw) * Bl, tiles, dsts, stage, sems)

        @pl.when(pl.program_id(1) == 0)
        def _():
            dw_ref[...] = jnp.zeros_like(dw_ref)

        dw_ref[0:1, :] += jnp.sum(gy * u, axis=0, keepdims=True)
        dw_ref[1:2, :] += jnp.sum(gy * u1, axis=0, keepdims=True)
        dw_ref[2:3, :] += jnp.sum(gy * u2, axis=0, keepdims=True)

    blk = pl.BlockSpec((1, S, cw), lambda j, b: (b, 0, j))
    return _call(body, name="conv_bwd", grid=(CONVW // cw, Bl), in_specs=secs + [wspec, blk, ANY],
                 out_specs=[ANY, wspec],
                 out_shape=[jax.ShapeDtypeStruct(dproj.shape, dproj.dtype), jax.ShapeDtypeStruct((3, CONVW), F32)],
                 scratch_shapes=_emit_scratch(4, S, cw), aliases={6: 0})(proj3, proj3, proj3, proj3, conv_w, dcc3, dproj)


def _mem_specs(S, tq):
    q = pl.BlockSpec((1, tq, MEMW), lambda b, j: (b, j, MQ // MEMW))
    z = pl.BlockSpec((1, tq, MEMW), lambda b, j: (b, j, ZM // MEMW))
    kv = pl.BlockSpec((1, MEM_HD, 2 * MEMW), lambda b, j: (b, 0, 0))
    vec = pl.BlockSpec((1, MEM_HD), lambda b, j: (0, 0))
    blk = pl.BlockSpec((1, tq, MEMW), lambda b, j: (b, j, 0))
    return q, z, kv, vec, blk


def _mem_fwd(proj3, mkv3, gq, gk):
    Bl, S, _ = proj3.shape
    tq = _tile(S, 512)
    q, z, kv, vec, blk = _mem_specs(S, tq)

    def body(q_ref, z_ref, kv_ref, gq_ref, gk_ref, o_ref, ot_ref):
        out = _mem_block(q_ref[0].astype(F32), z_ref[0].astype(F32), kv_ref[0], gq_ref[...], gk_ref[...])
        o_ref[0] = out.astype(o_ref.dtype)
        ot_ref[...] = out.T.astype(ot_ref.dtype)

    nq = S // tq
    return _call(body, name="mem_fwd", grid=(Bl, nq), in_specs=[q, z, kv, vec, vec],
                 out_specs=[blk, pl.BlockSpec((MEMW, tq), lambda b, j: (0, b * nq + j))],
                 out_shape=[jax.ShapeDtypeStruct((Bl, S, MEMW), MXU_DTYPE),
                            jax.ShapeDtypeStruct((MEMW, Bl * S), MXU_DTYPE)])(proj3, proj3, mkv3, gq, gk)


def _mem_bwd(proj3, mkv3, gq, gk, dmo3, dproj):
    Bl, S, _ = proj3.shape
    tq = _tile(S, 256)
    q, z, kv, vec, blk = _mem_specs(S, tq)
    nq = S // tq

    def body(q_ref, z_ref, kv_ref, gq_ref, gk_ref, d_ref, _, dproj_ref, dkv_ref, dgq_ref, dgk_ref, stage, sems):
        _, vjp = jax.vjp(_mem_block, q_ref[0].astype(F32), z_ref[0].astype(F32), kv_ref[0], gq_ref[...], gk_ref[...])
        dq, dz, dkv, dgq, dgk = vjp(d_ref[0])
        j = pl.program_id(1)
        rows = pl.ds(pl.program_id(0) * S + j * tq, tq)
        dsts = [dproj_ref.at[rows, pl.ds(MQ, MEMW)], dproj_ref.at[rows, pl.ds(ZM, MEMW)]]
        _emit_tiles(pl.program_id(0) * nq + j, Bl * nq, [dq, dz], dsts, stage, sems)

        @pl.when(j == 0)
        def _():
            dkv_ref[0] = jnp.zeros_like(dkv)

        @pl.when((j == 0) & (pl.program_id(0) == 0))
        def _():
            dgq_ref[...] = jnp.zeros_like(dgq_ref)
            dgk_ref[...] = jnp.zeros_like(dgk_ref)

        dkv_ref[0] += dkv
        dgq_ref[...] += dgq
        dgk_ref[...] += dgk

    gshp = jax.ShapeDtypeStruct((1, MEM_HD), F32)
    return _call(body, name="mem_bwd", grid=(Bl, nq), in_specs=[q, z, kv, vec, vec, blk, ANY],
                 out_specs=[ANY, kv, vec, vec],
                 out_shape=[jax.ShapeDtypeStruct(dproj.shape, dproj.dtype), jax.ShapeDtypeStruct(mkv3.shape, F32),
                            gshp, gshp],
                 scratch_shapes=_emit_scratch(2, tq, MEMW), aliases={6: 0})(proj3, proj3, mkv3, gq, gk, dmo3, dproj)


def _merge_specs(T, D, tm, tn):
    def act(w):
        return pl.BlockSpec((tm, w), lambda i, n: (i, 0))

    def wsp(w):
        return pl.BlockSpec((w, tn), lambda i, n: (0, n))

    gates = [pl.BlockSpec((tm, tn), lambda i, n, k=k: (i, (G0 + k * D) // tn + n)) for k in range(3)]
    tile = pl.BlockSpec((tm, tn), lambda i, n: (i, n))
    return act, wsp, gates, tile


def _merge_fwd(a, cc, mo, wa, wc, wm, proj2):
    T, D = a.shape[0], wa.shape[1]
    tm, tn = _tile(T, 1024), _tile(D, 512)
    act, wsp, gates, tile = _merge_specs(T, D, tm, tn)

    def body(a_ref, c_ref, m_ref, wa_ref, wc_ref, wm_ref, g0, g1, g2, mg_ref, mt_ref, pa_ref, pc_ref, pm_ref):
        pa = jnp.dot(a_ref[...], wa_ref[...], preferred_element_type=F32)
        pc = jnp.dot(c_ref[...], wc_ref[...], preferred_element_type=F32)
        pm = jnp.dot(m_ref[...], wm_ref[...], preferred_element_type=F32)
        mg = _sig(g0[...].astype(F32)) * pa + _sig(g1[...].astype(F32)) * pc + _sig(g2[...].astype(F32)) * pm
        mg_ref[...] = mg.astype(mg_ref.dtype)
        mt_ref[...] = mg.T.astype(mt_ref.dtype)
        pa_ref[...] = pa.astype(pa_ref.dtype)
        pc_ref[...] = pc.astype(pc_ref.dtype)
        pm_ref[...] = pm.astype(pm_ref.dtype)

    shp = jax.ShapeDtypeStruct((T, D), MXU_DTYPE)
    return _call(body, name="merge_fwd", grid=(T // tm, D // tn),
                 in_specs=[act(GW), act(CONVW), act(MEMW), wsp(GW), wsp(CONVW), wsp(MEMW)] + gates,
                 out_specs=[tile, pl.BlockSpec((tn, tm), lambda i, n: (n, i)), tile, tile, tile],
                 out_shape=[shp, jax.ShapeDtypeStruct((D, T), MXU_DTYPE), shp, shp, shp],
                 )(a, cc, mo, wa, wc, wm, proj2, proj2, proj2)


def _emit_tiles(step, nsteps, tiles, dsts, stage, sems):
    slot = step % 2

    def copies(s):
        return [pltpu.make_async_copy(stage.at[s, k], dsts[k], sems.at[s, k]) for k in range(len(tiles))]

    @pl.when(step >= 2)
    def _():
        for cp in copies(slot):
            cp.wait()

    for k, t in enumerate(tiles):
        stage[slot, k] = t.astype(stage.dtype)
    for cp in copies(slot):
        cp.start()

    @pl.when(step == nsteps - 1)
    def _():
        for cp in copies(slot):
            cp.wait()
        if nsteps > 1:
            for cp in copies(1 - slot):
                cp.wait()


def _emit_scratch(k, rows, cols):
    return [pltpu.VMEM((2, k, rows, cols), MXU_DTYPE), pltpu.SemaphoreType.DMA((2, k))]


def _merge_bwd(dyb, w_out, proj2, pa, pc, pm):
    T, D = dyb.shape
    IN = proj2.shape[1]
    tm, tn = _tile(T, 1024), _tile(D, 512)
    _, _, gates, tile = _merge_specs(T, D, tm, tn)
    nn = D // tn

    def body(dy_ref, w_ref, g0, g1, g2, p0, p1, p2, dp0, dp1, dp2, dproj_ref, stage, sems):
        i, n = pl.program_id(0), pl.program_id(1)
        dm = lax.dot_general(dy_ref[...], w_ref[...], _DIMS["nt"], preferred_element_type=F32)
        tiles, dsts = [], []
        for k, (g_ref, p_ref, dp_ref) in enumerate(((g0, p0, dp0), (g1, p1, dp1), (g2, p2, dp2))):
            gt = _sig(g_ref[...].astype(F32))
            dp_ref[...] = (gt * dm).astype(dp_ref.dtype)
            tiles.append(dm * p_ref[...].astype(F32) * gt * (1.0 - gt))
            dsts.append(dproj_ref.at[pl.ds(i * tm, tm), pl.ds(G0 + k * D + n * tn, tn)])
        _emit_tiles(i * nn + n, (T // tm) * nn, tiles, dsts, stage, sems)

    shp = jax.ShapeDtypeStruct((T, D), MXU_DTYPE)
    return _call(body, name="merge_bwd", grid=(T // tm, nn),
                 in_specs=[pl.BlockSpec((tm, D), lambda i, n: (i, 0)), pl.BlockSpec((tn, D), lambda i, n: (n, 0))]
                 + gates + [tile] * 3,
                 out_specs=[tile] * 3 + [ANY], out_shape=[shp] * 3 + [jax.ShapeDtypeStruct((T, IN), MXU_DTYPE)],
                 scratch_shapes=_emit_scratch(3, tm, tn))(dyb, w_out, proj2, proj2, proj2, pa, pc, pm)


def _out_loss(merged, w_out, x, tgt):
    T, D = x.shape
    tm = _tile(T, 512)

    def body(m_ref, w_ref, x_ref, t_ref, dy_ref, dyb_ref, loss_ref):
        err = x_ref[...] + jnp.dot(m_ref[...], w_ref[...], preferred_element_type=F32) - t_ref[...]
        dy = err * (1.0 / D)
        dy_ref[...] = dy
        dyb_ref[...] = dy.astype(dyb_ref.dtype)

        @pl.when(pl.program_id(0) == 0)
        def _():
            loss_ref[...] = jnp.zeros_like(loss_ref)

        loss_ref[...] += jnp.sum(err * err) * (0.5 / D)

    row = pl.BlockSpec((tm, D), lambda i: (i, 0))
    return _call(body, name="out_loss", grid=(T // tm,),
                 in_specs=[row, pl.BlockSpec((D, D), lambda i: (0, 0)), row, row],
                 out_specs=[row, row, pl.BlockSpec((1, 128), lambda i: (0, 0))],
                 out_shape=[jax.ShapeDtypeStruct((T, D), F32), jax.ShapeDtypeStruct((T, D), MXU_DTYPE),
                            jax.ShapeDtypeStruct((1, 128), F32)])(merged, w_out, x, tgt)


def _proj_chunk(hb, w, meta, j, nslots, half, buf, name):
    T, D = hb.shape
    Cs = w.shape[1] // 4
    tm, tn = _tile(T, 1024), _tile(Cs // 2, 2176)
    nh = Cs // 2 // tn
    per = nh if half is not None else 2 * nh

    def body(meta_ref, a_ref, b_ref, *rest):
        rest[-1][...] = jnp.dot(a_ref[...], b_ref[...], preferred_element_type=F32).astype(rest[-1].dtype)

    def tile(n, m):
        if half is None:
            return n % per
        return (m[4] if half == 0 else 1 - m[4]) * nh + n % per

    in_specs = [pl.BlockSpec((tm, D), lambda n, i, m: (i, 0)),
                pl.BlockSpec((D, tn), lambda n, i, m: (0, (j + n // per) * 2 * nh + tile(n, m)))]
    args = [meta, hb, w]
    if buf is not None:
        in_specs.append(ANY)
        args.append(buf)
    spec = pltpu.PrefetchScalarGridSpec(
        num_scalar_prefetch=1, grid=(nslots * per, T // tm), in_specs=in_specs,
        out_specs=pl.BlockSpec((tm, tn), lambda n, i, m: (i, m[j + n // per] * 2 * nh + tile(n, m))))
    return _call(body, name=name, grid_spec=spec, out_shape=jax.ShapeDtypeStruct((T, 4 * Cs), PROJ_DTYPE),
                 aliases={} if buf is None else {3: 0})(*args)


def _norms(x, mem, norm_g, mem_norm_g):
    D = x.shape[-1]
    hb, hbt = _rms_fwd(x.reshape(-1, D), norm_g.reshape(1, D), "rms_x")
    mhb, _ = _rms_fwd(mem.reshape(-1, D), mem_norm_g.reshape(1, D), "rms_mem")
    return hb, hbt, mhb


def _attention_fwd(proj2, Bl, gq_all, gk_all):
    T, IN = proj2.shape
    proj3 = proj2.reshape(Bl, T // Bl, IN)
    os, ls = [], []
    for g, d in enumerate(DILATIONS):
        o, l = _attn_fwd(proj3, gq_all[g:g + 1], gk_all[g:g + 1], g, d)
        os.append(o.reshape(T, GW))
        ls.append(l.reshape(T, GW))
    return os, ls, _combine_fwd(os, ls, proj2)


def _conv_branch_fwd(proj2, Bl, conv_w):
    T, IN = proj2.shape
    cc, cct = _conv_fwd(proj2.reshape(Bl, T // Bl, IN), conv_w)
    return cc.reshape(T, CONVW), cct


def _weight_grads(x, mem, tgt, norm_g, mem_norm_g, gq_all, gk_all, conv_w, mem_gq, mem_gk, W, pre, early=None):
    Bl, S, D = x.shape
    T = Bl * S
    hb, hbt, mhb, proj2, os, ls, (a, at), (cc, cct) = pre
    IN = proj2.shape[1]
    proj3 = proj2.reshape(Bl, S, IN)
    x2, tgt2 = x.reshape(T, D), tgt.reshape(T, D)
    mem2 = mem.reshape(-1, D)
    ng, mng = norm_g.reshape(1, D), mem_norm_g.reshape(1, D)
    mgq, mgk = mem_gq.reshape(1, MEM_HD), mem_gk.reshape(1, MEM_HD)
    gqs = [gq_all[g:g + 1] for g in range(NGROUP)]
    gks = [gk_all[g:g + 1] for g in range(NGROUP)]

    mkv = _matmul(mhb, W["mem_w_kv"], "nn", F32, name="mem_kv", tm=512, tn=1024, tk=D)
    mkv3 = mkv.reshape(Bl, -1, 2 * MEMW)
    mo, mot = _mem_fwd(proj3, mkv3, mgq, mgk)
    mo = mo.reshape(T, MEMW)
    merged, mergedt, pa, pc, pm = _merge_fwd(a, cc, mo, W["w_br_attn"], W["w_br_conv"], W["w_br_mem"], proj2)
    dy, dyb, loss = _out_loss(merged, W["w_out"], x2, tgt2)

    G = {}
    G["w_out"] = _matmul(mergedt, dyb, "nn", WIRE_DTYPE, name="dw_out", tm=1024, tn=512, tk=T)
    dpa, dpc, dpm, dproj = _merge_bwd(dyb, W["w_out"], proj2, pa, pc, pm)
    G["w_br_attn"] = _matmul(at, dpa, "nn", WIRE_DTYPE, name="dw_br_attn", tm=512, tn=512, tk=T)
    G["w_br_conv"] = _matmul(cct, dpc, "nn", WIRE_DTYPE, name="dw_br_conv", tm=1024, tn=512, tk=T)
    G["w_br_mem"] = _matmul(mot, dpm, "nn", WIRE_DTYPE, name="dw_br_mem", tm=1024, tn=512, tk=T)
    da = _matmul(dpa, W["w_br_attn"], "nt", F32, name="d_attn", tm=1024, tn=512, tk=D)
    dcc = _matmul(dpc, W["w_br_conv"], "nt", F32, name="d_conv", tm=1024, tn=1024, tk=D)
    dmo = _matmul(dpm, W["w_br_mem"], "nt", F32, name="d_mem", tm=1024, tn=1024, tk=D)
    dproj, dmkv3, dmgq, dmgk = _mem_bwd(proj3, mkv3, mgq, mgk, dmo.reshape(Bl, S, MEMW), dproj)
    dmkv = _cast(dmkv3.reshape(-1, 2 * MEMW), "cast_dmkv")
    G["mem_w_kv"] = _matmul(mhb, dmkv, "tn", WIRE_DTYPE, name="dw_mem_kv", tm=1024, tn=1024, tk=512)
    early_state, da = (None, da) if early is None else early(G, da)
    dmh = _matmul(dmkv, W["mem_w_kv"], "nt", F32, name="d_memh", tm=512, tn=1024, tk=2 * MEMW)
    _, dmng = _rms_bwd(mem2, dmh, mng, None, "rms_mem_bwd")

    dos, dls, dproj = _combine_bwd(os, ls, proj2, da, dproj)
    dgq, dgk = [], []
    for g, d in enumerate(DILATIONS):
        dq, dk, dv, gq_g, gk_g = _attn_bwd(proj3, gqs[g], gks[g], os[g].reshape(Bl, S, GW), ls[g].reshape(Bl, S, GW),
                                           dos[g].reshape(Bl, S, GW), dls[g].reshape(Bl, S, GW), g, d)
        for c0, part in ((Q0, dq), (K0, dk), (V0, dv)):
            dproj = lax.dynamic_update_slice(dproj, part.reshape(T, GW), (0, c0 + g * GW))
        dgq.append(gq_g)
        dgk.append(gk_g)
    dproj, dconv_w = _conv_bwd(proj3, conv_w, dcc.reshape(Bl, S, CONVW), dproj)
    small = [loss, None, dmng] + dgq + dgk + [dconv_w.reshape(1, 3 * CONVW), dmgq, dmgk]
    return G, (dproj, x2, ng, dy, small), early_state


def _dw_in_half(hbt, dproj, pos, own, name):
    D, T = hbt.shape
    IN = dproj.shape[1]
    R, tn = D // 2, _tile(IN, 1024)

    def body(pos_ref, a_ref, b_ref, o_ref):
        o_ref[...] = jnp.dot(a_ref[...], b_ref[...], preferred_element_type=F32).astype(o_ref.dtype)

    spec = pltpu.PrefetchScalarGridSpec(
        num_scalar_prefetch=1, grid=(IN // tn,),
        in_specs=[pl.BlockSpec((R, T), lambda j, p: (p[1] if own else 1 - p[1], 0)),
                  pl.BlockSpec((T, tn), lambda j, p: (0, j))],
        out_specs=pl.BlockSpec((R, tn), lambda j, p: (0, j)))
    return _call(body, name=name, grid_spec=spec, out_shape=jax.ShapeDtypeStruct((R, IN), WIRE_DTYPE))(pos, hbt, dproj)


def _d_h(dproj, w, order):
    T, IN = dproj.shape
    D, Cs = w.shape[0], IN // 4
    tm, tn = _tile(T, 1024), _tile(D, 1024)

    def body(order_ref, a_ref, b_ref, o_ref, acc_ref):
        part = lax.dot_general(a_ref[...], b_ref[...], _DIMS["nt"], preferred_element_type=F32)
        k = pl.program_id(2)

        @pl.when(k == 0)
        def _():
            acc_ref[...] = part

        @pl.when(k > 0)
        def _():
            acc_ref[...] += part

        @pl.when(k == 3)
        def _():
            o_ref[...] = acc_ref[...]

    spec = pltpu.PrefetchScalarGridSpec(
        num_scalar_prefetch=1, grid=(T // tm, D // tn, 4),
        in_specs=[pl.BlockSpec((tm, Cs), lambda i, n, k, o: (i, o[k])), pl.BlockSpec((tn, Cs), lambda i, n, k, o: (n, k))],
        out_specs=pl.BlockSpec((tm, tn), lambda i, n, k, o: (i, n)), scratch_shapes=[pltpu.VMEM((tm, tn), F32)])
    return _call(body, name="d_h", grid_spec=spec, out_shape=jax.ShapeDtypeStruct((T, D), F32))(order, dproj, w)


def _input_grad(rest, w_in, order):
    dproj, x2, ng, dy, small = rest
    dh = _d_h(dproj, w_in, order)
    grad_x, dng = _rms_bwd(x2, dh, ng, dy, "rms_x_bwd")
    small = [dng if t is None else t for t in small]
    return grad_x, jnp.concatenate(small, axis=1)


def _local_step(x, mem, tgt, norm_g, mem_norm_g, gq_all, gk_all, conv_w, mem_gq, mem_gk, W):
    hb, hbt, mhb = _norms(x, mem, norm_g, mem_norm_g)
    Cs = W["w_in"].shape[1] // 4
    shards = (0, 2, 1, 3)
    order = jnp.array(shards, dtype=jnp.int32)
    w_rel = jnp.concatenate([W["w_in"][:, s * Cs:(s + 1) * Cs] for s in shards], axis=1)
    meta = jnp.array(shards + (0,), dtype=jnp.int32)
    proj2 = _proj_chunk(hb, w_rel, meta, 0, 1, None, None, "proj_0")
    for j, nslots in ((1, 2), (3, 1)):
        for half in (1, 0):
            proj2 = _proj_chunk(hb, w_rel, meta, j, nslots, half, proj2, f"proj_{j}_{half}")
    pre = (hb, hbt, mhb, proj2, *_attention_fwd(proj2, x.shape[0], gq_all, gk_all),
           _conv_branch_fwd(proj2, x.shape[0], conv_w))
    G, rest, _ = _weight_grads(x, mem, tgt, norm_g, mem_norm_g, gq_all, gk_all, conv_w, mem_gq, mem_gk, W, pre)
    pos = jnp.zeros((2,), jnp.int32)
    G["w_in"] = jnp.concatenate([_dw_in_half(hbt, rest[0], pos, True, "dw_in_own"),
                                 _dw_in_half(hbt, rest[0], pos, False, "dw_in_sibling")], axis=0)
    grad_x, small = _input_grad(rest, w_rel, order)
    return grad_x.reshape(x.shape), G, small


BIG = (("w_in", "col"), ("mem_w_kv", "row"), ("w_br_attn", "col"), ("w_br_conv", "col"),
       ("w_br_mem", "col"), ("w_out", "row"))


def _coords():
    return lax.axis_index("x"), lax.axis_index("y"), lax.axis_index("c")


def _other_chips(x, y):
    return [(1 - x, y), (x, 1 - y), (1 - x, 1 - y)]


def _half(ref, kind, c):
    R, C = ref.shape
    if kind == "col":
        return ref.at[pl.ds(c * (R // 2), R // 2), :]
    return ref.at[:, pl.ds(c * (C // 2), C // 2)]


def _shard(ref, kind, s):
    R, C = ref.shape
    if kind == "col":
        return ref.at[:, pl.ds(s * (C // 4), C // 4)]
    return ref.at[pl.ds(s * (R // 4), R // 4), :]


def _piece(ref, kind, s, c):
    R, C = ref.shape
    if kind == "col":
        return ref.at[pl.ds(c * (R // 2), R // 2), pl.ds(s * (C // 4), C // 4)]
    return ref.at[pl.ds(s * (R // 4), R // 4), pl.ds(c * (C // 2), C // 2)]


def _remote(src, dst, sems_s, sems_r, k, dev):
    return pltpu.make_async_remote_copy(src_ref=src, dst_ref=dst, send_sem=sems_s.at[k], recv_sem=sems_r.at[k],
                                        device_id=dev, device_id_type=MESH)


HBM = pl.BlockSpec(memory_space=pltpu.HBM)
SEM = pl.BlockSpec(memory_space=pltpu.SEMAPHORE)
EFFECT = pltpu.SideEffectType.DATAFLOW_SIDE_EFFECTING


def _hbm(a):
    return pltpu.with_memory_space_constraint(a, pltpu.HBM)


def _start_copies(name, arrays, ncopies, make):
    n = len(arrays)

    def body(*refs):
        for cp in make(refs[:n], refs[n], refs[n + 1]):
            cp.start()

    outs = pl.pallas_call(
        body, name=name,
        out_shape=(pltpu.SemaphoreType.DMA((ncopies,)), pltpu.SemaphoreType.DMA((ncopies,)),
                   *[jax.ShapeDtypeStruct(t.shape, t.dtype) for t in arrays]),
        in_specs=[HBM] * n, out_specs=(SEM, SEM, *([HBM] * n)),
        input_output_aliases={i: i + 2 for i in range(n)},
        compiler_params=pltpu.CompilerParams(has_side_effects=EFFECT),
    )(*[_hbm(t) for t in arrays])
    return outs[0], outs[1], list(outs[2:])


def _wait_copies(name, send, recv, arrays, make, after):
    n = len(arrays)

    def body(*refs):
        for cp in make(refs[:n], refs[n], refs[n + 1]):
            cp.wait_send()
            cp.wait_recv()

    outs = pl.pallas_call(
        body, name=name, out_shape=[jax.ShapeDtypeStruct(t.shape, t.dtype) for t in arrays],
        in_specs=[HBM] * n + [SEM, SEM, ANY], out_specs=[HBM] * n,
        input_output_aliases={i: i for i in range(n)},
        compiler_params=pltpu.CompilerParams(has_side_effects=EFFECT),
    )(*arrays, send, recv, after)
    return list(outs)


def _w_in_copies(relations):
    def make(refs, send, recv):
        x, y, c = _coords()
        me = 2 * x + y
        chips = _other_chips(x, y)
        w, conv = refs[0], refs[1]
        cps = []
        for i, k in enumerate(relations):
            cps.append(_remote(_column_half(w, 0, c), _column_half(w, 1 + k, c), send, recv, 2 * i, (*chips[k], c)))
            mine = _shard(conv, "col", me)
            cps.append(_remote(mine, mine, send, recv, 2 * i + 1, (*chips[k], c)))
        return cps
    return make


def _column_half(w, slot, c):
    half = w.shape[1] // 8
    return w.at[:, pl.ds((2 * slot + c) * half, half)]


def _w_in_forward(relations):
    def make(refs, send, recv):
        x, y, c = _coords()
        cps = []
        for i, k in enumerate(relations):
            got = _column_half(refs[0], 1 + k, c)
            cps.append(_remote(got, got, send, recv, i, (x, y, 1 - c)))
        return cps
    return make


def _sibling_copy(refs, send, recv):
    x, y, c = _coords()
    return [_remote(refs[0], refs[1], send, recv, 0, (x, y, 1 - c))]


def _other_weight_copies(refs, send, recv):
    x, y, c = _coords()
    me = 2 * x + y
    cps = []
    for k, chip in enumerate(_other_chips(x, y)):
        for p, (_, kind) in enumerate(BIG[1:]):
            mine = _piece(refs[p], kind, me, c)
            cps.append(_remote(mine, mine, send, recv, 3 * p + k, (*chip, c)))
    return cps


def _other_weight_forward(refs, send, recv):
    x, y, c = _coords()
    cps = []
    for k, chip in enumerate(_other_chips(x, y)):
        s = 2 * chip[0] + chip[1]
        for p, (_, kind) in enumerate(BIG[1:]):
            got = _piece(refs[p], kind, s, c)
            cps.append(_remote(got, got, send, recv, 3 * p + k, (x, y, 1 - c)))
    return cps


def _share_copies(group):
    def make(refs, send, recv):
        x, y, c = _coords()
        cps = []
        for p, (_, kind) in enumerate(group):
            mine = _half(refs[p], kind, c)
            cps.append(_remote(mine, mine, send, recv, p, (x, y, 1 - c)))
        return cps
    return make


def _sibling_forward(name, arrays, ncp, halves):
    n = len(arrays)

    def body(*refs):
        outs = refs[n:2 * n]
        send, recv = refs[2 * n:]
        x, y, c = _coords()
        sib = (x, y, 1 - c)
        cps = [_remote(got, got, send, recv, i, sib) for i, got in enumerate(halves(outs, c))]
        for cp in cps:
            cp.start()
        for cp in cps:
            cp.wait_send()
        for i, got in enumerate(halves(outs, 1 - c)):
            _remote(got, got, send, recv, i, sib).wait_recv()

    return pl.pallas_call(
        body, name=name, out_shape=[jax.ShapeDtypeStruct(t.shape, t.dtype) for t in arrays],
        in_specs=[ANY] * n, out_specs=[ANY] * n, input_output_aliases={i: i for i in range(n)},
        scratch_shapes=[pltpu.SemaphoreType.DMA((ncp,)), pltpu.SemaphoreType.DMA((ncp,))],
    )(*arrays)


def _landed_halves(refs, c):
    return [_half(r, "col", c) for r in refs]


def _other_weight_halves(refs, c):
    x, y, _ = _coords()
    out = []
    for chip in _other_chips(x, y):
        s = 2 * chip[0] + chip[1]
        out += [_piece(refs[p], kind, s, c) for p, (_, kind) in enumerate(BIG[1:])]
    return out


def _sibling_exchange(grads, group, name):
    n = len(group)
    shapes = []
    for (_, kind), g in zip(group, grads):
        R, C = g.shape
        shapes.append(jax.ShapeDtypeStruct((R // 2, C) if kind == "col" else (R, C // 2), g.dtype))

    def body(*refs):
        ins, outs = refs[:n], refs[n:2 * n]
        send, recv = refs[2 * n:]
        x, y, c = _coords()
        sib = (x, y, 1 - c)
        cps = [_remote(_half(ins[p], group[p][1], 1 - c), outs[p], send, recv, p, sib) for p in range(n)]
        for cp in cps:
            cp.start()
        for cp in cps:
            cp.wait()

    return pl.pallas_call(
        body, name=name, out_shape=shapes, in_specs=[ANY] * n, out_specs=[ANY] * n,
        scratch_shapes=[pltpu.SemaphoreType.DMA((n,)), pltpu.SemaphoreType.DMA((n,))],
    )(*grads)


def _presum(g, got, kind, pos, name):
    R, C = got.shape
    tr, tc = _tile(R, 512, 16), _tile(C, 2048)
    nr, nc = R // tr, C // tc

    def body(pos_ref, a_ref, b_ref, o_ref):
        o_ref[...] = (a_ref[...].astype(F32) + b_ref[...].astype(F32)).astype(o_ref.dtype)

    blk = pl.BlockSpec((tr, tc), lambda i, j, pos_ref: (i, j))
    if g.shape == got.shape:
        mine = blk
    elif kind == "col":
        mine = pl.BlockSpec((tr, tc), lambda i, j, pos_ref: (pos_ref[1] * nr + i, j))
    else:
        mine = pl.BlockSpec((tr, tc), lambda i, j, pos_ref: (i, pos_ref[1] * nc + j))
    spec = pltpu.PrefetchScalarGridSpec(num_scalar_prefetch=1, grid=(nr, nc), in_specs=[mine, blk], out_specs=blk)
    return _call(body, name=name, grid_spec=spec, out_shape=jax.ShapeDtypeStruct((R, C), WIRE_DTYPE))(pos, g, got)


def _chip_copies(group):
    n = len(group)

    def make(refs, send, recv):
        x, y, c = _coords()
        cps = []
        for k, chip in enumerate(_other_chips(x, y)):
            s = 2 * chip[0] + chip[1]
            for p in range(n):
                cps.append(_remote(_shard(refs[p], group[p][1], s), refs[n + p].at[k], send, recv, 3 * p + k, (*chip, c)))
        return cps
    return make


def _landing_zones(pres, group):
    lands = []
    for (_, kind), g in zip(group, pres):
        R, C = g.shape
        lands.append(lax.empty((3, R, C // 4) if kind == "col" else (3, R // 4, C), g.dtype))
    return lands


def _exchange_start(G, group, pos, carry, tag, pres=None):
    n = len(group)
    if pres is None:
        parts = [G[name] for name, _ in group]
        got = _sibling_exchange(parts, group, "sibling_exchange_" + tag)
        pres = [_presum(parts[p], got[p], kind, pos, "presum_" + name) for p, (name, kind) in enumerate(group)]
    make = _chip_copies(group)
    send, recv, thru = _start_copies("chip_exchange_start_" + tag, [*pres, *_landing_zones(pres, group), carry], 3 * n, make)
    return (send, recv, thru[:2 * n], make, tag), thru[2 * n]


def _exchange_wait(state, after):
    send, recv, arrays, make, tag = state
    thru = _wait_copies("chip_exchange_wait_" + tag, send, recv, arrays, make, after)
    n = len(thru) // 2
    return thru[:n], thru[n:]


def _reduce_into_shard(slots, pre, kind, pos, name):
    K, R, C = slots.shape
    tr, tc = _tile(R, 512, 16), _tile(C, 2176)
    nr, nc = R // tr, C // tc

    def body(pos_ref, s_ref, p_ref, o_ref):
        acc = p_ref[...].astype(F32)
        for k in range(K):
            acc = acc + s_ref[k].astype(F32)
        o_ref[...] = acc

    if kind == "col":
        own = pl.BlockSpec((tr, tc), lambda i, j, pos_ref: (i, pos_ref[0] * nc + j))
        full, out = (2 * R, C), pl.BlockSpec((tr, tc), lambda i, j, pos_ref: (pos_ref[1] * nr + i, j))
    else:
        own = pl.BlockSpec((tr, tc), lambda i, j, pos_ref: (pos_ref[0] * nr + i, j))
        full, out = (R, 2 * C), pl.BlockSpec((tr, tc), lambda i, j, pos_ref: (i, pos_ref[1] * nc + j))
    spec = pltpu.PrefetchScalarGridSpec(
        num_scalar_prefetch=1, grid=(nr, nc),
        in_specs=[pl.BlockSpec((K, tr, tc), lambda i, j, pos_ref: (0, i, j)), own], out_specs=out)
    return _call(body, name=name, grid_spec=spec, out_shape=jax.ShapeDtypeStruct(full, F32))(pos, slots, pre)


def _share_reduced(reds):
    n = len(BIG)

    def body(*refs):
        outs = refs[n:2 * n]
        send, recv = refs[2 * n:]
        x, y, c = _coords()
        sib = (x, y, 1 - c)
        cps = []
        for p in range(n):
            mine = _half(outs[p], BIG[p][1], c)
            cps.append(_remote(mine, mine, send, recv, p, sib))
        for cp in cps:
            cp.start()
        for cp in cps:
            cp.wait_send()
        for p in range(n):
            got = _half(outs[p], BIG[p][1], 1 - c)
            _remote(got, got, send, recv, p, sib).wait_recv()

    return pl.pallas_call(
        body, name="share_reduced", out_shape=[jax.ShapeDtypeStruct(r.shape, r.dtype) for r in reds],
        in_specs=[ANY] * n, out_specs=[ANY] * n, input_output_aliases={p: p for p in range(n)},
        scratch_shapes=[pltpu.SemaphoreType.DMA((n,)), pltpu.SemaphoreType.DMA((n,))],
    )(*reds)


def _gather_small(pack):
    _, N = pack.shape

    def body(in_ref, out_ref, send, recv, loc):
        x, y, c = _coords()
        me = 4 * x + 2 * y + c
        own = pltpu.make_async_copy(in_ref, out_ref.at[me], loc)
        own.start()
        cps = []
        for k in range(1, 8):
            dev = (x ^ (k >> 2), y ^ ((k >> 1) & 1), c ^ (k & 1))
            cps.append(_remote(in_ref, out_ref.at[me], send, recv, k - 1, dev))
        for cp in cps:
            cp.start()
        for k in range(1, 8):
            src = 4 * (x ^ (k >> 2)) + 2 * (y ^ ((k >> 1) & 1)) + (c ^ (k & 1))
            _remote(in_ref, out_ref.at[src], send, recv, k - 1, (x, y, c)).wait_recv()
        for cp in cps:
            cp.wait_send()
        own.wait()

    return pl.pallas_call(
        body, name="gather_small", out_shape=jax.ShapeDtypeStruct((8, 1, N), pack.dtype),
        in_specs=[ANY], out_specs=ANY,
        scratch_shapes=[pltpu.SemaphoreType.DMA((7,)), pltpu.SemaphoreType.DMA((7,)), pltpu.SemaphoreType.DMA(())],
    )(pack)


def _sum_small(slots):
    K, _, N = slots.shape

    def body(s_ref, o_ref):
        acc = s_ref[0]
        for k in range(1, K):
            acc = acc + s_ref[k]
        o_ref[...] = acc

    return _call(body, name="sum_small", in_specs=[pl.BlockSpec(memory_space=pltpu.VMEM)],
                 out_specs=pl.BlockSpec(memory_space=pltpu.VMEM), out_shape=jax.ShapeDtypeStruct((1, N), F32))(slots)


def _adamw(w, g, m, v, name, with_grad=False):
    R, C = w.shape
    tr, tc = _tile(R, 256, 8), _tile(C, 2176)

    def body(w_ref, g_ref, m_ref, v_ref, d_ref, nm_ref, nv_ref, *g_out):
        gv = g_ref[...]
        for ref in g_out:
            ref[...] = gv
        nm = ADAM_B1 * m_ref[...] + (1.0 - ADAM_B1) * gv
        nv = ADAM_B2 * v_ref[...] + (1.0 - ADAM_B2) * gv * gv
        m_hat = nm / (1.0 - ADAM_B1 ** ADAM_STEP)
        v_hat = nv / (1.0 - ADAM_B2 ** ADAM_STEP)
        d_ref[...] = -ADAM_LR * (m_hat / (jnp.sqrt(v_hat) + ADAM_EPS) + ADAM_WD * w_ref[...])
        nm_ref[...] = nm
        nv_ref[...] = nv

    spec = pl.BlockSpec((tr, tc), lambda i, j: (i, j))
    shp = jax.ShapeDtypeStruct((R, C), F32)
    nout = 4 if with_grad else 3
    return _call(body, name=name, grid=(R // tr, C // tc), in_specs=[spec] * 4, out_specs=[spec] * nout,
                 out_shape=[shp] * nout)(w, g, m, v)


SMALL = ("norm_g", "mem_norm_g", "attn_q_norm", "attn_k_norm", "conv_w", "mem_q_norm", "mem_k_norm")
WEIGHTS = ("norm_g", "mem_norm_g", "w_in", "attn_q_norm", "attn_k_norm", "conv_w", "mem_w_kv", "mem_q_norm",
           "mem_k_norm", "w_br_attn", "w_br_conv", "w_br_mem", "w_out")


def kernel(x, mem, norm_g, mem_norm_g, w_in, attn_q_norm, attn_k_norm, conv_w, mem_w_kv, mem_q_norm, mem_k_norm, w_br_attn, w_br_conv, w_br_mem, w_out, loss_target, m_norm_g, m_mem_norm_g, m_w_in, m_attn_q_norm, m_attn_k_norm, m_conv_w, m_mem_w_kv, m_mem_q_norm, m_mem_k_norm, m_w_br_attn, m_w_br_conv, m_w_br_mem, m_w_out, v_norm_g, v_mem_norm_g, v_w_in, v_attn_q_norm, v_attn_k_norm, v_conv_w, v_mem_w_kv, v_mem_q_norm, v_mem_k_norm, v_w_br_attn, v_w_br_conv, v_w_br_mem, v_w_out):
    w = dict(norm_g=norm_g, mem_norm_g=mem_norm_g, w_in=w_in, attn_q_norm=attn_q_norm, attn_k_norm=attn_k_norm,
             conv_w=conv_w, mem_w_kv=mem_w_kv, mem_q_norm=mem_q_norm, mem_k_norm=mem_k_norm, w_br_attn=w_br_attn,
             w_br_conv=w_br_conv, w_br_mem=w_br_mem, w_out=w_out)
    m = dict(norm_g=m_norm_g, mem_norm_g=m_mem_norm_g, w_in=m_w_in, attn_q_norm=m_attn_q_norm,
             attn_k_norm=m_attn_k_norm, conv_w=m_conv_w, mem_w_kv=m_mem_w_kv, mem_q_norm=m_mem_q_norm,
             mem_k_norm=m_mem_k_norm, w_br_attn=m_w_br_attn, w_br_conv=m_w_br_conv, w_br_mem=m_w_br_mem, w_out=m_w_out)
    v = dict(norm_g=v_norm_g, mem_norm_g=v_mem_norm_g, w_in=v_w_in, attn_q_norm=v_attn_q_norm,
             attn_k_norm=v_attn_k_norm, conv_w=v_conv_w, mem_w_kv=v_mem_w_kv, mem_q_norm=v_mem_q_norm,
             mem_k_norm=v_mem_k_norm, w_br_attn=v_w_br_attn, w_br_conv=v_w_br_conv, w_br_mem=v_w_br_mem, w_out=v_w_out)
    Bl, _, D = x.shape
    cx, cy = lax.axis_index("x"), lax.axis_index("y")
    chip = 2 * cx + cy
    pos = jnp.stack([chip, lax.axis_index("c")]).astype(jnp.int32)
    order = jnp.stack([chip] + [2 * a + b for a, b in _other_chips(cx, cy)]).astype(jnp.int32)
    n = len(BIG)

    w_rel = _place_shard(w["w_in"], "col", jnp.zeros((1,), jnp.int32), WIRE_DTYPE, "place_w_in")
    conv_full = _place_shard(conv_w, "col", pos, F32, "place_conv_w")
    others = [_place_shard(w[name], kind, pos, WIRE_DTYPE, "place_" + name) for name, kind in BIG[1:]]
    hb, hbt, mhb = _norms(x, mem, norm_g, mem_norm_g)

    meta = jnp.concatenate([order, pos[1:]])
    near, near_fwd = _w_in_copies((0, 1)), _w_in_forward((0, 1))
    send, recv, (w_rel, conv_full) = _start_copies("gather_near_start", [w_rel, conv_full], 4, near)
    proj = _proj_chunk(hb, w_rel, meta, 0, 1, None, None, "proj_own")
    w_rel, conv_full, *others = _wait_copies("gather_near_wait", send, recv, [w_rel, conv_full, *others], near, proj)

    fsend, frecv, (w_rel,) = _start_copies("gather_near_forward_start", [w_rel], 2, near_fwd)
    far, far_fwd = _w_in_copies((2,)), _w_in_forward((2,))
    send, recv, (w_rel, conv_full) = _start_copies("gather_far_start", [w_rel, conv_full], 2, far)
    proj = _proj_chunk(hb, w_rel, meta, 1, 2, 0, proj, "proj_near_landed")
    w_rel, = _wait_copies("gather_near_forward_wait", fsend, frecv, [w_rel], near_fwd, proj)
    proj = _proj_chunk(hb, w_rel, meta, 1, 2, 1, proj, "proj_near_forwarded")
    w_rel, conv_full = _wait_copies("gather_far_wait", send, recv, [w_rel, conv_full], far, proj)

    fsend, frecv, (w_rel,) = _start_copies("gather_far_forward_start", [w_rel], 1, far_fwd)
    send, recv, (*others, w_rel) = _start_copies("gather_rest_start", [*others, w_rel], 3 * (n - 1), _other_weight_copies)
    proj = _proj_chunk(hb, w_rel, meta, 3, 1, 0, proj, "proj_far_landed")
    w_rel, = _wait_copies("gather_far_forward_wait", fsend, frecv, [w_rel], far_fwd, proj)
    proj = _proj_chunk(hb, w_rel, meta, 3, 1, 1, proj, "proj_far_forwarded")
    os, ls, a = _attention_fwd(proj, Bl, attn_q_norm, attn_k_norm)
    *others, w_rel = _wait_copies("gather_rest_wait", send, recv, [*others, w_rel], _other_weight_copies, a[0])
    fsend, frecv, (*others, proj) = _start_copies("gather_rest_forward_start", [*others, proj], 3 * (n - 1),
                                                  _other_weight_forward)
    cc = _conv_branch_fwd(proj, Bl, conv_full)
    others = _wait_copies("gather_rest_forward_wait", fsend, frecv, others, _other_weight_forward, cc[0])
    W = {name: others[p] for p, (name, _) in enumerate(BIG[1:])}

    G, rest, rest_state = _weight_grads(
        x, mem, loss_target, norm_g, mem_norm_g, attn_q_norm, attn_k_norm, conv_full, mem_q_norm, mem_k_norm, W,
        (hb, hbt, mhb, proj, os, ls, a, cc), early=lambda G, carry: _exchange_start(G, BIG[1:], pos, carry, "rest"))

    for_sibling = _dw_in_half(hbt, rest[0], pos, False, "dw_in_sibling")
    send, recv, (for_sibling, got, dproj) = _start_copies(
        "sibling_w_in_start", [for_sibling, lax.empty(for_sibling.shape, for_sibling.dtype), rest[0]], 1, _sibling_copy)
    mine = _dw_in_half(hbt, dproj, pos, True, "dw_in_own")
    for_sibling, got = _wait_copies("sibling_w_in_wait", send, recv, [for_sibling, got], _sibling_copy, mine)
    pre_w_in = _presum(mine, got, "col", pos, "presum_w_in")

    w_in_state, dproj = _exchange_start(G, BIG[:1], pos, dproj, "w_in", pres=[pre_w_in])
    grad_x, small = _input_grad((dproj, *rest[1:]), w_rel, order)
    pres_rest, slots_rest = _exchange_wait(rest_state, grad_x)
    reds_rest = [_reduce_into_shard(slots_rest[p], pres_rest[p], kind, pos, "reduce_" + name)
                 for p, (name, kind) in enumerate(BIG[1:])]
    share_rest = _share_copies(BIG[1:])
    rsend, rrecv, reds_rest = _start_copies("share_rest_start", reds_rest, n - 1, share_rest)
    pres, slots = _exchange_wait(w_in_state, grad_x)
    red_w_in = _reduce_into_shard(slots[0], pres[0], "col", pos, "reduce_w_in")
    share_w_in = _share_copies(BIG[:1])
    wsend, wrecv, (red_w_in, small) = _start_copies("share_w_in_start", [red_w_in, small], 1, share_w_in)
    grad_x = grad_x.reshape(x.shape)

    tot = _sum_small(_gather_small(small))
    reds_rest = _wait_copies("share_rest_wait", rsend, rrecv, reds_rest, share_rest, tot)
    grads = dict(zip([name for name, _ in BIG[1:]], reds_rest))
    tot = tot[0]
    loss = tot[0]
    off = 128
    for name, size in (("norm_g", D), ("mem_norm_g", D), ("attn_q_norm", NGROUP * HEAD), ("attn_k_norm", NGROUP * HEAD),
                       ("conv_w", 3 * CONVW), ("mem_q_norm", MEM_HD), ("mem_k_norm", MEM_HD)):
        grads[name] = tot[off:off + size]
        off += size
    cw = conv_w.shape[1]
    grads["conv_w"] = lax.dynamic_slice(grads["conv_w"].reshape(3, CONVW), (0, chip * cw), (3, cw))
    for name in SMALL:
        grads[name] = grads[name].reshape(w[name].shape)

    delta, new_m, new_v = {}, {}, {}
    for name, _ in BIG[1:]:
        delta[name], new_m[name], new_v[name], grads[name] = _adamw(w[name], grads[name], m[name], v[name],
                                                                    "adamw_" + name, with_grad=True)

    def packed(t):
        return jnp.concatenate([t[name].reshape(1, -1) for name in SMALL], axis=1)

    ds, ms, vs = _adamw(packed(w), packed(grads), packed(m), packed(v), "adamw_small")
    shared, = _wait_copies("share_w_in_wait", wsend, wrecv, [red_w_in], share_w_in, ds)
    delta["w_in"], new_m["w_in"], new_v["w_in"], grads["w_in"] = _adamw(w["w_in"], shared, m["w_in"], v["w_in"],
                                                                        "adamw_w_in", with_grad=True)
    off = 0
    for name in SMALL:
        size = w[name].size
        delta[name] = ds[0, off:off + size].reshape(w[name].shape)
        new_m[name] = ms[0, off:off + size].reshape(w[name].shape)
        new_v[name] = vs[0, off:off + size].reshape(w[name].shape)
        off += size

    return (loss, grad_x, *[grads[n] for n in WEIGHTS], *[delta[n] for n in WEIGHTS],
            *[new_m[n] for n in WEIGHTS], *[new_v[n] for n in WEIGHTS])
```

```python
import jax
import jax.numpy as jnp
from jax import lax
from jax.experimental import pallas as pl
from jax.experimental.pallas import tpu as pltpu

F32 = jnp.float32
MXU_DTYPE = jnp.bfloat16
WIRE_DTYPE = jnp.bfloat16
PROJ_DTYPE = jnp.bfloat16
EPS = 1e-6
NEG = -1e30

HEAD = 128
HPG = 4
GW = HPG * HEAD
DILATIONS = (1, 4, 16)
NGROUP = len(DILATIONS)
BLK = 128
QKV = NGROUP * GW
CONVW = 1024
MEM_HEADS = 4
MEM_HD = 256
MEMW = MEM_HEADS * MEM_HD
Q0, K0, V0 = 0, QKV, 2 * QKV
ZA = 3 * QKV
CB, CC, CV, ZC = ZA + GW, ZA + GW + CONVW, ZA + GW + 2 * CONVW, ZA + GW + 3 * CONVW
MQ = ZC + CONVW
ZM = MQ + MEMW
G0 = ZM + MEMW

ADAM_LR, ADAM_B1, ADAM_B2, ADAM_EPS, ADAM_WD, ADAM_STEP = 0.001, 0.9, 0.999, 1e-08, 0.01, 10

VMEM_LIMIT = 56 * 1024 * 1024
MESH = pl.DeviceIdType.MESH
ANY = pl.BlockSpec(memory_space=pl.ANY)


def _tile(n, pref, mult=128):
    t = min(pref, n)
    while t > mult and (n % t or t % mult):
        t -= mult
    assert n % t == 0, (n, pref)
    return t


def _call(body, *, name, out_shape, grid=(), in_specs=None, out_specs=None, scratch_shapes=(),
          aliases=None, grid_spec=None):
    kw = {}
    if grid_spec is not None:
        kw["grid_spec"] = grid_spec
        ngrid = len(grid_spec.grid)
    else:
        kw.update(grid=grid, in_specs=in_specs, out_specs=out_specs, scratch_shapes=list(scratch_shapes))
        ngrid = len(grid)
    params = pltpu.CompilerParams(dimension_semantics=("arbitrary",) * ngrid, vmem_limit_bytes=VMEM_LIMIT)
    return pl.pallas_call(body, name=name, out_shape=out_shape, compiler_params=params,
                          input_output_aliases=aliases or {}, **kw)


_DIMS = {"nn": (((1,), (0,)), ((), ())), "nt": (((1,), (1,)), ((), ())), "tn": (((0,), (0,)), ((), ()))}


def _mxu(a, b, mode):
    return lax.dot_general(a.astype(MXU_DTYPE), b.astype(MXU_DTYPE), _DIMS[mode], preferred_element_type=F32)


def _sig(z):
    return 1.0 / (1.0 + jnp.exp(-z))


def _silu(z):
    return z * _sig(z)


def _rms_rows(t, g):
    return t * lax.rsqrt(jnp.mean(t * t, axis=-1, keepdims=True) + EPS) * g


def _attn_block(q, k2, v2, gq, gk, first):
    qn = _rms_rows(q, gq)
    kn = _rms_rows(k2, gk)
    s = jnp.where(_band_mask(first, k2.shape[0]), _mxu(qn, kn, "nt") * (HEAD ** -0.5), NEG)
    m = jnp.max(s, axis=-1, keepdims=True)
    p = jnp.exp(s - m)
    den = jnp.sum(p, axis=-1, keepdims=True)
    o = _mxu(p, v2, "nn") / den
    return o, m + jnp.log(den)


def _band_mask(first, nkeys):
    a = lax.broadcasted_iota(jnp.int32, (BLK, nkeys), 0)
    b = lax.broadcasted_iota(jnp.int32, (BLK, nkeys), 1)
    if nkeys == BLK:
        return b <= a
    return (b >= a) & (b <= a + BLK) & (b >= jnp.where(first, BLK, 0))


def _norm_parts(t):
    r = lax.rsqrt(jnp.mean(t * t, axis=-1, keepdims=True) + EPS)
    return r, t * r


def _norm_bwd(dn, g, r, th):
    dth = dn * g
    return r * (dth - th * jnp.mean(dth * th, axis=-1, keepdims=True)), jnp.sum(dn * th, axis=0, keepdims=True)


def _attn_block_bwd(q, k2, v2, gq, gk, first, do, o, lse, dlse):
    scale = HEAD ** -0.5
    rq, qh = _norm_parts(q)
    rk, kh = _norm_parts(k2)
    qn, kn = qh * gq, kh * gk
    s = jnp.where(_band_mask(first, k2.shape[0]), _mxu(qn, kn, "nt") * scale, NEG)
    p = jnp.exp(s - lse)
    ds = p * (_mxu(do, v2, "nt") + (dlse - jnp.sum(do * o, axis=-1, keepdims=True))) * scale
    dq, dgq = _norm_bwd(_mxu(ds, kn, "nn"), gq, rq, qh)
    dk2, dgk = _norm_bwd(_mxu(ds, qn, "tn"), gk, rk, kh)
    return dq, dk2, _mxu(p, do, "tn"), dgq, dgk


def _combine(o1, o2, o3, l1, l2, l3, z):
    m = lax.stop_gradient(jnp.maximum(jnp.maximum(l1, l2), l3))
    e1, e2, e3 = jnp.exp(l1 - m), jnp.exp(l2 - m), jnp.exp(l3 - m)
    return (e1 * o1 + e2 * o2 + e3 * o3) / (e1 + e2 + e3) * _silu(z)


def _mem_block(q, z, kv, gq, gk):
    outs = []
    for h in range(MEM_HEADS):
        sl = slice(h * MEM_HD, (h + 1) * MEM_HD)
        qn = _rms_rows(q[:, sl], gq)
        kn = _rms_rows(kv[:, sl], gk)
        s = _mxu(qn, kn, "nt") * (MEM_HD ** -0.5)
        p = jnp.exp(s - jnp.max(s, axis=-1, keepdims=True))
        den = jnp.sum(p, axis=-1, keepdims=True)
        outs.append(_mxu(p, kv[:, MEMW + h * MEM_HD:MEMW + (h + 1) * MEM_HD], "nn") / den)
    return jnp.concatenate(outs, axis=-1) * _silu(z)


def _mem_block_bwd(q, z, kv, gq, gk, d):
    scale = MEM_HD ** -0.5
    dqs, dzs, dks, dvs = [], [], [], []
    dgq, dgk = jnp.zeros((1, MEM_HD), F32), jnp.zeros((1, MEM_HD), F32)
    for h in range(MEM_HEADS):
        sl = slice(h * MEM_HD, (h + 1) * MEM_HD)
        v = kv[:, MEMW + h * MEM_HD:MEMW + (h + 1) * MEM_HD]
        rq, qh = _norm_parts(q[:, sl])
        rk, kh = _norm_parts(kv[:, sl])
        qn, kn = qh * gq, kh * gk
        s = _mxu(qn, kn, "nt") * scale
        e = jnp.exp(s - jnp.max(s, axis=-1, keepdims=True))
        p = e / jnp.sum(e, axis=-1, keepdims=True)
        mo = _mxu(p, v, "nn")
        zh, dh = z[:, sl], d[:, sl]
        sg = _sig(zh)
        dmo = dh * zh * sg
        dzs.append(dh * mo * sg * (1.0 + zh * (1.0 - sg)))
        ds = p * (_mxu(dmo, v, "nt") - jnp.sum(dmo * mo, axis=-1, keepdims=True)) * scale
        dq, a = _norm_bwd(_mxu(ds, kn, "nn"), gq, rq, qh)
        dk, b = _norm_bwd(_mxu(ds, qn, "tn"), gk, rk, kh)
        dqs.append(dq)
        dks.append(dk)
        dvs.append(_mxu(p, dmo, "tn"))
        dgq, dgk = dgq + a, dgk + b
    return (jnp.concatenate(dqs, axis=-1), jnp.concatenate(dzs, axis=-1), jnp.concatenate(dks + dvs, axis=-1),
            dgq, dgk)


def _cast(w, name):
    R, C = w.shape
    tr, tc = _tile(R, 512, 8), _tile(C, 2176)

    def body(w_ref, o_ref):
        o_ref[...] = w_ref[...].astype(o_ref.dtype)

    spec = pl.BlockSpec((tr, tc), lambda i, j: (i, j))
    return _call(body, name=name, grid=(R // tr, C // tc), in_specs=[spec], out_specs=spec,
                 out_shape=jax.ShapeDtypeStruct((R, C), WIRE_DTYPE))(w)


def _place_shard(w, kind, pos, dtype, name, slot=0, into=None):
    R, C = w.shape
    tr, tc = _tile(R, 512, 8), _tile(C, 2176)
    nr, nc = R // tr, C // tc

    def body(pos_ref, w_ref, *rest):
        rest[-1][...] = w_ref[...].astype(rest[-1].dtype)

    if kind == "col":
        full, out = (R, 4 * C), pl.BlockSpec((tr, tc), lambda i, j, pos_ref: (i, pos_ref[slot] * nc + j))
    else:
        full, out = (4 * R, C), pl.BlockSpec((tr, tc), lambda i, j, pos_ref: (pos_ref[slot] * nr + i, j))
    in_specs, args = [pl.BlockSpec((tr, tc), lambda i, j, pos_ref: (i, j))], [pos, w]
    if into is not None:
        in_specs.append(ANY)
        args.append(into)
    spec = pltpu.PrefetchScalarGridSpec(num_scalar_prefetch=1, grid=(nr, nc), in_specs=in_specs, out_specs=out)
    return _call(body, name=name, grid_spec=spec, out_shape=jax.ShapeDtypeStruct(full, dtype),
                 aliases={} if into is None else {2: 0})(*args)


def _matmul(a, b, mode, out_dtype, *, name, tm=512, tn=512, tk=512):
    if mode == "nn":
        (M, K), (_, N) = a.shape, b.shape
    elif mode == "nt":
        (M, K), (N, _) = a.shape, b.shape
    else:
        (K, M), (_, N) = a.shape, b.shape
    tm, tn, tk = _tile(M, tm), _tile(N, tn), _tile(K, tk)
    nk = K // tk

    def body(a_ref, b_ref, o_ref, *acc):
        part = lax.dot_general(a_ref[...], b_ref[...], _DIMS[mode], preferred_element_type=F32)
        if nk == 1:
            o_ref[...] = part.astype(o_ref.dtype)
            return
        acc_ref, = acc
        k = pl.program_id(2)

        @pl.when(k == 0)
        def _():
            acc_ref[...] = part

        @pl.when(k > 0)
        def _():
            acc_ref[...] += part

        @pl.when(k == nk - 1)
        def _():
            o_ref[...] = acc_ref[...].astype(o_ref.dtype)

    a_spec = pl.BlockSpec((tk, tm), lambda i, j, k: (k, i)) if mode == "tn" else pl.BlockSpec((tm, tk), lambda i, j, k: (i, k))
    b_spec = pl.BlockSpec((tn, tk), lambda i, j, k: (j, k)) if mode == "nt" else pl.BlockSpec((tk, tn), lambda i, j, k: (k, j))
    return _call(body, name=name, grid=(M // tm, N // tn, nk), in_specs=[a_spec, b_spec],
                 out_specs=pl.BlockSpec((tm, tn), lambda i, j, k: (i, j)),
                 out_shape=jax.ShapeDtypeStruct((M, N), out_dtype),
                 scratch_shapes=[] if nk == 1 else [pltpu.VMEM((tm, tn), F32)])(a, b)


def _rms_fwd(x, g, name):
    R, D = x.shape
    tr = _tile(R, 512)

    def body(x_ref, g_ref, o_ref, t_ref):
        y = _rms_rows(x_ref[...], g_ref[...])
        o_ref[...] = y.astype(o_ref.dtype)
        t_ref[...] = y.T.astype(t_ref.dtype)

    row = pl.BlockSpec((tr, D), lambda i: (i, 0))
    return _call(body, name=name, grid=(R // tr,), in_specs=[row, pl.BlockSpec((1, D), lambda i: (0, 0))],
                 out_specs=[row, pl.BlockSpec((D, tr), lambda i: (0, i))],
                 out_shape=[jax.ShapeDtypeStruct((R, D), MXU_DTYPE), jax.ShapeDtypeStruct((D, R), MXU_DTYPE)])(x, g)


def _rms_bwd(x, dh, g, dy, name):
    R, D = x.shape
    tr = _tile(R, 256)
    with_dx = dy is not None

    def body(*refs):
        if with_dx:
            x_ref, dh_ref, g_ref, dy_ref, dx_ref, dg_ref = refs
        else:
            x_ref, dh_ref, g_ref, dg_ref = refs
        xv, dhv = x_ref[...], dh_ref[...]
        r = lax.rsqrt(jnp.mean(xv * xv, axis=-1, keepdims=True) + EPS)
        xh = xv * r

        @pl.when(pl.program_id(0) == 0)
        def _():
            dg_ref[...] = jnp.zeros_like(dg_ref)

        dg_ref[...] += jnp.sum(dhv * xh, axis=0, keepdims=True)
        if with_dx:
            dxh = dhv * g_ref[...]
            dx_ref[...] = dy_ref[...] + r * (dxh - xh * jnp.mean(dxh * xh, axis=-1, keepdims=True))

    row = pl.BlockSpec((tr, D), lambda i: (i, 0))
    vec = pl.BlockSpec((1, D), lambda i: (0, 0))
    dg_shape = jax.ShapeDtypeStruct((1, D), F32)
    if with_dx:
        return _call(body, name=name, grid=(R // tr,), in_specs=[row, row, vec, row], out_specs=[row, vec],
                     out_shape=[jax.ShapeDtypeStruct((R, D), F32), dg_shape])(x, dh, g, dy)
    return None, _call(body, name=name, grid=(R // tr,), in_specs=[row, row, vec], out_specs=vec,
                       out_shape=dg_shape)(x, dh, g)


def _attn_geom(g, d):
    hc = HPG if d == 1 else 1
    cw = hc * HEAD
    cq, ck, cv = (Q0 + g * GW) // cw, (K0 + g * GW) // cw, (V0 + g * GW) // cw
    return (1, BLK * d, cw), hc, HPG // hc, cq, ck, cv


def _rows(ref, r, d, sl):
    if d == 1:
        return ref[0, :, sl]
    return ref.at[0][pl.ds(r, BLK, stride=d), sl]


def _set_rows(ref, r, d, sl, val):
    if d == 1:
        ref[0, :, sl] = val
    else:
        ref.at[0][pl.ds(r, BLK, stride=d), sl] = val


def _stage_rows(ref, r, d, sl, val):
    if d == 1:
        ref[:, sl] = val
    else:
        ref[pl.ds(r, BLK, stride=d), sl] = val


def _proj_stages(blk, d):
    return [] if d == 1 else [pltpu.VMEM(blk[1:], F32)] * 5


def _proj_rows(refs, stages, d):
    if d == 1:
        return [lambda r, sl, ref=ref: ref[0, :, sl].astype(F32) for ref in refs]
    for ref, stage in zip(refs, stages):
        stage[...] = ref[0].astype(F32)
    return [lambda r, sl, stage=stage: stage[pl.ds(r, BLK, stride=d), sl] for stage in stages]


def _attn_fwd(proj3, gq, gk, g, d):
    Bl, S, _ = proj3.shape
    blk, hc, ncb, cq, ck, cv = _attn_geom(g, d)
    nb = S // blk[1]
    if nb == 1:
        return _attn_single_fwd(proj3, gq, gk, g, d)

    def body(q_ref, kp_ref, kc_ref, vp_ref, vc_ref, gq_ref, gk_ref, o_ref, lse_ref, *stages):
        first = pl.program_id(2) == 0
        q, kp, kc, vp, vc = _proj_rows((q_ref, kp_ref, kc_ref, vp_ref, vc_ref), stages, d)
        for r in range(d):
            for h in range(hc):
                sl = slice(h * HEAD, (h + 1) * HEAD)
                k2 = jnp.concatenate([kp(r, sl), kc(r, sl)], axis=0)
                v2 = jnp.concatenate([vp(r, sl), vc(r, sl)], axis=0)
                o, lse = _attn_block(q(r, sl), k2, v2, gq_ref[...], gk_ref[...], first)
                _set_rows(o_ref, r, d, sl, o)
                _set_rows(lse_ref, r, d, sl, jnp.broadcast_to(lse, (BLK, HEAD)))

    def cur(c0):
        return pl.BlockSpec(blk, lambda b, j, i: (b, i, c0 + j))

    def prev(c0):
        return pl.BlockSpec(blk, lambda b, j, i: (b, jnp.maximum(i - 1, 0), c0 + j))

    vec = pl.BlockSpec((1, HEAD), lambda b, j, i: (0, 0))
    out = pl.BlockSpec(blk, lambda b, j, i: (b, i, j))
    shp = jax.ShapeDtypeStruct((Bl, S, GW), F32)
    return _call(body, name=f"attn_fwd_g{g}", grid=(Bl, ncb, nb),
                 in_specs=[cur(cq), prev(ck), cur(ck), prev(cv), cur(cv), vec, vec],
                 out_specs=[out, out], out_shape=[shp, shp], scratch_shapes=_proj_stages(blk, d),
                 )(proj3, proj3, proj3, proj3, proj3, gq, gk)


def _attn_single_fwd(proj3, gq, gk, g, d):
    Bl, S, _ = proj3.shape
    blk, hc, ncb, cq, ck, cv = _attn_geom(g, d)

    def body(q_ref, k_ref, v_ref, gq_ref, gk_ref, o_ref, lse_ref, *stages):
        q, k, v = _proj_rows((q_ref, k_ref, v_ref), stages, d)
        for r in range(d):
            for h in range(hc):
                sl = slice(h * HEAD, (h + 1) * HEAD)
                o, lse = _attn_block(q(r, sl), k(r, sl), v(r, sl), gq_ref[...], gk_ref[...], True)
                _set_rows(o_ref, r, d, sl, o)
                _set_rows(lse_ref, r, d, sl, jnp.broadcast_to(lse, (BLK, HEAD)))

    def at(c0):
        return pl.BlockSpec(blk, lambda b, j: (b, 0, c0 + j))

    vec = pl.BlockSpec((1, HEAD), lambda b, j: (0, 0))
    shp = jax.ShapeDtypeStruct((Bl, S, GW), F32)
    return _call(body, name=f"attn_fwd_g{g}", grid=(Bl, ncb), in_specs=[at(cq), at(ck), at(cv), vec, vec],
                 out_specs=[at(0), at(0)], out_shape=[shp, shp], scratch_shapes=_proj_stages(blk, d)[:3],
                 )(proj3, proj3, proj3, gq, gk)


def _attn_single_bwd(proj3, gq, gk, o3, l3, do3, dl3, g, d):
    Bl, S, _ = proj3.shape
    blk, hc, ncb, cq, ck, cv = _attn_geom(g, d)

    def body(q_ref, k_ref, v_ref, gq_ref, gk_ref, o_ref, l_ref, do_ref, dl_ref,
             dq_ref, dk_ref, dv_ref, dgq_ref, dgk_ref, sq_ref, sk_ref, sv_ref, *stages):
        @pl.when((pl.program_id(0) == 0) & (pl.program_id(1) == 0))
        def _():
            dgq_ref[...] = jnp.zeros_like(dgq_ref)
            dgk_ref[...] = jnp.zeros_like(dgk_ref)

        dgq, dgk = jnp.zeros((1, HEAD), F32), jnp.zeros((1, HEAD), F32)
        q, k, v = _proj_rows((q_ref, k_ref, v_ref), stages, d)
        for r in range(d):
            for h in range(hc):
                sl = slice(h * HEAD, (h + 1) * HEAD)
                dq, dk, dv, a, b = _attn_block_bwd(
                    q(r, sl), k(r, sl), v(r, sl), gq_ref[...], gk_ref[...], True, _rows(do_ref, r, d, sl),
                    _rows(o_ref, r, d, sl), _rows(l_ref, r, d, sl)[:, :1], _rows(dl_ref, r, d, sl)[:, :1])
                _stage_rows(sq_ref, r, d, sl, dq)
                _stage_rows(sk_ref, r, d, sl, dk)
                _stage_rows(sv_ref, r, d, sl, dv)
                dgq, dgk = dgq + a, dgk + b
        dgq_ref[...] += dgq
        dgk_ref[...] += dgk
        dq_ref[0] = sq_ref[...].astype(dq_ref.dtype)
        dk_ref[0] = sk_ref[...].astype(dk_ref.dtype)
        dv_ref[0] = sv_ref[...].astype(dv_ref.dtype)

    def at(c0):
        return pl.BlockSpec(blk, lambda b, j: (b, 0, c0 + j))

    vec = pl.BlockSpec((1, HEAD), lambda b, j: (0, 0))
    shp = jax.ShapeDtypeStruct((Bl, S, GW), MXU_DTYPE)
    gshp = jax.ShapeDtypeStruct((1, HEAD), F32)
    return _call(body, name=f"attn_bwd_g{g}", grid=(Bl, ncb),
                 in_specs=[at(cq), at(ck), at(cv), vec, vec, at(0), at(0), at(0), at(0)],
                 out_specs=[at(0), at(0), at(0), vec, vec], out_shape=[shp, shp, shp, gshp, gshp],
                 scratch_shapes=[pltpu.VMEM(blk[1:], F32)] * 3 + _proj_stages(blk, d)[:3],
                 )(proj3, proj3, proj3, gq, gk, o3, l3, do3, dl3)


def _attn_bwd(proj3, gq, gk, o3, l3, do3, dl3, g, d):
    Bl, S, _ = proj3.shape
    blk, hc, ncb, cq, ck, cv = _attn_geom(g, d)
    nb = S // blk[1]
    if nb == 1:
        return _attn_single_bwd(proj3, gq, gk, o3, l3, do3, dl3, g, d)

    def body(q_ref, kp_ref, kc_ref, vp_ref, vc_ref, gq_ref, gk_ref, o_ref, l_ref, do_ref, dl_ref,
             dq_ref, dk_ref, dv_ref, dgq_ref, dgk_ref, ck_ref, cv_ref, sq_ref, sk_ref, sv_ref, *stages):
        i = pl.program_id(2)
        first = i == 0

        @pl.when((pl.program_id(0) == 0) & (pl.program_id(1) == 0) & first)
        def _():
            dgq_ref[...] = jnp.zeros_like(dgq_ref)
            dgk_ref[...] = jnp.zeros_like(dgk_ref)

        @pl.when(first)
        def _():
            ck_ref[...] = jnp.zeros_like(ck_ref)
            cv_ref[...] = jnp.zeros_like(cv_ref)

        @pl.when(i < nb)
        def _():
            dgq, dgk = jnp.zeros((1, HEAD), F32), jnp.zeros((1, HEAD), F32)
            q, kp, kc, vp, vc = _proj_rows((q_ref, kp_ref, kc_ref, vp_ref, vc_ref), stages, d)
            for r in range(d):
                rs = slice(r * BLK, (r + 1) * BLK)
                for h in range(hc):
                    sl = slice(h * HEAD, (h + 1) * HEAD)
                    k2 = jnp.concatenate([kp(r, sl), kc(r, sl)], axis=0)
                    v2 = jnp.concatenate([vp(r, sl), vc(r, sl)], axis=0)
                    dq, dk2, dv2, a, b = _attn_block_bwd(
                        q(r, sl), k2, v2, gq_ref[...], gk_ref[...], first, _rows(do_ref, r, d, sl),
                        _rows(o_ref, r, d, sl), _rows(l_ref, r, d, sl)[:, :1], _rows(dl_ref, r, d, sl)[:, :1])
                    _stage_rows(sq_ref, r, d, sl, dq)
                    _stage_rows(sk_ref, r, d, sl, ck_ref[rs, sl] + dk2[:BLK])
                    _stage_rows(sv_ref, r, d, sl, cv_ref[rs, sl] + dv2[:BLK])
                    ck_ref[rs, sl] = dk2[BLK:]
                    cv_ref[rs, sl] = dv2[BLK:]
                    dgq, dgk = dgq + a, dgk + b
            dgq_ref[...] += dgq
            dgk_ref[...] += dgk
            dq_ref[0] = sq_ref[...].astype(dq_ref.dtype)

        @pl.when(i == nb)
        def _():
            for r in range(d):
                rs = slice(r * BLK, (r + 1) * BLK)
                _stage_rows(sk_ref, r, d, slice(None), ck_ref[rs, :])
                _stage_rows(sv_ref, r, d, slice(None), cv_ref[rs, :])

        dk_ref[0] = sk_ref[...].astype(dk_ref.dtype)
        dv_ref[0] = sv_ref[...].astype(dv_ref.dtype)

    def cur(c0):
        return pl.BlockSpec(blk, lambda b, j, i: (b, jnp.minimum(i, nb - 1), c0 + j))

    def prev(c0):
        return pl.BlockSpec(blk, lambda b, j, i: (b, jnp.clip(i - 1, 0, nb - 1), c0 + j))

    vec = pl.BlockSpec((1, HEAD), lambda b, j, i: (0, 0))
    at_q = pl.BlockSpec(blk, lambda b, j, i: (b, jnp.minimum(i, nb - 1), j))
    at_k = pl.BlockSpec(blk, lambda b, j, i: (b, jnp.maximum(i - 1, 0), j))
    shp = jax.ShapeDtypeStruct((Bl, S, GW), MXU_DTYPE)
    gshp = jax.ShapeDtypeStruct((1, HEAD), F32)
    return _call(body, name=f"attn_bwd_g{g}", grid=(Bl, ncb, nb + 1),
                 in_specs=[cur(cq), prev(ck), cur(ck), prev(cv), cur(cv), vec, vec, at_q, at_q, at_q, at_q],
                 out_specs=[at_q, at_k, at_k, vec, vec], out_shape=[shp, shp, shp, gshp, gshp],
                 scratch_shapes=[pltpu.VMEM(blk[1:], F32)] * 5 + _proj_stages(blk, d),
                 )(proj3, proj3, proj3, proj3, proj3, gq, gk, o3, l3, do3, dl3)


def _combine_fwd(os, ls, proj2):
    T = proj2.shape[0]
    tr = _tile(T, 512)

    def body(o1, o2, o3, l1, l2, l3, z, a_ref, at_ref):
        a = _combine(o1[...], o2[...], o3[...], l1[...], l2[...], l3[...], z[...].astype(F32))
        a_ref[...] = a.astype(a_ref.dtype)
        at_ref[...] = a.T.astype(at_ref.dtype)

    row = pl.BlockSpec((tr, GW), lambda i: (i, 0))
    return _call(body, name="combine_fwd", grid=(T // tr,),
                 in_specs=[row] * 6 + [pl.BlockSpec((tr, GW), lambda i: (i, ZA // GW))],
                 out_specs=[row, pl.BlockSpec((GW, tr), lambda i: (0, i))],
                 out_shape=[jax.ShapeDtypeStruct((T, GW), MXU_DTYPE), jax.ShapeDtypeStruct((GW, T), MXU_DTYPE)],
                 )(*os, *ls, proj2)


def _combine_bwd(os, ls, proj2, da, dproj):
    T = proj2.shape[0]
    tr = _tile(T, 256)

    def body(o1, o2, o3, l1, l2, l3, z, da_ref, _, d1, d2, d3, e1, e2, e3, dz_ref):
        _, vjp = jax.vjp(_combine, o1[...], o2[...], o3[...], l1[...], l2[...], l3[...], z[...].astype(F32))
        go1, go2, go3, gl1, gl2, gl3, gz = vjp(da_ref[...])
        d1[...], d2[...], d3[...] = go1, go2, go3
        dz_ref[...] = gz.astype(dz_ref.dtype)
        for ref, gl in ((e1, gl1), (e2, gl2), (e3, gl3)):
            for h in range(HPG):
                sl = slice(h * HEAD, (h + 1) * HEAD)
                ref[:, sl] = jnp.broadcast_to(jnp.sum(gl[:, sl], axis=-1, keepdims=True), (tr, HEAD))

    row = pl.BlockSpec((tr, GW), lambda i: (i, 0))
    f = jax.ShapeDtypeStruct((T, GW), F32)
    z_attn = pl.BlockSpec((tr, GW), lambda i: (i, ZA // GW))
    outs = _call(body, name="combine_bwd", grid=(T // tr,), in_specs=[row] * 6 + [z_attn, row, ANY],
                 out_specs=[row] * 6 + [z_attn], out_shape=[f] * 6 + [jax.ShapeDtypeStruct(dproj.shape, dproj.dtype)],
                 aliases={8: 6})(*os, *ls, proj2, da, dproj)
    return outs[:3], outs[3:6], outs[6]


def _shift_down(u, j, t):
    return jnp.where(t >= j, pltpu.roll(u, j, 0), 0.0)


def _shift_up(u, j, t):
    n = u.shape[0]
    return jnp.where(t < n - j, pltpu.roll(u, n - j, 0), 0.0)


def _conv_specs(Bl, S, cw):
    def sec(c0):
        return pl.BlockSpec((1, S, cw), lambda j, b: (b, 0, c0 // cw + j))
    return [sec(CB), sec(CC), sec(CV), sec(ZC)], pl.BlockSpec((3, cw), lambda j, b: (0, j))


def _conv_fwd(proj3, conv_w):
    Bl, S, _ = proj3.shape
    cw = 256
    secs, wspec = _conv_specs(Bl, S, cw)

    def body(b_ref, c_ref, v_ref, z_ref, w_ref, o_ref, ot_ref):
        t = lax.broadcasted_iota(jnp.int32, (S, cw), 0)
        u = c_ref[0].astype(F32) * v_ref[0].astype(F32)
        y = w_ref[0:1, :] * u + w_ref[1:2, :] * _shift_down(u, 1, t) + w_ref[2:3, :] * _shift_down(u, 2, t)
        out = b_ref[0].astype(F32) * y * _silu(z_ref[0].astype(F32))
        o_ref[0] = out.astype(o_ref.dtype)
        ot_ref[...] = out.T.astype(ot_ref.dtype)

    return _call(body, name="conv_fwd", grid=(CONVW // cw, Bl), in_specs=secs + [wspec],
                 out_specs=[pl.BlockSpec((1, S, cw), lambda j, b: (b, 0, j)), pl.BlockSpec((cw, S), lambda j, b: (j, b))],
                 out_shape=[jax.ShapeDtypeStruct((Bl, S, CONVW), MXU_DTYPE),
                            jax.ShapeDtypeStruct((CONVW, Bl * S), MXU_DTYPE)])(proj3, proj3, proj3, proj3, conv_w)


def _conv_bwd(proj3, conv_w, dcc3, dproj):
    Bl, S, _ = proj3.shape
    cw = 256
    secs, wspec = _conv_specs(Bl, S, cw)

    def body(b_ref, c_ref, v_ref, z_ref, w_ref, d_ref, _, dproj_ref, dw_ref, stage, sems):
        t = lax.broadcasted_iota(jnp.int32, (S, cw), 0)
        bv, cv, vv, zv = (r[0].astype(F32) for r in (b_ref, c_ref, v_ref, z_ref))
        dv = d_ref[0]
        u = cv * vv
        u1, u2 = _shift_down(u, 1, t), _shift_down(u, 2, t)
        y = w_ref[0:1, :] * u + w_ref[1:2, :] * u1 + w_ref[2:3, :] * u2
        sg = _sig(zv)
        sz = zv * sg
        gy = dv * bv * sz
        du = w_ref[0:1, :] * gy + w_ref[1:2, :] * _shift_up(gy, 1, t) + w_ref[2:3, :] * _shift_up(gy, 2, t)
        j, b = pl.program_id(0), pl.program_id(1)
        tiles = [dv * y * sz, du * vv, du * cv, dv * bv * y * sg * (1.0 + zv * (1.0 - sg))]
        dsts = [dproj_ref.at[pl.ds(b * S, S), pl.ds(c0 + j * cw, cw)] for c0 in (CB, CC, CV, ZC)]
        _emit_tiles(j * Bl + b, (CONVW // cw) * Bl, tiles, dsts, stage, sems)

        @pl.when(pl.program_id(1) == 0)
        def _():
            dw_ref[...] = jnp.zeros_like(dw_ref)

        dw_ref[0:1, :] += jnp.sum(gy * u, axis=0, keepdims=True)
        dw_ref[1:2, :] += jnp.sum(gy * u1, axis=0, keepdims=True)
        dw_ref[2:3, :] += jnp.sum(gy * u2, axis=0, keepdims=True)

    blk = pl.BlockSpec((1, S, cw), lambda j, b: (b, 0, j))
    return _call(body, name="conv_bwd", grid=(CONVW // cw, Bl), in_specs=secs + [wspec, blk, ANY],
                 out_specs=[ANY, wspec],
                 out_shape=[jax.ShapeDtypeStruct(dproj.shape, dproj.dtype), jax.ShapeDtypeStruct((3, CONVW), F32)],
                 scratch_shapes=_emit_scratch(4, S, cw), aliases={6: 0})(proj3, proj3, proj3, proj3, conv_w, dcc3, dproj)


def _mem_specs(S, tq):
    q = pl.BlockSpec((1, tq, MEMW), lambda b, j: (b, j, MQ // MEMW))
    z = pl.BlockSpec((1, tq, MEMW), lambda b, j: (b, j, ZM // MEMW))
    kv = pl.BlockSpec((1, MEM_HD, 2 * MEMW), lambda b, j: (b, 0, 0))
    vec = pl.BlockSpec((1, MEM_HD), lambda b, j: (0, 0))
    blk = pl.BlockSpec((1, tq, MEMW), lambda b, j: (b, j, 0))
    return q, z, kv, vec, blk


def _mem_fwd(proj3, mkv3, gq, gk):
    Bl, S, _ = proj3.shape
    tq = _tile(S, 512)
    q, z, kv, vec, blk = _mem_specs(S, tq)

    def body(q_ref, z_ref, kv_ref, gq_ref, gk_ref, o_ref, ot_ref):
        out = _mem_block(q_ref[0].astype(F32), z_ref[0].astype(F32), kv_ref[0], gq_ref[...], gk_ref[...])
        o_ref[0] = out.astype(o_ref.dtype)
        ot_ref[...] = out.T.astype(ot_ref.dtype)

    nq = S // tq
    return _call(body, name="mem_fwd", grid=(Bl, nq), in_specs=[q, z, kv, vec, vec],
                 out_specs=[blk, pl.BlockSpec((MEMW, tq), lambda b, j: (0, b * nq + j))],
                 out_shape=[jax.ShapeDtypeStruct((Bl, S, MEMW), MXU_DTYPE),
                            jax.ShapeDtypeStruct((MEMW, Bl * S), MXU_DTYPE)])(proj3, proj3, mkv3, gq, gk)


def _mem_bwd(proj3, mkv3, gq, gk, dmo3, dproj):
    Bl, S, _ = proj3.shape
    tq = _tile(S, 256)
    q, z, kv, vec, blk = _mem_specs(S, tq)
    nq = S // tq

    def body(q_ref, z_ref, kv_ref, gq_ref, gk_ref, d_ref, _, dproj_ref, dkv_ref, dgq_ref, dgk_ref, stage, sems):
        dq, dz, dkv, dgq, dgk = _mem_block_bwd(q_ref[0].astype(F32), z_ref[0].astype(F32), kv_ref[0], gq_ref[...],
                                               gk_ref[...], d_ref[0])
        j = pl.program_id(1)
        rows = pl.ds(pl.program_id(0) * S + j * tq, tq)
        dsts = [dproj_ref.at[rows, pl.ds(MQ, MEMW)], dproj_ref.at[rows, pl.ds(ZM, MEMW)]]
        _emit_tiles(pl.program_id(0) * nq + j, Bl * nq, [dq, dz], dsts, stage, sems)

        @pl.when(j == 0)
        def _():
            dkv_ref[0] = jnp.zeros_like(dkv)

        @pl.when((j == 0) & (pl.program_id(0) == 0))
        def _():
            dgq_ref[...] = jnp.zeros_like(dgq_ref)
            dgk_ref[...] = jnp.zeros_like(dgk_ref)

        dkv_ref[0] += dkv
        dgq_ref[...] += dgq
        dgk_ref[...] += dgk

    gshp = jax.ShapeDtypeStruct((1, MEM_HD), F32)
    return _call(body, name="mem_bwd", grid=(Bl, nq), in_specs=[q, z, kv, vec, vec, blk, ANY],
                 out_specs=[ANY, kv, vec, vec],
                 out_shape=[jax.ShapeDtypeStruct(dproj.shape, dproj.dtype), jax.ShapeDtypeStruct(mkv3.shape, F32),
                            gshp, gshp],
                 scratch_shapes=_emit_scratch(2, tq, MEMW), aliases={6: 0})(proj3, proj3, mkv3, gq, gk, dmo3, dproj)


def _merge_specs(T, D, tm, tn):
    def act(w):
        return pl.BlockSpec((tm, w), lambda i, n: (i, 0))

    def wsp(w):
        return pl.BlockSpec((w, tn), lambda i, n: (0, n))

    gates = [pl.BlockSpec((tm, tn), lambda i, n, k=k: (i, (G0 + k * D) // tn + n)) for k in range(3)]
    tile = pl.BlockSpec((tm, tn), lambda i, n: (i, n))
    return act, wsp, gates, tile


def _merge_fwd(a, cc, mo, wa, wc, wm, proj2):
    T, D = a.shape[0], wa.shape[1]
    tm, tn = _tile(T, 1024), _tile(D, 512)
    act, wsp, gates, tile = _merge_specs(T, D, tm, tn)

    def body(a_ref, c_ref, m_ref, wa_ref, wc_ref, wm_ref, g0, g1, g2, mg_ref, mt_ref, pa_ref, pc_ref, pm_ref):
        pa = jnp.dot(a_ref[...], wa_ref[...], preferred_element_type=F32)
        pc = jnp.dot(c_ref[...], wc_ref[...], preferred_element_type=F32)
        pm = jnp.dot(m_ref[...], wm_ref[...], preferred_element_type=F32)
        mg = _sig(g0[...].astype(F32)) * pa + _sig(g1[...].astype(F32)) * pc + _sig(g2[...].astype(F32)) * pm
        mg_ref[...] = mg.astype(mg_ref.dtype)
        mt_ref[...] = mg.T.astype(mt_ref.dtype)
        pa_ref[...] = pa.astype(pa_ref.dtype)
        pc_ref[...] = pc.astype(pc_ref.dtype)
        pm_ref[...] = pm.astype(pm_ref.dtype)

    shp = jax.ShapeDtypeStruct((T, D), MXU_DTYPE)
    return _call(body, name="merge_fwd", grid=(T // tm, D // tn),
                 in_specs=[act(GW), act(CONVW), act(MEMW), wsp(GW), wsp(CONVW), wsp(MEMW)] + gates,
                 out_specs=[tile, pl.BlockSpec((tn, tm), lambda i, n: (n, i)), tile, tile, tile],
                 out_shape=[shp, jax.ShapeDtypeStruct((D, T), MXU_DTYPE), shp, shp, shp],
                 )(a, cc, mo, wa, wc, wm, proj2, proj2, proj2)


def _emit_tiles(step, nsteps, tiles, dsts, stage, sems):
    slot = step % 2

    def copies(s):
        return [pltpu.make_async_copy(stage.at[s, k], dsts[k], sems.at[s, k]) for k in range(len(tiles))]

    @pl.when(step >= 2)
    def _():
        for cp in copies(slot):
            cp.wait()

    for k, t in enumerate(tiles):
        stage[slot, k] = t.astype(stage.dtype)
    for cp in copies(slot):
        cp.start()

    @pl.when(step == nsteps - 1)
    def _():
        for cp in copies(slot):
            cp.wait()
        if nsteps > 1:
            for cp in copies(1 - slot):
                cp.wait()


def _emit_scratch(k, rows, cols):
    return [pltpu.VMEM((2, k, rows, cols), MXU_DTYPE), pltpu.SemaphoreType.DMA((2, k))]


def _merge_bwd(dyb, w_out, proj2, pa, pc, pm):
    T, D = dyb.shape
    IN = proj2.shape[1]
    tm, tn = _tile(T, 1024), _tile(D, 512)
    _, _, gates, tile = _merge_specs(T, D, tm, tn)
    nn = D // tn

    def body(dy_ref, w_ref, g0, g1, g2, p0, p1, p2, dp0, dp1, dp2, dproj_ref, stage, sems):
        i, n = pl.program_id(0), pl.program_id(1)
        dm = lax.dot_general(dy_ref[...], w_ref[...], _DIMS["nt"], preferred_element_type=F32)
        tiles, dsts = [], []
        for k, (g_ref, p_ref, dp_ref) in enumerate(((g0, p0, dp0), (g1, p1, dp1), (g2, p2, dp2))):
            gt = _sig(g_ref[...].astype(F32))
            dp_ref[...] = (gt * dm).astype(dp_ref.dtype)
            tiles.append(dm * p_ref[...].astype(F32) * gt * (1.0 - gt))
            dsts.append(dproj_ref.at[pl.ds(i * tm, tm), pl.ds(G0 + k * D + n * tn, tn)])
        _emit_tiles(i * nn + n, (T // tm) * nn, tiles, dsts, stage, sems)

    shp = jax.ShapeDtypeStruct((T, D), MXU_DTYPE)
    return _call(body, name="merge_bwd", grid=(T // tm, nn),
                 in_specs=[pl.BlockSpec((tm, D), lambda i, n: (i, 0)), pl.BlockSpec((tn, D), lambda i, n: (n, 0))]
                 + gates + [tile] * 3,
                 out_specs=[tile] * 3 + [ANY], out_shape=[shp] * 3 + [jax.ShapeDtypeStruct((T, IN), MXU_DTYPE)],
                 scratch_shapes=_emit_scratch(3, tm, tn))(dyb, w_out, proj2, proj2, proj2, pa, pc, pm)


def _out_loss(merged, w_out, x, tgt):
    T, D = x.shape
    tm = _tile(T, 512)

    def body(m_ref, w_ref, x_ref, t_ref, dy_ref, dyb_ref, loss_ref):
        err = x_ref[...] + jnp.dot(m_ref[...], w_ref[...], preferred_element_type=F32) - t_ref[...]
        dy = err * (1.0 / D)
        dy_ref[...] = dy
        dyb_ref[...] = dy.astype(dyb_ref.dtype)

        @pl.when(pl.program_id(0) == 0)
        def _():
            loss_ref[...] = jnp.zeros_like(loss_ref)

        loss_ref[...] += jnp.sum(err * err) * (0.5 / D)

    row = pl.BlockSpec((tm, D), lambda i: (i, 0))
    return _call(body, name="out_loss", grid=(T // tm,),
                 in_specs=[row, pl.BlockSpec((D, D), lambda i: (0, 0)), row, row],
                 out_specs=[row, row, pl.BlockSpec((1, 128), lambda i: (0, 0))],
                 out_shape=[jax.ShapeDtypeStruct((T, D), F32), jax.ShapeDtypeStruct((T, D), MXU_DTYPE),
                            jax.ShapeDtypeStruct((1, 128), F32)])(merged, w_out, x, tgt)


def _proj_chunk(hb, w, meta, j, nslots, half, buf, name):
    T, D = hb.shape
    Cs = w.shape[1] // 4
    tm, tn = _tile(T, 1024), _tile(Cs // 2, 2176)
    nh = Cs // 2 // tn
    per = nh if half is not None else 2 * nh

    def body(meta_ref, a_ref, b_ref, *rest):
        rest[-1][...] = jnp.dot(a_ref[...], b_ref[...], preferred_element_type=F32).astype(rest[-1].dtype)

    def tile(n, m):
        if half is None:
            return n % per
        return (m[4] if half == 0 else 1 - m[4]) * nh + n % per

    in_specs = [pl.BlockSpec((tm, D), lambda n, i, m: (i, 0)),
                pl.BlockSpec((D, tn), lambda n, i, m: (0, (j + n // per) * 2 * nh + tile(n, m)))]
    args = [meta, hb, w]
    if buf is not None:
        in_specs.append(ANY)
        args.append(buf)
    spec = pltpu.PrefetchScalarGridSpec(
        num_scalar_prefetch=1, grid=(nslots * per, T // tm), in_specs=in_specs,
        out_specs=pl.BlockSpec((tm, tn), lambda n, i, m: (i, m[j + n // per] * 2 * nh + tile(n, m))))
    return _call(body, name=name, grid_spec=spec, out_shape=jax.ShapeDtypeStruct((T, 4 * Cs), PROJ_DTYPE),
                 aliases={} if buf is None else {3: 0})(*args)


def _norms(x, mem, norm_g, mem_norm_g):
    D = x.shape[-1]
    hb, hbt = _rms_fwd(x.reshape(-1, D), norm_g.reshape(1, D), "rms_x")
    mhb, _ = _rms_fwd(mem.reshape(-1, D), mem_norm_g.reshape(1, D), "rms_mem")
    return hb, hbt, mhb


def _attention_fwd(proj2, Bl, gq_all, gk_all):
    T, IN = proj2.shape
    proj3 = proj2.reshape(Bl, T // Bl, IN)
    os, ls = [], []
    for g, d in enumerate(DILATIONS):
        o, l = _attn_fwd(proj3, gq_all[g:g + 1], gk_all[g:g + 1], g, d)
        os.append(o.reshape(T, GW))
        ls.append(l.reshape(T, GW))
    return os, ls, _combine_fwd(os, ls, proj2)


def _conv_branch_fwd(proj2, Bl, conv_w):
    T, IN = proj2.shape
    cc, cct = _conv_fwd(proj2.reshape(Bl, T // Bl, IN), conv_w)
    return cc.reshape(T, CONVW), cct


def _weight_grads(x, mem, tgt, norm_g, mem_norm_g, gq_all, gk_all, conv_w, mem_gq, mem_gk, W, pre, early=None):
    Bl, S, D = x.shape
    T = Bl * S
    hb, hbt, mhb, proj2, os, ls, (a, at), (cc, cct) = pre
    IN = proj2.shape[1]
    proj3 = proj2.reshape(Bl, S, IN)
    x2, tgt2 = x.reshape(T, D), tgt.reshape(T, D)
    mem2 = mem.reshape(-1, D)
    ng, mng = norm_g.reshape(1, D), mem_norm_g.reshape(1, D)
    mgq, mgk = mem_gq.reshape(1, MEM_HD), mem_gk.reshape(1, MEM_HD)
    gqs = [gq_all[g:g + 1] for g in range(NGROUP)]
    gks = [gk_all[g:g + 1] for g in range(NGROUP)]

    mkv = _matmul(mhb, W["mem_w_kv"], "nn", F32, name="mem_kv", tm=512, tn=1024, tk=D)
    mkv3 = mkv.reshape(Bl, -1, 2 * MEMW)
    mo, mot = _mem_fwd(proj3, mkv3, mgq, mgk)
    mo = mo.reshape(T, MEMW)
    merged, mergedt, pa, pc, pm = _merge_fwd(a, cc, mo, W["w_br_attn"], W["w_br_conv"], W["w_br_mem"], proj2)
    dy, dyb, loss = _out_loss(merged, W["w_out"], x2, tgt2)

    G = {}
    G["w_out"] = _matmul(mergedt, dyb, "nn", WIRE_DTYPE, name="dw_out", tm=1024, tn=512, tk=T)
    dpa, dpc, dpm, dproj = _merge_bwd(dyb, W["w_out"], proj2, pa, pc, pm)
    G["w_br_attn"] = _matmul(at, dpa, "nn", WIRE_DTYPE, name="dw_br_attn", tm=512, tn=512, tk=T)
    G["w_br_conv"] = _matmul(cct, dpc, "nn", WIRE_DTYPE, name="dw_br_conv", tm=1024, tn=512, tk=T)
    G["w_br_mem"] = _matmul(mot, dpm, "nn", WIRE_DTYPE, name="dw_br_mem", tm=1024, tn=512, tk=T)
    da = _matmul(dpa, W["w_br_attn"], "nt", F32, name="d_attn", tm=1024, tn=512, tk=D)
    dcc = _matmul(dpc, W["w_br_conv"], "nt", F32, name="d_conv", tm=1024, tn=1024, tk=D)
    dmo = _matmul(dpm, W["w_br_mem"], "nt", F32, name="d_mem", tm=1024, tn=1024, tk=D)
    dproj, dmkv3, dmgq, dmgk = _mem_bwd(proj3, mkv3, mgq, mgk, dmo.reshape(Bl, S, MEMW), dproj)
    dmkv = _cast(dmkv3.reshape(-1, 2 * MEMW), "cast_dmkv")
    G["mem_w_kv"] = _matmul(mhb, dmkv, "tn", WIRE_DTYPE, name="dw_mem_kv", tm=1024, tn=1024, tk=512)
    early_state, da = (None, da) if early is None else early(G, da)
    dmh = _matmul(dmkv, W["mem_w_kv"], "nt", F32, name="d_memh", tm=512, tn=1024, tk=2 * MEMW)
    _, dmng = _rms_bwd(mem2, dmh, mng, None, "rms_mem_bwd")

    dos, dls, dproj = _combine_bwd(os, ls, proj2, da, dproj)
    dgq, dgk = [], []
    for g, d in enumerate(DILATIONS):
        dq, dk, dv, gq_g, gk_g = _attn_bwd(proj3, gqs[g], gks[g], os[g].reshape(Bl, S, GW), ls[g].reshape(Bl, S, GW),
                                           dos[g].reshape(Bl, S, GW), dls[g].reshape(Bl, S, GW), g, d)
        for c0, part in ((Q0, dq), (K0, dk), (V0, dv)):
            dproj = lax.dynamic_update_slice(dproj, part.reshape(T, GW), (0, c0 + g * GW))
        dgq.append(gq_g)
        dgk.append(gk_g)
    dproj, dconv_w = _conv_bwd(proj3, conv_w, dcc.reshape(Bl, S, CONVW), dproj)
    small = [loss, None, dmng] + dgq + dgk + [dconv_w.reshape(1, 3 * CONVW), dmgq, dmgk]
    return G, (dproj, x2, ng, dy, small), early_state


def _dw_in_half(hbt, dproj, pos, own, name):
    D, T = hbt.shape
    IN = dproj.shape[1]
    R, tn = D // 2, _tile(IN, 1024)

    def body(pos_ref, a_ref, b_ref, o_ref):
        o_ref[...] = jnp.dot(a_ref[...], b_ref[...], preferred_element_type=F32).astype(o_ref.dtype)

    spec = pltpu.PrefetchScalarGridSpec(
        num_scalar_prefetch=1, grid=(IN // tn,),
        in_specs=[pl.BlockSpec((R, T), lambda j, p: (p[1] if own else 1 - p[1], 0)),
                  pl.BlockSpec((T, tn), lambda j, p: (0, j))],
        out_specs=pl.BlockSpec((R, tn), lambda j, p: (0, j)))
    return _call(body, name=name, grid_spec=spec, out_shape=jax.ShapeDtypeStruct((R, IN), WIRE_DTYPE))(pos, hbt, dproj)


def _d_h(dproj, w, order):
    T, IN = dproj.shape
    D, Cs = w.shape[0], IN // 4
    tm, tn = _tile(T, 1024), _tile(D, 1024)

    def body(order_ref, a_ref, b_ref, o_ref, acc_ref):
        part = lax.dot_general(a_ref[...], b_ref[...], _DIMS["nt"], preferred_element_type=F32)
        k = pl.program_id(2)

        @pl.when(k == 0)
        def _():
            acc_ref[...] = part

        @pl.when(k > 0)
        def _():
            acc_ref[...] += part

        @pl.when(k == 3)
        def _():
            o_ref[...] = acc_ref[...]

    spec = pltpu.PrefetchScalarGridSpec(
        num_scalar_prefetch=1, grid=(T // tm, D // tn, 4),
        in_specs=[pl.BlockSpec((tm, Cs), lambda i, n, k, o: (i, o[k])), pl.BlockSpec((tn, Cs), lambda i, n, k, o: (n, k))],
        out_specs=pl.BlockSpec((tm, tn), lambda i, n, k, o: (i, n)), scratch_shapes=[pltpu.VMEM((tm, tn), F32)])
    return _call(body, name="d_h", grid_spec=spec, out_shape=jax.ShapeDtypeStruct((T, D), F32))(order, dproj, w)


def _input_grad(rest, w_in, order):
    dproj, x2, ng, dy, small = rest
    dh = _d_h(dproj, w_in, order)
    grad_x, dng = _rms_bwd(x2, dh, ng, dy, "rms_x_bwd")
    small = [dng if t is None else t for t in small]
    return grad_x, jnp.concatenate(small, axis=1)


def _local_step(x, mem, tgt, norm_g, mem_norm_g, gq_all, gk_all, conv_w, mem_gq, mem_gk, W):
    hb, hbt, mhb = _norms(x, mem, norm_g, mem_norm_g)
    Cs = W["w_in"].shape[1] // 4
    shards = (0, 2, 1, 3)
    order = jnp.array(shards, dtype=jnp.int32)
    w_rel = jnp.concatenate([W["w_in"][:, s * Cs:(s + 1) * Cs] for s in shards], axis=1)
    meta = jnp.array(shards + (0,), dtype=jnp.int32)
    proj2 = _proj_chunk(hb, w_rel, meta, 0, 1, None, None, "proj_0")
    for j, nslots in ((1, 2), (3, 1)):
        for half in (1, 0):
            proj2 = _proj_chunk(hb, w_rel, meta, j, nslots, half, proj2, f"proj_{j}_{half}")
    pre = (hb, hbt, mhb, proj2, *_attention_fwd(proj2, x.shape[0], gq_all, gk_all),
           _conv_branch_fwd(proj2, x.shape[0], conv_w))
    G, rest, _ = _weight_grads(x, mem, tgt, norm_g, mem_norm_g, gq_all, gk_all, conv_w, mem_gq, mem_gk, W, pre)
    pos = jnp.zeros((2,), jnp.int32)
    G["w_in"] = jnp.concatenate([_dw_in_half(hbt, rest[0], pos, True, "dw_in_own"),
                                 _dw_in_half(hbt, rest[0], pos, False, "dw_in_sibling")], axis=0)
    grad_x, small = _input_grad(rest, w_rel, order)
    return grad_x.reshape(x.shape), G, small


BIG = (("w_in", "col"), ("mem_w_kv", "row"), ("w_br_attn", "col"), ("w_br_conv", "col"),
       ("w_br_mem", "col"), ("w_out", "row"))


def _coords():
    return lax.axis_index("x"), lax.axis_index("y"), lax.axis_index("c")


def _other_chips(x, y):
    return [(1 - x, y), (x, 1 - y), (1 - x, 1 - y)]


def _half(ref, kind, c):
    R, C = ref.shape
    if kind == "col":
        return ref.at[pl.ds(c * (R // 2), R // 2), :]
    return ref.at[:, pl.ds(c * (C // 2), C // 2)]


def _shard(ref, kind, s):
    R, C = ref.shape
    if kind == "col":
        return ref.at[:, pl.ds(s * (C // 4), C // 4)]
    return ref.at[pl.ds(s * (R // 4), R // 4), :]


def _piece(ref, kind, s, c):
    R, C = ref.shape
    if kind == "col":
        return ref.at[pl.ds(c * (R // 2), R // 2), pl.ds(s * (C // 4), C // 4)]
    return ref.at[pl.ds(s * (R // 4), R // 4), pl.ds(c * (C // 2), C // 2)]


def _remote(src, dst, sems_s, sems_r, k, dev):
    return pltpu.make_async_remote_copy(src_ref=src, dst_ref=dst, send_sem=sems_s.at[k], recv_sem=sems_r.at[k],
                                        device_id=dev, device_id_type=MESH)


HBM = pl.BlockSpec(memory_space=pltpu.HBM)
SEM = pl.BlockSpec(memory_space=pltpu.SEMAPHORE)
EFFECT = pltpu.SideEffectType.DATAFLOW_SIDE_EFFECTING


def _hbm(a):
    return pltpu.with_memory_space_constraint(a, pltpu.HBM)


def _start_copies(name, arrays, ncopies, make):
    n = len(arrays)

    def body(*refs):
        for cp in make(refs[:n], refs[n], refs[n + 1]):
            cp.start()

    outs = pl.pallas_call(
        body, name=name,
        out_shape=(pltpu.SemaphoreType.DMA((ncopies,)), pltpu.SemaphoreType.DMA((ncopies,)),
                   *[jax.ShapeDtypeStruct(t.shape, t.dtype) for t in arrays]),
        in_specs=[HBM] * n, out_specs=(SEM, SEM, *([HBM] * n)),
        input_output_aliases={i: i + 2 for i in range(n)},
        compiler_params=pltpu.CompilerParams(has_side_effects=EFFECT),
    )(*[_hbm(t) for t in arrays])
    return outs[0], outs[1], list(outs[2:])


def _wait_copies(name, send, recv, arrays, make, after):
    n = len(arrays)

    def body(*refs):
        for cp in make(refs[:n], refs[n], refs[n + 1]):
            cp.wait_send()
            cp.wait_recv()

    outs = pl.pallas_call(
        body, name=name, out_shape=[jax.ShapeDtypeStruct(t.shape, t.dtype) for t in arrays],
        in_specs=[HBM] * n + [SEM, SEM, ANY], out_specs=[HBM] * n,
        input_output_aliases={i: i for i in range(n)},
        compiler_params=pltpu.CompilerParams(has_side_effects=EFFECT),
    )(*arrays, send, recv, after)
    return list(outs)


def _w_in_copies(relations):
    def make(refs, send, recv):
        x, y, c = _coords()
        me = 2 * x + y
        chips = _other_chips(x, y)
        w, conv = refs[0], refs[1]
        cps = []
        for i, k in enumerate(relations):
            cps.append(_remote(_column_half(w, 0, c), _column_half(w, 1 + k, c), send, recv, 2 * i, (*chips[k], c)))
            mine = _shard(conv, "col", me)
            cps.append(_remote(mine, mine, send, recv, 2 * i + 1, (*chips[k], c)))
        return cps
    return make


def _column_half(w, slot, c):
    half = w.shape[1] // 8
    return w.at[:, pl.ds((2 * slot + c) * half, half)]


def _w_in_forward(relations):
    def make(refs, send, recv):
        x, y, c = _coords()
        cps = []
        for i, k in enumerate(relations):
            got = _column_half(refs[0], 1 + k, c)
            cps.append(_remote(got, got, send, recv, i, (x, y, 1 - c)))
        return cps
    return make


def _sibling_copy(refs, send, recv):
    x, y, c = _coords()
    return [_remote(refs[0], refs[1], send, recv, 0, (x, y, 1 - c))]


def _other_weight_copies(refs, send, recv):
    x, y, c = _coords()
    me = 2 * x + y
    cps = []
    for k, chip in enumerate(_other_chips(x, y)):
        for p, (_, kind) in enumerate(BIG[1:]):
            mine = _piece(refs[p], kind, me, c)
            cps.append(_remote(mine, mine, send, recv, 3 * p + k, (*chip, c)))
    return cps


def _other_weight_forward(refs, send, recv):
    x, y, c = _coords()
    cps = []
    for k, chip in enumerate(_other_chips(x, y)):
        s = 2 * chip[0] + chip[1]
        for p, (_, kind) in enumerate(BIG[1:]):
            got = _piece(refs[p], kind, s, c)
            cps.append(_remote(got, got, send, recv, 3 * p + k, (x, y, 1 - c)))
    return cps


def _share_copies(group):
    def make(refs, send, recv):
        x, y, c = _coords()
        cps = []
        for p, (_, kind) in enumerate(group):
            mine = _half(refs[p], kind, c)
            cps.append(_remote(mine, mine, send, recv, p, (x, y, 1 - c)))
        return cps
    return make


def _sibling_exchange(grads, group, name):
    n = len(group)
    shapes = []
    for (_, kind), g in zip(group, grads):
        R, C = g.shape
        shapes.append(jax.ShapeDtypeStruct((R // 2, C) if kind == "col" else (R, C // 2), g.dtype))

    def body(*refs):
        ins, outs = refs[:n], refs[n:2 * n]
        send, recv = refs[2 * n:]
        x, y, c = _coords()
        sib = (x, y, 1 - c)
        cps = [_remote(_half(ins[p], group[p][1], 1 - c), outs[p], send, recv, p, sib) for p in range(n)]
        for cp in cps:
            cp.start()
        for cp in cps:
            cp.wait()

    return pl.pallas_call(
        body, name=name, out_shape=shapes, in_specs=[ANY] * n, out_specs=[ANY] * n,
        scratch_shapes=[pltpu.SemaphoreType.DMA((n,)), pltpu.SemaphoreType.DMA((n,))],
    )(*grads)


def _presum(g, got, kind, pos, name):
    R, C = got.shape
    tr, tc = _tile(R, 512, 16), _tile(C, 2048)
    nr, nc = R // tr, C // tc

    def body(pos_ref, a_ref, b_ref, o_ref):
        o_ref[...] = (a_ref[...].astype(F32) + b_ref[...].astype(F32)).astype(o_ref.dtype)

    blk = pl.BlockSpec((tr, tc), lambda i, j, pos_ref: (i, j))
    if g.shape == got.shape:
        mine = blk
    elif kind == "col":
        mine = pl.BlockSpec((tr, tc), lambda i, j, pos_ref: (pos_ref[1] * nr + i, j))
    else:
        mine = pl.BlockSpec((tr, tc), lambda i, j, pos_ref: (i, pos_ref[1] * nc + j))
    spec = pltpu.PrefetchScalarGridSpec(num_scalar_prefetch=1, grid=(nr, nc), in_specs=[mine, blk], out_specs=blk)
    return _call(body, name=name, grid_spec=spec, out_shape=jax.ShapeDtypeStruct((R, C), WIRE_DTYPE))(pos, g, got)


def _chip_copies(group):
    n = len(group)

    def make(refs, send, recv):
        x, y, c = _coords()
        cps = []
        for k, chip in enumerate(_other_chips(x, y)):
            s = 2 * chip[0] + chip[1]
            for p in range(n):
                cps.append(_remote(_shard(refs[p], group[p][1], s), refs[n + p].at[k], send, recv, 3 * p + k, (*chip, c)))
        return cps
    return make


def _landing_zones(pres, group):
    lands = []
    for (_, kind), g in zip(group, pres):
        R, C = g.shape
        lands.append(lax.empty((3, R, C // 4) if kind == "col" else (3, R // 4, C), g.dtype))
    return lands


def _exchange_start(G, group, pos, carry, tag, pres=None):
    n = len(group)
    if pres is None:
        parts = [G[name] for name, _ in group]
        got = _sibling_exchange(parts, group, "sibling_exchange_" + tag)
        pres = [_presum(parts[p], got[p], kind, pos, "presum_" + name) for p, (name, kind) in enumerate(group)]
    make = _chip_copies(group)
    send, recv, thru = _start_copies("chip_exchange_start_" + tag, [*pres, *_landing_zones(pres, group), carry], 3 * n, make)
    return (send, recv, thru[:2 * n], make, tag), thru[2 * n]


def _exchange_wait(state, after):
    send, recv, arrays, make, tag = state
    thru = _wait_copies("chip_exchange_wait_" + tag, send, recv, arrays, make, after)
    n = len(thru) // 2
    return thru[:n], thru[n:]


def _reduce_into_shard(slots, pre, kind, pos, name):
    K, R, C = slots.shape
    tr, tc = _tile(R, 512, 16), _tile(C, 2176)
    nr, nc = R // tr, C // tc

    def body(pos_ref, s_ref, p_ref, o_ref):
        acc = p_ref[...].astype(F32)
        for k in range(K):
            acc = acc + s_ref[k].astype(F32)
        o_ref[...] = acc

    if kind == "col":
        own = pl.BlockSpec((tr, tc), lambda i, j, pos_ref: (i, pos_ref[0] * nc + j))
        full, out = (2 * R, C), pl.BlockSpec((tr, tc), lambda i, j, pos_ref: (pos_ref[1] * nr + i, j))
    else:
        own = pl.BlockSpec((tr, tc), lambda i, j, pos_ref: (pos_ref[0] * nr + i, j))
        full, out = (R, 2 * C), pl.BlockSpec((tr, tc), lambda i, j, pos_ref: (i, pos_ref[1] * nc + j))
    spec = pltpu.PrefetchScalarGridSpec(
        num_scalar_prefetch=1, grid=(nr, nc),
        in_specs=[pl.BlockSpec((K, tr, tc), lambda i, j, pos_ref: (0, i, j)), own], out_specs=out)
    return _call(body, name=name, grid_spec=spec, out_shape=jax.ShapeDtypeStruct(full, F32))(pos, slots, pre)


def _gather_small(pack):
    _, N = pack.shape

    def body(in_ref, out_ref, send, recv, loc):
        x, y, c = _coords()
        me = 4 * x + 2 * y + c
        own = pltpu.make_async_copy(in_ref, out_ref.at[me], loc)
        own.start()
        cps = []
        for k in range(1, 8):
            dev = (x ^ (k >> 2), y ^ ((k >> 1) & 1), c ^ (k & 1))
            cps.append(_remote(in_ref, out_ref.at[me], send, recv, k - 1, dev))
        for cp in cps:
            cp.start()
        for k in range(1, 8):
            src = 4 * (x ^ (k >> 2)) + 2 * (y ^ ((k >> 1) & 1)) + (c ^ (k & 1))
            _remote(in_ref, out_ref.at[src], send, recv, k - 1, (x, y, c)).wait_recv()
        for cp in cps:
            cp.wait_send()
        own.wait()

    return pl.pallas_call(
        body, name="gather_small", out_shape=jax.ShapeDtypeStruct((8, 1, N), pack.dtype),
        in_specs=[ANY], out_specs=ANY,
        scratch_shapes=[pltpu.SemaphoreType.DMA((7,)), pltpu.SemaphoreType.DMA((7,)), pltpu.SemaphoreType.DMA(())],
    )(pack)


def _sum_small(slots):
    K, _, N = slots.shape

    def body(s_ref, o_ref):
        acc = s_ref[0]
        for k in range(1, K):
            acc = acc + s_ref[k]
        o_ref[...] = acc

    return _call(body, name="sum_small", in_specs=[pl.BlockSpec(memory_space=pltpu.VMEM)],
                 out_specs=pl.BlockSpec(memory_space=pltpu.VMEM), out_shape=jax.ShapeDtypeStruct((1, N), F32))(slots)


def _adamw(w, g, m, v, name, with_grad=False):
    R, C = w.shape
    tr, tc = _tile(R, 256, 8), _tile(C, 2176)

    def body(w_ref, g_ref, m_ref, v_ref, d_ref, nm_ref, nv_ref, *g_out):
        gv = g_ref[...]
        for ref in g_out:
            ref[...] = gv
        nm = ADAM_B1 * m_ref[...] + (1.0 - ADAM_B1) * gv
        nv = ADAM_B2 * v_ref[...] + (1.0 - ADAM_B2) * gv * gv
        m_hat = nm / (1.0 - ADAM_B1 ** ADAM_STEP)
        v_hat = nv / (1.0 - ADAM_B2 ** ADAM_STEP)
        d_ref[...] = -ADAM_LR * (m_hat / (jnp.sqrt(v_hat) + ADAM_EPS) + ADAM_WD * w_ref[...])
        nm_ref[...] = nm
        nv_ref[...] = nv

    spec = pl.BlockSpec((tr, tc), lambda i, j: (i, j))
    shp = jax.ShapeDtypeStruct((R, C), F32)
    nout = 4 if with_grad else 3
    return _call(body, name=name, grid=(R // tr, C // tc), in_specs=[spec] * 4, out_specs=[spec] * nout,
                 out_shape=[shp] * nout)(w, g, m, v)


SMALL = ("norm_g", "mem_norm_g", "attn_q_norm", "attn_k_norm", "conv_w", "mem_q_norm", "mem_k_norm")
WEIGHTS = ("norm_g", "mem_norm_g", "w_in", "attn_q_norm", "attn_k_norm", "conv_w", "mem_w_kv", "mem_q_norm",
           "mem_k_norm", "w_br_attn", "w_br_conv", "w_br_mem", "w_out")


def kernel(x, mem, norm_g, mem_norm_g, w_in, attn_q_norm, attn_k_norm, conv_w, mem_w_kv, mem_q_norm, mem_k_norm, w_br_attn, w_br_conv, w_br_mem, w_out, loss_target, m_norm_g, m_mem_norm_g, m_w_in, m_attn_q_norm, m_attn_k_norm, m_conv_w, m_mem_w_kv, m_mem_q_norm, m_mem_k_norm, m_w_br_attn, m_w_br_conv, m_w_br_mem, m_w_out, v_norm_g, v_mem_norm_g, v_w_in, v_attn_q_norm, v_attn_k_norm, v_conv_w, v_mem_w_kv, v_mem_q_norm, v_mem_k_norm, v_w_br_attn, v_w_br_conv, v_w_br_mem, v_w_out):
    w = dict(norm_g=norm_g, mem_norm_g=mem_norm_g, w_in=w_in, attn_q_norm=attn_q_norm, attn_k_norm=attn_k_norm,
             conv_w=conv_w, mem_w_kv=mem_w_kv, mem_q_norm=mem_q_norm, mem_k_norm=mem_k_norm, w_br_attn=w_br_attn,
             w_br_conv=w_br_conv, w_br_mem=w_br_mem, w_out=w_out)
    m = dict(norm_g=m_norm_g, mem_norm_g=m_mem_norm_g, w_in=m_w_in, attn_q_norm=m_attn_q_norm,
             attn_k_norm=m_attn_k_norm, conv_w=m_conv_w, mem_w_kv=m_mem_w_kv, mem_q_norm=m_mem_q_norm,
             mem_k_norm=m_mem_k_norm, w_br_attn=m_w_br_attn, w_br_conv=m_w_br_conv, w_br_mem=m_w_br_mem, w_out=m_w_out)
    v = dict(norm_g=v_norm_g, mem_norm_g=v_mem_norm_g, w_in=v_w_in, attn_q_norm=v_attn_q_norm,
             attn_k_norm=v_attn_k_norm, conv_w=v_conv_w, mem_w_kv=v_mem_w_kv, mem_q_norm=v_mem_q_norm,
             mem_k_norm=v_mem_k_norm, w_br_attn=v_w_br_attn, w_br_conv=v_w_br_conv, w_br_mem=v_w_br_mem, w_out=v_w_out)
    Bl, _, D = x.shape
    cx, cy = lax.axis_index("x"), lax.axis_index("y")
    chip = 2 * cx + cy
    pos = jnp.stack([chip, lax.axis_index("c")]).astype(jnp.int32)
    order = jnp.stack([chip] + [2 * a + b for a, b in _other_chips(cx, cy)]).astype(jnp.int32)
    n = len(BIG)

    w_rel = _place_shard(w["w_in"], "col", jnp.zeros((1,), jnp.int32), WIRE_DTYPE, "place_w_in")
    conv_full = _place_shard(conv_w, "col", pos, F32, "place_conv_w")
    others = [_place_shard(w[name], kind, pos, WIRE_DTYPE, "place_" + name) for name, kind in BIG[1:]]
    hb, hbt, mhb = _norms(x, mem, norm_g, mem_norm_g)

    meta = jnp.concatenate([order, pos[1:]])
    near, near_fwd = _w_in_copies((0, 1)), _w_in_forward((0, 1))
    send, recv, (w_rel, conv_full) = _start_copies("gather_near_start", [w_rel, conv_full], 4, near)
    proj = _proj_chunk(hb, w_rel, meta, 0, 1, None, None, "proj_own")
    w_rel, conv_full, *others = _wait_copies("gather_near_wait", send, recv, [w_rel, conv_full, *others], near, proj)

    fsend, frecv, (w_rel,) = _start_copies("gather_near_forward_start", [w_rel], 2, near_fwd)
    far, far_fwd = _w_in_copies((2,)), _w_in_forward((2,))
    send, recv, (w_rel, conv_full) = _start_copies("gather_far_start", [w_rel, conv_full], 2, far)
    proj = _proj_chunk(hb, w_rel, meta, 1, 2, 0, proj, "proj_near_landed")
    w_rel, = _wait_copies("gather_near_forward_wait", fsend, frecv, [w_rel], near_fwd, proj)
    proj = _proj_chunk(hb, w_rel, meta, 1, 2, 1, proj, "proj_near_forwarded")
    w_rel, conv_full = _wait_copies("gather_far_wait", send, recv, [w_rel, conv_full], far, proj)

    fsend, frecv, (w_rel,) = _start_copies("gather_far_forward_start", [w_rel], 1, far_fwd)
    send, recv, (*others, w_rel) = _start_copies("gather_rest_start", [*others, w_rel], 3 * (n - 1), _other_weight_copies)
    proj = _proj_chunk(hb, w_rel, meta, 3, 1, 0, proj, "proj_far_landed")
    w_rel, = _wait_copies("gather_far_forward_wait", fsend, frecv, [w_rel], far_fwd, proj)
    proj = _proj_chunk(hb, w_rel, meta, 3, 1, 1, proj, "proj_far_forwarded")
    os, ls, a = _attention_fwd(proj, Bl, attn_q_norm, attn_k_norm)
    *others, w_rel = _wait_copies("gather_rest_wait", send, recv, [*others, w_rel], _other_weight_copies, a[0])
    fsend, frecv, (*others, proj) = _start_copies("gather_rest_forward_start", [*others, proj], 3 * (n - 1),
                                                  _other_weight_forward)
    cc = _conv_branch_fwd(proj, Bl, conv_full)
    others = _wait_copies("gather_rest_forward_wait", fsend, frecv, others, _other_weight_forward, cc[0])
    W = {name: others[p] for p, (name, _) in enumerate(BIG[1:])}

    G, rest, rest_state = _weight_grads(
        x, mem, loss_target, norm_g, mem_norm_g, attn_q_norm, attn_k_norm, conv_full, mem_q_norm, mem_k_norm, W,
        (hb, hbt, mhb, proj, os, ls, a, cc), early=lambda G, carry: _exchange_start(G, BIG[1:], pos, carry, "rest"))

    for_sibling = _dw_in_half(hbt, rest[0], pos, False, "dw_in_sibling")
    send, recv, (for_sibling, got, dproj) = _start_copies(
        "sibling_w_in_start", [for_sibling, lax.empty(for_sibling.shape, for_sibling.dtype), rest[0]], 1, _sibling_copy)
    mine = _dw_in_half(hbt, dproj, pos, True, "dw_in_own")
    for_sibling, got = _wait_copies("sibling_w_in_wait", send, recv, [for_sibling, got], _sibling_copy, mine)
    pre_w_in = _presum(mine, got, "col", pos, "presum_w_in")

    w_in_state, dproj = _exchange_start(G, BIG[:1], pos, dproj, "w_in", pres=[pre_w_in])
    grad_x, small = _input_grad((dproj, *rest[1:]), w_rel, order)
    pres_rest, slots_rest = _exchange_wait(rest_state, grad_x)
    reds_rest = [_reduce_into_shard(slots_rest[p], pres_rest[p], kind, pos, "reduce_" + name)
                 for p, (name, kind) in enumerate(BIG[1:])]
    share_rest = _share_copies(BIG[1:])
    rsend, rrecv, reds_rest = _start_copies("share_rest_start", reds_rest, n - 1, share_rest)
    pres, slots = _exchange_wait(w_in_state, grad_x)
    red_w_in = _reduce_into_shard(slots[0], pres[0], "col", pos, "reduce_w_in")
    share_w_in = _share_copies(BIG[:1])
    wsend, wrecv, (red_w_in, small) = _start_copies("share_w_in_start", [red_w_in, small], 1, share_w_in)
    grad_x = grad_x.reshape(x.shape)

    tot = _sum_small(_gather_small(small))
    reds_rest = _wait_copies("share_rest_wait", rsend, rrecv, reds_rest, share_rest, tot)
    grads = dict(zip([name for name, _ in BIG[1:]], reds_rest))
    tot = tot[0]
    loss = tot[0]
    off = 128
    for name, size in (("norm_g", D), ("mem_norm_g", D), ("attn_q_norm", NGROUP * HEAD), ("attn_k_norm", NGROUP * HEAD),
                       ("conv_w", 3 * CONVW), ("mem_q_norm", MEM_HD), ("mem_k_norm", MEM_HD)):
        grads[name] = tot[off:off + size]
        off += size
    cw = conv_w.shape[1]
    grads["conv_w"] = lax.dynamic_slice(grads["conv_w"].reshape(3, CONVW), (0, chip * cw), (3, cw))
    for name in SMALL:
        grads[name] = grads[name].reshape(w[name].shape)

    delta, new_m, new_v = {}, {}, {}
    for name, _ in BIG[1:]:
        delta[name], new_m[name], new_v[name], grads[name] = _adamw(w[name], grads[name], m[name], v[name],
                                                                    "adamw_" + name, with_grad=True)

    def packed(t):
        return jnp.concatenate([t[name].reshape(1, -1) for name in SMALL], axis=1)

    ds, ms, vs = _adamw(packed(w), packed(grads), packed(m), packed(v), "adamw_small")
    shared, = _wait_copies("share_w_in_wait", wsend, wrecv, [red_w_in], share_w_in, ds)
    delta["w_in"], new_m["w_in"], new_v["w_in"], grads["w_in"] = _adamw(w["w_in"], shared, m["w_in"], v["w_in"],
                                                                        "adamw_w_in", with_grad=True)
    off = 0
    for name in SMALL:
        size = w[name].size
        delta[name] = ds[0, off:off + size].reshape(w[name].shape)
        new_m[name] = ms[0, off:off + size].reshape(w[name].shape)
        new_v[name] = vs[0, off:off + size].reshape(w[name].shape)
        off += size

    return (loss, grad_x, *[grads[n] for n in WEIGHTS], *[delta[n] for n in WEIGHTS],
            *[new_m[n] for n in WEIGHTS], *[new_v[n] for n in WEIGHTS])
```

```python
import functools

import jax
import jax.numpy as jnp
from jax import lax
from jax.experimental import pallas as pl
from jax.experimental.pallas import tpu as pltpu

F32 = jnp.float32
MXU_DTYPE = jnp.bfloat16
WIRE_DTYPE = jnp.bfloat16
PROJ_DTYPE = jnp.bfloat16
EPS = 1e-6
NEG = -1e30

HEAD = 128
HPG = 4
GW = HPG * HEAD
DILATIONS = (1, 4, 16)
NGROUP = len(DILATIONS)
BLK = 128
QKV = NGROUP * GW
CONVW = 1024
MEM_HEADS = 4
MEM_HD = 256
MEMW = MEM_HEADS * MEM_HD
Q0, K0, V0 = 0, QKV, 2 * QKV
ZA = 3 * QKV
CB, CC, CV, ZC = ZA + GW, ZA + GW + CONVW, ZA + GW + 2 * CONVW, ZA + GW + 3 * CONVW
MQ = ZC + CONVW
ZM = MQ + MEMW
G0 = ZM + MEMW

ADAM_LR, ADAM_B1, ADAM_B2, ADAM_EPS, ADAM_WD, ADAM_STEP = 0.001, 0.9, 0.999, 1e-08, 0.01, 10

VMEM_LIMIT = 56 * 1024 * 1024
MESH = pl.DeviceIdType.MESH
ANY = pl.BlockSpec(memory_space=pl.ANY)


def _tile(n, pref, mult=128):
    t = min(pref, n)
    while t > mult and (n % t or t % mult):
        t -= mult
    assert n % t == 0, (n, pref)
    return t


def _call(body, *, name, out_shape, grid=(), in_specs=None, out_specs=None, scratch_shapes=(),
          aliases=None, grid_spec=None):
    kw = {}
    if grid_spec is not None:
        kw["grid_spec"] = grid_spec
        ngrid = len(grid_spec.grid)
    else:
        kw.update(grid=grid, in_specs=in_specs, out_specs=out_specs, scratch_shapes=list(scratch_shapes))
        ngrid = len(grid)
    params = pltpu.CompilerParams(dimension_semantics=("arbitrary",) * ngrid, vmem_limit_bytes=VMEM_LIMIT)
    return pl.pallas_call(body, name=name, out_shape=out_shape, compiler_params=params,
                          input_output_aliases=aliases or {}, **kw)


_DIMS = {"nn": (((1,), (0,)), ((), ())), "nt": (((1,), (1,)), ((), ())), "tn": (((0,), (0,)), ((), ()))}


def _mxu(a, b, mode):
    return lax.dot_general(a.astype(MXU_DTYPE), b.astype(MXU_DTYPE), _DIMS[mode], preferred_element_type=F32)


@functools.partial(jax.custom_vjp, nondiff_argnums=(2,))
def _dot(a, b, mode):
    return _mxu(a, b, mode)


def _dot_fwd(a, b, mode):
    return _mxu(a, b, mode), (a, b)


def _dot_bwd(mode, res, g):
    a, b = res
    if mode == "nn":
        return _mxu(g, b, "nt"), _mxu(a, g, "tn")
    if mode == "nt":
        return _mxu(g, b, "nn"), _mxu(g, a, "tn")
    return _mxu(b, g, "nt"), _mxu(a, g, "nn")


_dot.defvjp(_dot_fwd, _dot_bwd)


def _sig(z):
    return 1.0 / (1.0 + jnp.exp(-z))


def _silu(z):
    return z * _sig(z)


def _rms_rows(t, g):
    return t * lax.rsqrt(jnp.mean(t * t, axis=-1, keepdims=True) + EPS) * g


def _attn_block(q, k2, v2, gq, gk, first):
    qn = _rms_rows(q, gq)
    kn = _rms_rows(k2, gk)
    s = jnp.where(_band_mask(first, k2.shape[0]), _mxu(qn, kn, "nt") * (HEAD ** -0.5), NEG)
    m = jnp.max(s, axis=-1, keepdims=True)
    p = jnp.exp(s - m)
    den = jnp.sum(p, axis=-1, keepdims=True)
    o = _mxu(p, v2, "nn") / den
    return o, m + jnp.log(den)


def _band_mask(first, nkeys):
    a = lax.broadcasted_iota(jnp.int32, (BLK, nkeys), 0)
    b = lax.broadcasted_iota(jnp.int32, (BLK, nkeys), 1)
    if nkeys == BLK:
        return b <= a
    return (b >= a) & (b <= a + BLK) & (b >= jnp.where(first, BLK, 0))


def _norm_parts(t):
    r = lax.rsqrt(jnp.mean(t * t, axis=-1, keepdims=True) + EPS)
    return r, t * r


def _norm_bwd(dn, g, r, th):
    dth = dn * g
    return r * (dth - th * jnp.mean(dth * th, axis=-1, keepdims=True)), jnp.sum(dn * th, axis=0, keepdims=True)


def _attn_block_bwd(q, k2, v2, gq, gk, first, do, o, lse, dlse):
    scale = HEAD ** -0.5
    rq, qh = _norm_parts(q)
    rk, kh = _norm_parts(k2)
    qn, kn = qh * gq, kh * gk
    s = jnp.where(_band_mask(first, k2.shape[0]), _mxu(qn, kn, "nt") * scale, NEG)
    p = jnp.exp(s - lse)
    ds = p * (_mxu(do, v2, "nt") + (dlse - jnp.sum(do * o, axis=-1, keepdims=True))) * scale
    dq, dgq = _norm_bwd(_mxu(ds, kn, "nn"), gq, rq, qh)
    dk2, dgk = _norm_bwd(_mxu(ds, qn, "tn"), gk, rk, kh)
    return dq, dk2, _mxu(p, do, "tn"), dgq, dgk


def _combine(o1, o2, o3, l1, l2, l3, z):
    m = lax.stop_gradient(jnp.maximum(jnp.maximum(l1, l2), l3))
    e1, e2, e3 = jnp.exp(l1 - m), jnp.exp(l2 - m), jnp.exp(l3 - m)
    return (e1 * o1 + e2 * o2 + e3 * o3) / (e1 + e2 + e3) * _silu(z)


def _mem_block(q, z, kv, gq, gk):
    outs = []
    for h in range(MEM_HEADS):
        sl = slice(h * MEM_HD, (h + 1) * MEM_HD)
        qn = _rms_rows(q[:, sl], gq)
        kn = _rms_rows(kv[:, sl], gk)
        s = _dot(qn, kn, "nt") * (MEM_HD ** -0.5)
        m = lax.stop_gradient(jnp.max(s, axis=-1, keepdims=True))
        p = jnp.exp(s - m)
        den = jnp.sum(p, axis=-1, keepdims=True)
        outs.append(_dot(p, kv[:, MEMW + h * MEM_HD:MEMW + (h + 1) * MEM_HD], "nn") / den)
    return jnp.concatenate(outs, axis=-1) * _silu(z)


def _cast(w, name):
    R, C = w.shape
    tr, tc = _tile(R, 512, 8), _tile(C, 2176)

    def body(w_ref, o_ref):
        o_ref[...] = w_ref[...].astype(o_ref.dtype)

    spec = pl.BlockSpec((tr, tc), lambda i, j: (i, j))
    return _call(body, name=name, grid=(R // tr, C // tc), in_specs=[spec], out_specs=spec,
                 out_shape=jax.ShapeDtypeStruct((R, C), WIRE_DTYPE))(w)


def _place_shard(w, kind, pos, dtype, name, slot=0, into=None, half=None):
    R, C = w.shape
    tr, tc = _tile(R, 512, 8), _tile(C if half is None else C // 2, 2176)
    nr, nc = R // tr, C // tc
    ncols = nc if half is None else nc // 2

    def body(pos_ref, w_ref, *rest):
        rest[-1][...] = w_ref[...].astype(rest[-1].dtype)

    def col(j, pos_ref):
        if half is None:
            return j
        return (pos_ref[1] if half == 0 else 1 - pos_ref[1]) * ncols + j

    if kind == "col":
        full = (R, 4 * C)
        out = pl.BlockSpec((tr, tc), lambda i, j, pos_ref: (i, pos_ref[slot] * nc + col(j, pos_ref)))
    else:
        full, out = (4 * R, C), pl.BlockSpec((tr, tc), lambda i, j, pos_ref: (pos_ref[slot] * nr + i, j))
    in_specs, args = [pl.BlockSpec((tr, tc), lambda i, j, pos_ref: (i, col(j, pos_ref)))], [pos, w]
    if into is not None:
        in_specs.append(ANY)
        args.append(into)
    spec = pltpu.PrefetchScalarGridSpec(num_scalar_prefetch=1, grid=(nr, ncols), in_specs=in_specs, out_specs=out)
    return _call(body, name=name, grid_spec=spec, out_shape=jax.ShapeDtypeStruct(full, dtype),
                 aliases={} if into is None else {2: 0})(*args)


def _matmul(a, b, mode, out_dtype, *, name, tm=512, tn=512, tk=512):
    if mode == "nn":
        (M, K), (_, N) = a.shape, b.shape
    elif mode == "nt":
        (M, K), (N, _) = a.shape, b.shape
    else:
        (K, M), (_, N) = a.shape, b.shape
    tm, tn, tk = _tile(M, tm), _tile(N, tn), _tile(K, tk)
    nk = K // tk

    def body(a_ref, b_ref, o_ref, *acc):
        part = lax.dot_general(a_ref[...], b_ref[...], _DIMS[mode], preferred_element_type=F32)
        if nk == 1:
            o_ref[...] = part.astype(o_ref.dtype)
            return
        acc_ref, = acc
        k = pl.program_id(2)

        @pl.when(k == 0)
        def _():
            acc_ref[...] = part

        @pl.when(k > 0)
        def _():
            acc_ref[...] += part

        @pl.when(k == nk - 1)
        def _():
            o_ref[...] = acc_ref[...].astype(o_ref.dtype)

    a_spec = pl.BlockSpec((tk, tm), lambda i, j, k: (k, i)) if mode == "tn" else pl.BlockSpec((tm, tk), lambda i, j, k: (i, k))
    b_spec = pl.BlockSpec((tn, tk), lambda i, j, k: (j, k)) if mode == "nt" else pl.BlockSpec((tk, tn), lambda i, j, k: (k, j))
    return _call(body, name=name, grid=(M // tm, N // tn, nk), in_specs=[a_spec, b_spec],
                 out_specs=pl.BlockSpec((tm, tn), lambda i, j, k: (i, j)),
                 out_shape=jax.ShapeDtypeStruct((M, N), out_dtype),
                 scratch_shapes=[] if nk == 1 else [pltpu.VMEM((tm, tn), F32)])(a, b)


def _rms_fwd(x, g, name):
    R, D = x.shape
    tr = _tile(R, 512)

    def body(x_ref, g_ref, o_ref, t_ref):
        y = _rms_rows(x_ref[...], g_ref[...])
        o_ref[...] = y.astype(o_ref.dtype)
        t_ref[...] = y.T.astype(t_ref.dtype)

    row = pl.BlockSpec((tr, D), lambda i: (i, 0))
    return _call(body, name=name, grid=(R // tr,), in_specs=[row, pl.BlockSpec((1, D), lambda i: (0, 0))],
                 out_specs=[row, pl.BlockSpec((D, tr), lambda i: (0, i))],
                 out_shape=[jax.ShapeDtypeStruct((R, D), MXU_DTYPE), jax.ShapeDtypeStruct((D, R), MXU_DTYPE)])(x, g)


def _rms_bwd(x, dh, g, dy, name):
    R, D = x.shape
    tr = _tile(R, 256)
    with_dx = dy is not None

    def body(*refs):
        if with_dx:
            x_ref, dh_ref, g_ref, dy_ref, dx_ref, dg_ref = refs
        else:
            x_ref, dh_ref, g_ref, dg_ref = refs
        xv, dhv = x_ref[...], dh_ref[...]
        r = lax.rsqrt(jnp.mean(xv * xv, axis=-1, keepdims=True) + EPS)
        xh = xv * r

        @pl.when(pl.program_id(0) == 0)
        def _():
            dg_ref[...] = jnp.zeros_like(dg_ref)

        dg_ref[...] += jnp.sum(dhv * xh, axis=0, keepdims=True)
        if with_dx:
            dxh = dhv * g_ref[...]
            dx_ref[...] = dy_ref[...] + r * (dxh - xh * jnp.mean(dxh * xh, axis=-1, keepdims=True))

    row = pl.BlockSpec((tr, D), lambda i: (i, 0))
    vec = pl.BlockSpec((1, D), lambda i: (0, 0))
    dg_shape = jax.ShapeDtypeStruct((1, D), F32)
    if with_dx:
        return _call(body, name=name, grid=(R // tr,), in_specs=[row, row, vec, row], out_specs=[row, vec],
                     out_shape=[jax.ShapeDtypeStruct((R, D), F32), dg_shape])(x, dh, g, dy)
    return None, _call(body, name=name, grid=(R // tr,), in_specs=[row, row, vec], out_specs=vec,
                       out_shape=dg_shape)(x, dh, g)


def _attn_geom(g, d):
    hc = HPG if d == 1 else 1
    cw = hc * HEAD
    cq, ck, cv = (Q0 + g * GW) // cw, (K0 + g * GW) // cw, (V0 + g * GW) // cw
    return (1, BLK * d, cw), hc, HPG // hc, cq, ck, cv


def _rows(ref, r, d, sl):
    if d == 1:
        return ref[0, :, sl]
    return ref.at[0][pl.ds(r, BLK, stride=d), sl]


def _set_rows(ref, r, d, sl, val):
    if d == 1:
        ref[0, :, sl] = val
    else:
        ref.at[0][pl.ds(r, BLK, stride=d), sl] = val


def _stage_rows(ref, r, d, sl, val):
    if d == 1:
        ref[:, sl] = val
    else:
        ref[pl.ds(r, BLK, stride=d), sl] = val


def _proj_stages(blk, d):
    return [] if d == 1 else [pltpu.VMEM(blk[1:], F32)] * 5


def _proj_rows(refs, stages, d):
    if d == 1:
        return [lambda r, sl, ref=ref: ref[0, :, sl].astype(F32) for ref in refs]
    for ref, stage in zip(refs, stages):
        stage[...] = ref[0].astype(F32)
    return [lambda r, sl, stage=stage: stage[pl.ds(r, BLK, stride=d), sl] for stage in stages]


def _attn_fwd(proj3, gq, gk, g, d):
    Bl, S, _ = proj3.shape
    blk, hc, ncb, cq, ck, cv = _attn_geom(g, d)
    nb = S // blk[1]
    if nb == 1:
        return _attn_single_fwd(proj3, gq, gk, g, d)

    def body(q_ref, kp_ref, kc_ref, vp_ref, vc_ref, gq_ref, gk_ref, o_ref, lse_ref, *stages):
        first = pl.program_id(2) == 0
        q, kp, kc, vp, vc = _proj_rows((q_ref, kp_ref, kc_ref, vp_ref, vc_ref), stages, d)
        for r in range(d):
            for h in range(hc):
                sl = slice(h * HEAD, (h + 1) * HEAD)
                k2 = jnp.concatenate([kp(r, sl), kc(r, sl)], axis=0)
                v2 = jnp.concatenate([vp(r, sl), vc(r, sl)], axis=0)
                o, lse = _attn_block(q(r, sl), k2, v2, gq_ref[...], gk_ref[...], first)
                _set_rows(o_ref, r, d, sl, o)
                _set_rows(lse_ref, r, d, sl, jnp.broadcast_to(lse, (BLK, HEAD)))

    def cur(c0):
        return pl.BlockSpec(blk, lambda b, j, i: (b, i, c0 + j))

    def prev(c0):
        return pl.BlockSpec(blk, lambda b, j, i: (b, jnp.maximum(i - 1, 0), c0 + j))

    vec = pl.BlockSpec((1, HEAD), lambda b, j, i: (0, 0))
    out = pl.BlockSpec(blk, lambda b, j, i: (b, i, j))
    shp = jax.ShapeDtypeStruct((Bl, S, GW), F32)
    return _call(body, name=f"attn_fwd_g{g}", grid=(Bl, ncb, nb),
                 in_specs=[cur(cq), prev(ck), cur(ck), prev(cv), cur(cv), vec, vec],
                 out_specs=[out, out], out_shape=[shp, shp], scratch_shapes=_proj_stages(blk, d),
                 )(proj3, proj3, proj3, proj3, proj3, gq, gk)


def _attn_single_fwd(proj3, gq, gk, g, d):
    Bl, S, _ = proj3.shape
    blk, hc, ncb, cq, ck, cv = _attn_geom(g, d)

    def body(q_ref, k_ref, v_ref, gq_ref, gk_ref, o_ref, lse_ref, *stages):
        q, k, v = _proj_rows((q_ref, k_ref, v_ref), stages, d)
        for r in range(d):
            for h in range(hc):
                sl = slice(h * HEAD, (h + 1) * HEAD)
                o, lse = _attn_block(q(r, sl), k(r, sl), v(r, sl), gq_ref[...], gk_ref[...], True)
                _set_rows(o_ref, r, d, sl, o)
                _set_rows(lse_ref, r, d, sl, jnp.broadcast_to(lse, (BLK, HEAD)))

    def at(c0):
        return pl.BlockSpec(blk, lambda b, j: (b, 0, c0 + j))

    vec = pl.BlockSpec((1, HEAD), lambda b, j: (0, 0))
    shp = jax.ShapeDtypeStruct((Bl, S, GW), F32)
    return _call(body, name=f"attn_fwd_g{g}", grid=(Bl, ncb), in_specs=[at(cq), at(ck), at(cv), vec, vec],
                 out_specs=[at(0), at(0)], out_shape=[shp, shp], scratch_shapes=_proj_stages(blk, d)[:3],
                 )(proj3, proj3, proj3, gq, gk)


def _attn_single_bwd(proj3, gq, gk, o3, l3, do3, dl3, g, d):
    Bl, S, _ = proj3.shape
    blk, hc, ncb, cq, ck, cv = _attn_geom(g, d)

    def body(q_ref, k_ref, v_ref, gq_ref, gk_ref, o_ref, l_ref, do_ref, dl_ref,
             dq_ref, dk_ref, dv_ref, dgq_ref, dgk_ref, sq_ref, sk_ref, sv_ref, *stages):
        @pl.when((pl.program_id(0) == 0) & (pl.program_id(1) == 0))
        def _():
            dgq_ref[...] = jnp.zeros_like(dgq_ref)
            dgk_ref[...] = jnp.zeros_like(dgk_ref)

        dgq, dgk = jnp.zeros((1, HEAD), F32), jnp.zeros((1, HEAD), F32)
        q, k, v = _proj_rows((q_ref, k_ref, v_ref), stages, d)
        for r in range(d):
            for h in range(hc):
                sl = slice(h * HEAD, (h + 1) * HEAD)
                dq, dk, dv, a, b = _attn_block_bwd(
                    q(r, sl), k(r, sl), v(r, sl), gq_ref[...], gk_ref[...], True, _rows(do_ref, r, d, sl),
                    _rows(o_ref, r, d, sl), _rows(l_ref, r, d, sl)[:, :1], _rows(dl_ref, r, d, sl)[:, :1])
                _stage_rows(sq_ref, r, d, sl, dq)
                _stage_rows(sk_ref, r, d, sl, dk)
                _stage_rows(sv_ref, r, d, sl, dv)
                dgq, dgk = dgq + a, dgk + b
        dgq_ref[...] += dgq
        dgk_ref[...] += dgk
        dq_ref[0] = sq_ref[...].astype(dq_ref.dtype)
        dk_ref[0] = sk_ref[...].astype(dk_ref.dtype)
        dv_ref[0] = sv_ref[...].astype(dv_ref.dtype)

    def at(c0):
        return pl.BlockSpec(blk, lambda b, j: (b, 0, c0 + j))

    vec = pl.BlockSpec((1, HEAD), lambda b, j: (0, 0))
    shp = jax.ShapeDtypeStruct((Bl, S, GW), MXU_DTYPE)
    gshp = jax.ShapeDtypeStruct((1, HEAD), F32)
    return _call(body, name=f"attn_bwd_g{g}", grid=(Bl, ncb),
                 in_specs=[at(cq), at(ck), at(cv), vec, vec, at(0), at(0), at(0), at(0)],
                 out_specs=[at(0), at(0), at(0), vec, vec], out_shape=[shp, shp, shp, gshp, gshp],
                 scratch_shapes=[pltpu.VMEM(blk[1:], F32)] * 3 + _proj_stages(blk, d)[:3],
                 )(proj3, proj3, proj3, gq, gk, o3, l3, do3, dl3)


def _attn_bwd(proj3, gq, gk, o3, l3, do3, dl3, g, d):
    Bl, S, _ = proj3.shape
    blk, hc, ncb, cq, ck, cv = _attn_geom(g, d)
    nb = S // blk[1]
    if nb == 1:
        return _attn_single_bwd(proj3, gq, gk, o3, l3, do3, dl3, g, d)

    def body(q_ref, kp_ref, kc_ref, vp_ref, vc_ref, gq_ref, gk_ref, o_ref, l_ref, do_ref, dl_ref,
             dq_ref, dk_ref, dv_ref, dgq_ref, dgk_ref, ck_ref, cv_ref, sq_ref, sk_ref, sv_ref, *stages):
        i = pl.program_id(2)
        first = i == 0

        @pl.when((pl.program_id(0) == 0) & (pl.program_id(1) == 0) & first)
        def _():
            dgq_ref[...] = jnp.zeros_like(dgq_ref)
            dgk_ref[...] = jnp.zeros_like(dgk_ref)

        @pl.when(first)
        def _():
            ck_ref[...] = jnp.zeros_like(ck_ref)
            cv_ref[...] = jnp.zeros_like(cv_ref)

        @pl.when(i < nb)
        def _():
            dgq, dgk = jnp.zeros((1, HEAD), F32), jnp.zeros((1, HEAD), F32)
            q, kp, kc, vp, vc = _proj_rows((q_ref, kp_ref, kc_ref, vp_ref, vc_ref), stages, d)
            for r in range(d):
                rs = slice(r * BLK, (r + 1) * BLK)
                for h in range(hc):
                    sl = slice(h * HEAD, (h + 1) * HEAD)
                    k2 = jnp.concatenate([kp(r, sl), kc(r, sl)], axis=0)
                    v2 = jnp.concatenate([vp(r, sl), vc(r, sl)], axis=0)
                    dq, dk2, dv2, a, b = _attn_block_bwd(
                        q(r, sl), k2, v2, gq_ref[...], gk_ref[...], first, _rows(do_ref, r, d, sl),
                        _rows(o_ref, r, d, sl), _rows(l_ref, r, d, sl)[:, :1], _rows(dl_ref, r, d, sl)[:, :1])
                    _stage_rows(sq_ref, r, d, sl, dq)
                    _stage_rows(sk_ref, r, d, sl, ck_ref[rs, sl] + dk2[:BLK])
                    _stage_rows(sv_ref, r, d, sl, cv_ref[rs, sl] + dv2[:BLK])
                    ck_ref[rs, sl] = dk2[BLK:]
                    cv_ref[rs, sl] = dv2[BLK:]
                    dgq, dgk = dgq + a, dgk + b
            dgq_ref[...] += dgq
            dgk_ref[...] += dgk
            dq_ref[0] = sq_ref[...].astype(dq_ref.dtype)

        @pl.when(i == nb)
        def _():
            for r in range(d):
                rs = slice(r * BLK, (r + 1) * BLK)
                _stage_rows(sk_ref, r, d, slice(None), ck_ref[rs, :])
                _stage_rows(sv_ref, r, d, slice(None), cv_ref[rs, :])

        dk_ref[0] = sk_ref[...].astype(dk_ref.dtype)
        dv_ref[0] = sv_ref[...].astype(dv_ref.dtype)

    def cur(c0):
        return pl.BlockSpec(blk, lambda b, j, i: (b, jnp.minimum(i, nb - 1), c0 + j))

    def prev(c0):
        return pl.BlockSpec(blk, lambda b, j, i: (b, jnp.clip(i - 1, 0, nb - 1), c0 + j))

    vec = pl.BlockSpec((1, HEAD), lambda b, j, i: (0, 0))
    at_q = pl.BlockSpec(blk, lambda b, j, i: (b, jnp.minimum(i, nb - 1), j))
    at_k = pl.BlockSpec(blk, lambda b, j, i: (b, jnp.maximum(i - 1, 0), j))
    shp = jax.ShapeDtypeStruct((Bl, S, GW), MXU_DTYPE)
    gshp = jax.ShapeDtypeStruct((1, HEAD), F32)
    return _call(body, name=f"attn_bwd_g{g}", grid=(Bl, ncb, nb + 1),
                 in_specs=[cur(cq), prev(ck), cur(ck), prev(cv), cur(cv), vec, vec, at_q, at_q, at_q, at_q],
                 out_specs=[at_q, at_k, at_k, vec, vec], out_shape=[shp, shp, shp, gshp, gshp],
                 scratch_shapes=[pltpu.VMEM(blk[1:], F32)] * 5 + _proj_stages(blk, d),
                 )(proj3, proj3, proj3, proj3, proj3, gq, gk, o3, l3, do3, dl3)


def _combine_fwd(os, ls, proj2):
    T = proj2.shape[0]
    tr = _tile(T, 512)

    def body(o1, o2, o3, l1, l2, l3, z, a_ref, at_ref):
        a = _combine(o1[...], o2[...], o3[...], l1[...], l2[...], l3[...], z[...].astype(F32))
        a_ref[...] = a.astype(a_ref.dtype)
        at_ref[...] = a.T.astype(at_ref.dtype)

    row = pl.BlockSpec((tr, GW), lambda i: (i, 0))
    return _call(body, name="combine_fwd", grid=(T // tr,),
                 in_specs=[row] * 6 + [pl.BlockSpec((tr, GW), lambda i: (i, ZA // GW))],
                 out_specs=[row, pl.BlockSpec((GW, tr), lambda i: (0, i))],
                 out_shape=[jax.ShapeDtypeStruct((T, GW), MXU_DTYPE), jax.ShapeDtypeStruct((GW, T), MXU_DTYPE)],
                 )(*os, *ls, proj2)


def _combine_bwd(os, ls, proj2, da, dproj):
    T = proj2.shape[0]
    tr = _tile(T, 256)

    def body(o1, o2, o3, l1, l2, l3, z, da_ref, _, d1, d2, d3, e1, e2, e3, dz_ref):
        _, vjp = jax.vjp(_combine, o1[...], o2[...], o3[...], l1[...], l2[...], l3[...], z[...].astype(F32))
        go1, go2, go3, gl1, gl2, gl3, gz = vjp(da_ref[...])
        d1[...], d2[...], d3[...] = go1, go2, go3
        dz_ref[...] = gz.astype(dz_ref.dtype)
        for ref, gl in ((e1, gl1), (e2, gl2), (e3, gl3)):
            for h in range(HPG):
                sl = slice(h * HEAD, (h + 1) * HEAD)
                ref[:, sl] = jnp.broadcast_to(jnp.sum(gl[:, sl], axis=-1, keepdims=True), (tr, HEAD))

    row = pl.BlockSpec((tr, GW), lambda i: (i, 0))
    f = jax.ShapeDtypeStruct((T, GW), F32)
    z_attn = pl.BlockSpec((tr, GW), lambda i: (i, ZA // GW))
    outs = _call(body, name="combine_bwd", grid=(T // tr,), in_specs=[row] * 6 + [z_attn, row, ANY],
                 out_specs=[row] * 6 + [z_attn], out_shape=[f] * 6 + [jax.ShapeDtypeStruct(dproj.shape, dproj.dtype)],
                 aliases={8: 6})(*os, *ls, proj2, da, dproj)
    return outs[:3], outs[3:6], outs[6]


def _shift_down(u, j, t):
    return jnp.where(t >= j, pltpu.roll(u, j, 0), 0.0)


def _shift_up(u, j, t):
    n = u.shape[0]
    return jnp.where(t < n - j, pltpu.roll(u, n - j, 0), 0.0)


def _conv_specs(Bl, S, cw):
    def sec(c0):
        return pl.BlockSpec((1, S, cw), lambda j, b: (b, 0, c0 // cw + j))
    return [sec(CB), sec(CC), sec(CV), sec(ZC)], pl.BlockSpec((3, cw), lambda j, b: (0, j))


def _conv_fwd(proj3, conv_w):
    Bl, S, _ = proj3.shape
    cw = 256
    secs, wspec = _conv_specs(Bl, S, cw)

    def body(b_ref, c_ref, v_ref, z_ref, w_ref, o_ref, ot_ref):
        t = lax.broadcasted_iota(jnp.int32, (S, cw), 0)
        u = c_ref[0].astype(F32) * v_ref[0].astype(F32)
        y = w_ref[0:1, :] * u + w_ref[1:2, :] * _shift_down(u, 1, t) + w_ref[2:3, :] * _shift_down(u, 2, t)
        out = b_ref[0].astype(F32) * y * _silu(z_ref[0].astype(F32))
        o_ref[0] = out.astype(o_ref.dtype)
        ot_ref[...] = out.T.astype(ot_ref.dtype)

    return _call(body, name="conv_fwd", grid=(CONVW // cw, Bl), in_specs=secs + [wspec],
                 out_specs=[pl.BlockSpec((1, S, cw), lambda j, b: (b, 0, j)), pl.BlockSpec((cw, S), lambda j, b: (j, b))],
                 out_shape=[jax.ShapeDtypeStruct((Bl, S, CONVW), MXU_DTYPE),
                            jax.ShapeDtypeStruct((CONVW, Bl * S), MXU_DTYPE)])(proj3, proj3, proj3, proj3, conv_w)


def _conv_bwd(proj3, conv_w, dcc3, dproj):
    Bl, S, _ = proj3.shape
    cw = 256
    secs, wspec = _conv_specs(Bl, S, cw)

    def body(b_ref, c_ref, v_ref, z_ref, w_ref, d_ref, _, dproj_ref, dw_ref, stage, sems):
        t = lax.broadcasted_iota(jnp.int32, (S, cw), 0)
        bv, cv, vv, zv = (r[0].astype(F32) for r in (b_ref, c_ref, v_ref, z_ref))
        dv = d_ref[0]
        u = cv * vv
        u1, u2 = _shift_down(u, 1, t), _shift_down(u, 2, t)
        y = w_ref[0:1, :] * u + w_ref[1:2, :] * u1 + w_ref[2:3, :] * u2
        sg = _sig(zv)
        sz = zv * sg
        gy = dv * bv * sz
        du = w_ref[0:1, :] * gy + w_ref[1:2, :] * _shift_up(gy, 1, t) + w_ref[2:3, :] * _shift_up(gy, 2, t)
        j, b = pl.program_id(0), pl.program_id(1)
        tiles = [dv * y * sz, du * vv, du * cv, dv * bv * y * sg * (1.0 + zv * (1.0 - sg))]
        dsts = [dproj_ref.at[pl.ds(b * S, S), pl.ds(c0 + j * cw, cw)] for c0 in (CB, CC, CV, ZC)]
        _emit_tiles(j * Bl + b, (CONVW // cw) * Bl, tiles, dsts, stage, sems)

        @pl.when(pl.program_id(1) == 0)
        def _():
            dw_ref[...] = jnp.zeros_like(dw_ref)

        dw_ref[0:1, :] += jnp.sum(gy * u, axis=0, keepdims=True)
        dw_ref[1:2, :] += jnp.sum(gy * u1, axis=0, keepdims=True)
        dw_ref[2:3, :] += jnp.sum(gy * u2, axis=0, keepdims=True)

    blk = pl.BlockSpec((1, S, cw), lambda j, b: (b, 0, j))
    return _call(body, name="conv_bwd", grid=(CONVW // cw, Bl), in_specs=secs + [wspec, blk, ANY],
                 out_specs=[ANY, wspec],
                 out_shape=[jax.ShapeDtypeStruct(dproj.shape, dproj.dtype), jax.ShapeDtypeStruct((3, CONVW), F32)],
                 scratch_shapes=_emit_scratch(4, S, cw), aliases={6: 0})(proj3, proj3, proj3, proj3, conv_w, dcc3, dproj)


def _mem_specs(S, tq):
    q = pl.BlockSpec((1, tq, MEMW), lambda b, j: (b, j, MQ // MEMW))
    z = pl.BlockSpec((1, tq, MEMW), lambda b, j: (b, j, ZM // MEMW))
    kv = pl.BlockSpec((1, MEM_HD, 2 * MEMW), lambda b, j: (b, 0, 0))
    vec = pl.BlockSpec((1, MEM_HD), lambda b, j: (0, 0))
    blk = pl.BlockSpec((1, tq, MEMW), lambda b, j: (b, j, 0))
    return q, z, kv, vec, blk


def _mem_fwd(proj3, mkv3, gq, gk):
    Bl, S, _ = proj3.shape
    tq = _tile(S, 512)
    q, z, kv, vec, blk = _mem_specs(S, tq)

    def body(q_ref, z_ref, kv_ref, gq_ref, gk_ref, o_ref, ot_ref):
        out = _mem_block(q_ref[0].astype(F32), z_ref[0].astype(F32), kv_ref[0], gq_ref[...], gk_ref[...])
        o_ref[0] = out.astype(o_ref.dtype)
        ot_ref[...] = out.T.astype(ot_ref.dtype)

    nq = S // tq
    return _call(body, name="mem_fwd", grid=(Bl, nq), in_specs=[q, z, kv, vec, vec],
                 out_specs=[blk, pl.BlockSpec((MEMW, tq), lambda b, j: (0, b * nq + j))],
                 out_shape=[jax.ShapeDtypeStruct((Bl, S, MEMW), MXU_DTYPE),
                            jax.ShapeDtypeStruct((MEMW, Bl * S), MXU_DTYPE)])(proj3, proj3, mkv3, gq, gk)


def _mem_bwd(proj3, mkv3, gq, gk, dmo3, dproj):
    Bl, S, _ = proj3.shape
    tq = _tile(S, 256)
    q, z, kv, vec, blk = _mem_specs(S, tq)
    nq = S // tq

    def body(q_ref, z_ref, kv_ref, gq_ref, gk_ref, d_ref, _, dproj_ref, dkv_ref, dgq_ref, dgk_ref, stage, sems):
        _, vjp = jax.vjp(_mem_block, q_ref[0].astype(F32), z_ref[0].astype(F32), kv_ref[0], gq_ref[...], gk_ref[...])
        dq, dz, dkv, dgq, dgk = vjp(d_ref[0])
        j = pl.program_id(1)
        rows = pl.ds(pl.program_id(0) * S + j * tq, tq)
        dsts = [dproj_ref.at[rows, pl.ds(MQ, MEMW)], dproj_ref.at[rows, pl.ds(ZM, MEMW)]]
        _emit_tiles(pl.program_id(0) * nq + j, Bl * nq, [dq, dz], dsts, stage, sems)

        @pl.when(j == 0)
        def _():
            dkv_ref[0] = jnp.zeros_like(dkv)

        @pl.when((j == 0) & (pl.program_id(0) == 0))
        def _():
            dgq_ref[...] = jnp.zeros_like(dgq_ref)
            dgk_ref[...] = jnp.zeros_like(dgk_ref)

        dkv_ref[0] += dkv
        dgq_ref[...] += dgq
        dgk_ref[...] += dgk

    gshp = jax.ShapeDtypeStruct((1, MEM_HD), F32)
    return _call(body, name="mem_bwd", grid=(Bl, nq), in_specs=[q, z, kv, vec, vec, blk, ANY],
                 out_specs=[ANY, kv, vec, vec],
                 out_shape=[jax.ShapeDtypeStruct(dproj.shape, dproj.dtype), jax.ShapeDtypeStruct(mkv3.shape, F32),
                            gshp, gshp],
                 scratch_shapes=_emit_scratch(2, tq, MEMW), aliases={6: 0})(proj3, proj3, mkv3, gq, gk, dmo3, dproj)


def _merge_specs(T, D, tm, tn):
    def act(w):
        return pl.BlockSpec((tm, w), lambda i, n: (i, 0))

    def wsp(w):
        return pl.BlockSpec((w, tn), lambda i, n: (0, n))

    gates = [pl.BlockSpec((tm, tn), lambda i, n, k=k: (i, (G0 + k * D) // tn + n)) for k in range(3)]
    tile = pl.BlockSpec((tm, tn), lambda i, n: (i, n))
    return act, wsp, gates, tile


def _merge_fwd(a, cc, mo, wa, wc, wm, proj2):
    T, D = a.shape[0], wa.shape[1]
    tm, tn = _tile(T, 1024), _tile(D, 512)
    act, wsp, gates, tile = _merge_specs(T, D, tm, tn)

    def body(a_ref, c_ref, m_ref, wa_ref, wc_ref, wm_ref, g0, g1, g2, mg_ref, mt_ref, pa_ref, pc_ref, pm_ref):
        pa = jnp.dot(a_ref[...], wa_ref[...], preferred_element_type=F32)
        pc = jnp.dot(c_ref[...], wc_ref[...], preferred_element_type=F32)
        pm = jnp.dot(m_ref[...], wm_ref[...], preferred_element_type=F32)
        mg = _sig(g0[...].astype(F32)) * pa + _sig(g1[...].astype(F32)) * pc + _sig(g2[...].astype(F32)) * pm
        mg_ref[...] = mg.astype(mg_ref.dtype)
        mt_ref[...] = mg.T.astype(mt_ref.dtype)
        pa_ref[...] = pa.astype(pa_ref.dtype)
        pc_ref[...] = pc.astype(pc_ref.dtype)
        pm_ref[...] = pm.astype(pm_ref.dtype)

    shp = jax.ShapeDtypeStruct((T, D), MXU_DTYPE)
    return _call(body, name="merge_fwd", grid=(T // tm, D // tn),
                 in_specs=[act(GW), act(CONVW), act(MEMW), wsp(GW), wsp(CONVW), wsp(MEMW)] + gates,
                 out_specs=[tile, pl.BlockSpec((tn, tm), lambda i, n: (n, i)), tile, tile, tile],
                 out_shape=[shp, jax.ShapeDtypeStruct((D, T), MXU_DTYPE), shp, shp, shp],
                 )(a, cc, mo, wa, wc, wm, proj2, proj2, proj2)


def _emit_tiles(step, nsteps, tiles, dsts, stage, sems):
    slot = step % 2

    def copies(s):
        return [pltpu.make_async_copy(stage.at[s, k], dsts[k], sems.at[s, k]) for k in range(len(tiles))]

    @pl.when(step >= 2)
    def _():
        for cp in copies(slot):
            cp.wait()

    for k, t in enumerate(tiles):
        stage[slot, k] = t.astype(stage.dtype)
    for cp in copies(slot):
        cp.start()

    @pl.when(step == nsteps - 1)
    def _():
        for cp in copies(slot):
            cp.wait()
        if nsteps > 1:
            for cp in copies(1 - slot):
                cp.wait()


def _emit_scratch(k, rows, cols):
    return [pltpu.VMEM((2, k, rows, cols), MXU_DTYPE), pltpu.SemaphoreType.DMA((2, k))]


def _merge_bwd(dyb, w_out, proj2, pa, pc, pm):
    T, D = dyb.shape
    IN = proj2.shape[1]
    tm, tn = _tile(T, 1024), _tile(D, 512)
    _, _, gates, tile = _merge_specs(T, D, tm, tn)
    nn = D // tn

    def body(dy_ref, w_ref, g0, g1, g2, p0, p1, p2, dp0, dp1, dp2, dproj_ref, stage, sems):
        i, n = pl.program_id(0), pl.program_id(1)
        dm = lax.dot_general(dy_ref[...], w_ref[...], _DIMS["nt"], preferred_element_type=F32)
        tiles, dsts = [], []
        for k, (g_ref, p_ref, dp_ref) in enumerate(((g0, p0, dp0), (g1, p1, dp1), (g2, p2, dp2))):
            gt = _sig(g_ref[...].astype(F32))
            dp_ref[...] = (gt * dm).astype(dp_ref.dtype)
            tiles.append(dm * p_ref[...].astype(F32) * gt * (1.0 - gt))
            dsts.append(dproj_ref.at[pl.ds(i * tm, tm), pl.ds(G0 + k * D + n * tn, tn)])
        _emit_tiles(i * nn + n, (T // tm) * nn, tiles, dsts, stage, sems)

    shp = jax.ShapeDtypeStruct((T, D), MXU_DTYPE)
    return _call(body, name="merge_bwd", grid=(T // tm, nn),
                 in_specs=[pl.BlockSpec((tm, D), lambda i, n: (i, 0)), pl.BlockSpec((tn, D), lambda i, n: (n, 0))]
                 + gates + [tile] * 3,
                 out_specs=[tile] * 3 + [ANY], out_shape=[shp] * 3 + [jax.ShapeDtypeStruct((T, IN), MXU_DTYPE)],
                 scratch_shapes=_emit_scratch(3, tm, tn))(dyb, w_out, proj2, proj2, proj2, pa, pc, pm)


def _out_loss(merged, w_out, x, tgt):
    T, D = x.shape
    tm = _tile(T, 512)

    def body(m_ref, w_ref, x_ref, t_ref, dy_ref, dyb_ref, loss_ref):
        err = x_ref[...] + jnp.dot(m_ref[...], w_ref[...], preferred_element_type=F32) - t_ref[...]
        dy = err * (1.0 / D)
        dy_ref[...] = dy
        dyb_ref[...] = dy.astype(dyb_ref.dtype)

        @pl.when(pl.program_id(0) == 0)
        def _():
            loss_ref[...] = jnp.zeros_like(loss_ref)

        loss_ref[...] += jnp.sum(err * err) * (0.5 / D)

    row = pl.BlockSpec((tm, D), lambda i: (i, 0))
    return _call(body, name="out_loss", grid=(T // tm,),
                 in_specs=[row, pl.BlockSpec((D, D), lambda i: (0, 0)), row, row],
                 out_specs=[row, row, pl.BlockSpec((1, 128), lambda i: (0, 0))],
                 out_shape=[jax.ShapeDtypeStruct((T, D), F32), jax.ShapeDtypeStruct((T, D), MXU_DTYPE),
                            jax.ShapeDtypeStruct((1, 128), F32)])(merged, w_out, x, tgt)


def _proj_chunk(hb, w, meta, j, nslots, half, buf, name):
    T, D = hb.shape
    Cs = w.shape[1] // 4
    tm, tn = _tile(T, 1024), _tile(Cs // 2, 2176)
    nh = Cs // 2 // tn
    per = nh if half is not None else 2 * nh

    def body(meta_ref, a_ref, b_ref, *rest):
        rest[-1][...] = jnp.dot(a_ref[...], b_ref[...], preferred_element_type=F32).astype(rest[-1].dtype)

    def tile(n, m):
        if half is None:
            return n % per
        return (m[4] if half == 0 else 1 - m[4]) * nh + n % per

    in_specs = [pl.BlockSpec((tm, D), lambda n, i, m: (i, 0)),
                pl.BlockSpec((D, tn), lambda n, i, m: (0, (j + n // per) * 2 * nh + tile(n, m)))]
    args = [meta, hb, w]
    if buf is not None:
        in_specs.append(ANY)
        args.append(buf)
    spec = pltpu.PrefetchScalarGridSpec(
        num_scalar_prefetch=1, grid=(nslots * per, T // tm), in_specs=in_specs,
        out_specs=pl.BlockSpec((tm, tn), lambda n, i, m: (i, m[j + n // per] * 2 * nh + tile(n, m))))
    return _call(body, name=name, grid_spec=spec, out_shape=jax.ShapeDtypeStruct((T, 4 * Cs), PROJ_DTYPE),
                 aliases={} if buf is None else {3: 0})(*args)


def _norms(x, mem, norm_g, mem_norm_g):
    D = x.shape[-1]
    hb, hbt = _rms_fwd(x.reshape(-1, D), norm_g.reshape(1, D), "rms_x")
    mhb, _ = _rms_fwd(mem.reshape(-1, D), mem_norm_g.reshape(1, D), "rms_mem")
    return hb, hbt, mhb


def _attention_fwd(proj2, Bl, gq_all, gk_all):
    T, IN = proj2.shape
    proj3 = proj2.reshape(Bl, T // Bl, IN)
    os, ls = [], []
    for g, d in enumerate(DILATIONS):
        o, l = _attn_fwd(proj3, gq_all[g:g + 1], gk_all[g:g + 1], g, d)
        os.append(o.reshape(T, GW))
        ls.append(l.reshape(T, GW))
    return os, ls, _combine_fwd(os, ls, proj2)


def _conv_branch_fwd(proj2, Bl, conv_w):
    T, IN = proj2.shape
    cc, cct = _conv_fwd(proj2.reshape(Bl, T // Bl, IN), conv_w)
    return cc.reshape(T, CONVW), cct


def _weight_grads(x, mem, tgt, norm_g, mem_norm_g, gq_all, gk_all, conv_w, mem_gq, mem_gk, W, pre, early=None):
    Bl, S, D = x.shape
    T = Bl * S
    hb, hbt, mhb, proj2, os, ls, (a, at), (cc, cct) = pre
    IN = proj2.shape[1]
    proj3 = proj2.reshape(Bl, S, IN)
    x2, tgt2 = x.reshape(T, D), tgt.reshape(T, D)
    mem2 = mem.reshape(-1, D)
    ng, mng = norm_g.reshape(1, D), mem_norm_g.reshape(1, D)
    mgq, mgk = mem_gq.reshape(1, MEM_HD), mem_gk.reshape(1, MEM_HD)
    gqs = [gq_all[g:g + 1] for g in range(NGROUP)]
    gks = [gk_all[g:g + 1] for g in range(NGROUP)]

    mkv = _matmul(mhb, W["mem_w_kv"], "nn", F32, name="mem_kv", tm=512, tn=1024, tk=D)
    mkv3 = mkv.reshape(Bl, -1, 2 * MEMW)
    mo, mot = _mem_fwd(proj3, mkv3, mgq, mgk)
    mo = mo.reshape(T, MEMW)
    merged, mergedt, pa, pc, pm = _merge_fwd(a, cc, mo, W["w_br_attn"], W["w_br_conv"], W["w_br_mem"], proj2)
    dy, dyb, loss = _out_loss(merged, W["w_out"], x2, tgt2)

    G = {}
    G["w_out"] = _matmul(mergedt, dyb, "nn", WIRE_DTYPE, name="dw_out", tm=1024, tn=512, tk=T)
    dpa, dpc, dpm, dproj = _merge_bwd(dyb, W["w_out"], proj2, pa, pc, pm)
    G["w_br_attn"] = _matmul(at, dpa, "nn", WIRE_DTYPE, name="dw_br_attn", tm=512, tn=512, tk=T)
    G["w_br_conv"] = _matmul(cct, dpc, "nn", WIRE_DTYPE, name="dw_br_conv", tm=1024, tn=512, tk=T)
    G["w_br_mem"] = _matmul(mot, dpm, "nn", WIRE_DTYPE, name="dw_br_mem", tm=1024, tn=512, tk=T)
    da = _matmul(dpa, W["w_br_attn"], "nt", F32, name="d_attn", tm=1024, tn=512, tk=D)
    dcc = _matmul(dpc, W["w_br_conv"], "nt", F32, name="d_conv", tm=1024, tn=1024, tk=D)
    dmo = _matmul(dpm, W["w_br_mem"], "nt", F32, name="d_mem", tm=1024, tn=1024, tk=D)
    dproj, dmkv3, dmgq, dmgk = _mem_bwd(proj3, mkv3, mgq, mgk, dmo.reshape(Bl, S, MEMW), dproj)
    dmkv = _cast(dmkv3.reshape(-1, 2 * MEMW), "cast_dmkv")
    G["mem_w_kv"] = _matmul(mhb, dmkv, "tn", WIRE_DTYPE, name="dw_mem_kv", tm=1024, tn=1024, tk=512)
    early_state, da = (None, da) if early is None else early(G, da)
    dmh = _matmul(dmkv, W["mem_w_kv"], "nt", F32, name="d_memh", tm=512, tn=1024, tk=2 * MEMW)
    _, dmng = _rms_bwd(mem2, dmh, mng, None, "rms_mem_bwd")

    dos, dls, dproj = _combine_bwd(os, ls, proj2, da, dproj)
    dgq, dgk = [], []
    for g, d in enumerate(DILATIONS):
        dq, dk, dv, gq_g, gk_g = _attn_bwd(proj3, gqs[g], gks[g], os[g].reshape(Bl, S, GW), ls[g].reshape(Bl, S, GW),
                                           dos[g].reshape(Bl, S, GW), dls[g].reshape(Bl, S, GW), g, d)
        for c0, part in ((Q0, dq), (K0, dk), (V0, dv)):
            dproj = lax.dynamic_update_slice(dproj, part.reshape(T, GW), (0, c0 + g * GW))
        dgq.append(gq_g)
        dgk.append(gk_g)
    dproj, dconv_w = _conv_bwd(proj3, conv_w, dcc.reshape(Bl, S, CONVW), dproj)
    small = [loss, None, dmng] + dgq + dgk + [dconv_w.reshape(1, 3 * CONVW), dmgq, dmgk]
    return G, (dproj, x2, ng, dy, small), early_state


def _dw_in_half(hbt, dproj, pos, own, name):
    D, T = hbt.shape
    IN = dproj.shape[1]
    R, tn = D // 2, _tile(IN, 1024)

    def body(pos_ref, a_ref, b_ref, o_ref):
        o_ref[...] = jnp.dot(a_ref[...], b_ref[...], preferred_element_type=F32).astype(o_ref.dtype)

    spec = pltpu.PrefetchScalarGridSpec(
        num_scalar_prefetch=1, grid=(IN // tn,),
        in_specs=[pl.BlockSpec((R, T), lambda j, p: (p[1] if own else 1 - p[1], 0)),
                  pl.BlockSpec((T, tn), lambda j, p: (0, j))],
        out_specs=pl.BlockSpec((R, tn), lambda j, p: (0, j)))
    return _call(body, name=name, grid_spec=spec, out_shape=jax.ShapeDtypeStruct((R, IN), WIRE_DTYPE))(pos, hbt, dproj)


def _d_h(dproj, w, order):
    T, IN = dproj.shape
    D, Cs = w.shape[0], IN // 4
    tm, tn = _tile(T, 1024), _tile(D, 1024)

    def body(order_ref, a_ref, b_ref, o_ref, acc_ref):
        part = lax.dot_general(a_ref[...], b_ref[...], _DIMS["nt"], preferred_element_type=F32)
        k = pl.program_id(2)

        @pl.when(k == 0)
        def _():
            acc_ref[...] = part

        @pl.when(k > 0)
        def _():
            acc_ref[...] += part

        @pl.when(k == 3)
        def _():
            o_ref[...] = acc_ref[...]

    spec = pltpu.PrefetchScalarGridSpec(
        num_scalar_prefetch=1, grid=(T // tm, D // tn, 4),
        in_specs=[pl.BlockSpec((tm, Cs), lambda i, n, k, o: (i, o[k])), pl.BlockSpec((tn, Cs), lambda i, n, k, o: (n, k))],
        out_specs=pl.BlockSpec((tm, tn), lambda i, n, k, o: (i, n)), scratch_shapes=[pltpu.VMEM((tm, tn), F32)])
    return _call(body, name="d_h", grid_spec=spec, out_shape=jax.ShapeDtypeStruct((T, D), F32))(order, dproj, w)


def _input_grad(rest, w_in, order):
    dproj, x2, ng, dy, small = rest
    dh = _d_h(dproj, w_in, order)
    grad_x, dng = _rms_bwd(x2, dh, ng, dy, "rms_x_bwd")
    small = [dng if t is None else t for t in small]
    return grad_x, jnp.concatenate(small, axis=1)


def _local_step(x, mem, tgt, norm_g, mem_norm_g, gq_all, gk_all, conv_w, mem_gq, mem_gk, W):
    hb, hbt, mhb = _norms(x, mem, norm_g, mem_norm_g)
    Cs = W["w_in"].shape[1] // 4
    shards = (0, 2, 1, 3)
    order = jnp.array(shards, dtype=jnp.int32)
    w_rel = jnp.concatenate([W["w_in"][:, s * Cs:(s + 1) * Cs] for s in shards], axis=1)
    meta = jnp.array(shards + (0,), dtype=jnp.int32)
    proj2 = _proj_chunk(hb, w_rel, meta, 0, 1, None, None, "proj_0")
    for j, nslots in ((1, 2), (3, 1)):
        for half in (1, 0):
            proj2 = _proj_chunk(hb, w_rel, meta, j, nslots, half, proj2, f"proj_{j}_{half}")
    pre = (hb, hbt, mhb, proj2, *_attention_fwd(proj2, x.shape[0], gq_all, gk_all),
           _conv_branch_fwd(proj2, x.shape[0], conv_w))
    G, rest, _ = _weight_grads(x, mem, tgt, norm_g, mem_norm_g, gq_all, gk_all, conv_w, mem_gq, mem_gk, W, pre)
    pos = jnp.zeros((2,), jnp.int32)
    G["w_in"] = jnp.concatenate([_dw_in_half(hbt, rest[0], pos, True, "dw_in_own"),
                                 _dw_in_half(hbt, rest[0], pos, False, "dw_in_sibling")], axis=0)
    grad_x, small = _input_grad(rest, w_rel, order)
    return grad_x.reshape(x.shape), G, small


BIG = (("w_in", "col"), ("mem_w_kv", "row"), ("w_br_attn", "col"), ("w_br_conv", "col"),
       ("w_br_mem", "col"), ("w_out", "row"))


def _coords():
    return lax.axis_index("x"), lax.axis_index("y"), lax.axis_index("c")


def _other_chips(x, y):
    return [(1 - x, y), (x, 1 - y), (1 - x, 1 - y)]


def _half(ref, kind, c):
    R, C = ref.shape
    if kind == "col":
        return ref.at[pl.ds(c * (R // 2), R // 2), :]
    return ref.at[:, pl.ds(c * (C // 2), C // 2)]


def _shard(ref, kind, s):
    R, C = ref.shape
    if kind == "col":
        return ref.at[:, pl.ds(s * (C // 4), C // 4)]
    return ref.at[pl.ds(s * (R // 4), R // 4), :]


def _piece(ref, kind, s, c):
    R, C = ref.shape
    if kind == "col":
        return ref.at[pl.ds(c * (R // 2), R // 2), pl.ds(s * (C // 4), C // 4)]
    return ref.at[pl.ds(s * (R // 4), R // 4), pl.ds(c * (C // 2), C // 2)]


def _remote(src, dst, sems_s, sems_r, k, dev):
    return pltpu.make_async_remote_copy(src_ref=src, dst_ref=dst, send_sem=sems_s.at[k], recv_sem=sems_r.at[k],
                                        device_id=dev, device_id_type=MESH)


HBM = pl.BlockSpec(memory_space=pltpu.HBM)
SEM = pl.BlockSpec(memory_space=pltpu.SEMAPHORE)
EFFECT = pltpu.SideEffectType.DATAFLOW_SIDE_EFFECTING


def _hbm(a):
    return pltpu.with_memory_space_constraint(a, pltpu.HBM)


def _start_copies(name, arrays, ncopies, make):
    n = len(arrays)

    def body(*refs):
        for cp in make(refs[:n], refs[n], refs[n + 1]):
            cp.start()

    outs = pl.pallas_call(
        body, name=name,
        out_shape=(pltpu.SemaphoreType.DMA((ncopies,)), pltpu.SemaphoreType.DMA((ncopies,)),
                   *[jax.ShapeDtypeStruct(t.shape, t.dtype) for t in arrays]),
        in_specs=[HBM] * n, out_specs=(SEM, SEM, *([HBM] * n)),
        input_output_aliases={i: i + 2 for i in range(n)},
        compiler_params=pltpu.CompilerParams(has_side_effects=EFFECT),
    )(*[_hbm(t) for t in arrays])
    return outs[0], outs[1], list(outs[2:])


def _wait_copies(name, send, recv, arrays, make, after):
    n = len(arrays)

    def body(*refs):
        for cp in make(refs[:n], refs[n], refs[n + 1]):
            cp.wait_send()
            cp.wait_recv()

    outs = pl.pallas_call(
        body, name=name, out_shape=[jax.ShapeDtypeStruct(t.shape, t.dtype) for t in arrays],
        in_specs=[HBM] * n + [SEM, SEM, ANY], out_specs=[HBM] * n,
        input_output_aliases={i: i for i in range(n)},
        compiler_params=pltpu.CompilerParams(has_side_effects=EFFECT),
    )(*arrays, send, recv, after)
    return list(outs)


def _w_in_copies(relations):
    def make(refs, send, recv):
        x, y, c = _coords()
        me = 2 * x + y
        chips = _other_chips(x, y)
        w, conv = refs[0], refs[1]
        cps = []
        for i, k in enumerate(relations):
            cps.append(_remote(_column_half(w, 0, c), _column_half(w, 1 + k, c), send, recv, 2 * i, (*chips[k], c)))
            mine = _shard(conv, "col", me)
            cps.append(_remote(mine, mine, send, recv, 2 * i + 1, (*chips[k], c)))
        return cps
    return make


def _column_half(w, slot, c):
    half = w.shape[1] // 8
    return w.at[:, pl.ds((2 * slot + c) * half, half)]


def _w_in_forward(relations):
    def make(refs, send, recv):
        x, y, c = _coords()
        cps = []
        for i, k in enumerate(relations):
            got = _column_half(refs[0], 1 + k, c)
            cps.append(_remote(got, got, send, recv, i, (x, y, 1 - c)))
        return cps
    return make


def _sibling_copy(refs, send, recv):
    x, y, c = _coords()
    return [_remote(refs[0], refs[1], send, recv, 0, (x, y, 1 - c))]


def _other_weight_copies(refs, send, recv):
    x, y, c = _coords()
    me = 2 * x + y
    cps = []
    for k, chip in enumerate(_other_chips(x, y)):
        for p, (_, kind) in enumerate(BIG[1:]):
            mine = _piece(refs[p], kind, me, c)
            cps.append(_remote(mine, mine, send, recv, 3 * p + k, (*chip, c)))
    return cps


def _other_weight_forward(refs, send, recv):
    x, y, c = _coords()
    cps = []
    for k, chip in enumerate(_other_chips(x, y)):
        s = 2 * chip[0] + chip[1]
        for p, (_, kind) in enumerate(BIG[1:]):
            got = _piece(refs[p], kind, s, c)
            cps.append(_remote(got, got, send, recv, 3 * p + k, (x, y, 1 - c)))
    return cps


def _share_copies(group):
    def make(refs, send, recv):
        x, y, c = _coords()
        cps = []
        for p, (_, kind) in enumerate(group):
            mine = _half(refs[p], kind, c)
            cps.append(_remote(mine, mine, send, recv, p, (x, y, 1 - c)))
        return cps
    return make


def _sibling_exchange(grads, group, name):
    n = len(group)
    shapes = []
    for (_, kind), g in zip(group, grads):
        R, C = g.shape
        shapes.append(jax.ShapeDtypeStruct((R // 2, C) if kind == "col" else (R, C // 2), g.dtype))

    def body(*refs):
        ins, outs = refs[:n], refs[n:2 * n]
        send, recv = refs[2 * n:]
        x, y, c = _coords()
        sib = (x, y, 1 - c)
        cps = [_remote(_half(ins[p], group[p][1], 1 - c), outs[p], send, recv, p, sib) for p in range(n)]
        for cp in cps:
            cp.start()
        for cp in cps:
            cp.wait()

    return pl.pallas_call(
        body, name=name, out_shape=shapes, in_specs=[ANY] * n, out_specs=[ANY] * n,
        scratch_shapes=[pltpu.SemaphoreType.DMA((n,)), pltpu.SemaphoreType.DMA((n,))],
    )(*grads)


def _presum(g, got, kind, pos, name):
    R, C = got.shape
    tr, tc = _tile(R, 512, 16), _tile(C, 2048)
    nr, nc = R // tr, C // tc

    def body(pos_ref, a_ref, b_ref, o_ref):
        o_ref[...] = (a_ref[...].astype(F32) + b_ref[...].astype(F32)).astype(o_ref.dtype)

    blk = pl.BlockSpec((tr, tc), lambda i, j, pos_ref: (i, j))
    if g.shape == got.shape:
        mine = blk
    elif kind == "col":
        mine = pl.BlockSpec((tr, tc), lambda i, j, pos_ref: (pos_ref[1] * nr + i, j))
    else:
        mine = pl.BlockSpec((tr, tc), lambda i, j, pos_ref: (i, pos_ref[1] * nc + j))
    spec = pltpu.PrefetchScalarGridSpec(num_scalar_prefetch=1, grid=(nr, nc), in_specs=[mine, blk], out_specs=blk)
    return _call(body, name=name, grid_spec=spec, out_shape=jax.ShapeDtypeStruct((R, C), WIRE_DTYPE))(pos, g, got)


def _chip_copies(group):
    n = len(group)

    def make(refs, send, recv):
        x, y, c = _coords()
        cps = []
        for k, chip in enumerate(_other_chips(x, y)):
            s = 2 * chip[0] + chip[1]
            for p in range(n):
                cps.append(_remote(_shard(refs[p], group[p][1], s), refs[n + p].at[k], send, recv, 3 * p + k, (*chip, c)))
        return cps
    return make


def _landing_zones(pres, group):
    lands = []
    for (_, kind), g in zip(group, pres):
        R, C = g.shape
        lands.append(lax.empty((3, R, C // 4) if kind == "col" else (3, R // 4, C), g.dtype))
    return lands


def _exchange_start(G, group, pos, carry, tag, pres=None):
    n = len(group)
    if pres is None:
        parts = [G[name] for name, _ in group]
        got = _sibling_exchange(parts, group, "sibling_exchange_" + tag)
        pres = [_presum(parts[p], got[p], kind, pos, "presum_" + name) for p, (name, kind) in enumerate(group)]
    make = _chip_copies(group)
    send, recv, thru = _start_copies("chip_exchange_start_" + tag, [*pres, *_landing_zones(pres, group), carry], 3 * n, make)
    return (send, recv, thru[:2 * n], make, tag), thru[2 * n]


def _exchange_wait(state, after):
    send, recv, arrays, make, tag = state
    thru = _wait_copies("chip_exchange_wait_" + tag, send, recv, arrays, make, after)
    n = len(thru) // 2
    return thru[:n], thru[n:]


def _reduce_into_shard(slots, pre, kind, pos, name):
    K, R, C = slots.shape
    tr, tc = _tile(R, 512, 16), _tile(C, 2176)
    nr, nc = R // tr, C // tc

    def body(pos_ref, s_ref, p_ref, o_ref):
        acc = p_ref[...].astype(F32)
        for k in range(K):
            acc = acc + s_ref[k].astype(F32)
        o_ref[...] = acc

    if kind == "col":
        own = pl.BlockSpec((tr, tc), lambda i, j, pos_ref: (i, pos_ref[0] * nc + j))
        full, out = (2 * R, C), pl.BlockSpec((tr, tc), lambda i, j, pos_ref: (pos_ref[1] * nr + i, j))
    else:
        own = pl.BlockSpec((tr, tc), lambda i, j, pos_ref: (pos_ref[0] * nr + i, j))
        full, out = (R, 2 * C), pl.BlockSpec((tr, tc), lambda i, j, pos_ref: (i, pos_ref[1] * nc + j))
    spec = pltpu.PrefetchScalarGridSpec(
        num_scalar_prefetch=1, grid=(nr, nc),
        in_specs=[pl.BlockSpec((K, tr, tc), lambda i, j, pos_ref: (0, i, j)), own], out_specs=out)
    return _call(body, name=name, grid_spec=spec, out_shape=jax.ShapeDtypeStruct(full, F32))(pos, slots, pre)


def _gather_small(pack):
    _, N = pack.shape

    def body(in_ref, out_ref, send, recv, loc):
        x, y, c = _coords()
        me = 4 * x + 2 * y + c
        own = pltpu.make_async_copy(in_ref, out_ref.at[me], loc)
        own.start()
        cps = []
        for k in range(1, 8):
            dev = (x ^ (k >> 2), y ^ ((k >> 1) & 1), c ^ (k & 1))
            cps.append(_remote(in_ref, out_ref.at[me], send, recv, k - 1, dev))
        for cp in cps:
            cp.start()
        for k in range(1, 8):
            src = 4 * (x ^ (k >> 2)) + 2 * (y ^ ((k >> 1) & 1)) + (c ^ (k & 1))
            _remote(in_ref, out_ref.at[src], send, recv, k - 1, (x, y, c)).wait_recv()
        for cp in cps:
            cp.wait_send()
        own.wait()

    return pl.pallas_call(
        body, name="gather_small", out_shape=jax.ShapeDtypeStruct((8, 1, N), pack.dtype),
        in_specs=[ANY], out_specs=ANY,
        scratch_shapes=[pltpu.SemaphoreType.DMA((7,)), pltpu.SemaphoreType.DMA((7,)), pltpu.SemaphoreType.DMA(())],
    )(pack)


def _sum_small(slots):
    K, _, N = slots.shape

    def body(s_ref, o_ref):
        acc = s_ref[0]
        for k in range(1, K):
            acc = acc + s_ref[k]
        o_ref[...] = acc

    return _call(body, name="sum_small", in_specs=[pl.BlockSpec(memory_space=pltpu.VMEM)],
                 out_specs=pl.BlockSpec(memory_space=pltpu.VMEM), out_shape=jax.ShapeDtypeStruct((1, N), F32))(slots)


def _adamw(w, g, m, v, name, with_grad=False):
    R, C = w.shape
    tr, tc = _tile(R, 256, 8), _tile(C, 2176)

    def body(w_ref, g_ref, m_ref, v_ref, d_ref, nm_ref, nv_ref, *g_out):
        gv = g_ref[...]
        for ref in g_out:
            ref[...] = gv
        nm = ADAM_B1 * m_ref[...] + (1.0 - ADAM_B1) * gv
        nv = ADAM_B2 * v_ref[...] + (1.0 - ADAM_B2) * gv * gv
        m_hat = nm / (1.0 - ADAM_B1 ** ADAM_STEP)
        v_hat = nv / (1.0 - ADAM_B2 ** ADAM_STEP)
        d_ref[...] = -ADAM_LR * (m_hat / (jnp.sqrt(v_hat) + ADAM_EPS) + ADAM_WD * w_ref[...])
        nm_ref[...] = nm
        nv_ref[...] = nv

    spec = pl.BlockSpec((tr, tc), lambda i, j: (i, j))
    shp = jax.ShapeDtypeStruct((R, C), F32)
    nout = 4 if with_grad else 3
    return _call(body, name=name, grid=(R // tr, C // tc), in_specs=[spec] * 4, out_specs=[spec] * nout,
                 out_shape=[shp] * nout)(w, g, m, v)


SMALL = ("norm_g", "mem_norm_g", "attn_q_norm", "attn_k_norm", "conv_w", "mem_q_norm", "mem_k_norm")
WEIGHTS = ("norm_g", "mem_norm_g", "w_in", "attn_q_norm", "attn_k_norm", "conv_w", "mem_w_kv", "mem_q_norm",
           "mem_k_norm", "w_br_attn", "w_br_conv", "w_br_mem", "w_out")


def kernel(x, mem, norm_g, mem_norm_g, w_in, attn_q_norm, attn_k_norm, conv_w, mem_w_kv, mem_q_norm, mem_k_norm, w_br_attn, w_br_conv, w_br_mem, w_out, loss_target, m_norm_g, m_mem_norm_g, m_w_in, m_attn_q_norm, m_attn_k_norm, m_conv_w, m_mem_w_kv, m_mem_q_norm, m_mem_k_norm, m_w_br_attn, m_w_br_conv, m_w_br_mem, m_w_out, v_norm_g, v_mem_norm_g, v_w_in, v_attn_q_norm, v_attn_k_norm, v_conv_w, v_mem_w_kv, v_mem_q_norm, v_mem_k_norm, v_w_br_attn, v_w_br_conv, v_w_br_mem, v_w_out):
    w = dict(norm_g=norm_g, mem_norm_g=mem_norm_g, w_in=w_in, attn_q_norm=attn_q_norm, attn_k_norm=attn_k_norm,
             conv_w=conv_w, mem_w_kv=mem_w_kv, mem_q_norm=mem_q_norm, mem_k_norm=mem_k_norm, w_br_attn=w_br_attn,
             w_br_conv=w_br_conv, w_br_mem=w_br_mem, w_out=w_out)
    m = dict(norm_g=m_norm_g, mem_norm_g=m_mem_norm_g, w_in=m_w_in, attn_q_norm=m_attn_q_norm,
             attn_k_norm=m_attn_k_norm, conv_w=m_conv_w, mem_w_kv=m_mem_w_kv, mem_q_norm=m_mem_q_norm,
             mem_k_norm=m_mem_k_norm, w_br_attn=m_w_br_attn, w_br_conv=m_w_br_conv, w_br_mem=m_w_br_mem, w_out=m_w_out)
    v = dict(norm_g=v_norm_g, mem_norm_g=v_mem_norm_g, w_in=v_w_in, attn_q_norm=v_attn_q_norm,
             attn_k_norm=v_attn_k_norm, conv_w=v_conv_w, mem_w_kv=v_mem_w_kv, mem_q_norm=v_mem_q_norm,
             mem_k_norm=v_mem_k_norm, w_br_attn=v_w_br_attn, w_br_conv=v_w_br_conv, w_br_mem=v_w_br_mem, w_out=v_w_out)
    Bl, _, D = x.shape
    cx, cy = lax.axis_index("x"), lax.axis_index("y")
    chip = 2 * cx + cy
    pos = jnp.stack([chip, lax.axis_index("c")]).astype(jnp.int32)
    order = jnp.stack([chip] + [2 * a + b for a, b in _other_chips(cx, cy)]).astype(jnp.int32)
    n = len(BIG)

    slot0 = jnp.stack([jnp.zeros((), jnp.int32), pos[1]])
    w_rel = _place_shard(w["w_in"], "col", slot0, WIRE_DTYPE, "place_w_in_sent", half=0)
    conv_full = _place_shard(conv_w, "col", pos, F32, "place_conv_w")
    others = [_place_shard(w[name], kind, pos, WIRE_DTYPE, "place_" + name) for name, kind in BIG[1:]]
    hb, hbt, mhb = _norms(x, mem, norm_g, mem_norm_g)

    meta = jnp.concatenate([order, pos[1:]])
    near, near_fwd = _w_in_copies((0, 1)), _w_in_forward((0, 1))
    send, recv, (w_rel, conv_full) = _start_copies("gather_near_start", [w_rel, conv_full], 4, near)
    w_rel = _place_shard(w["w_in"], "col", slot0, WIRE_DTYPE, "place_w_in_kept", half=1, into=w_rel)
    proj = _proj_chunk(hb, w_rel, meta, 0, 1, None, None, "proj_own")
    w_rel, conv_full, *others = _wait_copies("gather_near_wait", send, recv, [w_rel, conv_full, *others], near, proj)

    fsend, frecv, (w_rel,) = _start_copies("gather_near_forward_start", [w_rel], 2, near_fwd)
    far, far_fwd = _w_in_copies((2,)), _w_in_forward((2,))
    send, recv, (w_rel, conv_full) = _start_copies("gather_far_start", [w_rel, conv_full], 2, far)
    proj = _proj_chunk(hb, w_rel, meta, 1, 2, 0, proj, "proj_near_landed")
    w_rel, = _wait_copies("gather_near_forward_wait", fsend, frecv, [w_rel], near_fwd, proj)
    proj = _proj_chunk(hb, w_rel, meta, 1, 2, 1, proj, "proj_near_forwarded")
    w_rel, conv_full = _wait_copies("gather_far_wait", send, recv, [w_rel, conv_full], far, proj)

    fsend, frecv, (w_rel,) = _start_copies("gather_far_forward_start", [w_rel], 1, far_fwd)
    send, recv, (*others, w_rel) = _start_copies("gather_rest_start", [*others, w_rel], 3 * (n - 1), _other_weight_copies)
    proj = _proj_chunk(hb, w_rel, meta, 3, 1, 0, proj, "proj_far_landed")
    w_rel, = _wait_copies("gather_far_forward_wait", fsend, frecv, [w_rel], far_fwd, proj)
    proj = _proj_chunk(hb, w_rel, meta, 3, 1, 1, proj, "proj_far_forwarded")
    os, ls, a = _attention_fwd(proj, Bl, attn_q_norm, attn_k_norm)
    *others, w_rel = _wait_copies("gather_rest_wait", send, recv, [*others, w_rel], _other_weight_copies, a[0])
    fsend, frecv, (*others, proj) = _start_copies("gather_rest_forward_start", [*others, proj], 3 * (n - 1),
                                                  _other_weight_forward)
    cc = _conv_branch_fwd(proj, Bl, conv_full)
    others = _wait_copies("gather_rest_forward_wait", fsend, frecv, others, _other_weight_forward, cc[0])
    W = {name: others[p] for p, (name, _) in enumerate(BIG[1:])}

    G, rest, rest_state = _weight_grads(
        x, mem, loss_target, norm_g, mem_norm_g, attn_q_norm, attn_k_norm, conv_full, mem_q_norm, mem_k_norm, W,
        (hb, hbt, mhb, proj, os, ls, a, cc), early=lambda G, carry: _exchange_start(G, BIG[1:], pos, carry, "rest"))

    for_sibling = _dw_in_half(hbt, rest[0], pos, False, "dw_in_sibling")
    send, recv, (for_sibling, got, dproj) = _start_copies(
        "sibling_w_in_start", [for_sibling, lax.empty(for_sibling.shape, for_sibling.dtype), rest[0]], 1, _sibling_copy)
    mine = _dw_in_half(hbt, dproj, pos, True, "dw_in_own")
    for_sibling, got = _wait_copies("sibling_w_in_wait", send, recv, [for_sibling, got], _sibling_copy, mine)
    pre_w_in = _presum(mine, got, "col", pos, "presum_w_in")

    w_in_state, dproj = _exchange_start(G, BIG[:1], pos, dproj, "w_in", pres=[pre_w_in])
    grad_x, small = _input_grad((dproj, *rest[1:]), w_rel, order)
    pres_rest, slots_rest = _exchange_wait(rest_state, grad_x)
    reds_rest = [_reduce_into_shard(slots_rest[p], pres_rest[p], kind, pos, "reduce_" + name)
                 for p, (name, kind) in enumerate(BIG[1:])]
    share_rest = _share_copies(BIG[1:])
    rsend, rrecv, reds_rest = _start_copies("share_rest_start", reds_rest, n - 1, share_rest)
    pres, slots = _exchange_wait(w_in_state, grad_x)
    red_w_in = _reduce_into_shard(slots[0], pres[0], "col", pos, "reduce_w_in")
    share_w_in = _share_copies(BIG[:1])
    wsend, wrecv, (red_w_in, small) = _start_copies("share_w_in_start", [red_w_in, small], 1, share_w_in)
    grad_x = grad_x.reshape(x.shape)

    tot = _sum_small(_gather_small(small))
    reds_rest = _wait_copies("share_rest_wait", rsend, rrecv, reds_rest, share_rest, tot)
    grads = dict(zip([name for name, _ in BIG[1:]], reds_rest))
    tot = tot[0]
    loss = tot[0]
    off = 128
    for name, size in (("norm_g", D), ("mem_norm_g", D), ("attn_q_norm", NGROUP * HEAD), ("attn_k_norm", NGROUP * HEAD),
                       ("conv_w", 3 * CONVW), ("mem_q_norm", MEM_HD), ("mem_k_norm", MEM_HD)):
        grads[name] = tot[off:off + size]
        off += size
    cw = conv_w.shape[1]
    grads["conv_w"] = lax.dynamic_slice(grads["conv_w"].reshape(3, CONVW), (0, chip * cw), (3, cw))
    for name in SMALL:
        grads[name] = grads[name].reshape(w[name].shape)

    delta, new_m, new_v = {}, {}, {}
    for name, _ in BIG[1:]:
        delta[name], new_m[name], new_v[name], grads[name] = _adamw(w[name], grads[name], m[name], v[name],
                                                                    "adamw_" + name, with_grad=True)

    def packed(t):
        return jnp.concatenate([t[name].reshape(1, -1) for name in SMALL], axis=1)

    ds, ms, vs = _adamw(packed(w), packed(grads), packed(m), packed(v), "adamw_small")
    shared, = _wait_copies("share_w_in_wait", wsend, wrecv, [red_w_in], share_w_in, ds)
    delta["w_in"], new_m["w_in"], new_v["w_in"], grads["w_in"] = _adamw(w["w_in"], shared, m["w_in"], v["w_in"],
                                                                        "adamw_w_in", with_grad=True)
    off = 0
    for name in SMALL:
        size = w[name].size
        delta[name] = ds[0, off:off + size].reshape(w[name].shape)
        new_m[name] = ms[0, off:off + size].reshape(w[name].shape)
        new_v[name] = vs[0, off:off + size].reshape(w[name].shape)
        off += size

    return (loss, grad_x, *[grads[n] for n in WEIGHTS], *[delta[n] for n in WEIGHTS],
            *[new_m[n] for n in WEIGHTS], *[new_v[n] for n in WEIGHTS])
```

```python
import functools

import jax
import jax.numpy as jnp
from jax import lax
from jax.experimental import pallas as pl
from jax.experimental.pallas import tpu as pltpu

F32 = jnp.float32
MXU_DTYPE = jnp.bfloat16
WIRE_DTYPE = jnp.bfloat16
PROJ_DTYPE = jnp.bfloat16
EPS = 1e-6
NEG = -1e30

HEAD = 128
HPG = 4
GW = HPG * HEAD
DILATIONS = (1, 4, 16)
NGROUP = len(DILATIONS)
BLK = 128
QKV = NGROUP * GW
CONVW = 1024
MEM_HEADS = 4
MEM_HD = 256
MEMW = MEM_HEADS * MEM_HD
Q0, K0, V0 = 0, QKV, 2 * QKV
ZA = 3 * QKV
CB, CC, CV, ZC = ZA + GW, ZA + GW + CONVW, ZA + GW + 2 * CONVW, ZA + GW + 3 * CONVW
MQ = ZC + CONVW
ZM = MQ + MEMW
G0 = ZM + MEMW

ADAM_LR, ADAM_B1, ADAM_B2, ADAM_EPS, ADAM_WD, ADAM_STEP = 0.001, 0.9, 0.999, 1e-08, 0.01, 10

VMEM_LIMIT = 56 * 1024 * 1024
MESH = pl.DeviceIdType.MESH
ANY = pl.BlockSpec(memory_space=pl.ANY)


def _tile(n, pref, mult=128):
    t = min(pref, n)
    while t > mult and (n % t or t % mult):
        t -= mult
    assert n % t == 0, (n, pref)
    return t


def _call(body, *, name, out_shape, grid=(), in_specs=None, out_specs=None, scratch_shapes=(),
          aliases=None, grid_spec=None):
    kw = {}
    if grid_spec is not None:
        kw["grid_spec"] = grid_spec
        ngrid = len(grid_spec.grid)
    else:
        kw.update(grid=grid, in_specs=in_specs, out_specs=out_specs, scratch_shapes=list(scratch_shapes))
        ngrid = len(grid)
    params = pltpu.CompilerParams(dimension_semantics=("arbitrary",) * ngrid, vmem_limit_bytes=VMEM_LIMIT)
    return pl.pallas_call(body, name=name, out_shape=out_shape, compiler_params=params,
                          input_output_aliases=aliases or {}, **kw)


_DIMS = {"nn": (((1,), (0,)), ((), ())), "nt": (((1,), (1,)), ((), ())), "tn": (((0,), (0,)), ((), ()))}


def _mxu(a, b, mode):
    return lax.dot_general(a.astype(MXU_DTYPE), b.astype(MXU_DTYPE), _DIMS[mode], preferred_element_type=F32)


@functools.partial(jax.custom_vjp, nondiff_argnums=(2,))
def _dot(a, b, mode):
    return _mxu(a, b, mode)


def _dot_fwd(a, b, mode):
    return _mxu(a, b, mode), (a, b)


def _dot_bwd(mode, res, g):
    a, b = res
    if mode == "nn":
        return _mxu(g, b, "nt"), _mxu(a, g, "tn")
    if mode == "nt":
        return _mxu(g, b, "nn"), _mxu(g, a, "tn")
    return _mxu(b, g, "nt"), _mxu(a, g, "nn")


_dot.defvjp(_dot_fwd, _dot_bwd)


def _sig(z):
    return 1.0 / (1.0 + jnp.exp(-z))


def _silu(z):
    return z * _sig(z)


def _rms_rows(t, g):
    return t * lax.rsqrt(jnp.mean(t * t, axis=-1, keepdims=True) + EPS) * g


def _attn_block(q, k2, v2, gq, gk, first):
    qn = _rms_rows(q, gq)
    kn = _rms_rows(k2, gk)
    s = jnp.where(_band_mask(first, k2.shape[0]), _mxu(qn, kn, "nt") * (HEAD ** -0.5), NEG)
    m = jnp.max(s, axis=-1, keepdims=True)
    p = jnp.exp(s - m)
    den = jnp.sum(p, axis=-1, keepdims=True)
    o = _mxu(p, v2, "nn") / den
    return o, m + jnp.log(den)


def _band_mask(first, nkeys):
    a = lax.broadcasted_iota(jnp.int32, (BLK, nkeys), 0)
    b = lax.broadcasted_iota(jnp.int32, (BLK, nkeys), 1)
    if nkeys == BLK:
        return b <= a
    return (b >= a) & (b <= a + BLK) & (b >= jnp.where(first, BLK, 0))


def _norm_parts(t):
    r = lax.rsqrt(jnp.mean(t * t, axis=-1, keepdims=True) + EPS)
    return r, t * r


def _norm_bwd(dn, g, r, th):
    dth = dn * g
    return r * (dth - th * jnp.mean(dth * th, axis=-1, keepdims=True)), jnp.sum(dn * th, axis=0, keepdims=True)


def _attn_block_bwd(q, k2, v2, gq, gk, first, do, o, lse, dlse):
    scale = HEAD ** -0.5
    rq, qh = _norm_parts(q)
    rk, kh = _norm_parts(k2)
    qn, kn = qh * gq, kh * gk
    s = jnp.where(_band_mask(first, k2.shape[0]), _mxu(qn, kn, "nt") * scale, NEG)
    p = jnp.exp(s - lse)
    ds = p * (_mxu(do, v2, "nt") + (dlse - jnp.sum(do * o, axis=-1, keepdims=True))) * scale
    dq, dgq = _norm_bwd(_mxu(ds, kn, "nn"), gq, rq, qh)
    dk2, dgk = _norm_bwd(_mxu(ds, qn, "tn"), gk, rk, kh)
    return dq, dk2, _mxu(p, do, "tn"), dgq, dgk


def _combine(o1, o2, o3, l1, l2, l3, z):
    m = lax.stop_gradient(jnp.maximum(jnp.maximum(l1, l2), l3))
    e1, e2, e3 = jnp.exp(l1 - m), jnp.exp(l2 - m), jnp.exp(l3 - m)
    return (e1 * o1 + e2 * o2 + e3 * o3) / (e1 + e2 + e3) * _silu(z)


def _mem_block(q, z, kv, gq, gk):
    outs = []
    for h in range(MEM_HEADS):
        sl = slice(h * MEM_HD, (h + 1) * MEM_HD)
        qn = _rms_rows(q[:, sl], gq)
        kn = _rms_rows(kv[:, sl], gk)
        s = _dot(qn, kn, "nt") * (MEM_HD ** -0.5)
        m = lax.stop_gradient(jnp.max(s, axis=-1, keepdims=True))
        p = jnp.exp(s - m)
        den = jnp.sum(p, axis=-1, keepdims=True)
        outs.append(_dot(p, kv[:, MEMW + h * MEM_HD:MEMW + (h + 1) * MEM_HD], "nn") / den)
    return jnp.concatenate(outs, axis=-1) * _silu(z)


def _cast(w, name):
    R, C = w.shape
    tr, tc = _tile(R, 512, 8), _tile(C, 2176)

    def body(w_ref, o_ref):
        o_ref[...] = w_ref[...].astype(o_ref.dtype)

    spec = pl.BlockSpec((tr, tc), lambda i, j: (i, j))
    return _call(body, name=name, grid=(R // tr, C // tc), in_specs=[spec], out_specs=spec,
                 out_shape=jax.ShapeDtypeStruct((R, C), WIRE_DTYPE))(w)


def _place_shard(w, kind, pos, dtype, name, slot=0, into=None, half=None):
    R, C = w.shape
    tr, tc = _tile(R, 512, 8), _tile(C if half is None else C // 2, 2176)
    nr, nc = R // tr, C // tc
    ncols = nc if half is None else nc // 2

    def body(pos_ref, w_ref, *rest):
        rest[-1][...] = w_ref[...].astype(rest[-1].dtype)

    def col(j, pos_ref):
        if half is None:
            return j
        return (pos_ref[1] if half == 0 else 1 - pos_ref[1]) * ncols + j

    if kind == "col":
        full = (R, 4 * C)
        out = pl.BlockSpec((tr, tc), lambda i, j, pos_ref: (i, pos_ref[slot] * nc + col(j, pos_ref)))
    else:
        full, out = (4 * R, C), pl.BlockSpec((tr, tc), lambda i, j, pos_ref: (pos_ref[slot] * nr + i, j))
    in_specs, args = [pl.BlockSpec((tr, tc), lambda i, j, pos_ref: (i, col(j, pos_ref)))], [pos, w]
    if into is not None:
        in_specs.append(ANY)
        args.append(into)
    spec = pltpu.PrefetchScalarGridSpec(num_scalar_prefetch=1, grid=(nr, ncols), in_specs=in_specs, out_specs=out)
    return _call(body, name=name, grid_spec=spec, out_shape=jax.ShapeDtypeStruct(full, dtype),
                 aliases={} if into is None else {2: 0})(*args)


def _matmul(a, b, mode, out_dtype, *, name, tm=512, tn=512, tk=512):
    if mode == "nn":
        (M, K), (_, N) = a.shape, b.shape
    elif mode == "nt":
        (M, K), (N, _) = a.shape, b.shape
    else:
        (K, M), (_, N) = a.shape, b.shape
    tm, tn, tk = _tile(M, tm), _tile(N, tn), _tile(K, tk)
    nk = K // tk

    def body(a_ref, b_ref, o_ref, *acc):
        part = lax.dot_general(a_ref[...], b_ref[...], _DIMS[mode], preferred_element_type=F32)
        if nk == 1:
            o_ref[...] = part.astype(o_ref.dtype)
            return
        acc_ref, = acc
        k = pl.program_id(2)

        @pl.when(k == 0)
        def _():
            acc_ref[...] = part

        @pl.when(k > 0)
        def _():
            acc_ref[...] += part

        @pl.when(k == nk - 1)
        def _():
            o_ref[...] = acc_ref[...].astype(o_ref.dtype)

    a_spec = pl.BlockSpec((tk, tm), lambda i, j, k: (k, i)) if mode == "tn" else pl.BlockSpec((tm, tk), lambda i, j, k: (i, k))
    b_spec = pl.BlockSpec((tn, tk), lambda i, j, k: (j, k)) if mode == "nt" else pl.BlockSpec((tk, tn), lambda i, j, k: (k, j))
    return _call(body, name=name, grid=(M // tm, N // tn, nk), in_specs=[a_spec, b_spec],
                 out_specs=pl.BlockSpec((tm, tn), lambda i, j, k: (i, j)),
                 out_shape=jax.ShapeDtypeStruct((M, N), out_dtype),
                 scratch_shapes=[] if nk == 1 else [pltpu.VMEM((tm, tn), F32)])(a, b)


def _rms_fwd(x, g, name):
    R, D = x.shape
    tr = _tile(R, 512)

    def body(x_ref, g_ref, o_ref, t_ref):
        y = _rms_rows(x_ref[...], g_ref[...])
        o_ref[...] = y.astype(o_ref.dtype)
        t_ref[...] = y.T.astype(t_ref.dtype)

    row = pl.BlockSpec((tr, D), lambda i: (i, 0))
    return _call(body, name=name, grid=(R // tr,), in_specs=[row, pl.BlockSpec((1, D), lambda i: (0, 0))],
                 out_specs=[row, pl.BlockSpec((D, tr), lambda i: (0, i))],
                 out_shape=[jax.ShapeDtypeStruct((R, D), MXU_DTYPE), jax.ShapeDtypeStruct((D, R), MXU_DTYPE)])(x, g)


def _rms_bwd(x, dh, g, dy, name):
    R, D = x.shape
    tr = _tile(R, 256)
    with_dx = dy is not None

    def body(*refs):
        if with_dx:
            x_ref, dh_ref, g_ref, dy_ref, dx_ref, dg_ref = refs
        else:
            x_ref, dh_ref, g_ref, dg_ref = refs
        xv, dhv = x_ref[...], dh_ref[...]
        r = lax.rsqrt(jnp.mean(xv * xv, axis=-1, keepdims=True) + EPS)
        xh = xv * r

        @pl.when(pl.program_id(0) == 0)
        def _():
            dg_ref[...] = jnp.zeros_like(dg_ref)

        dg_ref[...] += jnp.sum(dhv * xh, axis=0, keepdims=True)
        if with_dx:
            dxh = dhv * g_ref[...]
            dx_ref[...] = dy_ref[...] + r * (dxh - xh * jnp.mean(dxh * xh, axis=-1, keepdims=True))

    row = pl.BlockSpec((tr, D), lambda i: (i, 0))
    vec = pl.BlockSpec((1, D), lambda i: (0, 0))
    dg_shape = jax.ShapeDtypeStruct((1, D), F32)
    if with_dx:
        return _call(body, name=name, grid=(R // tr,), in_specs=[row, row, vec, row], out_specs=[row, vec],
                     out_shape=[jax.ShapeDtypeStruct((R, D), F32), dg_shape])(x, dh, g, dy)
    return None, _call(body, name=name, grid=(R // tr,), in_specs=[row, row, vec], out_specs=vec,
                       out_shape=dg_shape)(x, dh, g)


def _attn_geom(g, d):
    hc = HPG if d == 1 else 1
    cw = hc * HEAD
    cq, ck, cv = (Q0 + g * GW) // cw, (K0 + g * GW) // cw, (V0 + g * GW) // cw
    return (1, BLK * d, cw), hc, HPG // hc, cq, ck, cv


def _rows(ref, r, d, sl):
    if d == 1:
        return ref[0, :, sl]
    return ref.at[0][pl.ds(r, BLK, stride=d), sl]


def _set_rows(ref, r, d, sl, val):
    if d == 1:
        ref[0, :, sl] = val
    else:
        ref.at[0][pl.ds(r, BLK, stride=d), sl] = val


def _stage_rows(ref, r, d, sl, val):
    if d == 1:
        ref[:, sl] = val
    else:
        ref[pl.ds(r, BLK, stride=d), sl] = val


def _proj_stages(blk, d):
    return [] if d == 1 else [pltpu.VMEM(blk[1:], F32)] * 5


def _proj_rows(refs, stages, d):
    if d == 1:
        return [lambda r, sl, ref=ref: ref[0, :, sl].astype(F32) for ref in refs]
    for ref, stage in zip(refs, stages):
        stage[...] = ref[0].astype(F32)
    return [lambda r, sl, stage=stage: stage[pl.ds(r, BLK, stride=d), sl] for stage in stages]


def _attn_fwd(proj3, gq, gk, g, d):
    Bl, S, _ = proj3.shape
    blk, hc, ncb, cq, ck, cv = _attn_geom(g, d)
    nb = S // blk[1]
    if nb == 1:
        return _attn_single_fwd(proj3, gq, gk, g, d)

    def body(q_ref, kp_ref, kc_ref, vp_ref, vc_ref, gq_ref, gk_ref, o_ref, lse_ref, *stages):
        first = pl.program_id(2) == 0
        q, kp, kc, vp, vc = _proj_rows((q_ref, kp_ref, kc_ref, vp_ref, vc_ref), stages, d)
        def run(alone):
            for r in range(d):
                for h in range(hc):
                    sl = slice(h * HEAD, (h + 1) * HEAD)
                    if alone:
                        k2, v2 = kc(r, sl), vc(r, sl)
                    else:
                        k2 = jnp.concatenate([kp(r, sl), kc(r, sl)], axis=0)
                        v2 = jnp.concatenate([vp(r, sl), vc(r, sl)], axis=0)
                    o, lse = _attn_block(q(r, sl), k2, v2, gq_ref[...], gk_ref[...], False)
                    _set_rows(o_ref, r, d, sl, o)
                    _set_rows(lse_ref, r, d, sl, jnp.broadcast_to(lse, (BLK, HEAD)))

        pl.when(first)(lambda: run(True))
        pl.when(jnp.logical_not(first))(lambda: run(False))

    def cur(c0):
        return pl.BlockSpec(blk, lambda b, j, i: (b, i, c0 + j))

    def prev(c0):
        return pl.BlockSpec(blk, lambda b, j, i: (b, jnp.maximum(i - 1, 0), c0 + j))

    vec = pl.BlockSpec((1, HEAD), lambda b, j, i: (0, 0))
    out = pl.BlockSpec(blk, lambda b, j, i: (b, i, j))
    shp = jax.ShapeDtypeStruct((Bl, S, GW), F32)
    return _call(body, name=f"attn_fwd_g{g}", grid=(Bl, ncb, nb),
                 in_specs=[cur(cq), prev(ck), cur(ck), prev(cv), cur(cv), vec, vec],
                 out_specs=[out, out], out_shape=[shp, shp], scratch_shapes=_proj_stages(blk, d),
                 )(proj3, proj3, proj3, proj3, proj3, gq, gk)


def _attn_single_fwd(proj3, gq, gk, g, d):
    Bl, S, _ = proj3.shape
    blk, hc, ncb, cq, ck, cv = _attn_geom(g, d)

    def body(q_ref, k_ref, v_ref, gq_ref, gk_ref, o_ref, lse_ref, *stages):
        q, k, v = _proj_rows((q_ref, k_ref, v_ref), stages, d)
        for r in range(d):
            for h in range(hc):
                sl = slice(h * HEAD, (h + 1) * HEAD)
                o, lse = _attn_block(q(r, sl), k(r, sl), v(r, sl), gq_ref[...], gk_ref[...], True)
                _set_rows(o_ref, r, d, sl, o)
                _set_rows(lse_ref, r, d, sl, jnp.broadcast_to(lse, (BLK, HEAD)))

    def at(c0):
        return pl.BlockSpec(blk, lambda b, j: (b, 0, c0 + j))

    vec = pl.BlockSpec((1, HEAD), lambda b, j: (0, 0))
    shp = jax.ShapeDtypeStruct((Bl, S, GW), F32)
    return _call(body, name=f"attn_fwd_g{g}", grid=(Bl, ncb), in_specs=[at(cq), at(ck), at(cv), vec, vec],
                 out_specs=[at(0), at(0)], out_shape=[shp, shp], scratch_shapes=_proj_stages(blk, d)[:3],
                 )(proj3, proj3, proj3, gq, gk)


def _attn_single_bwd(proj3, gq, gk, o3, l3, do3, dl3, g, d):
    Bl, S, _ = proj3.shape
    blk, hc, ncb, cq, ck, cv = _attn_geom(g, d)

    def body(q_ref, k_ref, v_ref, gq_ref, gk_ref, o_ref, l_ref, do_ref, dl_ref,
             dq_ref, dk_ref, dv_ref, dgq_ref, dgk_ref, sq_ref, sk_ref, sv_ref, *stages):
        @pl.when((pl.program_id(0) == 0) & (pl.program_id(1) == 0))
        def _():
            dgq_ref[...] = jnp.zeros_like(dgq_ref)
            dgk_ref[...] = jnp.zeros_like(dgk_ref)

        dgq, dgk = jnp.zeros((1, HEAD), F32), jnp.zeros((1, HEAD), F32)
        q, k, v = _proj_rows((q_ref, k_ref, v_ref), stages, d)
        for r in range(d):
            for h in range(hc):
                sl = slice(h * HEAD, (h + 1) * HEAD)
                dq, dk, dv, a, b = _attn_block_bwd(
                    q(r, sl), k(r, sl), v(r, sl), gq_ref[...], gk_ref[...], True, _rows(do_ref, r, d, sl),
                    _rows(o_ref, r, d, sl), _rows(l_ref, r, d, sl)[:, :1], _rows(dl_ref, r, d, sl)[:, :1])
                _stage_rows(sq_ref, r, d, sl, dq)
                _stage_rows(sk_ref, r, d, sl, dk)
                _stage_rows(sv_ref, r, d, sl, dv)
                dgq, dgk = dgq + a, dgk + b
        dgq_ref[...] += dgq
        dgk_ref[...] += dgk
        dq_ref[0] = sq_ref[...].astype(dq_ref.dtype)
        dk_ref[0] = sk_ref[...].astype(dk_ref.dtype)
        dv_ref[0] = sv_ref[...].astype(dv_ref.dtype)

    def at(c0):
        return pl.BlockSpec(blk, lambda b, j: (b, 0, c0 + j))

    vec = pl.BlockSpec((1, HEAD), lambda b, j: (0, 0))
    shp = jax.ShapeDtypeStruct((Bl, S, GW), MXU_DTYPE)
    gshp = jax.ShapeDtypeStruct((1, HEAD), F32)
    return _call(body, name=f"attn_bwd_g{g}", grid=(Bl, ncb),
                 in_specs=[at(cq), at(ck), at(cv), vec, vec, at(0), at(0), at(0), at(0)],
                 out_specs=[at(0), at(0), at(0), vec, vec], out_shape=[shp, shp, shp, gshp, gshp],
                 scratch_shapes=[pltpu.VMEM(blk[1:], F32)] * 3 + _proj_stages(blk, d)[:3],
                 )(proj3, proj3, proj3, gq, gk, o3, l3, do3, dl3)


def _attn_bwd(proj3, gq, gk, o3, l3, do3, dl3, g, d):
    Bl, S, _ = proj3.shape
    blk, hc, ncb, cq, ck, cv = _attn_geom(g, d)
    nb = S // blk[1]
    if nb == 1:
        return _attn_single_bwd(proj3, gq, gk, o3, l3, do3, dl3, g, d)

    def body(q_ref, kp_ref, kc_ref, vp_ref, vc_ref, gq_ref, gk_ref, o_ref, l_ref, do_ref, dl_ref,
             dq_ref, dk_ref, dv_ref, dgq_ref, dgk_ref, ck_ref, cv_ref, sq_ref, sk_ref, sv_ref, *stages):
        i = pl.program_id(2)
        first = i == 0

        @pl.when((pl.program_id(0) == 0) & (pl.program_id(1) == 0) & first)
        def _():
            dgq_ref[...] = jnp.zeros_like(dgq_ref)
            dgk_ref[...] = jnp.zeros_like(dgk_ref)

        def run(alone):
            dgq, dgk = jnp.zeros((1, HEAD), F32), jnp.zeros((1, HEAD), F32)
            q, kp, kc, vp, vc = _proj_rows((q_ref, kp_ref, kc_ref, vp_ref, vc_ref), stages, d)
            for r in range(d):
                rs = slice(r * BLK, (r + 1) * BLK)
                for h in range(hc):
                    sl = slice(h * HEAD, (h + 1) * HEAD)
                    if alone:
                        k2, v2 = kc(r, sl), vc(r, sl)
                    else:
                        k2 = jnp.concatenate([kp(r, sl), kc(r, sl)], axis=0)
                        v2 = jnp.concatenate([vp(r, sl), vc(r, sl)], axis=0)
                    dq, dk2, dv2, a, b = _attn_block_bwd(
                        q(r, sl), k2, v2, gq_ref[...], gk_ref[...], False, _rows(do_ref, r, d, sl),
                        _rows(o_ref, r, d, sl), _rows(l_ref, r, d, sl)[:, :1], _rows(dl_ref, r, d, sl)[:, :1])
                    _stage_rows(sq_ref, r, d, sl, dq)
                    if alone:
                        _stage_rows(sk_ref, r, d, sl, jnp.zeros((BLK, HEAD), F32))
                        _stage_rows(sv_ref, r, d, sl, jnp.zeros((BLK, HEAD), F32))
                    else:
                        _stage_rows(sk_ref, r, d, sl, ck_ref[rs, sl] + dk2[:BLK])
                        _stage_rows(sv_ref, r, d, sl, cv_ref[rs, sl] + dv2[:BLK])
                    ck_ref[rs, sl] = dk2[-BLK:]
                    cv_ref[rs, sl] = dv2[-BLK:]
                    dgq, dgk = dgq + a, dgk + b
            dgq_ref[...] += dgq
            dgk_ref[...] += dgk
            dq_ref[0] = sq_ref[...].astype(dq_ref.dtype)

        pl.when(first)(lambda: run(True))
        pl.when((i > 0) & (i < nb))(lambda: run(False))

        @pl.when(i == nb)
        def _():
            for r in range(d):
                rs = slice(r * BLK, (r + 1) * BLK)
                _stage_rows(sk_ref, r, d, slice(None), ck_ref[rs, :])
                _stage_rows(sv_ref, r, d, slice(None), cv_ref[rs, :])

        dk_ref[0] = sk_ref[...].astype(dk_ref.dtype)
        dv_ref[0] = sv_ref[...].astype(dv_ref.dtype)

    def cur(c0):
        return pl.BlockSpec(blk, lambda b, j, i: (b, jnp.minimum(i, nb - 1), c0 + j))

    def prev(c0):
        return pl.BlockSpec(blk, lambda b, j, i: (b, jnp.clip(i - 1, 0, nb - 1), c0 + j))

    vec = pl.BlockSpec((1, HEAD), lambda b, j, i: (0, 0))
    at_q = pl.BlockSpec(blk, lambda b, j, i: (b, jnp.minimum(i, nb - 1), j))
    at_k = pl.BlockSpec(blk, lambda b, j, i: (b, jnp.maximum(i - 1, 0), j))
    shp = jax.ShapeDtypeStruct((Bl, S, GW), MXU_DTYPE)
    gshp = jax.ShapeDtypeStruct((1, HEAD), F32)
    return _call(body, name=f"attn_bwd_g{g}", grid=(Bl, ncb, nb + 1),
                 in_specs=[cur(cq), prev(ck), cur(ck), prev(cv), cur(cv), vec, vec, at_q, at_q, at_q, at_q],
                 out_specs=[at_q, at_k, at_k, vec, vec], out_shape=[shp, shp, shp, gshp, gshp],
                 scratch_shapes=[pltpu.VMEM(blk[1:], F32)] * 5 + _proj_stages(blk, d),
                 )(proj3, proj3, proj3, proj3, proj3, gq, gk, o3, l3, do3, dl3)


def _combine_fwd(os, ls, proj2):
    T = proj2.shape[0]
    tr = _tile(T, 512)

    def body(o1, o2, o3, l1, l2, l3, z, a_ref, at_ref):
        a = _combine(o1[...], o2[...], o3[...], l1[...], l2[...], l3[...], z[...].astype(F32))
        a_ref[...] = a.astype(a_ref.dtype)
        at_ref[...] = a.T.astype(at_ref.dtype)

    row = pl.BlockSpec((tr, GW), lambda i: (i, 0))
    return _call(body, name="combine_fwd", grid=(T // tr,),
                 in_specs=[row] * 6 + [pl.BlockSpec((tr, GW), lambda i: (i, ZA // GW))],
                 out_specs=[row, pl.BlockSpec((GW, tr), lambda i: (0, i))],
                 out_shape=[jax.ShapeDtypeStruct((T, GW), MXU_DTYPE), jax.ShapeDtypeStruct((GW, T), MXU_DTYPE)],
                 )(*os, *ls, proj2)


def _combine_bwd(os, ls, proj2, da, dproj):
    T = proj2.shape[0]
    tr = _tile(T, 256)

    def body(o1, o2, o3, l1, l2, l3, z, da_ref, _, d1, d2, d3, e1, e2, e3, dz_ref):
        _, vjp = jax.vjp(_combine, o1[...], o2[...], o3[...], l1[...], l2[...], l3[...], z[...].astype(F32))
        go1, go2, go3, gl1, gl2, gl3, gz = vjp(da_ref[...])
        d1[...], d2[...], d3[...] = go1, go2, go3
        dz_ref[...] = gz.astype(dz_ref.dtype)
        for ref, gl in ((e1, gl1), (e2, gl2), (e3, gl3)):
            for h in range(HPG):
                sl = slice(h * HEAD, (h + 1) * HEAD)
                ref[:, sl] = jnp.broadcast_to(jnp.sum(gl[:, sl], axis=-1, keepdims=True), (tr, HEAD))

    row = pl.BlockSpec((tr, GW), lambda i: (i, 0))
    f = jax.ShapeDtypeStruct((T, GW), F32)
    z_attn = pl.BlockSpec((tr, GW), lambda i: (i, ZA // GW))
    outs = _call(body, name="combine_bwd", grid=(T // tr,), in_specs=[row] * 6 + [z_attn, row, ANY],
                 out_specs=[row] * 6 + [z_attn], out_shape=[f] * 6 + [jax.ShapeDtypeStruct(dproj.shape, dproj.dtype)],
                 aliases={8: 6})(*os, *ls, proj2, da, dproj)
    return outs[:3], outs[3:6], outs[6]


def _shift_down(u, j, t):
    return jnp.where(t >= j, pltpu.roll(u, j, 0), 0.0)


def _shift_up(u, j, t):
    n = u.shape[0]
    return jnp.where(t < n - j, pltpu.roll(u, n - j, 0), 0.0)


def _conv_specs(Bl, S, cw):
    def sec(c0):
        return pl.BlockSpec((1, S, cw), lambda j, b: (b, 0, c0 // cw + j))
    return [sec(CB), sec(CC), sec(CV), sec(ZC)], pl.BlockSpec((3, cw), lambda j, b: (0, j))


def _conv_fwd(proj3, conv_w):
    Bl, S, _ = proj3.shape
    cw = 256
    secs, wspec = _conv_specs(Bl, S, cw)

    def body(b_ref, c_ref, v_ref, z_ref, w_ref, o_ref, ot_ref):
        t = lax.broadcasted_iota(jnp.int32, (S, cw), 0)
        u = c_ref[0].astype(F32) * v_ref[0].astype(F32)
        y = w_ref[0:1, :] * u + w_ref[1:2, :] * _shift_down(u, 1, t) + w_ref[2:3, :] * _shift_down(u, 2, t)
        out = b_ref[0].astype(F32) * y * _silu(z_ref[0].astype(F32))
        o_ref[0] = out.astype(o_ref.dtype)
        ot_ref[...] = out.T.astype(ot_ref.dtype)

    return _call(body, name="conv_fwd", grid=(CONVW // cw, Bl), in_specs=secs + [wspec],
                 out_specs=[pl.BlockSpec((1, S, cw), lambda j, b: (b, 0, j)), pl.BlockSpec((cw, S), lambda j, b: (j, b))],
                 out_shape=[jax.ShapeDtypeStruct((Bl, S, CONVW), MXU_DTYPE),
                            jax.ShapeDtypeStruct((CONVW, Bl * S), MXU_DTYPE)])(proj3, proj3, proj3, proj3, conv_w)


def _conv_bwd(proj3, conv_w, dcc3, dproj):
    Bl, S, _ = proj3.shape
    cw = 256
    secs, wspec = _conv_specs(Bl, S, cw)

    def body(b_ref, c_ref, v_ref, z_ref, w_ref, d_ref, _, dproj_ref, dw_ref, stage, sems):
        t = lax.broadcasted_iota(jnp.int32, (S, cw), 0)
        bv, cv, vv, zv = (r[0].astype(F32) for r in (b_ref, c_ref, v_ref, z_ref))
        dv = d_ref[0]
        u = cv * vv
        u1, u2 = _shift_down(u, 1, t), _shift_down(u, 2, t)
        y = w_ref[0:1, :] * u + w_ref[1:2, :] * u1 + w_ref[2:3, :] * u2
        sg = _sig(zv)
        sz = zv * sg
        gy = dv * bv * sz
        du = w_ref[0:1, :] * gy + w_ref[1:2, :] * _shift_up(gy, 1, t) + w_ref[2:3, :] * _shift_up(gy, 2, t)
        j, b = pl.program_id(0), pl.program_id(1)
        tiles = [dv * y * sz, du * vv, du * cv, dv * bv * y * sg * (1.0 + zv * (1.0 - sg))]
        dsts = [dproj_ref.at[pl.ds(b * S, S), pl.ds(c0 + j * cw, cw)] for c0 in (CB, CC, CV, ZC)]
        _emit_tiles(j * Bl + b, (CONVW // cw) * Bl, tiles, dsts, stage, sems)

        @pl.when(pl.program_id(1) == 0)
        def _():
            dw_ref[...] = jnp.zeros_like(dw_ref)

        dw_ref[0:1, :] += jnp.sum(gy * u, axis=0, keepdims=True)
        dw_ref[1:2, :] += jnp.sum(gy * u1, axis=0, keepdims=True)
        dw_ref[2:3, :] += jnp.sum(gy * u2, axis=0, keepdims=True)

    blk = pl.BlockSpec((1, S, cw), lambda j, b: (b, 0, j))
    return _call(body, name="conv_bwd", grid=(CONVW // cw, Bl), in_specs=secs + [wspec, blk, ANY],
                 out_specs=[ANY, wspec],
                 out_shape=[jax.ShapeDtypeStruct(dproj.shape, dproj.dtype), jax.ShapeDtypeStruct((3, CONVW), F32)],
                 scratch_shapes=_emit_scratch(4, S, cw), aliases={6: 0})(proj3, proj3, proj3, proj3, conv_w, dcc3, dproj)


def _mem_specs(S, tq):
    q = pl.BlockSpec((1, tq, MEMW), lambda b, j: (b, j, MQ // MEMW))
    z = pl.BlockSpec((1, tq, MEMW), lambda b, j: (b, j, ZM // MEMW))
    kv = pl.BlockSpec((1, MEM_HD, 2 * MEMW), lambda b, j: (b, 0, 0))
    vec = pl.BlockSpec((1, MEM_HD), lambda b, j: (0, 0))
    blk = pl.BlockSpec((1, tq, MEMW), lambda b, j: (b, j, 0))
    return q, z, kv, vec, blk


def _mem_fwd(proj3, mkv3, gq, gk):
    Bl, S, _ = proj3.shape
    tq = _tile(S, 512)
    q, z, kv, vec, blk = _mem_specs(S, tq)

    def body(q_ref, z_ref, kv_ref, gq_ref, gk_ref, o_ref, ot_ref):
        out = _mem_block(q_ref[0].astype(F32), z_ref[0].astype(F32), kv_ref[0], gq_ref[...], gk_ref[...])
        o_ref[0] = out.astype(o_ref.dtype)
        ot_ref[...] = out.T.astype(ot_ref.dtype)

    nq = S // tq
    return _call(body, name="mem_fwd", grid=(Bl, nq), in_specs=[q, z, kv, vec, vec],
                 out_specs=[blk, pl.BlockSpec((MEMW, tq), lambda b, j: (0, b * nq + j))],
                 out_shape=[jax.ShapeDtypeStruct((Bl, S, MEMW), MXU_DTYPE),
                            jax.ShapeDtypeStruct((MEMW, Bl * S), MXU_DTYPE)])(proj3, proj3, mkv3, gq, gk)


def _mem_bwd(proj3, mkv3, gq, gk, dmo3, dproj):
    Bl, S, _ = proj3.shape
    tq = _tile(S, 256)
    q, z, kv, vec, blk = _mem_specs(S, tq)
    nq = S // tq

    def body(q_ref, z_ref, kv_ref, gq_ref, gk_ref, d_ref, _, dproj_ref, dkv_ref, dgq_ref, dgk_ref, stage, sems):
        _, vjp = jax.vjp(_mem_block, q_ref[0].astype(F32), z_ref[0].astype(F32), kv_ref[0], gq_ref[...], gk_ref[...])
        dq, dz, dkv, dgq, dgk = vjp(d_ref[0])
        j = pl.program_id(1)
        rows = pl.ds(pl.program_id(0) * S + j * tq, tq)
        dsts = [dproj_ref.at[rows, pl.ds(MQ, MEMW)], dproj_ref.at[rows, pl.ds(ZM, MEMW)]]
        _emit_tiles(pl.program_id(0) * nq + j, Bl * nq, [dq, dz], dsts, stage, sems)

        @pl.when(j == 0)
        def _():
            dkv_ref[0] = jnp.zeros_like(dkv)

        @pl.when((j == 0) & (pl.program_id(0) == 0))
        def _():
            dgq_ref[...] = jnp.zeros_like(dgq_ref)
            dgk_ref[...] = jnp.zeros_like(dgk_ref)

        dkv_ref[0] += dkv
        dgq_ref[...] += dgq
        dgk_ref[...] += dgk

    gshp = jax.ShapeDtypeStruct((1, MEM_HD), F32)
    return _call(body, name="mem_bwd", grid=(Bl, nq), in_specs=[q, z, kv, vec, vec, blk, ANY],
                 out_specs=[ANY, kv, vec, vec],
                 out_shape=[jax.ShapeDtypeStruct(dproj.shape, dproj.dtype), jax.ShapeDtypeStruct(mkv3.shape, F32),
                            gshp, gshp],
                 scratch_shapes=_emit_scratch(2, tq, MEMW), aliases={6: 0})(proj3, proj3, mkv3, gq, gk, dmo3, dproj)


def _merge_specs(T, D, tm, tn):
    def act(w):
        return pl.BlockSpec((tm, w), lambda i, n: (i, 0))

    def wsp(w):
        return pl.BlockSpec((w, tn), lambda i, n: (0, n))

    gates = [pl.BlockSpec((tm, tn), lambda i, n, k=k: (i, (G0 + k * D) // tn + n)) for k in range(3)]
    tile = pl.BlockSpec((tm, tn), lambda i, n: (i, n))
    return act, wsp, gates, tile


def _merge_fwd(a, cc, mo, wa, wc, wm, proj2):
    T, D = a.shape[0], wa.shape[1]
    tm, tn = _tile(T, 1024), _tile(D, 512)
    act, wsp, gates, tile = _merge_specs(T, D, tm, tn)

    def body(a_ref, c_ref, m_ref, wa_ref, wc_ref, wm_ref, g0, g1, g2, mg_ref, mt_ref, pa_ref, pc_ref, pm_ref):
        pa = jnp.dot(a_ref[...], wa_ref[...], preferred_element_type=F32)
        pc = jnp.dot(c_ref[...], wc_ref[...], preferred_element_type=F32)
        pm = jnp.dot(m_ref[...], wm_ref[...], preferred_element_type=F32)
        mg = _sig(g0[...].astype(F32)) * pa + _sig(g1[...].astype(F32)) * pc + _sig(g2[...].astype(F32)) * pm
        mg_ref[...] = mg.astype(mg_ref.dtype)
        mt_ref[...] = mg.T.astype(mt_ref.dtype)
        pa_ref[...] = pa.astype(pa_ref.dtype)
        pc_ref[...] = pc.astype(pc_ref.dtype)
        pm_ref[...] = pm.astype(pm_ref.dtype)

    shp = jax.ShapeDtypeStruct((T, D), MXU_DTYPE)
    return _call(body, name="merge_fwd", grid=(T // tm, D // tn),
                 in_specs=[act(GW), act(CONVW), act(MEMW), wsp(GW), wsp(CONVW), wsp(MEMW)] + gates,
                 out_specs=[tile, pl.BlockSpec((tn, tm), lambda i, n: (n, i)), tile, tile, tile],
                 out_shape=[shp, jax.ShapeDtypeStruct((D, T), MXU_DTYPE), shp, shp, shp],
                 )(a, cc, mo, wa, wc, wm, proj2, proj2, proj2)


def _emit_tiles(step, nsteps, tiles, dsts, stage, sems):
    slot = step % 2

    def copies(s):
        return [pltpu.make_async_copy(stage.at[s, k], dsts[k], sems.at[s, k]) for k in range(len(tiles))]

    @pl.when(step >= 2)
    def _():
        for cp in copies(slot):
            cp.wait()

    for k, t in enumerate(tiles):
        stage[slot, k] = t.astype(stage.dtype)
    for cp in copies(slot):
        cp.start()

    @pl.when(step == nsteps - 1)
    def _():
        for cp in copies(slot):
            cp.wait()
        if nsteps > 1:
            for cp in copies(1 - slot):
                cp.wait()


def _emit_scratch(k, rows, cols):
    return [pltpu.VMEM((2, k, rows, cols), MXU_DTYPE), pltpu.SemaphoreType.DMA((2, k))]


def _merge_bwd(dyb, w_out, proj2, pa, pc, pm):
    T, D = dyb.shape
    IN = proj2.shape[1]
    tm, tn = _tile(T, 1024), _tile(D, 512)
    _, _, gates, tile = _merge_specs(T, D, tm, tn)
    nn = D // tn

    def body(dy_ref, w_ref, g0, g1, g2, p0, p1, p2, dp0, dp1, dp2, dproj_ref, stage, sems):
        i, n = pl.program_id(0), pl.program_id(1)
        dm = lax.dot_general(dy_ref[...], w_ref[...], _DIMS["nt"], preferred_element_type=F32)
        tiles, dsts = [], []
        for k, (g_ref, p_ref, dp_ref) in enumerate(((g0, p0, dp0), (g1, p1, dp1), (g2, p2, dp2))):
            gt = _sig(g_ref[...].astype(F32))
            dp_ref[...] = (gt * dm).astype(dp_ref.dtype)
            tiles.append(dm * p_ref[...].astype(F32) * gt * (1.0 - gt))
            dsts.append(dproj_ref.at[pl.ds(i * tm, tm), pl.ds(G0 + k * D + n * tn, tn)])
        _emit_tiles(i * nn + n, (T // tm) * nn, tiles, dsts, stage, sems)

    shp = jax.ShapeDtypeStruct((T, D), MXU_DTYPE)
    return _call(body, name="merge_bwd", grid=(T // tm, nn),
                 in_specs=[pl.BlockSpec((tm, D), lambda i, n: (i, 0)), pl.BlockSpec((tn, D), lambda i, n: (n, 0))]
                 + gates + [tile] * 3,
                 out_specs=[tile] * 3 + [ANY], out_shape=[shp] * 3 + [jax.ShapeDtypeStruct((T, IN), MXU_DTYPE)],
                 scratch_shapes=_emit_scratch(3, tm, tn))(dyb, w_out, proj2, proj2, proj2, pa, pc, pm)


def _out_loss(merged, w_out, x, tgt):
    T, D = x.shape
    tm = _tile(T, 512)

    def body(m_ref, w_ref, x_ref, t_ref, dy_ref, dyb_ref, loss_ref):
        err = x_ref[...] + jnp.dot(m_ref[...], w_ref[...], preferred_element_type=F32) - t_ref[...]
        dy = err * (1.0 / D)
        dy_ref[...] = dy
        dyb_ref[...] = dy.astype(dyb_ref.dtype)

        @pl.when(pl.program_id(0) == 0)
        def _():
            loss_ref[...] = jnp.zeros_like(loss_ref)

        loss_ref[...] += jnp.sum(err * err) * (0.5 / D)

    row = pl.BlockSpec((tm, D), lambda i: (i, 0))
    return _call(body, name="out_loss", grid=(T // tm,),
                 in_specs=[row, pl.BlockSpec((D, D), lambda i: (0, 0)), row, row],
                 out_specs=[row, row, pl.BlockSpec((1, 128), lambda i: (0, 0))],
                 out_shape=[jax.ShapeDtypeStruct((T, D), F32), jax.ShapeDtypeStruct((T, D), MXU_DTYPE),
                            jax.ShapeDtypeStruct((1, 128), F32)])(merged, w_out, x, tgt)


def _proj_chunk(hb, w, meta, j, nslots, half, buf, name):
    T, D = hb.shape
    Cs = w.shape[1] // 4
    tm, tn = _tile(T, 1024), _tile(Cs // 2, 2176)
    nh = Cs // 2 // tn
    per = nh if half is not None else 2 * nh

    def body(meta_ref, a_ref, b_ref, *rest):
        rest[-1][...] = jnp.dot(a_ref[...], b_ref[...], preferred_element_type=F32).astype(rest[-1].dtype)

    def tile(n, m):
        if half is None:
            return n % per
        return (m[4] if half == 0 else 1 - m[4]) * nh + n % per

    in_specs = [pl.BlockSpec((tm, D), lambda n, i, m: (i, 0)),
                pl.BlockSpec((D, tn), lambda n, i, m: (0, (j + n // per) * 2 * nh + tile(n, m)))]
    args = [meta, hb, w]
    if buf is not None:
        in_specs.append(ANY)
        args.append(buf)
    spec = pltpu.PrefetchScalarGridSpec(
        num_scalar_prefetch=1, grid=(nslots * per, T // tm), in_specs=in_specs,
        out_specs=pl.BlockSpec((tm, tn), lambda n, i, m: (i, m[j + n // per] * 2 * nh + tile(n, m))))
    return _call(body, name=name, grid_spec=spec, out_shape=jax.ShapeDtypeStruct((T, 4 * Cs), PROJ_DTYPE),
                 aliases={} if buf is None else {3: 0})(*args)


def _norms(x, mem, norm_g, mem_norm_g):
    D = x.shape[-1]
    hb, hbt = _rms_fwd(x.reshape(-1, D), norm_g.reshape(1, D), "rms_x")
    mhb, _ = _rms_fwd(mem.reshape(-1, D), mem_norm_g.reshape(1, D), "rms_mem")
    return hb, hbt, mhb


def _attention_fwd(proj2, Bl, gq_all, gk_all):
    T, IN = proj2.shape
    proj3 = proj2.reshape(Bl, T // Bl, IN)
    os, ls = [], []
    for g, d in enumerate(DILATIONS):
        o, l = _attn_fwd(proj3, gq_all[g:g + 1], gk_all[g:g + 1], g, d)
        os.append(o.reshape(T, GW))
        ls.append(l.reshape(T, GW))
    return os, ls, _combine_fwd(os, ls, proj2)


def _conv_branch_fwd(proj2, Bl, conv_w):
    T, IN = proj2.shape
    cc, cct = _conv_fwd(proj2.reshape(Bl, T // Bl, IN), conv_w)
    return cc.reshape(T, CONVW), cct


def _weight_grads(x, mem, tgt, norm_g, mem_norm_g, gq_all, gk_all, conv_w, mem_gq, mem_gk, W, pre, early=None):
    Bl, S, D = x.shape
    T = Bl * S
    hb, hbt, mhb, proj2, os, ls, (a, at), (cc, cct) = pre
    IN = proj2.shape[1]
    proj3 = proj2.reshape(Bl, S, IN)
    x2, tgt2 = x.reshape(T, D), tgt.reshape(T, D)
    mem2 = mem.reshape(-1, D)
    ng, mng = norm_g.reshape(1, D), mem_norm_g.reshape(1, D)
    mgq, mgk = mem_gq.reshape(1, MEM_HD), mem_gk.reshape(1, MEM_HD)
    gqs = [gq_all[g:g + 1] for g in range(NGROUP)]
    gks = [gk_all[g:g + 1] for g in range(NGROUP)]

    mkv = _matmul(mhb, W["mem_w_kv"], "nn", F32, name="mem_kv", tm=512, tn=1024, tk=D)
    mkv3 = mkv.reshape(Bl, -1, 2 * MEMW)
    mo, mot = _mem_fwd(proj3, mkv3, mgq, mgk)
    mo = mo.reshape(T, MEMW)
    merged, mergedt, pa, pc, pm = _merge_fwd(a, cc, mo, W["w_br_attn"], W["w_br_conv"], W["w_br_mem"], proj2)
    dy, dyb, loss = _out_loss(merged, W["w_out"], x2, tgt2)

    G = {}
    G["w_out"] = _matmul(mergedt, dyb, "nn", WIRE_DTYPE, name="dw_out", tm=1024, tn=512, tk=T)
    dpa, dpc, dpm, dproj = _merge_bwd(dyb, W["w_out"], proj2, pa, pc, pm)
    G["w_br_attn"] = _matmul(at, dpa, "nn", WIRE_DTYPE, name="dw_br_attn", tm=512, tn=512, tk=T)
    G["w_br_conv"] = _matmul(cct, dpc, "nn", WIRE_DTYPE, name="dw_br_conv", tm=1024, tn=512, tk=T)
    G["w_br_mem"] = _matmul(mot, dpm, "nn", WIRE_DTYPE, name="dw_br_mem", tm=1024, tn=512, tk=T)
    da = _matmul(dpa, W["w_br_attn"], "nt", F32, name="d_attn", tm=1024, tn=512, tk=D)
    dcc = _matmul(dpc, W["w_br_conv"], "nt", F32, name="d_conv", tm=1024, tn=1024, tk=D)
    dmo = _matmul(dpm, W["w_br_mem"], "nt", F32, name="d_mem", tm=1024, tn=1024, tk=D)
    dproj, dmkv3, dmgq, dmgk = _mem_bwd(proj3, mkv3, mgq, mgk, dmo.reshape(Bl, S, MEMW), dproj)
    dmkv = _cast(dmkv3.reshape(-1, 2 * MEMW), "cast_dmkv")
    G["mem_w_kv"] = _matmul(mhb, dmkv, "tn", WIRE_DTYPE, name="dw_mem_kv", tm=1024, tn=1024, tk=512)
    early_state, da = (None, da) if early is None else early(G, da)
    dmh = _matmul(dmkv, W["mem_w_kv"], "nt", F32, name="d_memh", tm=512, tn=1024, tk=2 * MEMW)
    _, dmng = _rms_bwd(mem2, dmh, mng, None, "rms_mem_bwd")

    dos, dls, dproj = _combine_bwd(os, ls, proj2, da, dproj)
    dgq, dgk = [], []
    for g, d in enumerate(DILATIONS):
        dq, dk, dv, gq_g, gk_g = _attn_bwd(proj3, gqs[g], gks[g], os[g].reshape(Bl, S, GW), ls[g].reshape(Bl, S, GW),
                                           dos[g].reshape(Bl, S, GW), dls[g].reshape(Bl, S, GW), g, d)
        for c0, part in ((Q0, dq), (K0, dk), (V0, dv)):
            dproj = lax.dynamic_update_slice(dproj, part.reshape(T, GW), (0, c0 + g * GW))
        dgq.append(gq_g)
        dgk.append(gk_g)
    dproj, dconv_w = _conv_bwd(proj3, conv_w, dcc.reshape(Bl, S, CONVW), dproj)
    small = [loss, None, dmng] + dgq + dgk + [dconv_w.reshape(1, 3 * CONVW), dmgq, dmgk]
    return G, (dproj, x2, ng, dy, small), early_state


def _dw_in_half(hbt, dproj, pos, own, name):
    D, T = hbt.shape
    IN = dproj.shape[1]
    R, tn = D // 2, _tile(IN, 1024)

    def body(pos_ref, a_ref, b_ref, o_ref):
        o_ref[...] = jnp.dot(a_ref[...], b_ref[...], preferred_element_type=F32).astype(o_ref.dtype)

    spec = pltpu.PrefetchScalarGridSpec(
        num_scalar_prefetch=1, grid=(IN // tn,),
        in_specs=[pl.BlockSpec((R, T), lambda j, p: (p[1] if own else 1 - p[1], 0)),
                  pl.BlockSpec((T, tn), lambda j, p: (0, j))],
        out_specs=pl.BlockSpec((R, tn), lambda j, p: (0, j)))
    return _call(body, name=name, grid_spec=spec, out_shape=jax.ShapeDtypeStruct((R, IN), WIRE_DTYPE))(pos, hbt, dproj)


def _d_h(dproj, w, order):
    T, IN = dproj.shape
    D, Cs = w.shape[0], IN // 4
    tm, tn = _tile(T, 1024), _tile(D, 1024)

    def body(order_ref, a_ref, b_ref, o_ref, acc_ref):
        part = lax.dot_general(a_ref[...], b_ref[...], _DIMS["nt"], preferred_element_type=F32)
        k = pl.program_id(2)

        @pl.when(k == 0)
        def _():
            acc_ref[...] = part

        @pl.when(k > 0)
        def _():
            acc_ref[...] += part

        @pl.when(k == 3)
        def _():
            o_ref[...] = acc_ref[...]

    spec = pltpu.PrefetchScalarGridSpec(
        num_scalar_prefetch=1, grid=(T // tm, D // tn, 4),
        in_specs=[pl.BlockSpec((tm, Cs), lambda i, n, k, o: (i, o[k])), pl.BlockSpec((tn, Cs), lambda i, n, k, o: (n, k))],
        out_specs=pl.BlockSpec((tm, tn), lambda i, n, k, o: (i, n)), scratch_shapes=[pltpu.VMEM((tm, tn), F32)])
    return _call(body, name="d_h", grid_spec=spec, out_shape=jax.ShapeDtypeStruct((T, D), F32))(order, dproj, w)


def _input_grad(rest, w_in, order):
    dproj, x2, ng, dy, small = rest
    dh = _d_h(dproj, w_in, order)
    grad_x, dng = _rms_bwd(x2, dh, ng, dy, "rms_x_bwd")
    small = [dng if t is None else t for t in small]
    return grad_x, jnp.concatenate(small, axis=1)


def _local_step(x, mem, tgt, norm_g, mem_norm_g, gq_all, gk_all, conv_w, mem_gq, mem_gk, W):
    hb, hbt, mhb = _norms(x, mem, norm_g, mem_norm_g)
    Cs = W["w_in"].shape[1] // 4
    shards = (0, 2, 1, 3)
    order = jnp.array(shards, dtype=jnp.int32)
    w_rel = jnp.concatenate([W["w_in"][:, s * Cs:(s + 1) * Cs] for s in shards], axis=1)
    meta = jnp.array(shards + (0,), dtype=jnp.int32)
    proj2 = _proj_chunk(hb, w_rel, meta, 0, 1, None, None, "proj_0")
    for j, nslots in ((1, 2), (3, 1)):
        for half in (1, 0):
            proj2 = _proj_chunk(hb, w_rel, meta, j, nslots, half, proj2, f"proj_{j}_{half}")
    pre = (hb, hbt, mhb, proj2, *_attention_fwd(proj2, x.shape[0], gq_all, gk_all),
           _conv_branch_fwd(proj2, x.shape[0], conv_w))
    G, rest, _ = _weight_grads(x, mem, tgt, norm_g, mem_norm_g, gq_all, gk_all, conv_w, mem_gq, mem_gk, W, pre)
    pos = jnp.zeros((2,), jnp.int32)
    G["w_in"] = jnp.concatenate([_dw_in_half(hbt, rest[0], pos, True, "dw_in_own"),
                                 _dw_in_half(hbt, rest[0], pos, False, "dw_in_sibling")], axis=0)
    grad_x, small = _input_grad(rest, w_rel, order)
    return grad_x.reshape(x.shape), G, small


BIG = (("w_in", "col"), ("mem_w_kv", "row"), ("w_br_attn", "col"), ("w_br_conv", "col"),
       ("w_br_mem", "col"), ("w_out", "row"))


def _coords():
    return lax.axis_index("x"), lax.axis_index("y"), lax.axis_index("c")


def _other_chips(x, y):
    return [(1 - x, y), (x, 1 - y), (1 - x, 1 - y)]


def _half(ref, kind, c):
    R, C = ref.shape
    if kind == "col":
        return ref.at[pl.ds(c * (R // 2), R // 2), :]
    return ref.at[:, pl.ds(c * (C // 2), C // 2)]


def _shard(ref, kind, s):
    R, C = ref.shape
    if kind == "col":
        return ref.at[:, pl.ds(s * (C // 4), C // 4)]
    return ref.at[pl.ds(s * (R // 4), R // 4), :]


def _piece(ref, kind, s, c):
    R, C = ref.shape
    if kind == "col":
        return ref.at[pl.ds(c * (R // 2), R // 2), pl.ds(s * (C // 4), C // 4)]
    return ref.at[pl.ds(s * (R // 4), R // 4), pl.ds(c * (C // 2), C // 2)]


def _remote(src, dst, sems_s, sems_r, k, dev):
    return pltpu.make_async_remote_copy(src_ref=src, dst_ref=dst, send_sem=sems_s.at[k], recv_sem=sems_r.at[k],
                                        device_id=dev, device_id_type=MESH)


HBM = pl.BlockSpec(memory_space=pltpu.HBM)
SEM = pl.BlockSpec(memory_space=pltpu.SEMAPHORE)
EFFECT = pltpu.SideEffectType.DATAFLOW_SIDE_EFFECTING


def _hbm(a):
    return pltpu.with_memory_space_constraint(a, pltpu.HBM)


def _start_copies(name, arrays, ncopies, make):
    n = len(arrays)

    def body(*refs):
        for cp in make(refs[:n], refs[n], refs[n + 1]):
            cp.start()

    outs = pl.pallas_call(
        body, name=name,
        out_shape=(pltpu.SemaphoreType.DMA((ncopies,)), pltpu.SemaphoreType.DMA((ncopies,)),
                   *[jax.ShapeDtypeStruct(t.shape, t.dtype) for t in arrays]),
        in_specs=[HBM] * n, out_specs=(SEM, SEM, *([HBM] * n)),
        input_output_aliases={i: i + 2 for i in range(n)},
        compiler_params=pltpu.CompilerParams(has_side_effects=EFFECT),
    )(*[_hbm(t) for t in arrays])
    return outs[0], outs[1], list(outs[2:])


def _wait_copies(name, send, recv, arrays, make, after):
    n = len(arrays)

    def body(*refs):
        for cp in make(refs[:n], refs[n], refs[n + 1]):
            cp.wait_send()
            cp.wait_recv()

    outs = pl.pallas_call(
        body, name=name, out_shape=[jax.ShapeDtypeStruct(t.shape, t.dtype) for t in arrays],
        in_specs=[HBM] * n + [SEM, SEM, ANY], out_specs=[HBM] * n,
        input_output_aliases={i: i for i in range(n)},
        compiler_params=pltpu.CompilerParams(has_side_effects=EFFECT),
    )(*arrays, send, recv, after)
    return list(outs)


def _w_in_copies(relations):
    def make(refs, send, recv):
        x, y, c = _coords()
        me = 2 * x + y
        chips = _other_chips(x, y)
        w, conv = refs[0], refs[1]
        cps = []
        for i, k in enumerate(relations):
            cps.append(_remote(_column_half(w, 0, c), _column_half(w, 1 + k, c), send, recv, 2 * i, (*chips[k], c)))
            mine = _shard(conv, "col", me)
            cps.append(_remote(mine, mine, send, recv, 2 * i + 1, (*chips[k], c)))
        return cps
    return make


def _column_half(w, slot, c):
    half = w.shape[1] // 8
    return w.at[:, pl.ds((2 * slot + c) * half, half)]


def _w_in_forward(relations):
    def make(refs, send, recv):
        x, y, c = _coords()
        cps = []
        for i, k in enumerate(relations):
            got = _column_half(refs[0], 1 + k, c)
            cps.append(_remote(got, got, send, recv, i, (x, y, 1 - c)))
        return cps
    return make


def _sibling_copy(refs, send, recv):
    x, y, c = _coords()
    return [_remote(refs[0], refs[1], send, recv, 0, (x, y, 1 - c))]


def _other_weight_copies(refs, send, recv):
    x, y, c = _coords()
    me = 2 * x + y
    cps = []
    for k, chip in enumerate(_other_chips(x, y)):
        for p, (_, kind) in enumerate(BIG[1:]):
            mine = _piece(refs[p], kind, me, c)
            cps.append(_remote(mine, mine, send, recv, 3 * p + k, (*chip, c)))
    return cps


def _other_weight_forward(refs, send, recv):
    x, y, c = _coords()
    cps = []
    for k, chip in enumerate(_other_chips(x, y)):
        s = 2 * chip[0] + chip[1]
        for p, (_, kind) in enumerate(BIG[1:]):
            got = _piece(refs[p], kind, s, c)
            cps.append(_remote(got, got, send, recv, 3 * p + k, (x, y, 1 - c)))
    return cps


def _share_copies(group):
    def make(refs, send, recv):
        x, y, c = _coords()
        cps = []
        for p, (_, kind) in enumerate(group):
            mine = _half(refs[p], kind, c)
            cps.append(_remote(mine, mine, send, recv, p, (x, y, 1 - c)))
        return cps
    return make


def _sibling_exchange(grads, group, name):
    n = len(group)
    shapes = []
    for (_, kind), g in zip(group, grads):
        R, C = g.shape
        shapes.append(jax.ShapeDtypeStruct((R // 2, C) if kind == "col" else (R, C // 2), g.dtype))

    def body(*refs):
        ins, outs = refs[:n], refs[n:2 * n]
        send, recv = refs[2 * n:]
        x, y, c = _coords()
        sib = (x, y, 1 - c)
        cps = [_remote(_half(ins[p], group[p][1], 1 - c), outs[p], send, recv, p, sib) for p in range(n)]
        for cp in cps:
            cp.start()
        for cp in cps:
            cp.wait()

    return pl.pallas_call(
        body, name=name, out_shape=shapes, in_specs=[ANY] * n, out_specs=[ANY] * n,
        scratch_shapes=[pltpu.SemaphoreType.DMA((n,)), pltpu.SemaphoreType.DMA((n,))],
    )(*grads)


def _presum(g, got, kind, pos, name):
    R, C = got.shape
    tr, tc = _tile(R, 512, 16), _tile(C, 2048)
    nr, nc = R // tr, C // tc

    def body(pos_ref, a_ref, b_ref, o_ref):
        o_ref[...] = (a_ref[...].astype(F32) + b_ref[...].astype(F32)).astype(o_ref.dtype)

    blk = pl.BlockSpec((tr, tc), lambda i, j, pos_ref: (i, j))
    if g.shape == got.shape:
        mine = blk
    elif kind == "col":
        mine = pl.BlockSpec((tr, tc), lambda i, j, pos_ref: (pos_ref[1] * nr + i, j))
    else:
        mine = pl.BlockSpec((tr, tc), lambda i, j, pos_ref: (i, pos_ref[1] * nc + j))
    spec = pltpu.PrefetchScalarGridSpec(num_scalar_prefetch=1, grid=(nr, nc), in_specs=[mine, blk], out_specs=blk)
    return _call(body, name=name, grid_spec=spec, out_shape=jax.ShapeDtypeStruct((R, C), WIRE_DTYPE))(pos, g, got)


def _chip_copies(group):
    n = len(group)

    def make(refs, send, recv):
        x, y, c = _coords()
        cps = []
        for k, chip in enumerate(_other_chips(x, y)):
            s = 2 * chip[0] + chip[1]
            for p in range(n):
                cps.append(_remote(_shard(refs[p], group[p][1], s), refs[n + p].at[k], send, recv, 3 * p + k, (*chip, c)))
        return cps
    return make


def _landing_zones(pres, group):
    lands = []
    for (_, kind), g in zip(group, pres):
        R, C = g.shape
        lands.append(lax.empty((3, R, C // 4) if kind == "col" else (3, R // 4, C), g.dtype))
    return lands


def _exchange_start(G, group, pos, carry, tag, pres=None):
    n = len(group)
    if pres is None:
        parts = [G[name] for name, _ in group]
        got = _sibling_exchange(parts, group, "sibling_exchange_" + tag)
        pres = [_presum(parts[p], got[p], kind, pos, "presum_" + name) for p, (name, kind) in enumerate(group)]
    make = _chip_copies(group)
    send, recv, thru = _start_copies("chip_exchange_start_" + tag, [*pres, *_landing_zones(pres, group), carry], 3 * n, make)
    return (send, recv, thru[:2 * n], make, tag), thru[2 * n]


def _exchange_wait(state, after):
    send, recv, arrays, make, tag = state
    thru = _wait_copies("chip_exchange_wait_" + tag, send, recv, arrays, make, after)
    n = len(thru) // 2
    return thru[:n], thru[n:]


def _reduce_into_shard(slots, pre, kind, pos, name):
    K, R, C = slots.shape
    tr, tc = _tile(R, 512, 16), _tile(C, 2176)
    nr, nc = R // tr, C // tc

    def body(pos_ref, s_ref, p_ref, o_ref):
        acc = p_ref[...].astype(F32)
        for k in range(K):
            acc = acc + s_ref[k].astype(F32)
        o_ref[...] = acc

    if kind == "col":
        own = pl.BlockSpec((tr, tc), lambda i, j, pos_ref: (i, pos_ref[0] * nc + j))
        full, out = (2 * R, C), pl.BlockSpec((tr, tc), lambda i, j, pos_ref: (pos_ref[1] * nr + i, j))
    else:
        own = pl.BlockSpec((tr, tc), lambda i, j, pos_ref: (pos_ref[0] * nr + i, j))
        full, out = (R, 2 * C), pl.BlockSpec((tr, tc), lambda i, j, pos_ref: (i, pos_ref[1] * nc + j))
    spec = pltpu.PrefetchScalarGridSpec(
        num_scalar_prefetch=1, grid=(nr, nc),
        in_specs=[pl.BlockSpec((K, tr, tc), lambda i, j, pos_ref: (0, i, j)), own], out_specs=out)
    return _call(body, name=name, grid_spec=spec, out_shape=jax.ShapeDtypeStruct(full, F32))(pos, slots, pre)


def _gather_small(pack):
    _, N = pack.shape

    def body(in_ref, out_ref, send, recv, loc):
        x, y, c = _coords()
        me = 4 * x + 2 * y + c
        own = pltpu.make_async_copy(in_ref, out_ref.at[me], loc)
        own.start()
        cps = []
        for k in range(1, 8):
            dev = (x ^ (k >> 2), y ^ ((k >> 1) & 1), c ^ (k & 1))
            cps.append(_remote(in_ref, out_ref.at[me], send, recv, k - 1, dev))
        for cp in cps:
            cp.start()
        for k in range(1, 8):
            src = 4 * (x ^ (k >> 2)) + 2 * (y ^ ((k >> 1) & 1)) + (c ^ (k & 1))
            _remote(in_ref, out_ref.at[src], send, recv, k - 1, (x, y, c)).wait_recv()
        for cp in cps:
            cp.wait_send()
        own.wait()

    return pl.pallas_call(
        body, name="gather_small", out_shape=jax.ShapeDtypeStruct((8, 1, N), pack.dtype),
        in_specs=[ANY], out_specs=ANY,
        scratch_shapes=[pltpu.SemaphoreType.DMA((7,)), pltpu.SemaphoreType.DMA((7,)), pltpu.SemaphoreType.DMA(())],
    )(pack)


def _sum_small(slots):
    K, _, N = slots.shape

    def body(s_ref, o_ref):
        acc = s_ref[0]
        for k in range(1, K):
            acc = acc + s_ref[k]
        o_ref[...] = acc

    return _call(body, name="sum_small", in_specs=[pl.BlockSpec(memory_space=pltpu.VMEM)],
                 out_specs=pl.BlockSpec(memory_space=pltpu.VMEM), out_shape=jax.ShapeDtypeStruct((1, N), F32))(slots)


def _adamw(w, g, m, v, name, with_grad=False):
    R, C = w.shape
    tr, tc = _tile(R, 256, 8), _tile(C, 2176)

    def body(w_ref, g_ref, m_ref, v_ref, d_ref, nm_ref, nv_ref, *g_out):
        gv = g_ref[...]
        for ref in g_out:
            ref[...] = gv
        nm = ADAM_B1 * m_ref[...] + (1.0 - ADAM_B1) * gv
        nv = ADAM_B2 * v_ref[...] + (1.0 - ADAM_B2) * gv * gv
        m_hat = nm / (1.0 - ADAM_B1 ** ADAM_STEP)
        v_hat = nv / (1.0 - ADAM_B2 ** ADAM_STEP)
        d_ref[...] = -ADAM_LR * (m_hat / (jnp.sqrt(v_hat) + ADAM_EPS) + ADAM_WD * w_ref[...])
        nm_ref[...] = nm
        nv_ref[...] = nv

    spec = pl.BlockSpec((tr, tc), lambda i, j: (i, j))
    shp = jax.ShapeDtypeStruct((R, C), F32)
    nout = 4 if with_grad else 3
    return _call(body, name=name, grid=(R // tr, C // tc), in_specs=[spec] * 4, out_specs=[spec] * nout,
                 out_shape=[shp] * nout)(w, g, m, v)


SMALL = ("norm_g", "mem_norm_g", "attn_q_norm", "attn_k_norm", "conv_w", "mem_q_norm", "mem_k_norm")
WEIGHTS = ("norm_g", "mem_norm_g", "w_in", "attn_q_norm", "attn_k_norm", "conv_w", "mem_w_kv", "mem_q_norm",
           "mem_k_norm", "w_br_attn", "w_br_conv", "w_br_mem", "w_out")


def kernel(x, mem, norm_g, mem_norm_g, w_in, attn_q_norm, attn_k_norm, conv_w, mem_w_kv, mem_q_norm, mem_k_norm, w_br_attn, w_br_conv, w_br_mem, w_out, loss_target, m_norm_g, m_mem_norm_g, m_w_in, m_attn_q_norm, m_attn_k_norm, m_conv_w, m_mem_w_kv, m_mem_q_norm, m_mem_k_norm, m_w_br_attn, m_w_br_conv, m_w_br_mem, m_w_out, v_norm_g, v_mem_norm_g, v_w_in, v_attn_q_norm, v_attn_k_norm, v_conv_w, v_mem_w_kv, v_mem_q_norm, v_mem_k_norm, v_w_br_attn, v_w_br_conv, v_w_br_mem, v_w_out):
    w = dict(norm_g=norm_g, mem_norm_g=mem_norm_g, w_in=w_in, attn_q_norm=attn_q_norm, attn_k_norm=attn_k_norm,
             conv_w=conv_w, mem_w_kv=mem_w_kv, mem_q_norm=mem_q_norm, mem_k_norm=mem_k_norm, w_br_attn=w_br_attn,
             w_br_conv=w_br_conv, w_br_mem=w_br_mem, w_out=w_out)
    m = dict(norm_g=m_norm_g, mem_norm_g=m_mem_norm_g, w_in=m_w_in, attn_q_norm=m_attn_q_norm,
             attn_k_norm=m_attn_k_norm, conv_w=m_conv_w, mem_w_kv=m_mem_w_kv, mem_q_norm=m_mem_q_norm,
             mem_k_norm=m_mem_k_norm, w_br_attn=m_w_br_attn, w_br_conv=m_w_br_conv, w_br_mem=m_w_br_mem, w_out=m_w_out)
    v = dict(norm_g=v_norm_g, mem_norm_g=v_mem_norm_g, w_in=v_w_in, attn_q_norm=v_attn_q_norm,
             attn_k_norm=v_attn_k_norm, conv_w=v_conv_w, mem_w_kv=v_mem_w_kv, mem_q_norm=v_mem_q_norm,
             mem_k_norm=v_mem_k_norm, w_br_attn=v_w_br_attn, w_br_conv=v_w_br_conv, w_br_mem=v_w_br_mem, w_out=v_w_out)
    Bl, _, D = x.shape
    cx, cy = lax.axis_index("x"), lax.axis_index("y")
    chip = 2 * cx + cy
    pos = jnp.stack([chip, lax.axis_index("c")]).astype(jnp.int32)
    order = jnp.stack([chip] + [2 * a + b for a, b in _other_chips(cx, cy)]).astype(jnp.int32)
    n = len(BIG)

    slot0 = jnp.stack([jnp.zeros((), jnp.int32), pos[1]])
    w_rel = _place_shard(w["w_in"], "col", slot0, WIRE_DTYPE, "place_w_in_sent", half=0)
    conv_full = _place_shard(conv_w, "col", pos, F32, "place_conv_w")
    others = [_place_shard(w[name], kind, pos, WIRE_DTYPE, "place_" + name) for name, kind in BIG[1:]]
    hb, hbt, mhb = _norms(x, mem, norm_g, mem_norm_g)

    meta = jnp.concatenate([order, pos[1:]])
    near, near_fwd = _w_in_copies((0, 1)), _w_in_forward((0, 1))
    send, recv, (w_rel, conv_full) = _start_copies("gather_near_start", [w_rel, conv_full], 4, near)
    w_rel = _place_shard(w["w_in"], "col", slot0, WIRE_DTYPE, "place_w_in_kept", half=1, into=w_rel)
    proj = _proj_chunk(hb, w_rel, meta, 0, 1, None, None, "proj_own")
    w_rel, conv_full, *others = _wait_copies("gather_near_wait", send, recv, [w_rel, conv_full, *others], near, proj)

    fsend, frecv, (w_rel,) = _start_copies("gather_near_forward_start", [w_rel], 2, near_fwd)
    far, far_fwd = _w_in_copies((2,)), _w_in_forward((2,))
    send, recv, (w_rel, conv_full) = _start_copies("gather_far_start", [w_rel, conv_full], 2, far)
    proj = _proj_chunk(hb, w_rel, meta, 1, 2, 0, proj, "proj_near_landed")
    w_rel, = _wait_copies("gather_near_forward_wait", fsend, frecv, [w_rel], near_fwd, proj)
    proj = _proj_chunk(hb, w_rel, meta, 1, 2, 1, proj, "proj_near_forwarded")
    w_rel, conv_full = _wait_copies("gather_far_wait", send, recv, [w_rel, conv_full], far, proj)

    fsend, frecv, (w_rel,) = _start_copies("gather_far_forward_start", [w_rel], 1, far_fwd)
    send, recv, (*others, w_rel) = _start_copies("gather_rest_start", [*others, w_rel], 3 * (n - 1), _other_weight_copies)
    proj = _proj_chunk(hb, w_rel, meta, 3, 1, 0, proj, "proj_far_landed")
    w_rel, = _wait_copies("gather_far_forward_wait", fsend, frecv, [w_rel], far_fwd, proj)
    proj = _proj_chunk(hb, w_rel, meta, 3, 1, 1, proj, "proj_far_forwarded")
    os, ls, a = _attention_fwd(proj, Bl, attn_q_norm, attn_k_norm)
    *others, w_rel = _wait_copies("gather_rest_wait", send, recv, [*others, w_rel], _other_weight_copies, a[0])
    fsend, frecv, (*others, proj) = _start_copies("gather_rest_forward_start", [*others, proj], 3 * (n - 1),
                                                  _other_weight_forward)
    cc = _conv_branch_fwd(proj, Bl, conv_full)
    others = _wait_copies("gather_rest_forward_wait", fsend, frecv, others, _other_weight_forward, cc[0])
    W = {name: others[p] for p, (name, _) in enumerate(BIG[1:])}

    G, rest, rest_state = _weight_grads(
        x, mem, loss_target, norm_g, mem_norm_g, attn_q_norm, attn_k_norm, conv_full, mem_q_norm, mem_k_norm, W,
        (hb, hbt, mhb, proj, os, ls, a, cc), early=lambda G, carry: _exchange_start(G, BIG[1:], pos, carry, "rest"))

    for_sibling = _dw_in_half(hbt, rest[0], pos, False, "dw_in_sibling")
    send, recv, (for_sibling, got, dproj) = _start_copies(
        "sibling_w_in_start", [for_sibling, lax.empty(for_sibling.shape, for_sibling.dtype), rest[0]], 1, _sibling_copy)
    mine = _dw_in_half(hbt, dproj, pos, True, "dw_in_own")
    for_sibling, got = _wait_copies("sibling_w_in_wait", send, recv, [for_sibling, got], _sibling_copy, mine)
    pre_w_in = _presum(mine, got, "col", pos, "presum_w_in")

    w_in_state, dproj = _exchange_start(G, BIG[:1], pos, dproj, "w_in", pres=[pre_w_in])
    grad_x, small = _input_grad((dproj, *rest[1:]), w_rel, order)
    pres_rest, slots_rest = _exchange_wait(rest_state, grad_x)
    reds_rest = [_reduce_into_shard(slots_rest[p], pres_rest[p], kind, pos, "reduce_" + name)
                 for p, (name, kind) in enumerate(BIG[1:])]
    share_rest = _share_copies(BIG[1:])
    rsend, rrecv, reds_rest = _start_copies("share_rest_start", reds_rest, n - 1, share_rest)
    pres, slots = _exchange_wait(w_in_state, grad_x)
    red_w_in = _reduce_into_shard(slots[0], pres[0], "col", pos, "reduce_w_in")
    share_w_in = _share_copies(BIG[:1])
    wsend, wrecv, (red_w_in, small) = _start_copies("share_w_in_start", [red_w_in, small], 1, share_w_in)
    grad_x = grad_x.reshape(x.shape)

    tot = _sum_small(_gather_small(small))
    reds_rest = _wait_copies("share_rest_wait", rsend, rrecv, reds_rest, share_rest, tot)
    grads = dict(zip([name for name, _ in BIG[1:]], reds_rest))
    tot = tot[0]
    loss = tot[0]
    off = 128
    for name, size in (("norm_g", D), ("mem_norm_g", D), ("attn_q_norm", NGROUP * HEAD), ("attn_k_norm", NGROUP * HEAD),
                       ("conv_w", 3 * CONVW), ("mem_q_norm", MEM_HD), ("mem_k_norm", MEM_HD)):
        grads[name] = tot[off:off + size]
        off += size
    cw = conv_w.shape[1]
    grads["conv_w"] = lax.dynamic_slice(grads["conv_w"].reshape(3, CONVW), (0, chip * cw), (3, cw))
    for name in SMALL:
        grads[name] = grads[name].reshape(w[name].shape)

    delta, new_m, new_v = {}, {}, {}
    for name, _ in BIG[1:]:
        delta[name], new_m[name], new_v[name], grads[name] = _adamw(w[name], grads[name], m[name], v[name],
                                                                    "adamw_" + name, with_grad=True)

    def packed(t):
        return jnp.concatenate([t[name].reshape(1, -1) for name in SMALL], axis=1)

    ds, ms, vs = _adamw(packed(w), packed(grads), packed(m), packed(v), "adamw_small")
    shared, = _wait_copies("share_w_in_wait", wsend, wrecv, [red_w_in], share_w_in, ds)
    delta["w_in"], new_m["w_in"], new_v["w_in"], grads["w_in"] = _adamw(w["w_in"], shared, m["w_in"], v["w_in"],
                                                                        "adamw_w_in", with_grad=True)
    off = 0
    for name in SMALL:
        size = w[name].size
        delta[name] = ds[0, off:off + size].reshape(w[name].shape)
        new_m[name] = ms[0, off:off + size].reshape(w[name].shape)
        new_v[name] = vs[0, off:off + size].reshape(w[name].shape)
        off += size

    return (loss, grad_x, *[grads[n] for n in WEIGHTS], *[delta[n] for n in WEIGHTS],
            *[new_m[n] for n in WEIGHTS], *[new_v[n] for n in WEIGHTS])
```

```python
import functools

import jax
import jax.numpy as jnp
from jax import lax
from jax.experimental import pallas as pl
from jax.experimental.pallas import tpu as pltpu

F32 = jnp.float32
MXU_DTYPE = jnp.bfloat16
WIRE_DTYPE = jnp.bfloat16
PROJ_DTYPE = jnp.bfloat16
EPS = 1e-6
NEG = -1e30

HEAD = 128
HPG = 4
GW = HPG * HEAD
DILATIONS = (1, 4, 16)
NGROUP = len(DILATIONS)
BLK = 128
QKV = NGROUP * GW
CONVW = 1024
MEM_HEADS = 4
MEM_HD = 256
MEMW = MEM_HEADS * MEM_HD
Q0, K0, V0 = 0, QKV, 2 * QKV
ZA = 3 * QKV
CB, CC, CV, ZC = ZA + GW, ZA + GW + CONVW, ZA + GW + 2 * CONVW, ZA + GW + 3 * CONVW
MQ = ZC + CONVW
ZM = MQ + MEMW
G0 = ZM + MEMW

ADAM_LR, ADAM_B1, ADAM_B2, ADAM_EPS, ADAM_WD, ADAM_STEP = 0.001, 0.9, 0.999, 1e-08, 0.01, 10

VMEM_LIMIT = 56 * 1024 * 1024
MESH = pl.DeviceIdType.MESH
ANY = pl.BlockSpec(memory_space=pl.ANY)


def _tile(n, pref, mult=128):
    t = min(pref, n)
    while t > mult and (n % t or t % mult):
        t -= mult
    assert n % t == 0, (n, pref)
    return t


def _call(body, *, name, out_shape, grid=(), in_specs=None, out_specs=None, scratch_shapes=(),
          aliases=None, grid_spec=None):
    kw = {}
    if grid_spec is not None:
        kw["grid_spec"] = grid_spec
        ngrid = len(grid_spec.grid)
    else:
        kw.update(grid=grid, in_specs=in_specs, out_specs=out_specs, scratch_shapes=list(scratch_shapes))
        ngrid = len(grid)
    params = pltpu.CompilerParams(dimension_semantics=("arbitrary",) * ngrid, vmem_limit_bytes=VMEM_LIMIT)
    return pl.pallas_call(body, name=name, out_shape=out_shape, compiler_params=params,
                          input_output_aliases=aliases or {}, **kw)


_DIMS = {"nn": (((1,), (0,)), ((), ())), "nt": (((1,), (1,)), ((), ())), "tn": (((0,), (0,)), ((), ()))}


def _mxu(a, b, mode):
    return lax.dot_general(a.astype(MXU_DTYPE), b.astype(MXU_DTYPE), _DIMS[mode], preferred_element_type=F32)


@functools.partial(jax.custom_vjp, nondiff_argnums=(2,))
def _dot(a, b, mode):
    return _mxu(a, b, mode)


def _dot_fwd(a, b, mode):
    return _mxu(a, b, mode), (a, b)


def _dot_bwd(mode, res, g):
    a, b = res
    if mode == "nn":
        return _mxu(g, b, "nt"), _mxu(a, g, "tn")
    if mode == "nt":
        return _mxu(g, b, "nn"), _mxu(g, a, "tn")
    return _mxu(b, g, "nt"), _mxu(a, g, "nn")


_dot.defvjp(_dot_fwd, _dot_bwd)


def _sig(z):
    return 1.0 / (1.0 + jnp.exp(-z))


def _silu(z):
    return z * _sig(z)


def _rms_rows(t, g):
    return t * lax.rsqrt(jnp.mean(t * t, axis=-1, keepdims=True) + EPS) * g


def _attn_block(q, k2, v2, gq, gk, first):
    qn = _rms_rows(q, gq)
    kn = _rms_rows(k2, gk)
    s = jnp.where(_band_mask(first, k2.shape[0]), _mxu(qn, kn, "nt") * (HEAD ** -0.5), NEG)
    m = jnp.max(s, axis=-1, keepdims=True)
    p = jnp.exp(s - m)
    den = jnp.sum(p, axis=-1, keepdims=True)
    o = _mxu(p, v2, "nn") / den
    return o, m + jnp.log(den)


def _band_mask(first, nkeys):
    a = lax.broadcasted_iota(jnp.int32, (BLK, nkeys), 0)
    b = lax.broadcasted_iota(jnp.int32, (BLK, nkeys), 1)
    if nkeys == BLK:
        return b <= a
    return (b >= a) & (b <= a + BLK) & (b >= jnp.where(first, BLK, 0))


def _norm_parts(t):
    r = lax.rsqrt(jnp.mean(t * t, axis=-1, keepdims=True) + EPS)
    return r, t * r


def _norm_bwd(dn, g, r, th):
    dth = dn * g
    return r * (dth - th * jnp.mean(dth * th, axis=-1, keepdims=True)), jnp.sum(dn * th, axis=0, keepdims=True)


def _attn_block_bwd(q, k2, v2, gq, gk, first, do, o, lse, dlse):
    scale = HEAD ** -0.5
    rq, qh = _norm_parts(q)
    rk, kh = _norm_parts(k2)
    qn, kn = qh * gq, kh * gk
    s = jnp.where(_band_mask(first, k2.shape[0]), _mxu(qn, kn, "nt") * scale, NEG)
    p = jnp.exp(s - lse)
    ds = p * (_mxu(do, v2, "nt") + (dlse - jnp.sum(do * o, axis=-1, keepdims=True))) * scale
    dq, dgq = _norm_bwd(_mxu(ds, kn, "nn"), gq, rq, qh)
    dk2, dgk = _norm_bwd(_mxu(ds, qn, "tn"), gk, rk, kh)
    return dq, dk2, _mxu(p, do, "tn"), dgq, dgk


def _combine(o1, o2, o3, l1, l2, l3, z):
    m = lax.stop_gradient(jnp.maximum(jnp.maximum(l1, l2), l3))
    e1, e2, e3 = jnp.exp(l1 - m), jnp.exp(l2 - m), jnp.exp(l3 - m)
    return (e1 * o1 + e2 * o2 + e3 * o3) / (e1 + e2 + e3) * _silu(z)


def _mem_block(q, z, kv, gq, gk):
    outs = []
    for h in range(MEM_HEADS):
        sl = slice(h * MEM_HD, (h + 1) * MEM_HD)
        qn = _rms_rows(q[:, sl], gq)
        kn = _rms_rows(kv[:, sl], gk)
        s = _dot(qn, kn, "nt") * (MEM_HD ** -0.5)
        m = lax.stop_gradient(jnp.max(s, axis=-1, keepdims=True))
        p = jnp.exp(s - m)
        den = jnp.sum(p, axis=-1, keepdims=True)
        outs.append(_dot(p, kv[:, MEMW + h * MEM_HD:MEMW + (h + 1) * MEM_HD], "nn") / den)
    return jnp.concatenate(outs, axis=-1) * _silu(z)


def _cast(w, name):
    R, C = w.shape
    tr, tc = _tile(R, 512, 8), _tile(C, 2176)

    def body(w_ref, o_ref):
        o_ref[...] = w_ref[...].astype(o_ref.dtype)

    spec = pl.BlockSpec((tr, tc), lambda i, j: (i, j))
    return _call(body, name=name, grid=(R // tr, C // tc), in_specs=[spec], out_specs=spec,
                 out_shape=jax.ShapeDtypeStruct((R, C), WIRE_DTYPE))(w)


def _place_shard(w, kind, pos, dtype, name, slot=0, into=None, half=None):
    R, C = w.shape
    tr, tc = _tile(R, 512, 8), _tile(C if half is None else C // 2, 2176)
    nr, nc = R // tr, C // tc
    ncols = nc if half is None else nc // 2

    def body(pos_ref, w_ref, *rest):
        rest[-1][...] = w_ref[...].astype(rest[-1].dtype)

    def col(j, pos_ref):
        if half is None:
            return j
        return (pos_ref[1] if half == 0 else 1 - pos_ref[1]) * ncols + j

    if kind == "col":
        full = (R, 4 * C)
        out = pl.BlockSpec((tr, tc), lambda i, j, pos_ref: (i, pos_ref[slot] * nc + col(j, pos_ref)))
    else:
        full, out = (4 * R, C), pl.BlockSpec((tr, tc), lambda i, j, pos_ref: (pos_ref[slot] * nr + i, j))
    in_specs, args = [pl.BlockSpec((tr, tc), lambda i, j, pos_ref: (i, col(j, pos_ref)))], [pos, w]
    if into is not None:
        in_specs.append(ANY)
        args.append(into)
    spec = pltpu.PrefetchScalarGridSpec(num_scalar_prefetch=1, grid=(nr, ncols), in_specs=in_specs, out_specs=out)
    return _call(body, name=name, grid_spec=spec, out_shape=jax.ShapeDtypeStruct(full, dtype),
                 aliases={} if into is None else {2: 0})(*args)


def _matmul(a, b, mode, out_dtype, *, name, tm=512, tn=512, tk=512):
    if mode == "nn":
        (M, K), (_, N) = a.shape, b.shape
    elif mode == "nt":
        (M, K), (N, _) = a.shape, b.shape
    else:
        (K, M), (_, N) = a.shape, b.shape
    tm, tn, tk = _tile(M, tm), _tile(N, tn), _tile(K, tk)
    nk = K // tk

    def body(a_ref, b_ref, o_ref, *acc):
        part = lax.dot_general(a_ref[...], b_ref[...], _DIMS[mode], preferred_element_type=F32)
        if nk == 1:
            o_ref[...] = part.astype(o_ref.dtype)
            return
        acc_ref, = acc
        k = pl.program_id(2)

        @pl.when(k == 0)
        def _():
            acc_ref[...] = part

        @pl.when(k > 0)
        def _():
            acc_ref[...] += part

        @pl.when(k == nk - 1)
        def _():
            o_ref[...] = acc_ref[...].astype(o_ref.dtype)

    a_spec = pl.BlockSpec((tk, tm), lambda i, j, k: (k, i)) if mode == "tn" else pl.BlockSpec((tm, tk), lambda i, j, k: (i, k))
    b_spec = pl.BlockSpec((tn, tk), lambda i, j, k: (j, k)) if mode == "nt" else pl.BlockSpec((tk, tn), lambda i, j, k: (k, j))
    return _call(body, name=name, grid=(M // tm, N // tn, nk), in_specs=[a_spec, b_spec],
                 out_specs=pl.BlockSpec((tm, tn), lambda i, j, k: (i, j)),
                 out_shape=jax.ShapeDtypeStruct((M, N), out_dtype),
                 scratch_shapes=[] if nk == 1 else [pltpu.VMEM((tm, tn), F32)])(a, b)


def _rms_fwd(x, g, name):
    R, D = x.shape
    tr = _tile(R, 512)

    def body(x_ref, g_ref, o_ref, t_ref):
        y = _rms_rows(x_ref[...], g_ref[...])
        o_ref[...] = y.astype(o_ref.dtype)
        t_ref[...] = y.T.astype(t_ref.dtype)

    row = pl.BlockSpec((tr, D), lambda i: (i, 0))
    return _call(body, name=name, grid=(R // tr,), in_specs=[row, pl.BlockSpec((1, D), lambda i: (0, 0))],
                 out_specs=[row, pl.BlockSpec((D, tr), lambda i: (0, i))],
                 out_shape=[jax.ShapeDtypeStruct((R, D), MXU_DTYPE), jax.ShapeDtypeStruct((D, R), MXU_DTYPE)])(x, g)


def _rms_bwd(x, dh, g, dy, name):
    R, D = x.shape
    tr = _tile(R, 256)
    with_dx = dy is not None

    def body(*refs):
        if with_dx:
            x_ref, dh_ref, g_ref, dy_ref, dx_ref, dg_ref = refs
        else:
            x_ref, dh_ref, g_ref, dg_ref = refs
        xv, dhv = x_ref[...], dh_ref[...]
        r = lax.rsqrt(jnp.mean(xv * xv, axis=-1, keepdims=True) + EPS)
        xh = xv * r

        @pl.when(pl.program_id(0) == 0)
        def _():
            dg_ref[...] = jnp.zeros_like(dg_ref)

        dg_ref[...] += jnp.sum(dhv * xh, axis=0, keepdims=True)
        if with_dx:
            dxh = dhv * g_ref[...]
            dx_ref[...] = dy_ref[...] + r * (dxh - xh * jnp.mean(dxh * xh, axis=-1, keepdims=True))

    row = pl.BlockSpec((tr, D), lambda i: (i, 0))
    vec = pl.BlockSpec((1, D), lambda i: (0, 0))
    dg_shape = jax.ShapeDtypeStruct((1, D), F32)
    if with_dx:
        return _call(body, name=name, grid=(R // tr,), in_specs=[row, row, vec, row], out_specs=[row, vec],
                     out_shape=[jax.ShapeDtypeStruct((R, D), F32), dg_shape])(x, dh, g, dy)
    return None, _call(body, name=name, grid=(R // tr,), in_specs=[row, row, vec], out_specs=vec,
                       out_shape=dg_shape)(x, dh, g)


def _attn_geom(g, d):
    hc = HPG if d == 1 else 1
    cw = hc * HEAD
    cq, ck, cv = (Q0 + g * GW) // cw, (K0 + g * GW) // cw, (V0 + g * GW) // cw
    return (1, BLK * d, cw), hc, HPG // hc, cq, ck, cv


def _rows(ref, r, d, sl):
    if d == 1:
        return ref[0, :, sl]
    return ref.at[0][pl.ds(r, BLK, stride=d), sl]


def _set_rows(ref, r, d, sl, val):
    if d == 1:
        ref[0, :, sl] = val
    else:
        ref.at[0][pl.ds(r, BLK, stride=d), sl] = val


def _stage_rows(ref, r, d, sl, val):
    if d == 1:
        ref[:, sl] = val
    else:
        ref[pl.ds(r, BLK, stride=d), sl] = val


def _proj_stages(blk, d):
    return [] if d == 1 else [pltpu.VMEM(blk[1:], F32)] * 5


def _proj_rows(refs, stages, d):
    if d == 1:
        return [lambda r, sl, ref=ref: ref[0, :, sl].astype(F32) for ref in refs]
    for ref, stage in zip(refs, stages):
        stage[...] = ref[0].astype(F32)
    return [lambda r, sl, stage=stage: stage[pl.ds(r, BLK, stride=d), sl] for stage in stages]


def _attn_fwd(proj3, gq, gk, g, d):
    Bl, S, _ = proj3.shape
    blk, hc, ncb, cq, ck, cv = _attn_geom(g, d)
    nb = S // blk[1]
    if nb == 1:
        return _attn_single_fwd(proj3, gq, gk, g, d)

    def body(q_ref, kp_ref, kc_ref, vp_ref, vc_ref, gq_ref, gk_ref, o_ref, lse_ref, *stages):
        first = pl.program_id(2) == 0
        q, kp, kc, vp, vc = _proj_rows((q_ref, kp_ref, kc_ref, vp_ref, vc_ref), stages, d)
        def run(alone):
            for r in range(d):
                for h in range(hc):
                    sl = slice(h * HEAD, (h + 1) * HEAD)
                    if alone:
                        k2, v2 = kc(r, sl), vc(r, sl)
                    else:
                        k2 = jnp.concatenate([kp(r, sl), kc(r, sl)], axis=0)
                        v2 = jnp.concatenate([vp(r, sl), vc(r, sl)], axis=0)
                    o, lse = _attn_block(q(r, sl), k2, v2, gq_ref[...], gk_ref[...], False)
                    _set_rows(o_ref, r, d, sl, o)
                    _set_rows(lse_ref, r, d, sl, jnp.broadcast_to(lse, (BLK, HEAD)))

        pl.when(first)(lambda: run(True))
        pl.when(jnp.logical_not(first))(lambda: run(False))

    def cur(c0):
        return pl.BlockSpec(blk, lambda b, j, i: (b, i, c0 + j))

    def prev(c0):
        return pl.BlockSpec(blk, lambda b, j, i: (b, jnp.maximum(i - 1, 0), c0 + j))

    vec = pl.BlockSpec((1, HEAD), lambda b, j, i: (0, 0))
    out = pl.BlockSpec(blk, lambda b, j, i: (b, i, j))
    shp = jax.ShapeDtypeStruct((Bl, S, GW), F32)
    return _call(body, name=f"attn_fwd_g{g}", grid=(Bl, ncb, nb),
                 in_specs=[cur(cq), prev(ck), cur(ck), prev(cv), cur(cv), vec, vec],
                 out_specs=[out, out], out_shape=[shp, shp], scratch_shapes=_proj_stages(blk, d),
                 )(proj3, proj3, proj3, proj3, proj3, gq, gk)


def _attn_single_fwd(proj3, gq, gk, g, d):
    Bl, S, _ = proj3.shape
    blk, hc, ncb, cq, ck, cv = _attn_geom(g, d)

    def body(q_ref, k_ref, v_ref, gq_ref, gk_ref, o_ref, lse_ref, *stages):
        q, k, v = _proj_rows((q_ref, k_ref, v_ref), stages, d)
        for r in range(d):
            for h in range(hc):
                sl = slice(h * HEAD, (h + 1) * HEAD)
                o, lse = _attn_block(q(r, sl), k(r, sl), v(r, sl), gq_ref[...], gk_ref[...], True)
                _set_rows(o_ref, r, d, sl, o)
                _set_rows(lse_ref, r, d, sl, jnp.broadcast_to(lse, (BLK, HEAD)))

    def at(c0):
        return pl.BlockSpec(blk, lambda b, j: (b, 0, c0 + j))

    vec = pl.BlockSpec((1, HEAD), lambda b, j: (0, 0))
    shp = jax.ShapeDtypeStruct((Bl, S, GW), F32)
    return _call(body, name=f"attn_fwd_g{g}", grid=(Bl, ncb), in_specs=[at(cq), at(ck), at(cv), vec, vec],
                 out_specs=[at(0), at(0)], out_shape=[shp, shp], scratch_shapes=_proj_stages(blk, d)[:3],
                 )(proj3, proj3, proj3, gq, gk)


def _attn_single_bwd(proj3, gq, gk, o3, l3, do3, dl3, g, d):
    Bl, S, _ = proj3.shape
    blk, hc, ncb, cq, ck, cv = _attn_geom(g, d)

    def body(q_ref, k_ref, v_ref, gq_ref, gk_ref, o_ref, l_ref, do_ref, dl_ref,
             dq_ref, dk_ref, dv_ref, dgq_ref, dgk_ref, sq_ref, sk_ref, sv_ref, *stages):
        @pl.when((pl.program_id(0) == 0) & (pl.program_id(1) == 0))
        def _():
            dgq_ref[...] = jnp.zeros_like(dgq_ref)
            dgk_ref[...] = jnp.zeros_like(dgk_ref)

        dgq, dgk = jnp.zeros((1, HEAD), F32), jnp.zeros((1, HEAD), F32)
        q, k, v = _proj_rows((q_ref, k_ref, v_ref), stages, d)
        for r in range(d):
            for h in range(hc):
                sl = slice(h * HEAD, (h + 1) * HEAD)
                dq, dk, dv, a, b = _attn_block_bwd(
                    q(r, sl), k(r, sl), v(r, sl), gq_ref[...], gk_ref[...], True, _rows(do_ref, r, d, sl),
                    _rows(o_ref, r, d, sl), _rows(l_ref, r, d, sl)[:, :1], _rows(dl_ref, r, d, sl)[:, :1])
                _stage_rows(sq_ref, r, d, sl, dq)
                _stage_rows(sk_ref, r, d, sl, dk)
                _stage_rows(sv_ref, r, d, sl, dv)
                dgq, dgk = dgq + a, dgk + b
        dgq_ref[...] += dgq
        dgk_ref[...] += dgk
        dq_ref[0] = sq_ref[...].astype(dq_ref.dtype)
        dk_ref[0] = sk_ref[...].astype(dk_ref.dtype)
        dv_ref[0] = sv_ref[...].astype(dv_ref.dtype)

    def at(c0):
        return pl.BlockSpec(blk, lambda b, j: (b, 0, c0 + j))

    vec = pl.BlockSpec((1, HEAD), lambda b, j: (0, 0))
    shp = jax.ShapeDtypeStruct((Bl, S, GW), MXU_DTYPE)
    gshp = jax.ShapeDtypeStruct((1, HEAD), F32)
    return _call(body, name=f"attn_bwd_g{g}", grid=(Bl, ncb),
                 in_specs=[at(cq), at(ck), at(cv), vec, vec, at(0), at(0), at(0), at(0)],
                 out_specs=[at(0), at(0), at(0), vec, vec], out_shape=[shp, shp, shp, gshp, gshp],
                 scratch_shapes=[pltpu.VMEM(blk[1:], F32)] * 3 + _proj_stages(blk, d)[:3],
                 )(proj3, proj3, proj3, gq, gk, o3, l3, do3, dl3)


def _attn_bwd(proj3, gq, gk, o3, l3, do3, dl3, g, d):
    Bl, S, _ = proj3.shape
    blk, hc, ncb, cq, ck, cv = _attn_geom(g, d)
    nb = S // blk[1]
    if nb == 1:
        return _attn_single_bwd(proj3, gq, gk, o3, l3, do3, dl3, g, d)

    def body(q_ref, kp_ref, kc_ref, vp_ref, vc_ref, gq_ref, gk_ref, o_ref, l_ref, do_ref, dl_ref,
             dq_ref, dk_ref, dv_ref, dgq_ref, dgk_ref, ck_ref, cv_ref, sq_ref, sk_ref, sv_ref, *stages):
        i = pl.program_id(2)
        first = i == 0

        @pl.when((pl.program_id(0) == 0) & (pl.program_id(1) == 0) & first)
        def _():
            dgq_ref[...] = jnp.zeros_like(dgq_ref)
            dgk_ref[...] = jnp.zeros_like(dgk_ref)

        def run(alone):
            dgq, dgk = jnp.zeros((1, HEAD), F32), jnp.zeros((1, HEAD), F32)
            q, kp, kc, vp, vc = _proj_rows((q_ref, kp_ref, kc_ref, vp_ref, vc_ref), stages, d)
            for r in range(d):
                rs = slice(r * BLK, (r + 1) * BLK)
                for h in range(hc):
                    sl = slice(h * HEAD, (h + 1) * HEAD)
                    if alone:
                        k2, v2 = kc(r, sl), vc(r, sl)
                    else:
                        k2 = jnp.concatenate([kp(r, sl), kc(r, sl)], axis=0)
                        v2 = jnp.concatenate([vp(r, sl), vc(r, sl)], axis=0)
                    dq, dk2, dv2, a, b = _attn_block_bwd(
                        q(r, sl), k2, v2, gq_ref[...], gk_ref[...], False, _rows(do_ref, r, d, sl),
                        _rows(o_ref, r, d, sl), _rows(l_ref, r, d, sl)[:, :1], _rows(dl_ref, r, d, sl)[:, :1])
                    _stage_rows(sq_ref, r, d, sl, dq)
                    if alone:
                        _stage_rows(sk_ref, r, d, sl, jnp.zeros((BLK, HEAD), F32))
                        _stage_rows(sv_ref, r, d, sl, jnp.zeros((BLK, HEAD), F32))
                    else:
                        _stage_rows(sk_ref, r, d, sl, ck_ref[rs, sl] + dk2[:BLK])
                        _stage_rows(sv_ref, r, d, sl, cv_ref[rs, sl] + dv2[:BLK])
                    ck_ref[rs, sl] = dk2[-BLK:]
                    cv_ref[rs, sl] = dv2[-BLK:]
                    dgq, dgk = dgq + a, dgk + b
            dgq_ref[...] += dgq
            dgk_ref[...] += dgk
            dq_ref[0] = sq_ref[...].astype(dq_ref.dtype)

        pl.when(first)(lambda: run(True))
        pl.when((i > 0) & (i < nb))(lambda: run(False))

        @pl.when(i == nb)
        def _():
            for r in range(d):
                rs = slice(r * BLK, (r + 1) * BLK)
                _stage_rows(sk_ref, r, d, slice(None), ck_ref[rs, :])
                _stage_rows(sv_ref, r, d, slice(None), cv_ref[rs, :])

        dk_ref[0] = sk_ref[...].astype(dk_ref.dtype)
        dv_ref[0] = sv_ref[...].astype(dv_ref.dtype)

    def cur(c0):
        return pl.BlockSpec(blk, lambda b, j, i: (b, jnp.minimum(i, nb - 1), c0 + j))

    def prev(c0):
        return pl.BlockSpec(blk, lambda b, j, i: (b, jnp.clip(i - 1, 0, nb - 1), c0 + j))

    vec = pl.BlockSpec((1, HEAD), lambda b, j, i: (0, 0))
    at_q = pl.BlockSpec(blk, lambda b, j, i: (b, jnp.minimum(i, nb - 1), j))
    at_k = pl.BlockSpec(blk, lambda b, j, i: (b, jnp.maximum(i - 1, 0), j))
    shp = jax.ShapeDtypeStruct((Bl, S, GW), MXU_DTYPE)
    gshp = jax.ShapeDtypeStruct((1, HEAD), F32)
    return _call(body, name=f"attn_bwd_g{g}", grid=(Bl, ncb, nb + 1),
                 in_specs=[cur(cq), prev(ck), cur(ck), prev(cv), cur(cv), vec, vec, at_q, at_q, at_q, at_q],
                 out_specs=[at_q, at_k, at_k, vec, vec], out_shape=[shp, shp, shp, gshp, gshp],
                 scratch_shapes=[pltpu.VMEM(blk[1:], F32)] * 5 + _proj_stages(blk, d),
                 )(proj3, proj3, proj3, proj3, proj3, gq, gk, o3, l3, do3, dl3)


def _combine_fwd(os, ls, proj2):
    T = proj2.shape[0]
    tr = _tile(T, 512)

    def body(o1, o2, o3, l1, l2, l3, z, a_ref, at_ref):
        a = _combine(o1[...], o2[...], o3[...], l1[...], l2[...], l3[...], z[...].astype(F32))
        a_ref[...] = a.astype(a_ref.dtype)
        at_ref[...] = a.T.astype(at_ref.dtype)

    row = pl.BlockSpec((tr, GW), lambda i: (i, 0))
    return _call(body, name="combine_fwd", grid=(T // tr,),
                 in_specs=[row] * 6 + [pl.BlockSpec((tr, GW), lambda i: (i, ZA // GW))],
                 out_specs=[row, pl.BlockSpec((GW, tr), lambda i: (0, i))],
                 out_shape=[jax.ShapeDtypeStruct((T, GW), MXU_DTYPE), jax.ShapeDtypeStruct((GW, T), MXU_DTYPE)],
                 )(*os, *ls, proj2)


def _combine_bwd(os, ls, proj2, da, dproj):
    T = proj2.shape[0]
    tr = _tile(T, 256)

    def body(o1, o2, o3, l1, l2, l3, z, da_ref, _, d1, d2, d3, e1, e2, e3, dz_ref):
        _, vjp = jax.vjp(_combine, o1[...], o2[...], o3[...], l1[...], l2[...], l3[...], z[...].astype(F32))
        go1, go2, go3, gl1, gl2, gl3, gz = vjp(da_ref[...])
        d1[...], d2[...], d3[...] = go1, go2, go3
        dz_ref[...] = gz.astype(dz_ref.dtype)
        for ref, gl in ((e1, gl1), (e2, gl2), (e3, gl3)):
            for h in range(HPG):
                sl = slice(h * HEAD, (h + 1) * HEAD)
                ref[:, sl] = jnp.broadcast_to(jnp.sum(gl[:, sl], axis=-1, keepdims=True), (tr, HEAD))

    row = pl.BlockSpec((tr, GW), lambda i: (i, 0))
    f = jax.ShapeDtypeStruct((T, GW), F32)
    z_attn = pl.BlockSpec((tr, GW), lambda i: (i, ZA // GW))
    outs = _call(body, name="combine_bwd", grid=(T // tr,), in_specs=[row] * 6 + [z_attn, row, ANY],
                 out_specs=[row] * 6 + [z_attn], out_shape=[f] * 6 + [jax.ShapeDtypeStruct(dproj.shape, dproj.dtype)],
                 aliases={8: 6})(*os, *ls, proj2, da, dproj)
    return outs[:3], outs[3:6], outs[6]


def _shift_down(u, j, t):
    return jnp.where(t >= j, pltpu.roll(u, j, 0), 0.0)


def _shift_up(u, j, t):
    n = u.shape[0]
    return jnp.where(t < n - j, pltpu.roll(u, n - j, 0), 0.0)


def _conv_specs(Bl, S, cw):
    def sec(c0):
        return pl.BlockSpec((1, S, cw), lambda j, b: (b, 0, c0 // cw + j))
    return [sec(CB), sec(CC), sec(CV), sec(ZC)], pl.BlockSpec((3, cw), lambda j, b: (0, j))


def _conv_fwd(proj3, conv_w):
    Bl, S, _ = proj3.shape
    cw = 256
    secs, wspec = _conv_specs(Bl, S, cw)

    def body(b_ref, c_ref, v_ref, z_ref, w_ref, o_ref, ot_ref):
        t = lax.broadcasted_iota(jnp.int32, (S, cw), 0)
        u = c_ref[0].astype(F32) * v_ref[0].astype(F32)
        y = w_ref[0:1, :] * u + w_ref[1:2, :] * _shift_down(u, 1, t) + w_ref[2:3, :] * _shift_down(u, 2, t)
        out = b_ref[0].astype(F32) * y * _silu(z_ref[0].astype(F32))
        o_ref[0] = out.astype(o_ref.dtype)
        ot_ref[...] = out.T.astype(ot_ref.dtype)

    return _call(body, name="conv_fwd", grid=(CONVW // cw, Bl), in_specs=secs + [wspec],
                 out_specs=[pl.BlockSpec((1, S, cw), lambda j, b: (b, 0, j)), pl.BlockSpec((cw, S), lambda j, b: (j, b))],
                 out_shape=[jax.ShapeDtypeStruct((Bl, S, CONVW), MXU_DTYPE),
                            jax.ShapeDtypeStruct((CONVW, Bl * S), MXU_DTYPE)])(proj3, proj3, proj3, proj3, conv_w)


def _conv_bwd(proj3, conv_w, dcc3, dproj):
    Bl, S, _ = proj3.shape
    cw = 256
    secs, wspec = _conv_specs(Bl, S, cw)

    def body(b_ref, c_ref, v_ref, z_ref, w_ref, d_ref, _, dproj_ref, dw_ref, stage, sems):
        t = lax.broadcasted_iota(jnp.int32, (S, cw), 0)
        bv, cv, vv, zv = (r[0].astype(F32) for r in (b_ref, c_ref, v_ref, z_ref))
        dv = d_ref[0]
        u = cv * vv
        u1, u2 = _shift_down(u, 1, t), _shift_down(u, 2, t)
        y = w_ref[0:1, :] * u + w_ref[1:2, :] * u1 + w_ref[2:3, :] * u2
        sg = _sig(zv)
        sz = zv * sg
        gy = dv * bv * sz
        du = w_ref[0:1, :] * gy + w_ref[1:2, :] * _shift_up(gy, 1, t) + w_ref[2:3, :] * _shift_up(gy, 2, t)
        j, b = pl.program_id(0), pl.program_id(1)
        tiles = [dv * y * sz, du * vv, du * cv, dv * bv * y * sg * (1.0 + zv * (1.0 - sg))]
        dsts = [dproj_ref.at[pl.ds(b * S, S), pl.ds(c0 + j * cw, cw)] for c0 in (CB, CC, CV, ZC)]
        _emit_tiles(j * Bl + b, (CONVW // cw) * Bl, tiles, dsts, stage, sems)

        @pl.when(pl.program_id(1) == 0)
        def _():
            dw_ref[...] = jnp.zeros_like(dw_ref)

        dw_ref[0:1, :] += jnp.sum(gy * u, axis=0, keepdims=True)
        dw_ref[1:2, :] += jnp.sum(gy * u1, axis=0, keepdims=True)
        dw_ref[2:3, :] += jnp.sum(gy * u2, axis=0, keepdims=True)

    blk = pl.BlockSpec((1, S, cw), lambda j, b: (b, 0, j))
    return _call(body, name="conv_bwd", grid=(CONVW // cw, Bl), in_specs=secs + [wspec, blk, ANY],
                 out_specs=[ANY, wspec],
                 out_shape=[jax.ShapeDtypeStruct(dproj.shape, dproj.dtype), jax.ShapeDtypeStruct((3, CONVW), F32)],
                 scratch_shapes=_emit_scratch(4, S, cw), aliases={6: 0})(proj3, proj3, proj3, proj3, conv_w, dcc3, dproj)


def _mem_specs(S, tq):
    q = pl.BlockSpec((1, tq, MEMW), lambda b, j: (b, j, MQ // MEMW))
    z = pl.BlockSpec((1, tq, MEMW), lambda b, j: (b, j, ZM // MEMW))
    kv = pl.BlockSpec((1, MEM_HD, 2 * MEMW), lambda b, j: (b, 0, 0))
    vec = pl.BlockSpec((1, MEM_HD), lambda b, j: (0, 0))
    blk = pl.BlockSpec((1, tq, MEMW), lambda b, j: (b, j, 0))
    return q, z, kv, vec, blk


def _mem_fwd(proj3, mkv3, gq, gk):
    Bl, S, _ = proj3.shape
    tq = _tile(S, 512)
    q, z, kv, vec, blk = _mem_specs(S, tq)

    def body(q_ref, z_ref, kv_ref, gq_ref, gk_ref, o_ref, ot_ref):
        out = _mem_block(q_ref[0].astype(F32), z_ref[0].astype(F32), kv_ref[0], gq_ref[...], gk_ref[...])
        o_ref[0] = out.astype(o_ref.dtype)
        ot_ref[...] = out.T.astype(ot_ref.dtype)

    nq = S // tq
    return _call(body, name="mem_fwd", grid=(Bl, nq), in_specs=[q, z, kv, vec, vec],
                 out_specs=[blk, pl.BlockSpec((MEMW, tq), lambda b, j: (0, b * nq + j))],
                 out_shape=[jax.ShapeDtypeStruct((Bl, S, MEMW), MXU_DTYPE),
                            jax.ShapeDtypeStruct((MEMW, Bl * S), MXU_DTYPE)])(proj3, proj3, mkv3, gq, gk)


def _mem_bwd(proj3, mkv3, gq, gk, dmo3, dproj):
    Bl, S, _ = proj3.shape
    tq = _tile(S, 256)
    q, z, kv, vec, blk = _mem_specs(S, tq)
    nq = S // tq

    def body(q_ref, z_ref, kv_ref, gq_ref, gk_ref, d_ref, _, dproj_ref, dkv_ref, dgq_ref, dgk_ref, stage, sems):
        _, vjp = jax.vjp(_mem_block, q_ref[0].astype(F32), z_ref[0].astype(F32), kv_ref[0], gq_ref[...], gk_ref[...])
        dq, dz, dkv, dgq, dgk = vjp(d_ref[0])
        j = pl.program_id(1)
        rows = pl.ds(pl.program_id(0) * S + j * tq, tq)
        dsts = [dproj_ref.at[rows, pl.ds(MQ, MEMW)], dproj_ref.at[rows, pl.ds(ZM, MEMW)]]
        _emit_tiles(pl.program_id(0) * nq + j, Bl * nq, [dq, dz], dsts, stage, sems)

        @pl.when(j == 0)
        def _():
            dkv_ref[0] = jnp.zeros_like(dkv)

        @pl.when((j == 0) & (pl.program_id(0) == 0))
        def _():
            dgq_ref[...] = jnp.zeros_like(dgq_ref)
            dgk_ref[...] = jnp.zeros_like(dgk_ref)

        dkv_ref[0] += dkv
        dgq_ref[...] += dgq
        dgk_ref[...] += dgk

    gshp = jax.ShapeDtypeStruct((1, MEM_HD), F32)
    return _call(body, name="mem_bwd", grid=(Bl, nq), in_specs=[q, z, kv, vec, vec, blk, ANY],
                 out_specs=[ANY, kv, vec, vec],
                 out_shape=[jax.ShapeDtypeStruct(dproj.shape, dproj.dtype), jax.ShapeDtypeStruct(mkv3.shape, F32),
                            gshp, gshp],
                 scratch_shapes=_emit_scratch(2, tq, MEMW), aliases={6: 0})(proj3, proj3, mkv3, gq, gk, dmo3, dproj)


def _merge_specs(T, D, tm, tn):
    def act(w):
        return pl.BlockSpec((tm, w), lambda i, n: (i, 0))

    def wsp(w):
        return pl.BlockSpec((w, tn), lambda i, n: (0, n))

    gates = [pl.BlockSpec((tm, tn), lambda i, n, k=k: (i, (G0 + k * D) // tn + n)) for k in range(3)]
    tile = pl.BlockSpec((tm, tn), lambda i, n: (i, n))
    return act, wsp, gates, tile


def _merge_fwd(a, cc, mo, wa, wc, wm, proj2):
    T, D = a.shape[0], wa.shape[1]
    tm, tn = _tile(T, 1024), _tile(D, 512)
    act, wsp, gates, tile = _merge_specs(T, D, tm, tn)

    def body(a_ref, c_ref, m_ref, wa_ref, wc_ref, wm_ref, g0, g1, g2, mg_ref, mt_ref, pa_ref, pc_ref, pm_ref):
        pa = jnp.dot(a_ref[...], wa_ref[...], preferred_element_type=F32)
        pc = jnp.dot(c_ref[...], wc_ref[...], preferred_element_type=F32)
        pm = jnp.dot(m_ref[...], wm_ref[...], preferred_element_type=F32)
        mg = _sig(g0[...].astype(F32)) * pa + _sig(g1[...].astype(F32)) * pc + _sig(g2[...].astype(F32)) * pm
        mg_ref[...] = mg.astype(mg_ref.dtype)
        mt_ref[...] = mg.T.astype(mt_ref.dtype)
        pa_ref[...] = pa.astype(pa_ref.dtype)
        pc_ref[...] = pc.astype(pc_ref.dtype)
        pm_ref[...] = pm.astype(pm_ref.dtype)

    shp = jax.ShapeDtypeStruct((T, D), MXU_DTYPE)
    return _call(body, name="merge_fwd", grid=(T // tm, D // tn),
                 in_specs=[act(GW), act(CONVW), act(MEMW), wsp(GW), wsp(CONVW), wsp(MEMW)] + gates,
                 out_specs=[tile, pl.BlockSpec((tn, tm), lambda i, n: (n, i)), tile, tile, tile],
                 out_shape=[shp, jax.ShapeDtypeStruct((D, T), MXU_DTYPE), shp, shp, shp],
                 )(a, cc, mo, wa, wc, wm, proj2, proj2, proj2)


def _emit_tiles(step, nsteps, tiles, dsts, stage, sems):
    slot = step % 2

    def copies(s):
        return [pltpu.make_async_copy(stage.at[s, k], dsts[k], sems.at[s, k]) for k in range(len(tiles))]

    @pl.when(step >= 2)
    def _():
        for cp in copies(slot):
            cp.wait()

    for k, t in enumerate(tiles):
        stage[slot, k] = t.astype(stage.dtype)
    for cp in copies(slot):
        cp.start()

    @pl.when(step == nsteps - 1)
    def _():
        for cp in copies(slot):
            cp.wait()
        if nsteps > 1:
            for cp in copies(1 - slot):
                cp.wait()


def _emit_scratch(k, rows, cols):
    return [pltpu.VMEM((2, k, rows, cols), MXU_DTYPE), pltpu.SemaphoreType.DMA((2, k))]


def _merge_bwd(dyb, w_out, proj2, pa, pc, pm):
    T, D = dyb.shape
    IN = proj2.shape[1]
    tm, tn = _tile(T, 1024), _tile(D, 512)
    _, _, gates, tile = _merge_specs(T, D, tm, tn)
    nn = D // tn

    def body(dy_ref, w_ref, g0, g1, g2, p0, p1, p2, dp0, dp1, dp2, dproj_ref, stage, sems):
        i, n = pl.program_id(0), pl.program_id(1)
        dm = lax.dot_general(dy_ref[...], w_ref[...], _DIMS["nt"], preferred_element_type=F32)
        tiles, dsts = [], []
        for k, (g_ref, p_ref, dp_ref) in enumerate(((g0, p0, dp0), (g1, p1, dp1), (g2, p2, dp2))):
            gt = _sig(g_ref[...].astype(F32))
            dp_ref[...] = (gt * dm).astype(dp_ref.dtype)
            tiles.append(dm * p_ref[...].astype(F32) * gt * (1.0 - gt))
            dsts.append(dproj_ref.at[pl.ds(i * tm, tm), pl.ds(G0 + k * D + n * tn, tn)])
        _emit_tiles(i * nn + n, (T // tm) * nn, tiles, dsts, stage, sems)

    shp = jax.ShapeDtypeStruct((T, D), MXU_DTYPE)
    return _call(body, name="merge_bwd", grid=(T // tm, nn),
                 in_specs=[pl.BlockSpec((tm, D), lambda i, n: (i, 0)), pl.BlockSpec((tn, D), lambda i, n: (n, 0))]
                 + gates + [tile] * 3,
                 out_specs=[tile] * 3 + [ANY], out_shape=[shp] * 3 + [jax.ShapeDtypeStruct((T, IN), MXU_DTYPE)],
                 scratch_shapes=_emit_scratch(3, tm, tn))(dyb, w_out, proj2, proj2, proj2, pa, pc, pm)


def _out_loss(merged, w_out, x, tgt):
    T, D = x.shape
    tm = _tile(T, 512)

    def body(m_ref, w_ref, x_ref, t_ref, dy_ref, dyb_ref, loss_ref):
        err = x_ref[...] + jnp.dot(m_ref[...], w_ref[...], preferred_element_type=F32) - t_ref[...]
        dy = err * (1.0 / D)
        dy_ref[...] = dy
        dyb_ref[...] = dy.astype(dyb_ref.dtype)

        @pl.when(pl.program_id(0) == 0)
        def _():
            loss_ref[...] = jnp.zeros_like(loss_ref)

        loss_ref[...] += jnp.sum(err * err) * (0.5 / D)

    row = pl.BlockSpec((tm, D), lambda i: (i, 0))
    return _call(body, name="out_loss", grid=(T // tm,),
                 in_specs=[row, pl.BlockSpec((D, D), lambda i: (0, 0)), row, row],
                 out_specs=[row, row, pl.BlockSpec((1, 128), lambda i: (0, 0))],
                 out_shape=[jax.ShapeDtypeStruct((T, D), F32), jax.ShapeDtypeStruct((T, D), MXU_DTYPE),
                            jax.ShapeDtypeStruct((1, 128), F32)])(merged, w_out, x, tgt)


def _proj_chunk(hb, w, meta, j, nslots, half, buf, name):
    T, D = hb.shape
    Cs = w.shape[1] // 4
    tm, tn = _tile(T, 1024), _tile(Cs // 2, 2176)
    nh = Cs // 2 // tn
    per = nh if half is not None else 2 * nh

    def body(meta_ref, a_ref, b_ref, *rest):
        rest[-1][...] = jnp.dot(a_ref[...], b_ref[...], preferred_element_type=F32).astype(rest[-1].dtype)

    def tile(n, m):
        if half is None:
            return n % per
        return (m[4] if half == 0 else 1 - m[4]) * nh + n % per

    in_specs = [pl.BlockSpec((tm, D), lambda n, i, m: (i, 0)),
                pl.BlockSpec((D, tn), lambda n, i, m: (0, (j + n // per) * 2 * nh + tile(n, m)))]
    args = [meta, hb, w]
    if buf is not None:
        in_specs.append(ANY)
        args.append(buf)
    spec = pltpu.PrefetchScalarGridSpec(
        num_scalar_prefetch=1, grid=(nslots * per, T // tm), in_specs=in_specs,
        out_specs=pl.BlockSpec((tm, tn), lambda n, i, m: (i, m[j + n // per] * 2 * nh + tile(n, m))))
    return _call(body, name=name, grid_spec=spec, out_shape=jax.ShapeDtypeStruct((T, 4 * Cs), PROJ_DTYPE),
                 aliases={} if buf is None else {3: 0})(*args)


def _norms(x, mem, norm_g, mem_norm_g):
    D = x.shape[-1]
    hb, hbt = _rms_fwd(x.reshape(-1, D), norm_g.reshape(1, D), "rms_x")
    mhb, _ = _rms_fwd(mem.reshape(-1, D), mem_norm_g.reshape(1, D), "rms_mem")
    return hb, hbt, mhb


def _attention_fwd(proj2, Bl, gq_all, gk_all):
    T, IN = proj2.shape
    proj3 = proj2.reshape(Bl, T // Bl, IN)
    os, ls = [], []
    for g, d in enumerate(DILATIONS):
        o, l = _attn_fwd(proj3, gq_all[g:g + 1], gk_all[g:g + 1], g, d)
        os.append(o.reshape(T, GW))
        ls.append(l.reshape(T, GW))
    return os, ls, _combine_fwd(os, ls, proj2)


def _conv_branch_fwd(proj2, Bl, conv_w):
    T, IN = proj2.shape
    cc, cct = _conv_fwd(proj2.reshape(Bl, T // Bl, IN), conv_w)
    return cc.reshape(T, CONVW), cct


def _weight_grads(x, mem, tgt, norm_g, mem_norm_g, gq_all, gk_all, conv_w, mem_gq, mem_gk, W, pre, early=None):
    Bl, S, D = x.shape
    T = Bl * S
    hb, hbt, mhb, proj2, os, ls, (a, at), (cc, cct) = pre
    IN = proj2.shape[1]
    proj3 = proj2.reshape(Bl, S, IN)
    x2, tgt2 = x.reshape(T, D), tgt.reshape(T, D)
    mem2 = mem.reshape(-1, D)
    ng, mng = norm_g.reshape(1, D), mem_norm_g.reshape(1, D)
    mgq, mgk = mem_gq.reshape(1, MEM_HD), mem_gk.reshape(1, MEM_HD)
    gqs = [gq_all[g:g + 1] for g in range(NGROUP)]
    gks = [gk_all[g:g + 1] for g in range(NGROUP)]

    mkv = _matmul(mhb, W["mem_w_kv"], "nn", F32, name="mem_kv", tm=512, tn=1024, tk=D)
    mkv3 = mkv.reshape(Bl, -1, 2 * MEMW)
    mo, mot = _mem_fwd(proj3, mkv3, mgq, mgk)
    mo = mo.reshape(T, MEMW)
    merged, mergedt, pa, pc, pm = _merge_fwd(a, cc, mo, W["w_br_attn"], W["w_br_conv"], W["w_br_mem"], proj2)
    dy, dyb, loss = _out_loss(merged, W["w_out"], x2, tgt2)

    G = {}
    G["w_out"] = _matmul(mergedt, dyb, "nn", WIRE_DTYPE, name="dw_out", tm=1024, tn=512, tk=T)
    dpa, dpc, dpm, dproj = _merge_bwd(dyb, W["w_out"], proj2, pa, pc, pm)
    G["w_br_attn"] = _matmul(at, dpa, "nn", WIRE_DTYPE, name="dw_br_attn", tm=512, tn=512, tk=T)
    G["w_br_conv"] = _matmul(cct, dpc, "nn", WIRE_DTYPE, name="dw_br_conv", tm=1024, tn=512, tk=T)
    G["w_br_mem"] = _matmul(mot, dpm, "nn", WIRE_DTYPE, name="dw_br_mem", tm=1024, tn=512, tk=T)
    da = _matmul(dpa, W["w_br_attn"], "nt", F32, name="d_attn", tm=1024, tn=512, tk=D)
    dcc = _matmul(dpc, W["w_br_conv"], "nt", F32, name="d_conv", tm=1024, tn=1024, tk=D)
    dmo = _matmul(dpm, W["w_br_mem"], "nt", F32, name="d_mem", tm=1024, tn=1024, tk=D)
    dproj, dmkv3, dmgq, dmgk = _mem_bwd(proj3, mkv3, mgq, mgk, dmo.reshape(Bl, S, MEMW), dproj)
    dmkv = _cast(dmkv3.reshape(-1, 2 * MEMW), "cast_dmkv")
    G["mem_w_kv"] = _matmul(mhb, dmkv, "tn", WIRE_DTYPE, name="dw_mem_kv", tm=1024, tn=1024, tk=512)
    early_state, dmkv = (None, dmkv) if early is None else early[0](G, dmkv)
    dmh = _matmul(dmkv, W["mem_w_kv"], "nt", F32, name="d_memh", tm=512, tn=1024, tk=2 * MEMW)
    _, dmng = _rms_bwd(mem2, dmh, mng, None, "rms_mem_bwd")
    if early is not None:
        early_state, da = early[1](early_state, dmng, da)

    dos, dls, dproj = _combine_bwd(os, ls, proj2, da, dproj)
    dgq, dgk = [], []
    for g, d in enumerate(DILATIONS):
        dq, dk, dv, gq_g, gk_g = _attn_bwd(proj3, gqs[g], gks[g], os[g].reshape(Bl, S, GW), ls[g].reshape(Bl, S, GW),
                                           dos[g].reshape(Bl, S, GW), dls[g].reshape(Bl, S, GW), g, d)
        for c0, part in ((Q0, dq), (K0, dk), (V0, dv)):
            dproj = lax.dynamic_update_slice(dproj, part.reshape(T, GW), (0, c0 + g * GW))
        dgq.append(gq_g)
        dgk.append(gk_g)
    dproj, dconv_w = _conv_bwd(proj3, conv_w, dcc.reshape(Bl, S, CONVW), dproj)
    small = [loss, None, dmng] + dgq + dgk + [dconv_w.reshape(1, 3 * CONVW), dmgq, dmgk]
    return G, (dproj, x2, ng, dy, small), early_state


def _dw_in_half(hbt, dproj, pos, own, name):
    D, T = hbt.shape
    IN = dproj.shape[1]
    R, tn = D // 2, _tile(IN, 1024)

    def body(pos_ref, a_ref, b_ref, o_ref):
        o_ref[...] = jnp.dot(a_ref[...], b_ref[...], preferred_element_type=F32).astype(o_ref.dtype)

    spec = pltpu.PrefetchScalarGridSpec(
        num_scalar_prefetch=1, grid=(IN // tn,),
        in_specs=[pl.BlockSpec((R, T), lambda j, p: (p[1] if own else 1 - p[1], 0)),
                  pl.BlockSpec((T, tn), lambda j, p: (0, j))],
        out_specs=pl.BlockSpec((R, tn), lambda j, p: (0, j)))
    return _call(body, name=name, grid_spec=spec, out_shape=jax.ShapeDtypeStruct((R, IN), WIRE_DTYPE))(pos, hbt, dproj)


def _d_h(dproj, w, order):
    T, IN = dproj.shape
    D, Cs = w.shape[0], IN // 4
    tm, tn = _tile(T, 1024), _tile(D, 1024)

    def body(order_ref, a_ref, b_ref, o_ref, acc_ref):
        part = lax.dot_general(a_ref[...], b_ref[...], _DIMS["nt"], preferred_element_type=F32)
        k = pl.program_id(2)

        @pl.when(k == 0)
        def _():
            acc_ref[...] = part

        @pl.when(k > 0)
        def _():
            acc_ref[...] += part

        @pl.when(k == 3)
        def _():
            o_ref[...] = acc_ref[...]

    spec = pltpu.PrefetchScalarGridSpec(
        num_scalar_prefetch=1, grid=(T // tm, D // tn, 4),
        in_specs=[pl.BlockSpec((tm, Cs), lambda i, n, k, o: (i, o[k])), pl.BlockSpec((tn, Cs), lambda i, n, k, o: (n, k))],
        out_specs=pl.BlockSpec((tm, tn), lambda i, n, k, o: (i, n)), scratch_shapes=[pltpu.VMEM((tm, tn), F32)])
    return _call(body, name="d_h", grid_spec=spec, out_shape=jax.ShapeDtypeStruct((T, D), F32))(order, dproj, w)


def _input_grad(rest, w_in, order):
    dproj, x2, ng, dy, small = rest
    dh = _d_h(dproj, w_in, order)
    grad_x, dng = _rms_bwd(x2, dh, ng, dy, "rms_x_bwd")
    small = [dng if t is None else t for t in small]
    return grad_x, jnp.concatenate(small, axis=1)


def _local_step(x, mem, tgt, norm_g, mem_norm_g, gq_all, gk_all, conv_w, mem_gq, mem_gk, W):
    hb, hbt, mhb = _norms(x, mem, norm_g, mem_norm_g)
    Cs = W["w_in"].shape[1] // 4
    shards = (0, 2, 1, 3)
    order = jnp.array(shards, dtype=jnp.int32)
    w_rel = jnp.concatenate([W["w_in"][:, s * Cs:(s + 1) * Cs] for s in shards], axis=1)
    meta = jnp.array(shards + (0,), dtype=jnp.int32)
    proj2 = _proj_chunk(hb, w_rel, meta, 0, 1, None, None, "proj_0")
    for j, nslots in ((1, 2), (3, 1)):
        for half in (1, 0):
            proj2 = _proj_chunk(hb, w_rel, meta, j, nslots, half, proj2, f"proj_{j}_{half}")
    pre = (hb, hbt, mhb, proj2, *_attention_fwd(proj2, x.shape[0], gq_all, gk_all),
           _conv_branch_fwd(proj2, x.shape[0], conv_w))
    G, rest, _ = _weight_grads(x, mem, tgt, norm_g, mem_norm_g, gq_all, gk_all, conv_w, mem_gq, mem_gk, W, pre)
    pos = jnp.zeros((2,), jnp.int32)
    G["w_in"] = jnp.concatenate([_dw_in_half(hbt, rest[0], pos, True, "dw_in_own"),
                                 _dw_in_half(hbt, rest[0], pos, False, "dw_in_sibling")], axis=0)
    grad_x, small = _input_grad(rest, w_rel, order)
    return grad_x.reshape(x.shape), G, small


BIG = (("w_in", "col"), ("mem_w_kv", "row"), ("w_br_attn", "col"), ("w_br_conv", "col"),
       ("w_br_mem", "col"), ("w_out", "row"))


def _coords():
    return lax.axis_index("x"), lax.axis_index("y"), lax.axis_index("c")


def _other_chips(x, y):
    return [(1 - x, y), (x, 1 - y), (1 - x, 1 - y)]


def _half(ref, kind, c):
    R, C = ref.shape
    if kind == "col":
        return ref.at[pl.ds(c * (R // 2), R // 2), :]
    return ref.at[:, pl.ds(c * (C // 2), C // 2)]


def _shard(ref, kind, s):
    R, C = ref.shape
    if kind == "col":
        return ref.at[:, pl.ds(s * (C // 4), C // 4)]
    return ref.at[pl.ds(s * (R // 4), R // 4), :]


def _piece(ref, kind, s, c):
    R, C = ref.shape
    if kind == "col":
        return ref.at[pl.ds(c * (R // 2), R // 2), pl.ds(s * (C // 4), C // 4)]
    return ref.at[pl.ds(s * (R // 4), R // 4), pl.ds(c * (C // 2), C // 2)]


def _remote(src, dst, sems_s, sems_r, k, dev):
    return pltpu.make_async_remote_copy(src_ref=src, dst_ref=dst, send_sem=sems_s.at[k], recv_sem=sems_r.at[k],
                                        device_id=dev, device_id_type=MESH)


HBM = pl.BlockSpec(memory_space=pltpu.HBM)
SEM = pl.BlockSpec(memory_space=pltpu.SEMAPHORE)
EFFECT = pltpu.SideEffectType.DATAFLOW_SIDE_EFFECTING


def _hbm(a):
    return pltpu.with_memory_space_constraint(a, pltpu.HBM)


def _start_copies(name, arrays, ncopies, make):
    n = len(arrays)

    def body(*refs):
        for cp in make(refs[:n], refs[n], refs[n + 1]):
            cp.start()

    outs = pl.pallas_call(
        body, name=name,
        out_shape=(pltpu.SemaphoreType.DMA((ncopies,)), pltpu.SemaphoreType.DMA((ncopies,)),
                   *[jax.ShapeDtypeStruct(t.shape, t.dtype) for t in arrays]),
        in_specs=[HBM] * n, out_specs=(SEM, SEM, *([HBM] * n)),
        input_output_aliases={i: i + 2 for i in range(n)},
        compiler_params=pltpu.CompilerParams(has_side_effects=EFFECT),
    )(*[_hbm(t) for t in arrays])
    return outs[0], outs[1], list(outs[2:])


def _wait_copies(name, send, recv, arrays, make, after):
    n = len(arrays)

    def body(*refs):
        for cp in make(refs[:n], refs[n], refs[n + 1]):
            cp.wait_send()
            cp.wait_recv()

    outs = pl.pallas_call(
        body, name=name, out_shape=[jax.ShapeDtypeStruct(t.shape, t.dtype) for t in arrays],
        in_specs=[HBM] * n + [SEM, SEM, ANY], out_specs=[HBM] * n,
        input_output_aliases={i: i for i in range(n)},
        compiler_params=pltpu.CompilerParams(has_side_effects=EFFECT),
    )(*arrays, send, recv, after)
    return list(outs)


def _w_in_copies(relations):
    def make(refs, send, recv):
        x, y, c = _coords()
        me = 2 * x + y
        chips = _other_chips(x, y)
        w, conv = refs[0], refs[1]
        cps = []
        for i, k in enumerate(relations):
            cps.append(_remote(_column_half(w, 0, c), _column_half(w, 1 + k, c), send, recv, 2 * i, (*chips[k], c)))
            mine = _shard(conv, "col", me)
            cps.append(_remote(mine, mine, send, recv, 2 * i + 1, (*chips[k], c)))
        return cps
    return make


def _column_half(w, slot, c):
    half = w.shape[1] // 8
    return w.at[:, pl.ds((2 * slot + c) * half, half)]


def _w_in_forward(relations):
    def make(refs, send, recv):
        x, y, c = _coords()
        cps = []
        for i, k in enumerate(relations):
            got = _column_half(refs[0], 1 + k, c)
            cps.append(_remote(got, got, send, recv, i, (x, y, 1 - c)))
        return cps
    return make


def _sibling_copy(refs, send, recv):
    x, y, c = _coords()
    return [_remote(refs[0], refs[1], send, recv, 0, (x, y, 1 - c))]


def _other_weight_copies(refs, send, recv):
    x, y, c = _coords()
    me = 2 * x + y
    cps = []
    for k, chip in enumerate(_other_chips(x, y)):
        for p, (_, kind) in enumerate(BIG[1:]):
            mine = _piece(refs[p], kind, me, c)
            cps.append(_remote(mine, mine, send, recv, 3 * p + k, (*chip, c)))
    return cps


def _other_weight_forward(refs, send, recv):
    x, y, c = _coords()
    cps = []
    for k, chip in enumerate(_other_chips(x, y)):
        s = 2 * chip[0] + chip[1]
        for p, (_, kind) in enumerate(BIG[1:]):
            got = _piece(refs[p], kind, s, c)
            cps.append(_remote(got, got, send, recv, 3 * p + k, (x, y, 1 - c)))
    return cps


def _share_copies(group):
    def make(refs, send, recv):
        x, y, c = _coords()
        cps = []
        for p, (_, kind) in enumerate(group):
            mine = _half(refs[p], kind, c)
            cps.append(_remote(mine, mine, send, recv, p, (x, y, 1 - c)))
        return cps
    return make


def _sibling_halves_start(G, group, carry, tag):
    n = len(group)
    parts = [G[name] for name, _ in group]
    lands = []
    for (_, kind), g in zip(group, parts):
        R, C = g.shape
        lands.append(lax.empty((R // 2, C) if kind == "col" else (R, C // 2), g.dtype))

    def make(refs, send, recv):
        x, y, c = _coords()
        return [_remote(_half(refs[p], group[p][1], 1 - c), refs[n + p], send, recv, p, (x, y, 1 - c)) for p in range(n)]

    send, recv, thru = _start_copies("sibling_halves_start_" + tag, [*parts, *lands, carry], n, make)
    return (send, recv, thru[:2 * n], make, tag), thru[2 * n]


def _presums(state, group, pos, after):
    send, recv, arrays, make, tag = state
    n = len(group)
    thru = _wait_copies("sibling_halves_wait_" + tag, send, recv, arrays, make, after)
    return [_presum(thru[p], thru[n + p], kind, pos, "presum_" + name) for p, (name, kind) in enumerate(group)]


def _presum(g, got, kind, pos, name):
    R, C = got.shape
    tr, tc = _tile(R, 512, 16), _tile(C, 2048)
    nr, nc = R // tr, C // tc

    def body(pos_ref, a_ref, b_ref, o_ref):
        o_ref[...] = (a_ref[...].astype(F32) + b_ref[...].astype(F32)).astype(o_ref.dtype)

    blk = pl.BlockSpec((tr, tc), lambda i, j, pos_ref: (i, j))
    if g.shape == got.shape:
        mine = blk
    elif kind == "col":
        mine = pl.BlockSpec((tr, tc), lambda i, j, pos_ref: (pos_ref[1] * nr + i, j))
    else:
        mine = pl.BlockSpec((tr, tc), lambda i, j, pos_ref: (i, pos_ref[1] * nc + j))
    spec = pltpu.PrefetchScalarGridSpec(num_scalar_prefetch=1, grid=(nr, nc), in_specs=[mine, blk], out_specs=blk)
    return _call(body, name=name, grid_spec=spec, out_shape=jax.ShapeDtypeStruct((R, C), WIRE_DTYPE))(pos, g, got)


def _chip_copies(group):
    n = len(group)

    def make(refs, send, recv):
        x, y, c = _coords()
        cps = []
        for k, chip in enumerate(_other_chips(x, y)):
            s = 2 * chip[0] + chip[1]
            for p in range(n):
                cps.append(_remote(_shard(refs[p], group[p][1], s), refs[n + p].at[k], send, recv, 3 * p + k, (*chip, c)))
        return cps
    return make


def _landing_zones(pres, group):
    lands = []
    for (_, kind), g in zip(group, pres):
        R, C = g.shape
        lands.append(lax.empty((3, R, C // 4) if kind == "col" else (3, R // 4, C), g.dtype))
    return lands


def _exchange_start(pres, group, carry, tag):
    n = len(group)
    make = _chip_copies(group)
    send, recv, thru = _start_copies("chip_exchange_start_" + tag, [*pres, *_landing_zones(pres, group), carry], 3 * n, make)
    return (send, recv, thru[:2 * n], make, tag), thru[2 * n]


def _exchange_wait(state, after):
    send, recv, arrays, make, tag = state
    thru = _wait_copies("chip_exchange_wait_" + tag, send, recv, arrays, make, after)
    n = len(thru) // 2
    return thru[:n], thru[n:]


def _reduce_into_shard(slots, pre, kind, pos, name):
    K, R, C = slots.shape
    tr, tc = _tile(R, 512, 16), _tile(C, 2176)
    nr, nc = R // tr, C // tc

    def body(pos_ref, s_ref, p_ref, o_ref):
        acc = p_ref[...].astype(F32)
        for k in range(K):
            acc = acc + s_ref[k].astype(F32)
        o_ref[...] = acc

    if kind == "col":
        own = pl.BlockSpec((tr, tc), lambda i, j, pos_ref: (i, pos_ref[0] * nc + j))
        full, out = (2 * R, C), pl.BlockSpec((tr, tc), lambda i, j, pos_ref: (pos_ref[1] * nr + i, j))
    else:
        own = pl.BlockSpec((tr, tc), lambda i, j, pos_ref: (pos_ref[0] * nr + i, j))
        full, out = (R, 2 * C), pl.BlockSpec((tr, tc), lambda i, j, pos_ref: (i, pos_ref[1] * nc + j))
    spec = pltpu.PrefetchScalarGridSpec(
        num_scalar_prefetch=1, grid=(nr, nc),
        in_specs=[pl.BlockSpec((K, tr, tc), lambda i, j, pos_ref: (0, i, j)), own], out_specs=out)
    return _call(body, name=name, grid_spec=spec, out_shape=jax.ShapeDtypeStruct(full, F32))(pos, slots, pre)


def _small_slots(pack, me):
    _, N = pack.shape

    def body(me_ref, p_ref, o_ref):
        o_ref[0] = p_ref[...]

    spec = pltpu.PrefetchScalarGridSpec(
        num_scalar_prefetch=1, grid=(1,), in_specs=[pl.BlockSpec((1, N), lambda i, me_ref: (0, 0))],
        out_specs=pl.BlockSpec((1, 1, N), lambda i, me_ref: (me_ref[0], 0, 0)))
    return _call(body, name="small_slots", grid_spec=spec, out_shape=jax.ShapeDtypeStruct((8, 1, N), pack.dtype))(me, pack)


def _small_copies(refs, send, recv):
    x, y, c = _coords()
    me = 4 * x + 2 * y + c
    cps = []
    for k in range(1, 8):
        dev = (x ^ (k >> 2), y ^ ((k >> 1) & 1), c ^ (k & 1))
        cps.append(_remote(refs[0], refs[1].at[me], send, recv, k - 1, dev))
    return cps


def _sum_small(slots):
    K, _, N = slots.shape

    def body(s_ref, o_ref):
        acc = s_ref[0]
        for k in range(1, K):
            acc = acc + s_ref[k]
        o_ref[...] = acc

    return _call(body, name="sum_small", in_specs=[pl.BlockSpec(memory_space=pltpu.VMEM)],
                 out_specs=pl.BlockSpec(memory_space=pltpu.VMEM), out_shape=jax.ShapeDtypeStruct((1, N), F32))(slots)


def _adamw(w, g, m, v, name, with_grad=False):
    R, C = w.shape
    tr, tc = _tile(R, 256, 8), _tile(C, 2176)

    def body(w_ref, g_ref, m_ref, v_ref, d_ref, nm_ref, nv_ref, *g_out):
        gv = g_ref[...]
        for ref in g_out:
            ref[...] = gv
        nm = ADAM_B1 * m_ref[...] + (1.0 - ADAM_B1) * gv
        nv = ADAM_B2 * v_ref[...] + (1.0 - ADAM_B2) * gv * gv
        m_hat = nm / (1.0 - ADAM_B1 ** ADAM_STEP)
        v_hat = nv / (1.0 - ADAM_B2 ** ADAM_STEP)
        d_ref[...] = -ADAM_LR * (m_hat / (jnp.sqrt(v_hat) + ADAM_EPS) + ADAM_WD * w_ref[...])
        nm_ref[...] = nm
        nv_ref[...] = nv

    spec = pl.BlockSpec((tr, tc), lambda i, j: (i, j))
    shp = jax.ShapeDtypeStruct((R, C), F32)
    nout = 4 if with_grad else 3
    return _call(body, name=name, grid=(R // tr, C // tc), in_specs=[spec] * 4, out_specs=[spec] * nout,
                 out_shape=[shp] * nout)(w, g, m, v)


SMALL = ("norm_g", "mem_norm_g", "attn_q_norm", "attn_k_norm", "conv_w", "mem_q_norm", "mem_k_norm")
WEIGHTS = ("norm_g", "mem_norm_g", "w_in", "attn_q_norm", "attn_k_norm", "conv_w", "mem_w_kv", "mem_q_norm",
           "mem_k_norm", "w_br_attn", "w_br_conv", "w_br_mem", "w_out")


def kernel(x, mem, norm_g, mem_norm_g, w_in, attn_q_norm, attn_k_norm, conv_w, mem_w_kv, mem_q_norm, mem_k_norm, w_br_attn, w_br_conv, w_br_mem, w_out, loss_target, m_norm_g, m_mem_norm_g, m_w_in, m_attn_q_norm, m_attn_k_norm, m_conv_w, m_mem_w_kv, m_mem_q_norm, m_mem_k_norm, m_w_br_attn, m_w_br_conv, m_w_br_mem, m_w_out, v_norm_g, v_mem_norm_g, v_w_in, v_attn_q_norm, v_attn_k_norm, v_conv_w, v_mem_w_kv, v_mem_q_norm, v_mem_k_norm, v_w_br_attn, v_w_br_conv, v_w_br_mem, v_w_out):
    w = dict(norm_g=norm_g, mem_norm_g=mem_norm_g, w_in=w_in, attn_q_norm=attn_q_norm, attn_k_norm=attn_k_norm,
             conv_w=conv_w, mem_w_kv=mem_w_kv, mem_q_norm=mem_q_norm, mem_k_norm=mem_k_norm, w_br_attn=w_br_attn,
             w_br_conv=w_br_conv, w_br_mem=w_br_mem, w_out=w_out)
    m = dict(norm_g=m_norm_g, mem_norm_g=m_mem_norm_g, w_in=m_w_in, attn_q_norm=m_attn_q_norm,
             attn_k_norm=m_attn_k_norm, conv_w=m_conv_w, mem_w_kv=m_mem_w_kv, mem_q_norm=m_mem_q_norm,
             mem_k_norm=m_mem_k_norm, w_br_attn=m_w_br_attn, w_br_conv=m_w_br_conv, w_br_mem=m_w_br_mem, w_out=m_w_out)
    v = dict(norm_g=v_norm_g, mem_norm_g=v_mem_norm_g, w_in=v_w_in, attn_q_norm=v_attn_q_norm,
             attn_k_norm=v_attn_k_norm, conv_w=v_conv_w, mem_w_kv=v_mem_w_kv, mem_q_norm=v_mem_q_norm,
             mem_k_norm=v_mem_k_norm, w_br_attn=v_w_br_attn, w_br_conv=v_w_br_conv, w_br_mem=v_w_br_mem, w_out=v_w_out)
    Bl, _, D = x.shape
    cx, cy = lax.axis_index("x"), lax.axis_index("y")
    chip = 2 * cx + cy
    pos = jnp.stack([chip, lax.axis_index("c")]).astype(jnp.int32)
    order = jnp.stack([chip] + [2 * a + b for a, b in _other_chips(cx, cy)]).astype(jnp.int32)
    n = len(BIG)

    slot0 = jnp.stack([jnp.zeros((), jnp.int32), pos[1]])
    w_rel = _place_shard(w["w_in"], "col", slot0, WIRE_DTYPE, "place_w_in_sent", half=0)
    conv_full = _place_shard(conv_w, "col", pos, F32, "place_conv_w")
    others = [_place_shard(w[name], kind, pos, WIRE_DTYPE, "place_" + name) for name, kind in BIG[1:]]
    hb, hbt, mhb = _norms(x, mem, norm_g, mem_norm_g)

    meta = jnp.concatenate([order, pos[1:]])
    near, near_fwd = _w_in_copies((0, 1)), _w_in_forward((0, 1))
    send, recv, (w_rel, conv_full) = _start_copies("gather_near_start", [w_rel, conv_full], 4, near)
    w_rel = _place_shard(w["w_in"], "col", slot0, WIRE_DTYPE, "place_w_in_kept", half=1, into=w_rel)
    proj = _proj_chunk(hb, w_rel, meta, 0, 1, None, None, "proj_own")
    w_rel, conv_full, *others = _wait_copies("gather_near_wait", send, recv, [w_rel, conv_full, *others], near, proj)

    fsend, frecv, (w_rel,) = _start_copies("gather_near_forward_start", [w_rel], 2, near_fwd)
    far, far_fwd = _w_in_copies((2,)), _w_in_forward((2,))
    send, recv, (w_rel, conv_full) = _start_copies("gather_far_start", [w_rel, conv_full], 2, far)
    proj = _proj_chunk(hb, w_rel, meta, 1, 2, 0, proj, "proj_near_landed")
    w_rel, = _wait_copies("gather_near_forward_wait", fsend, frecv, [w_rel], near_fwd, proj)
    proj = _proj_chunk(hb, w_rel, meta, 1, 2, 1, proj, "proj_near_forwarded")
    w_rel, conv_full = _wait_copies("gather_far_wait", send, recv, [w_rel, conv_full], far, proj)

    fsend, frecv, (w_rel,) = _start_copies("gather_far_forward_start", [w_rel], 1, far_fwd)
    send, recv, (*others, w_rel) = _start_copies("gather_rest_start", [*others, w_rel], 3 * (n - 1), _other_weight_copies)
    proj = _proj_chunk(hb, w_rel, meta, 3, 1, 0, proj, "proj_far_landed")
    w_rel, = _wait_copies("gather_far_forward_wait", fsend, frecv, [w_rel], far_fwd, proj)
    proj = _proj_chunk(hb, w_rel, meta, 3, 1, 1, proj, "proj_far_forwarded")
    os, ls, a = _attention_fwd(proj, Bl, attn_q_norm, attn_k_norm)
    *others, w_rel = _wait_copies("gather_rest_wait", send, recv, [*others, w_rel], _other_weight_copies, a[0])
    fsend, frecv, (*others, proj) = _start_copies("gather_rest_forward_start", [*others, proj], 3 * (n - 1),
                                                  _other_weight_forward)
    cc = _conv_branch_fwd(proj, Bl, conv_full)
    others = _wait_copies("gather_rest_forward_wait", fsend, frecv, others, _other_weight_forward, cc[0])
    W = {name: others[p] for p, (name, _) in enumerate(BIG[1:])}

    def rest_halves(G, carry):
        return _sibling_halves_start(G, BIG[1:], carry, "rest")

    def rest_exchange(state, after, carry):
        return _exchange_start(_presums(state, BIG[1:], pos, after), BIG[1:], carry, "rest")

    G, rest, rest_state = _weight_grads(
        x, mem, loss_target, norm_g, mem_norm_g, attn_q_norm, attn_k_norm, conv_full, mem_q_norm, mem_k_norm, W,
        (hb, hbt, mhb, proj, os, ls, a, cc), early=(rest_halves, rest_exchange))

    for_sibling = _dw_in_half(hbt, rest[0], pos, False, "dw_in_sibling")
    send, recv, (for_sibling, got, dproj) = _start_copies(
        "sibling_w_in_start", [for_sibling, lax.empty(for_sibling.shape, for_sibling.dtype), rest[0]], 1, _sibling_copy)
    mine = _dw_in_half(hbt, dproj, pos, True, "dw_in_own")
    for_sibling, got = _wait_copies("sibling_w_in_wait", send, recv, [for_sibling, got], _sibling_copy, mine)
    pre_w_in = _presum(mine, got, "col", pos, "presum_w_in")

    w_in_state, dproj = _exchange_start([pre_w_in], BIG[:1], dproj, "w_in")
    grad_x, small = _input_grad((dproj, *rest[1:]), w_rel, order)
    pres_rest, slots_rest = _exchange_wait(rest_state, grad_x)
    reds_rest = [_reduce_into_shard(slots_rest[p], pres_rest[p], kind, pos, "reduce_" + name)
                 for p, (name, kind) in enumerate(BIG[1:])]
    share_rest = _share_copies(BIG[1:])
    rsend, rrecv, reds_rest = _start_copies("share_rest_start", reds_rest, n - 1, share_rest)
    pres, slots = _exchange_wait(w_in_state, grad_x)
    red_w_in = _reduce_into_shard(slots[0], pres[0], "col", pos, "reduce_w_in")
    share_w_in = _share_copies(BIG[:1])
    wsend, wrecv, (red_w_in, small) = _start_copies("share_w_in_start", [red_w_in, small], 1, share_w_in)
    grad_x = grad_x.reshape(x.shape)

    slots = _small_slots(small, (2 * pos[:1] + pos[1:]))
    ssend, srecv, (small, slots) = _start_copies("gather_small_start", [small, slots], 7, _small_copies)
    reds_rest = _wait_copies("share_rest_wait", rsend, rrecv, reds_rest, share_rest, slots)
    grads = dict(zip([name for name, _ in BIG[1:]], reds_rest))
    delta, new_m, new_v = {}, {}, {}
    for name, _ in BIG[1:]:
        delta[name], new_m[name], new_v[name], grads[name] = _adamw(w[name], grads[name], m[name], v[name],
                                                                    "adamw_" + name, with_grad=True)

    small, slots = _wait_copies("gather_small_wait", ssend, srecv, [small, slots], _small_copies, delta[BIG[-1][0]])
    tot = _sum_small(slots)[0]
    loss = tot[0]
    off = 128
    for name, size in (("norm_g", D), ("mem_norm_g", D), ("attn_q_norm", NGROUP * HEAD), ("attn_k_norm", NGROUP * HEAD),
                       ("conv_w", 3 * CONVW), ("mem_q_norm", MEM_HD), ("mem_k_norm", MEM_HD)):
        grads[name] = tot[off:off + size]
        off += size
    cw = conv_w.shape[1]
    grads["conv_w"] = lax.dynamic_slice(grads["conv_w"].reshape(3, CONVW), (0, chip * cw), (3, cw))
    for name in SMALL:
        grads[name] = grads[name].reshape(w[name].shape)

    def packed(t):
        return jnp.concatenate([t[name].reshape(1, -1) for name in SMALL], axis=1)

    ds, ms, vs = _adamw(packed(w), packed(grads), packed(m), packed(v), "adamw_small")
    shared, = _wait_copies("share_w_in_wait", wsend, wrecv, [red_w_in], share_w_in, ds)
    delta["w_in"], new_m["w_in"], new_v["w_in"], grads["w_in"] = _adamw(w["w_in"], shared, m["w_in"], v["w_in"],
                                                                        "adamw_w_in", with_grad=True)
    off = 0
    for name in SMALL:
        size = w[name].size
        delta[name] = ds[0, off:off + size].reshape(w[name].shape)
        new_m[name] = ms[0, off:off + size].reshape(w[name].shape)
        new_v[name] = vs[0, off:off + size].reshape(w[name].shape)
        off += size

    return (loss, grad_x, *[grads[n] for n in WEIGHTS], *[delta[n] for n in WEIGHTS],
            *[new_m[n] for n in WEIGHTS], *[new_v[n] for n in WEIGHTS])
```

```python
import functools

import jax
import jax.numpy as jnp
from jax import lax
from jax.experimental import pallas as pl
from jax.experimental.pallas import tpu as pltpu

F32 = jnp.float32
MXU_DTYPE = jnp.bfloat16
WIRE_DTYPE = jnp.bfloat16
PROJ_DTYPE = jnp.bfloat16
EPS = 1e-6
NEG = -1e30

HEAD = 128
HPG = 4
GW = HPG * HEAD
DILATIONS = (1, 4, 16)
NGROUP = len(DILATIONS)
BLK = 128
QKV = NGROUP * GW
CONVW = 1024
MEM_HEADS = 4
MEM_HD = 256
MEMW = MEM_HEADS * MEM_HD
Q0, K0, V0 = 0, QKV, 2 * QKV
ZA = 3 * QKV
CB, CC, CV, ZC = ZA + GW, ZA + GW + CONVW, ZA + GW + 2 * CONVW, ZA + GW + 3 * CONVW
MQ = ZC + CONVW
ZM = MQ + MEMW
G0 = ZM + MEMW

ADAM_LR, ADAM_B1, ADAM_B2, ADAM_EPS, ADAM_WD, ADAM_STEP = 0.001, 0.9, 0.999, 1e-08, 0.01, 10

VMEM_LIMIT = 56 * 1024 * 1024
MESH = pl.DeviceIdType.MESH
ANY = pl.BlockSpec(memory_space=pl.ANY)


def _tile(n, pref, mult=128):
    t = min(pref, n)
    while t > mult and (n % t or t % mult):
        t -= mult
    assert n % t == 0, (n, pref)
    return t


def _call(body, *, name, out_shape, grid=(), in_specs=None, out_specs=None, scratch_shapes=(),
          aliases=None, grid_spec=None):
    kw = {}
    if grid_spec is not None:
        kw["grid_spec"] = grid_spec
        ngrid = len(grid_spec.grid)
    else:
        kw.update(grid=grid, in_specs=in_specs, out_specs=out_specs, scratch_shapes=list(scratch_shapes))
        ngrid = len(grid)
    params = pltpu.CompilerParams(dimension_semantics=("arbitrary",) * ngrid, vmem_limit_bytes=VMEM_LIMIT)
    return pl.pallas_call(body, name=name, out_shape=out_shape, compiler_params=params,
                          input_output_aliases=aliases or {}, **kw)


_DIMS = {"nn": (((1,), (0,)), ((), ())), "nt": (((1,), (1,)), ((), ())), "tn": (((0,), (0,)), ((), ()))}


def _mxu(a, b, mode):
    return lax.dot_general(a.astype(MXU_DTYPE), b.astype(MXU_DTYPE), _DIMS[mode], preferred_element_type=F32)


@functools.partial(jax.custom_vjp, nondiff_argnums=(2,))
def _dot(a, b, mode):
    return _mxu(a, b, mode)


def _dot_fwd(a, b, mode):
    return _mxu(a, b, mode), (a, b)


def _dot_bwd(mode, res, g):
    a, b = res
    if mode == "nn":
        return _mxu(g, b, "nt"), _mxu(a, g, "tn")
    if mode == "nt":
        return _mxu(g, b, "nn"), _mxu(g, a, "tn")
    return _mxu(b, g, "nt"), _mxu(a, g, "nn")


_dot.defvjp(_dot_fwd, _dot_bwd)


def _sig(z):
    return 1.0 / (1.0 + jnp.exp(-z))


def _silu(z):
    return z * _sig(z)


def _rms_rows(t, g):
    return t * lax.rsqrt(jnp.mean(t * t, axis=-1, keepdims=True) + EPS) * g


def _attn_block(q, k2, v2, gq, gk, first):
    qn = _rms_rows(q, gq)
    kn = _rms_rows(k2, gk)
    s = jnp.where(_band_mask(first, k2.shape[0]), _mxu(qn, kn, "nt") * (HEAD ** -0.5), NEG)
    m = jnp.max(s, axis=-1, keepdims=True)
    p = jnp.exp(s - m)
    den = jnp.sum(p, axis=-1, keepdims=True)
    o = _mxu(p, v2, "nn") / den
    return o, m + jnp.log(den)


def _band_mask(first, nkeys):
    a = lax.broadcasted_iota(jnp.int32, (BLK, nkeys), 0)
    b = lax.broadcasted_iota(jnp.int32, (BLK, nkeys), 1)
    if nkeys == BLK:
        return b <= a
    return (b >= a) & (b <= a + BLK) & (b >= jnp.where(first, BLK, 0))


def _norm_parts(t):
    r = lax.rsqrt(jnp.mean(t * t, axis=-1, keepdims=True) + EPS)
    return r, t * r


def _norm_bwd(dn, g, r, th):
    dth = dn * g
    return r * (dth - th * jnp.mean(dth * th, axis=-1, keepdims=True)), jnp.sum(dn * th, axis=0, keepdims=True)


def _attn_block_bwd(q, k2, v2, gq, gk, first, do, o, lse, dlse):
    scale = HEAD ** -0.5
    rq, qh = _norm_parts(q)
    rk, kh = _norm_parts(k2)
    qn, kn = qh * gq, kh * gk
    s = jnp.where(_band_mask(first, k2.shape[0]), _mxu(qn, kn, "nt") * scale, NEG)
    p = jnp.exp(s - lse)
    ds = p * (_mxu(do, v2, "nt") + (dlse - jnp.sum(do * o, axis=-1, keepdims=True))) * scale
    dq, dgq = _norm_bwd(_mxu(ds, kn, "nn"), gq, rq, qh)
    dk2, dgk = _norm_bwd(_mxu(ds, qn, "tn"), gk, rk, kh)
    return dq, dk2, _mxu(p, do, "tn"), dgq, dgk


def _combine(o1, o2, o3, l1, l2, l3, z):
    m = lax.stop_gradient(jnp.maximum(jnp.maximum(l1, l2), l3))
    e1, e2, e3 = jnp.exp(l1 - m), jnp.exp(l2 - m), jnp.exp(l3 - m)
    return (e1 * o1 + e2 * o2 + e3 * o3) / (e1 + e2 + e3) * _silu(z)


def _mem_block(q, z, kv, gq, gk):
    outs = []
    for h in range(MEM_HEADS):
        sl = slice(h * MEM_HD, (h + 1) * MEM_HD)
        qn = _rms_rows(q[:, sl], gq)
        kn = _rms_rows(kv[:, sl], gk)
        s = _dot(qn, kn, "nt") * (MEM_HD ** -0.5)
        m = lax.stop_gradient(jnp.max(s, axis=-1, keepdims=True))
        p = jnp.exp(s - m)
        den = jnp.sum(p, axis=-1, keepdims=True)
        outs.append(_dot(p, kv[:, MEMW + h * MEM_HD:MEMW + (h + 1) * MEM_HD], "nn") / den)
    return jnp.concatenate(outs, axis=-1) * _silu(z)


def _cast(w, name):
    R, C = w.shape
    tr, tc = _tile(R, 512, 8), _tile(C, 2176)

    def body(w_ref, o_ref):
        o_ref[...] = w_ref[...].astype(o_ref.dtype)

    spec = pl.BlockSpec((tr, tc), lambda i, j: (i, j))
    return _call(body, name=name, grid=(R // tr, C // tc), in_specs=[spec], out_specs=spec,
                 out_shape=jax.ShapeDtypeStruct((R, C), WIRE_DTYPE))(w)


def _place_shard(w, kind, pos, dtype, name, slot=0, into=None, half=None):
    R, C = w.shape
    tr, tc = _tile(R, 512, 8), _tile(C if half is None else C // 2, 2176)
    nr, nc = R // tr, C // tc
    ncols = nc if half is None else nc // 2

    def body(pos_ref, w_ref, *rest):
        rest[-1][...] = w_ref[...].astype(rest[-1].dtype)

    def col(j, pos_ref):
        if half is None:
            return j
        return (pos_ref[1] if half == 0 else 1 - pos_ref[1]) * ncols + j

    if kind == "col":
        full = (R, 4 * C)
        out = pl.BlockSpec((tr, tc), lambda i, j, pos_ref: (i, pos_ref[slot] * nc + col(j, pos_ref)))
    else:
        full, out = (4 * R, C), pl.BlockSpec((tr, tc), lambda i, j, pos_ref: (pos_ref[slot] * nr + i, j))
    in_specs, args = [pl.BlockSpec((tr, tc), lambda i, j, pos_ref: (i, col(j, pos_ref)))], [pos, w]
    if into is not None:
        in_specs.append(ANY)
        args.append(into)
    spec = pltpu.PrefetchScalarGridSpec(num_scalar_prefetch=1, grid=(nr, ncols), in_specs=in_specs, out_specs=out)
    return _call(body, name=name, grid_spec=spec, out_shape=jax.ShapeDtypeStruct(full, dtype),
                 aliases={} if into is None else {2: 0})(*args)


def _matmul(a, b, mode, out_dtype, *, name, tm=512, tn=512, tk=512):
    if mode == "nn":
        (M, K), (_, N) = a.shape, b.shape
    elif mode == "nt":
        (M, K), (N, _) = a.shape, b.shape
    else:
        (K, M), (_, N) = a.shape, b.shape
    tm, tn, tk = _tile(M, tm), _tile(N, tn), _tile(K, tk)
    nk = K // tk

    def body(a_ref, b_ref, o_ref, *acc):
        part = lax.dot_general(a_ref[...], b_ref[...], _DIMS[mode], preferred_element_type=F32)
        if nk == 1:
            o_ref[...] = part.astype(o_ref.dtype)
            return
        acc_ref, = acc
        k = pl.program_id(2)

        @pl.when(k == 0)
        def _():
            acc_ref[...] = part

        @pl.when(k > 0)
        def _():
            acc_ref[...] += part

        @pl.when(k == nk - 1)
        def _():
            o_ref[...] = acc_ref[...].astype(o_ref.dtype)

    a_spec = pl.BlockSpec((tk, tm), lambda i, j, k: (k, i)) if mode == "tn" else pl.BlockSpec((tm, tk), lambda i, j, k: (i, k))
    b_spec = pl.BlockSpec((tn, tk), lambda i, j, k: (j, k)) if mode == "nt" else pl.BlockSpec((tk, tn), lambda i, j, k: (k, j))
    return _call(body, name=name, grid=(M // tm, N // tn, nk), in_specs=[a_spec, b_spec],
                 out_specs=pl.BlockSpec((tm, tn), lambda i, j, k: (i, j)),
                 out_shape=jax.ShapeDtypeStruct((M, N), out_dtype),
                 scratch_shapes=[] if nk == 1 else [pltpu.VMEM((tm, tn), F32)])(a, b)


def _rms_fwd(x, g, name):
    R, D = x.shape
    tr = _tile(R, 512)

    def body(x_ref, g_ref, o_ref, t_ref):
        y = _rms_rows(x_ref[...], g_ref[...])
        o_ref[...] = y.astype(o_ref.dtype)
        t_ref[...] = y.T.astype(t_ref.dtype)

    row = pl.BlockSpec((tr, D), lambda i: (i, 0))
    return _call(body, name=name, grid=(R // tr,), in_specs=[row, pl.BlockSpec((1, D), lambda i: (0, 0))],
                 out_specs=[row, pl.BlockSpec((D, tr), lambda i: (0, i))],
                 out_shape=[jax.ShapeDtypeStruct((R, D), MXU_DTYPE), jax.ShapeDtypeStruct((D, R), MXU_DTYPE)])(x, g)


def _rms_bwd(x, dh, g, dy, name):
    R, D = x.shape
    tr = _tile(R, 256)
    with_dx = dy is not None

    def body(*refs):
        if with_dx:
            x_ref, dh_ref, g_ref, dy_ref, dx_ref, dg_ref = refs
        else:
            x_ref, dh_ref, g_ref, dg_ref = refs
        xv, dhv = x_ref[...], dh_ref[...]
        r = lax.rsqrt(jnp.mean(xv * xv, axis=-1, keepdims=True) + EPS)
        xh = xv * r

        @pl.when(pl.program_id(0) == 0)
        def _():
            dg_ref[...] = jnp.zeros_like(dg_ref)

        dg_ref[...] += jnp.sum(dhv * xh, axis=0, keepdims=True)
        if with_dx:
            dxh = dhv * g_ref[...]
            dx_ref[...] = dy_ref[...] + r * (dxh - xh * jnp.mean(dxh * xh, axis=-1, keepdims=True))

    row = pl.BlockSpec((tr, D), lambda i: (i, 0))
    vec = pl.BlockSpec((1, D), lambda i: (0, 0))
    dg_shape = jax.ShapeDtypeStruct((1, D), F32)
    if with_dx:
        return _call(body, name=name, grid=(R // tr,), in_specs=[row, row, vec, row], out_specs=[row, vec],
                     out_shape=[jax.ShapeDtypeStruct((R, D), F32), dg_shape])(x, dh, g, dy)
    return None, _call(body, name=name, grid=(R // tr,), in_specs=[row, row, vec], out_specs=vec,
                       out_shape=dg_shape)(x, dh, g)


def _attn_geom(g, d):
    hc = HPG if d == 1 else 1
    cw = hc * HEAD
    cq, ck, cv = (Q0 + g * GW) // cw, (K0 + g * GW) // cw, (V0 + g * GW) // cw
    return (1, BLK * d, cw), hc, HPG // hc, cq, ck, cv


def _rows(ref, r, d, sl):
    if d == 1:
        return ref[0, :, sl]
    return ref.at[0][pl.ds(r, BLK, stride=d), sl]


def _set_rows(ref, r, d, sl, val):
    if d == 1:
        ref[0, :, sl] = val
    else:
        ref.at[0][pl.ds(r, BLK, stride=d), sl] = val


def _stage_rows(ref, r, d, sl, val):
    if d == 1:
        ref[:, sl] = val
    else:
        ref[pl.ds(r, BLK, stride=d), sl] = val


def _proj_stages(blk, d):
    return [] if d == 1 else [pltpu.VMEM(blk[1:], F32)] * 5


def _proj_rows(refs, stages, d):
    if d == 1:
        return [lambda r, sl, ref=ref: ref[0, :, sl].astype(F32) for ref in refs]
    for ref, stage in zip(refs, stages):
        stage[...] = ref[0].astype(F32)
    return [lambda r, sl, stage=stage: stage[pl.ds(r, BLK, stride=d), sl] for stage in stages]


def _attn_fwd(proj3, gq, gk, g, d):
    Bl, S, _ = proj3.shape
    blk, hc, ncb, cq, ck, cv = _attn_geom(g, d)
    nb = S // blk[1]
    if nb == 1:
        return _attn_single_fwd(proj3, gq, gk, g, d)

    def body(q_ref, kp_ref, kc_ref, vp_ref, vc_ref, gq_ref, gk_ref, o_ref, lse_ref, *stages):
        first = pl.program_id(2) == 0
        q, kp, kc, vp, vc = _proj_rows((q_ref, kp_ref, kc_ref, vp_ref, vc_ref), stages, d)
        def run(alone):
            for r in range(d):
                for h in range(hc):
                    sl = slice(h * HEAD, (h + 1) * HEAD)
                    if alone:
                        k2, v2 = kc(r, sl), vc(r, sl)
                    else:
                        k2 = jnp.concatenate([kp(r, sl), kc(r, sl)], axis=0)
                        v2 = jnp.concatenate([vp(r, sl), vc(r, sl)], axis=0)
                    o, lse = _attn_block(q(r, sl), k2, v2, gq_ref[...], gk_ref[...], False)
                    _set_rows(o_ref, r, d, sl, o)
                    _set_rows(lse_ref, r, d, sl, jnp.broadcast_to(lse, (BLK, HEAD)))

        pl.when(first)(lambda: run(True))
        pl.when(jnp.logical_not(first))(lambda: run(False))

    def cur(c0):
        return pl.BlockSpec(blk, lambda b, j, i: (b, i, c0 + j))

    def prev(c0):
        return pl.BlockSpec(blk, lambda b, j, i: (b, jnp.maximum(i - 1, 0), c0 + j))

    vec = pl.BlockSpec((1, HEAD), lambda b, j, i: (0, 0))
    out = pl.BlockSpec(blk, lambda b, j, i: (b, i, j))
    shp = jax.ShapeDtypeStruct((Bl, S, GW), F32)
    return _call(body, name=f"attn_fwd_g{g}", grid=(Bl, ncb, nb),
                 in_specs=[cur(cq), prev(ck), cur(ck), prev(cv), cur(cv), vec, vec],
                 out_specs=[out, out], out_shape=[shp, shp], scratch_shapes=_proj_stages(blk, d),
                 )(proj3, proj3, proj3, proj3, proj3, gq, gk)


def _attn_single_fwd(proj3, gq, gk, g, d):
    Bl, S, _ = proj3.shape
    blk, hc, ncb, cq, ck, cv = _attn_geom(g, d)

    def body(q_ref, k_ref, v_ref, gq_ref, gk_ref, o_ref, lse_ref, *stages):
        q, k, v = _proj_rows((q_ref, k_ref, v_ref), stages, d)
        for r in range(d):
            for h in range(hc):
                sl = slice(h * HEAD, (h + 1) * HEAD)
                o, lse = _attn_block(q(r, sl), k(r, sl), v(r, sl), gq_ref[...], gk_ref[...], True)
                _set_rows(o_ref, r, d, sl, o)
                _set_rows(lse_ref, r, d, sl, jnp.broadcast_to(lse, (BLK, HEAD)))

    def at(c0):
        return pl.BlockSpec(blk, lambda b, j: (b, 0, c0 + j))

    vec = pl.BlockSpec((1, HEAD), lambda b, j: (0, 0))
    shp = jax.ShapeDtypeStruct((Bl, S, GW), F32)
    return _call(body, name=f"attn_fwd_g{g}", grid=(Bl, ncb), in_specs=[at(cq), at(ck), at(cv), vec, vec],
                 out_specs=[at(0), at(0)], out_shape=[shp, shp], scratch_shapes=_proj_stages(blk, d)[:3],
                 )(proj3, proj3, proj3, gq, gk)


def _attn_single_bwd(proj3, gq, gk, o3, l3, do3, dl3, g, d):
    Bl, S, _ = proj3.shape
    blk, hc, ncb, cq, ck, cv = _attn_geom(g, d)

    def body(q_ref, k_ref, v_ref, gq_ref, gk_ref, o_ref, l_ref, do_ref, dl_ref,
             dq_ref, dk_ref, dv_ref, dgq_ref, dgk_ref, sq_ref, sk_ref, sv_ref, *stages):
        @pl.when((pl.program_id(0) == 0) & (pl.program_id(1) == 0))
        def _():
            dgq_ref[...] = jnp.zeros_like(dgq_ref)
            dgk_ref[...] = jnp.zeros_like(dgk_ref)

        dgq, dgk = jnp.zeros((1, HEAD), F32), jnp.zeros((1, HEAD), F32)
        q, k, v = _proj_rows((q_ref, k_ref, v_ref), stages, d)
        for r in range(d):
            for h in range(hc):
                sl = slice(h * HEAD, (h + 1) * HEAD)
                dq, dk, dv, a, b = _attn_block_bwd(
                    q(r, sl), k(r, sl), v(r, sl), gq_ref[...], gk_ref[...], True, _rows(do_ref, r, d, sl),
                    _rows(o_ref, r, d, sl), _rows(l_ref, r, d, sl)[:, :1], _rows(dl_ref, r, d, sl)[:, :1])
                _stage_rows(sq_ref, r, d, sl, dq)
                _stage_rows(sk_ref, r, d, sl, dk)
                _stage_rows(sv_ref, r, d, sl, dv)
                dgq, dgk = dgq + a, dgk + b
        dgq_ref[...] += dgq
        dgk_ref[...] += dgk
        dq_ref[0] = sq_ref[...].astype(dq_ref.dtype)
        dk_ref[0] = sk_ref[...].astype(dk_ref.dtype)
        dv_ref[0] = sv_ref[...].astype(dv_ref.dtype)

    def at(c0):
        return pl.BlockSpec(blk, lambda b, j: (b, 0, c0 + j))

    vec = pl.BlockSpec((1, HEAD), lambda b, j: (0, 0))
    shp = jax.ShapeDtypeStruct((Bl, S, GW), MXU_DTYPE)
    gshp = jax.ShapeDtypeStruct((1, HEAD), F32)
    return _call(body, name=f"attn_bwd_g{g}", grid=(Bl, ncb),
                 in_specs=[at(cq), at(ck), at(cv), vec, vec, at(0), at(0), at(0), at(0)],
                 out_specs=[at(0), at(0), at(0), vec, vec], out_shape=[shp, shp, shp, gshp, gshp],
                 scratch_shapes=[pltpu.VMEM(blk[1:], F32)] * 3 + _proj_stages(blk, d)[:3],
                 )(proj3, proj3, proj3, gq, gk, o3, l3, do3, dl3)


def _attn_bwd(proj3, gq, gk, o3, l3, do3, dl3, g, d):
    Bl, S, _ = proj3.shape
    blk, hc, ncb, cq, ck, cv = _attn_geom(g, d)
    nb = S // blk[1]
    if nb == 1:
        return _attn_single_bwd(proj3, gq, gk, o3, l3, do3, dl3, g, d)

    def body(q_ref, kp_ref, kc_ref, vp_ref, vc_ref, gq_ref, gk_ref, o_ref, l_ref, do_ref, dl_ref,
             dq_ref, dk_ref, dv_ref, dgq_ref, dgk_ref, ck_ref, cv_ref, sq_ref, sk_ref, sv_ref, *stages):
        i = pl.program_id(2)
        first = i == 0

        @pl.when((pl.program_id(0) == 0) & (pl.program_id(1) == 0) & first)
        def _():
            dgq_ref[...] = jnp.zeros_like(dgq_ref)
            dgk_ref[...] = jnp.zeros_like(dgk_ref)

        def run(alone):
            dgq, dgk = jnp.zeros((1, HEAD), F32), jnp.zeros((1, HEAD), F32)
            q, kp, kc, vp, vc = _proj_rows((q_ref, kp_ref, kc_ref, vp_ref, vc_ref), stages, d)
            for r in range(d):
                rs = slice(r * BLK, (r + 1) * BLK)
                for h in range(hc):
                    sl = slice(h * HEAD, (h + 1) * HEAD)
                    if alone:
                        k2, v2 = kc(r, sl), vc(r, sl)
                    else:
                        k2 = jnp.concatenate([kp(r, sl), kc(r, sl)], axis=0)
                        v2 = jnp.concatenate([vp(r, sl), vc(r, sl)], axis=0)
                    dq, dk2, dv2, a, b = _attn_block_bwd(
                        q(r, sl), k2, v2, gq_ref[...], gk_ref[...], False, _rows(do_ref, r, d, sl),
                        _rows(o_ref, r, d, sl), _rows(l_ref, r, d, sl)[:, :1], _rows(dl_ref, r, d, sl)[:, :1])
                    _stage_rows(sq_ref, r, d, sl, dq)
                    if alone:
                        _stage_rows(sk_ref, r, d, sl, jnp.zeros((BLK, HEAD), F32))
                        _stage_rows(sv_ref, r, d, sl, jnp.zeros((BLK, HEAD), F32))
                    else:
                        _stage_rows(sk_ref, r, d, sl, ck_ref[rs, sl] + dk2[:BLK])
                        _stage_rows(sv_ref, r, d, sl, cv_ref[rs, sl] + dv2[:BLK])
                    ck_ref[rs, sl] = dk2[-BLK:]
                    cv_ref[rs, sl] = dv2[-BLK:]
                    dgq, dgk = dgq + a, dgk + b
            dgq_ref[...] += dgq
            dgk_ref[...] += dgk
            dq_ref[0] = sq_ref[...].astype(dq_ref.dtype)

        pl.when(first)(lambda: run(True))
        pl.when((i > 0) & (i < nb))(lambda: run(False))

        @pl.when(i == nb)
        def _():
            for r in range(d):
                rs = slice(r * BLK, (r + 1) * BLK)
                _stage_rows(sk_ref, r, d, slice(None), ck_ref[rs, :])
                _stage_rows(sv_ref, r, d, slice(None), cv_ref[rs, :])

        dk_ref[0] = sk_ref[...].astype(dk_ref.dtype)
        dv_ref[0] = sv_ref[...].astype(dv_ref.dtype)

    def cur(c0):
        return pl.BlockSpec(blk, lambda b, j, i: (b, jnp.minimum(i, nb - 1), c0 + j))

    def prev(c0):
        return pl.BlockSpec(blk, lambda b, j, i: (b, jnp.clip(i - 1, 0, nb - 1), c0 + j))

    vec = pl.BlockSpec((1, HEAD), lambda b, j, i: (0, 0))
    at_q = pl.BlockSpec(blk, lambda b, j, i: (b, jnp.minimum(i, nb - 1), j))
    at_k = pl.BlockSpec(blk, lambda b, j, i: (b, jnp.maximum(i - 1, 0), j))
    shp = jax.ShapeDtypeStruct((Bl, S, GW), MXU_DTYPE)
    gshp = jax.ShapeDtypeStruct((1, HEAD), F32)
    return _call(body, name=f"attn_bwd_g{g}", grid=(Bl, ncb, nb + 1),
                 in_specs=[cur(cq), prev(ck), cur(ck), prev(cv), cur(cv), vec, vec, at_q, at_q, at_q, at_q],
                 out_specs=[at_q, at_k, at_k, vec, vec], out_shape=[shp, shp, shp, gshp, gshp],
                 scratch_shapes=[pltpu.VMEM(blk[1:], F32)] * 5 + _proj_stages(blk, d),
                 )(proj3, proj3, proj3, proj3, proj3, gq, gk, o3, l3, do3, dl3)


def _combine_fwd(os, ls, proj2):
    T = proj2.shape[0]
    tr = _tile(T, 512)

    def body(o1, o2, o3, l1, l2, l3, z, a_ref, at_ref):
        a = _combine(o1[...], o2[...], o3[...], l1[...], l2[...], l3[...], z[...].astype(F32))
        a_ref[...] = a.astype(a_ref.dtype)
        at_ref[...] = a.T.astype(at_ref.dtype)

    row = pl.BlockSpec((tr, GW), lambda i: (i, 0))
    return _call(body, name="combine_fwd", grid=(T // tr,),
                 in_specs=[row] * 6 + [pl.BlockSpec((tr, GW), lambda i: (i, ZA // GW))],
                 out_specs=[row, pl.BlockSpec((GW, tr), lambda i: (0, i))],
                 out_shape=[jax.ShapeDtypeStruct((T, GW), MXU_DTYPE), jax.ShapeDtypeStruct((GW, T), MXU_DTYPE)],
                 )(*os, *ls, proj2)


def _combine_bwd(os, ls, proj2, da, dproj):
    T = proj2.shape[0]
    tr = _tile(T, 512)

    def body(o1, o2, o3, l1, l2, l3, z, da_ref, _, d1, d2, d3, e1, e2, e3, dz_ref):
        _, vjp = jax.vjp(_combine, o1[...], o2[...], o3[...], l1[...], l2[...], l3[...], z[...].astype(F32))
        go1, go2, go3, gl1, gl2, gl3, gz = vjp(da_ref[...])
        d1[...], d2[...], d3[...] = go1, go2, go3
        dz_ref[...] = gz.astype(dz_ref.dtype)
        for ref, gl in ((e1, gl1), (e2, gl2), (e3, gl3)):
            for h in range(HPG):
                sl = slice(h * HEAD, (h + 1) * HEAD)
                ref[:, sl] = jnp.broadcast_to(jnp.sum(gl[:, sl], axis=-1, keepdims=True), (tr, HEAD))

    row = pl.BlockSpec((tr, GW), lambda i: (i, 0))
    f = jax.ShapeDtypeStruct((T, GW), F32)
    z_attn = pl.BlockSpec((tr, GW), lambda i: (i, ZA // GW))
    outs = _call(body, name="combine_bwd", grid=(T // tr,), in_specs=[row] * 6 + [z_attn, row, ANY],
                 out_specs=[row] * 6 + [z_attn], out_shape=[f] * 6 + [jax.ShapeDtypeStruct(dproj.shape, dproj.dtype)],
                 aliases={8: 6})(*os, *ls, proj2, da, dproj)
    return outs[:3], outs[3:6], outs[6]


def _shift_down(u, j, t):
    return jnp.where(t >= j, pltpu.roll(u, j, 0), 0.0)


def _shift_up(u, j, t):
    n = u.shape[0]
    return jnp.where(t < n - j, pltpu.roll(u, n - j, 0), 0.0)


def _conv_specs(Bl, S, cw):
    def sec(c0):
        return pl.BlockSpec((1, S, cw), lambda j, b: (b, 0, c0 // cw + j))
    return [sec(CB), sec(CC), sec(CV), sec(ZC)], pl.BlockSpec((3, cw), lambda j, b: (0, j))


def _conv_fwd(proj3, conv_w):
    Bl, S, _ = proj3.shape
    cw = 256
    secs, wspec = _conv_specs(Bl, S, cw)

    def body(b_ref, c_ref, v_ref, z_ref, w_ref, o_ref, ot_ref):
        t = lax.broadcasted_iota(jnp.int32, (S, cw), 0)
        u = c_ref[0].astype(F32) * v_ref[0].astype(F32)
        y = w_ref[0:1, :] * u + w_ref[1:2, :] * _shift_down(u, 1, t) + w_ref[2:3, :] * _shift_down(u, 2, t)
        out = b_ref[0].astype(F32) * y * _silu(z_ref[0].astype(F32))
        o_ref[0] = out.astype(o_ref.dtype)
        ot_ref[...] = out.T.astype(ot_ref.dtype)

    return _call(body, name="conv_fwd", grid=(CONVW // cw, Bl), in_specs=secs + [wspec],
                 out_specs=[pl.BlockSpec((1, S, cw), lambda j, b: (b, 0, j)), pl.BlockSpec((cw, S), lambda j, b: (j, b))],
                 out_shape=[jax.ShapeDtypeStruct((Bl, S, CONVW), MXU_DTYPE),
                            jax.ShapeDtypeStruct((CONVW, Bl * S), MXU_DTYPE)])(proj3, proj3, proj3, proj3, conv_w)


def _conv_bwd(proj3, conv_w, dcc3, dproj):
    Bl, S, _ = proj3.shape
    cw = 256
    secs, wspec = _conv_specs(Bl, S, cw)

    def body(b_ref, c_ref, v_ref, z_ref, w_ref, d_ref, _, dproj_ref, dw_ref, stage, sems):
        t = lax.broadcasted_iota(jnp.int32, (S, cw), 0)
        bv, cv, vv, zv = (r[0].astype(F32) for r in (b_ref, c_ref, v_ref, z_ref))
        dv = d_ref[0]
        u = cv * vv
        u1, u2 = _shift_down(u, 1, t), _shift_down(u, 2, t)
        y = w_ref[0:1, :] * u + w_ref[1:2, :] * u1 + w_ref[2:3, :] * u2
        sg = _sig(zv)
        sz = zv * sg
        gy = dv * bv * sz
        du = w_ref[0:1, :] * gy + w_ref[1:2, :] * _shift_up(gy, 1, t) + w_ref[2:3, :] * _shift_up(gy, 2, t)
        j, b = pl.program_id(0), pl.program_id(1)
        tiles = [dv * y * sz, du * vv, du * cv, dv * bv * y * sg * (1.0 + zv * (1.0 - sg))]
        dsts = [dproj_ref.at[pl.ds(b * S, S), pl.ds(c0 + j * cw, cw)] for c0 in (CB, CC, CV, ZC)]
        _emit_tiles(j * Bl + b, (CONVW // cw) * Bl, tiles, dsts, stage, sems)

        @pl.when(pl.program_id(1) == 0)
        def _():
            dw_ref[...] = jnp.zeros_like(dw_ref)

        dw_ref[0:1, :] += jnp.sum(gy * u, axis=0, keepdims=True)
        dw_ref[1:2, :] += jnp.sum(gy * u1, axis=0, keepdims=True)
        dw_ref[2:3, :] += jnp.sum(gy * u2, axis=0, keepdims=True)

    blk = pl.BlockSpec((1, S, cw), lambda j, b: (b, 0, j))
    return _call(body, name="conv_bwd", grid=(CONVW // cw, Bl), in_specs=secs + [wspec, blk, ANY],
                 out_specs=[ANY, wspec],
                 out_shape=[jax.ShapeDtypeStruct(dproj.shape, dproj.dtype), jax.ShapeDtypeStruct((3, CONVW), F32)],
                 scratch_shapes=_emit_scratch(4, S, cw), aliases={6: 0})(proj3, proj3, proj3, proj3, conv_w, dcc3, dproj)


def _mem_specs(S, tq):
    q = pl.BlockSpec((1, tq, MEMW), lambda b, j: (b, j, MQ // MEMW))
    z = pl.BlockSpec((1, tq, MEMW), lambda b, j: (b, j, ZM // MEMW))
    kv = pl.BlockSpec((1, MEM_HD, 2 * MEMW), lambda b, j: (b, 0, 0))
    vec = pl.BlockSpec((1, MEM_HD), lambda b, j: (0, 0))
    blk = pl.BlockSpec((1, tq, MEMW), lambda b, j: (b, j, 0))
    return q, z, kv, vec, blk


def _mem_fwd(proj3, mkv3, gq, gk):
    Bl, S, _ = proj3.shape
    tq = _tile(S, 512)
    q, z, kv, vec, blk = _mem_specs(S, tq)

    def body(q_ref, z_ref, kv_ref, gq_ref, gk_ref, o_ref, ot_ref):
        out = _mem_block(q_ref[0].astype(F32), z_ref[0].astype(F32), kv_ref[0], gq_ref[...], gk_ref[...])
        o_ref[0] = out.astype(o_ref.dtype)
        ot_ref[...] = out.T.astype(ot_ref.dtype)

    nq = S // tq
    return _call(body, name="mem_fwd", grid=(Bl, nq), in_specs=[q, z, kv, vec, vec],
                 out_specs=[blk, pl.BlockSpec((MEMW, tq), lambda b, j: (0, b * nq + j))],
                 out_shape=[jax.ShapeDtypeStruct((Bl, S, MEMW), MXU_DTYPE),
                            jax.ShapeDtypeStruct((MEMW, Bl * S), MXU_DTYPE)])(proj3, proj3, mkv3, gq, gk)


def _mem_bwd(proj3, mkv3, gq, gk, dmo3, dproj):
    Bl, S, _ = proj3.shape
    tq = _tile(S, 256)
    q, z, kv, vec, blk = _mem_specs(S, tq)
    nq = S // tq

    def body(q_ref, z_ref, kv_ref, gq_ref, gk_ref, d_ref, _, dproj_ref, dkv_ref, dgq_ref, dgk_ref, stage, sems):
        _, vjp = jax.vjp(_mem_block, q_ref[0].astype(F32), z_ref[0].astype(F32), kv_ref[0], gq_ref[...], gk_ref[...])
        dq, dz, dkv, dgq, dgk = vjp(d_ref[0])
        j = pl.program_id(1)
        rows = pl.ds(pl.program_id(0) * S + j * tq, tq)
        dsts = [dproj_ref.at[rows, pl.ds(MQ, MEMW)], dproj_ref.at[rows, pl.ds(ZM, MEMW)]]
        _emit_tiles(pl.program_id(0) * nq + j, Bl * nq, [dq, dz], dsts, stage, sems)

        @pl.when(j == 0)
        def _():
            dkv_ref[0] = jnp.zeros_like(dkv)

        @pl.when((j == 0) & (pl.program_id(0) == 0))
        def _():
            dgq_ref[...] = jnp.zeros_like(dgq_ref)
            dgk_ref[...] = jnp.zeros_like(dgk_ref)

        dkv_ref[0] += dkv
        dgq_ref[...] += dgq
        dgk_ref[...] += dgk

    gshp = jax.ShapeDtypeStruct((1, MEM_HD), F32)
    return _call(body, name="mem_bwd", grid=(Bl, nq), in_specs=[q, z, kv, vec, vec, blk, ANY],
                 out_specs=[ANY, kv, vec, vec],
                 out_shape=[jax.ShapeDtypeStruct(dproj.shape, dproj.dtype), jax.ShapeDtypeStruct(mkv3.shape, F32),
                            gshp, gshp],
                 scratch_shapes=_emit_scratch(2, tq, MEMW), aliases={6: 0})(proj3, proj3, mkv3, gq, gk, dmo3, dproj)


def _merge_specs(T, D, tm, tn):
    def act(w):
        return pl.BlockSpec((tm, w), lambda i, n: (i, 0))

    def wsp(w):
        return pl.BlockSpec((w, tn), lambda i, n: (0, n))

    gates = [pl.BlockSpec((tm, tn), lambda i, n, k=k: (i, (G0 + k * D) // tn + n)) for k in range(3)]
    tile = pl.BlockSpec((tm, tn), lambda i, n: (i, n))
    return act, wsp, gates, tile


def _merge_fwd(a, cc, mo, wa, wc, wm, proj2):
    T, D = a.shape[0], wa.shape[1]
    tm, tn = _tile(T, 512), _tile(D, 1024)
    act, wsp, gates, tile = _merge_specs(T, D, tm, tn)

    def body(a_ref, c_ref, m_ref, wa_ref, wc_ref, wm_ref, g0, g1, g2, mg_ref, mt_ref, pa_ref, pc_ref, pm_ref):
        pa = jnp.dot(a_ref[...], wa_ref[...], preferred_element_type=F32)
        pc = jnp.dot(c_ref[...], wc_ref[...], preferred_element_type=F32)
        pm = jnp.dot(m_ref[...], wm_ref[...], preferred_element_type=F32)
        mg = _sig(g0[...].astype(F32)) * pa + _sig(g1[...].astype(F32)) * pc + _sig(g2[...].astype(F32)) * pm
        mg_ref[...] = mg.astype(mg_ref.dtype)
        mt_ref[...] = mg.T.astype(mt_ref.dtype)
        pa_ref[...] = pa.astype(pa_ref.dtype)
        pc_ref[...] = pc.astype(pc_ref.dtype)
        pm_ref[...] = pm.astype(pm_ref.dtype)

    shp = jax.ShapeDtypeStruct((T, D), MXU_DTYPE)
    return _call(body, name="merge_fwd", grid=(T // tm, D // tn),
                 in_specs=[act(GW), act(CONVW), act(MEMW), wsp(GW), wsp(CONVW), wsp(MEMW)] + gates,
                 out_specs=[tile, pl.BlockSpec((tn, tm), lambda i, n: (n, i)), tile, tile, tile],
                 out_shape=[shp, jax.ShapeDtypeStruct((D, T), MXU_DTYPE), shp, shp, shp],
                 )(a, cc, mo, wa, wc, wm, proj2, proj2, proj2)


def _emit_tiles(step, nsteps, tiles, dsts, stage, sems):
    slot = step % 2

    def copies(s):
        return [pltpu.make_async_copy(stage.at[s, k], dsts[k], sems.at[s, k]) for k in range(len(tiles))]

    @pl.when(step >= 2)
    def _():
        for cp in copies(slot):
            cp.wait()

    for k, t in enumerate(tiles):
        stage[slot, k] = t.astype(stage.dtype)
    for cp in copies(slot):
        cp.start()

    @pl.when(step == nsteps - 1)
    def _():
        for cp in copies(slot):
            cp.wait()
        if nsteps > 1:
            for cp in copies(1 - slot):
                cp.wait()


def _emit_scratch(k, rows, cols):
    return [pltpu.VMEM((2, k, rows, cols), MXU_DTYPE), pltpu.SemaphoreType.DMA((2, k))]


def _merge_bwd(dyb, w_out, proj2, pa, pc, pm):
    T, D = dyb.shape
    IN = proj2.shape[1]
    tm, tn = _tile(T, 512), _tile(D, 1024)
    _, _, gates, tile = _merge_specs(T, D, tm, tn)
    nn = D // tn

    def body(dy_ref, w_ref, g0, g1, g2, p0, p1, p2, dp0, dp1, dp2, dproj_ref, stage, sems):
        i, n = pl.program_id(0), pl.program_id(1)
        dm = lax.dot_general(dy_ref[...], w_ref[...], _DIMS["nt"], preferred_element_type=F32)
        tiles, dsts = [], []
        for k, (g_ref, p_ref, dp_ref) in enumerate(((g0, p0, dp0), (g1, p1, dp1), (g2, p2, dp2))):
            gt = _sig(g_ref[...].astype(F32))
            dp_ref[...] = (gt * dm).astype(dp_ref.dtype)
            tiles.append(dm * p_ref[...].astype(F32) * gt * (1.0 - gt))
            dsts.append(dproj_ref.at[pl.ds(i * tm, tm), pl.ds(G0 + k * D + n * tn, tn)])
        _emit_tiles(i * nn + n, (T // tm) * nn, tiles, dsts, stage, sems)

    shp = jax.ShapeDtypeStruct((T, D), MXU_DTYPE)
    return _call(body, name="merge_bwd", grid=(T // tm, nn),
                 in_specs=[pl.BlockSpec((tm, D), lambda i, n: (i, 0)), pl.BlockSpec((tn, D), lambda i, n: (n, 0))]
                 + gates + [tile] * 3,
                 out_specs=[tile] * 3 + [ANY], out_shape=[shp] * 3 + [jax.ShapeDtypeStruct((T, IN), MXU_DTYPE)],
                 scratch_shapes=_emit_scratch(3, tm, tn))(dyb, w_out, proj2, proj2, proj2, pa, pc, pm)


def _out_loss(merged, w_out, x, tgt):
    T, D = x.shape
    tm = _tile(T, 512)

    def body(m_ref, w_ref, x_ref, t_ref, dy_ref, dyb_ref, loss_ref):
        err = x_ref[...] + jnp.dot(m_ref[...], w_ref[...], preferred_element_type=F32) - t_ref[...]
        dy = err * (1.0 / D)
        dy_ref[...] = dy
        dyb_ref[...] = dy.astype(dyb_ref.dtype)

        @pl.when(pl.program_id(0) == 0)
        def _():
            loss_ref[...] = jnp.zeros_like(loss_ref)

        loss_ref[...] += jnp.sum(err * err) * (0.5 / D)

    row = pl.BlockSpec((tm, D), lambda i: (i, 0))
    return _call(body, name="out_loss", grid=(T // tm,),
                 in_specs=[row, pl.BlockSpec((D, D), lambda i: (0, 0)), row, row],
                 out_specs=[row, row, pl.BlockSpec((1, 128), lambda i: (0, 0))],
                 out_shape=[jax.ShapeDtypeStruct((T, D), F32), jax.ShapeDtypeStruct((T, D), MXU_DTYPE),
                            jax.ShapeDtypeStruct((1, 128), F32)])(merged, w_out, x, tgt)


def _proj_chunk(hb, w, meta, j, nslots, half, buf, name):
    T, D = hb.shape
    Cs = w.shape[1] // 4
    tm, tn = _tile(T, 1024), _tile(Cs // 2, 2176)
    nh = Cs // 2 // tn
    per = nh if half is not None else 2 * nh

    def body(meta_ref, a_ref, b_ref, *rest):
        rest[-1][...] = jnp.dot(a_ref[...], b_ref[...], preferred_element_type=F32).astype(rest[-1].dtype)

    def tile(n, m):
        if half is None:
            return n % per
        return (m[4] if half == 0 else 1 - m[4]) * nh + n % per

    in_specs = [pl.BlockSpec((tm, D), lambda n, i, m: (i, 0)),
                pl.BlockSpec((D, tn), lambda n, i, m: (0, (j + n // per) * 2 * nh + tile(n, m)))]
    args = [meta, hb, w]
    if buf is not None:
        in_specs.append(ANY)
        args.append(buf)
    spec = pltpu.PrefetchScalarGridSpec(
        num_scalar_prefetch=1, grid=(nslots * per, T // tm), in_specs=in_specs,
        out_specs=pl.BlockSpec((tm, tn), lambda n, i, m: (i, m[j + n // per] * 2 * nh + tile(n, m))))
    return _call(body, name=name, grid_spec=spec, out_shape=jax.ShapeDtypeStruct((T, 4 * Cs), PROJ_DTYPE),
                 aliases={} if buf is None else {3: 0})(*args)


def _norms(x, mem, norm_g, mem_norm_g):
    D = x.shape[-1]
    hb, hbt = _rms_fwd(x.reshape(-1, D), norm_g.reshape(1, D), "rms_x")
    mhb, _ = _rms_fwd(mem.reshape(-1, D), mem_norm_g.reshape(1, D), "rms_mem")
    return hb, hbt, mhb


def _attention_fwd(proj2, Bl, gq_all, gk_all):
    T, IN = proj2.shape
    proj3 = proj2.reshape(Bl, T // Bl, IN)
    os, ls = [], []
    for g, d in enumerate(DILATIONS):
        o, l = _attn_fwd(proj3, gq_all[g:g + 1], gk_all[g:g + 1], g, d)
        os.append(o.reshape(T, GW))
        ls.append(l.reshape(T, GW))
    return os, ls, _combine_fwd(os, ls, proj2)


def _conv_branch_fwd(proj2, Bl, conv_w):
    T, IN = proj2.shape
    cc, cct = _conv_fwd(proj2.reshape(Bl, T // Bl, IN), conv_w)
    return cc.reshape(T, CONVW), cct


def _weight_grads(x, mem, tgt, norm_g, mem_norm_g, gq_all, gk_all, conv_w, mem_gq, mem_gk, W, pre, early=None):
    Bl, S, D = x.shape
    T = Bl * S
    hb, hbt, mhb, proj2, os, ls, (a, at), (cc, cct) = pre
    IN = proj2.shape[1]
    proj3 = proj2.reshape(Bl, S, IN)
    x2, tgt2 = x.reshape(T, D), tgt.reshape(T, D)
    mem2 = mem.reshape(-1, D)
    ng, mng = norm_g.reshape(1, D), mem_norm_g.reshape(1, D)
    mgq, mgk = mem_gq.reshape(1, MEM_HD), mem_gk.reshape(1, MEM_HD)
    gqs = [gq_all[g:g + 1] for g in range(NGROUP)]
    gks = [gk_all[g:g + 1] for g in range(NGROUP)]

    mkv = _matmul(mhb, W["mem_w_kv"], "nn", F32, name="mem_kv", tm=512, tn=1024, tk=D)
    mkv3 = mkv.reshape(Bl, -1, 2 * MEMW)
    mo, mot = _mem_fwd(proj3, mkv3, mgq, mgk)
    mo = mo.reshape(T, MEMW)
    merged, mergedt, pa, pc, pm = _merge_fwd(a, cc, mo, W["w_br_attn"], W["w_br_conv"], W["w_br_mem"], proj2)
    dy, dyb, loss = _out_loss(merged, W["w_out"], x2, tgt2)

    G = {}
    G["w_out"] = _matmul(mergedt, dyb, "nn", WIRE_DTYPE, name="dw_out", tm=1024, tn=512, tk=T)
    dpa, dpc, dpm, dproj = _merge_bwd(dyb, W["w_out"], proj2, pa, pc, pm)
    G["w_br_attn"] = _matmul(at, dpa, "nn", WIRE_DTYPE, name="dw_br_attn", tm=512, tn=512, tk=T)
    G["w_br_conv"] = _matmul(cct, dpc, "nn", WIRE_DTYPE, name="dw_br_conv", tm=1024, tn=512, tk=T)
    G["w_br_mem"] = _matmul(mot, dpm, "nn", WIRE_DTYPE, name="dw_br_mem", tm=1024, tn=512, tk=T)
    da = _matmul(dpa, W["w_br_attn"], "nt", F32, name="d_attn", tm=1024, tn=512, tk=D)
    dcc = _matmul(dpc, W["w_br_conv"], "nt", F32, name="d_conv", tm=1024, tn=1024, tk=D)
    dmo = _matmul(dpm, W["w_br_mem"], "nt", F32, name="d_mem", tm=1024, tn=1024, tk=D)
    dproj, dmkv3, dmgq, dmgk = _mem_bwd(proj3, mkv3, mgq, mgk, dmo.reshape(Bl, S, MEMW), dproj)
    dmkv = _cast(dmkv3.reshape(-1, 2 * MEMW), "cast_dmkv")
    G["mem_w_kv"] = _matmul(mhb, dmkv, "tn", WIRE_DTYPE, name="dw_mem_kv", tm=1024, tn=1024, tk=512)
    early_state, dmkv = (None, dmkv) if early is None else early[0](G, dmkv)
    dmh = _matmul(dmkv, W["mem_w_kv"], "nt", F32, name="d_memh", tm=512, tn=1024, tk=2 * MEMW)
    _, dmng = _rms_bwd(mem2, dmh, mng, None, "rms_mem_bwd")
    if early is not None:
        early_state, da = early[1](early_state, dmng, da)

    dos, dls, dproj = _combine_bwd(os, ls, proj2, da, dproj)
    dgq, dgk = [], []
    for g, d in enumerate(DILATIONS):
        dq, dk, dv, gq_g, gk_g = _attn_bwd(proj3, gqs[g], gks[g], os[g].reshape(Bl, S, GW), ls[g].reshape(Bl, S, GW),
                                           dos[g].reshape(Bl, S, GW), dls[g].reshape(Bl, S, GW), g, d)
        for c0, part in ((Q0, dq), (K0, dk), (V0, dv)):
            dproj = lax.dynamic_update_slice(dproj, part.reshape(T, GW), (0, c0 + g * GW))
        dgq.append(gq_g)
        dgk.append(gk_g)
    dproj, dconv_w = _conv_bwd(proj3, conv_w, dcc.reshape(Bl, S, CONVW), dproj)
    small = [loss, None, dmng] + dgq + dgk + [dconv_w.reshape(1, 3 * CONVW), dmgq, dmgk]
    return G, (dproj, x2, ng, dy, small), early_state


def _dw_in_half(hbt, dproj, pos, own, name):
    D, T = hbt.shape
    IN = dproj.shape[1]
    R, tn = D // 2, _tile(IN, 1024)

    def body(pos_ref, a_ref, b_ref, o_ref):
        o_ref[...] = jnp.dot(a_ref[...], b_ref[...], preferred_element_type=F32).astype(o_ref.dtype)

    spec = pltpu.PrefetchScalarGridSpec(
        num_scalar_prefetch=1, grid=(IN // tn,),
        in_specs=[pl.BlockSpec((R, T), lambda j, p: (p[1] if own else 1 - p[1], 0)),
                  pl.BlockSpec((T, tn), lambda j, p: (0, j))],
        out_specs=pl.BlockSpec((R, tn), lambda j, p: (0, j)))
    return _call(body, name=name, grid_spec=spec, out_shape=jax.ShapeDtypeStruct((R, IN), WIRE_DTYPE))(pos, hbt, dproj)


def _d_h(dproj, w, order):
    T, IN = dproj.shape
    D, Cs = w.shape[0], IN // 4
    tm, tn = _tile(T, 1024), _tile(D, 1024)

    def body(order_ref, a_ref, b_ref, o_ref, acc_ref):
        part = lax.dot_general(a_ref[...], b_ref[...], _DIMS["nt"], preferred_element_type=F32)
        k = pl.program_id(2)

        @pl.when(k == 0)
        def _():
            acc_ref[...] = part

        @pl.when(k > 0)
        def _():
            acc_ref[...] += part

        @pl.when(k == 3)
        def _():
            o_ref[...] = acc_ref[...]

    spec = pltpu.PrefetchScalarGridSpec(
        num_scalar_prefetch=1, grid=(T // tm, D // tn, 4),
        in_specs=[pl.BlockSpec((tm, Cs), lambda i, n, k, o: (i, o[k])), pl.BlockSpec((tn, Cs), lambda i, n, k, o: (n, k))],
        out_specs=pl.BlockSpec((tm, tn), lambda i, n, k, o: (i, n)), scratch_shapes=[pltpu.VMEM((tm, tn), F32)])
    return _call(body, name="d_h", grid_spec=spec, out_shape=jax.ShapeDtypeStruct((T, D), F32))(order, dproj, w)


def _input_grad(rest, w_in, order):
    dproj, x2, ng, dy, small = rest
    dh = _d_h(dproj, w_in, order)
    grad_x, dng = _rms_bwd(x2, dh, ng, dy, "rms_x_bwd")
    small = [dng if t is None else t for t in small]
    return grad_x, jnp.concatenate(small, axis=1)


def _local_step(x, mem, tgt, norm_g, mem_norm_g, gq_all, gk_all, conv_w, mem_gq, mem_gk, W):
    hb, hbt, mhb = _norms(x, mem, norm_g, mem_norm_g)
    Cs = W["w_in"].shape[1] // 4
    shards = (0, 2, 1, 3)
    order = jnp.array(shards, dtype=jnp.int32)
    w_rel = jnp.concatenate([W["w_in"][:, s * Cs:(s + 1) * Cs] for s in shards], axis=1)
    meta = jnp.array(shards + (0,), dtype=jnp.int32)
    proj2 = _proj_chunk(hb, w_rel, meta, 0, 1, None, None, "proj_0")
    for j, nslots in ((1, 2), (3, 1)):
        for half in (1, 0):
            proj2 = _proj_chunk(hb, w_rel, meta, j, nslots, half, proj2, f"proj_{j}_{half}")
    pre = (hb, hbt, mhb, proj2, *_attention_fwd(proj2, x.shape[0], gq_all, gk_all),
           _conv_branch_fwd(proj2, x.shape[0], conv_w))
    G, rest, _ = _weight_grads(x, mem, tgt, norm_g, mem_norm_g, gq_all, gk_all, conv_w, mem_gq, mem_gk, W, pre)
    pos = jnp.zeros((2,), jnp.int32)
    G["w_in"] = jnp.concatenate([_dw_in_half(hbt, rest[0], pos, True, "dw_in_own"),
                                 _dw_in_half(hbt, rest[0], pos, False, "dw_in_sibling")], axis=0)
    grad_x, small = _input_grad(rest, w_rel, order)
    return grad_x.reshape(x.shape), G, small


BIG = (("w_in", "col"), ("mem_w_kv", "row"), ("w_br_attn", "col"), ("w_br_conv", "col"),
       ("w_br_mem", "col"), ("w_out", "row"))


def _coords():
    return lax.axis_index("x"), lax.axis_index("y"), lax.axis_index("c")


def _other_chips(x, y):
    return [(1 - x, y), (x, 1 - y), (1 - x, 1 - y)]


def _half(ref, kind, c):
    R, C = ref.shape
    if kind == "col":
        return ref.at[pl.ds(c * (R // 2), R // 2), :]
    return ref.at[:, pl.ds(c * (C // 2), C // 2)]


def _shard(ref, kind, s):
    R, C = ref.shape
    if kind == "col":
        return ref.at[:, pl.ds(s * (C // 4), C // 4)]
    return ref.at[pl.ds(s * (R // 4), R // 4), :]


def _piece(ref, kind, s, c):
    R, C = ref.shape
    if kind == "col":
        return ref.at[pl.ds(c * (R // 2), R // 2), pl.ds(s * (C // 4), C // 4)]
    return ref.at[pl.ds(s * (R // 4), R // 4), pl.ds(c * (C // 2), C // 2)]


def _remote(src, dst, sems_s, sems_r, k, dev):
    return pltpu.make_async_remote_copy(src_ref=src, dst_ref=dst, send_sem=sems_s.at[k], recv_sem=sems_r.at[k],
                                        device_id=dev, device_id_type=MESH)


HBM = pl.BlockSpec(memory_space=pltpu.HBM)
SEM = pl.BlockSpec(memory_space=pltpu.SEMAPHORE)
EFFECT = pltpu.SideEffectType.DATAFLOW_SIDE_EFFECTING


def _hbm(a):
    return pltpu.with_memory_space_constraint(a, pltpu.HBM)


def _start_copies(name, arrays, ncopies, make):
    n = len(arrays)

    def body(*refs):
        for cp in make(refs[:n], refs[n], refs[n + 1]):
            cp.start()

    outs = pl.pallas_call(
        body, name=name,
        out_shape=(pltpu.SemaphoreType.DMA((ncopies,)), pltpu.SemaphoreType.DMA((ncopies,)),
                   *[jax.ShapeDtypeStruct(t.shape, t.dtype) for t in arrays]),
        in_specs=[HBM] * n, out_specs=(SEM, SEM, *([HBM] * n)),
        input_output_aliases={i: i + 2 for i in range(n)},
        compiler_params=pltpu.CompilerParams(has_side_effects=EFFECT),
    )(*[_hbm(t) for t in arrays])
    return outs[0], outs[1], list(outs[2:])


def _wait_copies(name, send, recv, arrays, make, after):
    n = len(arrays)

    def body(*refs):
        for cp in make(refs[:n], refs[n], refs[n + 1]):
            cp.wait_send()
            cp.wait_recv()

    outs = pl.pallas_call(
        body, name=name, out_shape=[jax.ShapeDtypeStruct(t.shape, t.dtype) for t in arrays],
        in_specs=[HBM] * n + [SEM, SEM, ANY], out_specs=[HBM] * n,
        input_output_aliases={i: i for i in range(n)},
        compiler_params=pltpu.CompilerParams(has_side_effects=EFFECT),
    )(*arrays, send, recv, after)
    return list(outs)


def _w_in_copies(relations):
    def make(refs, send, recv):
        x, y, c = _coords()
        me = 2 * x + y
        chips = _other_chips(x, y)
        w, conv = refs[0], refs[1]
        cps = []
        for i, k in enumerate(relations):
            cps.append(_remote(_column_half(w, 0, c), _column_half(w, 1 + k, c), send, recv, 2 * i, (*chips[k], c)))
            mine = _shard(conv, "col", me)
            cps.append(_remote(mine, mine, send, recv, 2 * i + 1, (*chips[k], c)))
        return cps
    return make


def _column_half(w, slot, c):
    half = w.shape[1] // 8
    return w.at[:, pl.ds((2 * slot + c) * half, half)]


def _w_in_forward(relations):
    def make(refs, send, recv):
        x, y, c = _coords()
        cps = []
        for i, k in enumerate(relations):
            got = _column_half(refs[0], 1 + k, c)
            cps.append(_remote(got, got, send, recv, i, (x, y, 1 - c)))
        return cps
    return make


def _sibling_copy(refs, send, recv):
    x, y, c = _coords()
    return [_remote(refs[0], refs[1], send, recv, 0, (x, y, 1 - c))]


def _other_weight_copies(refs, send, recv):
    x, y, c = _coords()
    me = 2 * x + y
    cps = []
    for k, chip in enumerate(_other_chips(x, y)):
        for p, (_, kind) in enumerate(BIG[1:]):
            mine = _piece(refs[p], kind, me, c)
            cps.append(_remote(mine, mine, send, recv, 3 * p + k, (*chip, c)))
    return cps


def _other_weight_forward(refs, send, recv):
    x, y, c = _coords()
    cps = []
    for k, chip in enumerate(_other_chips(x, y)):
        s = 2 * chip[0] + chip[1]
        for p, (_, kind) in enumerate(BIG[1:]):
            got = _piece(refs[p], kind, s, c)
            cps.append(_remote(got, got, send, recv, 3 * p + k, (x, y, 1 - c)))
    return cps


def _share_copies(group):
    def make(refs, send, recv):
        x, y, c = _coords()
        cps = []
        for p, (_, kind) in enumerate(group):
            mine = _half(refs[p], kind, c)
            cps.append(_remote(mine, mine, send, recv, p, (x, y, 1 - c)))
        return cps
    return make


def _sibling_halves_start(G, group, carry, tag):
    n = len(group)
    parts = [G[name] for name, _ in group]
    lands = []
    for (_, kind), g in zip(group, parts):
        R, C = g.shape
        lands.append(lax.empty((R // 2, C) if kind == "col" else (R, C // 2), g.dtype))

    def make(refs, send, recv):
        x, y, c = _coords()
        return [_remote(_half(refs[p], group[p][1], 1 - c), refs[n + p], send, recv, p, (x, y, 1 - c)) for p in range(n)]

    send, recv, thru = _start_copies("sibling_halves_start_" + tag, [*parts, *lands, carry], n, make)
    return (send, recv, thru[:2 * n], make, tag), thru[2 * n]


def _presums(state, group, pos, after):
    send, recv, arrays, make, tag = state
    n = len(group)
    thru = _wait_copies("sibling_halves_wait_" + tag, send, recv, arrays, make, after)
    return [_presum(thru[p], thru[n + p], kind, pos, "presum_" + name) for p, (name, kind) in enumerate(group)]


def _presum(g, got, kind, pos, name):
    R, C = got.shape
    tr, tc = _tile(R, 512, 16), _tile(C, 2048)
    nr, nc = R // tr, C // tc

    def body(pos_ref, a_ref, b_ref, o_ref):
        o_ref[...] = (a_ref[...].astype(F32) + b_ref[...].astype(F32)).astype(o_ref.dtype)

    blk = pl.BlockSpec((tr, tc), lambda i, j, pos_ref: (i, j))
    if g.shape == got.shape:
        mine = blk
    elif kind == "col":
        mine = pl.BlockSpec((tr, tc), lambda i, j, pos_ref: (pos_ref[1] * nr + i, j))
    else:
        mine = pl.BlockSpec((tr, tc), lambda i, j, pos_ref: (i, pos_ref[1] * nc + j))
    spec = pltpu.PrefetchScalarGridSpec(num_scalar_prefetch=1, grid=(nr, nc), in_specs=[mine, blk], out_specs=blk)
    return _call(body, name=name, grid_spec=spec, out_shape=jax.ShapeDtypeStruct((R, C), WIRE_DTYPE))(pos, g, got)


def _chip_copies(group):
    n = len(group)

    def make(refs, send, recv):
        x, y, c = _coords()
        cps = []
        for k, chip in enumerate(_other_chips(x, y)):
            s = 2 * chip[0] + chip[1]
            for p in range(n):
                cps.append(_remote(_shard(refs[p], group[p][1], s), refs[n + p].at[k], send, recv, 3 * p + k, (*chip, c)))
        return cps
    return make


def _landing_zones(pres, group):
    lands = []
    for (_, kind), g in zip(group, pres):
        R, C = g.shape
        lands.append(lax.empty((3, R, C // 4) if kind == "col" else (3, R // 4, C), g.dtype))
    return lands


def _exchange_start(pres, group, carry, tag):
    n = len(group)
    make = _chip_copies(group)
    send, recv, thru = _start_copies("chip_exchange_start_" + tag, [*pres, *_landing_zones(pres, group), carry], 3 * n, make)
    return (send, recv, thru[:2 * n], make, tag), thru[2 * n]


def _exchange_wait(state, after):
    send, recv, arrays, make, tag = state
    thru = _wait_copies("chip_exchange_wait_" + tag, send, recv, arrays, make, after)
    n = len(thru) // 2
    return thru[:n], thru[n:]


def _reduce_into_shard(slots, pre, kind, pos, name):
    K, R, C = slots.shape
    tr, tc = _tile(R, 512, 16), _tile(C, 2176)
    nr, nc = R // tr, C // tc

    def body(pos_ref, s_ref, p_ref, o_ref):
        acc = p_ref[...].astype(F32)
        for k in range(K):
            acc = acc + s_ref[k].astype(F32)
        o_ref[...] = acc

    if kind == "col":
        own = pl.BlockSpec((tr, tc), lambda i, j, pos_ref: (i, pos_ref[0] * nc + j))
        full, out = (2 * R, C), pl.BlockSpec((tr, tc), lambda i, j, pos_ref: (pos_ref[1] * nr + i, j))
    else:
        own = pl.BlockSpec((tr, tc), lambda i, j, pos_ref: (pos_ref[0] * nr + i, j))
        full, out = (R, 2 * C), pl.BlockSpec((tr, tc), lambda i, j, pos_ref: (i, pos_ref[1] * nc + j))
    spec = pltpu.PrefetchScalarGridSpec(
        num_scalar_prefetch=1, grid=(nr, nc),
        in_specs=[pl.BlockSpec((K, tr, tc), lambda i, j, pos_ref: (0, i, j)), own], out_specs=out)
    return _call(body, name=name, grid_spec=spec, out_shape=jax.ShapeDtypeStruct(full, F32))(pos, slots, pre)


def _small_slots(pack, me):
    _, N = pack.shape

    def body(me_ref, p_ref, o_ref):
        o_ref[0] = p_ref[...]

    spec = pltpu.PrefetchScalarGridSpec(
        num_scalar_prefetch=1, grid=(1,), in_specs=[pl.BlockSpec((1, N), lambda i, me_ref: (0, 0))],
        out_specs=pl.BlockSpec((1, 1, N), lambda i, me_ref: (me_ref[0], 0, 0)))
    return _call(body, name="small_slots", grid_spec=spec, out_shape=jax.ShapeDtypeStruct((8, 1, N), pack.dtype))(me, pack)


def _small_copies(refs, send, recv):
    x, y, c = _coords()
    me = 4 * x + 2 * y + c
    cps = []
    for k in range(1, 8):
        dev = (x ^ (k >> 2), y ^ ((k >> 1) & 1), c ^ (k & 1))
        cps.append(_remote(refs[0], refs[1].at[me], send, recv, k - 1, dev))
    return cps


def _sum_small(slots):
    K, _, N = slots.shape

    def body(s_ref, o_ref):
        acc = s_ref[0]
        for k in range(1, K):
            acc = acc + s_ref[k]
        o_ref[...] = acc

    return _call(body, name="sum_small", in_specs=[pl.BlockSpec(memory_space=pltpu.VMEM)],
                 out_specs=pl.BlockSpec(memory_space=pltpu.VMEM), out_shape=jax.ShapeDtypeStruct((1, N), F32))(slots)


def _adamw(w, g, m, v, name, with_grad=False):
    R, C = w.shape
    tr, tc = _tile(R, 256, 8), _tile(C, 2176)

    def body(w_ref, g_ref, m_ref, v_ref, d_ref, nm_ref, nv_ref, *g_out):
        gv = g_ref[...]
        for ref in g_out:
            ref[...] = gv
        nm = ADAM_B1 * m_ref[...] + (1.0 - ADAM_B1) * gv
        nv = ADAM_B2 * v_ref[...] + (1.0 - ADAM_B2) * gv * gv
        m_hat = nm / (1.0 - ADAM_B1 ** ADAM_STEP)
        v_hat = nv / (1.0 - ADAM_B2 ** ADAM_STEP)
        d_ref[...] = -ADAM_LR * (m_hat / (jnp.sqrt(v_hat) + ADAM_EPS) + ADAM_WD * w_ref[...])
        nm_ref[...] = nm
        nv_ref[...] = nv

    spec = pl.BlockSpec((tr, tc), lambda i, j: (i, j))
    shp = jax.ShapeDtypeStruct((R, C), F32)
    nout = 4 if with_grad else 3
    return _call(body, name=name, grid=(R // tr, C // tc), in_specs=[spec] * 4, out_specs=[spec] * nout,
                 out_shape=[shp] * nout)(w, g, m, v)


SMALL = ("norm_g", "mem_norm_g", "attn_q_norm", "attn_k_norm", "conv_w", "mem_q_norm", "mem_k_norm")
WEIGHTS = ("norm_g", "mem_norm_g", "w_in", "attn_q_norm", "attn_k_norm", "conv_w", "mem_w_kv", "mem_q_norm",
           "mem_k_norm", "w_br_attn", "w_br_conv", "w_br_mem", "w_out")


def kernel(x, mem, norm_g, mem_norm_g, w_in, attn_q_norm, attn_k_norm, conv_w, mem_w_kv, mem_q_norm, mem_k_norm, w_br_attn, w_br_conv, w_br_mem, w_out, loss_target, m_norm_g, m_mem_norm_g, m_w_in, m_attn_q_norm, m_attn_k_norm, m_conv_w, m_mem_w_kv, m_mem_q_norm, m_mem_k_norm, m_w_br_attn, m_w_br_conv, m_w_br_mem, m_w_out, v_norm_g, v_mem_norm_g, v_w_in, v_attn_q_norm, v_attn_k_norm, v_conv_w, v_mem_w_kv, v_mem_q_norm, v_mem_k_norm, v_w_br_attn, v_w_br_conv, v_w_br_mem, v_w_out):
    w = dict(norm_g=norm_g, mem_norm_g=mem_norm_g, w_in=w_in, attn_q_norm=attn_q_norm, attn_k_norm=attn_k_norm,
             conv_w=conv_w, mem_w_kv=mem_w_kv, mem_q_norm=mem_q_norm, mem_k_norm=mem_k_norm, w_br_attn=w_br_attn,
             w_br_conv=w_br_conv, w_br_mem=w_br_mem, w_out=w_out)
    m = dict(norm_g=m_norm_g, mem_norm_g=m_mem_norm_g, w_in=m_w_in, attn_q_norm=m_attn_q_norm,
             attn_k_norm=m_attn_k_norm, conv_w=m_conv_w, mem_w_kv=m_mem_w_kv, mem_q_norm=m_mem_q_norm,
             mem_k_norm=m_mem_k_norm, w_br_attn=m_w_br_attn, w_br_conv=m_w_br_conv, w_br_mem=m_w_br_mem, w_out=m_w_out)
    v = dict(norm_g=v_norm_g, mem_norm_g=v_mem_norm_g, w_in=v_w_in, attn_q_norm=v_attn_q_norm,
             attn_k_norm=v_attn_k_norm, conv_w=v_conv_w, mem_w_kv=v_mem_w_kv, mem_q_norm=v_mem_q_norm,
             mem_k_norm=v_mem_k_norm, w_br_attn=v_w_br_attn, w_br_conv=v_w_br_conv, w_br_mem=v_w_br_mem, w_out=v_w_out)
    Bl, _, D = x.shape
    cx, cy = lax.axis_index("x"), lax.axis_index("y")
    chip = 2 * cx + cy
    pos = jnp.stack([chip, lax.axis_index("c")]).astype(jnp.int32)
    order = jnp.stack([chip] + [2 * a + b for a, b in _other_chips(cx, cy)]).astype(jnp.int32)
    n = len(BIG)

    slot0 = jnp.stack([jnp.zeros((), jnp.int32), pos[1]])
    w_rel = _place_shard(w["w_in"], "col", slot0, WIRE_DTYPE, "place_w_in_sent", half=0)
    conv_full = _place_shard(conv_w, "col", pos, F32, "place_conv_w")
    others = [_place_shard(w[name], kind, pos, WIRE_DTYPE, "place_" + name) for name, kind in BIG[1:]]
    hb, hbt, mhb = _norms(x, mem, norm_g, mem_norm_g)

    meta = jnp.concatenate([order, pos[1:]])
    near, near_fwd = _w_in_copies((0, 1)), _w_in_forward((0, 1))
    send, recv, (w_rel, conv_full) = _start_copies("gather_near_start", [w_rel, conv_full], 4, near)
    w_rel = _place_shard(w["w_in"], "col", slot0, WIRE_DTYPE, "place_w_in_kept", half=1, into=w_rel)
    proj = _proj_chunk(hb, w_rel, meta, 0, 1, None, None, "proj_own")
    w_rel, conv_full, *others = _wait_copies("gather_near_wait", send, recv, [w_rel, conv_full, *others], near, proj)

    fsend, frecv, (w_rel,) = _start_copies("gather_near_forward_start", [w_rel], 2, near_fwd)
    far, far_fwd = _w_in_copies((2,)), _w_in_forward((2,))
    send, recv, (w_rel, conv_full) = _start_copies("gather_far_start", [w_rel, conv_full], 2, far)
    proj = _proj_chunk(hb, w_rel, meta, 1, 2, 0, proj, "proj_near_landed")
    w_rel, = _wait_copies("gather_near_forward_wait", fsend, frecv, [w_rel], near_fwd, proj)
    proj = _proj_chunk(hb, w_rel, meta, 1, 2, 1, proj, "proj_near_forwarded")
    w_rel, conv_full = _wait_copies("gather_far_wait", send, recv, [w_rel, conv_full], far, proj)

    fsend, frecv, (w_rel,) = _start_copies("gather_far_forward_start", [w_rel], 1, far_fwd)
    send, recv, (*others, w_rel) = _start_copies("gather_rest_start", [*others, w_rel], 3 * (n - 1), _other_weight_copies)
    proj = _proj_chunk(hb, w_rel, meta, 3, 1, 0, proj, "proj_far_landed")
    w_rel, = _wait_copies("gather_far_forward_wait", fsend, frecv, [w_rel], far_fwd, proj)
    proj = _proj_chunk(hb, w_rel, meta, 3, 1, 1, proj, "proj_far_forwarded")
    os, ls, a = _attention_fwd(proj, Bl, attn_q_norm, attn_k_norm)
    *others, w_rel = _wait_copies("gather_rest_wait", send, recv, [*others, w_rel], _other_weight_copies, a[0])
    fsend, frecv, (*others, proj) = _start_copies("gather_rest_forward_start", [*others, proj], 3 * (n - 1),
                                                  _other_weight_forward)
    cc = _conv_branch_fwd(proj, Bl, conv_full)
    others = _wait_copies("gather_rest_forward_wait", fsend, frecv, others, _other_weight_forward, cc[0])
    W = {name: others[p] for p, (name, _) in enumerate(BIG[1:])}

    def rest_halves(G, carry):
        return _sibling_halves_start(G, BIG[1:], carry, "rest")

    def rest_exchange(state, after, carry):
        return _exchange_start(_presums(state, BIG[1:], pos, after), BIG[1:], carry, "rest")

    G, rest, rest_state = _weight_grads(
        x, mem, loss_target, norm_g, mem_norm_g, attn_q_norm, attn_k_norm, conv_full, mem_q_norm, mem_k_norm, W,
        (hb, hbt, mhb, proj, os, ls, a, cc), early=(rest_halves, rest_exchange))

    for_sibling = _dw_in_half(hbt, rest[0], pos, False, "dw_in_sibling")
    send, recv, (for_sibling, got, dproj) = _start_copies(
        "sibling_w_in_start", [for_sibling, lax.empty(for_sibling.shape, for_sibling.dtype), rest[0]], 1, _sibling_copy)
    mine = _dw_in_half(hbt, dproj, pos, True, "dw_in_own")
    for_sibling, got = _wait_copies("sibling_w_in_wait", send, recv, [for_sibling, got], _sibling_copy, mine)
    pre_w_in = _presum(mine, got, "col", pos, "presum_w_in")

    w_in_state, dproj = _exchange_start([pre_w_in], BIG[:1], dproj, "w_in")
    grad_x, small = _input_grad((dproj, *rest[1:]), w_rel, order)
    pres_rest, slots_rest = _exchange_wait(rest_state, grad_x)
    reds_rest = [_reduce_into_shard(slots_rest[p], pres_rest[p], kind, pos, "reduce_" + name)
                 for p, (name, kind) in enumerate(BIG[1:])]
    share_rest = _share_copies(BIG[1:])
    rsend, rrecv, reds_rest = _start_copies("share_rest_start", reds_rest, n - 1, share_rest)
    pres, slots = _exchange_wait(w_in_state, grad_x)
    red_w_in = _reduce_into_shard(slots[0], pres[0], "col", pos, "reduce_w_in")
    share_w_in = _share_copies(BIG[:1])
    wsend, wrecv, (red_w_in, small) = _start_copies("share_w_in_start", [red_w_in, small], 1, share_w_in)
    grad_x = grad_x.reshape(x.shape)

    slots = _small_slots(small, (2 * pos[:1] + pos[1:]))
    ssend, srecv, (small, slots) = _start_copies("gather_small_start", [small, slots], 7, _small_copies)
    reds_rest = _wait_copies("share_rest_wait", rsend, rrecv, reds_rest, share_rest, slots)
    grads = dict(zip([name for name, _ in BIG[1:]], reds_rest))
    delta, new_m, new_v = {}, {}, {}
    for name, _ in BIG[1:]:
        delta[name], new_m[name], new_v[name], grads[name] = _adamw(w[name], grads[name], m[name], v[name],
                                                                    "adamw_" + name, with_grad=True)

    small, slots = _wait_copies("gather_small_wait", ssend, srecv, [small, slots], _small_copies, delta[BIG[-1][0]])
    tot = _sum_small(slots)[0]
    loss = tot[0]
    off = 128
    for name, size in (("norm_g", D), ("mem_norm_g", D), ("attn_q_norm", NGROUP * HEAD), ("attn_k_norm", NGROUP * HEAD),
                       ("conv_w", 3 * CONVW), ("mem_q_norm", MEM_HD), ("mem_k_norm", MEM_HD)):
        grads[name] = tot[off:off + size]
        off += size
    cw = conv_w.shape[1]
    grads["conv_w"] = lax.dynamic_slice(grads["conv_w"].reshape(3, CONVW), (0, chip * cw), (3, cw))
    for name in SMALL:
        grads[name] = grads[name].reshape(w[name].shape)

    def packed(t):
        return jnp.concatenate([t[name].reshape(1, -1) for name in SMALL], axis=1)

    ds, ms, vs = _adamw(packed(w), packed(grads), packed(m), packed(v), "adamw_small")
    shared, = _wait_copies("share_w_in_wait", wsend, wrecv, [red_w_in], share_w_in, ds)
    delta["w_in"], new_m["w_in"], new_v["w_in"], grads["w_in"] = _adamw(w["w_in"], shared, m["w_in"], v["w_in"],
                                                                        "adamw_w_in", with_grad=True)
    off = 0
    for name in SMALL:
        size = w[name].size
        delta[name] = ds[0, off:off + size].reshape(w[name].shape)
        new_m[name] = ms[0, off:off + size].reshape(w[name].shape)
        new_v[name] = vs[0, off:off + size].reshape(w[name].shape)
        off += size

    return (loss, grad_x, *[grads[n] for n in WEIGHTS], *[delta[n] for n in WEIGHTS],
            *[new_m[n] for n in WEIGHTS], *[new_v[n] for n in WEIGHTS])
```

```python
import functools

import jax
import jax.numpy as jnp
from jax import lax
from jax.experimental import pallas as pl
from jax.experimental.pallas import tpu as pltpu

F32 = jnp.float32
MXU_DTYPE = jnp.bfloat16
WIRE_DTYPE = jnp.bfloat16
PROJ_DTYPE = jnp.bfloat16
EPS = 1e-6
NEG = -1e30

HEAD = 128
HPG = 4
GW = HPG * HEAD
DILATIONS = (1, 4, 16)
NGROUP = len(DILATIONS)
BLK = 128
QKV = NGROUP * GW
CONVW = 1024
MEM_HEADS = 4
MEM_HD = 256
MEMW = MEM_HEADS * MEM_HD
Q0, K0, V0 = 0, QKV, 2 * QKV
ZA = 3 * QKV
CB, CC, CV, ZC = ZA + GW, ZA + GW + CONVW, ZA + GW + 2 * CONVW, ZA + GW + 3 * CONVW
MQ = ZC + CONVW
ZM = MQ + MEMW
G0 = ZM + MEMW

ADAM_LR, ADAM_B1, ADAM_B2, ADAM_EPS, ADAM_WD, ADAM_STEP = 0.001, 0.9, 0.999, 1e-08, 0.01, 10

VMEM_LIMIT = 56 * 1024 * 1024
MESH = pl.DeviceIdType.MESH
ANY = pl.BlockSpec(memory_space=pl.ANY)


def _tile(n, pref, mult=128):
    t = min(pref, n)
    while t > mult and (n % t or t % mult):
        t -= mult
    assert n % t == 0, (n, pref)
    return t


def _call(body, *, name, out_shape, grid=(), in_specs=None, out_specs=None, scratch_shapes=(),
          aliases=None, grid_spec=None):
    kw = {}
    if grid_spec is not None:
        kw["grid_spec"] = grid_spec
        ngrid = len(grid_spec.grid)
    else:
        kw.update(grid=grid, in_specs=in_specs, out_specs=out_specs, scratch_shapes=list(scratch_shapes))
        ngrid = len(grid)
    params = pltpu.CompilerParams(dimension_semantics=("arbitrary",) * ngrid, vmem_limit_bytes=VMEM_LIMIT)
    return pl.pallas_call(body, name=name, out_shape=out_shape, compiler_params=params,
                          input_output_aliases=aliases or {}, **kw)


_DIMS = {"nn": (((1,), (0,)), ((), ())), "nt": (((1,), (1,)), ((), ())), "tn": (((0,), (0,)), ((), ()))}


def _mxu(a, b, mode):
    return lax.dot_general(a.astype(MXU_DTYPE), b.astype(MXU_DTYPE), _DIMS[mode], preferred_element_type=F32)


@functools.partial(jax.custom_vjp, nondiff_argnums=(2,))
def _dot(a, b, mode):
    return _mxu(a, b, mode)


def _dot_fwd(a, b, mode):
    return _mxu(a, b, mode), (a, b)


def _dot_bwd(mode, res, g):
    a, b = res
    if mode == "nn":
        return _mxu(g, b, "nt"), _mxu(a, g, "tn")
    if mode == "nt":
        return _mxu(g, b, "nn"), _mxu(g, a, "tn")
    return _mxu(b, g, "nt"), _mxu(a, g, "nn")


_dot.defvjp(_dot_fwd, _dot_bwd)


def _sig(z):
    return 1.0 / (1.0 + jnp.exp(-z))


def _silu(z):
    return z * _sig(z)


def _rms_rows(t, g):
    return t * lax.rsqrt(jnp.mean(t * t, axis=-1, keepdims=True) + EPS) * g


def _attn_block(q, k2, v2, gq, gk, first):
    qn = _rms_rows(q, gq)
    kn = _rms_rows(k2, gk)
    s = jnp.where(_band_mask(first, k2.shape[0]), _mxu(qn, kn, "nt") * (HEAD ** -0.5), NEG)
    m = jnp.max(s, axis=-1, keepdims=True)
    p = jnp.exp(s - m)
    den = jnp.sum(p, axis=-1, keepdims=True)
    o = _mxu(p, v2, "nn") / den
    return o, m + jnp.log(den)


def _band_mask(first, nkeys):
    a = lax.broadcasted_iota(jnp.int32, (BLK, nkeys), 0)
    b = lax.broadcasted_iota(jnp.int32, (BLK, nkeys), 1)
    if nkeys == BLK:
        return b <= a
    return (b >= a) & (b <= a + BLK) & (b >= jnp.where(first, BLK, 0))


def _norm_parts(t):
    r = lax.rsqrt(jnp.mean(t * t, axis=-1, keepdims=True) + EPS)
    return r, t * r


def _norm_bwd(dn, g, r, th):
    dth = dn * g
    return r * (dth - th * jnp.mean(dth * th, axis=-1, keepdims=True)), jnp.sum(dn * th, axis=0, keepdims=True)


def _attn_block_bwd(q, k2, v2, gq, gk, first, do, o, lse, dlse):
    scale = HEAD ** -0.5
    rq, qh = _norm_parts(q)
    rk, kh = _norm_parts(k2)
    qn, kn = qh * gq, kh * gk
    s = jnp.where(_band_mask(first, k2.shape[0]), _mxu(qn, kn, "nt") * scale, NEG)
    p = jnp.exp(s - lse)
    ds = p * (_mxu(do, v2, "nt") + (dlse - jnp.sum(do * o, axis=-1, keepdims=True))) * scale
    dq, dgq = _norm_bwd(_mxu(ds, kn, "nn"), gq, rq, qh)
    dk2, dgk = _norm_bwd(_mxu(ds, qn, "tn"), gk, rk, kh)
    return dq, dk2, _mxu(p, do, "tn"), dgq, dgk


def _combine(o1, o2, o3, l1, l2, l3, z):
    m = lax.stop_gradient(jnp.maximum(jnp.maximum(l1, l2), l3))
    e1, e2, e3 = jnp.exp(l1 - m), jnp.exp(l2 - m), jnp.exp(l3 - m)
    return (e1 * o1 + e2 * o2 + e3 * o3) / (e1 + e2 + e3) * _silu(z)


def _mem_block(q, z, kv, gq, gk):
    outs = []
    for h in range(MEM_HEADS):
        sl = slice(h * MEM_HD, (h + 1) * MEM_HD)
        qn = _rms_rows(q[:, sl], gq)
        kn = _rms_rows(kv[:, sl], gk)
        s = _dot(qn, kn, "nt") * (MEM_HD ** -0.5)
        m = lax.stop_gradient(jnp.max(s, axis=-1, keepdims=True))
        p = jnp.exp(s - m)
        den = jnp.sum(p, axis=-1, keepdims=True)
        outs.append(_dot(p, kv[:, MEMW + h * MEM_HD:MEMW + (h + 1) * MEM_HD], "nn") / den)
    return jnp.concatenate(outs, axis=-1) * _silu(z)


def _cast(w, name):
    R, C = w.shape
    tr, tc = _tile(R, 512, 8), _tile(C, 2176)

    def body(w_ref, o_ref):
        o_ref[...] = w_ref[...].astype(o_ref.dtype)

    spec = pl.BlockSpec((tr, tc), lambda i, j: (i, j))
    return _call(body, name=name, grid=(R // tr, C // tc), in_specs=[spec], out_specs=spec,
                 out_shape=jax.ShapeDtypeStruct((R, C), WIRE_DTYPE))(w)


def _place_shard(w, kind, pos, dtype, name, slot=0, into=None, half=None):
    R, C = w.shape
    tr, tc = _tile(R, 512, 8), _tile(C if half is None else C // 2, 2176)
    nr, nc = R // tr, C // tc
    ncols = nc if half is None else nc // 2

    def body(pos_ref, w_ref, *rest):
        rest[-1][...] = w_ref[...].astype(rest[-1].dtype)

    def col(j, pos_ref):
        if half is None:
            return j
        return (pos_ref[1] if half == 0 else 1 - pos_ref[1]) * ncols + j

    if kind == "col":
        full = (R, 4 * C)
        out = pl.BlockSpec((tr, tc), lambda i, j, pos_ref: (i, pos_ref[slot] * nc + col(j, pos_ref)))
    else:
        full, out = (4 * R, C), pl.BlockSpec((tr, tc), lambda i, j, pos_ref: (pos_ref[slot] * nr + i, j))
    in_specs, args = [pl.BlockSpec((tr, tc), lambda i, j, pos_ref: (i, col(j, pos_ref)))], [pos, w]
    if into is not None:
        in_specs.append(ANY)
        args.append(into)
    spec = pltpu.PrefetchScalarGridSpec(num_scalar_prefetch=1, grid=(nr, ncols), in_specs=in_specs, out_specs=out)
    return _call(body, name=name, grid_spec=spec, out_shape=jax.ShapeDtypeStruct(full, dtype),
                 aliases={} if into is None else {2: 0})(*args)


def _matmul(a, b, mode, out_dtype, *, name, tm=512, tn=512, tk=512):
    if mode == "nn":
        (M, K), (_, N) = a.shape, b.shape
    elif mode == "nt":
        (M, K), (N, _) = a.shape, b.shape
    else:
        (K, M), (_, N) = a.shape, b.shape
    tm, tn, tk = _tile(M, tm), _tile(N, tn), _tile(K, tk)
    nk = K // tk

    def body(a_ref, b_ref, o_ref, *acc):
        part = lax.dot_general(a_ref[...], b_ref[...], _DIMS[mode], preferred_element_type=F32)
        if nk == 1:
            o_ref[...] = part.astype(o_ref.dtype)
            return
        acc_ref, = acc
        k = pl.program_id(2)

        @pl.when(k == 0)
        def _():
            acc_ref[...] = part

        @pl.when(k > 0)
        def _():
            acc_ref[...] += part

        @pl.when(k == nk - 1)
        def _():
            o_ref[...] = acc_ref[...].astype(o_ref.dtype)

    a_spec = pl.BlockSpec((tk, tm), lambda i, j, k: (k, i)) if mode == "tn" else pl.BlockSpec((tm, tk), lambda i, j, k: (i, k))
    b_spec = pl.BlockSpec((tn, tk), lambda i, j, k: (j, k)) if mode == "nt" else pl.BlockSpec((tk, tn), lambda i, j, k: (k, j))
    return _call(body, name=name, grid=(M // tm, N // tn, nk), in_specs=[a_spec, b_spec],
                 out_specs=pl.BlockSpec((tm, tn), lambda i, j, k: (i, j)),
                 out_shape=jax.ShapeDtypeStruct((M, N), out_dtype),
                 scratch_shapes=[] if nk == 1 else [pltpu.VMEM((tm, tn), F32)])(a, b)


def _rms_fwd(x, g, name):
    R, D = x.shape
    tr = _tile(R, 512)

    def body(x_ref, g_ref, o_ref, t_ref):
        y = _rms_rows(x_ref[...], g_ref[...])
        o_ref[...] = y.astype(o_ref.dtype)
        t_ref[...] = y.T.astype(t_ref.dtype)

    row = pl.BlockSpec((tr, D), lambda i: (i, 0))
    return _call(body, name=name, grid=(R // tr,), in_specs=[row, pl.BlockSpec((1, D), lambda i: (0, 0))],
                 out_specs=[row, pl.BlockSpec((D, tr), lambda i: (0, i))],
                 out_shape=[jax.ShapeDtypeStruct((R, D), MXU_DTYPE), jax.ShapeDtypeStruct((D, R), MXU_DTYPE)])(x, g)


def _rms_bwd(x, dh, g, dy, name):
    R, D = x.shape
    tr = _tile(R, 256)
    with_dx = dy is not None

    def body(*refs):
        if with_dx:
            x_ref, dh_ref, g_ref, dy_ref, dx_ref, dg_ref = refs
        else:
            x_ref, dh_ref, g_ref, dg_ref = refs
        xv, dhv = x_ref[...], dh_ref[...]
        r = lax.rsqrt(jnp.mean(xv * xv, axis=-1, keepdims=True) + EPS)
        xh = xv * r

        @pl.when(pl.program_id(0) == 0)
        def _():
            dg_ref[...] = jnp.zeros_like(dg_ref)

        dg_ref[...] += jnp.sum(dhv * xh, axis=0, keepdims=True)
        if with_dx:
            dxh = dhv * g_ref[...]
            dx_ref[...] = dy_ref[...] + r * (dxh - xh * jnp.mean(dxh * xh, axis=-1, keepdims=True))

    row = pl.BlockSpec((tr, D), lambda i: (i, 0))
    vec = pl.BlockSpec((1, D), lambda i: (0, 0))
    dg_shape = jax.ShapeDtypeStruct((1, D), F32)
    if with_dx:
        return _call(body, name=name, grid=(R // tr,), in_specs=[row, row, vec, row], out_specs=[row, vec],
                     out_shape=[jax.ShapeDtypeStruct((R, D), F32), dg_shape])(x, dh, g, dy)
    return None, _call(body, name=name, grid=(R // tr,), in_specs=[row, row, vec], out_specs=vec,
                       out_shape=dg_shape)(x, dh, g)


def _attn_geom(g, d):
    hc = HPG if d == 1 else 1
    cw = hc * HEAD
    cq, ck, cv = (Q0 + g * GW) // cw, (K0 + g * GW) // cw, (V0 + g * GW) // cw
    return (1, BLK * d, cw), hc, HPG // hc, cq, ck, cv


def _rows(ref, r, d, sl):
    if d == 1:
        return ref[0, :, sl]
    return ref.at[0][pl.ds(r, BLK, stride=d), sl]


def _set_rows(ref, r, d, sl, val):
    if d == 1:
        ref[0, :, sl] = val
    else:
        ref.at[0][pl.ds(r, BLK, stride=d), sl] = val


def _stage_rows(ref, r, d, sl, val):
    if d == 1:
        ref[:, sl] = val
    else:
        ref[pl.ds(r, BLK, stride=d), sl] = val


def _proj_stages(blk, d):
    return [] if d == 1 else [pltpu.VMEM(blk[1:], F32)] * 5


def _proj_rows(refs, stages, d):
    if d == 1:
        return [lambda r, sl, ref=ref: ref[0, :, sl].astype(F32) for ref in refs]
    for ref, stage in zip(refs, stages):
        stage[...] = ref[0].astype(F32)
    return [lambda r, sl, stage=stage: stage[pl.ds(r, BLK, stride=d), sl] for stage in stages]


def _attn_fwd(proj3, gq, gk, g, d):
    Bl, S, _ = proj3.shape
    blk, hc, ncb, cq, ck, cv = _attn_geom(g, d)
    nb = S // blk[1]
    if nb == 1:
        return _attn_single_fwd(proj3, gq, gk, g, d)

    def body(q_ref, kp_ref, kc_ref, vp_ref, vc_ref, gq_ref, gk_ref, o_ref, lse_ref, *stages):
        first = pl.program_id(2) == 0
        q, kp, kc, vp, vc = _proj_rows((q_ref, kp_ref, kc_ref, vp_ref, vc_ref), stages, d)
        def run(alone):
            for r in range(d):
                for h in range(hc):
                    sl = slice(h * HEAD, (h + 1) * HEAD)
                    if alone:
                        k2, v2 = kc(r, sl), vc(r, sl)
                    else:
                        k2 = jnp.concatenate([kp(r, sl), kc(r, sl)], axis=0)
                        v2 = jnp.concatenate([vp(r, sl), vc(r, sl)], axis=0)
                    o, lse = _attn_block(q(r, sl), k2, v2, gq_ref[...], gk_ref[...], False)
                    _set_rows(o_ref, r, d, sl, o)
                    _set_rows(lse_ref, r, d, sl, jnp.broadcast_to(lse, (BLK, HEAD)))

        pl.when(first)(lambda: run(True))
        pl.when(jnp.logical_not(first))(lambda: run(False))

    def cur(c0):
        return pl.BlockSpec(blk, lambda b, j, i: (b, i, c0 + j))

    def prev(c0):
        return pl.BlockSpec(blk, lambda b, j, i: (b, jnp.maximum(i - 1, 0), c0 + j))

    vec = pl.BlockSpec((1, HEAD), lambda b, j, i: (0, 0))
    out = pl.BlockSpec(blk, lambda b, j, i: (b, i, j))
    shp = jax.ShapeDtypeStruct((Bl, S, GW), F32)
    return _call(body, name=f"attn_fwd_g{g}", grid=(Bl, ncb, nb),
                 in_specs=[cur(cq), prev(ck), cur(ck), prev(cv), cur(cv), vec, vec],
                 out_specs=[out, out], out_shape=[shp, shp], scratch_shapes=_proj_stages(blk, d),
                 )(proj3, proj3, proj3, proj3, proj3, gq, gk)


def _attn_single_fwd(proj3, gq, gk, g, d):
    Bl, S, _ = proj3.shape
    blk, hc, ncb, cq, ck, cv = _attn_geom(g, d)

    def body(q_ref, k_ref, v_ref, gq_ref, gk_ref, o_ref, lse_ref, *stages):
        q, k, v = _proj_rows((q_ref, k_ref, v_ref), stages, d)
        for r in range(d):
            for h in range(hc):
                sl = slice(h * HEAD, (h + 1) * HEAD)
                o, lse = _attn_block(q(r, sl), k(r, sl), v(r, sl), gq_ref[...], gk_ref[...], True)
                _set_rows(o_ref, r, d, sl, o)
                _set_rows(lse_ref, r, d, sl, jnp.broadcast_to(lse, (BLK, HEAD)))

    def at(c0):
        return pl.BlockSpec(blk, lambda b, j: (b, 0, c0 + j))

    vec = pl.BlockSpec((1, HEAD), lambda b, j: (0, 0))
    shp = jax.ShapeDtypeStruct((Bl, S, GW), F32)
    return _call(body, name=f"attn_fwd_g{g}", grid=(Bl, ncb), in_specs=[at(cq), at(ck), at(cv), vec, vec],
                 out_specs=[at(0), at(0)], out_shape=[shp, shp], scratch_shapes=_proj_stages(blk, d)[:3],
                 )(proj3, proj3, proj3, gq, gk)


def _attn_single_bwd(proj3, gq, gk, o3, l3, do3, dl3, g, d):
    Bl, S, _ = proj3.shape
    blk, hc, ncb, cq, ck, cv = _attn_geom(g, d)

    def body(q_ref, k_ref, v_ref, gq_ref, gk_ref, o_ref, l_ref, do_ref, dl_ref,
             dq_ref, dk_ref, dv_ref, dgq_ref, dgk_ref, sq_ref, sk_ref, sv_ref, *stages):
        @pl.when((pl.program_id(0) == 0) & (pl.program_id(1) == 0))
        def _():
            dgq_ref[...] = jnp.zeros_like(dgq_ref)
            dgk_ref[...] = jnp.zeros_like(dgk_ref)

        dgq, dgk = jnp.zeros((1, HEAD), F32), jnp.zeros((1, HEAD), F32)
        q, k, v = _proj_rows((q_ref, k_ref, v_ref), stages, d)
        for r in range(d):
            for h in range(hc):
                sl = slice(h * HEAD, (h + 1) * HEAD)
                dq, dk, dv, a, b = _attn_block_bwd(
                    q(r, sl), k(r, sl), v(r, sl), gq_ref[...], gk_ref[...], True, _rows(do_ref, r, d, sl),
                    _rows(o_ref, r, d, sl), _rows(l_ref, r, d, sl)[:, :1], _rows(dl_ref, r, d, sl)[:, :1])
                _stage_rows(sq_ref, r, d, sl, dq)
                _stage_rows(sk_ref, r, d, sl, dk)
                _stage_rows(sv_ref, r, d, sl, dv)
                dgq, dgk = dgq + a, dgk + b
        dgq_ref[...] += dgq
        dgk_ref[...] += dgk
        dq_ref[0] = sq_ref[...].astype(dq_ref.dtype)
        dk_ref[0] = sk_ref[...].astype(dk_ref.dtype)
        dv_ref[0] = sv_ref[...].astype(dv_ref.dtype)

    def at(c0):
        return pl.BlockSpec(blk, lambda b, j: (b, 0, c0 + j))

    vec = pl.BlockSpec((1, HEAD), lambda b, j: (0, 0))
    shp = jax.ShapeDtypeStruct((Bl, S, GW), MXU_DTYPE)
    gshp = jax.ShapeDtypeStruct((1, HEAD), F32)
    return _call(body, name=f"attn_bwd_g{g}", grid=(Bl, ncb),
                 in_specs=[at(cq), at(ck), at(cv), vec, vec, at(0), at(0), at(0), at(0)],
                 out_specs=[at(0), at(0), at(0), vec, vec], out_shape=[shp, shp, shp, gshp, gshp],
                 scratch_shapes=[pltpu.VMEM(blk[1:], F32)] * 3 + _proj_stages(blk, d)[:3],
                 )(proj3, proj3, proj3, gq, gk, o3, l3, do3, dl3)


def _attn_bwd(proj3, gq, gk, o3, l3, do3, dl3, g, d):
    Bl, S, _ = proj3.shape
    blk, hc, ncb, cq, ck, cv = _attn_geom(g, d)
    nb = S // blk[1]
    if nb == 1:
        return _attn_single_bwd(proj3, gq, gk, o3, l3, do3, dl3, g, d)

    def body(q_ref, kp_ref, kc_ref, vp_ref, vc_ref, gq_ref, gk_ref, o_ref, l_ref, do_ref, dl_ref,
             dq_ref, dk_ref, dv_ref, dgq_ref, dgk_ref, ck_ref, cv_ref, sq_ref, sk_ref, sv_ref, *stages):
        i = pl.program_id(2)
        first = i == 0

        @pl.when((pl.program_id(0) == 0) & (pl.program_id(1) == 0) & first)
        def _():
            dgq_ref[...] = jnp.zeros_like(dgq_ref)
            dgk_ref[...] = jnp.zeros_like(dgk_ref)

        def run(alone):
            dgq, dgk = jnp.zeros((1, HEAD), F32), jnp.zeros((1, HEAD), F32)
            q, kp, kc, vp, vc = _proj_rows((q_ref, kp_ref, kc_ref, vp_ref, vc_ref), stages, d)
            for r in range(d):
                rs = slice(r * BLK, (r + 1) * BLK)
                for h in range(hc):
                    sl = slice(h * HEAD, (h + 1) * HEAD)
                    if alone:
                        k2, v2 = kc(r, sl), vc(r, sl)
                    else:
                        k2 = jnp.concatenate([kp(r, sl), kc(r, sl)], axis=0)
                        v2 = jnp.concatenate([vp(r, sl), vc(r, sl)], axis=0)
                    dq, dk2, dv2, a, b = _attn_block_bwd(
                        q(r, sl), k2, v2, gq_ref[...], gk_ref[...], False, _rows(do_ref, r, d, sl),
                        _rows(o_ref, r, d, sl), _rows(l_ref, r, d, sl)[:, :1], _rows(dl_ref, r, d, sl)[:, :1])
                    _stage_rows(sq_ref, r, d, sl, dq)
                    if alone:
                        _stage_rows(sk_ref, r, d, sl, jnp.zeros((BLK, HEAD), F32))
                        _stage_rows(sv_ref, r, d, sl, jnp.zeros((BLK, HEAD), F32))
                    else:
                        _stage_rows(sk_ref, r, d, sl, ck_ref[rs, sl] + dk2[:BLK])
                        _stage_rows(sv_ref, r, d, sl, cv_ref[rs, sl] + dv2[:BLK])
                    ck_ref[rs, sl] = dk2[-BLK:]
                    cv_ref[rs, sl] = dv2[-BLK:]
                    dgq, dgk = dgq + a, dgk + b
            dgq_ref[...] += dgq
            dgk_ref[...] += dgk
            dq_ref[0] = sq_ref[...].astype(dq_ref.dtype)

        pl.when(first)(lambda: run(True))
        pl.when((i > 0) & (i < nb))(lambda: run(False))

        @pl.when(i == nb)
        def _():
            for r in range(d):
                rs = slice(r * BLK, (r + 1) * BLK)
                _stage_rows(sk_ref, r, d, slice(None), ck_ref[rs, :])
                _stage_rows(sv_ref, r, d, slice(None), cv_ref[rs, :])

        dk_ref[0] = sk_ref[...].astype(dk_ref.dtype)
        dv_ref[0] = sv_ref[...].astype(dv_ref.dtype)

    def cur(c0):
        return pl.BlockSpec(blk, lambda b, j, i: (b, jnp.minimum(i, nb - 1), c0 + j))

    def prev(c0):
        return pl.BlockSpec(blk, lambda b, j, i: (b, jnp.clip(i - 1, 0, nb - 1), c0 + j))

    vec = pl.BlockSpec((1, HEAD), lambda b, j, i: (0, 0))
    at_q = pl.BlockSpec(blk, lambda b, j, i: (b, jnp.minimum(i, nb - 1), j))
    at_k = pl.BlockSpec(blk, lambda b, j, i: (b, jnp.maximum(i - 1, 0), j))
    shp = jax.ShapeDtypeStruct((Bl, S, GW), MXU_DTYPE)
    gshp = jax.ShapeDtypeStruct((1, HEAD), F32)
    return _call(body, name=f"attn_bwd_g{g}", grid=(Bl, ncb, nb + 1),
                 in_specs=[cur(cq), prev(ck), cur(ck), prev(cv), cur(cv), vec, vec, at_q, at_q, at_q, at_q],
                 out_specs=[at_q, at_k, at_k, vec, vec], out_shape=[shp, shp, shp, gshp, gshp],
                 scratch_shapes=[pltpu.VMEM(blk[1:], F32)] * 5 + _proj_stages(blk, d),
                 )(proj3, proj3, proj3, proj3, proj3, gq, gk, o3, l3, do3, dl3)


def _combine_fwd(os, ls, proj2):
    T = proj2.shape[0]
    tr = _tile(T, 512)

    def body(o1, o2, o3, l1, l2, l3, z, a_ref, at_ref):
        a = _combine(o1[...], o2[...], o3[...], l1[...], l2[...], l3[...], z[...].astype(F32))
        a_ref[...] = a.astype(a_ref.dtype)
        at_ref[...] = a.T.astype(at_ref.dtype)

    row = pl.BlockSpec((tr, GW), lambda i: (i, 0))
    return _call(body, name="combine_fwd", grid=(T // tr,),
                 in_specs=[row] * 6 + [pl.BlockSpec((tr, GW), lambda i: (i, ZA // GW))],
                 out_specs=[row, pl.BlockSpec((GW, tr), lambda i: (0, i))],
                 out_shape=[jax.ShapeDtypeStruct((T, GW), MXU_DTYPE), jax.ShapeDtypeStruct((GW, T), MXU_DTYPE)],
                 )(*os, *ls, proj2)


def _combine_bwd(os, ls, proj2, da, dproj):
    T = proj2.shape[0]
    tr = _tile(T, 256)

    def body(o1, o2, o3, l1, l2, l3, z, da_ref, _, d1, d2, d3, e1, e2, e3, dz_ref):
        _, vjp = jax.vjp(_combine, o1[...], o2[...], o3[...], l1[...], l2[...], l3[...], z[...].astype(F32))
        go1, go2, go3, gl1, gl2, gl3, gz = vjp(da_ref[...])
        d1[...], d2[...], d3[...] = go1, go2, go3
        dz_ref[...] = gz.astype(dz_ref.dtype)
        for ref, gl in ((e1, gl1), (e2, gl2), (e3, gl3)):
            for h in range(HPG):
                sl = slice(h * HEAD, (h + 1) * HEAD)
                ref[:, sl] = jnp.broadcast_to(jnp.sum(gl[:, sl], axis=-1, keepdims=True), (tr, HEAD))

    row = pl.BlockSpec((tr, GW), lambda i: (i, 0))
    f = jax.ShapeDtypeStruct((T, GW), F32)
    z_attn = pl.BlockSpec((tr, GW), lambda i: (i, ZA // GW))
    outs = _call(body, name="combine_bwd", grid=(T // tr,), in_specs=[row] * 6 + [z_attn, row, ANY],
                 out_specs=[row] * 6 + [z_attn], out_shape=[f] * 6 + [jax.ShapeDtypeStruct(dproj.shape, dproj.dtype)],
                 aliases={8: 6})(*os, *ls, proj2, da, dproj)
    return outs[:3], outs[3:6], outs[6]


def _shift_down(u, j, t):
    return jnp.where(t >= j, pltpu.roll(u, j, 0), 0.0)


def _shift_up(u, j, t):
    n = u.shape[0]
    return jnp.where(t < n - j, pltpu.roll(u, n - j, 0), 0.0)


def _conv_specs(Bl, S, cw):
    def sec(c0):
        return pl.BlockSpec((1, S, cw), lambda j, b: (b, 0, c0 // cw + j))
    return [sec(CB), sec(CC), sec(CV), sec(ZC)], pl.BlockSpec((3, cw), lambda j, b: (0, j))


def _conv_fwd(proj3, conv_w):
    Bl, S, _ = proj3.shape
    cw = 256
    secs, wspec = _conv_specs(Bl, S, cw)

    def body(b_ref, c_ref, v_ref, z_ref, w_ref, o_ref, ot_ref):
        t = lax.broadcasted_iota(jnp.int32, (S, cw), 0)
        u = c_ref[0].astype(F32) * v_ref[0].astype(F32)
        y = w_ref[0:1, :] * u + w_ref[1:2, :] * _shift_down(u, 1, t) + w_ref[2:3, :] * _shift_down(u, 2, t)
        out = b_ref[0].astype(F32) * y * _silu(z_ref[0].astype(F32))
        o_ref[0] = out.astype(o_ref.dtype)
        ot_ref[...] = out.T.astype(ot_ref.dtype)

    return _call(body, name="conv_fwd", grid=(CONVW // cw, Bl), in_specs=secs + [wspec],
                 out_specs=[pl.BlockSpec((1, S, cw), lambda j, b: (b, 0, j)), pl.BlockSpec((cw, S), lambda j, b: (j, b))],
                 out_shape=[jax.ShapeDtypeStruct((Bl, S, CONVW), MXU_DTYPE),
                            jax.ShapeDtypeStruct((CONVW, Bl * S), MXU_DTYPE)])(proj3, proj3, proj3, proj3, conv_w)


def _conv_bwd(proj3, conv_w, dcc3, dproj):
    Bl, S, _ = proj3.shape
    cw = 256
    secs, wspec = _conv_specs(Bl, S, cw)

    def body(b_ref, c_ref, v_ref, z_ref, w_ref, d_ref, _, dproj_ref, dw_ref, stage, sems):
        t = lax.broadcasted_iota(jnp.int32, (S, cw), 0)
        bv, cv, vv, zv = (r[0].astype(F32) for r in (b_ref, c_ref, v_ref, z_ref))
        dv = d_ref[0]
        u = cv * vv
        u1, u2 = _shift_down(u, 1, t), _shift_down(u, 2, t)
        y = w_ref[0:1, :] * u + w_ref[1:2, :] * u1 + w_ref[2:3, :] * u2
        sg = _sig(zv)
        sz = zv * sg
        gy = dv * bv * sz
        du = w_ref[0:1, :] * gy + w_ref[1:2, :] * _shift_up(gy, 1, t) + w_ref[2:3, :] * _shift_up(gy, 2, t)
        j, b = pl.program_id(0), pl.program_id(1)
        tiles = [dv * y * sz, du * vv, du * cv, dv * bv * y * sg * (1.0 + zv * (1.0 - sg))]
        dsts = [dproj_ref.at[pl.ds(b * S, S), pl.ds(c0 + j * cw, cw)] for c0 in (CB, CC, CV, ZC)]
        _emit_tiles(j * Bl + b, (CONVW // cw) * Bl, tiles, dsts, stage, sems)

        @pl.when(pl.program_id(1) == 0)
        def _():
            dw_ref[...] = jnp.zeros_like(dw_ref)

        dw_ref[0:1, :] += jnp.sum(gy * u, axis=0, keepdims=True)
        dw_ref[1:2, :] += jnp.sum(gy * u1, axis=0, keepdims=True)
        dw_ref[2:3, :] += jnp.sum(gy * u2, axis=0, keepdims=True)

    blk = pl.BlockSpec((1, S, cw), lambda j, b: (b, 0, j))
    return _call(body, name="conv_bwd", grid=(CONVW // cw, Bl), in_specs=secs + [wspec, blk, ANY],
                 out_specs=[ANY, wspec],
                 out_shape=[jax.ShapeDtypeStruct(dproj.shape, dproj.dtype), jax.ShapeDtypeStruct((3, CONVW), F32)],
                 scratch_shapes=_emit_scratch(4, S, cw), aliases={6: 0})(proj3, proj3, proj3, proj3, conv_w, dcc3, dproj)


def _mem_specs(S, tq):
    q = pl.BlockSpec((1, tq, MEMW), lambda b, j: (b, j, MQ // MEMW))
    z = pl.BlockSpec((1, tq, MEMW), lambda b, j: (b, j, ZM // MEMW))
    kv = pl.BlockSpec((1, MEM_HD, 2 * MEMW), lambda b, j: (b, 0, 0))
    vec = pl.BlockSpec((1, MEM_HD), lambda b, j: (0, 0))
    blk = pl.BlockSpec((1, tq, MEMW), lambda b, j: (b, j, 0))
    return q, z, kv, vec, blk


def _mem_fwd(proj3, mkv3, gq, gk):
    Bl, S, _ = proj3.shape
    tq = _tile(S, 512)
    q, z, kv, vec, blk = _mem_specs(S, tq)

    def body(q_ref, z_ref, kv_ref, gq_ref, gk_ref, o_ref, ot_ref):
        out = _mem_block(q_ref[0].astype(F32), z_ref[0].astype(F32), kv_ref[0], gq_ref[...], gk_ref[...])
        o_ref[0] = out.astype(o_ref.dtype)
        ot_ref[...] = out.T.astype(ot_ref.dtype)

    nq = S // tq
    return _call(body, name="mem_fwd", grid=(Bl, nq), in_specs=[q, z, kv, vec, vec],
                 out_specs=[blk, pl.BlockSpec((MEMW, tq), lambda b, j: (0, b * nq + j))],
                 out_shape=[jax.ShapeDtypeStruct((Bl, S, MEMW), MXU_DTYPE),
                            jax.ShapeDtypeStruct((MEMW, Bl * S), MXU_DTYPE)])(proj3, proj3, mkv3, gq, gk)


def _mem_bwd(proj3, mkv3, gq, gk, dmo3, dproj):
    Bl, S, _ = proj3.shape
    tq = _tile(S, 256)
    q, z, kv, vec, blk = _mem_specs(S, tq)
    nq = S // tq

    def body(q_ref, z_ref, kv_ref, gq_ref, gk_ref, d_ref, _, dproj_ref, dkv_ref, dgq_ref, dgk_ref, stage, sems):
        _, vjp = jax.vjp(_mem_block, q_ref[0].astype(F32), z_ref[0].astype(F32), kv_ref[0], gq_ref[...], gk_ref[...])
        dq, dz, dkv, dgq, dgk = vjp(d_ref[0])
        j = pl.program_id(1)
        rows = pl.ds(pl.program_id(0) * S + j * tq, tq)
        dsts = [dproj_ref.at[rows, pl.ds(MQ, MEMW)], dproj_ref.at[rows, pl.ds(ZM, MEMW)]]
        _emit_tiles(pl.program_id(0) * nq + j, Bl * nq, [dq, dz], dsts, stage, sems)

        @pl.when(j == 0)
        def _():
            dkv_ref[0] = jnp.zeros_like(dkv)

        @pl.when((j == 0) & (pl.program_id(0) == 0))
        def _():
            dgq_ref[...] = jnp.zeros_like(dgq_ref)
            dgk_ref[...] = jnp.zeros_like(dgk_ref)

        dkv_ref[0] += dkv
        dgq_ref[...] += dgq
        dgk_ref[...] += dgk

    gshp = jax.ShapeDtypeStruct((1, MEM_HD), F32)
    return _call(body, name="mem_bwd", grid=(Bl, nq), in_specs=[q, z, kv, vec, vec, blk, ANY],
                 out_specs=[ANY, kv, vec, vec],
                 out_shape=[jax.ShapeDtypeStruct(dproj.shape, dproj.dtype), jax.ShapeDtypeStruct(mkv3.shape, F32),
                            gshp, gshp],
                 scratch_shapes=_emit_scratch(2, tq, MEMW), aliases={6: 0})(proj3, proj3, mkv3, gq, gk, dmo3, dproj)


def _merge_specs(T, D, tm, tn):
    def act(w):
        return pl.BlockSpec((tm, w), lambda i, n: (i, 0))

    def wsp(w):
        return pl.BlockSpec((w, tn), lambda i, n: (0, n))

    gates = [pl.BlockSpec((tm, tn), lambda i, n, k=k: (i, (G0 + k * D) // tn + n)) for k in range(3)]
    tile = pl.BlockSpec((tm, tn), lambda i, n: (i, n))
    return act, wsp, gates, tile


def _merge_fwd(a, cc, mo, wa, wc, wm, proj2):
    T, D = a.shape[0], wa.shape[1]
    tm, tn = _tile(T, 1024), _tile(D, 512)
    act, wsp, gates, tile = _merge_specs(T, D, tm, tn)

    def body(a_ref, c_ref, m_ref, wa_ref, wc_ref, wm_ref, g0, g1, g2, mg_ref, mt_ref, pa_ref, pc_ref, pm_ref):
        pa = jnp.dot(a_ref[...], wa_ref[...], preferred_element_type=F32)
        pc = jnp.dot(c_ref[...], wc_ref[...], preferred_element_type=F32)
        pm = jnp.dot(m_ref[...], wm_ref[...], preferred_element_type=F32)
        mg = _sig(g0[...].astype(F32)) * pa + _sig(g1[...].astype(F32)) * pc + _sig(g2[...].astype(F32)) * pm
        mg_ref[...] = mg.astype(mg_ref.dtype)
        mt_ref[...] = mg.T.astype(mt_ref.dtype)
        pa_ref[...] = pa.astype(pa_ref.dtype)
        pc_ref[...] = pc.astype(pc_ref.dtype)
        pm_ref[...] = pm.astype(pm_ref.dtype)

    shp = jax.ShapeDtypeStruct((T, D), MXU_DTYPE)
    return _call(body, name="merge_fwd", grid=(T // tm, D // tn),
                 in_specs=[act(GW), act(CONVW), act(MEMW), wsp(GW), wsp(CONVW), wsp(MEMW)] + gates,
                 out_specs=[tile, pl.BlockSpec((tn, tm), lambda i, n: (n, i)), tile, tile, tile],
                 out_shape=[shp, jax.ShapeDtypeStruct((D, T), MXU_DTYPE), shp, shp, shp],
                 )(a, cc, mo, wa, wc, wm, proj2, proj2, proj2)


def _emit_tiles(step, nsteps, tiles, dsts, stage, sems):
    slot = step % 2

    def copies(s):
        return [pltpu.make_async_copy(stage.at[s, k], dsts[k], sems.at[s, k]) for k in range(len(tiles))]

    @pl.when(step >= 2)
    def _():
        for cp in copies(slot):
            cp.wait()

    for k, t in enumerate(tiles):
        stage[slot, k] = t.astype(stage.dtype)
    for cp in copies(slot):
        cp.start()

    @pl.when(step == nsteps - 1)
    def _():
        for cp in copies(slot):
            cp.wait()
        if nsteps > 1:
            for cp in copies(1 - slot):
                cp.wait()


def _emit_scratch(k, rows, cols):
    return [pltpu.VMEM((2, k, rows, cols), MXU_DTYPE), pltpu.SemaphoreType.DMA((2, k))]


def _merge_bwd(dyb, w_out, proj2, pa, pc, pm):
    T, D = dyb.shape
    IN = proj2.shape[1]
    tm, tn = _tile(T, 1024), _tile(D, 512)
    _, _, gates, tile = _merge_specs(T, D, tm, tn)
    nn = D // tn

    def body(dy_ref, w_ref, g0, g1, g2, p0, p1, p2, dp0, dp1, dp2, dproj_ref, stage, sems):
        i, n = pl.program_id(0), pl.program_id(1)
        dm = lax.dot_general(dy_ref[...], w_ref[...], _DIMS["nt"], preferred_element_type=F32)
        tiles, dsts = [], []
        for k, (g_ref, p_ref, dp_ref) in enumerate(((g0, p0, dp0), (g1, p1, dp1), (g2, p2, dp2))):
            gt = _sig(g_ref[...].astype(F32))
            dp_ref[...] = (gt * dm).astype(dp_ref.dtype)
            tiles.append(dm * p_ref[...].astype(F32) * gt * (1.0 - gt))
            dsts.append(dproj_ref.at[pl.ds(i * tm, tm), pl.ds(G0 + k * D + n * tn, tn)])
        _emit_tiles(i * nn + n, (T // tm) * nn, tiles, dsts, stage, sems)

    shp = jax.ShapeDtypeStruct((T, D), MXU_DTYPE)
    return _call(body, name="merge_bwd", grid=(T // tm, nn),
                 in_specs=[pl.BlockSpec((tm, D), lambda i, n: (i, 0)), pl.BlockSpec((tn, D), lambda i, n: (n, 0))]
                 + gates + [tile] * 3,
                 out_specs=[tile] * 3 + [ANY], out_shape=[shp] * 3 + [jax.ShapeDtypeStruct((T, IN), MXU_DTYPE)],
                 scratch_shapes=_emit_scratch(3, tm, tn))(dyb, w_out, proj2, proj2, proj2, pa, pc, pm)


def _out_loss(merged, w_out, x, tgt):
    T, D = x.shape
    tm = _tile(T, 512)

    def body(m_ref, w_ref, x_ref, t_ref, dy_ref, dyb_ref, loss_ref):
        err = x_ref[...] + jnp.dot(m_ref[...], w_ref[...], preferred_element_type=F32) - t_ref[...]
        dy = err * (1.0 / D)
        dy_ref[...] = dy
        dyb_ref[...] = dy.astype(dyb_ref.dtype)

        @pl.when(pl.program_id(0) == 0)
        def _():
            loss_ref[...] = jnp.zeros_like(loss_ref)

        loss_ref[...] += jnp.sum(err * err) * (0.5 / D)

    row = pl.BlockSpec((tm, D), lambda i: (i, 0))
    return _call(body, name="out_loss", grid=(T // tm,),
                 in_specs=[row, pl.BlockSpec((D, D), lambda i: (0, 0)), row, row],
                 out_specs=[row, row, pl.BlockSpec((1, 128), lambda i: (0, 0))],
                 out_shape=[jax.ShapeDtypeStruct((T, D), F32), jax.ShapeDtypeStruct((T, D), MXU_DTYPE),
                            jax.ShapeDtypeStruct((1, 128), F32)])(merged, w_out, x, tgt)


def _proj_chunk(hb, w, meta, j, nslots, half, buf, name):
    T, D = hb.shape
    Cs = w.shape[1] // 4
    tm, tn = _tile(T, 1024), _tile(Cs // 2, 2176)
    nh = Cs // 2 // tn
    per = nh if half is not None else 2 * nh

    def body(meta_ref, a_ref, b_ref, *rest):
        rest[-1][...] = jnp.dot(a_ref[...], b_ref[...], preferred_element_type=F32).astype(rest[-1].dtype)

    def tile(n, m):
        if half is None:
            return n % per
        return (m[4] if half == 0 else 1 - m[4]) * nh + n % per

    in_specs = [pl.BlockSpec((tm, D), lambda n, i, m: (i, 0)),
                pl.BlockSpec((D, tn), lambda n, i, m: (0, (j + n // per) * 2 * nh + tile(n, m)))]
    args = [meta, hb, w]
    if buf is not None:
        in_specs.append(ANY)
        args.append(buf)
    spec = pltpu.PrefetchScalarGridSpec(
        num_scalar_prefetch=1, grid=(nslots * per, T // tm), in_specs=in_specs,
        out_specs=pl.BlockSpec((tm, tn), lambda n, i, m: (i, m[j + n // per] * 2 * nh + tile(n, m))))
    return _call(body, name=name, grid_spec=spec, out_shape=jax.ShapeDtypeStruct((T, 4 * Cs), PROJ_DTYPE),
                 aliases={} if buf is None else {3: 0})(*args)


def _norms(x, mem, norm_g, mem_norm_g):
    D = x.shape[-1]
    hb, hbt = _rms_fwd(x.reshape(-1, D), norm_g.reshape(1, D), "rms_x")
    mhb, _ = _rms_fwd(mem.reshape(-1, D), mem_norm_g.reshape(1, D), "rms_mem")
    return hb, hbt, mhb


def _attention_fwd(proj2, Bl, gq_all, gk_all):
    T, IN = proj2.shape
    proj3 = proj2.reshape(Bl, T // Bl, IN)
    os, ls = [], []
    for g, d in enumerate(DILATIONS):
        o, l = _attn_fwd(proj3, gq_all[g:g + 1], gk_all[g:g + 1], g, d)
        os.append(o.reshape(T, GW))
        ls.append(l.reshape(T, GW))
    return os, ls, _combine_fwd(os, ls, proj2)


def _conv_branch_fwd(proj2, Bl, conv_w):
    T, IN = proj2.shape
    cc, cct = _conv_fwd(proj2.reshape(Bl, T // Bl, IN), conv_w)
    return cc.reshape(T, CONVW), cct


def _weight_grads(x, mem, tgt, norm_g, mem_norm_g, gq_all, gk_all, conv_w, mem_gq, mem_gk, W, pre, early=None):
    Bl, S, D = x.shape
    T = Bl * S
    hb, hbt, mhb, proj2, os, ls, (a, at), (cc, cct) = pre
    IN = proj2.shape[1]
    proj3 = proj2.reshape(Bl, S, IN)
    x2, tgt2 = x.reshape(T, D), tgt.reshape(T, D)
    mem2 = mem.reshape(-1, D)
    ng, mng = norm_g.reshape(1, D), mem_norm_g.reshape(1, D)
    mgq, mgk = mem_gq.reshape(1, MEM_HD), mem_gk.reshape(1, MEM_HD)
    gqs = [gq_all[g:g + 1] for g in range(NGROUP)]
    gks = [gk_all[g:g + 1] for g in range(NGROUP)]

    mkv = _matmul(mhb, W["mem_w_kv"], "nn", F32, name="mem_kv", tm=512, tn=1024, tk=D)
    mkv3 = mkv.reshape(Bl, -1, 2 * MEMW)
    mo, mot = _mem_fwd(proj3, mkv3, mgq, mgk)
    mo = mo.reshape(T, MEMW)
    merged, mergedt, pa, pc, pm = _merge_fwd(a, cc, mo, W["w_br_attn"], W["w_br_conv"], W["w_br_mem"], proj2)
    dy, dyb, loss = _out_loss(merged, W["w_out"], x2, tgt2)

    G = {}
    G["w_out"] = _matmul(mergedt, dyb, "nn", WIRE_DTYPE, name="dw_out", tm=1024, tn=512, tk=T)
    dpa, dpc, dpm, dproj = _merge_bwd(dyb, W["w_out"], proj2, pa, pc, pm)
    G["w_br_attn"] = _matmul(at, dpa, "nn", WIRE_DTYPE, name="dw_br_attn", tm=512, tn=512, tk=T)
    G["w_br_conv"] = _matmul(cct, dpc, "nn", WIRE_DTYPE, name="dw_br_conv", tm=1024, tn=512, tk=T)
    G["w_br_mem"] = _matmul(mot, dpm, "nn", WIRE_DTYPE, name="dw_br_mem", tm=1024, tn=512, tk=T)
    da = _matmul(dpa, W["w_br_attn"], "nt", F32, name="d_attn", tm=1024, tn=512, tk=D)
    dcc = _matmul(dpc, W["w_br_conv"], "nt", F32, name="d_conv", tm=1024, tn=1024, tk=D)
    dmo = _matmul(dpm, W["w_br_mem"], "nt", F32, name="d_mem", tm=1024, tn=1024, tk=D)
    dproj, dmkv3, dmgq, dmgk = _mem_bwd(proj3, mkv3, mgq, mgk, dmo.reshape(Bl, S, MEMW), dproj)
    dmkv = _cast(dmkv3.reshape(-1, 2 * MEMW), "cast_dmkv")
    G["mem_w_kv"] = _matmul(mhb, dmkv, "tn", WIRE_DTYPE, name="dw_mem_kv", tm=1024, tn=1024, tk=512)
    early_state, dmkv = (None, dmkv) if early is None else early[0](G, dmkv)
    dmh = _matmul(dmkv, W["mem_w_kv"], "nt", F32, name="d_memh", tm=512, tn=1024, tk=2 * MEMW)
    _, dmng = _rms_bwd(mem2, dmh, mng, None, "rms_mem_bwd")
    if early is not None:
        early_state, da = early[1](early_state, dmng, da)

    dos, dls, dproj = _combine_bwd(os, ls, proj2, da, dproj)
    dgq, dgk = [], []
    for g, d in enumerate(DILATIONS):
        dq, dk, dv, gq_g, gk_g = _attn_bwd(proj3, gqs[g], gks[g], os[g].reshape(Bl, S, GW), ls[g].reshape(Bl, S, GW),
                                           dos[g].reshape(Bl, S, GW), dls[g].reshape(Bl, S, GW), g, d)
        for c0, part in ((Q0, dq), (K0, dk), (V0, dv)):
            dproj = lax.dynamic_update_slice(dproj, part.reshape(T, GW), (0, c0 + g * GW))
        dgq.append(gq_g)
        dgk.append(gk_g)
    dproj, dconv_w = _conv_bwd(proj3, conv_w, dcc.reshape(Bl, S, CONVW), dproj)
    small = [loss, None, dmng] + dgq + dgk + [dconv_w.reshape(1, 3 * CONVW), dmgq, dmgk]
    return G, (dproj, x2, ng, dy, small), early_state


def _dw_in_half(hbt, dproj, pos, own, name, add=None):
    D, T = hbt.shape
    IN = dproj.shape[1]
    R, tn = D // 2, _tile(IN, 1024)

    def body(pos_ref, a_ref, b_ref, *rest):
        acc = jnp.dot(a_ref[...], b_ref[...], preferred_element_type=F32)
        if add is not None:
            acc = acc + rest[0][...].astype(F32)
        rest[-1][...] = acc.astype(rest[-1].dtype)

    tile = pl.BlockSpec((R, tn), lambda j, p: (0, j))
    spec = pltpu.PrefetchScalarGridSpec(
        num_scalar_prefetch=1, grid=(IN // tn,),
        in_specs=[pl.BlockSpec((R, T), lambda j, p: (p[1] if own else 1 - p[1], 0)),
                  pl.BlockSpec((T, tn), lambda j, p: (0, j))] + ([] if add is None else [tile]),
        out_specs=tile)
    return _call(body, name=name, grid_spec=spec, out_shape=jax.ShapeDtypeStruct((R, IN), WIRE_DTYPE),
                 )(pos, hbt, dproj, *([] if add is None else [add]))


def _d_h(dproj, w, order):
    T, IN = dproj.shape
    D, Cs = w.shape[0], IN // 4
    tm, tn = _tile(T, 1024), _tile(D, 1024)

    def body(order_ref, a_ref, b_ref, o_ref, acc_ref):
        part = lax.dot_general(a_ref[...], b_ref[...], _DIMS["nt"], preferred_element_type=F32)
        k = pl.program_id(2)

        @pl.when(k == 0)
        def _():
            acc_ref[...] = part

        @pl.when(k > 0)
        def _():
            acc_ref[...] += part

        @pl.when(k == 3)
        def _():
            o_ref[...] = acc_ref[...]

    spec = pltpu.PrefetchScalarGridSpec(
        num_scalar_prefetch=1, grid=(T // tm, D // tn, 4),
        in_specs=[pl.BlockSpec((tm, Cs), lambda i, n, k, o: (i, o[k])), pl.BlockSpec((tn, Cs), lambda i, n, k, o: (n, k))],
        out_specs=pl.BlockSpec((tm, tn), lambda i, n, k, o: (i, n)), scratch_shapes=[pltpu.VMEM((tm, tn), F32)])
    return _call(body, name="d_h", grid_spec=spec, out_shape=jax.ShapeDtypeStruct((T, D), F32))(order, dproj, w)


def _input_grad(rest, w_in, order):
    dproj, x2, ng, dy, small = rest
    dh = _d_h(dproj, w_in, order)
    grad_x, dng = _rms_bwd(x2, dh, ng, dy, "rms_x_bwd")
    small = [dng if t is None else t for t in small]
    return grad_x, jnp.concatenate(small, axis=1)


def _local_step(x, mem, tgt, norm_g, mem_norm_g, gq_all, gk_all, conv_w, mem_gq, mem_gk, W):
    hb, hbt, mhb = _norms(x, mem, norm_g, mem_norm_g)
    Cs = W["w_in"].shape[1] // 4
    shards = (0, 2, 1, 3)
    order = jnp.array(shards, dtype=jnp.int32)
    w_rel = jnp.concatenate([W["w_in"][:, s * Cs:(s + 1) * Cs] for s in shards], axis=1)
    meta = jnp.array(shards + (0,), dtype=jnp.int32)
    proj2 = _proj_chunk(hb, w_rel, meta, 0, 1, None, None, "proj_0")
    for j, nslots in ((1, 2), (3, 1)):
        for half in (1, 0):
            proj2 = _proj_chunk(hb, w_rel, meta, j, nslots, half, proj2, f"proj_{j}_{half}")
    pre = (hb, hbt, mhb, proj2, *_attention_fwd(proj2, x.shape[0], gq_all, gk_all),
           _conv_branch_fwd(proj2, x.shape[0], conv_w))
    G, rest, _ = _weight_grads(x, mem, tgt, norm_g, mem_norm_g, gq_all, gk_all, conv_w, mem_gq, mem_gk, W, pre)
    pos = jnp.zeros((2,), jnp.int32)
    G["w_in"] = jnp.concatenate([_dw_in_half(hbt, rest[0], pos, True, "dw_in_own"),
                                 _dw_in_half(hbt, rest[0], pos, False, "dw_in_sibling")], axis=0)
    grad_x, small = _input_grad(rest, w_rel, order)
    return grad_x.reshape(x.shape), G, small


BIG = (("w_in", "col"), ("mem_w_kv", "row"), ("w_br_attn", "col"), ("w_br_conv", "col"),
       ("w_br_mem", "col"), ("w_out", "row"))


def _coords():
    return lax.axis_index("x"), lax.axis_index("y"), lax.axis_index("c")


def _other_chips(x, y):
    return [(1 - x, y), (x, 1 - y), (1 - x, 1 - y)]


def _half(ref, kind, c):
    R, C = ref.shape
    if kind == "col":
        return ref.at[pl.ds(c * (R // 2), R // 2), :]
    return ref.at[:, pl.ds(c * (C // 2), C // 2)]


def _shard(ref, kind, s):
    R, C = ref.shape
    if kind == "col":
        return ref.at[:, pl.ds(s * (C // 4), C // 4)]
    return ref.at[pl.ds(s * (R // 4), R // 4), :]


def _piece(ref, kind, s, c):
    R, C = ref.shape
    if kind == "col":
        return ref.at[pl.ds(c * (R // 2), R // 2), pl.ds(s * (C // 4), C // 4)]
    return ref.at[pl.ds(s * (R // 4), R // 4), pl.ds(c * (C // 2), C // 2)]


def _remote(src, dst, sems_s, sems_r, k, dev):
    return pltpu.make_async_remote_copy(src_ref=src, dst_ref=dst, send_sem=sems_s.at[k], recv_sem=sems_r.at[k],
                                        device_id=dev, device_id_type=MESH)


HBM = pl.BlockSpec(memory_space=pltpu.HBM)
SEM = pl.BlockSpec(memory_space=pltpu.SEMAPHORE)
EFFECT = pltpu.SideEffectType.DATAFLOW_SIDE_EFFECTING


def _hbm(a):
    return pltpu.with_memory_space_constraint(a, pltpu.HBM)


def _start_copies(name, arrays, ncopies, make):
    n = len(arrays)

    def body(*refs):
        for cp in make(refs[:n], refs[n], refs[n + 1]):
            cp.start()

    outs = pl.pallas_call(
        body, name=name,
        out_shape=(pltpu.SemaphoreType.DMA((ncopies,)), pltpu.SemaphoreType.DMA((ncopies,)),
                   *[jax.ShapeDtypeStruct(t.shape, t.dtype) for t in arrays]),
        in_specs=[HBM] * n, out_specs=(SEM, SEM, *([HBM] * n)),
        input_output_aliases={i: i + 2 for i in range(n)},
        compiler_params=pltpu.CompilerParams(has_side_effects=EFFECT),
    )(*[_hbm(t) for t in arrays])
    return outs[0], outs[1], list(outs[2:])


def _wait_copies(name, send, recv, arrays, make, after):
    n = len(arrays)

    def body(*refs):
        for cp in make(refs[:n], refs[n], refs[n + 1]):
            cp.wait_send()
            cp.wait_recv()

    outs = pl.pallas_call(
        body, name=name, out_shape=[jax.ShapeDtypeStruct(t.shape, t.dtype) for t in arrays],
        in_specs=[HBM] * n + [SEM, SEM, ANY], out_specs=[HBM] * n,
        input_output_aliases={i: i for i in range(n)},
        compiler_params=pltpu.CompilerParams(has_side_effects=EFFECT),
    )(*arrays, send, recv, after)
    return list(outs)


def _w_in_copies(relations):
    def make(refs, send, recv):
        x, y, c = _coords()
        me = 2 * x + y
        chips = _other_chips(x, y)
        w, conv = refs[0], refs[1]
        cps = []
        for i, k in enumerate(relations):
            cps.append(_remote(_column_half(w, 0, c), _column_half(w, 1 + k, c), send, recv, 2 * i, (*chips[k], c)))
            mine = _shard(conv, "col", me)
            cps.append(_remote(mine, mine, send, recv, 2 * i + 1, (*chips[k], c)))
        return cps
    return make


def _column_half(w, slot, c):
    half = w.shape[1] // 8
    return w.at[:, pl.ds((2 * slot + c) * half, half)]


def _w_in_forward(relations):
    def make(refs, send, recv):
        x, y, c = _coords()
        cps = []
        for i, k in enumerate(relations):
            got = _column_half(refs[0], 1 + k, c)
            cps.append(_remote(got, got, send, recv, i, (x, y, 1 - c)))
        return cps
    return make


def _sibling_copy(refs, send, recv):
    x, y, c = _coords()
    return [_remote(refs[0], refs[1], send, recv, 0, (x, y, 1 - c))]


def _other_weight_copies(refs, send, recv):
    x, y, c = _coords()
    me = 2 * x + y
    cps = []
    for k, chip in enumerate(_other_chips(x, y)):
        for p, (_, kind) in enumerate(BIG[1:]):
            mine = _piece(refs[p], kind, me, c)
            cps.append(_remote(mine, mine, send, recv, 3 * p + k, (*chip, c)))
    return cps


def _other_weight_forward(refs, send, recv):
    x, y, c = _coords()
    cps = []
    for k, chip in enumerate(_other_chips(x, y)):
        s = 2 * chip[0] + chip[1]
        for p, (_, kind) in enumerate(BIG[1:]):
            got = _piece(refs[p], kind, s, c)
            cps.append(_remote(got, got, send, recv, 3 * p + k, (x, y, 1 - c)))
    return cps


def _share_copies(group):
    def make(refs, send, recv):
        x, y, c = _coords()
        cps = []
        for p, (_, kind) in enumerate(group):
            mine = _half(refs[p], kind, c)
            cps.append(_remote(mine, mine, send, recv, p, (x, y, 1 - c)))
        return cps
    return make


def _sibling_halves_start(G, group, carry, tag):
    n = len(group)
    parts = [G[name] for name, _ in group]
    lands = []
    for (_, kind), g in zip(group, parts):
        R, C = g.shape
        lands.append(lax.empty((R // 2, C) if kind == "col" else (R, C // 2), g.dtype))

    def make(refs, send, recv):
        x, y, c = _coords()
        return [_remote(_half(refs[p], group[p][1], 1 - c), refs[n + p], send, recv, p, (x, y, 1 - c)) for p in range(n)]

    send, recv, thru = _start_copies("sibling_halves_start_" + tag, [*parts, *lands, carry], n, make)
    return (send, recv, thru[:2 * n], make, tag), thru[2 * n]


def _presums(state, group, pos, after):
    send, recv, arrays, make, tag = state
    n = len(group)
    thru = _wait_copies("sibling_halves_wait_" + tag, send, recv, arrays, make, after)
    return [_presum(thru[p], thru[n + p], kind, pos, "presum_" + name) for p, (name, kind) in enumerate(group)]


def _presum(g, got, kind, pos, name):
    R, C = got.shape
    tr, tc = _tile(R, 512, 16), _tile(C, 2048)
    nr, nc = R // tr, C // tc

    def body(pos_ref, a_ref, b_ref, o_ref):
        o_ref[...] = (a_ref[...].astype(F32) + b_ref[...].astype(F32)).astype(o_ref.dtype)

    blk = pl.BlockSpec((tr, tc), lambda i, j, pos_ref: (i, j))
    if kind == "col":
        mine = pl.BlockSpec((tr, tc), lambda i, j, pos_ref: (pos_ref[1] * nr + i, j))
    else:
        mine = pl.BlockSpec((tr, tc), lambda i, j, pos_ref: (i, pos_ref[1] * nc + j))
    spec = pltpu.PrefetchScalarGridSpec(num_scalar_prefetch=1, grid=(nr, nc), in_specs=[mine, blk], out_specs=blk)
    return _call(body, name=name, grid_spec=spec, out_shape=jax.ShapeDtypeStruct((R, C), WIRE_DTYPE))(pos, g, got)


def _chip_copies(group):
    n = len(group)

    def make(refs, send, recv):
        x, y, c = _coords()
        cps = []
        for k, chip in enumerate(_other_chips(x, y)):
            s = 2 * chip[0] + chip[1]
            for p in range(n):
                cps.append(_remote(_shard(refs[p], group[p][1], s), refs[n + p].at[k], send, recv, 3 * p + k, (*chip, c)))
        return cps
    return make


def _landing_zones(pres, group):
    lands = []
    for (_, kind), g in zip(group, pres):
        R, C = g.shape
        lands.append(lax.empty((3, R, C // 4) if kind == "col" else (3, R // 4, C), g.dtype))
    return lands


def _exchange_start(pres, group, carry, tag):
    n = len(group)
    make = _chip_copies(group)
    send, recv, thru = _start_copies("chip_exchange_start_" + tag, [*pres, *_landing_zones(pres, group), carry], 3 * n, make)
    return (send, recv, thru[:2 * n], make, tag), thru[2 * n]


def _exchange_wait(state, after):
    send, recv, arrays, make, tag = state
    thru = _wait_copies("chip_exchange_wait_" + tag, send, recv, arrays, make, after)
    n = len(thru) // 2
    return thru[:n], thru[n:]


def _reduce_into_shard(slots, pre, kind, pos, name):
    K, R, C = slots.shape
    tr, tc = _tile(R, 512, 16), _tile(C, 2176)
    nr, nc = R // tr, C // tc

    def body(pos_ref, s_ref, p_ref, o_ref):
        acc = p_ref[...].astype(F32)
        for k in range(K):
            acc = acc + s_ref[k].astype(F32)
        o_ref[...] = acc

    if kind == "col":
        own = pl.BlockSpec((tr, tc), lambda i, j, pos_ref: (i, pos_ref[0] * nc + j))
        full, out = (2 * R, C), pl.BlockSpec((tr, tc), lambda i, j, pos_ref: (pos_ref[1] * nr + i, j))
    else:
        own = pl.BlockSpec((tr, tc), lambda i, j, pos_ref: (pos_ref[0] * nr + i, j))
        full, out = (R, 2 * C), pl.BlockSpec((tr, tc), lambda i, j, pos_ref: (i, pos_ref[1] * nc + j))
    spec = pltpu.PrefetchScalarGridSpec(
        num_scalar_prefetch=1, grid=(nr, nc),
        in_specs=[pl.BlockSpec((K, tr, tc), lambda i, j, pos_ref: (0, i, j)), own], out_specs=out)
    return _call(body, name=name, grid_spec=spec, out_shape=jax.ShapeDtypeStruct(full, F32))(pos, slots, pre)


def _small_slots(pack, me):
    _, N = pack.shape

    def body(me_ref, p_ref, o_ref):
        o_ref[0] = p_ref[...]

    spec = pltpu.PrefetchScalarGridSpec(
        num_scalar_prefetch=1, grid=(1,), in_specs=[pl.BlockSpec((1, N), lambda i, me_ref: (0, 0))],
        out_specs=pl.BlockSpec((1, 1, N), lambda i, me_ref: (me_ref[0], 0, 0)))
    return _call(body, name="small_slots", grid_spec=spec, out_shape=jax.ShapeDtypeStruct((8, 1, N), pack.dtype))(me, pack)


def _small_copies(refs, send, recv):
    x, y, c = _coords()
    me = 4 * x + 2 * y + c
    cps = []
    for k in range(1, 8):
        dev = (x ^ (k >> 2), y ^ ((k >> 1) & 1), c ^ (k & 1))
        cps.append(_remote(refs[0], refs[1].at[me], send, recv, k - 1, dev))
    return cps


def _sum_small(slots):
    K, _, N = slots.shape

    def body(s_ref, o_ref):
        acc = s_ref[0]
        for k in range(1, K):
            acc = acc + s_ref[k]
        o_ref[...] = acc

    return _call(body, name="sum_small", in_specs=[pl.BlockSpec(memory_space=pltpu.VMEM)],
                 out_specs=pl.BlockSpec(memory_space=pltpu.VMEM), out_shape=jax.ShapeDtypeStruct((1, N), F32))(slots)


def _adamw(w, g, m, v, name, with_grad=False):
    R, C = w.shape
    tr, tc = _tile(R, 256, 8), _tile(C, 2176)

    def body(w_ref, g_ref, m_ref, v_ref, d_ref, nm_ref, nv_ref, *g_out):
        gv = g_ref[...]
        for ref in g_out:
            ref[...] = gv
        nm = ADAM_B1 * m_ref[...] + (1.0 - ADAM_B1) * gv
        nv = ADAM_B2 * v_ref[...] + (1.0 - ADAM_B2) * gv * gv
        m_hat = nm / (1.0 - ADAM_B1 ** ADAM_STEP)
        v_hat = nv / (1.0 - ADAM_B2 ** ADAM_STEP)
        d_ref[...] = -ADAM_LR * (m_hat / (jnp.sqrt(v_hat) + ADAM_EPS) + ADAM_WD * w_ref[...])
        nm_ref[...] = nm
        nv_ref[...] = nv

    spec = pl.BlockSpec((tr, tc), lambda i, j: (i, j))
    shp = jax.ShapeDtypeStruct((R, C), F32)
    nout = 4 if with_grad else 3
    return _call(body, name=name, grid=(R // tr, C // tc), in_specs=[spec] * 4, out_specs=[spec] * nout,
                 out_shape=[shp] * nout)(w, g, m, v)


SMALL = ("norm_g", "mem_norm_g", "attn_q_norm", "attn_k_norm", "conv_w", "mem_q_norm", "mem_k_norm")
WEIGHTS = ("norm_g", "mem_norm_g", "w_in", "attn_q_norm", "attn_k_norm", "conv_w", "mem_w_kv", "mem_q_norm",
           "mem_k_norm", "w_br_attn", "w_br_conv", "w_br_mem", "w_out")


def kernel(x, mem, norm_g, mem_norm_g, w_in, attn_q_norm, attn_k_norm, conv_w, mem_w_kv, mem_q_norm, mem_k_norm, w_br_attn, w_br_conv, w_br_mem, w_out, loss_target, m_norm_g, m_mem_norm_g, m_w_in, m_attn_q_norm, m_attn_k_norm, m_conv_w, m_mem_w_kv, m_mem_q_norm, m_mem_k_norm, m_w_br_attn, m_w_br_conv, m_w_br_mem, m_w_out, v_norm_g, v_mem_norm_g, v_w_in, v_attn_q_norm, v_attn_k_norm, v_conv_w, v_mem_w_kv, v_mem_q_norm, v_mem_k_norm, v_w_br_attn, v_w_br_conv, v_w_br_mem, v_w_out):
    w = dict(norm_g=norm_g, mem_norm_g=mem_norm_g, w_in=w_in, attn_q_norm=attn_q_norm, attn_k_norm=attn_k_norm,
             conv_w=conv_w, mem_w_kv=mem_w_kv, mem_q_norm=mem_q_norm, mem_k_norm=mem_k_norm, w_br_attn=w_br_attn,
             w_br_conv=w_br_conv, w_br_mem=w_br_mem, w_out=w_out)
    m = dict(norm_g=m_norm_g, mem_norm_g=m_mem_norm_g, w_in=m_w_in, attn_q_norm=m_attn_q_norm,
             attn_k_norm=m_attn_k_norm, conv_w=m_conv_w, mem_w_kv=m_mem_w_kv, mem_q_norm=m_mem_q_norm,
             mem_k_norm=m_mem_k_norm, w_br_attn=m_w_br_attn, w_br_conv=m_w_br_conv, w_br_mem=m_w_br_mem, w_out=m_w_out)
    v = dict(norm_g=v_norm_g, mem_norm_g=v_mem_norm_g, w_in=v_w_in, attn_q_norm=v_attn_q_norm,
             attn_k_norm=v_attn_k_norm, conv_w=v_conv_w, mem_w_kv=v_mem_w_kv, mem_q_norm=v_mem_q_norm,
             mem_k_norm=v_mem_k_norm, w_br_attn=v_w_br_attn, w_br_conv=v_w_br_conv, w_br_mem=v_w_br_mem, w_out=v_w_out)
    Bl, _, D = x.shape
    cx, cy = lax.axis_index("x"), lax.axis_index("y")
    chip = 2 * cx + cy
    pos = jnp.stack([chip, lax.axis_index("c")]).astype(jnp.int32)
    order = jnp.stack([chip] + [2 * a + b for a, b in _other_chips(cx, cy)]).astype(jnp.int32)
    n = len(BIG)

    slot0 = jnp.stack([jnp.zeros((), jnp.int32), pos[1]])
    w_rel = _place_shard(w["w_in"], "col", slot0, WIRE_DTYPE, "place_w_in_sent", half=0)
    conv_full = _place_shard(conv_w, "col", pos, F32, "place_conv_w")
    others = [_place_shard(w[name], kind, pos, WIRE_DTYPE, "place_" + name) for name, kind in BIG[1:]]
    hb, hbt, mhb = _norms(x, mem, norm_g, mem_norm_g)

    meta = jnp.concatenate([order, pos[1:]])
    near, near_fwd = _w_in_copies((0, 1)), _w_in_forward((0, 1))
    send, recv, (w_rel, conv_full) = _start_copies("gather_near_start", [w_rel, conv_full], 4, near)
    w_rel = _place_shard(w["w_in"], "col", slot0, WIRE_DTYPE, "place_w_in_kept", half=1, into=w_rel)
    proj = _proj_chunk(hb, w_rel, meta, 0, 1, None, None, "proj_own")
    w_rel, conv_full, *others = _wait_copies("gather_near_wait", send, recv, [w_rel, conv_full, *others], near, proj)

    fsend, frecv, (w_rel,) = _start_copies("gather_near_forward_start", [w_rel], 2, near_fwd)
    far, far_fwd = _w_in_copies((2,)), _w_in_forward((2,))
    send, recv, (w_rel, conv_full) = _start_copies("gather_far_start", [w_rel, conv_full], 2, far)
    proj = _proj_chunk(hb, w_rel, meta, 1, 2, 0, proj, "proj_near_landed")
    w_rel, = _wait_copies("gather_near_forward_wait", fsend, frecv, [w_rel], near_fwd, proj)
    proj = _proj_chunk(hb, w_rel, meta, 1, 2, 1, proj, "proj_near_forwarded")
    w_rel, conv_full = _wait_copies("gather_far_wait", send, recv, [w_rel, conv_full], far, proj)

    fsend, frecv, (w_rel,) = _start_copies("gather_far_forward_start", [w_rel], 1, far_fwd)
    send, recv, (*others, w_rel) = _start_copies("gather_rest_start", [*others, w_rel], 3 * (n - 1), _other_weight_copies)
    proj = _proj_chunk(hb, w_rel, meta, 3, 1, 0, proj, "proj_far_landed")
    w_rel, = _wait_copies("gather_far_forward_wait", fsend, frecv, [w_rel], far_fwd, proj)
    proj = _proj_chunk(hb, w_rel, meta, 3, 1, 1, proj, "proj_far_forwarded")
    os, ls, a = _attention_fwd(proj, Bl, attn_q_norm, attn_k_norm)
    *others, w_rel = _wait_copies("gather_rest_wait", send, recv, [*others, w_rel], _other_weight_copies, a[0])
    fsend, frecv, (*others, proj) = _start_copies("gather_rest_forward_start", [*others, proj], 3 * (n - 1),
                                                  _other_weight_forward)
    cc = _conv_branch_fwd(proj, Bl, conv_full)
    others = _wait_copies("gather_rest_forward_wait", fsend, frecv, others, _other_weight_forward, cc[0])
    W = {name: others[p] for p, (name, _) in enumerate(BIG[1:])}

    def rest_halves(G, carry):
        return _sibling_halves_start(G, BIG[1:], carry, "rest")

    def rest_exchange(state, after, carry):
        return _exchange_start(_presums(state, BIG[1:], pos, after), BIG[1:], carry, "rest")

    G, rest, rest_state = _weight_grads(
        x, mem, loss_target, norm_g, mem_norm_g, attn_q_norm, attn_k_norm, conv_full, mem_q_norm, mem_k_norm, W,
        (hb, hbt, mhb, proj, os, ls, a, cc), early=(rest_halves, rest_exchange))

    for_sibling = _dw_in_half(hbt, rest[0], pos, False, "dw_in_sibling")
    send, recv, (for_sibling, got, dproj) = _start_copies(
        "sibling_w_in_start", [for_sibling, lax.empty(for_sibling.shape, for_sibling.dtype), rest[0]], 1, _sibling_copy)
    pres_rest, slots_rest = _exchange_wait(rest_state, dproj)
    reds_rest = [_reduce_into_shard(slots_rest[p], pres_rest[p], kind, pos, "reduce_" + name)
                 for p, (name, kind) in enumerate(BIG[1:])]
    share_rest = _share_copies(BIG[1:])
    rsend, rrecv, reds_rest = _start_copies("share_rest_start", reds_rest, n - 1, share_rest)
    for_sibling, got = _wait_copies("sibling_w_in_wait", send, recv, [for_sibling, got], _sibling_copy, reds_rest[0])
    pre_w_in = _dw_in_half(hbt, dproj, pos, True, "dw_in_own", add=got)

    w_in_state, dproj = _exchange_start([pre_w_in], BIG[:1], dproj, "w_in")
    grad_x, small = _input_grad((dproj, *rest[1:]), w_rel, order)
    pres, slots = _exchange_wait(w_in_state, grad_x)
    red_w_in = _reduce_into_shard(slots[0], pres[0], "col", pos, "reduce_w_in")
    share_w_in = _share_copies(BIG[:1])
    wsend, wrecv, (red_w_in, small) = _start_copies("share_w_in_start", [red_w_in, small], 1, share_w_in)
    grad_x = grad_x.reshape(x.shape)

    slots = _small_slots(small, (2 * pos[:1] + pos[1:]))
    ssend, srecv, (small, slots) = _start_copies("gather_small_start", [small, slots], 7, _small_copies)
    reds_rest = _wait_copies("share_rest_wait", rsend, rrecv, reds_rest, share_rest, slots)
    grads = dict(zip([name for name, _ in BIG[1:]], reds_rest))
    delta, new_m, new_v = {}, {}, {}
    for name, _ in BIG[1:]:
        delta[name], new_m[name], new_v[name], grads[name] = _adamw(w[name], grads[name], m[name], v[name],
                                                                    "adamw_" + name, with_grad=True)

    small, slots = _wait_copies("gather_small_wait", ssend, srecv, [small, slots], _small_copies, delta[BIG[-1][0]])
    tot = _sum_small(slots)[0]
    loss = tot[0]
    off = 128
    for name, size in (("norm_g", D), ("mem_norm_g", D), ("attn_q_norm", NGROUP * HEAD), ("attn_k_norm", NGROUP * HEAD),
                       ("conv_w", 3 * CONVW), ("mem_q_norm", MEM_HD), ("mem_k_norm", MEM_HD)):
        grads[name] = tot[off:off + size]
        off += size
    cw = conv_w.shape[1]
    grads["conv_w"] = lax.dynamic_slice(grads["conv_w"].reshape(3, CONVW), (0, chip * cw), (3, cw))
    for name in SMALL:
        grads[name] = grads[name].reshape(w[name].shape)

    def packed(t):
        return jnp.concatenate([t[name].reshape(1, -1) for name in SMALL], axis=1)

    ds, ms, vs = _adamw(packed(w), packed(grads), packed(m), packed(v), "adamw_small")
    shared, = _wait_copies("share_w_in_wait", wsend, wrecv, [red_w_in], share_w_in, ds)
    delta["w_in"], new_m["w_in"], new_v["w_in"], grads["w_in"] = _adamw(w["w_in"], shared, m["w_in"], v["w_in"],
                                                                        "adamw_w_in", with_grad=True)
    off = 0
    for name in SMALL:
        size = w[name].size
        delta[name] = ds[0, off:off + size].reshape(w[name].shape)
        new_m[name] = ms[0, off:off + size].reshape(w[name].shape)
        new_v[name] = vs[0, off:off + size].reshape(w[name].shape)
        off += size

    return (loss, grad_x, *[grads[n] for n in WEIGHTS], *[delta[n] for n in WEIGHTS],
            *[new_m[n] for n in WEIGHTS], *[new_v[n] for n in WEIGHTS])
```

```python
import functools

import jax
import jax.numpy as jnp
from jax import lax
from jax.experimental import pallas as pl
from jax.experimental.pallas import tpu as pltpu

F32 = jnp.float32
MXU_DTYPE = jnp.bfloat16
WIRE_DTYPE = jnp.bfloat16
PROJ_DTYPE = jnp.bfloat16
EPS = 1e-6
NEG = -1e30

HEAD = 128
HPG = 4
GW = HPG * HEAD
DILATIONS = (1, 4, 16)
NGROUP = len(DILATIONS)
BLK = 128
QKV = NGROUP * GW
CONVW = 1024
MEM_HEADS = 4
MEM_HD = 256
MEMW = MEM_HEADS * MEM_HD
Q0, K0, V0 = 0, QKV, 2 * QKV
ZA = 3 * QKV
CB, CC, CV, ZC = ZA + GW, ZA + GW + CONVW, ZA + GW + 2 * CONVW, ZA + GW + 3 * CONVW
MQ = ZC + CONVW
ZM = MQ + MEMW
G0 = ZM + MEMW

ADAM_LR, ADAM_B1, ADAM_B2, ADAM_EPS, ADAM_WD, ADAM_STEP = 0.001, 0.9, 0.999, 1e-08, 0.01, 10

VMEM_LIMIT = 56 * 1024 * 1024
MESH = pl.DeviceIdType.MESH
ANY = pl.BlockSpec(memory_space=pl.ANY)


def _tile(n, pref, mult=128):
    t = min(pref, n)
    while t > mult and (n % t or t % mult):
        t -= mult
    assert n % t == 0, (n, pref)
    return t


def _call(body, *, name, out_shape, grid=(), in_specs=None, out_specs=None, scratch_shapes=(),
          aliases=None, grid_spec=None):
    kw = {}
    if grid_spec is not None:
        kw["grid_spec"] = grid_spec
        ngrid = len(grid_spec.grid)
    else:
        kw.update(grid=grid, in_specs=in_specs, out_specs=out_specs, scratch_shapes=list(scratch_shapes))
        ngrid = len(grid)
    params = pltpu.CompilerParams(dimension_semantics=("arbitrary",) * ngrid, vmem_limit_bytes=VMEM_LIMIT)
    return pl.pallas_call(body, name=name, out_shape=out_shape, compiler_params=params,
                          input_output_aliases=aliases or {}, **kw)


_DIMS = {"nn": (((1,), (0,)), ((), ())), "nt": (((1,), (1,)), ((), ())), "tn": (((0,), (0,)), ((), ()))}


def _mxu(a, b, mode):
    return lax.dot_general(a.astype(MXU_DTYPE), b.astype(MXU_DTYPE), _DIMS[mode], preferred_element_type=F32)


@functools.partial(jax.custom_vjp, nondiff_argnums=(2,))
def _dot(a, b, mode):
    return _mxu(a, b, mode)


def _dot_fwd(a, b, mode):
    return _mxu(a, b, mode), (a, b)


def _dot_bwd(mode, res, g):
    a, b = res
    if mode == "nn":
        return _mxu(g, b, "nt"), _mxu(a, g, "tn")
    if mode == "nt":
        return _mxu(g, b, "nn"), _mxu(g, a, "tn")
    return _mxu(b, g, "nt"), _mxu(a, g, "nn")


_dot.defvjp(_dot_fwd, _dot_bwd)


def _sig(z):
    return 1.0 / (1.0 + jnp.exp(-z))


def _silu(z):
    return z * _sig(z)


def _rms_rows(t, g):
    return t * lax.rsqrt(jnp.mean(t * t, axis=-1, keepdims=True) + EPS) * g


def _attn_block(q, k2, v2, gq, gk, first):
    qn = _rms_rows(q, gq)
    kn = _rms_rows(k2, gk)
    s = jnp.where(_band_mask(first, k2.shape[0]), _mxu(qn, kn, "nt") * (HEAD ** -0.5), NEG)
    m = jnp.max(s, axis=-1, keepdims=True)
    p = jnp.exp(s - m)
    den = jnp.sum(p, axis=-1, keepdims=True)
    o = _mxu(p, v2, "nn") / den
    return o, m + jnp.log(den)


def _band_mask(first, nkeys):
    a = lax.broadcasted_iota(jnp.int32, (BLK, nkeys), 0)
    b = lax.broadcasted_iota(jnp.int32, (BLK, nkeys), 1)
    if nkeys == BLK:
        return b <= a
    return (b >= a) & (b <= a + BLK) & (b >= jnp.where(first, BLK, 0))


def _norm_parts(t):
    r = lax.rsqrt(jnp.mean(t * t, axis=-1, keepdims=True) + EPS)
    return r, t * r


def _norm_bwd(dn, g, r, th):
    dth = dn * g
    return r * (dth - th * jnp.mean(dth * th, axis=-1, keepdims=True)), jnp.sum(dn * th, axis=0, keepdims=True)


def _attn_block_bwd(q, k2, v2, gq, gk, first, do, o, lse, dlse):
    scale = HEAD ** -0.5
    rq, qh = _norm_parts(q)
    rk, kh = _norm_parts(k2)
    qn, kn = qh * gq, kh * gk
    s = jnp.where(_band_mask(first, k2.shape[0]), _mxu(qn, kn, "nt") * scale, NEG)
    p = jnp.exp(s - lse)
    ds = p * (_mxu(do, v2, "nt") + (dlse - jnp.sum(do * o, axis=-1, keepdims=True))) * scale
    dq, dgq = _norm_bwd(_mxu(ds, kn, "nn"), gq, rq, qh)
    dk2, dgk = _norm_bwd(_mxu(ds, qn, "tn"), gk, rk, kh)
    return dq, dk2, _mxu(p, do, "tn"), dgq, dgk


def _combine(o1, o2, o3, l1, l2, l3, z):
    m = lax.stop_gradient(jnp.maximum(jnp.maximum(l1, l2), l3))
    e1, e2, e3 = jnp.exp(l1 - m), jnp.exp(l2 - m), jnp.exp(l3 - m)
    return (e1 * o1 + e2 * o2 + e3 * o3) / (e1 + e2 + e3) * _silu(z)


def _mem_block(q, z, kv, gq, gk):
    outs = []
    for h in range(MEM_HEADS):
        sl = slice(h * MEM_HD, (h + 1) * MEM_HD)
        qn = _rms_rows(q[:, sl], gq)
        kn = _rms_rows(kv[:, sl], gk)
        s = _dot(qn, kn, "nt") * (MEM_HD ** -0.5)
        m = lax.stop_gradient(jnp.max(s, axis=-1, keepdims=True))
        p = jnp.exp(s - m)
        den = jnp.sum(p, axis=-1, keepdims=True)
        outs.append(_dot(p, kv[:, MEMW + h * MEM_HD:MEMW + (h + 1) * MEM_HD], "nn") / den)
    return jnp.concatenate(outs, axis=-1) * _silu(z)


def _cast(w, name):
    R, C = w.shape
    tr, tc = _tile(R, 512, 8), _tile(C, 2176)

    def body(w_ref, o_ref):
        o_ref[...] = w_ref[...].astype(o_ref.dtype)

    spec = pl.BlockSpec((tr, tc), lambda i, j: (i, j))
    return _call(body, name=name, grid=(R // tr, C // tc), in_specs=[spec], out_specs=spec,
                 out_shape=jax.ShapeDtypeStruct((R, C), WIRE_DTYPE))(w)


def _place_shard(w, kind, pos, dtype, name, slot=0, into=None, half=None):
    R, C = w.shape
    tr, tc = _tile(R, 512, 8), _tile(C if half is None else C // 2, 2176)
    nr, nc = R // tr, C // tc
    ncols = nc if half is None else nc // 2

    def body(pos_ref, w_ref, *rest):
        rest[-1][...] = w_ref[...].astype(rest[-1].dtype)

    def col(j, pos_ref):
        if half is None:
            return j
        return (pos_ref[1] if half == 0 else 1 - pos_ref[1]) * ncols + j

    if kind == "col":
        full = (R, 4 * C)
        out = pl.BlockSpec((tr, tc), lambda i, j, pos_ref: (i, pos_ref[slot] * nc + col(j, pos_ref)))
    else:
        full, out = (4 * R, C), pl.BlockSpec((tr, tc), lambda i, j, pos_ref: (pos_ref[slot] * nr + i, j))
    in_specs, args = [pl.BlockSpec((tr, tc), lambda i, j, pos_ref: (i, col(j, pos_ref)))], [pos, w]
    if into is not None:
        in_specs.append(ANY)
        args.append(into)
    spec = pltpu.PrefetchScalarGridSpec(num_scalar_prefetch=1, grid=(nr, ncols), in_specs=in_specs, out_specs=out)
    return _call(body, name=name, grid_spec=spec, out_shape=jax.ShapeDtypeStruct(full, dtype),
                 aliases={} if into is None else {2: 0})(*args)


def _matmul(a, b, mode, out_dtype, *, name, tm=512, tn=512, tk=512):
    if mode == "nn":
        (M, K), (_, N) = a.shape, b.shape
    elif mode == "nt":
        (M, K), (N, _) = a.shape, b.shape
    else:
        (K, M), (_, N) = a.shape, b.shape
    tm, tn, tk = _tile(M, tm), _tile(N, tn), _tile(K, tk)
    nk = K // tk

    def body(a_ref, b_ref, o_ref, *acc):
        part = lax.dot_general(a_ref[...], b_ref[...], _DIMS[mode], preferred_element_type=F32)
        if nk == 1:
            o_ref[...] = part.astype(o_ref.dtype)
            return
        acc_ref, = acc
        k = pl.program_id(2)

        @pl.when(k == 0)
        def _():
            acc_ref[...] = part

        @pl.when(k > 0)
        def _():
            acc_ref[...] += part

        @pl.when(k == nk - 1)
        def _():
            o_ref[...] = acc_ref[...].astype(o_ref.dtype)

    a_spec = pl.BlockSpec((tk, tm), lambda i, j, k: (k, i)) if mode == "tn" else pl.BlockSpec((tm, tk), lambda i, j, k: (i, k))
    b_spec = pl.BlockSpec((tn, tk), lambda i, j, k: (j, k)) if mode == "nt" else pl.BlockSpec((tk, tn), lambda i, j, k: (k, j))
    return _call(body, name=name, grid=(M // tm, N // tn, nk), in_specs=[a_spec, b_spec],
                 out_specs=pl.BlockSpec((tm, tn), lambda i, j, k: (i, j)),
                 out_shape=jax.ShapeDtypeStruct((M, N), out_dtype),
                 scratch_shapes=[] if nk == 1 else [pltpu.VMEM((tm, tn), F32)])(a, b)


def _rms_fwd(x, g, name):
    R, D = x.shape
    tr = _tile(R, 512)

    def body(x_ref, g_ref, o_ref, t_ref):
        y = _rms_rows(x_ref[...], g_ref[...])
        o_ref[...] = y.astype(o_ref.dtype)
        t_ref[...] = y.T.astype(t_ref.dtype)

    row = pl.BlockSpec((tr, D), lambda i: (i, 0))
    return _call(body, name=name, grid=(R // tr,), in_specs=[row, pl.BlockSpec((1, D), lambda i: (0, 0))],
                 out_specs=[row, pl.BlockSpec((D, tr), lambda i: (0, i))],
                 out_shape=[jax.ShapeDtypeStruct((R, D), MXU_DTYPE), jax.ShapeDtypeStruct((D, R), MXU_DTYPE)])(x, g)


def _rms_bwd(x, dh, g, dy, name):
    R, D = x.shape
    tr = _tile(R, 256)
    with_dx = dy is not None

    def body(*refs):
        if with_dx:
            x_ref, dh_ref, g_ref, dy_ref, dx_ref, dg_ref = refs
        else:
            x_ref, dh_ref, g_ref, dg_ref = refs
        xv, dhv = x_ref[...], dh_ref[...]
        r = lax.rsqrt(jnp.mean(xv * xv, axis=-1, keepdims=True) + EPS)
        xh = xv * r

        @pl.when(pl.program_id(0) == 0)
        def _():
            dg_ref[...] = jnp.zeros_like(dg_ref)

        dg_ref[...] += jnp.sum(dhv * xh, axis=0, keepdims=True)
        if with_dx:
            dxh = dhv * g_ref[...]
            dx_ref[...] = dy_ref[...] + r * (dxh - xh * jnp.mean(dxh * xh, axis=-1, keepdims=True))

    row = pl.BlockSpec((tr, D), lambda i: (i, 0))
    vec = pl.BlockSpec((1, D), lambda i: (0, 0))
    dg_shape = jax.ShapeDtypeStruct((1, D), F32)
    if with_dx:
        return _call(body, name=name, grid=(R // tr,), in_specs=[row, row, vec, row], out_specs=[row, vec],
                     out_shape=[jax.ShapeDtypeStruct((R, D), F32), dg_shape])(x, dh, g, dy)
    return None, _call(body, name=name, grid=(R // tr,), in_specs=[row, row, vec], out_specs=vec,
                       out_shape=dg_shape)(x, dh, g)


def _attn_geom(g, d):
    hc = HPG if d == 1 else 1
    cw = hc * HEAD
    cq, ck, cv = (Q0 + g * GW) // cw, (K0 + g * GW) // cw, (V0 + g * GW) // cw
    return (1, BLK * d, cw), hc, HPG // hc, cq, ck, cv


def _rows(ref, r, d, sl):
    if d == 1:
        return ref[0, :, sl]
    return ref.at[0][pl.ds(r, BLK, stride=d), sl]


def _set_rows(ref, r, d, sl, val):
    if d == 1:
        ref[0, :, sl] = val
    else:
        ref.at[0][pl.ds(r, BLK, stride=d), sl] = val


def _stage_rows(ref, r, d, sl, val):
    if d == 1:
        ref[:, sl] = val
    else:
        ref[pl.ds(r, BLK, stride=d), sl] = val


def _proj_stages(blk, d):
    return [] if d == 1 else [pltpu.VMEM(blk[1:], F32)] * 5


def _proj_rows(refs, stages, d):
    if d == 1:
        return [lambda r, sl, ref=ref: ref[0, :, sl].astype(F32) for ref in refs]
    for ref, stage in zip(refs, stages):
        stage[...] = ref[0].astype(F32)
    return [lambda r, sl, stage=stage: stage[pl.ds(r, BLK, stride=d), sl] for stage in stages]


def _attn_fwd(proj3, gq, gk, g, d):
    Bl, S, _ = proj3.shape
    blk, hc, ncb, cq, ck, cv = _attn_geom(g, d)
    nb = S // blk[1]
    if nb == 1:
        return _attn_single_fwd(proj3, gq, gk, g, d)

    def body(q_ref, kp_ref, kc_ref, vp_ref, vc_ref, gq_ref, gk_ref, o_ref, lse_ref, *stages):
        first = pl.program_id(2) == 0
        q, kp, kc, vp, vc = _proj_rows((q_ref, kp_ref, kc_ref, vp_ref, vc_ref), stages, d)
        def run(alone):
            for r in range(d):
                for h in range(hc):
                    sl = slice(h * HEAD, (h + 1) * HEAD)
                    if alone:
                        k2, v2 = kc(r, sl), vc(r, sl)
                    else:
                        k2 = jnp.concatenate([kp(r, sl), kc(r, sl)], axis=0)
                        v2 = jnp.concatenate([vp(r, sl), vc(r, sl)], axis=0)
                    o, lse = _attn_block(q(r, sl), k2, v2, gq_ref[...], gk_ref[...], False)
                    _set_rows(o_ref, r, d, sl, o)
                    _set_rows(lse_ref, r, d, sl, jnp.broadcast_to(lse, (BLK, HEAD)))

        pl.when(first)(lambda: run(True))
        pl.when(jnp.logical_not(first))(lambda: run(False))

    def cur(c0):
        return pl.BlockSpec(blk, lambda b, j, i: (b, i, c0 + j))

    def prev(c0):
        return pl.BlockSpec(blk, lambda b, j, i: (b, jnp.maximum(i - 1, 0), c0 + j))

    vec = pl.BlockSpec((1, HEAD), lambda b, j, i: (0, 0))
    out = pl.BlockSpec(blk, lambda b, j, i: (b, i, j))
    shp = jax.ShapeDtypeStruct((Bl, S, GW), F32)
    return _call(body, name=f"attn_fwd_g{g}", grid=(Bl, ncb, nb),
                 in_specs=[cur(cq), prev(ck), cur(ck), prev(cv), cur(cv), vec, vec],
                 out_specs=[out, out], out_shape=[shp, shp], scratch_shapes=_proj_stages(blk, d),
                 )(proj3, proj3, proj3, proj3, proj3, gq, gk)


def _attn_single_fwd(proj3, gq, gk, g, d):
    Bl, S, _ = proj3.shape
    blk, hc, ncb, cq, ck, cv = _attn_geom(g, d)

    def body(q_ref, k_ref, v_ref, gq_ref, gk_ref, o_ref, lse_ref, *stages):
        q, k, v = _proj_rows((q_ref, k_ref, v_ref), stages, d)
        for r in range(d):
            for h in range(hc):
                sl = slice(h * HEAD, (h + 1) * HEAD)
                o, lse = _attn_block(q(r, sl), k(r, sl), v(r, sl), gq_ref[...], gk_ref[...], True)
                _set_rows(o_ref, r, d, sl, o)
                _set_rows(lse_ref, r, d, sl, jnp.broadcast_to(lse, (BLK, HEAD)))

    def at(c0):
        return pl.BlockSpec(blk, lambda b, j: (b, 0, c0 + j))

    vec = pl.BlockSpec((1, HEAD), lambda b, j: (0, 0))
    shp = jax.ShapeDtypeStruct((Bl, S, GW), F32)
    return _call(body, name=f"attn_fwd_g{g}", grid=(Bl, ncb), in_specs=[at(cq), at(ck), at(cv), vec, vec],
                 out_specs=[at(0), at(0)], out_shape=[shp, shp], scratch_shapes=_proj_stages(blk, d)[:3],
                 )(proj3, proj3, proj3, gq, gk)


def _attn_single_bwd(proj3, gq, gk, o3, l3, do3, dl3, g, d):
    Bl, S, _ = proj3.shape
    blk, hc, ncb, cq, ck, cv = _attn_geom(g, d)

    def body(q_ref, k_ref, v_ref, gq_ref, gk_ref, o_ref, l_ref, do_ref, dl_ref,
             dq_ref, dk_ref, dv_ref, dgq_ref, dgk_ref, sq_ref, sk_ref, sv_ref, *stages):
        @pl.when((pl.program_id(0) == 0) & (pl.program_id(1) == 0))
        def _():
            dgq_ref[...] = jnp.zeros_like(dgq_ref)
            dgk_ref[...] = jnp.zeros_like(dgk_ref)

        dgq, dgk = jnp.zeros((1, HEAD), F32), jnp.zeros((1, HEAD), F32)
        q, k, v = _proj_rows((q_ref, k_ref, v_ref), stages, d)
        for r in range(d):
            for h in range(hc):
                sl = slice(h * HEAD, (h + 1) * HEAD)
                dq, dk, dv, a, b = _attn_block_bwd(
                    q(r, sl), k(r, sl), v(r, sl), gq_ref[...], gk_ref[...], True, _rows(do_ref, r, d, sl),
                    _rows(o_ref, r, d, sl), _rows(l_ref, r, d, sl)[:, :1], _rows(dl_ref, r, d, sl)[:, :1])
                _stage_rows(sq_ref, r, d, sl, dq)
                _stage_rows(sk_ref, r, d, sl, dk)
                _stage_rows(sv_ref, r, d, sl, dv)
                dgq, dgk = dgq + a, dgk + b
        dgq_ref[...] += dgq
        dgk_ref[...] += dgk
        dq_ref[0] = sq_ref[...].astype(dq_ref.dtype)
        dk_ref[0] = sk_ref[...].astype(dk_ref.dtype)
        dv_ref[0] = sv_ref[...].astype(dv_ref.dtype)

    def at(c0):
        return pl.BlockSpec(blk, lambda b, j: (b, 0, c0 + j))

    vec = pl.BlockSpec((1, HEAD), lambda b, j: (0, 0))
    shp = jax.ShapeDtypeStruct((Bl, S, GW), MXU_DTYPE)
    gshp = jax.ShapeDtypeStruct((1, HEAD), F32)
    return _call(body, name=f"attn_bwd_g{g}", grid=(Bl, ncb),
                 in_specs=[at(cq), at(ck), at(cv), vec, vec, at(0), at(0), at(0), at(0)],
                 out_specs=[at(0), at(0), at(0), vec, vec], out_shape=[shp, shp, shp, gshp, gshp],
                 scratch_shapes=[pltpu.VMEM(blk[1:], F32)] * 3 + _proj_stages(blk, d)[:3],
                 )(proj3, proj3, proj3, gq, gk, o3, l3, do3, dl3)


def _attn_bwd(proj3, gq, gk, o3, l3, do3, dl3, g, d):
    Bl, S, _ = proj3.shape
    blk, hc, ncb, cq, ck, cv = _attn_geom(g, d)
    nb = S // blk[1]
    if nb == 1:
        return _attn_single_bwd(proj3, gq, gk, o3, l3, do3, dl3, g, d)

    def body(q_ref, kp_ref, kc_ref, vp_ref, vc_ref, gq_ref, gk_ref, o_ref, l_ref, do_ref, dl_ref,
             dq_ref, dk_ref, dv_ref, dgq_ref, dgk_ref, ck_ref, cv_ref, sq_ref, sk_ref, sv_ref, *stages):
        i = pl.program_id(2)
        first = i == 0

        @pl.when((pl.program_id(0) == 0) & (pl.program_id(1) == 0) & first)
        def _():
            dgq_ref[...] = jnp.zeros_like(dgq_ref)
            dgk_ref[...] = jnp.zeros_like(dgk_ref)

        def run(alone):
            dgq, dgk = jnp.zeros((1, HEAD), F32), jnp.zeros((1, HEAD), F32)
            q, kp, kc, vp, vc = _proj_rows((q_ref, kp_ref, kc_ref, vp_ref, vc_ref), stages, d)
            for r in range(d):
                rs = slice(r * BLK, (r + 1) * BLK)
                for h in range(hc):
                    sl = slice(h * HEAD, (h + 1) * HEAD)
                    if alone:
                        k2, v2 = kc(r, sl), vc(r, sl)
                    else:
                        k2 = jnp.concatenate([kp(r, sl), kc(r, sl)], axis=0)
                        v2 = jnp.concatenate([vp(r, sl), vc(r, sl)], axis=0)
                    dq, dk2, dv2, a, b = _attn_block_bwd(
                        q(r, sl), k2, v2, gq_ref[...], gk_ref[...], False, _rows(do_ref, r, d, sl),
                        _rows(o_ref, r, d, sl), _rows(l_ref, r, d, sl)[:, :1], _rows(dl_ref, r, d, sl)[:, :1])
                    _stage_rows(sq_ref, r, d, sl, dq)
                    if alone:
                        _stage_rows(sk_ref, r, d, sl, jnp.zeros((BLK, HEAD), F32))
                        _stage_rows(sv_ref, r, d, sl, jnp.zeros((BLK, HEAD), F32))
                    else:
                        _stage_rows(sk_ref, r, d, sl, ck_ref[rs, sl] + dk2[:BLK])
                        _stage_rows(sv_ref, r, d, sl, cv_ref[rs, sl] + dv2[:BLK])
                    ck_ref[rs, sl] = dk2[-BLK:]
                    cv_ref[rs, sl] = dv2[-BLK:]
                    dgq, dgk = dgq + a, dgk + b
            dgq_ref[...] += dgq
            dgk_ref[...] += dgk
            dq_ref[0] = sq_ref[...].astype(dq_ref.dtype)

        pl.when(first)(lambda: run(True))
        pl.when((i > 0) & (i < nb))(lambda: run(False))

        @pl.when(i == nb)
        def _():
            for r in range(d):
                rs = slice(r * BLK, (r + 1) * BLK)
                _stage_rows(sk_ref, r, d, slice(None), ck_ref[rs, :])
                _stage_rows(sv_ref, r, d, slice(None), cv_ref[rs, :])

        dk_ref[0] = sk_ref[...].astype(dk_ref.dtype)
        dv_ref[0] = sv_ref[...].astype(dv_ref.dtype)

    def cur(c0):
        return pl.BlockSpec(blk, lambda b, j, i: (b, jnp.minimum(i, nb - 1), c0 + j))

    def prev(c0):
        return pl.BlockSpec(blk, lambda b, j, i: (b, jnp.clip(i - 1, 0, nb - 1), c0 + j))

    vec = pl.BlockSpec((1, HEAD), lambda b, j, i: (0, 0))
    at_q = pl.BlockSpec(blk, lambda b, j, i: (b, jnp.minimum(i, nb - 1), j))
    at_k = pl.BlockSpec(blk, lambda b, j, i: (b, jnp.maximum(i - 1, 0), j))
    shp = jax.ShapeDtypeStruct((Bl, S, GW), MXU_DTYPE)
    gshp = jax.ShapeDtypeStruct((1, HEAD), F32)
    return _call(body, name=f"attn_bwd_g{g}", grid=(Bl, ncb, nb + 1),
                 in_specs=[cur(cq), prev(ck), cur(ck), prev(cv), cur(cv), vec, vec, at_q, at_q, at_q, at_q],
                 out_specs=[at_q, at_k, at_k, vec, vec], out_shape=[shp, shp, shp, gshp, gshp],
                 scratch_shapes=[pltpu.VMEM(blk[1:], F32)] * 5 + _proj_stages(blk, d),
                 )(proj3, proj3, proj3, proj3, proj3, gq, gk, o3, l3, do3, dl3)


def _combine_fwd(os, ls, proj2):
    T = proj2.shape[0]
    tr = _tile(T, 512)

    def body(o1, o2, o3, l1, l2, l3, z, a_ref, at_ref):
        a = _combine(o1[...], o2[...], o3[...], l1[...], l2[...], l3[...], z[...].astype(F32))
        a_ref[...] = a.astype(a_ref.dtype)
        at_ref[...] = a.T.astype(at_ref.dtype)

    row = pl.BlockSpec((tr, GW), lambda i: (i, 0))
    return _call(body, name="combine_fwd", grid=(T // tr,),
                 in_specs=[row] * 6 + [pl.BlockSpec((tr, GW), lambda i: (i, ZA // GW))],
                 out_specs=[row, pl.BlockSpec((GW, tr), lambda i: (0, i))],
                 out_shape=[jax.ShapeDtypeStruct((T, GW), MXU_DTYPE), jax.ShapeDtypeStruct((GW, T), MXU_DTYPE)],
                 )(*os, *ls, proj2)


def _combine_bwd(os, ls, proj2, da, dproj):
    T = proj2.shape[0]
    tr = _tile(T, 256)

    def body(o1, o2, o3, l1, l2, l3, z, da_ref, _, d1, d2, d3, e1, e2, e3, dz_ref):
        _, vjp = jax.vjp(_combine, o1[...], o2[...], o3[...], l1[...], l2[...], l3[...], z[...].astype(F32))
        go1, go2, go3, gl1, gl2, gl3, gz = vjp(da_ref[...])
        d1[...], d2[...], d3[...] = go1, go2, go3
        dz_ref[...] = gz.astype(dz_ref.dtype)
        for ref, gl in ((e1, gl1), (e2, gl2), (e3, gl3)):
            for h in range(HPG):
                sl = slice(h * HEAD, (h + 1) * HEAD)
                ref[:, sl] = jnp.broadcast_to(jnp.sum(gl[:, sl], axis=-1, keepdims=True), (tr, HEAD))

    row = pl.BlockSpec((tr, GW), lambda i: (i, 0))
    f = jax.ShapeDtypeStruct((T, GW), F32)
    z_attn = pl.BlockSpec((tr, GW), lambda i: (i, ZA // GW))
    outs = _call(body, name="combine_bwd", grid=(T // tr,), in_specs=[row] * 6 + [z_attn, row, ANY],
                 out_specs=[row] * 6 + [z_attn], out_shape=[f] * 6 + [jax.ShapeDtypeStruct(dproj.shape, dproj.dtype)],
                 aliases={8: 6})(*os, *ls, proj2, da, dproj)
    return outs[:3], outs[3:6], outs[6]


def _shift_down(u, j, t):
    return jnp.where(t >= j, pltpu.roll(u, j, 0), 0.0)


def _shift_up(u, j, t):
    n = u.shape[0]
    return jnp.where(t < n - j, pltpu.roll(u, n - j, 0), 0.0)


def _conv_specs(Bl, S, cw):
    def sec(c0):
        return pl.BlockSpec((1, S, cw), lambda j, b: (b, 0, c0 // cw + j))
    return [sec(CB), sec(CC), sec(CV), sec(ZC)], pl.BlockSpec((3, cw), lambda j, b: (0, j))


def _conv_fwd(proj3, conv_w):
    Bl, S, _ = proj3.shape
    cw = 256
    secs, wspec = _conv_specs(Bl, S, cw)

    def body(b_ref, c_ref, v_ref, z_ref, w_ref, o_ref, ot_ref):
        t = lax.broadcasted_iota(jnp.int32, (S, cw), 0)
        u = c_ref[0].astype(F32) * v_ref[0].astype(F32)
        y = w_ref[0:1, :] * u + w_ref[1:2, :] * _shift_down(u, 1, t) + w_ref[2:3, :] * _shift_down(u, 2, t)
        out = b_ref[0].astype(F32) * y * _silu(z_ref[0].astype(F32))
        o_ref[0] = out.astype(o_ref.dtype)
        ot_ref[...] = out.T.astype(ot_ref.dtype)

    return _call(body, name="conv_fwd", grid=(CONVW // cw, Bl), in_specs=secs + [wspec],
                 out_specs=[pl.BlockSpec((1, S, cw), lambda j, b: (b, 0, j)), pl.BlockSpec((cw, S), lambda j, b: (j, b))],
                 out_shape=[jax.ShapeDtypeStruct((Bl, S, CONVW), MXU_DTYPE),
                            jax.ShapeDtypeStruct((CONVW, Bl * S), MXU_DTYPE)])(proj3, proj3, proj3, proj3, conv_w)


def _conv_bwd(proj3, conv_w, dcc3, dproj):
    Bl, S, _ = proj3.shape
    cw = 256
    secs, wspec = _conv_specs(Bl, S, cw)

    def body(b_ref, c_ref, v_ref, z_ref, w_ref, d_ref, _, dproj_ref, dw_ref, stage, sems):
        t = lax.broadcasted_iota(jnp.int32, (S, cw), 0)
        bv, cv, vv, zv = (r[0].astype(F32) for r in (b_ref, c_ref, v_ref, z_ref))
        dv = d_ref[0]
        u = cv * vv
        u1, u2 = _shift_down(u, 1, t), _shift_down(u, 2, t)
        y = w_ref[0:1, :] * u + w_ref[1:2, :] * u1 + w_ref[2:3, :] * u2
        sg = _sig(zv)
        sz = zv * sg
        gy = dv * bv * sz
        du = w_ref[0:1, :] * gy + w_ref[1:2, :] * _shift_up(gy, 1, t) + w_ref[2:3, :] * _shift_up(gy, 2, t)
        j, b = pl.program_id(0), pl.program_id(1)
        tiles = [dv * y * sz, du * vv, du * cv, dv * bv * y * sg * (1.0 + zv * (1.0 - sg))]
        dsts = [dproj_ref.at[pl.ds(b * S, S), pl.ds(c0 + j * cw, cw)] for c0 in (CB, CC, CV, ZC)]
        _emit_tiles(j * Bl + b, (CONVW // cw) * Bl, tiles, dsts, stage, sems)

        @pl.when(pl.program_id(1) == 0)
        def _():
            dw_ref[...] = jnp.zeros_like(dw_ref)

        dw_ref[0:1, :] += jnp.sum(gy * u, axis=0, keepdims=True)
        dw_ref[1:2, :] += jnp.sum(gy * u1, axis=0, keepdims=True)
        dw_ref[2:3, :] += jnp.sum(gy * u2, axis=0, keepdims=True)

    blk = pl.BlockSpec((1, S, cw), lambda j, b: (b, 0, j))
    return _call(body, name="conv_bwd", grid=(CONVW // cw, Bl), in_specs=secs + [wspec, blk, ANY],
                 out_specs=[ANY, wspec],
                 out_shape=[jax.ShapeDtypeStruct(dproj.shape, dproj.dtype), jax.ShapeDtypeStruct((3, CONVW), F32)],
                 scratch_shapes=_emit_scratch(4, S, cw), aliases={6: 0})(proj3, proj3, proj3, proj3, conv_w, dcc3, dproj)


def _mem_specs(S, tq):
    q = pl.BlockSpec((1, tq, MEMW), lambda b, j: (b, j, MQ // MEMW))
    z = pl.BlockSpec((1, tq, MEMW), lambda b, j: (b, j, ZM // MEMW))
    kv = pl.BlockSpec((1, MEM_HD, 2 * MEMW), lambda b, j: (b, 0, 0))
    vec = pl.BlockSpec((1, MEM_HD), lambda b, j: (0, 0))
    blk = pl.BlockSpec((1, tq, MEMW), lambda b, j: (b, j, 0))
    return q, z, kv, vec, blk


def _mem_fwd(proj3, mkv3, gq, gk):
    Bl, S, _ = proj3.shape
    tq = _tile(S, 512)
    q, z, kv, vec, blk = _mem_specs(S, tq)

    def body(q_ref, z_ref, kv_ref, gq_ref, gk_ref, o_ref, ot_ref):
        out = _mem_block(q_ref[0].astype(F32), z_ref[0].astype(F32), kv_ref[0], gq_ref[...], gk_ref[...])
        o_ref[0] = out.astype(o_ref.dtype)
        ot_ref[...] = out.T.astype(ot_ref.dtype)

    nq = S // tq
    return _call(body, name="mem_fwd", grid=(Bl, nq), in_specs=[q, z, kv, vec, vec],
                 out_specs=[blk, pl.BlockSpec((MEMW, tq), lambda b, j: (0, b * nq + j))],
                 out_shape=[jax.ShapeDtypeStruct((Bl, S, MEMW), MXU_DTYPE),
                            jax.ShapeDtypeStruct((MEMW, Bl * S), MXU_DTYPE)])(proj3, proj3, mkv3, gq, gk)


def _mem_bwd(proj3, mkv3, gq, gk, dmo3, dproj):
    Bl, S, _ = proj3.shape
    tq = _tile(S, 256)
    q, z, kv, vec, blk = _mem_specs(S, tq)
    nq = S // tq

    def body(q_ref, z_ref, kv_ref, gq_ref, gk_ref, d_ref, _, dproj_ref, dkv_ref, dgq_ref, dgk_ref, stage, sems):
        _, vjp = jax.vjp(_mem_block, q_ref[0].astype(F32), z_ref[0].astype(F32), kv_ref[0], gq_ref[...], gk_ref[...])
        dq, dz, dkv, dgq, dgk = vjp(d_ref[0])
        j = pl.program_id(1)
        rows = pl.ds(pl.program_id(0) * S + j * tq, tq)
        dsts = [dproj_ref.at[rows, pl.ds(MQ, MEMW)], dproj_ref.at[rows, pl.ds(ZM, MEMW)]]
        _emit_tiles(pl.program_id(0) * nq + j, Bl * nq, [dq, dz], dsts, stage, sems)

        @pl.when(j == 0)
        def _():
            dkv_ref[0] = jnp.zeros_like(dkv)

        @pl.when((j == 0) & (pl.program_id(0) == 0))
        def _():
            dgq_ref[...] = jnp.zeros_like(dgq_ref)
            dgk_ref[...] = jnp.zeros_like(dgk_ref)

        dkv_ref[0] += dkv
        dgq_ref[...] += dgq
        dgk_ref[...] += dgk

    gshp = jax.ShapeDtypeStruct((1, MEM_HD), F32)
    return _call(body, name="mem_bwd", grid=(Bl, nq), in_specs=[q, z, kv, vec, vec, blk, ANY],
                 out_specs=[ANY, kv, vec, vec],
                 out_shape=[jax.ShapeDtypeStruct(dproj.shape, dproj.dtype), jax.ShapeDtypeStruct(mkv3.shape, F32),
                            gshp, gshp],
                 scratch_shapes=_emit_scratch(2, tq, MEMW), aliases={6: 0})(proj3, proj3, mkv3, gq, gk, dmo3, dproj)


def _merge_specs(T, D, tm, tn):
    def act(w):
        return pl.BlockSpec((tm, w), lambda i, n: (i, 0))

    def wsp(w):
        return pl.BlockSpec((w, tn), lambda i, n: (0, n))

    gates = [pl.BlockSpec((tm, tn), lambda i, n, k=k: (i, (G0 + k * D) // tn + n)) for k in range(3)]
    tile = pl.BlockSpec((tm, tn), lambda i, n: (i, n))
    return act, wsp, gates, tile


def _merge_fwd(a, cc, mo, wa, wc, wm, proj2):
    T, D = a.shape[0], wa.shape[1]
    tm, tn = _tile(T, 1024), _tile(D, 512)
    act, wsp, gates, tile = _merge_specs(T, D, tm, tn)

    def body(a_ref, c_ref, m_ref, wa_ref, wc_ref, wm_ref, g0, g1, g2, mg_ref, mt_ref, pa_ref, pc_ref, pm_ref):
        pa = jnp.dot(a_ref[...], wa_ref[...], preferred_element_type=F32)
        pc = jnp.dot(c_ref[...], wc_ref[...], preferred_element_type=F32)
        pm = jnp.dot(m_ref[...], wm_ref[...], preferred_element_type=F32)
        mg = _sig(g0[...].astype(F32)) * pa + _sig(g1[...].astype(F32)) * pc + _sig(g2[...].astype(F32)) * pm
        mg_ref[...] = mg.astype(mg_ref.dtype)
        mt_ref[...] = mg.T.astype(mt_ref.dtype)
        pa_ref[...] = pa.astype(pa_ref.dtype)
        pc_ref[...] = pc.astype(pc_ref.dtype)
        pm_ref[...] = pm.astype(pm_ref.dtype)

    shp = jax.ShapeDtypeStruct((T, D), MXU_DTYPE)
    return _call(body, name="merge_fwd", grid=(T // tm, D // tn),
                 in_specs=[act(GW), act(CONVW), act(MEMW), wsp(GW), wsp(CONVW), wsp(MEMW)] + gates,
                 out_specs=[tile, pl.BlockSpec((tn, tm), lambda i, n: (n, i)), tile, tile, tile],
                 out_shape=[shp, jax.ShapeDtypeStruct((D, T), MXU_DTYPE), shp, shp, shp],
                 )(a, cc, mo, wa, wc, wm, proj2, proj2, proj2)


def _emit_tiles(step, nsteps, tiles, dsts, stage, sems):
    slot = step % 2

    def copies(s):
        return [pltpu.make_async_copy(stage.at[s, k], dsts[k], sems.at[s, k]) for k in range(len(tiles))]

    @pl.when(step >= 2)
    def _():
        for cp in copies(slot):
            cp.wait()

    for k, t in enumerate(tiles):
        stage[slot, k] = t.astype(stage.dtype)
    for cp in copies(slot):
        cp.start()

    @pl.when(step == nsteps - 1)
    def _():
        for cp in copies(slot):
            cp.wait()
        if nsteps > 1:
            for cp in copies(1 - slot):
                cp.wait()


def _emit_scratch(k, rows, cols):
    return [pltpu.VMEM((2, k, rows, cols), MXU_DTYPE), pltpu.SemaphoreType.DMA((2, k))]


def _merge_bwd(dyb, w_out, proj2, pa, pc, pm):
    T, D = dyb.shape
    IN = proj2.shape[1]
    tm, tn = _tile(T, 1024), _tile(D, 512)
    _, _, gates, tile = _merge_specs(T, D, tm, tn)
    nn = D // tn

    def body(dy_ref, w_ref, g0, g1, g2, p0, p1, p2, dp0, dp1, dp2, dproj_ref, stage, sems):
        i, n = pl.program_id(0), pl.program_id(1)
        dm = lax.dot_general(dy_ref[...], w_ref[...], _DIMS["nt"], preferred_element_type=F32)
        tiles, dsts = [], []
        for k, (g_ref, p_ref, dp_ref) in enumerate(((g0, p0, dp0), (g1, p1, dp1), (g2, p2, dp2))):
            gt = _sig(g_ref[...].astype(F32))
            dp_ref[...] = (gt * dm).astype(dp_ref.dtype)
            tiles.append(dm * p_ref[...].astype(F32) * gt * (1.0 - gt))
            dsts.append(dproj_ref.at[pl.ds(i * tm, tm), pl.ds(G0 + k * D + n * tn, tn)])
        _emit_tiles(i * nn + n, (T // tm) * nn, tiles, dsts, stage, sems)

    shp = jax.ShapeDtypeStruct((T, D), MXU_DTYPE)
    return _call(body, name="merge_bwd", grid=(T // tm, nn),
                 in_specs=[pl.BlockSpec((tm, D), lambda i, n: (i, 0)), pl.BlockSpec((tn, D), lambda i, n: (n, 0))]
                 + gates + [tile] * 3,
                 out_specs=[tile] * 3 + [ANY], out_shape=[shp] * 3 + [jax.ShapeDtypeStruct((T, IN), MXU_DTYPE)],
                 scratch_shapes=_emit_scratch(3, tm, tn))(dyb, w_out, proj2, proj2, proj2, pa, pc, pm)


def _out_loss(merged, w_out, x, tgt):
    T, D = x.shape
    tm = _tile(T, 512)

    def body(m_ref, w_ref, x_ref, t_ref, dy_ref, dyb_ref, loss_ref):
        err = x_ref[...] + jnp.dot(m_ref[...], w_ref[...], preferred_element_type=F32) - t_ref[...]
        dy = err * (1.0 / D)
        dy_ref[...] = dy
        dyb_ref[...] = dy.astype(dyb_ref.dtype)

        @pl.when(pl.program_id(0) == 0)
        def _():
            loss_ref[...] = jnp.zeros_like(loss_ref)

        loss_ref[...] += jnp.sum(err * err) * (0.5 / D)

    row = pl.BlockSpec((tm, D), lambda i: (i, 0))
    return _call(body, name="out_loss", grid=(T // tm,),
                 in_specs=[row, pl.BlockSpec((D, D), lambda i: (0, 0)), row, row],
                 out_specs=[row, row, pl.BlockSpec((1, 128), lambda i: (0, 0))],
                 out_shape=[jax.ShapeDtypeStruct((T, D), F32), jax.ShapeDtypeStruct((T, D), MXU_DTYPE),
                            jax.ShapeDtypeStruct((1, 128), F32)])(merged, w_out, x, tgt)


def _proj_chunk(hb, w, meta, j, nslots, half, buf, name):
    T, D = hb.shape
    Cs = w.shape[1] // 4
    tm, tn = _tile(T, 1024), _tile(Cs // 2, 2176)
    nh = Cs // 2 // tn
    per = nh if half is not None else 2 * nh

    def body(meta_ref, a_ref, b_ref, *rest):
        rest[-1][...] = jnp.dot(a_ref[...], b_ref[...], preferred_element_type=F32).astype(rest[-1].dtype)

    def tile(n, m):
        if half is None:
            return n % per
        return (m[4] if half == 0 else 1 - m[4]) * nh + n % per

    in_specs = [pl.BlockSpec((tm, D), lambda n, i, m: (i, 0)),
                pl.BlockSpec((D, tn), lambda n, i, m: (0, (j + n // per) * 2 * nh + tile(n, m)))]
    args = [meta, hb, w]
    if buf is not None:
        in_specs.append(ANY)
        args.append(buf)
    spec = pltpu.PrefetchScalarGridSpec(
        num_scalar_prefetch=1, grid=(nslots * per, T // tm), in_specs=in_specs,
        out_specs=pl.BlockSpec((tm, tn), lambda n, i, m: (i, m[j + n // per] * 2 * nh + tile(n, m))))
    return _call(body, name=name, grid_spec=spec, out_shape=jax.ShapeDtypeStruct((T, 4 * Cs), PROJ_DTYPE),
                 aliases={} if buf is None else {3: 0})(*args)


def _norms(x, mem, norm_g, mem_norm_g):
    D = x.shape[-1]
    hb, hbt = _rms_fwd(x.reshape(-1, D), norm_g.reshape(1, D), "rms_x")
    mhb, _ = _rms_fwd(mem.reshape(-1, D), mem_norm_g.reshape(1, D), "rms_mem")
    return hb, hbt, mhb


def _attention_fwd(proj2, Bl, gq_all, gk_all):
    T, IN = proj2.shape
    proj3 = proj2.reshape(Bl, T // Bl, IN)
    os, ls = [], []
    for g, d in enumerate(DILATIONS):
        o, l = _attn_fwd(proj3, gq_all[g:g + 1], gk_all[g:g + 1], g, d)
        os.append(o.reshape(T, GW))
        ls.append(l.reshape(T, GW))
    return os, ls, _combine_fwd(os, ls, proj2)


def _conv_branch_fwd(proj2, Bl, conv_w):
    T, IN = proj2.shape
    cc, cct = _conv_fwd(proj2.reshape(Bl, T // Bl, IN), conv_w)
    return cc.reshape(T, CONVW), cct


def _weight_grads(x, mem, tgt, norm_g, mem_norm_g, gq_all, gk_all, conv_w, mem_gq, mem_gk, W, pre, early=None):
    Bl, S, D = x.shape
    T = Bl * S
    hb, hbt, mhb, proj2, os, ls, (a, at), (cc, cct) = pre
    IN = proj2.shape[1]
    proj3 = proj2.reshape(Bl, S, IN)
    x2, tgt2 = x.reshape(T, D), tgt.reshape(T, D)
    mem2 = mem.reshape(-1, D)
    ng, mng = norm_g.reshape(1, D), mem_norm_g.reshape(1, D)
    mgq, mgk = mem_gq.reshape(1, MEM_HD), mem_gk.reshape(1, MEM_HD)
    gqs = [gq_all[g:g + 1] for g in range(NGROUP)]
    gks = [gk_all[g:g + 1] for g in range(NGROUP)]

    mkv = _matmul(mhb, W["mem_w_kv"], "nn", F32, name="mem_kv", tm=512, tn=1024, tk=D)
    mkv3 = mkv.reshape(Bl, -1, 2 * MEMW)
    mo, mot = _mem_fwd(proj3, mkv3, mgq, mgk)
    mo = mo.reshape(T, MEMW)
    merged, mergedt, pa, pc, pm = _merge_fwd(a, cc, mo, W["w_br_attn"], W["w_br_conv"], W["w_br_mem"], proj2)
    dy, dyb, loss = _out_loss(merged, W["w_out"], x2, tgt2)

    G = {}
    G["w_out"] = _matmul(mergedt, dyb, "nn", WIRE_DTYPE, name="dw_out", tm=1024, tn=512, tk=T)
    dpa, dpc, dpm, dproj = _merge_bwd(dyb, W["w_out"], proj2, pa, pc, pm)
    G["w_br_attn"] = _matmul(at, dpa, "nn", WIRE_DTYPE, name="dw_br_attn", tm=512, tn=512, tk=T)
    G["w_br_conv"] = _matmul(cct, dpc, "nn", WIRE_DTYPE, name="dw_br_conv", tm=1024, tn=512, tk=T)
    G["w_br_mem"] = _matmul(mot, dpm, "nn", WIRE_DTYPE, name="dw_br_mem", tm=1024, tn=512, tk=T)
    da = _matmul(dpa, W["w_br_attn"], "nt", F32, name="d_attn", tm=1024, tn=512, tk=D)
    dcc = _matmul(dpc, W["w_br_conv"], "nt", F32, name="d_conv", tm=1024, tn=1024, tk=D)
    dmo = _matmul(dpm, W["w_br_mem"], "nt", F32, name="d_mem", tm=1024, tn=1024, tk=D)
    dproj, dmkv3, dmgq, dmgk = _mem_bwd(proj3, mkv3, mgq, mgk, dmo.reshape(Bl, S, MEMW), dproj)
    dmkv = _cast(dmkv3.reshape(-1, 2 * MEMW), "cast_dmkv")
    G["mem_w_kv"] = _matmul(mhb, dmkv, "tn", WIRE_DTYPE, name="dw_mem_kv", tm=1024, tn=1024, tk=512)
    early_state, dmkv = (None, dmkv) if early is None else early[0](G, dmkv)
    dmh = _matmul(dmkv, W["mem_w_kv"], "nt", F32, name="d_memh", tm=512, tn=1024, tk=2 * MEMW)
    _, dmng = _rms_bwd(mem2, dmh, mng, None, "rms_mem_bwd")
    if early is not None:
        early_state, da = early[1](early_state, dmng, da)

    dos, dls, dproj = _combine_bwd(os, ls, proj2, da, dproj)
    dgq, dgk = [], []
    for g, d in enumerate(DILATIONS):
        dq, dk, dv, gq_g, gk_g = _attn_bwd(proj3, gqs[g], gks[g], os[g].reshape(Bl, S, GW), ls[g].reshape(Bl, S, GW),
                                           dos[g].reshape(Bl, S, GW), dls[g].reshape(Bl, S, GW), g, d)
        for c0, part in ((Q0, dq), (K0, dk), (V0, dv)):
            dproj = lax.dynamic_update_slice(dproj, part.reshape(T, GW), (0, c0 + g * GW))
        dgq.append(gq_g)
        dgk.append(gk_g)
    dproj, dconv_w = _conv_bwd(proj3, conv_w, dcc.reshape(Bl, S, CONVW), dproj)
    small = [loss, None, dmng] + dgq + dgk + [dconv_w.reshape(1, 3 * CONVW), dmgq, dmgk]
    return G, (dproj, x2, ng, dy, small), early_state


DW_IN_TILE = 1024


def _dw_in_half(hbt, dproj, pos, own, name, add=None, tiles=None, into=None):
    D, T = hbt.shape
    IN = dproj.shape[1]
    R, tn = D // 2, _tile(IN, DW_IN_TILE)
    j0, nj = (0, IN // tn) if tiles is None else tiles

    def body(pos_ref, a_ref, b_ref, *rest):
        acc = jnp.dot(a_ref[...], b_ref[...], preferred_element_type=F32)
        if add is not None:
            acc = acc + rest[0][...].astype(F32)
        rest[-1][...] = acc.astype(rest[-1].dtype)

    tile = pl.BlockSpec((R, tn), lambda j, p: (0, j0 + j))
    in_specs = [pl.BlockSpec((R, T), lambda j, p: (p[1] if own else 1 - p[1], 0)),
                pl.BlockSpec((T, tn), lambda j, p: (0, j0 + j))]
    args = [pos, hbt, dproj]
    if add is not None:
        in_specs.append(tile)
        args.append(add)
    if into is not None:
        in_specs.append(ANY)
        args.append(into)
    spec = pltpu.PrefetchScalarGridSpec(num_scalar_prefetch=1, grid=(nj,), in_specs=in_specs, out_specs=tile)
    return _call(body, name=name, grid_spec=spec, out_shape=jax.ShapeDtypeStruct((R, IN), WIRE_DTYPE),
                 aliases={} if into is None else {len(args) - 1: 0})(*args)


def _d_h(dproj, w, order):
    T, IN = dproj.shape
    D, Cs = w.shape[0], IN // 4
    tm, tn = _tile(T, 1024), _tile(D, 1024)

    def body(order_ref, a_ref, b_ref, o_ref, acc_ref):
        part = lax.dot_general(a_ref[...], b_ref[...], _DIMS["nt"], preferred_element_type=F32)
        k = pl.program_id(2)

        @pl.when(k == 0)
        def _():
            acc_ref[...] = part

        @pl.when(k > 0)
        def _():
            acc_ref[...] += part

        @pl.when(k == 3)
        def _():
            o_ref[...] = acc_ref[...]

    spec = pltpu.PrefetchScalarGridSpec(
        num_scalar_prefetch=1, grid=(T // tm, D // tn, 4),
        in_specs=[pl.BlockSpec((tm, Cs), lambda i, n, k, o: (i, o[k])), pl.BlockSpec((tn, Cs), lambda i, n, k, o: (n, k))],
        out_specs=pl.BlockSpec((tm, tn), lambda i, n, k, o: (i, n)), scratch_shapes=[pltpu.VMEM((tm, tn), F32)])
    return _call(body, name="d_h", grid_spec=spec, out_shape=jax.ShapeDtypeStruct((T, D), F32))(order, dproj, w)


def _input_grad(rest, w_in, order):
    dproj, x2, ng, dy, small = rest
    dh = _d_h(dproj, w_in, order)
    grad_x, dng = _rms_bwd(x2, dh, ng, dy, "rms_x_bwd")
    small = [dng if t is None else t for t in small]
    return grad_x, jnp.concatenate(small, axis=1)


def _local_step(x, mem, tgt, norm_g, mem_norm_g, gq_all, gk_all, conv_w, mem_gq, mem_gk, W):
    hb, hbt, mhb = _norms(x, mem, norm_g, mem_norm_g)
    Cs = W["w_in"].shape[1] // 4
    shards = (0, 2, 1, 3)
    order = jnp.array(shards, dtype=jnp.int32)
    w_rel = jnp.concatenate([W["w_in"][:, s * Cs:(s + 1) * Cs] for s in shards], axis=1)
    meta = jnp.array(shards + (0,), dtype=jnp.int32)
    proj2 = _proj_chunk(hb, w_rel, meta, 0, 1, None, None, "proj_0")
    for j, nslots in ((1, 2), (3, 1)):
        for half in (1, 0):
            proj2 = _proj_chunk(hb, w_rel, meta, j, nslots, half, proj2, f"proj_{j}_{half}")
    pre = (hb, hbt, mhb, proj2, *_attention_fwd(proj2, x.shape[0], gq_all, gk_all),
           _conv_branch_fwd(proj2, x.shape[0], conv_w))
    G, rest, _ = _weight_grads(x, mem, tgt, norm_g, mem_norm_g, gq_all, gk_all, conv_w, mem_gq, mem_gk, W, pre)
    pos = jnp.zeros((2,), jnp.int32)
    ntiles = rest[0].shape[1] // _tile(rest[0].shape[1], DW_IN_TILE)
    other = _dw_in_half(hbt, rest[0], pos, False, "dw_in_sibling_early", tiles=(0, ntiles - ntiles // 2))
    other = _dw_in_half(hbt, rest[0], pos, False, "dw_in_sibling_late", tiles=(ntiles - ntiles // 2, ntiles // 2),
                        into=other)
    G["w_in"] = jnp.concatenate([_dw_in_half(hbt, rest[0], pos, True, "dw_in_own"), other], axis=0)
    grad_x, small = _input_grad(rest, w_rel, order)
    return grad_x.reshape(x.shape), G, small


BIG = (("w_in", "col"), ("mem_w_kv", "row"), ("w_br_attn", "col"), ("w_br_conv", "col"),
       ("w_br_mem", "col"), ("w_out", "row"))


def _coords():
    return lax.axis_index("x"), lax.axis_index("y"), lax.axis_index("c")


def _other_chips(x, y):
    return [(1 - x, y), (x, 1 - y), (1 - x, 1 - y)]


def _half(ref, kind, c):
    R, C = ref.shape
    if kind == "col":
        return ref.at[pl.ds(c * (R // 2), R // 2), :]
    return ref.at[:, pl.ds(c * (C // 2), C // 2)]


def _shard(ref, kind, s):
    R, C = ref.shape
    if kind == "col":
        return ref.at[:, pl.ds(s * (C // 4), C // 4)]
    return ref.at[pl.ds(s * (R // 4), R // 4), :]


def _piece(ref, kind, s, c):
    R, C = ref.shape
    if kind == "col":
        return ref.at[pl.ds(c * (R // 2), R // 2), pl.ds(s * (C // 4), C // 4)]
    return ref.at[pl.ds(s * (R // 4), R // 4), pl.ds(c * (C // 2), C // 2)]


def _remote(src, dst, sems_s, sems_r, k, dev):
    return pltpu.make_async_remote_copy(src_ref=src, dst_ref=dst, send_sem=sems_s.at[k], recv_sem=sems_r.at[k],
                                        device_id=dev, device_id_type=MESH)


HBM = pl.BlockSpec(memory_space=pltpu.HBM)
SEM = pl.BlockSpec(memory_space=pltpu.SEMAPHORE)
EFFECT = pltpu.SideEffectType.DATAFLOW_SIDE_EFFECTING


def _hbm(a):
    return pltpu.with_memory_space_constraint(a, pltpu.HBM)


def _start_copies(name, arrays, ncopies, make):
    n = len(arrays)

    def body(*refs):
        for cp in make(refs[:n], refs[n], refs[n + 1]):
            cp.start()

    outs = pl.pallas_call(
        body, name=name,
        out_shape=(pltpu.SemaphoreType.DMA((ncopies,)), pltpu.SemaphoreType.DMA((ncopies,)),
                   *[jax.ShapeDtypeStruct(t.shape, t.dtype) for t in arrays]),
        in_specs=[HBM] * n, out_specs=(SEM, SEM, *([HBM] * n)),
        input_output_aliases={i: i + 2 for i in range(n)},
        compiler_params=pltpu.CompilerParams(has_side_effects=EFFECT),
    )(*[_hbm(t) for t in arrays])
    return outs[0], outs[1], list(outs[2:])


def _wait_copies(name, send, recv, arrays, make, after):
    n = len(arrays)

    def body(*refs):
        for cp in make(refs[:n], refs[n], refs[n + 1]):
            cp.wait_send()
            cp.wait_recv()

    outs = pl.pallas_call(
        body, name=name, out_shape=[jax.ShapeDtypeStruct(t.shape, t.dtype) for t in arrays],
        in_specs=[HBM] * n + [SEM, SEM, ANY], out_specs=[HBM] * n,
        input_output_aliases={i: i for i in range(n)},
        compiler_params=pltpu.CompilerParams(has_side_effects=EFFECT),
    )(*arrays, send, recv, after)
    return list(outs)


def _w_in_copies(relations):
    def make(refs, send, recv):
        x, y, c = _coords()
        me = 2 * x + y
        chips = _other_chips(x, y)
        w, conv = refs[0], refs[1]
        cps = []
        for i, k in enumerate(relations):
            cps.append(_remote(_column_half(w, 0, c), _column_half(w, 1 + k, c), send, recv, 2 * i, (*chips[k], c)))
            mine = _shard(conv, "col", me)
            cps.append(_remote(mine, mine, send, recv, 2 * i + 1, (*chips[k], c)))
        return cps
    return make


def _column_half(w, slot, c):
    half = w.shape[1] // 8
    return w.at[:, pl.ds((2 * slot + c) * half, half)]


def _w_in_forward(relations):
    def make(refs, send, recv):
        x, y, c = _coords()
        cps = []
        for i, k in enumerate(relations):
            got = _column_half(refs[0], 1 + k, c)
            cps.append(_remote(got, got, send, recv, i, (x, y, 1 - c)))
        return cps
    return make


def _sibling_columns(c0, width):
    def make(refs, send, recv):
        x, y, c = _coords()
        cols = pl.ds(c0, width)
        return [_remote(refs[0].at[:, cols], refs[1].at[:, cols], send, recv, 0, (x, y, 1 - c))]
    return make


def _other_weight_copies(refs, send, recv):
    x, y, c = _coords()
    me = 2 * x + y
    cps = []
    for k, chip in enumerate(_other_chips(x, y)):
        for p, (_, kind) in enumerate(BIG[1:]):
            mine = _piece(refs[p], kind, me, c)
            cps.append(_remote(mine, mine, send, recv, 3 * p + k, (*chip, c)))
    return cps


def _other_weight_forward(refs, send, recv):
    x, y, c = _coords()
    cps = []
    for k, chip in enumerate(_other_chips(x, y)):
        s = 2 * chip[0] + chip[1]
        for p, (_, kind) in enumerate(BIG[1:]):
            got = _piece(refs[p], kind, s, c)
            cps.append(_remote(got, got, send, recv, 3 * p + k, (x, y, 1 - c)))
    return cps


def _share_copies(group):
    def make(refs, send, recv):
        x, y, c = _coords()
        cps = []
        for p, (_, kind) in enumerate(group):
            mine = _half(refs[p], kind, c)
            cps.append(_remote(mine, mine, send, recv, p, (x, y, 1 - c)))
        return cps
    return make


def _sibling_halves_start(G, group, carry, tag):
    n = len(group)
    parts = [G[name] for name, _ in group]
    lands = []
    for (_, kind), g in zip(group, parts):
        R, C = g.shape
        lands.append(lax.empty((R // 2, C) if kind == "col" else (R, C // 2), g.dtype))

    def make(refs, send, recv):
        x, y, c = _coords()
        return [_remote(_half(refs[p], group[p][1], 1 - c), refs[n + p], send, recv, p, (x, y, 1 - c)) for p in range(n)]

    send, recv, thru = _start_copies("sibling_halves_start_" + tag, [*parts, *lands, carry], n, make)
    return (send, recv, thru[:2 * n], make, tag), thru[2 * n]


def _presums(state, group, pos, after):
    send, recv, arrays, make, tag = state
    n = len(group)
    thru = _wait_copies("sibling_halves_wait_" + tag, send, recv, arrays, make, after)
    return [_presum(thru[p], thru[n + p], kind, pos, "presum_" + name) for p, (name, kind) in enumerate(group)]


def _presum(g, got, kind, pos, name):
    R, C = got.shape
    tr, tc = _tile(R, 512, 16), _tile(C, 2048)
    nr, nc = R // tr, C // tc

    def body(pos_ref, a_ref, b_ref, o_ref):
        o_ref[...] = (a_ref[...].astype(F32) + b_ref[...].astype(F32)).astype(o_ref.dtype)

    blk = pl.BlockSpec((tr, tc), lambda i, j, pos_ref: (i, j))
    if kind == "col":
        mine = pl.BlockSpec((tr, tc), lambda i, j, pos_ref: (pos_ref[1] * nr + i, j))
    else:
        mine = pl.BlockSpec((tr, tc), lambda i, j, pos_ref: (i, pos_ref[1] * nc + j))
    spec = pltpu.PrefetchScalarGridSpec(num_scalar_prefetch=1, grid=(nr, nc), in_specs=[mine, blk], out_specs=blk)
    return _call(body, name=name, grid_spec=spec, out_shape=jax.ShapeDtypeStruct((R, C), WIRE_DTYPE))(pos, g, got)


def _chip_copies(group):
    n = len(group)

    def make(refs, send, recv):
        x, y, c = _coords()
        cps = []
        for k, chip in enumerate(_other_chips(x, y)):
            s = 2 * chip[0] + chip[1]
            for p in range(n):
                cps.append(_remote(_shard(refs[p], group[p][1], s), refs[n + p].at[k], send, recv, 3 * p + k, (*chip, c)))
        return cps
    return make


def _landing_zones(pres, group):
    lands = []
    for (_, kind), g in zip(group, pres):
        R, C = g.shape
        lands.append(lax.empty((3, R, C // 4) if kind == "col" else (3, R // 4, C), g.dtype))
    return lands


def _exchange_start(pres, group, carry, tag):
    n = len(group)
    make = _chip_copies(group)
    send, recv, thru = _start_copies("chip_exchange_start_" + tag, [*pres, *_landing_zones(pres, group), carry], 3 * n, make)
    return (send, recv, thru[:2 * n], make, tag), thru[2 * n]


def _exchange_wait(state, after):
    send, recv, arrays, make, tag = state
    thru = _wait_copies("chip_exchange_wait_" + tag, send, recv, arrays, make, after)
    n = len(thru) // 2
    return thru[:n], thru[n:]


def _reduce_into_shard(slots, pre, kind, pos, name):
    K, R, C = slots.shape
    tr, tc = _tile(R, 512, 16), _tile(C, 2176)
    nr, nc = R // tr, C // tc

    def body(pos_ref, s_ref, p_ref, o_ref):
        acc = p_ref[...].astype(F32)
        for k in range(K):
            acc = acc + s_ref[k].astype(F32)
        o_ref[...] = acc

    if kind == "col":
        own = pl.BlockSpec((tr, tc), lambda i, j, pos_ref: (i, pos_ref[0] * nc + j))
        full, out = (2 * R, C), pl.BlockSpec((tr, tc), lambda i, j, pos_ref: (pos_ref[1] * nr + i, j))
    else:
        own = pl.BlockSpec((tr, tc), lambda i, j, pos_ref: (pos_ref[0] * nr + i, j))
        full, out = (R, 2 * C), pl.BlockSpec((tr, tc), lambda i, j, pos_ref: (i, pos_ref[1] * nc + j))
    spec = pltpu.PrefetchScalarGridSpec(
        num_scalar_prefetch=1, grid=(nr, nc),
        in_specs=[pl.BlockSpec((K, tr, tc), lambda i, j, pos_ref: (0, i, j)), own], out_specs=out)
    return _call(body, name=name, grid_spec=spec, out_shape=jax.ShapeDtypeStruct(full, F32))(pos, slots, pre)


def _small_slots(pack, me):
    _, N = pack.shape

    def body(me_ref, p_ref, o_ref):
        o_ref[0] = p_ref[...]

    spec = pltpu.PrefetchScalarGridSpec(
        num_scalar_prefetch=1, grid=(1,), in_specs=[pl.BlockSpec((1, N), lambda i, me_ref: (0, 0))],
        out_specs=pl.BlockSpec((1, 1, N), lambda i, me_ref: (me_ref[0], 0, 0)))
    return _call(body, name="small_slots", grid_spec=spec, out_shape=jax.ShapeDtypeStruct((8, 1, N), pack.dtype))(me, pack)


def _small_copies(refs, send, recv):
    x, y, c = _coords()
    me = 4 * x + 2 * y + c
    cps = []
    for k in range(1, 8):
        dev = (x ^ (k >> 2), y ^ ((k >> 1) & 1), c ^ (k & 1))
        cps.append(_remote(refs[0], refs[1].at[me], send, recv, k - 1, dev))
    return cps


def _sum_small(slots):
    K, _, N = slots.shape

    def body(s_ref, o_ref):
        acc = s_ref[0]
        for k in range(1, K):
            acc = acc + s_ref[k]
        o_ref[...] = acc

    return _call(body, name="sum_small", in_specs=[pl.BlockSpec(memory_space=pltpu.VMEM)],
                 out_specs=pl.BlockSpec(memory_space=pltpu.VMEM), out_shape=jax.ShapeDtypeStruct((1, N), F32))(slots)


def _adamw(w, g, m, v, name, with_grad=False):
    R, C = w.shape
    tr, tc = _tile(R, 256, 8), _tile(C, 2176)

    def body(w_ref, g_ref, m_ref, v_ref, d_ref, nm_ref, nv_ref, *g_out):
        gv = g_ref[...]
        for ref in g_out:
            ref[...] = gv
        nm = ADAM_B1 * m_ref[...] + (1.0 - ADAM_B1) * gv
        nv = ADAM_B2 * v_ref[...] + (1.0 - ADAM_B2) * gv * gv
        m_hat = nm / (1.0 - ADAM_B1 ** ADAM_STEP)
        v_hat = nv / (1.0 - ADAM_B2 ** ADAM_STEP)
        d_ref[...] = -ADAM_LR * (m_hat / (jnp.sqrt(v_hat) + ADAM_EPS) + ADAM_WD * w_ref[...])
        nm_ref[...] = nm
        nv_ref[...] = nv

    spec = pl.BlockSpec((tr, tc), lambda i, j: (i, j))
    shp = jax.ShapeDtypeStruct((R, C), F32)
    nout = 4 if with_grad else 3
    return _call(body, name=name, grid=(R // tr, C // tc), in_specs=[spec] * 4, out_specs=[spec] * nout,
                 out_shape=[shp] * nout)(w, g, m, v)


SMALL = ("norm_g", "mem_norm_g", "attn_q_norm", "attn_k_norm", "conv_w", "mem_q_norm", "mem_k_norm")
WEIGHTS = ("norm_g", "mem_norm_g", "w_in", "attn_q_norm", "attn_k_norm", "conv_w", "mem_w_kv", "mem_q_norm",
           "mem_k_norm", "w_br_attn", "w_br_conv", "w_br_mem", "w_out")


def kernel(x, mem, norm_g, mem_norm_g, w_in, attn_q_norm, attn_k_norm, conv_w, mem_w_kv, mem_q_norm, mem_k_norm, w_br_attn, w_br_conv, w_br_mem, w_out, loss_target, m_norm_g, m_mem_norm_g, m_w_in, m_attn_q_norm, m_attn_k_norm, m_conv_w, m_mem_w_kv, m_mem_q_norm, m_mem_k_norm, m_w_br_attn, m_w_br_conv, m_w_br_mem, m_w_out, v_norm_g, v_mem_norm_g, v_w_in, v_attn_q_norm, v_attn_k_norm, v_conv_w, v_mem_w_kv, v_mem_q_norm, v_mem_k_norm, v_w_br_attn, v_w_br_conv, v_w_br_mem, v_w_out):
    w = dict(norm_g=norm_g, mem_norm_g=mem_norm_g, w_in=w_in, attn_q_norm=attn_q_norm, attn_k_norm=attn_k_norm,
             conv_w=conv_w, mem_w_kv=mem_w_kv, mem_q_norm=mem_q_norm, mem_k_norm=mem_k_norm, w_br_attn=w_br_attn,
             w_br_conv=w_br_conv, w_br_mem=w_br_mem, w_out=w_out)
    m = dict(norm_g=m_norm_g, mem_norm_g=m_mem_norm_g, w_in=m_w_in, attn_q_norm=m_attn_q_norm,
             attn_k_norm=m_attn_k_norm, conv_w=m_conv_w, mem_w_kv=m_mem_w_kv, mem_q_norm=m_mem_q_norm,
             mem_k_norm=m_mem_k_norm, w_br_attn=m_w_br_attn, w_br_conv=m_w_br_conv, w_br_mem=m_w_br_mem, w_out=m_w_out)
    v = dict(norm_g=v_norm_g, mem_norm_g=v_mem_norm_g, w_in=v_w_in, attn_q_norm=v_attn_q_norm,
             attn_k_norm=v_attn_k_norm, conv_w=v_conv_w, mem_w_kv=v_mem_w_kv, mem_q_norm=v_mem_q_norm,
             mem_k_norm=v_mem_k_norm, w_br_attn=v_w_br_attn, w_br_conv=v_w_br_conv, w_br_mem=v_w_br_mem, w_out=v_w_out)
    Bl, _, D = x.shape
    cx, cy = lax.axis_index("x"), lax.axis_index("y")
    chip = 2 * cx + cy
    pos = jnp.stack([chip, lax.axis_index("c")]).astype(jnp.int32)
    order = jnp.stack([chip] + [2 * a + b for a, b in _other_chips(cx, cy)]).astype(jnp.int32)
    n = len(BIG)

    slot0 = jnp.stack([jnp.zeros((), jnp.int32), pos[1]])
    w_rel = _place_shard(w["w_in"], "col", slot0, WIRE_DTYPE, "place_w_in_sent", half=0)
    conv_full = _place_shard(conv_w, "col", pos, F32, "place_conv_w")
    others = [_place_shard(w[name], kind, pos, WIRE_DTYPE, "place_" + name) for name, kind in BIG[1:]]
    hb, hbt, mhb = _norms(x, mem, norm_g, mem_norm_g)

    meta = jnp.concatenate([order, pos[1:]])
    near, near_fwd = _w_in_copies((0, 1)), _w_in_forward((0, 1))
    send, recv, (w_rel, conv_full) = _start_copies("gather_near_start", [w_rel, conv_full], 4, near)
    w_rel = _place_shard(w["w_in"], "col", slot0, WIRE_DTYPE, "place_w_in_kept", half=1, into=w_rel)
    proj = _proj_chunk(hb, w_rel, meta, 0, 1, None, None, "proj_own")
    w_rel, conv_full, *others = _wait_copies("gather_near_wait", send, recv, [w_rel, conv_full, *others], near, proj)

    fsend, frecv, (w_rel,) = _start_copies("gather_near_forward_start", [w_rel], 2, near_fwd)
    far, far_fwd = _w_in_copies((2,)), _w_in_forward((2,))
    send, recv, (w_rel, conv_full) = _start_copies("gather_far_start", [w_rel, conv_full], 2, far)
    proj = _proj_chunk(hb, w_rel, meta, 1, 2, 0, proj, "proj_near_landed")
    w_rel, = _wait_copies("gather_near_forward_wait", fsend, frecv, [w_rel], near_fwd, proj)
    proj = _proj_chunk(hb, w_rel, meta, 1, 2, 1, proj, "proj_near_forwarded")
    w_rel, conv_full = _wait_copies("gather_far_wait", send, recv, [w_rel, conv_full], far, proj)

    fsend, frecv, (w_rel,) = _start_copies("gather_far_forward_start", [w_rel], 1, far_fwd)
    send, recv, (*others, w_rel) = _start_copies("gather_rest_start", [*others, w_rel], 3 * (n - 1), _other_weight_copies)
    proj = _proj_chunk(hb, w_rel, meta, 3, 1, 0, proj, "proj_far_landed")
    w_rel, = _wait_copies("gather_far_forward_wait", fsend, frecv, [w_rel], far_fwd, proj)
    proj = _proj_chunk(hb, w_rel, meta, 3, 1, 1, proj, "proj_far_forwarded")
    os, ls, a = _attention_fwd(proj, Bl, attn_q_norm, attn_k_norm)
    *others, w_rel = _wait_copies("gather_rest_wait", send, recv, [*others, w_rel], _other_weight_copies, a[0])
    fsend, frecv, (*others, proj) = _start_copies("gather_rest_forward_start", [*others, proj], 3 * (n - 1),
                                                  _other_weight_forward)
    cc = _conv_branch_fwd(proj, Bl, conv_full)
    others = _wait_copies("gather_rest_forward_wait", fsend, frecv, others, _other_weight_forward, cc[0])
    W = {name: others[p] for p, (name, _) in enumerate(BIG[1:])}

    def rest_halves(G, carry):
        return _sibling_halves_start(G, BIG[1:], carry, "rest")

    def rest_exchange(state, after, carry):
        return _exchange_start(_presums(state, BIG[1:], pos, after), BIG[1:], carry, "rest")

    G, rest, rest_state = _weight_grads(
        x, mem, loss_target, norm_g, mem_norm_g, attn_q_norm, attn_k_norm, conv_full, mem_q_norm, mem_k_norm, W,
        (hb, hbt, mhb, proj, os, ls, a, cc), early=(rest_halves, rest_exchange))

    tn = _tile(rest[0].shape[1], DW_IN_TILE)
    ntiles = rest[0].shape[1] // tn
    first = ntiles - ntiles // 2
    early_cols = _sibling_columns(0, first * tn)
    late_cols = _sibling_columns(first * tn, (ntiles - first) * tn)
    for_sibling = _dw_in_half(hbt, rest[0], pos, False, "dw_in_sibling_early", tiles=(0, first))
    send, recv, (for_sibling, got, dproj) = _start_copies(
        "sibling_w_in_start_early", [for_sibling, lax.empty(for_sibling.shape, for_sibling.dtype), rest[0]], 1, early_cols)
    for_sibling = _dw_in_half(hbt, dproj, pos, False, "dw_in_sibling_late", tiles=(first, ntiles - first),
                              into=for_sibling)
    lsend, lrecv, (for_sibling, got) = _start_copies("sibling_w_in_start_late", [for_sibling, got], 1, late_cols)
    pres_rest, slots_rest = _exchange_wait(rest_state, for_sibling)
    reds_rest = [_reduce_into_shard(slots_rest[p], pres_rest[p], kind, pos, "reduce_" + name)
                 for p, (name, kind) in enumerate(BIG[1:])]
    share_rest = _share_copies(BIG[1:])
    rsend, rrecv, reds_rest = _start_copies("share_rest_start", reds_rest, n - 1, share_rest)
    for_sibling, got = _wait_copies("sibling_w_in_wait_early", send, recv, [for_sibling, got], early_cols, reds_rest[0])
    for_sibling, got = _wait_copies("sibling_w_in_wait_late", lsend, lrecv, [for_sibling, got], late_cols, reds_rest[0])
    pre_w_in = _dw_in_half(hbt, dproj, pos, True, "dw_in_own", add=got)

    w_in_state, dproj = _exchange_start([pre_w_in], BIG[:1], dproj, "w_in")
    grad_x, small = _input_grad((dproj, *rest[1:]), w_rel, order)
    pres, slots = _exchange_wait(w_in_state, grad_x)
    red_w_in = _reduce_into_shard(slots[0], pres[0], "col", pos, "reduce_w_in")
    share_w_in = _share_copies(BIG[:1])
    wsend, wrecv, (red_w_in, small) = _start_copies("share_w_in_start", [red_w_in, small], 1, share_w_in)
    grad_x = grad_x.reshape(x.shape)

    slots = _small_slots(small, (2 * pos[:1] + pos[1:]))
    ssend, srecv, (small, slots) = _start_copies("gather_small_start", [small, slots], 7, _small_copies)
    reds_rest = _wait_copies("share_rest_wait", rsend, rrecv, reds_rest, share_rest, slots)
    grads = dict(zip([name for name, _ in BIG[1:]], reds_rest))
    delta, new_m, new_v = {}, {}, {}
    for name, _ in BIG[1:]:
        delta[name], new_m[name], new_v[name], grads[name] = _adamw(w[name], grads[name], m[name], v[name],
                                                                    "adamw_" + name, with_grad=True)

    small, slots = _wait_copies("gather_small_wait", ssend, srecv, [small, slots], _small_copies, delta[BIG[-1][0]])
    tot = _sum_small(slots)[0]
    loss = tot[0]
    off = 128
    for name, size in (("norm_g", D), ("mem_norm_g", D), ("attn_q_norm", NGROUP * HEAD), ("attn_k_norm", NGROUP * HEAD),
                       ("conv_w", 3 * CONVW), ("mem_q_norm", MEM_HD), ("mem_k_norm", MEM_HD)):
        grads[name] = tot[off:off + size]
        off += size
    cw = conv_w.shape[1]
    grads["conv_w"] = lax.dynamic_slice(grads["conv_w"].reshape(3, CONVW), (0, chip * cw), (3, cw))
    for name in SMALL:
        grads[name] = grads[name].reshape(w[name].shape)

    def packed(t):
        return jnp.concatenate([t[name].reshape(1, -1) for name in SMALL], axis=1)

    ds, ms, vs = _adamw(packed(w), packed(grads), packed(m), packed(v), "adamw_small")
    shared, = _wait_copies("share_w_in_wait", wsend, wrecv, [red_w_in], share_w_in, ds)
    delta["w_in"], new_m["w_in"], new_v["w_in"], grads["w_in"] = _adamw(w["w_in"], shared, m["w_in"], v["w_in"],
                                                                        "adamw_w_in", with_grad=True)
    off = 0
    for name in SMALL:
        size = w[name].size
        delta[name] = ds[0, off:off + size].reshape(w[name].shape)
        new_m[name] = ms[0, off:off + size].reshape(w[name].shape)
        new_v[name] = vs[0, off:off + size].reshape(w[name].shape)
        off += size

    return (loss, grad_x, *[grads[n] for n in WEIGHTS], *[delta[n] for n in WEIGHTS],
            *[new_m[n] for n in WEIGHTS], *[new_v[n] for n in WEIGHTS])
```

```python
import functools

import jax
import jax.numpy as jnp
from jax import lax
from jax.experimental import pallas as pl
from jax.experimental.pallas import tpu as pltpu

F32 = jnp.float32
MXU_DTYPE = jnp.bfloat16
WIRE_DTYPE = jnp.bfloat16
PROJ_DTYPE = jnp.bfloat16
EPS = 1e-6
NEG = -1e30

HEAD = 128
HPG = 4
GW = HPG * HEAD
DILATIONS = (1, 4, 16)
NGROUP = len(DILATIONS)
BLK = 128
QKV = NGROUP * GW
CONVW = 1024
MEM_HEADS = 4
MEM_HD = 256
MEMW = MEM_HEADS * MEM_HD
Q0, K0, V0 = 0, QKV, 2 * QKV
ZA = 3 * QKV
CB, CC, CV, ZC = ZA + GW, ZA + GW + CONVW, ZA + GW + 2 * CONVW, ZA + GW + 3 * CONVW
MQ = ZC + CONVW
ZM = MQ + MEMW
G0 = ZM + MEMW

ADAM_LR, ADAM_B1, ADAM_B2, ADAM_EPS, ADAM_WD, ADAM_STEP = 0.001, 0.9, 0.999, 1e-08, 0.01, 10

VMEM_LIMIT = 56 * 1024 * 1024
MESH = pl.DeviceIdType.MESH
ANY = pl.BlockSpec(memory_space=pl.ANY)


def _tile(n, pref, mult=128):
    t = min(pref, n)
    while t > mult and (n % t or t % mult):
        t -= mult
    assert n % t == 0, (n, pref)
    return t


def _call(body, *, name, out_shape, grid=(), in_specs=None, out_specs=None, scratch_shapes=(),
          aliases=None, grid_spec=None):
    kw = {}
    if grid_spec is not None:
        kw["grid_spec"] = grid_spec
        ngrid = len(grid_spec.grid)
    else:
        kw.update(grid=grid, in_specs=in_specs, out_specs=out_specs, scratch_shapes=list(scratch_shapes))
        ngrid = len(grid)
    params = pltpu.CompilerParams(dimension_semantics=("arbitrary",) * ngrid, vmem_limit_bytes=VMEM_LIMIT)
    return pl.pallas_call(body, name=name, out_shape=out_shape, compiler_params=params,
                          input_output_aliases=aliases or {}, **kw)


_DIMS = {"nn": (((1,), (0,)), ((), ())), "nt": (((1,), (1,)), ((), ())), "tn": (((0,), (0,)), ((), ()))}


def _mxu(a, b, mode):
    return lax.dot_general(a.astype(MXU_DTYPE), b.astype(MXU_DTYPE), _DIMS[mode], preferred_element_type=F32)


@functools.partial(jax.custom_vjp, nondiff_argnums=(2,))
def _dot(a, b, mode):
    return _mxu(a, b, mode)


def _dot_fwd(a, b, mode):
    return _mxu(a, b, mode), (a, b)


def _dot_bwd(mode, res, g):
    a, b = res
    if mode == "nn":
        return _mxu(g, b, "nt"), _mxu(a, g, "tn")
    if mode == "nt":
        return _mxu(g, b, "nn"), _mxu(g, a, "tn")
    return _mxu(b, g, "nt"), _mxu(a, g, "nn")


_dot.defvjp(_dot_fwd, _dot_bwd)


def _sig(z):
    return 0.5 * jnp.tanh(0.5 * z) + 0.5


def _silu(z):
    return z * _sig(z)


def _rms_rows(t, g):
    return t * lax.rsqrt(jnp.mean(t * t, axis=-1, keepdims=True) + EPS) * g


def _attn_block(q, k2, v2, gq, gk, first):
    qn = _rms_rows(q, gq)
    kn = _rms_rows(k2, gk)
    s = jnp.where(_band_mask(first, k2.shape[0]), _mxu(qn, kn, "nt") * (HEAD ** -0.5), NEG)
    m = jnp.max(s, axis=-1, keepdims=True)
    p = jnp.exp(s - m)
    den = jnp.sum(p, axis=-1, keepdims=True)
    o = _mxu(p, v2, "nn") / den
    return o, m + jnp.log(den)


def _band_mask(first, nkeys):
    a = lax.broadcasted_iota(jnp.int32, (BLK, nkeys), 0)
    b = lax.broadcasted_iota(jnp.int32, (BLK, nkeys), 1)
    if nkeys == BLK:
        return b <= a
    return (b >= a) & (b <= a + BLK) & (b >= jnp.where(first, BLK, 0))


def _norm_parts(t):
    r = lax.rsqrt(jnp.mean(t * t, axis=-1, keepdims=True) + EPS)
    return r, t * r


def _norm_bwd(dn, g, r, th):
    dth = dn * g
    return r * (dth - th * jnp.mean(dth * th, axis=-1, keepdims=True)), jnp.sum(dn * th, axis=0, keepdims=True)


def _attn_block_bwd(q, k2, v2, gq, gk, first, do, o, lse, dlse):
    scale = HEAD ** -0.5
    rq, qh = _norm_parts(q)
    rk, kh = _norm_parts(k2)
    qn, kn = qh * gq, kh * gk
    s = jnp.where(_band_mask(first, k2.shape[0]), _mxu(qn, kn, "nt") * scale, NEG)
    p = jnp.exp(s - lse)
    ds = p * (_mxu(do, v2, "nt") + (dlse - jnp.sum(do * o, axis=-1, keepdims=True))) * scale
    dq, dgq = _norm_bwd(_mxu(ds, kn, "nn"), gq, rq, qh)
    dk2, dgk = _norm_bwd(_mxu(ds, qn, "tn"), gk, rk, kh)
    return dq, dk2, _mxu(p, do, "tn"), dgq, dgk


def _combine(o1, o2, o3, l1, l2, l3, z):
    m = lax.stop_gradient(jnp.maximum(jnp.maximum(l1, l2), l3))
    e1, e2, e3 = jnp.exp(l1 - m), jnp.exp(l2 - m), jnp.exp(l3 - m)
    return (e1 * o1 + e2 * o2 + e3 * o3) / (e1 + e2 + e3) * _silu(z)


def _mem_block(q, z, kv, gq, gk):
    outs = []
    for h in range(MEM_HEADS):
        sl = slice(h * MEM_HD, (h + 1) * MEM_HD)
        qn = _rms_rows(q[:, sl], gq)
        kn = _rms_rows(kv[:, sl], gk)
        s = _dot(qn, kn, "nt") * (MEM_HD ** -0.5)
        m = lax.stop_gradient(jnp.max(s, axis=-1, keepdims=True))
        p = jnp.exp(s - m)
        den = jnp.sum(p, axis=-1, keepdims=True)
        outs.append(_dot(p, kv[:, MEMW + h * MEM_HD:MEMW + (h + 1) * MEM_HD], "nn") / den)
    return jnp.concatenate(outs, axis=-1) * _silu(z)


def _cast(w, name):
    R, C = w.shape
    tr, tc = _tile(R, 512, 8), _tile(C, 2176)

    def body(w_ref, o_ref):
        o_ref[...] = w_ref[...].astype(o_ref.dtype)

    spec = pl.BlockSpec((tr, tc), lambda i, j: (i, j))
    return _call(body, name=name, grid=(R // tr, C // tc), in_specs=[spec], out_specs=spec,
                 out_shape=jax.ShapeDtypeStruct((R, C), WIRE_DTYPE))(w)


def _place_shard(w, kind, pos, dtype, name, slot=0, into=None, half=None):
    R, C = w.shape
    tr, tc = _tile(R, 512, 8), _tile(C if half is None else C // 2, 2176)
    nr, nc = R // tr, C // tc
    ncols = nc if half is None else nc // 2

    def body(pos_ref, w_ref, *rest):
        rest[-1][...] = w_ref[...].astype(rest[-1].dtype)

    def col(j, pos_ref):
        if half is None:
            return j
        return (pos_ref[1] if half == 0 else 1 - pos_ref[1]) * ncols + j

    if kind == "col":
        full = (R, 4 * C)
        out = pl.BlockSpec((tr, tc), lambda i, j, pos_ref: (i, pos_ref[slot] * nc + col(j, pos_ref)))
    else:
        full, out = (4 * R, C), pl.BlockSpec((tr, tc), lambda i, j, pos_ref: (pos_ref[slot] * nr + i, j))
    in_specs, args = [pl.BlockSpec((tr, tc), lambda i, j, pos_ref: (i, col(j, pos_ref)))], [pos, w]
    if into is not None:
        in_specs.append(ANY)
        args.append(into)
    spec = pltpu.PrefetchScalarGridSpec(num_scalar_prefetch=1, grid=(nr, ncols), in_specs=in_specs, out_specs=out)
    return _call(body, name=name, grid_spec=spec, out_shape=jax.ShapeDtypeStruct(full, dtype),
                 aliases={} if into is None else {2: 0})(*args)


def _matmul(a, b, mode, out_dtype, *, name, tm=512, tn=512, tk=512):
    if mode == "nn":
        (M, K), (_, N) = a.shape, b.shape
    elif mode == "nt":
        (M, K), (N, _) = a.shape, b.shape
    else:
        (K, M), (_, N) = a.shape, b.shape
    tm, tn, tk = _tile(M, tm), _tile(N, tn), _tile(K, tk)
    nk = K // tk

    def body(a_ref, b_ref, o_ref, *acc):
        part = lax.dot_general(a_ref[...], b_ref[...], _DIMS[mode], preferred_element_type=F32)
        if nk == 1:
            o_ref[...] = part.astype(o_ref.dtype)
            return
        acc_ref, = acc
        k = pl.program_id(2)

        @pl.when(k == 0)
        def _():
            acc_ref[...] = part

        @pl.when(k > 0)
        def _():
            acc_ref[...] += part

        @pl.when(k == nk - 1)
        def _():
            o_ref[...] = acc_ref[...].astype(o_ref.dtype)

    a_spec = pl.BlockSpec((tk, tm), lambda i, j, k: (k, i)) if mode == "tn" else pl.BlockSpec((tm, tk), lambda i, j, k: (i, k))
    b_spec = pl.BlockSpec((tn, tk), lambda i, j, k: (j, k)) if mode == "nt" else pl.BlockSpec((tk, tn), lambda i, j, k: (k, j))
    return _call(body, name=name, grid=(M // tm, N // tn, nk), in_specs=[a_spec, b_spec],
                 out_specs=pl.BlockSpec((tm, tn), lambda i, j, k: (i, j)),
                 out_shape=jax.ShapeDtypeStruct((M, N), out_dtype),
                 scratch_shapes=[] if nk == 1 else [pltpu.VMEM((tm, tn), F32)])(a, b)


def _rms_fwd(x, g, name):
    R, D = x.shape
    tr = _tile(R, 512)

    def body(x_ref, g_ref, o_ref, t_ref):
        y = _rms_rows(x_ref[...], g_ref[...])
        o_ref[...] = y.astype(o_ref.dtype)
        t_ref[...] = y.T.astype(t_ref.dtype)

    row = pl.BlockSpec((tr, D), lambda i: (i, 0))
    return _call(body, name=name, grid=(R // tr,), in_specs=[row, pl.BlockSpec((1, D), lambda i: (0, 0))],
                 out_specs=[row, pl.BlockSpec((D, tr), lambda i: (0, i))],
                 out_shape=[jax.ShapeDtypeStruct((R, D), MXU_DTYPE), jax.ShapeDtypeStruct((D, R), MXU_DTYPE)])(x, g)


def _rms_bwd(x, dh, g, dy, name):
    R, D = x.shape
    tr = _tile(R, 256)
    with_dx = dy is not None

    def body(*refs):
        if with_dx:
            x_ref, dh_ref, g_ref, dy_ref, dx_ref, dg_ref = refs
        else:
            x_ref, dh_ref, g_ref, dg_ref = refs
        xv, dhv = x_ref[...], dh_ref[...]
        r = lax.rsqrt(jnp.mean(xv * xv, axis=-1, keepdims=True) + EPS)
        xh = xv * r

        @pl.when(pl.program_id(0) == 0)
        def _():
            dg_ref[...] = jnp.zeros_like(dg_ref)

        dg_ref[...] += jnp.sum(dhv * xh, axis=0, keepdims=True)
        if with_dx:
            dxh = dhv * g_ref[...]
            dx_ref[...] = dy_ref[...] + r * (dxh - xh * jnp.mean(dxh * xh, axis=-1, keepdims=True))

    row = pl.BlockSpec((tr, D), lambda i: (i, 0))
    vec = pl.BlockSpec((1, D), lambda i: (0, 0))
    dg_shape = jax.ShapeDtypeStruct((1, D), F32)
    if with_dx:
        return _call(body, name=name, grid=(R // tr,), in_specs=[row, row, vec, row], out_specs=[row, vec],
                     out_shape=[jax.ShapeDtypeStruct((R, D), F32), dg_shape])(x, dh, g, dy)
    return None, _call(body, name=name, grid=(R // tr,), in_specs=[row, row, vec], out_specs=vec,
                       out_shape=dg_shape)(x, dh, g)


def _attn_geom(g, d):
    hc = HPG if d == 1 else 1
    cw = hc * HEAD
    cq, ck, cv = (Q0 + g * GW) // cw, (K0 + g * GW) // cw, (V0 + g * GW) // cw
    return (1, BLK * d, cw), hc, HPG // hc, cq, ck, cv


def _rows(ref, r, d, sl):
    if d == 1:
        return ref[0, :, sl]
    return ref.at[0][pl.ds(r, BLK, stride=d), sl]


def _set_rows(ref, r, d, sl, val):
    if d == 1:
        ref[0, :, sl] = val
    else:
        ref.at[0][pl.ds(r, BLK, stride=d), sl] = val


def _stage_rows(ref, r, d, sl, val):
    if d == 1:
        ref[:, sl] = val
    else:
        ref[pl.ds(r, BLK, stride=d), sl] = val


def _proj_stages(blk, d):
    return [] if d == 1 else [pltpu.VMEM(blk[1:], F32)] * 5


def _proj_rows(refs, stages, d):
    if d == 1:
        return [lambda r, sl, ref=ref: ref[0, :, sl].astype(F32) for ref in refs]
    for ref, stage in zip(refs, stages):
        stage[...] = ref[0].astype(F32)
    return [lambda r, sl, stage=stage: stage[pl.ds(r, BLK, stride=d), sl] for stage in stages]


def _attn_fwd(proj3, gq, gk, g, d):
    Bl, S, _ = proj3.shape
    blk, hc, ncb, cq, ck, cv = _attn_geom(g, d)
    nb = S // blk[1]
    if nb == 1:
        return _attn_single_fwd(proj3, gq, gk, g, d)

    def body(q_ref, kp_ref, kc_ref, vp_ref, vc_ref, gq_ref, gk_ref, o_ref, lse_ref, *stages):
        first = pl.program_id(2) == 0
        q, kp, kc, vp, vc = _proj_rows((q_ref, kp_ref, kc_ref, vp_ref, vc_ref), stages, d)
        def run(alone):
            for r in range(d):
                for h in range(hc):
                    sl = slice(h * HEAD, (h + 1) * HEAD)
                    if alone:
                        k2, v2 = kc(r, sl), vc(r, sl)
                    else:
                        k2 = jnp.concatenate([kp(r, sl), kc(r, sl)], axis=0)
                        v2 = jnp.concatenate([vp(r, sl), vc(r, sl)], axis=0)
                    o, lse = _attn_block(q(r, sl), k2, v2, gq_ref[...], gk_ref[...], False)
                    _set_rows(o_ref, r, d, sl, o)
                    _set_rows(lse_ref, r, d, sl, jnp.broadcast_to(lse, (BLK, HEAD)))

        pl.when(first)(lambda: run(True))
        pl.when(jnp.logical_not(first))(lambda: run(False))

    def cur(c0):
        return pl.BlockSpec(blk, lambda b, j, i: (b, i, c0 + j))

    def prev(c0):
        return pl.BlockSpec(blk, lambda b, j, i: (b, jnp.maximum(i - 1, 0), c0 + j))

    vec = pl.BlockSpec((1, HEAD), lambda b, j, i: (0, 0))
    out = pl.BlockSpec(blk, lambda b, j, i: (b, i, j))
    shp = jax.ShapeDtypeStruct((Bl, S, GW), F32)
    return _call(body, name=f"attn_fwd_g{g}", grid=(Bl, ncb, nb),
                 in_specs=[cur(cq), prev(ck), cur(ck), prev(cv), cur(cv), vec, vec],
                 out_specs=[out, out], out_shape=[shp, shp], scratch_shapes=_proj_stages(blk, d),
                 )(proj3, proj3, proj3, proj3, proj3, gq, gk)


def _attn_single_fwd(proj3, gq, gk, g, d):
    Bl, S, _ = proj3.shape
    blk, hc, ncb, cq, ck, cv = _attn_geom(g, d)

    def body(q_ref, k_ref, v_ref, gq_ref, gk_ref, o_ref, lse_ref, *stages):
        q, k, v = _proj_rows((q_ref, k_ref, v_ref), stages, d)
        for r in range(d):
            for h in range(hc):
                sl = slice(h * HEAD, (h + 1) * HEAD)
                o, lse = _attn_block(q(r, sl), k(r, sl), v(r, sl), gq_ref[...], gk_ref[...], True)
                _set_rows(o_ref, r, d, sl, o)
                _set_rows(lse_ref, r, d, sl, jnp.broadcast_to(lse, (BLK, HEAD)))

    def at(c0):
        return pl.BlockSpec(blk, lambda b, j: (b, 0, c0 + j))

    vec = pl.BlockSpec((1, HEAD), lambda b, j: (0, 0))
    shp = jax.ShapeDtypeStruct((Bl, S, GW), F32)
    return _call(body, name=f"attn_fwd_g{g}", grid=(Bl, ncb), in_specs=[at(cq), at(ck), at(cv), vec, vec],
                 out_specs=[at(0), at(0)], out_shape=[shp, shp], scratch_shapes=_proj_stages(blk, d)[:3],
                 )(proj3, proj3, proj3, gq, gk)


def _attn_single_bwd(proj3, gq, gk, o3, l3, do3, dl3, g, d):
    Bl, S, _ = proj3.shape
    blk, hc, ncb, cq, ck, cv = _attn_geom(g, d)

    def body(q_ref, k_ref, v_ref, gq_ref, gk_ref, o_ref, l_ref, do_ref, dl_ref,
             dq_ref, dk_ref, dv_ref, dgq_ref, dgk_ref, sq_ref, sk_ref, sv_ref, *stages):
        @pl.when((pl.program_id(0) == 0) & (pl.program_id(1) == 0))
        def _():
            dgq_ref[...] = jnp.zeros_like(dgq_ref)
            dgk_ref[...] = jnp.zeros_like(dgk_ref)

        dgq, dgk = jnp.zeros((1, HEAD), F32), jnp.zeros((1, HEAD), F32)
        q, k, v = _proj_rows((q_ref, k_ref, v_ref), stages, d)
        for r in range(d):
            for h in range(hc):
                sl = slice(h * HEAD, (h + 1) * HEAD)
                dq, dk, dv, a, b = _attn_block_bwd(
                    q(r, sl), k(r, sl), v(r, sl), gq_ref[...], gk_ref[...], True, _rows(do_ref, r, d, sl),
                    _rows(o_ref, r, d, sl), _rows(l_ref, r, d, sl)[:, :1], _rows(dl_ref, r, d, sl)[:, :1])
                _stage_rows(sq_ref, r, d, sl, dq)
                _stage_rows(sk_ref, r, d, sl, dk)
                _stage_rows(sv_ref, r, d, sl, dv)
                dgq, dgk = dgq + a, dgk + b
        dgq_ref[...] += dgq
        dgk_ref[...] += dgk
        dq_ref[0] = sq_ref[...].astype(dq_ref.dtype)
        dk_ref[0] = sk_ref[...].astype(dk_ref.dtype)
        dv_ref[0] = sv_ref[...].astype(dv_ref.dtype)

    def at(c0):
        return pl.BlockSpec(blk, lambda b, j: (b, 0, c0 + j))

    vec = pl.BlockSpec((1, HEAD), lambda b, j: (0, 0))
    shp = jax.ShapeDtypeStruct((Bl, S, GW), MXU_DTYPE)
    gshp = jax.ShapeDtypeStruct((1, HEAD), F32)
    return _call(body, name=f"attn_bwd_g{g}", grid=(Bl, ncb),
                 in_specs=[at(cq), at(ck), at(cv), vec, vec, at(0), at(0), at(0), at(0)],
                 out_specs=[at(0), at(0), at(0), vec, vec], out_shape=[shp, shp, shp, gshp, gshp],
                 scratch_shapes=[pltpu.VMEM(blk[1:], F32)] * 3 + _proj_stages(blk, d)[:3],
                 )(proj3, proj3, proj3, gq, gk, o3, l3, do3, dl3)


def _attn_bwd(proj3, gq, gk, o3, l3, do3, dl3, g, d):
    Bl, S, _ = proj3.shape
    blk, hc, ncb, cq, ck, cv = _attn_geom(g, d)
    nb = S // blk[1]
    if nb == 1:
        return _attn_single_bwd(proj3, gq, gk, o3, l3, do3, dl3, g, d)

    def body(q_ref, kp_ref, kc_ref, vp_ref, vc_ref, gq_ref, gk_ref, o_ref, l_ref, do_ref, dl_ref,
             dq_ref, dk_ref, dv_ref, dgq_ref, dgk_ref, ck_ref, cv_ref, sq_ref, sk_ref, sv_ref, *stages):
        i = pl.program_id(2)
        first = i == 0

        @pl.when((pl.program_id(0) == 0) & (pl.program_id(1) == 0) & first)
        def _():
            dgq_ref[...] = jnp.zeros_like(dgq_ref)
            dgk_ref[...] = jnp.zeros_like(dgk_ref)

        def run(alone):
            dgq, dgk = jnp.zeros((1, HEAD), F32), jnp.zeros((1, HEAD), F32)
            q, kp, kc, vp, vc = _proj_rows((q_ref, kp_ref, kc_ref, vp_ref, vc_ref), stages, d)
            for r in range(d):
                rs = slice(r * BLK, (r + 1) * BLK)
                for h in range(hc):
                    sl = slice(h * HEAD, (h + 1) * HEAD)
                    if alone:
                        k2, v2 = kc(r, sl), vc(r, sl)
                    else:
                        k2 = jnp.concatenate([kp(r, sl), kc(r, sl)], axis=0)
                        v2 = jnp.concatenate([vp(r, sl), vc(r, sl)], axis=0)
                    dq, dk2, dv2, a, b = _attn_block_bwd(
                        q(r, sl), k2, v2, gq_ref[...], gk_ref[...], False, _rows(do_ref, r, d, sl),
                        _rows(o_ref, r, d, sl), _rows(l_ref, r, d, sl)[:, :1], _rows(dl_ref, r, d, sl)[:, :1])
                    _stage_rows(sq_ref, r, d, sl, dq)
                    if alone:
                        _stage_rows(sk_ref, r, d, sl, jnp.zeros((BLK, HEAD), F32))
                        _stage_rows(sv_ref, r, d, sl, jnp.zeros((BLK, HEAD), F32))
                    else:
                        _stage_rows(sk_ref, r, d, sl, ck_ref[rs, sl] + dk2[:BLK])
                        _stage_rows(sv_ref, r, d, sl, cv_ref[rs, sl] + dv2[:BLK])
                    ck_ref[rs, sl] = dk2[-BLK:]
                    cv_ref[rs, sl] = dv2[-BLK:]
                    dgq, dgk = dgq + a, dgk + b
            dgq_ref[...] += dgq
            dgk_ref[...] += dgk
            dq_ref[0] = sq_ref[...].astype(dq_ref.dtype)

        pl.when(first)(lambda: run(True))
        pl.when((i > 0) & (i < nb))(lambda: run(False))

        @pl.when(i == nb)
        def _():
            for r in range(d):
                rs = slice(r * BLK, (r + 1) * BLK)
                _stage_rows(sk_ref, r, d, slice(None), ck_ref[rs, :])
                _stage_rows(sv_ref, r, d, slice(None), cv_ref[rs, :])

        dk_ref[0] = sk_ref[...].astype(dk_ref.dtype)
        dv_ref[0] = sv_ref[...].astype(dv_ref.dtype)

    def cur(c0):
        return pl.BlockSpec(blk, lambda b, j, i: (b, jnp.minimum(i, nb - 1), c0 + j))

    def prev(c0):
        return pl.BlockSpec(blk, lambda b, j, i: (b, jnp.clip(i - 1, 0, nb - 1), c0 + j))

    vec = pl.BlockSpec((1, HEAD), lambda b, j, i: (0, 0))
    at_q = pl.BlockSpec(blk, lambda b, j, i: (b, jnp.minimum(i, nb - 1), j))
    at_k = pl.BlockSpec(blk, lambda b, j, i: (b, jnp.maximum(i - 1, 0), j))
    shp = jax.ShapeDtypeStruct((Bl, S, GW), MXU_DTYPE)
    gshp = jax.ShapeDtypeStruct((1, HEAD), F32)
    return _call(body, name=f"attn_bwd_g{g}", grid=(Bl, ncb, nb + 1),
                 in_specs=[cur(cq), prev(ck), cur(ck), prev(cv), cur(cv), vec, vec, at_q, at_q, at_q, at_q],
                 out_specs=[at_q, at_k, at_k, vec, vec], out_shape=[shp, shp, shp, gshp, gshp],
                 scratch_shapes=[pltpu.VMEM(blk[1:], F32)] * 5 + _proj_stages(blk, d),
                 )(proj3, proj3, proj3, proj3, proj3, gq, gk, o3, l3, do3, dl3)


def _combine_fwd(os, ls, proj2):
    T = proj2.shape[0]
    tr = _tile(T, 512)

    def body(o1, o2, o3, l1, l2, l3, z, a_ref, at_ref):
        a = _combine(o1[...], o2[...], o3[...], l1[...], l2[...], l3[...], z[...].astype(F32))
        a_ref[...] = a.astype(a_ref.dtype)
        at_ref[...] = a.T.astype(at_ref.dtype)

    row = pl.BlockSpec((tr, GW), lambda i: (i, 0))
    return _call(body, name="combine_fwd", grid=(T // tr,),
                 in_specs=[row] * 6 + [pl.BlockSpec((tr, GW), lambda i: (i, ZA // GW))],
                 out_specs=[row, pl.BlockSpec((GW, tr), lambda i: (0, i))],
                 out_shape=[jax.ShapeDtypeStruct((T, GW), MXU_DTYPE), jax.ShapeDtypeStruct((GW, T), MXU_DTYPE)],
                 )(*os, *ls, proj2)


def _combine_bwd(os, ls, proj2, da, dproj):
    T = proj2.shape[0]
    tr = _tile(T, 256)

    def body(o1, o2, o3, l1, l2, l3, z, da_ref, _, d1, d2, d3, e1, e2, e3, dz_ref):
        _, vjp = jax.vjp(_combine, o1[...], o2[...], o3[...], l1[...], l2[...], l3[...], z[...].astype(F32))
        go1, go2, go3, gl1, gl2, gl3, gz = vjp(da_ref[...])
        d1[...], d2[...], d3[...] = go1, go2, go3
        dz_ref[...] = gz.astype(dz_ref.dtype)
        for ref, gl in ((e1, gl1), (e2, gl2), (e3, gl3)):
            for h in range(HPG):
                sl = slice(h * HEAD, (h + 1) * HEAD)
                ref[:, sl] = jnp.broadcast_to(jnp.sum(gl[:, sl], axis=-1, keepdims=True), (tr, HEAD))

    row = pl.BlockSpec((tr, GW), lambda i: (i, 0))
    f = jax.ShapeDtypeStruct((T, GW), F32)
    z_attn = pl.BlockSpec((tr, GW), lambda i: (i, ZA // GW))
    outs = _call(body, name="combine_bwd", grid=(T // tr,), in_specs=[row] * 6 + [z_attn, row, ANY],
                 out_specs=[row] * 6 + [z_attn], out_shape=[f] * 6 + [jax.ShapeDtypeStruct(dproj.shape, dproj.dtype)],
                 aliases={8: 6})(*os, *ls, proj2, da, dproj)
    return outs[:3], outs[3:6], outs[6]


def _shift_down(u, j, t):
    return jnp.where(t >= j, pltpu.roll(u, j, 0), 0.0)


def _shift_up(u, j, t):
    n = u.shape[0]
    return jnp.where(t < n - j, pltpu.roll(u, n - j, 0), 0.0)


def _conv_specs(Bl, S, cw):
    def sec(c0):
        return pl.BlockSpec((1, S, cw), lambda j, b: (b, 0, c0 // cw + j))
    return [sec(CB), sec(CC), sec(CV), sec(ZC)], pl.BlockSpec((3, cw), lambda j, b: (0, j))


def _conv_fwd(proj3, conv_w):
    Bl, S, _ = proj3.shape
    cw = 256
    secs, wspec = _conv_specs(Bl, S, cw)

    def body(b_ref, c_ref, v_ref, z_ref, w_ref, o_ref, ot_ref):
        t = lax.broadcasted_iota(jnp.int32, (S, cw), 0)
        u = c_ref[0].astype(F32) * v_ref[0].astype(F32)
        y = w_ref[0:1, :] * u + w_ref[1:2, :] * _shift_down(u, 1, t) + w_ref[2:3, :] * _shift_down(u, 2, t)
        out = b_ref[0].astype(F32) * y * _silu(z_ref[0].astype(F32))
        o_ref[0] = out.astype(o_ref.dtype)
        ot_ref[...] = out.T.astype(ot_ref.dtype)

    return _call(body, name="conv_fwd", grid=(CONVW // cw, Bl), in_specs=secs + [wspec],
                 out_specs=[pl.BlockSpec((1, S, cw), lambda j, b: (b, 0, j)), pl.BlockSpec((cw, S), lambda j, b: (j, b))],
                 out_shape=[jax.ShapeDtypeStruct((Bl, S, CONVW), MXU_DTYPE),
                            jax.ShapeDtypeStruct((CONVW, Bl * S), MXU_DTYPE)])(proj3, proj3, proj3, proj3, conv_w)


def _conv_bwd(proj3, conv_w, dcc3, dproj):
    Bl, S, _ = proj3.shape
    cw = 256
    secs, wspec = _conv_specs(Bl, S, cw)

    def body(b_ref, c_ref, v_ref, z_ref, w_ref, d_ref, _, dproj_ref, dw_ref, stage, sems):
        t = lax.broadcasted_iota(jnp.int32, (S, cw), 0)
        bv, cv, vv, zv = (r[0].astype(F32) for r in (b_ref, c_ref, v_ref, z_ref))
        dv = d_ref[0]
        u = cv * vv
        u1, u2 = _shift_down(u, 1, t), _shift_down(u, 2, t)
        y = w_ref[0:1, :] * u + w_ref[1:2, :] * u1 + w_ref[2:3, :] * u2
        sg = _sig(zv)
        sz = zv * sg
        gy = dv * bv * sz
        du = w_ref[0:1, :] * gy + w_ref[1:2, :] * _shift_up(gy, 1, t) + w_ref[2:3, :] * _shift_up(gy, 2, t)
        j, b = pl.program_id(0), pl.program_id(1)
        tiles = [dv * y * sz, du * vv, du * cv, dv * bv * y * sg * (1.0 + zv * (1.0 - sg))]
        dsts = [dproj_ref.at[pl.ds(b * S, S), pl.ds(c0 + j * cw, cw)] for c0 in (CB, CC, CV, ZC)]
        _emit_tiles(j * Bl + b, (CONVW // cw) * Bl, tiles, dsts, stage, sems)

        @pl.when(pl.program_id(1) == 0)
        def _():
            dw_ref[...] = jnp.zeros_like(dw_ref)

        dw_ref[0:1, :] += jnp.sum(gy * u, axis=0, keepdims=True)
        dw_ref[1:2, :] += jnp.sum(gy * u1, axis=0, keepdims=True)
        dw_ref[2:3, :] += jnp.sum(gy * u2, axis=0, keepdims=True)

    blk = pl.BlockSpec((1, S, cw), lambda j, b: (b, 0, j))
    return _call(body, name="conv_bwd", grid=(CONVW // cw, Bl), in_specs=secs + [wspec, blk, ANY],
                 out_specs=[ANY, wspec],
                 out_shape=[jax.ShapeDtypeStruct(dproj.shape, dproj.dtype), jax.ShapeDtypeStruct((3, CONVW), F32)],
                 scratch_shapes=_emit_scratch(4, S, cw), aliases={6: 0})(proj3, proj3, proj3, proj3, conv_w, dcc3, dproj)


def _mem_specs(S, tq):
    q = pl.BlockSpec((1, tq, MEMW), lambda b, j: (b, j, MQ // MEMW))
    z = pl.BlockSpec((1, tq, MEMW), lambda b, j: (b, j, ZM // MEMW))
    kv = pl.BlockSpec((1, MEM_HD, 2 * MEMW), lambda b, j: (b, 0, 0))
    vec = pl.BlockSpec((1, MEM_HD), lambda b, j: (0, 0))
    blk = pl.BlockSpec((1, tq, MEMW), lambda b, j: (b, j, 0))
    return q, z, kv, vec, blk


def _mem_fwd(proj3, mkv3, gq, gk):
    Bl, S, _ = proj3.shape
    tq = _tile(S, 512)
    q, z, kv, vec, blk = _mem_specs(S, tq)

    def body(q_ref, z_ref, kv_ref, gq_ref, gk_ref, o_ref, ot_ref):
        out = _mem_block(q_ref[0].astype(F32), z_ref[0].astype(F32), kv_ref[0], gq_ref[...], gk_ref[...])
        o_ref[0] = out.astype(o_ref.dtype)
        ot_ref[...] = out.T.astype(ot_ref.dtype)

    nq = S // tq
    return _call(body, name="mem_fwd", grid=(Bl, nq), in_specs=[q, z, kv, vec, vec],
                 out_specs=[blk, pl.BlockSpec((MEMW, tq), lambda b, j: (0, b * nq + j))],
                 out_shape=[jax.ShapeDtypeStruct((Bl, S, MEMW), MXU_DTYPE),
                            jax.ShapeDtypeStruct((MEMW, Bl * S), MXU_DTYPE)])(proj3, proj3, mkv3, gq, gk)


def _mem_bwd(proj3, mkv3, gq, gk, dmo3, dproj):
    Bl, S, _ = proj3.shape
    tq = _tile(S, 256)
    q, z, kv, vec, blk = _mem_specs(S, tq)
    nq = S // tq

    def body(q_ref, z_ref, kv_ref, gq_ref, gk_ref, d_ref, _, dproj_ref, dkv_ref, dgq_ref, dgk_ref, stage, sems):
        _, vjp = jax.vjp(_mem_block, q_ref[0].astype(F32), z_ref[0].astype(F32), kv_ref[0], gq_ref[...], gk_ref[...])
        dq, dz, dkv, dgq, dgk = vjp(d_ref[0])
        j = pl.program_id(1)
        rows = pl.ds(pl.program_id(0) * S + j * tq, tq)
        dsts = [dproj_ref.at[rows, pl.ds(MQ, MEMW)], dproj_ref.at[rows, pl.ds(ZM, MEMW)]]
        _emit_tiles(pl.program_id(0) * nq + j, Bl * nq, [dq, dz], dsts, stage, sems)

        @pl.when(j == 0)
        def _():
            dkv_ref[0] = jnp.zeros_like(dkv)

        @pl.when((j == 0) & (pl.program_id(0) == 0))
        def _():
            dgq_ref[...] = jnp.zeros_like(dgq_ref)
            dgk_ref[...] = jnp.zeros_like(dgk_ref)

        dkv_ref[0] += dkv
        dgq_ref[...] += dgq
        dgk_ref[...] += dgk

    gshp = jax.ShapeDtypeStruct((1, MEM_HD), F32)
    return _call(body, name="mem_bwd", grid=(Bl, nq), in_specs=[q, z, kv, vec, vec, blk, ANY],
                 out_specs=[ANY, kv, vec, vec],
                 out_shape=[jax.ShapeDtypeStruct(dproj.shape, dproj.dtype), jax.ShapeDtypeStruct(mkv3.shape, F32),
                            gshp, gshp],
                 scratch_shapes=_emit_scratch(2, tq, MEMW), aliases={6: 0})(proj3, proj3, mkv3, gq, gk, dmo3, dproj)


def _merge_specs(T, D, tm, tn):
    def act(w):
        return pl.BlockSpec((tm, w), lambda i, n: (i, 0))

    def wsp(w):
        return pl.BlockSpec((w, tn), lambda i, n: (0, n))

    gates = [pl.BlockSpec((tm, tn), lambda i, n, k=k: (i, (G0 + k * D) // tn + n)) for k in range(3)]
    tile = pl.BlockSpec((tm, tn), lambda i, n: (i, n))
    return act, wsp, gates, tile


def _merge_fwd(a, cc, mo, wa, wc, wm, proj2):
    T, D = a.shape[0], wa.shape[1]
    tm, tn = _tile(T, 1024), _tile(D, 512)
    act, wsp, gates, tile = _merge_specs(T, D, tm, tn)

    def body(a_ref, c_ref, m_ref, wa_ref, wc_ref, wm_ref, g0, g1, g2, mg_ref, mt_ref, pa_ref, pc_ref, pm_ref):
        pa = jnp.dot(a_ref[...], wa_ref[...], preferred_element_type=F32)
        pc = jnp.dot(c_ref[...], wc_ref[...], preferred_element_type=F32)
        pm = jnp.dot(m_ref[...], wm_ref[...], preferred_element_type=F32)
        mg = _sig(g0[...].astype(F32)) * pa + _sig(g1[...].astype(F32)) * pc + _sig(g2[...].astype(F32)) * pm
        mg_ref[...] = mg.astype(mg_ref.dtype)
        mt_ref[...] = mg.T.astype(mt_ref.dtype)
        pa_ref[...] = pa.astype(pa_ref.dtype)
        pc_ref[...] = pc.astype(pc_ref.dtype)
        pm_ref[...] = pm.astype(pm_ref.dtype)

    shp = jax.ShapeDtypeStruct((T, D), MXU_DTYPE)
    return _call(body, name="merge_fwd", grid=(T // tm, D // tn),
                 in_specs=[act(GW), act(CONVW), act(MEMW), wsp(GW), wsp(CONVW), wsp(MEMW)] + gates,
                 out_specs=[tile, pl.BlockSpec((tn, tm), lambda i, n: (n, i)), tile, tile, tile],
                 out_shape=[shp, jax.ShapeDtypeStruct((D, T), MXU_DTYPE), shp, shp, shp],
                 )(a, cc, mo, wa, wc, wm, proj2, proj2, proj2)


def _emit_tiles(step, nsteps, tiles, dsts, stage, sems):
    slot = step % 2

    def copies(s):
        return [pltpu.make_async_copy(stage.at[s, k], dsts[k], sems.at[s, k]) for k in range(len(tiles))]

    @pl.when(step >= 2)
    def _():
        for cp in copies(slot):
            cp.wait()

    for k, t in enumerate(tiles):
        stage[slot, k] = t.astype(stage.dtype)
    for cp in copies(slot):
        cp.start()

    @pl.when(step == nsteps - 1)
    def _():
        for cp in copies(slot):
            cp.wait()
        if nsteps > 1:
            for cp in copies(1 - slot):
                cp.wait()


def _emit_scratch(k, rows, cols):
    return [pltpu.VMEM((2, k, rows, cols), MXU_DTYPE), pltpu.SemaphoreType.DMA((2, k))]


def _merge_bwd(dyb, w_out, proj2, pa, pc, pm):
    T, D = dyb.shape
    IN = proj2.shape[1]
    tm, tn = _tile(T, 1024), _tile(D, 512)
    _, _, gates, tile = _merge_specs(T, D, tm, tn)
    nn = D // tn

    def body(dy_ref, w_ref, g0, g1, g2, p0, p1, p2, dp0, dp1, dp2, dproj_ref, stage, sems):
        i, n = pl.program_id(0), pl.program_id(1)
        dm = lax.dot_general(dy_ref[...], w_ref[...], _DIMS["nt"], preferred_element_type=F32)
        tiles, dsts = [], []
        for k, (g_ref, p_ref, dp_ref) in enumerate(((g0, p0, dp0), (g1, p1, dp1), (g2, p2, dp2))):
            gt = _sig(g_ref[...].astype(F32))
            dp_ref[...] = (gt * dm).astype(dp_ref.dtype)
            tiles.append(dm * p_ref[...].astype(F32) * gt * (1.0 - gt))
            dsts.append(dproj_ref.at[pl.ds(i * tm, tm), pl.ds(G0 + k * D + n * tn, tn)])
        _emit_tiles(i * nn + n, (T // tm) * nn, tiles, dsts, stage, sems)

    shp = jax.ShapeDtypeStruct((T, D), MXU_DTYPE)
    return _call(body, name="merge_bwd", grid=(T // tm, nn),
                 in_specs=[pl.BlockSpec((tm, D), lambda i, n: (i, 0)), pl.BlockSpec((tn, D), lambda i, n: (n, 0))]
                 + gates + [tile] * 3,
                 out_specs=[tile] * 3 + [ANY], out_shape=[shp] * 3 + [jax.ShapeDtypeStruct((T, IN), MXU_DTYPE)],
                 scratch_shapes=_emit_scratch(3, tm, tn))(dyb, w_out, proj2, proj2, proj2, pa, pc, pm)


def _out_loss(merged, w_out, x, tgt):
    T, D = x.shape
    tm = _tile(T, 512)

    def body(m_ref, w_ref, x_ref, t_ref, dy_ref, dyb_ref, loss_ref):
        err = x_ref[...] + jnp.dot(m_ref[...], w_ref[...], preferred_element_type=F32) - t_ref[...]
        dy = err * (1.0 / D)
        dy_ref[...] = dy
        dyb_ref[...] = dy.astype(dyb_ref.dtype)

        @pl.when(pl.program_id(0) == 0)
        def _():
            loss_ref[...] = jnp.zeros_like(loss_ref)

        loss_ref[...] += jnp.sum(err * err) * (0.5 / D)

    row = pl.BlockSpec((tm, D), lambda i: (i, 0))
    return _call(body, name="out_loss", grid=(T // tm,),
                 in_specs=[row, pl.BlockSpec((D, D), lambda i: (0, 0)), row, row],
                 out_specs=[row, row, pl.BlockSpec((1, 128), lambda i: (0, 0))],
                 out_shape=[jax.ShapeDtypeStruct((T, D), F32), jax.ShapeDtypeStruct((T, D), MXU_DTYPE),
                            jax.ShapeDtypeStruct((1, 128), F32)])(merged, w_out, x, tgt)


def _proj_chunk(hb, w, meta, j, nslots, half, buf, name):
    T, D = hb.shape
    Cs = w.shape[1] // 4
    tm, tn = _tile(T, 1024), _tile(Cs // 2, 2176)
    nh = Cs // 2 // tn
    per = nh if half is not None else 2 * nh

    def body(meta_ref, a_ref, b_ref, *rest):
        rest[-1][...] = jnp.dot(a_ref[...], b_ref[...], preferred_element_type=F32).astype(rest[-1].dtype)

    def tile(n, m):
        if half is None:
            return n % per
        return (m[4] if half == 0 else 1 - m[4]) * nh + n % per

    in_specs = [pl.BlockSpec((tm, D), lambda n, i, m: (i, 0)),
                pl.BlockSpec((D, tn), lambda n, i, m: (0, (j + n // per) * 2 * nh + tile(n, m)))]
    args = [meta, hb, w]
    if buf is not None:
        in_specs.append(ANY)
        args.append(buf)
    spec = pltpu.PrefetchScalarGridSpec(
        num_scalar_prefetch=1, grid=(nslots * per, T // tm), in_specs=in_specs,
        out_specs=pl.BlockSpec((tm, tn), lambda n, i, m: (i, m[j + n // per] * 2 * nh + tile(n, m))))
    return _call(body, name=name, grid_spec=spec, out_shape=jax.ShapeDtypeStruct((T, 4 * Cs), PROJ_DTYPE),
                 aliases={} if buf is None else {3: 0})(*args)


def _norms(x, mem, norm_g, mem_norm_g):
    D = x.shape[-1]
    hb, hbt = _rms_fwd(x.reshape(-1, D), norm_g.reshape(1, D), "rms_x")
    mhb, _ = _rms_fwd(mem.reshape(-1, D), mem_norm_g.reshape(1, D), "rms_mem")
    return hb, hbt, mhb


def _attention_fwd(proj2, Bl, gq_all, gk_all):
    T, IN = proj2.shape
    proj3 = proj2.reshape(Bl, T // Bl, IN)
    os, ls = [], []
    for g, d in enumerate(DILATIONS):
        o, l = _attn_fwd(proj3, gq_all[g:g + 1], gk_all[g:g + 1], g, d)
        os.append(o.reshape(T, GW))
        ls.append(l.reshape(T, GW))
    return os, ls, _combine_fwd(os, ls, proj2)


def _conv_branch_fwd(proj2, Bl, conv_w):
    T, IN = proj2.shape
    cc, cct = _conv_fwd(proj2.reshape(Bl, T // Bl, IN), conv_w)
    return cc.reshape(T, CONVW), cct


def _weight_grads(x, mem, tgt, norm_g, mem_norm_g, gq_all, gk_all, conv_w, mem_gq, mem_gk, W, pre, early=None):
    Bl, S, D = x.shape
    T = Bl * S
    hb, hbt, mhb, proj2, os, ls, (a, at), (cc, cct) = pre
    IN = proj2.shape[1]
    proj3 = proj2.reshape(Bl, S, IN)
    x2, tgt2 = x.reshape(T, D), tgt.reshape(T, D)
    mem2 = mem.reshape(-1, D)
    ng, mng = norm_g.reshape(1, D), mem_norm_g.reshape(1, D)
    mgq, mgk = mem_gq.reshape(1, MEM_HD), mem_gk.reshape(1, MEM_HD)
    gqs = [gq_all[g:g + 1] for g in range(NGROUP)]
    gks = [gk_all[g:g + 1] for g in range(NGROUP)]

    mkv = _matmul(mhb, W["mem_w_kv"], "nn", F32, name="mem_kv", tm=512, tn=1024, tk=D)
    mkv3 = mkv.reshape(Bl, -1, 2 * MEMW)
    mo, mot = _mem_fwd(proj3, mkv3, mgq, mgk)
    mo = mo.reshape(T, MEMW)
    merged, mergedt, pa, pc, pm = _merge_fwd(a, cc, mo, W["w_br_attn"], W["w_br_conv"], W["w_br_mem"], proj2)
    dy, dyb, loss = _out_loss(merged, W["w_out"], x2, tgt2)

    G = {}
    G["w_out"] = _matmul(mergedt, dyb, "nn", WIRE_DTYPE, name="dw_out", tm=1024, tn=512, tk=T)
    dpa, dpc, dpm, dproj = _merge_bwd(dyb, W["w_out"], proj2, pa, pc, pm)
    G["w_br_attn"] = _matmul(at, dpa, "nn", WIRE_DTYPE, name="dw_br_attn", tm=512, tn=512, tk=T)
    G["w_br_conv"] = _matmul(cct, dpc, "nn", WIRE_DTYPE, name="dw_br_conv", tm=1024, tn=512, tk=T)
    G["w_br_mem"] = _matmul(mot, dpm, "nn", WIRE_DTYPE, name="dw_br_mem", tm=1024, tn=512, tk=T)
    da = _matmul(dpa, W["w_br_attn"], "nt", F32, name="d_attn", tm=1024, tn=512, tk=D)
    dcc = _matmul(dpc, W["w_br_conv"], "nt", F32, name="d_conv", tm=1024, tn=1024, tk=D)
    dmo = _matmul(dpm, W["w_br_mem"], "nt", F32, name="d_mem", tm=1024, tn=1024, tk=D)
    dproj, dmkv3, dmgq, dmgk = _mem_bwd(proj3, mkv3, mgq, mgk, dmo.reshape(Bl, S, MEMW), dproj)
    dmkv = _cast(dmkv3.reshape(-1, 2 * MEMW), "cast_dmkv")
    G["mem_w_kv"] = _matmul(mhb, dmkv, "tn", WIRE_DTYPE, name="dw_mem_kv", tm=1024, tn=1024, tk=512)
    early_state, dmkv = (None, dmkv) if early is None else early[0](G, dmkv)
    dmh = _matmul(dmkv, W["mem_w_kv"], "nt", F32, name="d_memh", tm=512, tn=1024, tk=2 * MEMW)
    _, dmng = _rms_bwd(mem2, dmh, mng, None, "rms_mem_bwd")
    if early is not None:
        early_state, da = early[1](early_state, dmng, da)

    dos, dls, dproj = _combine_bwd(os, ls, proj2, da, dproj)
    dgq, dgk = [], []
    for g, d in enumerate(DILATIONS):
        dq, dk, dv, gq_g, gk_g = _attn_bwd(proj3, gqs[g], gks[g], os[g].reshape(Bl, S, GW), ls[g].reshape(Bl, S, GW),
                                           dos[g].reshape(Bl, S, GW), dls[g].reshape(Bl, S, GW), g, d)
        for c0, part in ((Q0, dq), (K0, dk), (V0, dv)):
            dproj = lax.dynamic_update_slice(dproj, part.reshape(T, GW), (0, c0 + g * GW))
        dgq.append(gq_g)
        dgk.append(gk_g)
    dproj, dconv_w = _conv_bwd(proj3, conv_w, dcc.reshape(Bl, S, CONVW), dproj)
    small = [loss, None, dmng] + dgq + dgk + [dconv_w.reshape(1, 3 * CONVW), dmgq, dmgk]
    return G, (dproj, x2, ng, dy, small), early_state


DW_IN_TILE = 1024


def _dw_in_half(hbt, dproj, pos, own, name, add=None, tiles=None, into=None):
    D, T = hbt.shape
    IN = dproj.shape[1]
    R, tn = D // 2, _tile(IN, DW_IN_TILE)
    j0, nj = (0, IN // tn) if tiles is None else tiles

    def body(pos_ref, a_ref, b_ref, *rest):
        acc = jnp.dot(a_ref[...], b_ref[...], preferred_element_type=F32)
        if add is not None:
            acc = acc + rest[0][...].astype(F32)
        rest[-1][...] = acc.astype(rest[-1].dtype)

    tile = pl.BlockSpec((R, tn), lambda j, p: (0, j0 + j))
    in_specs = [pl.BlockSpec((R, T), lambda j, p: (p[1] if own else 1 - p[1], 0)),
                pl.BlockSpec((T, tn), lambda j, p: (0, j0 + j))]
    args = [pos, hbt, dproj]
    if add is not None:
        in_specs.append(tile)
        args.append(add)
    if into is not None:
        in_specs.append(ANY)
        args.append(into)
    spec = pltpu.PrefetchScalarGridSpec(num_scalar_prefetch=1, grid=(nj,), in_specs=in_specs, out_specs=tile)
    return _call(body, name=name, grid_spec=spec, out_shape=jax.ShapeDtypeStruct((R, IN), WIRE_DTYPE),
                 aliases={} if into is None else {len(args) - 1: 0})(*args)


def _d_h(dproj, w, order):
    T, IN = dproj.shape
    D, Cs = w.shape[0], IN // 4
    tm, tn = _tile(T, 1024), _tile(D, 1024)

    def body(order_ref, a_ref, b_ref, o_ref, acc_ref):
        part = lax.dot_general(a_ref[...], b_ref[...], _DIMS["nt"], preferred_element_type=F32)
        k = pl.program_id(2)

        @pl.when(k == 0)
        def _():
            acc_ref[...] = part

        @pl.when(k > 0)
        def _():
            acc_ref[...] += part

        @pl.when(k == 3)
        def _():
            o_ref[...] = acc_ref[...]

    spec = pltpu.PrefetchScalarGridSpec(
        num_scalar_prefetch=1, grid=(T // tm, D // tn, 4),
        in_specs=[pl.BlockSpec((tm, Cs), lambda i, n, k, o: (i, o[k])), pl.BlockSpec((tn, Cs), lambda i, n, k, o: (n, k))],
        out_specs=pl.BlockSpec((tm, tn), lambda i, n, k, o: (i, n)), scratch_shapes=[pltpu.VMEM((tm, tn), F32)])
    return _call(body, name="d_h", grid_spec=spec, out_shape=jax.ShapeDtypeStruct((T, D), F32))(order, dproj, w)


def _input_grad(rest, w_in, order):
    dproj, x2, ng, dy, small = rest
    dh = _d_h(dproj, w_in, order)
    grad_x, dng = _rms_bwd(x2, dh, ng, dy, "rms_x_bwd")
    small = [dng if t is None else t for t in small]
    return grad_x, jnp.concatenate(small, axis=1)


def _local_step(x, mem, tgt, norm_g, mem_norm_g, gq_all, gk_all, conv_w, mem_gq, mem_gk, W):
    hb, hbt, mhb = _norms(x, mem, norm_g, mem_norm_g)
    Cs = W["w_in"].shape[1] // 4
    shards = (0, 2, 1, 3)
    order = jnp.array(shards, dtype=jnp.int32)
    w_rel = jnp.concatenate([W["w_in"][:, s * Cs:(s + 1) * Cs] for s in shards], axis=1)
    meta = jnp.array(shards + (0,), dtype=jnp.int32)
    proj2 = _proj_chunk(hb, w_rel, meta, 0, 1, None, None, "proj_0")
    for j, nslots in ((1, 2), (3, 1)):
        for half in (1, 0):
            proj2 = _proj_chunk(hb, w_rel, meta, j, nslots, half, proj2, f"proj_{j}_{half}")
    pre = (hb, hbt, mhb, proj2, *_attention_fwd(proj2, x.shape[0], gq_all, gk_all),
           _conv_branch_fwd(proj2, x.shape[0], conv_w))
    G, rest, _ = _weight_grads(x, mem, tgt, norm_g, mem_norm_g, gq_all, gk_all, conv_w, mem_gq, mem_gk, W, pre)
    pos = jnp.zeros((2,), jnp.int32)
    ntiles = rest[0].shape[1] // _tile(rest[0].shape[1], DW_IN_TILE)
    other = _dw_in_half(hbt, rest[0], pos, False, "dw_in_sibling_early", tiles=(0, ntiles - ntiles // 2))
    other = _dw_in_half(hbt, rest[0], pos, False, "dw_in_sibling_late", tiles=(ntiles - ntiles // 2, ntiles // 2),
                        into=other)
    G["w_in"] = jnp.concatenate([_dw_in_half(hbt, rest[0], pos, True, "dw_in_own"), other], axis=0)
    grad_x, small = _input_grad(rest, w_rel, order)
    return grad_x.reshape(x.shape), G, small


BIG = (("w_in", "col"), ("mem_w_kv", "row"), ("w_br_attn", "col"), ("w_br_conv", "col"),
       ("w_br_mem", "col"), ("w_out", "row"))


def _coords():
    return lax.axis_index("x"), lax.axis_index("y"), lax.axis_index("c")


def _other_chips(x, y):
    return [(1 - x, y), (x, 1 - y), (1 - x, 1 - y)]


def _half(ref, kind, c):
    R, C = ref.shape
    if kind == "col":
        return ref.at[pl.ds(c * (R // 2), R // 2), :]
    return ref.at[:, pl.ds(c * (C // 2), C // 2)]


def _shard(ref, kind, s):
    R, C = ref.shape
    if kind == "col":
        return ref.at[:, pl.ds(s * (C // 4), C // 4)]
    return ref.at[pl.ds(s * (R // 4), R // 4), :]


def _piece(ref, kind, s, c):
    R, C = ref.shape
    if kind == "col":
        return ref.at[pl.ds(c * (R // 2), R // 2), pl.ds(s * (C // 4), C // 4)]
    return ref.at[pl.ds(s * (R // 4), R // 4), pl.ds(c * (C // 2), C // 2)]


def _remote(src, dst, sems_s, sems_r, k, dev):
    return pltpu.make_async_remote_copy(src_ref=src, dst_ref=dst, send_sem=sems_s.at[k], recv_sem=sems_r.at[k],
                                        device_id=dev, device_id_type=MESH)


HBM = pl.BlockSpec(memory_space=pltpu.HBM)
SEM = pl.BlockSpec(memory_space=pltpu.SEMAPHORE)
EFFECT = pltpu.SideEffectType.DATAFLOW_SIDE_EFFECTING


def _hbm(a):
    return pltpu.with_memory_space_constraint(a, pltpu.HBM)


def _start_copies(name, arrays, ncopies, make):
    n = len(arrays)

    def body(*refs):
        for cp in make(refs[:n], refs[n], refs[n + 1]):
            cp.start()

    outs = pl.pallas_call(
        body, name=name,
        out_shape=(pltpu.SemaphoreType.DMA((ncopies,)), pltpu.SemaphoreType.DMA((ncopies,)),
                   *[jax.ShapeDtypeStruct(t.shape, t.dtype) for t in arrays]),
        in_specs=[HBM] * n, out_specs=(SEM, SEM, *([HBM] * n)),
        input_output_aliases={i: i + 2 for i in range(n)},
        compiler_params=pltpu.CompilerParams(has_side_effects=EFFECT),
    )(*[_hbm(t) for t in arrays])
    return outs[0], outs[1], list(outs[2:])


def _wait_copies(name, send, recv, arrays, make, after):
    n = len(arrays)

    def body(*refs):
        for cp in make(refs[:n], refs[n], refs[n + 1]):
            cp.wait_send()
            cp.wait_recv()

    outs = pl.pallas_call(
        body, name=name, out_shape=[jax.ShapeDtypeStruct(t.shape, t.dtype) for t in arrays],
        in_specs=[HBM] * n + [SEM, SEM, ANY], out_specs=[HBM] * n,
        input_output_aliases={i: i for i in range(n)},
        compiler_params=pltpu.CompilerParams(has_side_effects=EFFECT),
    )(*arrays, send, recv, after)
    return list(outs)


def _w_in_copies(relations):
    def make(refs, send, recv):
        x, y, c = _coords()
        me = 2 * x + y
        chips = _other_chips(x, y)
        w, conv = refs[0], refs[1]
        cps = []
        for i, k in enumerate(relations):
            cps.append(_remote(_column_half(w, 0, c), _column_half(w, 1 + k, c), send, recv, 2 * i, (*chips[k], c)))
            mine = _shard(conv, "col", me)
            cps.append(_remote(mine, mine, send, recv, 2 * i + 1, (*chips[k], c)))
        return cps
    return make


def _column_half(w, slot, c):
    half = w.shape[1] // 8
    return w.at[:, pl.ds((2 * slot + c) * half, half)]


def _w_in_forward(relations):
    def make(refs, send, recv):
        x, y, c = _coords()
        cps = []
        for i, k in enumerate(relations):
            got = _column_half(refs[0], 1 + k, c)
            cps.append(_remote(got, got, send, recv, i, (x, y, 1 - c)))
        return cps
    return make


def _sibling_columns(c0, width):
    def make(refs, send, recv):
        x, y, c = _coords()
        cols = pl.ds(c0, width)
        return [_remote(refs[0].at[:, cols], refs[1].at[:, cols], send, recv, 0, (x, y, 1 - c))]
    return make


def _other_weight_copies(refs, send, recv):
    x, y, c = _coords()
    me = 2 * x + y
    cps = []
    for k, chip in enumerate(_other_chips(x, y)):
        for p, (_, kind) in enumerate(BIG[1:]):
            mine = _piece(refs[p], kind, me, c)
            cps.append(_remote(mine, mine, send, recv, 3 * p + k, (*chip, c)))
    return cps


def _other_weight_forward(refs, send, recv):
    x, y, c = _coords()
    cps = []
    for k, chip in enumerate(_other_chips(x, y)):
        s = 2 * chip[0] + chip[1]
        for p, (_, kind) in enumerate(BIG[1:]):
            got = _piece(refs[p], kind, s, c)
            cps.append(_remote(got, got, send, recv, 3 * p + k, (x, y, 1 - c)))
    return cps


def _share_copies(group):
    def make(refs, send, recv):
        x, y, c = _coords()
        cps = []
        for p, (_, kind) in enumerate(group):
            mine = _half(refs[p], kind, c)
            cps.append(_remote(mine, mine, send, recv, p, (x, y, 1 - c)))
        return cps
    return make


def _sibling_halves_start(G, group, carry, tag):
    n = len(group)
    parts = [G[name] for name, _ in group]
    lands = []
    for (_, kind), g in zip(group, parts):
        R, C = g.shape
        lands.append(lax.empty((R // 2, C) if kind == "col" else (R, C // 2), g.dtype))

    def make(refs, send, recv):
        x, y, c = _coords()
        return [_remote(_half(refs[p], group[p][1], 1 - c), refs[n + p], send, recv, p, (x, y, 1 - c)) for p in range(n)]

    send, recv, thru = _start_copies("sibling_halves_start_" + tag, [*parts, *lands, carry], n, make)
    return (send, recv, thru[:2 * n], make, tag), thru[2 * n]


def _presums(state, group, pos, after):
    send, recv, arrays, make, tag = state
    n = len(group)
    thru = _wait_copies("sibling_halves_wait_" + tag, send, recv, arrays, make, after)
    return [_presum(thru[p], thru[n + p], kind, pos, "presum_" + name) for p, (name, kind) in enumerate(group)]


def _presum(g, got, kind, pos, name):
    R, C = got.shape
    tr, tc = _tile(R, 512, 16), _tile(C, 2048)
    nr, nc = R // tr, C // tc

    def body(pos_ref, a_ref, b_ref, o_ref):
        o_ref[...] = (a_ref[...].astype(F32) + b_ref[...].astype(F32)).astype(o_ref.dtype)

    blk = pl.BlockSpec((tr, tc), lambda i, j, pos_ref: (i, j))
    if kind == "col":
        mine = pl.BlockSpec((tr, tc), lambda i, j, pos_ref: (pos_ref[1] * nr + i, j))
    else:
        mine = pl.BlockSpec((tr, tc), lambda i, j, pos_ref: (i, pos_ref[1] * nc + j))
    spec = pltpu.PrefetchScalarGridSpec(num_scalar_prefetch=1, grid=(nr, nc), in_specs=[mine, blk], out_specs=blk)
    return _call(body, name=name, grid_spec=spec, out_shape=jax.ShapeDtypeStruct((R, C), WIRE_DTYPE))(pos, g, got)


def _chip_copies(group):
    n = len(group)

    def make(refs, send, recv):
        x, y, c = _coords()
        cps = []
        for k, chip in enumerate(_other_chips(x, y)):
            s = 2 * chip[0] + chip[1]
            for p in range(n):
                cps.append(_remote(_shard(refs[p], group[p][1], s), refs[n + p].at[k], send, recv, 3 * p + k, (*chip, c)))
        return cps
    return make


def _landing_zones(pres, group):
    lands = []
    for (_, kind), g in zip(group, pres):
        R, C = g.shape
        lands.append(lax.empty((3, R, C // 4) if kind == "col" else (3, R // 4, C), g.dtype))
    return lands


def _exchange_start(pres, group, carry, tag):
    n = len(group)
    make = _chip_copies(group)
    send, recv, thru = _start_copies("chip_exchange_start_" + tag, [*pres, *_landing_zones(pres, group), carry], 3 * n, make)
    return (send, recv, thru[:2 * n], make, tag), thru[2 * n]


def _exchange_wait(state, after):
    send, recv, arrays, make, tag = state
    thru = _wait_copies("chip_exchange_wait_" + tag, send, recv, arrays, make, after)
    n = len(thru) // 2
    return thru[:n], thru[n:]


def _reduce_into_shard(slots, pre, kind, pos, name):
    K, R, C = slots.shape
    tr, tc = _tile(R, 512, 16), _tile(C, 2176)
    nr, nc = R // tr, C // tc

    def body(pos_ref, s_ref, p_ref, o_ref):
        acc = p_ref[...].astype(F32)
        for k in range(K):
            acc = acc + s_ref[k].astype(F32)
        o_ref[...] = acc

    if kind == "col":
        own = pl.BlockSpec((tr, tc), lambda i, j, pos_ref: (i, pos_ref[0] * nc + j))
        full, out = (2 * R, C), pl.BlockSpec((tr, tc), lambda i, j, pos_ref: (pos_ref[1] * nr + i, j))
    else:
        own = pl.BlockSpec((tr, tc), lambda i, j, pos_ref: (pos_ref[0] * nr + i, j))
        full, out = (R, 2 * C), pl.BlockSpec((tr, tc), lambda i, j, pos_ref: (i, pos_ref[1] * nc + j))
    spec = pltpu.PrefetchScalarGridSpec(
        num_scalar_prefetch=1, grid=(nr, nc),
        in_specs=[pl.BlockSpec((K, tr, tc), lambda i, j, pos_ref: (0, i, j)), own], out_specs=out)
    return _call(body, name=name, grid_spec=spec, out_shape=jax.ShapeDtypeStruct(full, F32))(pos, slots, pre)


def _small_slots(pack, me):
    _, N = pack.shape

    def body(me_ref, p_ref, o_ref):
        o_ref[0] = p_ref[...]

    spec = pltpu.PrefetchScalarGridSpec(
        num_scalar_prefetch=1, grid=(1,), in_specs=[pl.BlockSpec((1, N), lambda i, me_ref: (0, 0))],
        out_specs=pl.BlockSpec((1, 1, N), lambda i, me_ref: (me_ref[0], 0, 0)))
    return _call(body, name="small_slots", grid_spec=spec, out_shape=jax.ShapeDtypeStruct((8, 1, N), pack.dtype))(me, pack)


def _small_copies(refs, send, recv):
    x, y, c = _coords()
    me = 4 * x + 2 * y + c
    cps = []
    for k in range(1, 8):
        dev = (x ^ (k >> 2), y ^ ((k >> 1) & 1), c ^ (k & 1))
        cps.append(_remote(refs[0], refs[1].at[me], send, recv, k - 1, dev))
    return cps


def _sum_small(slots):
    K, _, N = slots.shape

    def body(s_ref, o_ref):
        acc = s_ref[0]
        for k in range(1, K):
            acc = acc + s_ref[k]
        o_ref[...] = acc

    return _call(body, name="sum_small", in_specs=[pl.BlockSpec(memory_space=pltpu.VMEM)],
                 out_specs=pl.BlockSpec(memory_space=pltpu.VMEM), out_shape=jax.ShapeDtypeStruct((1, N), F32))(slots)


def _adamw(w, g, m, v, name, with_grad=False):
    R, C = w.shape
    tr, tc = _tile(R, 256, 8), _tile(C, 2176)

    def body(w_ref, g_ref, m_ref, v_ref, d_ref, nm_ref, nv_ref, *g_out):
        gv = g_ref[...]
        for ref in g_out:
            ref[...] = gv
        nm = ADAM_B1 * m_ref[...] + (1.0 - ADAM_B1) * gv
        nv = ADAM_B2 * v_ref[...] + (1.0 - ADAM_B2) * gv * gv
        m_hat = nm / (1.0 - ADAM_B1 ** ADAM_STEP)
        v_hat = nv / (1.0 - ADAM_B2 ** ADAM_STEP)
        d_ref[...] = -ADAM_LR * (m_hat / (jnp.sqrt(v_hat) + ADAM_EPS) + ADAM_WD * w_ref[...])
        nm_ref[...] = nm
        nv_ref[...] = nv

    spec = pl.BlockSpec((tr, tc), lambda i, j: (i, j))
    shp = jax.ShapeDtypeStruct((R, C), F32)
    nout = 4 if with_grad else 3
    return _call(body, name=name, grid=(R // tr, C // tc), in_specs=[spec] * 4, out_specs=[spec] * nout,
                 out_shape=[shp] * nout)(w, g, m, v)


SMALL = ("norm_g", "mem_norm_g", "attn_q_norm", "attn_k_norm", "conv_w", "mem_q_norm", "mem_k_norm")
WEIGHTS = ("norm_g", "mem_norm_g", "w_in", "attn_q_norm", "attn_k_norm", "conv_w", "mem_w_kv", "mem_q_norm",
           "mem_k_norm", "w_br_attn", "w_br_conv", "w_br_mem", "w_out")


def kernel(x, mem, norm_g, mem_norm_g, w_in, attn_q_norm, attn_k_norm, conv_w, mem_w_kv, mem_q_norm, mem_k_norm, w_br_attn, w_br_conv, w_br_mem, w_out, loss_target, m_norm_g, m_mem_norm_g, m_w_in, m_attn_q_norm, m_attn_k_norm, m_conv_w, m_mem_w_kv, m_mem_q_norm, m_mem_k_norm, m_w_br_attn, m_w_br_conv, m_w_br_mem, m_w_out, v_norm_g, v_mem_norm_g, v_w_in, v_attn_q_norm, v_attn_k_norm, v_conv_w, v_mem_w_kv, v_mem_q_norm, v_mem_k_norm, v_w_br_attn, v_w_br_conv, v_w_br_mem, v_w_out):
    w = dict(norm_g=norm_g, mem_norm_g=mem_norm_g, w_in=w_in, attn_q_norm=attn_q_norm, attn_k_norm=attn_k_norm,
             conv_w=conv_w, mem_w_kv=mem_w_kv, mem_q_norm=mem_q_norm, mem_k_norm=mem_k_norm, w_br_attn=w_br_attn,
             w_br_conv=w_br_conv, w_br_mem=w_br_mem, w_out=w_out)
    m = dict(norm_g=m_norm_g, mem_norm_g=m_mem_norm_g, w_in=m_w_in, attn_q_norm=m_attn_q_norm,
             attn_k_norm=m_attn_k_norm, conv_w=m_conv_w, mem_w_kv=m_mem_w_kv, mem_q_norm=m_mem_q_norm,
             mem_k_norm=m_mem_k_norm, w_br_attn=m_w_br_attn, w_br_conv=m_w_br_conv, w_br_mem=m_w_br_mem, w_out=m_w_out)
    v = dict(norm_g=v_norm_g, mem_norm_g=v_mem_norm_g, w_in=v_w_in, attn_q_norm=v_attn_q_norm,
             attn_k_norm=v_attn_k_norm, conv_w=v_conv_w, mem_w_kv=v_mem_w_kv, mem_q_norm=v_mem_q_norm,
             mem_k_norm=v_mem_k_norm, w_br_attn=v_w_br_attn, w_br_conv=v_w_br_conv, w_br_mem=v_w_br_mem, w_out=v_w_out)
    Bl, _, D = x.shape
    cx, cy = lax.axis_index("x"), lax.axis_index("y")
    chip = 2 * cx + cy
    pos = jnp.stack([chip, lax.axis_index("c")]).astype(jnp.int32)
    order = jnp.stack([chip] + [2 * a + b for a, b in _other_chips(cx, cy)]).astype(jnp.int32)
    n = len(BIG)

    slot0 = jnp.stack([jnp.zeros((), jnp.int32), pos[1]])
    w_rel = _place_shard(w["w_in"], "col", slot0, WIRE_DTYPE, "place_w_in_sent", half=0)
    conv_full = _place_shard(conv_w, "col", pos, F32, "place_conv_w")
    others = [_place_shard(w[name], kind, pos, WIRE_DTYPE, "place_" + name) for name, kind in BIG[1:]]
    hb, hbt, mhb = _norms(x, mem, norm_g, mem_norm_g)

    meta = jnp.concatenate([order, pos[1:]])
    near, near_fwd = _w_in_copies((0, 1)), _w_in_forward((0, 1))
    send, recv, (w_rel, conv_full) = _start_copies("gather_near_start", [w_rel, conv_full], 4, near)
    w_rel = _place_shard(w["w_in"], "col", slot0, WIRE_DTYPE, "place_w_in_kept", half=1, into=w_rel)
    proj = _proj_chunk(hb, w_rel, meta, 0, 1, None, None, "proj_own")
    w_rel, conv_full, *others = _wait_copies("gather_near_wait", send, recv, [w_rel, conv_full, *others], near, proj)

    fsend, frecv, (w_rel,) = _start_copies("gather_near_forward_start", [w_rel], 2, near_fwd)
    far, far_fwd = _w_in_copies((2,)), _w_in_forward((2,))
    send, recv, (w_rel, conv_full) = _start_copies("gather_far_start", [w_rel, conv_full], 2, far)
    proj = _proj_chunk(hb, w_rel, meta, 1, 2, 0, proj, "proj_near_landed")
    w_rel, = _wait_copies("gather_near_forward_wait", fsend, frecv, [w_rel], near_fwd, proj)
    proj = _proj_chunk(hb, w_rel, meta, 1, 2, 1, proj, "proj_near_forwarded")
    w_rel, conv_full = _wait_copies("gather_far_wait", send, recv, [w_rel, conv_full], far, proj)

    fsend, frecv, (w_rel,) = _start_copies("gather_far_forward_start", [w_rel], 1, far_fwd)
    send, recv, (*others, w_rel) = _start_copies("gather_rest_start", [*others, w_rel], 3 * (n - 1), _other_weight_copies)
    proj = _proj_chunk(hb, w_rel, meta, 3, 1, 0, proj, "proj_far_landed")
    w_rel, = _wait_copies("gather_far_forward_wait", fsend, frecv, [w_rel], far_fwd, proj)
    proj = _proj_chunk(hb, w_rel, meta, 3, 1, 1, proj, "proj_far_forwarded")
    os, ls, a = _attention_fwd(proj, Bl, attn_q_norm, attn_k_norm)
    *others, w_rel = _wait_copies("gather_rest_wait", send, recv, [*others, w_rel], _other_weight_copies, a[0])
    fsend, frecv, (*others, proj) = _start_copies("gather_rest_forward_start", [*others, proj], 3 * (n - 1),
                                                  _other_weight_forward)
    cc = _conv_branch_fwd(proj, Bl, conv_full)
    others = _wait_copies("gather_rest_forward_wait", fsend, frecv, others, _other_weight_forward, cc[0])
    W = {name: others[p] for p, (name, _) in enumerate(BIG[1:])}

    def rest_halves(G, carry):
        return _sibling_halves_start(G, BIG[1:], carry, "rest")

    def rest_exchange(state, after, carry):
        return _exchange_start(_presums(state, BIG[1:], pos, after), BIG[1:], carry, "rest")

    G, rest, rest_state = _weight_grads(
        x, mem, loss_target, norm_g, mem_norm_g, attn_q_norm, attn_k_norm, conv_full, mem_q_norm, mem_k_norm, W,
        (hb, hbt, mhb, proj, os, ls, a, cc), early=(rest_halves, rest_exchange))

    tn = _tile(rest[0].shape[1], DW_IN_TILE)
    ntiles = rest[0].shape[1] // tn
    first = ntiles - ntiles // 2
    early_cols = _sibling_columns(0, first * tn)
    late_cols = _sibling_columns(first * tn, (ntiles - first) * tn)
    for_sibling = _dw_in_half(hbt, rest[0], pos, False, "dw_in_sibling_early", tiles=(0, first))
    send, recv, (for_sibling, got, dproj) = _start_copies(
        "sibling_w_in_start_early", [for_sibling, lax.empty(for_sibling.shape, for_sibling.dtype), rest[0]], 1, early_cols)
    for_sibling = _dw_in_half(hbt, dproj, pos, False, "dw_in_sibling_late", tiles=(first, ntiles - first),
                              into=for_sibling)
    lsend, lrecv, (for_sibling, got) = _start_copies("sibling_w_in_start_late", [for_sibling, got], 1, late_cols)
    pres_rest, slots_rest = _exchange_wait(rest_state, for_sibling)
    reds_rest = [_reduce_into_shard(slots_rest[p], pres_rest[p], kind, pos, "reduce_" + name)
                 for p, (name, kind) in enumerate(BIG[1:])]
    share_rest = _share_copies(BIG[1:])
    rsend, rrecv, reds_rest = _start_copies("share_rest_start", reds_rest, n - 1, share_rest)
    for_sibling, got = _wait_copies("sibling_w_in_wait_early", send, recv, [for_sibling, got], early_cols, reds_rest[0])
    for_sibling, got = _wait_copies("sibling_w_in_wait_late", lsend, lrecv, [for_sibling, got], late_cols, reds_rest[0])
    pre_w_in = _dw_in_half(hbt, dproj, pos, True, "dw_in_own", add=got)

    w_in_state, dproj = _exchange_start([pre_w_in], BIG[:1], dproj, "w_in")
    grad_x, small = _input_grad((dproj, *rest[1:]), w_rel, order)
    pres, slots = _exchange_wait(w_in_state, grad_x)
    red_w_in = _reduce_into_shard(slots[0], pres[0], "col", pos, "reduce_w_in")
    share_w_in = _share_copies(BIG[:1])
    wsend, wrecv, (red_w_in, small) = _start_copies("share_w_in_start", [red_w_in, small], 1, share_w_in)
    grad_x = grad_x.reshape(x.shape)

    slots = _small_slots(small, (2 * pos[:1] + pos[1:]))
    ssend, srecv, (small, slots) = _start_copies("gather_small_start", [small, slots], 7, _small_copies)
    reds_rest = _wait_copies("share_rest_wait", rsend, rrecv, reds_rest, share_rest, slots)
    grads = dict(zip([name for name, _ in BIG[1:]], reds_rest))
    delta, new_m, new_v = {}, {}, {}
    for name, _ in BIG[1:]:
        delta[name], new_m[name], new_v[name], grads[name] = _adamw(w[name], grads[name], m[name], v[name],
                                                                    "adamw_" + name, with_grad=True)

    small, slots = _wait_copies("gather_small_wait", ssend, srecv, [small, slots], _small_copies, delta[BIG[-1][0]])
    tot = _sum_small(slots)[0]
    loss = tot[0]
    off = 128
    for name, size in (("norm_g", D), ("mem_norm_g", D), ("attn_q_norm", NGROUP * HEAD), ("attn_k_norm", NGROUP * HEAD),
                       ("conv_w", 3 * CONVW), ("mem_q_norm", MEM_HD), ("mem_k_norm", MEM_HD)):
        grads[name] = tot[off:off + size]
        off += size
    cw = conv_w.shape[1]
    grads["conv_w"] = lax.dynamic_slice(grads["conv_w"].reshape(3, CONVW), (0, chip * cw), (3, cw))
    for name in SMALL:
        grads[name] = grads[name].reshape(w[name].shape)

    def packed(t):
        return jnp.concatenate([t[name].reshape(1, -1) for name in SMALL], axis=1)

    ds, ms, vs = _adamw(packed(w), packed(grads), packed(m), packed(v), "adamw_small")
    shared, = _wait_copies("share_w_in_wait", wsend, wrecv, [red_w_in], share_w_in, ds)
    delta["w_in"], new_m["w_in"], new_v["w_in"], grads["w_in"] = _adamw(w["w_in"], shared, m["w_in"], v["w_in"],
                                                                        "adamw_w_in", with_grad=True)
    off = 0
    for name in SMALL:
        size = w[name].size
        delta[name] = ds[0, off:off + size].reshape(w[name].shape)
        new_m[name] = ms[0, off:off + size].reshape(w[name].shape)
        new_v[name] = vs[0, off:off + size].reshape(w[name].shape)
        off += size

    return (loss, grad_x, *[grads[n] for n in WEIGHTS], *[delta[n] for n in WEIGHTS],
            *[new_m[n] for n in WEIGHTS], *[new_v[n] for n in WEIGHTS])
```

```python
import functools

import jax
import jax.numpy as jnp
from jax import lax
from jax.experimental import pallas as pl
from jax.experimental.pallas import tpu as pltpu

F32 = jnp.float32
MXU_DTYPE = jnp.bfloat16
WIRE_DTYPE = jnp.bfloat16
PROJ_DTYPE = jnp.bfloat16
EPS = 1e-6
NEG = -1e30

HEAD = 128
HPG = 4
GW = HPG * HEAD
DILATIONS = (1, 4, 16)
NGROUP = len(DILATIONS)
BLK = 128
QKV = NGROUP * GW
CONVW = 1024
MEM_HEADS = 4
MEM_HD = 256
MEMW = MEM_HEADS * MEM_HD
Q0, K0, V0 = 0, QKV, 2 * QKV
ZA = 3 * QKV
CB, CC, CV, ZC = ZA + GW, ZA + GW + CONVW, ZA + GW + 2 * CONVW, ZA + GW + 3 * CONVW
MQ = ZC + CONVW
ZM = MQ + MEMW
G0 = ZM + MEMW

ADAM_LR, ADAM_B1, ADAM_B2, ADAM_EPS, ADAM_WD, ADAM_STEP = 0.001, 0.9, 0.999, 1e-08, 0.01, 10

VMEM_LIMIT = 56 * 1024 * 1024
MESH = pl.DeviceIdType.MESH
ANY = pl.BlockSpec(memory_space=pl.ANY)


def _tile(n, pref, mult=128):
    t = min(pref, n)
    while t > mult and (n % t or t % mult):
        t -= mult
    assert n % t == 0, (n, pref)
    return t


def _call(body, *, name, out_shape, grid=(), in_specs=None, out_specs=None, scratch_shapes=(),
          aliases=None, grid_spec=None):
    kw = {}
    if grid_spec is not None:
        kw["grid_spec"] = grid_spec
        ngrid = len(grid_spec.grid)
    else:
        kw.update(grid=grid, in_specs=in_specs, out_specs=out_specs, scratch_shapes=list(scratch_shapes))
        ngrid = len(grid)
    params = pltpu.CompilerParams(dimension_semantics=("arbitrary",) * ngrid, vmem_limit_bytes=VMEM_LIMIT)
    return pl.pallas_call(body, name=name, out_shape=out_shape, compiler_params=params,
                          input_output_aliases=aliases or {}, **kw)


_DIMS = {"nn": (((1,), (0,)), ((), ())), "nt": (((1,), (1,)), ((), ())), "tn": (((0,), (0,)), ((), ()))}


def _mxu(a, b, mode):
    return lax.dot_general(a.astype(MXU_DTYPE), b.astype(MXU_DTYPE), _DIMS[mode], preferred_element_type=F32)


@functools.partial(jax.custom_vjp, nondiff_argnums=(2,))
def _dot(a, b, mode):
    return _mxu(a, b, mode)


def _dot_fwd(a, b, mode):
    return _mxu(a, b, mode), (a, b)


def _dot_bwd(mode, res, g):
    a, b = res
    if mode == "nn":
        return _mxu(g, b, "nt"), _mxu(a, g, "tn")
    if mode == "nt":
        return _mxu(g, b, "nn"), _mxu(g, a, "tn")
    return _mxu(b, g, "nt"), _mxu(a, g, "nn")


_dot.defvjp(_dot_fwd, _dot_bwd)


def _sig(z):
    return 1.0 / (1.0 + jnp.exp(-z))


def _silu(z):
    return z * _sig(z)


def _rms_rows(t, g):
    return t * lax.rsqrt(jnp.mean(t * t, axis=-1, keepdims=True) + EPS) * g


def _attn_block(q, k2, v2, gq, gk, first):
    qn = _rms_rows(q, gq)
    kn = _rms_rows(k2, gk)
    s = jnp.where(_band_mask(first, k2.shape[0]), _mxu(qn, kn, "nt") * (HEAD ** -0.5), NEG)
    m = jnp.max(s, axis=-1, keepdims=True)
    p = jnp.exp(s - m)
    den = jnp.sum(p, axis=-1, keepdims=True)
    o = _mxu(p, v2, "nn") / den
    return o, m + jnp.log(den)


def _band_mask(first, nkeys):
    a = lax.broadcasted_iota(jnp.int32, (BLK, nkeys), 0)
    b = lax.broadcasted_iota(jnp.int32, (BLK, nkeys), 1)
    if nkeys == BLK:
        return b <= a
    return (b >= a) & (b <= a + BLK) & (b >= jnp.where(first, BLK, 0))


def _norm_parts(t):
    r = lax.rsqrt(jnp.mean(t * t, axis=-1, keepdims=True) + EPS)
    return r, t * r


def _norm_bwd(dn, g, r, th):
    dth = dn * g
    return r * (dth - th * jnp.mean(dth * th, axis=-1, keepdims=True)), jnp.sum(dn * th, axis=0, keepdims=True)


def _attn_block_bwd(q, k2, v2, gq, gk, first, do, o, lse, dlse):
    scale = HEAD ** -0.5
    rq, qh = _norm_parts(q)
    rk, kh = _norm_parts(k2)
    qn, kn = qh * gq, kh * gk
    s = jnp.where(_band_mask(first, k2.shape[0]), _mxu(qn, kn, "nt") * scale, NEG)
    p = jnp.exp(s - lse)
    ds = p * (_mxu(do, v2, "nt") + (dlse - jnp.sum(do * o, axis=-1, keepdims=True))) * scale
    dq, dgq = _norm_bwd(_mxu(ds, kn, "nn"), gq, rq, qh)
    dk2, dgk = _norm_bwd(_mxu(ds, qn, "tn"), gk, rk, kh)
    return dq, dk2, _mxu(p, do, "tn"), dgq, dgk


def _combine(o1, o2, o3, l1, l2, l3, z):
    m = lax.stop_gradient(jnp.maximum(jnp.maximum(l1, l2), l3))
    e1, e2, e3 = jnp.exp(l1 - m), jnp.exp(l2 - m), jnp.exp(l3 - m)
    return (e1 * o1 + e2 * o2 + e3 * o3) / (e1 + e2 + e3) * _silu(z)


def _mem_block(q, z, kv, gq, gk):
    outs = []
    for h in range(MEM_HEADS):
        sl = slice(h * MEM_HD, (h + 1) * MEM_HD)
        qn = _rms_rows(q[:, sl], gq)
        kn = _rms_rows(kv[:, sl], gk)
        s = _dot(qn, kn, "nt") * (MEM_HD ** -0.5)
        m = lax.stop_gradient(jnp.max(s, axis=-1, keepdims=True))
        p = jnp.exp(s - m)
        den = jnp.sum(p, axis=-1, keepdims=True)
        outs.append(_dot(p, kv[:, MEMW + h * MEM_HD:MEMW + (h + 1) * MEM_HD], "nn") / den)
    return jnp.concatenate(outs, axis=-1) * _silu(z)


def _cast(w, name):
    R, C = w.shape
    tr, tc = _tile(R, 512, 8), _tile(C, 2176)

    def body(w_ref, o_ref):
        o_ref[...] = w_ref[...].astype(o_ref.dtype)

    spec = pl.BlockSpec((tr, tc), lambda i, j: (i, j))
    return _call(body, name=name, grid=(R // tr, C // tc), in_specs=[spec], out_specs=spec,
                 out_shape=jax.ShapeDtypeStruct((R, C), WIRE_DTYPE))(w)


def _place_shard(w, kind, pos, dtype, name, slot=0, into=None, half=None):
    R, C = w.shape
    tr, tc = _tile(R, 512, 8), _tile(C if half is None else C // 2, 2176)
    nr, nc = R // tr, C // tc
    ncols = nc if half is None else nc // 2

    def body(pos_ref, w_ref, *rest):
        rest[-1][...] = w_ref[...].astype(rest[-1].dtype)

    def col(j, pos_ref):
        if half is None:
            return j
        return (pos_ref[1] if half == 0 else 1 - pos_ref[1]) * ncols + j

    if kind == "col":
        full = (R, 4 * C)
        out = pl.BlockSpec((tr, tc), lambda i, j, pos_ref: (i, pos_ref[slot] * nc + col(j, pos_ref)))
    else:
        full, out = (4 * R, C), pl.BlockSpec((tr, tc), lambda i, j, pos_ref: (pos_ref[slot] * nr + i, j))
    in_specs, args = [pl.BlockSpec((tr, tc), lambda i, j, pos_ref: (i, col(j, pos_ref)))], [pos, w]
    if into is not None:
        in_specs.append(ANY)
        args.append(into)
    spec = pltpu.PrefetchScalarGridSpec(num_scalar_prefetch=1, grid=(nr, ncols), in_specs=in_specs, out_specs=out)
    return _call(body, name=name, grid_spec=spec, out_shape=jax.ShapeDtypeStruct(full, dtype),
                 aliases={} if into is None else {2: 0})(*args)


def _matmul(a, b, mode, out_dtype, *, name, tm=512, tn=512, tk=512):
    if mode == "nn":
        (M, K), (_, N) = a.shape, b.shape
    elif mode == "nt":
        (M, K), (N, _) = a.shape, b.shape
    else:
        (K, M), (_, N) = a.shape, b.shape
    tm, tn, tk = _tile(M, tm), _tile(N, tn), _tile(K, tk)
    nk = K // tk

    def body(a_ref, b_ref, o_ref, *acc):
        part = lax.dot_general(a_ref[...], b_ref[...], _DIMS[mode], preferred_element_type=F32)
        if nk == 1:
            o_ref[...] = part.astype(o_ref.dtype)
            return
        acc_ref, = acc
        k = pl.program_id(2)

        @pl.when(k == 0)
        def _():
            acc_ref[...] = part

        @pl.when(k > 0)
        def _():
            acc_ref[...] += part

        @pl.when(k == nk - 1)
        def _():
            o_ref[...] = acc_ref[...].astype(o_ref.dtype)

    a_spec = pl.BlockSpec((tk, tm), lambda i, j, k: (k, i)) if mode == "tn" else pl.BlockSpec((tm, tk), lambda i, j, k: (i, k))
    b_spec = pl.BlockSpec((tn, tk), lambda i, j, k: (j, k)) if mode == "nt" else pl.BlockSpec((tk, tn), lambda i, j, k: (k, j))
    return _call(body, name=name, grid=(M // tm, N // tn, nk), in_specs=[a_spec, b_spec],
                 out_specs=pl.BlockSpec((tm, tn), lambda i, j, k: (i, j)),
                 out_shape=jax.ShapeDtypeStruct((M, N), out_dtype),
                 scratch_shapes=[] if nk == 1 else [pltpu.VMEM((tm, tn), F32)])(a, b)


def _rms_fwd(x, g, name):
    R, D = x.shape
    tr = _tile(R, 512)

    def body(x_ref, g_ref, o_ref, t_ref):
        y = _rms_rows(x_ref[...], g_ref[...])
        o_ref[...] = y.astype(o_ref.dtype)
        t_ref[...] = y.T.astype(t_ref.dtype)

    row = pl.BlockSpec((tr, D), lambda i: (i, 0))
    return _call(body, name=name, grid=(R // tr,), in_specs=[row, pl.BlockSpec((1, D), lambda i: (0, 0))],
                 out_specs=[row, pl.BlockSpec((D, tr), lambda i: (0, i))],
                 out_shape=[jax.ShapeDtypeStruct((R, D), MXU_DTYPE), jax.ShapeDtypeStruct((D, R), MXU_DTYPE)])(x, g)


def _rms_bwd(x, dh, g, dy, name):
    R, D = x.shape
    tr = _tile(R, 256)
    with_dx = dy is not None

    def body(*refs):
        if with_dx:
            x_ref, dh_ref, g_ref, dy_ref, dx_ref, dg_ref = refs
        else:
            x_ref, dh_ref, g_ref, dg_ref = refs
        xv, dhv = x_ref[...], dh_ref[...]
        r = lax.rsqrt(jnp.mean(xv * xv, axis=-1, keepdims=True) + EPS)
        xh = xv * r

        @pl.when(pl.program_id(0) == 0)
        def _():
            dg_ref[...] = jnp.zeros_like(dg_ref)

        dg_ref[...] += jnp.sum(dhv * xh, axis=0, keepdims=True)
        if with_dx:
            dxh = dhv * g_ref[...]
            dx_ref[...] = dy_ref[...] + r * (dxh - xh * jnp.mean(dxh * xh, axis=-1, keepdims=True))

    row = pl.BlockSpec((tr, D), lambda i: (i, 0))
    vec = pl.BlockSpec((1, D), lambda i: (0, 0))
    dg_shape = jax.ShapeDtypeStruct((1, D), F32)
    if with_dx:
        return _call(body, name=name, grid=(R // tr,), in_specs=[row, row, vec, row], out_specs=[row, vec],
                     out_shape=[jax.ShapeDtypeStruct((R, D), F32), dg_shape])(x, dh, g, dy)
    return None, _call(body, name=name, grid=(R // tr,), in_specs=[row, row, vec], out_specs=vec,
                       out_shape=dg_shape)(x, dh, g)


def _attn_geom(g, d):
    hc = HPG if d == 1 else 1
    cw = hc * HEAD
    cq, ck, cv = (Q0 + g * GW) // cw, (K0 + g * GW) // cw, (V0 + g * GW) // cw
    return (1, BLK * d, cw), hc, HPG // hc, cq, ck, cv


def _rows(ref, r, d, sl):
    if d == 1:
        return ref[0, :, sl]
    return ref.at[0][pl.ds(r, BLK, stride=d), sl]


def _set_rows(ref, r, d, sl, val):
    if d == 1:
        ref[0, :, sl] = val
    else:
        ref.at[0][pl.ds(r, BLK, stride=d), sl] = val


def _stage_rows(ref, r, d, sl, val):
    if d == 1:
        ref[:, sl] = val
    else:
        ref[pl.ds(r, BLK, stride=d), sl] = val


def _proj_stages(blk, d):
    return [] if d == 1 else [pltpu.VMEM(blk[1:], F32)] * 5


def _proj_rows(refs, stages, d):
    if d == 1:
        return [lambda r, sl, ref=ref: ref[0, :, sl].astype(F32) for ref in refs]
    for ref, stage in zip(refs, stages):
        stage[...] = ref[0].astype(F32)
    return [lambda r, sl, stage=stage: stage[pl.ds(r, BLK, stride=d), sl] for stage in stages]


def _attn_fwd(proj3, gq, gk, g, d):
    Bl, S, _ = proj3.shape
    blk, hc, ncb, cq, ck, cv = _attn_geom(g, d)
    nb = S // blk[1]
    if nb == 1:
        return _attn_single_fwd(proj3, gq, gk, g, d)

    def body(q_ref, kp_ref, kc_ref, vp_ref, vc_ref, gq_ref, gk_ref, o_ref, lse_ref, *stages):
        first = pl.program_id(2) == 0
        q, kp, kc, vp, vc = _proj_rows((q_ref, kp_ref, kc_ref, vp_ref, vc_ref), stages, d)
        def run(alone):
            for r in range(d):
                for h in range(hc):
                    sl = slice(h * HEAD, (h + 1) * HEAD)
                    if alone:
                        k2, v2 = kc(r, sl), vc(r, sl)
                    else:
                        k2 = jnp.concatenate([kp(r, sl), kc(r, sl)], axis=0)
                        v2 = jnp.concatenate([vp(r, sl), vc(r, sl)], axis=0)
                    o, lse = _attn_block(q(r, sl), k2, v2, gq_ref[...], gk_ref[...], False)
                    _set_rows(o_ref, r, d, sl, o)
                    _set_rows(lse_ref, r, d, sl, jnp.broadcast_to(lse, (BLK, HEAD)))

        pl.when(first)(lambda: run(True))
        pl.when(jnp.logical_not(first))(lambda: run(False))

    def cur(c0):
        return pl.BlockSpec(blk, lambda b, j, i: (b, i, c0 + j))

    def prev(c0):
        return pl.BlockSpec(blk, lambda b, j, i: (b, jnp.maximum(i - 1, 0), c0 + j))

    vec = pl.BlockSpec((1, HEAD), lambda b, j, i: (0, 0))
    out = pl.BlockSpec(blk, lambda b, j, i: (b, i, j))
    shp = jax.ShapeDtypeStruct((Bl, S, GW), F32)
    return _call(body, name=f"attn_fwd_g{g}", grid=(Bl, ncb, nb),
                 in_specs=[cur(cq), prev(ck), cur(ck), prev(cv), cur(cv), vec, vec],
                 out_specs=[out, out], out_shape=[shp, shp], scratch_shapes=_proj_stages(blk, d),
                 )(proj3, proj3, proj3, proj3, proj3, gq, gk)


def _attn_single_fwd(proj3, gq, gk, g, d):
    Bl, S, _ = proj3.shape
    blk, hc, ncb, cq, ck, cv = _attn_geom(g, d)

    def body(q_ref, k_ref, v_ref, gq_ref, gk_ref, o_ref, lse_ref, *stages):
        q, k, v = _proj_rows((q_ref, k_ref, v_ref), stages, d)
        for r in range(d):
            for h in range(hc):
                sl = slice(h * HEAD, (h + 1) * HEAD)
                o, lse = _attn_block(q(r, sl), k(r, sl), v(r, sl), gq_ref[...], gk_ref[...], True)
                _set_rows(o_ref, r, d, sl, o)
                _set_rows(lse_ref, r, d, sl, jnp.broadcast_to(lse, (BLK, HEAD)))

    def at(c0):
        return pl.BlockSpec(blk, lambda b, j: (b, 0, c0 + j))

    vec = pl.BlockSpec((1, HEAD), lambda b, j: (0, 0))
    shp = jax.ShapeDtypeStruct((Bl, S, GW), F32)
    return _call(body, name=f"attn_fwd_g{g}", grid=(Bl, ncb), in_specs=[at(cq), at(ck), at(cv), vec, vec],
                 out_specs=[at(0), at(0)], out_shape=[shp, shp], scratch_shapes=_proj_stages(blk, d)[:3],
                 )(proj3, proj3, proj3, gq, gk)


def _attn_single_bwd(proj3, gq, gk, o3, l3, do3, dl3, g, d):
    Bl, S, _ = proj3.shape
    blk, hc, ncb, cq, ck, cv = _attn_geom(g, d)

    def body(q_ref, k_ref, v_ref, gq_ref, gk_ref, o_ref, l_ref, do_ref, dl_ref,
             dq_ref, dk_ref, dv_ref, dgq_ref, dgk_ref, sq_ref, sk_ref, sv_ref, *stages):
        @pl.when((pl.program_id(0) == 0) & (pl.program_id(1) == 0))
        def _():
            dgq_ref[...] = jnp.zeros_like(dgq_ref)
            dgk_ref[...] = jnp.zeros_like(dgk_ref)

        dgq, dgk = jnp.zeros((1, HEAD), F32), jnp.zeros((1, HEAD), F32)
        q, k, v = _proj_rows((q_ref, k_ref, v_ref), stages, d)
        for r in range(d):
            for h in range(hc):
                sl = slice(h * HEAD, (h + 1) * HEAD)
                dq, dk, dv, a, b = _attn_block_bwd(
                    q(r, sl), k(r, sl), v(r, sl), gq_ref[...], gk_ref[...], True, _rows(do_ref, r, d, sl),
                    _rows(o_ref, r, d, sl), _rows(l_ref, r, d, sl)[:, :1], _rows(dl_ref, r, d, sl)[:, :1])
                _stage_rows(sq_ref, r, d, sl, dq)
                _stage_rows(sk_ref, r, d, sl, dk)
                _stage_rows(sv_ref, r, d, sl, dv)
                dgq, dgk = dgq + a, dgk + b
        dgq_ref[...] += dgq
        dgk_ref[...] += dgk
        dq_ref[0] = sq_ref[...].astype(dq_ref.dtype)
        dk_ref[0] = sk_ref[...].astype(dk_ref.dtype)
        dv_ref[0] = sv_ref[...].astype(dv_ref.dtype)

    def at(c0):
        return pl.BlockSpec(blk, lambda b, j: (b, 0, c0 + j))

    vec = pl.BlockSpec((1, HEAD), lambda b, j: (0, 0))
    shp = jax.ShapeDtypeStruct((Bl, S, GW), MXU_DTYPE)
    gshp = jax.ShapeDtypeStruct((1, HEAD), F32)
    return _call(body, name=f"attn_bwd_g{g}", grid=(Bl, ncb),
                 in_specs=[at(cq), at(ck), at(cv), vec, vec, at(0), at(0), at(0), at(0)],
                 out_specs=[at(0), at(0), at(0), vec, vec], out_shape=[shp, shp, shp, gshp, gshp],
                 scratch_shapes=[pltpu.VMEM(blk[1:], F32)] * 3 + _proj_stages(blk, d)[:3],
                 )(proj3, proj3, proj3, gq, gk, o3, l3, do3, dl3)


def _attn_bwd(proj3, gq, gk, o3, l3, do3, dl3, g, d):
    Bl, S, _ = proj3.shape
    blk, hc, ncb, cq, ck, cv = _attn_geom(g, d)
    nb = S // blk[1]
    if nb == 1:
        return _attn_single_bwd(proj3, gq, gk, o3, l3, do3, dl3, g, d)

    def body(q_ref, kp_ref, kc_ref, vp_ref, vc_ref, gq_ref, gk_ref, o_ref, l_ref, do_ref, dl_ref,
             dq_ref, dk_ref, dv_ref, dgq_ref, dgk_ref, ck_ref, cv_ref, sq_ref, sk_ref, sv_ref, *stages):
        i = pl.program_id(2)
        first = i == 0

        @pl.when((pl.program_id(0) == 0) & (pl.program_id(1) == 0) & first)
        def _():
            dgq_ref[...] = jnp.zeros_like(dgq_ref)
            dgk_ref[...] = jnp.zeros_like(dgk_ref)

        def run(alone):
            dgq, dgk = jnp.zeros((1, HEAD), F32), jnp.zeros((1, HEAD), F32)
            q, kp, kc, vp, vc = _proj_rows((q_ref, kp_ref, kc_ref, vp_ref, vc_ref), stages, d)
            for r in range(d):
                rs = slice(r * BLK, (r + 1) * BLK)
                for h in range(hc):
                    sl = slice(h * HEAD, (h + 1) * HEAD)
                    if alone:
                        k2, v2 = kc(r, sl), vc(r, sl)
                    else:
                        k2 = jnp.concatenate([kp(r, sl), kc(r, sl)], axis=0)
                        v2 = jnp.concatenate([vp(r, sl), vc(r, sl)], axis=0)
                    dq, dk2, dv2, a, b = _attn_block_bwd(
                        q(r, sl), k2, v2, gq_ref[...], gk_ref[...], False, _rows(do_ref, r, d, sl),
                        _rows(o_ref, r, d, sl), _rows(l_ref, r, d, sl)[:, :1], _rows(dl_ref, r, d, sl)[:, :1])
                    _stage_rows(sq_ref, r, d, sl, dq)
                    if alone:
                        _stage_rows(sk_ref, r, d, sl, jnp.zeros((BLK, HEAD), F32))
                        _stage_rows(sv_ref, r, d, sl, jnp.zeros((BLK, HEAD), F32))
                    else:
                        _stage_rows(sk_ref, r, d, sl, ck_ref[rs, sl] + dk2[:BLK])
                        _stage_rows(sv_ref, r, d, sl, cv_ref[rs, sl] + dv2[:BLK])
                    ck_ref[rs, sl] = dk2[-BLK:]
                    cv_ref[rs, sl] = dv2[-BLK:]
                    dgq, dgk = dgq + a, dgk + b
            dgq_ref[...] += dgq
            dgk_ref[...] += dgk
            dq_ref[0] = sq_ref[...].astype(dq_ref.dtype)

        pl.when(first)(lambda: run(True))
        pl.when((i > 0) & (i < nb))(lambda: run(False))

        @pl.when(i == nb)
        def _():
            for r in range(d):
                rs = slice(r * BLK, (r + 1) * BLK)
                _stage_rows(sk_ref, r, d, slice(None), ck_ref[rs, :])
                _stage_rows(sv_ref, r, d, slice(None), cv_ref[rs, :])

        dk_ref[0] = sk_ref[...].astype(dk_ref.dtype)
        dv_ref[0] = sv_ref[...].astype(dv_ref.dtype)

    def cur(c0):
        return pl.BlockSpec(blk, lambda b, j, i: (b, jnp.minimum(i, nb - 1), c0 + j))

    def prev(c0):
        return pl.BlockSpec(blk, lambda b, j, i: (b, jnp.clip(i - 1, 0, nb - 1), c0 + j))

    vec = pl.BlockSpec((1, HEAD), lambda b, j, i: (0, 0))
    at_q = pl.BlockSpec(blk, lambda b, j, i: (b, jnp.minimum(i, nb - 1), j))
    at_k = pl.BlockSpec(blk, lambda b, j, i: (b, jnp.maximum(i - 1, 0), j))
    shp = jax.ShapeDtypeStruct((Bl, S, GW), MXU_DTYPE)
    gshp = jax.ShapeDtypeStruct((1, HEAD), F32)
    return _call(body, name=f"attn_bwd_g{g}", grid=(Bl, ncb, nb + 1),
                 in_specs=[cur(cq), prev(ck), cur(ck), prev(cv), cur(cv), vec, vec, at_q, at_q, at_q, at_q],
                 out_specs=[at_q, at_k, at_k, vec, vec], out_shape=[shp, shp, shp, gshp, gshp],
                 scratch_shapes=[pltpu.VMEM(blk[1:], F32)] * 5 + _proj_stages(blk, d),
                 )(proj3, proj3, proj3, proj3, proj3, gq, gk, o3, l3, do3, dl3)


def _combine_fwd(os, ls, proj2):
    T = proj2.shape[0]
    tr = _tile(T, 512)

    def body(o1, o2, o3, l1, l2, l3, z, a_ref, at_ref):
        a = _combine(o1[...], o2[...], o3[...], l1[...], l2[...], l3[...], z[...].astype(F32))
        a_ref[...] = a.astype(a_ref.dtype)
        at_ref[...] = a.T.astype(at_ref.dtype)

    row = pl.BlockSpec((tr, GW), lambda i: (i, 0))
    return _call(body, name="combine_fwd", grid=(T // tr,),
                 in_specs=[row] * 6 + [pl.BlockSpec((tr, GW), lambda i: (i, ZA // GW))],
                 out_specs=[row, pl.BlockSpec((GW, tr), lambda i: (0, i))],
                 out_shape=[jax.ShapeDtypeStruct((T, GW), MXU_DTYPE), jax.ShapeDtypeStruct((GW, T), MXU_DTYPE)],
                 )(*os, *ls, proj2)


def _combine_bwd(os, ls, proj2, da, dproj):
    T = proj2.shape[0]
    tr = _tile(T, 256)

    def body(o1, o2, o3, l1, l2, l3, z, da_ref, _, d1, d2, d3, e1, e2, e3, dz_ref):
        _, vjp = jax.vjp(_combine, o1[...], o2[...], o3[...], l1[...], l2[...], l3[...], z[...].astype(F32))
        go1, go2, go3, gl1, gl2, gl3, gz = vjp(da_ref[...])
        d1[...], d2[...], d3[...] = go1, go2, go3
        dz_ref[...] = gz.astype(dz_ref.dtype)
        for ref, gl in ((e1, gl1), (e2, gl2), (e3, gl3)):
            for h in range(HPG):
                sl = slice(h * HEAD, (h + 1) * HEAD)
                ref[:, sl] = jnp.broadcast_to(jnp.sum(gl[:, sl], axis=-1, keepdims=True), (tr, HEAD))

    row = pl.BlockSpec((tr, GW), lambda i: (i, 0))
    f = jax.ShapeDtypeStruct((T, GW), F32)
    z_attn = pl.BlockSpec((tr, GW), lambda i: (i, ZA // GW))
    outs = _call(body, name="combine_bwd", grid=(T // tr,), in_specs=[row] * 6 + [z_attn, row, ANY],
                 out_specs=[row] * 6 + [z_attn], out_shape=[f] * 6 + [jax.ShapeDtypeStruct(dproj.shape, dproj.dtype)],
                 aliases={8: 6})(*os, *ls, proj2, da, dproj)
    return outs[:3], outs[3:6], outs[6]


def _shift_down(u, j, t):
    return jnp.where(t >= j, pltpu.roll(u, j, 0), 0.0)


def _shift_up(u, j, t):
    n = u.shape[0]
    return jnp.where(t < n - j, pltpu.roll(u, n - j, 0), 0.0)


def _conv_specs(Bl, S, cw):
    def sec(c0):
        return pl.BlockSpec((1, S, cw), lambda j, b: (b, 0, c0 // cw + j))
    return [sec(CB), sec(CC), sec(CV), sec(ZC)], pl.BlockSpec((3, cw), lambda j, b: (0, j))


def _conv_fwd(proj3, conv_w):
    Bl, S, _ = proj3.shape
    cw = 256
    secs, wspec = _conv_specs(Bl, S, cw)

    def body(b_ref, c_ref, v_ref, z_ref, w_ref, o_ref, ot_ref):
        t = lax.broadcasted_iota(jnp.int32, (S, cw), 0)
        u = c_ref[0].astype(F32) * v_ref[0].astype(F32)
        y = w_ref[0:1, :] * u + w_ref[1:2, :] * _shift_down(u, 1, t) + w_ref[2:3, :] * _shift_down(u, 2, t)
        out = b_ref[0].astype(F32) * y * _silu(z_ref[0].astype(F32))
        o_ref[0] = out.astype(o_ref.dtype)
        ot_ref[...] = out.T.astype(ot_ref.dtype)

    return _call(body, name="conv_fwd", grid=(CONVW // cw, Bl), in_specs=secs + [wspec],
                 out_specs=[pl.BlockSpec((1, S, cw), lambda j, b: (b, 0, j)), pl.BlockSpec((cw, S), lambda j, b: (j, b))],
                 out_shape=[jax.ShapeDtypeStruct((Bl, S, CONVW), MXU_DTYPE),
                            jax.ShapeDtypeStruct((CONVW, Bl * S), MXU_DTYPE)])(proj3, proj3, proj3, proj3, conv_w)


def _conv_bwd(proj3, conv_w, dcc3, dproj):
    Bl, S, _ = proj3.shape
    cw = 256
    secs, wspec = _conv_specs(Bl, S, cw)

    def body(b_ref, c_ref, v_ref, z_ref, w_ref, d_ref, _, dproj_ref, dw_ref, stage, sems):
        t = lax.broadcasted_iota(jnp.int32, (S, cw), 0)
        bv, cv, vv, zv = (r[0].astype(F32) for r in (b_ref, c_ref, v_ref, z_ref))
        dv = d_ref[0]
        u = cv * vv
        u1, u2 = _shift_down(u, 1, t), _shift_down(u, 2, t)
        y = w_ref[0:1, :] * u + w_ref[1:2, :] * u1 + w_ref[2:3, :] * u2
        sg = _sig(zv)
        sz = zv * sg
        gy = dv * bv * sz
        du = w_ref[0:1, :] * gy + w_ref[1:2, :] * _shift_up(gy, 1, t) + w_ref[2:3, :] * _shift_up(gy, 2, t)
        j, b = pl.program_id(0), pl.program_id(1)
        tiles = [dv * y * sz, du * vv, du * cv, dv * bv * y * sg * (1.0 + zv * (1.0 - sg))]
        dsts = [dproj_ref.at[pl.ds(b * S, S), pl.ds(c0 + j * cw, cw)] for c0 in (CB, CC, CV, ZC)]
        _emit_tiles(j * Bl + b, (CONVW // cw) * Bl, tiles, dsts, stage, sems)

        @pl.when(pl.program_id(1) == 0)
        def _():
            dw_ref[...] = jnp.zeros_like(dw_ref)

        dw_ref[0:1, :] += jnp.sum(gy * u, axis=0, keepdims=True)
        dw_ref[1:2, :] += jnp.sum(gy * u1, axis=0, keepdims=True)
        dw_ref[2:3, :] += jnp.sum(gy * u2, axis=0, keepdims=True)

    blk = pl.BlockSpec((1, S, cw), lambda j, b: (b, 0, j))
    return _call(body, name="conv_bwd", grid=(CONVW // cw, Bl), in_specs=secs + [wspec, blk, ANY],
                 out_specs=[ANY, wspec],
                 out_shape=[jax.ShapeDtypeStruct(dproj.shape, dproj.dtype), jax.ShapeDtypeStruct((3, CONVW), F32)],
                 scratch_shapes=_emit_scratch(4, S, cw), aliases={6: 0})(proj3, proj3, proj3, proj3, conv_w, dcc3, dproj)


def _mem_specs(S, tq):
    q = pl.BlockSpec((1, tq, MEMW), lambda b, j: (b, j, MQ // MEMW))
    z = pl.BlockSpec((1, tq, MEMW), lambda b, j: (b, j, ZM // MEMW))
    kv = pl.BlockSpec((1, MEM_HD, 2 * MEMW), lambda b, j: (b, 0, 0))
    vec = pl.BlockSpec((1, MEM_HD), lambda b, j: (0, 0))
    blk = pl.BlockSpec((1, tq, MEMW), lambda b, j: (b, j, 0))
    return q, z, kv, vec, blk


def _mem_fwd(proj3, mkv3, gq, gk):
    Bl, S, _ = proj3.shape
    tq = _tile(S, 512)
    q, z, kv, vec, blk = _mem_specs(S, tq)

    def body(q_ref, z_ref, kv_ref, gq_ref, gk_ref, o_ref, ot_ref):
        out = _mem_block(q_ref[0].astype(F32), z_ref[0].astype(F32), kv_ref[0], gq_ref[...], gk_ref[...])
        o_ref[0] = out.astype(o_ref.dtype)
        ot_ref[...] = out.T.astype(ot_ref.dtype)

    nq = S // tq
    return _call(body, name="mem_fwd", grid=(Bl, nq), in_specs=[q, z, kv, vec, vec],
                 out_specs=[blk, pl.BlockSpec((MEMW, tq), lambda b, j: (0, b * nq + j))],
                 out_shape=[jax.ShapeDtypeStruct((Bl, S, MEMW), MXU_DTYPE),
                            jax.ShapeDtypeStruct((MEMW, Bl * S), MXU_DTYPE)])(proj3, proj3, mkv3, gq, gk)


def _mem_bwd(proj3, mkv3, gq, gk, dmo3, dproj):
    Bl, S, _ = proj3.shape
    tq = _tile(S, 256)
    q, z, kv, vec, blk = _mem_specs(S, tq)
    nq = S // tq

    def body(q_ref, z_ref, kv_ref, gq_ref, gk_ref, d_ref, _, dproj_ref, dkv_ref, dgq_ref, dgk_ref, stage, sems):
        _, vjp = jax.vjp(_mem_block, q_ref[0].astype(F32), z_ref[0].astype(F32), kv_ref[0], gq_ref[...], gk_ref[...])
        dq, dz, dkv, dgq, dgk = vjp(d_ref[0])
        j = pl.program_id(1)
        rows = pl.ds(pl.program_id(0) * S + j * tq, tq)
        dsts = [dproj_ref.at[rows, pl.ds(MQ, MEMW)], dproj_ref.at[rows, pl.ds(ZM, MEMW)]]
        _emit_tiles(pl.program_id(0) * nq + j, Bl * nq, [dq, dz], dsts, stage, sems)

        @pl.when(j == 0)
        def _():
            dkv_ref[0] = jnp.zeros_like(dkv)

        @pl.when((j == 0) & (pl.program_id(0) == 0))
        def _():
            dgq_ref[...] = jnp.zeros_like(dgq_ref)
            dgk_ref[...] = jnp.zeros_like(dgk_ref)

        dkv_ref[0] += dkv
        dgq_ref[...] += dgq
        dgk_ref[...] += dgk

    gshp = jax.ShapeDtypeStruct((1, MEM_HD), F32)
    return _call(body, name="mem_bwd", grid=(Bl, nq), in_specs=[q, z, kv, vec, vec, blk, ANY],
                 out_specs=[ANY, kv, vec, vec],
                 out_shape=[jax.ShapeDtypeStruct(dproj.shape, dproj.dtype), jax.ShapeDtypeStruct(mkv3.shape, F32),
                            gshp, gshp],
                 scratch_shapes=_emit_scratch(2, tq, MEMW), aliases={6: 0})(proj3, proj3, mkv3, gq, gk, dmo3, dproj)


def _merge_specs(T, D, tm, tn):
    def act(w):
        return pl.BlockSpec((tm, w), lambda i, n: (i, 0))

    def wsp(w):
        return pl.BlockSpec((w, tn), lambda i, n: (0, n))

    gates = [pl.BlockSpec((tm, tn), lambda i, n, k=k: (i, (G0 + k * D) // tn + n)) for k in range(3)]
    tile = pl.BlockSpec((tm, tn), lambda i, n: (i, n))
    return act, wsp, gates, tile


def _merge_fwd(a, cc, mo, wa, wc, wm, proj2):
    T, D = a.shape[0], wa.shape[1]
    tm, tn = _tile(T, 1024), _tile(D, 512)
    act, wsp, gates, tile = _merge_specs(T, D, tm, tn)

    def body(a_ref, c_ref, m_ref, wa_ref, wc_ref, wm_ref, g0, g1, g2, mg_ref, mt_ref, pa_ref, pc_ref, pm_ref):
        pa = jnp.dot(a_ref[...], wa_ref[...], preferred_element_type=F32)
        pc = jnp.dot(c_ref[...], wc_ref[...], preferred_element_type=F32)
        pm = jnp.dot(m_ref[...], wm_ref[...], preferred_element_type=F32)
        mg = _sig(g0[...].astype(F32)) * pa + _sig(g1[...].astype(F32)) * pc + _sig(g2[...].astype(F32)) * pm
        mg_ref[...] = mg.astype(mg_ref.dtype)
        mt_ref[...] = mg.T.astype(mt_ref.dtype)
        pa_ref[...] = pa.astype(pa_ref.dtype)
        pc_ref[...] = pc.astype(pc_ref.dtype)
        pm_ref[...] = pm.astype(pm_ref.dtype)

    shp = jax.ShapeDtypeStruct((T, D), MXU_DTYPE)
    return _call(body, name="merge_fwd", grid=(T // tm, D // tn),
                 in_specs=[act(GW), act(CONVW), act(MEMW), wsp(GW), wsp(CONVW), wsp(MEMW)] + gates,
                 out_specs=[tile, pl.BlockSpec((tn, tm), lambda i, n: (n, i)), tile, tile, tile],
                 out_shape=[shp, jax.ShapeDtypeStruct((D, T), MXU_DTYPE), shp, shp, shp],
                 )(a, cc, mo, wa, wc, wm, proj2, proj2, proj2)


def _emit_tiles(step, nsteps, tiles, dsts, stage, sems):
    slot = step % 2

    def copies(s):
        return [pltpu.make_async_copy(stage.at[s, k], dsts[k], sems.at[s, k]) for k in range(len(tiles))]

    @pl.when(step >= 2)
    def _():
        for cp in copies(slot):
            cp.wait()

    for k, t in enumerate(tiles):
        stage[slot, k] = t.astype(stage.dtype)
    for k, cp in enumerate(copies(slot)):
        cp.start(priority=k % 2)

    @pl.when(step == nsteps - 1)
    def _():
        for cp in copies(slot):
            cp.wait()
        if nsteps > 1:
            for cp in copies(1 - slot):
                cp.wait()


def _emit_scratch(k, rows, cols):
    return [pltpu.VMEM((2, k, rows, cols), MXU_DTYPE), pltpu.SemaphoreType.DMA((2, k))]


def _merge_bwd(dyb, w_out, proj2, pa, pc, pm):
    T, D = dyb.shape
    IN = proj2.shape[1]
    tm, tn = _tile(T, 1024), _tile(D, 512)
    _, _, gates, tile = _merge_specs(T, D, tm, tn)
    nn = D // tn

    def body(dy_ref, w_ref, g0, g1, g2, p0, p1, p2, dp0, dp1, dp2, dproj_ref, stage, sems):
        i, n = pl.program_id(0), pl.program_id(1)
        dm = lax.dot_general(dy_ref[...], w_ref[...], _DIMS["nt"], preferred_element_type=F32)
        tiles, dsts = [], []
        for k, (g_ref, p_ref, dp_ref) in enumerate(((g0, p0, dp0), (g1, p1, dp1), (g2, p2, dp2))):
            gt = _sig(g_ref[...].astype(F32))
            dp_ref[...] = (gt * dm).astype(dp_ref.dtype)
            tiles.append(dm * p_ref[...].astype(F32) * gt * (1.0 - gt))
            dsts.append(dproj_ref.at[pl.ds(i * tm, tm), pl.ds(G0 + k * D + n * tn, tn)])
        _emit_tiles(i * nn + n, (T // tm) * nn, tiles, dsts, stage, sems)

    shp = jax.ShapeDtypeStruct((T, D), MXU_DTYPE)
    return _call(body, name="merge_bwd", grid=(T // tm, nn),
                 in_specs=[pl.BlockSpec((tm, D), lambda i, n: (i, 0)), pl.BlockSpec((tn, D), lambda i, n: (n, 0))]
                 + gates + [tile] * 3,
                 out_specs=[tile] * 3 + [ANY], out_shape=[shp] * 3 + [jax.ShapeDtypeStruct((T, IN), MXU_DTYPE)],
                 scratch_shapes=_emit_scratch(3, tm, tn))(dyb, w_out, proj2, proj2, proj2, pa, pc, pm)


def _out_loss(merged, w_out, x, tgt):
    T, D = x.shape
    tm = _tile(T, 512)

    def body(m_ref, w_ref, x_ref, t_ref, dy_ref, dyb_ref, loss_ref):
        err = x_ref[...] + jnp.dot(m_ref[...], w_ref[...], preferred_element_type=F32) - t_ref[...]
        dy = err * (1.0 / D)
        dy_ref[...] = dy
        dyb_ref[...] = dy.astype(dyb_ref.dtype)

        @pl.when(pl.program_id(0) == 0)
        def _():
            loss_ref[...] = jnp.zeros_like(loss_ref)

        loss_ref[...] += jnp.sum(err * err) * (0.5 / D)

    row = pl.BlockSpec((tm, D), lambda i: (i, 0))
    return _call(body, name="out_loss", grid=(T // tm,),
                 in_specs=[row, pl.BlockSpec((D, D), lambda i: (0, 0)), row, row],
                 out_specs=[row, row, pl.BlockSpec((1, 128), lambda i: (0, 0))],
                 out_shape=[jax.ShapeDtypeStruct((T, D), F32), jax.ShapeDtypeStruct((T, D), MXU_DTYPE),
                            jax.ShapeDtypeStruct((1, 128), F32)])(merged, w_out, x, tgt)


def _proj_chunk(hb, w, meta, j, nslots, half, buf, name):
    T, D = hb.shape
    Cs = w.shape[1] // 4
    tm, tn = _tile(T, 1024), _tile(Cs // 2, 2176)
    nh = Cs // 2 // tn
    per = nh if half is not None else 2 * nh

    def body(meta_ref, a_ref, b_ref, *rest):
        rest[-1][...] = jnp.dot(a_ref[...], b_ref[...], preferred_element_type=F32).astype(rest[-1].dtype)

    def tile(n, m):
        if half is None:
            return n % per
        return (m[4] if half == 0 else 1 - m[4]) * nh + n % per

    in_specs = [pl.BlockSpec((tm, D), lambda n, i, m: (i, 0)),
                pl.BlockSpec((D, tn), lambda n, i, m: (0, (j + n // per) * 2 * nh + tile(n, m)))]
    args = [meta, hb, w]
    if buf is not None:
        in_specs.append(ANY)
        args.append(buf)
    spec = pltpu.PrefetchScalarGridSpec(
        num_scalar_prefetch=1, grid=(nslots * per, T // tm), in_specs=in_specs,
        out_specs=pl.BlockSpec((tm, tn), lambda n, i, m: (i, m[j + n // per] * 2 * nh + tile(n, m))))
    return _call(body, name=name, grid_spec=spec, out_shape=jax.ShapeDtypeStruct((T, 4 * Cs), PROJ_DTYPE),
                 aliases={} if buf is None else {3: 0})(*args)


def _norms(x, mem, norm_g, mem_norm_g):
    D = x.shape[-1]
    hb, hbt = _rms_fwd(x.reshape(-1, D), norm_g.reshape(1, D), "rms_x")
    mhb, _ = _rms_fwd(mem.reshape(-1, D), mem_norm_g.reshape(1, D), "rms_mem")
    return hb, hbt, mhb


def _attention_fwd(proj2, Bl, gq_all, gk_all):
    T, IN = proj2.shape
    proj3 = proj2.reshape(Bl, T // Bl, IN)
    os, ls = [], []
    for g, d in enumerate(DILATIONS):
        o, l = _attn_fwd(proj3, gq_all[g:g + 1], gk_all[g:g + 1], g, d)
        os.append(o.reshape(T, GW))
        ls.append(l.reshape(T, GW))
    return os, ls, _combine_fwd(os, ls, proj2)


def _conv_branch_fwd(proj2, Bl, conv_w):
    T, IN = proj2.shape
    cc, cct = _conv_fwd(proj2.reshape(Bl, T // Bl, IN), conv_w)
    return cc.reshape(T, CONVW), cct


def _weight_grads(x, mem, tgt, norm_g, mem_norm_g, gq_all, gk_all, conv_w, mem_gq, mem_gk, W, pre, early=None):
    Bl, S, D = x.shape
    T = Bl * S
    hb, hbt, mhb, proj2, os, ls, (a, at), (cc, cct) = pre
    IN = proj2.shape[1]
    proj3 = proj2.reshape(Bl, S, IN)
    x2, tgt2 = x.reshape(T, D), tgt.reshape(T, D)
    mem2 = mem.reshape(-1, D)
    ng, mng = norm_g.reshape(1, D), mem_norm_g.reshape(1, D)
    mgq, mgk = mem_gq.reshape(1, MEM_HD), mem_gk.reshape(1, MEM_HD)
    gqs = [gq_all[g:g + 1] for g in range(NGROUP)]
    gks = [gk_all[g:g + 1] for g in range(NGROUP)]

    mkv = _matmul(mhb, W["mem_w_kv"], "nn", F32, name="mem_kv", tm=512, tn=1024, tk=D)
    mkv3 = mkv.reshape(Bl, -1, 2 * MEMW)
    mo, mot = _mem_fwd(proj3, mkv3, mgq, mgk)
    mo = mo.reshape(T, MEMW)
    merged, mergedt, pa, pc, pm = _merge_fwd(a, cc, mo, W["w_br_attn"], W["w_br_conv"], W["w_br_mem"], proj2)
    dy, dyb, loss = _out_loss(merged, W["w_out"], x2, tgt2)

    G = {}
    G["w_out"] = _matmul(mergedt, dyb, "nn", WIRE_DTYPE, name="dw_out", tm=1024, tn=512, tk=T)
    dpa, dpc, dpm, dproj = _merge_bwd(dyb, W["w_out"], proj2, pa, pc, pm)
    G["w_br_attn"] = _matmul(at, dpa, "nn", WIRE_DTYPE, name="dw_br_attn", tm=512, tn=512, tk=T)
    G["w_br_conv"] = _matmul(cct, dpc, "nn", WIRE_DTYPE, name="dw_br_conv", tm=1024, tn=512, tk=T)
    G["w_br_mem"] = _matmul(mot, dpm, "nn", WIRE_DTYPE, name="dw_br_mem", tm=1024, tn=512, tk=T)
    da = _matmul(dpa, W["w_br_attn"], "nt", F32, name="d_attn", tm=1024, tn=512, tk=D)
    dcc = _matmul(dpc, W["w_br_conv"], "nt", F32, name="d_conv", tm=1024, tn=1024, tk=D)
    dmo = _matmul(dpm, W["w_br_mem"], "nt", F32, name="d_mem", tm=1024, tn=1024, tk=D)
    dproj, dmkv3, dmgq, dmgk = _mem_bwd(proj3, mkv3, mgq, mgk, dmo.reshape(Bl, S, MEMW), dproj)
    dmkv = _cast(dmkv3.reshape(-1, 2 * MEMW), "cast_dmkv")
    G["mem_w_kv"] = _matmul(mhb, dmkv, "tn", WIRE_DTYPE, name="dw_mem_kv", tm=1024, tn=1024, tk=512)
    early_state, dmkv = (None, dmkv) if early is None else early[0](G, dmkv)
    dmh = _matmul(dmkv, W["mem_w_kv"], "nt", F32, name="d_memh", tm=512, tn=1024, tk=2 * MEMW)
    _, dmng = _rms_bwd(mem2, dmh, mng, None, "rms_mem_bwd")
    if early is not None:
        early_state, da = early[1](early_state, dmng, da)

    dos, dls, dproj = _combine_bwd(os, ls, proj2, da, dproj)
    dgq, dgk = [], []
    for g, d in enumerate(DILATIONS):
        dq, dk, dv, gq_g, gk_g = _attn_bwd(proj3, gqs[g], gks[g], os[g].reshape(Bl, S, GW), ls[g].reshape(Bl, S, GW),
                                           dos[g].reshape(Bl, S, GW), dls[g].reshape(Bl, S, GW), g, d)
        for c0, part in ((Q0, dq), (K0, dk), (V0, dv)):
            dproj = lax.dynamic_update_slice(dproj, part.reshape(T, GW), (0, c0 + g * GW))
        dgq.append(gq_g)
        dgk.append(gk_g)
    dproj, dconv_w = _conv_bwd(proj3, conv_w, dcc.reshape(Bl, S, CONVW), dproj)
    small = [loss, None, dmng] + dgq + dgk + [dconv_w.reshape(1, 3 * CONVW), dmgq, dmgk]
    return G, (dproj, x2, ng, dy, small), early_state


DW_IN_TILE = 1024


def _dw_in_half(hbt, dproj, pos, own, name, add=None, tiles=None, into=None):
    D, T = hbt.shape
    IN = dproj.shape[1]
    R, tn = D // 2, _tile(IN, DW_IN_TILE)
    j0, nj = (0, IN // tn) if tiles is None else tiles

    def body(pos_ref, a_ref, b_ref, *rest):
        acc = jnp.dot(a_ref[...], b_ref[...], preferred_element_type=F32)
        if add is not None:
            acc = acc + rest[0][...].astype(F32)
        rest[-1][...] = acc.astype(rest[-1].dtype)

    tile = pl.BlockSpec((R, tn), lambda j, p: (0, j0 + j))
    in_specs = [pl.BlockSpec((R, T), lambda j, p: (p[1] if own else 1 - p[1], 0)),
                pl.BlockSpec((T, tn), lambda j, p: (0, j0 + j))]
    args = [pos, hbt, dproj]
    if add is not None:
        in_specs.append(tile)
        args.append(add)
    if into is not None:
        in_specs.append(ANY)
        args.append(into)
    spec = pltpu.PrefetchScalarGridSpec(num_scalar_prefetch=1, grid=(nj,), in_specs=in_specs, out_specs=tile)
    return _call(body, name=name, grid_spec=spec, out_shape=jax.ShapeDtypeStruct((R, IN), WIRE_DTYPE),
                 aliases={} if into is None else {len(args) - 1: 0})(*args)


def _d_h(dproj, w, order):
    T, IN = dproj.shape
    D, Cs = w.shape[0], IN // 4
    tm, tn = _tile(T, 1024), _tile(D, 1024)

    def body(order_ref, a_ref, b_ref, o_ref, acc_ref):
        part = lax.dot_general(a_ref[...], b_ref[...], _DIMS["nt"], preferred_element_type=F32)
        k = pl.program_id(2)

        @pl.when(k == 0)
        def _():
            acc_ref[...] = part

        @pl.when(k > 0)
        def _():
            acc_ref[...] += part

        @pl.when(k == 3)
        def _():
            o_ref[...] = acc_ref[...]

    spec = pltpu.PrefetchScalarGridSpec(
        num_scalar_prefetch=1, grid=(T // tm, D // tn, 4),
        in_specs=[pl.BlockSpec((tm, Cs), lambda i, n, k, o: (i, o[k])), pl.BlockSpec((tn, Cs), lambda i, n, k, o: (n, k))],
        out_specs=pl.BlockSpec((tm, tn), lambda i, n, k, o: (i, n)), scratch_shapes=[pltpu.VMEM((tm, tn), F32)])
    return _call(body, name="d_h", grid_spec=spec, out_shape=jax.ShapeDtypeStruct((T, D), F32))(order, dproj, w)


def _input_grad(rest, w_in, order):
    dproj, x2, ng, dy, small = rest
    dh = _d_h(dproj, w_in, order)
    grad_x, dng = _rms_bwd(x2, dh, ng, dy, "rms_x_bwd")
    small = [dng if t is None else t for t in small]
    return grad_x, jnp.concatenate(small, axis=1)


def _local_step(x, mem, tgt, norm_g, mem_norm_g, gq_all, gk_all, conv_w, mem_gq, mem_gk, W):
    hb, hbt, mhb = _norms(x, mem, norm_g, mem_norm_g)
    Cs = W["w_in"].shape[1] // 4
    shards = (0, 2, 1, 3)
    order = jnp.array(shards, dtype=jnp.int32)
    w_rel = jnp.concatenate([W["w_in"][:, s * Cs:(s + 1) * Cs] for s in shards], axis=1)
    meta = jnp.array(shards + (0,), dtype=jnp.int32)
    proj2 = _proj_chunk(hb, w_rel, meta, 0, 1, None, None, "proj_0")
    for j, nslots in ((1, 2), (3, 1)):
        for half in (1, 0):
            proj2 = _proj_chunk(hb, w_rel, meta, j, nslots, half, proj2, f"proj_{j}_{half}")
    pre = (hb, hbt, mhb, proj2, *_attention_fwd(proj2, x.shape[0], gq_all, gk_all),
           _conv_branch_fwd(proj2, x.shape[0], conv_w))
    G, rest, _ = _weight_grads(x, mem, tgt, norm_g, mem_norm_g, gq_all, gk_all, conv_w, mem_gq, mem_gk, W, pre)
    pos = jnp.zeros((2,), jnp.int32)
    ntiles = rest[0].shape[1] // _tile(rest[0].shape[1], DW_IN_TILE)
    other = _dw_in_half(hbt, rest[0], pos, False, "dw_in_sibling_early", tiles=(0, ntiles - ntiles // 2))
    other = _dw_in_half(hbt, rest[0], pos, False, "dw_in_sibling_late", tiles=(ntiles - ntiles // 2, ntiles // 2),
                        into=other)
    G["w_in"] = jnp.concatenate([_dw_in_half(hbt, rest[0], pos, True, "dw_in_own"), other], axis=0)
    grad_x, small = _input_grad(rest, w_rel, order)
    return grad_x.reshape(x.shape), G, small


BIG = (("w_in", "col"), ("mem_w_kv", "row"), ("w_br_attn", "col"), ("w_br_conv", "col"),
       ("w_br_mem", "col"), ("w_out", "row"))


def _coords():
    return lax.axis_index("x"), lax.axis_index("y"), lax.axis_index("c")


def _other_chips(x, y):
    return [(1 - x, y), (x, 1 - y), (1 - x, 1 - y)]


def _half(ref, kind, c):
    R, C = ref.shape
    if kind == "col":
        return ref.at[pl.ds(c * (R // 2), R // 2), :]
    return ref.at[:, pl.ds(c * (C // 2), C // 2)]


def _shard(ref, kind, s):
    R, C = ref.shape
    if kind == "col":
        return ref.at[:, pl.ds(s * (C // 4), C // 4)]
    return ref.at[pl.ds(s * (R // 4), R // 4), :]


def _piece(ref, kind, s, c):
    R, C = ref.shape
    if kind == "col":
        return ref.at[pl.ds(c * (R // 2), R // 2), pl.ds(s * (C // 4), C // 4)]
    return ref.at[pl.ds(s * (R // 4), R // 4), pl.ds(c * (C // 2), C // 2)]


def _remote(src, dst, sems_s, sems_r, k, dev):
    return pltpu.make_async_remote_copy(src_ref=src, dst_ref=dst, send_sem=sems_s.at[k], recv_sem=sems_r.at[k],
                                        device_id=dev, device_id_type=MESH)


HBM = pl.BlockSpec(memory_space=pltpu.HBM)
SEM = pl.BlockSpec(memory_space=pltpu.SEMAPHORE)
EFFECT = pltpu.SideEffectType.DATAFLOW_SIDE_EFFECTING


def _hbm(a):
    return pltpu.with_memory_space_constraint(a, pltpu.HBM)


def _start_copies(name, arrays, ncopies, make):
    n = len(arrays)

    def body(*refs):
        for cp in make(refs[:n], refs[n], refs[n + 1]):
            cp.start()

    outs = pl.pallas_call(
        body, name=name,
        out_shape=(pltpu.SemaphoreType.DMA((ncopies,)), pltpu.SemaphoreType.DMA((ncopies,)),
                   *[jax.ShapeDtypeStruct(t.shape, t.dtype) for t in arrays]),
        in_specs=[HBM] * n, out_specs=(SEM, SEM, *([HBM] * n)),
        input_output_aliases={i: i + 2 for i in range(n)},
        compiler_params=pltpu.CompilerParams(has_side_effects=EFFECT),
    )(*[_hbm(t) for t in arrays])
    return outs[0], outs[1], list(outs[2:])


def _wait_copies(name, send, recv, arrays, make, after):
    n = len(arrays)

    def body(*refs):
        for cp in make(refs[:n], refs[n], refs[n + 1]):
            cp.wait_send()
            cp.wait_recv()

    outs = pl.pallas_call(
        body, name=name, out_shape=[jax.ShapeDtypeStruct(t.shape, t.dtype) for t in arrays],
        in_specs=[HBM] * n + [SEM, SEM, ANY], out_specs=[HBM] * n,
        input_output_aliases={i: i for i in range(n)},
        compiler_params=pltpu.CompilerParams(has_side_effects=EFFECT),
    )(*arrays, send, recv, after)
    return list(outs)


def _w_in_copies(relations):
    def make(refs, send, recv):
        x, y, c = _coords()
        me = 2 * x + y
        chips = _other_chips(x, y)
        w, conv = refs[0], refs[1]
        cps = []
        for i, k in enumerate(relations):
            cps.append(_remote(_column_half(w, 0, c), _column_half(w, 1 + k, c), send, recv, 2 * i, (*chips[k], c)))
            mine = _shard(conv, "col", me)
            cps.append(_remote(mine, mine, send, recv, 2 * i + 1, (*chips[k], c)))
        return cps
    return make


def _column_half(w, slot, c):
    half = w.shape[1] // 8
    return w.at[:, pl.ds((2 * slot + c) * half, half)]


def _w_in_forward(relations):
    def make(refs, send, recv):
        x, y, c = _coords()
        cps = []
        for i, k in enumerate(relations):
            got = _column_half(refs[0], 1 + k, c)
            cps.append(_remote(got, got, send, recv, i, (x, y, 1 - c)))
        return cps
    return make


def _sibling_columns(c0, width):
    def make(refs, send, recv):
        x, y, c = _coords()
        cols = pl.ds(c0, width)
        return [_remote(refs[0].at[:, cols], refs[1].at[:, cols], send, recv, 0, (x, y, 1 - c))]
    return make


def _other_weight_copies(refs, send, recv):
    x, y, c = _coords()
    me = 2 * x + y
    cps = []
    for k, chip in enumerate(_other_chips(x, y)):
        for p, (_, kind) in enumerate(BIG[1:]):
            mine = _piece(refs[p], kind, me, c)
            cps.append(_remote(mine, mine, send, recv, 3 * p + k, (*chip, c)))
    return cps


def _other_weight_forward(refs, send, recv):
    x, y, c = _coords()
    cps = []
    for k, chip in enumerate(_other_chips(x, y)):
        s = 2 * chip[0] + chip[1]
        for p, (_, kind) in enumerate(BIG[1:]):
            got = _piece(refs[p], kind, s, c)
            cps.append(_remote(got, got, send, recv, 3 * p + k, (x, y, 1 - c)))
    return cps


def _share_copies(group):
    def make(refs, send, recv):
        x, y, c = _coords()
        cps = []
        for p, (_, kind) in enumerate(group):
            mine = _half(refs[p], kind, c)
            cps.append(_remote(mine, mine, send, recv, p, (x, y, 1 - c)))
        return cps
    return make


def _sibling_halves_start(G, group, carry, tag):
    n = len(group)
    parts = [G[name] for name, _ in group]
    lands = []
    for (_, kind), g in zip(group, parts):
        R, C = g.shape
        lands.append(lax.empty((R // 2, C) if kind == "col" else (R, C // 2), g.dtype))

    def make(refs, send, recv):
        x, y, c = _coords()
        return [_remote(_half(refs[p], group[p][1], 1 - c), refs[n + p], send, recv, p, (x, y, 1 - c)) for p in range(n)]

    send, recv, thru = _start_copies("sibling_halves_start_" + tag, [*parts, *lands, carry], n, make)
    return (send, recv, thru[:2 * n], make, tag), thru[2 * n]


def _presums(state, group, pos, after):
    send, recv, arrays, make, tag = state
    n = len(group)
    thru = _wait_copies("sibling_halves_wait_" + tag, send, recv, arrays, make, after)
    return [_presum(thru[p], thru[n + p], kind, pos, "presum_" + name) for p, (name, kind) in enumerate(group)]


def _presum(g, got, kind, pos, name):
    R, C = got.shape
    tr, tc = _tile(R, 512, 16), _tile(C, 2048)
    nr, nc = R // tr, C // tc

    def body(pos_ref, a_ref, b_ref, o_ref):
        o_ref[...] = (a_ref[...].astype(F32) + b_ref[...].astype(F32)).astype(o_ref.dtype)

    blk = pl.BlockSpec((tr, tc), lambda i, j, pos_ref: (i, j))
    if kind == "col":
        mine = pl.BlockSpec((tr, tc), lambda i, j, pos_ref: (pos_ref[1] * nr + i, j))
    else:
        mine = pl.BlockSpec((tr, tc), lambda i, j, pos_ref: (i, pos_ref[1] * nc + j))
    spec = pltpu.PrefetchScalarGridSpec(num_scalar_prefetch=1, grid=(nr, nc), in_specs=[mine, blk], out_specs=blk)
    return _call(body, name=name, grid_spec=spec, out_shape=jax.ShapeDtypeStruct((R, C), WIRE_DTYPE))(pos, g, got)


def _chip_copies(group):
    n = len(group)

    def make(refs, send, recv):
        x, y, c = _coords()
        cps = []
        for k, chip in enumerate(_other_chips(x, y)):
            s = 2 * chip[0] + chip[1]
            for p in range(n):
                cps.append(_remote(_shard(refs[p], group[p][1], s), refs[n + p].at[k], send, recv, 3 * p + k, (*chip, c)))
        return cps
    return make


def _landing_zones(pres, group):
    lands = []
    for (_, kind), g in zip(group, pres):
        R, C = g.shape
        lands.append(lax.empty((3, R, C // 4) if kind == "col" else (3, R // 4, C), g.dtype))
    return lands


def _exchange_start(pres, group, carry, tag):
    n = len(group)
    make = _chip_copies(group)
    send, recv, thru = _start_copies("chip_exchange_start_" + tag, [*pres, *_landing_zones(pres, group), carry], 3 * n, make)
    return (send, recv, thru[:2 * n], make, tag), thru[2 * n]


def _exchange_wait(state, after):
    send, recv, arrays, make, tag = state
    thru = _wait_copies("chip_exchange_wait_" + tag, send, recv, arrays, make, after)
    n = len(thru) // 2
    return thru[:n], thru[n:]


def _reduce_into_shard(slots, pre, kind, pos, name):
    K, R, C = slots.shape
    tr, tc = _tile(R, 512, 16), _tile(C, 2176)
    nr, nc = R // tr, C // tc

    def body(pos_ref, s_ref, p_ref, o_ref):
        acc = p_ref[...].astype(F32)
        for k in range(K):
            acc = acc + s_ref[k].astype(F32)
        o_ref[...] = acc

    if kind == "col":
        own = pl.BlockSpec((tr, tc), lambda i, j, pos_ref: (i, pos_ref[0] * nc + j))
        full, out = (2 * R, C), pl.BlockSpec((tr, tc), lambda i, j, pos_ref: (pos_ref[1] * nr + i, j))
    else:
        own = pl.BlockSpec((tr, tc), lambda i, j, pos_ref: (pos_ref[0] * nr + i, j))
        full, out = (R, 2 * C), pl.BlockSpec((tr, tc), lambda i, j, pos_ref: (i, pos_ref[1] * nc + j))
    spec = pltpu.PrefetchScalarGridSpec(
        num_scalar_prefetch=1, grid=(nr, nc),
        in_specs=[pl.BlockSpec((K, tr, tc), lambda i, j, pos_ref: (0, i, j)), own], out_specs=out)
    return _call(body, name=name, grid_spec=spec, out_shape=jax.ShapeDtypeStruct(full, F32))(pos, slots, pre)


def _small_slots(pack, me):
    _, N = pack.shape

    def body(me_ref, p_ref, o_ref):
        o_ref[0] = p_ref[...]

    spec = pltpu.PrefetchScalarGridSpec(
        num_scalar_prefetch=1, grid=(1,), in_specs=[pl.BlockSpec((1, N), lambda i, me_ref: (0, 0))],
        out_specs=pl.BlockSpec((1, 1, N), lambda i, me_ref: (me_ref[0], 0, 0)))
    return _call(body, name="small_slots", grid_spec=spec, out_shape=jax.ShapeDtypeStruct((8, 1, N), pack.dtype))(me, pack)


def _small_copies(refs, send, recv):
    x, y, c = _coords()
    me = 4 * x + 2 * y + c
    cps = []
    for k in range(1, 8):
        dev = (x ^ (k >> 2), y ^ ((k >> 1) & 1), c ^ (k & 1))
        cps.append(_remote(refs[0], refs[1].at[me], send, recv, k - 1, dev))
    return cps


def _sum_small(slots):
    K, _, N = slots.shape

    def body(s_ref, o_ref):
        acc = s_ref[0]
        for k in range(1, K):
            acc = acc + s_ref[k]
        o_ref[...] = acc

    return _call(body, name="sum_small", in_specs=[pl.BlockSpec(memory_space=pltpu.VMEM)],
                 out_specs=pl.BlockSpec(memory_space=pltpu.VMEM), out_shape=jax.ShapeDtypeStruct((1, N), F32))(slots)


def _adamw(w, g, m, v, name, with_grad=False):
    R, C = w.shape
    tr, tc = _tile(R, 256, 8), _tile(C, 2176)

    def body(w_ref, g_ref, m_ref, v_ref, d_ref, nm_ref, nv_ref, *g_out):
        gv = g_ref[...]
        for ref in g_out:
            ref[...] = gv
        nm = ADAM_B1 * m_ref[...] + (1.0 - ADAM_B1) * gv
        nv = ADAM_B2 * v_ref[...] + (1.0 - ADAM_B2) * gv * gv
        m_hat = nm / (1.0 - ADAM_B1 ** ADAM_STEP)
        v_hat = nv / (1.0 - ADAM_B2 ** ADAM_STEP)
        d_ref[...] = -ADAM_LR * (m_hat / (jnp.sqrt(v_hat) + ADAM_EPS) + ADAM_WD * w_ref[...])
        nm_ref[...] = nm
        nv_ref[...] = nv

    spec = pl.BlockSpec((tr, tc), lambda i, j: (i, j))
    shp = jax.ShapeDtypeStruct((R, C), F32)
    nout = 4 if with_grad else 3
    return _call(body, name=name, grid=(R // tr, C // tc), in_specs=[spec] * 4, out_specs=[spec] * nout,
                 out_shape=[shp] * nout)(w, g, m, v)


SMALL = ("norm_g", "mem_norm_g", "attn_q_norm", "attn_k_norm", "conv_w", "mem_q_norm", "mem_k_norm")
WEIGHTS = ("norm_g", "mem_norm_g", "w_in", "attn_q_norm", "attn_k_norm", "conv_w", "mem_w_kv", "mem_q_norm",
           "mem_k_norm", "w_br_attn", "w_br_conv", "w_br_mem", "w_out")


def kernel(x, mem, norm_g, mem_norm_g, w_in, attn_q_norm, attn_k_norm, conv_w, mem_w_kv, mem_q_norm, mem_k_norm, w_br_attn, w_br_conv, w_br_mem, w_out, loss_target, m_norm_g, m_mem_norm_g, m_w_in, m_attn_q_norm, m_attn_k_norm, m_conv_w, m_mem_w_kv, m_mem_q_norm, m_mem_k_norm, m_w_br_attn, m_w_br_conv, m_w_br_mem, m_w_out, v_norm_g, v_mem_norm_g, v_w_in, v_attn_q_norm, v_attn_k_norm, v_conv_w, v_mem_w_kv, v_mem_q_norm, v_mem_k_norm, v_w_br_attn, v_w_br_conv, v_w_br_mem, v_w_out):
    w = dict(norm_g=norm_g, mem_norm_g=mem_norm_g, w_in=w_in, attn_q_norm=attn_q_norm, attn_k_norm=attn_k_norm,
             conv_w=conv_w, mem_w_kv=mem_w_kv, mem_q_norm=mem_q_norm, mem_k_norm=mem_k_norm, w_br_attn=w_br_attn,
             w_br_conv=w_br_conv, w_br_mem=w_br_mem, w_out=w_out)
    m = dict(norm_g=m_norm_g, mem_norm_g=m_mem_norm_g, w_in=m_w_in, attn_q_norm=m_attn_q_norm,
             attn_k_norm=m_attn_k_norm, conv_w=m_conv_w, mem_w_kv=m_mem_w_kv, mem_q_norm=m_mem_q_norm,
             mem_k_norm=m_mem_k_norm, w_br_attn=m_w_br_attn, w_br_conv=m_w_br_conv, w_br_mem=m_w_br_mem, w_out=m_w_out)
    v = dict(norm_g=v_norm_g, mem_norm_g=v_mem_norm_g, w_in=v_w_in, attn_q_norm=v_attn_q_norm,
             attn_k_norm=v_attn_k_norm, conv_w=v_conv_w, mem_w_kv=v_mem_w_kv, mem_q_norm=v_mem_q_norm,
             mem_k_norm=v_mem_k_norm, w_br_attn=v_w_br_attn, w_br_conv=v_w_br_conv, w_br_mem=v_w_br_mem, w_out=v_w_out)
    Bl, _, D = x.shape
    cx, cy = lax.axis_index("x"), lax.axis_index("y")
    chip = 2 * cx + cy
    pos = jnp.stack([chip, lax.axis_index("c")]).astype(jnp.int32)
    order = jnp.stack([chip] + [2 * a + b for a, b in _other_chips(cx, cy)]).astype(jnp.int32)
    n = len(BIG)

    slot0 = jnp.stack([jnp.zeros((), jnp.int32), pos[1]])
    w_rel = _place_shard(w["w_in"], "col", slot0, WIRE_DTYPE, "place_w_in_sent", half=0)
    conv_full = _place_shard(conv_w, "col", pos, F32, "place_conv_w")
    others = [_place_shard(w[name], kind, pos, WIRE_DTYPE, "place_" + name) for name, kind in BIG[1:]]
    hb, hbt, mhb = _norms(x, mem, norm_g, mem_norm_g)

    meta = jnp.concatenate([order, pos[1:]])
    near, near_fwd = _w_in_copies((0, 1)), _w_in_forward((0, 1))
    send, recv, (w_rel, conv_full) = _start_copies("gather_near_start", [w_rel, conv_full], 4, near)
    w_rel = _place_shard(w["w_in"], "col", slot0, WIRE_DTYPE, "place_w_in_kept", half=1, into=w_rel)
    proj = _proj_chunk(hb, w_rel, meta, 0, 1, None, None, "proj_own")
    w_rel, conv_full, *others = _wait_copies("gather_near_wait", send, recv, [w_rel, conv_full, *others], near, proj)

    fsend, frecv, (w_rel,) = _start_copies("gather_near_forward_start", [w_rel], 2, near_fwd)
    far, far_fwd = _w_in_copies((2,)), _w_in_forward((2,))
    send, recv, (w_rel, conv_full) = _start_copies("gather_far_start", [w_rel, conv_full], 2, far)
    proj = _proj_chunk(hb, w_rel, meta, 1, 2, 0, proj, "proj_near_landed")
    w_rel, = _wait_copies("gather_near_forward_wait", fsend, frecv, [w_rel], near_fwd, proj)
    proj = _proj_chunk(hb, w_rel, meta, 1, 2, 1, proj, "proj_near_forwarded")
    w_rel, conv_full = _wait_copies("gather_far_wait", send, recv, [w_rel, conv_full], far, proj)

    fsend, frecv, (w_rel,) = _start_copies("gather_far_forward_start", [w_rel], 1, far_fwd)
    send, recv, (*others, w_rel) = _start_copies("gather_rest_start", [*others, w_rel], 3 * (n - 1), _other_weight_copies)
    proj = _proj_chunk(hb, w_rel, meta, 3, 1, 0, proj, "proj_far_landed")
    w_rel, = _wait_copies("gather_far_forward_wait", fsend, frecv, [w_rel], far_fwd, proj)
    proj = _proj_chunk(hb, w_rel, meta, 3, 1, 1, proj, "proj_far_forwarded")
    os, ls, a = _attention_fwd(proj, Bl, attn_q_norm, attn_k_norm)
    *others, w_rel = _wait_copies("gather_rest_wait", send, recv, [*others, w_rel], _other_weight_copies, a[0])
    fsend, frecv, (*others, proj) = _start_copies("gather_rest_forward_start", [*others, proj], 3 * (n - 1),
                                                  _other_weight_forward)
    cc = _conv_branch_fwd(proj, Bl, conv_full)
    others = _wait_copies("gather_rest_forward_wait", fsend, frecv, others, _other_weight_forward, cc[0])
    W = {name: others[p] for p, (name, _) in enumerate(BIG[1:])}

    def rest_halves(G, carry):
        return _sibling_halves_start(G, BIG[1:], carry, "rest")

    def rest_exchange(state, after, carry):
        return _exchange_start(_presums(state, BIG[1:], pos, after), BIG[1:], carry, "rest")

    G, rest, rest_state = _weight_grads(
        x, mem, loss_target, norm_g, mem_norm_g, attn_q_norm, attn_k_norm, conv_full, mem_q_norm, mem_k_norm, W,
        (hb, hbt, mhb, proj, os, ls, a, cc), early=(rest_halves, rest_exchange))

    tn = _tile(rest[0].shape[1], DW_IN_TILE)
    ntiles = rest[0].shape[1] // tn
    first = ntiles - ntiles // 2
    early_cols = _sibling_columns(0, first * tn)
    late_cols = _sibling_columns(first * tn, (ntiles - first) * tn)
    for_sibling = _dw_in_half(hbt, rest[0], pos, False, "dw_in_sibling_early", tiles=(0, first))
    send, recv, (for_sibling, got, dproj) = _start_copies(
        "sibling_w_in_start_early", [for_sibling, lax.empty(for_sibling.shape, for_sibling.dtype), rest[0]], 1, early_cols)
    for_sibling = _dw_in_half(hbt, dproj, pos, False, "dw_in_sibling_late", tiles=(first, ntiles - first),
                              into=for_sibling)
    lsend, lrecv, (for_sibling, got) = _start_copies("sibling_w_in_start_late", [for_sibling, got], 1, late_cols)
    pres_rest, slots_rest = _exchange_wait(rest_state, for_sibling)
    reds_rest = [_reduce_into_shard(slots_rest[p], pres_rest[p], kind, pos, "reduce_" + name)
                 for p, (name, kind) in enumerate(BIG[1:])]
    share_rest = _share_copies(BIG[1:])
    rsend, rrecv, reds_rest = _start_copies("share_rest_start", reds_rest, n - 1, share_rest)
    for_sibling, got = _wait_copies("sibling_w_in_wait_early", send, recv, [for_sibling, got], early_cols, reds_rest[0])
    for_sibling, got = _wait_copies("sibling_w_in_wait_late", lsend, lrecv, [for_sibling, got], late_cols, reds_rest[0])
    pre_w_in = _dw_in_half(hbt, dproj, pos, True, "dw_in_own", add=got)

    w_in_state, dproj = _exchange_start([pre_w_in], BIG[:1], dproj, "w_in")
    grad_x, small = _input_grad((dproj, *rest[1:]), w_rel, order)
    pres, slots = _exchange_wait(w_in_state, grad_x)
    red_w_in = _reduce_into_shard(slots[0], pres[0], "col", pos, "reduce_w_in")
    share_w_in = _share_copies(BIG[:1])
    wsend, wrecv, (red_w_in, small) = _start_copies("share_w_in_start", [red_w_in, small], 1, share_w_in)
    grad_x = grad_x.reshape(x.shape)

    slots = _small_slots(small, (2 * pos[:1] + pos[1:]))
    ssend, srecv, (small, slots) = _start_copies("gather_small_start", [small, slots], 7, _small_copies)
    reds_rest = _wait_copies("share_rest_wait", rsend, rrecv, reds_rest, share_rest, slots)
    grads = dict(zip([name for name, _ in BIG[1:]], reds_rest))
    delta, new_m, new_v = {}, {}, {}
    for name, _ in BIG[1:]:
        delta[name], new_m[name], new_v[name], grads[name] = _adamw(w[name], grads[name], m[name], v[name],
                                                                    "adamw_" + name, with_grad=True)

    small, slots = _wait_copies("gather_small_wait", ssend, srecv, [small, slots], _small_copies, delta[BIG[-1][0]])
    tot = _sum_small(slots)[0]
    loss = tot[0]
    off = 128
    for name, size in (("norm_g", D), ("mem_norm_g", D), ("attn_q_norm", NGROUP * HEAD), ("attn_k_norm", NGROUP * HEAD),
                       ("conv_w", 3 * CONVW), ("mem_q_norm", MEM_HD), ("mem_k_norm", MEM_HD)):
        grads[name] = tot[off:off + size]
        off += size
    cw = conv_w.shape[1]
    grads["conv_w"] = lax.dynamic_slice(grads["conv_w"].reshape(3, CONVW), (0, chip * cw), (3, cw))
    for name in SMALL:
        grads[name] = grads[name].reshape(w[name].shape)

    def packed(t):
        return jnp.concatenate([t[name].reshape(1, -1) for name in SMALL], axis=1)

    ds, ms, vs = _adamw(packed(w), packed(grads), packed(m), packed(v), "adamw_small")
    shared, = _wait_copies("share_w_in_wait", wsend, wrecv, [red_w_in], share_w_in, ds)
    delta["w_in"], new_m["w_in"], new_v["w_in"], grads["w_in"] = _adamw(w["w_in"], shared, m["w_in"], v["w_in"],
                                                                        "adamw_w_in", with_grad=True)
    off = 0
    for name in SMALL:
        size = w[name].size
        delta[name] = ds[0, off:off + size].reshape(w[name].shape)
        new_m[name] = ms[0, off:off + size].reshape(w[name].shape)
        new_v[name] = vs[0, off:off + size].reshape(w[name].shape)
        off += size

    return (loss, grad_x, *[grads[n] for n in WEIGHTS], *[delta[n] for n in WEIGHTS],
            *[new_m[n] for n in WEIGHTS], *[new_v[n] for n in WEIGHTS])
```

```python
import functools

import jax
import jax.numpy as jnp
from jax import lax
from jax.experimental import pallas as pl
from jax.experimental.pallas import tpu as pltpu

F32 = jnp.float32
MXU_DTYPE = jnp.bfloat16
WIRE_DTYPE = jnp.bfloat16
PROJ_DTYPE = jnp.bfloat16
EPS = 1e-6
NEG = -1e30

HEAD = 128
HPG = 4
GW = HPG * HEAD
DILATIONS = (1, 4, 16)
NGROUP = len(DILATIONS)
BLK = 128
QKV = NGROUP * GW
CONVW = 1024
MEM_HEADS = 4
MEM_HD = 256
MEMW = MEM_HEADS * MEM_HD
Q0, K0, V0 = 0, QKV, 2 * QKV
ZA = 3 * QKV
CB, CC, CV, ZC = ZA + GW, ZA + GW + CONVW, ZA + GW + 2 * CONVW, ZA + GW + 3 * CONVW
MQ = ZC + CONVW
ZM = MQ + MEMW
G0 = ZM + MEMW

ADAM_LR, ADAM_B1, ADAM_B2, ADAM_EPS, ADAM_WD, ADAM_STEP = 0.001, 0.9, 0.999, 1e-08, 0.01, 10

VMEM_LIMIT = 56 * 1024 * 1024
MESH = pl.DeviceIdType.MESH
ANY = pl.BlockSpec(memory_space=pl.ANY)


def _tile(n, pref, mult=128):
    t = min(pref, n)
    while t > mult and (n % t or t % mult):
        t -= mult
    assert n % t == 0, (n, pref)
    return t


def _call(body, *, name, out_shape, grid=(), in_specs=None, out_specs=None, scratch_shapes=(),
          aliases=None, grid_spec=None):
    kw = {}
    if grid_spec is not None:
        kw["grid_spec"] = grid_spec
        ngrid = len(grid_spec.grid)
    else:
        kw.update(grid=grid, in_specs=in_specs, out_specs=out_specs, scratch_shapes=list(scratch_shapes))
        ngrid = len(grid)
    params = pltpu.CompilerParams(dimension_semantics=("arbitrary",) * ngrid, vmem_limit_bytes=VMEM_LIMIT)
    return pl.pallas_call(body, name=name, out_shape=out_shape, compiler_params=params,
                          input_output_aliases=aliases or {}, **kw)


_DIMS = {"nn": (((1,), (0,)), ((), ())), "nt": (((1,), (1,)), ((), ())), "tn": (((0,), (0,)), ((), ()))}


def _mxu(a, b, mode):
    return lax.dot_general(a.astype(MXU_DTYPE), b.astype(MXU_DTYPE), _DIMS[mode], preferred_element_type=F32)


@functools.partial(jax.custom_vjp, nondiff_argnums=(2,))
def _dot(a, b, mode):
    return _mxu(a, b, mode)


def _dot_fwd(a, b, mode):
    return _mxu(a, b, mode), (a, b)


def _dot_bwd(mode, res, g):
    a, b = res
    if mode == "nn":
        return _mxu(g, b, "nt"), _mxu(a, g, "tn")
    if mode == "nt":
        return _mxu(g, b, "nn"), _mxu(g, a, "tn")
    return _mxu(b, g, "nt"), _mxu(a, g, "nn")


_dot.defvjp(_dot_fwd, _dot_bwd)


def _sig(z):
    return 1.0 / (1.0 + jnp.exp(-z))


def _silu(z):
    return z * _sig(z)


def _rms_rows(t, g):
    return t * lax.rsqrt(jnp.mean(t * t, axis=-1, keepdims=True) + EPS) * g


def _attn_block(q, k2, v2, gq, gk, first):
    qn = _rms_rows(q, gq)
    kn = _rms_rows(k2, gk)
    s = jnp.where(_band_mask(first, k2.shape[0]), _mxu(qn, kn, "nt") * (HEAD ** -0.5), NEG)
    m = jnp.max(s, axis=-1, keepdims=True)
    p = jnp.exp(s - m)
    den = jnp.sum(p, axis=-1, keepdims=True)
    o = _mxu(p, v2, "nn") / den
    return o, m + jnp.log(den)


def _band_mask(first, nkeys):
    a = lax.broadcasted_iota(jnp.int32, (BLK, nkeys), 0)
    b = lax.broadcasted_iota(jnp.int32, (BLK, nkeys), 1)
    if nkeys == BLK:
        return b <= a
    return (b >= a) & (b <= a + BLK) & (b >= jnp.where(first, BLK, 0))


def _norm_parts(t):
    r = lax.rsqrt(jnp.mean(t * t, axis=-1, keepdims=True) + EPS)
    return r, t * r


def _norm_bwd(dn, g, r, th):
    dth = dn * g
    return r * (dth - th * jnp.mean(dth * th, axis=-1, keepdims=True)), jnp.sum(dn * th, axis=0, keepdims=True)


def _attn_block_bwd(q, k2, v2, gq, gk, first, do, o, lse, dlse):
    scale = HEAD ** -0.5
    rq, qh = _norm_parts(q)
    rk, kh = _norm_parts(k2)
    qn, kn = qh * gq, kh * gk
    s = jnp.where(_band_mask(first, k2.shape[0]), _mxu(qn, kn, "nt") * scale, NEG)
    p = jnp.exp(s - lse)
    ds = p * (_mxu(do, v2, "nt") + (dlse - jnp.sum(do * o, axis=-1, keepdims=True))) * scale
    dq, dgq = _norm_bwd(_mxu(ds, kn, "nn"), gq, rq, qh)
    dk2, dgk = _norm_bwd(_mxu(ds, qn, "tn"), gk, rk, kh)
    return dq, dk2, _mxu(p, do, "tn"), dgq, dgk


def _combine(o1, o2, o3, l1, l2, l3, z):
    m = lax.stop_gradient(jnp.maximum(jnp.maximum(l1, l2), l3))
    e1, e2, e3 = jnp.exp(l1 - m), jnp.exp(l2 - m), jnp.exp(l3 - m)
    return (e1 * o1 + e2 * o2 + e3 * o3) / (e1 + e2 + e3) * _silu(z)


def _mem_block(q, z, kv, gq, gk):
    outs = []
    for h in range(MEM_HEADS):
        sl = slice(h * MEM_HD, (h + 1) * MEM_HD)
        qn = _rms_rows(q[:, sl], gq)
        kn = _rms_rows(kv[:, sl], gk)
        s = _dot(qn, kn, "nt") * (MEM_HD ** -0.5)
        m = lax.stop_gradient(jnp.max(s, axis=-1, keepdims=True))
        p = jnp.exp(s - m)
        den = jnp.sum(p, axis=-1, keepdims=True)
        outs.append(_dot(p, kv[:, MEMW + h * MEM_HD:MEMW + (h + 1) * MEM_HD], "nn") / den)
    return jnp.concatenate(outs, axis=-1) * _silu(z)


def _cast(w, name):
    R, C = w.shape
    tr, tc = _tile(R, 512, 8), _tile(C, 2176)

    def body(w_ref, o_ref):
        o_ref[...] = w_ref[...].astype(o_ref.dtype)

    spec = pl.BlockSpec((tr, tc), lambda i, j: (i, j))
    return _call(body, name=name, grid=(R // tr, C // tc), in_specs=[spec], out_specs=spec,
                 out_shape=jax.ShapeDtypeStruct((R, C), WIRE_DTYPE))(w)


def _place_shard(w, kind, pos, dtype, name, slot=0, into=None, half=None):
    R, C = w.shape
    tr, tc = _tile(R, 512, 8), _tile(C if half is None else C // 2, 2176)
    nr, nc = R // tr, C // tc
    ncols = nc if half is None else nc // 2

    def body(pos_ref, w_ref, *rest):
        rest[-1][...] = w_ref[...].astype(rest[-1].dtype)

    def col(j, pos_ref):
        if half is None:
            return j
        return (pos_ref[1] if half == 0 else 1 - pos_ref[1]) * ncols + j

    if kind == "col":
        full = (R, 4 * C)
        out = pl.BlockSpec((tr, tc), lambda i, j, pos_ref: (i, pos_ref[slot] * nc + col(j, pos_ref)))
    else:
        full, out = (4 * R, C), pl.BlockSpec((tr, tc), lambda i, j, pos_ref: (pos_ref[slot] * nr + i, j))
    in_specs, args = [pl.BlockSpec((tr, tc), lambda i, j, pos_ref: (i, col(j, pos_ref)))], [pos, w]
    if into is not None:
        in_specs.append(ANY)
        args.append(into)
    spec = pltpu.PrefetchScalarGridSpec(num_scalar_prefetch=1, grid=(nr, ncols), in_specs=in_specs, out_specs=out)
    return _call(body, name=name, grid_spec=spec, out_shape=jax.ShapeDtypeStruct(full, dtype),
                 aliases={} if into is None else {2: 0})(*args)


def _matmul(a, b, mode, out_dtype, *, name, tm=512, tn=512, tk=512):
    if mode == "nn":
        (M, K), (_, N) = a.shape, b.shape
    elif mode == "nt":
        (M, K), (N, _) = a.shape, b.shape
    else:
        (K, M), (_, N) = a.shape, b.shape
    tm, tn, tk = _tile(M, tm), _tile(N, tn), _tile(K, tk)
    nk = K // tk

    def body(a_ref, b_ref, o_ref, *acc):
        part = lax.dot_general(a_ref[...], b_ref[...], _DIMS[mode], preferred_element_type=F32)
        if nk == 1:
            o_ref[...] = part.astype(o_ref.dtype)
            return
        acc_ref, = acc
        k = pl.program_id(2)

        @pl.when(k == 0)
        def _():
            acc_ref[...] = part

        @pl.when(k > 0)
        def _():
            acc_ref[...] += part

        @pl.when(k == nk - 1)
        def _():
            o_ref[...] = acc_ref[...].astype(o_ref.dtype)

    a_spec = pl.BlockSpec((tk, tm), lambda i, j, k: (k, i)) if mode == "tn" else pl.BlockSpec((tm, tk), lambda i, j, k: (i, k))
    b_spec = pl.BlockSpec((tn, tk), lambda i, j, k: (j, k)) if mode == "nt" else pl.BlockSpec((tk, tn), lambda i, j, k: (k, j))
    return _call(body, name=name, grid=(M // tm, N // tn, nk), in_specs=[a_spec, b_spec],
                 out_specs=pl.BlockSpec((tm, tn), lambda i, j, k: (i, j)),
                 out_shape=jax.ShapeDtypeStruct((M, N), out_dtype),
                 scratch_shapes=[] if nk == 1 else [pltpu.VMEM((tm, tn), F32)])(a, b)


def _rms_fwd(x, g, name):
    R, D = x.shape
    tr = _tile(R, 512)

    def body(x_ref, g_ref, o_ref, t_ref):
        y = _rms_rows(x_ref[...], g_ref[...])
        o_ref[...] = y.astype(o_ref.dtype)
        t_ref[...] = y.T.astype(t_ref.dtype)

    row = pl.BlockSpec((tr, D), lambda i: (i, 0))
    return _call(body, name=name, grid=(R // tr,), in_specs=[row, pl.BlockSpec((1, D), lambda i: (0, 0))],
                 out_specs=[row, pl.BlockSpec((D, tr), lambda i: (0, i))],
                 out_shape=[jax.ShapeDtypeStruct((R, D), MXU_DTYPE), jax.ShapeDtypeStruct((D, R), MXU_DTYPE)])(x, g)


def _rms_bwd(x, dh, g, dy, name):
    R, D = x.shape
    tr = _tile(R, 256)
    with_dx = dy is not None

    def body(*refs):
        if with_dx:
            x_ref, dh_ref, g_ref, dy_ref, dx_ref, dg_ref = refs
        else:
            x_ref, dh_ref, g_ref, dg_ref = refs
        xv, dhv = x_ref[...], dh_ref[...]
        r = lax.rsqrt(jnp.mean(xv * xv, axis=-1, keepdims=True) + EPS)
        xh = xv * r

        @pl.when(pl.program_id(0) == 0)
        def _():
            dg_ref[...] = jnp.zeros_like(dg_ref)

        dg_ref[...] += jnp.sum(dhv * xh, axis=0, keepdims=True)
        if with_dx:
            dxh = dhv * g_ref[...]
            dx_ref[...] = dy_ref[...] + r * (dxh - xh * jnp.mean(dxh * xh, axis=-1, keepdims=True))

    row = pl.BlockSpec((tr, D), lambda i: (i, 0))
    vec = pl.BlockSpec((1, D), lambda i: (0, 0))
    dg_shape = jax.ShapeDtypeStruct((1, D), F32)
    if with_dx:
        return _call(body, name=name, grid=(R // tr,), in_specs=[row, row, vec, row], out_specs=[row, vec],
                     out_shape=[jax.ShapeDtypeStruct((R, D), F32), dg_shape])(x, dh, g, dy)
    return None, _call(body, name=name, grid=(R // tr,), in_specs=[row, row, vec], out_specs=vec,
                       out_shape=dg_shape)(x, dh, g)


def _attn_geom(g, d):
    hc = HPG if d == 1 else 1
    cw = hc * HEAD
    cq, ck, cv = (Q0 + g * GW) // cw, (K0 + g * GW) // cw, (V0 + g * GW) // cw
    return (1, BLK * d, cw), hc, HPG // hc, cq, ck, cv


def _rows(ref, r, d, sl):
    if d == 1:
        return ref[0, :, sl]
    return ref.at[0][pl.ds(r, BLK, stride=d), sl]


def _set_rows(ref, r, d, sl, val):
    if d == 1:
        ref[0, :, sl] = val
    else:
        ref.at[0][pl.ds(r, BLK, stride=d), sl] = val


def _stage_rows(ref, r, d, sl, val):
    if d == 1:
        ref[:, sl] = val
    else:
        ref[pl.ds(r, BLK, stride=d), sl] = val


def _proj_stages(blk, d):
    return [] if d == 1 else [pltpu.VMEM(blk[1:], F32)] * 5


def _proj_rows(refs, stages, d):
    if d == 1:
        return [lambda r, sl, ref=ref: ref[0, :, sl].astype(F32) for ref in refs]
    for ref, stage in zip(refs, stages):
        stage[...] = ref[0].astype(F32)
    return [lambda r, sl, stage=stage: stage[pl.ds(r, BLK, stride=d), sl] for stage in stages]


def _attn_fwd(proj3, gq, gk, g, d):
    Bl, S, _ = proj3.shape
    blk, hc, ncb, cq, ck, cv = _attn_geom(g, d)
    nb = S // blk[1]
    if nb == 1:
        return _attn_single_fwd(proj3, gq, gk, g, d)

    def body(q_ref, kp_ref, kc_ref, vp_ref, vc_ref, gq_ref, gk_ref, o_ref, lse_ref, *stages):
        first = pl.program_id(2) == 0
        q, kp, kc, vp, vc = _proj_rows((q_ref, kp_ref, kc_ref, vp_ref, vc_ref), stages, d)
        def run(alone):
            for r in range(d):
                for h in range(hc):
                    sl = slice(h * HEAD, (h + 1) * HEAD)
                    if alone:
                        k2, v2 = kc(r, sl), vc(r, sl)
                    else:
                        k2 = jnp.concatenate([kp(r, sl), kc(r, sl)], axis=0)
                        v2 = jnp.concatenate([vp(r, sl), vc(r, sl)], axis=0)
                    o, lse = _attn_block(q(r, sl), k2, v2, gq_ref[...], gk_ref[...], False)
                    _set_rows(o_ref, r, d, sl, o)
                    _set_rows(lse_ref, r, d, sl, jnp.broadcast_to(lse, (BLK, HEAD)))

        pl.when(first)(lambda: run(True))
        pl.when(jnp.logical_not(first))(lambda: run(False))

    def cur(c0):
        return pl.BlockSpec(blk, lambda b, j, i: (b, i, c0 + j))

    def prev(c0):
        return pl.BlockSpec(blk, lambda b, j, i: (b, jnp.maximum(i - 1, 0), c0 + j))

    vec = pl.BlockSpec((1, HEAD), lambda b, j, i: (0, 0))
    out = pl.BlockSpec(blk, lambda b, j, i: (b, i, j))
    shp = jax.ShapeDtypeStruct((Bl, S, GW), F32)
    return _call(body, name=f"attn_fwd_g{g}", grid=(Bl, ncb, nb),
                 in_specs=[cur(cq), prev(ck), cur(ck), prev(cv), cur(cv), vec, vec],
                 out_specs=[out, out], out_shape=[shp, shp], scratch_shapes=_proj_stages(blk, d),
                 )(proj3, proj3, proj3, proj3, proj3, gq, gk)


def _attn_single_fwd(proj3, gq, gk, g, d):
    Bl, S, _ = proj3.shape
    blk, hc, ncb, cq, ck, cv = _attn_geom(g, d)

    def body(q_ref, k_ref, v_ref, gq_ref, gk_ref, o_ref, lse_ref, *stages):
        q, k, v = _proj_rows((q_ref, k_ref, v_ref), stages, d)
        for r in range(d):
            for h in range(hc):
                sl = slice(h * HEAD, (h + 1) * HEAD)
                o, lse = _attn_block(q(r, sl), k(r, sl), v(r, sl), gq_ref[...], gk_ref[...], True)
                _set_rows(o_ref, r, d, sl, o)
                _set_rows(lse_ref, r, d, sl, jnp.broadcast_to(lse, (BLK, HEAD)))

    def at(c0):
        return pl.BlockSpec(blk, lambda b, j: (b, 0, c0 + j))

    vec = pl.BlockSpec((1, HEAD), lambda b, j: (0, 0))
    shp = jax.ShapeDtypeStruct((Bl, S, GW), F32)
    return _call(body, name=f"attn_fwd_g{g}", grid=(Bl, ncb), in_specs=[at(cq), at(ck), at(cv), vec, vec],
                 out_specs=[at(0), at(0)], out_shape=[shp, shp], scratch_shapes=_proj_stages(blk, d)[:3],
                 )(proj3, proj3, proj3, gq, gk)


def _attn_single_bwd(proj3, gq, gk, o3, l3, do3, dl3, g, d):
    Bl, S, _ = proj3.shape
    blk, hc, ncb, cq, ck, cv = _attn_geom(g, d)

    def body(q_ref, k_ref, v_ref, gq_ref, gk_ref, o_ref, l_ref, do_ref, dl_ref,
             dq_ref, dk_ref, dv_ref, dgq_ref, dgk_ref, sq_ref, sk_ref, sv_ref, *stages):
        @pl.when((pl.program_id(0) == 0) & (pl.program_id(1) == 0))
        def _():
            dgq_ref[...] = jnp.zeros_like(dgq_ref)
            dgk_ref[...] = jnp.zeros_like(dgk_ref)

        dgq, dgk = jnp.zeros((1, HEAD), F32), jnp.zeros((1, HEAD), F32)
        q, k, v = _proj_rows((q_ref, k_ref, v_ref), stages, d)
        for r in range(d):
            for h in range(hc):
                sl = slice(h * HEAD, (h + 1) * HEAD)
                dq, dk, dv, a, b = _attn_block_bwd(
                    q(r, sl), k(r, sl), v(r, sl), gq_ref[...], gk_ref[...], True, _rows(do_ref, r, d, sl),
                    _rows(o_ref, r, d, sl), _rows(l_ref, r, d, sl)[:, :1], _rows(dl_ref, r, d, sl)[:, :1])
                _stage_rows(sq_ref, r, d, sl, dq)
                _stage_rows(sk_ref, r, d, sl, dk)
                _stage_rows(sv_ref, r, d, sl, dv)
                dgq, dgk = dgq + a, dgk + b
        dgq_ref[...] += dgq
        dgk_ref[...] += dgk
        dq_ref[0] = sq_ref[...].astype(dq_ref.dtype)
        dk_ref[0] = sk_ref[...].astype(dk_ref.dtype)
        dv_ref[0] = sv_ref[...].astype(dv_ref.dtype)

    def at(c0):
        return pl.BlockSpec(blk, lambda b, j: (b, 0, c0 + j))

    vec = pl.BlockSpec((1, HEAD), lambda b, j: (0, 0))
    shp = jax.ShapeDtypeStruct((Bl, S, GW), MXU_DTYPE)
    gshp = jax.ShapeDtypeStruct((1, HEAD), F32)
    return _call(body, name=f"attn_bwd_g{g}", grid=(Bl, ncb),
                 in_specs=[at(cq), at(ck), at(cv), vec, vec, at(0), at(0), at(0), at(0)],
                 out_specs=[at(0), at(0), at(0), vec, vec], out_shape=[shp, shp, shp, gshp, gshp],
                 scratch_shapes=[pltpu.VMEM(blk[1:], F32)] * 3 + _proj_stages(blk, d)[:3],
                 )(proj3, proj3, proj3, gq, gk, o3, l3, do3, dl3)


def _attn_bwd(proj3, gq, gk, o3, l3, do3, dl3, g, d):
    Bl, S, _ = proj3.shape
    blk, hc, ncb, cq, ck, cv = _attn_geom(g, d)
    nb = S // blk[1]
    if nb == 1:
        return _attn_single_bwd(proj3, gq, gk, o3, l3, do3, dl3, g, d)

    def body(q_ref, kp_ref, kc_ref, vp_ref, vc_ref, gq_ref, gk_ref, o_ref, l_ref, do_ref, dl_ref,
             dq_ref, dk_ref, dv_ref, dgq_ref, dgk_ref, ck_ref, cv_ref, sq_ref, sk_ref, sv_ref, *stages):
        i = pl.program_id(2)
        first = i == 0

        @pl.when((pl.program_id(0) == 0) & (pl.program_id(1) == 0) & first)
        def _():
            dgq_ref[...] = jnp.zeros_like(dgq_ref)
            dgk_ref[...] = jnp.zeros_like(dgk_ref)

        def run(alone):
            dgq, dgk = jnp.zeros((1, HEAD), F32), jnp.zeros((1, HEAD), F32)
            q, kp, kc, vp, vc = _proj_rows((q_ref, kp_ref, kc_ref, vp_ref, vc_ref), stages, d)
            for r in range(d):
                rs = slice(r * BLK, (r + 1) * BLK)
                for h in range(hc):
                    sl = slice(h * HEAD, (h + 1) * HEAD)
                    if alone:
                        k2, v2 = kc(r, sl), vc(r, sl)
                    else:
                        k2 = jnp.concatenate([kp(r, sl), kc(r, sl)], axis=0)
                        v2 = jnp.concatenate([vp(r, sl), vc(r, sl)], axis=0)
                    dq, dk2, dv2, a, b = _attn_block_bwd(
                        q(r, sl), k2, v2, gq_ref[...], gk_ref[...], False, _rows(do_ref, r, d, sl),
                        _rows(o_ref, r, d, sl), _rows(l_ref, r, d, sl)[:, :1], _rows(dl_ref, r, d, sl)[:, :1])
                    _stage_rows(sq_ref, r, d, sl, dq)
                    if alone:
                        _stage_rows(sk_ref, r, d, sl, jnp.zeros((BLK, HEAD), F32))
                        _stage_rows(sv_ref, r, d, sl, jnp.zeros((BLK, HEAD), F32))
                    else:
                        _stage_rows(sk_ref, r, d, sl, ck_ref[rs, sl] + dk2[:BLK])
                        _stage_rows(sv_ref, r, d, sl, cv_ref[rs, sl] + dv2[:BLK])
                    ck_ref[rs, sl] = dk2[-BLK:]
                    cv_ref[rs, sl] = dv2[-BLK:]
                    dgq, dgk = dgq + a, dgk + b
            dgq_ref[...] += dgq
            dgk_ref[...] += dgk
            dq_ref[0] = sq_ref[...].astype(dq_ref.dtype)

        pl.when(first)(lambda: run(True))
        pl.when((i > 0) & (i < nb))(lambda: run(False))

        @pl.when(i == nb)
        def _():
            for r in range(d):
                rs = slice(r * BLK, (r + 1) * BLK)
                _stage_rows(sk_ref, r, d, slice(None), ck_ref[rs, :])
                _stage_rows(sv_ref, r, d, slice(None), cv_ref[rs, :])

        dk_ref[0] = sk_ref[...].astype(dk_ref.dtype)
        dv_ref[0] = sv_ref[...].astype(dv_ref.dtype)

    def cur(c0):
        return pl.BlockSpec(blk, lambda b, j, i: (b, jnp.minimum(i, nb - 1), c0 + j))

    def prev(c0):
        return pl.BlockSpec(blk, lambda b, j, i: (b, jnp.clip(i - 1, 0, nb - 1), c0 + j))

    vec = pl.BlockSpec((1, HEAD), lambda b, j, i: (0, 0))
    at_q = pl.BlockSpec(blk, lambda b, j, i: (b, jnp.minimum(i, nb - 1), j))
    at_k = pl.BlockSpec(blk, lambda b, j, i: (b, jnp.maximum(i - 1, 0), j))
    shp = jax.ShapeDtypeStruct((Bl, S, GW), MXU_DTYPE)
    gshp = jax.ShapeDtypeStruct((1, HEAD), F32)
    return _call(body, name=f"attn_bwd_g{g}", grid=(Bl, ncb, nb + 1),
                 in_specs=[cur(cq), prev(ck), cur(ck), prev(cv), cur(cv), vec, vec, at_q, at_q, at_q, at_q],
                 out_specs=[at_q, at_k, at_k, vec, vec], out_shape=[shp, shp, shp, gshp, gshp],
                 scratch_shapes=[pltpu.VMEM(blk[1:], F32)] * 5 + _proj_stages(blk, d),
                 )(proj3, proj3, proj3, proj3, proj3, gq, gk, o3, l3, do3, dl3)


def _combine_fwd(os, ls, proj2):
    T = proj2.shape[0]
    tr = _tile(T, 512)

    def body(o1, o2, o3, l1, l2, l3, z, a_ref, at_ref):
        a = _combine(o1[...], o2[...], o3[...], l1[...], l2[...], l3[...], z[...].astype(F32))
        a_ref[...] = a.astype(a_ref.dtype)
        at_ref[...] = a.T.astype(at_ref.dtype)

    row = pl.BlockSpec((tr, GW), lambda i: (i, 0))
    return _call(body, name="combine_fwd", grid=(T // tr,),
                 in_specs=[row] * 6 + [pl.BlockSpec((tr, GW), lambda i: (i, ZA // GW))],
                 out_specs=[row, pl.BlockSpec((GW, tr), lambda i: (0, i))],
                 out_shape=[jax.ShapeDtypeStruct((T, GW), MXU_DTYPE), jax.ShapeDtypeStruct((GW, T), MXU_DTYPE)],
                 )(*os, *ls, proj2)


def _combine_bwd(os, ls, proj2, da, dproj):
    T = proj2.shape[0]
    tr = _tile(T, 256)

    def body(o1, o2, o3, l1, l2, l3, z, da_ref, _, d1, d2, d3, e1, e2, e3, dz_ref):
        _, vjp = jax.vjp(_combine, o1[...], o2[...], o3[...], l1[...], l2[...], l3[...], z[...].astype(F32))
        go1, go2, go3, gl1, gl2, gl3, gz = vjp(da_ref[...].astype(F32))
        d1[...], d2[...], d3[...] = go1, go2, go3
        dz_ref[...] = gz.astype(dz_ref.dtype)
        for ref, gl in ((e1, gl1), (e2, gl2), (e3, gl3)):
            for h in range(HPG):
                sl = slice(h * HEAD, (h + 1) * HEAD)
                ref[:, sl] = jnp.broadcast_to(jnp.sum(gl[:, sl], axis=-1, keepdims=True), (tr, HEAD))

    row = pl.BlockSpec((tr, GW), lambda i: (i, 0))
    f = jax.ShapeDtypeStruct((T, GW), F32)
    z_attn = pl.BlockSpec((tr, GW), lambda i: (i, ZA // GW))
    outs = _call(body, name="combine_bwd", grid=(T // tr,), in_specs=[row] * 6 + [z_attn, row, ANY],
                 out_specs=[row] * 6 + [z_attn], out_shape=[f] * 6 + [jax.ShapeDtypeStruct(dproj.shape, dproj.dtype)],
                 aliases={8: 6})(*os, *ls, proj2, da, dproj)
    return outs[:3], outs[3:6], outs[6]


def _shift_down(u, j, t):
    return jnp.where(t >= j, pltpu.roll(u, j, 0), 0.0)


def _shift_up(u, j, t):
    n = u.shape[0]
    return jnp.where(t < n - j, pltpu.roll(u, n - j, 0), 0.0)


def _conv_specs(Bl, S, cw):
    def sec(c0):
        return pl.BlockSpec((1, S, cw), lambda j, b: (b, 0, c0 // cw + j))
    return [sec(CB), sec(CC), sec(CV), sec(ZC)], pl.BlockSpec((3, cw), lambda j, b: (0, j))


def _conv_fwd(proj3, conv_w):
    Bl, S, _ = proj3.shape
    cw = 256
    secs, wspec = _conv_specs(Bl, S, cw)

    def body(b_ref, c_ref, v_ref, z_ref, w_ref, o_ref, ot_ref):
        t = lax.broadcasted_iota(jnp.int32, (S, cw), 0)
        u = c_ref[0].astype(F32) * v_ref[0].astype(F32)
        y = w_ref[0:1, :] * u + w_ref[1:2, :] * _shift_down(u, 1, t) + w_ref[2:3, :] * _shift_down(u, 2, t)
        out = b_ref[0].astype(F32) * y * _silu(z_ref[0].astype(F32))
        o_ref[0] = out.astype(o_ref.dtype)
        ot_ref[...] = out.T.astype(ot_ref.dtype)

    return _call(body, name="conv_fwd", grid=(CONVW // cw, Bl), in_specs=secs + [wspec],
                 out_specs=[pl.BlockSpec((1, S, cw), lambda j, b: (b, 0, j)), pl.BlockSpec((cw, S), lambda j, b: (j, b))],
                 out_shape=[jax.ShapeDtypeStruct((Bl, S, CONVW), MXU_DTYPE),
                            jax.ShapeDtypeStruct((CONVW, Bl * S), MXU_DTYPE)])(proj3, proj3, proj3, proj3, conv_w)


def _conv_bwd(proj3, conv_w, dcc3, dproj):
    Bl, S, _ = proj3.shape
    cw = 256
    secs, wspec = _conv_specs(Bl, S, cw)

    def body(b_ref, c_ref, v_ref, z_ref, w_ref, d_ref, _, dproj_ref, dw_ref, stage, sems):
        t = lax.broadcasted_iota(jnp.int32, (S, cw), 0)
        bv, cv, vv, zv = (r[0].astype(F32) for r in (b_ref, c_ref, v_ref, z_ref))
        dv = d_ref[0].astype(F32)
        u = cv * vv
        u1, u2 = _shift_down(u, 1, t), _shift_down(u, 2, t)
        y = w_ref[0:1, :] * u + w_ref[1:2, :] * u1 + w_ref[2:3, :] * u2
        sg = _sig(zv)
        sz = zv * sg
        gy = dv * bv * sz
        du = w_ref[0:1, :] * gy + w_ref[1:2, :] * _shift_up(gy, 1, t) + w_ref[2:3, :] * _shift_up(gy, 2, t)
        j, b = pl.program_id(0), pl.program_id(1)
        tiles = [dv * y * sz, du * vv, du * cv, dv * bv * y * sg * (1.0 + zv * (1.0 - sg))]
        dsts = [dproj_ref.at[pl.ds(b * S, S), pl.ds(c0 + j * cw, cw)] for c0 in (CB, CC, CV, ZC)]
        _emit_tiles(j * Bl + b, (CONVW // cw) * Bl, tiles, dsts, stage, sems)

        @pl.when(pl.program_id(1) == 0)
        def _():
            dw_ref[...] = jnp.zeros_like(dw_ref)

        dw_ref[0:1, :] += jnp.sum(gy * u, axis=0, keepdims=True)
        dw_ref[1:2, :] += jnp.sum(gy * u1, axis=0, keepdims=True)
        dw_ref[2:3, :] += jnp.sum(gy * u2, axis=0, keepdims=True)

    blk = pl.BlockSpec((1, S, cw), lambda j, b: (b, 0, j))
    return _call(body, name="conv_bwd", grid=(CONVW // cw, Bl), in_specs=secs + [wspec, blk, ANY],
                 out_specs=[ANY, wspec],
                 out_shape=[jax.ShapeDtypeStruct(dproj.shape, dproj.dtype), jax.ShapeDtypeStruct((3, CONVW), F32)],
                 scratch_shapes=_emit_scratch(4, S, cw), aliases={6: 0})(proj3, proj3, proj3, proj3, conv_w, dcc3, dproj)


def _mem_specs(S, tq):
    q = pl.BlockSpec((1, tq, MEMW), lambda b, j: (b, j, MQ // MEMW))
    z = pl.BlockSpec((1, tq, MEMW), lambda b, j: (b, j, ZM // MEMW))
    kv = pl.BlockSpec((1, MEM_HD, 2 * MEMW), lambda b, j: (b, 0, 0))
    vec = pl.BlockSpec((1, MEM_HD), lambda b, j: (0, 0))
    blk = pl.BlockSpec((1, tq, MEMW), lambda b, j: (b, j, 0))
    return q, z, kv, vec, blk


def _mem_fwd(proj3, mkv3, gq, gk):
    Bl, S, _ = proj3.shape
    tq = _tile(S, 512)
    q, z, kv, vec, blk = _mem_specs(S, tq)

    def body(q_ref, z_ref, kv_ref, gq_ref, gk_ref, o_ref, ot_ref):
        out = _mem_block(q_ref[0].astype(F32), z_ref[0].astype(F32), kv_ref[0], gq_ref[...], gk_ref[...])
        o_ref[0] = out.astype(o_ref.dtype)
        ot_ref[...] = out.T.astype(ot_ref.dtype)

    nq = S // tq
    return _call(body, name="mem_fwd", grid=(Bl, nq), in_specs=[q, z, kv, vec, vec],
                 out_specs=[blk, pl.BlockSpec((MEMW, tq), lambda b, j: (0, b * nq + j))],
                 out_shape=[jax.ShapeDtypeStruct((Bl, S, MEMW), MXU_DTYPE),
                            jax.ShapeDtypeStruct((MEMW, Bl * S), MXU_DTYPE)])(proj3, proj3, mkv3, gq, gk)


def _mem_bwd(proj3, mkv3, gq, gk, dmo3, dproj):
    Bl, S, _ = proj3.shape
    tq = _tile(S, 256)
    q, z, kv, vec, blk = _mem_specs(S, tq)
    nq = S // tq

    def body(q_ref, z_ref, kv_ref, gq_ref, gk_ref, d_ref, _, dproj_ref, dkv_ref, dgq_ref, dgk_ref, stage, sems):
        _, vjp = jax.vjp(_mem_block, q_ref[0].astype(F32), z_ref[0].astype(F32), kv_ref[0], gq_ref[...], gk_ref[...])
        dq, dz, dkv, dgq, dgk = vjp(d_ref[0].astype(F32))
        j = pl.program_id(1)
        rows = pl.ds(pl.program_id(0) * S + j * tq, tq)
        dsts = [dproj_ref.at[rows, pl.ds(MQ, MEMW)], dproj_ref.at[rows, pl.ds(ZM, MEMW)]]
        _emit_tiles(pl.program_id(0) * nq + j, Bl * nq, [dq, dz], dsts, stage, sems)

        @pl.when(j == 0)
        def _():
            dkv_ref[0] = jnp.zeros_like(dkv)

        @pl.when((j == 0) & (pl.program_id(0) == 0))
        def _():
            dgq_ref[...] = jnp.zeros_like(dgq_ref)
            dgk_ref[...] = jnp.zeros_like(dgk_ref)

        dkv_ref[0] += dkv
        dgq_ref[...] += dgq
        dgk_ref[...] += dgk

    gshp = jax.ShapeDtypeStruct((1, MEM_HD), F32)
    return _call(body, name="mem_bwd", grid=(Bl, nq), in_specs=[q, z, kv, vec, vec, blk, ANY],
                 out_specs=[ANY, kv, vec, vec],
                 out_shape=[jax.ShapeDtypeStruct(dproj.shape, dproj.dtype), jax.ShapeDtypeStruct(mkv3.shape, F32),
                            gshp, gshp],
                 scratch_shapes=_emit_scratch(2, tq, MEMW), aliases={6: 0})(proj3, proj3, mkv3, gq, gk, dmo3, dproj)


def _merge_specs(T, D, tm, tn):
    def act(w):
        return pl.BlockSpec((tm, w), lambda i, n: (i, 0))

    def wsp(w):
        return pl.BlockSpec((w, tn), lambda i, n: (0, n))

    gates = [pl.BlockSpec((tm, tn), lambda i, n, k=k: (i, (G0 + k * D) // tn + n)) for k in range(3)]
    tile = pl.BlockSpec((tm, tn), lambda i, n: (i, n))
    return act, wsp, gates, tile


def _merge_fwd(a, cc, mo, wa, wc, wm, proj2):
    T, D = a.shape[0], wa.shape[1]
    tm, tn = _tile(T, 1024), _tile(D, 512)
    act, wsp, gates, tile = _merge_specs(T, D, tm, tn)

    def body(a_ref, c_ref, m_ref, wa_ref, wc_ref, wm_ref, g0, g1, g2, mg_ref, mt_ref, pa_ref, pc_ref, pm_ref):
        pa = jnp.dot(a_ref[...], wa_ref[...], preferred_element_type=F32)
        pc = jnp.dot(c_ref[...], wc_ref[...], preferred_element_type=F32)
        pm = jnp.dot(m_ref[...], wm_ref[...], preferred_element_type=F32)
        mg = _sig(g0[...].astype(F32)) * pa + _sig(g1[...].astype(F32)) * pc + _sig(g2[...].astype(F32)) * pm
        mg_ref[...] = mg.astype(mg_ref.dtype)
        mt_ref[...] = mg.T.astype(mt_ref.dtype)
        pa_ref[...] = pa.astype(pa_ref.dtype)
        pc_ref[...] = pc.astype(pc_ref.dtype)
        pm_ref[...] = pm.astype(pm_ref.dtype)

    shp = jax.ShapeDtypeStruct((T, D), MXU_DTYPE)
    return _call(body, name="merge_fwd", grid=(T // tm, D // tn),
                 in_specs=[act(GW), act(CONVW), act(MEMW), wsp(GW), wsp(CONVW), wsp(MEMW)] + gates,
                 out_specs=[tile, pl.BlockSpec((tn, tm), lambda i, n: (n, i)), tile, tile, tile],
                 out_shape=[shp, jax.ShapeDtypeStruct((D, T), MXU_DTYPE), shp, shp, shp],
                 )(a, cc, mo, wa, wc, wm, proj2, proj2, proj2)


def _emit_tiles(step, nsteps, tiles, dsts, stage, sems):
    slot = step % 2

    def copies(s):
        return [pltpu.make_async_copy(stage.at[s, k], dsts[k], sems.at[s, k]) for k in range(len(tiles))]

    @pl.when(step >= 2)
    def _():
        for cp in copies(slot):
            cp.wait()

    for k, t in enumerate(tiles):
        stage[slot, k] = t.astype(stage.dtype)
    for cp in copies(slot):
        cp.start()

    @pl.when(step == nsteps - 1)
    def _():
        for cp in copies(slot):
            cp.wait()
        if nsteps > 1:
            for cp in copies(1 - slot):
                cp.wait()


def _emit_scratch(k, rows, cols):
    return [pltpu.VMEM((2, k, rows, cols), MXU_DTYPE), pltpu.SemaphoreType.DMA((2, k))]


def _merge_bwd(dyb, w_out, proj2, pa, pc, pm):
    T, D = dyb.shape
    IN = proj2.shape[1]
    tm, tn = _tile(T, 1024), _tile(D, 512)
    _, _, gates, tile = _merge_specs(T, D, tm, tn)
    nn = D // tn

    def body(dy_ref, w_ref, g0, g1, g2, p0, p1, p2, dp0, dp1, dp2, dproj_ref, stage, sems):
        i, n = pl.program_id(0), pl.program_id(1)
        dm = lax.dot_general(dy_ref[...], w_ref[...], _DIMS["nt"], preferred_element_type=F32)
        tiles, dsts = [], []
        for k, (g_ref, p_ref, dp_ref) in enumerate(((g0, p0, dp0), (g1, p1, dp1), (g2, p2, dp2))):
            gt = _sig(g_ref[...].astype(F32))
            dp_ref[...] = (gt * dm).astype(dp_ref.dtype)
            tiles.append(dm * p_ref[...].astype(F32) * gt * (1.0 - gt))
            dsts.append(dproj_ref.at[pl.ds(i * tm, tm), pl.ds(G0 + k * D + n * tn, tn)])
        _emit_tiles(i * nn + n, (T // tm) * nn, tiles, dsts, stage, sems)

    shp = jax.ShapeDtypeStruct((T, D), MXU_DTYPE)
    return _call(body, name="merge_bwd", grid=(T // tm, nn),
                 in_specs=[pl.BlockSpec((tm, D), lambda i, n: (i, 0)), pl.BlockSpec((tn, D), lambda i, n: (n, 0))]
                 + gates + [tile] * 3,
                 out_specs=[tile] * 3 + [ANY], out_shape=[shp] * 3 + [jax.ShapeDtypeStruct((T, IN), MXU_DTYPE)],
                 scratch_shapes=_emit_scratch(3, tm, tn))(dyb, w_out, proj2, proj2, proj2, pa, pc, pm)


def _out_loss(merged, w_out, x, tgt):
    T, D = x.shape
    tm = _tile(T, 512)

    def body(m_ref, w_ref, x_ref, t_ref, dy_ref, dyb_ref, loss_ref):
        err = x_ref[...] + jnp.dot(m_ref[...], w_ref[...], preferred_element_type=F32) - t_ref[...]
        dy = err * (1.0 / D)
        dy_ref[...] = dy
        dyb_ref[...] = dy.astype(dyb_ref.dtype)

        @pl.when(pl.program_id(0) == 0)
        def _():
            loss_ref[...] = jnp.zeros_like(loss_ref)

        loss_ref[...] += jnp.sum(err * err) * (0.5 / D)

    row = pl.BlockSpec((tm, D), lambda i: (i, 0))
    return _call(body, name="out_loss", grid=(T // tm,),
                 in_specs=[row, pl.BlockSpec((D, D), lambda i: (0, 0)), row, row],
                 out_specs=[row, row, pl.BlockSpec((1, 128), lambda i: (0, 0))],
                 out_shape=[jax.ShapeDtypeStruct((T, D), F32), jax.ShapeDtypeStruct((T, D), MXU_DTYPE),
                            jax.ShapeDtypeStruct((1, 128), F32)])(merged, w_out, x, tgt)


def _proj_chunk(hb, w, meta, j, nslots, half, buf, name):
    T, D = hb.shape
    Cs = w.shape[1] // 4
    tm, tn = _tile(T, 1024), _tile(Cs // 2, 2176)
    nh = Cs // 2 // tn
    per = nh if half is not None else 2 * nh

    def body(meta_ref, a_ref, b_ref, *rest):
        rest[-1][...] = jnp.dot(a_ref[...], b_ref[...], preferred_element_type=F32).astype(rest[-1].dtype)

    def tile(n, m):
        if half is None:
            return n % per
        return (m[4] if half == 0 else 1 - m[4]) * nh + n % per

    in_specs = [pl.BlockSpec((tm, D), lambda n, i, m: (i, 0)),
                pl.BlockSpec((D, tn), lambda n, i, m: (0, (j + n // per) * 2 * nh + tile(n, m)))]
    args = [meta, hb, w]
    if buf is not None:
        in_specs.append(ANY)
        args.append(buf)
    spec = pltpu.PrefetchScalarGridSpec(
        num_scalar_prefetch=1, grid=(nslots * per, T // tm), in_specs=in_specs,
        out_specs=pl.BlockSpec((tm, tn), lambda n, i, m: (i, m[j + n // per] * 2 * nh + tile(n, m))))
    return _call(body, name=name, grid_spec=spec, out_shape=jax.ShapeDtypeStruct((T, 4 * Cs), PROJ_DTYPE),
                 aliases={} if buf is None else {3: 0})(*args)


def _norms(x, mem, norm_g, mem_norm_g):
    D = x.shape[-1]
    hb, hbt = _rms_fwd(x.reshape(-1, D), norm_g.reshape(1, D), "rms_x")
    mhb, _ = _rms_fwd(mem.reshape(-1, D), mem_norm_g.reshape(1, D), "rms_mem")
    return hb, hbt, mhb


def _attention_fwd(proj2, Bl, gq_all, gk_all):
    T, IN = proj2.shape
    proj3 = proj2.reshape(Bl, T // Bl, IN)
    os, ls = [], []
    for g, d in enumerate(DILATIONS):
        o, l = _attn_fwd(proj3, gq_all[g:g + 1], gk_all[g:g + 1], g, d)
        os.append(o.reshape(T, GW))
        ls.append(l.reshape(T, GW))
    return os, ls, _combine_fwd(os, ls, proj2)


def _conv_branch_fwd(proj2, Bl, conv_w):
    T, IN = proj2.shape
    cc, cct = _conv_fwd(proj2.reshape(Bl, T // Bl, IN), conv_w)
    return cc.reshape(T, CONVW), cct


def _weight_grads(x, mem, tgt, norm_g, mem_norm_g, gq_all, gk_all, conv_w, mem_gq, mem_gk, W, pre, early=None):
    Bl, S, D = x.shape
    T = Bl * S
    hb, hbt, mhb, proj2, os, ls, (a, at), (cc, cct) = pre
    IN = proj2.shape[1]
    proj3 = proj2.reshape(Bl, S, IN)
    x2, tgt2 = x.reshape(T, D), tgt.reshape(T, D)
    mem2 = mem.reshape(-1, D)
    ng, mng = norm_g.reshape(1, D), mem_norm_g.reshape(1, D)
    mgq, mgk = mem_gq.reshape(1, MEM_HD), mem_gk.reshape(1, MEM_HD)
    gqs = [gq_all[g:g + 1] for g in range(NGROUP)]
    gks = [gk_all[g:g + 1] for g in range(NGROUP)]

    mkv = _matmul(mhb, W["mem_w_kv"], "nn", F32, name="mem_kv", tm=512, tn=1024, tk=D)
    mkv3 = mkv.reshape(Bl, -1, 2 * MEMW)
    mo, mot = _mem_fwd(proj3, mkv3, mgq, mgk)
    mo = mo.reshape(T, MEMW)
    merged, mergedt, pa, pc, pm = _merge_fwd(a, cc, mo, W["w_br_attn"], W["w_br_conv"], W["w_br_mem"], proj2)
    dy, dyb, loss = _out_loss(merged, W["w_out"], x2, tgt2)

    G = {}
    G["w_out"] = _matmul(mergedt, dyb, "nn", WIRE_DTYPE, name="dw_out", tm=1024, tn=512, tk=T)
    dpa, dpc, dpm, dproj = _merge_bwd(dyb, W["w_out"], proj2, pa, pc, pm)
    G["w_br_attn"] = _matmul(at, dpa, "nn", WIRE_DTYPE, name="dw_br_attn", tm=512, tn=512, tk=T)
    G["w_br_conv"] = _matmul(cct, dpc, "nn", WIRE_DTYPE, name="dw_br_conv", tm=1024, tn=512, tk=T)
    G["w_br_mem"] = _matmul(mot, dpm, "nn", WIRE_DTYPE, name="dw_br_mem", tm=1024, tn=512, tk=T)
    da = _matmul(dpa, W["w_br_attn"], "nt", PROJ_DTYPE, name="d_attn", tm=1024, tn=512, tk=D)
    dcc = _matmul(dpc, W["w_br_conv"], "nt", PROJ_DTYPE, name="d_conv", tm=1024, tn=1024, tk=D)
    dmo = _matmul(dpm, W["w_br_mem"], "nt", PROJ_DTYPE, name="d_mem", tm=1024, tn=1024, tk=D)
    dproj, dmkv3, dmgq, dmgk = _mem_bwd(proj3, mkv3, mgq, mgk, dmo.reshape(Bl, S, MEMW), dproj)
    dmkv = _cast(dmkv3.reshape(-1, 2 * MEMW), "cast_dmkv")
    G["mem_w_kv"] = _matmul(mhb, dmkv, "tn", WIRE_DTYPE, name="dw_mem_kv", tm=1024, tn=1024, tk=512)
    early_state, dmkv = (None, dmkv) if early is None else early[0](G, dmkv)
    dmh = _matmul(dmkv, W["mem_w_kv"], "nt", F32, name="d_memh", tm=512, tn=1024, tk=2 * MEMW)
    _, dmng = _rms_bwd(mem2, dmh, mng, None, "rms_mem_bwd")
    if early is not None:
        early_state, da = early[1](early_state, dmng, da)

    dos, dls, dproj = _combine_bwd(os, ls, proj2, da, dproj)
    dgq, dgk = [], []
    for g, d in enumerate(DILATIONS):
        dq, dk, dv, gq_g, gk_g = _attn_bwd(proj3, gqs[g], gks[g], os[g].reshape(Bl, S, GW), ls[g].reshape(Bl, S, GW),
                                           dos[g].reshape(Bl, S, GW), dls[g].reshape(Bl, S, GW), g, d)
        for c0, part in ((Q0, dq), (K0, dk), (V0, dv)):
            dproj = lax.dynamic_update_slice(dproj, part.reshape(T, GW), (0, c0 + g * GW))
        dgq.append(gq_g)
        dgk.append(gk_g)
    dproj, dconv_w = _conv_bwd(proj3, conv_w, dcc.reshape(Bl, S, CONVW), dproj)
    small = [loss, None, dmng] + dgq + dgk + [dconv_w.reshape(1, 3 * CONVW), dmgq, dmgk]
    return G, (dproj, x2, ng, dy, small), early_state


DW_IN_TILE = 1024


def _dw_in_half(hbt, dproj, pos, own, name, add=None, tiles=None, into=None):
    D, T = hbt.shape
    IN = dproj.shape[1]
    R, tn = D // 2, _tile(IN, DW_IN_TILE)
    j0, nj = (0, IN // tn) if tiles is None else tiles

    def body(pos_ref, a_ref, b_ref, *rest):
        acc = jnp.dot(a_ref[...], b_ref[...], preferred_element_type=F32)
        if add is not None:
            acc = acc + rest[0][...].astype(F32)
        rest[-1][...] = acc.astype(rest[-1].dtype)

    tile = pl.BlockSpec((R, tn), lambda j, p: (0, j0 + j))
    in_specs = [pl.BlockSpec((R, T), lambda j, p: (p[1] if own else 1 - p[1], 0)),
                pl.BlockSpec((T, tn), lambda j, p: (0, j0 + j))]
    args = [pos, hbt, dproj]
    if add is not None:
        in_specs.append(tile)
        args.append(add)
    if into is not None:
        in_specs.append(ANY)
        args.append(into)
    spec = pltpu.PrefetchScalarGridSpec(num_scalar_prefetch=1, grid=(nj,), in_specs=in_specs, out_specs=tile)
    return _call(body, name=name, grid_spec=spec, out_shape=jax.ShapeDtypeStruct((R, IN), WIRE_DTYPE),
                 aliases={} if into is None else {len(args) - 1: 0})(*args)


def _d_h(dproj, w, order):
    T, IN = dproj.shape
    D, Cs = w.shape[0], IN // 4
    tm, tn = _tile(T, 1024), _tile(D, 1024)

    def body(order_ref, a_ref, b_ref, o_ref, acc_ref):
        part = lax.dot_general(a_ref[...], b_ref[...], _DIMS["nt"], preferred_element_type=F32)
        k = pl.program_id(2)

        @pl.when(k == 0)
        def _():
            acc_ref[...] = part

        @pl.when(k > 0)
        def _():
            acc_ref[...] += part

        @pl.when(k == 3)
        def _():
            o_ref[...] = acc_ref[...]

    spec = pltpu.PrefetchScalarGridSpec(
        num_scalar_prefetch=1, grid=(T // tm, D // tn, 4),
        in_specs=[pl.BlockSpec((tm, Cs), lambda i, n, k, o: (i, o[k])), pl.BlockSpec((tn, Cs), lambda i, n, k, o: (n, k))],
        out_specs=pl.BlockSpec((tm, tn), lambda i, n, k, o: (i, n)), scratch_shapes=[pltpu.VMEM((tm, tn), F32)])
    return _call(body, name="d_h", grid_spec=spec, out_shape=jax.ShapeDtypeStruct((T, D), F32))(order, dproj, w)


def _input_grad(rest, w_in, order):
    dproj, x2, ng, dy, small = rest
    dh = _d_h(dproj, w_in, order)
    grad_x, dng = _rms_bwd(x2, dh, ng, dy, "rms_x_bwd")
    small = [dng if t is None else t for t in small]
    return grad_x, jnp.concatenate(small, axis=1)


def _local_step(x, mem, tgt, norm_g, mem_norm_g, gq_all, gk_all, conv_w, mem_gq, mem_gk, W):
    hb, hbt, mhb = _norms(x, mem, norm_g, mem_norm_g)
    Cs = W["w_in"].shape[1] // 4
    shards = (0, 2, 1, 3)
    order = jnp.array(shards, dtype=jnp.int32)
    w_rel = jnp.concatenate([W["w_in"][:, s * Cs:(s + 1) * Cs] for s in shards], axis=1)
    meta = jnp.array(shards + (0,), dtype=jnp.int32)
    proj2 = _proj_chunk(hb, w_rel, meta, 0, 1, None, None, "proj_0")
    for j, nslots in ((1, 2), (3, 1)):
        for half in (1, 0):
            proj2 = _proj_chunk(hb, w_rel, meta, j, nslots, half, proj2, f"proj_{j}_{half}")
    pre = (hb, hbt, mhb, proj2, *_attention_fwd(proj2, x.shape[0], gq_all, gk_all),
           _conv_branch_fwd(proj2, x.shape[0], conv_w))
    G, rest, _ = _weight_grads(x, mem, tgt, norm_g, mem_norm_g, gq_all, gk_all, conv_w, mem_gq, mem_gk, W, pre)
    pos = jnp.zeros((2,), jnp.int32)
    ntiles = rest[0].shape[1] // _tile(rest[0].shape[1], DW_IN_TILE)
    other = _dw_in_half(hbt, rest[0], pos, False, "dw_in_sibling_early", tiles=(0, ntiles - ntiles // 2))
    other = _dw_in_half(hbt, rest[0], pos, False, "dw_in_sibling_late", tiles=(ntiles - ntiles // 2, ntiles // 2),
                        into=other)
    G["w_in"] = jnp.concatenate([_dw_in_half(hbt, rest[0], pos, True, "dw_in_own"), other], axis=0)
    grad_x, small = _input_grad(rest, w_rel, order)
    return grad_x.reshape(x.shape), G, small


BIG = (("w_in", "col"), ("mem_w_kv", "row"), ("w_br_attn", "col"), ("w_br_conv", "col"),
       ("w_br_mem", "col"), ("w_out", "row"))


def _coords():
    return lax.axis_index("x"), lax.axis_index("y"), lax.axis_index("c")


def _other_chips(x, y):
    return [(1 - x, y), (x, 1 - y), (1 - x, 1 - y)]


def _half(ref, kind, c):
    R, C = ref.shape
    if kind == "col":
        return ref.at[pl.ds(c * (R // 2), R // 2), :]
    return ref.at[:, pl.ds(c * (C // 2), C // 2)]


def _shard(ref, kind, s):
    R, C = ref.shape
    if kind == "col":
        return ref.at[:, pl.ds(s * (C // 4), C // 4)]
    return ref.at[pl.ds(s * (R // 4), R // 4), :]


def _piece(ref, kind, s, c):
    R, C = ref.shape
    if kind == "col":
        return ref.at[pl.ds(c * (R // 2), R // 2), pl.ds(s * (C // 4), C // 4)]
    return ref.at[pl.ds(s * (R // 4), R // 4), pl.ds(c * (C // 2), C // 2)]


def _remote(src, dst, sems_s, sems_r, k, dev):
    return pltpu.make_async_remote_copy(src_ref=src, dst_ref=dst, send_sem=sems_s.at[k], recv_sem=sems_r.at[k],
                                        device_id=dev, device_id_type=MESH)


HBM = pl.BlockSpec(memory_space=pltpu.HBM)
SEM = pl.BlockSpec(memory_space=pltpu.SEMAPHORE)
EFFECT = pltpu.SideEffectType.DATAFLOW_SIDE_EFFECTING


def _hbm(a):
    return pltpu.with_memory_space_constraint(a, pltpu.HBM)


def _start_copies(name, arrays, ncopies, make):
    n = len(arrays)

    def body(*refs):
        for cp in make(refs[:n], refs[n], refs[n + 1]):
            cp.start()

    outs = pl.pallas_call(
        body, name=name,
        out_shape=(pltpu.SemaphoreType.DMA((ncopies,)), pltpu.SemaphoreType.DMA((ncopies,)),
                   *[jax.ShapeDtypeStruct(t.shape, t.dtype) for t in arrays]),
        in_specs=[HBM] * n, out_specs=(SEM, SEM, *([HBM] * n)),
        input_output_aliases={i: i + 2 for i in range(n)},
        compiler_params=pltpu.CompilerParams(has_side_effects=EFFECT),
    )(*[_hbm(t) for t in arrays])
    return outs[0], outs[1], list(outs[2:])


def _wait_copies(name, send, recv, arrays, make, after):
    n = len(arrays)

    def body(*refs):
        for cp in make(refs[:n], refs[n], refs[n + 1]):
            cp.wait_send()
            cp.wait_recv()

    outs = pl.pallas_call(
        body, name=name, out_shape=[jax.ShapeDtypeStruct(t.shape, t.dtype) for t in arrays],
        in_specs=[HBM] * n + [SEM, SEM, ANY], out_specs=[HBM] * n,
        input_output_aliases={i: i for i in range(n)},
        compiler_params=pltpu.CompilerParams(has_side_effects=EFFECT),
    )(*arrays, send, recv, after)
    return list(outs)


def _w_in_copies(relations):
    def make(refs, send, recv):
        x, y, c = _coords()
        me = 2 * x + y
        chips = _other_chips(x, y)
        w, conv = refs[0], refs[1]
        cps = []
        for i, k in enumerate(relations):
            cps.append(_remote(_column_half(w, 0, c), _column_half(w, 1 + k, c), send, recv, 2 * i, (*chips[k], c)))
            mine = _shard(conv, "col", me)
            cps.append(_remote(mine, mine, send, recv, 2 * i + 1, (*chips[k], c)))
        return cps
    return make


def _column_half(w, slot, c):
    half = w.shape[1] // 8
    return w.at[:, pl.ds((2 * slot + c) * half, half)]


def _w_in_forward(relations):
    def make(refs, send, recv):
        x, y, c = _coords()
        cps = []
        for i, k in enumerate(relations):
            got = _column_half(refs[0], 1 + k, c)
            cps.append(_remote(got, got, send, recv, i, (x, y, 1 - c)))
        return cps
    return make


def _sibling_columns(c0, width):
    def make(refs, send, recv):
        x, y, c = _coords()
        cols = pl.ds(c0, width)
        return [_remote(refs[0].at[:, cols], refs[1].at[:, cols], send, recv, 0, (x, y, 1 - c))]
    return make


def _other_weight_copies(refs, send, recv):
    x, y, c = _coords()
    me = 2 * x + y
    cps = []
    for k, chip in enumerate(_other_chips(x, y)):
        for p, (_, kind) in enumerate(BIG[1:]):
            mine = _piece(refs[p], kind, me, c)
            cps.append(_remote(mine, mine, send, recv, 3 * p + k, (*chip, c)))
    return cps


def _other_weight_forward(refs, send, recv):
    x, y, c = _coords()
    cps = []
    for k, chip in enumerate(_other_chips(x, y)):
        s = 2 * chip[0] + chip[1]
        for p, (_, kind) in enumerate(BIG[1:]):
            got = _piece(refs[p], kind, s, c)
            cps.append(_remote(got, got, send, recv, 3 * p + k, (x, y, 1 - c)))
    return cps


def _share_copies(group):
    def make(refs, send, recv):
        x, y, c = _coords()
        cps = []
        for p, (_, kind) in enumerate(group):
            mine = _half(refs[p], kind, c)
            cps.append(_remote(mine, mine, send, recv, p, (x, y, 1 - c)))
        return cps
    return make


def _sibling_halves_start(G, group, carry, tag):
    n = len(group)
    parts = [G[name] for name, _ in group]
    lands = []
    for (_, kind), g in zip(group, parts):
        R, C = g.shape
        lands.append(lax.empty((R // 2, C) if kind == "col" else (R, C // 2), g.dtype))

    def make(refs, send, recv):
        x, y, c = _coords()
        return [_remote(_half(refs[p], group[p][1], 1 - c), refs[n + p], send, recv, p, (x, y, 1 - c)) for p in range(n)]

    send, recv, thru = _start_copies("sibling_halves_start_" + tag, [*parts, *lands, carry], n, make)
    return (send, recv, thru[:2 * n], make, tag), thru[2 * n]


def _presums(state, group, pos, after):
    send, recv, arrays, make, tag = state
    n = len(group)
    thru = _wait_copies("sibling_halves_wait_" + tag, send, recv, arrays, make, after)
    return [_presum(thru[p], thru[n + p], kind, pos, "presum_" + name) for p, (name, kind) in enumerate(group)]


def _presum(g, got, kind, pos, name):
    R, C = got.shape
    tr, tc = _tile(R, 512, 16), _tile(C, 2048)
    nr, nc = R // tr, C // tc

    def body(pos_ref, a_ref, b_ref, o_ref):
        o_ref[...] = (a_ref[...].astype(F32) + b_ref[...].astype(F32)).astype(o_ref.dtype)

    blk = pl.BlockSpec((tr, tc), lambda i, j, pos_ref: (i, j))
    if kind == "col":
        mine = pl.BlockSpec((tr, tc), lambda i, j, pos_ref: (pos_ref[1] * nr + i, j))
    else:
        mine = pl.BlockSpec((tr, tc), lambda i, j, pos_ref: (i, pos_ref[1] * nc + j))
    spec = pltpu.PrefetchScalarGridSpec(num_scalar_prefetch=1, grid=(nr, nc), in_specs=[mine, blk], out_specs=blk)
    return _call(body, name=name, grid_spec=spec, out_shape=jax.ShapeDtypeStruct((R, C), WIRE_DTYPE))(pos, g, got)


def _chip_copies(group):
    n = len(group)

    def make(refs, send, recv):
        x, y, c = _coords()
        cps = []
        for k, chip in enumerate(_other_chips(x, y)):
            s = 2 * chip[0] + chip[1]
            for p in range(n):
                cps.append(_remote(_shard(refs[p], group[p][1], s), refs[n + p].at[k], send, recv, 3 * p + k, (*chip, c)))
        return cps
    return make


def _landing_zones(pres, group):
    lands = []
    for (_, kind), g in zip(group, pres):
        R, C = g.shape
        lands.append(lax.empty((3, R, C // 4) if kind == "col" else (3, R // 4, C), g.dtype))
    return lands


def _exchange_start(pres, group, carry, tag):
    n = len(group)
    make = _chip_copies(group)
    send, recv, thru = _start_copies("chip_exchange_start_" + tag, [*pres, *_landing_zones(pres, group), carry], 3 * n, make)
    return (send, recv, thru[:2 * n], make, tag), thru[2 * n]


def _exchange_wait(state, after):
    send, recv, arrays, make, tag = state
    thru = _wait_copies("chip_exchange_wait_" + tag, send, recv, arrays, make, after)
    n = len(thru) // 2
    return thru[:n], thru[n:]


def _reduce_into_shard(slots, pre, kind, pos, name):
    K, R, C = slots.shape
    tr, tc = _tile(R, 512, 16), _tile(C, 2176)
    nr, nc = R // tr, C // tc

    def body(pos_ref, s_ref, p_ref, o_ref):
        acc = p_ref[...].astype(F32)
        for k in range(K):
            acc = acc + s_ref[k].astype(F32)
        o_ref[...] = acc

    if kind == "col":
        own = pl.BlockSpec((tr, tc), lambda i, j, pos_ref: (i, pos_ref[0] * nc + j))
        full, out = (2 * R, C), pl.BlockSpec((tr, tc), lambda i, j, pos_ref: (pos_ref[1] * nr + i, j))
    else:
        own = pl.BlockSpec((tr, tc), lambda i, j, pos_ref: (pos_ref[0] * nr + i, j))
        full, out = (R, 2 * C), pl.BlockSpec((tr, tc), lambda i, j, pos_ref: (i, pos_ref[1] * nc + j))
    spec = pltpu.PrefetchScalarGridSpec(
        num_scalar_prefetch=1, grid=(nr, nc),
        in_specs=[pl.BlockSpec((K, tr, tc), lambda i, j, pos_ref: (0, i, j)), own], out_specs=out)
    return _call(body, name=name, grid_spec=spec, out_shape=jax.ShapeDtypeStruct(full, F32))(pos, slots, pre)


def _small_slots(pack, me):
    _, N = pack.shape

    def body(me_ref, p_ref, o_ref):
        o_ref[0] = p_ref[...]

    spec = pltpu.PrefetchScalarGridSpec(
        num_scalar_prefetch=1, grid=(1,), in_specs=[pl.BlockSpec((1, N), lambda i, me_ref: (0, 0))],
        out_specs=pl.BlockSpec((1, 1, N), lambda i, me_ref: (me_ref[0], 0, 0)))
    return _call(body, name="small_slots", grid_spec=spec, out_shape=jax.ShapeDtypeStruct((8, 1, N), pack.dtype))(me, pack)


def _small_copies(refs, send, recv):
    x, y, c = _coords()
    me = 4 * x + 2 * y + c
    cps = []
    for k in range(1, 8):
        dev = (x ^ (k >> 2), y ^ ((k >> 1) & 1), c ^ (k & 1))
        cps.append(_remote(refs[0], refs[1].at[me], send, recv, k - 1, dev))
    return cps


def _sum_small(slots):
    K, _, N = slots.shape

    def body(s_ref, o_ref):
        acc = s_ref[0]
        for k in range(1, K):
            acc = acc + s_ref[k]
        o_ref[...] = acc

    return _call(body, name="sum_small", in_specs=[pl.BlockSpec(memory_space=pltpu.VMEM)],
                 out_specs=pl.BlockSpec(memory_space=pltpu.VMEM), out_shape=jax.ShapeDtypeStruct((1, N), F32))(slots)


def _adamw(w, g, m, v, name, with_grad=False):
    R, C = w.shape
    tr, tc = _tile(R, 256, 8), _tile(C, 2176)

    def body(w_ref, g_ref, m_ref, v_ref, d_ref, nm_ref, nv_ref, *g_out):
        gv = g_ref[...]
        for ref in g_out:
            ref[...] = gv
        nm = ADAM_B1 * m_ref[...] + (1.0 - ADAM_B1) * gv
        nv = ADAM_B2 * v_ref[...] + (1.0 - ADAM_B2) * gv * gv
        m_hat = nm / (1.0 - ADAM_B1 ** ADAM_STEP)
        v_hat = nv / (1.0 - ADAM_B2 ** ADAM_STEP)
        d_ref[...] = -ADAM_LR * (m_hat / (jnp.sqrt(v_hat) + ADAM_EPS) + ADAM_WD * w_ref[...])
        nm_ref[...] = nm
        nv_ref[...] = nv

    spec = pl.BlockSpec((tr, tc), lambda i, j: (i, j))
    shp = jax.ShapeDtypeStruct((R, C), F32)
    nout = 4 if with_grad else 3
    return _call(body, name=name, grid=(R // tr, C // tc), in_specs=[spec] * 4, out_specs=[spec] * nout,
                 out_shape=[shp] * nout)(w, g, m, v)


SMALL = ("norm_g", "mem_norm_g", "attn_q_norm", "attn_k_norm", "conv_w", "mem_q_norm", "mem_k_norm")
WEIGHTS = ("norm_g", "mem_norm_g", "w_in", "attn_q_norm", "attn_k_norm", "conv_w", "mem_w_kv", "mem_q_norm",
           "mem_k_norm", "w_br_attn", "w_br_conv", "w_br_mem", "w_out")


def kernel(x, mem, norm_g, mem_norm_g, w_in, attn_q_norm, attn_k_norm, conv_w, mem_w_kv, mem_q_norm, mem_k_norm, w_br_attn, w_br_conv, w_br_mem, w_out, loss_target, m_norm_g, m_mem_norm_g, m_w_in, m_attn_q_norm, m_attn_k_norm, m_conv_w, m_mem_w_kv, m_mem_q_norm, m_mem_k_norm, m_w_br_attn, m_w_br_conv, m_w_br_mem, m_w_out, v_norm_g, v_mem_norm_g, v_w_in, v_attn_q_norm, v_attn_k_norm, v_conv_w, v_mem_w_kv, v_mem_q_norm, v_mem_k_norm, v_w_br_attn, v_w_br_conv, v_w_br_mem, v_w_out):
    w = dict(norm_g=norm_g, mem_norm_g=mem_norm_g, w_in=w_in, attn_q_norm=attn_q_norm, attn_k_norm=attn_k_norm,
             conv_w=conv_w, mem_w_kv=mem_w_kv, mem_q_norm=mem_q_norm, mem_k_norm=mem_k_norm, w_br_attn=w_br_attn,
             w_br_conv=w_br_conv, w_br_mem=w_br_mem, w_out=w_out)
    m = dict(norm_g=m_norm_g, mem_norm_g=m_mem_norm_g, w_in=m_w_in, attn_q_norm=m_attn_q_norm,
             attn_k_norm=m_attn_k_norm, conv_w=m_conv_w, mem_w_kv=m_mem_w_kv, mem_q_norm=m_mem_q_norm,
             mem_k_norm=m_mem_k_norm, w_br_attn=m_w_br_attn, w_br_conv=m_w_br_conv, w_br_mem=m_w_br_mem, w_out=m_w_out)
    v = dict(norm_g=v_norm_g, mem_norm_g=v_mem_norm_g, w_in=v_w_in, attn_q_norm=v_attn_q_norm,
             attn_k_norm=v_attn_k_norm, conv_w=v_conv_w, mem_w_kv=v_mem_w_kv, mem_q_norm=v_mem_q_norm,
             mem_k_norm=v_mem_k_norm, w_br_attn=v_w_br_attn, w_br_conv=v_w_br_conv, w_br_mem=v_w_br_mem, w_out=v_w_out)
    Bl, _, D = x.shape
    cx, cy = lax.axis_index("x"), lax.axis_index("y")
    chip = 2 * cx + cy
    pos = jnp.stack([chip, lax.axis_index("c")]).astype(jnp.int32)
    order = jnp.stack([chip] + [2 * a + b for a, b in _other_chips(cx, cy)]).astype(jnp.int32)
    n = len(BIG)

    slot0 = jnp.stack([jnp.zeros((), jnp.int32), pos[1]])
    w_rel = _place_shard(w["w_in"], "col", slot0, WIRE_DTYPE, "place_w_in_sent", half=0)
    conv_full = _place_shard(conv_w, "col", pos, F32, "place_conv_w")
    others = [_place_shard(w[name], kind, pos, WIRE_DTYPE, "place_" + name) for name, kind in BIG[1:]]
    hb, hbt, mhb = _norms(x, mem, norm_g, mem_norm_g)

    meta = jnp.concatenate([order, pos[1:]])
    near, near_fwd = _w_in_copies((0, 1)), _w_in_forward((0, 1))
    send, recv, (w_rel, conv_full) = _start_copies("gather_near_start", [w_rel, conv_full], 4, near)
    w_rel = _place_shard(w["w_in"], "col", slot0, WIRE_DTYPE, "place_w_in_kept", half=1, into=w_rel)
    proj = _proj_chunk(hb, w_rel, meta, 0, 1, None, None, "proj_own")
    w_rel, conv_full, *others = _wait_copies("gather_near_wait", send, recv, [w_rel, conv_full, *others], near, proj)

    fsend, frecv, (w_rel,) = _start_copies("gather_near_forward_start", [w_rel], 2, near_fwd)
    far, far_fwd = _w_in_copies((2,)), _w_in_forward((2,))
    send, recv, (w_rel, conv_full) = _start_copies("gather_far_start", [w_rel, conv_full], 2, far)
    proj = _proj_chunk(hb, w_rel, meta, 1, 2, 0, proj, "proj_near_landed")
    w_rel, = _wait_copies("gather_near_forward_wait", fsend, frecv, [w_rel], near_fwd, proj)
    proj = _proj_chunk(hb, w_rel, meta, 1, 2, 1, proj, "proj_near_forwarded")
    w_rel, conv_full = _wait_copies("gather_far_wait", send, recv, [w_rel, conv_full], far, proj)

    fsend, frecv, (w_rel,) = _start_copies("gather_far_forward_start", [w_rel], 1, far_fwd)
    send, recv, (*others, w_rel) = _start_copies("gather_rest_start", [*others, w_rel], 3 * (n - 1), _other_weight_copies)
    proj = _proj_chunk(hb, w_rel, meta, 3, 1, 0, proj, "proj_far_landed")
    w_rel, = _wait_copies("gather_far_forward_wait", fsend, frecv, [w_rel], far_fwd, proj)
    proj = _proj_chunk(hb, w_rel, meta, 3, 1, 1, proj, "proj_far_forwarded")
    os, ls, a = _attention_fwd(proj, Bl, attn_q_norm, attn_k_norm)
    *others, w_rel = _wait_copies("gather_rest_wait", send, recv, [*others, w_rel], _other_weight_copies, a[0])
    fsend, frecv, (*others, proj) = _start_copies("gather_rest_forward_start", [*others, proj], 3 * (n - 1),
                                                  _other_weight_forward)
    cc = _conv_branch_fwd(proj, Bl, conv_full)
    others = _wait_copies("gather_rest_forward_wait", fsend, frecv, others, _other_weight_forward, cc[0])
    W = {name: others[p] for p, (name, _) in enumerate(BIG[1:])}

    def rest_halves(G, carry):
        return _sibling_halves_start(G, BIG[1:], carry, "rest")

    def rest_exchange(state, after, carry):
        return _exchange_start(_presums(state, BIG[1:], pos, after), BIG[1:], carry, "rest")

    G, rest, rest_state = _weight_grads(
        x, mem, loss_target, norm_g, mem_norm_g, attn_q_norm, attn_k_norm, conv_full, mem_q_norm, mem_k_norm, W,
        (hb, hbt, mhb, proj, os, ls, a, cc), early=(rest_halves, rest_exchange))

    tn = _tile(rest[0].shape[1], DW_IN_TILE)
    ntiles = rest[0].shape[1] // tn
    first = ntiles - ntiles // 2
    early_cols = _sibling_columns(0, first * tn)
    late_cols = _sibling_columns(first * tn, (ntiles - first) * tn)
    for_sibling = _dw_in_half(hbt, rest[0], pos, False, "dw_in_sibling_early", tiles=(0, first))
    send, recv, (for_sibling, got, dproj) = _start_copies(
        "sibling_w_in_start_early", [for_sibling, lax.empty(for_sibling.shape, for_sibling.dtype), rest[0]], 1, early_cols)
    for_sibling = _dw_in_half(hbt, dproj, pos, False, "dw_in_sibling_late", tiles=(first, ntiles - first),
                              into=for_sibling)
    lsend, lrecv, (for_sibling, got) = _start_copies("sibling_w_in_start_late", [for_sibling, got], 1, late_cols)
    pres_rest, slots_rest = _exchange_wait(rest_state, for_sibling)
    reds_rest = [_reduce_into_shard(slots_rest[p], pres_rest[p], kind, pos, "reduce_" + name)
                 for p, (name, kind) in enumerate(BIG[1:])]
    share_rest = _share_copies(BIG[1:])
    rsend, rrecv, reds_rest = _start_copies("share_rest_start", reds_rest, n - 1, share_rest)
    for_sibling, got = _wait_copies("sibling_w_in_wait_early", send, recv, [for_sibling, got], early_cols, reds_rest[0])
    for_sibling, got = _wait_copies("sibling_w_in_wait_late", lsend, lrecv, [for_sibling, got], late_cols, reds_rest[0])
    pre_w_in = _dw_in_half(hbt, dproj, pos, True, "dw_in_own", add=got)

    w_in_state, dproj = _exchange_start([pre_w_in], BIG[:1], dproj, "w_in")
    grad_x, small = _input_grad((dproj, *rest[1:]), w_rel, order)
    pres, slots = _exchange_wait(w_in_state, grad_x)
    red_w_in = _reduce_into_shard(slots[0], pres[0], "col", pos, "reduce_w_in")
    share_w_in = _share_copies(BIG[:1])
    wsend, wrecv, (red_w_in, small) = _start_copies("share_w_in_start", [red_w_in, small], 1, share_w_in)
    grad_x = grad_x.reshape(x.shape)

    slots = _small_slots(small, (2 * pos[:1] + pos[1:]))
    ssend, srecv, (small, slots) = _start_copies("gather_small_start", [small, slots], 7, _small_copies)
    reds_rest = _wait_copies("share_rest_wait", rsend, rrecv, reds_rest, share_rest, slots)
    grads = dict(zip([name for name, _ in BIG[1:]], reds_rest))
    delta, new_m, new_v = {}, {}, {}
    for name, _ in BIG[1:]:
        delta[name], new_m[name], new_v[name], grads[name] = _adamw(w[name], grads[name], m[name], v[name],
                                                                    "adamw_" + name, with_grad=True)

    small, slots = _wait_copies("gather_small_wait", ssend, srecv, [small, slots], _small_copies, delta[BIG[-1][0]])
    tot = _sum_small(slots)[0]
    loss = tot[0]
    off = 128
    for name, size in (("norm_g", D), ("mem_norm_g", D), ("attn_q_norm", NGROUP * HEAD), ("attn_k_norm", NGROUP * HEAD),
                       ("conv_w", 3 * CONVW), ("mem_q_norm", MEM_HD), ("mem_k_norm", MEM_HD)):
        grads[name] = tot[off:off + size]
        off += size
    cw = conv_w.shape[1]
    grads["conv_w"] = lax.dynamic_slice(grads["conv_w"].reshape(3, CONVW), (0, chip * cw), (3, cw))
    for name in SMALL:
        grads[name] = grads[name].reshape(w[name].shape)

    def packed(t):
        return jnp.concatenate([t[name].reshape(1, -1) for name in SMALL], axis=1)

    ds, ms, vs = _adamw(packed(w), packed(grads), packed(m), packed(v), "adamw_small")
    shared, = _wait_copies("share_w_in_wait", wsend, wrecv, [red_w_in], share_w_in, ds)
    delta["w_in"], new_m["w_in"], new_v["w_in"], grads["w_in"] = _adamw(w["w_in"], shared, m["w_in"], v["w_in"],
                                                                        "adamw_w_in", with_grad=True)
    off = 0
    for name in SMALL:
        size = w[name].size
        delta[name] = ds[0, off:off + size].reshape(w[name].shape)
        new_m[name] = ms[0, off:off + size].reshape(w[name].shape)
        new_v[name] = vs[0, off:off + size].reshape(w[name].shape)
        off += size

    return (loss, grad_x, *[grads[n] for n in WEIGHTS], *[delta[n] for n in WEIGHTS],
            *[new_m[n] for n in WEIGHTS], *[new_v[n] for n in WEIGHTS])
```

```python
import functools

import jax
import jax.numpy as jnp
from jax import lax
from jax.experimental import pallas as pl
from jax.experimental.pallas import tpu as pltpu

F32 = jnp.float32
MXU_DTYPE = jnp.bfloat16
WIRE_DTYPE = jnp.bfloat16
PROJ_DTYPE = jnp.bfloat16
EPS = 1e-6
NEG = -1e30

HEAD = 128
HPG = 4
GW = HPG * HEAD
DILATIONS = (1, 4, 16)
NGROUP = len(DILATIONS)
BLK = 128
QKV = NGROUP * GW
CONVW = 1024
MEM_HEADS = 4
MEM_HD = 256
MEMW = MEM_HEADS * MEM_HD
Q0, K0, V0 = 0, QKV, 2 * QKV
ZA = 3 * QKV
CB, CC, CV, ZC = ZA + GW, ZA + GW + CONVW, ZA + GW + 2 * CONVW, ZA + GW + 3 * CONVW
MQ = ZC + CONVW
ZM = MQ + MEMW
G0 = ZM + MEMW

ADAM_LR, ADAM_B1, ADAM_B2, ADAM_EPS, ADAM_WD, ADAM_STEP = 0.001, 0.9, 0.999, 1e-08, 0.01, 10

VMEM_LIMIT = 56 * 1024 * 1024
MESH = pl.DeviceIdType.MESH
ANY = pl.BlockSpec(memory_space=pl.ANY)


def _tile(n, pref, mult=128):
    t = min(pref, n)
    while t > mult and (n % t or t % mult):
        t -= mult
    assert n % t == 0, (n, pref)
    return t


def _call(body, *, name, out_shape, grid=(), in_specs=None, out_specs=None, scratch_shapes=(),
          aliases=None, grid_spec=None):
    kw = {}
    if grid_spec is not None:
        kw["grid_spec"] = grid_spec
        ngrid = len(grid_spec.grid)
    else:
        kw.update(grid=grid, in_specs=in_specs, out_specs=out_specs, scratch_shapes=list(scratch_shapes))
        ngrid = len(grid)
    params = pltpu.CompilerParams(dimension_semantics=("arbitrary",) * ngrid, vmem_limit_bytes=VMEM_LIMIT)
    return pl.pallas_call(body, name=name, out_shape=out_shape, compiler_params=params,
                          input_output_aliases=aliases or {}, **kw)


_DIMS = {"nn": (((1,), (0,)), ((), ())), "nt": (((1,), (1,)), ((), ())), "tn": (((0,), (0,)), ((), ()))}


def _mxu(a, b, mode):
    return lax.dot_general(a.astype(MXU_DTYPE), b.astype(MXU_DTYPE), _DIMS[mode], preferred_element_type=F32)


@functools.partial(jax.custom_vjp, nondiff_argnums=(2,))
def _dot(a, b, mode):
    return _mxu(a, b, mode)


def _dot_fwd(a, b, mode):
    return _mxu(a, b, mode), (a, b)


def _dot_bwd(mode, res, g):
    a, b = res
    if mode == "nn":
        return _mxu(g, b, "nt"), _mxu(a, g, "tn")
    if mode == "nt":
        return _mxu(g, b, "nn"), _mxu(g, a, "tn")
    return _mxu(b, g, "nt"), _mxu(a, g, "nn")


_dot.defvjp(_dot_fwd, _dot_bwd)


def _sig(z):
    return 1.0 / (1.0 + jnp.exp(-z))


def _silu(z):
    return z * _sig(z)


def _rms_rows(t, g):
    return t * lax.rsqrt(jnp.mean(t * t, axis=-1, keepdims=True) + EPS) * g


def _attn_block(q, k2, v2, gq, gk, first):
    qn = _rms_rows(q, gq)
    kn = _rms_rows(k2, gk)
    s = jnp.where(_band_mask(first, k2.shape[0]), _mxu(qn, kn, "nt") * (HEAD ** -0.5), NEG)
    m = jnp.max(s, axis=-1, keepdims=True)
    p = jnp.exp(s - m)
    den = jnp.sum(p, axis=-1, keepdims=True)
    o = _mxu(p, v2, "nn") / den
    return o, m + jnp.log(den)


def _band_mask(first, nkeys):
    a = lax.broadcasted_iota(jnp.int32, (BLK, nkeys), 0)
    b = lax.broadcasted_iota(jnp.int32, (BLK, nkeys), 1)
    if nkeys == BLK:
        return b <= a
    return (b >= a) & (b <= a + BLK) & (b >= jnp.where(first, BLK, 0))


def _norm_parts(t):
    r = lax.rsqrt(jnp.mean(t * t, axis=-1, keepdims=True) + EPS)
    return r, t * r


def _norm_bwd(dn, g, r, th):
    dth = dn * g
    return r * (dth - th * jnp.mean(dth * th, axis=-1, keepdims=True)), jnp.sum(dn * th, axis=0, keepdims=True)


def _attn_block_bwd(q, k2, v2, gq, gk, first, do, o, lse, dlse):
    scale = HEAD ** -0.5
    rq, qh = _norm_parts(q)
    rk, kh = _norm_parts(k2)
    qn, kn = qh * gq, kh * gk
    s = jnp.where(_band_mask(first, k2.shape[0]), _mxu(qn, kn, "nt") * scale, NEG)
    p = jnp.exp(s - lse)
    ds = p * (_mxu(do, v2, "nt") + (dlse - jnp.sum(do * o, axis=-1, keepdims=True))) * scale
    dq, dgq = _norm_bwd(_mxu(ds, kn, "nn"), gq, rq, qh)
    dk2, dgk = _norm_bwd(_mxu(ds, qn, "tn"), gk, rk, kh)
    return dq, dk2, _mxu(p, do, "tn"), dgq, dgk


def _combine(o1, o2, o3, l1, l2, l3, z):
    m = lax.stop_gradient(jnp.maximum(jnp.maximum(l1, l2), l3))
    e1, e2, e3 = jnp.exp(l1 - m), jnp.exp(l2 - m), jnp.exp(l3 - m)
    return (e1 * o1 + e2 * o2 + e3 * o3) / (e1 + e2 + e3) * _silu(z)


def _mem_block(q, z, kv, gq, gk):
    outs = []
    for h in range(MEM_HEADS):
        sl = slice(h * MEM_HD, (h + 1) * MEM_HD)
        qn = _rms_rows(q[:, sl], gq)
        kn = _rms_rows(kv[:, sl], gk)
        s = _dot(qn, kn, "nt") * (MEM_HD ** -0.5)
        m = lax.stop_gradient(jnp.max(s, axis=-1, keepdims=True))
        p = jnp.exp(s - m)
        den = jnp.sum(p, axis=-1, keepdims=True)
        outs.append(_dot(p, kv[:, MEMW + h * MEM_HD:MEMW + (h + 1) * MEM_HD], "nn") / den)
    return jnp.concatenate(outs, axis=-1) * _silu(z)


def _cast(w, name):
    R, C = w.shape
    tr, tc = _tile(R, 512, 8), _tile(C, 2176)

    def body(w_ref, o_ref):
        o_ref[...] = w_ref[...].astype(o_ref.dtype)

    spec = pl.BlockSpec((tr, tc), lambda i, j: (i, j))
    return _call(body, name=name, grid=(R // tr, C // tc), in_specs=[spec], out_specs=spec,
                 out_shape=jax.ShapeDtypeStruct((R, C), WIRE_DTYPE))(w)


def _place_shard(w, kind, pos, dtype, name, slot=0, into=None, half=None):
    R, C = w.shape
    tr, tc = _tile(R, 512, 8), _tile(C if half is None else C // 2, 2176)
    nr, nc = R // tr, C // tc
    ncols = nc if half is None else nc // 2

    def body(pos_ref, w_ref, *rest):
        rest[-1][...] = w_ref[...].astype(rest[-1].dtype)

    def col(j, pos_ref):
        if half is None:
            return j
        return (pos_ref[1] if half == 0 else 1 - pos_ref[1]) * ncols + j

    if kind == "col":
        full = (R, 4 * C)
        out = pl.BlockSpec((tr, tc), lambda i, j, pos_ref: (i, pos_ref[slot] * nc + col(j, pos_ref)))
    else:
        full, out = (4 * R, C), pl.BlockSpec((tr, tc), lambda i, j, pos_ref: (pos_ref[slot] * nr + i, j))
    in_specs, args = [pl.BlockSpec((tr, tc), lambda i, j, pos_ref: (i, col(j, pos_ref)))], [pos, w]
    if into is not None:
        in_specs.append(ANY)
        args.append(into)
    spec = pltpu.PrefetchScalarGridSpec(num_scalar_prefetch=1, grid=(nr, ncols), in_specs=in_specs, out_specs=out)
    return _call(body, name=name, grid_spec=spec, out_shape=jax.ShapeDtypeStruct(full, dtype),
                 aliases={} if into is None else {2: 0})(*args)


def _matmul(a, b, mode, out_dtype, *, name, tm=512, tn=512, tk=512):
    if mode == "nn":
        (M, K), (_, N) = a.shape, b.shape
    elif mode == "nt":
        (M, K), (N, _) = a.shape, b.shape
    else:
        (K, M), (_, N) = a.shape, b.shape
    tm, tn, tk = _tile(M, tm), _tile(N, tn), _tile(K, tk)
    nk = K // tk

    def body(a_ref, b_ref, o_ref, *acc):
        part = lax.dot_general(a_ref[...], b_ref[...], _DIMS[mode], preferred_element_type=F32)
        if nk == 1:
            o_ref[...] = part.astype(o_ref.dtype)
            return
        acc_ref, = acc
        k = pl.program_id(2)

        @pl.when(k == 0)
        def _():
            acc_ref[...] = part

        @pl.when(k > 0)
        def _():
            acc_ref[...] += part

        @pl.when(k == nk - 1)
        def _():
            o_ref[...] = acc_ref[...].astype(o_ref.dtype)

    a_spec = pl.BlockSpec((tk, tm), lambda i, j, k: (k, i)) if mode == "tn" else pl.BlockSpec((tm, tk), lambda i, j, k: (i, k))
    b_spec = pl.BlockSpec((tn, tk), lambda i, j, k: (j, k)) if mode == "nt" else pl.BlockSpec((tk, tn), lambda i, j, k: (k, j))
    return _call(body, name=name, grid=(M // tm, N // tn, nk), in_specs=[a_spec, b_spec],
                 out_specs=pl.BlockSpec((tm, tn), lambda i, j, k: (i, j)),
                 out_shape=jax.ShapeDtypeStruct((M, N), out_dtype),
                 scratch_shapes=[] if nk == 1 else [pltpu.VMEM((tm, tn), F32)])(a, b)


def _rms_fwd(x, g, name):
    R, D = x.shape
    tr = _tile(R, 512)

    def body(x_ref, g_ref, o_ref, t_ref):
        y = _rms_rows(x_ref[...], g_ref[...])
        o_ref[...] = y.astype(o_ref.dtype)
        t_ref[...] = y.T.astype(t_ref.dtype)

    row = pl.BlockSpec((tr, D), lambda i: (i, 0))
    return _call(body, name=name, grid=(R // tr,), in_specs=[row, pl.BlockSpec((1, D), lambda i: (0, 0))],
                 out_specs=[row, pl.BlockSpec((D, tr), lambda i: (0, i))],
                 out_shape=[jax.ShapeDtypeStruct((R, D), MXU_DTYPE), jax.ShapeDtypeStruct((D, R), MXU_DTYPE)])(x, g)


def _rms_bwd(x, dh, g, dy, name):
    R, D = x.shape
    tr = _tile(R, 512)
    with_dx = dy is not None

    def body(*refs):
        if with_dx:
            x_ref, dh_ref, g_ref, dy_ref, dx_ref, dg_ref = refs
        else:
            x_ref, dh_ref, g_ref, dg_ref = refs
        xv, dhv = x_ref[...], dh_ref[...]
        r = lax.rsqrt(jnp.mean(xv * xv, axis=-1, keepdims=True) + EPS)
        xh = xv * r

        @pl.when(pl.program_id(0) == 0)
        def _():
            dg_ref[...] = jnp.zeros_like(dg_ref)

        dg_ref[...] += jnp.sum(dhv * xh, axis=0, keepdims=True)
        if with_dx:
            dxh = dhv * g_ref[...]
            dx_ref[...] = dy_ref[...] + r * (dxh - xh * jnp.mean(dxh * xh, axis=-1, keepdims=True))

    row = pl.BlockSpec((tr, D), lambda i: (i, 0))
    vec = pl.BlockSpec((1, D), lambda i: (0, 0))
    dg_shape = jax.ShapeDtypeStruct((1, D), F32)
    if with_dx:
        return _call(body, name=name, grid=(R // tr,), in_specs=[row, row, vec, row], out_specs=[row, vec],
                     out_shape=[jax.ShapeDtypeStruct((R, D), F32), dg_shape])(x, dh, g, dy)
    return None, _call(body, name=name, grid=(R // tr,), in_specs=[row, row, vec], out_specs=vec,
                       out_shape=dg_shape)(x, dh, g)


def _attn_geom(g, d):
    hc = HPG if d == 1 else 1
    cw = hc * HEAD
    cq, ck, cv = (Q0 + g * GW) // cw, (K0 + g * GW) // cw, (V0 + g * GW) // cw
    return (1, BLK * d, cw), hc, HPG // hc, cq, ck, cv


def _rows(ref, r, d, sl):
    if d == 1:
        return ref[0, :, sl]
    return ref.at[0][pl.ds(r, BLK, stride=d), sl]


def _set_rows(ref, r, d, sl, val):
    if d == 1:
        ref[0, :, sl] = val
    else:
        ref.at[0][pl.ds(r, BLK, stride=d), sl] = val


def _stage_rows(ref, r, d, sl, val):
    if d == 1:
        ref[:, sl] = val
    else:
        ref[pl.ds(r, BLK, stride=d), sl] = val


def _proj_stages(blk, d):
    return [] if d == 1 else [pltpu.VMEM(blk[1:], F32)] * 5


def _proj_rows(refs, stages, d):
    if d == 1:
        return [lambda r, sl, ref=ref: ref[0, :, sl].astype(F32) for ref in refs]
    for ref, stage in zip(refs, stages):
        stage[...] = ref[0].astype(F32)
    return [lambda r, sl, stage=stage: stage[pl.ds(r, BLK, stride=d), sl] for stage in stages]


def _attn_fwd(proj3, gq, gk, g, d):
    Bl, S, _ = proj3.shape
    blk, hc, ncb, cq, ck, cv = _attn_geom(g, d)
    nb = S // blk[1]
    if nb == 1:
        return _attn_single_fwd(proj3, gq, gk, g, d)

    def body(q_ref, kp_ref, kc_ref, vp_ref, vc_ref, gq_ref, gk_ref, o_ref, lse_ref, *stages):
        first = pl.program_id(2) == 0
        q, kp, kc, vp, vc = _proj_rows((q_ref, kp_ref, kc_ref, vp_ref, vc_ref), stages, d)
        def run(alone):
            for r in range(d):
                for h in range(hc):
                    sl = slice(h * HEAD, (h + 1) * HEAD)
                    if alone:
                        k2, v2 = kc(r, sl), vc(r, sl)
                    else:
                        k2 = jnp.concatenate([kp(r, sl), kc(r, sl)], axis=0)
                        v2 = jnp.concatenate([vp(r, sl), vc(r, sl)], axis=0)
                    o, lse = _attn_block(q(r, sl), k2, v2, gq_ref[...], gk_ref[...], False)
                    _set_rows(o_ref, r, d, sl, o)
                    _set_rows(lse_ref, r, d, sl, jnp.broadcast_to(lse, (BLK, HEAD)))

        pl.when(first)(lambda: run(True))
        pl.when(jnp.logical_not(first))(lambda: run(False))

    def cur(c0):
        return pl.BlockSpec(blk, lambda b, j, i: (b, i, c0 + j))

    def prev(c0):
        return pl.BlockSpec(blk, lambda b, j, i: (b, jnp.maximum(i - 1, 0), c0 + j))

    vec = pl.BlockSpec((1, HEAD), lambda b, j, i: (0, 0))
    out = pl.BlockSpec(blk, lambda b, j, i: (b, i, j))
    shp = jax.ShapeDtypeStruct((Bl, S, GW), F32)
    return _call(body, name=f"attn_fwd_g{g}", grid=(Bl, ncb, nb),
                 in_specs=[cur(cq), prev(ck), cur(ck), prev(cv), cur(cv), vec, vec],
                 out_specs=[out, out], out_shape=[shp, shp], scratch_shapes=_proj_stages(blk, d),
                 )(proj3, proj3, proj3, proj3, proj3, gq, gk)


def _attn_single_fwd(proj3, gq, gk, g, d):
    Bl, S, _ = proj3.shape
    blk, hc, ncb, cq, ck, cv = _attn_geom(g, d)

    def body(q_ref, k_ref, v_ref, gq_ref, gk_ref, o_ref, lse_ref, *stages):
        q, k, v = _proj_rows((q_ref, k_ref, v_ref), stages, d)
        for r in range(d):
            for h in range(hc):
                sl = slice(h * HEAD, (h + 1) * HEAD)
                o, lse = _attn_block(q(r, sl), k(r, sl), v(r, sl), gq_ref[...], gk_ref[...], True)
                _set_rows(o_ref, r, d, sl, o)
                _set_rows(lse_ref, r, d, sl, jnp.broadcast_to(lse, (BLK, HEAD)))

    def at(c0):
        return pl.BlockSpec(blk, lambda b, j: (b, 0, c0 + j))

    vec = pl.BlockSpec((1, HEAD), lambda b, j: (0, 0))
    shp = jax.ShapeDtypeStruct((Bl, S, GW), F32)
    return _call(body, name=f"attn_fwd_g{g}", grid=(Bl, ncb), in_specs=[at(cq), at(ck), at(cv), vec, vec],
                 out_specs=[at(0), at(0)], out_shape=[shp, shp], scratch_shapes=_proj_stages(blk, d)[:3],
                 )(proj3, proj3, proj3, gq, gk)


def _attn_single_bwd(proj3, gq, gk, o3, l3, do3, dl3, g, d):
    Bl, S, _ = proj3.shape
    blk, hc, ncb, cq, ck, cv = _attn_geom(g, d)

    def body(q_ref, k_ref, v_ref, gq_ref, gk_ref, o_ref, l_ref, do_ref, dl_ref,
             dq_ref, dk_ref, dv_ref, dgq_ref, dgk_ref, sq_ref, sk_ref, sv_ref, *stages):
        @pl.when((pl.program_id(0) == 0) & (pl.program_id(1) == 0))
        def _():
            dgq_ref[...] = jnp.zeros_like(dgq_ref)
            dgk_ref[...] = jnp.zeros_like(dgk_ref)

        dgq, dgk = jnp.zeros((1, HEAD), F32), jnp.zeros((1, HEAD), F32)
        q, k, v = _proj_rows((q_ref, k_ref, v_ref), stages, d)
        for r in range(d):
            for h in range(hc):
                sl = slice(h * HEAD, (h + 1) * HEAD)
                dq, dk, dv, a, b = _attn_block_bwd(
                    q(r, sl), k(r, sl), v(r, sl), gq_ref[...], gk_ref[...], True, _rows(do_ref, r, d, sl),
                    _rows(o_ref, r, d, sl), _rows(l_ref, r, d, sl)[:, :1], _rows(dl_ref, r, d, sl)[:, :1])
                _stage_rows(sq_ref, r, d, sl, dq)
                _stage_rows(sk_ref, r, d, sl, dk)
                _stage_rows(sv_ref, r, d, sl, dv)
                dgq, dgk = dgq + a, dgk + b
        dgq_ref[...] += dgq
        dgk_ref[...] += dgk
        dq_ref[0] = sq_ref[...].astype(dq_ref.dtype)
        dk_ref[0] = sk_ref[...].astype(dk_ref.dtype)
        dv_ref[0] = sv_ref[...].astype(dv_ref.dtype)

    def at(c0):
        return pl.BlockSpec(blk, lambda b, j: (b, 0, c0 + j))

    vec = pl.BlockSpec((1, HEAD), lambda b, j: (0, 0))
    shp = jax.ShapeDtypeStruct((Bl, S, GW), MXU_DTYPE)
    gshp = jax.ShapeDtypeStruct((1, HEAD), F32)
    return _call(body, name=f"attn_bwd_g{g}", grid=(Bl, ncb),
                 in_specs=[at(cq), at(ck), at(cv), vec, vec, at(0), at(0), at(0), at(0)],
                 out_specs=[at(0), at(0), at(0), vec, vec], out_shape=[shp, shp, shp, gshp, gshp],
                 scratch_shapes=[pltpu.VMEM(blk[1:], F32)] * 3 + _proj_stages(blk, d)[:3],
                 )(proj3, proj3, proj3, gq, gk, o3, l3, do3, dl3)


def _attn_bwd(proj3, gq, gk, o3, l3, do3, dl3, g, d):
    Bl, S, _ = proj3.shape
    blk, hc, ncb, cq, ck, cv = _attn_geom(g, d)
    nb = S // blk[1]
    if nb == 1:
        return _attn_single_bwd(proj3, gq, gk, o3, l3, do3, dl3, g, d)

    def body(q_ref, kp_ref, kc_ref, vp_ref, vc_ref, gq_ref, gk_ref, o_ref, l_ref, do_ref, dl_ref,
             dq_ref, dk_ref, dv_ref, dgq_ref, dgk_ref, ck_ref, cv_ref, sq_ref, sk_ref, sv_ref, *stages):
        i = pl.program_id(2)
        first = i == 0

        @pl.when((pl.program_id(0) == 0) & (pl.program_id(1) == 0) & first)
        def _():
            dgq_ref[...] = jnp.zeros_like(dgq_ref)
            dgk_ref[...] = jnp.zeros_like(dgk_ref)

        def run(alone):
            dgq, dgk = jnp.zeros((1, HEAD), F32), jnp.zeros((1, HEAD), F32)
            q, kp, kc, vp, vc = _proj_rows((q_ref, kp_ref, kc_ref, vp_ref, vc_ref), stages, d)
            for r in range(d):
                rs = slice(r * BLK, (r + 1) * BLK)
                for h in range(hc):
                    sl = slice(h * HEAD, (h + 1) * HEAD)
                    if alone:
                        k2, v2 = kc(r, sl), vc(r, sl)
                    else:
                        k2 = jnp.concatenate([kp(r, sl), kc(r, sl)], axis=0)
                        v2 = jnp.concatenate([vp(r, sl), vc(r, sl)], axis=0)
                    dq, dk2, dv2, a, b = _attn_block_bwd(
                        q(r, sl), k2, v2, gq_ref[...], gk_ref[...], False, _rows(do_ref, r, d, sl),
                        _rows(o_ref, r, d, sl), _rows(l_ref, r, d, sl)[:, :1], _rows(dl_ref, r, d, sl)[:, :1])
                    _stage_rows(sq_ref, r, d, sl, dq)
                    if alone:
                        _stage_rows(sk_ref, r, d, sl, jnp.zeros((BLK, HEAD), F32))
                        _stage_rows(sv_ref, r, d, sl, jnp.zeros((BLK, HEAD), F32))
                    else:
                        _stage_rows(sk_ref, r, d, sl, ck_ref[rs, sl] + dk2[:BLK])
                        _stage_rows(sv_ref, r, d, sl, cv_ref[rs, sl] + dv2[:BLK])
                    ck_ref[rs, sl] = dk2[-BLK:]
                    cv_ref[rs, sl] = dv2[-BLK:]
                    dgq, dgk = dgq + a, dgk + b
            dgq_ref[...] += dgq
            dgk_ref[...] += dgk
            dq_ref[0] = sq_ref[...].astype(dq_ref.dtype)

        pl.when(first)(lambda: run(True))
        pl.when((i > 0) & (i < nb))(lambda: run(False))

        @pl.when(i == nb)
        def _():
            for r in range(d):
                rs = slice(r * BLK, (r + 1) * BLK)
                _stage_rows(sk_ref, r, d, slice(None), ck_ref[rs, :])
                _stage_rows(sv_ref, r, d, slice(None), cv_ref[rs, :])

        dk_ref[0] = sk_ref[...].astype(dk_ref.dtype)
        dv_ref[0] = sv_ref[...].astype(dv_ref.dtype)

    def cur(c0):
        return pl.BlockSpec(blk, lambda b, j, i: (b, jnp.minimum(i, nb - 1), c0 + j))

    def prev(c0):
        return pl.BlockSpec(blk, lambda b, j, i: (b, jnp.clip(i - 1, 0, nb - 1), c0 + j))

    vec = pl.BlockSpec((1, HEAD), lambda b, j, i: (0, 0))
    at_q = pl.BlockSpec(blk, lambda b, j, i: (b, jnp.minimum(i, nb - 1), j))
    at_k = pl.BlockSpec(blk, lambda b, j, i: (b, jnp.maximum(i - 1, 0), j))
    shp = jax.ShapeDtypeStruct((Bl, S, GW), MXU_DTYPE)
    gshp = jax.ShapeDtypeStruct((1, HEAD), F32)
    return _call(body, name=f"attn_bwd_g{g}", grid=(Bl, ncb, nb + 1),
                 in_specs=[cur(cq), prev(ck), cur(ck), prev(cv), cur(cv), vec, vec, at_q, at_q, at_q, at_q],
                 out_specs=[at_q, at_k, at_k, vec, vec], out_shape=[shp, shp, shp, gshp, gshp],
                 scratch_shapes=[pltpu.VMEM(blk[1:], F32)] * 5 + _proj_stages(blk, d),
                 )(proj3, proj3, proj3, proj3, proj3, gq, gk, o3, l3, do3, dl3)


def _combine_fwd(os, ls, proj2):
    T = proj2.shape[0]
    tr = _tile(T, 512)

    def body(o1, o2, o3, l1, l2, l3, z, a_ref, at_ref):
        a = _combine(o1[...], o2[...], o3[...], l1[...], l2[...], l3[...], z[...].astype(F32))
        a_ref[...] = a.astype(a_ref.dtype)
        at_ref[...] = a.T.astype(at_ref.dtype)

    row = pl.BlockSpec((tr, GW), lambda i: (i, 0))
    return _call(body, name="combine_fwd", grid=(T // tr,),
                 in_specs=[row] * 6 + [pl.BlockSpec((tr, GW), lambda i: (i, ZA // GW))],
                 out_specs=[row, pl.BlockSpec((GW, tr), lambda i: (0, i))],
                 out_shape=[jax.ShapeDtypeStruct((T, GW), MXU_DTYPE), jax.ShapeDtypeStruct((GW, T), MXU_DTYPE)],
                 )(*os, *ls, proj2)


def _combine_bwd(os, ls, proj2, da, dproj):
    T = proj2.shape[0]
    tr = _tile(T, 512)

    def body(o1, o2, o3, l1, l2, l3, z, da_ref, _, d1, d2, d3, e1, e2, e3, dz_ref):
        _, vjp = jax.vjp(_combine, o1[...], o2[...], o3[...], l1[...], l2[...], l3[...], z[...].astype(F32))
        go1, go2, go3, gl1, gl2, gl3, gz = vjp(da_ref[...])
        d1[...], d2[...], d3[...] = go1, go2, go3
        dz_ref[...] = gz.astype(dz_ref.dtype)
        for ref, gl in ((e1, gl1), (e2, gl2), (e3, gl3)):
            for h in range(HPG):
                sl = slice(h * HEAD, (h + 1) * HEAD)
                ref[:, sl] = jnp.broadcast_to(jnp.sum(gl[:, sl], axis=-1, keepdims=True), (tr, HEAD))

    row = pl.BlockSpec((tr, GW), lambda i: (i, 0))
    f = jax.ShapeDtypeStruct((T, GW), F32)
    z_attn = pl.BlockSpec((tr, GW), lambda i: (i, ZA // GW))
    outs = _call(body, name="combine_bwd", grid=(T // tr,), in_specs=[row] * 6 + [z_attn, row, ANY],
                 out_specs=[row] * 6 + [z_attn], out_shape=[f] * 6 + [jax.ShapeDtypeStruct(dproj.shape, dproj.dtype)],
                 aliases={8: 6})(*os, *ls, proj2, da, dproj)
    return outs[:3], outs[3:6], outs[6]


def _shift_down(u, j, t):
    return jnp.where(t >= j, pltpu.roll(u, j, 0), 0.0)


def _shift_up(u, j, t):
    n = u.shape[0]
    return jnp.where(t < n - j, pltpu.roll(u, n - j, 0), 0.0)


def _conv_specs(Bl, S, cw):
    def sec(c0):
        return pl.BlockSpec((1, S, cw), lambda j, b: (b, 0, c0 // cw + j))
    return [sec(CB), sec(CC), sec(CV), sec(ZC)], pl.BlockSpec((3, cw), lambda j, b: (0, j))


def _conv_fwd(proj3, conv_w):
    Bl, S, _ = proj3.shape
    cw = 256
    secs, wspec = _conv_specs(Bl, S, cw)

    def body(b_ref, c_ref, v_ref, z_ref, w_ref, o_ref, ot_ref):
        t = lax.broadcasted_iota(jnp.int32, (S, cw), 0)
        u = c_ref[0].astype(F32) * v_ref[0].astype(F32)
        y = w_ref[0:1, :] * u + w_ref[1:2, :] * _shift_down(u, 1, t) + w_ref[2:3, :] * _shift_down(u, 2, t)
        out = b_ref[0].astype(F32) * y * _silu(z_ref[0].astype(F32))
        o_ref[0] = out.astype(o_ref.dtype)
        ot_ref[...] = out.T.astype(ot_ref.dtype)

    return _call(body, name="conv_fwd", grid=(CONVW // cw, Bl), in_specs=secs + [wspec],
                 out_specs=[pl.BlockSpec((1, S, cw), lambda j, b: (b, 0, j)), pl.BlockSpec((cw, S), lambda j, b: (j, b))],
                 out_shape=[jax.ShapeDtypeStruct((Bl, S, CONVW), MXU_DTYPE),
                            jax.ShapeDtypeStruct((CONVW, Bl * S), MXU_DTYPE)])(proj3, proj3, proj3, proj3, conv_w)


def _conv_bwd(proj3, conv_w, dcc3, dproj):
    Bl, S, _ = proj3.shape
    cw = 256
    secs, wspec = _conv_specs(Bl, S, cw)

    def body(b_ref, c_ref, v_ref, z_ref, w_ref, d_ref, _, dproj_ref, dw_ref, stage, sems):
        t = lax.broadcasted_iota(jnp.int32, (S, cw), 0)
        bv, cv, vv, zv = (r[0].astype(F32) for r in (b_ref, c_ref, v_ref, z_ref))
        dv = d_ref[0]
        u = cv * vv
        u1, u2 = _shift_down(u, 1, t), _shift_down(u, 2, t)
        y = w_ref[0:1, :] * u + w_ref[1:2, :] * u1 + w_ref[2:3, :] * u2
        sg = _sig(zv)
        sz = zv * sg
        gy = dv * bv * sz
        du = w_ref[0:1, :] * gy + w_ref[1:2, :] * _shift_up(gy, 1, t) + w_ref[2:3, :] * _shift_up(gy, 2, t)
        j, b = pl.program_id(0), pl.program_id(1)
        tiles = [dv * y * sz, du * vv, du * cv, dv * bv * y * sg * (1.0 + zv * (1.0 - sg))]
        dsts = [dproj_ref.at[pl.ds(b * S, S), pl.ds(c0 + j * cw, cw)] for c0 in (CB, CC, CV, ZC)]
        _emit_tiles(j * Bl + b, (CONVW // cw) * Bl, tiles, dsts, stage, sems)

        @pl.when(pl.program_id(1) == 0)
        def _():
            dw_ref[...] = jnp.zeros_like(dw_ref)

        dw_ref[0:1, :] += jnp.sum(gy * u, axis=0, keepdims=True)
        dw_ref[1:2, :] += jnp.sum(gy * u1, axis=0, keepdims=True)
        dw_ref[2:3, :] += jnp.sum(gy * u2, axis=0, keepdims=True)

    blk = pl.BlockSpec((1, S, cw), lambda j, b: (b, 0, j))
    return _call(body, name="conv_bwd", grid=(CONVW // cw, Bl), in_specs=secs + [wspec, blk, ANY],
                 out_specs=[ANY, wspec],
                 out_shape=[jax.ShapeDtypeStruct(dproj.shape, dproj.dtype), jax.ShapeDtypeStruct((3, CONVW), F32)],
                 scratch_shapes=_emit_scratch(4, S, cw), aliases={6: 0})(proj3, proj3, proj3, proj3, conv_w, dcc3, dproj)


def _mem_specs(S, tq):
    q = pl.BlockSpec((1, tq, MEMW), lambda b, j: (b, j, MQ // MEMW))
    z = pl.BlockSpec((1, tq, MEMW), lambda b, j: (b, j, ZM // MEMW))
    kv = pl.BlockSpec((1, MEM_HD, 2 * MEMW), lambda b, j: (b, 0, 0))
    vec = pl.BlockSpec((1, MEM_HD), lambda b, j: (0, 0))
    blk = pl.BlockSpec((1, tq, MEMW), lambda b, j: (b, j, 0))
    return q, z, kv, vec, blk


def _mem_fwd(proj3, mkv3, gq, gk):
    Bl, S, _ = proj3.shape
    tq = _tile(S, 512)
    q, z, kv, vec, blk = _mem_specs(S, tq)

    def body(q_ref, z_ref, kv_ref, gq_ref, gk_ref, o_ref, ot_ref):
        out = _mem_block(q_ref[0].astype(F32), z_ref[0].astype(F32), kv_ref[0], gq_ref[...], gk_ref[...])
        o_ref[0] = out.astype(o_ref.dtype)
        ot_ref[...] = out.T.astype(ot_ref.dtype)

    nq = S // tq
    return _call(body, name="mem_fwd", grid=(Bl, nq), in_specs=[q, z, kv, vec, vec],
                 out_specs=[blk, pl.BlockSpec((MEMW, tq), lambda b, j: (0, b * nq + j))],
                 out_shape=[jax.ShapeDtypeStruct((Bl, S, MEMW), MXU_DTYPE),
                            jax.ShapeDtypeStruct((MEMW, Bl * S), MXU_DTYPE)])(proj3, proj3, mkv3, gq, gk)


def _mem_bwd(proj3, mkv3, gq, gk, dmo3, dproj):
    Bl, S, _ = proj3.shape
    tq = _tile(S, 256)
    q, z, kv, vec, blk = _mem_specs(S, tq)
    nq = S // tq

    def body(q_ref, z_ref, kv_ref, gq_ref, gk_ref, d_ref, _, dproj_ref, dkv_ref, dgq_ref, dgk_ref, stage, sems):
        _, vjp = jax.vjp(_mem_block, q_ref[0].astype(F32), z_ref[0].astype(F32), kv_ref[0], gq_ref[...], gk_ref[...])
        dq, dz, dkv, dgq, dgk = vjp(d_ref[0])
        j = pl.program_id(1)
        rows = pl.ds(pl.program_id(0) * S + j * tq, tq)
        dsts = [dproj_ref.at[rows, pl.ds(MQ, MEMW)], dproj_ref.at[rows, pl.ds(ZM, MEMW)]]
        _emit_tiles(pl.program_id(0) * nq + j, Bl * nq, [dq, dz], dsts, stage, sems)

        @pl.when(j == 0)
        def _():
            dkv_ref[0] = jnp.zeros_like(dkv)

        @pl.when((j == 0) & (pl.program_id(0) == 0))
        def _():
            dgq_ref[...] = jnp.zeros_like(dgq_ref)
            dgk_ref[...] = jnp.zeros_like(dgk_ref)

        dkv_ref[0] += dkv
        dgq_ref[...] += dgq
        dgk_ref[...] += dgk

    gshp = jax.ShapeDtypeStruct((1, MEM_HD), F32)
    return _call(body, name="mem_bwd", grid=(Bl, nq), in_specs=[q, z, kv, vec, vec, blk, ANY],
                 out_specs=[ANY, kv, vec, vec],
                 out_shape=[jax.ShapeDtypeStruct(dproj.shape, dproj.dtype), jax.ShapeDtypeStruct(mkv3.shape, F32),
                            gshp, gshp],
                 scratch_shapes=_emit_scratch(2, tq, MEMW), aliases={6: 0})(proj3, proj3, mkv3, gq, gk, dmo3, dproj)


def _merge_specs(T, D, tm, tn):
    def act(w):
        return pl.BlockSpec((tm, w), lambda i, n: (i, 0))

    def wsp(w):
        return pl.BlockSpec((w, tn), lambda i, n: (0, n))

    gates = [pl.BlockSpec((tm, tn), lambda i, n, k=k: (i, (G0 + k * D) // tn + n)) for k in range(3)]
    tile = pl.BlockSpec((tm, tn), lambda i, n: (i, n))
    return act, wsp, gates, tile


def _merge_fwd(a, cc, mo, wa, wc, wm, proj2):
    T, D = a.shape[0], wa.shape[1]
    tm, tn = _tile(T, 1024), _tile(D, 512)
    act, wsp, gates, tile = _merge_specs(T, D, tm, tn)

    def body(a_ref, c_ref, m_ref, wa_ref, wc_ref, wm_ref, g0, g1, g2, mg_ref, mt_ref, pa_ref, pc_ref, pm_ref):
        pa = jnp.dot(a_ref[...], wa_ref[...], preferred_element_type=F32)
        pc = jnp.dot(c_ref[...], wc_ref[...], preferred_element_type=F32)
        pm = jnp.dot(m_ref[...], wm_ref[...], preferred_element_type=F32)
        mg = _sig(g0[...].astype(F32)) * pa + _sig(g1[...].astype(F32)) * pc + _sig(g2[...].astype(F32)) * pm
        mg_ref[...] = mg.astype(mg_ref.dtype)
        mt_ref[...] = mg.T.astype(mt_ref.dtype)
        pa_ref[...] = pa.astype(pa_ref.dtype)
        pc_ref[...] = pc.astype(pc_ref.dtype)
        pm_ref[...] = pm.astype(pm_ref.dtype)

    shp = jax.ShapeDtypeStruct((T, D), MXU_DTYPE)
    return _call(body, name="merge_fwd", grid=(T // tm, D // tn),
                 in_specs=[act(GW), act(CONVW), act(MEMW), wsp(GW), wsp(CONVW), wsp(MEMW)] + gates,
                 out_specs=[tile, pl.BlockSpec((tn, tm), lambda i, n: (n, i)), tile, tile, tile],
                 out_shape=[shp, jax.ShapeDtypeStruct((D, T), MXU_DTYPE), shp, shp, shp],
                 )(a, cc, mo, wa, wc, wm, proj2, proj2, proj2)


def _emit_tiles(step, nsteps, tiles, dsts, stage, sems):
    slot = step % 2

    def copies(s):
        return [pltpu.make_async_copy(stage.at[s, k], dsts[k], sems.at[s, k]) for k in range(len(tiles))]

    @pl.when(step >= 2)
    def _():
        for cp in copies(slot):
            cp.wait()

    for k, t in enumerate(tiles):
        stage[slot, k] = t.astype(stage.dtype)
    for cp in copies(slot):
        cp.start()

    @pl.when(step == nsteps - 1)
    def _():
        for cp in copies(slot):
            cp.wait()
        if nsteps > 1:
            for cp in copies(1 - slot):
                cp.wait()


def _emit_scratch(k, rows, cols):
    return [pltpu.VMEM((2, k, rows, cols), MXU_DTYPE), pltpu.SemaphoreType.DMA((2, k))]


def _merge_bwd(dyb, w_out, proj2, pa, pc, pm):
    T, D = dyb.shape
    IN = proj2.shape[1]
    tm, tn = _tile(T, 1024), _tile(D, 512)
    _, _, gates, tile = _merge_specs(T, D, tm, tn)
    nn = D // tn

    def body(dy_ref, w_ref, g0, g1, g2, p0, p1, p2, dp0, dp1, dp2, dproj_ref, stage, sems):
        i, n = pl.program_id(0), pl.program_id(1)
        dm = lax.dot_general(dy_ref[...], w_ref[...], _DIMS["nt"], preferred_element_type=F32)
        tiles, dsts = [], []
        for k, (g_ref, p_ref, dp_ref) in enumerate(((g0, p0, dp0), (g1, p1, dp1), (g2, p2, dp2))):
            gt = _sig(g_ref[...].astype(F32))
            dp_ref[...] = (gt * dm).astype(dp_ref.dtype)
            tiles.append(dm * p_ref[...].astype(F32) * gt * (1.0 - gt))
            dsts.append(dproj_ref.at[pl.ds(i * tm, tm), pl.ds(G0 + k * D + n * tn, tn)])
        _emit_tiles(i * nn + n, (T // tm) * nn, tiles, dsts, stage, sems)

    shp = jax.ShapeDtypeStruct((T, D), MXU_DTYPE)
    return _call(body, name="merge_bwd", grid=(T // tm, nn),
                 in_specs=[pl.BlockSpec((tm, D), lambda i, n: (i, 0)), pl.BlockSpec((tn, D), lambda i, n: (n, 0))]
                 + gates + [tile] * 3,
                 out_specs=[tile] * 3 + [ANY], out_shape=[shp] * 3 + [jax.ShapeDtypeStruct((T, IN), MXU_DTYPE)],
                 scratch_shapes=_emit_scratch(3, tm, tn))(dyb, w_out, proj2, proj2, proj2, pa, pc, pm)


def _out_loss(merged, w_out, x, tgt):
    T, D = x.shape
    tm = _tile(T, 512)

    def body(m_ref, w_ref, x_ref, t_ref, dy_ref, dyb_ref, loss_ref):
        err = x_ref[...] + jnp.dot(m_ref[...], w_ref[...], preferred_element_type=F32) - t_ref[...]
        dy = err * (1.0 / D)
        dy_ref[...] = dy
        dyb_ref[...] = dy.astype(dyb_ref.dtype)

        @pl.when(pl.program_id(0) == 0)
        def _():
            loss_ref[...] = jnp.zeros_like(loss_ref)

        loss_ref[...] += jnp.sum(err * err) * (0.5 / D)

    row = pl.BlockSpec((tm, D), lambda i: (i, 0))
    return _call(body, name="out_loss", grid=(T // tm,),
                 in_specs=[row, pl.BlockSpec((D, D), lambda i: (0, 0)), row, row],
                 out_specs=[row, row, pl.BlockSpec((1, 128), lambda i: (0, 0))],
                 out_shape=[jax.ShapeDtypeStruct((T, D), F32), jax.ShapeDtypeStruct((T, D), MXU_DTYPE),
                            jax.ShapeDtypeStruct((1, 128), F32)])(merged, w_out, x, tgt)


def _proj_chunk(hb, w, meta, j, nslots, half, buf, name):
    T, D = hb.shape
    Cs = w.shape[1] // 4
    tm, tn = _tile(T, 1024), _tile(Cs // 2, 2176)
    nh = Cs // 2 // tn
    per = nh if half is not None else 2 * nh

    def body(meta_ref, a_ref, b_ref, *rest):
        rest[-1][...] = jnp.dot(a_ref[...], b_ref[...], preferred_element_type=F32).astype(rest[-1].dtype)

    def tile(n, m):
        if half is None:
            return n % per
        return (m[4] if half == 0 else 1 - m[4]) * nh + n % per

    in_specs = [pl.BlockSpec((tm, D), lambda n, i, m: (i, 0)),
                pl.BlockSpec((D, tn), lambda n, i, m: (0, (j + n // per) * 2 * nh + tile(n, m)))]
    args = [meta, hb, w]
    if buf is not None:
        in_specs.append(ANY)
        args.append(buf)
    spec = pltpu.PrefetchScalarGridSpec(
        num_scalar_prefetch=1, grid=(nslots * per, T // tm), in_specs=in_specs,
        out_specs=pl.BlockSpec((tm, tn), lambda n, i, m: (i, m[j + n // per] * 2 * nh + tile(n, m))))
    return _call(body, name=name, grid_spec=spec, out_shape=jax.ShapeDtypeStruct((T, 4 * Cs), PROJ_DTYPE),
                 aliases={} if buf is None else {3: 0})(*args)


def _norms(x, mem, norm_g, mem_norm_g):
    D = x.shape[-1]
    hb, hbt = _rms_fwd(x.reshape(-1, D), norm_g.reshape(1, D), "rms_x")
    mhb, _ = _rms_fwd(mem.reshape(-1, D), mem_norm_g.reshape(1, D), "rms_mem")
    return hb, hbt, mhb


def _attention_fwd(proj2, Bl, gq_all, gk_all):
    T, IN = proj2.shape
    proj3 = proj2.reshape(Bl, T // Bl, IN)
    os, ls = [], []
    for g, d in enumerate(DILATIONS):
        o, l = _attn_fwd(proj3, gq_all[g:g + 1], gk_all[g:g + 1], g, d)
        os.append(o.reshape(T, GW))
        ls.append(l.reshape(T, GW))
    return os, ls, _combine_fwd(os, ls, proj2)


def _conv_branch_fwd(proj2, Bl, conv_w):
    T, IN = proj2.shape
    cc, cct = _conv_fwd(proj2.reshape(Bl, T // Bl, IN), conv_w)
    return cc.reshape(T, CONVW), cct


def _weight_grads(x, mem, tgt, norm_g, mem_norm_g, gq_all, gk_all, conv_w, mem_gq, mem_gk, W, pre, early=None):
    Bl, S, D = x.shape
    T = Bl * S
    hb, hbt, mhb, proj2, os, ls, (a, at), (cc, cct) = pre
    IN = proj2.shape[1]
    proj3 = proj2.reshape(Bl, S, IN)
    x2, tgt2 = x.reshape(T, D), tgt.reshape(T, D)
    mem2 = mem.reshape(-1, D)
    ng, mng = norm_g.reshape(1, D), mem_norm_g.reshape(1, D)
    mgq, mgk = mem_gq.reshape(1, MEM_HD), mem_gk.reshape(1, MEM_HD)
    gqs = [gq_all[g:g + 1] for g in range(NGROUP)]
    gks = [gk_all[g:g + 1] for g in range(NGROUP)]

    mkv = _matmul(mhb, W["mem_w_kv"], "nn", F32, name="mem_kv", tm=512, tn=1024, tk=D)
    mkv3 = mkv.reshape(Bl, -1, 2 * MEMW)
    mo, mot = _mem_fwd(proj3, mkv3, mgq, mgk)
    mo = mo.reshape(T, MEMW)
    merged, mergedt, pa, pc, pm = _merge_fwd(a, cc, mo, W["w_br_attn"], W["w_br_conv"], W["w_br_mem"], proj2)
    dy, dyb, loss = _out_loss(merged, W["w_out"], x2, tgt2)

    G = {}
    G["w_out"] = _matmul(mergedt, dyb, "nn", WIRE_DTYPE, name="dw_out", tm=1024, tn=512, tk=T)
    dpa, dpc, dpm, dproj = _merge_bwd(dyb, W["w_out"], proj2, pa, pc, pm)
    G["w_br_attn"] = _matmul(at, dpa, "nn", WIRE_DTYPE, name="dw_br_attn", tm=512, tn=512, tk=T)
    G["w_br_conv"] = _matmul(cct, dpc, "nn", WIRE_DTYPE, name="dw_br_conv", tm=1024, tn=512, tk=T)
    G["w_br_mem"] = _matmul(mot, dpm, "nn", WIRE_DTYPE, name="dw_br_mem", tm=1024, tn=512, tk=T)
    da = _matmul(dpa, W["w_br_attn"], "nt", F32, name="d_attn", tm=1024, tn=512, tk=D)
    dcc = _matmul(dpc, W["w_br_conv"], "nt", F32, name="d_conv", tm=1024, tn=1024, tk=D)
    dmo = _matmul(dpm, W["w_br_mem"], "nt", F32, name="d_mem", tm=1024, tn=1024, tk=D)
    dproj, dmkv3, dmgq, dmgk = _mem_bwd(proj3, mkv3, mgq, mgk, dmo.reshape(Bl, S, MEMW), dproj)
    dmkv = _cast(dmkv3.reshape(-1, 2 * MEMW), "cast_dmkv")
    G["mem_w_kv"] = _matmul(mhb, dmkv, "tn", WIRE_DTYPE, name="dw_mem_kv", tm=1024, tn=1024, tk=512)
    early_state, dmkv = (None, dmkv) if early is None else early[0](G, dmkv)
    dmh = _matmul(dmkv, W["mem_w_kv"], "nt", F32, name="d_memh", tm=512, tn=1024, tk=2 * MEMW)
    _, dmng = _rms_bwd(mem2, dmh, mng, None, "rms_mem_bwd")
    if early is not None:
        early_state, da = early[1](early_state, dmng, da)

    dos, dls, dproj = _combine_bwd(os, ls, proj2, da, dproj)
    dgq, dgk = [], []
    for g, d in enumerate(DILATIONS):
        dq, dk, dv, gq_g, gk_g = _attn_bwd(proj3, gqs[g], gks[g], os[g].reshape(Bl, S, GW), ls[g].reshape(Bl, S, GW),
                                           dos[g].reshape(Bl, S, GW), dls[g].reshape(Bl, S, GW), g, d)
        for c0, part in ((Q0, dq), (K0, dk), (V0, dv)):
            dproj = lax.dynamic_update_slice(dproj, part.reshape(T, GW), (0, c0 + g * GW))
        dgq.append(gq_g)
        dgk.append(gk_g)
    dproj, dconv_w = _conv_bwd(proj3, conv_w, dcc.reshape(Bl, S, CONVW), dproj)
    small = [loss, None, dmng] + dgq + dgk + [dconv_w.reshape(1, 3 * CONVW), dmgq, dmgk]
    return G, (dproj, x2, ng, dy, small), early_state


DW_IN_TILE = 1024


def _dw_in_half(hbt, dproj, pos, own, name, add=None, tiles=None, into=None):
    D, T = hbt.shape
    IN = dproj.shape[1]
    R, tn = D // 2, _tile(IN, DW_IN_TILE)
    j0, nj = (0, IN // tn) if tiles is None else tiles

    def body(pos_ref, a_ref, b_ref, *rest):
        acc = jnp.dot(a_ref[...], b_ref[...], preferred_element_type=F32)
        if add is not None:
            acc = acc + rest[0][...].astype(F32)
        rest[-1][...] = acc.astype(rest[-1].dtype)

    tile = pl.BlockSpec((R, tn), lambda j, p: (0, j0 + j))
    in_specs = [pl.BlockSpec((R, T), lambda j, p: (p[1] if own else 1 - p[1], 0)),
                pl.BlockSpec((T, tn), lambda j, p: (0, j0 + j))]
    args = [pos, hbt, dproj]
    if add is not None:
        in_specs.append(tile)
        args.append(add)
    if into is not None:
        in_specs.append(ANY)
        args.append(into)
    spec = pltpu.PrefetchScalarGridSpec(num_scalar_prefetch=1, grid=(nj,), in_specs=in_specs, out_specs=tile)
    return _call(body, name=name, grid_spec=spec, out_shape=jax.ShapeDtypeStruct((R, IN), WIRE_DTYPE),
                 aliases={} if into is None else {len(args) - 1: 0})(*args)


def _d_h(dproj, w, order):
    T, IN = dproj.shape
    D, Cs = w.shape[0], IN // 4
    tm, tn = _tile(T, 1024), _tile(D, 1024)

    def body(order_ref, a_ref, b_ref, o_ref, acc_ref):
        part = lax.dot_general(a_ref[...], b_ref[...], _DIMS["nt"], preferred_element_type=F32)
        k = pl.program_id(2)

        @pl.when(k == 0)
        def _():
            acc_ref[...] = part

        @pl.when(k > 0)
        def _():
            acc_ref[...] += part

        @pl.when(k == 3)
        def _():
            o_ref[...] = acc_ref[...]

    spec = pltpu.PrefetchScalarGridSpec(
        num_scalar_prefetch=1, grid=(T // tm, D // tn, 4),
        in_specs=[pl.BlockSpec((tm, Cs), lambda i, n, k, o: (i, o[k])), pl.BlockSpec((tn, Cs), lambda i, n, k, o: (n, k))],
        out_specs=pl.BlockSpec((tm, tn), lambda i, n, k, o: (i, n)), scratch_shapes=[pltpu.VMEM((tm, tn), F32)])
    return _call(body, name="d_h", grid_spec=spec, out_shape=jax.ShapeDtypeStruct((T, D), F32))(order, dproj, w)


def _input_grad(rest, w_in, order):
    dproj, x2, ng, dy, small = rest
    dh = _d_h(dproj, w_in, order)
    grad_x, dng = _rms_bwd(x2, dh, ng, dy, "rms_x_bwd")
    small = [dng if t is None else t for t in small]
    return grad_x, jnp.concatenate(small, axis=1)


def _local_step(x, mem, tgt, norm_g, mem_norm_g, gq_all, gk_all, conv_w, mem_gq, mem_gk, W):
    hb, hbt, mhb = _norms(x, mem, norm_g, mem_norm_g)
    Cs = W["w_in"].shape[1] // 4
    shards = (0, 2, 1, 3)
    order = jnp.array(shards, dtype=jnp.int32)
    w_rel = jnp.concatenate([W["w_in"][:, s * Cs:(s + 1) * Cs] for s in shards], axis=1)
    meta = jnp.array(shards + (0,), dtype=jnp.int32)
    proj2 = _proj_chunk(hb, w_rel, meta, 0, 1, None, None, "proj_0")
    for j, nslots in ((1, 2), (3, 1)):
        for half in (1, 0):
            proj2 = _proj_chunk(hb, w_rel, meta, j, nslots, half, proj2, f"proj_{j}_{half}")
    pre = (hb, hbt, mhb, proj2, *_attention_fwd(proj2, x.shape[0], gq_all, gk_all),
           _conv_branch_fwd(proj2, x.shape[0], conv_w))
    G, rest, _ = _weight_grads(x, mem, tgt, norm_g, mem_norm_g, gq_all, gk_all, conv_w, mem_gq, mem_gk, W, pre)
    pos = jnp.zeros((2,), jnp.int32)
    ntiles = rest[0].shape[1] // _tile(rest[0].shape[1], DW_IN_TILE)
    other = _dw_in_half(hbt, rest[0], pos, False, "dw_in_sibling_early", tiles=(0, ntiles - ntiles // 2))
    other = _dw_in_half(hbt, rest[0], pos, False, "dw_in_sibling_late", tiles=(ntiles - ntiles // 2, ntiles // 2),
                        into=other)
    G["w_in"] = jnp.concatenate([_dw_in_half(hbt, rest[0], pos, True, "dw_in_own"), other], axis=0)
    grad_x, small = _input_grad(rest, w_rel, order)
    return grad_x.reshape(x.shape), G, small


BIG = (("w_in", "col"), ("mem_w_kv", "row"), ("w_br_attn", "col"), ("w_br_conv", "col"),
       ("w_br_mem", "col"), ("w_out", "row"))


def _coords():
    return lax.axis_index("x"), lax.axis_index("y"), lax.axis_index("c")


def _other_chips(x, y):
    return [(1 - x, y), (x, 1 - y), (1 - x, 1 - y)]


def _half(ref, kind, c):
    R, C = ref.shape
    if kind == "col":
        return ref.at[pl.ds(c * (R // 2), R // 2), :]
    return ref.at[:, pl.ds(c * (C // 2), C // 2)]


def _shard(ref, kind, s):
    R, C = ref.shape
    if kind == "col":
        return ref.at[:, pl.ds(s * (C // 4), C // 4)]
    return ref.at[pl.ds(s * (R // 4), R // 4), :]


def _piece(ref, kind, s, c):
    R, C = ref.shape
    if kind == "col":
        return ref.at[pl.ds(c * (R // 2), R // 2), pl.ds(s * (C // 4), C // 4)]
    return ref.at[pl.ds(s * (R // 4), R // 4), pl.ds(c * (C // 2), C // 2)]


def _remote(src, dst, sems_s, sems_r, k, dev):
    return pltpu.make_async_remote_copy(src_ref=src, dst_ref=dst, send_sem=sems_s.at[k], recv_sem=sems_r.at[k],
                                        device_id=dev, device_id_type=MESH)


HBM = pl.BlockSpec(memory_space=pltpu.HBM)
SEM = pl.BlockSpec(memory_space=pltpu.SEMAPHORE)
EFFECT = pltpu.SideEffectType.DATAFLOW_SIDE_EFFECTING


def _hbm(a):
    return pltpu.with_memory_space_constraint(a, pltpu.HBM)


def _start_copies(name, arrays, ncopies, make):
    n = len(arrays)

    def body(*refs):
        for cp in make(refs[:n], refs[n], refs[n + 1]):
            cp.start()

    outs = pl.pallas_call(
        body, name=name,
        out_shape=(pltpu.SemaphoreType.DMA((ncopies,)), pltpu.SemaphoreType.DMA((ncopies,)),
                   *[jax.ShapeDtypeStruct(t.shape, t.dtype) for t in arrays]),
        in_specs=[HBM] * n, out_specs=(SEM, SEM, *([HBM] * n)),
        input_output_aliases={i: i + 2 for i in range(n)},
        compiler_params=pltpu.CompilerParams(has_side_effects=EFFECT),
    )(*[_hbm(t) for t in arrays])
    return outs[0], outs[1], list(outs[2:])


def _wait_copies(name, send, recv, arrays, make, after):
    n = len(arrays)

    def body(*refs):
        for cp in make(refs[:n], refs[n], refs[n + 1]):
            cp.wait_send()
            cp.wait_recv()

    outs = pl.pallas_call(
        body, name=name, out_shape=[jax.ShapeDtypeStruct(t.shape, t.dtype) for t in arrays],
        in_specs=[HBM] * n + [SEM, SEM, ANY], out_specs=[HBM] * n,
        input_output_aliases={i: i for i in range(n)},
        compiler_params=pltpu.CompilerParams(has_side_effects=EFFECT),
    )(*arrays, send, recv, after)
    return list(outs)


def _w_in_copies(relations):
    def make(refs, send, recv):
        x, y, c = _coords()
        me = 2 * x + y
        chips = _other_chips(x, y)
        w, conv = refs[0], refs[1]
        cps = []
        for i, k in enumerate(relations):
            cps.append(_remote(_column_half(w, 0, c), _column_half(w, 1 + k, c), send, recv, 2 * i, (*chips[k], c)))
            mine = _shard(conv, "col", me)
            cps.append(_remote(mine, mine, send, recv, 2 * i + 1, (*chips[k], c)))
        return cps
    return make


def _column_half(w, slot, c):
    half = w.shape[1] // 8
    return w.at[:, pl.ds((2 * slot + c) * half, half)]


def _w_in_forward(relations):
    def make(refs, send, recv):
        x, y, c = _coords()
        cps = []
        for i, k in enumerate(relations):
            got = _column_half(refs[0], 1 + k, c)
            cps.append(_remote(got, got, send, recv, i, (x, y, 1 - c)))
        return cps
    return make


def _sibling_columns(c0, width):
    def make(refs, send, recv):
        x, y, c = _coords()
        cols = pl.ds(c0, width)
        return [_remote(refs[0].at[:, cols], refs[1].at[:, cols], send, recv, 0, (x, y, 1 - c))]
    return make


def _other_weight_copies(refs, send, recv):
    x, y, c = _coords()
    me = 2 * x + y
    cps = []
    for k, chip in enumerate(_other_chips(x, y)):
        for p, (_, kind) in enumerate(BIG[1:]):
            mine = _piece(refs[p], kind, me, c)
            cps.append(_remote(mine, mine, send, recv, 3 * p + k, (*chip, c)))
    return cps


def _other_weight_forward(refs, send, recv):
    x, y, c = _coords()
    cps = []
    for k, chip in enumerate(_other_chips(x, y)):
        s = 2 * chip[0] + chip[1]
        for p, (_, kind) in enumerate(BIG[1:]):
            got = _piece(refs[p], kind, s, c)
            cps.append(_remote(got, got, send, recv, 3 * p + k, (x, y, 1 - c)))
    return cps


def _share_copies(group):
    def make(refs, send, recv):
        x, y, c = _coords()
        cps = []
        for p, (_, kind) in enumerate(group):
            mine = _half(refs[p], kind, c)
            cps.append(_remote(mine, mine, send, recv, p, (x, y, 1 - c)))
        return cps
    return make


def _sibling_halves_start(G, group, carry, tag):
    n = len(group)
    parts = [G[name] for name, _ in group]
    lands = []
    for (_, kind), g in zip(group, parts):
        R, C = g.shape
        lands.append(lax.empty((R // 2, C) if kind == "col" else (R, C // 2), g.dtype))

    def make(refs, send, recv):
        x, y, c = _coords()
        return [_remote(_half(refs[p], group[p][1], 1 - c), refs[n + p], send, recv, p, (x, y, 1 - c)) for p in range(n)]

    send, recv, thru = _start_copies("sibling_halves_start_" + tag, [*parts, *lands, carry], n, make)
    return (send, recv, thru[:2 * n], make, tag), thru[2 * n]


def _presums(state, group, pos, after):
    send, recv, arrays, make, tag = state
    n = len(group)
    thru = _wait_copies("sibling_halves_wait_" + tag, send, recv, arrays, make, after)
    return [_presum(thru[p], thru[n + p], kind, pos, "presum_" + name) for p, (name, kind) in enumerate(group)]


def _presum(g, got, kind, pos, name):
    R, C = got.shape
    tr, tc = _tile(R, 512, 16), _tile(C, 2048)
    nr, nc = R // tr, C // tc

    def body(pos_ref, a_ref, b_ref, o_ref):
        o_ref[...] = (a_ref[...].astype(F32) + b_ref[...].astype(F32)).astype(o_ref.dtype)

    blk = pl.BlockSpec((tr, tc), lambda i, j, pos_ref: (i, j))
    if kind == "col":
        mine = pl.BlockSpec((tr, tc), lambda i, j, pos_ref: (pos_ref[1] * nr + i, j))
    else:
        mine = pl.BlockSpec((tr, tc), lambda i, j, pos_ref: (i, pos_ref[1] * nc + j))
    spec = pltpu.PrefetchScalarGridSpec(num_scalar_prefetch=1, grid=(nr, nc), in_specs=[mine, blk], out_specs=blk)
    return _call(body, name=name, grid_spec=spec, out_shape=jax.ShapeDtypeStruct((R, C), WIRE_DTYPE))(pos, g, got)


def _chip_copies(group):
    n = len(group)

    def make(refs, send, recv):
        x, y, c = _coords()
        cps = []
        for k, chip in enumerate(_other_chips(x, y)):
            s = 2 * chip[0] + chip[1]
            for p in range(n):
                cps.append(_remote(_shard(refs[p], group[p][1], s), refs[n + p].at[k], send, recv, 3 * p + k, (*chip, c)))
        return cps
    return make


def _landing_zones(pres, group):
    lands = []
    for (_, kind), g in zip(group, pres):
        R, C = g.shape
        lands.append(lax.empty((3, R, C // 4) if kind == "col" else (3, R // 4, C), g.dtype))
    return lands


def _exchange_start(pres, group, carry, tag):
    n = len(group)
    make = _chip_copies(group)
    send, recv, thru = _start_copies("chip_exchange_start_" + tag, [*pres, *_landing_zones(pres, group), carry], 3 * n, make)
    return (send, recv, thru[:2 * n], make, tag), thru[2 * n]


def _exchange_wait(state, after):
    send, recv, arrays, make, tag = state
    thru = _wait_copies("chip_exchange_wait_" + tag, send, recv, arrays, make, after)
    n = len(thru) // 2
    return thru[:n], thru[n:]


def _reduce_into_shard(slots, pre, kind, pos, name):
    K, R, C = slots.shape
    tr, tc = _tile(R, 512, 16), _tile(C, 2176)
    nr, nc = R // tr, C // tc

    def body(pos_ref, s_ref, p_ref, o_ref):
        acc = p_ref[...].astype(F32)
        for k in range(K):
            acc = acc + s_ref[k].astype(F32)
        o_ref[...] = acc

    if kind == "col":
        own = pl.BlockSpec((tr, tc), lambda i, j, pos_ref: (i, pos_ref[0] * nc + j))
        full, out = (2 * R, C), pl.BlockSpec((tr, tc), lambda i, j, pos_ref: (pos_ref[1] * nr + i, j))
    else:
        own = pl.BlockSpec((tr, tc), lambda i, j, pos_ref: (pos_ref[0] * nr + i, j))
        full, out = (R, 2 * C), pl.BlockSpec((tr, tc), lambda i, j, pos_ref: (i, pos_ref[1] * nc + j))
    spec = pltpu.PrefetchScalarGridSpec(
        num_scalar_prefetch=1, grid=(nr, nc),
        in_specs=[pl.BlockSpec((K, tr, tc), lambda i, j, pos_ref: (0, i, j)), own], out_specs=out)
    return _call(body, name=name, grid_spec=spec, out_shape=jax.ShapeDtypeStruct(full, F32))(pos, slots, pre)


def _small_slots(pack, me):
    _, N = pack.shape

    def body(me_ref, p_ref, o_ref):
        o_ref[0] = p_ref[...]

    spec = pltpu.PrefetchScalarGridSpec(
        num_scalar_prefetch=1, grid=(1,), in_specs=[pl.BlockSpec((1, N), lambda i, me_ref: (0, 0))],
        out_specs=pl.BlockSpec((1, 1, N), lambda i, me_ref: (me_ref[0], 0, 0)))
    return _call(body, name="small_slots", grid_spec=spec, out_shape=jax.ShapeDtypeStruct((8, 1, N), pack.dtype))(me, pack)


def _small_copies(refs, send, recv):
    x, y, c = _coords()
    me = 4 * x + 2 * y + c
    cps = []
    for k in range(1, 8):
        dev = (x ^ (k >> 2), y ^ ((k >> 1) & 1), c ^ (k & 1))
        cps.append(_remote(refs[0], refs[1].at[me], send, recv, k - 1, dev))
    return cps


def _sum_small(slots):
    K, _, N = slots.shape

    def body(s_ref, o_ref):
        acc = s_ref[0]
        for k in range(1, K):
            acc = acc + s_ref[k]
        o_ref[...] = acc

    return _call(body, name="sum_small", in_specs=[pl.BlockSpec(memory_space=pltpu.VMEM)],
                 out_specs=pl.BlockSpec(memory_space=pltpu.VMEM), out_shape=jax.ShapeDtypeStruct((1, N), F32))(slots)


def _adamw(w, g, m, v, name, with_grad=False):
    R, C = w.shape
    tr, tc = _tile(R, 256, 8), _tile(C, 2176)

    def body(w_ref, g_ref, m_ref, v_ref, d_ref, nm_ref, nv_ref, *g_out):
        gv = g_ref[...]
        for ref in g_out:
            ref[...] = gv
        nm = ADAM_B1 * m_ref[...] + (1.0 - ADAM_B1) * gv
        nv = ADAM_B2 * v_ref[...] + (1.0 - ADAM_B2) * gv * gv
        m_hat = nm / (1.0 - ADAM_B1 ** ADAM_STEP)
        v_hat = nv / (1.0 - ADAM_B2 ** ADAM_STEP)
        d_ref[...] = -ADAM_LR * (m_hat / (jnp.sqrt(v_hat) + ADAM_EPS) + ADAM_WD * w_ref[...])
        nm_ref[...] = nm
        nv_ref[...] = nv

    spec = pl.BlockSpec((tr, tc), lambda i, j: (i, j))
    shp = jax.ShapeDtypeStruct((R, C), F32)
    nout = 4 if with_grad else 3
    return _call(body, name=name, grid=(R // tr, C // tc), in_specs=[spec] * 4, out_specs=[spec] * nout,
                 out_shape=[shp] * nout)(w, g, m, v)


SMALL = ("norm_g", "mem_norm_g", "attn_q_norm", "attn_k_norm", "conv_w", "mem_q_norm", "mem_k_norm")
WEIGHTS = ("norm_g", "mem_norm_g", "w_in", "attn_q_norm", "attn_k_norm", "conv_w", "mem_w_kv", "mem_q_norm",
           "mem_k_norm", "w_br_attn", "w_br_conv", "w_br_mem", "w_out")


def kernel(x, mem, norm_g, mem_norm_g, w_in, attn_q_norm, attn_k_norm, conv_w, mem_w_kv, mem_q_norm, mem_k_norm, w_br_attn, w_br_conv, w_br_mem, w_out, loss_target, m_norm_g, m_mem_norm_g, m_w_in, m_attn_q_norm, m_attn_k_norm, m_conv_w, m_mem_w_kv, m_mem_q_norm, m_mem_k_norm, m_w_br_attn, m_w_br_conv, m_w_br_mem, m_w_out, v_norm_g, v_mem_norm_g, v_w_in, v_attn_q_norm, v_attn_k_norm, v_conv_w, v_mem_w_kv, v_mem_q_norm, v_mem_k_norm, v_w_br_attn, v_w_br_conv, v_w_br_mem, v_w_out):
    w = dict(norm_g=norm_g, mem_norm_g=mem_norm_g, w_in=w_in, attn_q_norm=attn_q_norm, attn_k_norm=attn_k_norm,
             conv_w=conv_w, mem_w_kv=mem_w_kv, mem_q_norm=mem_q_norm, mem_k_norm=mem_k_norm, w_br_attn=w_br_attn,
             w_br_conv=w_br_conv, w_br_mem=w_br_mem, w_out=w_out)
    m = dict(norm_g=m_norm_g, mem_norm_g=m_mem_norm_g, w_in=m_w_in, attn_q_norm=m_attn_q_norm,
             attn_k_norm=m_attn_k_norm, conv_w=m_conv_w, mem_w_kv=m_mem_w_kv, mem_q_norm=m_mem_q_norm,
             mem_k_norm=m_mem_k_norm, w_br_attn=m_w_br_attn, w_br_conv=m_w_br_conv, w_br_mem=m_w_br_mem, w_out=m_w_out)
    v = dict(norm_g=v_norm_g, mem_norm_g=v_mem_norm_g, w_in=v_w_in, attn_q_norm=v_attn_q_norm,
             attn_k_norm=v_attn_k_norm, conv_w=v_conv_w, mem_w_kv=v_mem_w_kv, mem_q_norm=v_mem_q_norm,
             mem_k_norm=v_mem_k_norm, w_br_attn=v_w_br_attn, w_br_conv=v_w_br_conv, w_br_mem=v_w_br_mem, w_out=v_w_out)
    Bl, _, D = x.shape
    cx, cy = lax.axis_index("x"), lax.axis_index("y")
    chip = 2 * cx + cy
    pos = jnp.stack([chip, lax.axis_index("c")]).astype(jnp.int32)
    order = jnp.stack([chip] + [2 * a + b for a, b in _other_chips(cx, cy)]).astype(jnp.int32)
    n = len(BIG)

    slot0 = jnp.stack([jnp.zeros((), jnp.int32), pos[1]])
    w_rel = _place_shard(w["w_in"], "col", slot0, WIRE_DTYPE, "place_w_in_sent", half=0)
    conv_full = _place_shard(conv_w, "col", pos, F32, "place_conv_w")
    others = [_place_shard(w[name], kind, pos, WIRE_DTYPE, "place_" + name) for name, kind in BIG[1:]]
    hb, hbt, mhb = _norms(x, mem, norm_g, mem_norm_g)

    meta = jnp.concatenate([order, pos[1:]])
    near, near_fwd = _w_in_copies((0, 1)), _w_in_forward((0, 1))
    send, recv, (w_rel, conv_full) = _start_copies("gather_near_start", [w_rel, conv_full], 4, near)
    w_rel = _place_shard(w["w_in"], "col", slot0, WIRE_DTYPE, "place_w_in_kept", half=1, into=w_rel)
    proj = _proj_chunk(hb, w_rel, meta, 0, 1, None, None, "proj_own")
    w_rel, conv_full, *others = _wait_copies("gather_near_wait", send, recv, [w_rel, conv_full, *others], near, proj)

    fsend, frecv, (w_rel,) = _start_copies("gather_near_forward_start", [w_rel], 2, near_fwd)
    far, far_fwd = _w_in_copies((2,)), _w_in_forward((2,))
    send, recv, (w_rel, conv_full) = _start_copies("gather_far_start", [w_rel, conv_full], 2, far)
    proj = _proj_chunk(hb, w_rel, meta, 1, 2, 0, proj, "proj_near_landed")
    w_rel, = _wait_copies("gather_near_forward_wait", fsend, frecv, [w_rel], near_fwd, proj)
    proj = _proj_chunk(hb, w_rel, meta, 1, 2, 1, proj, "proj_near_forwarded")
    w_rel, conv_full = _wait_copies("gather_far_wait", send, recv, [w_rel, conv_full], far, proj)

    fsend, frecv, (w_rel,) = _start_copies("gather_far_forward_start", [w_rel], 1, far_fwd)
    send, recv, (*others, w_rel) = _start_copies("gather_rest_start", [*others, w_rel], 3 * (n - 1), _other_weight_copies)
    proj = _proj_chunk(hb, w_rel, meta, 3, 1, 0, proj, "proj_far_landed")
    w_rel, = _wait_copies("gather_far_forward_wait", fsend, frecv, [w_rel], far_fwd, proj)
    proj = _proj_chunk(hb, w_rel, meta, 3, 1, 1, proj, "proj_far_forwarded")
    os, ls, a = _attention_fwd(proj, Bl, attn_q_norm, attn_k_norm)
    *others, w_rel = _wait_copies("gather_rest_wait", send, recv, [*others, w_rel], _other_weight_copies, a[0])
    fsend, frecv, (*others, proj) = _start_copies("gather_rest_forward_start", [*others, proj], 3 * (n - 1),
                                                  _other_weight_forward)
    cc = _conv_branch_fwd(proj, Bl, conv_full)
    others = _wait_copies("gather_rest_forward_wait", fsend, frecv, others, _other_weight_forward, cc[0])
    W = {name: others[p] for p, (name, _) in enumerate(BIG[1:])}

    def rest_halves(G, carry):
        return _sibling_halves_start(G, BIG[1:], carry, "rest")

    def rest_exchange(state, after, carry):
        return _exchange_start(_presums(state, BIG[1:], pos, after), BIG[1:], carry, "rest")

    G, rest, rest_state = _weight_grads(
        x, mem, loss_target, norm_g, mem_norm_g, attn_q_norm, attn_k_norm, conv_full, mem_q_norm, mem_k_norm, W,
        (hb, hbt, mhb, proj, os, ls, a, cc), early=(rest_halves, rest_exchange))

    tn = _tile(rest[0].shape[1], DW_IN_TILE)
    ntiles = rest[0].shape[1] // tn
    first = ntiles - ntiles // 2
    early_cols = _sibling_columns(0, first * tn)
    late_cols = _sibling_columns(first * tn, (ntiles - first) * tn)
    for_sibling = _dw_in_half(hbt, rest[0], pos, False, "dw_in_sibling_early", tiles=(0, first))
    send, recv, (for_sibling, got, dproj) = _start_copies(
        "sibling_w_in_start_early", [for_sibling, lax.empty(for_sibling.shape, for_sibling.dtype), rest[0]], 1, early_cols)
    for_sibling = _dw_in_half(hbt, dproj, pos, False, "dw_in_sibling_late", tiles=(first, ntiles - first),
                              into=for_sibling)
    lsend, lrecv, (for_sibling, got) = _start_copies("sibling_w_in_start_late", [for_sibling, got], 1, late_cols)
    pres_rest, slots_rest = _exchange_wait(rest_state, for_sibling)
    reds_rest = [_reduce_into_shard(slots_rest[p], pres_rest[p], kind, pos, "reduce_" + name)
                 for p, (name, kind) in enumerate(BIG[1:])]
    share_rest = _share_copies(BIG[1:])
    rsend, rrecv, reds_rest = _start_copies("share_rest_start", reds_rest, n - 1, share_rest)
    for_sibling, got = _wait_copies("sibling_w_in_wait_early", send, recv, [for_sibling, got], early_cols, reds_rest[0])
    for_sibling, got = _wait_copies("sibling_w_in_wait_late", lsend, lrecv, [for_sibling, got], late_cols, reds_rest[0])
    pre_w_in = _dw_in_half(hbt, dproj, pos, True, "dw_in_own", add=got)

    w_in_state, dproj = _exchange_start([pre_w_in], BIG[:1], dproj, "w_in")
    grad_x, small = _input_grad((dproj, *rest[1:]), w_rel, order)
    pres, slots = _exchange_wait(w_in_state, grad_x)
    red_w_in = _reduce_into_shard(slots[0], pres[0], "col", pos, "reduce_w_in")
    share_w_in = _share_copies(BIG[:1])
    wsend, wrecv, (red_w_in, small) = _start_copies("share_w_in_start", [red_w_in, small], 1, share_w_in)
    grad_x = grad_x.reshape(x.shape)

    slots = _small_slots(small, (2 * pos[:1] + pos[1:]))
    ssend, srecv, (small, slots) = _start_copies("gather_small_start", [small, slots], 7, _small_copies)
    reds_rest = _wait_copies("share_rest_wait", rsend, rrecv, reds_rest, share_rest, slots)
    grads = dict(zip([name for name, _ in BIG[1:]], reds_rest))
    delta, new_m, new_v = {}, {}, {}
    for name, _ in BIG[1:]:
        delta[name], new_m[name], new_v[name], grads[name] = _adamw(w[name], grads[name], m[name], v[name],
                                                                    "adamw_" + name, with_grad=True)

    small, slots = _wait_copies("gather_small_wait", ssend, srecv, [small, slots], _small_copies, delta[BIG[-1][0]])
    tot = _sum_small(slots)[0]
    loss = tot[0]
    off = 128
    for name, size in (("norm_g", D), ("mem_norm_g", D), ("attn_q_norm", NGROUP * HEAD), ("attn_k_norm", NGROUP * HEAD),
                       ("conv_w", 3 * CONVW), ("mem_q_norm", MEM_HD), ("mem_k_norm", MEM_HD)):
        grads[name] = tot[off:off + size]
        off += size
    cw = conv_w.shape[1]
    grads["conv_w"] = lax.dynamic_slice(grads["conv_w"].reshape(3, CONVW), (0, chip * cw), (3, cw))
    for name in SMALL:
        grads[name] = grads[name].reshape(w[name].shape)

    def packed(t):
        return jnp.concatenate([t[name].reshape(1, -1) for name in SMALL], axis=1)

    ds, ms, vs = _adamw(packed(w), packed(grads), packed(m), packed(v), "adamw_small")
    shared, = _wait_copies("share_w_in_wait", wsend, wrecv, [red_w_in], share_w_in, ds)
    delta["w_in"], new_m["w_in"], new_v["w_in"], grads["w_in"] = _adamw(w["w_in"], shared, m["w_in"], v["w_in"],
                                                                        "adamw_w_in", with_grad=True)
    off = 0
    for name in SMALL:
        size = w[name].size
        delta[name] = ds[0, off:off + size].reshape(w[name].shape)
        new_m[name] = ms[0, off:off + size].reshape(w[name].shape)
        new_v[name] = vs[0, off:off + size].reshape(w[name].shape)
        off += size

    return (loss, grad_x, *[grads[n] for n in WEIGHTS], *[delta[n] for n in WEIGHTS],
            *[new_m[n] for n in WEIGHTS], *[new_v[n] for n in WEIGHTS])
```
